```python
import math
import jax, jax.numpy as jnp
from jax import lax
import numpy as np

D_MODEL = 1024
BATCH = 16
SEQ = 2048
DEPTH = 2

SSD_HEADS = 16
SSD_HEAD_DIM = 64
SSD_WIDTH = SSD_HEADS * SSD_HEAD_DIM
SSD_GROUPS = 4
SSD_STATE = 128
CONV_K = 4
SSD_CHUNK = 128
CONV_DIM = SSD_WIDTH + 2 * SSD_GROUPS * SSD_STATE

ATT_HEADS = 16
ATT_HEAD_DIM = 64
ATT_WIDTH = ATT_HEADS * ATT_HEAD_DIM
ATT_BRANCHES = ((128, 1), (512, 4), (2048, 16))
BAND_BLOCK = 128

MIX_WIDTH = SSD_WIDTH + ATT_WIDTH
IN_PROJ = SSD_WIDTH + CONV_DIM + SSD_HEADS + 3 * ATT_WIDTH
D_FF = 2816
EPS = 1e-6

kernel_name = "hymba_ssd_dilated_macaron"


def rms_norm(x, w):
    xf = x.astype(jnp.float32)
    y = xf * lax.rsqrt(jnp.mean(xf * xf, axis=-1, keepdims=True) + EPS)
    return (y * w.astype(jnp.float32)).astype(x.dtype)


def swiglu(x, w_gate, w_up, w_down):
    return (jax.nn.silu(x @ w_gate) * (x @ w_up)) @ w_down


def causal_depthwise_conv(x, w, b):
    c = x.shape[-1]
    y = lax.conv_general_dilated(
        x, w[:, None, :], window_strides=(1,), padding=[(CONV_K - 1, 0)],
        dimension_numbers=("NWC", "WIO", "NWC"), feature_group_count=c)
    return y + b


def ssd_mixer(xbc, dt_raw, z, dt_bias, a_log, d_skip, norm_w):
    out_dtype = z.dtype
    b, s, _ = xbc.shape
    H, P, G, N, L = SSD_HEADS, SSD_HEAD_DIM, SSD_GROUPS, SSD_STATE, SSD_CHUNK
    R = H // G
    nc = s // L
    xbc = xbc.astype(jnp.float32)
    xs, bm, cm = jnp.split(xbc, [SSD_WIDTH, SSD_WIDTH + G * N], axis=-1)
    dt = jax.nn.softplus(dt_raw.astype(jnp.float32) + dt_bias.astype(jnp.float32))
    a = -jnp.exp(a_log.astype(jnp.float32))
    xh = xs.reshape(b, s, H, P)
    X = (xh * dt[..., None]).reshape(b, nc, L, G, R, P)
    adt = (dt * a).reshape(b, nc, L, G, R).transpose(0, 3, 4, 1, 2)
    a_cum = jnp.cumsum(adt, axis=-1)
    Bc = bm.reshape(b, nc, L, G, N)
    Cc = cm.reshape(b, nc, L, G, N)
    causal = jnp.tril(jnp.ones((L, L), dtype=bool))
    seg = a_cum[..., :, None] - a_cum[..., None, :]
    lmat = jnp.where(causal, jnp.exp(jnp.where(causal, seg, 0.0)), 0.0)
    cb = jnp.einsum("bclgn,bcsgn->bgcls", Cc, Bc)
    y_diag = jnp.einsum("bgcls,bgrcls,bcsgrp->bclgrp", cb, lmat, X)
    decay_states = jnp.exp(a_cum[..., -1:] - a_cum)
    states = jnp.einsum("bclgn,bgrcl,bclgrp->bcgrpn", Bc, decay_states, X)
    chunk_decay = jnp.exp(a_cum[..., -1])

    def step(h, inp):
        st, dec = inp
        return h * dec[..., None, None] + st, h

    h0 = jnp.zeros((b, G, R, P, N), jnp.float32)
    _, prev = lax.scan(step, h0, (states.transpose(1, 0, 2, 3, 4, 5),
                                  chunk_decay.transpose(3, 0, 1, 2)))
    y_off = jnp.einsum("bclgn,cbgrpn,bgrcl->bclgrp", Cc, prev, jnp.exp(a_cum))
    y = (y_diag + y_off).reshape(b, s, H, P) + xh * d_skip.astype(jnp.float32)[:, None]
    y = y.reshape(b, s, SSD_WIDTH) * jax.nn.silu(z.astype(jnp.float32))
    yg = y.reshape(b, s, G, SSD_WIDTH // G)
    yg = yg * lax.rsqrt(jnp.mean(yg * yg, axis=-1, keepdims=True) + EPS)
    y = yg.reshape(b, s, SSD_WIDTH) * norm_w.astype(jnp.float32)
    return y.astype(out_dtype)


def dilated_branch(q, k, v, dilation, back):
    b, s, h, e = q.shape
    n = s // dilation
    nb = -(-n // BAND_BLOCK)
    npad = nb * BAND_BLOCK
    blk = BAND_BLOCK

    def blocks(t):
        t = t.reshape(b, n, dilation, h, e)
        t = jnp.pad(t, ((0, 0), (0, npad - n), (0, 0), (0, 0), (0, 0)))
        return t.reshape(b, nb, blk, dilation, h, e)

    def with_prev(t):
        prev = jnp.pad(t[:, :-1], ((0, 0), (1, 0), (0, 0), (0, 0), (0, 0), (0, 0)))
        return jnp.concatenate([prev, t], axis=2)

    qb = blocks(q)
    kk = with_prev(blocks(k))
    vv = with_prev(blocks(v))
    scores = jnp.einsum("bnqrhe,bnkrhe->bnrhqk", qb, kk).astype(jnp.float32)
    scores = scores * (1.0 / math.sqrt(e))
    qi = jnp.arange(blk)[:, None]
    kj = jnp.arange(2 * blk)[None, :]
    dist = qi + blk - kj
    bidx = jnp.arange(nb)[:, None, None]
    valid = (dist >= 0) & (dist <= back) & (bidx * blk - blk + kj >= 0)
    scores = jnp.where(valid[None, :, None, None], scores, -jnp.inf)
    m = jnp.max(scores, axis=-1, keepdims=True)
    p = jnp.exp(scores - m)
    den = jnp.sum(p, axis=-1, keepdims=True)
    o = jnp.einsum("bnrhqk,bnkrhe->bnqrhe", (p / den).astype(v.dtype), vv)
    lse = (m + jnp.log(den))[..., 0]
    o = o.reshape(b, npad, dilation, h, e)[:, :n].reshape(b, s, h, e)
    lse = lse.transpose(0, 1, 4, 2, 3).reshape(b, npad, dilation, h)[:, :n].reshape(b, s, h)
    return o, lse


def dilated_attention(q, k, v):
    outs, lses = [], []
    for window, dilation in ATT_BRANCHES:
        o, lse = dilated_branch(q, k, v, dilation, window // dilation)
        outs.append(o)
        lses.append(lse)
    wts = jax.nn.softmax(jnp.stack(lses, axis=0), axis=0)
    y = jnp.einsum("kbsh,kbshe->bshe", wts.astype(q.dtype), jnp.stack(outs, axis=0))
    return y


def _fwd_setup_inputs(seed: int = 0) -> dict:
    key = jax.random.key(seed)
    ks = jax.random.split(key, 24)
    f = jnp.float32

    def normal(k, shape, scale):
        return jax.random.normal(k, shape, f) * scale

    def gain(k, shape):
        return 1.0 + 0.02 * jax.random.normal(k, shape, f)

    dt = jnp.exp(jax.random.uniform(ks[10], (DEPTH, SSD_HEADS), f)
                 * (math.log(0.1) - math.log(0.001)) + math.log(0.001))
    return {
        "x": jax.random.normal(ks[0], (BATCH, SEQ, D_MODEL), f),
        "ffn1_norm": gain(ks[1], (DEPTH, D_MODEL)),
        "ffn1_w_gate": normal(ks[2], (DEPTH, D_MODEL, D_FF), D_MODEL ** -0.5),
        "ffn1_w_up": normal(ks[3], (DEPTH, D_MODEL, D_FF), D_MODEL ** -0.5),
        "ffn1_w_down": normal(ks[4], (DEPTH, D_FF, D_MODEL), D_FF ** -0.5),
        "mix_norm": gain(ks[5], (DEPTH, D_MODEL)),
        "w_in": normal(ks[6], (DEPTH, D_MODEL, IN_PROJ), D_MODEL ** -0.5),
        "conv_w": normal(ks[7], (DEPTH, CONV_K, CONV_DIM), CONV_K ** -0.5),
        "conv_b": normal(ks[8], (DEPTH, CONV_DIM), 0.02),
        "dt_bias": dt + jnp.log(-jnp.expm1(-dt)),
        "a_log": jnp.log(jax.random.uniform(ks[11], (DEPTH, SSD_HEADS), f, 1.0, 16.0)),
        "d_skip": gain(ks[12], (DEPTH, SSD_HEADS)),
        "ssd_norm": gain(ks[13], (DEPTH, SSD_WIDTH)),
        "q_norm": gain(ks[14], (DEPTH, ATT_HEAD_DIM)),
        "k_norm": gain(ks[15], (DEPTH, ATT_HEAD_DIM)),
        "w_out": normal(ks[16], (DEPTH, MIX_WIDTH, D_MODEL), MIX_WIDTH ** -0.5),
        "ffn2_norm": gain(ks[17], (DEPTH, D_MODEL)),
        "ffn2_w_gate": normal(ks[18], (DEPTH, D_MODEL, D_FF), D_MODEL ** -0.5),
        "ffn2_w_up": normal(ks[19], (DEPTH, D_MODEL, D_FF), D_MODEL ** -0.5),
        "ffn2_w_down": normal(ks[20], (DEPTH, D_FF, D_MODEL), D_FF ** -0.5),
    }


def _fwd_reference(x, ffn1_norm, ffn1_w_gate, ffn1_w_up, ffn1_w_down, mix_norm, w_in,
              conv_w, conv_b, dt_bias, a_log, d_skip, ssd_norm, q_norm, k_norm,
              w_out, ffn2_norm, ffn2_w_gate, ffn2_w_up, ffn2_w_down):
    b, s, _ = x.shape
    splits = np.cumsum([SSD_WIDTH, CONV_DIM, SSD_HEADS, ATT_WIDTH, ATT_WIDTH]).tolist()
    for i in range(DEPTH):
        x = x + 0.5 * swiglu(rms_norm(x, ffn1_norm[i]), ffn1_w_gate[i], ffn1_w_up[i], ffn1_w_down[i])
        h = rms_norm(x, mix_norm[i])
        proj = h @ w_in[i]
        z, xbc, dt_raw, q, k, v = jnp.split(proj, splits, axis=-1)
        xbc = jax.nn.silu(causal_depthwise_conv(xbc, conv_w[i], conv_b[i]))
        y_ssd = ssd_mixer(xbc, dt_raw, z, dt_bias[i], a_log[i], d_skip[i], ssd_norm[i])
        q = rms_norm(q.reshape(b, s, ATT_HEADS, ATT_HEAD_DIM), q_norm[i])
        k = rms_norm(k.reshape(b, s, ATT_HEADS, ATT_HEAD_DIM), k_norm[i])
        v = v.reshape(b, s, ATT_HEADS, ATT_HEAD_DIM)
        y_att = dilated_attention(q, k, v).reshape(b, s, ATT_WIDTH)
        x = x + jnp.concatenate([y_ssd, y_att], axis=-1) @ w_out[i]
        x = x + 0.5 * swiglu(rms_norm(x, ffn2_norm[i]), ffn2_w_gate[i], ffn2_w_up[i], ffn2_w_down[i])
    return x


import jax as _jax
import jax.numpy as _jnp

TWIN_FORMAT = 'train_step'
FWD_PARAMS = ['x', 'ffn1_norm', 'ffn1_w_gate', 'ffn1_w_up', 'ffn1_w_down', 'mix_norm', 'w_in', 'conv_w', 'conv_b', 'dt_bias', 'a_log', 'd_skip', 'ssd_norm', 'q_norm', 'k_norm', 'w_out', 'ffn2_norm', 'ffn2_w_gate', 'ffn2_w_up', 'ffn2_w_down']
TWIN_WEIGHTS = ['ffn1_norm', 'ffn1_w_gate', 'ffn1_w_up', 'ffn1_w_down', 'mix_norm', 'w_in', 'conv_w', 'conv_b', 'dt_bias', 'a_log', 'd_skip', 'ssd_norm', 'q_norm', 'k_norm', 'w_out', 'ffn2_norm', 'ffn2_w_gate', 'ffn2_w_up', 'ffn2_w_down']
TWIN_DIFF_INPUT = 'x'
TWIN_INPUTS = ['x', 'ffn1_norm', 'ffn1_w_gate', 'ffn1_w_up', 'ffn1_w_down', 'mix_norm', 'w_in', 'conv_w', 'conv_b', 'dt_bias', 'a_log', 'd_skip', 'ssd_norm', 'q_norm', 'k_norm', 'w_out', 'ffn2_norm', 'ffn2_w_gate', 'ffn2_w_up', 'ffn2_w_down', 'loss_target', 'm_ffn1_norm', 'm_ffn1_w_gate', 'm_ffn1_w_up', 'm_ffn1_w_down', 'm_mix_norm', 'm_w_in', 'm_conv_w', 'm_conv_b', 'm_dt_bias', 'm_a_log', 'm_d_skip', 'm_ssd_norm', 'm_q_norm', 'm_k_norm', 'm_w_out', 'm_ffn2_norm', 'm_ffn2_w_gate', 'm_ffn2_w_up', 'm_ffn2_w_down', 'v_ffn1_norm', 'v_ffn1_w_gate', 'v_ffn1_w_up', 'v_ffn1_w_down', 'v_mix_norm', 'v_w_in', 'v_conv_w', 'v_conv_b', 'v_dt_bias', 'v_a_log', 'v_d_skip', 'v_ssd_norm', 'v_q_norm', 'v_k_norm', 'v_w_out', 'v_ffn2_norm', 'v_ffn2_w_gate', 'v_ffn2_w_up', 'v_ffn2_w_down']
TWIN_OUTPUTS = ['loss', 'grad_x', 'grad_ffn1_norm', 'grad_ffn1_w_gate', 'grad_ffn1_w_up', 'grad_ffn1_w_down', 'grad_mix_norm', 'grad_w_in', 'grad_conv_w', 'grad_conv_b', 'grad_dt_bias', 'grad_a_log', 'grad_d_skip', 'grad_ssd_norm', 'grad_q_norm', 'grad_k_norm', 'grad_w_out', 'grad_ffn2_norm', 'grad_ffn2_w_gate', 'grad_ffn2_w_up', 'grad_ffn2_w_down', 'delta_ffn1_norm', 'delta_ffn1_w_gate', 'delta_ffn1_w_up', 'delta_ffn1_w_down', 'delta_mix_norm', 'delta_w_in', 'delta_conv_w', 'delta_conv_b', 'delta_dt_bias', 'delta_a_log', 'delta_d_skip', 'delta_ssd_norm', 'delta_q_norm', 'delta_k_norm', 'delta_w_out', 'delta_ffn2_norm', 'delta_ffn2_w_gate', 'delta_ffn2_w_up', 'delta_ffn2_w_down', 'new_m_ffn1_norm', 'new_m_ffn1_w_gate', 'new_m_ffn1_w_up', 'new_m_ffn1_w_down', 'new_m_mix_norm', 'new_m_w_in', 'new_m_conv_w', 'new_m_conv_b', 'new_m_dt_bias', 'new_m_a_log', 'new_m_d_skip', 'new_m_ssd_norm', 'new_m_q_norm', 'new_m_k_norm', 'new_m_w_out', 'new_m_ffn2_norm', 'new_m_ffn2_w_gate', 'new_m_ffn2_w_up', 'new_m_ffn2_w_down', 'new_v_ffn1_norm', 'new_v_ffn1_w_gate', 'new_v_ffn1_w_up', 'new_v_ffn1_w_down', 'new_v_mix_norm', 'new_v_w_in', 'new_v_conv_w', 'new_v_conv_b', 'new_v_dt_bias', 'new_v_a_log', 'new_v_d_skip', 'new_v_ssd_norm', 'new_v_q_norm', 'new_v_k_norm', 'new_v_w_out', 'new_v_ffn2_norm', 'new_v_ffn2_w_gate', 'new_v_ffn2_w_up', 'new_v_ffn2_w_down']
TWIN_LEAF_KINDS = {'loss': 'loss', 'grad_x': 'grad_x', 'grad_ffn1_norm': 'grad_w', 'grad_ffn1_w_gate': 'grad_w', 'grad_ffn1_w_up': 'grad_w', 'grad_ffn1_w_down': 'grad_w', 'grad_mix_norm': 'grad_w', 'grad_w_in': 'grad_w', 'grad_conv_w': 'grad_w', 'grad_conv_b': 'grad_w', 'grad_dt_bias': 'grad_w', 'grad_a_log': 'grad_w', 'grad_d_skip': 'grad_w', 'grad_ssd_norm': 'grad_w', 'grad_q_norm': 'grad_w', 'grad_k_norm': 'grad_w', 'grad_w_out': 'grad_w', 'grad_ffn2_norm': 'grad_w', 'grad_ffn2_w_gate': 'grad_w', 'grad_ffn2_w_up': 'grad_w', 'grad_ffn2_w_down': 'grad_w', 'delta_ffn1_norm': 'delta_w', 'delta_ffn1_w_gate': 'delta_w', 'delta_ffn1_w_up': 'delta_w', 'delta_ffn1_w_down': 'delta_w', 'delta_mix_norm': 'delta_w', 'delta_w_in': 'delta_w', 'delta_conv_w': 'delta_w', 'delta_conv_b': 'delta_w', 'delta_dt_bias': 'delta_w', 'delta_a_log': 'delta_w', 'delta_d_skip': 'delta_w', 'delta_ssd_norm': 'delta_w', 'delta_q_norm': 'delta_w', 'delta_k_norm': 'delta_w', 'delta_w_out': 'delta_w', 'delta_ffn2_norm': 'delta_w', 'delta_ffn2_w_gate': 'delta_w', 'delta_ffn2_w_up': 'delta_w', 'delta_ffn2_w_down': 'delta_w', 'new_m_ffn1_norm': 'new_m', 'new_m_ffn1_w_gate': 'new_m', 'new_m_ffn1_w_up': 'new_m', 'new_m_ffn1_w_down': 'new_m', 'new_m_mix_norm': 'new_m', 'new_m_w_in': 'new_m', 'new_m_conv_w': 'new_m', 'new_m_conv_b': 'new_m', 'new_m_dt_bias': 'new_m', 'new_m_a_log': 'new_m', 'new_m_d_skip': 'new_m', 'new_m_ssd_norm': 'new_m', 'new_m_q_norm': 'new_m', 'new_m_k_norm': 'new_m', 'new_m_w_out': 'new_m', 'new_m_ffn2_norm': 'new_m', 'new_m_ffn2_w_gate': 'new_m', 'new_m_ffn2_w_up': 'new_m', 'new_m_ffn2_w_down': 'new_m', 'new_v_ffn1_norm': 'new_v', 'new_v_ffn1_w_gate': 'new_v', 'new_v_ffn1_w_up': 'new_v', 'new_v_ffn1_w_down': 'new_v', 'new_v_mix_norm': 'new_v', 'new_v_w_in': 'new_v', 'new_v_conv_w': 'new_v', 'new_v_conv_b': 'new_v', 'new_v_dt_bias': 'new_v', 'new_v_a_log': 'new_v', 'new_v_d_skip': 'new_v', 'new_v_ssd_norm': 'new_v', 'new_v_q_norm': 'new_v', 'new_v_k_norm': 'new_v', 'new_v_w_out': 'new_v', 'new_v_ffn2_norm': 'new_v', 'new_v_ffn2_w_gate': 'new_v', 'new_v_ffn2_w_up': 'new_v', 'new_v_ffn2_w_down': 'new_v'}


def _forward(args):
    return _fwd_reference(*[args[k] for k in FWD_PARAMS])


def _output_shape():
    out = _jax.eval_shape(lambda: _forward(_fwd_setup_inputs(0)))
    return out.shape, out.dtype

N_MICROBATCH = 1
ADAM_LR = 0.001
ADAM_B1 = 0.9
ADAM_B2 = 0.999
ADAM_EPS = 1e-08
ADAM_WD = 0.01
ADAM_STEP = 10
PER_EXAMPLE_BATCH_AXIS = {'x': 0, 'loss_target': 0}
SHARED_INPUTS = []
_WEIGHT_DTYPES = {'ffn1_norm': _jnp.float32, 'ffn1_w_gate': _jnp.float32, 'ffn1_w_up': _jnp.float32, 'ffn1_w_down': _jnp.float32, 'mix_norm': _jnp.float32, 'w_in': _jnp.float32, 'conv_w': _jnp.float32, 'conv_b': _jnp.float32, 'dt_bias': _jnp.float32, 'a_log': _jnp.float32, 'd_skip': _jnp.float32, 'ssd_norm': _jnp.float32, 'q_norm': _jnp.float32, 'k_norm': _jnp.float32, 'w_out': _jnp.float32, 'ffn2_norm': _jnp.float32, 'ffn2_w_gate': _jnp.float32, 'ffn2_w_up': _jnp.float32, 'ffn2_w_down': _jnp.float32}
MOMENT_SCALE = {'ffn1_norm': 6.013016e+00, 'ffn1_w_gate': 1.162318e-01, 'ffn1_w_up': 1.193237e-01, 'ffn1_w_down': 1.957662e-01, 'mix_norm': 6.291103e-01, 'w_in': 2.216502e-01, 'conv_w': 4.670869e-01, 'conv_b': 1.518156e+00, 'dt_bias': 9.175186e-01, 'a_log': 4.533564e+00, 'd_skip': 3.707276e+00, 'ssd_norm': 2.180785e+01, 'q_norm': 1.686700e+00, 'k_norm': 1.690212e+00, 'w_out': 9.864113e-01, 'ffn2_norm': 6.087139e+00, 'ffn2_w_gate': 1.009301e-01, 'ffn2_w_up': 1.062057e-01, 'ffn2_w_down': 1.731856e-01}


def _to_microbatches(a, axis):
    t = _jnp.moveaxis(a, axis, 0)
    t = t.reshape((N_MICROBATCH, t.shape[0] // N_MICROBATCH) + t.shape[1:])
    return _jnp.moveaxis(t, 1, axis + 1)


def setup_inputs(seed: int = 0) -> dict:
    inp = _fwd_setup_inputs(seed)
    key = _jax.random.fold_in(_jax.random.key(seed), 7919)
    shape, _ = _output_shape()
    out = dict(inp)
    out["loss_target"] = _jax.random.normal(_jax.random.fold_in(key, 0), shape, _jnp.float32)
    for i, name in enumerate(TWIN_WEIGHTS):
        w = inp[name].astype(_jnp.float32)
        if MOMENT_SCALE is None:
            s = _jnp.sqrt(_jnp.mean(_jnp.square(w)) + 1e-30)
        else:
            s = MOMENT_SCALE[name]
        km, kv = _jax.random.split(_jax.random.fold_in(key, i + 1))
        out[name] = w
        out["m_" + name] = s * _jax.random.normal(km, w.shape, _jnp.float32)
        out["v_" + name] = (s * s) * _jax.random.uniform(kv, w.shape, _jnp.float32, 0.5, 1.5)
    if N_MICROBATCH > 1:
        for name, axis in PER_EXAMPLE_BATCH_AXIS.items():
            out[name] = _to_microbatches(out[name], axis)
    return {'x': out['x'], 'ffn1_norm': out['ffn1_norm'], 'ffn1_w_gate': out['ffn1_w_gate'], 'ffn1_w_up': out['ffn1_w_up'], 'ffn1_w_down': out['ffn1_w_down'], 'mix_norm': out['mix_norm'], 'w_in': out['w_in'], 'conv_w': out['conv_w'], 'conv_b': out['conv_b'], 'dt_bias': out['dt_bias'], 'a_log': out['a_log'], 'd_skip': out['d_skip'], 'ssd_norm': out['ssd_norm'], 'q_norm': out['q_norm'], 'k_norm': out['k_norm'], 'w_out': out['w_out'], 'ffn2_norm': out['ffn2_norm'], 'ffn2_w_gate': out['ffn2_w_gate'], 'ffn2_w_up': out['ffn2_w_up'], 'ffn2_w_down': out['ffn2_w_down'], 'loss_target': out['loss_target'], 'm_ffn1_norm': out['m_ffn1_norm'], 'm_ffn1_w_gate': out['m_ffn1_w_gate'], 'm_ffn1_w_up': out['m_ffn1_w_up'], 'm_ffn1_w_down': out['m_ffn1_w_down'], 'm_mix_norm': out['m_mix_norm'], 'm_w_in': out['m_w_in'], 'm_conv_w': out['m_conv_w'], 'm_conv_b': out['m_conv_b'], 'm_dt_bias': out['m_dt_bias'], 'm_a_log': out['m_a_log'], 'm_d_skip': out['m_d_skip'], 'm_ssd_norm': out['m_ssd_norm'], 'm_q_norm': out['m_q_norm'], 'm_k_norm': out['m_k_norm'], 'm_w_out': out['m_w_out'], 'm_ffn2_norm': out['m_ffn2_norm'], 'm_ffn2_w_gate': out['m_ffn2_w_gate'], 'm_ffn2_w_up': out['m_ffn2_w_up'], 'm_ffn2_w_down': out['m_ffn2_w_down'], 'v_ffn1_norm': out['v_ffn1_norm'], 'v_ffn1_w_gate': out['v_ffn1_w_gate'], 'v_ffn1_w_up': out['v_ffn1_w_up'], 'v_ffn1_w_down': out['v_ffn1_w_down'], 'v_mix_norm': out['v_mix_norm'], 'v_w_in': out['v_w_in'], 'v_conv_w': out['v_conv_w'], 'v_conv_b': out['v_conv_b'], 'v_dt_bias': out['v_dt_bias'], 'v_a_log': out['v_a_log'], 'v_d_skip': out['v_d_skip'], 'v_ssd_norm': out['v_ssd_norm'], 'v_q_norm': out['v_q_norm'], 'v_k_norm': out['v_k_norm'], 'v_w_out': out['v_w_out'], 'v_ffn2_norm': out['v_ffn2_norm'], 'v_ffn2_w_gate': out['v_ffn2_w_gate'], 'v_ffn2_w_up': out['v_ffn2_w_up'], 'v_ffn2_w_down': out['v_ffn2_w_down']}


def _loss(weights, diff, rest, loss_target):
    with _jax.named_scope("forward"):
        args = {**rest, TWIN_DIFF_INPUT: diff, **{k: w.astype(_WEIGHT_DTYPES[k]) for k, w in weights.items()}}
        y = _forward(args)
    with _jax.named_scope("loss_head"):
        err = _jnp.square(y.astype(_jnp.float32) - loss_target)
        return 0.5 * _jnp.sum(_jnp.mean(err, axis=-1)) if err.ndim else 0.5 * err


def _adamw(w, g, m, v):
    m = ADAM_B1 * m + (1.0 - ADAM_B1) * g
    v = ADAM_B2 * v + (1.0 - ADAM_B2) * _jnp.square(g)
    m_hat = m / (1.0 - ADAM_B1 ** ADAM_STEP)
    v_hat = v / (1.0 - ADAM_B2 ** ADAM_STEP)
    delta = -ADAM_LR * (m_hat / (_jnp.sqrt(v_hat) + ADAM_EPS) + ADAM_WD * w)
    return delta, m, v


def reference(x, ffn1_norm, ffn1_w_gate, ffn1_w_up, ffn1_w_down, mix_norm, w_in, conv_w, conv_b, dt_bias, a_log, d_skip, ssd_norm, q_norm, k_norm, w_out, ffn2_norm, ffn2_w_gate, ffn2_w_up, ffn2_w_down, loss_target, m_ffn1_norm, m_ffn1_w_gate, m_ffn1_w_up, m_ffn1_w_down, m_mix_norm, m_w_in, m_conv_w, m_conv_b, m_dt_bias, m_a_log, m_d_skip, m_ssd_norm, m_q_norm, m_k_norm, m_w_out, m_ffn2_norm, m_ffn2_w_gate, m_ffn2_w_up, m_ffn2_w_down, v_ffn1_norm, v_ffn1_w_gate, v_ffn1_w_up, v_ffn1_w_down, v_mix_norm, v_w_in, v_conv_w, v_conv_b, v_dt_bias, v_a_log, v_d_skip, v_ssd_norm, v_q_norm, v_k_norm, v_w_out, v_ffn2_norm, v_ffn2_w_gate, v_ffn2_w_up, v_ffn2_w_down):
    given = dict(x=x, ffn1_norm=ffn1_norm, ffn1_w_gate=ffn1_w_gate, ffn1_w_up=ffn1_w_up, ffn1_w_down=ffn1_w_down, mix_norm=mix_norm, w_in=w_in, conv_w=conv_w, conv_b=conv_b, dt_bias=dt_bias, a_log=a_log, d_skip=d_skip, ssd_norm=ssd_norm, q_norm=q_norm, k_norm=k_norm, w_out=w_out, ffn2_norm=ffn2_norm, ffn2_w_gate=ffn2_w_gate, ffn2_w_up=ffn2_w_up, ffn2_w_down=ffn2_w_down, loss_target=loss_target, m_ffn1_norm=m_ffn1_norm, m_ffn1_w_gate=m_ffn1_w_gate, m_ffn1_w_up=m_ffn1_w_up, m_ffn1_w_down=m_ffn1_w_down, m_mix_norm=m_mix_norm, m_w_in=m_w_in, m_conv_w=m_conv_w, m_conv_b=m_conv_b, m_dt_bias=m_dt_bias, m_a_log=m_a_log, m_d_skip=m_d_skip, m_ssd_norm=m_ssd_norm, m_q_norm=m_q_norm, m_k_norm=m_k_norm, m_w_out=m_w_out, m_ffn2_norm=m_ffn2_norm, m_ffn2_w_gate=m_ffn2_w_gate, m_ffn2_w_up=m_ffn2_w_up, m_ffn2_w_down=m_ffn2_w_down, v_ffn1_norm=v_ffn1_norm, v_ffn1_w_gate=v_ffn1_w_gate, v_ffn1_w_up=v_ffn1_w_up, v_ffn1_w_down=v_ffn1_w_down, v_mix_norm=v_mix_norm, v_w_in=v_w_in, v_conv_w=v_conv_w, v_conv_b=v_conv_b, v_dt_bias=v_dt_bias, v_a_log=v_a_log, v_d_skip=v_d_skip, v_ssd_norm=v_ssd_norm, v_q_norm=v_q_norm, v_k_norm=v_k_norm, v_w_out=v_w_out, v_ffn2_norm=v_ffn2_norm, v_ffn2_w_gate=v_ffn2_w_gate, v_ffn2_w_up=v_ffn2_w_up, v_ffn2_w_down=v_ffn2_w_down)
    weights = {n: given[n] for n in TWIN_WEIGHTS}
    shared = {n: given[n] for n in SHARED_INPUTS}
    per_example = {n: given[n] for n in ['x']}
    grad_fn = _jax.value_and_grad(_loss, argnums=(0, 1))

    def one_microbatch(ex, loss_target):
        ex = dict(ex)
        diff = ex.pop(TWIN_DIFF_INPUT)
        return grad_fn(weights, diff, {**shared, **ex}, loss_target)

    if N_MICROBATCH == 1:
        loss, (grad_w, grad_x) = one_microbatch(per_example, given["loss_target"])
    else:
        def body(carry, xs):
            loss_sum, grad_sum = carry
            l_k, (gw_k, gx_k) = one_microbatch(xs[0], xs[1])
            with _jax.named_scope("update"):
                return (loss_sum + l_k, _jax.tree.map(_jnp.add, grad_sum, gw_k)), gx_k

        init = (_jnp.zeros((), _jnp.float32), _jax.tree.map(_jnp.zeros_like, weights))
        (loss, grad_w), grad_x = _jax.lax.scan(body, init, (per_example, given["loss_target"]))
    with _jax.named_scope("update"):
        delta_w, new_m, new_v = {}, {}, {}
        for n in TWIN_WEIGHTS:
            delta_w[n], new_m[n], new_v[n] = _adamw(weights[n], grad_w[n], given["m_" + n], given["v_" + n])
    return (loss, grad_x, *[grad_w[n] for n in TWIN_WEIGHTS], *[delta_w[n] for n in TWIN_WEIGHTS],
            *[new_m[n] for n in TWIN_WEIGHTS], *[new_v[n] for n in TWIN_WEIGHTS])
```

```python
import functools
import math

import numpy as np
import jax
import jax.numpy as jnp
from jax import lax
from jax.experimental import pallas as pl
from jax.experimental.pallas import tpu as pltpu

F32 = jnp.float32
BF16 = jnp.bfloat16

D_MODEL = 1024
DEPTH = 2
N_SHARD = 4
D_FF = 2816
FF_SH = D_FF // N_SHARD
SSD_HEADS = 16
HEAD_DIM = 64
SSD_GROUPS = 4
GROUP_W = 256
SSD_STATE = 128
CONV_K = 4
CONV_DIM = 2048
ATT_HEADS = 16
MIX_W = 2048
MIX_SH = MIX_W // N_SHARD
IN_PROJ = 6160
IN_SH = IN_PROJ // N_SHARD
IN_PAD = 6272
PROJ_TN = 896
COL_Z, COL_XBC, COL_Q, COL_K, COL_V, COL_DT = 0, 1024, 3072, 4096, 5120, 6144
EPS = 1e-6
NEG = -1e30
SSD_L = 256
ATT_B = 256
ROW_T = 512
CONV_CT = 256
CONV_R = 256
PAD_R = 8

ADAM_LR, ADAM_B1, ADAM_B2, ADAM_EPS, ADAM_WD, ADAM_STEP = 0.001, 0.9, 0.999, 1e-08, 0.01, 10

NN = (((1,), (0,)), ((), ()))
NT = (((1,), (1,)), ((), ()))
TN = (((0,), (0,)), ((), ()))

VMEM_LIMIT = 56 * 1024 * 1024


def _cp(*sem):
    return pltpu.CompilerParams(dimension_semantics=sem, vmem_limit_bytes=VMEM_LIMIT)


def _dot(a, b, dims):
    return lax.dot_general(a, b, dims, preferred_element_type=F32)


def _sigmoid(x):
    return 1.0 / (1.0 + jnp.exp(-x))


def _softplus(x):
    return jnp.maximum(x, 0.0) + jnp.log(1.0 + jnp.exp(-jnp.abs(x)))


def _mm(name, pairs, out_shape, out_spec, grid, dims, acc_shape, res=None, scale=1.0, prev=None):
    nk = grid[2]
    npair = len(pairs)

    def body(*refs):
        ab = refs[:2 * npair]
        pos = 2 * npair
        res_ref = None
        if res is not None:
            res_ref = refs[pos]
            pos += 1
        if prev is not None:
            pos += 1
        out_ref, acc = refs[pos], refs[pos + 1]
        k = pl.program_id(2)

        @pl.when(k == 0)
        def _():
            acc[...] = jnp.zeros_like(acc)

        s = None
        for p in range(npair):
            d = _dot(ab[2 * p][...].astype(BF16), ab[2 * p + 1][...].astype(BF16), dims)
            s = d if s is None else s + d
        acc[...] += s

        @pl.when(k == nk - 1)
        def _():
            r = acc[...]
            if scale != 1.0:
                r = r * scale
            if res_ref is not None:
                r = r + res_ref[...]
            out_ref[...] = r.astype(out_ref.dtype)

    args, specs = [], []
    for a, a_spec, b, b_spec in pairs:
        args += [a, b]
        specs += [a_spec, b_spec]
    if res is not None:
        args.append(res[0])
        specs.append(res[1])
    aliases = {}
    if prev is not None:
        aliases = {len(args): 0}
        args.append(prev)
        specs.append(pl.BlockSpec(memory_space=pl.ANY))
    return pl.pallas_call(
        body, out_shape=out_shape, grid=grid, in_specs=specs, out_specs=out_spec,
        scratch_shapes=[pltpu.VMEM(acc_shape, F32)], input_output_aliases=aliases, name=name,
        compiler_params=_cp("parallel", "parallel", "arbitrary"))(*args)


def _rms_fwd(name, x, w):
    T = x.shape[0]

    def body(x_ref, w_ref, o_ref):
        xv = x_ref[...]
        r = lax.rsqrt(jnp.mean(xv * xv, axis=-1, keepdims=True) + EPS)
        o_ref[...] = (xv * r * w_ref[...]).astype(BF16)

    return pl.pallas_call(
        body, out_shape=jax.ShapeDtypeStruct((T, D_MODEL), BF16), grid=(T // ROW_T,),
        in_specs=[pl.BlockSpec((ROW_T, D_MODEL), lambda i: (i, 0)), pl.BlockSpec((1, D_MODEL), lambda i: (0, 0))],
        out_specs=pl.BlockSpec((ROW_T, D_MODEL), lambda i: (i, 0)), name=name, compiler_params=_cp("parallel"))(x, w)


def _rms_bwd(name, dh, x, w, dres):
    T = x.shape[0]

    def body(dh_ref, x_ref, w_ref, dres_ref, dx_ref, dw_ref):
        @pl.when(pl.program_id(0) == 0)
        def _():
            dw_ref[...] = jnp.zeros_like(dw_ref)

        xv = x_ref[...]
        r = lax.rsqrt(jnp.mean(xv * xv, axis=-1, keepdims=True) + EPS)
        xhat = xv * r
        dhv = dh_ref[...]
        dxhat = dhv * w_ref[...]
        m = jnp.mean(dxhat * xhat, axis=-1, keepdims=True)
        dx_ref[...] = dres_ref[...] + r * (dxhat - xhat * m)
        dw_ref[...] += jnp.sum(dhv * xhat, axis=0, keepdims=True)

    row = pl.BlockSpec((ROW_T, D_MODEL), lambda i: (i, 0))
    vec = pl.BlockSpec((1, D_MODEL), lambda i: (0, 0))
    return pl.pallas_call(
        body, out_shape=(jax.ShapeDtypeStruct((T, D_MODEL), F32), jax.ShapeDtypeStruct((1, D_MODEL), F32)),
        grid=(T // ROW_T,), in_specs=[row, row, vec, row], out_specs=(row, vec), name=name,
        compiler_params=_cp("arbitrary"))(dh, x, w, dres)


def _loss_grad(name, y, t):
    T = y.shape[0]

    def body(y_ref, t_ref, dy_ref, l_ref):
        @pl.when(pl.program_id(0) == 0)
        def _():
            l_ref[...] = jnp.zeros_like(l_ref)

        e = y_ref[...] - t_ref[...]
        dy_ref[...] = e * (1.0 / D_MODEL)
        l_ref[...] += jnp.sum(e * e, axis=0, keepdims=True)

    row = pl.BlockSpec((ROW_T, D_MODEL), lambda i: (i, 0))
    vec = pl.BlockSpec((1, D_MODEL), lambda i: (0, 0))
    return pl.pallas_call(
        body, out_shape=(jax.ShapeDtypeStruct((T, D_MODEL), F32), jax.ShapeDtypeStruct((1, D_MODEL), F32)),
        grid=(T // ROW_T,), in_specs=[row, row], out_specs=(row, vec), name=name,
        compiler_params=_cp("arbitrary"))(y, t)


def _ffn_gate_up(name, h, wg, wu, l):
    T = h.shape[0]

    def body(h_ref, wg_ref, wu_ref, g_ref, u_ref, a_ref):
        hv = h_ref[...]
        g = _dot(hv, wg_ref[...], NN)
        u = _dot(hv, wu_ref[...], NN)
        g_ref[...] = g.astype(BF16)
        u_ref[...] = u.astype(BF16)
        a_ref[...] = (g * _sigmoid(g) * u).astype(BF16)

    wspec = pl.BlockSpec((None, None, D_MODEL, FF_SH), lambda j, i: (j, l, 0, 0))
    ospec = pl.BlockSpec((None, ROW_T, FF_SH), lambda j, i: (j, i, 0))
    osh = jax.ShapeDtypeStruct((N_SHARD, T, FF_SH), BF16)
    return pl.pallas_call(
        body, out_shape=(osh, osh, osh), grid=(N_SHARD, T // ROW_T),
        in_specs=[pl.BlockSpec((ROW_T, D_MODEL), lambda j, i: (i, 0)), wspec, wspec],
        out_specs=(ospec, ospec, ospec), name=name, compiler_params=_cp("parallel", "parallel"))(h, wg, wu)


def _ffn_dact(name, dx, wd, g, u, l):
    T = dx.shape[0]

    def body(dx_ref, wd_ref, g_ref, u_ref, dg_ref, du_ref):
        da = 0.5 * _dot(dx_ref[...].astype(BF16), wd_ref[...], NT)
        gv = g_ref[...].astype(F32)
        uv = u_ref[...].astype(F32)
        sg = _sigmoid(gv)
        dg_ref[...] = (da * uv * (sg * (1.0 + gv * (1.0 - sg)))).astype(BF16)
        du_ref[...] = (da * gv * sg).astype(BF16)

    aspec = pl.BlockSpec((None, ROW_T, FF_SH), lambda j, i: (j, i, 0))
    osh = jax.ShapeDtypeStruct((N_SHARD, T, FF_SH), BF16)
    return pl.pallas_call(
        body, out_shape=(osh, osh), grid=(N_SHARD, T // ROW_T),
        in_specs=[pl.BlockSpec((ROW_T, D_MODEL), lambda j, i: (i, 0)),
                  pl.BlockSpec((None, None, FF_SH, D_MODEL), lambda j, i: (j, l, 0, 0)), aspec, aspec],
        out_specs=(aspec, aspec), name=name, compiler_params=_cp("parallel", "parallel"))(dx, wd, g, u)


def _ffn_fwd(tag, x, nw, wg, wu, wd, l):
    T = x.shape[0]
    h = _rms_fwd(tag + "_rms", x, nw)
    g, u, a = _ffn_gate_up(tag + "_gu", h, wg, wu, l)
    nt = T // ROW_T
    xo = _mm(tag + "_down",
             [(a, pl.BlockSpec((None, ROW_T, FF_SH), lambda i, n, k: (k, i, 0)),
               wd, pl.BlockSpec((None, None, FF_SH, D_MODEL), lambda i, n, k: (k, l, 0, 0)))],
             jax.ShapeDtypeStruct((T, D_MODEL), F32), pl.BlockSpec((ROW_T, D_MODEL), lambda i, n, k: (i, 0)),
             (nt, 1, N_SHARD), NN, (ROW_T, D_MODEL),
             res=(x, pl.BlockSpec((ROW_T, D_MODEL), lambda i, n, k: (i, 0))), scale=0.5)
    return xo, (x, h, g, u, a)


def _ffn_bwd(tag, dxo, saved, nw, wg, wu, wd, l, gbuf):
    x, h, g, u, a = saved
    T = x.shape[0]
    nt = T // ROW_T
    dg, du = _ffn_dact(tag + "_dact", dxo, wd, g, u, l)
    act = lambda f: pl.BlockSpec((None, ROW_T, FF_SH), f)
    gd = _mm(tag + "_dwd",
             [(a, act(lambda m, n, k: (m, k, 0)), dxo, pl.BlockSpec((ROW_T, D_MODEL), lambda m, n, k: (k, 0)))],
             jax.ShapeDtypeStruct((N_SHARD, DEPTH, FF_SH, D_MODEL), BF16),
             pl.BlockSpec((None, None, FF_SH, D_MODEL), lambda m, n, k: (m, l, 0, 0)),
             (N_SHARD, 1, nt), TN, (FF_SH, D_MODEL), scale=0.5, prev=gbuf[2])
    hspec = pl.BlockSpec((ROW_T, D_MODEL), lambda j, n, k: (k, 0))
    gsh = jax.ShapeDtypeStruct((N_SHARD, DEPTH, D_MODEL, FF_SH), BF16)
    gspec = pl.BlockSpec((None, None, D_MODEL, FF_SH), lambda j, n, k: (j, l, 0, 0))
    gg = _mm(tag + "_dwg", [(h, hspec, dg, act(lambda j, n, k: (j, k, 0)))], gsh, gspec,
             (N_SHARD, 1, nt), TN, (D_MODEL, FF_SH), prev=gbuf[0])
    gu = _mm(tag + "_dwu", [(h, hspec, du, act(lambda j, n, k: (j, k, 0)))], gsh, gspec,
             (N_SHARD, 1, nt), TN, (D_MODEL, FF_SH), prev=gbuf[1])
    wspec = pl.BlockSpec((None, None, D_MODEL, FF_SH), lambda i, n, k: (k, l, 0, 0))
    dh = _mm(tag + "_dh",
             [(dg, act(lambda i, n, k: (k, i, 0)), wg, wspec), (du, act(lambda i, n, k: (k, i, 0)), wu, wspec)],
             jax.ShapeDtypeStruct((T, D_MODEL), F32), pl.BlockSpec((ROW_T, D_MODEL), lambda i, n, k: (i, 0)),
             (nt, 1, N_SHARD), NT, (ROW_T, D_MODEL))
    dx, dnw = _rms_bwd(tag + "_rmsb", dh, x, nw, dxo)
    return dx, dnw, (gg, gu, gd)


def _conv_fwd(name, xpad, w, b):
    B, SP, C = xpad.shape
    S = SP - 2 * PAD_R

    def body(x_ref, w_ref, b_ref, o_ref):
        wv = w_ref[...]
        for c in range(S // CONV_R):
            r0 = c * CONV_R
            ch = x_ref[pl.ds(r0, CONV_R + PAD_R), :]
            pre = ch[PAD_R:] * wv[3:4] + b_ref[...]
            for s in range(1, CONV_K):
                pre = pre + pltpu.roll(ch, s, axis=0)[PAD_R:] * wv[3 - s:4 - s]
            o_ref[pl.ds(r0, CONV_R), :] = pre * _sigmoid(pre)

    return pl.pallas_call(
        body, out_shape=jax.ShapeDtypeStruct((B, S, C), F32), grid=(B, C // CONV_CT),
        in_specs=[pl.BlockSpec((None, SP, CONV_CT), lambda bi, ci: (bi, 0, ci)),
                  pl.BlockSpec((CONV_K, CONV_CT), lambda bi, ci: (0, ci)),
                  pl.BlockSpec((1, CONV_CT), lambda bi, ci: (0, ci))],
        out_specs=pl.BlockSpec((None, S, CONV_CT), lambda bi, ci: (bi, 0, ci)), name=name,
        compiler_params=_cp("parallel", "parallel"))(xpad, w, b)


def _conv_bwd(name, xpad, dxc_pad, w, b):
    B, SP, C = xpad.shape
    S = SP - 2 * PAD_R
    RW = CONV_R + PAD_R

    def body(x_ref, d_ref, w_ref, b_ref, dx_ref, dw_ref, db_ref):
        @pl.when(pl.program_id(1) == 0)
        def _():
            dw_ref[...] = jnp.zeros_like(dw_ref)
            db_ref[...] = jnp.zeros_like(db_ref)

        wv = w_ref[...]
        dw = [jnp.zeros((1, CONV_CT), F32) for _ in range(CONV_K)]
        db = jnp.zeros((1, CONV_CT), F32)
        for c in range(S // CONV_R):
            r0 = c * CONV_R
            ch = x_ref[pl.ds(r0, RW + PAD_R), :]
            xs = [ch[PAD_R:]] + [pltpu.roll(ch, s, axis=0)[PAD_R:] for s in range(1, CONV_K)]
            pre = b_ref[...] + xs[0] * wv[3:4]
            for s in range(1, CONV_K):
                pre = pre + xs[s] * wv[3 - s:4 - s]
            sg = _sigmoid(pre)
            dpre = d_ref[pl.ds(r0, RW), :] * (sg * (1.0 + pre * (1.0 - sg)))
            dx = dpre[:CONV_R] * wv[3:4]
            for s in range(1, CONV_K):
                dx = dx + pltpu.roll(dpre, RW - s, axis=0)[:CONV_R] * wv[3 - s:4 - s]
            dx_ref[pl.ds(r0, CONV_R), :] = dx
            dcur = dpre[:CONV_R]
            db = db + jnp.sum(dcur, axis=0, keepdims=True)
            for s in range(CONV_K):
                dw[3 - s] = dw[3 - s] + jnp.sum(dcur * xs[s][:CONV_R], axis=0, keepdims=True)
        db_ref[...] += db
        for k in range(CONV_K):
            dw_ref[k:k + 1, :] += dw[k]

    return pl.pallas_call(
        body,
        out_shape=(jax.ShapeDtypeStruct((B, S, C), F32), jax.ShapeDtypeStruct((CONV_K, C), F32),
                   jax.ShapeDtypeStruct((1, C), F32)),
        grid=(C // CONV_CT, B),
        in_specs=[pl.BlockSpec((None, SP, CONV_CT), lambda ci, bi: (bi, 0, ci)),
                  pl.BlockSpec((None, S + PAD_R, CONV_CT), lambda ci, bi: (bi, 0, ci)),
                  pl.BlockSpec((CONV_K, CONV_CT), lambda ci, bi: (0, ci)),
                  pl.BlockSpec((1, CONV_CT), lambda ci, bi: (0, ci))],
        out_specs=(pl.BlockSpec((None, S, CONV_CT), lambda ci, bi: (bi, 0, ci)),
                   pl.BlockSpec((CONV_K, CONV_CT), lambda ci, bi: (0, ci)),
                   pl.BlockSpec((1, CONV_CT), lambda ci, bi: (0, ci))),
        name=name, compiler_params=_cp("parallel", "arbitrary"))(xpad, dxc_pad, w, b)


def _ssd_common(dtc_ref, dtr_ref, pcol_ref, prow_ref, b_ref, c_ref):
    L = SSD_L
    bias_c, alog_c = pcol_ref[0:1, :], pcol_ref[1:2, :]
    a_c = -jnp.exp(alog_c)
    dt_c = _softplus(dtc_ref[...] + bias_c)
    row = lax.broadcasted_iota(jnp.int32, (L, L), 0)
    col = lax.broadcasted_iota(jnp.int32, (L, L), 1)
    causal = row >= col
    tri = causal.astype(F32)
    hp = lax.Precision.HIGHEST
    cum_c = lax.dot_general(tri, dt_c * a_c, NN, precision=hp, preferred_element_type=F32)
    a_r = -jnp.exp(prow_ref[:, 1:2])
    dt_r = _softplus(dtr_ref[...] + prow_ref[:, 0:1])
    cum_r = lax.dot_general(dt_r * a_r, tri, NT, precision=hp, preferred_element_type=F32)
    bb = b_ref[...].astype(BF16)
    cb = c_ref[...].astype(BF16)
    G = _dot(cb, bb, NT)
    return a_c, dt_c, causal, tri, cum_c, cum_r, bb, cb, G


def _ssd_fwd(name, xc, proj, dtc, dtr, pcol, prow, nw, B):
    T = xc.shape[0]
    S = T // B
    nb = S // SSD_L
    L = SSD_L

    def body(xs_ref, b_ref, c_ref, z_ref, dtc_ref, dtr_ref, pcol_ref, prow_ref, nw_ref, y_ref, yn_ref, hs_ref, H):
        @pl.when(pl.program_id(2) == 0)
        def _():
            H[...] = jnp.zeros_like(H)

        a_c, dt_c, causal, tri, cum_c, cum_r, bb, cb, G = _ssd_common(dtc_ref, dtr_ref, pcol_ref, prow_ref, b_ref, c_ref)
        dsk = pcol_ref[2:3, :]
        clast = cum_c[L - 1:L, :]
        bf = b_ref[...]
        for h in range(4):
            sl = slice(HEAD_DIM * h, HEAD_DIM * (h + 1))
            cc = cum_c[:, h:h + 1]
            lm = jnp.exp(jnp.where(causal, cc - cum_r[h:h + 1, :], NEG))
            M = (G * lm).astype(BF16)
            xh = xs_ref[:, sl]
            Xb = (xh * dt_c[:, h:h + 1]).astype(BF16)
            Hh = H[h]
            y = _dot(M, Xb, NN) + jnp.exp(cc) * _dot(cb, Hh.astype(BF16), NN)
            y_ref[:, sl] = y + dsk[:, h:h + 1] * xh
            hs_ref[h] = Hh
            cl = clast[:, h:h + 1]
            Bw = (bf * jnp.exp(cl - cc)).astype(BF16)
            H[h] = jnp.exp(cl) * Hh + _dot(Bw, Xb, TN)
        zv = z_ref[...]
        y2 = y_ref[...] * (zv * _sigmoid(zv))
        r = lax.rsqrt(jnp.mean(y2 * y2, axis=-1, keepdims=True) + EPS)
        yn_ref[...] = (y2 * r * nw_ref[...]).astype(BF16)

    rowi = lambda b, g, i: b * nb + i
    grp = pl.BlockSpec((L, GROUP_W), lambda b, g, i: (rowi(b, g, i), g))
    return pl.pallas_call(
        body,
        out_shape=(jax.ShapeDtypeStruct((T, 1024), F32), jax.ShapeDtypeStruct((T, 1024), BF16),
                   jax.ShapeDtypeStruct((B, SSD_GROUPS, nb, 4, SSD_STATE, HEAD_DIM), F32)),
        grid=(B, SSD_GROUPS, nb),
        in_specs=[grp,
                  pl.BlockSpec((L, SSD_STATE), lambda b, g, i: (rowi(b, g, i), 8 + g)),
                  pl.BlockSpec((L, SSD_STATE), lambda b, g, i: (rowi(b, g, i), 12 + g)),
                  grp,
                  pl.BlockSpec((None, L, 4), lambda b, g, i: (g, rowi(b, g, i), 0)),
                  pl.BlockSpec((None, 4, L), lambda b, g, i: (g, 0, rowi(b, g, i))),
                  pl.BlockSpec((None, 3, 4), lambda b, g, i: (g, 0, 0)),
                  pl.BlockSpec((None, 4, 3), lambda b, g, i: (g, 0, 0)),
                  pl.BlockSpec((1, GROUP_W), lambda b, g, i: (0, g))],
        out_specs=(grp, grp,
                   pl.BlockSpec((None, None, None, 4, SSD_STATE, HEAD_DIM), lambda b, g, i: (b, g, i, 0, 0, 0))),
        scratch_shapes=[pltpu.VMEM((4, SSD_STATE, HEAD_DIM), F32)], name=name,
        compiler_params=_cp("parallel", "parallel", "arbitrary"))(xc, xc, xc, proj, dtc, dtr, pcol, prow, nw)


def _ssd_bwd(name, dyn, Y, xc, proj, dtc, dtr, pcol, prow, nw, hs, B):
    T = xc.shape[0]
    S = T // B
    nb = S // SSD_L
    L = SSD_L

    def body(dyn_ref, y_ref, xs_ref, b_ref, c_ref, z_ref, dtc_ref, dtr_ref, pcol_ref, prow_ref, nw_ref, hs_ref,
             dxs_ref, db_ref, dc_ref, dz_ref, ddt_ref, dpar_ref, dnw_ref, dH):
        @pl.when(pl.program_id(2) == 0)
        def _():
            dH[...] = jnp.zeros_like(dH)
            dpar_ref[...] = jnp.zeros_like(dpar_ref)
            dnw_ref[...] = jnp.zeros_like(dnw_ref)

        a_c, dt_c, causal, tri, cum_c, cum_r, bb, cb, G = _ssd_common(dtc_ref, dtr_ref, pcol_ref, prow_ref, b_ref, c_ref)
        dsk = pcol_ref[2:3, :]
        clast = cum_c[L - 1:L, :]
        bf = b_ref[...]
        cf = c_ref[...]
        Yv = y_ref[...]
        zv = z_ref[...]
        sz = _sigmoid(zv)
        silu = zv * sz
        y2 = Yv * silu
        r = lax.rsqrt(jnp.mean(y2 * y2, axis=-1, keepdims=True) + EPS)
        yhat = y2 * r
        dyv = dyn_ref[...]
        dnw_ref[...] += jnp.sum(dyv * yhat, axis=0, keepdims=True)
        dyhat = dyv * nw_ref[...]
        dy2 = r * (dyhat - yhat * jnp.mean(dyhat * yhat, axis=-1, keepdims=True))
        dY = dy2 * silu
        dz_ref[...] = dy2 * Yv * (sz * (1.0 + zv * (1.0 - sz)))

        lane4 = lax.broadcasted_iota(jnp.int32, (1, 4), 1)
        dG = jnp.zeros((L, L), F32)
        dBs = jnp.zeros((L, SSD_STATE), F32)
        dCs = jnp.zeros((L, SSD_STATE), F32)
        dA = jnp.zeros((L, 4), F32)
        ddtx = jnp.zeros((L, 4), F32)
        ddsk = jnp.zeros((1, 4), F32)
        dcl = jnp.zeros((1, 4), F32)
        for h in range(4):
            sl = slice(HEAD_DIM * h, HEAD_DIM * (h + 1))
            onehot = (lane4 == h).astype(F32)
            cc = cum_c[:, h:h + 1]
            cl = clast[:, h:h + 1]
            lm = jnp.exp(jnp.where(causal, cc - cum_r[h:h + 1, :], NEG))
            M = (G * lm).astype(BF16)
            xh = xs_ref[:, sl]
            dth = dt_c[:, h:h + 1]
            X = xh * dth
            Xb = X.astype(BF16)
            dYh = dY[:, sl]
            dYb = dYh.astype(BF16)
            Hb = hs_ref[h].astype(BF16)
            dHh = dH[h]
            dHb = dHh.astype(BF16)
            alpha = jnp.exp(cc)
            beta = jnp.exp(cl - cc)
            dXoff = beta * _dot(bb, dHb, NN)
            dX = _dot(M, dYb, TN) + dXoff
            dG = dG + _dot(dYb, Xb, NT) * lm
            dCs = dCs + _dot((alpha * dYh).astype(BF16), Hb, NT)
            dBs = dBs + _dot((beta * X).astype(BF16), dHb, NT)
            ypre = Yv[:, sl] - dsk[:, h:h + 1] * xh
            dA_h = (jnp.sum(dYb.astype(F32) * ypre, axis=-1, keepdims=True)
                    - jnp.sum(Xb.astype(F32) * dX, axis=-1, keepdims=True))
            dA = dA + dA_h * onehot
            dcl_h = (jnp.sum(jnp.sum(dHh * (jnp.exp(cl) * hs_ref[h]), axis=-1, keepdims=True), axis=0, keepdims=True)
                     + jnp.sum(jnp.sum(Xb.astype(F32) * dXoff, axis=-1, keepdims=True), axis=0, keepdims=True))
            dcl = dcl + dcl_h * onehot
            ddtx = ddtx + jnp.sum(dX * xh, axis=-1, keepdims=True) * onehot
            ddsk = ddsk + jnp.sum(jnp.sum(dYh * xh, axis=-1, keepdims=True), axis=0, keepdims=True) * onehot
            dxs_ref[:, sl] = dsk[:, h:h + 1] * dYh + dX * dth
            dH[h] = jnp.exp(cl) * dHh + _dot((alpha * cf).astype(BF16), dYb, TN)
        dGb = dG.astype(BF16)
        dc_ref[...] = _dot(dGb, bb, NN) + dCs
        db_ref[...] = _dot(dGb, cb, TN) + dBs
        hp = lax.Precision.HIGHEST
        last = lax.broadcasted_iota(jnp.int32, (L, 1), 0) == L - 1
        dA = dA + jnp.where(last, dcl, 0.0)
        dadt = lax.dot_general(tri, dA, TN, precision=hp, preferred_element_type=F32)
        ddt = dadt * a_c + ddtx
        d_a = jnp.sum(dadt * dt_c, axis=0, keepdims=True)
        ddraw = ddt * _sigmoid(dtc_ref[...] + pcol_ref[0:1, :])
        ddt_ref[...] = ddraw
        dpar_ref[0:1, :] += jnp.sum(ddraw, axis=0, keepdims=True)
        dpar_ref[1:2, :] += d_a * a_c
        dpar_ref[2:3, :] += ddsk

    rowi = lambda b, g, i: b * nb + (nb - 1 - i)
    grp = pl.BlockSpec((L, GROUP_W), lambda b, g, i: (rowi(b, g, i), g))
    st = pl.BlockSpec((L, SSD_STATE), lambda b, g, i: (rowi(b, g, i), g))
    f = jax.ShapeDtypeStruct
    return pl.pallas_call(
        body,
        out_shape=(f((T, 1024), F32), f((T, 512), F32), f((T, 512), F32), f((T, 1024), F32),
                   f((SSD_GROUPS, T, 4), F32), f((B, SSD_GROUPS, 3, 4), F32), f((B, 1, 1024), F32)),
        grid=(B, SSD_GROUPS, nb),
        in_specs=[grp, grp, grp,
                  pl.BlockSpec((L, SSD_STATE), lambda b, g, i: (rowi(b, g, i), 8 + g)),
                  pl.BlockSpec((L, SSD_STATE), lambda b, g, i: (rowi(b, g, i), 12 + g)),
                  grp,
                  pl.BlockSpec((None, L, 4), lambda b, g, i: (g, rowi(b, g, i), 0)),
                  pl.BlockSpec((None, 4, L), lambda b, g, i: (g, 0, rowi(b, g, i))),
                  pl.BlockSpec((None, 3, 4), lambda b, g, i: (g, 0, 0)),
                  pl.BlockSpec((None, 4, 3), lambda b, g, i: (g, 0, 0)),
                  pl.BlockSpec((1, GROUP_W), lambda b, g, i: (0, g)),
                  pl.BlockSpec((None, None, None, 4, SSD_STATE, HEAD_DIM), lambda b, g, i: (b, g, nb - 1 - i, 0, 0, 0))],
        out_specs=(grp, st, st, grp,
                   pl.BlockSpec((None, L, 4), lambda b, g, i: (g, rowi(b, g, i), 0)),
                   pl.BlockSpec((None, None, 3, 4), lambda b, g, i: (b, g, 0, 0)),
                   pl.BlockSpec((None, 1, GROUP_W), lambda b, g, i: (b, 0, g))),
        scratch_shapes=[pltpu.VMEM((4, SSD_STATE, HEAD_DIM), F32)], name=name,
        compiler_params=_cp("parallel", "parallel", "arbitrary"))(dyn, Y, xc, xc, xc, proj, dtc, dtr, pcol, prow, nw, hs)


def _headnorm_fwd(name, proj, col_block, w):
    T = proj.shape[0]

    def body(x_ref, w_ref, o_ref):
        for h in range(ATT_HEADS):
            sl = slice(HEAD_DIM * h, HEAD_DIM * (h + 1))
            xh = x_ref[:, sl]
            r = lax.rsqrt(jnp.mean(xh * xh, axis=-1, keepdims=True) + EPS)
            o_ref[:, sl] = (xh * r * w_ref[...]).astype(BF16)

    return pl.pallas_call(
        body, out_shape=jax.ShapeDtypeStruct((T, 1024), BF16), grid=(T // ROW_T,),
        in_specs=[pl.BlockSpec((ROW_T, 1024), lambda i: (i, col_block)), pl.BlockSpec((1, HEAD_DIM), lambda i: (0, 0))],
        out_specs=pl.BlockSpec((ROW_T, 1024), lambda i: (i, 0)), name=name, compiler_params=_cp("parallel"))(proj, w)


def _headnorm_bwd(name, dn, proj, col_block, w):
    T = proj.shape[0]

    def body(dn_ref, x_ref, w_ref, dx_ref, dw_ref):
        @pl.when(pl.program_id(0) == 0)
        def _():
            dw_ref[...] = jnp.zeros_like(dw_ref)

        dw = jnp.zeros((1, HEAD_DIM), F32)
        for h in range(ATT_HEADS):
            sl = slice(HEAD_DIM * h, HEAD_DIM * (h + 1))
            xh = x_ref[:, sl]
            r = lax.rsqrt(jnp.mean(xh * xh, axis=-1, keepdims=True) + EPS)
            xhat = xh * r
            dnh = dn_ref[:, sl]
            dxhat = dnh * w_ref[...]
            dx_ref[:, sl] = r * (dxhat - xhat * jnp.mean(dxhat * xhat, axis=-1, keepdims=True))
            dw = dw + jnp.sum(dnh * xhat, axis=0, keepdims=True)
        dw_ref[...] += dw

    return pl.pallas_call(
        body, out_shape=(jax.ShapeDtypeStruct((T, 1024), F32), jax.ShapeDtypeStruct((1, HEAD_DIM), F32)),
        grid=(T // ROW_T,),
        in_specs=[pl.BlockSpec((ROW_T, 1024), lambda i: (i, 0)), pl.BlockSpec((ROW_T, 1024), lambda i: (i, col_block)),
                  pl.BlockSpec((1, HEAD_DIM), lambda i: (0, 0))],
        out_specs=(pl.BlockSpec((ROW_T, 1024), lambda i: (i, 0)), pl.BlockSpec((1, HEAD_DIM), lambda i: (0, 0))),
        name=name, compiler_params=_cp("arbitrary"))(dn, proj, w)


def _att_bias(nq):
    i = np.arange(ATT_B)[:, None]
    j = np.arange(ATT_B)[None, :]
    out = np.empty((nq, ATT_B, ATT_B), np.float32)
    for dblk in range(nq):
        dl = ATT_B * dblk + i - j
        cnt = ((dl >= 0) & (dl <= 128)).astype(np.float32)
        cnt += ((dl >= 0) & (dl % 4 == 0) & (dl <= 512))
        cnt += ((dl >= 0) & (dl % 16 == 0) & (dl <= 2048))
        out[dblk] = np.where(cnt > 0, np.log(np.maximum(cnt, 1.0)), NEG)
    return jnp.asarray(out)


def _row_pair(nq):
    def f(r, c):
        first = c <= r
        return jnp.where(first, r, nq - 1 - r), jnp.where(first, c, c - (r + 1))
    return f


def _col_pair(nq):
    def f(r, c):
        first = c < nq - r
        kj = jnp.where(first, r, nq - 1 - r)
        return jnp.where(first, r + c, nq - 1 - r + (c - (nq - r))), kj
    return f


ATT_SCALE = 1.0 / math.sqrt(HEAD_DIM)
V_CB = COL_V // 128


def _att_fwd(name, qn, kn, proj, bias, B):
    T = qn.shape[0]
    nq = (T // B) // ATT_B
    qk = _row_pair(nq)

    def body(q_ref, k_ref, v_ref, bias_ref, o_ref, lse_ref, m_s, l_s, acc_s):
        qi, kj = qk(pl.program_id(2), pl.program_id(3))

        @pl.when(kj == 0)
        def _():
            m_s[...] = jnp.full_like(m_s, NEG)
            l_s[...] = jnp.zeros_like(l_s)
            acc_s[...] = jnp.zeros_like(acc_s)

        bv = bias_ref[...]
        for hh in range(2):
            sl = slice(HEAD_DIM * hh, HEAD_DIM * (hh + 1))
            s = _dot(q_ref[:, sl], k_ref[:, sl], NT) * ATT_SCALE + bv
            m_prev = m_s[hh]
            m_new = jnp.maximum(m_prev, jnp.max(s, axis=-1, keepdims=True))
            alpha = jnp.exp(m_prev - m_new)
            p = jnp.exp(s - m_new)
            l_s[hh] = alpha * l_s[hh] + jnp.sum(p, axis=-1, keepdims=True)
            acc_s[hh] = alpha * acc_s[hh] + _dot(p.astype(BF16), v_ref[:, sl].astype(BF16), NN)
            m_s[hh] = m_new

        @pl.when(kj == qi)
        def _():
            for hh in range(2):
                sl = slice(HEAD_DIM * hh, HEAD_DIM * (hh + 1))
                o_ref[:, sl] = (acc_s[hh] / l_s[hh]).astype(BF16)
                lse_ref[:, hh:hh + 1] = m_s[hh] + jnp.log(l_s[hh])

    def qmap(b, hp, r, c):
        return b * nq + qk(r, c)[0], hp

    def kmap(b, hp, r, c):
        return b * nq + qk(r, c)[1], hp

    def vmap_(b, hp, r, c):
        return b * nq + qk(r, c)[1], V_CB + hp

    def bmap(b, hp, r, c):
        qi, kj = qk(r, c)
        return qi - kj, 0, 0

    def lmap(b, hp, r, c):
        return hp, b * nq + qk(r, c)[0], 0

    blk = (ATT_B, 128)
    return pl.pallas_call(
        body,
        out_shape=(jax.ShapeDtypeStruct((T, 1024), BF16), jax.ShapeDtypeStruct((ATT_HEADS // 2, T, 2), F32)),
        grid=(B, ATT_HEADS // 2, nq // 2, nq + 1),
        in_specs=[pl.BlockSpec(blk, qmap), pl.BlockSpec(blk, kmap), pl.BlockSpec(blk, vmap_),
                  pl.BlockSpec((None, ATT_B, ATT_B), bmap)],
        out_specs=(pl.BlockSpec(blk, qmap), pl.BlockSpec((None, ATT_B, 2), lmap)),
        scratch_shapes=[pltpu.VMEM((2, ATT_B, 1), F32), pltpu.VMEM((2, ATT_B, 1), F32),
                        pltpu.VMEM((2, ATT_B, HEAD_DIM), F32)],
        name=name, compiler_params=_cp("parallel", "parallel", "arbitrary", "arbitrary"))(qn, kn, proj, bias)


def _att_p_ds(q_ref, k_ref, v_ref, do_ref, o_ref, lse_ref, bv, hh):
    sl = slice(HEAD_DIM * hh, HEAD_DIM * (hh + 1))
    q = q_ref[:, sl]
    k = k_ref[:, sl]
    dof = do_ref[:, sl]
    dob = dof.astype(BF16)
    delta = jnp.sum(dof * o_ref[:, sl].astype(F32), axis=-1, keepdims=True)
    s = _dot(q, k, NT) * ATT_SCALE + bv
    p = jnp.exp(s - lse_ref[:, hh:hh + 1])
    dp = _dot(dob, v_ref[:, sl].astype(BF16), NT)
    ds = p * (dp - delta)
    return q, k, dob, p, ds


def _att_bwd_dq(name, qn, kn, proj, bias, dyn, o, lse, B):
    T = qn.shape[0]
    nq = (T // B) // ATT_B
    qk = _row_pair(nq)

    def body(q_ref, k_ref, v_ref, bias_ref, do_ref, o_ref, lse_ref, dq_ref, acc_s):
        qi, kj = qk(pl.program_id(2), pl.program_id(3))

        @pl.when(kj == 0)
        def _():
            acc_s[...] = jnp.zeros_like(acc_s)

        bv = bias_ref[...]
        for hh in range(2):
            q, k, dob, p, ds = _att_p_ds(q_ref, k_ref, v_ref, do_ref, o_ref, lse_ref, bv, hh)
            acc_s[hh] += _dot(ds.astype(BF16), k, NN) * ATT_SCALE

        @pl.when(kj == qi)
        def _():
            for hh in range(2):
                dq_ref[:, HEAD_DIM * hh:HEAD_DIM * (hh + 1)] = acc_s[hh]

    def qmap(b, hp, r, c):
        return b * nq + qk(r, c)[0], hp

    def domap(b, hp, r, c):
        return b * nq + qk(r, c)[0], 8 + hp

    def kmap(b, hp, r, c):
        return b * nq + qk(r, c)[1], hp

    def vmap_(b, hp, r, c):
        return b * nq + qk(r, c)[1], V_CB + hp

    def bmap(b, hp, r, c):
        qi, kj = qk(r, c)
        return qi - kj, 0, 0

    def lmap(b, hp, r, c):
        return hp, b * nq + qk(r, c)[0], 0

    blk = (ATT_B, 128)
    return pl.pallas_call(
        body, out_shape=jax.ShapeDtypeStruct((T, 1024), F32), grid=(B, ATT_HEADS // 2, nq // 2, nq + 1),
        in_specs=[pl.BlockSpec(blk, qmap), pl.BlockSpec(blk, kmap), pl.BlockSpec(blk, vmap_),
                  pl.BlockSpec((None, ATT_B, ATT_B), bmap), pl.BlockSpec(blk, domap), pl.BlockSpec(blk, qmap),
                  pl.BlockSpec((None, ATT_B, 2), lmap)],
        out_specs=pl.BlockSpec(blk, qmap),
        scratch_shapes=[pltpu.VMEM((2, ATT_B, HEAD_DIM), F32)],
        name=name, compiler_params=_cp("parallel", "parallel", "arbitrary", "arbitrary"))(qn, kn, proj, bias, dyn, o, lse)


def _att_bwd_dkv(name, qn, kn, proj, bias, dyn, o, lse, B):
    T = qn.shape[0]
    nq = (T // B) // ATT_B
    qk = _col_pair(nq)

    def body(q_ref, k_ref, v_ref, bias_ref, do_ref, o_ref, lse_ref, dk_ref, dv_ref, dk_s, dv_s):
        qi, kj = qk(pl.program_id(2), pl.program_id(3))

        @pl.when(qi == kj)
        def _():
            dk_s[...] = jnp.zeros_like(dk_s)
            dv_s[...] = jnp.zeros_like(dv_s)

        bv = bias_ref[...]
        for hh in range(2):
            q, k, dob, p, ds = _att_p_ds(q_ref, k_ref, v_ref, do_ref, o_ref, lse_ref, bv, hh)
            dv_s[hh] += _dot(p.astype(BF16), dob, TN)
            dk_s[hh] += _dot(ds.astype(BF16), q, TN) * ATT_SCALE

        @pl.when(qi == nq - 1)
        def _():
            for hh in range(2):
                sl = slice(HEAD_DIM * hh, HEAD_DIM * (hh + 1))
                dk_ref[:, sl] = dk_s[hh]
                dv_ref[:, sl] = dv_s[hh]

    def qmap(b, hp, r, c):
        return b * nq + qk(r, c)[0], hp

    def domap(b, hp, r, c):
        return b * nq + qk(r, c)[0], 8 + hp

    def kmap(b, hp, r, c):
        return b * nq + qk(r, c)[1], hp

    def vmap_(b, hp, r, c):
        return b * nq + qk(r, c)[1], V_CB + hp

    def bmap(b, hp, r, c):
        qi, kj = qk(r, c)
        return qi - kj, 0, 0

    def lmap(b, hp, r, c):
        return hp, b * nq + qk(r, c)[0], 0

    blk = (ATT_B, 128)
    osh = jax.ShapeDtypeStruct((T, 1024), F32)
    return pl.pallas_call(
        body, out_shape=(osh, osh), grid=(B, ATT_HEADS // 2, nq // 2, nq + 1),
        in_specs=[pl.BlockSpec(blk, qmap), pl.BlockSpec(blk, kmap), pl.BlockSpec(blk, vmap_),
                  pl.BlockSpec((None, ATT_B, ATT_B), bmap), pl.BlockSpec(blk, domap), pl.BlockSpec(blk, qmap),
                  pl.BlockSpec((None, ATT_B, 2), lmap)],
        out_specs=(pl.BlockSpec(blk, kmap), pl.BlockSpec(blk, kmap)),
        scratch_shapes=[pltpu.VMEM((2, ATT_B, HEAD_DIM), F32), pltpu.VMEM((2, ATT_B, HEAD_DIM), F32)],
        name=name, compiler_params=_cp("parallel", "parallel", "arbitrary", "arbitrary"))(qn, kn, proj, bias, dyn, o, lse)


def _group_cols(v):
    return v.reshape(SSD_GROUPS, 4)


def _ssd_params(p):
    rows = jnp.stack([_group_cols(p["dt_bias"]), _group_cols(p["a_log"]), _group_cols(p["d_skip"])], axis=1)
    return rows, jnp.swapaxes(rows, 1, 2)


def _layer_fwd(l, x, p, W, bias, B):
    T = x.shape[0]
    S = T // B
    nt = T // ROW_T
    tag = "l%d" % l
    x1, ffn1 = _ffn_fwd(tag + "f1", x, p["ffn1_norm"][None], W["g1"], W["u1"], W["d1"], l)
    h2 = _rms_fwd(tag + "_mixrms", x1, p["mix_norm"][None])
    win = W["win"][l]
    proj = _mm(tag + "_proj",
               [(h2, pl.BlockSpec((ROW_T, D_MODEL), lambda j, i, k: (i, 0)),
                 win, pl.BlockSpec((D_MODEL, PROJ_TN), lambda j, i, k: (0, j)))],
               jax.ShapeDtypeStruct((T, IN_PAD), F32), pl.BlockSpec((ROW_T, PROJ_TN), lambda j, i, k: (i, j)),
               (IN_PAD // PROJ_TN, nt, 1), NN, (ROW_T, PROJ_TN))
    xbc = proj[:, COL_XBC:COL_Q].reshape(B, S, CONV_DIM)
    xpad = jnp.pad(xbc, ((0, 0), (PAD_R, PAD_R), (0, 0)))
    cw, cbias = p["conv_w"], p["conv_b"][None]
    xc = _conv_fwd(tag + "_conv", xpad, cw, cbias).reshape(T, CONV_DIM)
    dtraw = proj[:, COL_DT:COL_DT + SSD_HEADS].reshape(T, SSD_GROUPS, 4)
    dtc = jnp.transpose(dtraw, (1, 0, 2))
    dtr = jnp.transpose(dtraw, (1, 2, 0))
    pcol, prow = _ssd_params(p)
    Y, y_ssd, hs = _ssd_fwd(tag + "_ssd", xc, proj, dtc, dtr, pcol, prow, p["ssd_norm"][None], B)
    qn = _headnorm_fwd(tag + "_qn", proj, COL_Q // 1024, p["q_norm"][None])
    kn = _headnorm_fwd(tag + "_kn", proj, COL_K // 1024, p["k_norm"][None])
    o, lse = _att_fwd(tag + "_att", qn, kn, proj, bias, B)
    ymix = jnp.concatenate([y_ssd, o], axis=1)
    x2 = _mm(tag + "_out",
             [(ymix, pl.BlockSpec((ROW_T, MIX_SH), lambda i, n, k: (i, k)),
               W["wout"], pl.BlockSpec((None, None, MIX_SH, D_MODEL), lambda i, n, k: (k, l, 0, 0)))],
             jax.ShapeDtypeStruct((T, D_MODEL), F32), pl.BlockSpec((ROW_T, D_MODEL), lambda i, n, k: (i, 0)),
             (nt, 1, N_SHARD), NN, (ROW_T, D_MODEL),
             res=(x1, pl.BlockSpec((ROW_T, D_MODEL), lambda i, n, k: (i, 0))))
    x3, ffn2 = _ffn_fwd(tag + "f2", x2, p["ffn2_norm"][None], W["g2"], W["u2"], W["d2"], l)
    saved = dict(ffn1=ffn1, x1=x1, h2=h2, proj=proj, xpad=xpad, xc=xc, dtc=dtc, dtr=dtr, Y=Y, hs=hs,
                 qn=qn, kn=kn, o=o, lse=lse, ymix=ymix, ffn2=ffn2)
    return x3, saved


def _layer_bwd(l, dx3, sv, p, W, bias, B, gbuf):
    T = dx3.shape[0]
    S = T // B
    nt = T // ROW_T
    tag = "l%db" % l
    sg = {}
    dx2, sg["ffn2_norm"], (gg2, gu2, gd2) = _ffn_bwd(tag + "f2", dx3, sv["ffn2"], p["ffn2_norm"][None],
                                                    W["g2"], W["u2"], W["d2"], l, (gbuf["g2"], gbuf["u2"], gbuf["d2"]))
    dymix = _mm(tag + "_dymix",
                [(dx2, pl.BlockSpec((ROW_T, D_MODEL), lambda n, i, k: (i, 0)),
                  W["wout"], pl.BlockSpec((None, None, MIX_SH, D_MODEL), lambda n, i, k: (n, l, 0, 0)))],
                jax.ShapeDtypeStruct((T, MIX_W), F32), pl.BlockSpec((ROW_T, MIX_SH), lambda n, i, k: (i, n)),
                (N_SHARD, nt, 1), NT, (ROW_T, MIX_SH))
    gwout = _mm(tag + "_dwout",
                [(sv["ymix"], pl.BlockSpec((ROW_T, MIX_SH), lambda m, n, k: (k, m)),
                  dx2, pl.BlockSpec((ROW_T, D_MODEL), lambda m, n, k: (k, 0)))],
                jax.ShapeDtypeStruct((N_SHARD, DEPTH, MIX_SH, D_MODEL), BF16),
                pl.BlockSpec((None, None, MIX_SH, D_MODEL), lambda m, n, k: (m, l, 0, 0)),
                (N_SHARD, 1, nt), TN, (MIX_SH, D_MODEL), prev=gbuf["wout"])
    proj = sv["proj"]
    dqn = _att_bwd_dq(tag + "_attdq", sv["qn"], sv["kn"], proj, bias, dymix, sv["o"], sv["lse"], B)
    dkn, dv = _att_bwd_dkv(tag + "_attdkv", sv["qn"], sv["kn"], proj, bias, dymix, sv["o"], sv["lse"], B)
    dq, sg["q_norm"] = _headnorm_bwd(tag + "_qnb", dqn, proj, COL_Q // 1024, p["q_norm"][None])
    dk, sg["k_norm"] = _headnorm_bwd(tag + "_knb", dkn, proj, COL_K // 1024, p["k_norm"][None])
    pcol, prow = _ssd_params(p)
    dxs, dB, dC, dz, ddt, dpar, dnw = _ssd_bwd(tag + "_ssdb", dymix, sv["Y"], sv["xc"], proj, sv["dtc"], sv["dtr"],
                                               pcol, prow, p["ssd_norm"][None], sv["hs"], B)
    dpar = jnp.sum(dpar, axis=0)
    sg["dt_bias"] = dpar[:, 0, :].reshape(SSD_HEADS)
    sg["a_log"] = dpar[:, 1, :].reshape(SSD_HEADS)
    sg["d_skip"] = dpar[:, 2, :].reshape(SSD_HEADS)
    sg["ssd_norm"] = jnp.sum(dnw, axis=0)
    dxc = jnp.concatenate([dxs, dB, dC], axis=1).reshape(B, S, CONV_DIM)
    dxc_pad = jnp.pad(dxc, ((0, 0), (0, PAD_R), (0, 0)))
    dxbc, sg["conv_w"], sg["conv_b"] = _conv_bwd(tag + "_convb", sv["xpad"], dxc_pad, p["conv_w"], p["conv_b"][None])
    ddt16 = jnp.transpose(ddt, (1, 0, 2)).reshape(T, SSD_HEADS)
    dproj = jnp.concatenate([dz, dxbc.reshape(T, CONV_DIM), dq, dk, dv, ddt16,
                             jnp.zeros((T, IN_PAD - COL_DT - SSD_HEADS), F32)], axis=1).astype(BF16)
    win = W["win"][l]
    gwin = _mm(tag + "_dwin",
               [(sv["h2"], pl.BlockSpec((ROW_T, D_MODEL), lambda n, m, k: (k, 0)),
                 dproj, pl.BlockSpec((ROW_T, PROJ_TN), lambda n, m, k: (k, n)))],
               jax.ShapeDtypeStruct((D_MODEL, IN_PAD), BF16), pl.BlockSpec((D_MODEL, PROJ_TN), lambda n, m, k: (0, n)),
               (IN_PAD // PROJ_TN, 1, nt), TN, (D_MODEL, PROJ_TN))
    dh2 = _mm(tag + "_dh2",
              [(dproj, pl.BlockSpec((ROW_T, PROJ_TN), lambda i, n, k: (i, k)),
                win, pl.BlockSpec((D_MODEL, PROJ_TN), lambda i, n, k: (0, k)))],
              jax.ShapeDtypeStruct((T, D_MODEL), F32), pl.BlockSpec((ROW_T, D_MODEL), lambda i, n, k: (i, 0)),
              (nt, 1, IN_PAD // PROJ_TN), NT, (ROW_T, D_MODEL))
    dx1, sg["mix_norm"] = _rms_bwd(tag + "_mixrmsb", dh2, sv["x1"], p["mix_norm"][None], dx2)
    dx0, sg["ffn1_norm"], (gg1, gu1, gd1) = _ffn_bwd(tag + "f1", dx1, sv["ffn1"], p["ffn1_norm"][None],
                                                    W["g1"], W["u1"], W["d1"], l, (gbuf["g1"], gbuf["u1"], gbuf["d1"]))
    gbuf = dict(g1=gg1, u1=gu1, d1=gd1, g2=gg2, u2=gu2, d2=gd2, wout=gwout)
    return dx0, sg, gbuf, gwin


def _win_pack(w):
    return jnp.concatenate([w[:, :3072], w[:, 3088:], w[:, 3072:3088],
                            jnp.zeros((w.shape[0], IN_PAD - IN_PROJ), w.dtype)], axis=1)


def _win_unpack(g):
    return jnp.concatenate([g[:, :3072], g[:, COL_DT:COL_DT + SSD_HEADS], g[:, 3072:COL_DT]], axis=1)


def _local_step(x, target, small, W, B):
    T = x.shape[0]
    nq = (T // B) // ATT_B
    bias = _att_bias(nq)
    saved = []
    h = x
    for l in range(DEPTH):
        p = {k: v[l] for k, v in small.items()}
        h, sv = _layer_fwd(l, h, p, W, bias, B)
        saved.append(sv)
    dy, lsum = _loss_grad("loss", h, target)
    gbuf = dict(g1=None, u1=None, d1=None, g2=None, u2=None, d2=None, wout=None)
    sgrads = [None] * DEPTH
    gwin = [None] * DEPTH
    d = dy
    for l in reversed(range(DEPTH)):
        p = {k: v[l] for k, v in small.items()}
        d, sgrads[l], gbuf, gwin[l] = _layer_bwd(l, d, saved[l], p, W, bias, B, gbuf)
    return lsum, d, sgrads, gbuf, gwin


MESH = pl.DeviceIdType.MESH
ANY = pl.BlockSpec(memory_space=pl.ANY)


def _place():
    return lax.axis_index("x"), lax.axis_index("y"), lax.axis_index("c")


def _other_chips(x, y):
    return [(1 - x, y), (x, 1 - y), (1 - x, 1 - y)]


def _gather_big(own):
    n = len(own)

    def body(*refs):
        src, dst = refs[:n], refs[n:2 * n]
        send, recv, loc = refs[2 * n:]
        x, y, c = _place()
        me = 2 * x + y
        local = [pltpu.make_async_copy(src[a], dst[a].at[me], loc.at[a]) for a in range(n)]
        for cp in local:
            cp.start()
        chips = _other_chips(x, y)
        sends = []
        for k, (px, py) in enumerate(chips):
            for a in range(n):
                cp = pltpu.make_async_remote_copy(src_ref=src[a], dst_ref=dst[a].at[me], send_sem=send.at[k * n + a],
                                                  recv_sem=recv.at[k * n + a], device_id=(px, py, c), device_id_type=MESH)
                cp.start()
                sends.append(cp)
        for k, (px, py) in enumerate(chips):
            for a in range(n):
                pltpu.make_async_remote_copy(src_ref=src[a], dst_ref=dst[a].at[2 * px + py], send_sem=send.at[k * n + a],
                                             recv_sem=recv.at[k * n + a], device_id=(px, py, c),
                                             device_id_type=MESH).wait_recv()
        for cp in sends:
            cp.wait_send()
        for cp in local:
            cp.wait()

    return pl.pallas_call(
        body, out_shape=[jax.ShapeDtypeStruct((N_SHARD,) + o.shape, o.dtype) for o in own],
        in_specs=[ANY] * n, out_specs=[ANY] * n,
        scratch_shapes=[pltpu.SemaphoreType.DMA((3 * n,)), pltpu.SemaphoreType.DMA((3 * n,)), pltpu.SemaphoreType.DMA((n,))],
        name="gather_weights")(*own)


def _scatter_grads(grads):
    n = len(grads)

    def body(*refs):
        src, dst = refs[:n], refs[n:2 * n]
        send, recv = refs[2 * n:]
        x, y, c = _place()
        chips = _other_chips(x, y)
        sends = []
        for k, (px, py) in enumerate(chips):
            for a in range(n):
                cp = pltpu.make_async_remote_copy(src_ref=src[a].at[2 * px + py], dst_ref=dst[a].at[k],
                                                  send_sem=send.at[k * n + a], recv_sem=recv.at[k * n + a],
                                                  device_id=(px, py, c), device_id_type=MESH)
                cp.start()
                sends.append(cp)
        for k, (px, py) in enumerate(chips):
            for a in range(n):
                pltpu.make_async_remote_copy(src_ref=src[a].at[2 * px + py], dst_ref=dst[a].at[k],
                                             send_sem=send.at[k * n + a], recv_sem=recv.at[k * n + a],
                                             device_id=(px, py, c), device_id_type=MESH).wait_recv()
        for cp in sends:
            cp.wait_send()

    return pl.pallas_call(
        body, out_shape=[jax.ShapeDtypeStruct((3,) + g.shape[1:], g.dtype) for g in grads],
        in_specs=[ANY] * n, out_specs=[ANY] * n,
        scratch_shapes=[pltpu.SemaphoreType.DMA((3 * n,)), pltpu.SemaphoreType.DMA((3 * n,))],
        name="scatter_grads")(*grads)


def _swap_sibling(parts):
    n = len(parts)

    def body(*refs):
        src, dst = refs[:n], refs[n:2 * n]
        send, recv = refs[2 * n:]
        x, y, c = _place()
        cps = [pltpu.make_async_remote_copy(src_ref=src[a], dst_ref=dst[a], send_sem=send.at[a], recv_sem=recv.at[a],
                                            device_id=(x, y, 1 - c), device_id_type=MESH) for a in range(n)]
        for cp in cps:
            cp.start()
        for cp in cps:
            cp.wait_recv()
        for cp in cps:
            cp.wait_send()

    return pl.pallas_call(
        body, out_shape=[jax.ShapeDtypeStruct(p.shape, p.dtype) for p in parts],
        in_specs=[ANY] * n, out_specs=[ANY] * n,
        scratch_shapes=[pltpu.SemaphoreType.DMA((n,)), pltpu.SemaphoreType.DMA((n,))],
        name="swap_sibling")(*parts)


def _allreduce_small(name, v):
    R = v.shape[0]

    def body(v_ref, o_ref, buf, send, recv):
        x, y, c = _place()
        me = 4 * x + 2 * y + c
        buf[me] = v_ref[...]
        cps = []
        for k in range(1, 8):
            fx, fy, fc = (k >> 2) & 1, (k >> 1) & 1, k & 1
            px = 1 - x if fx else x
            py = 1 - y if fy else y
            pc = 1 - c if fc else c
            cp = pltpu.make_async_remote_copy(src_ref=v_ref, dst_ref=buf.at[me], send_sem=send.at[k - 1],
                                              recv_sem=recv.at[k - 1], device_id=(px, py, pc), device_id_type=MESH)
            cp.start()
            cps.append((cp, 4 * px + 2 * py + pc))
        for k, (cp, peer) in enumerate(cps):
            pltpu.make_async_remote_copy(src_ref=v_ref, dst_ref=buf.at[peer], send_sem=send.at[k], recv_sem=recv.at[k],
                                         device_id=(x, y, c), device_id_type=MESH).wait_recv()
        for cp, _ in cps:
            cp.wait_send()
        acc = buf[0]
        for d in range(1, 8):
            acc = acc + buf[d]
        o_ref[...] = acc

    return pl.pallas_call(
        body, out_shape=jax.ShapeDtypeStruct((R, 128), F32),
        in_specs=[pl.BlockSpec(memory_space=pltpu.VMEM)], out_specs=pl.BlockSpec(memory_space=pltpu.VMEM),
        scratch_shapes=[pltpu.VMEM((8, R, 128), F32), pltpu.SemaphoreType.DMA((7,)), pltpu.SemaphoreType.DMA((7,))],
        name=name)(v)


def _row_tile(r):
    for t in (256, 128, 64, 32, 16, 8):
        if r % t == 0:
            return t
    raise ValueError(r)


def _sum4(name, own, got):
    R, C = own.shape
    tr = _row_tile(R)

    def body(o_ref, g_ref, s_ref):
        s = o_ref[...].astype(F32)
        for k in range(3):
            s = s + g_ref[k].astype(F32)
        s_ref[...] = s

    return pl.pallas_call(
        body, out_shape=jax.ShapeDtypeStruct((R, C), F32), grid=(R // tr,),
        in_specs=[pl.BlockSpec((tr, C), lambda i: (i, 0)), pl.BlockSpec((3, tr, C), lambda i: (0, i, 0))],
        out_specs=pl.BlockSpec((tr, C), lambda i: (i, 0)), name=name, compiler_params=_cp("parallel"))(own, got)


def _adamw(name, w, gparts, m, v):
    R, C = w.shape
    tr = _row_tile(R)
    ng = len(gparts)
    c1 = 1.0 - ADAM_B1 ** ADAM_STEP
    c2 = 1.0 - ADAM_B2 ** ADAM_STEP

    def body(*refs):
        w_ref = refs[0]
        g_refs = refs[1:1 + ng]
        m_ref, v_ref, go_ref, d_ref, mo_ref, vo_ref = refs[1 + ng:]
        g = g_refs[0][...]
        for r in g_refs[1:]:
            g = g + r[...]
        mn = ADAM_B1 * m_ref[...] + (1.0 - ADAM_B1) * g
        vn = ADAM_B2 * v_ref[...] + (1.0 - ADAM_B2) * (g * g)
        go_ref[...] = g
        mo_ref[...] = mn
        vo_ref[...] = vn
        d_ref[...] = -ADAM_LR * ((mn / c1) / (jnp.sqrt(vn / c2) + ADAM_EPS) + ADAM_WD * w_ref[...])

    blk = pl.BlockSpec((tr, C), lambda i: (i, 0))
    osh = jax.ShapeDtypeStruct((R, C), F32)
    return pl.pallas_call(
        body, out_shape=(osh, osh, osh, osh), grid=(R // tr,), in_specs=[blk] * (3 + ng), out_specs=(blk,) * 4,
        name=name, compiler_params=_cp("parallel"))(w, *gparts, m, v)


BIG = [("ffn1_w_gate", "g1"), ("ffn1_w_up", "u1"), ("ffn1_w_down", "d1"), ("w_in", "win"), ("w_out", "wout"),
       ("ffn2_w_gate", "g2"), ("ffn2_w_up", "u2"), ("ffn2_w_down", "d2")]
SMALL = ["ffn1_norm", "mix_norm", "conv_b", "dt_bias", "a_log", "d_skip", "ssd_norm", "q_norm", "k_norm", "ffn2_norm"]
WEIGHTS = ["ffn1_norm", "ffn1_w_gate", "ffn1_w_up", "ffn1_w_down", "mix_norm", "w_in", "conv_w", "conv_b", "dt_bias",
           "a_log", "d_skip", "ssd_norm", "q_norm", "k_norm", "w_out", "ffn2_norm", "ffn2_w_gate", "ffn2_w_up",
           "ffn2_w_down"]
CONV_SH = CONV_DIM // N_SHARD


def _pad128(v):
    v = v.reshape(-1)
    return jnp.pad(v, (0, (-v.shape[0]) % 128))


def _pack(pieces):
    flat, offs, pos = [], [], 0
    for p in pieces:
        q = _pad128(p.astype(F32))
        offs.append(pos)
        pos += q.shape[0] // 128
        flat.append(q)
    total = -(-pos // 8) * 8
    out = jnp.concatenate(flat + [jnp.zeros(((total - pos) * 128,), F32)]).reshape(total, 128)
    return out, offs


def _unpack(packed, offs, shapes):
    out = []
    for off, shp in zip(offs, shapes):
        n = int(np.prod(shp))
        rows = -(-n // 128)
        out.append(packed[off:off + rows].reshape(-1)[:n].reshape(shp))
    return out


def kernel(x, ffn1_norm, ffn1_w_gate, ffn1_w_up, ffn1_w_down, mix_norm, w_in, conv_w, conv_b, dt_bias, a_log, d_skip, ssd_norm, q_norm, k_norm, w_out, ffn2_norm, ffn2_w_gate, ffn2_w_up, ffn2_w_down, loss_target, m_ffn1_norm, m_ffn1_w_gate, m_ffn1_w_up, m_ffn1_w_down, m_mix_norm, m_w_in, m_conv_w, m_conv_b, m_dt_bias, m_a_log, m_d_skip, m_ssd_norm, m_q_norm, m_k_norm, m_w_out, m_ffn2_norm, m_ffn2_w_gate, m_ffn2_w_up, m_ffn2_w_down, v_ffn1_norm, v_ffn1_w_gate, v_ffn1_w_up, v_ffn1_w_down, v_mix_norm, v_w_in, v_conv_w, v_conv_b, v_dt_bias, v_a_log, v_d_skip, v_ssd_norm, v_q_norm, v_k_norm, v_w_out, v_ffn2_norm, v_ffn2_w_gate, v_ffn2_w_up, v_ffn2_w_down):
    A = dict(locals())
    ix, iy, ic = _place()
    me = 2 * ix + iy
    B, S, _ = x.shape
    T = B * S

    own = [A[name].astype(BF16) for name, _ in BIG]
    gathered = _gather_big(own)
    W = {key: g for (_, key), g in zip(BIG, gathered)}
    W["win"] = [_win_pack(jnp.concatenate([W["win"][j, l] for j in range(N_SHARD)], axis=1)) for l in range(DEPTH)]
    placed = lax.dynamic_update_slice(jnp.zeros((DEPTH, CONV_K, CONV_DIM), F32),
                                      conv_w * (ic == 0).astype(F32), (0, 0, me * CONV_SH))
    conv_full = _allreduce_small("gather_conv_w", placed.reshape(-1, 128)).reshape(DEPTH, CONV_K, CONV_DIM)

    small = {name: A[name] for name in SMALL}
    small["conv_w"] = conv_full
    lsum, dx, sgrads, gbuf, gwin = _local_step(x.reshape(T, D_MODEL), loss_target.reshape(T, D_MODEL), small, W, B)

    names = SMALL + ["conv_w"]
    pieces = [jnp.stack([sgrads[l][n].reshape(small[n].shape[1:]) for l in range(DEPTH)]) for n in names]
    pieces.append(0.5 / D_MODEL * jnp.sum(lsum))
    packed, offs = _pack(pieces)
    red = _allreduce_small("allreduce_small", packed)
    shapes = [small[n].shape for n in names] + [()]
    red = _unpack(red, offs, shapes)
    loss = red[-1]
    sg = dict(zip(names, red[:-1]))

    gwin_st = jnp.stack([jnp.transpose(_win_unpack(gwin[l]).reshape(D_MODEL, N_SHARD, IN_SH), (1, 0, 2))
                         for l in range(DEPTH)], axis=1)
    gbuf = dict(gbuf, win=gwin_st)
    glist = [gbuf[key] for _, key in BIG]
    got = _scatter_grads(glist)
    sums = []
    for (name, key), g, r in zip(BIG, glist, got):
        _, _, R, C = g.shape
        mine = lax.dynamic_index_in_dim(g, me, axis=0, keepdims=False).reshape(DEPTH * R, C)
        sums.append(_sum4("sum_" + key, mine, r.reshape(3, DEPTH * R, C)))
    theirs = _swap_sibling(sums)

    out = {}
    for (name, key), s, t in zip(BIG, sums, theirs):
        shp = A[name].shape
        flat = lambda a: a.reshape(shp[0] * shp[1], shp[2])
        res = _adamw("adamw_" + key, flat(A[name]), [s, t], flat(A["m_" + name]), flat(A["v_" + name]))
        out[name] = [r.reshape(shp) for r in res]

    wp, offs = _pack([A[n] for n in SMALL])
    gp, _ = _pack([sg[n] for n in SMALL])
    mp, _ = _pack([A["m_" + n] for n in SMALL])
    vp, _ = _pack([A["v_" + n] for n in SMALL])
    res = _adamw("adamw_small", wp, [gp], mp, vp)
    shapes = [A[n].shape for n in SMALL]
    res = [_unpack(r, offs, shapes) for r in res]
    for i, n in enumerate(SMALL):
        out[n] = [res[q][i] for q in range(4)]
    gcw = lax.dynamic_slice_in_dim(sg["conv_w"], me * CONV_SH, CONV_SH, axis=2)
    flat = lambda a: a.reshape(DEPTH * CONV_K, CONV_SH)
    res = _adamw("adamw_conv_w", flat(conv_w), [flat(gcw)], flat(m_conv_w), flat(v_conv_w))
    out["conv_w"] = [r.reshape(conv_w.shape) for r in res]

    outs = [loss, dx.reshape(B, S, D_MODEL)]
    for q in range(4):
        outs += [out[n][q] for n in WEIGHTS]
    return tuple(outs)
```

```python
import functools
import math

import numpy as np
import jax
import jax.numpy as jnp
from jax import lax
from jax.experimental import pallas as pl
from jax.experimental.pallas import tpu as pltpu

F32 = jnp.float32
BF16 = jnp.bfloat16

D_MODEL = 1024
DEPTH = 2
N_SHARD = 4
D_FF = 2816
FF_SH = D_FF // N_SHARD
SSD_HEADS = 16
HEAD_DIM = 64
SSD_GROUPS = 4
GROUP_W = 256
SSD_STATE = 128
CONV_K = 4
CONV_DIM = 2048
ATT_HEADS = 16
MIX_W = 2048
MIX_SH = MIX_W // N_SHARD
IN_PROJ = 6160
IN_SH = IN_PROJ // N_SHARD
IN_PAD = 6272
PROJ_TN = 896
COL_Z, COL_XBC, COL_Q, COL_K, COL_V, COL_DT = 0, 1024, 3072, 4096, 5120, 6144
EPS = 1e-6
NEG = -1e30
SSD_L = 256
ATT_B = 256
ROW_T = 512
CONV_CT = 256
CONV_R = 256
PAD_R = 8

ADAM_LR, ADAM_B1, ADAM_B2, ADAM_EPS, ADAM_WD, ADAM_STEP = 0.001, 0.9, 0.999, 1e-08, 0.01, 10

NN = (((1,), (0,)), ((), ()))
NT = (((1,), (1,)), ((), ()))
TN = (((0,), (0,)), ((), ()))

VMEM_LIMIT = 56 * 1024 * 1024


def _cp(*sem):
    return pltpu.CompilerParams(dimension_semantics=sem, vmem_limit_bytes=VMEM_LIMIT)


def _dot(a, b, dims):
    return lax.dot_general(a, b, dims, preferred_element_type=F32)


def _sigmoid(x):
    return 1.0 / (1.0 + jnp.exp(-x))


def _softplus(x):
    return jnp.maximum(x, 0.0) + jnp.log(1.0 + jnp.exp(-jnp.abs(x)))


def _mm(name, pairs, out_shape, out_spec, grid, dims, acc_shape, res=None, scale=1.0, prev=None):
    nk = grid[2]
    npair = len(pairs)

    def body(*refs):
        ab = refs[:2 * npair]
        pos = 2 * npair
        res_ref = None
        if res is not None:
            res_ref = refs[pos]
            pos += 1
        if prev is not None:
            pos += 1
        out_ref, acc = refs[pos], refs[pos + 1]
        k = pl.program_id(2)

        @pl.when(k == 0)
        def _():
            acc[...] = jnp.zeros_like(acc)

        s = None
        for p in range(npair):
            d = _dot(ab[2 * p][...].astype(BF16), ab[2 * p + 1][...].astype(BF16), dims)
            s = d if s is None else s + d
        acc[...] += s

        @pl.when(k == nk - 1)
        def _():
            r = acc[...]
            if scale != 1.0:
                r = r * scale
            if res_ref is not None:
                r = r + res_ref[...]
            out_ref[...] = r.astype(out_ref.dtype)

    args, specs = [], []
    for a, a_spec, b, b_spec in pairs:
        args += [a, b]
        specs += [a_spec, b_spec]
    if res is not None:
        args.append(res[0])
        specs.append(res[1])
    aliases = {}
    if prev is not None:
        aliases = {len(args): 0}
        args.append(prev)
        specs.append(pl.BlockSpec(memory_space=pl.ANY))
    return pl.pallas_call(
        body, out_shape=out_shape, grid=grid, in_specs=specs, out_specs=out_spec,
        scratch_shapes=[pltpu.VMEM(acc_shape, F32)], input_output_aliases=aliases, name=name,
        compiler_params=_cp("parallel", "parallel", "arbitrary"))(*args)


def _rms_fwd(name, x, w):
    T = x.shape[0]

    def body(x_ref, w_ref, o_ref):
        xv = x_ref[...]
        r = lax.rsqrt(jnp.mean(xv * xv, axis=-1, keepdims=True) + EPS)
        o_ref[...] = (xv * r * w_ref[...]).astype(BF16)

    return pl.pallas_call(
        body, out_shape=jax.ShapeDtypeStruct((T, D_MODEL), BF16), grid=(T // ROW_T,),
        in_specs=[pl.BlockSpec((ROW_T, D_MODEL), lambda i: (i, 0)), pl.BlockSpec((1, D_MODEL), lambda i: (0, 0))],
        out_specs=pl.BlockSpec((ROW_T, D_MODEL), lambda i: (i, 0)), name=name, compiler_params=_cp("parallel"))(x, w)


def _rms_bwd(name, dh, x, w, dres):
    T = x.shape[0]

    def body(dh_ref, x_ref, w_ref, dres_ref, dx_ref, dw_ref):
        @pl.when(pl.program_id(0) == 0)
        def _():
            dw_ref[...] = jnp.zeros_like(dw_ref)

        xv = x_ref[...]
        r = lax.rsqrt(jnp.mean(xv * xv, axis=-1, keepdims=True) + EPS)
        xhat = xv * r
        dhv = dh_ref[...]
        dxhat = dhv * w_ref[...]
        m = jnp.mean(dxhat * xhat, axis=-1, keepdims=True)
        dx_ref[...] = dres_ref[...] + r * (dxhat - xhat * m)
        dw_ref[...] += jnp.sum(dhv * xhat, axis=0, keepdims=True)

    row = pl.BlockSpec((ROW_T, D_MODEL), lambda i: (i, 0))
    vec = pl.BlockSpec((1, D_MODEL), lambda i: (0, 0))
    return pl.pallas_call(
        body, out_shape=(jax.ShapeDtypeStruct((T, D_MODEL), F32), jax.ShapeDtypeStruct((1, D_MODEL), F32)),
        grid=(T // ROW_T,), in_specs=[row, row, vec, row], out_specs=(row, vec), name=name,
        compiler_params=_cp("arbitrary"))(dh, x, w, dres)


def _loss_grad(name, y, t):
    T = y.shape[0]

    def body(y_ref, t_ref, dy_ref, l_ref):
        @pl.when(pl.program_id(0) == 0)
        def _():
            l_ref[...] = jnp.zeros_like(l_ref)

        e = y_ref[...] - t_ref[...]
        dy_ref[...] = e * (1.0 / D_MODEL)
        l_ref[...] += jnp.sum(e * e, axis=0, keepdims=True)

    row = pl.BlockSpec((ROW_T, D_MODEL), lambda i: (i, 0))
    vec = pl.BlockSpec((1, D_MODEL), lambda i: (0, 0))
    return pl.pallas_call(
        body, out_shape=(jax.ShapeDtypeStruct((T, D_MODEL), F32), jax.ShapeDtypeStruct((1, D_MODEL), F32)),
        grid=(T // ROW_T,), in_specs=[row, row], out_specs=(row, vec), name=name,
        compiler_params=_cp("arbitrary"))(y, t)


def _ffn_gate_up(name, h, wg, wu, l):
    T = h.shape[0]

    def body(h_ref, wg_ref, wu_ref, g_ref, u_ref, a_ref):
        hv = h_ref[...]
        g = _dot(hv, wg_ref[...], NN)
        u = _dot(hv, wu_ref[...], NN)
        g_ref[...] = g.astype(BF16)
        u_ref[...] = u.astype(BF16)
        a_ref[...] = (g * _sigmoid(g) * u).astype(BF16)

    wspec = pl.BlockSpec((None, None, D_MODEL, FF_SH), lambda j, i: (j, l, 0, 0))
    ospec = pl.BlockSpec((None, ROW_T, FF_SH), lambda j, i: (j, i, 0))
    osh = jax.ShapeDtypeStruct((N_SHARD, T, FF_SH), BF16)
    return pl.pallas_call(
        body, out_shape=(osh, osh, osh), grid=(N_SHARD, T // ROW_T),
        in_specs=[pl.BlockSpec((ROW_T, D_MODEL), lambda j, i: (i, 0)), wspec, wspec],
        out_specs=(ospec, ospec, ospec), name=name, compiler_params=_cp("parallel", "parallel"))(h, wg, wu)


def _ffn_dact(name, dx, wd, g, u, l):
    T = dx.shape[0]

    def body(dx_ref, wd_ref, g_ref, u_ref, dg_ref, du_ref):
        da = 0.5 * _dot(dx_ref[...].astype(BF16), wd_ref[...], NT)
        gv = g_ref[...].astype(F32)
        uv = u_ref[...].astype(F32)
        sg = _sigmoid(gv)
        dg_ref[...] = (da * uv * (sg * (1.0 + gv * (1.0 - sg)))).astype(BF16)
        du_ref[...] = (da * gv * sg).astype(BF16)

    aspec = pl.BlockSpec((None, ROW_T, FF_SH), lambda j, i: (j, i, 0))
    osh = jax.ShapeDtypeStruct((N_SHARD, T, FF_SH), BF16)
    return pl.pallas_call(
        body, out_shape=(osh, osh), grid=(N_SHARD, T // ROW_T),
        in_specs=[pl.BlockSpec((ROW_T, D_MODEL), lambda j, i: (i, 0)),
                  pl.BlockSpec((None, None, FF_SH, D_MODEL), lambda j, i: (j, l, 0, 0)), aspec, aspec],
        out_specs=(aspec, aspec), name=name, compiler_params=_cp("parallel", "parallel"))(dx, wd, g, u)


def _ffn_fwd(tag, x, nw, wg, wu, wd, l):
    T = x.shape[0]
    h = _rms_fwd(tag + "_rms", x, nw)
    g, u, a = _ffn_gate_up(tag + "_gu", h, wg, wu, l)
    nt = T // ROW_T
    xo = _mm(tag + "_down",
             [(a, pl.BlockSpec((None, ROW_T, FF_SH), lambda i, n, k: (k, i, 0)),
               wd, pl.BlockSpec((None, None, FF_SH, D_MODEL), lambda i, n, k: (k, l, 0, 0)))],
             jax.ShapeDtypeStruct((T, D_MODEL), F32), pl.BlockSpec((ROW_T, D_MODEL), lambda i, n, k: (i, 0)),
             (nt, 1, N_SHARD), NN, (ROW_T, D_MODEL),
             res=(x, pl.BlockSpec((ROW_T, D_MODEL), lambda i, n, k: (i, 0))), scale=0.5)
    return xo, (x, h, g, u, a)


def _ffn_bwd(tag, dxo, saved, nw, wg, wu, wd, l, gbuf):
    x, h, g, u, a = saved
    T = x.shape[0]
    nt = T // ROW_T
    dg, du = _ffn_dact(tag + "_dact", dxo, wd, g, u, l)
    act = lambda f: pl.BlockSpec((None, ROW_T, FF_SH), f)
    gd = _mm(tag + "_dwd",
             [(a, act(lambda m, n, k: (m, k, 0)), dxo, pl.BlockSpec((ROW_T, D_MODEL), lambda m, n, k: (k, 0)))],
             jax.ShapeDtypeStruct((N_SHARD, DEPTH, FF_SH, D_MODEL), BF16),
             pl.BlockSpec((None, None, FF_SH, D_MODEL), lambda m, n, k: (m, l, 0, 0)),
             (N_SHARD, 1, nt), TN, (FF_SH, D_MODEL), scale=0.5, prev=gbuf[2])
    hspec = pl.BlockSpec((ROW_T, D_MODEL), lambda j, n, k: (k, 0))
    gsh = jax.ShapeDtypeStruct((N_SHARD, DEPTH, D_MODEL, FF_SH), BF16)
    gspec = pl.BlockSpec((None, None, D_MODEL, FF_SH), lambda j, n, k: (j, l, 0, 0))
    gg = _mm(tag + "_dwg", [(h, hspec, dg, act(lambda j, n, k: (j, k, 0)))], gsh, gspec,
             (N_SHARD, 1, nt), TN, (D_MODEL, FF_SH), prev=gbuf[0])
    gu = _mm(tag + "_dwu", [(h, hspec, du, act(lambda j, n, k: (j, k, 0)))], gsh, gspec,
             (N_SHARD, 1, nt), TN, (D_MODEL, FF_SH), prev=gbuf[1])
    wspec = pl.BlockSpec((None, None, D_MODEL, FF_SH), lambda i, n, k: (k, l, 0, 0))
    dh = _mm(tag + "_dh",
             [(dg, act(lambda i, n, k: (k, i, 0)), wg, wspec), (du, act(lambda i, n, k: (k, i, 0)), wu, wspec)],
             jax.ShapeDtypeStruct((T, D_MODEL), F32), pl.BlockSpec((ROW_T, D_MODEL), lambda i, n, k: (i, 0)),
             (nt, 1, N_SHARD), NT, (ROW_T, D_MODEL))
    dx, dnw = _rms_bwd(tag + "_rmsb", dh, x, nw, dxo)
    return dx, dnw, (gg, gu, gd)


def _conv_fwd(name, xpad, w, b):
    B, SP, C = xpad.shape
    S = SP - 2 * PAD_R

    def body(x_ref, w_ref, b_ref, o_ref):
        wv = w_ref[...]
        for c in range(S // CONV_R):
            r0 = c * CONV_R
            ch = x_ref[pl.ds(r0, CONV_R + PAD_R), :]
            pre = ch[PAD_R:] * wv[3:4] + b_ref[...]
            for s in range(1, CONV_K):
                pre = pre + pltpu.roll(ch, s, axis=0)[PAD_R:] * wv[3 - s:4 - s]
            o_ref[pl.ds(r0, CONV_R), :] = pre * _sigmoid(pre)

    return pl.pallas_call(
        body, out_shape=jax.ShapeDtypeStruct((B, S, C), F32), grid=(B, C // CONV_CT),
        in_specs=[pl.BlockSpec((None, SP, CONV_CT), lambda bi, ci: (bi, 0, ci)),
                  pl.BlockSpec((CONV_K, CONV_CT), lambda bi, ci: (0, ci)),
                  pl.BlockSpec((1, CONV_CT), lambda bi, ci: (0, ci))],
        out_specs=pl.BlockSpec((None, S, CONV_CT), lambda bi, ci: (bi, 0, ci)), name=name,
        compiler_params=_cp("parallel", "parallel"))(xpad, w, b)


def _conv_bwd(name, xpad, dxc_pad, w, b):
    B, SP, C = xpad.shape
    S = SP - 2 * PAD_R
    RW = CONV_R + PAD_R

    def body(x_ref, d_ref, w_ref, b_ref, dx_ref, dw_ref, db_ref):
        @pl.when(pl.program_id(1) == 0)
        def _():
            dw_ref[...] = jnp.zeros_like(dw_ref)
            db_ref[...] = jnp.zeros_like(db_ref)

        wv = w_ref[...]
        dw = [jnp.zeros((1, CONV_CT), F32) for _ in range(CONV_K)]
        db = jnp.zeros((1, CONV_CT), F32)
        for c in range(S // CONV_R):
            r0 = c * CONV_R
            ch = x_ref[pl.ds(r0, RW + PAD_R), :]
            xs = [ch[PAD_R:]] + [pltpu.roll(ch, s, axis=0)[PAD_R:] for s in range(1, CONV_K)]
            pre = b_ref[...] + xs[0] * wv[3:4]
            for s in range(1, CONV_K):
                pre = pre + xs[s] * wv[3 - s:4 - s]
            sg = _sigmoid(pre)
            dpre = d_ref[pl.ds(r0, RW), :] * (sg * (1.0 + pre * (1.0 - sg)))
            dx = dpre[:CONV_R] * wv[3:4]
            for s in range(1, CONV_K):
                dx = dx + pltpu.roll(dpre, RW - s, axis=0)[:CONV_R] * wv[3 - s:4 - s]
            dx_ref[pl.ds(r0, CONV_R), :] = dx
            dcur = dpre[:CONV_R]
            db = db + jnp.sum(dcur, axis=0, keepdims=True)
            for s in range(CONV_K):
                dw[3 - s] = dw[3 - s] + jnp.sum(dcur * xs[s][:CONV_R], axis=0, keepdims=True)
        db_ref[...] += db
        for k in range(CONV_K):
            dw_ref[k:k + 1, :] += dw[k]

    return pl.pallas_call(
        body,
        out_shape=(jax.ShapeDtypeStruct((B, S, C), F32), jax.ShapeDtypeStruct((CONV_K, C), F32),
                   jax.ShapeDtypeStruct((1, C), F32)),
        grid=(C // CONV_CT, B),
        in_specs=[pl.BlockSpec((None, SP, CONV_CT), lambda ci, bi: (bi, 0, ci)),
                  pl.BlockSpec((None, S + PAD_R, CONV_CT), lambda ci, bi: (bi, 0, ci)),
                  pl.BlockSpec((CONV_K, CONV_CT), lambda ci, bi: (0, ci)),
                  pl.BlockSpec((1, CONV_CT), lambda ci, bi: (0, ci))],
        out_specs=(pl.BlockSpec((None, S, CONV_CT), lambda ci, bi: (bi, 0, ci)),
                   pl.BlockSpec((CONV_K, CONV_CT), lambda ci, bi: (0, ci)),
                   pl.BlockSpec((1, CONV_CT), lambda ci, bi: (0, ci))),
        name=name, compiler_params=_cp("parallel", "arbitrary"))(xpad, dxc_pad, w, b)


def _ssd_common(dtc_ref, dtr_ref, pcol_ref, prow_ref, b_ref, c_ref):
    L = SSD_L
    bias_c, alog_c = pcol_ref[0:1, :], pcol_ref[1:2, :]
    a_c = -jnp.exp(alog_c)
    dt_c = _softplus(dtc_ref[...] + bias_c)
    row = lax.broadcasted_iota(jnp.int32, (L, L), 0)
    col = lax.broadcasted_iota(jnp.int32, (L, L), 1)
    causal = row >= col
    tri = causal.astype(F32)
    hp = lax.Precision.HIGHEST
    cum_c = lax.dot_general(tri, dt_c * a_c, NN, precision=hp, preferred_element_type=F32)
    a_r = -jnp.exp(prow_ref[:, 1:2])
    dt_r = _softplus(dtr_ref[...] + prow_ref[:, 0:1])
    cum_r = lax.dot_general(dt_r * a_r, tri, NT, precision=hp, preferred_element_type=F32)
    bb = b_ref[...].astype(BF16)
    cb = c_ref[...].astype(BF16)
    G = _dot(cb, bb, NT)
    return a_c, dt_c, causal, tri, cum_c, cum_r, bb, cb, G


def _ssd_fwd(name, xc, proj, dtc, dtr, pcol, prow, nw, B):
    T = xc.shape[0]
    S = T // B
    nb = S // SSD_L
    L = SSD_L

    def body(xs_ref, b_ref, c_ref, z_ref, dtc_ref, dtr_ref, pcol_ref, prow_ref, nw_ref, y_ref, yn_ref, hs_ref, H):
        @pl.when(pl.program_id(2) == 0)
        def _():
            H[...] = jnp.zeros_like(H)

        a_c, dt_c, causal, tri, cum_c, cum_r, bb, cb, G = _ssd_common(dtc_ref, dtr_ref, pcol_ref, prow_ref, b_ref, c_ref)
        dsk = pcol_ref[2:3, :]
        clast = cum_c[L - 1:L, :]
        bf = b_ref[...]
        for h in range(4):
            sl = slice(HEAD_DIM * h, HEAD_DIM * (h + 1))
            cc = cum_c[:, h:h + 1]
            lm = jnp.exp(jnp.where(causal, cc - cum_r[h:h + 1, :], NEG))
            M = (G * lm).astype(BF16)
            xh = xs_ref[:, sl]
            Xb = (xh * dt_c[:, h:h + 1]).astype(BF16)
            Hh = H[h]
            y = _dot(M, Xb, NN) + jnp.exp(cc) * _dot(cb, Hh.astype(BF16), NN)
            y_ref[:, sl] = y + dsk[:, h:h + 1] * xh
            hs_ref[h] = Hh
            cl = clast[:, h:h + 1]
            Bw = (bf * jnp.exp(cl - cc)).astype(BF16)
            H[h] = jnp.exp(cl) * Hh + _dot(Bw, Xb, TN)
        zv = z_ref[...]
        y2 = y_ref[...] * (zv * _sigmoid(zv))
        r = lax.rsqrt(jnp.mean(y2 * y2, axis=-1, keepdims=True) + EPS)
        yn_ref[...] = (y2 * r * nw_ref[...]).astype(BF16)

    rowi = lambda b, g, i: b * nb + i
    grp = pl.BlockSpec((L, GROUP_W), lambda b, g, i: (rowi(b, g, i), g))
    return pl.pallas_call(
        body,
        out_shape=(jax.ShapeDtypeStruct((T, 1024), F32), jax.ShapeDtypeStruct((T, 1024), BF16),
                   jax.ShapeDtypeStruct((B, SSD_GROUPS, nb, 4, SSD_STATE, HEAD_DIM), F32)),
        grid=(B, SSD_GROUPS, nb),
        in_specs=[grp,
                  pl.BlockSpec((L, SSD_STATE), lambda b, g, i: (rowi(b, g, i), 8 + g)),
                  pl.BlockSpec((L, SSD_STATE), lambda b, g, i: (rowi(b, g, i), 12 + g)),
                  grp,
                  pl.BlockSpec((None, L, 4), lambda b, g, i: (g, rowi(b, g, i), 0)),
                  pl.BlockSpec((None, 4, L), lambda b, g, i: (g, 0, rowi(b, g, i))),
                  pl.BlockSpec((None, 3, 4), lambda b, g, i: (g, 0, 0)),
                  pl.BlockSpec((None, 4, 3), lambda b, g, i: (g, 0, 0)),
                  pl.BlockSpec((1, GROUP_W), lambda b, g, i: (0, g))],
        out_specs=(grp, grp,
                   pl.BlockSpec((None, None, None, 4, SSD_STATE, HEAD_DIM), lambda b, g, i: (b, g, i, 0, 0, 0))),
        scratch_shapes=[pltpu.VMEM((4, SSD_STATE, HEAD_DIM), F32)], name=name,
        compiler_params=_cp("parallel", "parallel", "arbitrary"))(xc, xc, xc, proj, dtc, dtr, pcol, prow, nw)


def _ssd_bwd(name, dyn, Y, xc, proj, dtc, dtr, pcol, prow, nw, hs, B):
    T = xc.shape[0]
    S = T // B
    nb = S // SSD_L
    L = SSD_L

    def body(dyn_ref, y_ref, xs_ref, b_ref, c_ref, z_ref, dtc_ref, dtr_ref, pcol_ref, prow_ref, nw_ref, hs_ref,
             dxs_ref, db_ref, dc_ref, dz_ref, ddt_ref, dpar_ref, dnw_ref, dH):
        @pl.when(pl.program_id(2) == 0)
        def _():
            dH[...] = jnp.zeros_like(dH)
            dpar_ref[...] = jnp.zeros_like(dpar_ref)
            dnw_ref[...] = jnp.zeros_like(dnw_ref)

        a_c, dt_c, causal, tri, cum_c, cum_r, bb, cb, G = _ssd_common(dtc_ref, dtr_ref, pcol_ref, prow_ref, b_ref, c_ref)
        dsk = pcol_ref[2:3, :]
        clast = cum_c[L - 1:L, :]
        bf = b_ref[...]
        cf = c_ref[...]
        Yv = y_ref[...]
        zv = z_ref[...]
        sz = _sigmoid(zv)
        silu = zv * sz
        y2 = Yv * silu
        r = lax.rsqrt(jnp.mean(y2 * y2, axis=-1, keepdims=True) + EPS)
        yhat = y2 * r
        dyv = dyn_ref[...]
        dnw_ref[...] += jnp.sum(dyv * yhat, axis=0, keepdims=True)
        dyhat = dyv * nw_ref[...]
        dy2 = r * (dyhat - yhat * jnp.mean(dyhat * yhat, axis=-1, keepdims=True))
        dY = dy2 * silu
        dz_ref[...] = dy2 * Yv * (sz * (1.0 + zv * (1.0 - sz)))

        lane4 = lax.broadcasted_iota(jnp.int32, (1, 4), 1)
        dG = jnp.zeros((L, L), F32)
        dBs = jnp.zeros((L, SSD_STATE), F32)
        dCs = jnp.zeros((L, SSD_STATE), F32)
        dA = jnp.zeros((L, 4), F32)
        ddtx = jnp.zeros((L, 4), F32)
        ddsk = jnp.zeros((1, 4), F32)
        dcl = jnp.zeros((1, 4), F32)
        for h in range(4):
            sl = slice(HEAD_DIM * h, HEAD_DIM * (h + 1))
            onehot = (lane4 == h).astype(F32)
            cc = cum_c[:, h:h + 1]
            cl = clast[:, h:h + 1]
            lm = jnp.exp(jnp.where(causal, cc - cum_r[h:h + 1, :], NEG))
            M = (G * lm).astype(BF16)
            xh = xs_ref[:, sl]
            dth = dt_c[:, h:h + 1]
            X = xh * dth
            Xb = X.astype(BF16)
            dYh = dY[:, sl]
            dYb = dYh.astype(BF16)
            Hb = hs_ref[h].astype(BF16)
            dHh = dH[h]
            dHb = dHh.astype(BF16)
            alpha = jnp.exp(cc)
            beta = jnp.exp(cl - cc)
            dXoff = beta * _dot(bb, dHb, NN)
            dX = _dot(M, dYb, TN) + dXoff
            dG = dG + _dot(dYb, Xb, NT) * lm
            dCs = dCs + _dot((alpha * dYh).astype(BF16), Hb, NT)
            dBs = dBs + _dot((beta * X).astype(BF16), dHb, NT)
            ypre = Yv[:, sl] - dsk[:, h:h + 1] * xh
            dA_h = (jnp.sum(dYb.astype(F32) * ypre, axis=-1, keepdims=True)
                    - jnp.sum(Xb.astype(F32) * dX, axis=-1, keepdims=True))
            dA = dA + dA_h * onehot
            dcl_h = (jnp.sum(jnp.sum(dHh * (jnp.exp(cl) * hs_ref[h]), axis=-1, keepdims=True), axis=0, keepdims=True)
                     + jnp.sum(jnp.sum(Xb.astype(F32) * dXoff, axis=-1, keepdims=True), axis=0, keepdims=True))
            dcl = dcl + dcl_h * onehot
            ddtx = ddtx + jnp.sum(dX * xh, axis=-1, keepdims=True) * onehot
            ddsk = ddsk + jnp.sum(jnp.sum(dYh * xh, axis=-1, keepdims=True), axis=0, keepdims=True) * onehot
            dxs_ref[:, sl] = dsk[:, h:h + 1] * dYh + dX * dth
            dH[h] = jnp.exp(cl) * dHh + _dot((alpha * cf).astype(BF16), dYb, TN)
        dGb = dG.astype(BF16)
        dc_ref[...] = _dot(dGb, bb, NN) + dCs
        db_ref[...] = _dot(dGb, cb, TN) + dBs
        hp = lax.Precision.HIGHEST
        last = lax.broadcasted_iota(jnp.int32, (L, 1), 0) == L - 1
        dA = dA + jnp.where(last, dcl, 0.0)
        dadt = lax.dot_general(tri, dA, TN, precision=hp, preferred_element_type=F32)
        ddt = dadt * a_c + ddtx
        d_a = jnp.sum(dadt * dt_c, axis=0, keepdims=True)
        ddraw = ddt * _sigmoid(dtc_ref[...] + pcol_ref[0:1, :])
        ddt_ref[...] = ddraw
        dpar_ref[0:1, :] += jnp.sum(ddraw, axis=0, keepdims=True)
        dpar_ref[1:2, :] += d_a * a_c
        dpar_ref[2:3, :] += ddsk

    rowi = lambda b, g, i: b * nb + (nb - 1 - i)
    grp = pl.BlockSpec((L, GROUP_W), lambda b, g, i: (rowi(b, g, i), g))
    st = pl.BlockSpec((L, SSD_STATE), lambda b, g, i: (rowi(b, g, i), g))
    f = jax.ShapeDtypeStruct
    return pl.pallas_call(
        body,
        out_shape=(f((T, 1024), F32), f((T, 512), F32), f((T, 512), F32), f((T, 1024), F32),
                   f((SSD_GROUPS, T, 4), F32), f((B, SSD_GROUPS, 3, 4), F32), f((B, 1, 1024), F32)),
        grid=(B, SSD_GROUPS, nb),
        in_specs=[grp, grp, grp,
                  pl.BlockSpec((L, SSD_STATE), lambda b, g, i: (rowi(b, g, i), 8 + g)),
                  pl.BlockSpec((L, SSD_STATE), lambda b, g, i: (rowi(b, g, i), 12 + g)),
                  grp,
                  pl.BlockSpec((None, L, 4), lambda b, g, i: (g, rowi(b, g, i), 0)),
                  pl.BlockSpec((None, 4, L), lambda b, g, i: (g, 0, rowi(b, g, i))),
                  pl.BlockSpec((None, 3, 4), lambda b, g, i: (g, 0, 0)),
                  pl.BlockSpec((None, 4, 3), lambda b, g, i: (g, 0, 0)),
                  pl.BlockSpec((1, GROUP_W), lambda b, g, i: (0, g)),
                  pl.BlockSpec((None, None, None, 4, SSD_STATE, HEAD_DIM), lambda b, g, i: (b, g, nb - 1 - i, 0, 0, 0))],
        out_specs=(grp, st, st, grp,
                   pl.BlockSpec((None, L, 4), lambda b, g, i: (g, rowi(b, g, i), 0)),
                   pl.BlockSpec((None, None, 3, 4), lambda b, g, i: (b, g, 0, 0)),
                   pl.BlockSpec((None, 1, GROUP_W), lambda b, g, i: (b, 0, g))),
        scratch_shapes=[pltpu.VMEM((4, SSD_STATE, HEAD_DIM), F32)], name=name,
        compiler_params=_cp("parallel", "parallel", "arbitrary"))(dyn, Y, xc, xc, xc, proj, dtc, dtr, pcol, prow, nw, hs)


def _headnorm_fwd(name, proj, col_block, w):
    T = proj.shape[0]

    def body(x_ref, w_ref, o_ref):
        for h in range(ATT_HEADS):
            sl = slice(HEAD_DIM * h, HEAD_DIM * (h + 1))
            xh = x_ref[:, sl]
            r = lax.rsqrt(jnp.mean(xh * xh, axis=-1, keepdims=True) + EPS)
            o_ref[:, sl] = (xh * r * w_ref[...]).astype(BF16)

    return pl.pallas_call(
        body, out_shape=jax.ShapeDtypeStruct((T, 1024), BF16), grid=(T // ROW_T,),
        in_specs=[pl.BlockSpec((ROW_T, 1024), lambda i: (i, col_block)), pl.BlockSpec((1, HEAD_DIM), lambda i: (0, 0))],
        out_specs=pl.BlockSpec((ROW_T, 1024), lambda i: (i, 0)), name=name, compiler_params=_cp("parallel"))(proj, w)


def _headnorm_bwd(name, dn, proj, col_block, w):
    T = proj.shape[0]

    def body(dn_ref, x_ref, w_ref, dx_ref, dw_ref):
        @pl.when(pl.program_id(0) == 0)
        def _():
            dw_ref[...] = jnp.zeros_like(dw_ref)

        dw = jnp.zeros((1, HEAD_DIM), F32)
        for h in range(ATT_HEADS):
            sl = slice(HEAD_DIM * h, HEAD_DIM * (h + 1))
            xh = x_ref[:, sl]
            r = lax.rsqrt(jnp.mean(xh * xh, axis=-1, keepdims=True) + EPS)
            xhat = xh * r
            dnh = dn_ref[:, sl]
            dxhat = dnh * w_ref[...]
            dx_ref[:, sl] = r * (dxhat - xhat * jnp.mean(dxhat * xhat, axis=-1, keepdims=True))
            dw = dw + jnp.sum(dnh * xhat, axis=0, keepdims=True)
        dw_ref[...] += dw

    return pl.pallas_call(
        body, out_shape=(jax.ShapeDtypeStruct((T, 1024), F32), jax.ShapeDtypeStruct((1, HEAD_DIM), F32)),
        grid=(T // ROW_T,),
        in_specs=[pl.BlockSpec((ROW_T, 1024), lambda i: (i, 0)), pl.BlockSpec((ROW_T, 1024), lambda i: (i, col_block)),
                  pl.BlockSpec((1, HEAD_DIM), lambda i: (0, 0))],
        out_specs=(pl.BlockSpec((ROW_T, 1024), lambda i: (i, 0)), pl.BlockSpec((1, HEAD_DIM), lambda i: (0, 0))),
        name=name, compiler_params=_cp("arbitrary"))(dn, proj, w)


def _att_bias(nq):
    j = np.arange(ATT_B)[:, None]
    i = np.arange(ATT_B)[None, :]
    out = np.empty((nq, ATT_B, ATT_B), np.float32)
    for dblk in range(nq):
        dl = ATT_B * dblk + i - j
        cnt = ((dl >= 0) & (dl <= 128)).astype(np.float32)
        cnt += ((dl >= 0) & (dl % 4 == 0) & (dl <= 512))
        cnt += ((dl >= 0) & (dl % 16 == 0) & (dl <= 2048))
        out[dblk] = np.where(cnt > 0, np.log(np.maximum(cnt, 1.0)), NEG)
    return jnp.asarray(out)


def _row_pair(nq):
    def f(r, c):
        first = c <= r
        return jnp.where(first, r, nq - 1 - r), jnp.where(first, c, c - (r + 1))
    return f


def _col_pair(nq):
    def f(r, c):
        first = c < nq - r
        kj = jnp.where(first, r, nq - 1 - r)
        return jnp.where(first, r + c, nq - 1 - r + (c - (nq - r))), kj
    return f


ATT_SCALE = 1.0 / math.sqrt(HEAD_DIM)
ATT_HS = 4
ATT_W = ATT_HS * HEAD_DIM


def _att_maps(nq, qk):
    return dict(
        q_tok=lambda b, g, r, c: (b * nq + qk(r, c)[0], g),
        k_tok=lambda b, g, r, c: (b * nq + qk(r, c)[1], g),
        q_feat=lambda b, g, r, c: (g, b * nq + qk(r, c)[0]),
        k_feat=lambda b, g, r, c: (g, b * nq + qk(r, c)[1]),
        bias=lambda b, g, r, c: (qk(r, c)[0] - qk(r, c)[1], 0, 0),
        lse=lambda b, g, r, c: (g, 0, b * nq + qk(r, c)[0]),
        do_tok=lambda b, g, r, c: (b * nq + qk(r, c)[0], ATT_HS + g))


def _att_fwd(name, kn, qT, vT, bias, B):
    T = kn.shape[0]
    nq = (T // B) // ATT_B
    qk = _row_pair(nq)
    mp = _att_maps(nq, qk)

    def body(k_ref, qT_ref, vT_ref, bias_ref, oT_ref, lse_ref, m_s, l_s, acc_s):
        qi, kj = qk(pl.program_id(2), pl.program_id(3))

        @pl.when(kj == 0)
        def _():
            m_s[...] = jnp.full_like(m_s, NEG)
            l_s[...] = jnp.zeros_like(l_s)
            acc_s[...] = jnp.zeros_like(acc_s)

        bv = bias_ref[...]
        for h in range(ATT_HS):
            rs = slice(HEAD_DIM * h, HEAD_DIM * (h + 1))
            s = _dot(k_ref[:, rs], qT_ref[rs, :], NN) + bv
            m_prev = m_s[h:h + 1, :]
            m_new = jnp.maximum(m_prev, jnp.max(s, axis=0, keepdims=True))
            alpha = jnp.exp(m_prev - m_new)
            p = jnp.exp(s - m_new)
            l_s[h:h + 1, :] = alpha * l_s[h:h + 1, :] + jnp.sum(p, axis=0, keepdims=True)
            acc_s[rs, :] = alpha * acc_s[rs, :] + _dot(vT_ref[rs, :], p.astype(BF16), NN)
            m_s[h:h + 1, :] = m_new

        @pl.when(kj == qi)
        def _():
            for h in range(ATT_HS):
                rs = slice(HEAD_DIM * h, HEAD_DIM * (h + 1))
                oT_ref[rs, :] = (acc_s[rs, :] / l_s[h:h + 1, :]).astype(BF16)
            lse_ref[...] = m_s[...] + jnp.log(l_s[...])

    tok = (ATT_B, ATT_W)
    feat = (ATT_W, ATT_B)
    return pl.pallas_call(
        body,
        out_shape=(jax.ShapeDtypeStruct((1024, T), BF16), jax.ShapeDtypeStruct((ATT_HEADS // ATT_HS, ATT_HS, T), F32)),
        grid=(B, ATT_HEADS // ATT_HS, nq // 2, nq + 1),
        in_specs=[pl.BlockSpec(tok, mp["k_tok"]), pl.BlockSpec(feat, mp["q_feat"]), pl.BlockSpec(feat, mp["k_feat"]),
                  pl.BlockSpec((None, ATT_B, ATT_B), mp["bias"])],
        out_specs=(pl.BlockSpec(feat, mp["q_feat"]), pl.BlockSpec((None, ATT_HS, ATT_B), mp["lse"])),
        scratch_shapes=[pltpu.VMEM((ATT_HS, ATT_B), F32), pltpu.VMEM((ATT_HS, ATT_B), F32),
                        pltpu.VMEM((ATT_W, ATT_B), F32)],
        name=name, compiler_params=_cp("parallel", "parallel", "arbitrary", "arbitrary"))(kn, qT, vT, bias)


def _att_p_ds(k_ref, qT_ref, v_ref, doT_ref, oT_ref, lse_ref, bv, h):
    rs = slice(HEAD_DIM * h, HEAD_DIM * (h + 1))
    dof = doT_ref[rs, :]
    delta = jnp.sum(dof * oT_ref[rs, :].astype(F32), axis=0, keepdims=True)
    s = _dot(k_ref[:, rs], qT_ref[rs, :], NN) + bv
    p = jnp.exp(s - lse_ref[h:h + 1, :])
    dp = _dot(v_ref[:, rs], dof.astype(BF16), NN)
    return p, p * (dp - delta)


def _att_bwd_dq(name, kn, qT, vb, knT, bias, doT, oT, lse, B):
    T = kn.shape[0]
    nq = (T // B) // ATT_B
    qk = _row_pair(nq)
    mp = _att_maps(nq, qk)

    def body(k_ref, qT_ref, v_ref, kT_ref, bias_ref, doT_ref, oT_ref, lse_ref, dqT_ref, acc_s):
        qi, kj = qk(pl.program_id(2), pl.program_id(3))

        @pl.when(kj == 0)
        def _():
            acc_s[...] = jnp.zeros_like(acc_s)

        bv = bias_ref[...]
        for h in range(ATT_HS):
            rs = slice(HEAD_DIM * h, HEAD_DIM * (h + 1))
            p, ds = _att_p_ds(k_ref, qT_ref, v_ref, doT_ref, oT_ref, lse_ref, bv, h)
            acc_s[rs, :] += _dot(kT_ref[rs, :], ds.astype(BF16), NN)

        @pl.when(kj == qi)
        def _():
            dqT_ref[...] = acc_s[...] * ATT_SCALE

    tok = (ATT_B, ATT_W)
    feat = (ATT_W, ATT_B)
    return pl.pallas_call(
        body, out_shape=jax.ShapeDtypeStruct((1024, T), F32), grid=(B, ATT_HEADS // ATT_HS, nq // 2, nq + 1),
        in_specs=[pl.BlockSpec(tok, mp["k_tok"]), pl.BlockSpec(feat, mp["q_feat"]), pl.BlockSpec(tok, mp["k_tok"]),
                  pl.BlockSpec(feat, mp["k_feat"]), pl.BlockSpec((None, ATT_B, ATT_B), mp["bias"]),
                  pl.BlockSpec(feat, mp["q_feat"]), pl.BlockSpec(feat, mp["q_feat"]),
                  pl.BlockSpec((None, ATT_HS, ATT_B), mp["lse"])],
        out_specs=pl.BlockSpec(feat, mp["q_feat"]),
        scratch_shapes=[pltpu.VMEM((ATT_W, ATT_B), F32)],
        name=name, compiler_params=_cp("parallel", "parallel", "arbitrary", "arbitrary"))(
            kn, qT, vb, knT, bias, doT, oT, lse)


def _att_bwd_dkv(name, kn, qT, vb, qn, bias, doT, oT, lse, dyn, B):
    T = kn.shape[0]
    nq = (T // B) // ATT_B
    qk = _col_pair(nq)
    mp = _att_maps(nq, qk)

    def body(k_ref, qT_ref, v_ref, q_ref, bias_ref, doT_ref, oT_ref, lse_ref, do_ref, dk_ref, dv_ref, dk_s, dv_s):
        qi, kj = qk(pl.program_id(2), pl.program_id(3))

        @pl.when(qi == kj)
        def _():
            dk_s[...] = jnp.zeros_like(dk_s)
            dv_s[...] = jnp.zeros_like(dv_s)

        bv = bias_ref[...]
        for h in range(ATT_HS):
            rs = slice(HEAD_DIM * h, HEAD_DIM * (h + 1))
            p, ds = _att_p_ds(k_ref, qT_ref, v_ref, doT_ref, oT_ref, lse_ref, bv, h)
            dv_s[h] += _dot(p.astype(BF16), do_ref[:, rs].astype(BF16), NN)
            dk_s[h] += _dot(ds.astype(BF16), q_ref[:, rs], NN)

        @pl.when(qi == nq - 1)
        def _():
            for h in range(ATT_HS):
                rs = slice(HEAD_DIM * h, HEAD_DIM * (h + 1))
                dk_ref[:, rs] = dk_s[h] * ATT_SCALE
                dv_ref[:, rs] = dv_s[h]

    tok = (ATT_B, ATT_W)
    feat = (ATT_W, ATT_B)
    osh = jax.ShapeDtypeStruct((T, 1024), F32)
    return pl.pallas_call(
        body, out_shape=(osh, osh), grid=(B, ATT_HEADS // ATT_HS, nq // 2, nq + 1),
        in_specs=[pl.BlockSpec(tok, mp["k_tok"]), pl.BlockSpec(feat, mp["q_feat"]), pl.BlockSpec(tok, mp["k_tok"]),
                  pl.BlockSpec(tok, mp["q_tok"]), pl.BlockSpec((None, ATT_B, ATT_B), mp["bias"]),
                  pl.BlockSpec(feat, mp["q_feat"]), pl.BlockSpec(feat, mp["q_feat"]),
                  pl.BlockSpec((None, ATT_HS, ATT_B), mp["lse"]), pl.BlockSpec(tok, mp["do_tok"])],
        out_specs=(pl.BlockSpec(tok, mp["k_tok"]), pl.BlockSpec(tok, mp["k_tok"])),
        scratch_shapes=[pltpu.VMEM((ATT_HS, ATT_B, HEAD_DIM), F32), pltpu.VMEM((ATT_HS, ATT_B, HEAD_DIM), F32)],
        name=name, compiler_params=_cp("parallel", "parallel", "arbitrary", "arbitrary"))(
            kn, qT, vb, qn, bias, doT, oT, lse, dyn)


def _group_cols(v):
    return v.reshape(SSD_GROUPS, 4)


def _ssd_params(p):
    rows = jnp.stack([_group_cols(p["dt_bias"]), _group_cols(p["a_log"]), _group_cols(p["d_skip"])], axis=1)
    return rows, jnp.swapaxes(rows, 1, 2)


def _layer_fwd(l, x, p, W, bias, B):
    T = x.shape[0]
    S = T // B
    nt = T // ROW_T
    tag = "l%d" % l
    x1, ffn1 = _ffn_fwd(tag + "f1", x, p["ffn1_norm"][None], W["g1"], W["u1"], W["d1"], l)
    h2 = _rms_fwd(tag + "_mixrms", x1, p["mix_norm"][None])
    win = W["win"][l]
    proj = _mm(tag + "_proj",
               [(h2, pl.BlockSpec((ROW_T, D_MODEL), lambda j, i, k: (i, 0)),
                 win, pl.BlockSpec((D_MODEL, PROJ_TN), lambda j, i, k: (0, j)))],
               jax.ShapeDtypeStruct((T, IN_PAD), F32), pl.BlockSpec((ROW_T, PROJ_TN), lambda j, i, k: (i, j)),
               (IN_PAD // PROJ_TN, nt, 1), NN, (ROW_T, PROJ_TN))
    xbc = proj[:, COL_XBC:COL_Q].reshape(B, S, CONV_DIM)
    xpad = jnp.pad(xbc, ((0, 0), (PAD_R, PAD_R), (0, 0)))
    cw, cbias = p["conv_w"], p["conv_b"][None]
    xc = _conv_fwd(tag + "_conv", xpad, cw, cbias).reshape(T, CONV_DIM)
    dtraw = proj[:, COL_DT:COL_DT + SSD_HEADS].reshape(T, SSD_GROUPS, 4)
    dtc = jnp.transpose(dtraw, (1, 0, 2))
    dtr = jnp.transpose(dtraw, (1, 2, 0))
    pcol, prow = _ssd_params(p)
    Y, y_ssd, hs = _ssd_fwd(tag + "_ssd", xc, proj, dtc, dtr, pcol, prow, p["ssd_norm"][None], B)
    qn = _headnorm_fwd(tag + "_qn", proj, COL_Q // 1024, p["q_norm"][None])
    kn = _headnorm_fwd(tag + "_kn", proj, COL_K // 1024, p["k_norm"][None])
    qT = (qn * ATT_SCALE).T
    vb = proj[:, COL_V:COL_V + 1024].astype(BF16)
    oT, lse = _att_fwd(tag + "_att", kn, qT, vb.T, bias, B)
    ymix = jnp.concatenate([y_ssd, oT.T], axis=1)
    x2 = _mm(tag + "_out",
             [(ymix, pl.BlockSpec((ROW_T, MIX_SH), lambda i, n, k: (i, k)),
               W["wout"], pl.BlockSpec((None, None, MIX_SH, D_MODEL), lambda i, n, k: (k, l, 0, 0)))],
             jax.ShapeDtypeStruct((T, D_MODEL), F32), pl.BlockSpec((ROW_T, D_MODEL), lambda i, n, k: (i, 0)),
             (nt, 1, N_SHARD), NN, (ROW_T, D_MODEL),
             res=(x1, pl.BlockSpec((ROW_T, D_MODEL), lambda i, n, k: (i, 0))))
    x3, ffn2 = _ffn_fwd(tag + "f2", x2, p["ffn2_norm"][None], W["g2"], W["u2"], W["d2"], l)
    saved = dict(ffn1=ffn1, x1=x1, h2=h2, proj=proj, xpad=xpad, xc=xc, dtc=dtc, dtr=dtr, Y=Y, hs=hs,
                 qn=qn, kn=kn, qT=qT, vb=vb, oT=oT, lse=lse, ymix=ymix, ffn2=ffn2)
    return x3, saved


def _layer_bwd(l, dx3, sv, p, W, bias, B, gbuf):
    T = dx3.shape[0]
    S = T // B
    nt = T // ROW_T
    tag = "l%db" % l
    sg = {}
    dx2, sg["ffn2_norm"], (gg2, gu2, gd2) = _ffn_bwd(tag + "f2", dx3, sv["ffn2"], p["ffn2_norm"][None],
                                                    W["g2"], W["u2"], W["d2"], l, (gbuf["g2"], gbuf["u2"], gbuf["d2"]))
    dymix = _mm(tag + "_dymix",
                [(dx2, pl.BlockSpec((ROW_T, D_MODEL), lambda n, i, k: (i, 0)),
                  W["wout"], pl.BlockSpec((None, None, MIX_SH, D_MODEL), lambda n, i, k: (n, l, 0, 0)))],
                jax.ShapeDtypeStruct((T, MIX_W), F32), pl.BlockSpec((ROW_T, MIX_SH), lambda n, i, k: (i, n)),
                (N_SHARD, nt, 1), NT, (ROW_T, MIX_SH))
    gwout = _mm(tag + "_dwout",
                [(sv["ymix"], pl.BlockSpec((ROW_T, MIX_SH), lambda m, n, k: (k, m)),
                  dx2, pl.BlockSpec((ROW_T, D_MODEL), lambda m, n, k: (k, 0)))],
                jax.ShapeDtypeStruct((N_SHARD, DEPTH, MIX_SH, D_MODEL), BF16),
                pl.BlockSpec((None, None, MIX_SH, D_MODEL), lambda m, n, k: (m, l, 0, 0)),
                (N_SHARD, 1, nt), TN, (MIX_SH, D_MODEL), prev=gbuf["wout"])
    proj = sv["proj"]
    doT = dymix[:, 1024:].T
    dqn = _att_bwd_dq(tag + "_attdq", sv["kn"], sv["qT"], sv["vb"], sv["kn"].T, bias, doT, sv["oT"], sv["lse"], B).T
    dkn, dv = _att_bwd_dkv(tag + "_attdkv", sv["kn"], sv["qT"], sv["vb"], sv["qn"], bias, doT, sv["oT"], sv["lse"],
                           dymix, B)
    dq, sg["q_norm"] = _headnorm_bwd(tag + "_qnb", dqn, proj, COL_Q // 1024, p["q_norm"][None])
    dk, sg["k_norm"] = _headnorm_bwd(tag + "_knb", dkn, proj, COL_K // 1024, p["k_norm"][None])
    pcol, prow = _ssd_params(p)
    dxs, dB, dC, dz, ddt, dpar, dnw = _ssd_bwd(tag + "_ssdb", dymix, sv["Y"], sv["xc"], proj, sv["dtc"], sv["dtr"],
                                               pcol, prow, p["ssd_norm"][None], sv["hs"], B)
    dpar = jnp.sum(dpar, axis=0)
    sg["dt_bias"] = dpar[:, 0, :].reshape(SSD_HEADS)
    sg["a_log"] = dpar[:, 1, :].reshape(SSD_HEADS)
    sg["d_skip"] = dpar[:, 2, :].reshape(SSD_HEADS)
    sg["ssd_norm"] = jnp.sum(dnw, axis=0)
    dxc = jnp.concatenate([dxs, dB, dC], axis=1).reshape(B, S, CONV_DIM)
    dxc_pad = jnp.pad(dxc, ((0, 0), (0, PAD_R), (0, 0)))
    dxbc, sg["conv_w"], sg["conv_b"] = _conv_bwd(tag + "_convb", sv["xpad"], dxc_pad, p["conv_w"], p["conv_b"][None])
    ddt16 = jnp.transpose(ddt, (1, 0, 2)).reshape(T, SSD_HEADS)
    dproj = jnp.concatenate([dz, dxbc.reshape(T, CONV_DIM), dq, dk, dv, ddt16,
                             jnp.zeros((T, IN_PAD - COL_DT - SSD_HEADS), F32)], axis=1).astype(BF16)
    win = W["win"][l]
    gwin = _mm(tag + "_dwin",
               [(sv["h2"], pl.BlockSpec((ROW_T, D_MODEL), lambda n, m, k: (k, 0)),
                 dproj, pl.BlockSpec((ROW_T, PROJ_TN), lambda n, m, k: (k, n)))],
               jax.ShapeDtypeStruct((D_MODEL, IN_PAD), BF16), pl.BlockSpec((D_MODEL, PROJ_TN), lambda n, m, k: (0, n)),
               (IN_PAD // PROJ_TN, 1, nt), TN, (D_MODEL, PROJ_TN))
    dh2 = _mm(tag + "_dh2",
              [(dproj, pl.BlockSpec((ROW_T, PROJ_TN), lambda i, n, k: (i, k)),
                win, pl.BlockSpec((D_MODEL, PROJ_TN), lambda i, n, k: (0, k)))],
              jax.ShapeDtypeStruct((T, D_MODEL), F32), pl.BlockSpec((ROW_T, D_MODEL), lambda i, n, k: (i, 0)),
              (nt, 1, IN_PAD // PROJ_TN), NT, (ROW_T, D_MODEL))
    dx1, sg["mix_norm"] = _rms_bwd(tag + "_mixrmsb", dh2, sv["x1"], p["mix_norm"][None], dx2)
    dx0, sg["ffn1_norm"], (gg1, gu1, gd1) = _ffn_bwd(tag + "f1", dx1, sv["ffn1"], p["ffn1_norm"][None],
                                                    W["g1"], W["u1"], W["d1"], l, (gbuf["g1"], gbuf["u1"], gbuf["d1"]))
    gbuf = dict(g1=gg1, u1=gu1, d1=gd1, g2=gg2, u2=gu2, d2=gd2, wout=gwout)
    return dx0, sg, gbuf, gwin


def _win_pack(w):
    return jnp.concatenate([w[:, :3072], w[:, 3088:], w[:, 3072:3088],
                            jnp.zeros((w.shape[0], IN_PAD - IN_PROJ), w.dtype)], axis=1)


def _win_unpack(g):
    return jnp.concatenate([g[:, :3072], g[:, COL_DT:COL_DT + SSD_HEADS], g[:, 3072:COL_DT]], axis=1)


def _local_step(x, target, small, W, B):
    T = x.shape[0]
    nq = (T // B) // ATT_B
    bias = _att_bias(nq)
    saved = []
    h = x
    for l in range(DEPTH):
        p = {k: v[l] for k, v in small.items()}
        h, sv = _layer_fwd(l, h, p, W, bias, B)
        saved.append(sv)
    dy, lsum = _loss_grad("loss", h, target)
    gbuf = dict(g1=None, u1=None, d1=None, g2=None, u2=None, d2=None, wout=None)
    sgrads = [None] * DEPTH
    gwin = [None] * DEPTH
    d = dy
    for l in reversed(range(DEPTH)):
        p = {k: v[l] for k, v in small.items()}
        d, sgrads[l], gbuf, gwin[l] = _layer_bwd(l, d, saved[l], p, W, bias, B, gbuf)
    return lsum, d, sgrads, gbuf, gwin


MESH = pl.DeviceIdType.MESH
ANY = pl.BlockSpec(memory_space=pl.ANY)


def _place():
    return lax.axis_index("x"), lax.axis_index("y"), lax.axis_index("c")


def _other_chips(x, y):
    return [(1 - x, y), (x, 1 - y), (1 - x, 1 - y)]


def _gather_big(own):
    n = len(own)

    def body(*refs):
        src, dst = refs[:n], refs[n:2 * n]
        send, recv, loc = refs[2 * n:]
        x, y, c = _place()
        me = 2 * x + y
        local = [pltpu.make_async_copy(src[a], dst[a].at[me], loc.at[a]) for a in range(n)]
        for cp in local:
            cp.start()
        chips = _other_chips(x, y)
        sends = []
        for k, (px, py) in enumerate(chips):
            for a in range(n):
                cp = pltpu.make_async_remote_copy(src_ref=src[a], dst_ref=dst[a].at[me], send_sem=send.at[k * n + a],
                                                  recv_sem=recv.at[k * n + a], device_id=(px, py, c), device_id_type=MESH)
                cp.start()
                sends.append(cp)
        for k, (px, py) in enumerate(chips):
            for a in range(n):
                pltpu.make_async_remote_copy(src_ref=src[a], dst_ref=dst[a].at[2 * px + py], send_sem=send.at[k * n + a],
                                             recv_sem=recv.at[k * n + a], device_id=(px, py, c),
                                             device_id_type=MESH).wait_recv()
        for cp in sends:
            cp.wait_send()
        for cp in local:
            cp.wait()

    return pl.pallas_call(
        body, out_shape=[jax.ShapeDtypeStruct((N_SHARD,) + o.shape, o.dtype) for o in own],
        in_specs=[ANY] * n, out_specs=[ANY] * n,
        scratch_shapes=[pltpu.SemaphoreType.DMA((3 * n,)), pltpu.SemaphoreType.DMA((3 * n,)), pltpu.SemaphoreType.DMA((n,))],
        name="gather_weights")(*own)


def _scatter_grads(grads):
    n = len(grads)

    def body(*refs):
        src, dst = refs[:n], refs[n:2 * n]
        send, recv = refs[2 * n:]
        x, y, c = _place()
        chips = _other_chips(x, y)
        sends = []
        for k, (px, py) in enumerate(chips):
            for a in range(n):
                cp = pltpu.make_async_remote_copy(src_ref=src[a].at[2 * px + py], dst_ref=dst[a].at[k],
                                                  send_sem=send.at[k * n + a], recv_sem=recv.at[k * n + a],
                                                  device_id=(px, py, c), device_id_type=MESH)
                cp.start()
                sends.append(cp)
        for k, (px, py) in enumerate(chips):
            for a in range(n):
                pltpu.make_async_remote_copy(src_ref=src[a].at[2 * px + py], dst_ref=dst[a].at[k],
                                             send_sem=send.at[k * n + a], recv_sem=recv.at[k * n + a],
                                             device_id=(px, py, c), device_id_type=MESH).wait_recv()
        for cp in sends:
            cp.wait_send()

    return pl.pallas_call(
        body, out_shape=[jax.ShapeDtypeStruct((3,) + g.shape[1:], g.dtype) for g in grads],
        in_specs=[ANY] * n, out_specs=[ANY] * n,
        scratch_shapes=[pltpu.SemaphoreType.DMA((3 * n,)), pltpu.SemaphoreType.DMA((3 * n,))],
        name="scatter_grads")(*grads)


def _swap_sibling(parts):
    n = len(parts)

    def body(*refs):
        src, dst = refs[:n], refs[n:2 * n]
        send, recv = refs[2 * n:]
        x, y, c = _place()
        cps = [pltpu.make_async_remote_copy(src_ref=src[a], dst_ref=dst[a], send_sem=send.at[a], recv_sem=recv.at[a],
                                            device_id=(x, y, 1 - c), device_id_type=MESH) for a in range(n)]
        for cp in cps:
            cp.start()
        for cp in cps:
            cp.wait_recv()
        for cp in cps:
            cp.wait_send()

    return pl.pallas_call(
        body, out_shape=[jax.ShapeDtypeStruct(p.shape, p.dtype) for p in parts],
        in_specs=[ANY] * n, out_specs=[ANY] * n,
        scratch_shapes=[pltpu.SemaphoreType.DMA((n,)), pltpu.SemaphoreType.DMA((n,))],
        name="swap_sibling")(*parts)


def _allreduce_small(name, v):
    R = v.shape[0]

    def body(v_ref, o_ref, buf, send, recv):
        x, y, c = _place()
        me = 4 * x + 2 * y + c
        buf[me] = v_ref[...]
        cps = []
        for k in range(1, 8):
            fx, fy, fc = (k >> 2) & 1, (k >> 1) & 1, k & 1
            px = 1 - x if fx else x
            py = 1 - y if fy else y
            pc = 1 - c if fc else c
            cp = pltpu.make_async_remote_copy(src_ref=v_ref, dst_ref=buf.at[me], send_sem=send.at[k - 1],
                                              recv_sem=recv.at[k - 1], device_id=(px, py, pc), device_id_type=MESH)
            cp.start()
            cps.append((cp, 4 * px + 2 * py + pc))
        for k, (cp, peer) in enumerate(cps):
            pltpu.make_async_remote_copy(src_ref=v_ref, dst_ref=buf.at[peer], send_sem=send.at[k], recv_sem=recv.at[k],
                                         device_id=(x, y, c), device_id_type=MESH).wait_recv()
        for cp, _ in cps:
            cp.wait_send()
        acc = buf[0]
        for d in range(1, 8):
            acc = acc + buf[d]
        o_ref[...] = acc

    return pl.pallas_call(
        body, out_shape=jax.ShapeDtypeStruct((R, 128), F32),
        in_specs=[pl.BlockSpec(memory_space=pltpu.VMEM)], out_specs=pl.BlockSpec(memory_space=pltpu.VMEM),
        scratch_shapes=[pltpu.VMEM((8, R, 128), F32), pltpu.SemaphoreType.DMA((7,)), pltpu.SemaphoreType.DMA((7,))],
        name=name)(v)


def _row_tile(r):
    for t in (256, 128, 64, 32, 16, 8):
        if r % t == 0:
            return t
    raise ValueError(r)


def _sum4(name, own, got):
    R, C = own.shape
    tr = _row_tile(R)

    def body(o_ref, g_ref, s_ref):
        s = o_ref[...].astype(F32)
        for k in range(3):
            s = s + g_ref[k].astype(F32)
        s_ref[...] = s

    return pl.pallas_call(
        body, out_shape=jax.ShapeDtypeStruct((R, C), F32), grid=(R // tr,),
        in_specs=[pl.BlockSpec((tr, C), lambda i: (i, 0)), pl.BlockSpec((3, tr, C), lambda i: (0, i, 0))],
        out_specs=pl.BlockSpec((tr, C), lambda i: (i, 0)), name=name, compiler_params=_cp("parallel"))(own, got)


def _adamw(name, w, gparts, m, v):
    R, C = w.shape
    tr = _row_tile(R)
    ng = len(gparts)
    c1 = 1.0 - ADAM_B1 ** ADAM_STEP
    c2 = 1.0 - ADAM_B2 ** ADAM_STEP

    def body(*refs):
        w_ref = refs[0]
        g_refs = refs[1:1 + ng]
        m_ref, v_ref, go_ref, d_ref, mo_ref, vo_ref = refs[1 + ng:]
        g = g_refs[0][...]
        for r in g_refs[1:]:
            g = g + r[...]
        mn = ADAM_B1 * m_ref[...] + (1.0 - ADAM_B1) * g
        vn = ADAM_B2 * v_ref[...] + (1.0 - ADAM_B2) * (g * g)
        go_ref[...] = g
        mo_ref[...] = mn
        vo_ref[...] = vn
        d_ref[...] = -ADAM_LR * ((mn / c1) / (jnp.sqrt(vn / c2) + ADAM_EPS) + ADAM_WD * w_ref[...])

    blk = pl.BlockSpec((tr, C), lambda i: (i, 0))
    osh = jax.ShapeDtypeStruct((R, C), F32)
    return pl.pallas_call(
        body, out_shape=(osh, osh, osh, osh), grid=(R // tr,), in_specs=[blk] * (3 + ng), out_specs=(blk,) * 4,
        name=name, compiler_params=_cp("parallel"))(w, *gparts, m, v)


BIG = [("ffn1_w_gate", "g1"), ("ffn1_w_up", "u1"), ("ffn1_w_down", "d1"), ("w_in", "win"), ("w_out", "wout"),
       ("ffn2_w_gate", "g2"), ("ffn2_w_up", "u2"), ("ffn2_w_down", "d2")]
SMALL = ["ffn1_norm", "mix_norm", "conv_b", "dt_bias", "a_log", "d_skip", "ssd_norm", "q_norm", "k_norm", "ffn2_norm"]
WEIGHTS = ["ffn1_norm", "ffn1_w_gate", "ffn1_w_up", "ffn1_w_down", "mix_norm", "w_in", "conv_w", "conv_b", "dt_bias",
           "a_log", "d_skip", "ssd_norm", "q_norm", "k_norm", "w_out", "ffn2_norm", "ffn2_w_gate", "ffn2_w_up",
           "ffn2_w_down"]
CONV_SH = CONV_DIM // N_SHARD


def _pad128(v):
    v = v.reshape(-1)
    return jnp.pad(v, (0, (-v.shape[0]) % 128))


def _pack(pieces):
    flat, offs, pos = [], [], 0
    for p in pieces:
        q = _pad128(p.astype(F32))
        offs.append(pos)
        pos += q.shape[0] // 128
        flat.append(q)
    total = -(-pos // 8) * 8
    out = jnp.concatenate(flat + [jnp.zeros(((total - pos) * 128,), F32)]).reshape(total, 128)
    return out, offs


def _unpack(packed, offs, shapes):
    out = []
    for off, shp in zip(offs, shapes):
        n = int(np.prod(shp))
        rows = -(-n // 128)
        out.append(packed[off:off + rows].reshape(-1)[:n].reshape(shp))
    return out


def kernel(x, ffn1_norm, ffn1_w_gate, ffn1_w_up, ffn1_w_down, mix_norm, w_in, conv_w, conv_b, dt_bias, a_log, d_skip, ssd_norm, q_norm, k_norm, w_out, ffn2_norm, ffn2_w_gate, ffn2_w_up, ffn2_w_down, loss_target, m_ffn1_norm, m_ffn1_w_gate, m_ffn1_w_up, m_ffn1_w_down, m_mix_norm, m_w_in, m_conv_w, m_conv_b, m_dt_bias, m_a_log, m_d_skip, m_ssd_norm, m_q_norm, m_k_norm, m_w_out, m_ffn2_norm, m_ffn2_w_gate, m_ffn2_w_up, m_ffn2_w_down, v_ffn1_norm, v_ffn1_w_gate, v_ffn1_w_up, v_ffn1_w_down, v_mix_norm, v_w_in, v_conv_w, v_conv_b, v_dt_bias, v_a_log, v_d_skip, v_ssd_norm, v_q_norm, v_k_norm, v_w_out, v_ffn2_norm, v_ffn2_w_gate, v_ffn2_w_up, v_ffn2_w_down):
    A = dict(locals())
    ix, iy, ic = _place()
    me = 2 * ix + iy
    B, S, _ = x.shape
    T = B * S

    own = [A[name].astype(BF16) for name, _ in BIG]
    gathered = _gather_big(own)
    W = {key: g for (_, key), g in zip(BIG, gathered)}
    W["win"] = [_win_pack(jnp.concatenate([W["win"][j, l] for j in range(N_SHARD)], axis=1)) for l in range(DEPTH)]
    placed = lax.dynamic_update_slice(jnp.zeros((DEPTH, CONV_K, CONV_DIM), F32),
                                      conv_w * (ic == 0).astype(F32), (0, 0, me * CONV_SH))
    conv_full = _allreduce_small("gather_conv_w", placed.reshape(-1, 128)).reshape(DEPTH, CONV_K, CONV_DIM)

    small = {name: A[name] for name in SMALL}
    small["conv_w"] = conv_full
    lsum, dx, sgrads, gbuf, gwin = _local_step(x.reshape(T, D_MODEL), loss_target.reshape(T, D_MODEL), small, W, B)

    names = SMALL + ["conv_w"]
    pieces = [jnp.stack([sgrads[l][n].reshape(small[n].shape[1:]) for l in range(DEPTH)]) for n in names]
    pieces.append(0.5 / D_MODEL * jnp.sum(lsum))
    packed, offs = _pack(pieces)
    red = _allreduce_small("allreduce_small", packed)
    shapes = [small[n].shape for n in names] + [()]
    red = _unpack(red, offs, shapes)
    loss = red[-1]
    sg = dict(zip(names, red[:-1]))

    gwin_st = jnp.stack([jnp.transpose(_win_unpack(gwin[l]).reshape(D_MODEL, N_SHARD, IN_SH), (1, 0, 2))
                         for l in range(DEPTH)], axis=1)
    gbuf = dict(gbuf, win=gwin_st)
    glist = [gbuf[key] for _, key in BIG]
    got = _scatter_grads(glist)
    sums = []
    for (name, key), g, r in zip(BIG, glist, got):
        _, _, R, C = g.shape
        mine = lax.dynamic_index_in_dim(g, me, axis=0, keepdims=False).reshape(DEPTH * R, C)
        sums.append(_sum4("sum_" + key, mine, r.reshape(3, DEPTH * R, C)))
    theirs = _swap_sibling(sums)

    out = {}
    for (name, key), s, t in zip(BIG, sums, theirs):
        shp = A[name].shape
        flat = lambda a: a.reshape(shp[0] * shp[1], shp[2])
        res = _adamw("adamw_" + key, flat(A[name]), [s, t], flat(A["m_" + name]), flat(A["v_" + name]))
        out[name] = [r.reshape(shp) for r in res]

    wp, offs = _pack([A[n] for n in SMALL])
    gp, _ = _pack([sg[n] for n in SMALL])
    mp, _ = _pack([A["m_" + n] for n in SMALL])
    vp, _ = _pack([A["v_" + n] for n in SMALL])
    res = _adamw("adamw_small", wp, [gp], mp, vp)
    shapes = [A[n].shape for n in SMALL]
    res = [_unpack(r, offs, shapes) for r in res]
    for i, n in enumerate(SMALL):
        out[n] = [res[q][i] for q in range(4)]
    gcw = lax.dynamic_slice_in_dim(sg["conv_w"], me * CONV_SH, CONV_SH, axis=2)
    flat = lambda a: a.reshape(DEPTH * CONV_K, CONV_SH)
    res = _adamw("adamw_conv_w", flat(conv_w), [flat(gcw)], flat(m_conv_w), flat(v_conv_w))
    out["conv_w"] = [r.reshape(conv_w.shape) for r in res]

    outs = [loss, dx.reshape(B, S, D_MODEL)]
    for q in range(4):
        outs += [out[n][q] for n in WEIGHTS]
    return tuple(outs)
```

```python
import functools
import math

import numpy as np
import jax
import jax.numpy as jnp
from jax import lax
from jax.experimental import pallas as pl
from jax.experimental.pallas import tpu as pltpu

F32 = jnp.float32
BF16 = jnp.bfloat16

D_MODEL = 1024
DEPTH = 2
N_SHARD = 4
D_FF = 2816
FF_SH = D_FF // N_SHARD
SSD_HEADS = 16
HEAD_DIM = 64
SSD_GROUPS = 4
GROUP_W = 256
SSD_STATE = 128
CONV_K = 4
CONV_DIM = 2048
ATT_HEADS = 16
MIX_W = 2048
MIX_SH = MIX_W // N_SHARD
IN_PROJ = 6160
IN_SH = IN_PROJ // N_SHARD
IN_PAD = 6272
PROJ_TN = 896
COL_Z, COL_XBC, COL_Q, COL_K, COL_V, COL_DT = 0, 1024, 3072, 4096, 5120, 6144
EPS = 1e-6
NEG = -1e30
SSD_L = 256
ATT_B = 256
ROW_T = 512
CONV_CT = 256
CONV_R = 256
PAD_R = 8

ADAM_LR, ADAM_B1, ADAM_B2, ADAM_EPS, ADAM_WD, ADAM_STEP = 0.001, 0.9, 0.999, 1e-08, 0.01, 10

NN = (((1,), (0,)), ((), ()))
NT = (((1,), (1,)), ((), ()))
TN = (((0,), (0,)), ((), ()))

VMEM_LIMIT = 56 * 1024 * 1024


def _cp(*sem):
    return pltpu.CompilerParams(dimension_semantics=sem, vmem_limit_bytes=VMEM_LIMIT)


def _dot(a, b, dims):
    return lax.dot_general(a, b, dims, preferred_element_type=F32)


def _sigmoid(x):
    return 1.0 / (1.0 + jnp.exp(-x))


def _softplus(x):
    return jnp.maximum(x, 0.0) + jnp.log(1.0 + jnp.exp(-jnp.abs(x)))


def _mm(name, pairs, out_shape, out_spec, grid, dims, acc_shape, res=None, scale=1.0, prev=None):
    nk = grid[2]
    npair = len(pairs)

    def body(*refs):
        ab = refs[:2 * npair]
        pos = 2 * npair
        res_ref = None
        if res is not None:
            res_ref = refs[pos]
            pos += 1
        if prev is not None:
            pos += 1
        out_ref, acc = refs[pos], refs[pos + 1]
        k = pl.program_id(2)

        @pl.when(k == 0)
        def _():
            acc[...] = jnp.zeros_like(acc)

        s = None
        for p in range(npair):
            d = _dot(ab[2 * p][...].astype(BF16), ab[2 * p + 1][...].astype(BF16), dims)
            s = d if s is None else s + d
        acc[...] += s

        @pl.when(k == nk - 1)
        def _():
            r = acc[...]
            if scale != 1.0:
                r = r * scale
            if res_ref is not None:
                r = r + res_ref[...]
            out_ref[...] = r.astype(out_ref.dtype)

    args, specs = [], []
    for a, a_spec, b, b_spec in pairs:
        args += [a, b]
        specs += [a_spec, b_spec]
    if res is not None:
        args.append(res[0])
        specs.append(res[1])
    aliases = {}
    if prev is not None:
        aliases = {len(args): 0}
        args.append(prev)
        specs.append(pl.BlockSpec(memory_space=pl.ANY))
    return pl.pallas_call(
        body, out_shape=out_shape, grid=grid, in_specs=specs, out_specs=out_spec,
        scratch_shapes=[pltpu.VMEM(acc_shape, F32)], input_output_aliases=aliases, name=name,
        compiler_params=_cp("parallel", "parallel", "arbitrary"))(*args)


def _rms_fwd(name, x, w):
    T = x.shape[0]

    def body(x_ref, w_ref, o_ref):
        xv = x_ref[...]
        r = lax.rsqrt(jnp.mean(xv * xv, axis=-1, keepdims=True) + EPS)
        o_ref[...] = (xv * r * w_ref[...]).astype(BF16)

    return pl.pallas_call(
        body, out_shape=jax.ShapeDtypeStruct((T, D_MODEL), BF16), grid=(T // ROW_T,),
        in_specs=[pl.BlockSpec((ROW_T, D_MODEL), lambda i: (i, 0)), pl.BlockSpec((1, D_MODEL), lambda i: (0, 0))],
        out_specs=pl.BlockSpec((ROW_T, D_MODEL), lambda i: (i, 0)), name=name, compiler_params=_cp("parallel"))(x, w)


def _rms_bwd(name, dh, x, w, dres):
    T = x.shape[0]

    def body(dh_ref, x_ref, w_ref, dres_ref, dx_ref, dw_ref):
        @pl.when(pl.program_id(0) == 0)
        def _():
            dw_ref[...] = jnp.zeros_like(dw_ref)

        xv = x_ref[...]
        r = lax.rsqrt(jnp.mean(xv * xv, axis=-1, keepdims=True) + EPS)
        xhat = xv * r
        dhv = dh_ref[...]
        dxhat = dhv * w_ref[...]
        m = jnp.mean(dxhat * xhat, axis=-1, keepdims=True)
        dx_ref[...] = dres_ref[...] + r * (dxhat - xhat * m)
        dw_ref[...] += jnp.sum(dhv * xhat, axis=0, keepdims=True)

    row = pl.BlockSpec((ROW_T, D_MODEL), lambda i: (i, 0))
    vec = pl.BlockSpec((1, D_MODEL), lambda i: (0, 0))
    return pl.pallas_call(
        body, out_shape=(jax.ShapeDtypeStruct((T, D_MODEL), F32), jax.ShapeDtypeStruct((1, D_MODEL), F32)),
        grid=(T // ROW_T,), in_specs=[row, row, vec, row], out_specs=(row, vec), name=name,
        compiler_params=_cp("arbitrary"))(dh, x, w, dres)


def _loss_grad(name, y, t):
    T = y.shape[0]

    def body(y_ref, t_ref, dy_ref, l_ref):
        @pl.when(pl.program_id(0) == 0)
        def _():
            l_ref[...] = jnp.zeros_like(l_ref)

        e = y_ref[...] - t_ref[...]
        dy_ref[...] = e * (1.0 / D_MODEL)
        l_ref[...] += jnp.sum(e * e, axis=0, keepdims=True)

    row = pl.BlockSpec((ROW_T, D_MODEL), lambda i: (i, 0))
    vec = pl.BlockSpec((1, D_MODEL), lambda i: (0, 0))
    return pl.pallas_call(
        body, out_shape=(jax.ShapeDtypeStruct((T, D_MODEL), F32), jax.ShapeDtypeStruct((1, D_MODEL), F32)),
        grid=(T // ROW_T,), in_specs=[row, row], out_specs=(row, vec), name=name,
        compiler_params=_cp("arbitrary"))(y, t)


def _ffn_gate_up(name, h, wg, wu, l):
    T = h.shape[0]

    def body(h_ref, wg_ref, wu_ref, g_ref, u_ref, a_ref):
        hv = h_ref[...]
        g = _dot(hv, wg_ref[...], NN)
        u = _dot(hv, wu_ref[...], NN)
        g_ref[...] = g.astype(BF16)
        u_ref[...] = u.astype(BF16)
        a_ref[...] = (g * _sigmoid(g) * u).astype(BF16)

    wspec = pl.BlockSpec((None, None, D_MODEL, FF_SH), lambda j, i: (j, l, 0, 0))
    ospec = pl.BlockSpec((None, ROW_T, FF_SH), lambda j, i: (j, i, 0))
    osh = jax.ShapeDtypeStruct((N_SHARD, T, FF_SH), BF16)
    return pl.pallas_call(
        body, out_shape=(osh, osh, osh), grid=(N_SHARD, T // ROW_T),
        in_specs=[pl.BlockSpec((ROW_T, D_MODEL), lambda j, i: (i, 0)), wspec, wspec],
        out_specs=(ospec, ospec, ospec), name=name, compiler_params=_cp("parallel", "parallel"))(h, wg, wu)


def _ffn_dact(name, dx, wd, g, u, l):
    T = dx.shape[0]

    def body(dx_ref, wd_ref, g_ref, u_ref, dg_ref, du_ref):
        da = 0.5 * _dot(dx_ref[...].astype(BF16), wd_ref[...], NT)
        gv = g_ref[...].astype(F32)
        uv = u_ref[...].astype(F32)
        sg = _sigmoid(gv)
        dg_ref[...] = (da * uv * (sg * (1.0 + gv * (1.0 - sg)))).astype(BF16)
        du_ref[...] = (da * gv * sg).astype(BF16)

    aspec = pl.BlockSpec((None, ROW_T, FF_SH), lambda j, i: (j, i, 0))
    osh = jax.ShapeDtypeStruct((N_SHARD, T, FF_SH), BF16)
    return pl.pallas_call(
        body, out_shape=(osh, osh), grid=(N_SHARD, T // ROW_T),
        in_specs=[pl.BlockSpec((ROW_T, D_MODEL), lambda j, i: (i, 0)),
                  pl.BlockSpec((None, None, FF_SH, D_MODEL), lambda j, i: (j, l, 0, 0)), aspec, aspec],
        out_specs=(aspec, aspec), name=name, compiler_params=_cp("parallel", "parallel"))(dx, wd, g, u)


def _ffn_fwd(tag, x, nw, wg, wu, wd, l):
    T = x.shape[0]
    h = _rms_fwd(tag + "_rms", x, nw)
    g, u, a = _ffn_gate_up(tag + "_gu", h, wg, wu, l)
    nt = T // ROW_T
    xo = _mm(tag + "_down",
             [(a, pl.BlockSpec((None, ROW_T, FF_SH), lambda i, n, k: (k, i, 0)),
               wd, pl.BlockSpec((None, None, FF_SH, D_MODEL), lambda i, n, k: (k, l, 0, 0)))],
             jax.ShapeDtypeStruct((T, D_MODEL), F32), pl.BlockSpec((ROW_T, D_MODEL), lambda i, n, k: (i, 0)),
             (nt, 1, N_SHARD), NN, (ROW_T, D_MODEL),
             res=(x, pl.BlockSpec((ROW_T, D_MODEL), lambda i, n, k: (i, 0))), scale=0.5)
    return xo, (x, h, g, u, a)


def _ffn_bwd(tag, dxo, saved, nw, wg, wu, wd, l, gbuf):
    x, h, g, u, a = saved
    T = x.shape[0]
    nt = T // ROW_T
    dg, du = _ffn_dact(tag + "_dact", dxo, wd, g, u, l)
    act = lambda f: pl.BlockSpec((None, ROW_T, FF_SH), f)
    gd = _mm(tag + "_dwd",
             [(a, act(lambda m, n, k: (m, k, 0)), dxo, pl.BlockSpec((ROW_T, D_MODEL), lambda m, n, k: (k, 0)))],
             jax.ShapeDtypeStruct((N_SHARD, DEPTH, FF_SH, D_MODEL), BF16),
             pl.BlockSpec((None, None, FF_SH, D_MODEL), lambda m, n, k: (m, l, 0, 0)),
             (N_SHARD, 1, nt), TN, (FF_SH, D_MODEL), scale=0.5, prev=gbuf[2])
    hspec = pl.BlockSpec((ROW_T, D_MODEL), lambda j, n, k: (k, 0))
    gsh = jax.ShapeDtypeStruct((N_SHARD, DEPTH, D_MODEL, FF_SH), BF16)
    gspec = pl.BlockSpec((None, None, D_MODEL, FF_SH), lambda j, n, k: (j, l, 0, 0))
    gg = _mm(tag + "_dwg", [(h, hspec, dg, act(lambda j, n, k: (j, k, 0)))], gsh, gspec,
             (N_SHARD, 1, nt), TN, (D_MODEL, FF_SH), prev=gbuf[0])
    gu = _mm(tag + "_dwu", [(h, hspec, du, act(lambda j, n, k: (j, k, 0)))], gsh, gspec,
             (N_SHARD, 1, nt), TN, (D_MODEL, FF_SH), prev=gbuf[1])
    wspec = pl.BlockSpec((None, None, D_MODEL, FF_SH), lambda i, n, k: (k, l, 0, 0))
    dh = _mm(tag + "_dh",
             [(dg, act(lambda i, n, k: (k, i, 0)), wg, wspec), (du, act(lambda i, n, k: (k, i, 0)), wu, wspec)],
             jax.ShapeDtypeStruct((T, D_MODEL), F32), pl.BlockSpec((ROW_T, D_MODEL), lambda i, n, k: (i, 0)),
             (nt, 1, N_SHARD), NT, (ROW_T, D_MODEL))
    dx, dnw = _rms_bwd(tag + "_rmsb", dh, x, nw, dxo)
    return dx, dnw, (gg, gu, gd)


def _conv_fwd(name, xpad, w, b):
    B, SP, C = xpad.shape
    S = SP - 2 * PAD_R

    def body(x_ref, w_ref, b_ref, o_ref):
        wv = w_ref[...]
        for c in range(S // CONV_R):
            r0 = c * CONV_R
            ch = x_ref[pl.ds(r0, CONV_R + PAD_R), :]
            pre = ch[PAD_R:] * wv[3:4] + b_ref[...]
            for s in range(1, CONV_K):
                pre = pre + pltpu.roll(ch, s, axis=0)[PAD_R:] * wv[3 - s:4 - s]
            o_ref[pl.ds(r0, CONV_R), :] = pre * _sigmoid(pre)

    return pl.pallas_call(
        body, out_shape=jax.ShapeDtypeStruct((B, S, C), F32), grid=(B, C // CONV_CT),
        in_specs=[pl.BlockSpec((None, SP, CONV_CT), lambda bi, ci: (bi, 0, ci)),
                  pl.BlockSpec((CONV_K, CONV_CT), lambda bi, ci: (0, ci)),
                  pl.BlockSpec((1, CONV_CT), lambda bi, ci: (0, ci))],
        out_specs=pl.BlockSpec((None, S, CONV_CT), lambda bi, ci: (bi, 0, ci)), name=name,
        compiler_params=_cp("parallel", "parallel"))(xpad, w, b)


def _conv_bwd(name, xpad, dxc_pad, w, b):
    B, SP, C = xpad.shape
    S = SP - 2 * PAD_R
    RW = CONV_R + PAD_R

    def body(x_ref, d_ref, w_ref, b_ref, dx_ref, dw_ref, db_ref):
        @pl.when(pl.program_id(1) == 0)
        def _():
            dw_ref[...] = jnp.zeros_like(dw_ref)
            db_ref[...] = jnp.zeros_like(db_ref)

        wv = w_ref[...]
        dw = [jnp.zeros((1, CONV_CT), F32) for _ in range(CONV_K)]
        db = jnp.zeros((1, CONV_CT), F32)
        for c in range(S // CONV_R):
            r0 = c * CONV_R
            ch = x_ref[pl.ds(r0, RW + PAD_R), :]
            xs = [ch[PAD_R:]] + [pltpu.roll(ch, s, axis=0)[PAD_R:] for s in range(1, CONV_K)]
            pre = b_ref[...] + xs[0] * wv[3:4]
            for s in range(1, CONV_K):
                pre = pre + xs[s] * wv[3 - s:4 - s]
            sg = _sigmoid(pre)
            dpre = d_ref[pl.ds(r0, RW), :] * (sg * (1.0 + pre * (1.0 - sg)))
            dx = dpre[:CONV_R] * wv[3:4]
            for s in range(1, CONV_K):
                dx = dx + pltpu.roll(dpre, RW - s, axis=0)[:CONV_R] * wv[3 - s:4 - s]
            dx_ref[pl.ds(r0, CONV_R), :] = dx
            dcur = dpre[:CONV_R]
            db = db + jnp.sum(dcur, axis=0, keepdims=True)
            for s in range(CONV_K):
                dw[3 - s] = dw[3 - s] + jnp.sum(dcur * xs[s][:CONV_R], axis=0, keepdims=True)
        db_ref[...] += db
        for k in range(CONV_K):
            dw_ref[k:k + 1, :] += dw[k]

    return pl.pallas_call(
        body,
        out_shape=(jax.ShapeDtypeStruct((B, S, C), F32), jax.ShapeDtypeStruct((CONV_K, C), F32),
                   jax.ShapeDtypeStruct((1, C), F32)),
        grid=(C // CONV_CT, B),
        in_specs=[pl.BlockSpec((None, SP, CONV_CT), lambda ci, bi: (bi, 0, ci)),
                  pl.BlockSpec((None, S + PAD_R, CONV_CT), lambda ci, bi: (bi, 0, ci)),
                  pl.BlockSpec((CONV_K, CONV_CT), lambda ci, bi: (0, ci)),
                  pl.BlockSpec((1, CONV_CT), lambda ci, bi: (0, ci))],
        out_specs=(pl.BlockSpec((None, S, CONV_CT), lambda ci, bi: (bi, 0, ci)),
                   pl.BlockSpec((CONV_K, CONV_CT), lambda ci, bi: (0, ci)),
                   pl.BlockSpec((1, CONV_CT), lambda ci, bi: (0, ci))),
        name=name, compiler_params=_cp("parallel", "arbitrary"))(xpad, dxc_pad, w, b)


def _ssd_common(dtc_ref, dtr_ref, pcol_ref, prow_ref, b_ref, c_ref):
    L = SSD_L
    bias_c, alog_c = pcol_ref[0:1, :], pcol_ref[1:2, :]
    a_c = -jnp.exp(alog_c)
    dt_c = _softplus(dtc_ref[...] + bias_c)
    row = lax.broadcasted_iota(jnp.int32, (L, L), 0)
    col = lax.broadcasted_iota(jnp.int32, (L, L), 1)
    causal = row >= col
    tri = causal.astype(F32)
    hp = lax.Precision.HIGHEST
    cum_c = lax.dot_general(tri, dt_c * a_c, NN, precision=hp, preferred_element_type=F32)
    a_r = -jnp.exp(prow_ref[:, 1:2])
    dt_r = _softplus(dtr_ref[...] + prow_ref[:, 0:1])
    cum_r = lax.dot_general(dt_r * a_r, tri, NT, precision=hp, preferred_element_type=F32)
    bb = b_ref[...].astype(BF16)
    cb = c_ref[...].astype(BF16)
    G = _dot(cb, bb, NT)
    return a_c, dt_c, causal, tri, cum_c, cum_r, bb, cb, G


def _ssd_fwd(name, xc, proj, dtc, dtr, pcol, prow, nw, B):
    T = xc.shape[0]
    S = T // B
    nb = S // SSD_L
    L = SSD_L

    def body(xs_ref, b_ref, c_ref, z_ref, dtc_ref, dtr_ref, pcol_ref, prow_ref, nw_ref, y_ref, yn_ref, hs_ref, H):
        @pl.when(pl.program_id(2) == 0)
        def _():
            H[...] = jnp.zeros_like(H)

        a_c, dt_c, causal, tri, cum_c, cum_r, bb, cb, G = _ssd_common(dtc_ref, dtr_ref, pcol_ref, prow_ref, b_ref, c_ref)
        dsk = pcol_ref[2:3, :]
        clast = cum_c[L - 1:L, :]
        bf = b_ref[...]
        for h in range(4):
            sl = slice(HEAD_DIM * h, HEAD_DIM * (h + 1))
            cc = cum_c[:, h:h + 1]
            lm = jnp.exp(jnp.where(causal, cc - cum_r[h:h + 1, :], NEG))
            M = (G * lm).astype(BF16)
            xh = xs_ref[:, sl]
            Xb = (xh * dt_c[:, h:h + 1]).astype(BF16)
            Hh = H[h]
            y = _dot(M, Xb, NN) + jnp.exp(cc) * _dot(cb, Hh.astype(BF16), NN)
            y_ref[:, sl] = y + dsk[:, h:h + 1] * xh
            hs_ref[h] = Hh
            cl = clast[:, h:h + 1]
            Bw = (bf * jnp.exp(cl - cc)).astype(BF16)
            H[h] = jnp.exp(cl) * Hh + _dot(Bw, Xb, TN)
        zv = z_ref[...]
        y2 = y_ref[...] * (zv * _sigmoid(zv))
        r = lax.rsqrt(jnp.mean(y2 * y2, axis=-1, keepdims=True) + EPS)
        yn_ref[...] = (y2 * r * nw_ref[...]).astype(BF16)

    rowi = lambda b, g, i: b * nb + i
    grp = pl.BlockSpec((L, GROUP_W), lambda b, g, i: (rowi(b, g, i), g))
    return pl.pallas_call(
        body,
        out_shape=(jax.ShapeDtypeStruct((T, 1024), F32), jax.ShapeDtypeStruct((T, 1024), BF16),
                   jax.ShapeDtypeStruct((B, SSD_GROUPS, nb, 4, SSD_STATE, HEAD_DIM), F32)),
        grid=(B, SSD_GROUPS, nb),
        in_specs=[grp,
                  pl.BlockSpec((L, SSD_STATE), lambda b, g, i: (rowi(b, g, i), 8 + g)),
                  pl.BlockSpec((L, SSD_STATE), lambda b, g, i: (rowi(b, g, i), 12 + g)),
                  grp,
                  pl.BlockSpec((None, L, 4), lambda b, g, i: (g, rowi(b, g, i), 0)),
                  pl.BlockSpec((None, 4, L), lambda b, g, i: (g, 0, rowi(b, g, i))),
                  pl.BlockSpec((None, 3, 4), lambda b, g, i: (g, 0, 0)),
                  pl.BlockSpec((None, 4, 3), lambda b, g, i: (g, 0, 0)),
                  pl.BlockSpec((1, GROUP_W), lambda b, g, i: (0, g))],
        out_specs=(grp, grp,
                   pl.BlockSpec((None, None, None, 4, SSD_STATE, HEAD_DIM), lambda b, g, i: (b, g, i, 0, 0, 0))),
        scratch_shapes=[pltpu.VMEM((4, SSD_STATE, HEAD_DIM), F32)], name=name,
        compiler_params=_cp("parallel", "parallel", "arbitrary"))(xc, xc, xc, proj, dtc, dtr, pcol, prow, nw)


def _ssd_bwd(name, dyn, Y, xc, proj, dtc, dtr, pcol, prow, nw, hs, B):
    T = xc.shape[0]
    S = T // B
    nb = S // SSD_L
    L = SSD_L

    def body(dyn_ref, y_ref, xs_ref, b_ref, c_ref, z_ref, dtc_ref, dtr_ref, pcol_ref, prow_ref, nw_ref, hs_ref,
             dxs_ref, db_ref, dc_ref, dz_ref, ddt_ref, dpar_ref, dnw_ref, dH):
        @pl.when(pl.program_id(2) == 0)
        def _():
            dH[...] = jnp.zeros_like(dH)
            dpar_ref[...] = jnp.zeros_like(dpar_ref)
            dnw_ref[...] = jnp.zeros_like(dnw_ref)

        a_c, dt_c, causal, tri, cum_c, cum_r, bb, cb, G = _ssd_common(dtc_ref, dtr_ref, pcol_ref, prow_ref, b_ref, c_ref)
        dsk = pcol_ref[2:3, :]
        clast = cum_c[L - 1:L, :]
        bf = b_ref[...]
        cf = c_ref[...]
        Yv = y_ref[...]
        zv = z_ref[...]
        sz = _sigmoid(zv)
        silu = zv * sz
        y2 = Yv * silu
        r = lax.rsqrt(jnp.mean(y2 * y2, axis=-1, keepdims=True) + EPS)
        yhat = y2 * r
        dyv = dyn_ref[...]
        dnw_ref[...] += jnp.sum(dyv * yhat, axis=0, keepdims=True)
        dyhat = dyv * nw_ref[...]
        dy2 = r * (dyhat - yhat * jnp.mean(dyhat * yhat, axis=-1, keepdims=True))
        dY = dy2 * silu
        dz_ref[...] = dy2 * Yv * (sz * (1.0 + zv * (1.0 - sz)))

        lane4 = lax.broadcasted_iota(jnp.int32, (1, 4), 1)
        dG = jnp.zeros((L, L), F32)
        dBs = jnp.zeros((L, SSD_STATE), F32)
        dCs = jnp.zeros((L, SSD_STATE), F32)
        dA = jnp.zeros((L, 4), F32)
        ddtx = jnp.zeros((L, 4), F32)
        ddsk = jnp.zeros((1, 4), F32)
        dcl = jnp.zeros((1, 4), F32)
        for h in range(4):
            sl = slice(HEAD_DIM * h, HEAD_DIM * (h + 1))
            onehot = (lane4 == h).astype(F32)
            cc = cum_c[:, h:h + 1]
            cl = clast[:, h:h + 1]
            lm = jnp.exp(jnp.where(causal, cc - cum_r[h:h + 1, :], NEG))
            M = (G * lm).astype(BF16)
            xh = xs_ref[:, sl]
            dth = dt_c[:, h:h + 1]
            X = xh * dth
            Xb = X.astype(BF16)
            dYh = dY[:, sl]
            dYb = dYh.astype(BF16)
            Hb = hs_ref[h].astype(BF16)
            dHh = dH[h]
            dHb = dHh.astype(BF16)
            alpha = jnp.exp(cc)
            beta = jnp.exp(cl - cc)
            dXoff = beta * _dot(bb, dHb, NN)
            dX = _dot(M, dYb, TN) + dXoff
            dG = dG + _dot(dYb, Xb, NT) * lm
            dCs = dCs + _dot((alpha * dYh).astype(BF16), Hb, NT)
            dBs = dBs + _dot((beta * X).astype(BF16), dHb, NT)
            ypre = Yv[:, sl] - dsk[:, h:h + 1] * xh
            dA_h = (jnp.sum(dYb.astype(F32) * ypre, axis=-1, keepdims=True)
                    - jnp.sum(Xb.astype(F32) * dX, axis=-1, keepdims=True))
            dA = dA + dA_h * onehot
            dcl_h = (jnp.sum(jnp.sum(dHh * (jnp.exp(cl) * hs_ref[h]), axis=-1, keepdims=True), axis=0, keepdims=True)
                     + jnp.sum(jnp.sum(Xb.astype(F32) * dXoff, axis=-1, keepdims=True), axis=0, keepdims=True))
            dcl = dcl + dcl_h * onehot
            ddtx = ddtx + jnp.sum(dX * xh, axis=-1, keepdims=True) * onehot
            ddsk = ddsk + jnp.sum(jnp.sum(dYh * xh, axis=-1, keepdims=True), axis=0, keepdims=True) * onehot
            dxs_ref[:, sl] = dsk[:, h:h + 1] * dYh + dX * dth
            dH[h] = jnp.exp(cl) * dHh + _dot((alpha * cf).astype(BF16), dYb, TN)
        dGb = dG.astype(BF16)
        dc_ref[...] = _dot(dGb, bb, NN) + dCs
        db_ref[...] = _dot(dGb, cb, TN) + dBs
        hp = lax.Precision.HIGHEST
        last = lax.broadcasted_iota(jnp.int32, (L, 1), 0) == L - 1
        dA = dA + jnp.where(last, dcl, 0.0)
        dadt = lax.dot_general(tri, dA, TN, precision=hp, preferred_element_type=F32)
        ddt = dadt * a_c + ddtx
        d_a = jnp.sum(dadt * dt_c, axis=0, keepdims=True)
        ddraw = ddt * _sigmoid(dtc_ref[...] + pcol_ref[0:1, :])
        ddt_ref[...] = ddraw
        dpar_ref[0:1, :] += jnp.sum(ddraw, axis=0, keepdims=True)
        dpar_ref[1:2, :] += d_a * a_c
        dpar_ref[2:3, :] += ddsk

    rowi = lambda b, g, i: b * nb + (nb - 1 - i)
    grp = pl.BlockSpec((L, GROUP_W), lambda b, g, i: (rowi(b, g, i), g))
    st = pl.BlockSpec((L, SSD_STATE), lambda b, g, i: (rowi(b, g, i), g))
    f = jax.ShapeDtypeStruct
    return pl.pallas_call(
        body,
        out_shape=(f((T, 1024), F32), f((T, 512), F32), f((T, 512), F32), f((T, 1024), F32),
                   f((SSD_GROUPS, T, 4), F32), f((B, SSD_GROUPS, 3, 4), F32), f((B, 1, 1024), F32)),
        grid=(B, SSD_GROUPS, nb),
        in_specs=[grp, grp, grp,
                  pl.BlockSpec((L, SSD_STATE), lambda b, g, i: (rowi(b, g, i), 8 + g)),
                  pl.BlockSpec((L, SSD_STATE), lambda b, g, i: (rowi(b, g, i), 12 + g)),
                  grp,
                  pl.BlockSpec((None, L, 4), lambda b, g, i: (g, rowi(b, g, i), 0)),
                  pl.BlockSpec((None, 4, L), lambda b, g, i: (g, 0, rowi(b, g, i))),
                  pl.BlockSpec((None, 3, 4), lambda b, g, i: (g, 0, 0)),
                  pl.BlockSpec((None, 4, 3), lambda b, g, i: (g, 0, 0)),
                  pl.BlockSpec((1, GROUP_W), lambda b, g, i: (0, g)),
                  pl.BlockSpec((None, None, None, 4, SSD_STATE, HEAD_DIM), lambda b, g, i: (b, g, nb - 1 - i, 0, 0, 0))],
        out_specs=(grp, st, st, grp,
                   pl.BlockSpec((None, L, 4), lambda b, g, i: (g, rowi(b, g, i), 0)),
                   pl.BlockSpec((None, None, 3, 4), lambda b, g, i: (b, g, 0, 0)),
                   pl.BlockSpec((None, 1, GROUP_W), lambda b, g, i: (b, 0, g))),
        scratch_shapes=[pltpu.VMEM((4, SSD_STATE, HEAD_DIM), F32)], name=name,
        compiler_params=_cp("parallel", "parallel", "arbitrary"))(dyn, Y, xc, xc, xc, proj, dtc, dtr, pcol, prow, nw, hs)


def _headnorm_fwd(name, proj, col_block, w):
    T = proj.shape[0]

    def body(x_ref, w_ref, o_ref):
        for h in range(ATT_HEADS):
            sl = slice(HEAD_DIM * h, HEAD_DIM * (h + 1))
            xh = x_ref[:, sl]
            r = lax.rsqrt(jnp.mean(xh * xh, axis=-1, keepdims=True) + EPS)
            o_ref[:, sl] = (xh * r * w_ref[...]).astype(BF16)

    return pl.pallas_call(
        body, out_shape=jax.ShapeDtypeStruct((T, 1024), BF16), grid=(T // ROW_T,),
        in_specs=[pl.BlockSpec((ROW_T, 1024), lambda i: (i, col_block)), pl.BlockSpec((1, HEAD_DIM), lambda i: (0, 0))],
        out_specs=pl.BlockSpec((ROW_T, 1024), lambda i: (i, 0)), name=name, compiler_params=_cp("parallel"))(proj, w)


def _headnorm_bwd(name, dn, proj, col_block, w):
    T = proj.shape[0]

    def body(dn_ref, x_ref, w_ref, dx_ref, dw_ref):
        @pl.when(pl.program_id(0) == 0)
        def _():
            dw_ref[...] = jnp.zeros_like(dw_ref)

        dw = jnp.zeros((1, HEAD_DIM), F32)
        for h in range(ATT_HEADS):
            sl = slice(HEAD_DIM * h, HEAD_DIM * (h + 1))
            xh = x_ref[:, sl]
            r = lax.rsqrt(jnp.mean(xh * xh, axis=-1, keepdims=True) + EPS)
            xhat = xh * r
            dnh = dn_ref[:, sl]
            dxhat = dnh * w_ref[...]
            dx_ref[:, sl] = r * (dxhat - xhat * jnp.mean(dxhat * xhat, axis=-1, keepdims=True))
            dw = dw + jnp.sum(dnh * xhat, axis=0, keepdims=True)
        dw_ref[...] += dw

    return pl.pallas_call(
        body, out_shape=(jax.ShapeDtypeStruct((T, 1024), F32), jax.ShapeDtypeStruct((1, HEAD_DIM), F32)),
        grid=(T // ROW_T,),
        in_specs=[pl.BlockSpec((ROW_T, 1024), lambda i: (i, 0)), pl.BlockSpec((ROW_T, 1024), lambda i: (i, col_block)),
                  pl.BlockSpec((1, HEAD_DIM), lambda i: (0, 0))],
        out_specs=(pl.BlockSpec((ROW_T, 1024), lambda i: (i, 0)), pl.BlockSpec((1, HEAD_DIM), lambda i: (0, 0))),
        name=name, compiler_params=_cp("arbitrary"))(dn, proj, w)


def _att_bias(nq):
    j = np.arange(ATT_B)[:, None]
    i = np.arange(ATT_B)[None, :]
    out = np.empty((nq, ATT_B, ATT_B), np.float32)
    for dblk in range(nq):
        dl = ATT_B * dblk + i - j
        cnt = ((dl >= 0) & (dl <= 128)).astype(np.float32)
        cnt += ((dl >= 0) & (dl % 4 == 0) & (dl <= 512))
        cnt += ((dl >= 0) & (dl % 16 == 0) & (dl <= 2048))
        out[dblk] = np.where(cnt > 0, np.log(np.maximum(cnt, 1.0)), NEG)
    return jnp.asarray(out)


def _row_pair(nq):
    def f(r, c):
        first = c <= r
        return jnp.where(first, r, nq - 1 - r), jnp.where(first, c, c - (r + 1))
    return f


def _col_pair(nq):
    def f(r, c):
        first = c < nq - r
        kj = jnp.where(first, r, nq - 1 - r)
        return jnp.where(first, r + c, nq - 1 - r + (c - (nq - r))), kj
    return f


ATT_SCALE = 1.0 / math.sqrt(HEAD_DIM)
ATT_HS = 4
ATT_W = ATT_HS * HEAD_DIM


def _att_maps(nq, qk):
    return dict(
        q_tok=lambda b, g, r, c: (b * nq + qk(r, c)[0], g),
        k_tok=lambda b, g, r, c: (b * nq + qk(r, c)[1], g),
        q_feat=lambda b, g, r, c: (g, b * nq + qk(r, c)[0]),
        k_feat=lambda b, g, r, c: (g, b * nq + qk(r, c)[1]),
        bias=lambda b, g, r, c: (qk(r, c)[0] - qk(r, c)[1], 0, 0),
        lse=lambda b, g, r, c: (g, 0, b * nq + qk(r, c)[0]),
        do_tok=lambda b, g, r, c: (b * nq + qk(r, c)[0], ATT_HS + g))


def _att_fwd(name, kn, qT, vT, bias, B):
    T = kn.shape[0]
    nq = (T // B) // ATT_B
    qk = _row_pair(nq)
    mp = _att_maps(nq, qk)

    def body(k_ref, qT_ref, vT_ref, bias_ref, oT_ref, lse_ref, m_s, l_s, acc_s, s_s):
        qi, kj = qk(pl.program_id(2), pl.program_id(3))

        @pl.when(kj == 0)
        def _():
            m_s[...] = jnp.full_like(m_s, NEG)
            l_s[...] = jnp.zeros_like(l_s)
            acc_s[...] = jnp.zeros_like(acc_s)

        bv = bias_ref[...]
        for h in range(ATT_HS):
            rs = slice(HEAD_DIM * h, HEAD_DIM * (h + 1))
            s_s[h] = _dot(k_ref[:, rs], qT_ref[rs, :], NN)
        for h in range(ATT_HS):
            rs = slice(HEAD_DIM * h, HEAD_DIM * (h + 1))
            s = s_s[h] + bv
            m_prev = m_s[h:h + 1, :]
            m_new = jnp.maximum(m_prev, jnp.max(s, axis=0, keepdims=True))
            alpha = jnp.exp(m_prev - m_new)
            p = jnp.exp(s - m_new)
            l_s[h:h + 1, :] = alpha * l_s[h:h + 1, :] + jnp.sum(p, axis=0, keepdims=True)
            acc_s[rs, :] = alpha * acc_s[rs, :] + _dot(vT_ref[rs, :], p.astype(BF16), NN)
            m_s[h:h + 1, :] = m_new

        @pl.when(kj == qi)
        def _():
            for h in range(ATT_HS):
                rs = slice(HEAD_DIM * h, HEAD_DIM * (h + 1))
                oT_ref[rs, :] = (acc_s[rs, :] / l_s[h:h + 1, :]).astype(BF16)
            lse_ref[...] = m_s[...] + jnp.log(l_s[...])

    tok = (ATT_B, ATT_W)
    feat = (ATT_W, ATT_B)
    return pl.pallas_call(
        body,
        out_shape=(jax.ShapeDtypeStruct((1024, T), BF16), jax.ShapeDtypeStruct((ATT_HEADS // ATT_HS, ATT_HS, T), F32)),
        grid=(B, ATT_HEADS // ATT_HS, nq // 2, nq + 1),
        in_specs=[pl.BlockSpec(tok, mp["k_tok"]), pl.BlockSpec(feat, mp["q_feat"]), pl.BlockSpec(feat, mp["k_feat"]),
                  pl.BlockSpec((None, ATT_B, ATT_B), mp["bias"])],
        out_specs=(pl.BlockSpec(feat, mp["q_feat"]), pl.BlockSpec((None, ATT_HS, ATT_B), mp["lse"])),
        scratch_shapes=[pltpu.VMEM((ATT_HS, ATT_B), F32), pltpu.VMEM((ATT_HS, ATT_B), F32),
                        pltpu.VMEM((ATT_W, ATT_B), F32), pltpu.VMEM((ATT_HS, ATT_B, ATT_B), F32)],
        name=name, compiler_params=_cp("parallel", "parallel", "arbitrary", "arbitrary"))(kn, qT, vT, bias)


def _att_scores(k_ref, qT_ref, v_ref, doT_ref, s_s, dp_s):
    for h in range(ATT_HS):
        rs = slice(HEAD_DIM * h, HEAD_DIM * (h + 1))
        s_s[h] = _dot(k_ref[:, rs], qT_ref[rs, :], NN)
        dp_s[h] = _dot(v_ref[:, rs], doT_ref[rs, :].astype(BF16), NN)


def _att_p_ds(s_s, dp_s, doT_ref, oT_ref, lse_ref, bv, h):
    rs = slice(HEAD_DIM * h, HEAD_DIM * (h + 1))
    delta = jnp.sum(doT_ref[rs, :] * oT_ref[rs, :].astype(F32), axis=0, keepdims=True)
    p = jnp.exp(s_s[h] + bv - lse_ref[h:h + 1, :])
    return p, p * (dp_s[h] - delta)


def _att_bwd_dq(name, kn, qT, vb, knT, bias, doT, oT, lse, B):
    T = kn.shape[0]
    nq = (T // B) // ATT_B
    qk = _row_pair(nq)
    mp = _att_maps(nq, qk)

    def body(k_ref, qT_ref, v_ref, kT_ref, bias_ref, doT_ref, oT_ref, lse_ref, dqT_ref, acc_s, s_s, dp_s):
        qi, kj = qk(pl.program_id(2), pl.program_id(3))

        @pl.when(kj == 0)
        def _():
            acc_s[...] = jnp.zeros_like(acc_s)

        bv = bias_ref[...]
        _att_scores(k_ref, qT_ref, v_ref, doT_ref, s_s, dp_s)
        for h in range(ATT_HS):
            rs = slice(HEAD_DIM * h, HEAD_DIM * (h + 1))
            p, ds = _att_p_ds(s_s, dp_s, doT_ref, oT_ref, lse_ref, bv, h)
            acc_s[rs, :] += _dot(kT_ref[rs, :], ds.astype(BF16), NN)

        @pl.when(kj == qi)
        def _():
            dqT_ref[...] = acc_s[...] * ATT_SCALE

    tok = (ATT_B, ATT_W)
    feat = (ATT_W, ATT_B)
    return pl.pallas_call(
        body, out_shape=jax.ShapeDtypeStruct((1024, T), F32), grid=(B, ATT_HEADS // ATT_HS, nq // 2, nq + 1),
        in_specs=[pl.BlockSpec(tok, mp["k_tok"]), pl.BlockSpec(feat, mp["q_feat"]), pl.BlockSpec(tok, mp["k_tok"]),
                  pl.BlockSpec(feat, mp["k_feat"]), pl.BlockSpec((None, ATT_B, ATT_B), mp["bias"]),
                  pl.BlockSpec(feat, mp["q_feat"]), pl.BlockSpec(feat, mp["q_feat"]),
                  pl.BlockSpec((None, ATT_HS, ATT_B), mp["lse"])],
        out_specs=pl.BlockSpec(feat, mp["q_feat"]),
        scratch_shapes=[pltpu.VMEM((ATT_W, ATT_B), F32), pltpu.VMEM((ATT_HS, ATT_B, ATT_B), F32),
                        pltpu.VMEM((ATT_HS, ATT_B, ATT_B), F32)],
        name=name, compiler_params=_cp("parallel", "parallel", "arbitrary", "arbitrary"))(
            kn, qT, vb, knT, bias, doT, oT, lse)


def _att_bwd_dkv(name, kn, qT, vb, qn, bias, doT, oT, lse, dyn, B):
    T = kn.shape[0]
    nq = (T // B) // ATT_B
    qk = _col_pair(nq)
    mp = _att_maps(nq, qk)

    def body(k_ref, qT_ref, v_ref, q_ref, bias_ref, doT_ref, oT_ref, lse_ref, do_ref, dk_ref, dv_ref, dk_s, dv_s,
             s_s, dp_s):
        qi, kj = qk(pl.program_id(2), pl.program_id(3))

        @pl.when(qi == kj)
        def _():
            dk_s[...] = jnp.zeros_like(dk_s)
            dv_s[...] = jnp.zeros_like(dv_s)

        bv = bias_ref[...]
        _att_scores(k_ref, qT_ref, v_ref, doT_ref, s_s, dp_s)
        for h in range(ATT_HS):
            rs = slice(HEAD_DIM * h, HEAD_DIM * (h + 1))
            p, ds = _att_p_ds(s_s, dp_s, doT_ref, oT_ref, lse_ref, bv, h)
            dv_s[h] += _dot(p.astype(BF16), do_ref[:, rs].astype(BF16), NN)
            dk_s[h] += _dot(ds.astype(BF16), q_ref[:, rs], NN)

        @pl.when(qi == nq - 1)
        def _():
            for h in range(ATT_HS):
                rs = slice(HEAD_DIM * h, HEAD_DIM * (h + 1))
                dk_ref[:, rs] = dk_s[h] * ATT_SCALE
                dv_ref[:, rs] = dv_s[h]

    tok = (ATT_B, ATT_W)
    feat = (ATT_W, ATT_B)
    osh = jax.ShapeDtypeStruct((T, 1024), F32)
    return pl.pallas_call(
        body, out_shape=(osh, osh), grid=(B, ATT_HEADS // ATT_HS, nq // 2, nq + 1),
        in_specs=[pl.BlockSpec(tok, mp["k_tok"]), pl.BlockSpec(feat, mp["q_feat"]), pl.BlockSpec(tok, mp["k_tok"]),
                  pl.BlockSpec(tok, mp["q_tok"]), pl.BlockSpec((None, ATT_B, ATT_B), mp["bias"]),
                  pl.BlockSpec(feat, mp["q_feat"]), pl.BlockSpec(feat, mp["q_feat"]),
                  pl.BlockSpec((None, ATT_HS, ATT_B), mp["lse"]), pl.BlockSpec(tok, mp["do_tok"])],
        out_specs=(pl.BlockSpec(tok, mp["k_tok"]), pl.BlockSpec(tok, mp["k_tok"])),
        scratch_shapes=[pltpu.VMEM((ATT_HS, ATT_B, HEAD_DIM), F32), pltpu.VMEM((ATT_HS, ATT_B, HEAD_DIM), F32),
                        pltpu.VMEM((ATT_HS, ATT_B, ATT_B), F32), pltpu.VMEM((ATT_HS, ATT_B, ATT_B), F32)],
        name=name, compiler_params=_cp("parallel", "parallel", "arbitrary", "arbitrary"))(
            kn, qT, vb, qn, bias, doT, oT, lse, dyn)


def _group_cols(v):
    return v.reshape(SSD_GROUPS, 4)


def _ssd_params(p):
    rows = jnp.stack([_group_cols(p["dt_bias"]), _group_cols(p["a_log"]), _group_cols(p["d_skip"])], axis=1)
    return rows, jnp.swapaxes(rows, 1, 2)


def _layer_fwd(l, x, p, W, bias, B):
    T = x.shape[0]
    S = T // B
    nt = T // ROW_T
    tag = "l%d" % l
    x1, ffn1 = _ffn_fwd(tag + "f1", x, p["ffn1_norm"][None], W["g1"], W["u1"], W["d1"], l)
    h2 = _rms_fwd(tag + "_mixrms", x1, p["mix_norm"][None])
    win = W["win"][l]
    proj = _mm(tag + "_proj",
               [(h2, pl.BlockSpec((ROW_T, D_MODEL), lambda j, i, k: (i, 0)),
                 win, pl.BlockSpec((D_MODEL, PROJ_TN), lambda j, i, k: (0, j)))],
               jax.ShapeDtypeStruct((T, IN_PAD), F32), pl.BlockSpec((ROW_T, PROJ_TN), lambda j, i, k: (i, j)),
               (IN_PAD // PROJ_TN, nt, 1), NN, (ROW_T, PROJ_TN))
    xbc = proj[:, COL_XBC:COL_Q].reshape(B, S, CONV_DIM)
    xpad = jnp.pad(xbc, ((0, 0), (PAD_R, PAD_R), (0, 0)))
    cw, cbias = p["conv_w"], p["conv_b"][None]
    xc = _conv_fwd(tag + "_conv", xpad, cw, cbias).reshape(T, CONV_DIM)
    dtraw = proj[:, COL_DT:COL_DT + SSD_HEADS].reshape(T, SSD_GROUPS, 4)
    dtc = jnp.transpose(dtraw, (1, 0, 2))
    dtr = jnp.transpose(dtraw, (1, 2, 0))
    pcol, prow = _ssd_params(p)
    Y, y_ssd, hs = _ssd_fwd(tag + "_ssd", xc, proj, dtc, dtr, pcol, prow, p["ssd_norm"][None], B)
    qn = _headnorm_fwd(tag + "_qn", proj, COL_Q // 1024, p["q_norm"][None])
    kn = _headnorm_fwd(tag + "_kn", proj, COL_K // 1024, p["k_norm"][None])
    qT = (qn * ATT_SCALE).T
    vb = proj[:, COL_V:COL_V + 1024].astype(BF16)
    oT, lse = _att_fwd(tag + "_att", kn, qT, vb.T, bias, B)
    ymix = jnp.concatenate([y_ssd, oT.T], axis=1)
    x2 = _mm(tag + "_out",
             [(ymix, pl.BlockSpec((ROW_T, MIX_SH), lambda i, n, k: (i, k)),
               W["wout"], pl.BlockSpec((None, None, MIX_SH, D_MODEL), lambda i, n, k: (k, l, 0, 0)))],
             jax.ShapeDtypeStruct((T, D_MODEL), F32), pl.BlockSpec((ROW_T, D_MODEL), lambda i, n, k: (i, 0)),
             (nt, 1, N_SHARD), NN, (ROW_T, D_MODEL),
             res=(x1, pl.BlockSpec((ROW_T, D_MODEL), lambda i, n, k: (i, 0))))
    x3, ffn2 = _ffn_fwd(tag + "f2", x2, p["ffn2_norm"][None], W["g2"], W["u2"], W["d2"], l)
    saved = dict(ffn1=ffn1, x1=x1, h2=h2, proj=proj, xpad=xpad, xc=xc, dtc=dtc, dtr=dtr, Y=Y, hs=hs,
                 qn=qn, kn=kn, qT=qT, vb=vb, oT=oT, lse=lse, ymix=ymix, ffn2=ffn2)
    return x3, saved


def _layer_bwd(l, dx3, sv, p, W, bias, B, gbuf):
    T = dx3.shape[0]
    S = T // B
    nt = T // ROW_T
    tag = "l%db" % l
    sg = {}
    dx2, sg["ffn2_norm"], (gg2, gu2, gd2) = _ffn_bwd(tag + "f2", dx3, sv["ffn2"], p["ffn2_norm"][None],
                                                    W["g2"], W["u2"], W["d2"], l, (gbuf["g2"], gbuf["u2"], gbuf["d2"]))
    dymix = _mm(tag + "_dymix",
                [(dx2, pl.BlockSpec((ROW_T, D_MODEL), lambda n, i, k: (i, 0)),
                  W["wout"], pl.BlockSpec((None, None, MIX_SH, D_MODEL), lambda n, i, k: (n, l, 0, 0)))],
                jax.ShapeDtypeStruct((T, MIX_W), F32), pl.BlockSpec((ROW_T, MIX_SH), lambda n, i, k: (i, n)),
                (N_SHARD, nt, 1), NT, (ROW_T, MIX_SH))
    gwout = _mm(tag + "_dwout",
                [(sv["ymix"], pl.BlockSpec((ROW_T, MIX_SH), lambda m, n, k: (k, m)),
                  dx2, pl.BlockSpec((ROW_T, D_MODEL), lambda m, n, k: (k, 0)))],
                jax.ShapeDtypeStruct((N_SHARD, DEPTH, MIX_SH, D_MODEL), BF16),
                pl.BlockSpec((None, None, MIX_SH, D_MODEL), lambda m, n, k: (m, l, 0, 0)),
                (N_SHARD, 1, nt), TN, (MIX_SH, D_MODEL), prev=gbuf["wout"])
    proj = sv["proj"]
    doT = dymix[:, 1024:].T
    dqn = _att_bwd_dq(tag + "_attdq", sv["kn"], sv["qT"], sv["vb"], sv["kn"].T, bias, doT, sv["oT"], sv["lse"], B).T
    dkn, dv = _att_bwd_dkv(tag + "_attdkv", sv["kn"], sv["qT"], sv["vb"], sv["qn"], bias, doT, sv["oT"], sv["lse"],
                           dymix, B)
    dq, sg["q_norm"] = _headnorm_bwd(tag + "_qnb", dqn, proj, COL_Q // 1024, p["q_norm"][None])
    dk, sg["k_norm"] = _headnorm_bwd(tag + "_knb", dkn, proj, COL_K // 1024, p["k_norm"][None])
    pcol, prow = _ssd_params(p)
    dxs, dB, dC, dz, ddt, dpar, dnw = _ssd_bwd(tag + "_ssdb", dymix, sv["Y"], sv["xc"], proj, sv["dtc"], sv["dtr"],
                                               pcol, prow, p["ssd_norm"][None], sv["hs"], B)
    dpar = jnp.sum(dpar, axis=0)
    sg["dt_bias"] = dpar[:, 0, :].reshape(SSD_HEADS)
    sg["a_log"] = dpar[:, 1, :].reshape(SSD_HEADS)
    sg["d_skip"] = dpar[:, 2, :].reshape(SSD_HEADS)
    sg["ssd_norm"] = jnp.sum(dnw, axis=0)
    dxc = jnp.concatenate([dxs, dB, dC], axis=1).reshape(B, S, CONV_DIM)
    dxc_pad = jnp.pad(dxc, ((0, 0), (0, PAD_R), (0, 0)))
    dxbc, sg["conv_w"], sg["conv_b"] = _conv_bwd(tag + "_convb", sv["xpad"], dxc_pad, p["conv_w"], p["conv_b"][None])
    ddt16 = jnp.transpose(ddt, (1, 0, 2)).reshape(T, SSD_HEADS)
    dproj = jnp.concatenate([dz, dxbc.reshape(T, CONV_DIM), dq, dk, dv, ddt16,
                             jnp.zeros((T, IN_PAD - COL_DT - SSD_HEADS), F32)], axis=1).astype(BF16)
    win = W["win"][l]
    gwin = _mm(tag + "_dwin",
               [(sv["h2"], pl.BlockSpec((ROW_T, D_MODEL), lambda n, m, k: (k, 0)),
                 dproj, pl.BlockSpec((ROW_T, PROJ_TN), lambda n, m, k: (k, n)))],
               jax.ShapeDtypeStruct((D_MODEL, IN_PAD), BF16), pl.BlockSpec((D_MODEL, PROJ_TN), lambda n, m, k: (0, n)),
               (IN_PAD // PROJ_TN, 1, nt), TN, (D_MODEL, PROJ_TN))
    dh2 = _mm(tag + "_dh2",
              [(dproj, pl.BlockSpec((ROW_T, PROJ_TN), lambda i, n, k: (i, k)),
                win, pl.BlockSpec((D_MODEL, PROJ_TN), lambda i, n, k: (0, k)))],
              jax.ShapeDtypeStruct((T, D_MODEL), F32), pl.BlockSpec((ROW_T, D_MODEL), lambda i, n, k: (i, 0)),
              (nt, 1, IN_PAD // PROJ_TN), NT, (ROW_T, D_MODEL))
    dx1, sg["mix_norm"] = _rms_bwd(tag + "_mixrmsb", dh2, sv["x1"], p["mix_norm"][None], dx2)
    dx0, sg["ffn1_norm"], (gg1, gu1, gd1) = _ffn_bwd(tag + "f1", dx1, sv["ffn1"], p["ffn1_norm"][None],
                                                    W["g1"], W["u1"], W["d1"], l, (gbuf["g1"], gbuf["u1"], gbuf["d1"]))
    gbuf = dict(g1=gg1, u1=gu1, d1=gd1, g2=gg2, u2=gu2, d2=gd2, wout=gwout)
    return dx0, sg, gbuf, gwin


def _win_pack(w):
    return jnp.concatenate([w[:, :3072], w[:, 3088:], w[:, 3072:3088],
                            jnp.zeros((w.shape[0], IN_PAD - IN_PROJ), w.dtype)], axis=1)


def _win_unpack(g):
    return jnp.concatenate([g[:, :3072], g[:, COL_DT:COL_DT + SSD_HEADS], g[:, 3072:COL_DT]], axis=1)


def _local_step(x, target, small, W, B):
    T = x.shape[0]
    nq = (T // B) // ATT_B
    bias = _att_bias(nq)
    saved = []
    h = x
    for l in range(DEPTH):
        p = {k: v[l] for k, v in small.items()}
        h, sv = _layer_fwd(l, h, p, W, bias, B)
        saved.append(sv)
    dy, lsum = _loss_grad("loss", h, target)
    gbuf = dict(g1=None, u1=None, d1=None, g2=None, u2=None, d2=None, wout=None)
    sgrads = [None] * DEPTH
    gwin = [None] * DEPTH
    d = dy
    for l in reversed(range(DEPTH)):
        p = {k: v[l] for k, v in small.items()}
        d, sgrads[l], gbuf, gwin[l] = _layer_bwd(l, d, saved[l], p, W, bias, B, gbuf)
    return lsum, d, sgrads, gbuf, gwin


MESH = pl.DeviceIdType.MESH
ANY = pl.BlockSpec(memory_space=pl.ANY)


def _place():
    return lax.axis_index("x"), lax.axis_index("y"), lax.axis_index("c")


def _other_chips(x, y):
    return [(1 - x, y), (x, 1 - y), (1 - x, 1 - y)]


def _gather_big(own):
    n = len(own)

    def body(*refs):
        src, dst = refs[:n], refs[n:2 * n]
        send, recv, loc = refs[2 * n:]
        x, y, c = _place()
        me = 2 * x + y
        local = [pltpu.make_async_copy(src[a], dst[a].at[me], loc.at[a]) for a in range(n)]
        for cp in local:
            cp.start()
        chips = _other_chips(x, y)
        sends = []
        for k, (px, py) in enumerate(chips):
            for a in range(n):
                cp = pltpu.make_async_remote_copy(src_ref=src[a], dst_ref=dst[a].at[me], send_sem=send.at[k * n + a],
                                                  recv_sem=recv.at[k * n + a], device_id=(px, py, c), device_id_type=MESH)
                cp.start()
                sends.append(cp)
        for k, (px, py) in enumerate(chips):
            for a in range(n):
                pltpu.make_async_remote_copy(src_ref=src[a], dst_ref=dst[a].at[2 * px + py], send_sem=send.at[k * n + a],
                                             recv_sem=recv.at[k * n + a], device_id=(px, py, c),
                                             device_id_type=MESH).wait_recv()
        for cp in sends:
            cp.wait_send()
        for cp in local:
            cp.wait()

    return pl.pallas_call(
        body, out_shape=[jax.ShapeDtypeStruct((N_SHARD,) + o.shape, o.dtype) for o in own],
        in_specs=[ANY] * n, out_specs=[ANY] * n,
        scratch_shapes=[pltpu.SemaphoreType.DMA((3 * n,)), pltpu.SemaphoreType.DMA((3 * n,)), pltpu.SemaphoreType.DMA((n,))],
        name="gather_weights")(*own)


def _scatter_grads(grads):
    n = len(grads)

    def body(*refs):
        src, dst = refs[:n], refs[n:2 * n]
        send, recv = refs[2 * n:]
        x, y, c = _place()
        chips = _other_chips(x, y)
        sends = []
        for k, (px, py) in enumerate(chips):
            for a in range(n):
                cp = pltpu.make_async_remote_copy(src_ref=src[a].at[2 * px + py], dst_ref=dst[a].at[k],
                                                  send_sem=send.at[k * n + a], recv_sem=recv.at[k * n + a],
                                                  device_id=(px, py, c), device_id_type=MESH)
                cp.start()
                sends.append(cp)
        for k, (px, py) in enumerate(chips):
            for a in range(n):
                pltpu.make_async_remote_copy(src_ref=src[a].at[2 * px + py], dst_ref=dst[a].at[k],
                                             send_sem=send.at[k * n + a], recv_sem=recv.at[k * n + a],
                                             device_id=(px, py, c), device_id_type=MESH).wait_recv()
        for cp in sends:
            cp.wait_send()

    return pl.pallas_call(
        body, out_shape=[jax.ShapeDtypeStruct((3,) + g.shape[1:], g.dtype) for g in grads],
        in_specs=[ANY] * n, out_specs=[ANY] * n,
        scratch_shapes=[pltpu.SemaphoreType.DMA((3 * n,)), pltpu.SemaphoreType.DMA((3 * n,))],
        name="scatter_grads")(*grads)


def _swap_sibling(parts):
    n = len(parts)

    def body(*refs):
        src, dst = refs[:n], refs[n:2 * n]
        send, recv = refs[2 * n:]
        x, y, c = _place()
        cps = [pltpu.make_async_remote_copy(src_ref=src[a], dst_ref=dst[a], send_sem=send.at[a], recv_sem=recv.at[a],
                                            device_id=(x, y, 1 - c), device_id_type=MESH) for a in range(n)]
        for cp in cps:
            cp.start()
        for cp in cps:
            cp.wait_recv()
        for cp in cps:
            cp.wait_send()

    return pl.pallas_call(
        body, out_shape=[jax.ShapeDtypeStruct(p.shape, p.dtype) for p in parts],
        in_specs=[ANY] * n, out_specs=[ANY] * n,
        scratch_shapes=[pltpu.SemaphoreType.DMA((n,)), pltpu.SemaphoreType.DMA((n,))],
        name="swap_sibling")(*parts)


def _allreduce_small(name, v):
    R = v.shape[0]

    def body(v_ref, o_ref, buf, send, recv):
        x, y, c = _place()
        me = 4 * x + 2 * y + c
        buf[me] = v_ref[...]
        cps = []
        for k in range(1, 8):
            fx, fy, fc = (k >> 2) & 1, (k >> 1) & 1, k & 1
            px = 1 - x if fx else x
            py = 1 - y if fy else y
            pc = 1 - c if fc else c
            cp = pltpu.make_async_remote_copy(src_ref=v_ref, dst_ref=buf.at[me], send_sem=send.at[k - 1],
                                              recv_sem=recv.at[k - 1], device_id=(px, py, pc), device_id_type=MESH)
            cp.start()
            cps.append((cp, 4 * px + 2 * py + pc))
        for k, (cp, peer) in enumerate(cps):
            pltpu.make_async_remote_copy(src_ref=v_ref, dst_ref=buf.at[peer], send_sem=send.at[k], recv_sem=recv.at[k],
                                         device_id=(x, y, c), device_id_type=MESH).wait_recv()
        for cp, _ in cps:
            cp.wait_send()
        acc = buf[0]
        for d in range(1, 8):
            acc = acc + buf[d]
        o_ref[...] = acc

    return pl.pallas_call(
        body, out_shape=jax.ShapeDtypeStruct((R, 128), F32),
        in_specs=[pl.BlockSpec(memory_space=pltpu.VMEM)], out_specs=pl.BlockSpec(memory_space=pltpu.VMEM),
        scratch_shapes=[pltpu.VMEM((8, R, 128), F32), pltpu.SemaphoreType.DMA((7,)), pltpu.SemaphoreType.DMA((7,))],
        name=name)(v)


def _row_tile(r):
    for t in (256, 128, 64, 32, 16, 8):
        if r % t == 0:
            return t
    raise ValueError(r)


def _sum4(name, own, got):
    R, C = own.shape
    tr = _row_tile(R)

    def body(o_ref, g_ref, s_ref):
        s = o_ref[...].astype(F32)
        for k in range(3):
            s = s + g_ref[k].astype(F32)
        s_ref[...] = s

    return pl.pallas_call(
        body, out_shape=jax.ShapeDtypeStruct((R, C), F32), grid=(R // tr,),
        in_specs=[pl.BlockSpec((tr, C), lambda i: (i, 0)), pl.BlockSpec((3, tr, C), lambda i: (0, i, 0))],
        out_specs=pl.BlockSpec((tr, C), lambda i: (i, 0)), name=name, compiler_params=_cp("parallel"))(own, got)


def _adamw(name, w, gparts, m, v):
    R, C = w.shape
    tr = _row_tile(R)
    ng = len(gparts)
    c1 = 1.0 - ADAM_B1 ** ADAM_STEP
    c2 = 1.0 - ADAM_B2 ** ADAM_STEP

    def body(*refs):
        w_ref = refs[0]
        g_refs = refs[1:1 + ng]
        m_ref, v_ref, go_ref, d_ref, mo_ref, vo_ref = refs[1 + ng:]
        g = g_refs[0][...]
        for r in g_refs[1:]:
            g = g + r[...]
        mn = ADAM_B1 * m_ref[...] + (1.0 - ADAM_B1) * g
        vn = ADAM_B2 * v_ref[...] + (1.0 - ADAM_B2) * (g * g)
        go_ref[...] = g
        mo_ref[...] = mn
        vo_ref[...] = vn
        d_ref[...] = -ADAM_LR * ((mn / c1) / (jnp.sqrt(vn / c2) + ADAM_EPS) + ADAM_WD * w_ref[...])

    blk = pl.BlockSpec((tr, C), lambda i: (i, 0))
    osh = jax.ShapeDtypeStruct((R, C), F32)
    return pl.pallas_call(
        body, out_shape=(osh, osh, osh, osh), grid=(R // tr,), in_specs=[blk] * (3 + ng), out_specs=(blk,) * 4,
        name=name, compiler_params=_cp("parallel"))(w, *gparts, m, v)


BIG = [("ffn1_w_gate", "g1"), ("ffn1_w_up", "u1"), ("ffn1_w_down", "d1"), ("w_in", "win"), ("w_out", "wout"),
       ("ffn2_w_gate", "g2"), ("ffn2_w_up", "u2"), ("ffn2_w_down", "d2")]
SMALL = ["ffn1_norm", "mix_norm", "conv_b", "dt_bias", "a_log", "d_skip", "ssd_norm", "q_norm", "k_norm", "ffn2_norm"]
WEIGHTS = ["ffn1_norm", "ffn1_w_gate", "ffn1_w_up", "ffn1_w_down", "mix_norm", "w_in", "conv_w", "conv_b", "dt_bias",
           "a_log", "d_skip", "ssd_norm", "q_norm", "k_norm", "w_out", "ffn2_norm", "ffn2_w_gate", "ffn2_w_up",
           "ffn2_w_down"]
CONV_SH = CONV_DIM // N_SHARD


def _pad128(v):
    v = v.reshape(-1)
    return jnp.pad(v, (0, (-v.shape[0]) % 128))


def _pack(pieces):
    flat, offs, pos = [], [], 0
    for p in pieces:
        q = _pad128(p.astype(F32))
        offs.append(pos)
        pos += q.shape[0] // 128
        flat.append(q)
    total = -(-pos // 8) * 8
    out = jnp.concatenate(flat + [jnp.zeros(((total - pos) * 128,), F32)]).reshape(total, 128)
    return out, offs


def _unpack(packed, offs, shapes):
    out = []
    for off, shp in zip(offs, shapes):
        n = int(np.prod(shp))
        rows = -(-n // 128)
        out.append(packed[off:off + rows].reshape(-1)[:n].reshape(shp))
    return out


def kernel(x, ffn1_norm, ffn1_w_gate, ffn1_w_up, ffn1_w_down, mix_norm, w_in, conv_w, conv_b, dt_bias, a_log, d_skip, ssd_norm, q_norm, k_norm, w_out, ffn2_norm, ffn2_w_gate, ffn2_w_up, ffn2_w_down, loss_target, m_ffn1_norm, m_ffn1_w_gate, m_ffn1_w_up, m_ffn1_w_down, m_mix_norm, m_w_in, m_conv_w, m_conv_b, m_dt_bias, m_a_log, m_d_skip, m_ssd_norm, m_q_norm, m_k_norm, m_w_out, m_ffn2_norm, m_ffn2_w_gate, m_ffn2_w_up, m_ffn2_w_down, v_ffn1_norm, v_ffn1_w_gate, v_ffn1_w_up, v_ffn1_w_down, v_mix_norm, v_w_in, v_conv_w, v_conv_b, v_dt_bias, v_a_log, v_d_skip, v_ssd_norm, v_q_norm, v_k_norm, v_w_out, v_ffn2_norm, v_ffn2_w_gate, v_ffn2_w_up, v_ffn2_w_down):
    A = dict(locals())
    ix, iy, ic = _place()
    me = 2 * ix + iy
    B, S, _ = x.shape
    T = B * S

    own = [A[name].astype(BF16) for name, _ in BIG]
    gathered = _gather_big(own)
    W = {key: g for (_, key), g in zip(BIG, gathered)}
    W["win"] = [_win_pack(jnp.concatenate([W["win"][j, l] for j in range(N_SHARD)], axis=1)) for l in range(DEPTH)]
    placed = lax.dynamic_update_slice(jnp.zeros((DEPTH, CONV_K, CONV_DIM), F32),
                                      conv_w * (ic == 0).astype(F32), (0, 0, me * CONV_SH))
    conv_full = _allreduce_small("gather_conv_w", placed.reshape(-1, 128)).reshape(DEPTH, CONV_K, CONV_DIM)

    small = {name: A[name] for name in SMALL}
    small["conv_w"] = conv_full
    lsum, dx, sgrads, gbuf, gwin = _local_step(x.reshape(T, D_MODEL), loss_target.reshape(T, D_MODEL), small, W, B)

    names = SMALL + ["conv_w"]
    pieces = [jnp.stack([sgrads[l][n].reshape(small[n].shape[1:]) for l in range(DEPTH)]) for n in names]
    pieces.append(0.5 / D_MODEL * jnp.sum(lsum))
    packed, offs = _pack(pieces)
    red = _allreduce_small("allreduce_small", packed)
    shapes = [small[n].shape for n in names] + [()]
    red = _unpack(red, offs, shapes)
    loss = red[-1]
    sg = dict(zip(names, red[:-1]))

    gwin_st = jnp.stack([jnp.transpose(_win_unpack(gwin[l]).reshape(D_MODEL, N_SHARD, IN_SH), (1, 0, 2))
                         for l in range(DEPTH)], axis=1)
    gbuf = dict(gbuf, win=gwin_st)
    glist = [gbuf[key] for _, key in BIG]
    got = _scatter_grads(glist)
    sums = []
    for (name, key), g, r in zip(BIG, glist, got):
        _, _, R, C = g.shape
        mine = lax.dynamic_index_in_dim(g, me, axis=0, keepdims=False).reshape(DEPTH * R, C)
        sums.append(_sum4("sum_" + key, mine, r.reshape(3, DEPTH * R, C)))
    theirs = _swap_sibling(sums)

    out = {}
    for (name, key), s, t in zip(BIG, sums, theirs):
        shp = A[name].shape
        flat = lambda a: a.reshape(shp[0] * shp[1], shp[2])
        res = _adamw("adamw_" + key, flat(A[name]), [s, t], flat(A["m_" + name]), flat(A["v_" + name]))
        out[name] = [r.reshape(shp) for r in res]

    wp, offs = _pack([A[n] for n in SMALL])
    gp, _ = _pack([sg[n] for n in SMALL])
    mp, _ = _pack([A["m_" + n] for n in SMALL])
    vp, _ = _pack([A["v_" + n] for n in SMALL])
    res = _adamw("adamw_small", wp, [gp], mp, vp)
    shapes = [A[n].shape for n in SMALL]
    res = [_unpack(r, offs, shapes) for r in res]
    for i, n in enumerate(SMALL):
        out[n] = [res[q][i] for q in range(4)]
    gcw = lax.dynamic_slice_in_dim(sg["conv_w"], me * CONV_SH, CONV_SH, axis=2)
    flat = lambda a: a.reshape(DEPTH * CONV_K, CONV_SH)
    res = _adamw("adamw_conv_w", flat(conv_w), [flat(gcw)], flat(m_conv_w), flat(v_conv_w))
    out["conv_w"] = [r.reshape(conv_w.shape) for r in res]

    outs = [loss, dx.reshape(B, S, D_MODEL)]
    for q in range(4):
        outs += [out[n][q] for n in WEIGHTS]
    return tuple(outs)
```

```python
import functools
import math

import numpy as np
import jax
import jax.numpy as jnp
from jax import lax
from jax.experimental import pallas as pl
from jax.experimental.pallas import tpu as pltpu

F32 = jnp.float32
BF16 = jnp.bfloat16

D_MODEL = 1024
DEPTH = 2
N_SHARD = 4
D_FF = 2816
FF_SH = D_FF // N_SHARD
SSD_HEADS = 16
HEAD_DIM = 64
SSD_GROUPS = 4
GROUP_W = 256
SSD_STATE = 128
CONV_K = 4
CONV_DIM = 2048
ATT_HEADS = 16
MIX_W = 2048
MIX_SH = MIX_W // N_SHARD
IN_PROJ = 6160
IN_SH = IN_PROJ // N_SHARD
IN_PAD = 6272
PROJ_TN = 896
COL_Z, COL_XBC, COL_Q, COL_K, COL_V, COL_DT = 0, 1024, 3072, 4096, 5120, 6144
EPS = 1e-6
NEG = -1e30
SSD_L = 256
ATT_B = 256
ROW_T = 512
CONV_CT = 256
CONV_R = 256
PAD_R = 8

ADAM_LR, ADAM_B1, ADAM_B2, ADAM_EPS, ADAM_WD, ADAM_STEP = 0.001, 0.9, 0.999, 1e-08, 0.01, 10

NN = (((1,), (0,)), ((), ()))
NT = (((1,), (1,)), ((), ()))
TN = (((0,), (0,)), ((), ()))

VMEM_LIMIT = 56 * 1024 * 1024


def _cp(*sem):
    return pltpu.CompilerParams(dimension_semantics=sem, vmem_limit_bytes=VMEM_LIMIT)


def _dot(a, b, dims):
    return lax.dot_general(a, b, dims, preferred_element_type=F32)


def _sigmoid(x):
    return 1.0 / (1.0 + jnp.exp(-x))


def _softplus(x):
    return jnp.maximum(x, 0.0) + jnp.log(1.0 + jnp.exp(-jnp.abs(x)))


def _mm(name, pairs, out_shape, out_spec, grid, dims, acc_shape, res=None, scale=1.0):
    nk = grid[2]
    npair = len(pairs)

    def body(*refs):
        ab = refs[:2 * npair]
        pos = 2 * npair
        res_ref = None
        if res is not None:
            res_ref = refs[pos]
            pos += 1
        out_ref, acc = refs[pos], refs[pos + 1]
        k = pl.program_id(2)

        @pl.when(k == 0)
        def _():
            acc[...] = jnp.zeros_like(acc)

        s = None
        for p in range(npair):
            d = _dot(ab[2 * p][...].astype(BF16), ab[2 * p + 1][...].astype(BF16), dims)
            s = d if s is None else s + d
        acc[...] += s

        @pl.when(k == nk - 1)
        def _():
            r = acc[...]
            if scale != 1.0:
                r = r * scale
            if res_ref is not None:
                r = r + res_ref[...]
            out_ref[...] = r.astype(out_ref.dtype)

    args, specs = [], []
    for a, a_spec, b, b_spec in pairs:
        args += [a, b]
        specs += [a_spec, b_spec]
    if res is not None:
        args.append(res[0])
        specs.append(res[1])
    return pl.pallas_call(
        body, out_shape=out_shape, grid=grid, in_specs=specs, out_specs=out_spec,
        scratch_shapes=[pltpu.VMEM(acc_shape, F32)], name=name,
        compiler_params=_cp("parallel", "parallel", "arbitrary"))(*args)


def _rms_fwd(name, x, w):
    T = x.shape[0]

    def body(x_ref, w_ref, o_ref):
        xv = x_ref[...]
        r = lax.rsqrt(jnp.mean(xv * xv, axis=-1, keepdims=True) + EPS)
        o_ref[...] = (xv * r * w_ref[...]).astype(BF16)

    return pl.pallas_call(
        body, out_shape=jax.ShapeDtypeStruct((T, D_MODEL), BF16), grid=(T // ROW_T,),
        in_specs=[pl.BlockSpec((ROW_T, D_MODEL), lambda i: (i, 0)), pl.BlockSpec((1, D_MODEL), lambda i: (0, 0))],
        out_specs=pl.BlockSpec((ROW_T, D_MODEL), lambda i: (i, 0)), name=name, compiler_params=_cp("parallel"))(x, w)


def _rms_bwd(name, dh, x, w, dres):
    T = x.shape[0]

    def body(dh_ref, x_ref, w_ref, dres_ref, dx_ref, dw_ref):
        @pl.when(pl.program_id(0) == 0)
        def _():
            dw_ref[...] = jnp.zeros_like(dw_ref)

        xv = x_ref[...]
        r = lax.rsqrt(jnp.mean(xv * xv, axis=-1, keepdims=True) + EPS)
        xhat = xv * r
        dhv = dh_ref[...]
        dxhat = dhv * w_ref[...]
        m = jnp.mean(dxhat * xhat, axis=-1, keepdims=True)
        dx_ref[...] = dres_ref[...] + r * (dxhat - xhat * m)
        dw_ref[...] += jnp.sum(dhv * xhat, axis=0, keepdims=True)

    row = pl.BlockSpec((ROW_T, D_MODEL), lambda i: (i, 0))
    vec = pl.BlockSpec((1, D_MODEL), lambda i: (0, 0))
    return pl.pallas_call(
        body, out_shape=(jax.ShapeDtypeStruct((T, D_MODEL), F32), jax.ShapeDtypeStruct((1, D_MODEL), F32)),
        grid=(T // ROW_T,), in_specs=[row, row, vec, row], out_specs=(row, vec), name=name,
        compiler_params=_cp("arbitrary"))(dh, x, w, dres)


def _loss_grad(name, y, t):
    T = y.shape[0]

    def body(y_ref, t_ref, dy_ref, l_ref):
        @pl.when(pl.program_id(0) == 0)
        def _():
            l_ref[...] = jnp.zeros_like(l_ref)

        e = y_ref[...] - t_ref[...]
        dy_ref[...] = e * (1.0 / D_MODEL)
        l_ref[...] += jnp.sum(e * e, axis=0, keepdims=True)

    row = pl.BlockSpec((ROW_T, D_MODEL), lambda i: (i, 0))
    vec = pl.BlockSpec((1, D_MODEL), lambda i: (0, 0))
    return pl.pallas_call(
        body, out_shape=(jax.ShapeDtypeStruct((T, D_MODEL), F32), jax.ShapeDtypeStruct((1, D_MODEL), F32)),
        grid=(T // ROW_T,), in_specs=[row, row], out_specs=(row, vec), name=name,
        compiler_params=_cp("arbitrary"))(y, t)


def _ffn_gate_up(name, h, wg, wu):
    T = h.shape[0]

    def body(h_ref, wg_ref, wu_ref, g_ref, u_ref, a_ref):
        hv = h_ref[...]
        g = _dot(hv, wg_ref[...], NN)
        u = _dot(hv, wu_ref[...], NN)
        g_ref[...] = g.astype(BF16)
        u_ref[...] = u.astype(BF16)
        a_ref[...] = (g * _sigmoid(g) * u).astype(BF16)

    wspec = pl.BlockSpec((None, D_MODEL, FF_SH), lambda j, i: (j, 0, 0))
    ospec = pl.BlockSpec((None, ROW_T, FF_SH), lambda j, i: (j, i, 0))
    osh = jax.ShapeDtypeStruct((N_SHARD, T, FF_SH), BF16)
    return pl.pallas_call(
        body, out_shape=(osh, osh, osh), grid=(N_SHARD, T // ROW_T),
        in_specs=[pl.BlockSpec((ROW_T, D_MODEL), lambda j, i: (i, 0)), wspec, wspec],
        out_specs=(ospec, ospec, ospec), name=name, compiler_params=_cp("parallel", "parallel"))(h, wg, wu)


def _ffn_dact(name, dx, wd, g, u):
    T = dx.shape[0]

    def body(dx_ref, wd_ref, g_ref, u_ref, dg_ref, du_ref):
        da = 0.5 * _dot(dx_ref[...].astype(BF16), wd_ref[...], NT)
        gv = g_ref[...].astype(F32)
        uv = u_ref[...].astype(F32)
        sg = _sigmoid(gv)
        dg_ref[...] = (da * uv * (sg * (1.0 + gv * (1.0 - sg)))).astype(BF16)
        du_ref[...] = (da * gv * sg).astype(BF16)

    aspec = pl.BlockSpec((None, ROW_T, FF_SH), lambda j, i: (j, i, 0))
    osh = jax.ShapeDtypeStruct((N_SHARD, T, FF_SH), BF16)
    return pl.pallas_call(
        body, out_shape=(osh, osh), grid=(N_SHARD, T // ROW_T),
        in_specs=[pl.BlockSpec((ROW_T, D_MODEL), lambda j, i: (i, 0)),
                  pl.BlockSpec((None, FF_SH, D_MODEL), lambda j, i: (j, 0, 0)), aspec, aspec],
        out_specs=(aspec, aspec), name=name, compiler_params=_cp("parallel", "parallel"))(dx, wd, g, u)


def _ffn_fwd(tag, x, nw, wg, wu, wd):
    T = x.shape[0]
    h = _rms_fwd(tag + "_rms", x, nw)
    g, u, a = _ffn_gate_up(tag + "_gu", h, wg, wu)
    nt = T // ROW_T
    xo = _mm(tag + "_down",
             [(a, pl.BlockSpec((None, ROW_T, FF_SH), lambda i, n, k: (k, i, 0)),
               wd, pl.BlockSpec((None, FF_SH, D_MODEL), lambda i, n, k: (k, 0, 0)))],
             jax.ShapeDtypeStruct((T, D_MODEL), F32), pl.BlockSpec((ROW_T, D_MODEL), lambda i, n, k: (i, 0)),
             (nt, 1, N_SHARD), NN, (ROW_T, D_MODEL),
             res=(x, pl.BlockSpec((ROW_T, D_MODEL), lambda i, n, k: (i, 0))), scale=0.5)
    return xo, (x, h, g, u, a)


def _ffn_bwd(tag, dxo, saved, nw, wg, wu, wd):
    x, h, g, u, a = saved
    T = x.shape[0]
    nt = T // ROW_T
    dg, du = _ffn_dact(tag + "_dact", dxo, wd, g, u)
    act = lambda f: pl.BlockSpec((None, ROW_T, FF_SH), f)
    gd = _mm(tag + "_dwd",
             [(a, act(lambda m, n, k: (m, k, 0)), dxo, pl.BlockSpec((ROW_T, D_MODEL), lambda m, n, k: (k, 0)))],
             jax.ShapeDtypeStruct((N_SHARD, FF_SH, D_MODEL), BF16),
             pl.BlockSpec((None, FF_SH, D_MODEL), lambda m, n, k: (m, 0, 0)),
             (N_SHARD, 1, nt), TN, (FF_SH, D_MODEL), scale=0.5)
    hspec = pl.BlockSpec((ROW_T, D_MODEL), lambda j, n, k: (k, 0))
    gsh = jax.ShapeDtypeStruct((N_SHARD, D_MODEL, FF_SH), BF16)
    gspec = pl.BlockSpec((None, D_MODEL, FF_SH), lambda j, n, k: (j, 0, 0))
    gg = _mm(tag + "_dwg", [(h, hspec, dg, act(lambda j, n, k: (j, k, 0)))], gsh, gspec,
             (N_SHARD, 1, nt), TN, (D_MODEL, FF_SH))
    gu = _mm(tag + "_dwu", [(h, hspec, du, act(lambda j, n, k: (j, k, 0)))], gsh, gspec,
             (N_SHARD, 1, nt), TN, (D_MODEL, FF_SH))
    wspec = pl.BlockSpec((None, D_MODEL, FF_SH), lambda i, n, k: (k, 0, 0))
    dh = _mm(tag + "_dh",
             [(dg, act(lambda i, n, k: (k, i, 0)), wg, wspec), (du, act(lambda i, n, k: (k, i, 0)), wu, wspec)],
             jax.ShapeDtypeStruct((T, D_MODEL), F32), pl.BlockSpec((ROW_T, D_MODEL), lambda i, n, k: (i, 0)),
             (nt, 1, N_SHARD), NT, (ROW_T, D_MODEL))
    dx, dnw = _rms_bwd(tag + "_rmsb", dh, x, nw, dxo)
    return dx, dnw, (gg, gu, gd)


def _conv_fwd(name, xpad, w, b):
    B, SP, C = xpad.shape
    S = SP - 2 * PAD_R

    def body(x_ref, w_ref, b_ref, o_ref):
        wv = w_ref[...]
        for c in range(S // CONV_R):
            r0 = c * CONV_R
            ch = x_ref[pl.ds(r0, CONV_R + PAD_R), :]
            pre = ch[PAD_R:] * wv[3:4] + b_ref[...]
            for s in range(1, CONV_K):
                pre = pre + pltpu.roll(ch, s, axis=0)[PAD_R:] * wv[3 - s:4 - s]
            o_ref[pl.ds(r0, CONV_R), :] = pre * _sigmoid(pre)

    return pl.pallas_call(
        body, out_shape=jax.ShapeDtypeStruct((B, S, C), F32), grid=(B, C // CONV_CT),
        in_specs=[pl.BlockSpec((None, SP, CONV_CT), lambda bi, ci: (bi, 0, ci)),
                  pl.BlockSpec((CONV_K, CONV_CT), lambda bi, ci: (0, ci)),
                  pl.BlockSpec((1, CONV_CT), lambda bi, ci: (0, ci))],
        out_specs=pl.BlockSpec((None, S, CONV_CT), lambda bi, ci: (bi, 0, ci)), name=name,
        compiler_params=_cp("parallel", "parallel"))(xpad, w, b)


def _conv_bwd(name, xpad, dxc_pad, w, b):
    B, SP, C = xpad.shape
    S = SP - 2 * PAD_R
    RW = CONV_R + PAD_R

    def body(x_ref, d_ref, w_ref, b_ref, dx_ref, dw_ref, db_ref):
        @pl.when(pl.program_id(1) == 0)
        def _():
            dw_ref[...] = jnp.zeros_like(dw_ref)
            db_ref[...] = jnp.zeros_like(db_ref)

        wv = w_ref[...]
        dw = [jnp.zeros((1, CONV_CT), F32) for _ in range(CONV_K)]
        db = jnp.zeros((1, CONV_CT), F32)
        for c in range(S // CONV_R):
            r0 = c * CONV_R
            ch = x_ref[pl.ds(r0, RW + PAD_R), :]
            xs = [ch[PAD_R:]] + [pltpu.roll(ch, s, axis=0)[PAD_R:] for s in range(1, CONV_K)]
            pre = b_ref[...] + xs[0] * wv[3:4]
            for s in range(1, CONV_K):
                pre = pre + xs[s] * wv[3 - s:4 - s]
            sg = _sigmoid(pre)
            dpre = d_ref[pl.ds(r0, RW), :] * (sg * (1.0 + pre * (1.0 - sg)))
            dx = dpre[:CONV_R] * wv[3:4]
            for s in range(1, CONV_K):
                dx = dx + pltpu.roll(dpre, RW - s, axis=0)[:CONV_R] * wv[3 - s:4 - s]
            dx_ref[pl.ds(r0, CONV_R), :] = dx
            dcur = dpre[:CONV_R]
            db = db + jnp.sum(dcur, axis=0, keepdims=True)
            for s in range(CONV_K):
                dw[3 - s] = dw[3 - s] + jnp.sum(dcur * xs[s][:CONV_R], axis=0, keepdims=True)
        db_ref[...] += db
        for k in range(CONV_K):
            dw_ref[k:k + 1, :] += dw[k]

    return pl.pallas_call(
        body,
        out_shape=(jax.ShapeDtypeStruct((B, S, C), F32), jax.ShapeDtypeStruct((CONV_K, C), F32),
                   jax.ShapeDtypeStruct((1, C), F32)),
        grid=(C // CONV_CT, B),
        in_specs=[pl.BlockSpec((None, SP, CONV_CT), lambda ci, bi: (bi, 0, ci)),
                  pl.BlockSpec((None, S + PAD_R, CONV_CT), lambda ci, bi: (bi, 0, ci)),
                  pl.BlockSpec((CONV_K, CONV_CT), lambda ci, bi: (0, ci)),
                  pl.BlockSpec((1, CONV_CT), lambda ci, bi: (0, ci))],
        out_specs=(pl.BlockSpec((None, S, CONV_CT), lambda ci, bi: (bi, 0, ci)),
                   pl.BlockSpec((CONV_K, CONV_CT), lambda ci, bi: (0, ci)),
                   pl.BlockSpec((1, CONV_CT), lambda ci, bi: (0, ci))),
        name=name, compiler_params=_cp("parallel", "arbitrary"))(xpad, dxc_pad, w, b)


def _ssd_common(dtc_ref, dtr_ref, pcol_ref, prow_ref, b_ref, c_ref):
    L = SSD_L
    bias_c, alog_c = pcol_ref[0:1, :], pcol_ref[1:2, :]
    a_c = -jnp.exp(alog_c)
    dt_c = _softplus(dtc_ref[...] + bias_c)
    row = lax.broadcasted_iota(jnp.int32, (L, L), 0)
    col = lax.broadcasted_iota(jnp.int32, (L, L), 1)
    causal = row >= col
    tri = causal.astype(F32)
    hp = lax.Precision.HIGHEST
    cum_c = lax.dot_general(tri, dt_c * a_c, NN, precision=hp, preferred_element_type=F32)
    a_r = -jnp.exp(prow_ref[:, 1:2])
    dt_r = _softplus(dtr_ref[...] + prow_ref[:, 0:1])
    cum_r = lax.dot_general(dt_r * a_r, tri, NT, precision=hp, preferred_element_type=F32)
    bb = b_ref[...].astype(BF16)
    cb = c_ref[...].astype(BF16)
    G = _dot(cb, bb, NT)
    return a_c, dt_c, causal, tri, cum_c, cum_r, bb, cb, G


def _ssd_fwd(name, xc, proj, dtc, dtr, pcol, prow, nw, B):
    T = xc.shape[0]
    S = T // B
    nb = S // SSD_L
    L = SSD_L

    def body(xs_ref, b_ref, c_ref, z_ref, dtc_ref, dtr_ref, pcol_ref, prow_ref, nw_ref, y_ref, yn_ref, hs_ref, H):
        @pl.when(pl.program_id(2) == 0)
        def _():
            H[...] = jnp.zeros_like(H)

        a_c, dt_c, causal, tri, cum_c, cum_r, bb, cb, G = _ssd_common(dtc_ref, dtr_ref, pcol_ref, prow_ref, b_ref, c_ref)
        dsk = pcol_ref[2:3, :]
        clast = cum_c[L - 1:L, :]
        bf = b_ref[...]
        for h in range(4):
            sl = slice(HEAD_DIM * h, HEAD_DIM * (h + 1))
            cc = cum_c[:, h:h + 1]
            lm = jnp.exp(jnp.where(causal, cc - cum_r[h:h + 1, :], NEG))
            M = (G * lm).astype(BF16)
            xh = xs_ref[:, sl]
            Xb = (xh * dt_c[:, h:h + 1]).astype(BF16)
            Hh = H[h]
            y = _dot(M, Xb, NN) + jnp.exp(cc) * _dot(cb, Hh.astype(BF16), NN)
            y_ref[:, sl] = y + dsk[:, h:h + 1] * xh
            hs_ref[h] = Hh
            cl = clast[:, h:h + 1]
            Bw = (bf * jnp.exp(cl - cc)).astype(BF16)
            H[h] = jnp.exp(cl) * Hh + _dot(Bw, Xb, TN)
        zv = z_ref[...]
        y2 = y_ref[...] * (zv * _sigmoid(zv))
        r = lax.rsqrt(jnp.mean(y2 * y2, axis=-1, keepdims=True) + EPS)
        yn_ref[...] = (y2 * r * nw_ref[...]).astype(BF16)

    rowi = lambda b, g, i: b * nb + i
    grp = pl.BlockSpec((L, GROUP_W), lambda b, g, i: (rowi(b, g, i), g))
    return pl.pallas_call(
        body,
        out_shape=(jax.ShapeDtypeStruct((T, 1024), F32), jax.ShapeDtypeStruct((T, 1024), BF16),
                   jax.ShapeDtypeStruct((B, SSD_GROUPS, nb, 4, SSD_STATE, HEAD_DIM), F32)),
        grid=(B, SSD_GROUPS, nb),
        in_specs=[grp,
                  pl.BlockSpec((L, SSD_STATE), lambda b, g, i: (rowi(b, g, i), 8 + g)),
                  pl.BlockSpec((L, SSD_STATE), lambda b, g, i: (rowi(b, g, i), 12 + g)),
                  grp,
                  pl.BlockSpec((None, L, 4), lambda b, g, i: (g, rowi(b, g, i), 0)),
                  pl.BlockSpec((None, 4, L), lambda b, g, i: (g, 0, rowi(b, g, i))),
                  pl.BlockSpec((None, 3, 4), lambda b, g, i: (g, 0, 0)),
                  pl.BlockSpec((None, 4, 3), lambda b, g, i: (g, 0, 0)),
                  pl.BlockSpec((1, GROUP_W), lambda b, g, i: (0, g))],
        out_specs=(grp, grp,
                   pl.BlockSpec((None, None, None, 4, SSD_STATE, HEAD_DIM), lambda b, g, i: (b, g, i, 0, 0, 0))),
        scratch_shapes=[pltpu.VMEM((4, SSD_STATE, HEAD_DIM), F32)], name=name,
        compiler_params=_cp("parallel", "parallel", "arbitrary"))(xc, xc, xc, proj, dtc, dtr, pcol, prow, nw)


def _ssd_bwd(name, dyn, Y, xc, proj, dtc, dtr, pcol, prow, nw, hs, B):
    T = xc.shape[0]
    S = T // B
    nb = S // SSD_L
    L = SSD_L

    def body(dyn_ref, y_ref, xs_ref, b_ref, c_ref, z_ref, dtc_ref, dtr_ref, pcol_ref, prow_ref, nw_ref, hs_ref,
             dxs_ref, db_ref, dc_ref, dz_ref, ddt_ref, dpar_ref, dnw_ref, dH):
        @pl.when(pl.program_id(2) == 0)
        def _():
            dH[...] = jnp.zeros_like(dH)
            dpar_ref[...] = jnp.zeros_like(dpar_ref)
            dnw_ref[...] = jnp.zeros_like(dnw_ref)

        a_c, dt_c, causal, tri, cum_c, cum_r, bb, cb, G = _ssd_common(dtc_ref, dtr_ref, pcol_ref, prow_ref, b_ref, c_ref)
        dsk = pcol_ref[2:3, :]
        clast = cum_c[L - 1:L, :]
        bf = b_ref[...]
        cf = c_ref[...]
        Yv = y_ref[...]
        zv = z_ref[...]
        sz = _sigmoid(zv)
        silu = zv * sz
        y2 = Yv * silu
        r = lax.rsqrt(jnp.mean(y2 * y2, axis=-1, keepdims=True) + EPS)
        yhat = y2 * r
        dyv = dyn_ref[...]
        dnw_ref[...] += jnp.sum(dyv * yhat, axis=0, keepdims=True)
        dyhat = dyv * nw_ref[...]
        dy2 = r * (dyhat - yhat * jnp.mean(dyhat * yhat, axis=-1, keepdims=True))
        dY = dy2 * silu
        dz_ref[...] = dy2 * Yv * (sz * (1.0 + zv * (1.0 - sz)))

        lane4 = lax.broadcasted_iota(jnp.int32, (1, 4), 1)
        dG = jnp.zeros((L, L), F32)
        dBs = jnp.zeros((L, SSD_STATE), F32)
        dCs = jnp.zeros((L, SSD_STATE), F32)
        dA = jnp.zeros((L, 4), F32)
        ddtx = jnp.zeros((L, 4), F32)
        ddsk = jnp.zeros((1, 4), F32)
        dcl = jnp.zeros((1, 4), F32)
        for h in range(4):
            sl = slice(HEAD_DIM * h, HEAD_DIM * (h + 1))
            onehot = (lane4 == h).astype(F32)
            cc = cum_c[:, h:h + 1]
            cl = clast[:, h:h + 1]
            lm = jnp.exp(jnp.where(causal, cc - cum_r[h:h + 1, :], NEG))
            M = (G * lm).astype(BF16)
            xh = xs_ref[:, sl]
            dth = dt_c[:, h:h + 1]
            X = xh * dth
            Xb = X.astype(BF16)
            dYh = dY[:, sl]
            dYb = dYh.astype(BF16)
            Hb = hs_ref[h].astype(BF16)
            dHh = dH[h]
            dHb = dHh.astype(BF16)
            alpha = jnp.exp(cc)
            beta = jnp.exp(cl - cc)
            dXoff = beta * _dot(bb, dHb, NN)
            dX = _dot(M, dYb, TN) + dXoff
            dG = dG + _dot(dYb, Xb, NT) * lm
            dCs = dCs + _dot((alpha * dYh).astype(BF16), Hb, NT)
            dBs = dBs + _dot((beta * X).astype(BF16), dHb, NT)
            ypre = Yv[:, sl] - dsk[:, h:h + 1] * xh
            dA_h = (jnp.sum(dYb.astype(F32) * ypre, axis=-1, keepdims=True)
                    - jnp.sum(Xb.astype(F32) * dX, axis=-1, keepdims=True))
            dA = dA + dA_h * onehot
            dcl_h = (jnp.sum(jnp.sum(dHh * (jnp.exp(cl) * hs_ref[h]), axis=-1, keepdims=True), axis=0, keepdims=True)
                     + jnp.sum(jnp.sum(Xb.astype(F32) * dXoff, axis=-1, keepdims=True), axis=0, keepdims=True))
            dcl = dcl + dcl_h * onehot
            ddtx = ddtx + jnp.sum(dX * xh, axis=-1, keepdims=True) * onehot
            ddsk = ddsk + jnp.sum(jnp.sum(dYh * xh, axis=-1, keepdims=True), axis=0, keepdims=True) * onehot
            dxs_ref[:, sl] = dsk[:, h:h + 1] * dYh + dX * dth
            dH[h] = jnp.exp(cl) * dHh + _dot((alpha * cf).astype(BF16), dYb, TN)
        dGb = dG.astype(BF16)
        dc_ref[...] = _dot(dGb, bb, NN) + dCs
        db_ref[...] = _dot(dGb, cb, TN) + dBs
        hp = lax.Precision.HIGHEST
        last = lax.broadcasted_iota(jnp.int32, (L, 1), 0) == L - 1
        dA = dA + jnp.where(last, dcl, 0.0)
        dadt = lax.dot_general(tri, dA, TN, precision=hp, preferred_element_type=F32)
        ddt = dadt * a_c + ddtx
        d_a = jnp.sum(dadt * dt_c, axis=0, keepdims=True)
        ddraw = ddt * _sigmoid(dtc_ref[...] + pcol_ref[0:1, :])
        ddt_ref[...] = ddraw
        dpar_ref[0:1, :] += jnp.sum(ddraw, axis=0, keepdims=True)
        dpar_ref[1:2, :] += d_a * a_c
        dpar_ref[2:3, :] += ddsk

    rowi = lambda b, g, i: b * nb + (nb - 1 - i)
    grp = pl.BlockSpec((L, GROUP_W), lambda b, g, i: (rowi(b, g, i), g))
    st = pl.BlockSpec((L, SSD_STATE), lambda b, g, i: (rowi(b, g, i), g))
    f = jax.ShapeDtypeStruct
    return pl.pallas_call(
        body,
        out_shape=(f((T, 1024), F32), f((T, 512), F32), f((T, 512), F32), f((T, 1024), F32),
                   f((SSD_GROUPS, T, 4), F32), f((B, SSD_GROUPS, 3, 4), F32), f((B, 1, 1024), F32)),
        grid=(B, SSD_GROUPS, nb),
        in_specs=[grp, grp, grp,
                  pl.BlockSpec((L, SSD_STATE), lambda b, g, i: (rowi(b, g, i), 8 + g)),
                  pl.BlockSpec((L, SSD_STATE), lambda b, g, i: (rowi(b, g, i), 12 + g)),
                  grp,
                  pl.BlockSpec((None, L, 4), lambda b, g, i: (g, rowi(b, g, i), 0)),
                  pl.BlockSpec((None, 4, L), lambda b, g, i: (g, 0, rowi(b, g, i))),
                  pl.BlockSpec((None, 3, 4), lambda b, g, i: (g, 0, 0)),
                  pl.BlockSpec((None, 4, 3), lambda b, g, i: (g, 0, 0)),
                  pl.BlockSpec((1, GROUP_W), lambda b, g, i: (0, g)),
                  pl.BlockSpec((None, None, None, 4, SSD_STATE, HEAD_DIM), lambda b, g, i: (b, g, nb - 1 - i, 0, 0, 0))],
        out_specs=(grp, st, st, grp,
                   pl.BlockSpec((None, L, 4), lambda b, g, i: (g, rowi(b, g, i), 0)),
                   pl.BlockSpec((None, None, 3, 4), lambda b, g, i: (b, g, 0, 0)),
                   pl.BlockSpec((None, 1, GROUP_W), lambda b, g, i: (b, 0, g))),
        scratch_shapes=[pltpu.VMEM((4, SSD_STATE, HEAD_DIM), F32)], name=name,
        compiler_params=_cp("parallel", "parallel", "arbitrary"))(dyn, Y, xc, xc, xc, proj, dtc, dtr, pcol, prow, nw, hs)


def _headnorm_fwd(name, proj, col_block, w):
    T = proj.shape[0]

    def body(x_ref, w_ref, o_ref):
        for h in range(ATT_HEADS):
            sl = slice(HEAD_DIM * h, HEAD_DIM * (h + 1))
            xh = x_ref[:, sl]
            r = lax.rsqrt(jnp.mean(xh * xh, axis=-1, keepdims=True) + EPS)
            o_ref[:, sl] = (xh * r * w_ref[...]).astype(BF16)

    return pl.pallas_call(
        body, out_shape=jax.ShapeDtypeStruct((T, 1024), BF16), grid=(T // ROW_T,),
        in_specs=[pl.BlockSpec((ROW_T, 1024), lambda i: (i, col_block)), pl.BlockSpec((1, HEAD_DIM), lambda i: (0, 0))],
        out_specs=pl.BlockSpec((ROW_T, 1024), lambda i: (i, 0)), name=name, compiler_params=_cp("parallel"))(proj, w)


def _headnorm_bwd(name, dn, proj, col_block, w):
    T = proj.shape[0]

    def body(dn_ref, x_ref, w_ref, dx_ref, dw_ref):
        @pl.when(pl.program_id(0) == 0)
        def _():
            dw_ref[...] = jnp.zeros_like(dw_ref)

        dw = jnp.zeros((1, HEAD_DIM), F32)
        for h in range(ATT_HEADS):
            sl = slice(HEAD_DIM * h, HEAD_DIM * (h + 1))
            xh = x_ref[:, sl]
            r = lax.rsqrt(jnp.mean(xh * xh, axis=-1, keepdims=True) + EPS)
            xhat = xh * r
            dnh = dn_ref[:, sl]
            dxhat = dnh * w_ref[...]
            dx_ref[:, sl] = r * (dxhat - xhat * jnp.mean(dxhat * xhat, axis=-1, keepdims=True))
            dw = dw + jnp.sum(dnh * xhat, axis=0, keepdims=True)
        dw_ref[...] += dw

    return pl.pallas_call(
        body, out_shape=(jax.ShapeDtypeStruct((T, 1024), F32), jax.ShapeDtypeStruct((1, HEAD_DIM), F32)),
        grid=(T // ROW_T,),
        in_specs=[pl.BlockSpec((ROW_T, 1024), lambda i: (i, 0)), pl.BlockSpec((ROW_T, 1024), lambda i: (i, col_block)),
                  pl.BlockSpec((1, HEAD_DIM), lambda i: (0, 0))],
        out_specs=(pl.BlockSpec((ROW_T, 1024), lambda i: (i, 0)), pl.BlockSpec((1, HEAD_DIM), lambda i: (0, 0))),
        name=name, compiler_params=_cp("arbitrary"))(dn, proj, w)


def _att_bias(nq):
    j = np.arange(ATT_B)[:, None]
    i = np.arange(ATT_B)[None, :]
    out = np.empty((nq, ATT_B, ATT_B), np.float32)
    for dblk in range(nq):
        dl = ATT_B * dblk + i - j
        cnt = ((dl >= 0) & (dl <= 128)).astype(np.float32)
        cnt += ((dl >= 0) & (dl % 4 == 0) & (dl <= 512))
        cnt += ((dl >= 0) & (dl % 16 == 0) & (dl <= 2048))
        out[dblk] = np.where(cnt > 0, np.log(np.maximum(cnt, 1.0)), NEG)
    return jnp.asarray(out)


def _row_pair(nq):
    def f(r, c):
        first = c <= r
        return jnp.where(first, r, nq - 1 - r), jnp.where(first, c, c - (r + 1))
    return f


def _col_pair(nq):
    def f(r, c):
        first = c < nq - r
        kj = jnp.where(first, r, nq - 1 - r)
        return jnp.where(first, r + c, nq - 1 - r + (c - (nq - r))), kj
    return f


ATT_SCALE = 1.0 / math.sqrt(HEAD_DIM)
ATT_HS = 4
ATT_W = ATT_HS * HEAD_DIM


def _att_maps(nq, qk):
    return dict(
        q_tok=lambda b, g, r, c: (b * nq + qk(r, c)[0], g),
        k_tok=lambda b, g, r, c: (b * nq + qk(r, c)[1], g),
        q_feat=lambda b, g, r, c: (g, b * nq + qk(r, c)[0]),
        k_feat=lambda b, g, r, c: (g, b * nq + qk(r, c)[1]),
        bias=lambda b, g, r, c: (qk(r, c)[0] - qk(r, c)[1], 0, 0),
        lse=lambda b, g, r, c: (g, 0, b * nq + qk(r, c)[0]),
        do_tok=lambda b, g, r, c: (b * nq + qk(r, c)[0], ATT_HS + g))


def _att_fwd(name, kn, qT, vT, bias, B):
    T = kn.shape[0]
    nq = (T // B) // ATT_B
    qk = _row_pair(nq)
    mp = _att_maps(nq, qk)

    def body(k_ref, qT_ref, vT_ref, bias_ref, oT_ref, lse_ref, m_s, l_s, acc_s, s_s):
        qi, kj = qk(pl.program_id(2), pl.program_id(3))

        @pl.when(kj == 0)
        def _():
            m_s[...] = jnp.full_like(m_s, NEG)
            l_s[...] = jnp.zeros_like(l_s)
            acc_s[...] = jnp.zeros_like(acc_s)

        bv = bias_ref[...]
        for h in range(ATT_HS):
            rs = slice(HEAD_DIM * h, HEAD_DIM * (h + 1))
            s_s[h] = _dot(k_ref[:, rs], qT_ref[rs, :], NN)
        for h in range(ATT_HS):
            rs = slice(HEAD_DIM * h, HEAD_DIM * (h + 1))
            s = s_s[h] + bv
            m_prev = m_s[h:h + 1, :]
            m_new = jnp.maximum(m_prev, jnp.max(s, axis=0, keepdims=True))
            alpha = jnp.exp(m_prev - m_new)
            p = jnp.exp(s - m_new)
            l_s[h:h + 1, :] = alpha * l_s[h:h + 1, :] + jnp.sum(p, axis=0, keepdims=True)
            acc_s[rs, :] = alpha * acc_s[rs, :] + _dot(vT_ref[rs, :], p.astype(BF16), NN)
            m_s[h:h + 1, :] = m_new

        @pl.when(kj == qi)
        def _():
            for h in range(ATT_HS):
                rs = slice(HEAD_DIM * h, HEAD_DIM * (h + 1))
                oT_ref[rs, :] = (acc_s[rs, :] / l_s[h:h + 1, :]).astype(BF16)
            lse_ref[...] = m_s[...] + jnp.log(l_s[...])

    tok = (ATT_B, ATT_W)
    feat = (ATT_W, ATT_B)
    return pl.pallas_call(
        body,
        out_shape=(jax.ShapeDtypeStruct((1024, T), BF16), jax.ShapeDtypeStruct((ATT_HEADS // ATT_HS, ATT_HS, T), F32)),
        grid=(B, ATT_HEADS // ATT_HS, nq // 2, nq + 1),
        in_specs=[pl.BlockSpec(tok, mp["k_tok"]), pl.BlockSpec(feat, mp["q_feat"]), pl.BlockSpec(feat, mp["k_feat"]),
                  pl.BlockSpec((None, ATT_B, ATT_B), mp["bias"])],
        out_specs=(pl.BlockSpec(feat, mp["q_feat"]), pl.BlockSpec((None, ATT_HS, ATT_B), mp["lse"])),
        scratch_shapes=[pltpu.VMEM((ATT_HS, ATT_B), F32), pltpu.VMEM((ATT_HS, ATT_B), F32),
                        pltpu.VMEM((ATT_W, ATT_B), F32), pltpu.VMEM((ATT_HS, ATT_B, ATT_B), F32)],
        name=name, compiler_params=_cp("parallel", "parallel", "arbitrary", "arbitrary"))(kn, qT, vT, bias)


def _att_scores(k_ref, qT_ref, v_ref, doT_ref, s_s, dp_s):
    for h in range(ATT_HS):
        rs = slice(HEAD_DIM * h, HEAD_DIM * (h + 1))
        s_s[h] = _dot(k_ref[:, rs], qT_ref[rs, :], NN)
        dp_s[h] = _dot(v_ref[:, rs], doT_ref[rs, :].astype(BF16), NN)


def _att_p_ds(s_s, dp_s, doT_ref, oT_ref, lse_ref, bv, h):
    rs = slice(HEAD_DIM * h, HEAD_DIM * (h + 1))
    delta = jnp.sum(doT_ref[rs, :] * oT_ref[rs, :].astype(F32), axis=0, keepdims=True)
    p = jnp.exp(s_s[h] + bv - lse_ref[h:h + 1, :])
    return p, p * (dp_s[h] - delta)


def _att_bwd_dq(name, kn, qT, vb, knT, bias, doT, oT, lse, B):
    T = kn.shape[0]
    nq = (T // B) // ATT_B
    qk = _row_pair(nq)
    mp = _att_maps(nq, qk)

    def body(k_ref, qT_ref, v_ref, kT_ref, bias_ref, doT_ref, oT_ref, lse_ref, dqT_ref, acc_s, s_s, dp_s):
        qi, kj = qk(pl.program_id(2), pl.program_id(3))

        @pl.when(kj == 0)
        def _():
            acc_s[...] = jnp.zeros_like(acc_s)

        bv = bias_ref[...]
        _att_scores(k_ref, qT_ref, v_ref, doT_ref, s_s, dp_s)
        for h in range(ATT_HS):
            rs = slice(HEAD_DIM * h, HEAD_DIM * (h + 1))
            p, ds = _att_p_ds(s_s, dp_s, doT_ref, oT_ref, lse_ref, bv, h)
            acc_s[rs, :] += _dot(kT_ref[rs, :], ds.astype(BF16), NN)

        @pl.when(kj == qi)
        def _():
            dqT_ref[...] = acc_s[...] * ATT_SCALE

    tok = (ATT_B, ATT_W)
    feat = (ATT_W, ATT_B)
    return pl.pallas_call(
        body, out_shape=jax.ShapeDtypeStruct((1024, T), F32), grid=(B, ATT_HEADS // ATT_HS, nq // 2, nq + 1),
        in_specs=[pl.BlockSpec(tok, mp["k_tok"]), pl.BlockSpec(feat, mp["q_feat"]), pl.BlockSpec(tok, mp["k_tok"]),
                  pl.BlockSpec(feat, mp["k_feat"]), pl.BlockSpec((None, ATT_B, ATT_B), mp["bias"]),
                  pl.BlockSpec(feat, mp["q_feat"]), pl.BlockSpec(feat, mp["q_feat"]),
                  pl.BlockSpec((None, ATT_HS, ATT_B), mp["lse"])],
        out_specs=pl.BlockSpec(feat, mp["q_feat"]),
        scratch_shapes=[pltpu.VMEM((ATT_W, ATT_B), F32), pltpu.VMEM((ATT_HS, ATT_B, ATT_B), F32),
                        pltpu.VMEM((ATT_HS, ATT_B, ATT_B), F32)],
        name=name, compiler_params=_cp("parallel", "parallel", "arbitrary", "arbitrary"))(
            kn, qT, vb, knT, bias, doT, oT, lse)


def _att_bwd_dkv(name, kn, qT, vb, qn, bias, doT, oT, lse, dyn, B):
    T = kn.shape[0]
    nq = (T // B) // ATT_B
    qk = _col_pair(nq)
    mp = _att_maps(nq, qk)

    def body(k_ref, qT_ref, v_ref, q_ref, bias_ref, doT_ref, oT_ref, lse_ref, do_ref, dk_ref, dv_ref, dk_s, dv_s,
             s_s, dp_s):
        qi, kj = qk(pl.program_id(2), pl.program_id(3))

        @pl.when(qi == kj)
        def _():
            dk_s[...] = jnp.zeros_like(dk_s)
            dv_s[...] = jnp.zeros_like(dv_s)

        bv = bias_ref[...]
        _att_scores(k_ref, qT_ref, v_ref, doT_ref, s_s, dp_s)
        for h in range(ATT_HS):
            rs = slice(HEAD_DIM * h, HEAD_DIM * (h + 1))
            p, ds = _att_p_ds(s_s, dp_s, doT_ref, oT_ref, lse_ref, bv, h)
            dv_s[h] += _dot(p.astype(BF16), do_ref[:, rs].astype(BF16), NN)
            dk_s[h] += _dot(ds.astype(BF16), q_ref[:, rs], NN)

        @pl.when(qi == nq - 1)
        def _():
            for h in range(ATT_HS):
                rs = slice(HEAD_DIM * h, HEAD_DIM * (h + 1))
                dk_ref[:, rs] = dk_s[h] * ATT_SCALE
                dv_ref[:, rs] = dv_s[h]

    tok = (ATT_B, ATT_W)
    feat = (ATT_W, ATT_B)
    osh = jax.ShapeDtypeStruct((T, 1024), F32)
    return pl.pallas_call(
        body, out_shape=(osh, osh), grid=(B, ATT_HEADS // ATT_HS, nq // 2, nq + 1),
        in_specs=[pl.BlockSpec(tok, mp["k_tok"]), pl.BlockSpec(feat, mp["q_feat"]), pl.BlockSpec(tok, mp["k_tok"]),
                  pl.BlockSpec(tok, mp["q_tok"]), pl.BlockSpec((None, ATT_B, ATT_B), mp["bias"]),
                  pl.BlockSpec(feat, mp["q_feat"]), pl.BlockSpec(feat, mp["q_feat"]),
                  pl.BlockSpec((None, ATT_HS, ATT_B), mp["lse"]), pl.BlockSpec(tok, mp["do_tok"])],
        out_specs=(pl.BlockSpec(tok, mp["k_tok"]), pl.BlockSpec(tok, mp["k_tok"])),
        scratch_shapes=[pltpu.VMEM((ATT_HS, ATT_B, HEAD_DIM), F32), pltpu.VMEM((ATT_HS, ATT_B, HEAD_DIM), F32),
                        pltpu.VMEM((ATT_HS, ATT_B, ATT_B), F32), pltpu.VMEM((ATT_HS, ATT_B, ATT_B), F32)],
        name=name, compiler_params=_cp("parallel", "parallel", "arbitrary", "arbitrary"))(
            kn, qT, vb, qn, bias, doT, oT, lse, dyn)


def _group_cols(v):
    return v.reshape(SSD_GROUPS, 4)


def _ssd_params(p):
    rows = jnp.stack([_group_cols(p["dt_bias"]), _group_cols(p["a_log"]), _group_cols(p["d_skip"])], axis=1)
    return rows, jnp.swapaxes(rows, 1, 2)


def _mixer_fwd(tag, x1, p, weights, bias, B):
    T = x1.shape[0]
    S = T // B
    nt = T // ROW_T
    h2 = _rms_fwd(tag + "_mixrms", x1, p["mix_norm"][None])
    win = weights("win", h2)["win"]
    proj = _mm(tag + "_proj",
               [(h2, pl.BlockSpec((ROW_T, D_MODEL), lambda j, i, k: (i, 0)),
                 win, pl.BlockSpec((D_MODEL, PROJ_TN), lambda j, i, k: (0, j)))],
               jax.ShapeDtypeStruct((T, IN_PAD), F32), pl.BlockSpec((ROW_T, PROJ_TN), lambda j, i, k: (i, j)),
               (IN_PAD // PROJ_TN, nt, 1), NN, (ROW_T, PROJ_TN))
    xbc = proj[:, COL_XBC:COL_Q].reshape(B, S, CONV_DIM)
    xpad = jnp.pad(xbc, ((0, 0), (PAD_R, PAD_R), (0, 0)))
    cw, cbias = p["conv_w"], p["conv_b"][None]
    xc = _conv_fwd(tag + "_conv", xpad, cw, cbias).reshape(T, CONV_DIM)
    dtraw = proj[:, COL_DT:COL_DT + SSD_HEADS].reshape(T, SSD_GROUPS, 4)
    dtc = jnp.transpose(dtraw, (1, 0, 2))
    dtr = jnp.transpose(dtraw, (1, 2, 0))
    pcol, prow = _ssd_params(p)
    Y, y_ssd, hs = _ssd_fwd(tag + "_ssd", xc, proj, dtc, dtr, pcol, prow, p["ssd_norm"][None], B)
    qn = _headnorm_fwd(tag + "_qn", proj, COL_Q // 1024, p["q_norm"][None])
    kn = _headnorm_fwd(tag + "_kn", proj, COL_K // 1024, p["k_norm"][None])
    qT = (qn * ATT_SCALE).T
    vb = proj[:, COL_V:COL_V + 1024].astype(BF16)
    oT, lse = _att_fwd(tag + "_att", kn, qT, vb.T, bias, B)
    ymix = jnp.concatenate([y_ssd, oT.T], axis=1)
    rest = weights("rest", ymix)
    x2 = _mm(tag + "_out",
             [(ymix, pl.BlockSpec((ROW_T, MIX_SH), lambda i, n, k: (i, k)),
               rest["wout"], pl.BlockSpec((None, MIX_SH, D_MODEL), lambda i, n, k: (k, 0, 0)))],
             jax.ShapeDtypeStruct((T, D_MODEL), F32), pl.BlockSpec((ROW_T, D_MODEL), lambda i, n, k: (i, 0)),
             (nt, 1, N_SHARD), NN, (ROW_T, D_MODEL),
             res=(x1, pl.BlockSpec((ROW_T, D_MODEL), lambda i, n, k: (i, 0))))
    saved = dict(x1=x1, h2=h2, proj=proj, xpad=xpad, xc=xc, dtc=dtc, dtr=dtr, Y=Y, hs=hs,
                 qn=qn, kn=kn, qT=qT, vb=vb, oT=oT, lse=lse, ymix=ymix, win=win, wout=rest["wout"])
    return x2, saved


def _mixer_bwd(tag, dx2, sv, p, bias, B):
    T = dx2.shape[0]
    S = T // B
    nt = T // ROW_T
    sg = {}
    dymix = _mm(tag + "_dymix",
                [(dx2, pl.BlockSpec((ROW_T, D_MODEL), lambda n, i, k: (i, 0)),
                  sv["wout"], pl.BlockSpec((None, MIX_SH, D_MODEL), lambda n, i, k: (n, 0, 0)))],
                jax.ShapeDtypeStruct((T, MIX_W), F32), pl.BlockSpec((ROW_T, MIX_SH), lambda n, i, k: (i, n)),
                (N_SHARD, nt, 1), NT, (ROW_T, MIX_SH))
    gwout = _mm(tag + "_dwout",
                [(sv["ymix"], pl.BlockSpec((ROW_T, MIX_SH), lambda m, n, k: (k, m)),
                  dx2, pl.BlockSpec((ROW_T, D_MODEL), lambda m, n, k: (k, 0)))],
                jax.ShapeDtypeStruct((N_SHARD, MIX_SH, D_MODEL), BF16),
                pl.BlockSpec((None, MIX_SH, D_MODEL), lambda m, n, k: (m, 0, 0)),
                (N_SHARD, 1, nt), TN, (MIX_SH, D_MODEL))
    proj = sv["proj"]
    doT = dymix[:, 1024:].T
    dqn = _att_bwd_dq(tag + "_attdq", sv["kn"], sv["qT"], sv["vb"], sv["kn"].T, bias, doT, sv["oT"], sv["lse"], B).T
    dkn, dv = _att_bwd_dkv(tag + "_attdkv", sv["kn"], sv["qT"], sv["vb"], sv["qn"], bias, doT, sv["oT"], sv["lse"],
                           dymix, B)
    dq, sg["q_norm"] = _headnorm_bwd(tag + "_qnb", dqn, proj, COL_Q // 1024, p["q_norm"][None])
    dk, sg["k_norm"] = _headnorm_bwd(tag + "_knb", dkn, proj, COL_K // 1024, p["k_norm"][None])
    pcol, prow = _ssd_params(p)
    dxs, dB, dC, dz, ddt, dpar, dnw = _ssd_bwd(tag + "_ssdb", dymix, sv["Y"], sv["xc"], proj, sv["dtc"], sv["dtr"],
                                               pcol, prow, p["ssd_norm"][None], sv["hs"], B)
    dpar = jnp.sum(dpar, axis=0)
    sg["dt_bias"] = dpar[:, 0, :].reshape(SSD_HEADS)
    sg["a_log"] = dpar[:, 1, :].reshape(SSD_HEADS)
    sg["d_skip"] = dpar[:, 2, :].reshape(SSD_HEADS)
    sg["ssd_norm"] = jnp.sum(dnw, axis=0)
    dxc = jnp.concatenate([dxs, dB, dC], axis=1).reshape(B, S, CONV_DIM)
    dxc_pad = jnp.pad(dxc, ((0, 0), (0, PAD_R), (0, 0)))
    dxbc, sg["conv_w"], sg["conv_b"] = _conv_bwd(tag + "_convb", sv["xpad"], dxc_pad, p["conv_w"], p["conv_b"][None])
    ddt16 = jnp.transpose(ddt, (1, 0, 2)).reshape(T, SSD_HEADS)
    dproj = jnp.concatenate([dz, dxbc.reshape(T, CONV_DIM), dq, dk, dv, ddt16,
                             jnp.zeros((T, IN_PAD - COL_DT - SSD_HEADS), F32)], axis=1).astype(BF16)
    win = sv["win"]
    gwin = _mm(tag + "_dwin",
               [(sv["h2"], pl.BlockSpec((ROW_T, D_MODEL), lambda n, m, k: (k, 0)),
                 dproj, pl.BlockSpec((ROW_T, PROJ_TN), lambda n, m, k: (k, n)))],
               jax.ShapeDtypeStruct((D_MODEL, IN_PAD), BF16), pl.BlockSpec((D_MODEL, PROJ_TN), lambda n, m, k: (0, n)),
               (IN_PAD // PROJ_TN, 1, nt), TN, (D_MODEL, PROJ_TN))
    dh2 = _mm(tag + "_dh2",
              [(dproj, pl.BlockSpec((ROW_T, PROJ_TN), lambda i, n, k: (i, k)),
                win, pl.BlockSpec((D_MODEL, PROJ_TN), lambda i, n, k: (0, k)))],
              jax.ShapeDtypeStruct((T, D_MODEL), F32), pl.BlockSpec((ROW_T, D_MODEL), lambda i, n, k: (i, 0)),
              (nt, 1, IN_PAD // PROJ_TN), NT, (ROW_T, D_MODEL))
    dx1, sg["mix_norm"] = _rms_bwd(tag + "_mixrmsb", dh2, sv["x1"], p["mix_norm"][None], dx2)
    return dx1, sg, gwout, gwin


def _win_pack(w):
    return jnp.concatenate([w[:, :3072], w[:, 3088:], w[:, 3072:3088],
                            jnp.zeros((w.shape[0], IN_PAD - IN_PROJ), w.dtype)], axis=1)


def _win_unpack(g):
    return jnp.concatenate([g[:, :3072], g[:, COL_DT:COL_DT + SSD_HEADS], g[:, 3072:COL_DT]], axis=1)


def _local_step(x, target, small, weights, scatter, B):
    T = x.shape[0]
    bias = _att_bias((T // B) // ATT_B)
    saved = []
    h = x
    for l in range(DEPTH):
        tag = "l%d" % l
        p = {k: v[l] for k, v in small.items()}
        w1 = weights(l, "ffn1", h)
        x1, ffn1 = _ffn_fwd(tag + "f1", h, p["ffn1_norm"][None], w1["g1"], w1["u1"], w1["d1"])
        x2, sv = _mixer_fwd(tag, x1, p, functools.partial(weights, l), bias, B)
        w2 = weights(l, "rest", x2)
        h, ffn2 = _ffn_fwd(tag + "f2", x2, p["ffn2_norm"][None], w2["g2"], w2["u2"], w2["d2"])
        saved.append((ffn1, sv, ffn2, w1, w2))
    d, lsum = _loss_grad("loss", h, target)
    sgrads = [None] * DEPTH
    for l in reversed(range(DEPTH)):
        tag = "l%db" % l
        p = {k: v[l] for k, v in small.items()}
        ffn1, sv, ffn2, w1, w2 = saved[l]
        sg = {}
        d, sg["ffn2_norm"], (gg, gu, gd) = _ffn_bwd(tag + "f2", d, ffn2, p["ffn2_norm"][None],
                                                   w2["g2"], w2["u2"], w2["d2"])
        d = scatter(l, "ffn2", dict(g2=gg, u2=gu, d2=gd), d)
        d, sgm, gwout, gwin = _mixer_bwd(tag, d, sv, p, bias, B)
        sg.update(sgm)
        d = scatter(l, "mixer", dict(wout=gwout, win=gwin), d)
        d, sg["ffn1_norm"], (gg, gu, gd) = _ffn_bwd(tag + "f1", d, ffn1, p["ffn1_norm"][None],
                                                   w1["g1"], w1["u1"], w1["d1"])
        d = scatter(l, "ffn1", dict(g1=gg, u1=gu, d1=gd), d)
        sgrads[l] = sg
    return lsum, d, sgrads


MESH = pl.DeviceIdType.MESH
ANY = pl.BlockSpec(memory_space=pl.ANY)


def _place():
    return lax.axis_index("x"), lax.axis_index("y"), lax.axis_index("c")


def _other_chips(x, y):
    return [(1 - x, y), (x, 1 - y), (1 - x, 1 - y)]


HBM = pl.BlockSpec(memory_space=pltpu.HBM)
SEM = pl.BlockSpec(memory_space=pltpu.SEMAPHORE)
EFFECT = pltpu.SideEffectType.DATAFLOW_SIDE_EFFECTING


def _hbm(a):
    return pltpu.with_memory_space_constraint(a, pltpu.HBM)


def _exchange(gather, src, land, send, recv, n, act):
    x, y, c = _place()
    for k, (px, py) in enumerate(_other_chips(x, y)):
        for a in range(n):
            if gather:
                s_out, d_out, d_in = src[a], land[a].at[2 * x + y], land[a].at[2 * px + py]
            else:
                s_out, d_out, d_in = src[a].at[2 * px + py], land[a].at[k], land[a].at[k]
            act(pltpu.make_async_remote_copy(
                src_ref=s_out, dst_ref=d_out if act is _start else d_in, send_sem=send.at[k * n + a],
                recv_sem=recv.at[k * n + a], device_id=(px, py, c), device_id_type=MESH))


def _start(cp):
    cp.start()


def _finish(cp):
    cp.wait_send()
    cp.wait_recv()


def _exchange_start(name, gather, srcs, carry):
    n = len(srcs)
    lands = [lax.empty(((N_SHARD,) + s.shape) if gather else ((3,) + s.shape[1:]), s.dtype) for s in srcs]

    def body(*refs):
        _exchange(gather, refs[:n], refs[n:2 * n], refs[2 * n + 1], refs[2 * n + 2], n, _start)

    ops = [_hbm(a) for a in list(srcs) + lands + [carry]]
    out = pl.pallas_call(
        body, name=name,
        out_shape=(pltpu.SemaphoreType.DMA((3 * n,)), pltpu.SemaphoreType.DMA((3 * n,)),
                   *[pltpu.HBM(a.shape, a.dtype) for a in ops]),
        in_specs=[HBM] * len(ops), out_specs=(SEM, SEM, *[HBM] * len(ops)),
        input_output_aliases={i: 2 + i for i in range(len(ops))},
        compiler_params=pltpu.CompilerParams(has_side_effects=EFFECT))(*ops)
    return dict(gather=gather, send=out[0], recv=out[1], srcs=list(out[2:2 + n]), lands=list(out[2 + n:2 + 2 * n])), out[-1]


def _exchange_wait(name, ex, after):
    n = len(ex["srcs"])
    gather = ex["gather"]

    def body(*refs):
        _exchange(gather, refs[:n], refs[n:2 * n], refs[2 * n], refs[2 * n + 1], n, _finish)

    ops = ex["srcs"] + ex["lands"]
    out = pl.pallas_call(
        body, name=name, out_shape=[pltpu.HBM(a.shape, a.dtype) for a in ops],
        in_specs=[HBM] * len(ops) + [SEM, SEM, ANY], out_specs=[HBM] * len(ops),
        input_output_aliases={i: i for i in range(len(ops))},
        compiler_params=pltpu.CompilerParams(has_side_effects=EFFECT))(*ops, ex["send"], ex["recv"], after)
    return list(out[:n]), list(out[n:])


def _swap_sibling(parts):
    n = len(parts)

    def body(*refs):
        src, dst = refs[:n], refs[n:2 * n]
        send, recv = refs[2 * n:]
        x, y, c = _place()
        cps = [pltpu.make_async_remote_copy(src_ref=src[a], dst_ref=dst[a], send_sem=send.at[a], recv_sem=recv.at[a],
                                            device_id=(x, y, 1 - c), device_id_type=MESH) for a in range(n)]
        for cp in cps:
            cp.start()
        for cp in cps:
            cp.wait_recv()
        for cp in cps:
            cp.wait_send()

    return pl.pallas_call(
        body, out_shape=[jax.ShapeDtypeStruct(p.shape, p.dtype) for p in parts],
        in_specs=[ANY] * n, out_specs=[ANY] * n,
        scratch_shapes=[pltpu.SemaphoreType.DMA((n,)), pltpu.SemaphoreType.DMA((n,))],
        name="swap_sibling")(*parts)


def _allreduce_small(name, v):
    R = v.shape[0]

    def body(v_ref, o_ref, buf, send, recv):
        x, y, c = _place()
        me = 4 * x + 2 * y + c
        buf[me] = v_ref[...]
        cps = []
        for k in range(1, 8):
            fx, fy, fc = (k >> 2) & 1, (k >> 1) & 1, k & 1
            px = 1 - x if fx else x
            py = 1 - y if fy else y
            pc = 1 - c if fc else c
            cp = pltpu.make_async_remote_copy(src_ref=v_ref, dst_ref=buf.at[me], send_sem=send.at[k - 1],
                                              recv_sem=recv.at[k - 1], device_id=(px, py, pc), device_id_type=MESH)
            cp.start()
            cps.append((cp, 4 * px + 2 * py + pc))
        for k, (cp, peer) in enumerate(cps):
            pltpu.make_async_remote_copy(src_ref=v_ref, dst_ref=buf.at[peer], send_sem=send.at[k], recv_sem=recv.at[k],
                                         device_id=(x, y, c), device_id_type=MESH).wait_recv()
        for cp, _ in cps:
            cp.wait_send()
        acc = buf[0]
        for d in range(1, 8):
            acc = acc + buf[d]
        o_ref[...] = acc

    return pl.pallas_call(
        body, out_shape=jax.ShapeDtypeStruct((R, 128), F32),
        in_specs=[pl.BlockSpec(memory_space=pltpu.VMEM)], out_specs=pl.BlockSpec(memory_space=pltpu.VMEM),
        scratch_shapes=[pltpu.VMEM((8, R, 128), F32), pltpu.SemaphoreType.DMA((7,)), pltpu.SemaphoreType.DMA((7,))],
        name=name)(v)


def _row_tile(r):
    for t in (256, 128, 64, 32, 16, 8):
        if r % t == 0:
            return t
    raise ValueError(r)


def _sum4(name, own, got):
    R, C = own.shape
    tr = _row_tile(R)

    def body(o_ref, g_ref, s_ref):
        s = o_ref[...].astype(F32)
        for k in range(3):
            s = s + g_ref[k].astype(F32)
        s_ref[...] = s

    return pl.pallas_call(
        body, out_shape=jax.ShapeDtypeStruct((R, C), F32), grid=(R // tr,),
        in_specs=[pl.BlockSpec((tr, C), lambda i: (i, 0)), pl.BlockSpec((3, tr, C), lambda i: (0, i, 0))],
        out_specs=pl.BlockSpec((tr, C), lambda i: (i, 0)), name=name, compiler_params=_cp("parallel"))(own, got)


def _adamw(name, w, gparts, m, v):
    R, C = w.shape
    tr = _row_tile(R)
    ng = len(gparts)
    c1 = 1.0 - ADAM_B1 ** ADAM_STEP
    c2 = 1.0 - ADAM_B2 ** ADAM_STEP

    def body(*refs):
        w_ref = refs[0]
        g_refs = refs[1:1 + ng]
        m_ref, v_ref, go_ref, d_ref, mo_ref, vo_ref = refs[1 + ng:]
        g = g_refs[0][...]
        for r in g_refs[1:]:
            g = g + r[...]
        mn = ADAM_B1 * m_ref[...] + (1.0 - ADAM_B1) * g
        vn = ADAM_B2 * v_ref[...] + (1.0 - ADAM_B2) * (g * g)
        go_ref[...] = g
        mo_ref[...] = mn
        vo_ref[...] = vn
        d_ref[...] = -ADAM_LR * ((mn / c1) / (jnp.sqrt(vn / c2) + ADAM_EPS) + ADAM_WD * w_ref[...])

    blk = pl.BlockSpec((tr, C), lambda i: (i, 0))
    osh = jax.ShapeDtypeStruct((R, C), F32)
    return pl.pallas_call(
        body, out_shape=(osh, osh, osh, osh), grid=(R // tr,), in_specs=[blk] * (3 + ng), out_specs=(blk,) * 4,
        name=name, compiler_params=_cp("parallel"))(w, *gparts, m, v)


def _adamw_layers(name, w, sums, m, v):
    R2, C = w.shape
    R = R2 // DEPTH
    tr = _row_tile(R)
    nr = R // tr
    c1 = 1.0 - ADAM_B1 ** ADAM_STEP
    c2 = 1.0 - ADAM_B2 ** ADAM_STEP

    def body(w_ref, a0, b0, a1, b1, m_ref, v_ref, go_ref, d_ref, mo_ref, vo_ref):
        g = jnp.where(pl.program_id(0) == 0, a0[...] + b0[...], a1[...] + b1[...])
        mn = ADAM_B1 * m_ref[...] + (1.0 - ADAM_B1) * g
        vn = ADAM_B2 * v_ref[...] + (1.0 - ADAM_B2) * (g * g)
        go_ref[...] = g
        mo_ref[...] = mn
        vo_ref[...] = vn
        d_ref[...] = -ADAM_LR * ((mn / c1) / (jnp.sqrt(vn / c2) + ADAM_EPS) + ADAM_WD * w_ref[...])

    blk = pl.BlockSpec((tr, C), lambda l, i: (l * nr + i, 0))
    lay0 = pl.BlockSpec((tr, C), lambda l, i: (jnp.where(l == 0, i, nr - 1), 0))
    lay1 = pl.BlockSpec((tr, C), lambda l, i: (jnp.where(l == 1, i, 0), 0))
    osh = jax.ShapeDtypeStruct((R2, C), F32)
    return pl.pallas_call(
        body, out_shape=(osh, osh, osh, osh), grid=(DEPTH, nr),
        in_specs=[blk, lay0, lay0, lay1, lay1, blk, blk], out_specs=(blk,) * 4,
        name=name, compiler_params=_cp("arbitrary", "arbitrary"))(w, *sums[0], *sums[1], m, v)


BIG = [("ffn1_w_gate", "g1"), ("ffn1_w_up", "u1"), ("ffn1_w_down", "d1"), ("w_in", "win"), ("w_out", "wout"),
       ("ffn2_w_gate", "g2"), ("ffn2_w_up", "u2"), ("ffn2_w_down", "d2")]
SMALL = ["ffn1_norm", "mix_norm", "conv_b", "dt_bias", "a_log", "d_skip", "ssd_norm", "q_norm", "k_norm", "ffn2_norm"]
WEIGHTS = ["ffn1_norm", "ffn1_w_gate", "ffn1_w_up", "ffn1_w_down", "mix_norm", "w_in", "conv_w", "conv_b", "dt_bias",
           "a_log", "d_skip", "ssd_norm", "q_norm", "k_norm", "w_out", "ffn2_norm", "ffn2_w_gate", "ffn2_w_up",
           "ffn2_w_down"]
CONV_SH = CONV_DIM // N_SHARD
GATHER_GROUPS = [(0, "ffn1", ["g1", "u1", "d1"]), (0, "win", ["win"]), (0, "rest", ["wout", "g2", "u2", "d2"]),
                 (1, "all", ["g1", "u1", "d1", "win", "wout", "g2", "u2", "d2"])]


def _pad128(v):
    v = v.reshape(-1)
    return jnp.pad(v, (0, (-v.shape[0]) % 128))


def _pack(pieces):
    flat, offs, pos = [], [], 0
    for p in pieces:
        q = _pad128(p.astype(F32))
        offs.append(pos)
        pos += q.shape[0] // 128
        flat.append(q)
    total = -(-pos // 8) * 8
    out = jnp.concatenate(flat + [jnp.zeros(((total - pos) * 128,), F32)]).reshape(total, 128)
    return out, offs


def _unpack(packed, offs, shapes):
    out = []
    for off, shp in zip(offs, shapes):
        n = int(np.prod(shp))
        rows = -(-n // 128)
        out.append(packed[off:off + rows].reshape(-1)[:n].reshape(shp))
    return out


def kernel(x, ffn1_norm, ffn1_w_gate, ffn1_w_up, ffn1_w_down, mix_norm, w_in, conv_w, conv_b, dt_bias, a_log, d_skip, ssd_norm, q_norm, k_norm, w_out, ffn2_norm, ffn2_w_gate, ffn2_w_up, ffn2_w_down, loss_target, m_ffn1_norm, m_ffn1_w_gate, m_ffn1_w_up, m_ffn1_w_down, m_mix_norm, m_w_in, m_conv_w, m_conv_b, m_dt_bias, m_a_log, m_d_skip, m_ssd_norm, m_q_norm, m_k_norm, m_w_out, m_ffn2_norm, m_ffn2_w_gate, m_ffn2_w_up, m_ffn2_w_down, v_ffn1_norm, v_ffn1_w_gate, v_ffn1_w_up, v_ffn1_w_down, v_mix_norm, v_w_in, v_conv_w, v_conv_b, v_dt_bias, v_a_log, v_d_skip, v_ssd_norm, v_q_norm, v_k_norm, v_w_out, v_ffn2_norm, v_ffn2_w_gate, v_ffn2_w_up, v_ffn2_w_down):
    A = dict(locals())
    ix, iy, ic = _place()
    me = 2 * ix + iy
    B, S, _ = x.shape
    T = B * S

    own = {key: A[name].astype(BF16) for name, key in BIG}
    exs, carry = [], jnp.zeros((8, 128), F32)
    for gi, (l, _, keys) in enumerate(GATHER_GROUPS):
        ex, carry = _exchange_start("gather_start%d" % gi, True, [own[key][l] for key in keys], carry)
        exs.append(ex)
    landed = {}

    def weights(l, group, after):
        gi = [i for i, (gl, gname, _) in enumerate(GATHER_GROUPS) if gl == l and gname in (group, "all")][0]
        if gi not in landed:
            srcs, lands = _exchange_wait("gather_wait%d" % gi, exs[gi], after)
            landed[gi] = {}
            for key, mine, land in zip(GATHER_GROUPS[gi][2], srcs, lands):
                full = lax.dynamic_update_slice(land, mine[None], (me, 0, 0))
                if key == "win":
                    full = _win_pack(jnp.concatenate([full[j] for j in range(N_SHARD)], axis=1))
                landed[gi][key] = full
        return landed[gi]

    placed = lax.dynamic_update_slice(jnp.zeros((DEPTH, CONV_K, CONV_DIM), F32),
                                      conv_w * (ic == 0).astype(F32), (0, 0, me * CONV_SH))
    conv_full = _allreduce_small("gather_conv_w", placed.reshape(-1, 128)).reshape(DEPTH, CONV_K, CONV_DIM)

    pending = []

    def scatter(l, group, grads, carry):
        keys = sorted(grads)
        arrs = [grads[key] for key in keys]
        if "win" in grads:
            arrs[keys.index("win")] = jnp.transpose(_win_unpack(grads["win"]).reshape(D_MODEL, N_SHARD, IN_SH), (1, 0, 2))
        ex, carry = _exchange_start("scatter_start_l%d_%s" % (l, group), False, arrs, carry)
        pending.append((l, keys, ex))
        return carry

    small = {name: A[name] for name in SMALL}
    small["conv_w"] = conv_full
    lsum, dx, sgrads = _local_step(x.reshape(T, D_MODEL), loss_target.reshape(T, D_MODEL), small, weights, scatter, B)

    names = SMALL + ["conv_w"]
    pieces = [jnp.stack([sgrads[l][n].reshape(small[n].shape[1:]) for l in range(DEPTH)]) for n in names]
    pieces.append(0.5 / D_MODEL * jnp.sum(lsum))
    packed, offs = _pack(pieces)
    red = _allreduce_small("allreduce_small", packed)
    shapes = [small[n].shape for n in names] + [()]
    red = _unpack(red, offs, shapes)
    loss = red[-1]
    sg = dict(zip(names, red[:-1]))

    sums, after = {}, dx
    for idx, (l, keys, ex) in enumerate(pending):
        srcs, lands = _exchange_wait("scatter_wait%d" % idx, ex, after)
        for key, g, got in zip(keys, srcs, lands):
            mine = lax.dynamic_index_in_dim(g, me, axis=0, keepdims=False)
            sums[key, l] = after = _sum4("sum_%s_l%d" % (key, l), mine, got)
    order = [(key, l) for _, key in BIG for l in range(DEPTH)]
    theirs = dict(zip(order, _swap_sibling([sums[k] for k in order])))

    out = {}
    for name, key in BIG:
        shp = A[name].shape
        flat = lambda a: a.reshape(shp[0] * shp[1], shp[2])
        res = _adamw_layers("adamw_" + key, flat(A[name]), [(sums[key, l], theirs[key, l]) for l in range(DEPTH)],
                            flat(A["m_" + name]), flat(A["v_" + name]))
        out[name] = [r.reshape(shp) for r in res]

    wp, offs = _pack([A[n] for n in SMALL])
    gp, _ = _pack([sg[n] for n in SMALL])
    mp, _ = _pack([A["m_" + n] for n in SMALL])
    vp, _ = _pack([A["v_" + n] for n in SMALL])
    res = _adamw("adamw_small", wp, [gp], mp, vp)
    shapes = [A[n].shape for n in SMALL]
    res = [_unpack(r, offs, shapes) for r in res]
    for i, n in enumerate(SMALL):
        out[n] = [res[q][i] for q in range(4)]
    gcw = lax.dynamic_slice_in_dim(sg["conv_w"], me * CONV_SH, CONV_SH, axis=2)
    flat = lambda a: a.reshape(DEPTH * CONV_K, CONV_SH)
    res = _adamw("adamw_conv_w", flat(conv_w), [flat(gcw)], flat(m_conv_w), flat(v_conv_w))
    out["conv_w"] = [r.reshape(conv_w.shape) for r in res]

    outs = [loss, dx.reshape(B, S, D_MODEL)]
    for q in range(4):
        outs += [out[n][q] for n in WEIGHTS]
    return tuple(outs)
```

```python
import functools
import math

import numpy as np
import jax
import jax.numpy as jnp
from jax import lax
from jax.experimental import pallas as pl
from jax.experimental.pallas import tpu as pltpu

F32 = jnp.float32
BF16 = jnp.bfloat16

D_MODEL = 1024
DEPTH = 2
N_SHARD = 4
D_FF = 2816
FF_SH = D_FF // N_SHARD
SSD_HEADS = 16
HEAD_DIM = 64
SSD_GROUPS = 4
GROUP_W = 256
SSD_STATE = 128
CONV_K = 4
CONV_DIM = 2048
ATT_HEADS = 16
MIX_W = 2048
MIX_SH = MIX_W // N_SHARD
IN_PROJ = 6160
IN_SH = IN_PROJ // N_SHARD
IN_PAD = 6272
PROJ_TN = 896
COL_Z, COL_XBC, COL_Q, COL_K, COL_V, COL_DT = 0, 1024, 3072, 4096, 5120, 6144
EPS = 1e-6
NEG = -1e30
SSD_L = 256
ATT_B = 256
ROW_T = 512
CONV_CT = 256
CONV_R = 256
PAD_R = 8

ADAM_LR, ADAM_B1, ADAM_B2, ADAM_EPS, ADAM_WD, ADAM_STEP = 0.001, 0.9, 0.999, 1e-08, 0.01, 10

NN = (((1,), (0,)), ((), ()))
NT = (((1,), (1,)), ((), ()))
TN = (((0,), (0,)), ((), ()))

VMEM_LIMIT = 56 * 1024 * 1024


def _cp(*sem):
    return pltpu.CompilerParams(dimension_semantics=sem, vmem_limit_bytes=VMEM_LIMIT)


def _dot(a, b, dims):
    return lax.dot_general(a, b, dims, preferred_element_type=F32)


def _sigmoid(x):
    return 1.0 / (1.0 + jnp.exp(-x))


def _softplus(x):
    return jnp.maximum(x, 0.0) + jnp.log(1.0 + jnp.exp(-jnp.abs(x)))


def _mm(name, pairs, out_shape, out_spec, grid, dims, acc_shape, res=None, scale=1.0):
    nk = grid[2]
    npair = len(pairs)

    def body(*refs):
        ab = refs[:2 * npair]
        pos = 2 * npair
        res_ref = None
        if res is not None:
            res_ref = refs[pos]
            pos += 1
        out_ref, acc = refs[pos], refs[pos + 1]
        k = pl.program_id(2)

        @pl.when(k == 0)
        def _():
            acc[...] = jnp.zeros_like(acc)

        s = None
        for p in range(npair):
            d = _dot(ab[2 * p][...].astype(BF16), ab[2 * p + 1][...].astype(BF16), dims)
            s = d if s is None else s + d
        acc[...] += s

        @pl.when(k == nk - 1)
        def _():
            r = acc[...]
            if scale != 1.0:
                r = r * scale
            if res_ref is not None:
                r = r + res_ref[...]
            out_ref[...] = r.astype(out_ref.dtype)

    args, specs = [], []
    for a, a_spec, b, b_spec in pairs:
        args += [a, b]
        specs += [a_spec, b_spec]
    if res is not None:
        args.append(res[0])
        specs.append(res[1])
    return pl.pallas_call(
        body, out_shape=out_shape, grid=grid, in_specs=specs, out_specs=out_spec,
        scratch_shapes=[pltpu.VMEM(acc_shape, F32)], name=name,
        compiler_params=_cp("parallel", "parallel", "arbitrary"))(*args)


def _rms_fwd(name, x, w):
    T = x.shape[0]

    def body(x_ref, w_ref, o_ref):
        xv = x_ref[...]
        r = lax.rsqrt(jnp.mean(xv * xv, axis=-1, keepdims=True) + EPS)
        o_ref[...] = (xv * r * w_ref[...]).astype(BF16)

    return pl.pallas_call(
        body, out_shape=jax.ShapeDtypeStruct((T, D_MODEL), BF16), grid=(T // ROW_T,),
        in_specs=[pl.BlockSpec((ROW_T, D_MODEL), lambda i: (i, 0)), pl.BlockSpec((1, D_MODEL), lambda i: (0, 0))],
        out_specs=pl.BlockSpec((ROW_T, D_MODEL), lambda i: (i, 0)), name=name, compiler_params=_cp("parallel"))(x, w)


def _rms_bwd(name, dh, x, w, dres):
    T = x.shape[0]

    def body(dh_ref, x_ref, w_ref, dres_ref, dx_ref, dw_ref):
        @pl.when(pl.program_id(0) == 0)
        def _():
            dw_ref[...] = jnp.zeros_like(dw_ref)

        xv = x_ref[...]
        r = lax.rsqrt(jnp.mean(xv * xv, axis=-1, keepdims=True) + EPS)
        xhat = xv * r
        dhv = dh_ref[...]
        dxhat = dhv * w_ref[...]
        m = jnp.mean(dxhat * xhat, axis=-1, keepdims=True)
        dx_ref[...] = dres_ref[...] + r * (dxhat - xhat * m)
        dw_ref[...] += jnp.sum(dhv * xhat, axis=0, keepdims=True)

    row = pl.BlockSpec((ROW_T, D_MODEL), lambda i: (i, 0))
    vec = pl.BlockSpec((1, D_MODEL), lambda i: (0, 0))
    return pl.pallas_call(
        body, out_shape=(jax.ShapeDtypeStruct((T, D_MODEL), F32), jax.ShapeDtypeStruct((1, D_MODEL), F32)),
        grid=(T // ROW_T,), in_specs=[row, row, vec, row], out_specs=(row, vec), name=name,
        compiler_params=_cp("arbitrary"))(dh, x, w, dres)


def _loss_grad(name, y, t):
    T = y.shape[0]

    def body(y_ref, t_ref, dy_ref, l_ref):
        @pl.when(pl.program_id(0) == 0)
        def _():
            l_ref[...] = jnp.zeros_like(l_ref)

        e = y_ref[...] - t_ref[...]
        dy_ref[...] = e * (1.0 / D_MODEL)
        l_ref[...] += jnp.sum(e * e, axis=0, keepdims=True)

    row = pl.BlockSpec((ROW_T, D_MODEL), lambda i: (i, 0))
    vec = pl.BlockSpec((1, D_MODEL), lambda i: (0, 0))
    return pl.pallas_call(
        body, out_shape=(jax.ShapeDtypeStruct((T, D_MODEL), F32), jax.ShapeDtypeStruct((1, D_MODEL), F32)),
        grid=(T // ROW_T,), in_specs=[row, row], out_specs=(row, vec), name=name,
        compiler_params=_cp("arbitrary"))(y, t)


def _ffn_gate_up(name, h, wg, wu):
    T = h.shape[0]

    def body(h_ref, wg_ref, wu_ref, g_ref, u_ref, a_ref):
        hv = h_ref[...]
        g = _dot(hv, wg_ref[...], NN)
        u = _dot(hv, wu_ref[...], NN)
        g_ref[...] = g.astype(BF16)
        u_ref[...] = u.astype(BF16)
        a_ref[...] = (g * _sigmoid(g) * u).astype(BF16)

    wspec = pl.BlockSpec((None, D_MODEL, FF_SH), lambda j, i: (j, 0, 0))
    ospec = pl.BlockSpec((None, ROW_T, FF_SH), lambda j, i: (j, i, 0))
    osh = jax.ShapeDtypeStruct((N_SHARD, T, FF_SH), BF16)
    return pl.pallas_call(
        body, out_shape=(osh, osh, osh), grid=(N_SHARD, T // ROW_T),
        in_specs=[pl.BlockSpec((ROW_T, D_MODEL), lambda j, i: (i, 0)), wspec, wspec],
        out_specs=(ospec, ospec, ospec), name=name, compiler_params=_cp("parallel", "parallel"))(h, wg, wu)


def _ffn_dact(name, dx, wd, g, u):
    T = dx.shape[0]

    def body(dx_ref, wd_ref, g_ref, u_ref, dg_ref, du_ref):
        da = 0.5 * _dot(dx_ref[...].astype(BF16), wd_ref[...], NT)
        gv = g_ref[...].astype(F32)
        uv = u_ref[...].astype(F32)
        sg = _sigmoid(gv)
        dg_ref[...] = (da * uv * (sg * (1.0 + gv * (1.0 - sg)))).astype(BF16)
        du_ref[...] = (da * gv * sg).astype(BF16)

    aspec = pl.BlockSpec((None, ROW_T, FF_SH), lambda j, i: (j, i, 0))
    osh = jax.ShapeDtypeStruct((N_SHARD, T, FF_SH), BF16)
    return pl.pallas_call(
        body, out_shape=(osh, osh), grid=(N_SHARD, T // ROW_T),
        in_specs=[pl.BlockSpec((ROW_T, D_MODEL), lambda j, i: (i, 0)),
                  pl.BlockSpec((None, FF_SH, D_MODEL), lambda j, i: (j, 0, 0)), aspec, aspec],
        out_specs=(aspec, aspec), name=name, compiler_params=_cp("parallel", "parallel"))(dx, wd, g, u)


def _ffn_fwd(tag, x, nw, wg, wu, wd):
    T = x.shape[0]
    h = _rms_fwd(tag + "_rms", x, nw)
    g, u, a = _ffn_gate_up(tag + "_gu", h, wg, wu)
    nt = T // ROW_T
    xo = _mm(tag + "_down",
             [(a, pl.BlockSpec((None, ROW_T, FF_SH), lambda i, n, k: (k, i, 0)),
               wd, pl.BlockSpec((None, FF_SH, D_MODEL), lambda i, n, k: (k, 0, 0)))],
             jax.ShapeDtypeStruct((T, D_MODEL), F32), pl.BlockSpec((ROW_T, D_MODEL), lambda i, n, k: (i, 0)),
             (nt, 1, N_SHARD), NN, (ROW_T, D_MODEL),
             res=(x, pl.BlockSpec((ROW_T, D_MODEL), lambda i, n, k: (i, 0))), scale=0.5)
    return xo, (x, h, g, u, a)


def _ffn_bwd(tag, dxo, saved, nw, wg, wu, wd, emit):
    x, h, g, u, a = saved
    T = x.shape[0]
    nt = T // ROW_T
    dg, du = _ffn_dact(tag + "_dact", dxo, wd, g, u)
    act = lambda f: pl.BlockSpec((None, ROW_T, FF_SH), f)
    gd = _mm(tag + "_dwd",
             [(a, act(lambda m, n, k: (m, k, 0)), dxo, pl.BlockSpec((ROW_T, D_MODEL), lambda m, n, k: (k, 0)))],
             jax.ShapeDtypeStruct((N_SHARD, FF_SH, D_MODEL), BF16),
             pl.BlockSpec((None, FF_SH, D_MODEL), lambda m, n, k: (m, 0, 0)),
             (N_SHARD, 1, nt), TN, (FF_SH, D_MODEL), scale=0.5)
    hspec = pl.BlockSpec((ROW_T, D_MODEL), lambda j, n, k: (k, 0))
    gsh = jax.ShapeDtypeStruct((N_SHARD, D_MODEL, FF_SH), BF16)
    gspec = pl.BlockSpec((None, D_MODEL, FF_SH), lambda j, n, k: (j, 0, 0))
    gg = _mm(tag + "_dwg", [(h, hspec, dg, act(lambda j, n, k: (j, k, 0)))], gsh, gspec,
             (N_SHARD, 1, nt), TN, (D_MODEL, FF_SH))
    gu = _mm(tag + "_dwu", [(h, hspec, du, act(lambda j, n, k: (j, k, 0)))], gsh, gspec,
             (N_SHARD, 1, nt), TN, (D_MODEL, FF_SH))
    dg = emit(gg, gu, gd, dg)
    wspec = pl.BlockSpec((None, D_MODEL, FF_SH), lambda i, n, k: (k, 0, 0))
    dh = _mm(tag + "_dh",
             [(dg, act(lambda i, n, k: (k, i, 0)), wg, wspec), (du, act(lambda i, n, k: (k, i, 0)), wu, wspec)],
             jax.ShapeDtypeStruct((T, D_MODEL), F32), pl.BlockSpec((ROW_T, D_MODEL), lambda i, n, k: (i, 0)),
             (nt, 1, N_SHARD), NT, (ROW_T, D_MODEL))
    return _rms_bwd(tag + "_rmsb", dh, x, nw, dxo)


def _conv_fwd(name, xpad, w, b):
    B, SP, C = xpad.shape
    S = SP - 2 * PAD_R

    def body(x_ref, w_ref, b_ref, o_ref):
        wv = w_ref[...]
        for c in range(S // CONV_R):
            r0 = c * CONV_R
            ch = x_ref[pl.ds(r0, CONV_R + PAD_R), :]
            pre = ch[PAD_R:] * wv[3:4] + b_ref[...]
            for s in range(1, CONV_K):
                pre = pre + pltpu.roll(ch, s, axis=0)[PAD_R:] * wv[3 - s:4 - s]
            o_ref[pl.ds(r0, CONV_R), :] = pre * _sigmoid(pre)

    return pl.pallas_call(
        body, out_shape=jax.ShapeDtypeStruct((B, S, C), F32), grid=(B, C // CONV_CT),
        in_specs=[pl.BlockSpec((None, SP, CONV_CT), lambda bi, ci: (bi, 0, ci)),
                  pl.BlockSpec((CONV_K, CONV_CT), lambda bi, ci: (0, ci)),
                  pl.BlockSpec((1, CONV_CT), lambda bi, ci: (0, ci))],
        out_specs=pl.BlockSpec((None, S, CONV_CT), lambda bi, ci: (bi, 0, ci)), name=name,
        compiler_params=_cp("parallel", "parallel"))(xpad, w, b)


def _conv_bwd(name, xpad, dxc_pad, w, b):
    B, SP, C = xpad.shape
    S = SP - 2 * PAD_R
    RW = CONV_R + PAD_R

    def body(x_ref, d_ref, w_ref, b_ref, dx_ref, dw_ref, db_ref):
        @pl.when(pl.program_id(1) == 0)
        def _():
            dw_ref[...] = jnp.zeros_like(dw_ref)
            db_ref[...] = jnp.zeros_like(db_ref)

        wv = w_ref[...]
        dw = [jnp.zeros((1, CONV_CT), F32) for _ in range(CONV_K)]
        db = jnp.zeros((1, CONV_CT), F32)
        for c in range(S // CONV_R):
            r0 = c * CONV_R
            ch = x_ref[pl.ds(r0, RW + PAD_R), :]
            xs = [ch[PAD_R:]] + [pltpu.roll(ch, s, axis=0)[PAD_R:] for s in range(1, CONV_K)]
            pre = b_ref[...] + xs[0] * wv[3:4]
            for s in range(1, CONV_K):
                pre = pre + xs[s] * wv[3 - s:4 - s]
            sg = _sigmoid(pre)
            dpre = d_ref[pl.ds(r0, RW), :] * (sg * (1.0 + pre * (1.0 - sg)))
            dx = dpre[:CONV_R] * wv[3:4]
            for s in range(1, CONV_K):
                dx = dx + pltpu.roll(dpre, RW - s, axis=0)[:CONV_R] * wv[3 - s:4 - s]
            dx_ref[pl.ds(r0, CONV_R), :] = dx
            dcur = dpre[:CONV_R]
            db = db + jnp.sum(dcur, axis=0, keepdims=True)
            for s in range(CONV_K):
                dw[3 - s] = dw[3 - s] + jnp.sum(dcur * xs[s][:CONV_R], axis=0, keepdims=True)
        db_ref[...] += db
        for k in range(CONV_K):
            dw_ref[k:k + 1, :] += dw[k]

    return pl.pallas_call(
        body,
        out_shape=(jax.ShapeDtypeStruct((B, S, C), F32), jax.ShapeDtypeStruct((CONV_K, C), F32),
                   jax.ShapeDtypeStruct((1, C), F32)),
        grid=(C // CONV_CT, B),
        in_specs=[pl.BlockSpec((None, SP, CONV_CT), lambda ci, bi: (bi, 0, ci)),
                  pl.BlockSpec((None, S + PAD_R, CONV_CT), lambda ci, bi: (bi, 0, ci)),
                  pl.BlockSpec((CONV_K, CONV_CT), lambda ci, bi: (0, ci)),
                  pl.BlockSpec((1, CONV_CT), lambda ci, bi: (0, ci))],
        out_specs=(pl.BlockSpec((None, S, CONV_CT), lambda ci, bi: (bi, 0, ci)),
                   pl.BlockSpec((CONV_K, CONV_CT), lambda ci, bi: (0, ci)),
                   pl.BlockSpec((1, CONV_CT), lambda ci, bi: (0, ci))),
        name=name, compiler_params=_cp("parallel", "arbitrary"))(xpad, dxc_pad, w, b)


def _ssd_common(dtc_ref, dtr_ref, pcol_ref, prow_ref, b_ref, c_ref):
    L = SSD_L
    bias_c, alog_c = pcol_ref[0:1, :], pcol_ref[1:2, :]
    a_c = -jnp.exp(alog_c)
    dt_c = _softplus(dtc_ref[...] + bias_c)
    row = lax.broadcasted_iota(jnp.int32, (L, L), 0)
    col = lax.broadcasted_iota(jnp.int32, (L, L), 1)
    causal = row >= col
    tri = causal.astype(F32)
    hp = lax.Precision.HIGHEST
    cum_c = lax.dot_general(tri, dt_c * a_c, NN, precision=hp, preferred_element_type=F32)
    a_r = -jnp.exp(prow_ref[:, 1:2])
    dt_r = _softplus(dtr_ref[...] + prow_ref[:, 0:1])
    cum_r = lax.dot_general(dt_r * a_r, tri, NT, precision=hp, preferred_element_type=F32)
    bb = b_ref[...].astype(BF16)
    cb = c_ref[...].astype(BF16)
    G = _dot(cb, bb, NT)
    return a_c, dt_c, causal, tri, cum_c, cum_r, bb, cb, G


def _ssd_fwd(name, xc, proj, dtc, dtr, pcol, prow, nw, B):
    T = xc.shape[0]
    S = T // B
    nb = S // SSD_L
    L = SSD_L

    def body(xs_ref, b_ref, c_ref, z_ref, dtc_ref, dtr_ref, pcol_ref, prow_ref, nw_ref, y_ref, yn_ref, hs_ref, H):
        @pl.when(pl.program_id(2) == 0)
        def _():
            H[...] = jnp.zeros_like(H)

        a_c, dt_c, causal, tri, cum_c, cum_r, bb, cb, G = _ssd_common(dtc_ref, dtr_ref, pcol_ref, prow_ref, b_ref, c_ref)
        dsk = pcol_ref[2:3, :]
        clast = cum_c[L - 1:L, :]
        bf = b_ref[...]
        for h in range(4):
            sl = slice(HEAD_DIM * h, HEAD_DIM * (h + 1))
            cc = cum_c[:, h:h + 1]
            lm = jnp.exp(jnp.where(causal, cc - cum_r[h:h + 1, :], NEG))
            M = (G * lm).astype(BF16)
            xh = xs_ref[:, sl]
            Xb = (xh * dt_c[:, h:h + 1]).astype(BF16)
            Hh = H[h]
            y = _dot(M, Xb, NN) + jnp.exp(cc) * _dot(cb, Hh.astype(BF16), NN)
            y_ref[:, sl] = y + dsk[:, h:h + 1] * xh
            hs_ref[h] = Hh
            cl = clast[:, h:h + 1]
            Bw = (bf * jnp.exp(cl - cc)).astype(BF16)
            H[h] = jnp.exp(cl) * Hh + _dot(Bw, Xb, TN)
        zv = z_ref[...]
        y2 = y_ref[...] * (zv * _sigmoid(zv))
        r = lax.rsqrt(jnp.mean(y2 * y2, axis=-1, keepdims=True) + EPS)
        yn_ref[...] = (y2 * r * nw_ref[...]).astype(BF16)

    rowi = lambda b, g, i: b * nb + i
    grp = pl.BlockSpec((L, GROUP_W), lambda b, g, i: (rowi(b, g, i), g))
    return pl.pallas_call(
        body,
        out_shape=(jax.ShapeDtypeStruct((T, 1024), F32), jax.ShapeDtypeStruct((T, 1024), BF16),
                   jax.ShapeDtypeStruct((B, SSD_GROUPS, nb, 4, SSD_STATE, HEAD_DIM), F32)),
        grid=(B, SSD_GROUPS, nb),
        in_specs=[grp,
                  pl.BlockSpec((L, SSD_STATE), lambda b, g, i: (rowi(b, g, i), 8 + g)),
                  pl.BlockSpec((L, SSD_STATE), lambda b, g, i: (rowi(b, g, i), 12 + g)),
                  grp,
                  pl.BlockSpec((None, L, 4), lambda b, g, i: (g, rowi(b, g, i), 0)),
                  pl.BlockSpec((None, 4, L), lambda b, g, i: (g, 0, rowi(b, g, i))),
                  pl.BlockSpec((None, 3, 4), lambda b, g, i: (g, 0, 0)),
                  pl.BlockSpec((None, 4, 3), lambda b, g, i: (g, 0, 0)),
                  pl.BlockSpec((1, GROUP_W), lambda b, g, i: (0, g))],
        out_specs=(grp, grp,
                   pl.BlockSpec((None, None, None, 4, SSD_STATE, HEAD_DIM), lambda b, g, i: (b, g, i, 0, 0, 0))),
        scratch_shapes=[pltpu.VMEM((4, SSD_STATE, HEAD_DIM), F32)], name=name,
        compiler_params=_cp("parallel", "parallel", "arbitrary"))(xc, xc, xc, proj, dtc, dtr, pcol, prow, nw)


def _ssd_bwd(name, dyn, Y, xc, proj, dtc, dtr, pcol, prow, nw, hs, B):
    T = xc.shape[0]
    S = T // B
    nb = S // SSD_L
    L = SSD_L

    def body(dyn_ref, y_ref, xs_ref, b_ref, c_ref, z_ref, dtc_ref, dtr_ref, pcol_ref, prow_ref, nw_ref, hs_ref,
             dxs_ref, db_ref, dc_ref, dz_ref, ddt_ref, dpar_ref, dnw_ref, dH):
        @pl.when(pl.program_id(2) == 0)
        def _():
            dH[...] = jnp.zeros_like(dH)
            dpar_ref[...] = jnp.zeros_like(dpar_ref)
            dnw_ref[...] = jnp.zeros_like(dnw_ref)

        a_c, dt_c, causal, tri, cum_c, cum_r, bb, cb, G = _ssd_common(dtc_ref, dtr_ref, pcol_ref, prow_ref, b_ref, c_ref)
        dsk = pcol_ref[2:3, :]
        clast = cum_c[L - 1:L, :]
        bf = b_ref[...]
        cf = c_ref[...]
        Yv = y_ref[...]
        zv = z_ref[...]
        sz = _sigmoid(zv)
        silu = zv * sz
        y2 = Yv * silu
        r = lax.rsqrt(jnp.mean(y2 * y2, axis=-1, keepdims=True) + EPS)
        yhat = y2 * r
        dyv = dyn_ref[...]
        dnw_ref[...] += jnp.sum(dyv * yhat, axis=0, keepdims=True)
        dyhat = dyv * nw_ref[...]
        dy2 = r * (dyhat - yhat * jnp.mean(dyhat * yhat, axis=-1, keepdims=True))
        dY = dy2 * silu
        dz_ref[...] = dy2 * Yv * (sz * (1.0 + zv * (1.0 - sz)))

        lane4 = lax.broadcasted_iota(jnp.int32, (1, 4), 1)
        dG = jnp.zeros((L, L), F32)
        dBs = jnp.zeros((L, SSD_STATE), F32)
        dCs = jnp.zeros((L, SSD_STATE), F32)
        dA = jnp.zeros((L, 4), F32)
        ddtx = jnp.zeros((L, 4), F32)
        ddsk = jnp.zeros((1, 4), F32)
        dcl = jnp.zeros((1, 4), F32)
        for h in range(4):
            sl = slice(HEAD_DIM * h, HEAD_DIM * (h + 1))
            onehot = (lane4 == h).astype(F32)
            cc = cum_c[:, h:h + 1]
            cl = clast[:, h:h + 1]
            lm = jnp.exp(jnp.where(causal, cc - cum_r[h:h + 1, :], NEG))
            M = (G * lm).astype(BF16)
            xh = xs_ref[:, sl]
            dth = dt_c[:, h:h + 1]
            X = xh * dth
            Xb = X.astype(BF16)
            dYh = dY[:, sl]
            dYb = dYh.astype(BF16)
            Hb = hs_ref[h].astype(BF16)
            dHh = dH[h]
            dHb = dHh.astype(BF16)
            alpha = jnp.exp(cc)
            beta = jnp.exp(cl - cc)
            dXoff = beta * _dot(bb, dHb, NN)
            dX = _dot(M, dYb, TN) + dXoff
            dG = dG + _dot(dYb, Xb, NT) * lm
            dCs = dCs + _dot((alpha * dYh).astype(BF16), Hb, NT)
            dBs = dBs + _dot((beta * X).astype(BF16), dHb, NT)
            ypre = Yv[:, sl] - dsk[:, h:h + 1] * xh
            dA_h = (jnp.sum(dYb.astype(F32) * ypre, axis=-1, keepdims=True)
                    - jnp.sum(Xb.astype(F32) * dX, axis=-1, keepdims=True))
            dA = dA + dA_h * onehot
            dcl_h = (jnp.sum(jnp.sum(dHh * (jnp.exp(cl) * hs_ref[h]), axis=-1, keepdims=True), axis=0, keepdims=True)
                     + jnp.sum(jnp.sum(Xb.astype(F32) * dXoff, axis=-1, keepdims=True), axis=0, keepdims=True))
            dcl = dcl + dcl_h * onehot
            ddtx = ddtx + jnp.sum(dX * xh, axis=-1, keepdims=True) * onehot
            ddsk = ddsk + jnp.sum(jnp.sum(dYh * xh, axis=-1, keepdims=True), axis=0, keepdims=True) * onehot
            dxs_ref[:, sl] = dsk[:, h:h + 1] * dYh + dX * dth
            dH[h] = jnp.exp(cl) * dHh + _dot((alpha * cf).astype(BF16), dYb, TN)
        dGb = dG.astype(BF16)
        dc_ref[...] = _dot(dGb, bb, NN) + dCs
        db_ref[...] = _dot(dGb, cb, TN) + dBs
        hp = lax.Precision.HIGHEST
        last = lax.broadcasted_iota(jnp.int32, (L, 1), 0) == L - 1
        dA = dA + jnp.where(last, dcl, 0.0)
        dadt = lax.dot_general(tri, dA, TN, precision=hp, preferred_element_type=F32)
        ddt = dadt * a_c + ddtx
        d_a = jnp.sum(dadt * dt_c, axis=0, keepdims=True)
        ddraw = ddt * _sigmoid(dtc_ref[...] + pcol_ref[0:1, :])
        ddt_ref[...] = ddraw
        dpar_ref[0:1, :] += jnp.sum(ddraw, axis=0, keepdims=True)
        dpar_ref[1:2, :] += d_a * a_c
        dpar_ref[2:3, :] += ddsk

    rowi = lambda b, g, i: b * nb + (nb - 1 - i)
    grp = pl.BlockSpec((L, GROUP_W), lambda b, g, i: (rowi(b, g, i), g))
    st = pl.BlockSpec((L, SSD_STATE), lambda b, g, i: (rowi(b, g, i), g))
    f = jax.ShapeDtypeStruct
    return pl.pallas_call(
        body,
        out_shape=(f((T, 1024), F32), f((T, 512), F32), f((T, 512), F32), f((T, 1024), F32),
                   f((SSD_GROUPS, T, 4), F32), f((B, SSD_GROUPS, 3, 4), F32), f((B, 1, 1024), F32)),
        grid=(B, SSD_GROUPS, nb),
        in_specs=[grp, grp, grp,
                  pl.BlockSpec((L, SSD_STATE), lambda b, g, i: (rowi(b, g, i), 8 + g)),
                  pl.BlockSpec((L, SSD_STATE), lambda b, g, i: (rowi(b, g, i), 12 + g)),
                  grp,
                  pl.BlockSpec((None, L, 4), lambda b, g, i: (g, rowi(b, g, i), 0)),
                  pl.BlockSpec((None, 4, L), lambda b, g, i: (g, 0, rowi(b, g, i))),
                  pl.BlockSpec((None, 3, 4), lambda b, g, i: (g, 0, 0)),
                  pl.BlockSpec((None, 4, 3), lambda b, g, i: (g, 0, 0)),
                  pl.BlockSpec((1, GROUP_W), lambda b, g, i: (0, g)),
                  pl.BlockSpec((None, None, None, 4, SSD_STATE, HEAD_DIM), lambda b, g, i: (b, g, nb - 1 - i, 0, 0, 0))],
        out_specs=(grp, st, st, grp,
                   pl.BlockSpec((None, L, 4), lambda b, g, i: (g, rowi(b, g, i), 0)),
                   pl.BlockSpec((None, None, 3, 4), lambda b, g, i: (b, g, 0, 0)),
                   pl.BlockSpec((None, 1, GROUP_W), lambda b, g, i: (b, 0, g))),
        scratch_shapes=[pltpu.VMEM((4, SSD_STATE, HEAD_DIM), F32)], name=name,
        compiler_params=_cp("parallel", "parallel", "arbitrary"))(dyn, Y, xc, xc, xc, proj, dtc, dtr, pcol, prow, nw, hs)


def _headnorm_fwd(name, proj, col_block, w):
    T = proj.shape[0]

    def body(x_ref, w_ref, o_ref):
        for h in range(ATT_HEADS):
            sl = slice(HEAD_DIM * h, HEAD_DIM * (h + 1))
            xh = x_ref[:, sl]
            r = lax.rsqrt(jnp.mean(xh * xh, axis=-1, keepdims=True) + EPS)
            o_ref[:, sl] = (xh * r * w_ref[...]).astype(BF16)

    return pl.pallas_call(
        body, out_shape=jax.ShapeDtypeStruct((T, 1024), BF16), grid=(T // ROW_T,),
        in_specs=[pl.BlockSpec((ROW_T, 1024), lambda i: (i, col_block)), pl.BlockSpec((1, HEAD_DIM), lambda i: (0, 0))],
        out_specs=pl.BlockSpec((ROW_T, 1024), lambda i: (i, 0)), name=name, compiler_params=_cp("parallel"))(proj, w)


def _headnorm_bwd(name, dn, proj, col_block, w):
    T = proj.shape[0]

    def body(dn_ref, x_ref, w_ref, dx_ref, dw_ref):
        @pl.when(pl.program_id(0) == 0)
        def _():
            dw_ref[...] = jnp.zeros_like(dw_ref)

        dw = jnp.zeros((1, HEAD_DIM), F32)
        for h in range(ATT_HEADS):
            sl = slice(HEAD_DIM * h, HEAD_DIM * (h + 1))
            xh = x_ref[:, sl]
            r = lax.rsqrt(jnp.mean(xh * xh, axis=-1, keepdims=True) + EPS)
            xhat = xh * r
            dnh = dn_ref[:, sl]
            dxhat = dnh * w_ref[...]
            dx_ref[:, sl] = r * (dxhat - xhat * jnp.mean(dxhat * xhat, axis=-1, keepdims=True))
            dw = dw + jnp.sum(dnh * xhat, axis=0, keepdims=True)
        dw_ref[...] += dw

    return pl.pallas_call(
        body, out_shape=(jax.ShapeDtypeStruct((T, 1024), F32), jax.ShapeDtypeStruct((1, HEAD_DIM), F32)),
        grid=(T // ROW_T,),
        in_specs=[pl.BlockSpec((ROW_T, 1024), lambda i: (i, 0)), pl.BlockSpec((ROW_T, 1024), lambda i: (i, col_block)),
                  pl.BlockSpec((1, HEAD_DIM), lambda i: (0, 0))],
        out_specs=(pl.BlockSpec((ROW_T, 1024), lambda i: (i, 0)), pl.BlockSpec((1, HEAD_DIM), lambda i: (0, 0))),
        name=name, compiler_params=_cp("arbitrary"))(dn, proj, w)


def _att_bias(nq):
    j = np.arange(ATT_B)[:, None]
    i = np.arange(ATT_B)[None, :]
    out = np.empty((nq, ATT_B, ATT_B), np.float32)
    for dblk in range(nq):
        dl = ATT_B * dblk + i - j
        cnt = ((dl >= 0) & (dl <= 128)).astype(np.float32)
        cnt += ((dl >= 0) & (dl % 4 == 0) & (dl <= 512))
        cnt += ((dl >= 0) & (dl % 16 == 0) & (dl <= 2048))
        out[dblk] = np.where(cnt > 0, np.log(np.maximum(cnt, 1.0)), NEG)
    return jnp.asarray(out)


def _row_pair(nq):
    def f(r, c):
        first = c <= r
        return jnp.where(first, r, nq - 1 - r), jnp.where(first, c, c - (r + 1))
    return f


def _col_pair(nq):
    def f(r, c):
        first = c < nq - r
        kj = jnp.where(first, r, nq - 1 - r)
        return jnp.where(first, r + c, nq - 1 - r + (c - (nq - r))), kj
    return f


ATT_SCALE = 1.0 / math.sqrt(HEAD_DIM)
ATT_HS = 4
ATT_W = ATT_HS * HEAD_DIM


def _att_maps(nq, qk):
    return dict(
        q_tok=lambda b, g, r, c: (b * nq + qk(r, c)[0], g),
        k_tok=lambda b, g, r, c: (b * nq + qk(r, c)[1], g),
        q_feat=lambda b, g, r, c: (g, b * nq + qk(r, c)[0]),
        k_feat=lambda b, g, r, c: (g, b * nq + qk(r, c)[1]),
        bias=lambda b, g, r, c: (qk(r, c)[0] - qk(r, c)[1], 0, 0),
        lse=lambda b, g, r, c: (g, 0, b * nq + qk(r, c)[0]),
        do_tok=lambda b, g, r, c: (b * nq + qk(r, c)[0], ATT_HS + g))


def _att_fwd(name, kn, qT, vT, bias, B):
    T = kn.shape[0]
    nq = (T // B) // ATT_B
    qk = _row_pair(nq)
    mp = _att_maps(nq, qk)

    def body(k_ref, qT_ref, vT_ref, bias_ref, oT_ref, lse_ref, m_s, l_s, acc_s, s_s):
        qi, kj = qk(pl.program_id(2), pl.program_id(3))

        @pl.when(kj == 0)
        def _():
            m_s[...] = jnp.full_like(m_s, NEG)
            l_s[...] = jnp.zeros_like(l_s)
            acc_s[...] = jnp.zeros_like(acc_s)

        bv = bias_ref[...]
        for h in range(ATT_HS):
            rs = slice(HEAD_DIM * h, HEAD_DIM * (h + 1))
            s_s[h] = _dot(k_ref[:, rs], qT_ref[rs, :], NN)
        for h in range(ATT_HS):
            rs = slice(HEAD_DIM * h, HEAD_DIM * (h + 1))
            s = s_s[h] + bv
            m_prev = m_s[h:h + 1, :]
            m_new = jnp.maximum(m_prev, jnp.max(s, axis=0, keepdims=True))
            alpha = jnp.exp(m_prev - m_new)
            p = jnp.exp(s - m_new)
            l_s[h:h + 1, :] = alpha * l_s[h:h + 1, :] + jnp.sum(p, axis=0, keepdims=True)
            acc_s[rs, :] = alpha * acc_s[rs, :] + _dot(vT_ref[rs, :], p.astype(BF16), NN)
            m_s[h:h + 1, :] = m_new

        @pl.when(kj == qi)
        def _():
            for h in range(ATT_HS):
                rs = slice(HEAD_DIM * h, HEAD_DIM * (h + 1))
                oT_ref[rs, :] = (acc_s[rs, :] / l_s[h:h + 1, :]).astype(BF16)
            lse_ref[...] = m_s[...] + jnp.log(l_s[...])

    tok = (ATT_B, ATT_W)
    feat = (ATT_W, ATT_B)
    return pl.pallas_call(
        body,
        out_shape=(jax.ShapeDtypeStruct((1024, T), BF16), jax.ShapeDtypeStruct((ATT_HEADS // ATT_HS, ATT_HS, T), F32)),
        grid=(B, ATT_HEADS // ATT_HS, nq // 2, nq + 1),
        in_specs=[pl.BlockSpec(tok, mp["k_tok"]), pl.BlockSpec(feat, mp["q_feat"]), pl.BlockSpec(feat, mp["k_feat"]),
                  pl.BlockSpec((None, ATT_B, ATT_B), mp["bias"])],
        out_specs=(pl.BlockSpec(feat, mp["q_feat"]), pl.BlockSpec((None, ATT_HS, ATT_B), mp["lse"])),
        scratch_shapes=[pltpu.VMEM((ATT_HS, ATT_B), F32), pltpu.VMEM((ATT_HS, ATT_B), F32),
                        pltpu.VMEM((ATT_W, ATT_B), F32), pltpu.VMEM((ATT_HS, ATT_B, ATT_B), F32)],
        name=name, compiler_params=_cp("parallel", "parallel", "arbitrary", "arbitrary"))(kn, qT, vT, bias)


def _att_scores(k_ref, qT_ref, v_ref, doT_ref, s_s, dp_s):
    for h in range(ATT_HS):
        rs = slice(HEAD_DIM * h, HEAD_DIM * (h + 1))
        s_s[h] = _dot(k_ref[:, rs], qT_ref[rs, :], NN)
        dp_s[h] = _dot(v_ref[:, rs], doT_ref[rs, :].astype(BF16), NN)


def _att_p_ds(s_s, dp_s, doT_ref, oT_ref, lse_ref, bv, h):
    rs = slice(HEAD_DIM * h, HEAD_DIM * (h + 1))
    delta = jnp.sum(doT_ref[rs, :] * oT_ref[rs, :].astype(F32), axis=0, keepdims=True)
    p = jnp.exp(s_s[h] + bv - lse_ref[h:h + 1, :])
    return p, p * (dp_s[h] - delta)


def _att_bwd_dq(name, kn, qT, vb, knT, bias, doT, oT, lse, B):
    T = kn.shape[0]
    nq = (T // B) // ATT_B
    qk = _row_pair(nq)
    mp = _att_maps(nq, qk)

    def body(k_ref, qT_ref, v_ref, kT_ref, bias_ref, doT_ref, oT_ref, lse_ref, dqT_ref, acc_s, s_s, dp_s):
        qi, kj = qk(pl.program_id(2), pl.program_id(3))

        @pl.when(kj == 0)
        def _():
            acc_s[...] = jnp.zeros_like(acc_s)

        bv = bias_ref[...]
        _att_scores(k_ref, qT_ref, v_ref, doT_ref, s_s, dp_s)
        for h in range(ATT_HS):
            rs = slice(HEAD_DIM * h, HEAD_DIM * (h + 1))
            p, ds = _att_p_ds(s_s, dp_s, doT_ref, oT_ref, lse_ref, bv, h)
            acc_s[rs, :] += _dot(kT_ref[rs, :], ds.astype(BF16), NN)

        @pl.when(kj == qi)
        def _():
            dqT_ref[...] = acc_s[...] * ATT_SCALE

    tok = (ATT_B, ATT_W)
    feat = (ATT_W, ATT_B)
    return pl.pallas_call(
        body, out_shape=jax.ShapeDtypeStruct((1024, T), F32), grid=(B, ATT_HEADS // ATT_HS, nq // 2, nq + 1),
        in_specs=[pl.BlockSpec(tok, mp["k_tok"]), pl.BlockSpec(feat, mp["q_feat"]), pl.BlockSpec(tok, mp["k_tok"]),
                  pl.BlockSpec(feat, mp["k_feat"]), pl.BlockSpec((None, ATT_B, ATT_B), mp["bias"]),
                  pl.BlockSpec(feat, mp["q_feat"]), pl.BlockSpec(feat, mp["q_feat"]),
                  pl.BlockSpec((None, ATT_HS, ATT_B), mp["lse"])],
        out_specs=pl.BlockSpec(feat, mp["q_feat"]),
        scratch_shapes=[pltpu.VMEM((ATT_W, ATT_B), F32), pltpu.VMEM((ATT_HS, ATT_B, ATT_B), F32),
                        pltpu.VMEM((ATT_HS, ATT_B, ATT_B), F32)],
        name=name, compiler_params=_cp("parallel", "parallel", "arbitrary", "arbitrary"))(
            kn, qT, vb, knT, bias, doT, oT, lse)


def _att_bwd_dkv(name, kn, qT, vb, qn, bias, doT, oT, lse, dyn, B):
    T = kn.shape[0]
    nq = (T // B) // ATT_B
    qk = _col_pair(nq)
    mp = _att_maps(nq, qk)

    def body(k_ref, qT_ref, v_ref, q_ref, bias_ref, doT_ref, oT_ref, lse_ref, do_ref, dk_ref, dv_ref, dk_s, dv_s,
             s_s, dp_s):
        qi, kj = qk(pl.program_id(2), pl.program_id(3))

        @pl.when(qi == kj)
        def _():
            dk_s[...] = jnp.zeros_like(dk_s)
            dv_s[...] = jnp.zeros_like(dv_s)

        bv = bias_ref[...]
        _att_scores(k_ref, qT_ref, v_ref, doT_ref, s_s, dp_s)
        for h in range(ATT_HS):
            rs = slice(HEAD_DIM * h, HEAD_DIM * (h + 1))
            p, ds = _att_p_ds(s_s, dp_s, doT_ref, oT_ref, lse_ref, bv, h)
            dv_s[h] += _dot(p.astype(BF16), do_ref[:, rs].astype(BF16), NN)
            dk_s[h] += _dot(ds.astype(BF16), q_ref[:, rs], NN)

        @pl.when(qi == nq - 1)
        def _():
            for h in range(ATT_HS):
                rs = slice(HEAD_DIM * h, HEAD_DIM * (h + 1))
                dk_ref[:, rs] = dk_s[h] * ATT_SCALE
                dv_ref[:, rs] = dv_s[h]

    tok = (ATT_B, ATT_W)
    feat = (ATT_W, ATT_B)
    osh = jax.ShapeDtypeStruct((T, 1024), F32)
    return pl.pallas_call(
        body, out_shape=(osh, osh), grid=(B, ATT_HEADS // ATT_HS, nq // 2, nq + 1),
        in_specs=[pl.BlockSpec(tok, mp["k_tok"]), pl.BlockSpec(feat, mp["q_feat"]), pl.BlockSpec(tok, mp["k_tok"]),
                  pl.BlockSpec(tok, mp["q_tok"]), pl.BlockSpec((None, ATT_B, ATT_B), mp["bias"]),
                  pl.BlockSpec(feat, mp["q_feat"]), pl.BlockSpec(feat, mp["q_feat"]),
                  pl.BlockSpec((None, ATT_HS, ATT_B), mp["lse"]), pl.BlockSpec(tok, mp["do_tok"])],
        out_specs=(pl.BlockSpec(tok, mp["k_tok"]), pl.BlockSpec(tok, mp["k_tok"])),
        scratch_shapes=[pltpu.VMEM((ATT_HS, ATT_B, HEAD_DIM), F32), pltpu.VMEM((ATT_HS, ATT_B, HEAD_DIM), F32),
                        pltpu.VMEM((ATT_HS, ATT_B, ATT_B), F32), pltpu.VMEM((ATT_HS, ATT_B, ATT_B), F32)],
        name=name, compiler_params=_cp("parallel", "parallel", "arbitrary", "arbitrary"))(
            kn, qT, vb, qn, bias, doT, oT, lse, dyn)


def _group_cols(v):
    return v.reshape(SSD_GROUPS, 4)


def _ssd_params(p):
    rows = jnp.stack([_group_cols(p["dt_bias"]), _group_cols(p["a_log"]), _group_cols(p["d_skip"])], axis=1)
    return rows, jnp.swapaxes(rows, 1, 2)


def _mixer_fwd(tag, x1, p, weights, bias, B):
    T = x1.shape[0]
    S = T // B
    nt = T // ROW_T
    h2 = _rms_fwd(tag + "_mixrms", x1, p["mix_norm"][None])
    win = weights("win", h2)["win"]
    proj = _mm(tag + "_proj",
               [(h2, pl.BlockSpec((ROW_T, D_MODEL), lambda j, i, k: (i, 0)),
                 win, pl.BlockSpec((D_MODEL, PROJ_TN), lambda j, i, k: (0, j)))],
               jax.ShapeDtypeStruct((T, IN_PAD), F32), pl.BlockSpec((ROW_T, PROJ_TN), lambda j, i, k: (i, j)),
               (IN_PAD // PROJ_TN, nt, 1), NN, (ROW_T, PROJ_TN))
    xbc = proj[:, COL_XBC:COL_Q].reshape(B, S, CONV_DIM)
    xpad = jnp.pad(xbc, ((0, 0), (PAD_R, PAD_R), (0, 0)))
    cw, cbias = p["conv_w"], p["conv_b"][None]
    xc = _conv_fwd(tag + "_conv", xpad, cw, cbias).reshape(T, CONV_DIM)
    dtraw = proj[:, COL_DT:COL_DT + SSD_HEADS].reshape(T, SSD_GROUPS, 4)
    dtc = jnp.transpose(dtraw, (1, 0, 2))
    dtr = jnp.transpose(dtraw, (1, 2, 0))
    pcol, prow = _ssd_params(p)
    Y, y_ssd, hs = _ssd_fwd(tag + "_ssd", xc, proj, dtc, dtr, pcol, prow, p["ssd_norm"][None], B)
    qn = _headnorm_fwd(tag + "_qn", proj, COL_Q // 1024, p["q_norm"][None])
    kn = _headnorm_fwd(tag + "_kn", proj, COL_K // 1024, p["k_norm"][None])
    qT = (qn * ATT_SCALE).T
    vb = proj[:, COL_V:COL_V + 1024].astype(BF16)
    oT, lse = _att_fwd(tag + "_att", kn, qT, vb.T, bias, B)
    ymix = jnp.concatenate([y_ssd, oT.T], axis=1)
    rest = weights("rest", ymix)
    x2 = _mm(tag + "_out",
             [(ymix, pl.BlockSpec((ROW_T, MIX_SH), lambda i, n, k: (i, k)),
               rest["wout"], pl.BlockSpec((None, MIX_SH, D_MODEL), lambda i, n, k: (k, 0, 0)))],
             jax.ShapeDtypeStruct((T, D_MODEL), F32), pl.BlockSpec((ROW_T, D_MODEL), lambda i, n, k: (i, 0)),
             (nt, 1, N_SHARD), NN, (ROW_T, D_MODEL),
             res=(x1, pl.BlockSpec((ROW_T, D_MODEL), lambda i, n, k: (i, 0))))
    saved = dict(x1=x1, h2=h2, proj=proj, xpad=xpad, xc=xc, dtc=dtc, dtr=dtr, Y=Y, hs=hs,
                 qn=qn, kn=kn, qT=qT, vb=vb, oT=oT, lse=lse, ymix=ymix, win=win, wout=rest["wout"])
    return x2, saved


def _mixer_bwd(tag, dx2, sv, p, bias, B):
    T = dx2.shape[0]
    S = T // B
    nt = T // ROW_T
    sg = {}
    dymix = _mm(tag + "_dymix",
                [(dx2, pl.BlockSpec((ROW_T, D_MODEL), lambda n, i, k: (i, 0)),
                  sv["wout"], pl.BlockSpec((None, MIX_SH, D_MODEL), lambda n, i, k: (n, 0, 0)))],
                jax.ShapeDtypeStruct((T, MIX_W), F32), pl.BlockSpec((ROW_T, MIX_SH), lambda n, i, k: (i, n)),
                (N_SHARD, nt, 1), NT, (ROW_T, MIX_SH))
    gwout = _mm(tag + "_dwout",
                [(sv["ymix"], pl.BlockSpec((ROW_T, MIX_SH), lambda m, n, k: (k, m)),
                  dx2, pl.BlockSpec((ROW_T, D_MODEL), lambda m, n, k: (k, 0)))],
                jax.ShapeDtypeStruct((N_SHARD, MIX_SH, D_MODEL), BF16),
                pl.BlockSpec((None, MIX_SH, D_MODEL), lambda m, n, k: (m, 0, 0)),
                (N_SHARD, 1, nt), TN, (MIX_SH, D_MODEL))
    proj = sv["proj"]
    doT = dymix[:, 1024:].T
    dqn = _att_bwd_dq(tag + "_attdq", sv["kn"], sv["qT"], sv["vb"], sv["kn"].T, bias, doT, sv["oT"], sv["lse"], B).T
    dkn, dv = _att_bwd_dkv(tag + "_attdkv", sv["kn"], sv["qT"], sv["vb"], sv["qn"], bias, doT, sv["oT"], sv["lse"],
                           dymix, B)
    dq, sg["q_norm"] = _headnorm_bwd(tag + "_qnb", dqn, proj, COL_Q // 1024, p["q_norm"][None])
    dk, sg["k_norm"] = _headnorm_bwd(tag + "_knb", dkn, proj, COL_K // 1024, p["k_norm"][None])
    pcol, prow = _ssd_params(p)
    dxs, dB, dC, dz, ddt, dpar, dnw = _ssd_bwd(tag + "_ssdb", dymix, sv["Y"], sv["xc"], proj, sv["dtc"], sv["dtr"],
                                               pcol, prow, p["ssd_norm"][None], sv["hs"], B)
    dpar = jnp.sum(dpar, axis=0)
    sg["dt_bias"] = dpar[:, 0, :].reshape(SSD_HEADS)
    sg["a_log"] = dpar[:, 1, :].reshape(SSD_HEADS)
    sg["d_skip"] = dpar[:, 2, :].reshape(SSD_HEADS)
    sg["ssd_norm"] = jnp.sum(dnw, axis=0)
    dxc = jnp.concatenate([dxs, dB, dC], axis=1).reshape(B, S, CONV_DIM)
    dxc_pad = jnp.pad(dxc, ((0, 0), (0, PAD_R), (0, 0)))
    dxbc, sg["conv_w"], sg["conv_b"] = _conv_bwd(tag + "_convb", sv["xpad"], dxc_pad, p["conv_w"], p["conv_b"][None])
    ddt16 = jnp.transpose(ddt, (1, 0, 2)).reshape(T, SSD_HEADS)
    dproj = jnp.concatenate([dz, dxbc.reshape(T, CONV_DIM), dq, dk, dv, ddt16,
                             jnp.zeros((T, IN_PAD - COL_DT - SSD_HEADS), F32)], axis=1).astype(BF16)
    win = sv["win"]
    gwin = _mm(tag + "_dwin",
               [(sv["h2"], pl.BlockSpec((ROW_T, D_MODEL), lambda n, m, k: (k, 0)),
                 dproj, pl.BlockSpec((ROW_T, PROJ_TN), lambda n, m, k: (k, n)))],
               jax.ShapeDtypeStruct((D_MODEL, IN_PAD), BF16), pl.BlockSpec((D_MODEL, PROJ_TN), lambda n, m, k: (0, n)),
               (IN_PAD // PROJ_TN, 1, nt), TN, (D_MODEL, PROJ_TN))
    dh2 = _mm(tag + "_dh2",
              [(dproj, pl.BlockSpec((ROW_T, PROJ_TN), lambda i, n, k: (i, k)),
                win, pl.BlockSpec((D_MODEL, PROJ_TN), lambda i, n, k: (0, k)))],
              jax.ShapeDtypeStruct((T, D_MODEL), F32), pl.BlockSpec((ROW_T, D_MODEL), lambda i, n, k: (i, 0)),
              (nt, 1, IN_PAD // PROJ_TN), NT, (ROW_T, D_MODEL))
    dx1, sg["mix_norm"] = _rms_bwd(tag + "_mixrmsb", dh2, sv["x1"], p["mix_norm"][None], dx2)
    return dx1, sg, gwout, gwin


def _win_pack(w):
    return jnp.concatenate([w[:, :3072], w[:, 3088:], w[:, 3072:3088],
                            jnp.zeros((w.shape[0], IN_PAD - IN_PROJ), w.dtype)], axis=1)


def _win_unpack(g):
    return jnp.concatenate([g[:, :3072], g[:, COL_DT:COL_DT + SSD_HEADS], g[:, 3072:COL_DT]], axis=1)


def _local_step(x, target, small, weights, scatter, B):
    T = x.shape[0]
    bias = _att_bias((T // B) // ATT_B)
    saved = []
    h = x
    for l in range(DEPTH):
        tag = "l%d" % l
        p = {k: v[l] for k, v in small.items()}
        w1 = weights(l, "ffn1", h)
        x1, ffn1 = _ffn_fwd(tag + "f1", h, p["ffn1_norm"][None], w1["g1"], w1["u1"], w1["d1"])
        x2, sv = _mixer_fwd(tag, x1, p, functools.partial(weights, l), bias, B)
        w2 = weights(l, "rest", x2)
        h, ffn2 = _ffn_fwd(tag + "f2", x2, p["ffn2_norm"][None], w2["g2"], w2["u2"], w2["d2"])
        saved.append((ffn1, sv, ffn2, w1, w2))
    d, lsum = _loss_grad("loss", h, target)
    sgrads = [None] * DEPTH
    for l in reversed(range(DEPTH)):
        tag = "l%db" % l
        p = {k: v[l] for k, v in small.items()}
        ffn1, sv, ffn2, w1, w2 = saved[l]
        sg = {}
        d, sg["ffn2_norm"] = _ffn_bwd(tag + "f2", d, ffn2, p["ffn2_norm"][None], w2["g2"], w2["u2"], w2["d2"],
                                      lambda gg, gu, gd, c, l=l: scatter(l, "ffn2", dict(g2=gg, u2=gu, d2=gd), c))
        d, sgm, gwout, gwin = _mixer_bwd(tag, d, sv, p, bias, B)
        sg.update(sgm)
        d = scatter(l, "mixer", dict(wout=gwout, win=gwin), d)
        d, sg["ffn1_norm"] = _ffn_bwd(tag + "f1", d, ffn1, p["ffn1_norm"][None], w1["g1"], w1["u1"], w1["d1"],
                                      lambda gg, gu, gd, c, l=l: scatter(l, "ffn1", dict(g1=gg, u1=gu, d1=gd), c))
        sgrads[l] = sg
    return lsum, d, sgrads


MESH = pl.DeviceIdType.MESH
ANY = pl.BlockSpec(memory_space=pl.ANY)


def _place():
    return lax.axis_index("x"), lax.axis_index("y"), lax.axis_index("c")


def _other_chips(x, y):
    return [(1 - x, y), (x, 1 - y), (1 - x, 1 - y)]


HBM = pl.BlockSpec(memory_space=pltpu.HBM)
SEM = pl.BlockSpec(memory_space=pltpu.SEMAPHORE)
EFFECT = pltpu.SideEffectType.DATAFLOW_SIDE_EFFECTING


def _hbm(a):
    return pltpu.with_memory_space_constraint(a, pltpu.HBM)


def _exchange(gather, src, land, send, recv, n, act):
    x, y, c = _place()
    for k, (px, py) in enumerate(_other_chips(x, y)):
        for a in range(n):
            if gather:
                s_out, d_out, d_in = src[a], land[a].at[2 * x + y], land[a].at[2 * px + py]
            else:
                s_out, d_out, d_in = src[a].at[2 * px + py], land[a].at[k], land[a].at[k]
            act(pltpu.make_async_remote_copy(
                src_ref=s_out, dst_ref=d_out if act is _start else d_in, send_sem=send.at[k * n + a],
                recv_sem=recv.at[k * n + a], device_id=(px, py, c), device_id_type=MESH))


def _start(cp):
    cp.start()


def _finish(cp):
    cp.wait_send()
    cp.wait_recv()


def _exchange_start(name, gather, srcs, carry):
    n = len(srcs)
    lands = [lax.empty(((N_SHARD,) + s.shape) if gather else ((3,) + s.shape[1:]), s.dtype) for s in srcs]

    def body(*refs):
        _exchange(gather, refs[:n], refs[n:2 * n], refs[2 * n + 1], refs[2 * n + 2], n, _start)

    ops = [_hbm(a) for a in list(srcs) + lands + [carry]]
    out = pl.pallas_call(
        body, name=name,
        out_shape=(pltpu.SemaphoreType.DMA((3 * n,)), pltpu.SemaphoreType.DMA((3 * n,)),
                   *[pltpu.HBM(a.shape, a.dtype) for a in ops]),
        in_specs=[HBM] * len(ops), out_specs=(SEM, SEM, *[HBM] * len(ops)),
        input_output_aliases={i: 2 + i for i in range(len(ops))},
        compiler_params=pltpu.CompilerParams(has_side_effects=EFFECT))(*ops)
    return dict(gather=gather, send=out[0], recv=out[1], srcs=list(out[2:2 + n]), lands=list(out[2 + n:2 + 2 * n])), out[-1]


def _exchange_wait(name, ex, after):
    n = len(ex["srcs"])
    gather = ex["gather"]

    def body(*refs):
        _exchange(gather, refs[:n], refs[n:2 * n], refs[2 * n], refs[2 * n + 1], n, _finish)

    ops = ex["srcs"] + ex["lands"]
    out = pl.pallas_call(
        body, name=name, out_shape=[pltpu.HBM(a.shape, a.dtype) for a in ops],
        in_specs=[HBM] * len(ops) + [SEM, SEM, ANY], out_specs=[HBM] * len(ops),
        input_output_aliases={i: i for i in range(len(ops))},
        compiler_params=pltpu.CompilerParams(has_side_effects=EFFECT))(*ops, ex["send"], ex["recv"], after)
    return list(out[:n]), list(out[n:])


def _swap_sibling(parts):
    n = len(parts)

    def body(*refs):
        src, dst = refs[:n], refs[n:2 * n]
        send, recv = refs[2 * n:]
        x, y, c = _place()
        cps = [pltpu.make_async_remote_copy(src_ref=src[a], dst_ref=dst[a], send_sem=send.at[a], recv_sem=recv.at[a],
                                            device_id=(x, y, 1 - c), device_id_type=MESH) for a in range(n)]
        for cp in cps:
            cp.start()
        for cp in cps:
            cp.wait_recv()
        for cp in cps:
            cp.wait_send()

    return pl.pallas_call(
        body, out_shape=[jax.ShapeDtypeStruct(p.shape, p.dtype) for p in parts],
        in_specs=[ANY] * n, out_specs=[ANY] * n,
        scratch_shapes=[pltpu.SemaphoreType.DMA((n,)), pltpu.SemaphoreType.DMA((n,))],
        name="swap_sibling")(*parts)


def _allreduce_small(name, v):
    R = v.shape[0]

    def body(v_ref, o_ref, buf, send, recv):
        x, y, c = _place()
        me = 4 * x + 2 * y + c
        buf[me] = v_ref[...]
        cps = []
        for k in range(1, 8):
            fx, fy, fc = (k >> 2) & 1, (k >> 1) & 1, k & 1
            px = 1 - x if fx else x
            py = 1 - y if fy else y
            pc = 1 - c if fc else c
            cp = pltpu.make_async_remote_copy(src_ref=v_ref, dst_ref=buf.at[me], send_sem=send.at[k - 1],
                                              recv_sem=recv.at[k - 1], device_id=(px, py, pc), device_id_type=MESH)
            cp.start()
            cps.append((cp, 4 * px + 2 * py + pc))
        for k, (cp, peer) in enumerate(cps):
            pltpu.make_async_remote_copy(src_ref=v_ref, dst_ref=buf.at[peer], send_sem=send.at[k], recv_sem=recv.at[k],
                                         device_id=(x, y, c), device_id_type=MESH).wait_recv()
        for cp, _ in cps:
            cp.wait_send()
        acc = buf[0]
        for d in range(1, 8):
            acc = acc + buf[d]
        o_ref[...] = acc

    return pl.pallas_call(
        body, out_shape=jax.ShapeDtypeStruct((R, 128), F32),
        in_specs=[pl.BlockSpec(memory_space=pltpu.VMEM)], out_specs=pl.BlockSpec(memory_space=pltpu.VMEM),
        scratch_shapes=[pltpu.VMEM((8, R, 128), F32), pltpu.SemaphoreType.DMA((7,)), pltpu.SemaphoreType.DMA((7,))],
        name=name)(v)


def _row_tile(r):
    for t in (256, 128, 64, 32, 16, 8):
        if r % t == 0:
            return t
    raise ValueError(r)


def _sum4(name, own, got):
    R, C = own.shape
    tr = _row_tile(R)

    def body(o_ref, g_ref, s_ref):
        s = o_ref[...].astype(F32)
        for k in range(3):
            s = s + g_ref[k].astype(F32)
        s_ref[...] = s

    return pl.pallas_call(
        body, out_shape=jax.ShapeDtypeStruct((R, C), F32), grid=(R // tr,),
        in_specs=[pl.BlockSpec((tr, C), lambda i: (i, 0)), pl.BlockSpec((3, tr, C), lambda i: (0, i, 0))],
        out_specs=pl.BlockSpec((tr, C), lambda i: (i, 0)), name=name, compiler_params=_cp("parallel"))(own, got)


def _adamw(name, w, gparts, m, v):
    R, C = w.shape
    tr = _row_tile(R)
    ng = len(gparts)
    c1 = 1.0 - ADAM_B1 ** ADAM_STEP
    c2 = 1.0 - ADAM_B2 ** ADAM_STEP

    def body(*refs):
        w_ref = refs[0]
        g_refs = refs[1:1 + ng]
        m_ref, v_ref, go_ref, d_ref, mo_ref, vo_ref = refs[1 + ng:]
        g = g_refs[0][...]
        for r in g_refs[1:]:
            g = g + r[...]
        mn = ADAM_B1 * m_ref[...] + (1.0 - ADAM_B1) * g
        vn = ADAM_B2 * v_ref[...] + (1.0 - ADAM_B2) * (g * g)
        go_ref[...] = g
        mo_ref[...] = mn
        vo_ref[...] = vn
        d_ref[...] = -ADAM_LR * ((mn / c1) / (jnp.sqrt(vn / c2) + ADAM_EPS) + ADAM_WD * w_ref[...])

    blk = pl.BlockSpec((tr, C), lambda i: (i, 0))
    osh = jax.ShapeDtypeStruct((R, C), F32)
    return pl.pallas_call(
        body, out_shape=(osh, osh, osh, osh), grid=(R // tr,), in_specs=[blk] * (3 + ng), out_specs=(blk,) * 4,
        name=name, compiler_params=_cp("parallel"))(w, *gparts, m, v)


def _adamw_layers(name, w, sums, m, v):
    R2, C = w.shape
    R = R2 // DEPTH
    tr = _row_tile(R)
    nr = R // tr
    c1 = 1.0 - ADAM_B1 ** ADAM_STEP
    c2 = 1.0 - ADAM_B2 ** ADAM_STEP

    def body(w_ref, a0, b0, a1, b1, m_ref, v_ref, go_ref, d_ref, mo_ref, vo_ref):
        g = jnp.where(pl.program_id(0) == 0, a0[...] + b0[...], a1[...] + b1[...])
        mn = ADAM_B1 * m_ref[...] + (1.0 - ADAM_B1) * g
        vn = ADAM_B2 * v_ref[...] + (1.0 - ADAM_B2) * (g * g)
        go_ref[...] = g
        mo_ref[...] = mn
        vo_ref[...] = vn
        d_ref[...] = -ADAM_LR * ((mn / c1) / (jnp.sqrt(vn / c2) + ADAM_EPS) + ADAM_WD * w_ref[...])

    blk = pl.BlockSpec((tr, C), lambda l, i: (l * nr + i, 0))
    lay0 = pl.BlockSpec((tr, C), lambda l, i: (jnp.where(l == 0, i, nr - 1), 0))
    lay1 = pl.BlockSpec((tr, C), lambda l, i: (jnp.where(l == 1, i, 0), 0))
    osh = jax.ShapeDtypeStruct((R2, C), F32)
    return pl.pallas_call(
        body, out_shape=(osh, osh, osh, osh), grid=(DEPTH, nr),
        in_specs=[blk, lay0, lay0, lay1, lay1, blk, blk], out_specs=(blk,) * 4,
        name=name, compiler_params=_cp("arbitrary", "arbitrary"))(w, *sums[0], *sums[1], m, v)


BIG = [("ffn1_w_gate", "g1"), ("ffn1_w_up", "u1"), ("ffn1_w_down", "d1"), ("w_in", "win"), ("w_out", "wout"),
       ("ffn2_w_gate", "g2"), ("ffn2_w_up", "u2"), ("ffn2_w_down", "d2")]
SMALL = ["ffn1_norm", "mix_norm", "conv_b", "dt_bias", "a_log", "d_skip", "ssd_norm", "q_norm", "k_norm", "ffn2_norm"]
WEIGHTS = ["ffn1_norm", "ffn1_w_gate", "ffn1_w_up", "ffn1_w_down", "mix_norm", "w_in", "conv_w", "conv_b", "dt_bias",
           "a_log", "d_skip", "ssd_norm", "q_norm", "k_norm", "w_out", "ffn2_norm", "ffn2_w_gate", "ffn2_w_up",
           "ffn2_w_down"]
CONV_SH = CONV_DIM // N_SHARD
GATHER_GROUPS = [(0, "ffn1", ["g1", "u1", "d1"]), (0, "win", ["win"]), (0, "rest", ["wout", "g2", "u2", "d2"]),
                 (1, "all", ["g1", "u1", "d1", "win", "wout", "g2", "u2", "d2"])]


def _pad128(v):
    v = v.reshape(-1)
    return jnp.pad(v, (0, (-v.shape[0]) % 128))


def _pack(pieces):
    flat, offs, pos = [], [], 0
    for p in pieces:
        q = _pad128(p.astype(F32))
        offs.append(pos)
        pos += q.shape[0] // 128
        flat.append(q)
    total = -(-pos // 8) * 8
    out = jnp.concatenate(flat + [jnp.zeros(((total - pos) * 128,), F32)]).reshape(total, 128)
    return out, offs


def _unpack(packed, offs, shapes):
    out = []
    for off, shp in zip(offs, shapes):
        n = int(np.prod(shp))
        rows = -(-n // 128)
        out.append(packed[off:off + rows].reshape(-1)[:n].reshape(shp))
    return out


def kernel(x, ffn1_norm, ffn1_w_gate, ffn1_w_up, ffn1_w_down, mix_norm, w_in, conv_w, conv_b, dt_bias, a_log, d_skip, ssd_norm, q_norm, k_norm, w_out, ffn2_norm, ffn2_w_gate, ffn2_w_up, ffn2_w_down, loss_target, m_ffn1_norm, m_ffn1_w_gate, m_ffn1_w_up, m_ffn1_w_down, m_mix_norm, m_w_in, m_conv_w, m_conv_b, m_dt_bias, m_a_log, m_d_skip, m_ssd_norm, m_q_norm, m_k_norm, m_w_out, m_ffn2_norm, m_ffn2_w_gate, m_ffn2_w_up, m_ffn2_w_down, v_ffn1_norm, v_ffn1_w_gate, v_ffn1_w_up, v_ffn1_w_down, v_mix_norm, v_w_in, v_conv_w, v_conv_b, v_dt_bias, v_a_log, v_d_skip, v_ssd_norm, v_q_norm, v_k_norm, v_w_out, v_ffn2_norm, v_ffn2_w_gate, v_ffn2_w_up, v_ffn2_w_down):
    A = dict(locals())
    ix, iy, ic = _place()
    me = 2 * ix + iy
    B, S, _ = x.shape
    T = B * S

    own = {key: A[name].astype(BF16) for name, key in BIG}
    exs, first_norm = [], ffn1_norm
    for gi, (l, _, keys) in enumerate(GATHER_GROUPS):
        ex, first_norm = _exchange_start("gather_start%d" % gi, True, [own[key][l] for key in keys], first_norm)
        exs.append(ex)
    landed = {}

    def weights(l, group, after):
        gi = [i for i, (gl, gname, _) in enumerate(GATHER_GROUPS) if gl == l and gname in (group, "all")][0]
        if gi not in landed:
            srcs, lands = _exchange_wait("gather_wait%d" % gi, exs[gi], after)
            landed[gi] = {}
            for key, mine, land in zip(GATHER_GROUPS[gi][2], srcs, lands):
                full = lax.dynamic_update_slice(land, mine[None], (me, 0, 0))
                if key == "win":
                    full = _win_pack(jnp.concatenate([full[j] for j in range(N_SHARD)], axis=1))
                landed[gi][key] = full
        return landed[gi]

    placed = lax.dynamic_update_slice(jnp.zeros((DEPTH, CONV_K, CONV_DIM), F32),
                                      conv_w * (ic == 0).astype(F32), (0, 0, me * CONV_SH))
    conv_full = _allreduce_small("gather_conv_w", placed.reshape(-1, 128)).reshape(DEPTH, CONV_K, CONV_DIM)

    pending = []

    def scatter(l, group, grads, carry):
        keys = sorted(grads)
        arrs = [grads[key] for key in keys]
        if "win" in grads:
            arrs[keys.index("win")] = jnp.transpose(_win_unpack(grads["win"]).reshape(D_MODEL, N_SHARD, IN_SH), (1, 0, 2))
        ex, carry = _exchange_start("scatter_start_l%d_%s" % (l, group), False, arrs, carry)
        pending.append((l, keys, ex))
        return carry

    small = {name: A[name] for name in SMALL}
    small["ffn1_norm"] = first_norm
    small["conv_w"] = conv_full
    lsum, dx, sgrads = _local_step(x.reshape(T, D_MODEL), loss_target.reshape(T, D_MODEL), small, weights, scatter, B)

    names = SMALL + ["conv_w"]
    pieces = [jnp.stack([sgrads[l][n].reshape(small[n].shape[1:]) for l in range(DEPTH)]) for n in names]
    pieces.append(0.5 / D_MODEL * jnp.sum(lsum))
    packed, offs = _pack(pieces)
    red = _allreduce_small("allreduce_small", packed)
    shapes = [small[n].shape for n in names] + [()]
    red = _unpack(red, offs, shapes)
    loss = red[-1]
    sg = dict(zip(names, red[:-1]))

    sums, after = {}, dx
    for idx, (l, keys, ex) in enumerate(pending):
        srcs, lands = _exchange_wait("scatter_wait%d" % idx, ex, after)
        for key, g, got in zip(keys, srcs, lands):
            mine = lax.dynamic_index_in_dim(g, me, axis=0, keepdims=False)
            sums[key, l] = after = _sum4("sum_%s_l%d" % (key, l), mine, got)
    order = [(key, l) for _, key in BIG for l in range(DEPTH)]
    theirs = dict(zip(order, _swap_sibling([sums[k] for k in order])))

    out = {}
    for name, key in BIG:
        shp = A[name].shape
        flat = lambda a: a.reshape(shp[0] * shp[1], shp[2])
        res = _adamw_layers("adamw_" + key, flat(A[name]), [(sums[key, l], theirs[key, l]) for l in range(DEPTH)],
                            flat(A["m_" + name]), flat(A["v_" + name]))
        out[name] = [r.reshape(shp) for r in res]

    wp, offs = _pack([A[n] for n in SMALL])
    gp, _ = _pack([sg[n] for n in SMALL])
    mp, _ = _pack([A["m_" + n] for n in SMALL])
    vp, _ = _pack([A["v_" + n] for n in SMALL])
    res = _adamw("adamw_small", wp, [gp], mp, vp)
    shapes = [A[n].shape for n in SMALL]
    res = [_unpack(r, offs, shapes) for r in res]
    for i, n in enumerate(SMALL):
        out[n] = [res[q][i] for q in range(4)]
    gcw = lax.dynamic_slice_in_dim(sg["conv_w"], me * CONV_SH, CONV_SH, axis=2)
    flat = lambda a: a.reshape(DEPTH * CONV_K, CONV_SH)
    res = _adamw("adamw_conv_w", flat(conv_w), [flat(gcw)], flat(m_conv_w), flat(v_conv_w))
    out["conv_w"] = [r.reshape(conv_w.shape) for r in res]

    outs = [loss, dx.reshape(B, S, D_MODEL)]
    for q in range(4):
        outs += [out[n][q] for n in WEIGHTS]
    return tuple(outs)
```

```python
import functools
import math

import numpy as np
import jax
import jax.numpy as jnp
from jax import lax
from jax.experimental import pallas as pl
from jax.experimental.pallas import tpu as pltpu

F32 = jnp.float32
BF16 = jnp.bfloat16

D_MODEL = 1024
DEPTH = 2
N_SHARD = 4
D_FF = 2816
FF_SH = D_FF // N_SHARD
SSD_HEADS = 16
HEAD_DIM = 64
SSD_GROUPS = 4
GROUP_W = 256
SSD_STATE = 128
CONV_K = 4
CONV_DIM = 2048
ATT_HEADS = 16
MIX_W = 2048
MIX_SH = MIX_W // N_SHARD
IN_PROJ = 6160
IN_SH = IN_PROJ // N_SHARD
IN_PAD = 6272
PROJ_TN = 896
COL_Z, COL_XBC, COL_Q, COL_K, COL_V, COL_DT = 0, 1024, 3072, 4096, 5120, 6144
EPS = 1e-6
NEG = -1e30
SSD_L = 256
ATT_B = 256
ROW_T = 512
CONV_CT = 256
CONV_R = 256
PAD_R = 8

ADAM_LR, ADAM_B1, ADAM_B2, ADAM_EPS, ADAM_WD, ADAM_STEP = 0.001, 0.9, 0.999, 1e-08, 0.01, 10

NN = (((1,), (0,)), ((), ()))
NT = (((1,), (1,)), ((), ()))
TN = (((0,), (0,)), ((), ()))

VMEM_LIMIT = 56 * 1024 * 1024


def _cp(*sem):
    return pltpu.CompilerParams(dimension_semantics=sem, vmem_limit_bytes=VMEM_LIMIT)


def _dot(a, b, dims):
    return lax.dot_general(a, b, dims, preferred_element_type=F32)


def _sigmoid(x):
    return 1.0 / (1.0 + jnp.exp(-x))


def _softplus(x):
    return jnp.maximum(x, 0.0) + jnp.log(1.0 + jnp.exp(-jnp.abs(x)))


def _mm(name, pairs, out_shape, out_spec, grid, dims, acc_shape, res=None, scale=1.0):
    nk = grid[2]
    npair = len(pairs)

    def body(*refs):
        ab = refs[:2 * npair]
        pos = 2 * npair
        res_ref = None
        if res is not None:
            res_ref = refs[pos]
            pos += 1
        out_ref, acc = refs[pos], refs[pos + 1]
        k = pl.program_id(2)

        @pl.when(k == 0)
        def _():
            acc[...] = jnp.zeros_like(acc)

        s = None
        for p in range(npair):
            d = _dot(ab[2 * p][...].astype(BF16), ab[2 * p + 1][...].astype(BF16), dims)
            s = d if s is None else s + d
        acc[...] += s

        @pl.when(k == nk - 1)
        def _():
            r = acc[...]
            if scale != 1.0:
                r = r * scale
            if res_ref is not None:
                r = r + res_ref[...]
            out_ref[...] = r.astype(out_ref.dtype)

    args, specs = [], []
    for a, a_spec, b, b_spec in pairs:
        args += [a, b]
        specs += [a_spec, b_spec]
    if res is not None:
        args.append(res[0])
        specs.append(res[1])
    return pl.pallas_call(
        body, out_shape=out_shape, grid=grid, in_specs=specs, out_specs=out_spec,
        scratch_shapes=[pltpu.VMEM(acc_shape, F32)], name=name,
        compiler_params=_cp("parallel", "parallel", "arbitrary"))(*args)


def _rms_fwd(name, x, w):
    T = x.shape[0]

    def body(x_ref, w_ref, o_ref):
        xv = x_ref[...]
        r = lax.rsqrt(jnp.mean(xv * xv, axis=-1, keepdims=True) + EPS)
        o_ref[...] = (xv * r * w_ref[...]).astype(BF16)

    return pl.pallas_call(
        body, out_shape=jax.ShapeDtypeStruct((T, D_MODEL), BF16), grid=(T // ROW_T,),
        in_specs=[pl.BlockSpec((ROW_T, D_MODEL), lambda i: (i, 0)), pl.BlockSpec((1, D_MODEL), lambda i: (0, 0))],
        out_specs=pl.BlockSpec((ROW_T, D_MODEL), lambda i: (i, 0)), name=name, compiler_params=_cp("parallel"))(x, w)


def _rms_bwd(name, dh, x, w, dres):
    T = x.shape[0]

    def body(dh_ref, x_ref, w_ref, dres_ref, dx_ref, dw_ref):
        @pl.when(pl.program_id(0) == 0)
        def _():
            dw_ref[...] = jnp.zeros_like(dw_ref)

        xv = x_ref[...]
        r = lax.rsqrt(jnp.mean(xv * xv, axis=-1, keepdims=True) + EPS)
        xhat = xv * r
        dhv = dh_ref[...]
        dxhat = dhv * w_ref[...]
        m = jnp.mean(dxhat * xhat, axis=-1, keepdims=True)
        dx_ref[...] = dres_ref[...] + r * (dxhat - xhat * m)
        dw_ref[...] += jnp.sum(dhv * xhat, axis=0, keepdims=True)

    row = pl.BlockSpec((ROW_T, D_MODEL), lambda i: (i, 0))
    vec = pl.BlockSpec((1, D_MODEL), lambda i: (0, 0))
    return pl.pallas_call(
        body, out_shape=(jax.ShapeDtypeStruct((T, D_MODEL), F32), jax.ShapeDtypeStruct((1, D_MODEL), F32)),
        grid=(T // ROW_T,), in_specs=[row, row, vec, row], out_specs=(row, vec), name=name,
        compiler_params=_cp("arbitrary"))(dh, x, w, dres)


def _loss_grad(name, y, t):
    T = y.shape[0]

    def body(y_ref, t_ref, dy_ref, l_ref):
        @pl.when(pl.program_id(0) == 0)
        def _():
            l_ref[...] = jnp.zeros_like(l_ref)

        e = y_ref[...] - t_ref[...]
        dy_ref[...] = e * (1.0 / D_MODEL)
        l_ref[...] += jnp.sum(e * e, axis=0, keepdims=True)

    row = pl.BlockSpec((ROW_T, D_MODEL), lambda i: (i, 0))
    vec = pl.BlockSpec((1, D_MODEL), lambda i: (0, 0))
    return pl.pallas_call(
        body, out_shape=(jax.ShapeDtypeStruct((T, D_MODEL), F32), jax.ShapeDtypeStruct((1, D_MODEL), F32)),
        grid=(T // ROW_T,), in_specs=[row, row], out_specs=(row, vec), name=name,
        compiler_params=_cp("arbitrary"))(y, t)


def _ffn_gate_up(name, h, wg, wu):
    T = h.shape[0]

    def body(h_ref, wg_ref, wu_ref, g_ref, u_ref, a_ref):
        hv = h_ref[...]
        g = _dot(hv, wg_ref[...], NN)
        u = _dot(hv, wu_ref[...], NN)
        g_ref[...] = g.astype(BF16)
        u_ref[...] = u.astype(BF16)
        a_ref[...] = (g * _sigmoid(g) * u).astype(BF16)

    wspec = pl.BlockSpec((None, D_MODEL, FF_SH), lambda j, i: (j, 0, 0))
    ospec = pl.BlockSpec((None, ROW_T, FF_SH), lambda j, i: (j, i, 0))
    osh = jax.ShapeDtypeStruct((N_SHARD, T, FF_SH), BF16)
    return pl.pallas_call(
        body, out_shape=(osh, osh, osh), grid=(N_SHARD, T // ROW_T),
        in_specs=[pl.BlockSpec((ROW_T, D_MODEL), lambda j, i: (i, 0)), wspec, wspec],
        out_specs=(ospec, ospec, ospec), name=name, compiler_params=_cp("parallel", "parallel"))(h, wg, wu)


def _ffn_dact(name, dx, wd, g, u):
    T = dx.shape[0]

    def body(dx_ref, wd_ref, g_ref, u_ref, dg_ref, du_ref):
        da = 0.5 * _dot(dx_ref[...].astype(BF16), wd_ref[...], NT)
        gv = g_ref[...].astype(F32)
        uv = u_ref[...].astype(F32)
        sg = _sigmoid(gv)
        dg_ref[...] = (da * uv * (sg * (1.0 + gv * (1.0 - sg)))).astype(BF16)
        du_ref[...] = (da * gv * sg).astype(BF16)

    aspec = pl.BlockSpec((None, ROW_T, FF_SH), lambda j, i: (j, i, 0))
    osh = jax.ShapeDtypeStruct((N_SHARD, T, FF_SH), BF16)
    return pl.pallas_call(
        body, out_shape=(osh, osh), grid=(N_SHARD, T // ROW_T),
        in_specs=[pl.BlockSpec((ROW_T, D_MODEL), lambda j, i: (i, 0)),
                  pl.BlockSpec((None, FF_SH, D_MODEL), lambda j, i: (j, 0, 0)), aspec, aspec],
        out_specs=(aspec, aspec), name=name, compiler_params=_cp("parallel", "parallel"))(dx, wd, g, u)


def _ffn_fwd(tag, x, nw, wg, wu, wd):
    T = x.shape[0]
    h = _rms_fwd(tag + "_rms", x, nw)
    g, u, a = _ffn_gate_up(tag + "_gu", h, wg, wu)
    nt = T // ROW_T
    xo = _mm(tag + "_down",
             [(a, pl.BlockSpec((None, ROW_T, FF_SH), lambda i, n, k: (k, i, 0)),
               wd, pl.BlockSpec((None, FF_SH, D_MODEL), lambda i, n, k: (k, 0, 0)))],
             jax.ShapeDtypeStruct((T, D_MODEL), F32), pl.BlockSpec((ROW_T, D_MODEL), lambda i, n, k: (i, 0)),
             (nt, 1, N_SHARD), NN, (ROW_T, D_MODEL),
             res=(x, pl.BlockSpec((ROW_T, D_MODEL), lambda i, n, k: (i, 0))), scale=0.5)
    return xo, (x, h, g, u, a)


def _ffn_bwd(tag, dxo, saved, nw, wg, wu, wd, emit):
    x, h, g, u, a = saved
    T = x.shape[0]
    nt = T // ROW_T
    dg, du = _ffn_dact(tag + "_dact", dxo, wd, g, u)
    act = lambda f: pl.BlockSpec((None, ROW_T, FF_SH), f)
    gd = _mm(tag + "_dwd",
             [(a, act(lambda m, n, k: (m, k, 0)), dxo, pl.BlockSpec((ROW_T, D_MODEL), lambda m, n, k: (k, 0)))],
             jax.ShapeDtypeStruct((N_SHARD, FF_SH, D_MODEL), BF16),
             pl.BlockSpec((None, FF_SH, D_MODEL), lambda m, n, k: (m, 0, 0)),
             (N_SHARD, 1, nt), TN, (FF_SH, D_MODEL), scale=0.5)
    hspec = pl.BlockSpec((ROW_T, D_MODEL), lambda j, n, k: (k, 0))
    gsh = jax.ShapeDtypeStruct((N_SHARD, D_MODEL, FF_SH), BF16)
    gspec = pl.BlockSpec((None, D_MODEL, FF_SH), lambda j, n, k: (j, 0, 0))
    gg = _mm(tag + "_dwg", [(h, hspec, dg, act(lambda j, n, k: (j, k, 0)))], gsh, gspec,
             (N_SHARD, 1, nt), TN, (D_MODEL, FF_SH))
    gu = _mm(tag + "_dwu", [(h, hspec, du, act(lambda j, n, k: (j, k, 0)))], gsh, gspec,
             (N_SHARD, 1, nt), TN, (D_MODEL, FF_SH))
    dg = emit(gg, gu, gd, dg)
    wspec = pl.BlockSpec((None, D_MODEL, FF_SH), lambda i, n, k: (k, 0, 0))
    dh = _mm(tag + "_dh",
             [(dg, act(lambda i, n, k: (k, i, 0)), wg, wspec), (du, act(lambda i, n, k: (k, i, 0)), wu, wspec)],
             jax.ShapeDtypeStruct((T, D_MODEL), F32), pl.BlockSpec((ROW_T, D_MODEL), lambda i, n, k: (i, 0)),
             (nt, 1, N_SHARD), NT, (ROW_T, D_MODEL))
    return _rms_bwd(tag + "_rmsb", dh, x, nw, dxo)


def _conv_fwd(name, xpad, w, b):
    B, SP, C = xpad.shape
    S = SP - 2 * PAD_R

    def body(x_ref, w_ref, b_ref, o_ref):
        wv = w_ref[...]
        for c in range(S // CONV_R):
            r0 = c * CONV_R
            ch = x_ref[pl.ds(r0, CONV_R + PAD_R), :]
            pre = ch[PAD_R:] * wv[3:4] + b_ref[...]
            for s in range(1, CONV_K):
                pre = pre + pltpu.roll(ch, s, axis=0)[PAD_R:] * wv[3 - s:4 - s]
            o_ref[pl.ds(r0, CONV_R), :] = pre * _sigmoid(pre)

    return pl.pallas_call(
        body, out_shape=jax.ShapeDtypeStruct((B, S, C), F32), grid=(B, C // CONV_CT),
        in_specs=[pl.BlockSpec((None, SP, CONV_CT), lambda bi, ci: (bi, 0, ci)),
                  pl.BlockSpec((CONV_K, CONV_CT), lambda bi, ci: (0, ci)),
                  pl.BlockSpec((1, CONV_CT), lambda bi, ci: (0, ci))],
        out_specs=pl.BlockSpec((None, S, CONV_CT), lambda bi, ci: (bi, 0, ci)), name=name,
        compiler_params=_cp("parallel", "parallel"))(xpad, w, b)


def _conv_bwd(name, xpad, dxc_pad, w, b):
    B, SP, C = xpad.shape
    S = SP - 2 * PAD_R
    RW = CONV_R + PAD_R

    def body(x_ref, d_ref, w_ref, b_ref, dx_ref, dw_ref, db_ref):
        @pl.when(pl.program_id(1) == 0)
        def _():
            dw_ref[...] = jnp.zeros_like(dw_ref)
            db_ref[...] = jnp.zeros_like(db_ref)

        wv = w_ref[...]
        dw = [jnp.zeros((1, CONV_CT), F32) for _ in range(CONV_K)]
        db = jnp.zeros((1, CONV_CT), F32)
        for c in range(S // CONV_R):
            r0 = c * CONV_R
            ch = x_ref[pl.ds(r0, RW + PAD_R), :]
            xs = [ch[PAD_R:]] + [pltpu.roll(ch, s, axis=0)[PAD_R:] for s in range(1, CONV_K)]
            pre = b_ref[...] + xs[0] * wv[3:4]
            for s in range(1, CONV_K):
                pre = pre + xs[s] * wv[3 - s:4 - s]
            sg = _sigmoid(pre)
            dpre = d_ref[pl.ds(r0, RW), :] * (sg * (1.0 + pre * (1.0 - sg)))
            dx = dpre[:CONV_R] * wv[3:4]
            for s in range(1, CONV_K):
                dx = dx + pltpu.roll(dpre, RW - s, axis=0)[:CONV_R] * wv[3 - s:4 - s]
            dx_ref[pl.ds(r0, CONV_R), :] = dx
            dcur = dpre[:CONV_R]
            db = db + jnp.sum(dcur, axis=0, keepdims=True)
            for s in range(CONV_K):
                dw[3 - s] = dw[3 - s] + jnp.sum(dcur * xs[s][:CONV_R], axis=0, keepdims=True)
        db_ref[...] += db
        for k in range(CONV_K):
            dw_ref[k:k + 1, :] += dw[k]

    return pl.pallas_call(
        body,
        out_shape=(jax.ShapeDtypeStruct((B, S, C), F32), jax.ShapeDtypeStruct((CONV_K, C), F32),
                   jax.ShapeDtypeStruct((1, C), F32)),
        grid=(C // CONV_CT, B),
        in_specs=[pl.BlockSpec((None, SP, CONV_CT), lambda ci, bi: (bi, 0, ci)),
                  pl.BlockSpec((None, S + PAD_R, CONV_CT), lambda ci, bi: (bi, 0, ci)),
                  pl.BlockSpec((CONV_K, CONV_CT), lambda ci, bi: (0, ci)),
                  pl.BlockSpec((1, CONV_CT), lambda ci, bi: (0, ci))],
        out_specs=(pl.BlockSpec((None, S, CONV_CT), lambda ci, bi: (bi, 0, ci)),
                   pl.BlockSpec((CONV_K, CONV_CT), lambda ci, bi: (0, ci)),
                   pl.BlockSpec((1, CONV_CT), lambda ci, bi: (0, ci))),
        name=name, compiler_params=_cp("parallel", "arbitrary"))(xpad, dxc_pad, w, b)


def _ssd_common(dtc_ref, dtr_ref, pcol_ref, prow_ref, b_ref, c_ref):
    L = SSD_L
    bias_c, alog_c = pcol_ref[0:1, :], pcol_ref[1:2, :]
    a_c = -jnp.exp(alog_c)
    dt_c = _softplus(dtc_ref[...] + bias_c)
    row = lax.broadcasted_iota(jnp.int32, (L, L), 0)
    col = lax.broadcasted_iota(jnp.int32, (L, L), 1)
    causal = row >= col
    tri = causal.astype(F32)
    hp = lax.Precision.HIGHEST
    cum_c = lax.dot_general(tri, dt_c * a_c, NN, precision=hp, preferred_element_type=F32)
    a_r = -jnp.exp(prow_ref[:, 1:2])
    dt_r = _softplus(dtr_ref[...] + prow_ref[:, 0:1])
    cum_r = lax.dot_general(dt_r * a_r, tri, NT, precision=hp, preferred_element_type=F32)
    bb = b_ref[...].astype(BF16)
    cb = c_ref[...].astype(BF16)
    G = _dot(cb, bb, NT)
    return a_c, dt_c, causal, tri, cum_c, cum_r, bb, cb, G


def _ssd_fwd(name, xc, proj, dtc, dtr, pcol, prow, nw, B):
    T = xc.shape[0]
    S = T // B
    nb = S // SSD_L
    L = SSD_L

    def body(xs_ref, b_ref, c_ref, z_ref, dtc_ref, dtr_ref, pcol_ref, prow_ref, nw_ref, y_ref, yn_ref, hs_ref, H):
        @pl.when(pl.program_id(2) == 0)
        def _():
            H[...] = jnp.zeros_like(H)

        a_c, dt_c, causal, tri, cum_c, cum_r, bb, cb, G = _ssd_common(dtc_ref, dtr_ref, pcol_ref, prow_ref, b_ref, c_ref)
        dsk = pcol_ref[2:3, :]
        clast = cum_c[L - 1:L, :]
        bf = b_ref[...]
        for h in range(4):
            sl = slice(HEAD_DIM * h, HEAD_DIM * (h + 1))
            cc = cum_c[:, h:h + 1]
            lm = jnp.exp(jnp.where(causal, cc - cum_r[h:h + 1, :], NEG))
            M = (G * lm).astype(BF16)
            xh = xs_ref[:, sl]
            Xb = (xh * dt_c[:, h:h + 1]).astype(BF16)
            Hh = H[h]
            y = _dot(M, Xb, NN) + jnp.exp(cc) * _dot(cb, Hh.astype(BF16), NN)
            y_ref[:, sl] = y + dsk[:, h:h + 1] * xh
            hs_ref[h] = Hh
            cl = clast[:, h:h + 1]
            Bw = (bf * jnp.exp(cl - cc)).astype(BF16)
            H[h] = jnp.exp(cl) * Hh + _dot(Bw, Xb, TN)
        zv = z_ref[...]
        y2 = y_ref[...] * (zv * _sigmoid(zv))
        r = lax.rsqrt(jnp.mean(y2 * y2, axis=-1, keepdims=True) + EPS)
        yn_ref[...] = (y2 * r * nw_ref[...]).astype(BF16)

    rowi = lambda b, g, i: b * nb + i
    grp = pl.BlockSpec((L, GROUP_W), lambda b, g, i: (rowi(b, g, i), g))
    return pl.pallas_call(
        body,
        out_shape=(jax.ShapeDtypeStruct((T, 1024), F32), jax.ShapeDtypeStruct((T, 1024), BF16),
                   jax.ShapeDtypeStruct((B, SSD_GROUPS, nb, 4, SSD_STATE, HEAD_DIM), F32)),
        grid=(B, SSD_GROUPS, nb),
        in_specs=[grp,
                  pl.BlockSpec((L, SSD_STATE), lambda b, g, i: (rowi(b, g, i), 8 + g)),
                  pl.BlockSpec((L, SSD_STATE), lambda b, g, i: (rowi(b, g, i), 12 + g)),
                  grp,
                  pl.BlockSpec((None, L, 4), lambda b, g, i: (g, rowi(b, g, i), 0)),
                  pl.BlockSpec((None, 4, L), lambda b, g, i: (g, 0, rowi(b, g, i))),
                  pl.BlockSpec((None, 3, 4), lambda b, g, i: (g, 0, 0)),
                  pl.BlockSpec((None, 4, 3), lambda b, g, i: (g, 0, 0)),
                  pl.BlockSpec((1, GROUP_W), lambda b, g, i: (0, g))],
        out_specs=(grp, grp,
                   pl.BlockSpec((None, None, None, 4, SSD_STATE, HEAD_DIM), lambda b, g, i: (b, g, i, 0, 0, 0))),
        scratch_shapes=[pltpu.VMEM((4, SSD_STATE, HEAD_DIM), F32)], name=name,
        compiler_params=_cp("parallel", "parallel", "arbitrary"))(xc, xc, xc, proj, dtc, dtr, pcol, prow, nw)


def _ssd_bwd(name, dyn, Y, xc, proj, dtc, dtr, pcol, prow, nw, hs, B):
    T = xc.shape[0]
    S = T // B
    nb = S // SSD_L
    L = SSD_L

    def body(dyn_ref, y_ref, xs_ref, b_ref, c_ref, z_ref, dtc_ref, dtr_ref, pcol_ref, prow_ref, nw_ref, hs_ref,
             dxs_ref, db_ref, dc_ref, dz_ref, ddt_ref, dpar_ref, dnw_ref, dH):
        @pl.when(pl.program_id(2) == 0)
        def _():
            dH[...] = jnp.zeros_like(dH)
            dpar_ref[...] = jnp.zeros_like(dpar_ref)
            dnw_ref[...] = jnp.zeros_like(dnw_ref)

        a_c, dt_c, causal, tri, cum_c, cum_r, bb, cb, G = _ssd_common(dtc_ref, dtr_ref, pcol_ref, prow_ref, b_ref, c_ref)
        dsk = pcol_ref[2:3, :]
        clast = cum_c[L - 1:L, :]
        bf = b_ref[...]
        cf = c_ref[...]
        Yv = y_ref[...]
        zv = z_ref[...]
        sz = _sigmoid(zv)
        silu = zv * sz
        y2 = Yv * silu
        r = lax.rsqrt(jnp.mean(y2 * y2, axis=-1, keepdims=True) + EPS)
        yhat = y2 * r
        dyv = dyn_ref[...]
        dnw_ref[...] += jnp.sum(dyv * yhat, axis=0, keepdims=True)
        dyhat = dyv * nw_ref[...]
        dy2 = r * (dyhat - yhat * jnp.mean(dyhat * yhat, axis=-1, keepdims=True))
        dY = dy2 * silu
        dz_ref[...] = dy2 * Yv * (sz * (1.0 + zv * (1.0 - sz)))

        lane4 = lax.broadcasted_iota(jnp.int32, (1, 4), 1)
        dG = jnp.zeros((L, L), F32)
        dBs = jnp.zeros((L, SSD_STATE), F32)
        dCs = jnp.zeros((L, SSD_STATE), F32)
        dA = jnp.zeros((L, 4), F32)
        ddtx = jnp.zeros((L, 4), F32)
        ddsk = jnp.zeros((1, 4), F32)
        dcl = jnp.zeros((1, 4), F32)
        for h in range(4):
            sl = slice(HEAD_DIM * h, HEAD_DIM * (h + 1))
            onehot = (lane4 == h).astype(F32)
            cc = cum_c[:, h:h + 1]
            cl = clast[:, h:h + 1]
            lm = jnp.exp(jnp.where(causal, cc - cum_r[h:h + 1, :], NEG))
            M = (G * lm).astype(BF16)
            xh = xs_ref[:, sl]
            dth = dt_c[:, h:h + 1]
            X = xh * dth
            Xb = X.astype(BF16)
            dYh = dY[:, sl]
            dYb = dYh.astype(BF16)
            Hb = hs_ref[h].astype(BF16)
            dHh = dH[h]
            dHb = dHh.astype(BF16)
            alpha = jnp.exp(cc)
            beta = jnp.exp(cl - cc)
            dXoff = beta * _dot(bb, dHb, NN)
            dX = _dot(M, dYb, TN) + dXoff
            dG = dG + _dot(dYb, Xb, NT) * lm
            dCs = dCs + _dot((alpha * dYh).astype(BF16), Hb, NT)
            dBs = dBs + _dot((beta * X).astype(BF16), dHb, NT)
            ypre = Yv[:, sl] - dsk[:, h:h + 1] * xh
            dA_h = (jnp.sum(dYb.astype(F32) * ypre, axis=-1, keepdims=True)
                    - jnp.sum(Xb.astype(F32) * dX, axis=-1, keepdims=True))
            dA = dA + dA_h * onehot
            dcl_h = (jnp.sum(jnp.sum(dHh * (jnp.exp(cl) * hs_ref[h]), axis=-1, keepdims=True), axis=0, keepdims=True)
                     + jnp.sum(jnp.sum(Xb.astype(F32) * dXoff, axis=-1, keepdims=True), axis=0, keepdims=True))
            dcl = dcl + dcl_h * onehot
            ddtx = ddtx + jnp.sum(dX * xh, axis=-1, keepdims=True) * onehot
            ddsk = ddsk + jnp.sum(jnp.sum(dYh * xh, axis=-1, keepdims=True), axis=0, keepdims=True) * onehot
            dxs_ref[:, sl] = dsk[:, h:h + 1] * dYh + dX * dth
            dH[h] = jnp.exp(cl) * dHh + _dot((alpha * cf).astype(BF16), dYb, TN)
        dGb = dG.astype(BF16)
        dc_ref[...] = _dot(dGb, bb, NN) + dCs
        db_ref[...] = _dot(dGb, cb, TN) + dBs
        hp = lax.Precision.HIGHEST
        last = lax.broadcasted_iota(jnp.int32, (L, 1), 0) == L - 1
        dA = dA + jnp.where(last, dcl, 0.0)
        dadt = lax.dot_general(tri, dA, TN, precision=hp, preferred_element_type=F32)
        ddt = dadt * a_c + ddtx
        d_a = jnp.sum(dadt * dt_c, axis=0, keepdims=True)
        ddraw = ddt * _sigmoid(dtc_ref[...] + pcol_ref[0:1, :])
        ddt_ref[...] = ddraw
        dpar_ref[0:1, :] += jnp.sum(ddraw, axis=0, keepdims=True)
        dpar_ref[1:2, :] += d_a * a_c
        dpar_ref[2:3, :] += ddsk

    rowi = lambda b, g, i: b * nb + (nb - 1 - i)
    grp = pl.BlockSpec((L, GROUP_W), lambda b, g, i: (rowi(b, g, i), g))
    st = pl.BlockSpec((L, SSD_STATE), lambda b, g, i: (rowi(b, g, i), g))
    f = jax.ShapeDtypeStruct
    return pl.pallas_call(
        body,
        out_shape=(f((T, 1024), F32), f((T, 512), F32), f((T, 512), F32), f((T, 1024), F32),
                   f((SSD_GROUPS, T, 4), F32), f((B, SSD_GROUPS, 3, 4), F32), f((B, 1, 1024), F32)),
        grid=(B, SSD_GROUPS, nb),
        in_specs=[grp, grp, grp,
                  pl.BlockSpec((L, SSD_STATE), lambda b, g, i: (rowi(b, g, i), 8 + g)),
                  pl.BlockSpec((L, SSD_STATE), lambda b, g, i: (rowi(b, g, i), 12 + g)),
                  grp,
                  pl.BlockSpec((None, L, 4), lambda b, g, i: (g, rowi(b, g, i), 0)),
                  pl.BlockSpec((None, 4, L), lambda b, g, i: (g, 0, rowi(b, g, i))),
                  pl.BlockSpec((None, 3, 4), lambda b, g, i: (g, 0, 0)),
                  pl.BlockSpec((None, 4, 3), lambda b, g, i: (g, 0, 0)),
                  pl.BlockSpec((1, GROUP_W), lambda b, g, i: (0, g)),
                  pl.BlockSpec((None, None, None, 4, SSD_STATE, HEAD_DIM), lambda b, g, i: (b, g, nb - 1 - i, 0, 0, 0))],
        out_specs=(grp, st, st, grp,
                   pl.BlockSpec((None, L, 4), lambda b, g, i: (g, rowi(b, g, i), 0)),
                   pl.BlockSpec((None, None, 3, 4), lambda b, g, i: (b, g, 0, 0)),
                   pl.BlockSpec((None, 1, GROUP_W), lambda b, g, i: (b, 0, g))),
        scratch_shapes=[pltpu.VMEM((4, SSD_STATE, HEAD_DIM), F32)], name=name,
        compiler_params=_cp("parallel", "parallel", "arbitrary"))(dyn, Y, xc, xc, xc, proj, dtc, dtr, pcol, prow, nw, hs)


def _headnorm_fwd(name, proj, col_block, w):
    T = proj.shape[0]

    def body(x_ref, w_ref, o_ref):
        for h in range(ATT_HEADS):
            sl = slice(HEAD_DIM * h, HEAD_DIM * (h + 1))
            xh = x_ref[:, sl]
            r = lax.rsqrt(jnp.mean(xh * xh, axis=-1, keepdims=True) + EPS)
            o_ref[:, sl] = (xh * r * w_ref[...]).astype(BF16)

    return pl.pallas_call(
        body, out_shape=jax.ShapeDtypeStruct((T, 1024), BF16), grid=(T // ROW_T,),
        in_specs=[pl.BlockSpec((ROW_T, 1024), lambda i: (i, col_block)), pl.BlockSpec((1, HEAD_DIM), lambda i: (0, 0))],
        out_specs=pl.BlockSpec((ROW_T, 1024), lambda i: (i, 0)), name=name, compiler_params=_cp("parallel"))(proj, w)


def _headnorm_bwd(name, dn, proj, col_block, w):
    T = proj.shape[0]

    def body(dn_ref, x_ref, w_ref, dx_ref, dw_ref):
        @pl.when(pl.program_id(0) == 0)
        def _():
            dw_ref[...] = jnp.zeros_like(dw_ref)

        dw = jnp.zeros((1, HEAD_DIM), F32)
        for h in range(ATT_HEADS):
            sl = slice(HEAD_DIM * h, HEAD_DIM * (h + 1))
            xh = x_ref[:, sl]
            r = lax.rsqrt(jnp.mean(xh * xh, axis=-1, keepdims=True) + EPS)
            xhat = xh * r
            dnh = dn_ref[:, sl]
            dxhat = dnh * w_ref[...]
            dx_ref[:, sl] = r * (dxhat - xhat * jnp.mean(dxhat * xhat, axis=-1, keepdims=True))
            dw = dw + jnp.sum(dnh * xhat, axis=0, keepdims=True)
        dw_ref[...] += dw

    return pl.pallas_call(
        body, out_shape=(jax.ShapeDtypeStruct((T, 1024), F32), jax.ShapeDtypeStruct((1, HEAD_DIM), F32)),
        grid=(T // ROW_T,),
        in_specs=[pl.BlockSpec((ROW_T, 1024), lambda i: (i, 0)), pl.BlockSpec((ROW_T, 1024), lambda i: (i, col_block)),
                  pl.BlockSpec((1, HEAD_DIM), lambda i: (0, 0))],
        out_specs=(pl.BlockSpec((ROW_T, 1024), lambda i: (i, 0)), pl.BlockSpec((1, HEAD_DIM), lambda i: (0, 0))),
        name=name, compiler_params=_cp("arbitrary"))(dn, proj, w)


def _att_bias(nq):
    j = np.arange(ATT_B)[:, None]
    i = np.arange(ATT_B)[None, :]
    out = np.empty((nq, ATT_B, ATT_B), np.float32)
    for dblk in range(nq):
        dl = ATT_B * dblk + i - j
        cnt = ((dl >= 0) & (dl <= 128)).astype(np.float32)
        cnt += ((dl >= 0) & (dl % 4 == 0) & (dl <= 512))
        cnt += ((dl >= 0) & (dl % 16 == 0) & (dl <= 2048))
        out[dblk] = np.where(cnt > 0, np.log(np.maximum(cnt, 1.0)), NEG)
    return jnp.asarray(out)


def _row_pair(nq):
    def f(r, c):
        first = c <= r
        return jnp.where(first, r, nq - 1 - r), jnp.where(first, c, c - (r + 1))
    return f


def _col_pair(nq):
    def f(r, c):
        first = c < nq - r
        kj = jnp.where(first, r, nq - 1 - r)
        return jnp.where(first, r + c, nq - 1 - r + (c - (nq - r))), kj
    return f


ATT_SCALE = 1.0 / math.sqrt(HEAD_DIM)
ATT_HS = 4
ATT_W = ATT_HS * HEAD_DIM


def _att_maps(nq, qk):
    return dict(
        q_tok=lambda b, g, r, c: (b * nq + qk(r, c)[0], g),
        k_tok=lambda b, g, r, c: (b * nq + qk(r, c)[1], g),
        q_feat=lambda b, g, r, c: (g, b * nq + qk(r, c)[0]),
        k_feat=lambda b, g, r, c: (g, b * nq + qk(r, c)[1]),
        bias=lambda b, g, r, c: (qk(r, c)[0] - qk(r, c)[1], 0, 0),
        lse=lambda b, g, r, c: (g, 0, b * nq + qk(r, c)[0]),
        do_tok=lambda b, g, r, c: (b * nq + qk(r, c)[0], ATT_HS + g))


def _att_fwd(name, kn, qT, vT, bias, B):
    T = kn.shape[0]
    nq = (T // B) // ATT_B
    qk = _row_pair(nq)
    mp = _att_maps(nq, qk)

    def body(k_ref, qT_ref, vT_ref, bias_ref, oT_ref, lse_ref, m_s, l_s, acc_s, s_s):
        qi, kj = qk(pl.program_id(2), pl.program_id(3))

        @pl.when(kj == 0)
        def _():
            m_s[...] = jnp.full_like(m_s, NEG)
            l_s[...] = jnp.zeros_like(l_s)
            acc_s[...] = jnp.zeros_like(acc_s)

        bv = bias_ref[...]
        for h in range(ATT_HS):
            rs = slice(HEAD_DIM * h, HEAD_DIM * (h + 1))
            s_s[h] = _dot(k_ref[:, rs], qT_ref[rs, :], NN)
        for h in range(ATT_HS):
            rs = slice(HEAD_DIM * h, HEAD_DIM * (h + 1))
            s = s_s[h] + bv
            m_prev = m_s[h:h + 1, :]
            m_new = jnp.maximum(m_prev, jnp.max(s, axis=0, keepdims=True))
            alpha = jnp.exp(m_prev - m_new)
            p = jnp.exp(s - m_new)
            l_s[h:h + 1, :] = alpha * l_s[h:h + 1, :] + jnp.sum(p, axis=0, keepdims=True)
            acc_s[rs, :] = alpha * acc_s[rs, :] + _dot(vT_ref[rs, :], p.astype(BF16), NN)
            m_s[h:h + 1, :] = m_new

        @pl.when(kj == qi)
        def _():
            for h in range(ATT_HS):
                rs = slice(HEAD_DIM * h, HEAD_DIM * (h + 1))
                oT_ref[rs, :] = (acc_s[rs, :] / l_s[h:h + 1, :]).astype(BF16)
            lse_ref[...] = m_s[...] + jnp.log(l_s[...])

    tok = (ATT_B, ATT_W)
    feat = (ATT_W, ATT_B)
    return pl.pallas_call(
        body,
        out_shape=(jax.ShapeDtypeStruct((1024, T), BF16), jax.ShapeDtypeStruct((ATT_HEADS // ATT_HS, ATT_HS, T), F32)),
        grid=(B, ATT_HEADS // ATT_HS, nq // 2, nq + 1),
        in_specs=[pl.BlockSpec(tok, mp["k_tok"]), pl.BlockSpec(feat, mp["q_feat"]), pl.BlockSpec(feat, mp["k_feat"]),
                  pl.BlockSpec((None, ATT_B, ATT_B), mp["bias"])],
        out_specs=(pl.BlockSpec(feat, mp["q_feat"]), pl.BlockSpec((None, ATT_HS, ATT_B), mp["lse"])),
        scratch_shapes=[pltpu.VMEM((ATT_HS, ATT_B), F32), pltpu.VMEM((ATT_HS, ATT_B), F32),
                        pltpu.VMEM((ATT_W, ATT_B), F32), pltpu.VMEM((ATT_HS, ATT_B, ATT_B), F32)],
        name=name, compiler_params=_cp("parallel", "parallel", "arbitrary", "arbitrary"))(kn, qT, vT, bias)


def _att_scores(k_ref, qT_ref, v_ref, doT_ref, s_s, dp_s):
    for h in range(ATT_HS):
        rs = slice(HEAD_DIM * h, HEAD_DIM * (h + 1))
        s_s[h] = _dot(k_ref[:, rs], qT_ref[rs, :], NN)
        dp_s[h] = _dot(v_ref[:, rs], doT_ref[rs, :].astype(BF16), NN)


def _att_p_ds(s_s, dp_s, doT_ref, oT_ref, lse_ref, bv, h):
    rs = slice(HEAD_DIM * h, HEAD_DIM * (h + 1))
    delta = jnp.sum(doT_ref[rs, :] * oT_ref[rs, :].astype(F32), axis=0, keepdims=True)
    p = jnp.exp(s_s[h] + bv - lse_ref[h:h + 1, :])
    return p, p * (dp_s[h] - delta)


def _att_bwd_dq(name, kn, qT, vb, knT, bias, doT, oT, lse, B):
    T = kn.shape[0]
    nq = (T // B) // ATT_B
    qk = _row_pair(nq)
    mp = _att_maps(nq, qk)

    def body(k_ref, qT_ref, v_ref, kT_ref, bias_ref, doT_ref, oT_ref, lse_ref, dqT_ref, acc_s, s_s, dp_s):
        qi, kj = qk(pl.program_id(2), pl.program_id(3))

        @pl.when(kj == 0)
        def _():
            acc_s[...] = jnp.zeros_like(acc_s)

        bv = bias_ref[...]
        _att_scores(k_ref, qT_ref, v_ref, doT_ref, s_s, dp_s)
        for h in range(ATT_HS):
            rs = slice(HEAD_DIM * h, HEAD_DIM * (h + 1))
            p, ds = _att_p_ds(s_s, dp_s, doT_ref, oT_ref, lse_ref, bv, h)
            acc_s[rs, :] += _dot(kT_ref[rs, :], ds.astype(BF16), NN)

        @pl.when(kj == qi)
        def _():
            dqT_ref[...] = acc_s[...] * ATT_SCALE

    tok = (ATT_B, ATT_W)
    feat = (ATT_W, ATT_B)
    return pl.pallas_call(
        body, out_shape=jax.ShapeDtypeStruct((1024, T), F32), grid=(B, ATT_HEADS // ATT_HS, nq // 2, nq + 1),
        in_specs=[pl.BlockSpec(tok, mp["k_tok"]), pl.BlockSpec(feat, mp["q_feat"]), pl.BlockSpec(tok, mp["k_tok"]),
                  pl.BlockSpec(feat, mp["k_feat"]), pl.BlockSpec((None, ATT_B, ATT_B), mp["bias"]),
                  pl.BlockSpec(feat, mp["q_feat"]), pl.BlockSpec(feat, mp["q_feat"]),
                  pl.BlockSpec((None, ATT_HS, ATT_B), mp["lse"])],
        out_specs=pl.BlockSpec(feat, mp["q_feat"]),
        scratch_shapes=[pltpu.VMEM((ATT_W, ATT_B), F32), pltpu.VMEM((ATT_HS, ATT_B, ATT_B), F32),
                        pltpu.VMEM((ATT_HS, ATT_B, ATT_B), F32)],
        name=name, compiler_params=_cp("parallel", "parallel", "arbitrary", "arbitrary"))(
            kn, qT, vb, knT, bias, doT, oT, lse)


def _att_bwd_dkv(name, kn, qT, vb, qn, bias, doT, oT, lse, dyn, B):
    T = kn.shape[0]
    nq = (T // B) // ATT_B
    qk = _col_pair(nq)
    mp = _att_maps(nq, qk)

    def body(k_ref, qT_ref, v_ref, q_ref, bias_ref, doT_ref, oT_ref, lse_ref, do_ref, dk_ref, dv_ref, dk_s, dv_s,
             s_s, dp_s):
        qi, kj = qk(pl.program_id(2), pl.program_id(3))

        @pl.when(qi == kj)
        def _():
            dk_s[...] = jnp.zeros_like(dk_s)
            dv_s[...] = jnp.zeros_like(dv_s)

        bv = bias_ref[...]
        _att_scores(k_ref, qT_ref, v_ref, doT_ref, s_s, dp_s)
        for h in range(ATT_HS):
            rs = slice(HEAD_DIM * h, HEAD_DIM * (h + 1))
            p, ds = _att_p_ds(s_s, dp_s, doT_ref, oT_ref, lse_ref, bv, h)
            dv_s[h] += _dot(p.astype(BF16), do_ref[:, rs].astype(BF16), NN)
            dk_s[h] += _dot(ds.astype(BF16), q_ref[:, rs], NN)

        @pl.when(qi == nq - 1)
        def _():
            for h in range(ATT_HS):
                rs = slice(HEAD_DIM * h, HEAD_DIM * (h + 1))
                dk_ref[:, rs] = dk_s[h] * ATT_SCALE
                dv_ref[:, rs] = dv_s[h]

    tok = (ATT_B, ATT_W)
    feat = (ATT_W, ATT_B)
    osh = jax.ShapeDtypeStruct((T, 1024), F32)
    return pl.pallas_call(
        body, out_shape=(osh, osh), grid=(B, ATT_HEADS // ATT_HS, nq // 2, nq + 1),
        in_specs=[pl.BlockSpec(tok, mp["k_tok"]), pl.BlockSpec(feat, mp["q_feat"]), pl.BlockSpec(tok, mp["k_tok"]),
                  pl.BlockSpec(tok, mp["q_tok"]), pl.BlockSpec((None, ATT_B, ATT_B), mp["bias"]),
                  pl.BlockSpec(feat, mp["q_feat"]), pl.BlockSpec(feat, mp["q_feat"]),
                  pl.BlockSpec((None, ATT_HS, ATT_B), mp["lse"]), pl.BlockSpec(tok, mp["do_tok"])],
        out_specs=(pl.BlockSpec(tok, mp["k_tok"]), pl.BlockSpec(tok, mp["k_tok"])),
        scratch_shapes=[pltpu.VMEM((ATT_HS, ATT_B, HEAD_DIM), F32), pltpu.VMEM((ATT_HS, ATT_B, HEAD_DIM), F32),
                        pltpu.VMEM((ATT_HS, ATT_B, ATT_B), F32), pltpu.VMEM((ATT_HS, ATT_B, ATT_B), F32)],
        name=name, compiler_params=_cp("parallel", "parallel", "arbitrary", "arbitrary"))(
            kn, qT, vb, qn, bias, doT, oT, lse, dyn)


def _group_cols(v):
    return v.reshape(SSD_GROUPS, 4)


def _ssd_params(p):
    rows = jnp.stack([_group_cols(p["dt_bias"]), _group_cols(p["a_log"]), _group_cols(p["d_skip"])], axis=1)
    return rows, jnp.swapaxes(rows, 1, 2)


def _mixer_fwd(tag, x1, p, weights, bias, B):
    T = x1.shape[0]
    S = T // B
    nt = T // ROW_T
    h2 = _rms_fwd(tag + "_mixrms", x1, p["mix_norm"][None])
    wi = weights("win", h2)
    win, cw = wi["win"], wi["cw"]
    proj = _mm(tag + "_proj",
               [(h2, pl.BlockSpec((ROW_T, D_MODEL), lambda j, i, k: (i, 0)),
                 win, pl.BlockSpec((D_MODEL, PROJ_TN), lambda j, i, k: (0, j)))],
               jax.ShapeDtypeStruct((T, IN_PAD), F32), pl.BlockSpec((ROW_T, PROJ_TN), lambda j, i, k: (i, j)),
               (IN_PAD // PROJ_TN, nt, 1), NN, (ROW_T, PROJ_TN))
    xbc = proj[:, COL_XBC:COL_Q].reshape(B, S, CONV_DIM)
    xpad = jnp.pad(xbc, ((0, 0), (PAD_R, PAD_R), (0, 0)))
    xc = _conv_fwd(tag + "_conv", xpad, cw, p["conv_b"][None]).reshape(T, CONV_DIM)
    dtraw = proj[:, COL_DT:COL_DT + SSD_HEADS].reshape(T, SSD_GROUPS, 4)
    dtc = jnp.transpose(dtraw, (1, 0, 2))
    dtr = jnp.transpose(dtraw, (1, 2, 0))
    pcol, prow = _ssd_params(p)
    Y, y_ssd, hs = _ssd_fwd(tag + "_ssd", xc, proj, dtc, dtr, pcol, prow, p["ssd_norm"][None], B)
    qn = _headnorm_fwd(tag + "_qn", proj, COL_Q // 1024, p["q_norm"][None])
    kn = _headnorm_fwd(tag + "_kn", proj, COL_K // 1024, p["k_norm"][None])
    qT = (qn * ATT_SCALE).T
    vb = proj[:, COL_V:COL_V + 1024].astype(BF16)
    oT, lse = _att_fwd(tag + "_att", kn, qT, vb.T, bias, B)
    ymix = jnp.concatenate([y_ssd, oT.T], axis=1)
    rest = weights("rest", ymix)
    x2 = _mm(tag + "_out",
             [(ymix, pl.BlockSpec((ROW_T, MIX_SH), lambda i, n, k: (i, k)),
               rest["wout"], pl.BlockSpec((None, MIX_SH, D_MODEL), lambda i, n, k: (k, 0, 0)))],
             jax.ShapeDtypeStruct((T, D_MODEL), F32), pl.BlockSpec((ROW_T, D_MODEL), lambda i, n, k: (i, 0)),
             (nt, 1, N_SHARD), NN, (ROW_T, D_MODEL),
             res=(x1, pl.BlockSpec((ROW_T, D_MODEL), lambda i, n, k: (i, 0))))
    saved = dict(x1=x1, h2=h2, proj=proj, xpad=xpad, xc=xc, dtc=dtc, dtr=dtr, Y=Y, hs=hs,
                 qn=qn, kn=kn, qT=qT, vb=vb, oT=oT, lse=lse, ymix=ymix, win=win, cw=cw, wout=rest["wout"])
    return x2, saved


def _mixer_bwd(tag, dx2, sv, p, bias, B):
    T = dx2.shape[0]
    S = T // B
    nt = T // ROW_T
    sg = {}
    dymix = _mm(tag + "_dymix",
                [(dx2, pl.BlockSpec((ROW_T, D_MODEL), lambda n, i, k: (i, 0)),
                  sv["wout"], pl.BlockSpec((None, MIX_SH, D_MODEL), lambda n, i, k: (n, 0, 0)))],
                jax.ShapeDtypeStruct((T, MIX_W), F32), pl.BlockSpec((ROW_T, MIX_SH), lambda n, i, k: (i, n)),
                (N_SHARD, nt, 1), NT, (ROW_T, MIX_SH))
    gwout = _mm(tag + "_dwout",
                [(sv["ymix"], pl.BlockSpec((ROW_T, MIX_SH), lambda m, n, k: (k, m)),
                  dx2, pl.BlockSpec((ROW_T, D_MODEL), lambda m, n, k: (k, 0)))],
                jax.ShapeDtypeStruct((N_SHARD, MIX_SH, D_MODEL), BF16),
                pl.BlockSpec((None, MIX_SH, D_MODEL), lambda m, n, k: (m, 0, 0)),
                (N_SHARD, 1, nt), TN, (MIX_SH, D_MODEL))
    proj = sv["proj"]
    doT = dymix[:, 1024:].T
    dqn = _att_bwd_dq(tag + "_attdq", sv["kn"], sv["qT"], sv["vb"], sv["kn"].T, bias, doT, sv["oT"], sv["lse"], B).T
    dkn, dv = _att_bwd_dkv(tag + "_attdkv", sv["kn"], sv["qT"], sv["vb"], sv["qn"], bias, doT, sv["oT"], sv["lse"],
                           dymix, B)
    dq, sg["q_norm"] = _headnorm_bwd(tag + "_qnb", dqn, proj, COL_Q // 1024, p["q_norm"][None])
    dk, sg["k_norm"] = _headnorm_bwd(tag + "_knb", dkn, proj, COL_K // 1024, p["k_norm"][None])
    pcol, prow = _ssd_params(p)
    dxs, dB, dC, dz, ddt, dpar, dnw = _ssd_bwd(tag + "_ssdb", dymix, sv["Y"], sv["xc"], proj, sv["dtc"], sv["dtr"],
                                               pcol, prow, p["ssd_norm"][None], sv["hs"], B)
    dpar = jnp.sum(dpar, axis=0)
    sg["dt_bias"] = dpar[:, 0, :].reshape(SSD_HEADS)
    sg["a_log"] = dpar[:, 1, :].reshape(SSD_HEADS)
    sg["d_skip"] = dpar[:, 2, :].reshape(SSD_HEADS)
    sg["ssd_norm"] = jnp.sum(dnw, axis=0)
    dxc = jnp.concatenate([dxs, dB, dC], axis=1).reshape(B, S, CONV_DIM)
    dxc_pad = jnp.pad(dxc, ((0, 0), (0, PAD_R), (0, 0)))
    dxbc, sg["conv_w"], sg["conv_b"] = _conv_bwd(tag + "_convb", sv["xpad"], dxc_pad, sv["cw"], p["conv_b"][None])
    ddt16 = jnp.transpose(ddt, (1, 0, 2)).reshape(T, SSD_HEADS)
    dproj = jnp.concatenate([dz, dxbc.reshape(T, CONV_DIM), dq, dk, dv, ddt16,
                             jnp.zeros((T, IN_PAD - COL_DT - SSD_HEADS), F32)], axis=1).astype(BF16)
    win = sv["win"]
    gwin = _mm(tag + "_dwin",
               [(sv["h2"], pl.BlockSpec((ROW_T, D_MODEL), lambda n, m, k: (k, 0)),
                 dproj, pl.BlockSpec((ROW_T, PROJ_TN), lambda n, m, k: (k, n)))],
               jax.ShapeDtypeStruct((D_MODEL, IN_PAD), BF16), pl.BlockSpec((D_MODEL, PROJ_TN), lambda n, m, k: (0, n)),
               (IN_PAD // PROJ_TN, 1, nt), TN, (D_MODEL, PROJ_TN))
    dh2 = _mm(tag + "_dh2",
              [(dproj, pl.BlockSpec((ROW_T, PROJ_TN), lambda i, n, k: (i, k)),
                win, pl.BlockSpec((D_MODEL, PROJ_TN), lambda i, n, k: (0, k)))],
              jax.ShapeDtypeStruct((T, D_MODEL), F32), pl.BlockSpec((ROW_T, D_MODEL), lambda i, n, k: (i, 0)),
              (nt, 1, IN_PAD // PROJ_TN), NT, (ROW_T, D_MODEL))
    dx1, sg["mix_norm"] = _rms_bwd(tag + "_mixrmsb", dh2, sv["x1"], p["mix_norm"][None], dx2)
    return dx1, sg, gwout, gwin


def _win_pack(w):
    return jnp.concatenate([w[:, :3072], w[:, 3088:], w[:, 3072:3088],
                            jnp.zeros((w.shape[0], IN_PAD - IN_PROJ), w.dtype)], axis=1)


def _win_unpack(g):
    return jnp.concatenate([g[:, :3072], g[:, COL_DT:COL_DT + SSD_HEADS], g[:, 3072:COL_DT]], axis=1)


def _local_step(x, target, small, weights, scatter, B):
    T = x.shape[0]
    bias = _att_bias((T // B) // ATT_B)
    saved = []
    h = x
    for l in range(DEPTH):
        tag = "l%d" % l
        p = {k: v[l] for k, v in small.items()}
        w1 = weights(l, "ffn1", h)
        x1, ffn1 = _ffn_fwd(tag + "f1", h, p["ffn1_norm"][None], w1["g1"], w1["u1"], w1["d1"])
        x2, sv = _mixer_fwd(tag, x1, p, functools.partial(weights, l), bias, B)
        w2 = weights(l, "rest", x2)
        h, ffn2 = _ffn_fwd(tag + "f2", x2, p["ffn2_norm"][None], w2["g2"], w2["u2"], w2["d2"])
        saved.append((ffn1, sv, ffn2, w1, w2))
    d, lsum = _loss_grad("loss", h, target)
    sgrads = [None] * DEPTH
    for l in reversed(range(DEPTH)):
        tag = "l%db" % l
        p = {k: v[l] for k, v in small.items()}
        ffn1, sv, ffn2, w1, w2 = saved[l]
        sg = {}
        d, sg["ffn2_norm"] = _ffn_bwd(tag + "f2", d, ffn2, p["ffn2_norm"][None], w2["g2"], w2["u2"], w2["d2"],
                                      lambda gg, gu, gd, c, l=l: scatter(l, "ffn2", dict(g2=gg, u2=gu, d2=gd), c))
        d, sgm, gwout, gwin = _mixer_bwd(tag, d, sv, p, bias, B)
        sg.update(sgm)
        d = scatter(l, "mixer", dict(wout=gwout, win=gwin), d)
        d, sg["ffn1_norm"] = _ffn_bwd(tag + "f1", d, ffn1, p["ffn1_norm"][None], w1["g1"], w1["u1"], w1["d1"],
                                      lambda gg, gu, gd, c, l=l: scatter(l, "ffn1", dict(g1=gg, u1=gu, d1=gd), c))
        sgrads[l] = sg
    return lsum, d, sgrads


MESH = pl.DeviceIdType.MESH
ANY = pl.BlockSpec(memory_space=pl.ANY)


def _place():
    return lax.axis_index("x"), lax.axis_index("y"), lax.axis_index("c")


def _other_chips(x, y):
    return [(1 - x, y), (x, 1 - y), (1 - x, 1 - y)]


HBM = pl.BlockSpec(memory_space=pltpu.HBM)
SEM = pl.BlockSpec(memory_space=pltpu.SEMAPHORE)
EFFECT = pltpu.SideEffectType.DATAFLOW_SIDE_EFFECTING


def _hbm(a):
    return pltpu.with_memory_space_constraint(a, pltpu.HBM)


def _exchange(gather, src, land, send, recv, n, act):
    x, y, c = _place()
    for k, (px, py) in enumerate(_other_chips(x, y)):
        for a in range(n):
            if gather:
                s_out, d_out, d_in = src[a], land[a].at[2 * x + y], land[a].at[2 * px + py]
            else:
                s_out, d_out, d_in = src[a].at[2 * px + py], land[a].at[k], land[a].at[k]
            act(pltpu.make_async_remote_copy(
                src_ref=s_out, dst_ref=d_out if act is _start else d_in, send_sem=send.at[k * n + a],
                recv_sem=recv.at[k * n + a], device_id=(px, py, c), device_id_type=MESH))


def _start(cp):
    cp.start()


def _finish(cp):
    cp.wait_send()
    cp.wait_recv()


def _exchange_start(name, gather, srcs, carry):
    n = len(srcs)
    lands = [lax.empty(((N_SHARD,) + s.shape) if gather else ((3,) + s.shape[1:]), s.dtype) for s in srcs]

    def body(*refs):
        _exchange(gather, refs[:n], refs[n:2 * n], refs[2 * n + 1], refs[2 * n + 2], n, _start)

    ops = [_hbm(a) for a in list(srcs) + lands + [carry]]
    out = pl.pallas_call(
        body, name=name,
        out_shape=(pltpu.SemaphoreType.DMA((3 * n,)), pltpu.SemaphoreType.DMA((3 * n,)),
                   *[pltpu.HBM(a.shape, a.dtype) for a in ops]),
        in_specs=[HBM] * len(ops), out_specs=(SEM, SEM, *[HBM] * len(ops)),
        input_output_aliases={i: 2 + i for i in range(len(ops))},
        compiler_params=pltpu.CompilerParams(has_side_effects=EFFECT))(*ops)
    return dict(gather=gather, send=out[0], recv=out[1], srcs=list(out[2:2 + n]), lands=list(out[2 + n:2 + 2 * n])), out[-1]


def _exchange_wait(name, ex, after):
    n = len(ex["srcs"])
    gather = ex["gather"]

    def body(*refs):
        _exchange(gather, refs[:n], refs[n:2 * n], refs[2 * n], refs[2 * n + 1], n, _finish)

    ops = ex["srcs"] + ex["lands"]
    out = pl.pallas_call(
        body, name=name, out_shape=[pltpu.HBM(a.shape, a.dtype) for a in ops],
        in_specs=[HBM] * len(ops) + [SEM, SEM, ANY], out_specs=[HBM] * len(ops),
        input_output_aliases={i: i for i in range(len(ops))},
        compiler_params=pltpu.CompilerParams(has_side_effects=EFFECT))(*ops, ex["send"], ex["recv"], after)
    return list(out[:n]), list(out[n:])


def _swap_sibling(parts):
    n = len(parts)

    def body(*refs):
        src, dst = refs[:n], refs[n:2 * n]
        send, recv = refs[2 * n:]
        x, y, c = _place()
        cps = [pltpu.make_async_remote_copy(src_ref=src[a], dst_ref=dst[a], send_sem=send.at[a], recv_sem=recv.at[a],
                                            device_id=(x, y, 1 - c), device_id_type=MESH) for a in range(n)]
        for cp in cps:
            cp.start()
        for cp in cps:
            cp.wait_recv()
        for cp in cps:
            cp.wait_send()

    return pl.pallas_call(
        body, out_shape=[jax.ShapeDtypeStruct(p.shape, p.dtype) for p in parts],
        in_specs=[ANY] * n, out_specs=[ANY] * n,
        scratch_shapes=[pltpu.SemaphoreType.DMA((n,)), pltpu.SemaphoreType.DMA((n,))],
        name="swap_sibling")(*parts)


def _allreduce_small(name, v):
    R = v.shape[0]

    def body(v_ref, o_ref, buf, send, recv):
        x, y, c = _place()
        me = 4 * x + 2 * y + c
        buf[me] = v_ref[...]
        cps = []
        for k in range(1, 8):
            fx, fy, fc = (k >> 2) & 1, (k >> 1) & 1, k & 1
            px = 1 - x if fx else x
            py = 1 - y if fy else y
            pc = 1 - c if fc else c
            cp = pltpu.make_async_remote_copy(src_ref=v_ref, dst_ref=buf.at[me], send_sem=send.at[k - 1],
                                              recv_sem=recv.at[k - 1], device_id=(px, py, pc), device_id_type=MESH)
            cp.start()
            cps.append((cp, 4 * px + 2 * py + pc))
        for k, (cp, peer) in enumerate(cps):
            pltpu.make_async_remote_copy(src_ref=v_ref, dst_ref=buf.at[peer], send_sem=send.at[k], recv_sem=recv.at[k],
                                         device_id=(x, y, c), device_id_type=MESH).wait_recv()
        for cp, _ in cps:
            cp.wait_send()
        acc = buf[0]
        for d in range(1, 8):
            acc = acc + buf[d]
        o_ref[...] = acc

    return pl.pallas_call(
        body, out_shape=jax.ShapeDtypeStruct((R, 128), F32),
        in_specs=[pl.BlockSpec(memory_space=pltpu.VMEM)], out_specs=pl.BlockSpec(memory_space=pltpu.VMEM),
        scratch_shapes=[pltpu.VMEM((8, R, 128), F32), pltpu.SemaphoreType.DMA((7,)), pltpu.SemaphoreType.DMA((7,))],
        name=name)(v)


def _row_tile(r):
    for t in (256, 128, 64, 32, 16, 8):
        if r % t == 0:
            return t
    raise ValueError(r)


def _sum4(name, own, got):
    R, C = own.shape
    tr = _row_tile(R)

    def body(o_ref, g_ref, s_ref):
        s = o_ref[...].astype(F32)
        for k in range(3):
            s = s + g_ref[k].astype(F32)
        s_ref[...] = s

    return pl.pallas_call(
        body, out_shape=jax.ShapeDtypeStruct((R, C), F32), grid=(R // tr,),
        in_specs=[pl.BlockSpec((tr, C), lambda i: (i, 0)), pl.BlockSpec((3, tr, C), lambda i: (0, i, 0))],
        out_specs=pl.BlockSpec((tr, C), lambda i: (i, 0)), name=name, compiler_params=_cp("parallel"))(own, got)


def _adamw(name, w, gparts, m, v):
    R, C = w.shape
    tr = _row_tile(R)
    ng = len(gparts)
    c1 = 1.0 - ADAM_B1 ** ADAM_STEP
    c2 = 1.0 - ADAM_B2 ** ADAM_STEP

    def body(*refs):
        w_ref = refs[0]
        g_refs = refs[1:1 + ng]
        m_ref, v_ref, go_ref, d_ref, mo_ref, vo_ref = refs[1 + ng:]
        g = g_refs[0][...]
        for r in g_refs[1:]:
            g = g + r[...]
        mn = ADAM_B1 * m_ref[...] + (1.0 - ADAM_B1) * g
        vn = ADAM_B2 * v_ref[...] + (1.0 - ADAM_B2) * (g * g)
        go_ref[...] = g
        mo_ref[...] = mn
        vo_ref[...] = vn
        d_ref[...] = -ADAM_LR * ((mn / c1) / (jnp.sqrt(vn / c2) + ADAM_EPS) + ADAM_WD * w_ref[...])

    blk = pl.BlockSpec((tr, C), lambda i: (i, 0))
    osh = jax.ShapeDtypeStruct((R, C), F32)
    return pl.pallas_call(
        body, out_shape=(osh, osh, osh, osh), grid=(R // tr,), in_specs=[blk] * (3 + ng), out_specs=(blk,) * 4,
        name=name, compiler_params=_cp("parallel"))(w, *gparts, m, v)


def _adamw_layers(name, w, sums, m, v):
    R2, C = w.shape
    R = R2 // DEPTH
    tr = _row_tile(R)
    nr = R // tr
    c1 = 1.0 - ADAM_B1 ** ADAM_STEP
    c2 = 1.0 - ADAM_B2 ** ADAM_STEP

    def body(w_ref, a0, b0, a1, b1, m_ref, v_ref, go_ref, d_ref, mo_ref, vo_ref):
        g = jnp.where(pl.program_id(0) == 0, a0[...] + b0[...], a1[...] + b1[...])
        mn = ADAM_B1 * m_ref[...] + (1.0 - ADAM_B1) * g
        vn = ADAM_B2 * v_ref[...] + (1.0 - ADAM_B2) * (g * g)
        go_ref[...] = g
        mo_ref[...] = mn
        vo_ref[...] = vn
        d_ref[...] = -ADAM_LR * ((mn / c1) / (jnp.sqrt(vn / c2) + ADAM_EPS) + ADAM_WD * w_ref[...])

    blk = pl.BlockSpec((tr, C), lambda l, i: (l * nr + i, 0))
    lay0 = pl.BlockSpec((tr, C), lambda l, i: (jnp.where(l == 0, i, nr - 1), 0))
    lay1 = pl.BlockSpec((tr, C), lambda l, i: (jnp.where(l == 1, i, 0), 0))
    osh = jax.ShapeDtypeStruct((R2, C), F32)
    return pl.pallas_call(
        body, out_shape=(osh, osh, osh, osh), grid=(DEPTH, nr),
        in_specs=[blk, lay0, lay0, lay1, lay1, blk, blk], out_specs=(blk,) * 4,
        name=name, compiler_params=_cp("arbitrary", "arbitrary"))(w, *sums[0], *sums[1], m, v)


BIG = [("ffn1_w_gate", "g1"), ("ffn1_w_up", "u1"), ("ffn1_w_down", "d1"), ("w_in", "win"), ("w_out", "wout"),
       ("ffn2_w_gate", "g2"), ("ffn2_w_up", "u2"), ("ffn2_w_down", "d2")]
SMALL = ["ffn1_norm", "mix_norm", "conv_b", "dt_bias", "a_log", "d_skip", "ssd_norm", "q_norm", "k_norm", "ffn2_norm"]
WEIGHTS = ["ffn1_norm", "ffn1_w_gate", "ffn1_w_up", "ffn1_w_down", "mix_norm", "w_in", "conv_w", "conv_b", "dt_bias",
           "a_log", "d_skip", "ssd_norm", "q_norm", "k_norm", "w_out", "ffn2_norm", "ffn2_w_gate", "ffn2_w_up",
           "ffn2_w_down"]
CONV_SH = CONV_DIM // N_SHARD
GATHER_GROUPS = [(0, "ffn1", ["g1", "u1", "d1"]), (0, "win", ["win", "cw"]), (0, "rest", ["wout", "g2", "u2", "d2"]),
                 (1, "all", ["g1", "u1", "d1", "win", "cw", "wout", "g2", "u2", "d2"])]


def _pad128(v):
    v = v.reshape(-1)
    return jnp.pad(v, (0, (-v.shape[0]) % 128))


def _pack(pieces):
    flat, offs, pos = [], [], 0
    for p in pieces:
        q = _pad128(p.astype(F32))
        offs.append(pos)
        pos += q.shape[0] // 128
        flat.append(q)
    total = -(-pos // 8) * 8
    out = jnp.concatenate(flat + [jnp.zeros(((total - pos) * 128,), F32)]).reshape(total, 128)
    return out, offs


def _unpack(packed, offs, shapes):
    out = []
    for off, shp in zip(offs, shapes):
        n = int(np.prod(shp))
        rows = -(-n // 128)
        out.append(packed[off:off + rows].reshape(-1)[:n].reshape(shp))
    return out


def kernel(x, ffn1_norm, ffn1_w_gate, ffn1_w_up, ffn1_w_down, mix_norm, w_in, conv_w, conv_b, dt_bias, a_log, d_skip, ssd_norm, q_norm, k_norm, w_out, ffn2_norm, ffn2_w_gate, ffn2_w_up, ffn2_w_down, loss_target, m_ffn1_norm, m_ffn1_w_gate, m_ffn1_w_up, m_ffn1_w_down, m_mix_norm, m_w_in, m_conv_w, m_conv_b, m_dt_bias, m_a_log, m_d_skip, m_ssd_norm, m_q_norm, m_k_norm, m_w_out, m_ffn2_norm, m_ffn2_w_gate, m_ffn2_w_up, m_ffn2_w_down, v_ffn1_norm, v_ffn1_w_gate, v_ffn1_w_up, v_ffn1_w_down, v_mix_norm, v_w_in, v_conv_w, v_conv_b, v_dt_bias, v_a_log, v_d_skip, v_ssd_norm, v_q_norm, v_k_norm, v_w_out, v_ffn2_norm, v_ffn2_w_gate, v_ffn2_w_up, v_ffn2_w_down):
    A = dict(locals())
    ix, iy, ic = _place()
    me = 2 * ix + iy
    B, S, _ = x.shape
    T = B * S

    own = {key: A[name].astype(BF16) for name, key in BIG}
    own["cw"] = conv_w
    exs, first_norm = [], ffn1_norm
    for gi, (l, _, keys) in enumerate(GATHER_GROUPS):
        ex, first_norm = _exchange_start("gather_start%d" % gi, True, [own[key][l] for key in keys], first_norm)
        exs.append(ex)
    landed = {}

    def weights(l, group, after):
        gi = [i for i, (gl, gname, _) in enumerate(GATHER_GROUPS) if gl == l and gname in (group, "all")][0]
        if gi not in landed:
            srcs, lands = _exchange_wait("gather_wait%d" % gi, exs[gi], after)
            landed[gi] = {}
            for key, mine, land in zip(GATHER_GROUPS[gi][2], srcs, lands):
                full = lax.dynamic_update_slice(land, mine[None], (me, 0, 0))
                if key == "win":
                    full = _win_pack(jnp.concatenate([full[j] for j in range(N_SHARD)], axis=1))
                if key == "cw":
                    full = jnp.transpose(full, (1, 0, 2)).reshape(CONV_K, CONV_DIM)
                landed[gi][key] = full
        return landed[gi]

    pending = []

    def scatter(l, group, grads, carry):
        keys = sorted(grads)
        arrs = [grads[key] for key in keys]
        if "win" in grads:
            arrs[keys.index("win")] = jnp.transpose(_win_unpack(grads["win"]).reshape(D_MODEL, N_SHARD, IN_SH), (1, 0, 2))
        ex, carry = _exchange_start("scatter_start_l%d_%s" % (l, group), False, arrs, carry)
        pending.append((l, keys, ex))
        return carry

    small = {name: A[name] for name in SMALL}
    small["ffn1_norm"] = first_norm
    lsum, dx, sgrads = _local_step(x.reshape(T, D_MODEL), loss_target.reshape(T, D_MODEL), small, weights, scatter, B)

    names = SMALL + ["conv_w"]
    shapes = [A[n].shape for n in SMALL] + [(DEPTH, CONV_K, CONV_DIM), ()]
    pieces = [jnp.stack([sgrads[l][n].reshape(shp[1:]) for l in range(DEPTH)]) for n, shp in zip(names, shapes)]
    pieces.append(0.5 / D_MODEL * jnp.sum(lsum))
    packed, offs = _pack(pieces)
    red = _allreduce_small("allreduce_small", packed)
    red = _unpack(red, offs, shapes)
    loss = red[-1]
    sg = dict(zip(names, red[:-1]))

    sums, after = {}, dx
    for idx, (l, keys, ex) in enumerate(pending):
        srcs, lands = _exchange_wait("scatter_wait%d" % idx, ex, after)
        for key, g, got in zip(keys, srcs, lands):
            mine = lax.dynamic_index_in_dim(g, me, axis=0, keepdims=False)
            sums[key, l] = after = _sum4("sum_%s_l%d" % (key, l), mine, got)
    order = [(key, l) for _, key in BIG for l in range(DEPTH)]
    theirs = dict(zip(order, _swap_sibling([sums[k] for k in order])))

    out = {}
    for name, key in BIG:
        shp = A[name].shape
        flat = lambda a: a.reshape(shp[0] * shp[1], shp[2])
        res = _adamw_layers("adamw_" + key, flat(A[name]), [(sums[key, l], theirs[key, l]) for l in range(DEPTH)],
                            flat(A["m_" + name]), flat(A["v_" + name]))
        out[name] = [r.reshape(shp) for r in res]

    wp, offs = _pack([A[n] for n in SMALL])
    gp, _ = _pack([sg[n] for n in SMALL])
    mp, _ = _pack([A["m_" + n] for n in SMALL])
    vp, _ = _pack([A["v_" + n] for n in SMALL])
    res = _adamw("adamw_small", wp, [gp], mp, vp)
    shapes = [A[n].shape for n in SMALL]
    res = [_unpack(r, offs, shapes) for r in res]
    for i, n in enumerate(SMALL):
        out[n] = [res[q][i] for q in range(4)]
    gcw = lax.dynamic_slice_in_dim(sg["conv_w"], me * CONV_SH, CONV_SH, axis=2)
    flat = lambda a: a.reshape(DEPTH * CONV_K, CONV_SH)
    res = _adamw("adamw_conv_w", flat(conv_w), [flat(gcw)], flat(m_conv_w), flat(v_conv_w))
    out["conv_w"] = [r.reshape(conv_w.shape) for r in res]

    outs = [loss, dx.reshape(B, S, D_MODEL)]
    for q in range(4):
        outs += [out[n][q] for n in WEIGHTS]
    return tuple(outs)
```

```python
import functools
import math

import numpy as np
import jax
import jax.numpy as jnp
from jax import lax
from jax.experimental import pallas as pl
from jax.experimental.pallas import tpu as pltpu

F32 = jnp.float32
BF16 = jnp.bfloat16

D_MODEL = 1024
DEPTH = 2
N_SHARD = 4
D_FF = 2816
FF_SH = D_FF // N_SHARD
SSD_HEADS = 16
HEAD_DIM = 64
SSD_GROUPS = 4
GROUP_W = 256
SSD_STATE = 128
CONV_K = 4
CONV_DIM = 2048
ATT_HEADS = 16
MIX_W = 2048
MIX_SH = MIX_W // N_SHARD
IN_PROJ = 6160
IN_SH = IN_PROJ // N_SHARD
IN_PAD = 6272
PROJ_TN = 896
COL_Z, COL_XBC, COL_Q, COL_K, COL_V, COL_DT = 0, 1024, 3072, 4096, 5120, 6144
EPS = 1e-6
NEG = -1e30
SSD_L = 256
ATT_B = 256
ROW_T = 512
CONV_CT = 256
CONV_R = 256
PAD_R = 8

ADAM_LR, ADAM_B1, ADAM_B2, ADAM_EPS, ADAM_WD, ADAM_STEP = 0.001, 0.9, 0.999, 1e-08, 0.01, 10

NN = (((1,), (0,)), ((), ()))
NT = (((1,), (1,)), ((), ()))
TN = (((0,), (0,)), ((), ()))

VMEM_LIMIT = 56 * 1024 * 1024


def _cp(*sem):
    return pltpu.CompilerParams(dimension_semantics=sem, vmem_limit_bytes=VMEM_LIMIT)


def _dot(a, b, dims):
    return lax.dot_general(a, b, dims, preferred_element_type=F32)


def _sigmoid(x):
    return 0.5 * jnp.tanh(0.5 * x) + 0.5


def _softplus(x):
    return jnp.maximum(x, 0.0) + jnp.log(1.0 + jnp.exp(-jnp.abs(x)))


def _mm(name, pairs, out_shape, out_spec, grid, dims, acc_shape, res=None, scale=1.0):
    nk = grid[2]
    npair = len(pairs)

    def body(*refs):
        ab = refs[:2 * npair]
        pos = 2 * npair
        res_ref = None
        if res is not None:
            res_ref = refs[pos]
            pos += 1
        out_ref, acc = refs[pos], refs[pos + 1]
        k = pl.program_id(2)

        @pl.when(k == 0)
        def _():
            acc[...] = jnp.zeros_like(acc)

        s = None
        for p in range(npair):
            d = _dot(ab[2 * p][...].astype(BF16), ab[2 * p + 1][...].astype(BF16), dims)
            s = d if s is None else s + d
        acc[...] += s

        @pl.when(k == nk - 1)
        def _():
            r = acc[...]
            if scale != 1.0:
                r = r * scale
            if res_ref is not None:
                r = r + res_ref[...]
            out_ref[...] = r.astype(out_ref.dtype)

    args, specs = [], []
    for a, a_spec, b, b_spec in pairs:
        args += [a, b]
        specs += [a_spec, b_spec]
    if res is not None:
        args.append(res[0])
        specs.append(res[1])
    return pl.pallas_call(
        body, out_shape=out_shape, grid=grid, in_specs=specs, out_specs=out_spec,
        scratch_shapes=[pltpu.VMEM(acc_shape, F32)], name=name,
        compiler_params=_cp("parallel", "parallel", "arbitrary"))(*args)


def _rms_fwd(name, x, w):
    T = x.shape[0]

    def body(x_ref, w_ref, o_ref):
        xv = x_ref[...]
        r = lax.rsqrt(jnp.mean(xv * xv, axis=-1, keepdims=True) + EPS)
        o_ref[...] = (xv * r * w_ref[...]).astype(BF16)

    return pl.pallas_call(
        body, out_shape=jax.ShapeDtypeStruct((T, D_MODEL), BF16), grid=(T // ROW_T,),
        in_specs=[pl.BlockSpec((ROW_T, D_MODEL), lambda i: (i, 0)), pl.BlockSpec((1, D_MODEL), lambda i: (0, 0))],
        out_specs=pl.BlockSpec((ROW_T, D_MODEL), lambda i: (i, 0)), name=name, compiler_params=_cp("parallel"))(x, w)


def _rms_bwd(name, dh, x, w, dres):
    T = x.shape[0]

    def body(dh_ref, x_ref, w_ref, dres_ref, dx_ref, dw_ref):
        @pl.when(pl.program_id(0) == 0)
        def _():
            dw_ref[...] = jnp.zeros_like(dw_ref)

        xv = x_ref[...]
        r = lax.rsqrt(jnp.mean(xv * xv, axis=-1, keepdims=True) + EPS)
        xhat = xv * r
        dhv = dh_ref[...]
        dxhat = dhv * w_ref[...]
        m = jnp.mean(dxhat * xhat, axis=-1, keepdims=True)
        dx_ref[...] = dres_ref[...] + r * (dxhat - xhat * m)
        dw_ref[...] += jnp.sum(dhv * xhat, axis=0, keepdims=True)

    row = pl.BlockSpec((ROW_T, D_MODEL), lambda i: (i, 0))
    vec = pl.BlockSpec((1, D_MODEL), lambda i: (0, 0))
    return pl.pallas_call(
        body, out_shape=(jax.ShapeDtypeStruct((T, D_MODEL), F32), jax.ShapeDtypeStruct((1, D_MODEL), F32)),
        grid=(T // ROW_T,), in_specs=[row, row, vec, row], out_specs=(row, vec), name=name,
        compiler_params=_cp("arbitrary"))(dh, x, w, dres)


def _loss_grad(name, y, t):
    T = y.shape[0]

    def body(y_ref, t_ref, dy_ref, l_ref):
        @pl.when(pl.program_id(0) == 0)
        def _():
            l_ref[...] = jnp.zeros_like(l_ref)

        e = y_ref[...] - t_ref[...]
        dy_ref[...] = e * (1.0 / D_MODEL)
        l_ref[...] += jnp.sum(e * e, axis=0, keepdims=True)

    row = pl.BlockSpec((ROW_T, D_MODEL), lambda i: (i, 0))
    vec = pl.BlockSpec((1, D_MODEL), lambda i: (0, 0))
    return pl.pallas_call(
        body, out_shape=(jax.ShapeDtypeStruct((T, D_MODEL), F32), jax.ShapeDtypeStruct((1, D_MODEL), F32)),
        grid=(T // ROW_T,), in_specs=[row, row], out_specs=(row, vec), name=name,
        compiler_params=_cp("arbitrary"))(y, t)


def _ffn_gate_up(name, h, wg, wu):
    T = h.shape[0]

    def body(h_ref, wg_ref, wu_ref, g_ref, u_ref, a_ref):
        hv = h_ref[...]
        g = _dot(hv, wg_ref[...], NN)
        u = _dot(hv, wu_ref[...], NN)
        g_ref[...] = g.astype(BF16)
        u_ref[...] = u.astype(BF16)
        a_ref[...] = (g * _sigmoid(g) * u).astype(BF16)

    wspec = pl.BlockSpec((None, D_MODEL, FF_SH), lambda j, i: (j, 0, 0))
    ospec = pl.BlockSpec((None, ROW_T, FF_SH), lambda j, i: (j, i, 0))
    osh = jax.ShapeDtypeStruct((N_SHARD, T, FF_SH), BF16)
    return pl.pallas_call(
        body, out_shape=(osh, osh, osh), grid=(N_SHARD, T // ROW_T),
        in_specs=[pl.BlockSpec((ROW_T, D_MODEL), lambda j, i: (i, 0)), wspec, wspec],
        out_specs=(ospec, ospec, ospec), name=name, compiler_params=_cp("parallel", "parallel"))(h, wg, wu)


def _ffn_dact(name, dx, wd, g, u):
    T = dx.shape[0]

    def body(dx_ref, wd_ref, g_ref, u_ref, dg_ref, du_ref):
        da = 0.5 * _dot(dx_ref[...].astype(BF16), wd_ref[...], NT)
        gv = g_ref[...].astype(F32)
        uv = u_ref[...].astype(F32)
        sg = _sigmoid(gv)
        dg_ref[...] = (da * uv * (sg * (1.0 + gv * (1.0 - sg)))).astype(BF16)
        du_ref[...] = (da * gv * sg).astype(BF16)

    aspec = pl.BlockSpec((None, ROW_T, FF_SH), lambda j, i: (j, i, 0))
    osh = jax.ShapeDtypeStruct((N_SHARD, T, FF_SH), BF16)
    return pl.pallas_call(
        body, out_shape=(osh, osh), grid=(N_SHARD, T // ROW_T),
        in_specs=[pl.BlockSpec((ROW_T, D_MODEL), lambda j, i: (i, 0)),
                  pl.BlockSpec((None, FF_SH, D_MODEL), lambda j, i: (j, 0, 0)), aspec, aspec],
        out_specs=(aspec, aspec), name=name, compiler_params=_cp("parallel", "parallel"))(dx, wd, g, u)


def _ffn_fwd(tag, x, nw, wg, wu, wd):
    T = x.shape[0]
    h = _rms_fwd(tag + "_rms", x, nw)
    g, u, a = _ffn_gate_up(tag + "_gu", h, wg, wu)
    nt = T // ROW_T
    xo = _mm(tag + "_down",
             [(a, pl.BlockSpec((None, ROW_T, FF_SH), lambda i, n, k: (k, i, 0)),
               wd, pl.BlockSpec((None, FF_SH, D_MODEL), lambda i, n, k: (k, 0, 0)))],
             jax.ShapeDtypeStruct((T, D_MODEL), F32), pl.BlockSpec((ROW_T, D_MODEL), lambda i, n, k: (i, 0)),
             (nt, 1, N_SHARD), NN, (ROW_T, D_MODEL),
             res=(x, pl.BlockSpec((ROW_T, D_MODEL), lambda i, n, k: (i, 0))), scale=0.5)
    return xo, (x, h, g, u, a)


def _ffn_bwd(tag, dxo, saved, nw, wg, wu, wd, emit):
    x, h, g, u, a = saved
    T = x.shape[0]
    nt = T // ROW_T
    dg, du = _ffn_dact(tag + "_dact", dxo, wd, g, u)
    act = lambda f: pl.BlockSpec((None, ROW_T, FF_SH), f)
    gd = _mm(tag + "_dwd",
             [(a, act(lambda m, n, k: (m, k, 0)), dxo, pl.BlockSpec((ROW_T, D_MODEL), lambda m, n, k: (k, 0)))],
             jax.ShapeDtypeStruct((N_SHARD, FF_SH, D_MODEL), BF16),
             pl.BlockSpec((None, FF_SH, D_MODEL), lambda m, n, k: (m, 0, 0)),
             (N_SHARD, 1, nt), TN, (FF_SH, D_MODEL), scale=0.5)
    hspec = pl.BlockSpec((ROW_T, D_MODEL), lambda j, n, k: (k, 0))
    gsh = jax.ShapeDtypeStruct((N_SHARD, D_MODEL, FF_SH), BF16)
    gspec = pl.BlockSpec((None, D_MODEL, FF_SH), lambda j, n, k: (j, 0, 0))
    gg = _mm(tag + "_dwg", [(h, hspec, dg, act(lambda j, n, k: (j, k, 0)))], gsh, gspec,
             (N_SHARD, 1, nt), TN, (D_MODEL, FF_SH))
    gu = _mm(tag + "_dwu", [(h, hspec, du, act(lambda j, n, k: (j, k, 0)))], gsh, gspec,
             (N_SHARD, 1, nt), TN, (D_MODEL, FF_SH))
    dg = emit(gg, gu, gd, dg)
    wspec = pl.BlockSpec((None, D_MODEL, FF_SH), lambda i, n, k: (k, 0, 0))
    dh = _mm(tag + "_dh",
             [(dg, act(lambda i, n, k: (k, i, 0)), wg, wspec), (du, act(lambda i, n, k: (k, i, 0)), wu, wspec)],
             jax.ShapeDtypeStruct((T, D_MODEL), F32), pl.BlockSpec((ROW_T, D_MODEL), lambda i, n, k: (i, 0)),
             (nt, 1, N_SHARD), NT, (ROW_T, D_MODEL))
    return _rms_bwd(tag + "_rmsb", dh, x, nw, dxo)


def _seq_rows(ref, start, size, S):
    lo, hi = max(start, 0), min(start + size, S)
    parts = [ref[pl.ds(lo, hi - lo), :]]
    if lo > start:
        parts.insert(0, jnp.zeros((lo - start, ref.shape[1]), F32))
    if start + size > hi:
        parts.append(jnp.zeros((start + size - hi, ref.shape[1]), F32))
    return parts[0] if len(parts) == 1 else jnp.concatenate(parts, axis=0)


XBC_CB = COL_XBC // CONV_CT


def _conv_fwd(name, proj, w, b, B):
    T = proj.shape[0]
    S = T // B
    C = CONV_DIM

    def body(x_ref, w_ref, b_ref, o_ref):
        wv = w_ref[...]
        for c in range(S // CONV_R):
            r0 = c * CONV_R
            ch = _seq_rows(x_ref, r0 - PAD_R, CONV_R + PAD_R, S)
            pre = ch[PAD_R:] * wv[3:4] + b_ref[...]
            for s in range(1, CONV_K):
                pre = pre + pltpu.roll(ch, s, axis=0)[PAD_R:] * wv[3 - s:4 - s]
            o_ref[pl.ds(r0, CONV_R), :] = pre * _sigmoid(pre)

    return pl.pallas_call(
        body, out_shape=jax.ShapeDtypeStruct((T, C), F32), grid=(B, C // CONV_CT),
        in_specs=[pl.BlockSpec((S, CONV_CT), lambda bi, ci: (bi, XBC_CB + ci)),
                  pl.BlockSpec((CONV_K, CONV_CT), lambda bi, ci: (0, ci)),
                  pl.BlockSpec((1, CONV_CT), lambda bi, ci: (0, ci))],
        out_specs=pl.BlockSpec((S, CONV_CT), lambda bi, ci: (bi, ci)), name=name,
        compiler_params=_cp("parallel", "parallel"))(proj, w, b)


def _conv_bwd(name, proj, dxs, dB, dC, w, b, dproj, B):
    T = proj.shape[0]
    S = T // B
    C = CONV_DIM
    RW = CONV_R + PAD_R
    nx, nb = dxs.shape[1] // CONV_CT, dB.shape[1] // CONV_CT

    def body(x_ref, dx_in, db_in, dc_in, w_ref, b_ref, buf_ref, dx_ref, dw_ref, db_ref):
        @pl.when(pl.program_id(1) == 0)
        def _():
            dw_ref[...] = jnp.zeros_like(dw_ref)
            db_ref[...] = jnp.zeros_like(db_ref)

        ci = pl.program_id(0)
        wv = w_ref[...]
        dw = [jnp.zeros((1, CONV_CT), F32) for _ in range(CONV_K)]
        db = jnp.zeros((1, CONV_CT), F32)
        for c in range(S // CONV_R):
            r0 = c * CONV_R
            ch = _seq_rows(x_ref, r0 - PAD_R, RW + PAD_R, S)
            xs = [ch[PAD_R:]] + [pltpu.roll(ch, s, axis=0)[PAD_R:] for s in range(1, CONV_K)]
            pre = b_ref[...] + xs[0] * wv[3:4]
            for s in range(1, CONV_K):
                pre = pre + xs[s] * wv[3 - s:4 - s]
            sg = _sigmoid(pre)
            dout = jnp.where(ci < nx, _seq_rows(dx_in, r0, RW, S),
                             jnp.where(ci < nx + nb, _seq_rows(db_in, r0, RW, S), _seq_rows(dc_in, r0, RW, S)))
            dpre = dout * (sg * (1.0 + pre * (1.0 - sg)))
            dx = dpre[:CONV_R] * wv[3:4]
            for s in range(1, CONV_K):
                dx = dx + pltpu.roll(dpre, RW - s, axis=0)[:CONV_R] * wv[3 - s:4 - s]
            dx_ref[pl.ds(r0, CONV_R), :] = dx.astype(BF16)
            dcur = dpre[:CONV_R]
            db = db + jnp.sum(dcur, axis=0, keepdims=True)
            for s in range(CONV_K):
                dw[3 - s] = dw[3 - s] + jnp.sum(dcur * xs[s][:CONV_R], axis=0, keepdims=True)
        db_ref[...] += db
        for k in range(CONV_K):
            dw_ref[k:k + 1, :] += dw[k]

    seq = lambda f: pl.BlockSpec((S, CONV_CT), f)
    return pl.pallas_call(
        body,
        out_shape=(jax.ShapeDtypeStruct(dproj.shape, dproj.dtype), jax.ShapeDtypeStruct((CONV_K, C), F32),
                   jax.ShapeDtypeStruct((1, C), F32)),
        grid=(C // CONV_CT, B),
        in_specs=[seq(lambda ci, bi: (bi, XBC_CB + ci)),
                  seq(lambda ci, bi: (bi, jnp.minimum(ci, nx - 1))),
                  seq(lambda ci, bi: (bi, jnp.clip(ci - nx, 0, nb - 1))),
                  seq(lambda ci, bi: (bi, jnp.clip(ci - nx - nb, 0, nb - 1))),
                  pl.BlockSpec((CONV_K, CONV_CT), lambda ci, bi: (0, ci)),
                  pl.BlockSpec((1, CONV_CT), lambda ci, bi: (0, ci)), ANY],
        out_specs=(seq(lambda ci, bi: (bi, XBC_CB + ci)),
                   pl.BlockSpec((CONV_K, CONV_CT), lambda ci, bi: (0, ci)),
                   pl.BlockSpec((1, CONV_CT), lambda ci, bi: (0, ci))),
        input_output_aliases={6: 0},
        name=name, compiler_params=_cp("parallel", "arbitrary"))(proj, dxs, dB, dC, w, b, dproj)


def _ssd_common(dtc_ref, dtr_ref, pcol_ref, prow_ref, b_ref, c_ref):
    L = SSD_L
    bias_c, alog_c = pcol_ref[0:1, :], pcol_ref[1:2, :]
    a_c = -jnp.exp(alog_c)
    dt_c = _softplus(dtc_ref[...] + bias_c)
    row = lax.broadcasted_iota(jnp.int32, (L, L), 0)
    col = lax.broadcasted_iota(jnp.int32, (L, L), 1)
    causal = row >= col
    tri = causal.astype(F32)
    hp = lax.Precision.HIGHEST
    cum_c = lax.dot_general(tri, dt_c * a_c, NN, precision=hp, preferred_element_type=F32)
    a_r = -jnp.exp(prow_ref[:, 1:2])
    dt_r = _softplus(dtr_ref[...] + prow_ref[:, 0:1])
    cum_r = lax.dot_general(dt_r * a_r, tri, NT, precision=hp, preferred_element_type=F32)
    bb = b_ref[...].astype(BF16)
    cb = c_ref[...].astype(BF16)
    G = _dot(cb, bb, NT)
    return a_c, dt_c, causal, tri, cum_c, cum_r, bb, cb, G


def _ssd_fwd(name, xc, proj, dtc, dtr, pcol, prow, nw, B):
    T = xc.shape[0]
    S = T // B
    nb = S // SSD_L
    L = SSD_L

    def body(xs_ref, b_ref, c_ref, z_ref, dtc_ref, dtr_ref, pcol_ref, prow_ref, nw_ref, y_ref, yn_ref, hs_ref, H):
        @pl.when(pl.program_id(2) == 0)
        def _():
            H[...] = jnp.zeros_like(H)

        a_c, dt_c, causal, tri, cum_c, cum_r, bb, cb, G = _ssd_common(dtc_ref, dtr_ref, pcol_ref, prow_ref, b_ref, c_ref)
        dsk = pcol_ref[2:3, :]
        clast = cum_c[L - 1:L, :]
        bf = b_ref[...]
        for h in range(4):
            sl = slice(HEAD_DIM * h, HEAD_DIM * (h + 1))
            cc = cum_c[:, h:h + 1]
            lm = jnp.exp(jnp.where(causal, cc - cum_r[h:h + 1, :], NEG))
            M = (G * lm).astype(BF16)
            xh = xs_ref[:, sl]
            Xb = (xh * dt_c[:, h:h + 1]).astype(BF16)
            Hh = H[h]
            y = _dot(M, Xb, NN) + jnp.exp(cc) * _dot(cb, Hh.astype(BF16), NN)
            y_ref[:, sl] = y + dsk[:, h:h + 1] * xh
            hs_ref[h] = Hh
            cl = clast[:, h:h + 1]
            Bw = (bf * jnp.exp(cl - cc)).astype(BF16)
            H[h] = jnp.exp(cl) * Hh + _dot(Bw, Xb, TN)
        zv = z_ref[...]
        y2 = y_ref[...] * (zv * _sigmoid(zv))
        r = lax.rsqrt(jnp.mean(y2 * y2, axis=-1, keepdims=True) + EPS)
        yn_ref[...] = (y2 * r * nw_ref[...]).astype(BF16)

    rowi = lambda b, g, i: b * nb + i
    grp = pl.BlockSpec((L, GROUP_W), lambda b, g, i: (rowi(b, g, i), g))
    return pl.pallas_call(
        body,
        out_shape=(jax.ShapeDtypeStruct((T, 1024), F32), jax.ShapeDtypeStruct((T, 1024), BF16),
                   jax.ShapeDtypeStruct((B, SSD_GROUPS, nb, 4, SSD_STATE, HEAD_DIM), F32)),
        grid=(B, SSD_GROUPS, nb),
        in_specs=[grp,
                  pl.BlockSpec((L, SSD_STATE), lambda b, g, i: (rowi(b, g, i), 8 + g)),
                  pl.BlockSpec((L, SSD_STATE), lambda b, g, i: (rowi(b, g, i), 12 + g)),
                  grp,
                  pl.BlockSpec((None, L, 4), lambda b, g, i: (g, rowi(b, g, i), 0)),
                  pl.BlockSpec((None, 4, L), lambda b, g, i: (g, 0, rowi(b, g, i))),
                  pl.BlockSpec((None, 3, 4), lambda b, g, i: (g, 0, 0)),
                  pl.BlockSpec((None, 4, 3), lambda b, g, i: (g, 0, 0)),
                  pl.BlockSpec((1, GROUP_W), lambda b, g, i: (0, g))],
        out_specs=(grp, grp,
                   pl.BlockSpec((None, None, None, 4, SSD_STATE, HEAD_DIM), lambda b, g, i: (b, g, i, 0, 0, 0))),
        scratch_shapes=[pltpu.VMEM((4, SSD_STATE, HEAD_DIM), F32)], name=name,
        compiler_params=_cp("parallel", "parallel", "arbitrary"))(xc, xc, xc, proj, dtc, dtr, pcol, prow, nw)


def _ssd_bwd(name, dyn, Y, xc, proj, dtc, dtr, pcol, prow, nw, hs, dproj, B):
    T = xc.shape[0]
    S = T // B
    nb = S // SSD_L
    L = SSD_L

    def body(dyn_ref, y_ref, xs_ref, b_ref, c_ref, z_ref, dtc_ref, dtr_ref, pcol_ref, prow_ref, nw_ref, hs_ref, buf_ref,
             dxs_ref, db_ref, dc_ref, dz_ref, ddt_ref, dpar_ref, dnw_ref, dH):
        @pl.when(pl.program_id(2) == 0)
        def _():
            dH[...] = jnp.zeros_like(dH)
            dpar_ref[...] = jnp.zeros_like(dpar_ref)
            dnw_ref[...] = jnp.zeros_like(dnw_ref)

        a_c, dt_c, causal, tri, cum_c, cum_r, bb, cb, G = _ssd_common(dtc_ref, dtr_ref, pcol_ref, prow_ref, b_ref, c_ref)
        dsk = pcol_ref[2:3, :]
        clast = cum_c[L - 1:L, :]
        bf = b_ref[...]
        cf = c_ref[...]
        Yv = y_ref[...]
        zv = z_ref[...]
        sz = _sigmoid(zv)
        silu = zv * sz
        y2 = Yv * silu
        r = lax.rsqrt(jnp.mean(y2 * y2, axis=-1, keepdims=True) + EPS)
        yhat = y2 * r
        dyv = dyn_ref[...]
        dnw_ref[...] += jnp.sum(dyv * yhat, axis=0, keepdims=True)
        dyhat = dyv * nw_ref[...]
        dy2 = r * (dyhat - yhat * jnp.mean(dyhat * yhat, axis=-1, keepdims=True))
        dY = dy2 * silu
        dz_ref[...] = (dy2 * Yv * (sz * (1.0 + zv * (1.0 - sz)))).astype(BF16)

        lane4 = lax.broadcasted_iota(jnp.int32, (1, 4), 1)
        dG = jnp.zeros((L, L), F32)
        dBs = jnp.zeros((L, SSD_STATE), F32)
        dCs = jnp.zeros((L, SSD_STATE), F32)
        dA = jnp.zeros((L, 4), F32)
        ddtx = jnp.zeros((L, 4), F32)
        ddsk = jnp.zeros((1, 4), F32)
        dcl = jnp.zeros((1, 4), F32)
        for h in range(4):
            sl = slice(HEAD_DIM * h, HEAD_DIM * (h + 1))
            onehot = (lane4 == h).astype(F32)
            cc = cum_c[:, h:h + 1]
            cl = clast[:, h:h + 1]
            lm = jnp.exp(jnp.where(causal, cc - cum_r[h:h + 1, :], NEG))
            M = (G * lm).astype(BF16)
            xh = xs_ref[:, sl]
            dth = dt_c[:, h:h + 1]
            X = xh * dth
            Xb = X.astype(BF16)
            dYh = dY[:, sl]
            dYb = dYh.astype(BF16)
            Hb = hs_ref[h].astype(BF16)
            dHh = dH[h]
            dHb = dHh.astype(BF16)
            alpha = jnp.exp(cc)
            beta = jnp.exp(cl - cc)
            dXoff = beta * _dot(bb, dHb, NN)
            dX = _dot(M, dYb, TN) + dXoff
            dG = dG + _dot(dYb, Xb, NT) * lm
            dCs = dCs + _dot((alpha * dYh).astype(BF16), Hb, NT)
            dBs = dBs + _dot((beta * X).astype(BF16), dHb, NT)
            ypre = Yv[:, sl] - dsk[:, h:h + 1] * xh
            dA_h = (jnp.sum(dYb.astype(F32) * ypre, axis=-1, keepdims=True)
                    - jnp.sum(Xb.astype(F32) * dX, axis=-1, keepdims=True))
            dA = dA + dA_h * onehot
            dcl_h = (jnp.sum(jnp.sum(dHh * (jnp.exp(cl) * hs_ref[h]), axis=-1, keepdims=True), axis=0, keepdims=True)
                     + jnp.sum(jnp.sum(Xb.astype(F32) * dXoff, axis=-1, keepdims=True), axis=0, keepdims=True))
            dcl = dcl + dcl_h * onehot
            ddtx = ddtx + jnp.sum(dX * xh, axis=-1, keepdims=True) * onehot
            ddsk = ddsk + jnp.sum(jnp.sum(dYh * xh, axis=-1, keepdims=True), axis=0, keepdims=True) * onehot
            dxs_ref[:, sl] = dsk[:, h:h + 1] * dYh + dX * dth
            dH[h] = jnp.exp(cl) * dHh + _dot((alpha * cf).astype(BF16), dYb, TN)
        dGb = dG.astype(BF16)
        dc_ref[...] = _dot(dGb, bb, NN) + dCs
        db_ref[...] = _dot(dGb, cb, TN) + dBs
        hp = lax.Precision.HIGHEST
        last = lax.broadcasted_iota(jnp.int32, (L, 1), 0) == L - 1
        dA = dA + jnp.where(last, dcl, 0.0)
        dadt = lax.dot_general(tri, dA, TN, precision=hp, preferred_element_type=F32)
        ddt = dadt * a_c + ddtx
        d_a = jnp.sum(dadt * dt_c, axis=0, keepdims=True)
        ddraw = ddt * _sigmoid(dtc_ref[...] + pcol_ref[0:1, :])
        ddt_ref[...] = ddraw
        dpar_ref[0:1, :] += jnp.sum(ddraw, axis=0, keepdims=True)
        dpar_ref[1:2, :] += d_a * a_c
        dpar_ref[2:3, :] += ddsk

    rowi = lambda b, g, i: b * nb + (nb - 1 - i)
    grp = pl.BlockSpec((L, GROUP_W), lambda b, g, i: (rowi(b, g, i), g))
    st = pl.BlockSpec((L, SSD_STATE), lambda b, g, i: (rowi(b, g, i), g))
    f = jax.ShapeDtypeStruct
    return pl.pallas_call(
        body,
        out_shape=(f((T, 1024), F32), f((T, 512), F32), f((T, 512), F32), f(dproj.shape, dproj.dtype),
                   f((SSD_GROUPS, T, 4), F32), f((B, SSD_GROUPS, 3, 4), F32), f((B, 1, 1024), F32)),
        grid=(B, SSD_GROUPS, nb),
        in_specs=[grp, grp, grp,
                  pl.BlockSpec((L, SSD_STATE), lambda b, g, i: (rowi(b, g, i), 8 + g)),
                  pl.BlockSpec((L, SSD_STATE), lambda b, g, i: (rowi(b, g, i), 12 + g)),
                  grp,
                  pl.BlockSpec((None, L, 4), lambda b, g, i: (g, rowi(b, g, i), 0)),
                  pl.BlockSpec((None, 4, L), lambda b, g, i: (g, 0, rowi(b, g, i))),
                  pl.BlockSpec((None, 3, 4), lambda b, g, i: (g, 0, 0)),
                  pl.BlockSpec((None, 4, 3), lambda b, g, i: (g, 0, 0)),
                  pl.BlockSpec((1, GROUP_W), lambda b, g, i: (0, g)),
                  pl.BlockSpec((None, None, None, 4, SSD_STATE, HEAD_DIM), lambda b, g, i: (b, g, nb - 1 - i, 0, 0, 0)),
                  ANY],
        out_specs=(grp, st, st, grp,
                   pl.BlockSpec((None, L, 4), lambda b, g, i: (g, rowi(b, g, i), 0)),
                   pl.BlockSpec((None, None, 3, 4), lambda b, g, i: (b, g, 0, 0)),
                   pl.BlockSpec((None, 1, GROUP_W), lambda b, g, i: (b, 0, g))),
        input_output_aliases={12: 3},
        scratch_shapes=[pltpu.VMEM((4, SSD_STATE, HEAD_DIM), F32)], name=name,
        compiler_params=_cp("parallel", "parallel", "arbitrary"))(
            dyn, Y, xc, xc, xc, proj, dtc, dtr, pcol, prow, nw, hs, dproj)


def _headnorm_fwd(name, proj, col_block, w):
    T = proj.shape[0]

    def body(x_ref, w_ref, o_ref):
        for h in range(ATT_HEADS):
            sl = slice(HEAD_DIM * h, HEAD_DIM * (h + 1))
            xh = x_ref[:, sl]
            r = lax.rsqrt(jnp.mean(xh * xh, axis=-1, keepdims=True) + EPS)
            o_ref[:, sl] = (xh * r * w_ref[...]).astype(BF16)

    return pl.pallas_call(
        body, out_shape=jax.ShapeDtypeStruct((T, 1024), BF16), grid=(T // ROW_T,),
        in_specs=[pl.BlockSpec((ROW_T, 1024), lambda i: (i, col_block)), pl.BlockSpec((1, HEAD_DIM), lambda i: (0, 0))],
        out_specs=pl.BlockSpec((ROW_T, 1024), lambda i: (i, 0)), name=name, compiler_params=_cp("parallel"))(proj, w)


def _headnorm_bwd(name, dn, proj, col_block, w, dproj):
    T = proj.shape[0]

    def body(dn_ref, x_ref, w_ref, buf_ref, dx_ref, dw_ref):
        @pl.when(pl.program_id(0) == 0)
        def _():
            dw_ref[...] = jnp.zeros_like(dw_ref)

        dw = jnp.zeros((1, HEAD_DIM), F32)
        for h in range(ATT_HEADS):
            sl = slice(HEAD_DIM * h, HEAD_DIM * (h + 1))
            xh = x_ref[:, sl]
            r = lax.rsqrt(jnp.mean(xh * xh, axis=-1, keepdims=True) + EPS)
            xhat = xh * r
            dnh = dn_ref[:, sl]
            dxhat = dnh * w_ref[...]
            dx_ref[:, sl] = (r * (dxhat - xhat * jnp.mean(dxhat * xhat, axis=-1, keepdims=True))).astype(BF16)
            dw = dw + jnp.sum(dnh * xhat, axis=0, keepdims=True)
        dw_ref[...] += dw

    here = pl.BlockSpec((ROW_T, 1024), lambda i: (i, col_block))
    return pl.pallas_call(
        body, out_shape=(jax.ShapeDtypeStruct(dproj.shape, dproj.dtype), jax.ShapeDtypeStruct((1, HEAD_DIM), F32)),
        grid=(T // ROW_T,),
        in_specs=[pl.BlockSpec((ROW_T, 1024), lambda i: (i, 0)), here, pl.BlockSpec((1, HEAD_DIM), lambda i: (0, 0)), ANY],
        out_specs=(here, pl.BlockSpec((1, HEAD_DIM), lambda i: (0, 0))), input_output_aliases={3: 0},
        name=name, compiler_params=_cp("arbitrary"))(dn, proj, w, dproj)


def _att_bias(nq):
    j = np.arange(ATT_B)[:, None]
    i = np.arange(ATT_B)[None, :]
    out = np.empty((nq, ATT_B, ATT_B), np.float32)
    for dblk in range(nq):
        dl = ATT_B * dblk + i - j
        cnt = ((dl >= 0) & (dl <= 128)).astype(np.float32)
        cnt += ((dl >= 0) & (dl % 4 == 0) & (dl <= 512))
        cnt += ((dl >= 0) & (dl % 16 == 0) & (dl <= 2048))
        out[dblk] = np.where(cnt > 0, np.log(np.maximum(cnt, 1.0)), NEG)
    return jnp.asarray(out)


def _row_pair(nq):
    def f(r, c):
        first = c <= r
        return jnp.where(first, r, nq - 1 - r), jnp.where(first, c, c - (r + 1))
    return f


def _col_pair(nq):
    def f(r, c):
        first = c < nq - r
        kj = jnp.where(first, r, nq - 1 - r)
        return jnp.where(first, r + c, nq - 1 - r + (c - (nq - r))), kj
    return f


ATT_SCALE = 1.0 / math.sqrt(HEAD_DIM)
ATT_HS = 4
ATT_W = ATT_HS * HEAD_DIM


def _att_maps(nq, qk):
    return dict(
        q_tok=lambda b, g, r, c: (b * nq + qk(r, c)[0], g),
        k_tok=lambda b, g, r, c: (b * nq + qk(r, c)[1], g),
        q_feat=lambda b, g, r, c: (g, b * nq + qk(r, c)[0]),
        k_feat=lambda b, g, r, c: (g, b * nq + qk(r, c)[1]),
        bias=lambda b, g, r, c: (qk(r, c)[0] - qk(r, c)[1], 0, 0),
        lse=lambda b, g, r, c: (g, 0, b * nq + qk(r, c)[0]),
        do_tok=lambda b, g, r, c: (b * nq + qk(r, c)[0], ATT_HS + g))


def _att_fwd(name, kn, qT, vT, bias, B):
    T = kn.shape[0]
    nq = (T // B) // ATT_B
    qk = _row_pair(nq)
    mp = _att_maps(nq, qk)

    def body(k_ref, qT_ref, vT_ref, bias_ref, oT_ref, lse_ref, m_s, l_s, acc_s, s_s):
        qi, kj = qk(pl.program_id(2), pl.program_id(3))

        @pl.when(kj == 0)
        def _():
            m_s[...] = jnp.full_like(m_s, NEG)
            l_s[...] = jnp.zeros_like(l_s)
            acc_s[...] = jnp.zeros_like(acc_s)

        bv = bias_ref[...]
        for h in range(ATT_HS):
            rs = slice(HEAD_DIM * h, HEAD_DIM * (h + 1))
            s_s[h] = _dot(k_ref[:, rs], qT_ref[rs, :], NN)
        for h in range(ATT_HS):
            rs = slice(HEAD_DIM * h, HEAD_DIM * (h + 1))
            s = s_s[h] + bv
            m_prev = m_s[h:h + 1, :]
            m_new = jnp.maximum(m_prev, jnp.max(s, axis=0, keepdims=True))
            alpha = jnp.exp(m_prev - m_new)
            p = jnp.exp(s - m_new)
            l_s[h:h + 1, :] = alpha * l_s[h:h + 1, :] + jnp.sum(p, axis=0, keepdims=True)
            acc_s[rs, :] = alpha * acc_s[rs, :] + _dot(vT_ref[rs, :], p.astype(BF16), NN)
            m_s[h:h + 1, :] = m_new

        @pl.when(kj == qi)
        def _():
            for h in range(ATT_HS):
                rs = slice(HEAD_DIM * h, HEAD_DIM * (h + 1))
                oT_ref[rs, :] = (acc_s[rs, :] / l_s[h:h + 1, :]).astype(BF16)
            lse_ref[...] = m_s[...] + jnp.log(l_s[...])

    tok = (ATT_B, ATT_W)
    feat = (ATT_W, ATT_B)
    return pl.pallas_call(
        body,
        out_shape=(jax.ShapeDtypeStruct((1024, T), BF16), jax.ShapeDtypeStruct((ATT_HEADS // ATT_HS, ATT_HS, T), F32)),
        grid=(B, ATT_HEADS // ATT_HS, nq // 2, nq + 1),
        in_specs=[pl.BlockSpec(tok, mp["k_tok"]), pl.BlockSpec(feat, mp["q_feat"]), pl.BlockSpec(feat, mp["k_feat"]),
                  pl.BlockSpec((None, ATT_B, ATT_B), mp["bias"])],
        out_specs=(pl.BlockSpec(feat, mp["q_feat"]), pl.BlockSpec((None, ATT_HS, ATT_B), mp["lse"])),
        scratch_shapes=[pltpu.VMEM((ATT_HS, ATT_B), F32), pltpu.VMEM((ATT_HS, ATT_B), F32),
                        pltpu.VMEM((ATT_W, ATT_B), F32), pltpu.VMEM((ATT_HS, ATT_B, ATT_B), F32)],
        name=name, compiler_params=_cp("parallel", "parallel", "arbitrary", "arbitrary"))(kn, qT, vT, bias)


def _att_scores(k_ref, qT_ref, v_ref, doT_ref, s_s, dp_s):
    for h in range(ATT_HS):
        rs = slice(HEAD_DIM * h, HEAD_DIM * (h + 1))
        s_s[h] = _dot(k_ref[:, rs], qT_ref[rs, :], NN)
        dp_s[h] = _dot(v_ref[:, rs], doT_ref[rs, :].astype(BF16), NN)


def _att_p_ds(s_s, dp_s, doT_ref, oT_ref, lse_ref, bv, h):
    rs = slice(HEAD_DIM * h, HEAD_DIM * (h + 1))
    delta = jnp.sum(doT_ref[rs, :] * oT_ref[rs, :].astype(F32), axis=0, keepdims=True)
    p = jnp.exp(s_s[h] + bv - lse_ref[h:h + 1, :])
    return p, p * (dp_s[h] - delta)


def _att_bwd_dq(name, kn, qT, vb, knT, bias, doT, oT, lse, B):
    T = kn.shape[0]
    nq = (T // B) // ATT_B
    qk = _row_pair(nq)
    mp = _att_maps(nq, qk)

    def body(k_ref, qT_ref, v_ref, kT_ref, bias_ref, doT_ref, oT_ref, lse_ref, dqT_ref, acc_s, s_s, dp_s):
        qi, kj = qk(pl.program_id(2), pl.program_id(3))

        @pl.when(kj == 0)
        def _():
            acc_s[...] = jnp.zeros_like(acc_s)

        bv = bias_ref[...]
        _att_scores(k_ref, qT_ref, v_ref, doT_ref, s_s, dp_s)
        for h in range(ATT_HS):
            rs = slice(HEAD_DIM * h, HEAD_DIM * (h + 1))
            p, ds = _att_p_ds(s_s, dp_s, doT_ref, oT_ref, lse_ref, bv, h)
            acc_s[rs, :] += _dot(kT_ref[rs, :], ds.astype(BF16), NN)

        @pl.when(kj == qi)
        def _():
            dqT_ref[...] = acc_s[...] * ATT_SCALE

    tok = (ATT_B, ATT_W)
    feat = (ATT_W, ATT_B)
    return pl.pallas_call(
        body, out_shape=jax.ShapeDtypeStruct((1024, T), F32), grid=(B, ATT_HEADS // ATT_HS, nq // 2, nq + 1),
        in_specs=[pl.BlockSpec(tok, mp["k_tok"]), pl.BlockSpec(feat, mp["q_feat"]), pl.BlockSpec(tok, mp["k_tok"]),
                  pl.BlockSpec(feat, mp["k_feat"]), pl.BlockSpec((None, ATT_B, ATT_B), mp["bias"]),
                  pl.BlockSpec(feat, mp["q_feat"]), pl.BlockSpec(feat, mp["q_feat"]),
                  pl.BlockSpec((None, ATT_HS, ATT_B), mp["lse"])],
        out_specs=pl.BlockSpec(feat, mp["q_feat"]),
        scratch_shapes=[pltpu.VMEM((ATT_W, ATT_B), F32), pltpu.VMEM((ATT_HS, ATT_B, ATT_B), F32),
                        pltpu.VMEM((ATT_HS, ATT_B, ATT_B), F32)],
        name=name, compiler_params=_cp("parallel", "parallel", "arbitrary", "arbitrary"))(
            kn, qT, vb, knT, bias, doT, oT, lse)


def _att_bwd_dkv(name, kn, qT, vb, qn, bias, doT, oT, lse, dyn, dproj, B):
    T = kn.shape[0]
    nq = (T // B) // ATT_B
    qk = _col_pair(nq)
    mp = _att_maps(nq, qk)

    def body(k_ref, qT_ref, v_ref, q_ref, bias_ref, doT_ref, oT_ref, lse_ref, do_ref, buf_ref, dk_ref, dv_ref, dk_s, dv_s,
             s_s, dp_s):
        qi, kj = qk(pl.program_id(2), pl.program_id(3))

        @pl.when(qi == kj)
        def _():
            dk_s[...] = jnp.zeros_like(dk_s)
            dv_s[...] = jnp.zeros_like(dv_s)

        bv = bias_ref[...]
        _att_scores(k_ref, qT_ref, v_ref, doT_ref, s_s, dp_s)
        for h in range(ATT_HS):
            rs = slice(HEAD_DIM * h, HEAD_DIM * (h + 1))
            p, ds = _att_p_ds(s_s, dp_s, doT_ref, oT_ref, lse_ref, bv, h)
            dv_s[h] += _dot(p.astype(BF16), do_ref[:, rs].astype(BF16), NN)
            dk_s[h] += _dot(ds.astype(BF16), q_ref[:, rs], NN)

        @pl.when(qi == nq - 1)
        def _():
            for h in range(ATT_HS):
                rs = slice(HEAD_DIM * h, HEAD_DIM * (h + 1))
                dk_ref[:, rs] = dk_s[h] * ATT_SCALE
                dv_ref[:, rs] = dv_s[h].astype(BF16)

    tok = (ATT_B, ATT_W)
    feat = (ATT_W, ATT_B)
    v_cb = COL_V // ATT_W
    return pl.pallas_call(
        body, out_shape=(jax.ShapeDtypeStruct((T, 1024), F32), jax.ShapeDtypeStruct(dproj.shape, dproj.dtype)),
        grid=(B, ATT_HEADS // ATT_HS, nq // 2, nq + 1),
        in_specs=[pl.BlockSpec(tok, mp["k_tok"]), pl.BlockSpec(feat, mp["q_feat"]), pl.BlockSpec(tok, mp["k_tok"]),
                  pl.BlockSpec(tok, mp["q_tok"]), pl.BlockSpec((None, ATT_B, ATT_B), mp["bias"]),
                  pl.BlockSpec(feat, mp["q_feat"]), pl.BlockSpec(feat, mp["q_feat"]),
                  pl.BlockSpec((None, ATT_HS, ATT_B), mp["lse"]), pl.BlockSpec(tok, mp["do_tok"]), ANY],
        out_specs=(pl.BlockSpec(tok, mp["k_tok"]),
                   pl.BlockSpec(tok, lambda b, g, r, c: (b * nq + qk(r, c)[1], v_cb + g))),
        input_output_aliases={9: 1},
        scratch_shapes=[pltpu.VMEM((ATT_HS, ATT_B, HEAD_DIM), F32), pltpu.VMEM((ATT_HS, ATT_B, HEAD_DIM), F32),
                        pltpu.VMEM((ATT_HS, ATT_B, ATT_B), F32), pltpu.VMEM((ATT_HS, ATT_B, ATT_B), F32)],
        name=name, compiler_params=_cp("parallel", "parallel", "arbitrary", "arbitrary"))(
            kn, qT, vb, qn, bias, doT, oT, lse, dyn, dproj)


def _group_cols(v):
    return v.reshape(SSD_GROUPS, 4)


def _ssd_params(p):
    rows = jnp.stack([_group_cols(p["dt_bias"]), _group_cols(p["a_log"]), _group_cols(p["d_skip"])], axis=1)
    return rows, jnp.swapaxes(rows, 1, 2)


def _mixer_fwd(tag, x1, p, weights, bias, B):
    T = x1.shape[0]
    S = T // B
    nt = T // ROW_T
    h2 = _rms_fwd(tag + "_mixrms", x1, p["mix_norm"][None])
    wi = weights("win", h2)
    win, cw = wi["win"], wi["cw"]
    proj = _mm(tag + "_proj",
               [(h2, pl.BlockSpec((ROW_T, D_MODEL), lambda j, i, k: (i, 0)),
                 win, pl.BlockSpec((D_MODEL, PROJ_TN), lambda j, i, k: (0, j)))],
               jax.ShapeDtypeStruct((T, IN_PAD), F32), pl.BlockSpec((ROW_T, PROJ_TN), lambda j, i, k: (i, j)),
               (IN_PAD // PROJ_TN, nt, 1), NN, (ROW_T, PROJ_TN))
    xc = _conv_fwd(tag + "_conv", proj, cw, p["conv_b"][None], B)
    dtraw = proj[:, COL_DT:COL_DT + SSD_HEADS].reshape(T, SSD_GROUPS, 4)
    dtc = jnp.transpose(dtraw, (1, 0, 2))
    dtr = jnp.transpose(dtraw, (1, 2, 0))
    pcol, prow = _ssd_params(p)
    Y, y_ssd, hs = _ssd_fwd(tag + "_ssd", xc, proj, dtc, dtr, pcol, prow, p["ssd_norm"][None], B)
    qn = _headnorm_fwd(tag + "_qn", proj, COL_Q // 1024, p["q_norm"][None])
    kn = _headnorm_fwd(tag + "_kn", proj, COL_K // 1024, p["k_norm"][None])
    qT = (qn * ATT_SCALE).T
    vb = proj[:, COL_V:COL_V + 1024].astype(BF16)
    oT, lse = _att_fwd(tag + "_att", kn, qT, vb.T, bias, B)
    ymix = jnp.concatenate([y_ssd, oT.T], axis=1)
    rest = weights("rest", ymix)
    x2 = _mm(tag + "_out",
             [(ymix, pl.BlockSpec((ROW_T, MIX_SH), lambda i, n, k: (i, k)),
               rest["wout"], pl.BlockSpec((None, MIX_SH, D_MODEL), lambda i, n, k: (k, 0, 0)))],
             jax.ShapeDtypeStruct((T, D_MODEL), F32), pl.BlockSpec((ROW_T, D_MODEL), lambda i, n, k: (i, 0)),
             (nt, 1, N_SHARD), NN, (ROW_T, D_MODEL),
             res=(x1, pl.BlockSpec((ROW_T, D_MODEL), lambda i, n, k: (i, 0))))
    saved = dict(x1=x1, h2=h2, proj=proj, xc=xc, dtc=dtc, dtr=dtr, Y=Y, hs=hs,
                 qn=qn, kn=kn, qT=qT, vb=vb, oT=oT, lse=lse, ymix=ymix, win=win, cw=cw, wout=rest["wout"])
    return x2, saved


def _mixer_bwd(tag, dx2, sv, p, bias, B):
    T = dx2.shape[0]
    S = T // B
    nt = T // ROW_T
    sg = {}
    dymix = _mm(tag + "_dymix",
                [(dx2, pl.BlockSpec((ROW_T, D_MODEL), lambda n, i, k: (i, 0)),
                  sv["wout"], pl.BlockSpec((None, MIX_SH, D_MODEL), lambda n, i, k: (n, 0, 0)))],
                jax.ShapeDtypeStruct((T, MIX_W), F32), pl.BlockSpec((ROW_T, MIX_SH), lambda n, i, k: (i, n)),
                (N_SHARD, nt, 1), NT, (ROW_T, MIX_SH))
    gwout = _mm(tag + "_dwout",
                [(sv["ymix"], pl.BlockSpec((ROW_T, MIX_SH), lambda m, n, k: (k, m)),
                  dx2, pl.BlockSpec((ROW_T, D_MODEL), lambda m, n, k: (k, 0)))],
                jax.ShapeDtypeStruct((N_SHARD, MIX_SH, D_MODEL), BF16),
                pl.BlockSpec((None, MIX_SH, D_MODEL), lambda m, n, k: (m, 0, 0)),
                (N_SHARD, 1, nt), TN, (MIX_SH, D_MODEL))
    proj = sv["proj"]
    doT = dymix[:, 1024:].T
    dqn = _att_bwd_dq(tag + "_attdq", sv["kn"], sv["qT"], sv["vb"], sv["kn"].T, bias, doT, sv["oT"], sv["lse"], B).T
    dproj = lax.empty((T, IN_PAD), BF16)
    dkn, dproj = _att_bwd_dkv(tag + "_attdkv", sv["kn"], sv["qT"], sv["vb"], sv["qn"], bias, doT, sv["oT"], sv["lse"],
                              dymix, dproj, B)
    dproj, sg["q_norm"] = _headnorm_bwd(tag + "_qnb", dqn, proj, COL_Q // 1024, p["q_norm"][None], dproj)
    dproj, sg["k_norm"] = _headnorm_bwd(tag + "_knb", dkn, proj, COL_K // 1024, p["k_norm"][None], dproj)
    pcol, prow = _ssd_params(p)
    dxs, dB, dC, dproj, ddt, dpar, dnw = _ssd_bwd(tag + "_ssdb", dymix, sv["Y"], sv["xc"], proj, sv["dtc"], sv["dtr"],
                                                  pcol, prow, p["ssd_norm"][None], sv["hs"], dproj, B)
    dpar = jnp.sum(dpar, axis=0)
    sg["dt_bias"] = dpar[:, 0, :].reshape(SSD_HEADS)
    sg["a_log"] = dpar[:, 1, :].reshape(SSD_HEADS)
    sg["d_skip"] = dpar[:, 2, :].reshape(SSD_HEADS)
    sg["ssd_norm"] = jnp.sum(dnw, axis=0)
    dproj, sg["conv_w"], sg["conv_b"] = _conv_bwd(tag + "_convb", proj, dxs, dB, dC, sv["cw"], p["conv_b"][None],
                                                  dproj, B)
    ddt16 = jnp.transpose(ddt, (1, 0, 2)).reshape(T, SSD_HEADS)
    dproj = lax.dynamic_update_slice(dproj, jnp.pad(ddt16, ((0, 0), (0, IN_PAD - COL_DT - SSD_HEADS))).astype(BF16),
                                     (0, COL_DT))
    win = sv["win"]
    gwin = _mm(tag + "_dwin",
               [(sv["h2"], pl.BlockSpec((ROW_T, D_MODEL), lambda n, m, k: (k, 0)),
                 dproj, pl.BlockSpec((ROW_T, PROJ_TN), lambda n, m, k: (k, n)))],
               jax.ShapeDtypeStruct((D_MODEL, IN_PAD), BF16), pl.BlockSpec((D_MODEL, PROJ_TN), lambda n, m, k: (0, n)),
               (IN_PAD // PROJ_TN, 1, nt), TN, (D_MODEL, PROJ_TN))
    dh2 = _mm(tag + "_dh2",
              [(dproj, pl.BlockSpec((ROW_T, PROJ_TN), lambda i, n, k: (i, k)),
                win, pl.BlockSpec((D_MODEL, PROJ_TN), lambda i, n, k: (0, k)))],
              jax.ShapeDtypeStruct((T, D_MODEL), F32), pl.BlockSpec((ROW_T, D_MODEL), lambda i, n, k: (i, 0)),
              (nt, 1, IN_PAD // PROJ_TN), NT, (ROW_T, D_MODEL))
    dx1, sg["mix_norm"] = _rms_bwd(tag + "_mixrmsb", dh2, sv["x1"], p["mix_norm"][None], dx2)
    return dx1, sg, gwout, gwin


def _win_pack(w):
    return jnp.concatenate([w[:, :3072], w[:, 3088:], w[:, 3072:3088],
                            jnp.zeros((w.shape[0], IN_PAD - IN_PROJ), w.dtype)], axis=1)


def _win_unpack(g):
    return jnp.concatenate([g[:, :3072], g[:, COL_DT:COL_DT + SSD_HEADS], g[:, 3072:COL_DT]], axis=1)


def _local_step(x, target, small, weights, scatter, B):
    T = x.shape[0]
    bias = _att_bias((T // B) // ATT_B)
    saved = []
    h = x
    for l in range(DEPTH):
        tag = "l%d" % l
        p = {k: v[l] for k, v in small.items()}
        w1 = weights(l, "ffn1", h)
        x1, ffn1 = _ffn_fwd(tag + "f1", h, p["ffn1_norm"][None], w1["g1"], w1["u1"], w1["d1"])
        x2, sv = _mixer_fwd(tag, x1, p, functools.partial(weights, l), bias, B)
        w2 = weights(l, "rest", x2)
        h, ffn2 = _ffn_fwd(tag + "f2", x2, p["ffn2_norm"][None], w2["g2"], w2["u2"], w2["d2"])
        saved.append((ffn1, sv, ffn2, w1, w2))
    d, lsum = _loss_grad("loss", h, target)
    sgrads = [None] * DEPTH
    for l in reversed(range(DEPTH)):
        tag = "l%db" % l
        p = {k: v[l] for k, v in small.items()}
        ffn1, sv, ffn2, w1, w2 = saved[l]
        sg = {}
        d, sg["ffn2_norm"] = _ffn_bwd(tag + "f2", d, ffn2, p["ffn2_norm"][None], w2["g2"], w2["u2"], w2["d2"],
                                      lambda gg, gu, gd, c, l=l: scatter(l, "ffn2", dict(g2=gg, u2=gu, d2=gd), c))
        d, sgm, gwout, gwin = _mixer_bwd(tag, d, sv, p, bias, B)
        sg.update(sgm)
        d = scatter(l, "mixer", dict(wout=gwout, win=gwin), d)
        d, sg["ffn1_norm"] = _ffn_bwd(tag + "f1", d, ffn1, p["ffn1_norm"][None], w1["g1"], w1["u1"], w1["d1"],
                                      lambda gg, gu, gd, c, l=l: scatter(l, "ffn1", dict(g1=gg, u1=gu, d1=gd), c))
        sgrads[l] = sg
    return lsum, d, sgrads


MESH = pl.DeviceIdType.MESH
ANY = pl.BlockSpec(memory_space=pl.ANY)


def _place():
    return lax.axis_index("x"), lax.axis_index("y"), lax.axis_index("c")


def _other_chips(x, y):
    return [(1 - x, y), (x, 1 - y), (1 - x, 1 - y)]


HBM = pl.BlockSpec(memory_space=pltpu.HBM)
SEM = pl.BlockSpec(memory_space=pltpu.SEMAPHORE)
EFFECT = pltpu.SideEffectType.DATAFLOW_SIDE_EFFECTING


def _hbm(a):
    return pltpu.with_memory_space_constraint(a, pltpu.HBM)


def _exchange(gather, src, land, send, recv, n, act):
    x, y, c = _place()
    for k, (px, py) in enumerate(_other_chips(x, y)):
        for a in range(n):
            if gather:
                s_out, d_out, d_in = src[a], land[a].at[2 * x + y], land[a].at[2 * px + py]
            else:
                s_out, d_out, d_in = src[a].at[2 * px + py], land[a].at[k], land[a].at[k]
            act(pltpu.make_async_remote_copy(
                src_ref=s_out, dst_ref=d_out if act is _start else d_in, send_sem=send.at[k * n + a],
                recv_sem=recv.at[k * n + a], device_id=(px, py, c), device_id_type=MESH))


def _start(cp):
    cp.start()


def _finish(cp):
    cp.wait_send()
    cp.wait_recv()


def _exchange_start(name, gather, srcs, carry):
    n = len(srcs)
    lands = [lax.empty(((N_SHARD,) + s.shape) if gather else ((3,) + s.shape[1:]), s.dtype) for s in srcs]

    def body(*refs):
        _exchange(gather, refs[:n], refs[n:2 * n], refs[2 * n + 1], refs[2 * n + 2], n, _start)

    ops = [_hbm(a) for a in list(srcs) + lands + [carry]]
    out = pl.pallas_call(
        body, name=name,
        out_shape=(pltpu.SemaphoreType.DMA((3 * n,)), pltpu.SemaphoreType.DMA((3 * n,)),
                   *[pltpu.HBM(a.shape, a.dtype) for a in ops]),
        in_specs=[HBM] * len(ops), out_specs=(SEM, SEM, *[HBM] * len(ops)),
        input_output_aliases={i: 2 + i for i in range(len(ops))},
        compiler_params=pltpu.CompilerParams(has_side_effects=EFFECT))(*ops)
    return dict(gather=gather, send=out[0], recv=out[1], srcs=list(out[2:2 + n]), lands=list(out[2 + n:2 + 2 * n])), out[-1]


def _exchange_wait(name, ex, after):
    n = len(ex["srcs"])
    gather = ex["gather"]

    def body(*refs):
        _exchange(gather, refs[:n], refs[n:2 * n], refs[2 * n], refs[2 * n + 1], n, _finish)

    ops = ex["srcs"] + ex["lands"]
    out = pl.pallas_call(
        body, name=name, out_shape=[pltpu.HBM(a.shape, a.dtype) for a in ops],
        in_specs=[HBM] * len(ops) + [SEM, SEM, ANY], out_specs=[HBM] * len(ops),
        input_output_aliases={i: i for i in range(len(ops))},
        compiler_params=pltpu.CompilerParams(has_side_effects=EFFECT))(*ops, ex["send"], ex["recv"], after)
    return list(out[:n]), list(out[n:])


def _swap_sibling(parts):
    n = len(parts)

    def body(*refs):
        src, dst = refs[:n], refs[n:2 * n]
        send, recv = refs[2 * n:]
        x, y, c = _place()
        cps = [pltpu.make_async_remote_copy(src_ref=src[a], dst_ref=dst[a], send_sem=send.at[a], recv_sem=recv.at[a],
                                            device_id=(x, y, 1 - c), device_id_type=MESH) for a in range(n)]
        for cp in cps:
            cp.start()
        for cp in cps:
            cp.wait_recv()
        for cp in cps:
            cp.wait_send()

    return pl.pallas_call(
        body, out_shape=[jax.ShapeDtypeStruct(p.shape, p.dtype) for p in parts],
        in_specs=[ANY] * n, out_specs=[ANY] * n,
        scratch_shapes=[pltpu.SemaphoreType.DMA((n,)), pltpu.SemaphoreType.DMA((n,))],
        name="swap_sibling")(*parts)


def _allreduce_small(name, v):
    R = v.shape[0]

    def body(v_ref, o_ref, buf, send, recv):
        x, y, c = _place()
        me = 4 * x + 2 * y + c
        buf[me] = v_ref[...]
        cps = []
        for k in range(1, 8):
            fx, fy, fc = (k >> 2) & 1, (k >> 1) & 1, k & 1
            px = 1 - x if fx else x
            py = 1 - y if fy else y
            pc = 1 - c if fc else c
            cp = pltpu.make_async_remote_copy(src_ref=v_ref, dst_ref=buf.at[me], send_sem=send.at[k - 1],
                                              recv_sem=recv.at[k - 1], device_id=(px, py, pc), device_id_type=MESH)
            cp.start()
            cps.append((cp, 4 * px + 2 * py + pc))
        for k, (cp, peer) in enumerate(cps):
            pltpu.make_async_remote_copy(src_ref=v_ref, dst_ref=buf.at[peer], send_sem=send.at[k], recv_sem=recv.at[k],
                                         device_id=(x, y, c), device_id_type=MESH).wait_recv()
        for cp, _ in cps:
            cp.wait_send()
        acc = buf[0]
        for d in range(1, 8):
            acc = acc + buf[d]
        o_ref[...] = acc

    return pl.pallas_call(
        body, out_shape=jax.ShapeDtypeStruct((R, 128), F32),
        in_specs=[pl.BlockSpec(memory_space=pltpu.VMEM)], out_specs=pl.BlockSpec(memory_space=pltpu.VMEM),
        scratch_shapes=[pltpu.VMEM((8, R, 128), F32), pltpu.SemaphoreType.DMA((7,)), pltpu.SemaphoreType.DMA((7,))],
        name=name)(v)


def _row_tile(r):
    for t in (256, 128, 64, 32, 16, 8):
        if r % t == 0:
            return t
    raise ValueError(r)


def _sum4(name, own, got):
    R, C = own.shape
    tr = _row_tile(R)

    def body(o_ref, g_ref, s_ref):
        s = o_ref[...].astype(F32)
        for k in range(3):
            s = s + g_ref[k].astype(F32)
        s_ref[...] = s

    return pl.pallas_call(
        body, out_shape=jax.ShapeDtypeStruct((R, C), F32), grid=(R // tr,),
        in_specs=[pl.BlockSpec((tr, C), lambda i: (i, 0)), pl.BlockSpec((3, tr, C), lambda i: (0, i, 0))],
        out_specs=pl.BlockSpec((tr, C), lambda i: (i, 0)), name=name, compiler_params=_cp("parallel"))(own, got)


def _adamw(name, w, gparts, m, v):
    R, C = w.shape
    tr = _row_tile(R)
    ng = len(gparts)
    c1 = 1.0 - ADAM_B1 ** ADAM_STEP
    c2 = 1.0 - ADAM_B2 ** ADAM_STEP

    def body(*refs):
        w_ref = refs[0]
        g_refs = refs[1:1 + ng]
        m_ref, v_ref, go_ref, d_ref, mo_ref, vo_ref = refs[1 + ng:]
        g = g_refs[0][...]
        for r in g_refs[1:]:
            g = g + r[...]
        mn = ADAM_B1 * m_ref[...] + (1.0 - ADAM_B1) * g
        vn = ADAM_B2 * v_ref[...] + (1.0 - ADAM_B2) * (g * g)
        go_ref[...] = g
        mo_ref[...] = mn
        vo_ref[...] = vn
        d_ref[...] = -ADAM_LR * ((mn / c1) / (jnp.sqrt(vn / c2) + ADAM_EPS) + ADAM_WD * w_ref[...])

    blk = pl.BlockSpec((tr, C), lambda i: (i, 0))
    osh = jax.ShapeDtypeStruct((R, C), F32)
    return pl.pallas_call(
        body, out_shape=(osh, osh, osh, osh), grid=(R // tr,), in_specs=[blk] * (3 + ng), out_specs=(blk,) * 4,
        name=name, compiler_params=_cp("parallel"))(w, *gparts, m, v)


def _adamw_layers(name, w, sums, m, v):
    R2, C = w.shape
    R = R2 // DEPTH
    tr = _row_tile(R)
    nr = R // tr
    c1 = 1.0 - ADAM_B1 ** ADAM_STEP
    c2 = 1.0 - ADAM_B2 ** ADAM_STEP

    def body(w_ref, a0, b0, a1, b1, m_ref, v_ref, go_ref, d_ref, mo_ref, vo_ref):
        g = jnp.where(pl.program_id(0) == 0, a0[...] + b0[...], a1[...] + b1[...])
        mn = ADAM_B1 * m_ref[...] + (1.0 - ADAM_B1) * g
        vn = ADAM_B2 * v_ref[...] + (1.0 - ADAM_B2) * (g * g)
        go_ref[...] = g
        mo_ref[...] = mn
        vo_ref[...] = vn
        d_ref[...] = -ADAM_LR * ((mn / c1) / (jnp.sqrt(vn / c2) + ADAM_EPS) + ADAM_WD * w_ref[...])

    blk = pl.BlockSpec((tr, C), lambda l, i: (l * nr + i, 0))
    lay0 = pl.BlockSpec((tr, C), lambda l, i: (jnp.where(l == 0, i, nr - 1), 0))
    lay1 = pl.BlockSpec((tr, C), lambda l, i: (jnp.where(l == 1, i, 0), 0))
    osh = jax.ShapeDtypeStruct((R2, C), F32)
    return pl.pallas_call(
        body, out_shape=(osh, osh, osh, osh), grid=(DEPTH, nr),
        in_specs=[blk, lay0, lay0, lay1, lay1, blk, blk], out_specs=(blk,) * 4,
        name=name, compiler_params=_cp("arbitrary", "arbitrary"))(w, *sums[0], *sums[1], m, v)


BIG = [("ffn1_w_gate", "g1"), ("ffn1_w_up", "u1"), ("ffn1_w_down", "d1"), ("w_in", "win"), ("w_out", "wout"),
       ("ffn2_w_gate", "g2"), ("ffn2_w_up", "u2"), ("ffn2_w_down", "d2")]
SMALL = ["ffn1_norm", "mix_norm", "conv_b", "dt_bias", "a_log", "d_skip", "ssd_norm", "q_norm", "k_norm", "ffn2_norm"]
WEIGHTS = ["ffn1_norm", "ffn1_w_gate", "ffn1_w_up", "ffn1_w_down", "mix_norm", "w_in", "conv_w", "conv_b", "dt_bias",
           "a_log", "d_skip", "ssd_norm", "q_norm", "k_norm", "w_out", "ffn2_norm", "ffn2_w_gate", "ffn2_w_up",
           "ffn2_w_down"]
CONV_SH = CONV_DIM // N_SHARD
GATHER_GROUPS = [(0, "ffn1", ["g1", "u1", "d1"]), (0, "win", ["win", "cw"]), (0, "rest", ["wout", "g2", "u2", "d2"]),
                 (1, "all", ["g1", "u1", "d1", "win", "cw", "wout", "g2", "u2", "d2"])]


def _pad128(v):
    v = v.reshape(-1)
    return jnp.pad(v, (0, (-v.shape[0]) % 128))


def _pack(pieces):
    flat, offs, pos = [], [], 0
    for p in pieces:
        q = _pad128(p.astype(F32))
        offs.append(pos)
        pos += q.shape[0] // 128
        flat.append(q)
    total = -(-pos // 8) * 8
    out = jnp.concatenate(flat + [jnp.zeros(((total - pos) * 128,), F32)]).reshape(total, 128)
    return out, offs


def _unpack(packed, offs, shapes):
    out = []
    for off, shp in zip(offs, shapes):
        n = int(np.prod(shp))
        rows = -(-n // 128)
        out.append(packed[off:off + rows].reshape(-1)[:n].reshape(shp))
    return out


def kernel(x, ffn1_norm, ffn1_w_gate, ffn1_w_up, ffn1_w_down, mix_norm, w_in, conv_w, conv_b, dt_bias, a_log, d_skip, ssd_norm, q_norm, k_norm, w_out, ffn2_norm, ffn2_w_gate, ffn2_w_up, ffn2_w_down, loss_target, m_ffn1_norm, m_ffn1_w_gate, m_ffn1_w_up, m_ffn1_w_down, m_mix_norm, m_w_in, m_conv_w, m_conv_b, m_dt_bias, m_a_log, m_d_skip, m_ssd_norm, m_q_norm, m_k_norm, m_w_out, m_ffn2_norm, m_ffn2_w_gate, m_ffn2_w_up, m_ffn2_w_down, v_ffn1_norm, v_ffn1_w_gate, v_ffn1_w_up, v_ffn1_w_down, v_mix_norm, v_w_in, v_conv_w, v_conv_b, v_dt_bias, v_a_log, v_d_skip, v_ssd_norm, v_q_norm, v_k_norm, v_w_out, v_ffn2_norm, v_ffn2_w_gate, v_ffn2_w_up, v_ffn2_w_down):
    A = dict(locals())
    ix, iy, ic = _place()
    me = 2 * ix + iy
    B, S, _ = x.shape
    T = B * S

    own = {key: A[name].astype(BF16) for name, key in BIG}
    own["cw"] = conv_w
    exs, first_norm = [], ffn1_norm
    for gi, (l, _, keys) in enumerate(GATHER_GROUPS):
        ex, first_norm = _exchange_start("gather_start%d" % gi, True, [own[key][l] for key in keys], first_norm)
        exs.append(ex)
    landed = {}

    def weights(l, group, after):
        gi = [i for i, (gl, gname, _) in enumerate(GATHER_GROUPS) if gl == l and gname in (group, "all")][0]
        if gi not in landed:
            srcs, lands = _exchange_wait("gather_wait%d" % gi, exs[gi], after)
            landed[gi] = {}
            for key, mine, land in zip(GATHER_GROUPS[gi][2], srcs, lands):
                full = lax.dynamic_update_slice(land, mine[None], (me, 0, 0))
                if key == "win":
                    full = _win_pack(jnp.concatenate([full[j] for j in range(N_SHARD)], axis=1))
                if key == "cw":
                    full = jnp.transpose(full, (1, 0, 2)).reshape(CONV_K, CONV_DIM)
                landed[gi][key] = full
        return landed[gi]

    pending = []

    def scatter(l, group, grads, carry):
        keys = sorted(grads)
        arrs = [grads[key] for key in keys]
        if "win" in grads:
            arrs[keys.index("win")] = jnp.transpose(_win_unpack(grads["win"]).reshape(D_MODEL, N_SHARD, IN_SH), (1, 0, 2))
        ex, carry = _exchange_start("scatter_start_l%d_%s" % (l, group), False, arrs, carry)
        pending.append((l, keys, ex))
        return carry

    small = {name: A[name] for name in SMALL}
    small["ffn1_norm"] = first_norm
    lsum, dx, sgrads = _local_step(x.reshape(T, D_MODEL), loss_target.reshape(T, D_MODEL), small, weights, scatter, B)

    names = SMALL + ["conv_w"]
    shapes = [A[n].shape for n in SMALL] + [(DEPTH, CONV_K, CONV_DIM), ()]
    pieces = [jnp.stack([sgrads[l][n].reshape(shp[1:]) for l in range(DEPTH)]) for n, shp in zip(names, shapes)]
    pieces.append(0.5 / D_MODEL * jnp.sum(lsum))
    packed, offs = _pack(pieces)
    red = _allreduce_small("allreduce_small", packed)
    red = _unpack(red, offs, shapes)
    loss = red[-1]
    sg = dict(zip(names, red[:-1]))

    sums, after = {}, dx
    for idx, (l, keys, ex) in enumerate(pending):
        srcs, lands = _exchange_wait("scatter_wait%d" % idx, ex, after)
        for key, g, got in zip(keys, srcs, lands):
            mine = lax.dynamic_index_in_dim(g, me, axis=0, keepdims=False)
            sums[key, l] = after = _sum4("sum_%s_l%d" % (key, l), mine, got)
    order = [(key, l) for _, key in BIG for l in range(DEPTH)]
    theirs = dict(zip(order, _swap_sibling([sums[k] for k in order])))

    out = {}
    for name, key in BIG:
        shp = A[name].shape
        flat = lambda a: a.reshape(shp[0] * shp[1], shp[2])
        res = _adamw_layers("adamw_" + key, flat(A[name]), [(sums[key, l], theirs[key, l]) for l in range(DEPTH)],
                            flat(A["m_" + name]), flat(A["v_" + name]))
        out[name] = [r.reshape(shp) for r in res]

    wp, offs = _pack([A[n] for n in SMALL])
    gp, _ = _pack([sg[n] for n in SMALL])
    mp, _ = _pack([A["m_" + n] for n in SMALL])
    vp, _ = _pack([A["v_" + n] for n in SMALL])
    res = _adamw("adamw_small", wp, [gp], mp, vp)
    shapes = [A[n].shape for n in SMALL]
    res = [_unpack(r, offs, shapes) for r in res]
    for i, n in enumerate(SMALL):
        out[n] = [res[q][i] for q in range(4)]
    gcw = lax.dynamic_slice_in_dim(sg["conv_w"], me * CONV_SH, CONV_SH, axis=2)
    flat = lambda a: a.reshape(DEPTH * CONV_K, CONV_SH)
    res = _adamw("adamw_conv_w", flat(conv_w), [flat(gcw)], flat(m_conv_w), flat(v_conv_w))
    out["conv_w"] = [r.reshape(conv_w.shape) for r in res]

    outs = [loss, dx.reshape(B, S, D_MODEL)]
    for q in range(4):
        outs += [out[n][q] for n in WEIGHTS]
    return tuple(outs)
```

```python
import functools
import math

import numpy as np
import jax
import jax.numpy as jnp
from jax import lax
from jax.experimental import pallas as pl
from jax.experimental.pallas import tpu as pltpu

F32 = jnp.float32
BF16 = jnp.bfloat16

D_MODEL = 1024
DEPTH = 2
N_SHARD = 4
D_FF = 2816
FF_SH = D_FF // N_SHARD
SSD_HEADS = 16
HEAD_DIM = 64
SSD_GROUPS = 4
GROUP_W = 256
SSD_STATE = 128
CONV_K = 4
CONV_DIM = 2048
ATT_HEADS = 16
MIX_W = 2048
MIX_SH = MIX_W // N_SHARD
IN_PROJ = 6160
IN_SH = IN_PROJ // N_SHARD
IN_PAD = 6272
PROJ_TN = 896
COL_Z, COL_XBC, COL_Q, COL_K, COL_V, COL_DT = 0, 1024, 3072, 4096, 5120, 6144
EPS = 1e-6
NEG = -1e30
SSD_L = 256
ATT_B = 256
ROW_T = 512
CONV_CT = 256
CONV_R = 256
PAD_R = 8

ADAM_LR, ADAM_B1, ADAM_B2, ADAM_EPS, ADAM_WD, ADAM_STEP = 0.001, 0.9, 0.999, 1e-08, 0.01, 10

NN = (((1,), (0,)), ((), ()))
NT = (((1,), (1,)), ((), ()))
TN = (((0,), (0,)), ((), ()))

VMEM_LIMIT = 56 * 1024 * 1024


def _cp(*sem):
    return pltpu.CompilerParams(dimension_semantics=sem, vmem_limit_bytes=VMEM_LIMIT)


def _dot(a, b, dims):
    return lax.dot_general(a, b, dims, preferred_element_type=F32)


def _sigmoid(x):
    return 0.5 * jnp.tanh(0.5 * x) + 0.5


def _softplus(x):
    return jnp.maximum(x, 0.0) + jnp.log(1.0 + jnp.exp(-jnp.abs(x)))


def _mm(name, pairs, out_shape, out_spec, grid, dims, acc_shape, res=None, scale=1.0):
    nk = grid[2]
    npair = len(pairs)

    def body(*refs):
        ab = refs[:2 * npair]
        pos = 2 * npair
        res_ref = None
        if res is not None:
            res_ref = refs[pos]
            pos += 1
        out_ref, acc = refs[pos], refs[pos + 1]
        k = pl.program_id(2)

        @pl.when(k == 0)
        def _():
            acc[...] = jnp.zeros_like(acc)

        s = None
        for p in range(npair):
            d = _dot(ab[2 * p][...].astype(BF16), ab[2 * p + 1][...].astype(BF16), dims)
            s = d if s is None else s + d
        acc[...] += s

        @pl.when(k == nk - 1)
        def _():
            r = acc[...]
            if scale != 1.0:
                r = r * scale
            if res_ref is not None:
                r = r + res_ref[...]
            out_ref[...] = r.astype(out_ref.dtype)

    args, specs = [], []
    for a, a_spec, b, b_spec in pairs:
        args += [a, b]
        specs += [a_spec, b_spec]
    if res is not None:
        args.append(res[0])
        specs.append(res[1])
    return pl.pallas_call(
        body, out_shape=out_shape, grid=grid, in_specs=specs, out_specs=out_spec,
        scratch_shapes=[pltpu.VMEM(acc_shape, F32)], name=name,
        compiler_params=_cp("parallel", "parallel", "arbitrary"))(*args)


def _rms_fwd(name, x, w):
    T = x.shape[0]

    def body(x_ref, w_ref, o_ref):
        xv = x_ref[...]
        r = lax.rsqrt(jnp.mean(xv * xv, axis=-1, keepdims=True) + EPS)
        o_ref[...] = (xv * r * w_ref[...]).astype(BF16)

    return pl.pallas_call(
        body, out_shape=jax.ShapeDtypeStruct((T, D_MODEL), BF16), grid=(T // ROW_T,),
        in_specs=[pl.BlockSpec((ROW_T, D_MODEL), lambda i: (i, 0)), pl.BlockSpec((1, D_MODEL), lambda i: (0, 0))],
        out_specs=pl.BlockSpec((ROW_T, D_MODEL), lambda i: (i, 0)), name=name, compiler_params=_cp("parallel"))(x, w)


def _rms_bwd(name, dh, x, w, dres):
    T = x.shape[0]

    def body(dh_ref, x_ref, w_ref, dres_ref, dx_ref, dw_ref):
        @pl.when(pl.program_id(0) == 0)
        def _():
            dw_ref[...] = jnp.zeros_like(dw_ref)

        xv = x_ref[...]
        r = lax.rsqrt(jnp.mean(xv * xv, axis=-1, keepdims=True) + EPS)
        xhat = xv * r
        dhv = dh_ref[...]
        dxhat = dhv * w_ref[...]
        m = jnp.mean(dxhat * xhat, axis=-1, keepdims=True)
        dx_ref[...] = dres_ref[...] + r * (dxhat - xhat * m)
        dw_ref[...] += jnp.sum(dhv * xhat, axis=0, keepdims=True)

    row = pl.BlockSpec((ROW_T, D_MODEL), lambda i: (i, 0))
    vec = pl.BlockSpec((1, D_MODEL), lambda i: (0, 0))
    return pl.pallas_call(
        body, out_shape=(jax.ShapeDtypeStruct((T, D_MODEL), F32), jax.ShapeDtypeStruct((1, D_MODEL), F32)),
        grid=(T // ROW_T,), in_specs=[row, row, vec, row], out_specs=(row, vec), name=name,
        compiler_params=_cp("arbitrary"))(dh, x, w, dres)


def _loss_grad(name, y, t):
    T = y.shape[0]

    def body(y_ref, t_ref, dy_ref, l_ref):
        @pl.when(pl.program_id(0) == 0)
        def _():
            l_ref[...] = jnp.zeros_like(l_ref)

        e = y_ref[...] - t_ref[...]
        dy_ref[...] = e * (1.0 / D_MODEL)
        l_ref[...] += jnp.sum(e * e, axis=0, keepdims=True)

    row = pl.BlockSpec((ROW_T, D_MODEL), lambda i: (i, 0))
    vec = pl.BlockSpec((1, D_MODEL), lambda i: (0, 0))
    return pl.pallas_call(
        body, out_shape=(jax.ShapeDtypeStruct((T, D_MODEL), F32), jax.ShapeDtypeStruct((1, D_MODEL), F32)),
        grid=(T // ROW_T,), in_specs=[row, row], out_specs=(row, vec), name=name,
        compiler_params=_cp("arbitrary"))(y, t)


def _ffn_gate_up(name, h, wg, wu):
    T = h.shape[0]

    def body(h_ref, wg_ref, wu_ref, g_ref, u_ref, a_ref):
        hv = h_ref[...]
        g = _dot(hv, wg_ref[...], NN)
        u = _dot(hv, wu_ref[...], NN)
        g_ref[...] = g.astype(BF16)
        u_ref[...] = u.astype(BF16)
        a_ref[...] = (g * _sigmoid(g) * u).astype(BF16)

    wspec = pl.BlockSpec((None, D_MODEL, FF_SH), lambda j, i: (j, 0, 0))
    ospec = pl.BlockSpec((None, ROW_T, FF_SH), lambda j, i: (j, i, 0))
    osh = jax.ShapeDtypeStruct((N_SHARD, T, FF_SH), BF16)
    return pl.pallas_call(
        body, out_shape=(osh, osh, osh), grid=(N_SHARD, T // ROW_T),
        in_specs=[pl.BlockSpec((ROW_T, D_MODEL), lambda j, i: (i, 0)), wspec, wspec],
        out_specs=(ospec, ospec, ospec), name=name, compiler_params=_cp("parallel", "parallel"))(h, wg, wu)


def _ffn_dact(name, dx, wd, g, u):
    T = dx.shape[0]

    def body(dx_ref, wd_ref, g_ref, u_ref, dg_ref, du_ref):
        da = 0.5 * _dot(dx_ref[...].astype(BF16), wd_ref[...], NT)
        gv = g_ref[...].astype(F32)
        uv = u_ref[...].astype(F32)
        sg = _sigmoid(gv)
        dg_ref[...] = (da * uv * (sg * (1.0 + gv * (1.0 - sg)))).astype(BF16)
        du_ref[...] = (da * gv * sg).astype(BF16)

    aspec = pl.BlockSpec((None, ROW_T, FF_SH), lambda j, i: (j, i, 0))
    osh = jax.ShapeDtypeStruct((N_SHARD, T, FF_SH), BF16)
    return pl.pallas_call(
        body, out_shape=(osh, osh), grid=(N_SHARD, T // ROW_T),
        in_specs=[pl.BlockSpec((ROW_T, D_MODEL), lambda j, i: (i, 0)),
                  pl.BlockSpec((None, FF_SH, D_MODEL), lambda j, i: (j, 0, 0)), aspec, aspec],
        out_specs=(aspec, aspec), name=name, compiler_params=_cp("parallel", "parallel"))(dx, wd, g, u)


def _ffn_fwd(tag, x, nw, wg, wu, wd):
    T = x.shape[0]
    h = _rms_fwd(tag + "_rms", x, nw)
    g, u, a = _ffn_gate_up(tag + "_gu", h, wg, wu)
    nt = T // ROW_T
    xo = _mm(tag + "_down",
             [(a, pl.BlockSpec((None, ROW_T, FF_SH), lambda i, n, k: (k, i, 0)),
               wd, pl.BlockSpec((None, FF_SH, D_MODEL), lambda i, n, k: (k, 0, 0)))],
             jax.ShapeDtypeStruct((T, D_MODEL), F32), pl.BlockSpec((ROW_T, D_MODEL), lambda i, n, k: (i, 0)),
             (nt, 1, N_SHARD), NN, (ROW_T, D_MODEL),
             res=(x, pl.BlockSpec((ROW_T, D_MODEL), lambda i, n, k: (i, 0))), scale=0.5)
    return xo, (x, h, g, u, a)


def _ffn_bwd(tag, dxo, saved, nw, wg, wu, wd, emit):
    x, h, g, u, a = saved
    T = x.shape[0]
    nt = T // ROW_T
    dg, du = _ffn_dact(tag + "_dact", dxo, wd, g, u)
    act = lambda f: pl.BlockSpec((None, ROW_T, FF_SH), f)
    gd = _mm(tag + "_dwd",
             [(a, act(lambda m, n, k: (m, k, 0)), dxo, pl.BlockSpec((ROW_T, D_MODEL), lambda m, n, k: (k, 0)))],
             jax.ShapeDtypeStruct((N_SHARD, FF_SH, D_MODEL), BF16),
             pl.BlockSpec((None, FF_SH, D_MODEL), lambda m, n, k: (m, 0, 0)),
             (N_SHARD, 1, nt), TN, (FF_SH, D_MODEL), scale=0.5)
    hspec = pl.BlockSpec((ROW_T, D_MODEL), lambda j, n, k: (k, 0))
    gsh = jax.ShapeDtypeStruct((N_SHARD, D_MODEL, FF_SH), BF16)
    gspec = pl.BlockSpec((None, D_MODEL, FF_SH), lambda j, n, k: (j, 0, 0))
    gg = _mm(tag + "_dwg", [(h, hspec, dg, act(lambda j, n, k: (j, k, 0)))], gsh, gspec,
             (N_SHARD, 1, nt), TN, (D_MODEL, FF_SH))
    gu = _mm(tag + "_dwu", [(h, hspec, du, act(lambda j, n, k: (j, k, 0)))], gsh, gspec,
             (N_SHARD, 1, nt), TN, (D_MODEL, FF_SH))
    dg = emit(gg, gu, gd, dg)
    wspec = pl.BlockSpec((None, D_MODEL, FF_SH), lambda i, n, k: (k, 0, 0))
    dh = _mm(tag + "_dh",
             [(dg, act(lambda i, n, k: (k, i, 0)), wg, wspec), (du, act(lambda i, n, k: (k, i, 0)), wu, wspec)],
             jax.ShapeDtypeStruct((T, D_MODEL), F32), pl.BlockSpec((ROW_T, D_MODEL), lambda i, n, k: (i, 0)),
             (nt, 1, N_SHARD), NT, (ROW_T, D_MODEL))
    return _rms_bwd(tag + "_rmsb", dh, x, nw, dxo)


def _seq_rows(ref, start, size, S):
    lo, hi = max(start, 0), min(start + size, S)
    parts = [ref[pl.ds(lo, hi - lo), :]]
    if lo > start:
        parts.insert(0, jnp.zeros((lo - start, ref.shape[1]), F32))
    if start + size > hi:
        parts.append(jnp.zeros((start + size - hi, ref.shape[1]), F32))
    return parts[0] if len(parts) == 1 else jnp.concatenate(parts, axis=0)


XBC_CB = COL_XBC // CONV_CT


def _conv_fwd(name, proj, w, b, B):
    T = proj.shape[0]
    S = T // B
    C = CONV_DIM

    def body(x_ref, w_ref, b_ref, o_ref):
        wv = w_ref[...]
        for c in range(S // CONV_R):
            r0 = c * CONV_R
            ch = _seq_rows(x_ref, r0 - PAD_R, CONV_R + PAD_R, S)
            pre = ch[PAD_R:] * wv[3:4] + b_ref[...]
            for s in range(1, CONV_K):
                pre = pre + pltpu.roll(ch, s, axis=0)[PAD_R:] * wv[3 - s:4 - s]
            o_ref[pl.ds(r0, CONV_R), :] = pre * _sigmoid(pre)

    return pl.pallas_call(
        body, out_shape=jax.ShapeDtypeStruct((T, C), F32), grid=(B, C // CONV_CT),
        in_specs=[pl.BlockSpec((S, CONV_CT), lambda bi, ci: (bi, XBC_CB + ci)),
                  pl.BlockSpec((CONV_K, CONV_CT), lambda bi, ci: (0, ci)),
                  pl.BlockSpec((1, CONV_CT), lambda bi, ci: (0, ci))],
        out_specs=pl.BlockSpec((S, CONV_CT), lambda bi, ci: (bi, ci)), name=name,
        compiler_params=_cp("parallel", "parallel"))(proj, w, b)


def _conv_bwd(name, proj, dxs, dB, dC, w, b, dproj, B):
    T = proj.shape[0]
    S = T // B
    C = CONV_DIM
    RW = CONV_R + PAD_R
    nx, nb = dxs.shape[1] // CONV_CT, dB.shape[1] // CONV_CT

    def body(x_ref, dx_in, db_in, dc_in, w_ref, b_ref, buf_ref, dx_ref, dw_ref, db_ref):
        @pl.when(pl.program_id(1) == 0)
        def _():
            dw_ref[...] = jnp.zeros_like(dw_ref)
            db_ref[...] = jnp.zeros_like(db_ref)

        ci = pl.program_id(0)
        wv = w_ref[...]
        dw = [jnp.zeros((1, CONV_CT), F32) for _ in range(CONV_K)]
        db = jnp.zeros((1, CONV_CT), F32)
        for c in range(S // CONV_R):
            r0 = c * CONV_R
            ch = _seq_rows(x_ref, r0 - PAD_R, RW + PAD_R, S)
            xs = [ch[PAD_R:]] + [pltpu.roll(ch, s, axis=0)[PAD_R:] for s in range(1, CONV_K)]
            pre = b_ref[...] + xs[0] * wv[3:4]
            for s in range(1, CONV_K):
                pre = pre + xs[s] * wv[3 - s:4 - s]
            sg = _sigmoid(pre)
            dout = jnp.where(ci < nx, _seq_rows(dx_in, r0, RW, S),
                             jnp.where(ci < nx + nb, _seq_rows(db_in, r0, RW, S), _seq_rows(dc_in, r0, RW, S)))
            dpre = dout * (sg * (1.0 + pre * (1.0 - sg)))
            dx = dpre[:CONV_R] * wv[3:4]
            for s in range(1, CONV_K):
                dx = dx + pltpu.roll(dpre, RW - s, axis=0)[:CONV_R] * wv[3 - s:4 - s]
            dx_ref[pl.ds(r0, CONV_R), :] = dx.astype(BF16)
            dcur = dpre[:CONV_R]
            db = db + jnp.sum(dcur, axis=0, keepdims=True)
            for s in range(CONV_K):
                dw[3 - s] = dw[3 - s] + jnp.sum(dcur * xs[s][:CONV_R], axis=0, keepdims=True)
        db_ref[...] += db
        for k in range(CONV_K):
            dw_ref[k:k + 1, :] += dw[k]

    seq = lambda f: pl.BlockSpec((S, CONV_CT), f)
    return pl.pallas_call(
        body,
        out_shape=(jax.ShapeDtypeStruct(dproj.shape, dproj.dtype), jax.ShapeDtypeStruct((CONV_K, C), F32),
                   jax.ShapeDtypeStruct((1, C), F32)),
        grid=(C // CONV_CT, B),
        in_specs=[seq(lambda ci, bi: (bi, XBC_CB + ci)),
                  seq(lambda ci, bi: (bi, jnp.minimum(ci, nx - 1))),
                  seq(lambda ci, bi: (bi, jnp.clip(ci - nx, 0, nb - 1))),
                  seq(lambda ci, bi: (bi, jnp.clip(ci - nx - nb, 0, nb - 1))),
                  pl.BlockSpec((CONV_K, CONV_CT), lambda ci, bi: (0, ci)),
                  pl.BlockSpec((1, CONV_CT), lambda ci, bi: (0, ci)), ANY],
        out_specs=(seq(lambda ci, bi: (bi, XBC_CB + ci)),
                   pl.BlockSpec((CONV_K, CONV_CT), lambda ci, bi: (0, ci)),
                   pl.BlockSpec((1, CONV_CT), lambda ci, bi: (0, ci))),
        input_output_aliases={6: 0},
        name=name, compiler_params=_cp("parallel", "arbitrary"))(proj, dxs, dB, dC, w, b, dproj)


def _tri_sum(tri, x, dims, tri_first):
    hi = x.astype(BF16)
    r1 = x - hi.astype(F32)
    mid = r1.astype(BF16)
    lo = (r1 - mid.astype(F32)).astype(BF16)
    out = None
    for part in (hi, mid, lo):
        d = _dot(tri, part, dims) if tri_first else _dot(part, tri, dims)
        out = d if out is None else out + d
    return out


def _total(x):
    return jnp.sum(jnp.sum(x, axis=0, keepdims=True), axis=-1, keepdims=True)


def _ssd_common(dtc_ref, dtr_ref, pcol_ref, prow_ref, b_ref, c_ref):
    L = SSD_L
    bias_c, alog_c = pcol_ref[0:1, :], pcol_ref[1:2, :]
    a_c = -jnp.exp(alog_c)
    dt_c = _softplus(dtc_ref[...] + bias_c)
    row = lax.broadcasted_iota(jnp.int32, (L, L), 0)
    col = lax.broadcasted_iota(jnp.int32, (L, L), 1)
    causal = row >= col
    tri = causal.astype(BF16)
    cum_c = _tri_sum(tri, dt_c * a_c, NN, True)
    a_r = -jnp.exp(prow_ref[:, 1:2])
    dt_r = _softplus(dtr_ref[...] + prow_ref[:, 0:1])
    cum_r = _tri_sum(tri, dt_r * a_r, NT, False)
    bb = b_ref[...].astype(BF16)
    cb = c_ref[...].astype(BF16)
    G = _dot(cb, bb, NT)
    return a_c, dt_c, causal, tri, cum_c, cum_r, bb, cb, G


def _ssd_fwd(name, xc, proj, dtc, dtr, pcol, prow, nw, B):
    T = xc.shape[0]
    S = T // B
    nb = S // SSD_L
    L = SSD_L

    def body(xs_ref, b_ref, c_ref, z_ref, dtc_ref, dtr_ref, pcol_ref, prow_ref, nw_ref, y_ref, yn_ref, hs_ref, H, yo_s):
        @pl.when(pl.program_id(2) == 0)
        def _():
            H[...] = jnp.zeros_like(H)

        a_c, dt_c, causal, tri, cum_c, cum_r, bb, cb, G = _ssd_common(dtc_ref, dtr_ref, pcol_ref, prow_ref, b_ref, c_ref)
        dsk = pcol_ref[2:3, :]
        clast = cum_c[L - 1:L, :]
        bf = b_ref[...]
        for h in range(4):
            hs_ref[h] = H[h]
            yo_s[h] = _dot(cb, H[h].astype(BF16), NN)
        for h in range(4):
            sl = slice(HEAD_DIM * h, HEAD_DIM * (h + 1))
            cc = cum_c[:, h:h + 1]
            lm = jnp.exp(jnp.where(causal, cc - cum_r[h:h + 1, :], NEG))
            M = (G * lm).astype(BF16)
            xh = xs_ref[:, sl]
            Xb = (xh * dt_c[:, h:h + 1]).astype(BF16)
            Hh = H[h]
            y = _dot(M, Xb, NN) + jnp.exp(cc) * yo_s[h]
            y_ref[:, sl] = y + dsk[:, h:h + 1] * xh
            cl = clast[:, h:h + 1]
            Bw = (bf * jnp.exp(cl - cc)).astype(BF16)
            H[h] = jnp.exp(cl) * Hh + _dot(Bw, Xb, TN)
        zv = z_ref[...]
        y2 = y_ref[...] * (zv * _sigmoid(zv))
        r = lax.rsqrt(jnp.mean(y2 * y2, axis=-1, keepdims=True) + EPS)
        yn_ref[...] = (y2 * r * nw_ref[...]).astype(BF16)

    rowi = lambda b, g, i: b * nb + i
    grp = pl.BlockSpec((L, GROUP_W), lambda b, g, i: (rowi(b, g, i), g))
    return pl.pallas_call(
        body,
        out_shape=(jax.ShapeDtypeStruct((T, 1024), F32), jax.ShapeDtypeStruct((T, 1024), BF16),
                   jax.ShapeDtypeStruct((B, SSD_GROUPS, nb, 4, SSD_STATE, HEAD_DIM), F32)),
        grid=(B, SSD_GROUPS, nb),
        in_specs=[grp,
                  pl.BlockSpec((L, SSD_STATE), lambda b, g, i: (rowi(b, g, i), 8 + g)),
                  pl.BlockSpec((L, SSD_STATE), lambda b, g, i: (rowi(b, g, i), 12 + g)),
                  grp,
                  pl.BlockSpec((None, L, 4), lambda b, g, i: (g, rowi(b, g, i), 0)),
                  pl.BlockSpec((None, 4, L), lambda b, g, i: (g, 0, rowi(b, g, i))),
                  pl.BlockSpec((None, 3, 4), lambda b, g, i: (g, 0, 0)),
                  pl.BlockSpec((None, 4, 3), lambda b, g, i: (g, 0, 0)),
                  pl.BlockSpec((1, GROUP_W), lambda b, g, i: (0, g))],
        out_specs=(grp, grp,
                   pl.BlockSpec((None, None, None, 4, SSD_STATE, HEAD_DIM), lambda b, g, i: (b, g, i, 0, 0, 0))),
        scratch_shapes=[pltpu.VMEM((4, SSD_STATE, HEAD_DIM), F32), pltpu.VMEM((4, L, HEAD_DIM), F32)], name=name,
        compiler_params=_cp("parallel", "parallel", "arbitrary"))(xc, xc, xc, proj, dtc, dtr, pcol, prow, nw)


def _ssd_bwd(name, dyn, Y, xc, proj, dtc, dtr, pcol, prow, nw, hs, dproj, B):
    T = xc.shape[0]
    S = T // B
    nb = S // SSD_L
    L = SSD_L

    def body(dyn_ref, y_ref, xs_ref, b_ref, c_ref, z_ref, dtc_ref, dtr_ref, pcol_ref, prow_ref, nw_ref, hs_ref, buf_ref,
             dxs_ref, db_ref, dc_ref, dz_ref, ddt_ref, dpar_ref, dnw_ref, dH, dm_s, dxo_s, ea_s, ex_s):
        @pl.when(pl.program_id(2) == 0)
        def _():
            dH[...] = jnp.zeros_like(dH)
            dpar_ref[...] = jnp.zeros_like(dpar_ref)
            dnw_ref[...] = jnp.zeros_like(dnw_ref)

        a_c, dt_c, causal, tri, cum_c, cum_r, bb, cb, G = _ssd_common(dtc_ref, dtr_ref, pcol_ref, prow_ref, b_ref, c_ref)
        dsk = pcol_ref[2:3, :]
        clast = cum_c[L - 1:L, :]
        bf = b_ref[...]
        cf = c_ref[...]
        Yv = y_ref[...]
        zv = z_ref[...]
        sz = _sigmoid(zv)
        silu = zv * sz
        y2 = Yv * silu
        r = lax.rsqrt(jnp.mean(y2 * y2, axis=-1, keepdims=True) + EPS)
        yhat = y2 * r
        dyv = dyn_ref[...]
        dnw_ref[...] += jnp.sum(dyv * yhat, axis=0, keepdims=True)
        dyhat = dyv * nw_ref[...]
        dy2 = r * (dyhat - yhat * jnp.mean(dyhat * yhat, axis=-1, keepdims=True))
        dY = dy2 * silu
        dz_ref[...] = (dy2 * Yv * (sz * (1.0 + zv * (1.0 - sz)))).astype(BF16)

        lane4 = lax.broadcasted_iota(jnp.int32, (1, 4), 1)
        dG = jnp.zeros((L, L), F32)
        dBs = jnp.zeros((L, SSD_STATE), F32)
        dCs = jnp.zeros((L, SSD_STATE), F32)
        ddsk = jnp.zeros((1, 4), F32)
        dcl = jnp.zeros((1, 4), F32)
        for h in range(4):
            sl = slice(HEAD_DIM * h, HEAD_DIM * (h + 1))
            xb = (xs_ref[:, sl] * dt_c[:, h:h + 1]).astype(BF16)
            dm_s[h] = _dot(dY[:, sl].astype(BF16), xb, NT)
            dxo_s[h] = _dot(bb, dH[h].astype(BF16), NN)
        for h in range(4):
            sl = slice(HEAD_DIM * h, HEAD_DIM * (h + 1))
            onehot = (lane4 == h).astype(F32)
            cc = cum_c[:, h:h + 1]
            cl = clast[:, h:h + 1]
            lm = jnp.exp(jnp.where(causal, cc - cum_r[h:h + 1, :], NEG))
            M = (G * lm).astype(BF16)
            xh = xs_ref[:, sl]
            dth = dt_c[:, h:h + 1]
            X = xh * dth
            Xb = X.astype(BF16)
            dYh = dY[:, sl]
            dYb = dYh.astype(BF16)
            Hb = hs_ref[h].astype(BF16)
            dHh = dH[h]
            dHb = dHh.astype(BF16)
            alpha = jnp.exp(cc)
            beta = jnp.exp(cl - cc)
            dXoff = beta * dxo_s[h]
            dX = _dot(M, dYb, TN) + dXoff
            dG = dG + dm_s[h] * lm
            dCs = dCs + _dot((alpha * dYh).astype(BF16), Hb, NT)
            dBs = dBs + _dot((beta * X).astype(BF16), dHb, NT)
            ypre = Yv[:, sl] - dsk[:, h:h + 1] * xh
            ea_s[:, sl] = dYb.astype(F32) * ypre - Xb.astype(F32) * dX
            ex_s[:, sl] = dX * xh
            dcl_h = (_total(dHh * (jnp.exp(cl) * hs_ref[h])) + _total(Xb.astype(F32) * dXoff))
            dcl = dcl + dcl_h * onehot
            ddsk = ddsk + _total(dYh * xh) * onehot
            dxs_ref[:, sl] = dsk[:, h:h + 1] * dYh + dX * dth
            dH[h] = jnp.exp(cl) * dHh + _dot((alpha * cf).astype(BF16), dYb, TN)
        dGb = dG.astype(BF16)
        dc_ref[...] = _dot(dGb, bb, NN) + dCs
        db_ref[...] = _dot(dGb, cb, TN) + dBs
        feat = lax.broadcasted_iota(jnp.int32, (GROUP_W, 4), 0)
        head = lax.broadcasted_iota(jnp.int32, (GROUP_W, 4), 1) * HEAD_DIM
        sel = ((feat >= head) & (feat < head + HEAD_DIM)).astype(BF16)
        dA = _tri_sum(sel, ea_s[...], NN, False)
        ddtx = _tri_sum(sel, ex_s[...], NN, False)
        last = lax.broadcasted_iota(jnp.int32, (L, 1), 0) == L - 1
        dA = dA + jnp.where(last, dcl, 0.0)
        dadt = _tri_sum(tri, dA, TN, True)
        ddt = dadt * a_c + ddtx
        d_a = jnp.sum(dadt * dt_c, axis=0, keepdims=True)
        ddraw = ddt * _sigmoid(dtc_ref[...] + pcol_ref[0:1, :])
        ddt_ref[...] = ddraw
        dpar_ref[0:1, :] += jnp.sum(ddraw, axis=0, keepdims=True)
        dpar_ref[1:2, :] += d_a * a_c
        dpar_ref[2:3, :] += ddsk

    rowi = lambda b, g, i: b * nb + (nb - 1 - i)
    grp = pl.BlockSpec((L, GROUP_W), lambda b, g, i: (rowi(b, g, i), g))
    st = pl.BlockSpec((L, SSD_STATE), lambda b, g, i: (rowi(b, g, i), g))
    f = jax.ShapeDtypeStruct
    return pl.pallas_call(
        body,
        out_shape=(f((T, 1024), F32), f((T, 512), F32), f((T, 512), F32), f(dproj.shape, dproj.dtype),
                   f((SSD_GROUPS, T, 4), F32), f((B, SSD_GROUPS, 3, 4), F32), f((B, 1, 1024), F32)),
        grid=(B, SSD_GROUPS, nb),
        in_specs=[grp, grp, grp,
                  pl.BlockSpec((L, SSD_STATE), lambda b, g, i: (rowi(b, g, i), 8 + g)),
                  pl.BlockSpec((L, SSD_STATE), lambda b, g, i: (rowi(b, g, i), 12 + g)),
                  grp,
                  pl.BlockSpec((None, L, 4), lambda b, g, i: (g, rowi(b, g, i), 0)),
                  pl.BlockSpec((None, 4, L), lambda b, g, i: (g, 0, rowi(b, g, i))),
                  pl.BlockSpec((None, 3, 4), lambda b, g, i: (g, 0, 0)),
                  pl.BlockSpec((None, 4, 3), lambda b, g, i: (g, 0, 0)),
                  pl.BlockSpec((1, GROUP_W), lambda b, g, i: (0, g)),
                  pl.BlockSpec((None, None, None, 4, SSD_STATE, HEAD_DIM), lambda b, g, i: (b, g, nb - 1 - i, 0, 0, 0)),
                  ANY],
        out_specs=(grp, st, st, grp,
                   pl.BlockSpec((None, L, 4), lambda b, g, i: (g, rowi(b, g, i), 0)),
                   pl.BlockSpec((None, None, 3, 4), lambda b, g, i: (b, g, 0, 0)),
                   pl.BlockSpec((None, 1, GROUP_W), lambda b, g, i: (b, 0, g))),
        input_output_aliases={12: 3},
        scratch_shapes=[pltpu.VMEM((4, SSD_STATE, HEAD_DIM), F32), pltpu.VMEM((4, L, L), F32),
                        pltpu.VMEM((4, L, HEAD_DIM), F32), pltpu.VMEM((L, GROUP_W), F32),
                        pltpu.VMEM((L, GROUP_W), F32)], name=name,
        compiler_params=_cp("parallel", "parallel", "arbitrary"))(
            dyn, Y, xc, xc, xc, proj, dtc, dtr, pcol, prow, nw, hs, dproj)


def _headnorm_fwd(name, proj, col_block, w):
    T = proj.shape[0]

    def body(x_ref, w_ref, o_ref):
        for h in range(ATT_HEADS):
            sl = slice(HEAD_DIM * h, HEAD_DIM * (h + 1))
            xh = x_ref[:, sl]
            r = lax.rsqrt(jnp.mean(xh * xh, axis=-1, keepdims=True) + EPS)
            o_ref[:, sl] = (xh * r * w_ref[...]).astype(BF16)

    return pl.pallas_call(
        body, out_shape=jax.ShapeDtypeStruct((T, 1024), BF16), grid=(T // ROW_T,),
        in_specs=[pl.BlockSpec((ROW_T, 1024), lambda i: (i, col_block)), pl.BlockSpec((1, HEAD_DIM), lambda i: (0, 0))],
        out_specs=pl.BlockSpec((ROW_T, 1024), lambda i: (i, 0)), name=name, compiler_params=_cp("parallel"))(proj, w)


def _headnorm_bwd(name, dn, proj, col_block, w, dproj):
    T = proj.shape[0]

    def body(dn_ref, x_ref, w_ref, buf_ref, dx_ref, dw_ref):
        @pl.when(pl.program_id(0) == 0)
        def _():
            dw_ref[...] = jnp.zeros_like(dw_ref)

        dw = jnp.zeros((1, HEAD_DIM), F32)
        for h in range(ATT_HEADS):
            sl = slice(HEAD_DIM * h, HEAD_DIM * (h + 1))
            xh = x_ref[:, sl]
            r = lax.rsqrt(jnp.mean(xh * xh, axis=-1, keepdims=True) + EPS)
            xhat = xh * r
            dnh = dn_ref[:, sl]
            dxhat = dnh * w_ref[...]
            dx_ref[:, sl] = (r * (dxhat - xhat * jnp.mean(dxhat * xhat, axis=-1, keepdims=True))).astype(BF16)
            dw = dw + jnp.sum(dnh * xhat, axis=0, keepdims=True)
        dw_ref[...] += dw

    here = pl.BlockSpec((ROW_T, 1024), lambda i: (i, col_block))
    return pl.pallas_call(
        body, out_shape=(jax.ShapeDtypeStruct(dproj.shape, dproj.dtype), jax.ShapeDtypeStruct((1, HEAD_DIM), F32)),
        grid=(T // ROW_T,),
        in_specs=[pl.BlockSpec((ROW_T, 1024), lambda i: (i, 0)), here, pl.BlockSpec((1, HEAD_DIM), lambda i: (0, 0)), ANY],
        out_specs=(here, pl.BlockSpec((1, HEAD_DIM), lambda i: (0, 0))), input_output_aliases={3: 0},
        name=name, compiler_params=_cp("arbitrary"))(dn, proj, w, dproj)


def _att_bias(nq):
    j = np.arange(ATT_B)[:, None]
    i = np.arange(ATT_B)[None, :]
    out = np.empty((nq, ATT_B, ATT_B), np.float32)
    for dblk in range(nq):
        dl = ATT_B * dblk + i - j
        cnt = ((dl >= 0) & (dl <= 128)).astype(np.float32)
        cnt += ((dl >= 0) & (dl % 4 == 0) & (dl <= 512))
        cnt += ((dl >= 0) & (dl % 16 == 0) & (dl <= 2048))
        out[dblk] = np.where(cnt > 0, np.log(np.maximum(cnt, 1.0)), NEG)
    return jnp.asarray(out)


def _row_pair(nq):
    def f(r, c):
        first = c <= r
        return jnp.where(first, r, nq - 1 - r), jnp.where(first, c, c - (r + 1))
    return f


def _col_pair(nq):
    def f(r, c):
        first = c < nq - r
        kj = jnp.where(first, r, nq - 1 - r)
        return jnp.where(first, r + c, nq - 1 - r + (c - (nq - r))), kj
    return f


ATT_SCALE = 1.0 / math.sqrt(HEAD_DIM)
ATT_HS = 4
ATT_W = ATT_HS * HEAD_DIM


def _att_maps(nq, qk):
    return dict(
        q_tok=lambda b, g, r, c: (b * nq + qk(r, c)[0], g),
        k_tok=lambda b, g, r, c: (b * nq + qk(r, c)[1], g),
        q_feat=lambda b, g, r, c: (g, b * nq + qk(r, c)[0]),
        k_feat=lambda b, g, r, c: (g, b * nq + qk(r, c)[1]),
        bias=lambda b, g, r, c: (qk(r, c)[0] - qk(r, c)[1], 0, 0),
        lse=lambda b, g, r, c: (g, 0, b * nq + qk(r, c)[0]),
        do_tok=lambda b, g, r, c: (b * nq + qk(r, c)[0], ATT_HS + g))


def _att_fwd(name, kn, qT, vT, bias, B):
    T = kn.shape[0]
    nq = (T // B) // ATT_B
    qk = _row_pair(nq)
    mp = _att_maps(nq, qk)

    def body(k_ref, qT_ref, vT_ref, bias_ref, oT_ref, lse_ref, m_s, l_s, acc_s, s_s):
        qi, kj = qk(pl.program_id(2), pl.program_id(3))

        @pl.when(kj == 0)
        def _():
            m_s[...] = jnp.full_like(m_s, NEG)
            l_s[...] = jnp.zeros_like(l_s)
            acc_s[...] = jnp.zeros_like(acc_s)

        bv = bias_ref[...]
        for h in range(ATT_HS):
            rs = slice(HEAD_DIM * h, HEAD_DIM * (h + 1))
            s_s[h] = _dot(k_ref[:, rs], qT_ref[rs, :], NN)
        for h in range(ATT_HS):
            rs = slice(HEAD_DIM * h, HEAD_DIM * (h + 1))
            s = s_s[h] + bv
            m_prev = m_s[h:h + 1, :]
            m_new = jnp.maximum(m_prev, jnp.max(s, axis=0, keepdims=True))
            alpha = jnp.exp(m_prev - m_new)
            p = jnp.exp(s - m_new)
            l_s[h:h + 1, :] = alpha * l_s[h:h + 1, :] + jnp.sum(p, axis=0, keepdims=True)
            acc_s[rs, :] = alpha * acc_s[rs, :] + _dot(vT_ref[rs, :], p.astype(BF16), NN)
            m_s[h:h + 1, :] = m_new

        @pl.when(kj == qi)
        def _():
            for h in range(ATT_HS):
                rs = slice(HEAD_DIM * h, HEAD_DIM * (h + 1))
                oT_ref[rs, :] = (acc_s[rs, :] / l_s[h:h + 1, :]).astype(BF16)
            lse_ref[...] = m_s[...] + jnp.log(l_s[...])

    tok = (ATT_B, ATT_W)
    feat = (ATT_W, ATT_B)
    return pl.pallas_call(
        body,
        out_shape=(jax.ShapeDtypeStruct((1024, T), BF16), jax.ShapeDtypeStruct((ATT_HEADS // ATT_HS, ATT_HS, T), F32)),
        grid=(B, ATT_HEADS // ATT_HS, nq // 2, nq + 1),
        in_specs=[pl.BlockSpec(tok, mp["k_tok"]), pl.BlockSpec(feat, mp["q_feat"]), pl.BlockSpec(feat, mp["k_feat"]),
                  pl.BlockSpec((None, ATT_B, ATT_B), mp["bias"])],
        out_specs=(pl.BlockSpec(feat, mp["q_feat"]), pl.BlockSpec((None, ATT_HS, ATT_B), mp["lse"])),
        scratch_shapes=[pltpu.VMEM((ATT_HS, ATT_B), F32), pltpu.VMEM((ATT_HS, ATT_B), F32),
                        pltpu.VMEM((ATT_W, ATT_B), F32), pltpu.VMEM((ATT_HS, ATT_B, ATT_B), F32)],
        name=name, compiler_params=_cp("parallel", "parallel", "arbitrary", "arbitrary"))(kn, qT, vT, bias)


def _att_scores(k_ref, qT_ref, v_ref, doT_ref, s_s, dp_s):
    for h in range(ATT_HS):
        rs = slice(HEAD_DIM * h, HEAD_DIM * (h + 1))
        s_s[h] = _dot(k_ref[:, rs], qT_ref[rs, :], NN)
        dp_s[h] = _dot(v_ref[:, rs], doT_ref[rs, :].astype(BF16), NN)


def _att_p_ds(s_s, dp_s, doT_ref, oT_ref, lse_ref, bv, h):
    rs = slice(HEAD_DIM * h, HEAD_DIM * (h + 1))
    delta = jnp.sum(doT_ref[rs, :] * oT_ref[rs, :].astype(F32), axis=0, keepdims=True)
    p = jnp.exp(s_s[h] + bv - lse_ref[h:h + 1, :])
    return p, p * (dp_s[h] - delta)


def _att_bwd_dq(name, kn, qT, vb, knT, bias, doT, oT, lse, B):
    T = kn.shape[0]
    nq = (T // B) // ATT_B
    qk = _row_pair(nq)
    mp = _att_maps(nq, qk)

    def body(k_ref, qT_ref, v_ref, kT_ref, bias_ref, doT_ref, oT_ref, lse_ref, dqT_ref, acc_s, s_s, dp_s):
        qi, kj = qk(pl.program_id(2), pl.program_id(3))

        @pl.when(kj == 0)
        def _():
            acc_s[...] = jnp.zeros_like(acc_s)

        bv = bias_ref[...]
        _att_scores(k_ref, qT_ref, v_ref, doT_ref, s_s, dp_s)
        for h in range(ATT_HS):
            rs = slice(HEAD_DIM * h, HEAD_DIM * (h + 1))
            p, ds = _att_p_ds(s_s, dp_s, doT_ref, oT_ref, lse_ref, bv, h)
            acc_s[rs, :] += _dot(kT_ref[rs, :], ds.astype(BF16), NN)

        @pl.when(kj == qi)
        def _():
            dqT_ref[...] = acc_s[...] * ATT_SCALE

    tok = (ATT_B, ATT_W)
    feat = (ATT_W, ATT_B)
    return pl.pallas_call(
        body, out_shape=jax.ShapeDtypeStruct((1024, T), F32), grid=(B, ATT_HEADS // ATT_HS, nq // 2, nq + 1),
        in_specs=[pl.BlockSpec(tok, mp["k_tok"]), pl.BlockSpec(feat, mp["q_feat"]), pl.BlockSpec(tok, mp["k_tok"]),
                  pl.BlockSpec(feat, mp["k_feat"]), pl.BlockSpec((None, ATT_B, ATT_B), mp["bias"]),
                  pl.BlockSpec(feat, mp["q_feat"]), pl.BlockSpec(feat, mp["q_feat"]),
                  pl.BlockSpec((None, ATT_HS, ATT_B), mp["lse"])],
        out_specs=pl.BlockSpec(feat, mp["q_feat"]),
        scratch_shapes=[pltpu.VMEM((ATT_W, ATT_B), F32), pltpu.VMEM((ATT_HS, ATT_B, ATT_B), F32),
                        pltpu.VMEM((ATT_HS, ATT_B, ATT_B), F32)],
        name=name, compiler_params=_cp("parallel", "parallel", "arbitrary", "arbitrary"))(
            kn, qT, vb, knT, bias, doT, oT, lse)


def _att_bwd_dkv(name, kn, qT, vb, qn, bias, doT, oT, lse, dyn, dproj, B):
    T = kn.shape[0]
    nq = (T // B) // ATT_B
    qk = _col_pair(nq)
    mp = _att_maps(nq, qk)

    def body(k_ref, qT_ref, v_ref, q_ref, bias_ref, doT_ref, oT_ref, lse_ref, do_ref, buf_ref, dk_ref, dv_ref, dk_s, dv_s,
             s_s, dp_s):
        qi, kj = qk(pl.program_id(2), pl.program_id(3))

        @pl.when(qi == kj)
        def _():
            dk_s[...] = jnp.zeros_like(dk_s)
            dv_s[...] = jnp.zeros_like(dv_s)

        bv = bias_ref[...]
        _att_scores(k_ref, qT_ref, v_ref, doT_ref, s_s, dp_s)
        for h in range(ATT_HS):
            rs = slice(HEAD_DIM * h, HEAD_DIM * (h + 1))
            p, ds = _att_p_ds(s_s, dp_s, doT_ref, oT_ref, lse_ref, bv, h)
            dv_s[h] += _dot(p.astype(BF16), do_ref[:, rs].astype(BF16), NN)
            dk_s[h] += _dot(ds.astype(BF16), q_ref[:, rs], NN)

        @pl.when(qi == nq - 1)
        def _():
            for h in range(ATT_HS):
                rs = slice(HEAD_DIM * h, HEAD_DIM * (h + 1))
                dk_ref[:, rs] = dk_s[h] * ATT_SCALE
                dv_ref[:, rs] = dv_s[h].astype(BF16)

    tok = (ATT_B, ATT_W)
    feat = (ATT_W, ATT_B)
    v_cb = COL_V // ATT_W
    return pl.pallas_call(
        body, out_shape=(jax.ShapeDtypeStruct((T, 1024), F32), jax.ShapeDtypeStruct(dproj.shape, dproj.dtype)),
        grid=(B, ATT_HEADS // ATT_HS, nq // 2, nq + 1),
        in_specs=[pl.BlockSpec(tok, mp["k_tok"]), pl.BlockSpec(feat, mp["q_feat"]), pl.BlockSpec(tok, mp["k_tok"]),
                  pl.BlockSpec(tok, mp["q_tok"]), pl.BlockSpec((None, ATT_B, ATT_B), mp["bias"]),
                  pl.BlockSpec(feat, mp["q_feat"]), pl.BlockSpec(feat, mp["q_feat"]),
                  pl.BlockSpec((None, ATT_HS, ATT_B), mp["lse"]), pl.BlockSpec(tok, mp["do_tok"]), ANY],
        out_specs=(pl.BlockSpec(tok, mp["k_tok"]),
                   pl.BlockSpec(tok, lambda b, g, r, c: (b * nq + qk(r, c)[1], v_cb + g))),
        input_output_aliases={9: 1},
        scratch_shapes=[pltpu.VMEM((ATT_HS, ATT_B, HEAD_DIM), F32), pltpu.VMEM((ATT_HS, ATT_B, HEAD_DIM), F32),
                        pltpu.VMEM((ATT_HS, ATT_B, ATT_B), F32), pltpu.VMEM((ATT_HS, ATT_B, ATT_B), F32)],
        name=name, compiler_params=_cp("parallel", "parallel", "arbitrary", "arbitrary"))(
            kn, qT, vb, qn, bias, doT, oT, lse, dyn, dproj)


def _group_cols(v):
    return v.reshape(SSD_GROUPS, 4)


def _ssd_params(p):
    rows = jnp.stack([_group_cols(p["dt_bias"]), _group_cols(p["a_log"]), _group_cols(p["d_skip"])], axis=1)
    return rows, jnp.swapaxes(rows, 1, 2)


def _mixer_fwd(tag, x1, p, weights, bias, B):
    T = x1.shape[0]
    S = T // B
    nt = T // ROW_T
    h2 = _rms_fwd(tag + "_mixrms", x1, p["mix_norm"][None])
    wi = weights("win", h2)
    win, cw = wi["win"], wi["cw"]
    proj = _mm(tag + "_proj",
               [(h2, pl.BlockSpec((ROW_T, D_MODEL), lambda j, i, k: (i, 0)),
                 win, pl.BlockSpec((D_MODEL, PROJ_TN), lambda j, i, k: (0, j)))],
               jax.ShapeDtypeStruct((T, IN_PAD), F32), pl.BlockSpec((ROW_T, PROJ_TN), lambda j, i, k: (i, j)),
               (IN_PAD // PROJ_TN, nt, 1), NN, (ROW_T, PROJ_TN))
    xc = _conv_fwd(tag + "_conv", proj, cw, p["conv_b"][None], B)
    dtraw = proj[:, COL_DT:COL_DT + SSD_HEADS].reshape(T, SSD_GROUPS, 4)
    dtc = jnp.transpose(dtraw, (1, 0, 2))
    dtr = jnp.transpose(dtraw, (1, 2, 0))
    pcol, prow = _ssd_params(p)
    Y, y_ssd, hs = _ssd_fwd(tag + "_ssd", xc, proj, dtc, dtr, pcol, prow, p["ssd_norm"][None], B)
    qn = _headnorm_fwd(tag + "_qn", proj, COL_Q // 1024, p["q_norm"][None])
    kn = _headnorm_fwd(tag + "_kn", proj, COL_K // 1024, p["k_norm"][None])
    qT = (qn * ATT_SCALE).T
    vb = proj[:, COL_V:COL_V + 1024].astype(BF16)
    oT, lse = _att_fwd(tag + "_att", kn, qT, vb.T, bias, B)
    ymix = jnp.concatenate([y_ssd, oT.T], axis=1)
    rest = weights("rest", ymix)
    x2 = _mm(tag + "_out",
             [(ymix, pl.BlockSpec((ROW_T, MIX_SH), lambda i, n, k: (i, k)),
               rest["wout"], pl.BlockSpec((None, MIX_SH, D_MODEL), lambda i, n, k: (k, 0, 0)))],
             jax.ShapeDtypeStruct((T, D_MODEL), F32), pl.BlockSpec((ROW_T, D_MODEL), lambda i, n, k: (i, 0)),
             (nt, 1, N_SHARD), NN, (ROW_T, D_MODEL),
             res=(x1, pl.BlockSpec((ROW_T, D_MODEL), lambda i, n, k: (i, 0))))
    saved = dict(x1=x1, h2=h2, proj=proj, xc=xc, dtc=dtc, dtr=dtr, Y=Y, hs=hs,
                 qn=qn, kn=kn, qT=qT, vb=vb, oT=oT, lse=lse, ymix=ymix, win=win, cw=cw, wout=rest["wout"])
    return x2, saved


def _mixer_bwd(tag, dx2, sv, p, bias, B):
    T = dx2.shape[0]
    S = T // B
    nt = T // ROW_T
    sg = {}
    dymix = _mm(tag + "_dymix",
                [(dx2, pl.BlockSpec((ROW_T, D_MODEL), lambda n, i, k: (i, 0)),
                  sv["wout"], pl.BlockSpec((None, MIX_SH, D_MODEL), lambda n, i, k: (n, 0, 0)))],
                jax.ShapeDtypeStruct((T, MIX_W), F32), pl.BlockSpec((ROW_T, MIX_SH), lambda n, i, k: (i, n)),
                (N_SHARD, nt, 1), NT, (ROW_T, MIX_SH))
    gwout = _mm(tag + "_dwout",
                [(sv["ymix"], pl.BlockSpec((ROW_T, MIX_SH), lambda m, n, k: (k, m)),
                  dx2, pl.BlockSpec((ROW_T, D_MODEL), lambda m, n, k: (k, 0)))],
                jax.ShapeDtypeStruct((N_SHARD, MIX_SH, D_MODEL), BF16),
                pl.BlockSpec((None, MIX_SH, D_MODEL), lambda m, n, k: (m, 0, 0)),
                (N_SHARD, 1, nt), TN, (MIX_SH, D_MODEL))
    proj = sv["proj"]
    doT = dymix[:, 1024:].T
    dqn = _att_bwd_dq(tag + "_attdq", sv["kn"], sv["qT"], sv["vb"], sv["kn"].T, bias, doT, sv["oT"], sv["lse"], B).T
    dproj = lax.empty((T, IN_PAD), BF16)
    dkn, dproj = _att_bwd_dkv(tag + "_attdkv", sv["kn"], sv["qT"], sv["vb"], sv["qn"], bias, doT, sv["oT"], sv["lse"],
                              dymix, dproj, B)
    dproj, sg["q_norm"] = _headnorm_bwd(tag + "_qnb", dqn, proj, COL_Q // 1024, p["q_norm"][None], dproj)
    dproj, sg["k_norm"] = _headnorm_bwd(tag + "_knb", dkn, proj, COL_K // 1024, p["k_norm"][None], dproj)
    pcol, prow = _ssd_params(p)
    dxs, dB, dC, dproj, ddt, dpar, dnw = _ssd_bwd(tag + "_ssdb", dymix, sv["Y"], sv["xc"], proj, sv["dtc"], sv["dtr"],
                                                  pcol, prow, p["ssd_norm"][None], sv["hs"], dproj, B)
    dpar = jnp.sum(dpar, axis=0)
    sg["dt_bias"] = dpar[:, 0, :].reshape(SSD_HEADS)
    sg["a_log"] = dpar[:, 1, :].reshape(SSD_HEADS)
    sg["d_skip"] = dpar[:, 2, :].reshape(SSD_HEADS)
    sg["ssd_norm"] = jnp.sum(dnw, axis=0)
    dproj, sg["conv_w"], sg["conv_b"] = _conv_bwd(tag + "_convb", proj, dxs, dB, dC, sv["cw"], p["conv_b"][None],
                                                  dproj, B)
    ddt16 = jnp.transpose(ddt, (1, 0, 2)).reshape(T, SSD_HEADS)
    dproj = lax.dynamic_update_slice(dproj, jnp.pad(ddt16, ((0, 0), (0, IN_PAD - COL_DT - SSD_HEADS))).astype(BF16),
                                     (0, COL_DT))
    win = sv["win"]
    gwin = _mm(tag + "_dwin",
               [(sv["h2"], pl.BlockSpec((ROW_T, D_MODEL), lambda n, m, k: (k, 0)),
                 dproj, pl.BlockSpec((ROW_T, PROJ_TN), lambda n, m, k: (k, n)))],
               jax.ShapeDtypeStruct((D_MODEL, IN_PAD), BF16), pl.BlockSpec((D_MODEL, PROJ_TN), lambda n, m, k: (0, n)),
               (IN_PAD // PROJ_TN, 1, nt), TN, (D_MODEL, PROJ_TN))
    dh2 = _mm(tag + "_dh2",
              [(dproj, pl.BlockSpec((ROW_T, PROJ_TN), lambda i, n, k: (i, k)),
                win, pl.BlockSpec((D_MODEL, PROJ_TN), lambda i, n, k: (0, k)))],
              jax.ShapeDtypeStruct((T, D_MODEL), F32), pl.BlockSpec((ROW_T, D_MODEL), lambda i, n, k: (i, 0)),
              (nt, 1, IN_PAD // PROJ_TN), NT, (ROW_T, D_MODEL))
    dx1, sg["mix_norm"] = _rms_bwd(tag + "_mixrmsb", dh2, sv["x1"], p["mix_norm"][None], dx2)
    return dx1, sg, gwout, gwin


def _win_pack(w):
    return jnp.concatenate([w[:, :3072], w[:, 3088:], w[:, 3072:3088],
                            jnp.zeros((w.shape[0], IN_PAD - IN_PROJ), w.dtype)], axis=1)


def _win_unpack(g):
    return jnp.concatenate([g[:, :3072], g[:, COL_DT:COL_DT + SSD_HEADS], g[:, 3072:COL_DT]], axis=1)


def _local_step(x, target, small, weights, scatter, B):
    T = x.shape[0]
    bias = _att_bias((T // B) // ATT_B)
    saved = []
    h = x
    for l in range(DEPTH):
        tag = "l%d" % l
        p = {k: v[l] for k, v in small.items()}
        w1 = weights(l, "ffn1", h)
        x1, ffn1 = _ffn_fwd(tag + "f1", h, p["ffn1_norm"][None], w1["g1"], w1["u1"], w1["d1"])
        x2, sv = _mixer_fwd(tag, x1, p, functools.partial(weights, l), bias, B)
        w2 = weights(l, "rest", x2)
        h, ffn2 = _ffn_fwd(tag + "f2", x2, p["ffn2_norm"][None], w2["g2"], w2["u2"], w2["d2"])
        saved.append((ffn1, sv, ffn2, w1, w2))
    d, lsum = _loss_grad("loss", h, target)
    sgrads = [None] * DEPTH
    for l in reversed(range(DEPTH)):
        tag = "l%db" % l
        p = {k: v[l] for k, v in small.items()}
        ffn1, sv, ffn2, w1, w2 = saved[l]
        sg = {}
        d, sg["ffn2_norm"] = _ffn_bwd(tag + "f2", d, ffn2, p["ffn2_norm"][None], w2["g2"], w2["u2"], w2["d2"],
                                      lambda gg, gu, gd, c, l=l: scatter(l, "ffn2", dict(g2=gg, u2=gu, d2=gd), c))
        d, sgm, gwout, gwin = _mixer_bwd(tag, d, sv, p, bias, B)
        sg.update(sgm)
        d = scatter(l, "mixer", dict(wout=gwout, win=gwin), d)
        d, sg["ffn1_norm"] = _ffn_bwd(tag + "f1", d, ffn1, p["ffn1_norm"][None], w1["g1"], w1["u1"], w1["d1"],
                                      lambda gg, gu, gd, c, l=l: scatter(l, "ffn1", dict(g1=gg, u1=gu, d1=gd), c))
        sgrads[l] = sg
    return lsum, d, sgrads


MESH = pl.DeviceIdType.MESH
ANY = pl.BlockSpec(memory_space=pl.ANY)


def _place():
    return lax.axis_index("x"), lax.axis_index("y"), lax.axis_index("c")


def _other_chips(x, y):
    return [(1 - x, y), (x, 1 - y), (1 - x, 1 - y)]


HBM = pl.BlockSpec(memory_space=pltpu.HBM)
SEM = pl.BlockSpec(memory_space=pltpu.SEMAPHORE)
EFFECT = pltpu.SideEffectType.DATAFLOW_SIDE_EFFECTING


def _hbm(a):
    return pltpu.with_memory_space_constraint(a, pltpu.HBM)


def _exchange(gather, src, land, send, recv, n, act):
    x, y, c = _place()
    for k, (px, py) in enumerate(_other_chips(x, y)):
        for a in range(n):
            if gather:
                s_out, d_out, d_in = src[a], land[a].at[2 * x + y], land[a].at[2 * px + py]
            else:
                s_out, d_out, d_in = src[a].at[2 * px + py], land[a].at[k], land[a].at[k]
            act(pltpu.make_async_remote_copy(
                src_ref=s_out, dst_ref=d_out if act is _start else d_in, send_sem=send.at[k * n + a],
                recv_sem=recv.at[k * n + a], device_id=(px, py, c), device_id_type=MESH))


def _start(cp):
    cp.start()


def _finish(cp):
    cp.wait_send()
    cp.wait_recv()


def _exchange_start(name, gather, srcs, carry):
    n = len(srcs)
    lands = [lax.empty(((N_SHARD,) + s.shape) if gather else ((3,) + s.shape[1:]), s.dtype) for s in srcs]

    def body(*refs):
        _exchange(gather, refs[:n], refs[n:2 * n], refs[2 * n + 1], refs[2 * n + 2], n, _start)

    ops = [_hbm(a) for a in list(srcs) + lands + [carry]]
    out = pl.pallas_call(
        body, name=name,
        out_shape=(pltpu.SemaphoreType.DMA((3 * n,)), pltpu.SemaphoreType.DMA((3 * n,)),
                   *[pltpu.HBM(a.shape, a.dtype) for a in ops]),
        in_specs=[HBM] * len(ops), out_specs=(SEM, SEM, *[HBM] * len(ops)),
        input_output_aliases={i: 2 + i for i in range(len(ops))},
        compiler_params=pltpu.CompilerParams(has_side_effects=EFFECT))(*ops)
    return dict(gather=gather, send=out[0], recv=out[1], srcs=list(out[2:2 + n]), lands=list(out[2 + n:2 + 2 * n])), out[-1]


def _exchange_wait(name, ex, after):
    n = len(ex["srcs"])
    gather = ex["gather"]

    def body(*refs):
        _exchange(gather, refs[:n], refs[n:2 * n], refs[2 * n], refs[2 * n + 1], n, _finish)

    ops = ex["srcs"] + ex["lands"]
    out = pl.pallas_call(
        body, name=name, out_shape=[pltpu.HBM(a.shape, a.dtype) for a in ops],
        in_specs=[HBM] * len(ops) + [SEM, SEM, ANY], out_specs=[HBM] * len(ops),
        input_output_aliases={i: i for i in range(len(ops))},
        compiler_params=pltpu.CompilerParams(has_side_effects=EFFECT))(*ops, ex["send"], ex["recv"], after)
    return list(out[:n]), list(out[n:])


def _swap_sibling(parts):
    n = len(parts)

    def body(*refs):
        src, dst = refs[:n], refs[n:2 * n]
        send, recv = refs[2 * n:]
        x, y, c = _place()
        cps = [pltpu.make_async_remote_copy(src_ref=src[a], dst_ref=dst[a], send_sem=send.at[a], recv_sem=recv.at[a],
                                            device_id=(x, y, 1 - c), device_id_type=MESH) for a in range(n)]
        for cp in cps:
            cp.start()
        for cp in cps:
            cp.wait_recv()
        for cp in cps:
            cp.wait_send()

    return pl.pallas_call(
        body, out_shape=[jax.ShapeDtypeStruct(p.shape, p.dtype) for p in parts],
        in_specs=[ANY] * n, out_specs=[ANY] * n,
        scratch_shapes=[pltpu.SemaphoreType.DMA((n,)), pltpu.SemaphoreType.DMA((n,))],
        name="swap_sibling")(*parts)


def _allreduce_small(name, v):
    R = v.shape[0]

    def body(v_ref, o_ref, buf, send, recv):
        x, y, c = _place()
        me = 4 * x + 2 * y + c
        buf[me] = v_ref[...]
        cps = []
        for k in range(1, 8):
            fx, fy, fc = (k >> 2) & 1, (k >> 1) & 1, k & 1
            px = 1 - x if fx else x
            py = 1 - y if fy else y
            pc = 1 - c if fc else c
            cp = pltpu.make_async_remote_copy(src_ref=v_ref, dst_ref=buf.at[me], send_sem=send.at[k - 1],
                                              recv_sem=recv.at[k - 1], device_id=(px, py, pc), device_id_type=MESH)
            cp.start()
            cps.append((cp, 4 * px + 2 * py + pc))
        for k, (cp, peer) in enumerate(cps):
            pltpu.make_async_remote_copy(src_ref=v_ref, dst_ref=buf.at[peer], send_sem=send.at[k], recv_sem=recv.at[k],
                                         device_id=(x, y, c), device_id_type=MESH).wait_recv()
        for cp, _ in cps:
            cp.wait_send()
        acc = buf[0]
        for d in range(1, 8):
            acc = acc + buf[d]
        o_ref[...] = acc

    return pl.pallas_call(
        body, out_shape=jax.ShapeDtypeStruct((R, 128), F32),
        in_specs=[pl.BlockSpec(memory_space=pltpu.VMEM)], out_specs=pl.BlockSpec(memory_space=pltpu.VMEM),
        scratch_shapes=[pltpu.VMEM((8, R, 128), F32), pltpu.SemaphoreType.DMA((7,)), pltpu.SemaphoreType.DMA((7,))],
        name=name)(v)


def _row_tile(r):
    for t in (256, 128, 64, 32, 16, 8):
        if r % t == 0:
            return t
    raise ValueError(r)


def _sum4(name, own, got):
    R, C = own.shape
    tr = _row_tile(R)

    def body(o_ref, g_ref, s_ref):
        s = o_ref[...].astype(F32)
        for k in range(3):
            s = s + g_ref[k].astype(F32)
        s_ref[...] = s

    return pl.pallas_call(
        body, out_shape=jax.ShapeDtypeStruct((R, C), F32), grid=(R // tr,),
        in_specs=[pl.BlockSpec((tr, C), lambda i: (i, 0)), pl.BlockSpec((3, tr, C), lambda i: (0, i, 0))],
        out_specs=pl.BlockSpec((tr, C), lambda i: (i, 0)), name=name, compiler_params=_cp("parallel"))(own, got)


def _adamw(name, w, gparts, m, v):
    R, C = w.shape
    tr = _row_tile(R)
    ng = len(gparts)
    c1 = 1.0 - ADAM_B1 ** ADAM_STEP
    c2 = 1.0 - ADAM_B2 ** ADAM_STEP

    def body(*refs):
        w_ref = refs[0]
        g_refs = refs[1:1 + ng]
        m_ref, v_ref, go_ref, d_ref, mo_ref, vo_ref = refs[1 + ng:]
        g = g_refs[0][...]
        for r in g_refs[1:]:
            g = g + r[...]
        mn = ADAM_B1 * m_ref[...] + (1.0 - ADAM_B1) * g
        vn = ADAM_B2 * v_ref[...] + (1.0 - ADAM_B2) * (g * g)
        go_ref[...] = g
        mo_ref[...] = mn
        vo_ref[...] = vn
        d_ref[...] = -ADAM_LR * ((mn / c1) / (jnp.sqrt(vn / c2) + ADAM_EPS) + ADAM_WD * w_ref[...])

    blk = pl.BlockSpec((tr, C), lambda i: (i, 0))
    osh = jax.ShapeDtypeStruct((R, C), F32)
    return pl.pallas_call(
        body, out_shape=(osh, osh, osh, osh), grid=(R // tr,), in_specs=[blk] * (3 + ng), out_specs=(blk,) * 4,
        name=name, compiler_params=_cp("parallel"))(w, *gparts, m, v)


def _adamw_layers(name, w, sums, m, v):
    R2, C = w.shape
    R = R2 // DEPTH
    tr = _row_tile(R)
    nr = R // tr
    c1 = 1.0 - ADAM_B1 ** ADAM_STEP
    c2 = 1.0 - ADAM_B2 ** ADAM_STEP

    def body(w_ref, a0, b0, a1, b1, m_ref, v_ref, go_ref, d_ref, mo_ref, vo_ref):
        g = jnp.where(pl.program_id(0) == 0, a0[...] + b0[...], a1[...] + b1[...])
        mn = ADAM_B1 * m_ref[...] + (1.0 - ADAM_B1) * g
        vn = ADAM_B2 * v_ref[...] + (1.0 - ADAM_B2) * (g * g)
        go_ref[...] = g
        mo_ref[...] = mn
        vo_ref[...] = vn
        d_ref[...] = -ADAM_LR * ((mn / c1) / (jnp.sqrt(vn / c2) + ADAM_EPS) + ADAM_WD * w_ref[...])

    blk = pl.BlockSpec((tr, C), lambda l, i: (l * nr + i, 0))
    lay0 = pl.BlockSpec((tr, C), lambda l, i: (jnp.where(l == 0, i, nr - 1), 0))
    lay1 = pl.BlockSpec((tr, C), lambda l, i: (jnp.where(l == 1, i, 0), 0))
    osh = jax.ShapeDtypeStruct((R2, C), F32)
    return pl.pallas_call(
        body, out_shape=(osh, osh, osh, osh), grid=(DEPTH, nr),
        in_specs=[blk, lay0, lay0, lay1, lay1, blk, blk], out_specs=(blk,) * 4,
        name=name, compiler_params=_cp("arbitrary", "arbitrary"))(w, *sums[0], *sums[1], m, v)


BIG = [("ffn1_w_gate", "g1"), ("ffn1_w_up", "u1"), ("ffn1_w_down", "d1"), ("w_in", "win"), ("w_out", "wout"),
       ("ffn2_w_gate", "g2"), ("ffn2_w_up", "u2"), ("ffn2_w_down", "d2")]
SMALL = ["ffn1_norm", "mix_norm", "conv_b", "dt_bias", "a_log", "d_skip", "ssd_norm", "q_norm", "k_norm", "ffn2_norm"]
WEIGHTS = ["ffn1_norm", "ffn1_w_gate", "ffn1_w_up", "ffn1_w_down", "mix_norm", "w_in", "conv_w", "conv_b", "dt_bias",
           "a_log", "d_skip", "ssd_norm", "q_norm", "k_norm", "w_out", "ffn2_norm", "ffn2_w_gate", "ffn2_w_up",
           "ffn2_w_down"]
CONV_SH = CONV_DIM // N_SHARD
GATHER_GROUPS = [(0, "ffn1", ["g1", "u1", "d1"]), (0, "win", ["win", "cw"]), (0, "rest", ["wout", "g2", "u2", "d2"]),
                 (1, "all", ["g1", "u1", "d1", "win", "cw", "wout", "g2", "u2", "d2"])]


def _pad128(v):
    v = v.reshape(-1)
    return jnp.pad(v, (0, (-v.shape[0]) % 128))


def _pack(pieces):
    flat, offs, pos = [], [], 0
    for p in pieces:
        q = _pad128(p.astype(F32))
        offs.append(pos)
        pos += q.shape[0] // 128
        flat.append(q)
    total = -(-pos // 8) * 8
    out = jnp.concatenate(flat + [jnp.zeros(((total - pos) * 128,), F32)]).reshape(total, 128)
    return out, offs


def _unpack(packed, offs, shapes):
    out = []
    for off, shp in zip(offs, shapes):
        n = int(np.prod(shp))
        rows = -(-n // 128)
        out.append(packed[off:off + rows].reshape(-1)[:n].reshape(shp))
    return out


def kernel(x, ffn1_norm, ffn1_w_gate, ffn1_w_up, ffn1_w_down, mix_norm, w_in, conv_w, conv_b, dt_bias, a_log, d_skip, ssd_norm, q_norm, k_norm, w_out, ffn2_norm, ffn2_w_gate, ffn2_w_up, ffn2_w_down, loss_target, m_ffn1_norm, m_ffn1_w_gate, m_ffn1_w_up, m_ffn1_w_down, m_mix_norm, m_w_in, m_conv_w, m_conv_b, m_dt_bias, m_a_log, m_d_skip, m_ssd_norm, m_q_norm, m_k_norm, m_w_out, m_ffn2_norm, m_ffn2_w_gate, m_ffn2_w_up, m_ffn2_w_down, v_ffn1_norm, v_ffn1_w_gate, v_ffn1_w_up, v_ffn1_w_down, v_mix_norm, v_w_in, v_conv_w, v_conv_b, v_dt_bias, v_a_log, v_d_skip, v_ssd_norm, v_q_norm, v_k_norm, v_w_out, v_ffn2_norm, v_ffn2_w_gate, v_ffn2_w_up, v_ffn2_w_down):
    A = dict(locals())
    ix, iy, ic = _place()
    me = 2 * ix + iy
    B, S, _ = x.shape
    T = B * S

    own = {key: A[name].astype(BF16) for name, key in BIG}
    own["cw"] = conv_w
    exs, first_norm = [], ffn1_norm
    for gi, (l, _, keys) in enumerate(GATHER_GROUPS):
        ex, first_norm = _exchange_start("gather_start%d" % gi, True, [own[key][l] for key in keys], first_norm)
        exs.append(ex)
    landed = {}

    def weights(l, group, after):
        gi = [i for i, (gl, gname, _) in enumerate(GATHER_GROUPS) if gl == l and gname in (group, "all")][0]
        if gi not in landed:
            srcs, lands = _exchange_wait("gather_wait%d" % gi, exs[gi], after)
            landed[gi] = {}
            for key, mine, land in zip(GATHER_GROUPS[gi][2], srcs, lands):
                full = lax.dynamic_update_slice(land, mine[None], (me, 0, 0))
                if key == "win":
                    full = _win_pack(jnp.concatenate([full[j] for j in range(N_SHARD)], axis=1))
                if key == "cw":
                    full = jnp.transpose(full, (1, 0, 2)).reshape(CONV_K, CONV_DIM)
                landed[gi][key] = full
        return landed[gi]

    pending = []

    def scatter(l, group, grads, carry):
        keys = sorted(grads)
        arrs = [grads[key] for key in keys]
        if "win" in grads:
            arrs[keys.index("win")] = jnp.transpose(_win_unpack(grads["win"]).reshape(D_MODEL, N_SHARD, IN_SH), (1, 0, 2))
        ex, carry = _exchange_start("scatter_start_l%d_%s" % (l, group), False, arrs, carry)
        pending.append((l, keys, ex))
        return carry

    small = {name: A[name] for name in SMALL}
    small["ffn1_norm"] = first_norm
    lsum, dx, sgrads = _local_step(x.reshape(T, D_MODEL), loss_target.reshape(T, D_MODEL), small, weights, scatter, B)

    names = SMALL + ["conv_w"]
    shapes = [A[n].shape for n in SMALL] + [(DEPTH, CONV_K, CONV_DIM), ()]
    pieces = [jnp.stack([sgrads[l][n].reshape(shp[1:]) for l in range(DEPTH)]) for n, shp in zip(names, shapes)]
    pieces.append(0.5 / D_MODEL * jnp.sum(lsum))
    packed, offs = _pack(pieces)
    red = _allreduce_small("allreduce_small", packed)
    red = _unpack(red, offs, shapes)
    loss = red[-1]
    sg = dict(zip(names, red[:-1]))

    sums, after = {}, dx
    for idx, (l, keys, ex) in enumerate(pending):
        srcs, lands = _exchange_wait("scatter_wait%d" % idx, ex, after)
        for key, g, got in zip(keys, srcs, lands):
            mine = lax.dynamic_index_in_dim(g, me, axis=0, keepdims=False)
            sums[key, l] = after = _sum4("sum_%s_l%d" % (key, l), mine, got)
    order = [(key, l) for _, key in BIG for l in range(DEPTH)]
    theirs = dict(zip(order, _swap_sibling([sums[k] for k in order])))

    out = {}
    for name, key in BIG:
        shp = A[name].shape
        flat = lambda a: a.reshape(shp[0] * shp[1], shp[2])
        res = _adamw_layers("adamw_" + key, flat(A[name]), [(sums[key, l], theirs[key, l]) for l in range(DEPTH)],
                            flat(A["m_" + name]), flat(A["v_" + name]))
        out[name] = [r.reshape(shp) for r in res]

    wp, offs = _pack([A[n] for n in SMALL])
    gp, _ = _pack([sg[n] for n in SMALL])
    mp, _ = _pack([A["m_" + n] for n in SMALL])
    vp, _ = _pack([A["v_" + n] for n in SMALL])
    res = _adamw("adamw_small", wp, [gp], mp, vp)
    shapes = [A[n].shape for n in SMALL]
    res = [_unpack(r, offs, shapes) for r in res]
    for i, n in enumerate(SMALL):
        out[n] = [res[q][i] for q in range(4)]
    gcw = lax.dynamic_slice_in_dim(sg["conv_w"], me * CONV_SH, CONV_SH, axis=2)
    flat = lambda a: a.reshape(DEPTH * CONV_K, CONV_SH)
    res = _adamw("adamw_conv_w", flat(conv_w), [flat(gcw)], flat(m_conv_w), flat(v_conv_w))
    out["conv_w"] = [r.reshape(conv_w.shape) for r in res]

    outs = [loss, dx.reshape(B, S, D_MODEL)]
    for q in range(4):
        outs += [out[n][q] for n in WEIGHTS]
    return tuple(outs)
```

```python
import functools
import math

import numpy as np
import jax
import jax.numpy as jnp
from jax import lax
from jax.experimental import pallas as pl
from jax.experimental.pallas import tpu as pltpu

F32 = jnp.float32
BF16 = jnp.bfloat16

D_MODEL = 1024
DEPTH = 2
N_SHARD = 4
D_FF = 2816
FF_SH = D_FF // N_SHARD
SSD_HEADS = 16
HEAD_DIM = 64
SSD_GROUPS = 4
GROUP_W = 256
SSD_STATE = 128
CONV_K = 4
CONV_DIM = 2048
ATT_HEADS = 16
MIX_W = 2048
MIX_SH = MIX_W // N_SHARD
IN_PROJ = 6160
IN_SH = IN_PROJ // N_SHARD
IN_PAD = 6272
PROJ_TN = 896
COL_Z, COL_XBC, COL_Q, COL_K, COL_V, COL_DT = 0, 1024, 3072, 4096, 5120, 6144
EPS = 1e-6
NEG = -1e30
SSD_L = 256
ATT_B = 256
ROW_T = 512
HALF_T = ROW_T // 2
TK_W = 2048
CONV_CT = 256
CONV_R = 256
PAD_R = 8

ADAM_LR, ADAM_B1, ADAM_B2, ADAM_EPS, ADAM_WD, ADAM_STEP = 0.001, 0.9, 0.999, 1e-08, 0.01, 10

NN = (((1,), (0,)), ((), ()))
NT = (((1,), (1,)), ((), ()))
TN = (((0,), (0,)), ((), ()))

VMEM_LIMIT = 56 * 1024 * 1024


def _cp(*sem):
    return pltpu.CompilerParams(dimension_semantics=sem, vmem_limit_bytes=VMEM_LIMIT)


def _dot(a, b, dims):
    return lax.dot_general(a, b, dims, preferred_element_type=F32)


def _sigmoid(x):
    return 0.5 * jnp.tanh(0.5 * x) + 0.5


def _softplus(x):
    return jnp.maximum(x, 0.0) + jnp.log(1.0 + jnp.exp(-jnp.abs(x)))


def _mm(name, pairs, out_shape, out_spec, grid, dims, acc_shape, res=None, scale=1.0):
    nk = grid[2]
    npair = len(pairs)

    def body(*refs):
        ab = refs[:2 * npair]
        pos = 2 * npair
        res_ref = None
        if res is not None:
            res_ref = refs[pos]
            pos += 1
        out_ref = refs[pos]
        s = None
        for p in range(npair):
            d = _dot(ab[2 * p][...].astype(BF16), ab[2 * p + 1][...].astype(BF16), dims)
            s = d if s is None else s + d

        def finish(r):
            if scale != 1.0:
                r = r * scale
            if res_ref is not None:
                r = r + res_ref[...]
            out_ref[...] = r.astype(out_ref.dtype)

        if nk == 1:
            finish(s)
            return
        acc = refs[pos + 1]
        k = pl.program_id(2)

        @pl.when(k == 0)
        def _():
            acc[...] = s

        @pl.when(k > 0)
        def _():
            acc[...] += s

        @pl.when(k == nk - 1)
        def _():
            finish(acc[...])

    args, specs = [], []
    for a, a_spec, b, b_spec in pairs:
        args += [a, b]
        specs += [a_spec, b_spec]
    if res is not None:
        args.append(res[0])
        specs.append(res[1])
    return pl.pallas_call(
        body, out_shape=out_shape, grid=grid, in_specs=specs, out_specs=out_spec,
        scratch_shapes=[] if nk == 1 else [pltpu.VMEM(acc_shape, F32)], name=name,
        compiler_params=_cp("parallel", "parallel", "arbitrary"))(*args)


def _rms_fwd(name, x, w):
    T = x.shape[0]

    def body(x_ref, w_ref, o_ref):
        xv = x_ref[...]
        r = lax.rsqrt(jnp.mean(xv * xv, axis=-1, keepdims=True) + EPS)
        o_ref[...] = (xv * r * w_ref[...]).astype(BF16)

    return pl.pallas_call(
        body, out_shape=jax.ShapeDtypeStruct((T, D_MODEL), BF16), grid=(T // ROW_T,),
        in_specs=[pl.BlockSpec((ROW_T, D_MODEL), lambda i: (i, 0)), pl.BlockSpec((1, D_MODEL), lambda i: (0, 0))],
        out_specs=pl.BlockSpec((ROW_T, D_MODEL), lambda i: (i, 0)), name=name, compiler_params=_cp("parallel"))(x, w)


def _rms_bwd(name, dh, x, w, dres):
    T = x.shape[0]

    def body(dh_ref, x_ref, w_ref, dres_ref, dx_ref, dw_ref):
        @pl.when(pl.program_id(0) == 0)
        def _():
            dw_ref[...] = jnp.zeros_like(dw_ref)

        xv = x_ref[...]
        r = lax.rsqrt(jnp.mean(xv * xv, axis=-1, keepdims=True) + EPS)
        xhat = xv * r
        dhv = dh_ref[...]
        dxhat = dhv * w_ref[...]
        m = jnp.mean(dxhat * xhat, axis=-1, keepdims=True)
        dx_ref[...] = dres_ref[...] + r * (dxhat - xhat * m)
        dw_ref[...] += jnp.sum(dhv * xhat, axis=0, keepdims=True)

    row = pl.BlockSpec((ROW_T, D_MODEL), lambda i: (i, 0))
    vec = pl.BlockSpec((1, D_MODEL), lambda i: (0, 0))
    return pl.pallas_call(
        body, out_shape=(jax.ShapeDtypeStruct((T, D_MODEL), F32), jax.ShapeDtypeStruct((1, D_MODEL), F32)),
        grid=(T // ROW_T,), in_specs=[row, row, vec, row], out_specs=(row, vec), name=name,
        compiler_params=_cp("arbitrary"))(dh, x, w, dres)


def _loss_grad(name, y, t):
    T = y.shape[0]

    def body(y_ref, t_ref, dy_ref, l_ref):
        @pl.when(pl.program_id(0) == 0)
        def _():
            l_ref[...] = jnp.zeros_like(l_ref)

        e = y_ref[...] - t_ref[...]
        dy_ref[...] = e * (1.0 / D_MODEL)
        l_ref[...] += jnp.sum(e * e, axis=0, keepdims=True)

    row = pl.BlockSpec((ROW_T, D_MODEL), lambda i: (i, 0))
    vec = pl.BlockSpec((1, D_MODEL), lambda i: (0, 0))
    return pl.pallas_call(
        body, out_shape=(jax.ShapeDtypeStruct((T, D_MODEL), F32), jax.ShapeDtypeStruct((1, D_MODEL), F32)),
        grid=(T // ROW_T,), in_specs=[row, row], out_specs=(row, vec), name=name,
        compiler_params=_cp("arbitrary"))(y, t)


def _ffn_gate_up(name, h, wg, wu):
    T = h.shape[0]

    def body(h_ref, wg_ref, wu_ref, g_ref, u_ref, a_ref):
        for r in range(0, ROW_T, HALF_T):
            rows = slice(r, r + HALF_T)
            hv = h_ref[rows, :]
            g = _dot(hv, wg_ref[...], NN)
            u = _dot(hv, wu_ref[...], NN)
            g_ref[rows, :] = g.astype(BF16)
            u_ref[rows, :] = u.astype(BF16)
            a_ref[rows, :] = (g * _sigmoid(g) * u).astype(BF16)

    wspec = pl.BlockSpec((None, D_MODEL, FF_SH), lambda j, i: (j, 0, 0))
    ospec = pl.BlockSpec((None, ROW_T, FF_SH), lambda j, i: (j, i, 0))
    osh = jax.ShapeDtypeStruct((N_SHARD, T, FF_SH), BF16)
    return pl.pallas_call(
        body, out_shape=(osh, osh, osh), grid=(N_SHARD, T // ROW_T),
        in_specs=[pl.BlockSpec((ROW_T, D_MODEL), lambda j, i: (i, 0)), wspec, wspec],
        out_specs=(ospec, ospec, ospec), name=name, compiler_params=_cp("parallel", "parallel"))(h, wg, wu)


def _ffn_dact(name, dx, wd, g, u):
    T = dx.shape[0]

    def body(dx_ref, wd_ref, g_ref, u_ref, dg_ref, du_ref):
        for r in range(0, ROW_T, HALF_T):
            rows = slice(r, r + HALF_T)
            da = 0.5 * _dot(dx_ref[rows, :].astype(BF16), wd_ref[...], NT)
            gv = g_ref[rows, :].astype(F32)
            uv = u_ref[rows, :].astype(F32)
            sg = _sigmoid(gv)
            dg_ref[rows, :] = (da * uv * (sg * (1.0 + gv * (1.0 - sg)))).astype(BF16)
            du_ref[rows, :] = (da * gv * sg).astype(BF16)

    aspec = pl.BlockSpec((None, ROW_T, FF_SH), lambda j, i: (j, i, 0))
    osh = jax.ShapeDtypeStruct((N_SHARD, T, FF_SH), BF16)
    return pl.pallas_call(
        body, out_shape=(osh, osh), grid=(N_SHARD, T // ROW_T),
        in_specs=[pl.BlockSpec((ROW_T, D_MODEL), lambda j, i: (i, 0)),
                  pl.BlockSpec((None, FF_SH, D_MODEL), lambda j, i: (j, 0, 0)), aspec, aspec],
        out_specs=(aspec, aspec), name=name, compiler_params=_cp("parallel", "parallel"))(dx, wd, g, u)


def _ffn_fwd(tag, x, nw, wg, wu, wd):
    T = x.shape[0]
    h = _rms_fwd(tag + "_rms", x, nw)
    g, u, a = _ffn_gate_up(tag + "_gu", h, wg, wu)
    nt = T // ROW_T
    xo = _mm(tag + "_down",
             [(a, pl.BlockSpec((None, ROW_T, FF_SH), lambda i, n, k, j=j: (j, i, 0)),
               wd, pl.BlockSpec((None, FF_SH, D_MODEL), lambda i, n, k, j=j: (j, 0, 0))) for j in range(N_SHARD)],
             jax.ShapeDtypeStruct((T, D_MODEL), F32), pl.BlockSpec((ROW_T, D_MODEL), lambda i, n, k: (i, 0)),
             (nt, 1, 1), NN, (ROW_T, D_MODEL),
             res=(x, pl.BlockSpec((ROW_T, D_MODEL), lambda i, n, k: (i, 0))), scale=0.5)
    return xo, (x, h, g, u, a)


def _ffn_bwd(tag, dxo, saved, nw, wg, wu, wd, emit):
    x, h, g, u, a = saved
    T = x.shape[0]
    nt = T // ROW_T
    tkw = min(TK_W, T)
    nw_t = T // tkw
    dg, du = _ffn_dact(tag + "_dact", dxo, wd, g, u)
    actw = lambda f: pl.BlockSpec((None, tkw, FF_SH), f)
    gd = _mm(tag + "_dwd",
             [(a, actw(lambda m, n, k: (m, k, 0)), dxo, pl.BlockSpec((tkw, D_MODEL), lambda m, n, k: (k, 0)))],
             jax.ShapeDtypeStruct((N_SHARD, FF_SH, D_MODEL), BF16),
             pl.BlockSpec((None, FF_SH, D_MODEL), lambda m, n, k: (m, 0, 0)),
             (N_SHARD, 1, nw_t), TN, (FF_SH, D_MODEL), scale=0.5)
    hspec = pl.BlockSpec((tkw, D_MODEL), lambda j, n, k: (k, 0))
    gsh = jax.ShapeDtypeStruct((N_SHARD, D_MODEL, FF_SH), BF16)
    gspec = pl.BlockSpec((None, D_MODEL, FF_SH), lambda j, n, k: (j, 0, 0))
    gg = _mm(tag + "_dwg", [(h, hspec, dg, actw(lambda j, n, k: (j, k, 0)))], gsh, gspec,
             (N_SHARD, 1, nw_t), TN, (D_MODEL, FF_SH))
    gu = _mm(tag + "_dwu", [(h, hspec, du, actw(lambda j, n, k: (j, k, 0)))], gsh, gspec,
             (N_SHARD, 1, nw_t), TN, (D_MODEL, FF_SH))
    dg = emit(gg, gu, gd, dg)
    act = lambda j: pl.BlockSpec((None, ROW_T, FF_SH), lambda i, n, k: (j, i, 0))
    wsp = lambda j: pl.BlockSpec((None, D_MODEL, FF_SH), lambda i, n, k: (j, 0, 0))
    dh = _mm(tag + "_dh",
             [(dd, act(j), w, wsp(j)) for j in range(N_SHARD) for dd, w in ((dg, wg), (du, wu))],
             jax.ShapeDtypeStruct((T, D_MODEL), F32), pl.BlockSpec((ROW_T, D_MODEL), lambda i, n, k: (i, 0)),
             (nt, 1, 1), NT, (ROW_T, D_MODEL))
    return _rms_bwd(tag + "_rmsb", dh, x, nw, dxo)


def _seq_rows(ref, start, size, S):
    lo, hi = max(start, 0), min(start + size, S)
    parts = [ref[pl.ds(lo, hi - lo), :]]
    if lo > start:
        parts.insert(0, jnp.zeros((lo - start, ref.shape[1]), F32))
    if start + size > hi:
        parts.append(jnp.zeros((start + size - hi, ref.shape[1]), F32))
    return parts[0] if len(parts) == 1 else jnp.concatenate(parts, axis=0)


XBC_CB = COL_XBC // CONV_CT


def _conv_fwd(name, proj, w, b, B):
    T = proj.shape[0]
    S = T // B
    C = CONV_DIM

    def body(x_ref, w_ref, b_ref, o_ref):
        wv = w_ref[...]
        for c in range(S // CONV_R):
            r0 = c * CONV_R
            ch = _seq_rows(x_ref, r0 - PAD_R, CONV_R + PAD_R, S)
            pre = ch[PAD_R:] * wv[3:4] + b_ref[...]
            for s in range(1, CONV_K):
                pre = pre + pltpu.roll(ch, s, axis=0)[PAD_R:] * wv[3 - s:4 - s]
            o_ref[pl.ds(r0, CONV_R), :] = pre * _sigmoid(pre)

    return pl.pallas_call(
        body, out_shape=jax.ShapeDtypeStruct((T, C), F32), grid=(B, C // CONV_CT),
        in_specs=[pl.BlockSpec((S, CONV_CT), lambda bi, ci: (bi, XBC_CB + ci)),
                  pl.BlockSpec((CONV_K, CONV_CT), lambda bi, ci: (0, ci)),
                  pl.BlockSpec((1, CONV_CT), lambda bi, ci: (0, ci))],
        out_specs=pl.BlockSpec((S, CONV_CT), lambda bi, ci: (bi, ci)), name=name,
        compiler_params=_cp("parallel", "parallel"))(proj, w, b)


def _conv_bwd(name, proj, dxs, dB, dC, w, b, dproj, B):
    T = proj.shape[0]
    S = T // B
    C = CONV_DIM
    RW = CONV_R + PAD_R
    nx, nb = dxs.shape[1] // CONV_CT, dB.shape[1] // CONV_CT

    def body(x_ref, dx_in, db_in, dc_in, w_ref, b_ref, buf_ref, dx_ref, dw_ref, db_ref):
        @pl.when(pl.program_id(1) == 0)
        def _():
            dw_ref[...] = jnp.zeros_like(dw_ref)
            db_ref[...] = jnp.zeros_like(db_ref)

        ci = pl.program_id(0)
        wv = w_ref[...]
        dw = [jnp.zeros((1, CONV_CT), F32) for _ in range(CONV_K)]
        db = jnp.zeros((1, CONV_CT), F32)
        for c in range(S // CONV_R):
            r0 = c * CONV_R
            ch = _seq_rows(x_ref, r0 - PAD_R, RW + PAD_R, S)
            xs = [ch[PAD_R:]] + [pltpu.roll(ch, s, axis=0)[PAD_R:] for s in range(1, CONV_K)]
            pre = b_ref[...] + xs[0] * wv[3:4]
            for s in range(1, CONV_K):
                pre = pre + xs[s] * wv[3 - s:4 - s]
            sg = _sigmoid(pre)
            dout = jnp.where(ci < nx, _seq_rows(dx_in, r0, RW, S),
                             jnp.where(ci < nx + nb, _seq_rows(db_in, r0, RW, S), _seq_rows(dc_in, r0, RW, S)))
            dpre = dout * (sg * (1.0 + pre * (1.0 - sg)))
            dx = dpre[:CONV_R] * wv[3:4]
            for s in range(1, CONV_K):
                dx = dx + pltpu.roll(dpre, RW - s, axis=0)[:CONV_R] * wv[3 - s:4 - s]
            dx_ref[pl.ds(r0, CONV_R), :] = dx.astype(BF16)
            dcur = dpre[:CONV_R]
            db = db + jnp.sum(dcur, axis=0, keepdims=True)
            for s in range(CONV_K):
                dw[3 - s] = dw[3 - s] + jnp.sum(dcur * xs[s][:CONV_R], axis=0, keepdims=True)
        db_ref[...] += db
        for k in range(CONV_K):
            dw_ref[k:k + 1, :] += dw[k]

    seq = lambda f: pl.BlockSpec((S, CONV_CT), f)
    return pl.pallas_call(
        body,
        out_shape=(jax.ShapeDtypeStruct(dproj.shape, dproj.dtype), jax.ShapeDtypeStruct((CONV_K, C), F32),
                   jax.ShapeDtypeStruct((1, C), F32)),
        grid=(C // CONV_CT, B),
        in_specs=[seq(lambda ci, bi: (bi, XBC_CB + ci)),
                  seq(lambda ci, bi: (bi, jnp.minimum(ci, nx - 1))),
                  seq(lambda ci, bi: (bi, jnp.clip(ci - nx, 0, nb - 1))),
                  seq(lambda ci, bi: (bi, jnp.clip(ci - nx - nb, 0, nb - 1))),
                  pl.BlockSpec((CONV_K, CONV_CT), lambda ci, bi: (0, ci)),
                  pl.BlockSpec((1, CONV_CT), lambda ci, bi: (0, ci)), ANY],
        out_specs=(seq(lambda ci, bi: (bi, XBC_CB + ci)),
                   pl.BlockSpec((CONV_K, CONV_CT), lambda ci, bi: (0, ci)),
                   pl.BlockSpec((1, CONV_CT), lambda ci, bi: (0, ci))),
        input_output_aliases={6: 0},
        name=name, compiler_params=_cp("parallel", "arbitrary"))(proj, dxs, dB, dC, w, b, dproj)


def _tri_sum(tri, x, dims, tri_first):
    hi = x.astype(BF16)
    r1 = x - hi.astype(F32)
    mid = r1.astype(BF16)
    lo = (r1 - mid.astype(F32)).astype(BF16)
    out = None
    for part in (hi, mid, lo):
        d = _dot(tri, part, dims) if tri_first else _dot(part, tri, dims)
        out = d if out is None else out + d
    return out


def _total(x):
    return jnp.sum(jnp.sum(x, axis=0, keepdims=True), axis=-1, keepdims=True)


def _ssd_common(dtc_ref, dtr_ref, pcol_ref, prow_ref, b_ref, c_ref):
    L = SSD_L
    bias_c, alog_c = pcol_ref[0:1, :], pcol_ref[1:2, :]
    a_c = -jnp.exp(alog_c)
    dt_c = _softplus(dtc_ref[...] + bias_c)
    row = lax.broadcasted_iota(jnp.int32, (L, L), 0)
    col = lax.broadcasted_iota(jnp.int32, (L, L), 1)
    causal = row >= col
    tri = causal.astype(BF16)
    cum_c = _tri_sum(tri, dt_c * a_c, NN, True)
    a_r = -jnp.exp(prow_ref[:, 1:2])
    dt_r = _softplus(dtr_ref[...] + prow_ref[:, 0:1])
    cum_r = _tri_sum(tri, dt_r * a_r, NT, False)
    bb = b_ref[...].astype(BF16)
    cb = c_ref[...].astype(BF16)
    G = _dot(cb, bb, NT)
    return a_c, dt_c, causal, tri, cum_c, cum_r, bb, cb, G


def _ssd_fwd(name, xc, proj, dtc, dtr, pcol, prow, nw, B):
    T = xc.shape[0]
    S = T // B
    nb = S // SSD_L
    L = SSD_L

    def body(xs_ref, b_ref, c_ref, z_ref, dtc_ref, dtr_ref, pcol_ref, prow_ref, nw_ref, y_ref, yn_ref, hs_ref, H, yo_s):
        @pl.when(pl.program_id(2) == 0)
        def _():
            H[...] = jnp.zeros_like(H)

        a_c, dt_c, causal, tri, cum_c, cum_r, bb, cb, G = _ssd_common(dtc_ref, dtr_ref, pcol_ref, prow_ref, b_ref, c_ref)
        dsk = pcol_ref[2:3, :]
        clast = cum_c[L - 1:L, :]
        bf = b_ref[...]
        for h in range(4):
            hs_ref[h] = H[h]
            yo_s[h] = _dot(cb, H[h].astype(BF16), NN)
        for h in range(4):
            sl = slice(HEAD_DIM * h, HEAD_DIM * (h + 1))
            cc = cum_c[:, h:h + 1]
            lm = jnp.exp(jnp.where(causal, cc - cum_r[h:h + 1, :], NEG))
            M = (G * lm).astype(BF16)
            xh = xs_ref[:, sl]
            Xb = (xh * dt_c[:, h:h + 1]).astype(BF16)
            Hh = H[h]
            y = _dot(M, Xb, NN) + jnp.exp(cc) * yo_s[h]
            y_ref[:, sl] = y + dsk[:, h:h + 1] * xh
            cl = clast[:, h:h + 1]
            Bw = (bf * jnp.exp(cl - cc)).astype(BF16)
            H[h] = jnp.exp(cl) * Hh + _dot(Bw, Xb, TN)
        zv = z_ref[...]
        y2 = y_ref[...] * (zv * _sigmoid(zv))
        r = lax.rsqrt(jnp.mean(y2 * y2, axis=-1, keepdims=True) + EPS)
        yn_ref[...] = (y2 * r * nw_ref[...]).astype(BF16)

    rowi = lambda b, g, i: b * nb + i
    grp = pl.BlockSpec((L, GROUP_W), lambda b, g, i: (rowi(b, g, i), g))
    return pl.pallas_call(
        body,
        out_shape=(jax.ShapeDtypeStruct((T, 1024), F32), jax.ShapeDtypeStruct((T, 1024), BF16),
                   jax.ShapeDtypeStruct((B, SSD_GROUPS, nb, 4, SSD_STATE, HEAD_DIM), F32)),
        grid=(B, SSD_GROUPS, nb),
        in_specs=[grp,
                  pl.BlockSpec((L, SSD_STATE), lambda b, g, i: (rowi(b, g, i), 8 + g)),
                  pl.BlockSpec((L, SSD_STATE), lambda b, g, i: (rowi(b, g, i), 12 + g)),
                  grp,
                  pl.BlockSpec((None, L, 4), lambda b, g, i: (g, rowi(b, g, i), 0)),
                  pl.BlockSpec((None, 4, L), lambda b, g, i: (g, 0, rowi(b, g, i))),
                  pl.BlockSpec((None, 3, 4), lambda b, g, i: (g, 0, 0)),
                  pl.BlockSpec((None, 4, 3), lambda b, g, i: (g, 0, 0)),
                  pl.BlockSpec((1, GROUP_W), lambda b, g, i: (0, g))],
        out_specs=(grp, grp,
                   pl.BlockSpec((None, None, None, 4, SSD_STATE, HEAD_DIM), lambda b, g, i: (b, g, i, 0, 0, 0))),
        scratch_shapes=[pltpu.VMEM((4, SSD_STATE, HEAD_DIM), F32), pltpu.VMEM((4, L, HEAD_DIM), F32)], name=name,
        compiler_params=_cp("parallel", "parallel", "arbitrary"))(xc, xc, xc, proj, dtc, dtr, pcol, prow, nw)


def _ssd_bwd(name, dyn, Y, xc, proj, dtc, dtr, pcol, prow, nw, hs, dproj, B):
    T = xc.shape[0]
    S = T // B
    nb = S // SSD_L
    L = SSD_L

    def body(dyn_ref, y_ref, xs_ref, b_ref, c_ref, z_ref, dtc_ref, dtr_ref, pcol_ref, prow_ref, nw_ref, hs_ref, buf_ref,
             dxs_ref, db_ref, dc_ref, dz_ref, ddt_ref, dpar_ref, dnw_ref, dH, dm_s, dxo_s, ea_s, ex_s):
        @pl.when(pl.program_id(2) == 0)
        def _():
            dH[...] = jnp.zeros_like(dH)
            dpar_ref[...] = jnp.zeros_like(dpar_ref)
            dnw_ref[...] = jnp.zeros_like(dnw_ref)

        a_c, dt_c, causal, tri, cum_c, cum_r, bb, cb, G = _ssd_common(dtc_ref, dtr_ref, pcol_ref, prow_ref, b_ref, c_ref)
        dsk = pcol_ref[2:3, :]
        clast = cum_c[L - 1:L, :]
        bf = b_ref[...]
        cf = c_ref[...]
        Yv = y_ref[...]
        zv = z_ref[...]
        sz = _sigmoid(zv)
        silu = zv * sz
        y2 = Yv * silu
        r = lax.rsqrt(jnp.mean(y2 * y2, axis=-1, keepdims=True) + EPS)
        yhat = y2 * r
        dyv = dyn_ref[...]
        dnw_ref[...] += jnp.sum(dyv * yhat, axis=0, keepdims=True)
        dyhat = dyv * nw_ref[...]
        dy2 = r * (dyhat - yhat * jnp.mean(dyhat * yhat, axis=-1, keepdims=True))
        dY = dy2 * silu
        dz_ref[...] = (dy2 * Yv * (sz * (1.0 + zv * (1.0 - sz)))).astype(BF16)

        lane4 = lax.broadcasted_iota(jnp.int32, (1, 4), 1)
        dG = jnp.zeros((L, L), F32)
        dBs = jnp.zeros((L, SSD_STATE), F32)
        dCs = jnp.zeros((L, SSD_STATE), F32)
        ddsk = jnp.zeros((1, 4), F32)
        dcl = jnp.zeros((1, 4), F32)
        for h in range(4):
            sl = slice(HEAD_DIM * h, HEAD_DIM * (h + 1))
            xb = (xs_ref[:, sl] * dt_c[:, h:h + 1]).astype(BF16)
            dm_s[h] = _dot(dY[:, sl].astype(BF16), xb, NT)
            dxo_s[h] = _dot(bb, dH[h].astype(BF16), NN)
        for h in range(4):
            sl = slice(HEAD_DIM * h, HEAD_DIM * (h + 1))
            onehot = (lane4 == h).astype(F32)
            cc = cum_c[:, h:h + 1]
            cl = clast[:, h:h + 1]
            lm = jnp.exp(jnp.where(causal, cc - cum_r[h:h + 1, :], NEG))
            M = (G * lm).astype(BF16)
            xh = xs_ref[:, sl]
            dth = dt_c[:, h:h + 1]
            X = xh * dth
            Xb = X.astype(BF16)
            dYh = dY[:, sl]
            dYb = dYh.astype(BF16)
            Hb = hs_ref[h].astype(BF16)
            dHh = dH[h]
            dHb = dHh.astype(BF16)
            alpha = jnp.exp(cc)
            beta = jnp.exp(cl - cc)
            dXoff = beta * dxo_s[h]
            dX = _dot(M, dYb, TN) + dXoff
            dG = dG + dm_s[h] * lm
            dCs = dCs + _dot((alpha * dYh).astype(BF16), Hb, NT)
            dBs = dBs + _dot((beta * X).astype(BF16), dHb, NT)
            ypre = Yv[:, sl] - dsk[:, h:h + 1] * xh
            ea_s[:, sl] = dYb.astype(F32) * ypre - Xb.astype(F32) * dX
            ex_s[:, sl] = dX * xh
            dcl_h = (_total(dHh * (jnp.exp(cl) * hs_ref[h])) + _total(Xb.astype(F32) * dXoff))
            dcl = dcl + dcl_h * onehot
            ddsk = ddsk + _total(dYh * xh) * onehot
            dxs_ref[:, sl] = dsk[:, h:h + 1] * dYh + dX * dth
            dH[h] = jnp.exp(cl) * dHh + _dot((alpha * cf).astype(BF16), dYb, TN)
        dGb = dG.astype(BF16)
        dc_ref[...] = _dot(dGb, bb, NN) + dCs
        db_ref[...] = _dot(dGb, cb, TN) + dBs
        feat = lax.broadcasted_iota(jnp.int32, (GROUP_W, 4), 0)
        head = lax.broadcasted_iota(jnp.int32, (GROUP_W, 4), 1) * HEAD_DIM
        sel = ((feat >= head) & (feat < head + HEAD_DIM)).astype(BF16)
        dA = _tri_sum(sel, ea_s[...], NN, False)
        ddtx = _tri_sum(sel, ex_s[...], NN, False)
        last = lax.broadcasted_iota(jnp.int32, (L, 1), 0) == L - 1
        dA = dA + jnp.where(last, dcl, 0.0)
        dadt = _tri_sum(tri, dA, TN, True)
        ddt = dadt * a_c + ddtx
        d_a = jnp.sum(dadt * dt_c, axis=0, keepdims=True)
        ddraw = ddt * _sigmoid(dtc_ref[...] + pcol_ref[0:1, :])
        ddt_ref[...] = ddraw
        dpar_ref[0:1, :] += jnp.sum(ddraw, axis=0, keepdims=True)
        dpar_ref[1:2, :] += d_a * a_c
        dpar_ref[2:3, :] += ddsk

    rowi = lambda b, g, i: b * nb + (nb - 1 - i)
    grp = pl.BlockSpec((L, GROUP_W), lambda b, g, i: (rowi(b, g, i), g))
    st = pl.BlockSpec((L, SSD_STATE), lambda b, g, i: (rowi(b, g, i), g))
    f = jax.ShapeDtypeStruct
    return pl.pallas_call(
        body,
        out_shape=(f((T, 1024), F32), f((T, 512), F32), f((T, 512), F32), f(dproj.shape, dproj.dtype),
                   f((SSD_GROUPS, T, 4), F32), f((B, SSD_GROUPS, 3, 4), F32), f((B, 1, 1024), F32)),
        grid=(B, SSD_GROUPS, nb),
        in_specs=[grp, grp, grp,
                  pl.BlockSpec((L, SSD_STATE), lambda b, g, i: (rowi(b, g, i), 8 + g)),
                  pl.BlockSpec((L, SSD_STATE), lambda b, g, i: (rowi(b, g, i), 12 + g)),
                  grp,
                  pl.BlockSpec((None, L, 4), lambda b, g, i: (g, rowi(b, g, i), 0)),
                  pl.BlockSpec((None, 4, L), lambda b, g, i: (g, 0, rowi(b, g, i))),
                  pl.BlockSpec((None, 3, 4), lambda b, g, i: (g, 0, 0)),
                  pl.BlockSpec((None, 4, 3), lambda b, g, i: (g, 0, 0)),
                  pl.BlockSpec((1, GROUP_W), lambda b, g, i: (0, g)),
                  pl.BlockSpec((None, None, None, 4, SSD_STATE, HEAD_DIM), lambda b, g, i: (b, g, nb - 1 - i, 0, 0, 0)),
                  ANY],
        out_specs=(grp, st, st, grp,
                   pl.BlockSpec((None, L, 4), lambda b, g, i: (g, rowi(b, g, i), 0)),
                   pl.BlockSpec((None, None, 3, 4), lambda b, g, i: (b, g, 0, 0)),
                   pl.BlockSpec((None, 1, GROUP_W), lambda b, g, i: (b, 0, g))),
        input_output_aliases={12: 3},
        scratch_shapes=[pltpu.VMEM((4, SSD_STATE, HEAD_DIM), F32), pltpu.VMEM((4, L, L), F32),
                        pltpu.VMEM((4, L, HEAD_DIM), F32), pltpu.VMEM((L, GROUP_W), F32),
                        pltpu.VMEM((L, GROUP_W), F32)], name=name,
        compiler_params=_cp("parallel", "parallel", "arbitrary"))(
            dyn, Y, xc, xc, xc, proj, dtc, dtr, pcol, prow, nw, hs, dproj)


def _headnorm_fwd(name, proj, col_block, w):
    T = proj.shape[0]

    def body(x_ref, w_ref, o_ref):
        for h in range(ATT_HEADS):
            sl = slice(HEAD_DIM * h, HEAD_DIM * (h + 1))
            xh = x_ref[:, sl]
            r = lax.rsqrt(jnp.mean(xh * xh, axis=-1, keepdims=True) + EPS)
            o_ref[:, sl] = (xh * r * w_ref[...]).astype(BF16)

    return pl.pallas_call(
        body, out_shape=jax.ShapeDtypeStruct((T, 1024), BF16), grid=(T // ROW_T,),
        in_specs=[pl.BlockSpec((ROW_T, 1024), lambda i: (i, col_block)), pl.BlockSpec((1, HEAD_DIM), lambda i: (0, 0))],
        out_specs=pl.BlockSpec((ROW_T, 1024), lambda i: (i, 0)), name=name, compiler_params=_cp("parallel"))(proj, w)


def _headnorm_bwd(name, dn, proj, col_block, w, dproj):
    T = proj.shape[0]

    def body(dn_ref, x_ref, w_ref, buf_ref, dx_ref, dw_ref):
        @pl.when(pl.program_id(0) == 0)
        def _():
            dw_ref[...] = jnp.zeros_like(dw_ref)

        dw = jnp.zeros((1, HEAD_DIM), F32)
        for h in range(ATT_HEADS):
            sl = slice(HEAD_DIM * h, HEAD_DIM * (h + 1))
            xh = x_ref[:, sl]
            r = lax.rsqrt(jnp.mean(xh * xh, axis=-1, keepdims=True) + EPS)
            xhat = xh * r
            dnh = dn_ref[:, sl]
            dxhat = dnh * w_ref[...]
            dx_ref[:, sl] = (r * (dxhat - xhat * jnp.mean(dxhat * xhat, axis=-1, keepdims=True))).astype(BF16)
            dw = dw + jnp.sum(dnh * xhat, axis=0, keepdims=True)
        dw_ref[...] += dw

    here = pl.BlockSpec((ROW_T, 1024), lambda i: (i, col_block))
    return pl.pallas_call(
        body, out_shape=(jax.ShapeDtypeStruct(dproj.shape, dproj.dtype), jax.ShapeDtypeStruct((1, HEAD_DIM), F32)),
        grid=(T // ROW_T,),
        in_specs=[pl.BlockSpec((ROW_T, 1024), lambda i: (i, 0)), here, pl.BlockSpec((1, HEAD_DIM), lambda i: (0, 0)), ANY],
        out_specs=(here, pl.BlockSpec((1, HEAD_DIM), lambda i: (0, 0))), input_output_aliases={3: 0},
        name=name, compiler_params=_cp("arbitrary"))(dn, proj, w, dproj)


def _att_bias(nq):
    j = np.arange(ATT_B)[:, None]
    i = np.arange(ATT_B)[None, :]
    out = np.empty((nq, ATT_B, ATT_B), np.float32)
    for dblk in range(nq):
        dl = ATT_B * dblk + i - j
        cnt = ((dl >= 0) & (dl <= 128)).astype(np.float32)
        cnt += ((dl >= 0) & (dl % 4 == 0) & (dl <= 512))
        cnt += ((dl >= 0) & (dl % 16 == 0) & (dl <= 2048))
        out[dblk] = np.where(cnt > 0, np.log(np.maximum(cnt, 1.0)), NEG)
    return jnp.asarray(out)


def _row_pair(nq):
    def f(r, c):
        first = c <= r
        return jnp.where(first, r, nq - 1 - r), jnp.where(first, c, c - (r + 1))
    return f


def _col_pair(nq):
    def f(r, c):
        first = c < nq - r
        kj = jnp.where(first, r, nq - 1 - r)
        return jnp.where(first, r + c, nq - 1 - r + (c - (nq - r))), kj
    return f


ATT_SCALE = 1.0 / math.sqrt(HEAD_DIM)
ATT_HS = 4
ATT_W = ATT_HS * HEAD_DIM


def _att_maps(nq, qk):
    return dict(
        q_tok=lambda b, g, r, c: (b * nq + qk(r, c)[0], g),
        k_tok=lambda b, g, r, c: (b * nq + qk(r, c)[1], g),
        q_feat=lambda b, g, r, c: (g, b * nq + qk(r, c)[0]),
        k_feat=lambda b, g, r, c: (g, b * nq + qk(r, c)[1]),
        bias=lambda b, g, r, c: (qk(r, c)[0] - qk(r, c)[1], 0, 0),
        lse=lambda b, g, r, c: (g, 0, b * nq + qk(r, c)[0]),
        do_tok=lambda b, g, r, c: (b * nq + qk(r, c)[0], ATT_HS + g))


def _att_fwd(name, kn, qT, vT, bias, B):
    T = kn.shape[0]
    nq = (T // B) // ATT_B
    qk = _row_pair(nq)
    mp = _att_maps(nq, qk)

    def body(k_ref, qT_ref, vT_ref, bias_ref, oT_ref, lse_ref, m_s, l_s, acc_s, s_s):
        qi, kj = qk(pl.program_id(2), pl.program_id(3))

        @pl.when(kj == 0)
        def _():
            m_s[...] = jnp.full_like(m_s, NEG)
            l_s[...] = jnp.zeros_like(l_s)
            acc_s[...] = jnp.zeros_like(acc_s)

        bv = bias_ref[...]
        for h in range(ATT_HS):
            rs = slice(HEAD_DIM * h, HEAD_DIM * (h + 1))
            s_s[h] = _dot(k_ref[:, rs], qT_ref[rs, :], NN)
        for h in range(ATT_HS):
            rs = slice(HEAD_DIM * h, HEAD_DIM * (h + 1))
            s = s_s[h] + bv
            m_prev = m_s[h:h + 1, :]
            m_new = jnp.maximum(m_prev, jnp.max(s, axis=0, keepdims=True))
            alpha = jnp.exp(m_prev - m_new)
            p = jnp.exp(s - m_new)
            l_s[h:h + 1, :] = alpha * l_s[h:h + 1, :] + jnp.sum(p, axis=0, keepdims=True)
            acc_s[rs, :] = alpha * acc_s[rs, :] + _dot(vT_ref[rs, :], p.astype(BF16), NN)
            m_s[h:h + 1, :] = m_new

        @pl.when(kj == qi)
        def _():
            for h in range(ATT_HS):
                rs = slice(HEAD_DIM * h, HEAD_DIM * (h + 1))
                oT_ref[rs, :] = (acc_s[rs, :] / l_s[h:h + 1, :]).astype(BF16)
            lse_ref[...] = m_s[...] + jnp.log(l_s[...])

    tok = (ATT_B, ATT_W)
    feat = (ATT_W, ATT_B)
    return pl.pallas_call(
        body,
        out_shape=(jax.ShapeDtypeStruct((1024, T), BF16), jax.ShapeDtypeStruct((ATT_HEADS // ATT_HS, ATT_HS, T), F32)),
        grid=(B, ATT_HEADS // ATT_HS, nq // 2, nq + 1),
        in_specs=[pl.BlockSpec(tok, mp["k_tok"]), pl.BlockSpec(feat, mp["q_feat"]), pl.BlockSpec(feat, mp["k_feat"]),
                  pl.BlockSpec((None, ATT_B, ATT_B), mp["bias"])],
        out_specs=(pl.BlockSpec(feat, mp["q_feat"]), pl.BlockSpec((None, ATT_HS, ATT_B), mp["lse"])),
        scratch_shapes=[pltpu.VMEM((ATT_HS, ATT_B), F32), pltpu.VMEM((ATT_HS, ATT_B), F32),
                        pltpu.VMEM((ATT_W, ATT_B), F32), pltpu.VMEM((ATT_HS, ATT_B, ATT_B), F32)],
        name=name, compiler_params=_cp("parallel", "parallel", "arbitrary", "arbitrary"))(kn, qT, vT, bias)


def _att_scores(k_ref, qT_ref, v_ref, doT_ref, s_s, dp_s):
    for h in range(ATT_HS):
        rs = slice(HEAD_DIM * h, HEAD_DIM * (h + 1))
        s_s[h] = _dot(k_ref[:, rs], qT_ref[rs, :], NN)
        dp_s[h] = _dot(v_ref[:, rs], doT_ref[rs, :].astype(BF16), NN)


def _att_p_ds(s_s, dp_s, doT_ref, oT_ref, lse_ref, bv, h):
    rs = slice(HEAD_DIM * h, HEAD_DIM * (h + 1))
    delta = jnp.sum(doT_ref[rs, :] * oT_ref[rs, :].astype(F32), axis=0, keepdims=True)
    p = jnp.exp(s_s[h] + bv - lse_ref[h:h + 1, :])
    return p, p * (dp_s[h] - delta)


def _att_bwd_dq(name, kn, qT, vb, knT, bias, doT, oT, lse, B):
    T = kn.shape[0]
    nq = (T // B) // ATT_B
    qk = _row_pair(nq)
    mp = _att_maps(nq, qk)

    def body(k_ref, qT_ref, v_ref, kT_ref, bias_ref, doT_ref, oT_ref, lse_ref, dqT_ref, acc_s, s_s, dp_s):
        qi, kj = qk(pl.program_id(2), pl.program_id(3))

        @pl.when(kj == 0)
        def _():
            acc_s[...] = jnp.zeros_like(acc_s)

        bv = bias_ref[...]
        _att_scores(k_ref, qT_ref, v_ref, doT_ref, s_s, dp_s)
        for h in range(ATT_HS):
            rs = slice(HEAD_DIM * h, HEAD_DIM * (h + 1))
            p, ds = _att_p_ds(s_s, dp_s, doT_ref, oT_ref, lse_ref, bv, h)
            acc_s[rs, :] += _dot(kT_ref[rs, :], ds.astype(BF16), NN)

        @pl.when(kj == qi)
        def _():
            dqT_ref[...] = acc_s[...] * ATT_SCALE

    tok = (ATT_B, ATT_W)
    feat = (ATT_W, ATT_B)
    return pl.pallas_call(
        body, out_shape=jax.ShapeDtypeStruct((1024, T), F32), grid=(B, ATT_HEADS // ATT_HS, nq // 2, nq + 1),
        in_specs=[pl.BlockSpec(tok, mp["k_tok"]), pl.BlockSpec(feat, mp["q_feat"]), pl.BlockSpec(tok, mp["k_tok"]),
                  pl.BlockSpec(feat, mp["k_feat"]), pl.BlockSpec((None, ATT_B, ATT_B), mp["bias"]),
                  pl.BlockSpec(feat, mp["q_feat"]), pl.BlockSpec(feat, mp["q_feat"]),
                  pl.BlockSpec((None, ATT_HS, ATT_B), mp["lse"])],
        out_specs=pl.BlockSpec(feat, mp["q_feat"]),
        scratch_shapes=[pltpu.VMEM((ATT_W, ATT_B), F32), pltpu.VMEM((ATT_HS, ATT_B, ATT_B), F32),
                        pltpu.VMEM((ATT_HS, ATT_B, ATT_B), F32)],
        name=name, compiler_params=_cp("parallel", "parallel", "arbitrary", "arbitrary"))(
            kn, qT, vb, knT, bias, doT, oT, lse)


def _att_bwd_dkv(name, kn, qT, vb, qn, bias, doT, oT, lse, dyn, dproj, B):
    T = kn.shape[0]
    nq = (T // B) // ATT_B
    qk = _col_pair(nq)
    mp = _att_maps(nq, qk)

    def body(k_ref, qT_ref, v_ref, q_ref, bias_ref, doT_ref, oT_ref, lse_ref, do_ref, buf_ref, dk_ref, dv_ref, dk_s, dv_s,
             s_s, dp_s):
        qi, kj = qk(pl.program_id(2), pl.program_id(3))

        @pl.when(qi == kj)
        def _():
            dk_s[...] = jnp.zeros_like(dk_s)
            dv_s[...] = jnp.zeros_like(dv_s)

        bv = bias_ref[...]
        _att_scores(k_ref, qT_ref, v_ref, doT_ref, s_s, dp_s)
        for h in range(ATT_HS):
            rs = slice(HEAD_DIM * h, HEAD_DIM * (h + 1))
            p, ds = _att_p_ds(s_s, dp_s, doT_ref, oT_ref, lse_ref, bv, h)
            dv_s[h] += _dot(p.astype(BF16), do_ref[:, rs].astype(BF16), NN)
            dk_s[h] += _dot(ds.astype(BF16), q_ref[:, rs], NN)

        @pl.when(qi == nq - 1)
        def _():
            for h in range(ATT_HS):
                rs = slice(HEAD_DIM * h, HEAD_DIM * (h + 1))
                dk_ref[:, rs] = dk_s[h] * ATT_SCALE
                dv_ref[:, rs] = dv_s[h].astype(BF16)

    tok = (ATT_B, ATT_W)
    feat = (ATT_W, ATT_B)
    v_cb = COL_V // ATT_W
    return pl.pallas_call(
        body, out_shape=(jax.ShapeDtypeStruct((T, 1024), F32), jax.ShapeDtypeStruct(dproj.shape, dproj.dtype)),
        grid=(B, ATT_HEADS // ATT_HS, nq // 2, nq + 1),
        in_specs=[pl.BlockSpec(tok, mp["k_tok"]), pl.BlockSpec(feat, mp["q_feat"]), pl.BlockSpec(tok, mp["k_tok"]),
                  pl.BlockSpec(tok, mp["q_tok"]), pl.BlockSpec((None, ATT_B, ATT_B), mp["bias"]),
                  pl.BlockSpec(feat, mp["q_feat"]), pl.BlockSpec(feat, mp["q_feat"]),
                  pl.BlockSpec((None, ATT_HS, ATT_B), mp["lse"]), pl.BlockSpec(tok, mp["do_tok"]), ANY],
        out_specs=(pl.BlockSpec(tok, mp["k_tok"]),
                   pl.BlockSpec(tok, lambda b, g, r, c: (b * nq + qk(r, c)[1], v_cb + g))),
        input_output_aliases={9: 1},
        scratch_shapes=[pltpu.VMEM((ATT_HS, ATT_B, HEAD_DIM), F32), pltpu.VMEM((ATT_HS, ATT_B, HEAD_DIM), F32),
                        pltpu.VMEM((ATT_HS, ATT_B, ATT_B), F32), pltpu.VMEM((ATT_HS, ATT_B, ATT_B), F32)],
        name=name, compiler_params=_cp("parallel", "parallel", "arbitrary", "arbitrary"))(
            kn, qT, vb, qn, bias, doT, oT, lse, dyn, dproj)


def _group_cols(v):
    return v.reshape(SSD_GROUPS, 4)


def _ssd_params(p):
    rows = jnp.stack([_group_cols(p["dt_bias"]), _group_cols(p["a_log"]), _group_cols(p["d_skip"])], axis=1)
    return rows, jnp.swapaxes(rows, 1, 2)


def _mixer_fwd(tag, x1, p, weights, bias, B):
    T = x1.shape[0]
    S = T // B
    nt = T // ROW_T
    h2 = _rms_fwd(tag + "_mixrms", x1, p["mix_norm"][None])
    wi = weights("win", h2)
    win, cw = wi["win"], wi["cw"]
    proj = _mm(tag + "_proj",
               [(h2, pl.BlockSpec((ROW_T, D_MODEL), lambda j, i, k: (i, 0)),
                 win, pl.BlockSpec((D_MODEL, PROJ_TN), lambda j, i, k: (0, j)))],
               jax.ShapeDtypeStruct((T, IN_PAD), F32), pl.BlockSpec((ROW_T, PROJ_TN), lambda j, i, k: (i, j)),
               (IN_PAD // PROJ_TN, nt, 1), NN, (ROW_T, PROJ_TN))
    xc = _conv_fwd(tag + "_conv", proj, cw, p["conv_b"][None], B)
    dtraw = proj[:, COL_DT:COL_DT + SSD_HEADS].reshape(T, SSD_GROUPS, 4)
    dtc = jnp.transpose(dtraw, (1, 0, 2))
    dtr = jnp.transpose(dtraw, (1, 2, 0))
    pcol, prow = _ssd_params(p)
    Y, y_ssd, hs = _ssd_fwd(tag + "_ssd", xc, proj, dtc, dtr, pcol, prow, p["ssd_norm"][None], B)
    qn = _headnorm_fwd(tag + "_qn", proj, COL_Q // 1024, p["q_norm"][None])
    kn = _headnorm_fwd(tag + "_kn", proj, COL_K // 1024, p["k_norm"][None])
    qT = (qn * ATT_SCALE).T
    vb = proj[:, COL_V:COL_V + 1024].astype(BF16)
    oT, lse = _att_fwd(tag + "_att", kn, qT, vb.T, bias, B)
    ymix = jnp.concatenate([y_ssd, oT.T], axis=1)
    rest = weights("rest", ymix)
    x2 = _mm(tag + "_out",
             [(ymix, pl.BlockSpec((ROW_T, MIX_SH), lambda i, n, k, j=j: (i, j)),
               rest["wout"], pl.BlockSpec((None, MIX_SH, D_MODEL), lambda i, n, k, j=j: (j, 0, 0)))
              for j in range(N_SHARD)],
             jax.ShapeDtypeStruct((T, D_MODEL), F32), pl.BlockSpec((ROW_T, D_MODEL), lambda i, n, k: (i, 0)),
             (nt, 1, 1), NN, (ROW_T, D_MODEL),
             res=(x1, pl.BlockSpec((ROW_T, D_MODEL), lambda i, n, k: (i, 0))))
    saved = dict(x1=x1, h2=h2, proj=proj, xc=xc, dtc=dtc, dtr=dtr, Y=Y, hs=hs,
                 qn=qn, kn=kn, qT=qT, vb=vb, oT=oT, lse=lse, ymix=ymix, win=win, cw=cw, wout=rest["wout"])
    return x2, saved


def _mixer_bwd(tag, dx2, sv, p, bias, B):
    T = dx2.shape[0]
    S = T // B
    nt = T // ROW_T
    sg = {}
    dymix = _mm(tag + "_dymix",
                [(dx2, pl.BlockSpec((ROW_T, D_MODEL), lambda n, i, k: (i, 0)),
                  sv["wout"], pl.BlockSpec((None, MIX_SH, D_MODEL), lambda n, i, k: (n, 0, 0)))],
                jax.ShapeDtypeStruct((T, MIX_W), F32), pl.BlockSpec((ROW_T, MIX_SH), lambda n, i, k: (i, n)),
                (N_SHARD, nt, 1), NT, (ROW_T, MIX_SH))
    tkw = min(TK_W, T)
    gwout = _mm(tag + "_dwout",
                [(sv["ymix"], pl.BlockSpec((tkw, MIX_SH), lambda m, n, k: (k, m)),
                  dx2, pl.BlockSpec((tkw, D_MODEL), lambda m, n, k: (k, 0)))],
                jax.ShapeDtypeStruct((N_SHARD, MIX_SH, D_MODEL), BF16),
                pl.BlockSpec((None, MIX_SH, D_MODEL), lambda m, n, k: (m, 0, 0)),
                (N_SHARD, 1, T // tkw), TN, (MIX_SH, D_MODEL))
    proj = sv["proj"]
    doT = dymix[:, 1024:].T
    dqn = _att_bwd_dq(tag + "_attdq", sv["kn"], sv["qT"], sv["vb"], sv["kn"].T, bias, doT, sv["oT"], sv["lse"], B).T
    dproj = lax.empty((T, IN_PAD), BF16)
    dkn, dproj = _att_bwd_dkv(tag + "_attdkv", sv["kn"], sv["qT"], sv["vb"], sv["qn"], bias, doT, sv["oT"], sv["lse"],
                              dymix, dproj, B)
    dproj, sg["q_norm"] = _headnorm_bwd(tag + "_qnb", dqn, proj, COL_Q // 1024, p["q_norm"][None], dproj)
    dproj, sg["k_norm"] = _headnorm_bwd(tag + "_knb", dkn, proj, COL_K // 1024, p["k_norm"][None], dproj)
    pcol, prow = _ssd_params(p)
    dxs, dB, dC, dproj, ddt, dpar, dnw = _ssd_bwd(tag + "_ssdb", dymix, sv["Y"], sv["xc"], proj, sv["dtc"], sv["dtr"],
                                                  pcol, prow, p["ssd_norm"][None], sv["hs"], dproj, B)
    dpar = jnp.sum(dpar, axis=0)
    sg["dt_bias"] = dpar[:, 0, :].reshape(SSD_HEADS)
    sg["a_log"] = dpar[:, 1, :].reshape(SSD_HEADS)
    sg["d_skip"] = dpar[:, 2, :].reshape(SSD_HEADS)
    sg["ssd_norm"] = jnp.sum(dnw, axis=0)
    dproj, sg["conv_w"], sg["conv_b"] = _conv_bwd(tag + "_convb", proj, dxs, dB, dC, sv["cw"], p["conv_b"][None],
                                                  dproj, B)
    ddt16 = jnp.transpose(ddt, (1, 0, 2)).reshape(T, SSD_HEADS)
    dproj = lax.dynamic_update_slice(dproj, jnp.pad(ddt16, ((0, 0), (0, IN_PAD - COL_DT - SSD_HEADS))).astype(BF16),
                                     (0, COL_DT))
    win = sv["win"]
    gwin = _mm(tag + "_dwin",
               [(sv["h2"], pl.BlockSpec((tkw, D_MODEL), lambda n, m, k: (k, 0)),
                 dproj, pl.BlockSpec((tkw, PROJ_TN), lambda n, m, k: (k, n)))],
               jax.ShapeDtypeStruct((D_MODEL, IN_PAD), BF16), pl.BlockSpec((D_MODEL, PROJ_TN), lambda n, m, k: (0, n)),
               (IN_PAD // PROJ_TN, 1, T // tkw), TN, (D_MODEL, PROJ_TN))
    dh2 = _mm(tag + "_dh2",
              [(dproj, pl.BlockSpec((ROW_T, PROJ_TN), lambda i, n, k: (i, k)),
                win, pl.BlockSpec((D_MODEL, PROJ_TN), lambda i, n, k: (0, k)))],
              jax.ShapeDtypeStruct((T, D_MODEL), F32), pl.BlockSpec((ROW_T, D_MODEL), lambda i, n, k: (i, 0)),
              (nt, 1, IN_PAD // PROJ_TN), NT, (ROW_T, D_MODEL))
    dx1, sg["mix_norm"] = _rms_bwd(tag + "_mixrmsb", dh2, sv["x1"], p["mix_norm"][None], dx2)
    return dx1, sg, gwout, gwin


def _win_pack(w):
    return jnp.concatenate([w[:, :3072], w[:, 3088:], w[:, 3072:3088],
                            jnp.zeros((w.shape[0], IN_PAD - IN_PROJ), w.dtype)], axis=1)


def _win_unpack(g):
    return jnp.concatenate([g[:, :3072], g[:, COL_DT:COL_DT + SSD_HEADS], g[:, 3072:COL_DT]], axis=1)


def _local_step(x, target, small, weights, scatter, B):
    T = x.shape[0]
    bias = _att_bias((T // B) // ATT_B)
    saved = []
    h = x
    for l in range(DEPTH):
        tag = "l%d" % l
        p = {k: v[l] for k, v in small.items()}
        w1 = weights(l, "ffn1", h)
        x1, ffn1 = _ffn_fwd(tag + "f1", h, p["ffn1_norm"][None], w1["g1"], w1["u1"], w1["d1"])
        x2, sv = _mixer_fwd(tag, x1, p, functools.partial(weights, l), bias, B)
        w2 = weights(l, "rest", x2)
        h, ffn2 = _ffn_fwd(tag + "f2", x2, p["ffn2_norm"][None], w2["g2"], w2["u2"], w2["d2"])
        saved.append((ffn1, sv, ffn2, w1, w2))
    d, lsum = _loss_grad("loss", h, target)
    sgrads = [None] * DEPTH
    for l in reversed(range(DEPTH)):
        tag = "l%db" % l
        p = {k: v[l] for k, v in small.items()}
        ffn1, sv, ffn2, w1, w2 = saved[l]
        sg = {}
        d, sg["ffn2_norm"] = _ffn_bwd(tag + "f2", d, ffn2, p["ffn2_norm"][None], w2["g2"], w2["u2"], w2["d2"],
                                      lambda gg, gu, gd, c, l=l: scatter(l, "ffn2", dict(g2=gg, u2=gu, d2=gd), c))
        d, sgm, gwout, gwin = _mixer_bwd(tag, d, sv, p, bias, B)
        sg.update(sgm)
        d = scatter(l, "mixer", dict(wout=gwout, win=gwin), d)
        d, sg["ffn1_norm"] = _ffn_bwd(tag + "f1", d, ffn1, p["ffn1_norm"][None], w1["g1"], w1["u1"], w1["d1"],
                                      lambda gg, gu, gd, c, l=l: scatter(l, "ffn1", dict(g1=gg, u1=gu, d1=gd), c))
        sgrads[l] = sg
    return lsum, d, sgrads


MESH = pl.DeviceIdType.MESH
ANY = pl.BlockSpec(memory_space=pl.ANY)


def _place():
    return lax.axis_index("x"), lax.axis_index("y"), lax.axis_index("c")


def _other_chips(x, y):
    return [(1 - x, y), (x, 1 - y), (1 - x, 1 - y)]


HBM = pl.BlockSpec(memory_space=pltpu.HBM)
SEM = pl.BlockSpec(memory_space=pltpu.SEMAPHORE)
EFFECT = pltpu.SideEffectType.DATAFLOW_SIDE_EFFECTING


def _hbm(a):
    return pltpu.with_memory_space_constraint(a, pltpu.HBM)


def _exchange(gather, src, land, send, recv, n, act):
    x, y, c = _place()
    for k, (px, py) in enumerate(_other_chips(x, y)):
        for a in range(n):
            if gather:
                s_out, d_out, d_in = src[a], land[a].at[2 * x + y], land[a].at[2 * px + py]
            else:
                s_out, d_out, d_in = src[a].at[2 * px + py], land[a].at[k], land[a].at[k]
            act(pltpu.make_async_remote_copy(
                src_ref=s_out, dst_ref=d_out if act is _start else d_in, send_sem=send.at[k * n + a],
                recv_sem=recv.at[k * n + a], device_id=(px, py, c), device_id_type=MESH))


def _start(cp):
    cp.start()


def _finish(cp):
    cp.wait_send()
    cp.wait_recv()


def _exchange_start(name, gather, srcs, carry):
    n = len(srcs)
    lands = [lax.empty(((N_SHARD,) + s.shape) if gather else ((3,) + s.shape[1:]), s.dtype) for s in srcs]

    def body(*refs):
        _exchange(gather, refs[:n], refs[n:2 * n], refs[2 * n + 1], refs[2 * n + 2], n, _start)

    ops = [_hbm(a) for a in list(srcs) + lands + [carry]]
    out = pl.pallas_call(
        body, name=name,
        out_shape=(pltpu.SemaphoreType.DMA((3 * n,)), pltpu.SemaphoreType.DMA((3 * n,)),
                   *[pltpu.HBM(a.shape, a.dtype) for a in ops]),
        in_specs=[HBM] * len(ops), out_specs=(SEM, SEM, *[HBM] * len(ops)),
        input_output_aliases={i: 2 + i for i in range(len(ops))},
        compiler_params=pltpu.CompilerParams(has_side_effects=EFFECT))(*ops)
    return dict(gather=gather, send=out[0], recv=out[1], srcs=list(out[2:2 + n]), lands=list(out[2 + n:2 + 2 * n])), out[-1]


def _exchange_wait(name, ex, after):
    n = len(ex["srcs"])
    gather = ex["gather"]

    def body(*refs):
        _exchange(gather, refs[:n], refs[n:2 * n], refs[2 * n], refs[2 * n + 1], n, _finish)

    ops = ex["srcs"] + ex["lands"]
    out = pl.pallas_call(
        body, name=name, out_shape=[pltpu.HBM(a.shape, a.dtype) for a in ops],
        in_specs=[HBM] * len(ops) + [SEM, SEM, ANY], out_specs=[HBM] * len(ops),
        input_output_aliases={i: i for i in range(len(ops))},
        compiler_params=pltpu.CompilerParams(has_side_effects=EFFECT))(*ops, ex["send"], ex["recv"], after)
    return list(out[:n]), list(out[n:])


def _swap_sibling(parts):
    n = len(parts)

    def body(*refs):
        src, dst = refs[:n], refs[n:2 * n]
        send, recv = refs[2 * n:]
        x, y, c = _place()
        cps = [pltpu.make_async_remote_copy(src_ref=src[a], dst_ref=dst[a], send_sem=send.at[a], recv_sem=recv.at[a],
                                            device_id=(x, y, 1 - c), device_id_type=MESH) for a in range(n)]
        for cp in cps:
            cp.start()
        for cp in cps:
            cp.wait_recv()
        for cp in cps:
            cp.wait_send()

    return pl.pallas_call(
        body, out_shape=[jax.ShapeDtypeStruct(p.shape, p.dtype) for p in parts],
        in_specs=[ANY] * n, out_specs=[ANY] * n,
        scratch_shapes=[pltpu.SemaphoreType.DMA((n,)), pltpu.SemaphoreType.DMA((n,))],
        name="swap_sibling")(*parts)


def _allreduce_small(name, v):
    R = v.shape[0]

    def body(v_ref, o_ref, buf, send, recv):
        x, y, c = _place()
        me = 4 * x + 2 * y + c
        buf[me] = v_ref[...]
        cps = []
        for k in range(1, 8):
            fx, fy, fc = (k >> 2) & 1, (k >> 1) & 1, k & 1
            px = 1 - x if fx else x
            py = 1 - y if fy else y
            pc = 1 - c if fc else c
            cp = pltpu.make_async_remote_copy(src_ref=v_ref, dst_ref=buf.at[me], send_sem=send.at[k - 1],
                                              recv_sem=recv.at[k - 1], device_id=(px, py, pc), device_id_type=MESH)
            cp.start()
            cps.append((cp, 4 * px + 2 * py + pc))
        for k, (cp, peer) in enumerate(cps):
            pltpu.make_async_remote_copy(src_ref=v_ref, dst_ref=buf.at[peer], send_sem=send.at[k], recv_sem=recv.at[k],
                                         device_id=(x, y, c), device_id_type=MESH).wait_recv()
        for cp, _ in cps:
            cp.wait_send()
        acc = buf[0]
        for d in range(1, 8):
            acc = acc + buf[d]
        o_ref[...] = acc

    return pl.pallas_call(
        body, out_shape=jax.ShapeDtypeStruct((R, 128), F32),
        in_specs=[pl.BlockSpec(memory_space=pltpu.VMEM)], out_specs=pl.BlockSpec(memory_space=pltpu.VMEM),
        scratch_shapes=[pltpu.VMEM((8, R, 128), F32), pltpu.SemaphoreType.DMA((7,)), pltpu.SemaphoreType.DMA((7,))],
        name=name)(v)


def _row_tile(r):
    for t in (256, 128, 64, 32, 16, 8):
        if r % t == 0:
            return t
    raise ValueError(r)


def _sum4(name, own, got):
    R, C = own.shape
    tr = _row_tile(R)

    def body(o_ref, g_ref, s_ref):
        s = o_ref[...].astype(F32)
        for k in range(3):
            s = s + g_ref[k].astype(F32)
        s_ref[...] = s

    return pl.pallas_call(
        body, out_shape=jax.ShapeDtypeStruct((R, C), F32), grid=(R // tr,),
        in_specs=[pl.BlockSpec((tr, C), lambda i: (i, 0)), pl.BlockSpec((3, tr, C), lambda i: (0, i, 0))],
        out_specs=pl.BlockSpec((tr, C), lambda i: (i, 0)), name=name, compiler_params=_cp("parallel"))(own, got)


def _adamw(name, w, gparts, m, v):
    R, C = w.shape
    tr = _row_tile(R)
    ng = len(gparts)
    c1 = 1.0 - ADAM_B1 ** ADAM_STEP
    c2 = 1.0 - ADAM_B2 ** ADAM_STEP

    def body(*refs):
        w_ref = refs[0]
        g_refs = refs[1:1 + ng]
        m_ref, v_ref, go_ref, d_ref, mo_ref, vo_ref = refs[1 + ng:]
        g = g_refs[0][...]
        for r in g_refs[1:]:
            g = g + r[...]
        mn = ADAM_B1 * m_ref[...] + (1.0 - ADAM_B1) * g
        vn = ADAM_B2 * v_ref[...] + (1.0 - ADAM_B2) * (g * g)
        go_ref[...] = g
        mo_ref[...] = mn
        vo_ref[...] = vn
        d_ref[...] = -ADAM_LR * ((mn / c1) / (jnp.sqrt(vn / c2) + ADAM_EPS) + ADAM_WD * w_ref[...])

    blk = pl.BlockSpec((tr, C), lambda i: (i, 0))
    osh = jax.ShapeDtypeStruct((R, C), F32)
    return pl.pallas_call(
        body, out_shape=(osh, osh, osh, osh), grid=(R // tr,), in_specs=[blk] * (3 + ng), out_specs=(blk,) * 4,
        name=name, compiler_params=_cp("parallel"))(w, *gparts, m, v)


def _adamw_layers(name, w, sums, m, v):
    R2, C = w.shape
    R = R2 // DEPTH
    tr = _row_tile(R)
    nr = R // tr
    c1 = 1.0 - ADAM_B1 ** ADAM_STEP
    c2 = 1.0 - ADAM_B2 ** ADAM_STEP

    def body(w_ref, a0, b0, a1, b1, m_ref, v_ref, go_ref, d_ref, mo_ref, vo_ref):
        g = jnp.where(pl.program_id(0) == 0, a0[...] + b0[...], a1[...] + b1[...])
        mn = ADAM_B1 * m_ref[...] + (1.0 - ADAM_B1) * g
        vn = ADAM_B2 * v_ref[...] + (1.0 - ADAM_B2) * (g * g)
        go_ref[...] = g
        mo_ref[...] = mn
        vo_ref[...] = vn
        d_ref[...] = -ADAM_LR * ((mn / c1) / (jnp.sqrt(vn / c2) + ADAM_EPS) + ADAM_WD * w_ref[...])

    blk = pl.BlockSpec((tr, C), lambda l, i: (l * nr + i, 0))
    lay0 = pl.BlockSpec((tr, C), lambda l, i: (jnp.where(l == 0, i, nr - 1), 0))
    lay1 = pl.BlockSpec((tr, C), lambda l, i: (jnp.where(l == 1, i, 0), 0))
    osh = jax.ShapeDtypeStruct((R2, C), F32)
    return pl.pallas_call(
        body, out_shape=(osh, osh, osh, osh), grid=(DEPTH, nr),
        in_specs=[blk, lay0, lay0, lay1, lay1, blk, blk], out_specs=(blk,) * 4,
        name=name, compiler_params=_cp("arbitrary", "arbitrary"))(w, *sums[0], *sums[1], m, v)


BIG = [("ffn1_w_gate", "g1"), ("ffn1_w_up", "u1"), ("ffn1_w_down", "d1"), ("w_in", "win"), ("w_out", "wout"),
       ("ffn2_w_gate", "g2"), ("ffn2_w_up", "u2"), ("ffn2_w_down", "d2")]
SMALL = ["ffn1_norm", "mix_norm", "conv_b", "dt_bias", "a_log", "d_skip", "ssd_norm", "q_norm", "k_norm", "ffn2_norm"]
WEIGHTS = ["ffn1_norm", "ffn1_w_gate", "ffn1_w_up", "ffn1_w_down", "mix_norm", "w_in", "conv_w", "conv_b", "dt_bias",
           "a_log", "d_skip", "ssd_norm", "q_norm", "k_norm", "w_out", "ffn2_norm", "ffn2_w_gate", "ffn2_w_up",
           "ffn2_w_down"]
CONV_SH = CONV_DIM // N_SHARD
GATHER_GROUPS = [(0, "ffn1", ["g1", "u1", "d1"]), (0, "win", ["win", "cw"]), (0, "rest", ["wout", "g2", "u2", "d2"]),
                 (1, "all", ["g1", "u1", "d1", "win", "cw", "wout", "g2", "u2", "d2"])]


def _pad128(v):
    v = v.reshape(-1)
    return jnp.pad(v, (0, (-v.shape[0]) % 128))


def _pack(pieces):
    flat, offs, pos = [], [], 0
    for p in pieces:
        q = _pad128(p.astype(F32))
        offs.append(pos)
        pos += q.shape[0] // 128
        flat.append(q)
    total = -(-pos // 8) * 8
    out = jnp.concatenate(flat + [jnp.zeros(((total - pos) * 128,), F32)]).reshape(total, 128)
    return out, offs


def _unpack(packed, offs, shapes):
    out = []
    for off, shp in zip(offs, shapes):
        n = int(np.prod(shp))
        rows = -(-n // 128)
        out.append(packed[off:off + rows].reshape(-1)[:n].reshape(shp))
    return out


def kernel(x, ffn1_norm, ffn1_w_gate, ffn1_w_up, ffn1_w_down, mix_norm, w_in, conv_w, conv_b, dt_bias, a_log, d_skip, ssd_norm, q_norm, k_norm, w_out, ffn2_norm, ffn2_w_gate, ffn2_w_up, ffn2_w_down, loss_target, m_ffn1_norm, m_ffn1_w_gate, m_ffn1_w_up, m_ffn1_w_down, m_mix_norm, m_w_in, m_conv_w, m_conv_b, m_dt_bias, m_a_log, m_d_skip, m_ssd_norm, m_q_norm, m_k_norm, m_w_out, m_ffn2_norm, m_ffn2_w_gate, m_ffn2_w_up, m_ffn2_w_down, v_ffn1_norm, v_ffn1_w_gate, v_ffn1_w_up, v_ffn1_w_down, v_mix_norm, v_w_in, v_conv_w, v_conv_b, v_dt_bias, v_a_log, v_d_skip, v_ssd_norm, v_q_norm, v_k_norm, v_w_out, v_ffn2_norm, v_ffn2_w_gate, v_ffn2_w_up, v_ffn2_w_down):
    A = dict(locals())
    ix, iy, ic = _place()
    me = 2 * ix + iy
    B, S, _ = x.shape
    T = B * S

    own = {key: A[name].astype(BF16) for name, key in BIG}
    own["cw"] = conv_w
    exs, first_norm = [], ffn1_norm
    for gi, (l, _, keys) in enumerate(GATHER_GROUPS):
        ex, first_norm = _exchange_start("gather_start%d" % gi, True, [own[key][l] for key in keys], first_norm)
        exs.append(ex)
    landed = {}

    def weights(l, group, after):
        gi = [i for i, (gl, gname, _) in enumerate(GATHER_GROUPS) if gl == l and gname in (group, "all")][0]
        if gi not in landed:
            srcs, lands = _exchange_wait("gather_wait%d" % gi, exs[gi], after)
            landed[gi] = {}
            for key, mine, land in zip(GATHER_GROUPS[gi][2], srcs, lands):
                full = lax.dynamic_update_slice(land, mine[None], (me, 0, 0))
                if key == "win":
                    full = _win_pack(jnp.concatenate([full[j] for j in range(N_SHARD)], axis=1))
                if key == "cw":
                    full = jnp.transpose(full, (1, 0, 2)).reshape(CONV_K, CONV_DIM)
                landed[gi][key] = full
        return landed[gi]

    pending = []

    def scatter(l, group, grads, carry):
        keys = sorted(grads)
        arrs = [grads[key] for key in keys]
        if "win" in grads:
            arrs[keys.index("win")] = jnp.transpose(_win_unpack(grads["win"]).reshape(D_MODEL, N_SHARD, IN_SH), (1, 0, 2))
        ex, carry = _exchange_start("scatter_start_l%d_%s" % (l, group), False, arrs, carry)
        pending.append((l, keys, ex))
        return carry

    small = {name: A[name] for name in SMALL}
    small["ffn1_norm"] = first_norm
    lsum, dx, sgrads = _local_step(x.reshape(T, D_MODEL), loss_target.reshape(T, D_MODEL), small, weights, scatter, B)

    names = SMALL + ["conv_w"]
    shapes = [A[n].shape for n in SMALL] + [(DEPTH, CONV_K, CONV_DIM), ()]
    pieces = [jnp.stack([sgrads[l][n].reshape(shp[1:]) for l in range(DEPTH)]) for n, shp in zip(names, shapes)]
    pieces.append(0.5 / D_MODEL * jnp.sum(lsum))
    packed, offs = _pack(pieces)
    red = _allreduce_small("allreduce_small", packed)
    red = _unpack(red, offs, shapes)
    loss = red[-1]
    sg = dict(zip(names, red[:-1]))

    sums, after = {}, dx
    for idx, (l, keys, ex) in enumerate(pending):
        srcs, lands = _exchange_wait("scatter_wait%d" % idx, ex, after)
        for key, g, got in zip(keys, srcs, lands):
            mine = lax.dynamic_index_in_dim(g, me, axis=0, keepdims=False)
            sums[key, l] = after = _sum4("sum_%s_l%d" % (key, l), mine, got)
    order = [(key, l) for _, key in BIG for l in range(DEPTH)]
    theirs = dict(zip(order, _swap_sibling([sums[k] for k in order])))

    out = {}
    for name, key in BIG:
        shp = A[name].shape
        flat = lambda a: a.reshape(shp[0] * shp[1], shp[2])
        res = _adamw_layers("adamw_" + key, flat(A[name]), [(sums[key, l], theirs[key, l]) for l in range(DEPTH)],
                            flat(A["m_" + name]), flat(A["v_" + name]))
        out[name] = [r.reshape(shp) for r in res]

    wp, offs = _pack([A[n] for n in SMALL])
    gp, _ = _pack([sg[n] for n in SMALL])
    mp, _ = _pack([A["m_" + n] for n in SMALL])
    vp, _ = _pack([A["v_" + n] for n in SMALL])
    res = _adamw("adamw_small", wp, [gp], mp, vp)
    shapes = [A[n].shape for n in SMALL]
    res = [_unpack(r, offs, shapes) for r in res]
    for i, n in enumerate(SMALL):
        out[n] = [res[q][i] for q in range(4)]
    gcw = lax.dynamic_slice_in_dim(sg["conv_w"], me * CONV_SH, CONV_SH, axis=2)
    flat = lambda a: a.reshape(DEPTH * CONV_K, CONV_SH)
    res = _adamw("adamw_conv_w", flat(conv_w), [flat(gcw)], flat(m_conv_w), flat(v_conv_w))
    out["conv_w"] = [r.reshape(conv_w.shape) for r in res]

    outs = [loss, dx.reshape(B, S, D_MODEL)]
    for q in range(4):
        outs += [out[n][q] for n in WEIGHTS]
    return tuple(outs)
```

```python
import functools
import math

import numpy as np
import jax
import jax.numpy as jnp
from jax import lax
from jax.experimental import pallas as pl
from jax.experimental.pallas import tpu as pltpu

F32 = jnp.float32
BF16 = jnp.bfloat16

D_MODEL = 1024
DEPTH = 2
N_SHARD = 4
D_FF = 2816
FF_SH = D_FF // N_SHARD
SSD_HEADS = 16
HEAD_DIM = 64
SSD_GROUPS = 4
GROUP_W = 256
SSD_STATE = 128
CONV_K = 4
CONV_DIM = 2048
ATT_HEADS = 16
MIX_W = 2048
MIX_SH = MIX_W // N_SHARD
IN_PROJ = 6160
IN_SH = IN_PROJ // N_SHARD
IN_PAD = 6272
PROJ_TN = 896
COL_Z, COL_XBC, COL_Q, COL_K, COL_V, COL_DT = 0, 1024, 3072, 4096, 5120, 6144
EPS = 1e-6
NEG = -1e30
SSD_L = 256
ATT_B = 256
ROW_T = 512
HALF_T = ROW_T // 2
TK_W = 2048
CONV_CT = 256
CONV_R = 256
PAD_R = 8

ADAM_LR, ADAM_B1, ADAM_B2, ADAM_EPS, ADAM_WD, ADAM_STEP = 0.001, 0.9, 0.999, 1e-08, 0.01, 10

NN = (((1,), (0,)), ((), ()))
NT = (((1,), (1,)), ((), ()))
TN = (((0,), (0,)), ((), ()))

VMEM_LIMIT = 56 * 1024 * 1024


def _cp(*sem):
    return pltpu.CompilerParams(dimension_semantics=sem, vmem_limit_bytes=VMEM_LIMIT)


def _dot(a, b, dims):
    return lax.dot_general(a, b, dims, preferred_element_type=F32)


def _sigmoid(x):
    return 0.5 * jnp.tanh(0.5 * x) + 0.5


def _softplus(x):
    return jnp.maximum(x, 0.0) + jnp.log(1.0 + jnp.exp(-jnp.abs(x)))


def _mm(name, pairs, out_shape, out_spec, grid, dims, acc_shape, res=None, scale=1.0):
    nk = grid[2]
    npair = len(pairs)

    def body(*refs):
        ab = refs[:2 * npair]
        pos = 2 * npair
        res_ref = None
        if res is not None:
            res_ref = refs[pos]
            pos += 1
        out_ref = refs[pos]
        s = None
        for p in range(npair):
            d = _dot(ab[2 * p][...].astype(BF16), ab[2 * p + 1][...].astype(BF16), dims)
            s = d if s is None else s + d

        def finish(r):
            if scale != 1.0:
                r = r * scale
            if res_ref is not None:
                r = r + res_ref[...]
            out_ref[...] = r.astype(out_ref.dtype)

        if nk == 1:
            finish(s)
            return
        acc = refs[pos + 1]
        k = pl.program_id(2)

        @pl.when(k == 0)
        def _():
            acc[...] = s

        @pl.when(k > 0)
        def _():
            acc[...] += s

        @pl.when(k == nk - 1)
        def _():
            finish(acc[...])

    args, specs = [], []
    for a, a_spec, b, b_spec in pairs:
        args += [a, b]
        specs += [a_spec, b_spec]
    if res is not None:
        args.append(res[0])
        specs.append(res[1])
    return pl.pallas_call(
        body, out_shape=out_shape, grid=grid, in_specs=specs, out_specs=out_spec,
        scratch_shapes=[] if nk == 1 else [pltpu.VMEM(acc_shape, F32)], name=name,
        compiler_params=_cp("parallel", "parallel", "arbitrary"))(*args)


def _rms_fwd(name, x, w):
    T = x.shape[0]

    def body(x_ref, w_ref, o_ref):
        xv = x_ref[...]
        r = lax.rsqrt(jnp.mean(xv * xv, axis=-1, keepdims=True) + EPS)
        o_ref[...] = (xv * r * w_ref[...]).astype(BF16)

    return pl.pallas_call(
        body, out_shape=jax.ShapeDtypeStruct((T, D_MODEL), BF16), grid=(T // ROW_T,),
        in_specs=[pl.BlockSpec((ROW_T, D_MODEL), lambda i: (i, 0)), pl.BlockSpec((1, D_MODEL), lambda i: (0, 0))],
        out_specs=pl.BlockSpec((ROW_T, D_MODEL), lambda i: (i, 0)), name=name, compiler_params=_cp("parallel"))(x, w)


def _rms_bwd(name, dh, x, w, dres):
    T = x.shape[0]

    def body(dh_ref, x_ref, w_ref, dres_ref, dx_ref, dw_ref):
        @pl.when(pl.program_id(0) == 0)
        def _():
            dw_ref[...] = jnp.zeros_like(dw_ref)

        xv = x_ref[...]
        r = lax.rsqrt(jnp.mean(xv * xv, axis=-1, keepdims=True) + EPS)
        xhat = xv * r
        dhv = dh_ref[...]
        dxhat = dhv * w_ref[...]
        m = jnp.mean(dxhat * xhat, axis=-1, keepdims=True)
        dx_ref[...] = dres_ref[...] + r * (dxhat - xhat * m)
        dw_ref[...] += jnp.sum(dhv * xhat, axis=0, keepdims=True)

    row = pl.BlockSpec((ROW_T, D_MODEL), lambda i: (i, 0))
    vec = pl.BlockSpec((1, D_MODEL), lambda i: (0, 0))
    return pl.pallas_call(
        body, out_shape=(jax.ShapeDtypeStruct((T, D_MODEL), F32), jax.ShapeDtypeStruct((1, D_MODEL), F32)),
        grid=(T // ROW_T,), in_specs=[row, row, vec, row], out_specs=(row, vec), name=name,
        compiler_params=_cp("arbitrary"))(dh, x, w, dres)


def _loss_grad(name, y, t):
    T = y.shape[0]

    def body(y_ref, t_ref, dy_ref, l_ref):
        @pl.when(pl.program_id(0) == 0)
        def _():
            l_ref[...] = jnp.zeros_like(l_ref)

        e = y_ref[...] - t_ref[...]
        dy_ref[...] = e * (1.0 / D_MODEL)
        l_ref[...] += jnp.sum(e * e, axis=0, keepdims=True)

    row = pl.BlockSpec((ROW_T, D_MODEL), lambda i: (i, 0))
    vec = pl.BlockSpec((1, D_MODEL), lambda i: (0, 0))
    return pl.pallas_call(
        body, out_shape=(jax.ShapeDtypeStruct((T, D_MODEL), F32), jax.ShapeDtypeStruct((1, D_MODEL), F32)),
        grid=(T // ROW_T,), in_specs=[row, row], out_specs=(row, vec), name=name,
        compiler_params=_cp("arbitrary"))(y, t)


def _ffn_gate_up(name, h, wg, wu):
    T = h.shape[0]

    def body(h_ref, wg_ref, wu_ref, dgf_ref, duf_ref, a_ref):
        for r in range(0, ROW_T, HALF_T):
            rows = slice(r, r + HALF_T)
            hv = h_ref[rows, :]
            g = _dot(hv, wg_ref[...], NN)
            u = _dot(hv, wu_ref[...], NN)
            sg = _sigmoid(g)
            silu = g * sg
            dgf_ref[rows, :] = (u * (sg * (1.0 + g * (1.0 - sg)))).astype(BF16)
            duf_ref[rows, :] = silu.astype(BF16)
            a_ref[rows, :] = (silu * u).astype(BF16)

    wspec = pl.BlockSpec((None, D_MODEL, FF_SH), lambda j, i: (j, 0, 0))
    ospec = pl.BlockSpec((None, ROW_T, FF_SH), lambda j, i: (j, i, 0))
    osh = jax.ShapeDtypeStruct((N_SHARD, T, FF_SH), BF16)
    return pl.pallas_call(
        body, out_shape=(osh, osh, osh), grid=(N_SHARD, T // ROW_T),
        in_specs=[pl.BlockSpec((ROW_T, D_MODEL), lambda j, i: (i, 0)), wspec, wspec],
        out_specs=(ospec, ospec, ospec), name=name, compiler_params=_cp("parallel", "parallel"))(h, wg, wu)


def _ffn_dact(name, dx, wd, g, u):
    T = dx.shape[0]

    def body(dx_ref, wd_ref, g_ref, u_ref, dg_ref, du_ref):
        for r in range(0, ROW_T, HALF_T):
            rows = slice(r, r + HALF_T)
            da = 0.5 * _dot(dx_ref[rows, :].astype(BF16), wd_ref[...], NT)
            dg_ref[rows, :] = (da * g_ref[rows, :].astype(F32)).astype(BF16)
            du_ref[rows, :] = (da * u_ref[rows, :].astype(F32)).astype(BF16)

    aspec = pl.BlockSpec((None, ROW_T, FF_SH), lambda j, i: (j, i, 0))
    osh = jax.ShapeDtypeStruct((N_SHARD, T, FF_SH), BF16)
    return pl.pallas_call(
        body, out_shape=(osh, osh), grid=(N_SHARD, T // ROW_T),
        in_specs=[pl.BlockSpec((ROW_T, D_MODEL), lambda j, i: (i, 0)),
                  pl.BlockSpec((None, FF_SH, D_MODEL), lambda j, i: (j, 0, 0)), aspec, aspec],
        out_specs=(aspec, aspec), name=name, compiler_params=_cp("parallel", "parallel"))(dx, wd, g, u)


def _ffn_fwd(tag, x, nw, wg, wu, wd):
    T = x.shape[0]
    h = _rms_fwd(tag + "_rms", x, nw)
    g, u, a = _ffn_gate_up(tag + "_gu", h, wg, wu)
    nt = T // ROW_T
    xo = _mm(tag + "_down",
             [(a, pl.BlockSpec((None, ROW_T, FF_SH), lambda i, n, k, j=j: (j, i, 0)),
               wd, pl.BlockSpec((None, FF_SH, D_MODEL), lambda i, n, k, j=j: (j, 0, 0))) for j in range(N_SHARD)],
             jax.ShapeDtypeStruct((T, D_MODEL), F32), pl.BlockSpec((ROW_T, D_MODEL), lambda i, n, k: (i, 0)),
             (nt, 1, 1), NN, (ROW_T, D_MODEL),
             res=(x, pl.BlockSpec((ROW_T, D_MODEL), lambda i, n, k: (i, 0))), scale=0.5)
    return xo, (x, h, g, u, a)


def _ffn_bwd(tag, dxo, saved, nw, wg, wu, wd, emit):
    x, h, g, u, a = saved
    T = x.shape[0]
    nt = T // ROW_T
    tkw = min(TK_W, T)
    nw_t = T // tkw
    dg, du = _ffn_dact(tag + "_dact", dxo, wd, g, u)
    actw = lambda f: pl.BlockSpec((None, tkw, FF_SH), f)
    gd = _mm(tag + "_dwd",
             [(a, actw(lambda m, n, k: (m, k, 0)), dxo, pl.BlockSpec((tkw, D_MODEL), lambda m, n, k: (k, 0)))],
             jax.ShapeDtypeStruct((N_SHARD, FF_SH, D_MODEL), BF16),
             pl.BlockSpec((None, FF_SH, D_MODEL), lambda m, n, k: (m, 0, 0)),
             (N_SHARD, 1, nw_t), TN, (FF_SH, D_MODEL), scale=0.5)
    hspec = pl.BlockSpec((tkw, D_MODEL), lambda j, n, k: (k, 0))
    gsh = jax.ShapeDtypeStruct((N_SHARD, D_MODEL, FF_SH), BF16)
    gspec = pl.BlockSpec((None, D_MODEL, FF_SH), lambda j, n, k: (j, 0, 0))
    gg = _mm(tag + "_dwg", [(h, hspec, dg, actw(lambda j, n, k: (j, k, 0)))], gsh, gspec,
             (N_SHARD, 1, nw_t), TN, (D_MODEL, FF_SH))
    gu = _mm(tag + "_dwu", [(h, hspec, du, actw(lambda j, n, k: (j, k, 0)))], gsh, gspec,
             (N_SHARD, 1, nw_t), TN, (D_MODEL, FF_SH))
    dg = emit(gg, gu, gd, dg)
    act = lambda j: pl.BlockSpec((None, ROW_T, FF_SH), lambda i, n, k: (j, i, 0))
    wsp = lambda j: pl.BlockSpec((None, D_MODEL, FF_SH), lambda i, n, k: (j, 0, 0))
    dh = _mm(tag + "_dh",
             [(dd, act(j), w, wsp(j)) for j in range(N_SHARD) for dd, w in ((dg, wg), (du, wu))],
             jax.ShapeDtypeStruct((T, D_MODEL), F32), pl.BlockSpec((ROW_T, D_MODEL), lambda i, n, k: (i, 0)),
             (nt, 1, 1), NT, (ROW_T, D_MODEL))
    return _rms_bwd(tag + "_rmsb", dh, x, nw, dxo)


def _seq_rows(ref, start, size, S):
    lo, hi = max(start, 0), min(start + size, S)
    parts = [ref[pl.ds(lo, hi - lo), :]]
    if lo > start:
        parts.insert(0, jnp.zeros((lo - start, ref.shape[1]), F32))
    if start + size > hi:
        parts.append(jnp.zeros((start + size - hi, ref.shape[1]), F32))
    return parts[0] if len(parts) == 1 else jnp.concatenate(parts, axis=0)


XBC_CB = COL_XBC // CONV_CT


def _conv_fwd(name, proj, w, b, B):
    T = proj.shape[0]
    S = T // B
    C = CONV_DIM

    def body(x_ref, w_ref, b_ref, o_ref):
        wv = w_ref[...]
        for c in range(S // CONV_R):
            r0 = c * CONV_R
            ch = _seq_rows(x_ref, r0 - PAD_R, CONV_R + PAD_R, S)
            pre = ch[PAD_R:] * wv[3:4] + b_ref[...]
            for s in range(1, CONV_K):
                pre = pre + pltpu.roll(ch, s, axis=0)[PAD_R:] * wv[3 - s:4 - s]
            o_ref[pl.ds(r0, CONV_R), :] = pre * _sigmoid(pre)

    return pl.pallas_call(
        body, out_shape=jax.ShapeDtypeStruct((T, C), F32), grid=(B, C // CONV_CT),
        in_specs=[pl.BlockSpec((S, CONV_CT), lambda bi, ci: (bi, XBC_CB + ci)),
                  pl.BlockSpec((CONV_K, CONV_CT), lambda bi, ci: (0, ci)),
                  pl.BlockSpec((1, CONV_CT), lambda bi, ci: (0, ci))],
        out_specs=pl.BlockSpec((S, CONV_CT), lambda bi, ci: (bi, ci)), name=name,
        compiler_params=_cp("parallel", "parallel"))(proj, w, b)


def _conv_bwd(name, proj, dxs, dB, dC, w, b, dproj, B):
    T = proj.shape[0]
    S = T // B
    C = CONV_DIM
    RW = CONV_R + PAD_R
    nx, nb = dxs.shape[1] // CONV_CT, dB.shape[1] // CONV_CT

    def body(x_ref, dx_in, db_in, dc_in, w_ref, b_ref, buf_ref, dx_ref, dw_ref, db_ref):
        @pl.when(pl.program_id(1) == 0)
        def _():
            dw_ref[...] = jnp.zeros_like(dw_ref)
            db_ref[...] = jnp.zeros_like(db_ref)

        ci = pl.program_id(0)
        wv = w_ref[...]
        dw = [jnp.zeros((1, CONV_CT), F32) for _ in range(CONV_K)]
        db = jnp.zeros((1, CONV_CT), F32)
        for c in range(S // CONV_R):
            r0 = c * CONV_R
            ch = _seq_rows(x_ref, r0 - PAD_R, RW + PAD_R, S)
            xs = [ch[PAD_R:]] + [pltpu.roll(ch, s, axis=0)[PAD_R:] for s in range(1, CONV_K)]
            pre = b_ref[...] + xs[0] * wv[3:4]
            for s in range(1, CONV_K):
                pre = pre + xs[s] * wv[3 - s:4 - s]
            sg = _sigmoid(pre)
            dout = jnp.where(ci < nx, _seq_rows(dx_in, r0, RW, S),
                             jnp.where(ci < nx + nb, _seq_rows(db_in, r0, RW, S), _seq_rows(dc_in, r0, RW, S)))
            dpre = dout * (sg * (1.0 + pre * (1.0 - sg)))
            dx = dpre[:CONV_R] * wv[3:4]
            for s in range(1, CONV_K):
                dx = dx + pltpu.roll(dpre, RW - s, axis=0)[:CONV_R] * wv[3 - s:4 - s]
            dx_ref[pl.ds(r0, CONV_R), :] = dx.astype(BF16)
            dcur = dpre[:CONV_R]
            db = db + jnp.sum(dcur, axis=0, keepdims=True)
            for s in range(CONV_K):
                dw[3 - s] = dw[3 - s] + jnp.sum(dcur * xs[s][:CONV_R], axis=0, keepdims=True)
        db_ref[...] += db
        for k in range(CONV_K):
            dw_ref[k:k + 1, :] += dw[k]

    seq = lambda f: pl.BlockSpec((S, CONV_CT), f)
    return pl.pallas_call(
        body,
        out_shape=(jax.ShapeDtypeStruct(dproj.shape, dproj.dtype), jax.ShapeDtypeStruct((CONV_K, C), F32),
                   jax.ShapeDtypeStruct((1, C), F32)),
        grid=(C // CONV_CT, B),
        in_specs=[seq(lambda ci, bi: (bi, XBC_CB + ci)),
                  seq(lambda ci, bi: (bi, jnp.minimum(ci, nx - 1))),
                  seq(lambda ci, bi: (bi, jnp.clip(ci - nx, 0, nb - 1))),
                  seq(lambda ci, bi: (bi, jnp.clip(ci - nx - nb, 0, nb - 1))),
                  pl.BlockSpec((CONV_K, CONV_CT), lambda ci, bi: (0, ci)),
                  pl.BlockSpec((1, CONV_CT), lambda ci, bi: (0, ci)), ANY],
        out_specs=(seq(lambda ci, bi: (bi, XBC_CB + ci)),
                   pl.BlockSpec((CONV_K, CONV_CT), lambda ci, bi: (0, ci)),
                   pl.BlockSpec((1, CONV_CT), lambda ci, bi: (0, ci))),
        input_output_aliases={6: 0},
        name=name, compiler_params=_cp("parallel", "arbitrary"))(proj, dxs, dB, dC, w, b, dproj)


def _tri_sum(tri, x, dims, tri_first):
    hi = x.astype(BF16)
    r1 = x - hi.astype(F32)
    mid = r1.astype(BF16)
    lo = (r1 - mid.astype(F32)).astype(BF16)
    out = None
    for part in (hi, mid, lo):
        d = _dot(tri, part, dims) if tri_first else _dot(part, tri, dims)
        out = d if out is None else out + d
    return out


def _total(x):
    return jnp.sum(jnp.sum(x, axis=0, keepdims=True), axis=-1, keepdims=True)


def _ssd_common(dtc_ref, dtr_ref, pcol_ref, prow_ref, b_ref, c_ref):
    L = SSD_L
    bias_c, alog_c = pcol_ref[0:1, :], pcol_ref[1:2, :]
    a_c = -jnp.exp(alog_c)
    dt_c = _softplus(dtc_ref[...] + bias_c)
    row = lax.broadcasted_iota(jnp.int32, (L, L), 0)
    col = lax.broadcasted_iota(jnp.int32, (L, L), 1)
    causal = row >= col
    tri = causal.astype(BF16)
    cum_c = _tri_sum(tri, dt_c * a_c, NN, True)
    a_r = -jnp.exp(prow_ref[:, 1:2])
    dt_r = _softplus(dtr_ref[...] + prow_ref[:, 0:1])
    cum_r = _tri_sum(tri, dt_r * a_r, NT, False)
    bb = b_ref[...].astype(BF16)
    cb = c_ref[...].astype(BF16)
    G = _dot(cb, bb, NT)
    return a_c, dt_c, causal, tri, cum_c, cum_r, bb, cb, G


def _ssd_fwd(name, xc, proj, dtc, dtr, pcol, prow, nw, B):
    T = xc.shape[0]
    S = T // B
    nb = S // SSD_L
    L = SSD_L

    def body(xs_ref, b_ref, c_ref, z_ref, dtc_ref, dtr_ref, pcol_ref, prow_ref, nw_ref, y_ref, yn_ref, hs_ref, H, yo_s):
        @pl.when(pl.program_id(2) == 0)
        def _():
            H[...] = jnp.zeros_like(H)

        a_c, dt_c, causal, tri, cum_c, cum_r, bb, cb, G = _ssd_common(dtc_ref, dtr_ref, pcol_ref, prow_ref, b_ref, c_ref)
        dsk = pcol_ref[2:3, :]
        clast = cum_c[L - 1:L, :]
        bf = b_ref[...]
        for h in range(4):
            hs_ref[h] = H[h]
            yo_s[h] = _dot(cb, H[h].astype(BF16), NN)
        for h in range(4):
            sl = slice(HEAD_DIM * h, HEAD_DIM * (h + 1))
            cc = cum_c[:, h:h + 1]
            lm = jnp.exp(jnp.where(causal, cc - cum_r[h:h + 1, :], NEG))
            M = (G * lm).astype(BF16)
            xh = xs_ref[:, sl]
            Xb = (xh * dt_c[:, h:h + 1]).astype(BF16)
            Hh = H[h]
            y = _dot(M, Xb, NN) + jnp.exp(cc) * yo_s[h]
            y_ref[:, sl] = y + dsk[:, h:h + 1] * xh
            cl = clast[:, h:h + 1]
            Bw = (bf * jnp.exp(cl - cc)).astype(BF16)
            H[h] = jnp.exp(cl) * Hh + _dot(Bw, Xb, TN)
        zv = z_ref[...]
        y2 = y_ref[...] * (zv * _sigmoid(zv))
        r = lax.rsqrt(jnp.mean(y2 * y2, axis=-1, keepdims=True) + EPS)
        yn_ref[...] = (y2 * r * nw_ref[...]).astype(BF16)

    rowi = lambda b, g, i: b * nb + i
    grp = pl.BlockSpec((L, GROUP_W), lambda b, g, i: (rowi(b, g, i), g))
    return pl.pallas_call(
        body,
        out_shape=(jax.ShapeDtypeStruct((T, 1024), F32), jax.ShapeDtypeStruct((T, 1024), BF16),
                   jax.ShapeDtypeStruct((B, SSD_GROUPS, nb, 4, SSD_STATE, HEAD_DIM), F32)),
        grid=(B, SSD_GROUPS, nb),
        in_specs=[grp,
                  pl.BlockSpec((L, SSD_STATE), lambda b, g, i: (rowi(b, g, i), 8 + g)),
                  pl.BlockSpec((L, SSD_STATE), lambda b, g, i: (rowi(b, g, i), 12 + g)),
                  grp,
                  pl.BlockSpec((None, L, 4), lambda b, g, i: (g, rowi(b, g, i), 0)),
                  pl.BlockSpec((None, 4, L), lambda b, g, i: (g, 0, rowi(b, g, i))),
                  pl.BlockSpec((None, 3, 4), lambda b, g, i: (g, 0, 0)),
                  pl.BlockSpec((None, 4, 3), lambda b, g, i: (g, 0, 0)),
                  pl.BlockSpec((1, GROUP_W), lambda b, g, i: (0, g))],
        out_specs=(grp, grp,
                   pl.BlockSpec((None, None, None, 4, SSD_STATE, HEAD_DIM), lambda b, g, i: (b, g, i, 0, 0, 0))),
        scratch_shapes=[pltpu.VMEM((4, SSD_STATE, HEAD_DIM), F32), pltpu.VMEM((4, L, HEAD_DIM), F32)], name=name,
        compiler_params=_cp("parallel", "parallel", "arbitrary"))(xc, xc, xc, proj, dtc, dtr, pcol, prow, nw)


def _ssd_bwd(name, dyn, Y, xc, proj, dtc, dtr, pcol, prow, nw, hs, dproj, B):
    T = xc.shape[0]
    S = T // B
    nb = S // SSD_L
    L = SSD_L

    def body(dyn_ref, y_ref, xs_ref, b_ref, c_ref, z_ref, dtc_ref, dtr_ref, pcol_ref, prow_ref, nw_ref, hs_ref, buf_ref,
             dxs_ref, db_ref, dc_ref, dz_ref, ddt_ref, dpar_ref, dnw_ref, dH, dm_s, dxo_s, ea_s, ex_s):
        @pl.when(pl.program_id(2) == 0)
        def _():
            dH[...] = jnp.zeros_like(dH)
            dpar_ref[...] = jnp.zeros_like(dpar_ref)
            dnw_ref[...] = jnp.zeros_like(dnw_ref)

        a_c, dt_c, causal, tri, cum_c, cum_r, bb, cb, G = _ssd_common(dtc_ref, dtr_ref, pcol_ref, prow_ref, b_ref, c_ref)
        dsk = pcol_ref[2:3, :]
        clast = cum_c[L - 1:L, :]
        bf = b_ref[...]
        cf = c_ref[...]
        Yv = y_ref[...]
        zv = z_ref[...]
        sz = _sigmoid(zv)
        silu = zv * sz
        y2 = Yv * silu
        r = lax.rsqrt(jnp.mean(y2 * y2, axis=-1, keepdims=True) + EPS)
        yhat = y2 * r
        dyv = dyn_ref[...]
        dnw_ref[...] += jnp.sum(dyv * yhat, axis=0, keepdims=True)
        dyhat = dyv * nw_ref[...]
        dy2 = r * (dyhat - yhat * jnp.mean(dyhat * yhat, axis=-1, keepdims=True))
        dY = dy2 * silu
        dz_ref[...] = (dy2 * Yv * (sz * (1.0 + zv * (1.0 - sz)))).astype(BF16)

        lane4 = lax.broadcasted_iota(jnp.int32, (1, 4), 1)
        dG = jnp.zeros((L, L), F32)
        dBs = jnp.zeros((L, SSD_STATE), F32)
        dCs = jnp.zeros((L, SSD_STATE), F32)
        ddsk = jnp.zeros((1, 4), F32)
        dcl = jnp.zeros((1, 4), F32)
        for h in range(4):
            sl = slice(HEAD_DIM * h, HEAD_DIM * (h + 1))
            xb = (xs_ref[:, sl] * dt_c[:, h:h + 1]).astype(BF16)
            dm_s[h] = _dot(dY[:, sl].astype(BF16), xb, NT)
            dxo_s[h] = _dot(bb, dH[h].astype(BF16), NN)
        for h in range(4):
            sl = slice(HEAD_DIM * h, HEAD_DIM * (h + 1))
            onehot = (lane4 == h).astype(F32)
            cc = cum_c[:, h:h + 1]
            cl = clast[:, h:h + 1]
            lm = jnp.exp(jnp.where(causal, cc - cum_r[h:h + 1, :], NEG))
            M = (G * lm).astype(BF16)
            xh = xs_ref[:, sl]
            dth = dt_c[:, h:h + 1]
            X = xh * dth
            Xb = X.astype(BF16)
            dYh = dY[:, sl]
            dYb = dYh.astype(BF16)
            Hb = hs_ref[h].astype(BF16)
            dHh = dH[h]
            dHb = dHh.astype(BF16)
            alpha = jnp.exp(cc)
            beta = jnp.exp(cl - cc)
            dXoff = beta * dxo_s[h]
            dX = _dot(M, dYb, TN) + dXoff
            dG = dG + dm_s[h] * lm
            dCs = dCs + _dot((alpha * dYh).astype(BF16), Hb, NT)
            dBs = dBs + _dot((beta * X).astype(BF16), dHb, NT)
            ypre = Yv[:, sl] - dsk[:, h:h + 1] * xh
            ea_s[:, sl] = dYb.astype(F32) * ypre - Xb.astype(F32) * dX
            ex_s[:, sl] = dX * xh
            dcl_h = (_total(dHh * (jnp.exp(cl) * hs_ref[h])) + _total(Xb.astype(F32) * dXoff))
            dcl = dcl + dcl_h * onehot
            ddsk = ddsk + _total(dYh * xh) * onehot
            dxs_ref[:, sl] = dsk[:, h:h + 1] * dYh + dX * dth
            dH[h] = jnp.exp(cl) * dHh + _dot((alpha * cf).astype(BF16), dYb, TN)
        dGb = dG.astype(BF16)
        dc_ref[...] = _dot(dGb, bb, NN) + dCs
        db_ref[...] = _dot(dGb, cb, TN) + dBs
        feat = lax.broadcasted_iota(jnp.int32, (GROUP_W, 4), 0)
        head = lax.broadcasted_iota(jnp.int32, (GROUP_W, 4), 1) * HEAD_DIM
        sel = ((feat >= head) & (feat < head + HEAD_DIM)).astype(BF16)
        dA = _tri_sum(sel, ea_s[...], NN, False)
        ddtx = _tri_sum(sel, ex_s[...], NN, False)
        last = lax.broadcasted_iota(jnp.int32, (L, 1), 0) == L - 1
        dA = dA + jnp.where(last, dcl, 0.0)
        dadt = _tri_sum(tri, dA, TN, True)
        ddt = dadt * a_c + ddtx
        d_a = jnp.sum(dadt * dt_c, axis=0, keepdims=True)
        ddraw = ddt * _sigmoid(dtc_ref[...] + pcol_ref[0:1, :])
        ddt_ref[...] = ddraw
        dpar_ref[0:1, :] += jnp.sum(ddraw, axis=0, keepdims=True)
        dpar_ref[1:2, :] += d_a * a_c
        dpar_ref[2:3, :] += ddsk

    rowi = lambda b, g, i: b * nb + (nb - 1 - i)
    grp = pl.BlockSpec((L, GROUP_W), lambda b, g, i: (rowi(b, g, i), g))
    st = pl.BlockSpec((L, SSD_STATE), lambda b, g, i: (rowi(b, g, i), g))
    f = jax.ShapeDtypeStruct
    return pl.pallas_call(
        body,
        out_shape=(f((T, 1024), F32), f((T, 512), F32), f((T, 512), F32), f(dproj.shape, dproj.dtype),
                   f((SSD_GROUPS, T, 4), F32), f((B, SSD_GROUPS, 3, 4), F32), f((B, 1, 1024), F32)),
        grid=(B, SSD_GROUPS, nb),
        in_specs=[grp, grp, grp,
                  pl.BlockSpec((L, SSD_STATE), lambda b, g, i: (rowi(b, g, i), 8 + g)),
                  pl.BlockSpec((L, SSD_STATE), lambda b, g, i: (rowi(b, g, i), 12 + g)),
                  grp,
                  pl.BlockSpec((None, L, 4), lambda b, g, i: (g, rowi(b, g, i), 0)),
                  pl.BlockSpec((None, 4, L), lambda b, g, i: (g, 0, rowi(b, g, i))),
                  pl.BlockSpec((None, 3, 4), lambda b, g, i: (g, 0, 0)),
                  pl.BlockSpec((None, 4, 3), lambda b, g, i: (g, 0, 0)),
                  pl.BlockSpec((1, GROUP_W), lambda b, g, i: (0, g)),
                  pl.BlockSpec((None, None, None, 4, SSD_STATE, HEAD_DIM), lambda b, g, i: (b, g, nb - 1 - i, 0, 0, 0)),
                  ANY],
        out_specs=(grp, st, st, grp,
                   pl.BlockSpec((None, L, 4), lambda b, g, i: (g, rowi(b, g, i), 0)),
                   pl.BlockSpec((None, None, 3, 4), lambda b, g, i: (b, g, 0, 0)),
                   pl.BlockSpec((None, 1, GROUP_W), lambda b, g, i: (b, 0, g))),
        input_output_aliases={12: 3},
        scratch_shapes=[pltpu.VMEM((4, SSD_STATE, HEAD_DIM), F32), pltpu.VMEM((4, L, L), F32),
                        pltpu.VMEM((4, L, HEAD_DIM), F32), pltpu.VMEM((L, GROUP_W), F32),
                        pltpu.VMEM((L, GROUP_W), F32)], name=name,
        compiler_params=_cp("parallel", "parallel", "arbitrary"))(
            dyn, Y, xc, xc, xc, proj, dtc, dtr, pcol, prow, nw, hs, dproj)


def _headnorm_fwd(name, proj, col_block, w):
    T = proj.shape[0]

    def body(x_ref, w_ref, o_ref):
        for h in range(ATT_HEADS):
            sl = slice(HEAD_DIM * h, HEAD_DIM * (h + 1))
            xh = x_ref[:, sl]
            r = lax.rsqrt(jnp.mean(xh * xh, axis=-1, keepdims=True) + EPS)
            o_ref[:, sl] = (xh * r * w_ref[...]).astype(BF16)

    return pl.pallas_call(
        body, out_shape=jax.ShapeDtypeStruct((T, 1024), BF16), grid=(T // ROW_T,),
        in_specs=[pl.BlockSpec((ROW_T, 1024), lambda i: (i, col_block)), pl.BlockSpec((1, HEAD_DIM), lambda i: (0, 0))],
        out_specs=pl.BlockSpec((ROW_T, 1024), lambda i: (i, 0)), name=name, compiler_params=_cp("parallel"))(proj, w)


def _headnorm_bwd(name, dn, proj, col_block, w, dproj):
    T = proj.shape[0]

    def body(dn_ref, x_ref, w_ref, buf_ref, dx_ref, dw_ref):
        @pl.when(pl.program_id(0) == 0)
        def _():
            dw_ref[...] = jnp.zeros_like(dw_ref)

        dw = jnp.zeros((1, HEAD_DIM), F32)
        for h in range(ATT_HEADS):
            sl = slice(HEAD_DIM * h, HEAD_DIM * (h + 1))
            xh = x_ref[:, sl]
            r = lax.rsqrt(jnp.mean(xh * xh, axis=-1, keepdims=True) + EPS)
            xhat = xh * r
            dnh = dn_ref[:, sl]
            dxhat = dnh * w_ref[...]
            dx_ref[:, sl] = (r * (dxhat - xhat * jnp.mean(dxhat * xhat, axis=-1, keepdims=True))).astype(BF16)
            dw = dw + jnp.sum(dnh * xhat, axis=0, keepdims=True)
        dw_ref[...] += dw

    here = pl.BlockSpec((ROW_T, 1024), lambda i: (i, col_block))
    return pl.pallas_call(
        body, out_shape=(jax.ShapeDtypeStruct(dproj.shape, dproj.dtype), jax.ShapeDtypeStruct((1, HEAD_DIM), F32)),
        grid=(T // ROW_T,),
        in_specs=[pl.BlockSpec((ROW_T, 1024), lambda i: (i, 0)), here, pl.BlockSpec((1, HEAD_DIM), lambda i: (0, 0)), ANY],
        out_specs=(here, pl.BlockSpec((1, HEAD_DIM), lambda i: (0, 0))), input_output_aliases={3: 0},
        name=name, compiler_params=_cp("arbitrary"))(dn, proj, w, dproj)


def _att_bias(nq):
    j = np.arange(ATT_B)[:, None]
    i = np.arange(ATT_B)[None, :]
    out = np.empty((nq, ATT_B, ATT_B), np.float32)
    for dblk in range(nq):
        dl = ATT_B * dblk + i - j
        cnt = ((dl >= 0) & (dl <= 128)).astype(np.float32)
        cnt += ((dl >= 0) & (dl % 4 == 0) & (dl <= 512))
        cnt += ((dl >= 0) & (dl % 16 == 0) & (dl <= 2048))
        out[dblk] = np.where(cnt > 0, np.log(np.maximum(cnt, 1.0)), NEG)
    return jnp.asarray(out)


def _row_pair(nq):
    def f(r, c):
        first = c <= r
        return jnp.where(first, r, nq - 1 - r), jnp.where(first, c, c - (r + 1))
    return f


def _col_pair(nq):
    def f(r, c):
        first = c < nq - r
        kj = jnp.where(first, r, nq - 1 - r)
        return jnp.where(first, r + c, nq - 1 - r + (c - (nq - r))), kj
    return f


ATT_SCALE = 1.0 / math.sqrt(HEAD_DIM)
ATT_HS = 4
ATT_W = ATT_HS * HEAD_DIM


def _att_maps(nq, qk):
    return dict(
        q_tok=lambda b, g, r, c: (b * nq + qk(r, c)[0], g),
        k_tok=lambda b, g, r, c: (b * nq + qk(r, c)[1], g),
        v_tok=lambda b, g, r, c: (b * nq + qk(r, c)[1], COL_V // ATT_W + g),
        q_feat=lambda b, g, r, c: (g, b * nq + qk(r, c)[0]),
        k_feat=lambda b, g, r, c: (g, b * nq + qk(r, c)[1]),
        bias=lambda b, g, r, c: (qk(r, c)[0] - qk(r, c)[1], 0, 0),
        lse=lambda b, g, r, c: (g, 0, b * nq + qk(r, c)[0]),
        do_tok=lambda b, g, r, c: (b * nq + qk(r, c)[0], ATT_HS + g))


def _att_fwd(name, kn, qT, vT, bias, B):
    T = kn.shape[0]
    nq = (T // B) // ATT_B
    qk = _row_pair(nq)
    mp = _att_maps(nq, qk)

    def body(k_ref, qT_ref, vT_ref, bias_ref, oT_ref, lse_ref, m_s, l_s, acc_s, s_s):
        qi, kj = qk(pl.program_id(2), pl.program_id(3))

        @pl.when(kj == 0)
        def _():
            m_s[...] = jnp.full_like(m_s, NEG)
            l_s[...] = jnp.zeros_like(l_s)
            acc_s[...] = jnp.zeros_like(acc_s)

        bv = bias_ref[...]
        for h in range(ATT_HS):
            rs = slice(HEAD_DIM * h, HEAD_DIM * (h + 1))
            s_s[h] = _dot(k_ref[:, rs], qT_ref[rs, :], NN)
        for h in range(ATT_HS):
            rs = slice(HEAD_DIM * h, HEAD_DIM * (h + 1))
            s = s_s[h] + bv
            m_prev = m_s[h:h + 1, :]
            m_new = jnp.maximum(m_prev, jnp.max(s, axis=0, keepdims=True))
            alpha = jnp.exp(m_prev - m_new)
            p = jnp.exp(s - m_new)
            l_s[h:h + 1, :] = alpha * l_s[h:h + 1, :] + jnp.sum(p, axis=0, keepdims=True)
            acc_s[rs, :] = alpha * acc_s[rs, :] + _dot(vT_ref[rs, :], p.astype(BF16), NN)
            m_s[h:h + 1, :] = m_new

        @pl.when(kj == qi)
        def _():
            for h in range(ATT_HS):
                rs = slice(HEAD_DIM * h, HEAD_DIM * (h + 1))
                oT_ref[rs, :] = (acc_s[rs, :] / l_s[h:h + 1, :]).astype(BF16)
            lse_ref[...] = m_s[...] + jnp.log(l_s[...])

    tok = (ATT_B, ATT_W)
    feat = (ATT_W, ATT_B)
    return pl.pallas_call(
        body,
        out_shape=(jax.ShapeDtypeStruct((1024, T), BF16), jax.ShapeDtypeStruct((ATT_HEADS // ATT_HS, ATT_HS, T), F32)),
        grid=(B, ATT_HEADS // ATT_HS, nq // 2, nq + 1),
        in_specs=[pl.BlockSpec(tok, mp["k_tok"]), pl.BlockSpec(feat, mp["q_feat"]), pl.BlockSpec(feat, mp["k_feat"]),
                  pl.BlockSpec((None, ATT_B, ATT_B), mp["bias"])],
        out_specs=(pl.BlockSpec(feat, mp["q_feat"]), pl.BlockSpec((None, ATT_HS, ATT_B), mp["lse"])),
        scratch_shapes=[pltpu.VMEM((ATT_HS, ATT_B), F32), pltpu.VMEM((ATT_HS, ATT_B), F32),
                        pltpu.VMEM((ATT_W, ATT_B), F32), pltpu.VMEM((ATT_HS, ATT_B, ATT_B), F32)],
        name=name, compiler_params=_cp("parallel", "parallel", "arbitrary", "arbitrary"))(kn, qT, vT, bias)


def _att_scores(k_ref, qT_ref, v_ref, doT_ref, s_s, dp_s):
    for h in range(ATT_HS):
        rs = slice(HEAD_DIM * h, HEAD_DIM * (h + 1))
        s_s[h] = _dot(k_ref[:, rs], qT_ref[rs, :], NN)
        dp_s[h] = _dot(v_ref[:, rs].astype(BF16), doT_ref[rs, :].astype(BF16), NN)


def _att_p_ds(s_s, dp_s, doT_ref, oT_ref, lse_ref, bv, h):
    rs = slice(HEAD_DIM * h, HEAD_DIM * (h + 1))
    delta = jnp.sum(doT_ref[rs, :] * oT_ref[rs, :].astype(F32), axis=0, keepdims=True)
    p = jnp.exp(s_s[h] + bv - lse_ref[h:h + 1, :])
    return p, p * (dp_s[h] - delta)


def _att_bwd_dq(name, kn, qT, vb, knT, bias, doT, oT, lse, B):
    T = kn.shape[0]
    nq = (T // B) // ATT_B
    qk = _row_pair(nq)
    mp = _att_maps(nq, qk)

    def body(k_ref, qT_ref, v_ref, kT_ref, bias_ref, doT_ref, oT_ref, lse_ref, dqT_ref, acc_s, s_s, dp_s):
        qi, kj = qk(pl.program_id(2), pl.program_id(3))

        @pl.when(kj == 0)
        def _():
            acc_s[...] = jnp.zeros_like(acc_s)

        bv = bias_ref[...]
        _att_scores(k_ref, qT_ref, v_ref, doT_ref, s_s, dp_s)
        for h in range(ATT_HS):
            rs = slice(HEAD_DIM * h, HEAD_DIM * (h + 1))
            p, ds = _att_p_ds(s_s, dp_s, doT_ref, oT_ref, lse_ref, bv, h)
            acc_s[rs, :] += _dot(kT_ref[rs, :], ds.astype(BF16), NN)

        @pl.when(kj == qi)
        def _():
            dqT_ref[...] = acc_s[...] * ATT_SCALE

    tok = (ATT_B, ATT_W)
    feat = (ATT_W, ATT_B)
    return pl.pallas_call(
        body, out_shape=jax.ShapeDtypeStruct((1024, T), F32), grid=(B, ATT_HEADS // ATT_HS, nq // 2, nq + 1),
        in_specs=[pl.BlockSpec(tok, mp["k_tok"]), pl.BlockSpec(feat, mp["q_feat"]), pl.BlockSpec(tok, mp["v_tok"]),
                  pl.BlockSpec(feat, mp["k_feat"]), pl.BlockSpec((None, ATT_B, ATT_B), mp["bias"]),
                  pl.BlockSpec(feat, mp["q_feat"]), pl.BlockSpec(feat, mp["q_feat"]),
                  pl.BlockSpec((None, ATT_HS, ATT_B), mp["lse"])],
        out_specs=pl.BlockSpec(feat, mp["q_feat"]),
        scratch_shapes=[pltpu.VMEM((ATT_W, ATT_B), F32), pltpu.VMEM((ATT_HS, ATT_B, ATT_B), F32),
                        pltpu.VMEM((ATT_HS, ATT_B, ATT_B), F32)],
        name=name, compiler_params=_cp("parallel", "parallel", "arbitrary", "arbitrary"))(
            kn, qT, vb, knT, bias, doT, oT, lse)


def _att_bwd_dkv(name, kn, qT, vb, qn, bias, doT, oT, lse, dyn, dproj, B):
    T = kn.shape[0]
    nq = (T // B) // ATT_B
    qk = _col_pair(nq)
    mp = _att_maps(nq, qk)

    def body(k_ref, qT_ref, v_ref, q_ref, bias_ref, doT_ref, oT_ref, lse_ref, do_ref, buf_ref, dk_ref, dv_ref, dk_s, dv_s,
             s_s, dp_s):
        qi, kj = qk(pl.program_id(2), pl.program_id(3))

        @pl.when(qi == kj)
        def _():
            dk_s[...] = jnp.zeros_like(dk_s)
            dv_s[...] = jnp.zeros_like(dv_s)

        bv = bias_ref[...]
        _att_scores(k_ref, qT_ref, v_ref, doT_ref, s_s, dp_s)
        for h in range(ATT_HS):
            rs = slice(HEAD_DIM * h, HEAD_DIM * (h + 1))
            p, ds = _att_p_ds(s_s, dp_s, doT_ref, oT_ref, lse_ref, bv, h)
            dv_s[h] += _dot(p.astype(BF16), do_ref[:, rs].astype(BF16), NN)
            dk_s[h] += _dot(ds.astype(BF16), q_ref[:, rs], NN)

        @pl.when(qi == nq - 1)
        def _():
            for h in range(ATT_HS):
                rs = slice(HEAD_DIM * h, HEAD_DIM * (h + 1))
                dk_ref[:, rs] = dk_s[h] * ATT_SCALE
                dv_ref[:, rs] = dv_s[h].astype(BF16)

    tok = (ATT_B, ATT_W)
    feat = (ATT_W, ATT_B)
    v_cb = COL_V // ATT_W
    return pl.pallas_call(
        body, out_shape=(jax.ShapeDtypeStruct((T, 1024), F32), jax.ShapeDtypeStruct(dproj.shape, dproj.dtype)),
        grid=(B, ATT_HEADS // ATT_HS, nq // 2, nq + 1),
        in_specs=[pl.BlockSpec(tok, mp["k_tok"]), pl.BlockSpec(feat, mp["q_feat"]), pl.BlockSpec(tok, mp["v_tok"]),
                  pl.BlockSpec(tok, mp["q_tok"]), pl.BlockSpec((None, ATT_B, ATT_B), mp["bias"]),
                  pl.BlockSpec(feat, mp["q_feat"]), pl.BlockSpec(feat, mp["q_feat"]),
                  pl.BlockSpec((None, ATT_HS, ATT_B), mp["lse"]), pl.BlockSpec(tok, mp["do_tok"]), ANY],
        out_specs=(pl.BlockSpec(tok, mp["k_tok"]),
                   pl.BlockSpec(tok, lambda b, g, r, c: (b * nq + qk(r, c)[1], v_cb + g))),
        input_output_aliases={9: 1},
        scratch_shapes=[pltpu.VMEM((ATT_HS, ATT_B, HEAD_DIM), F32), pltpu.VMEM((ATT_HS, ATT_B, HEAD_DIM), F32),
                        pltpu.VMEM((ATT_HS, ATT_B, ATT_B), F32), pltpu.VMEM((ATT_HS, ATT_B, ATT_B), F32)],
        name=name, compiler_params=_cp("parallel", "parallel", "arbitrary", "arbitrary"))(
            kn, qT, vb, qn, bias, doT, oT, lse, dyn, dproj)


def _group_cols(v):
    return v.reshape(SSD_GROUPS, 4)


def _ssd_params(p):
    rows = jnp.stack([_group_cols(p["dt_bias"]), _group_cols(p["a_log"]), _group_cols(p["d_skip"])], axis=1)
    return rows, jnp.swapaxes(rows, 1, 2)


def _dymix(name, dx, wout):
    T = dx.shape[0]

    def body(dx_ref, w_ref, o_ref):
        dxb = dx_ref[...].astype(BF16)
        for n in range(N_SHARD):
            o_ref[:, MIX_SH * n:MIX_SH * (n + 1)] = _dot(dxb, w_ref[n], NT)

    return pl.pallas_call(
        body, out_shape=jax.ShapeDtypeStruct((T, MIX_W), F32), grid=(T // ROW_T,),
        in_specs=[pl.BlockSpec((ROW_T, D_MODEL), lambda i: (i, 0)),
                  pl.BlockSpec((N_SHARD, MIX_SH, D_MODEL), lambda i: (0, 0, 0))],
        out_specs=pl.BlockSpec((ROW_T, MIX_W), lambda i: (i, 0)), name=name, compiler_params=_cp("parallel"))(dx, wout)


def _mixer_fwd(tag, x1, p, weights, bias, B):
    T = x1.shape[0]
    S = T // B
    nt = T // ROW_T
    h2 = _rms_fwd(tag + "_mixrms", x1, p["mix_norm"][None])
    wi = weights("win", h2)
    win, cw = wi["win"], wi["cw"]
    proj = _mm(tag + "_proj",
               [(h2, pl.BlockSpec((ROW_T, D_MODEL), lambda j, i, k: (i, 0)),
                 win, pl.BlockSpec((D_MODEL, PROJ_TN), lambda j, i, k: (0, j)))],
               jax.ShapeDtypeStruct((T, IN_PAD), F32), pl.BlockSpec((ROW_T, PROJ_TN), lambda j, i, k: (i, j)),
               (IN_PAD // PROJ_TN, nt, 1), NN, (ROW_T, PROJ_TN))
    xc = _conv_fwd(tag + "_conv", proj, cw, p["conv_b"][None], B)
    dtraw = proj[:, COL_DT:COL_DT + SSD_HEADS].reshape(T, SSD_GROUPS, 4)
    dtc = jnp.transpose(dtraw, (1, 0, 2))
    dtr = jnp.transpose(dtraw, (1, 2, 0))
    pcol, prow = _ssd_params(p)
    Y, y_ssd, hs = _ssd_fwd(tag + "_ssd", xc, proj, dtc, dtr, pcol, prow, p["ssd_norm"][None], B)
    qn = _headnorm_fwd(tag + "_qn", proj, COL_Q // 1024, p["q_norm"][None])
    kn = _headnorm_fwd(tag + "_kn", proj, COL_K // 1024, p["k_norm"][None])
    qT = (qn * ATT_SCALE).T
    oT, lse = _att_fwd(tag + "_att", kn, qT, proj[:, COL_V:COL_V + 1024].T.astype(BF16), bias, B)
    ymix = jnp.concatenate([y_ssd, oT.T], axis=1)
    rest = weights("rest", ymix)
    x2 = _mm(tag + "_out",
             [(ymix, pl.BlockSpec((ROW_T, MIX_SH), lambda i, n, k, j=j: (i, j)),
               rest["wout"], pl.BlockSpec((None, MIX_SH, D_MODEL), lambda i, n, k, j=j: (j, 0, 0)))
              for j in range(N_SHARD)],
             jax.ShapeDtypeStruct((T, D_MODEL), F32), pl.BlockSpec((ROW_T, D_MODEL), lambda i, n, k: (i, 0)),
             (nt, 1, 1), NN, (ROW_T, D_MODEL),
             res=(x1, pl.BlockSpec((ROW_T, D_MODEL), lambda i, n, k: (i, 0))))
    saved = dict(x1=x1, h2=h2, proj=proj, xc=xc, dtc=dtc, dtr=dtr, Y=Y, hs=hs,
                 qn=qn, kn=kn, qT=qT, oT=oT, lse=lse, ymix=ymix, win=win, cw=cw, wout=rest["wout"])
    return x2, saved


def _mixer_bwd(tag, dx2, sv, p, bias, B):
    T = dx2.shape[0]
    S = T // B
    nt = T // ROW_T
    sg = {}
    dymix = _dymix(tag + "_dymix", dx2, sv["wout"])
    tkw = min(TK_W, T)
    gwout = _mm(tag + "_dwout",
                [(sv["ymix"], pl.BlockSpec((tkw, MIX_SH), lambda m, n, k: (k, m)),
                  dx2, pl.BlockSpec((tkw, D_MODEL), lambda m, n, k: (k, 0)))],
                jax.ShapeDtypeStruct((N_SHARD, MIX_SH, D_MODEL), BF16),
                pl.BlockSpec((None, MIX_SH, D_MODEL), lambda m, n, k: (m, 0, 0)),
                (N_SHARD, 1, T // tkw), TN, (MIX_SH, D_MODEL))
    proj = sv["proj"]
    doT = dymix[:, 1024:].T
    dqn = _att_bwd_dq(tag + "_attdq", sv["kn"], sv["qT"], proj, sv["kn"].T, bias, doT, sv["oT"], sv["lse"], B).T
    dproj = lax.empty((T, IN_PAD), BF16)
    dkn, dproj = _att_bwd_dkv(tag + "_attdkv", sv["kn"], sv["qT"], proj, sv["qn"], bias, doT, sv["oT"], sv["lse"],
                              dymix, dproj, B)
    dproj, sg["q_norm"] = _headnorm_bwd(tag + "_qnb", dqn, proj, COL_Q // 1024, p["q_norm"][None], dproj)
    dproj, sg["k_norm"] = _headnorm_bwd(tag + "_knb", dkn, proj, COL_K // 1024, p["k_norm"][None], dproj)
    pcol, prow = _ssd_params(p)
    dxs, dB, dC, dproj, ddt, dpar, dnw = _ssd_bwd(tag + "_ssdb", dymix, sv["Y"], sv["xc"], proj, sv["dtc"], sv["dtr"],
                                                  pcol, prow, p["ssd_norm"][None], sv["hs"], dproj, B)
    dpar = jnp.sum(dpar, axis=0)
    sg["dt_bias"] = dpar[:, 0, :].reshape(SSD_HEADS)
    sg["a_log"] = dpar[:, 1, :].reshape(SSD_HEADS)
    sg["d_skip"] = dpar[:, 2, :].reshape(SSD_HEADS)
    sg["ssd_norm"] = jnp.sum(dnw, axis=0)
    dproj, sg["conv_w"], sg["conv_b"] = _conv_bwd(tag + "_convb", proj, dxs, dB, dC, sv["cw"], p["conv_b"][None],
                                                  dproj, B)
    ddt16 = jnp.transpose(ddt, (1, 0, 2)).reshape(T, SSD_HEADS)
    dproj = lax.dynamic_update_slice(dproj, jnp.pad(ddt16, ((0, 0), (0, IN_PAD - COL_DT - SSD_HEADS))).astype(BF16),
                                     (0, COL_DT))
    win = sv["win"]
    gwin = _mm(tag + "_dwin",
               [(sv["h2"], pl.BlockSpec((tkw, D_MODEL), lambda n, m, k: (k, 0)),
                 dproj, pl.BlockSpec((tkw, PROJ_TN), lambda n, m, k: (k, n)))],
               jax.ShapeDtypeStruct((D_MODEL, IN_PAD), BF16), pl.BlockSpec((D_MODEL, PROJ_TN), lambda n, m, k: (0, n)),
               (IN_PAD // PROJ_TN, 1, T // tkw), TN, (D_MODEL, PROJ_TN))
    dh2 = _mm(tag + "_dh2",
              [(dproj, pl.BlockSpec((ROW_T, PROJ_TN), lambda i, n, k, j=j: (i, j)),
                win, pl.BlockSpec((D_MODEL, PROJ_TN), lambda i, n, k, j=j: (0, j))) for j in range(IN_PAD // PROJ_TN)],
              jax.ShapeDtypeStruct((T, D_MODEL), F32), pl.BlockSpec((ROW_T, D_MODEL), lambda i, n, k: (i, 0)),
              (nt, 1, 1), NT, (ROW_T, D_MODEL))
    dx1, sg["mix_norm"] = _rms_bwd(tag + "_mixrmsb", dh2, sv["x1"], p["mix_norm"][None], dx2)
    return dx1, sg, gwout, gwin


def _win_pack(w):
    return jnp.concatenate([w[:, :3072], w[:, 3088:], w[:, 3072:3088],
                            jnp.zeros((w.shape[0], IN_PAD - IN_PROJ), w.dtype)], axis=1)


def _win_unpack(g):
    return jnp.concatenate([g[:, :3072], g[:, COL_DT:COL_DT + SSD_HEADS], g[:, 3072:COL_DT]], axis=1)


DT_LO = IN_SH * 2 - COL_Q


def _win_from_shards(sh):
    main = IN_SH - DT_LO
    return jnp.concatenate([sh[0], sh[1][:, :main], sh[2][:, SSD_HEADS - DT_LO:], sh[3], sh[1][:, main:],
                            sh[2][:, :SSD_HEADS - DT_LO], jnp.zeros((sh.shape[1], IN_PAD - IN_PROJ), sh.dtype)], axis=1)


def _win_to_shards(g):
    main = IN_SH - DT_LO
    a, b = IN_SH + main, IN_SH + 2 * main
    return jnp.stack([g[:, :IN_SH],
                      jnp.concatenate([g[:, IN_SH:a], g[:, COL_DT:COL_DT + DT_LO]], axis=1),
                      jnp.concatenate([g[:, COL_DT + DT_LO:COL_DT + SSD_HEADS], g[:, a:b]], axis=1),
                      g[:, b:COL_DT]])


def _local_step(x, target, small, weights, scatter, B):
    T = x.shape[0]
    bias = _att_bias((T // B) // ATT_B)
    saved = []
    h = x
    for l in range(DEPTH):
        tag = "l%d" % l
        p = {k: v[l] for k, v in small.items()}
        w1 = weights(l, "ffn1", h)
        x1, ffn1 = _ffn_fwd(tag + "f1", h, p["ffn1_norm"][None], w1["g1"], w1["u1"], w1["d1"])
        x2, sv = _mixer_fwd(tag, x1, p, functools.partial(weights, l), bias, B)
        w2 = weights(l, "rest", x2)
        h, ffn2 = _ffn_fwd(tag + "f2", x2, p["ffn2_norm"][None], w2["g2"], w2["u2"], w2["d2"])
        saved.append((ffn1, sv, ffn2, w1, w2))
    d, lsum = _loss_grad("loss", h, target)
    sgrads = [None] * DEPTH
    for l in reversed(range(DEPTH)):
        tag = "l%db" % l
        p = {k: v[l] for k, v in small.items()}
        ffn1, sv, ffn2, w1, w2 = saved[l]
        sg = {}
        d, sg["ffn2_norm"] = _ffn_bwd(tag + "f2", d, ffn2, p["ffn2_norm"][None], w2["g2"], w2["u2"], w2["d2"],
                                      lambda gg, gu, gd, c, l=l: scatter(l, "ffn2", dict(g2=gg, u2=gu, d2=gd), c))
        d, sgm, gwout, gwin = _mixer_bwd(tag, d, sv, p, bias, B)
        sg.update(sgm)
        d = scatter(l, "mixer", dict(wout=gwout, win=gwin), d)
        d, sg["ffn1_norm"] = _ffn_bwd(tag + "f1", d, ffn1, p["ffn1_norm"][None], w1["g1"], w1["u1"], w1["d1"],
                                      lambda gg, gu, gd, c, l=l: scatter(l, "ffn1", dict(g1=gg, u1=gu, d1=gd), c))
        sgrads[l] = sg
    return lsum, d, sgrads


MESH = pl.DeviceIdType.MESH
ANY = pl.BlockSpec(memory_space=pl.ANY)


def _place():
    return lax.axis_index("x"), lax.axis_index("y"), lax.axis_index("c")


def _other_chips(x, y):
    return [(1 - x, y), (x, 1 - y), (1 - x, 1 - y)]


HBM = pl.BlockSpec(memory_space=pltpu.HBM)
SEM = pl.BlockSpec(memory_space=pltpu.SEMAPHORE)
EFFECT = pltpu.SideEffectType.DATAFLOW_SIDE_EFFECTING


def _hbm(a):
    return pltpu.with_memory_space_constraint(a, pltpu.HBM)


def _exchange(gather, src, land, send, recv, n, act):
    x, y, c = _place()
    for k, (px, py) in enumerate(_other_chips(x, y)):
        for a in range(n):
            if gather:
                s_out, d_out, d_in = src[a], land[a].at[2 * x + y], land[a].at[2 * px + py]
            else:
                s_out, d_out, d_in = src[a].at[2 * px + py], land[a].at[k], land[a].at[k]
            act(pltpu.make_async_remote_copy(
                src_ref=s_out, dst_ref=d_out if act is _start else d_in, send_sem=send.at[k * n + a],
                recv_sem=recv.at[k * n + a], device_id=(px, py, c), device_id_type=MESH))


def _start(cp):
    cp.start()


def _finish(cp):
    cp.wait_send()
    cp.wait_recv()


def _exchange_start(name, gather, srcs, carry):
    n = len(srcs)
    lands = [lax.empty(((N_SHARD,) + s.shape) if gather else ((3,) + s.shape[1:]), s.dtype) for s in srcs]

    def body(*refs):
        _exchange(gather, refs[:n], refs[n:2 * n], refs[2 * n + 1], refs[2 * n + 2], n, _start)

    ops = [_hbm(a) for a in list(srcs) + lands + [carry]]
    out = pl.pallas_call(
        body, name=name,
        out_shape=(pltpu.SemaphoreType.DMA((3 * n,)), pltpu.SemaphoreType.DMA((3 * n,)),
                   *[pltpu.HBM(a.shape, a.dtype) for a in ops]),
        in_specs=[HBM] * len(ops), out_specs=(SEM, SEM, *[HBM] * len(ops)),
        input_output_aliases={i: 2 + i for i in range(len(ops))},
        compiler_params=pltpu.CompilerParams(has_side_effects=EFFECT))(*ops)
    return dict(gather=gather, send=out[0], recv=out[1], srcs=list(out[2:2 + n]), lands=list(out[2 + n:2 + 2 * n])), out[-1]


def _exchange_wait(name, ex, after):
    n = len(ex["srcs"])
    gather = ex["gather"]

    def body(*refs):
        _exchange(gather, refs[:n], refs[n:2 * n], refs[2 * n], refs[2 * n + 1], n, _finish)

    ops = ex["srcs"] + ex["lands"]
    out = pl.pallas_call(
        body, name=name, out_shape=[pltpu.HBM(a.shape, a.dtype) for a in ops],
        in_specs=[HBM] * len(ops) + [SEM, SEM, ANY], out_specs=[HBM] * len(ops),
        input_output_aliases={i: i for i in range(len(ops))},
        compiler_params=pltpu.CompilerParams(has_side_effects=EFFECT))(*ops, ex["send"], ex["recv"], after)
    return list(out[:n]), list(out[n:])


def _swap_sibling(parts):
    n = len(parts)

    def body(*refs):
        src, dst = refs[:n], refs[n:2 * n]
        send, recv = refs[2 * n:]
        x, y, c = _place()
        cps = [pltpu.make_async_remote_copy(src_ref=src[a], dst_ref=dst[a], send_sem=send.at[a], recv_sem=recv.at[a],
                                            device_id=(x, y, 1 - c), device_id_type=MESH) for a in range(n)]
        for cp in cps:
            cp.start()
        for cp in cps:
            cp.wait_recv()
        for cp in cps:
            cp.wait_send()

    return pl.pallas_call(
        body, out_shape=[jax.ShapeDtypeStruct(p.shape, p.dtype) for p in parts],
        in_specs=[ANY] * n, out_specs=[ANY] * n,
        scratch_shapes=[pltpu.SemaphoreType.DMA((n,)), pltpu.SemaphoreType.DMA((n,))],
        name="swap_sibling")(*parts)


def _allreduce_small(name, v):
    R = v.shape[0]

    def body(v_ref, o_ref, buf, send, recv):
        x, y, c = _place()
        me = 4 * x + 2 * y + c
        buf[me] = v_ref[...]
        cps = []
        for k in range(1, 8):
            fx, fy, fc = (k >> 2) & 1, (k >> 1) & 1, k & 1
            px = 1 - x if fx else x
            py = 1 - y if fy else y
            pc = 1 - c if fc else c
            cp = pltpu.make_async_remote_copy(src_ref=v_ref, dst_ref=buf.at[me], send_sem=send.at[k - 1],
                                              recv_sem=recv.at[k - 1], device_id=(px, py, pc), device_id_type=MESH)
            cp.start()
            cps.append((cp, 4 * px + 2 * py + pc))
        for k, (cp, peer) in enumerate(cps):
            pltpu.make_async_remote_copy(src_ref=v_ref, dst_ref=buf.at[peer], send_sem=send.at[k], recv_sem=recv.at[k],
                                         device_id=(x, y, c), device_id_type=MESH).wait_recv()
        for cp, _ in cps:
            cp.wait_send()
        acc = buf[0]
        for d in range(1, 8):
            acc = acc + buf[d]
        o_ref[...] = acc

    return pl.pallas_call(
        body, out_shape=jax.ShapeDtypeStruct((R, 128), F32),
        in_specs=[pl.BlockSpec(memory_space=pltpu.VMEM)], out_specs=pl.BlockSpec(memory_space=pltpu.VMEM),
        scratch_shapes=[pltpu.VMEM((8, R, 128), F32), pltpu.SemaphoreType.DMA((7,)), pltpu.SemaphoreType.DMA((7,))],
        name=name)(v)


def _row_tile(r):
    for t in (256, 128, 64, 32, 16, 8):
        if r % t == 0:
            return t
    raise ValueError(r)


def _sum4(name, own, got):
    R, C = own.shape
    tr = _row_tile(R)

    def body(o_ref, g_ref, s_ref):
        s = o_ref[...].astype(F32)
        for k in range(3):
            s = s + g_ref[k].astype(F32)
        s_ref[...] = s

    return pl.pallas_call(
        body, out_shape=jax.ShapeDtypeStruct((R, C), F32), grid=(R // tr,),
        in_specs=[pl.BlockSpec((tr, C), lambda i: (i, 0)), pl.BlockSpec((3, tr, C), lambda i: (0, i, 0))],
        out_specs=pl.BlockSpec((tr, C), lambda i: (i, 0)), name=name, compiler_params=_cp("parallel"))(own, got)


def _adamw(name, w, gparts, m, v):
    R, C = w.shape
    tr = _row_tile(R)
    ng = len(gparts)
    c1 = 1.0 - ADAM_B1 ** ADAM_STEP
    c2 = 1.0 - ADAM_B2 ** ADAM_STEP

    def body(*refs):
        w_ref = refs[0]
        g_refs = refs[1:1 + ng]
        m_ref, v_ref, go_ref, d_ref, mo_ref, vo_ref = refs[1 + ng:]
        g = g_refs[0][...]
        for r in g_refs[1:]:
            g = g + r[...]
        mn = ADAM_B1 * m_ref[...] + (1.0 - ADAM_B1) * g
        vn = ADAM_B2 * v_ref[...] + (1.0 - ADAM_B2) * (g * g)
        go_ref[...] = g
        mo_ref[...] = mn
        vo_ref[...] = vn
        d_ref[...] = -ADAM_LR * ((mn / c1) / (jnp.sqrt(vn / c2) + ADAM_EPS) + ADAM_WD * w_ref[...])

    blk = pl.BlockSpec((tr, C), lambda i: (i, 0))
    osh = jax.ShapeDtypeStruct((R, C), F32)
    return pl.pallas_call(
        body, out_shape=(osh, osh, osh, osh), grid=(R // tr,), in_specs=[blk] * (3 + ng), out_specs=(blk,) * 4,
        name=name, compiler_params=_cp("parallel"))(w, *gparts, m, v)


def _adamw_layers(name, w, sums, m, v):
    R2, C = w.shape
    R = R2 // DEPTH
    tr = _row_tile(R)
    nr = R // tr
    c1 = 1.0 - ADAM_B1 ** ADAM_STEP
    c2 = 1.0 - ADAM_B2 ** ADAM_STEP

    def body(w_ref, a0, b0, a1, b1, m_ref, v_ref, go_ref, d_ref, mo_ref, vo_ref):
        g = jnp.where(pl.program_id(0) == 0, a0[...] + b0[...], a1[...] + b1[...])
        mn = ADAM_B1 * m_ref[...] + (1.0 - ADAM_B1) * g
        vn = ADAM_B2 * v_ref[...] + (1.0 - ADAM_B2) * (g * g)
        go_ref[...] = g
        mo_ref[...] = mn
        vo_ref[...] = vn
        d_ref[...] = -ADAM_LR * ((mn / c1) / (jnp.sqrt(vn / c2) + ADAM_EPS) + ADAM_WD * w_ref[...])

    blk = pl.BlockSpec((tr, C), lambda l, i: (l * nr + i, 0))
    lay0 = pl.BlockSpec((tr, C), lambda l, i: (jnp.where(l == 0, i, nr - 1), 0))
    lay1 = pl.BlockSpec((tr, C), lambda l, i: (jnp.where(l == 1, i, 0), 0))
    osh = jax.ShapeDtypeStruct((R2, C), F32)
    return pl.pallas_call(
        body, out_shape=(osh, osh, osh, osh), grid=(DEPTH, nr),
        in_specs=[blk, lay0, lay0, lay1, lay1, blk, blk], out_specs=(blk,) * 4,
        name=name, compiler_params=_cp("arbitrary", "arbitrary"))(w, *sums[0], *sums[1], m, v)


BIG = [("ffn1_w_gate", "g1"), ("ffn1_w_up", "u1"), ("ffn1_w_down", "d1"), ("w_in", "win"), ("w_out", "wout"),
       ("ffn2_w_gate", "g2"), ("ffn2_w_up", "u2"), ("ffn2_w_down", "d2")]
SMALL = ["ffn1_norm", "mix_norm", "conv_b", "dt_bias", "a_log", "d_skip", "ssd_norm", "q_norm", "k_norm", "ffn2_norm"]
WEIGHTS = ["ffn1_norm", "ffn1_w_gate", "ffn1_w_up", "ffn1_w_down", "mix_norm", "w_in", "conv_w", "conv_b", "dt_bias",
           "a_log", "d_skip", "ssd_norm", "q_norm", "k_norm", "w_out", "ffn2_norm", "ffn2_w_gate", "ffn2_w_up",
           "ffn2_w_down"]
CONV_SH = CONV_DIM // N_SHARD
GATHER_GROUPS = [(0, "ffn1", ["g1", "u1", "d1"]), (0, "win", ["win", "cw"]), (0, "rest", ["wout", "g2", "u2", "d2"]),
                 (1, "all", ["g1", "u1", "d1", "win", "cw", "wout", "g2", "u2", "d2"])]


def _pad128(v):
    v = v.reshape(-1)
    return jnp.pad(v, (0, (-v.shape[0]) % 128))


def _pack(pieces):
    flat, offs, pos = [], [], 0
    for p in pieces:
        q = _pad128(p.astype(F32))
        offs.append(pos)
        pos += q.shape[0] // 128
        flat.append(q)
    total = -(-pos // 8) * 8
    out = jnp.concatenate(flat + [jnp.zeros(((total - pos) * 128,), F32)]).reshape(total, 128)
    return out, offs


def _unpack(packed, offs, shapes):
    out = []
    for off, shp in zip(offs, shapes):
        n = int(np.prod(shp))
        rows = -(-n // 128)
        out.append(packed[off:off + rows].reshape(-1)[:n].reshape(shp))
    return out


def kernel(x, ffn1_norm, ffn1_w_gate, ffn1_w_up, ffn1_w_down, mix_norm, w_in, conv_w, conv_b, dt_bias, a_log, d_skip, ssd_norm, q_norm, k_norm, w_out, ffn2_norm, ffn2_w_gate, ffn2_w_up, ffn2_w_down, loss_target, m_ffn1_norm, m_ffn1_w_gate, m_ffn1_w_up, m_ffn1_w_down, m_mix_norm, m_w_in, m_conv_w, m_conv_b, m_dt_bias, m_a_log, m_d_skip, m_ssd_norm, m_q_norm, m_k_norm, m_w_out, m_ffn2_norm, m_ffn2_w_gate, m_ffn2_w_up, m_ffn2_w_down, v_ffn1_norm, v_ffn1_w_gate, v_ffn1_w_up, v_ffn1_w_down, v_mix_norm, v_w_in, v_conv_w, v_conv_b, v_dt_bias, v_a_log, v_d_skip, v_ssd_norm, v_q_norm, v_k_norm, v_w_out, v_ffn2_norm, v_ffn2_w_gate, v_ffn2_w_up, v_ffn2_w_down):
    A = dict(locals())
    ix, iy, ic = _place()
    me = 2 * ix + iy
    B, S, _ = x.shape
    T = B * S

    own = {key: A[name].astype(BF16) for name, key in BIG}
    own["cw"] = conv_w
    exs, first_norm = [], ffn1_norm
    for gi, (l, _, keys) in enumerate(GATHER_GROUPS):
        ex, first_norm = _exchange_start("gather_start%d" % gi, True, [own[key][l] for key in keys], first_norm)
        exs.append(ex)
    landed = {}

    def weights(l, group, after):
        gi = [i for i, (gl, gname, _) in enumerate(GATHER_GROUPS) if gl == l and gname in (group, "all")][0]
        if gi not in landed:
            srcs, lands = _exchange_wait("gather_wait%d" % gi, exs[gi], after)
            landed[gi] = {}
            for key, mine, land in zip(GATHER_GROUPS[gi][2], srcs, lands):
                full = lax.dynamic_update_slice(land, mine[None], (me, 0, 0))
                if key == "win":
                    full = _win_from_shards(full)
                if key == "cw":
                    full = jnp.transpose(full, (1, 0, 2)).reshape(CONV_K, CONV_DIM)
                landed[gi][key] = full
        return landed[gi]

    pending = []

    def scatter(l, group, grads, carry):
        keys = sorted(grads)
        arrs = [grads[key] for key in keys]
        if "win" in grads:
            arrs[keys.index("win")] = _win_to_shards(grads["win"])
        ex, carry = _exchange_start("scatter_start_l%d_%s" % (l, group), False, arrs, carry)
        pending.append((l, keys, ex))
        return carry

    small = {name: A[name] for name in SMALL}
    small["ffn1_norm"] = first_norm
    lsum, dx, sgrads = _local_step(x.reshape(T, D_MODEL), loss_target.reshape(T, D_MODEL), small, weights, scatter, B)

    names = SMALL + ["conv_w"]
    shapes = [A[n].shape for n in SMALL] + [(DEPTH, CONV_K, CONV_DIM), ()]
    pieces = [jnp.stack([sgrads[l][n].reshape(shp[1:]) for l in range(DEPTH)]) for n, shp in zip(names, shapes)]
    pieces.append(0.5 / D_MODEL * jnp.sum(lsum))
    packed, offs = _pack(pieces)
    red = _allreduce_small("allreduce_small", packed)
    red = _unpack(red, offs, shapes)
    loss = red[-1]
    sg = dict(zip(names, red[:-1]))

    sums, after = {}, dx
    for idx, (l, keys, ex) in enumerate(pending):
        srcs, lands = _exchange_wait("scatter_wait%d" % idx, ex, after)
        for key, g, got in zip(keys, srcs, lands):
            mine = lax.dynamic_index_in_dim(g, me, axis=0, keepdims=False)
            sums[key, l] = after = _sum4("sum_%s_l%d" % (key, l), mine, got)
    order = [(key, l) for _, key in BIG for l in range(DEPTH)]
    theirs = dict(zip(order, _swap_sibling([sums[k] for k in order])))

    out = {}
    for name, key in BIG:
        shp = A[name].shape
        flat = lambda a: a.reshape(shp[0] * shp[1], shp[2])
        res = _adamw_layers("adamw_" + key, flat(A[name]), [(sums[key, l], theirs[key, l]) for l in range(DEPTH)],
                            flat(A["m_" + name]), flat(A["v_" + name]))
        out[name] = [r.reshape(shp) for r in res]

    wp, offs = _pack([A[n] for n in SMALL])
    gp, _ = _pack([sg[n] for n in SMALL])
    mp, _ = _pack([A["m_" + n] for n in SMALL])
    vp, _ = _pack([A["v_" + n] for n in SMALL])
    res = _adamw("adamw_small", wp, [gp], mp, vp)
    shapes = [A[n].shape for n in SMALL]
    res = [_unpack(r, offs, shapes) for r in res]
    for i, n in enumerate(SMALL):
        out[n] = [res[q][i] for q in range(4)]
    gcw = lax.dynamic_slice_in_dim(sg["conv_w"], me * CONV_SH, CONV_SH, axis=2)
    flat = lambda a: a.reshape(DEPTH * CONV_K, CONV_SH)
    res = _adamw("adamw_conv_w", flat(conv_w), [flat(gcw)], flat(m_conv_w), flat(v_conv_w))
    out["conv_w"] = [r.reshape(conv_w.shape) for r in res]

    outs = [loss, dx.reshape(B, S, D_MODEL)]
    for q in range(4):
        outs += [out[n][q] for n in WEIGHTS]
    return tuple(outs)
```

```python
import functools
import math

import numpy as np
import jax
import jax.numpy as jnp
from jax import lax
from jax.experimental import pallas as pl
from jax.experimental.pallas import tpu as pltpu

F32 = jnp.float32
BF16 = jnp.bfloat16

D_MODEL = 1024
DEPTH = 2
N_SHARD = 4
D_FF = 2816
FF_SH = D_FF // N_SHARD
SSD_HEADS = 16
HEAD_DIM = 64
SSD_GROUPS = 4
GROUP_W = 256
SSD_STATE = 128
CONV_K = 4
CONV_DIM = 2048
ATT_HEADS = 16
MIX_W = 2048
MIX_SH = MIX_W // N_SHARD
IN_PROJ = 6160
IN_SH = IN_PROJ // N_SHARD
IN_PAD = 6272
PROJ_TN = 896
COL_Z, COL_XBC, COL_Q, COL_K, COL_V, COL_DT = 0, 1024, 3072, 4096, 5120, 6144
EPS = 1e-6
NEG = -1e30
SSD_L = 256
ATT_B = 256
ROW_T = 512
HALF_T = ROW_T // 2
TK_W = 2048
CONV_CT = 256
CONV_R = 256
PAD_R = 8

ADAM_LR, ADAM_B1, ADAM_B2, ADAM_EPS, ADAM_WD, ADAM_STEP = 0.001, 0.9, 0.999, 1e-08, 0.01, 10

NN = (((1,), (0,)), ((), ()))
NT = (((1,), (1,)), ((), ()))
TN = (((0,), (0,)), ((), ()))

VMEM_LIMIT = 56 * 1024 * 1024


def _cp(*sem):
    return pltpu.CompilerParams(dimension_semantics=sem, vmem_limit_bytes=VMEM_LIMIT)


def _dot(a, b, dims):
    return lax.dot_general(a, b, dims, preferred_element_type=F32)


def _sigmoid(x):
    return 0.5 * jnp.tanh(0.5 * x) + 0.5


def _softplus(x):
    return jnp.maximum(x, 0.0) + jnp.log(1.0 + jnp.exp(-jnp.abs(x)))


def _mm(name, pairs, out_shape, out_spec, grid, dims, acc_shape, res=None, scale=1.0):
    nk = grid[2]
    npair = len(pairs)

    def body(*refs):
        ab = refs[:2 * npair]
        pos = 2 * npair
        res_ref = None
        if res is not None:
            res_ref = refs[pos]
            pos += 1
        out_ref = refs[pos]
        s = None
        for p in range(npair):
            d = _dot(ab[2 * p][...].astype(BF16), ab[2 * p + 1][...].astype(BF16), dims)
            s = d if s is None else s + d

        def finish(r):
            if scale != 1.0:
                r = r * scale
            if res_ref is not None:
                r = r + res_ref[...]
            out_ref[...] = r.astype(out_ref.dtype)

        if nk == 1:
            finish(s)
            return
        acc = refs[pos + 1]
        k = pl.program_id(2)

        @pl.when(k == 0)
        def _():
            acc[...] = s

        @pl.when(k > 0)
        def _():
            acc[...] += s

        @pl.when(k == nk - 1)
        def _():
            finish(acc[...])

    args, specs = [], []
    for a, a_spec, b, b_spec in pairs:
        args += [a, b]
        specs += [a_spec, b_spec]
    if res is not None:
        args.append(res[0])
        specs.append(res[1])
    return pl.pallas_call(
        body, out_shape=out_shape, grid=grid, in_specs=specs, out_specs=out_spec,
        scratch_shapes=[] if nk == 1 else [pltpu.VMEM(acc_shape, F32)], name=name,
        compiler_params=_cp("parallel", "parallel", "arbitrary"))(*args)


def _rms_fwd(name, x, w):
    T = x.shape[0]

    def body(x_ref, w_ref, o_ref):
        xv = x_ref[...]
        r = lax.rsqrt(jnp.mean(xv * xv, axis=-1, keepdims=True) + EPS)
        o_ref[...] = (xv * r * w_ref[...]).astype(BF16)

    return pl.pallas_call(
        body, out_shape=jax.ShapeDtypeStruct((T, D_MODEL), BF16), grid=(T // ROW_T,),
        in_specs=[pl.BlockSpec((ROW_T, D_MODEL), lambda i: (i, 0)), pl.BlockSpec((1, D_MODEL), lambda i: (0, 0))],
        out_specs=pl.BlockSpec((ROW_T, D_MODEL), lambda i: (i, 0)), name=name, compiler_params=_cp("parallel"))(x, w)


def _rms_bwd(name, dh, x, w, dres):
    T = x.shape[0]

    def body(dh_ref, x_ref, w_ref, dres_ref, dx_ref, dw_ref):
        @pl.when(pl.program_id(0) == 0)
        def _():
            dw_ref[...] = jnp.zeros_like(dw_ref)

        xv = x_ref[...]
        r = lax.rsqrt(jnp.mean(xv * xv, axis=-1, keepdims=True) + EPS)
        xhat = xv * r
        dhv = dh_ref[...]
        dxhat = dhv * w_ref[...]
        m = jnp.mean(dxhat * xhat, axis=-1, keepdims=True)
        dx_ref[...] = dres_ref[...] + r * (dxhat - xhat * m)
        dw_ref[...] += jnp.sum(dhv * xhat, axis=0, keepdims=True)

    row = pl.BlockSpec((ROW_T, D_MODEL), lambda i: (i, 0))
    vec = pl.BlockSpec((1, D_MODEL), lambda i: (0, 0))
    return pl.pallas_call(
        body, out_shape=(jax.ShapeDtypeStruct((T, D_MODEL), F32), jax.ShapeDtypeStruct((1, D_MODEL), F32)),
        grid=(T // ROW_T,), in_specs=[row, row, vec, row], out_specs=(row, vec), name=name,
        compiler_params=_cp("arbitrary"))(dh, x, w, dres)


def _loss_grad(name, y, t):
    T = y.shape[0]

    def body(y_ref, t_ref, dy_ref, l_ref):
        @pl.when(pl.program_id(0) == 0)
        def _():
            l_ref[...] = jnp.zeros_like(l_ref)

        e = y_ref[...] - t_ref[...]
        dy_ref[...] = e * (1.0 / D_MODEL)
        l_ref[...] += jnp.sum(e * e, axis=0, keepdims=True)

    row = pl.BlockSpec((ROW_T, D_MODEL), lambda i: (i, 0))
    vec = pl.BlockSpec((1, D_MODEL), lambda i: (0, 0))
    return pl.pallas_call(
        body, out_shape=(jax.ShapeDtypeStruct((T, D_MODEL), F32), jax.ShapeDtypeStruct((1, D_MODEL), F32)),
        grid=(T // ROW_T,), in_specs=[row, row], out_specs=(row, vec), name=name,
        compiler_params=_cp("arbitrary"))(y, t)


def _ffn_gate_up(name, h, wg, wu):
    T = h.shape[0]

    def body(h_ref, wg_ref, wu_ref, dgf_ref, duf_ref, a_ref):
        for r in range(0, ROW_T, HALF_T):
            rows = slice(r, r + HALF_T)
            hv = h_ref[rows, :]
            g = _dot(hv, wg_ref[...], NN)
            u = _dot(hv, wu_ref[...], NN)
            sg = _sigmoid(g)
            silu = g * sg
            dgf_ref[rows, :] = (u * (sg * (1.0 + g * (1.0 - sg)))).astype(BF16)
            duf_ref[rows, :] = silu.astype(BF16)
            a_ref[rows, :] = (silu * u).astype(BF16)

    wspec = pl.BlockSpec((None, D_MODEL, FF_SH), lambda j, i: (j, 0, 0))
    ospec = pl.BlockSpec((None, ROW_T, FF_SH), lambda j, i: (j, i, 0))
    osh = jax.ShapeDtypeStruct((N_SHARD, T, FF_SH), BF16)
    return pl.pallas_call(
        body, out_shape=(osh, osh, osh), grid=(N_SHARD, T // ROW_T),
        in_specs=[pl.BlockSpec((ROW_T, D_MODEL), lambda j, i: (i, 0)), wspec, wspec],
        out_specs=(ospec, ospec, ospec), name=name, compiler_params=_cp("parallel", "parallel"))(h, wg, wu)


def _ffn_dact(name, dx, wd, g, u):
    T = dx.shape[0]

    def body(dx_ref, wd_ref, g_ref, u_ref, dg_ref, du_ref):
        for r in range(0, ROW_T, HALF_T):
            rows = slice(r, r + HALF_T)
            da = 0.5 * _dot(dx_ref[rows, :].astype(BF16), wd_ref[...], NT)
            dg_ref[rows, :] = (da * g_ref[rows, :].astype(F32)).astype(BF16)
            du_ref[rows, :] = (da * u_ref[rows, :].astype(F32)).astype(BF16)

    aspec = pl.BlockSpec((None, ROW_T, FF_SH), lambda j, i: (j, i, 0))
    osh = jax.ShapeDtypeStruct((N_SHARD, T, FF_SH), BF16)
    return pl.pallas_call(
        body, out_shape=(osh, osh), grid=(N_SHARD, T // ROW_T),
        in_specs=[pl.BlockSpec((ROW_T, D_MODEL), lambda j, i: (i, 0)),
                  pl.BlockSpec((None, FF_SH, D_MODEL), lambda j, i: (j, 0, 0)), aspec, aspec],
        out_specs=(aspec, aspec), name=name, compiler_params=_cp("parallel", "parallel"))(dx, wd, g, u)


def _ffn_fwd(tag, x, nw, wg, wu, wd):
    T = x.shape[0]
    h = _rms_fwd(tag + "_rms", x, nw)
    g, u, a = _ffn_gate_up(tag + "_gu", h, wg, wu)
    if callable(wd):
        wd = wd(a)
    nt = T // ROW_T
    xo = _mm(tag + "_down",
             [(a, pl.BlockSpec((None, ROW_T, FF_SH), lambda i, n, k, j=j: (j, i, 0)),
               wd, pl.BlockSpec((None, FF_SH, D_MODEL), lambda i, n, k, j=j: (j, 0, 0))) for j in range(N_SHARD)],
             jax.ShapeDtypeStruct((T, D_MODEL), F32), pl.BlockSpec((ROW_T, D_MODEL), lambda i, n, k: (i, 0)),
             (nt, 1, 1), NN, (ROW_T, D_MODEL),
             res=(x, pl.BlockSpec((ROW_T, D_MODEL), lambda i, n, k: (i, 0))), scale=0.5)
    return xo, (x, h, g, u, a), wd


def _ffn_bwd(tag, dxo, saved, nw, wg, wu, wd, emit):
    x, h, g, u, a = saved
    T = x.shape[0]
    nt = T // ROW_T
    tkw = min(TK_W, T)
    nw_t = T // tkw
    dg, du = _ffn_dact(tag + "_dact", dxo, wd, g, u)
    actw = lambda f: pl.BlockSpec((None, tkw, FF_SH), f)
    gd = _mm(tag + "_dwd",
             [(a, actw(lambda m, n, k: (m, k, 0)), dxo, pl.BlockSpec((tkw, D_MODEL), lambda m, n, k: (k, 0)))],
             jax.ShapeDtypeStruct((N_SHARD, FF_SH, D_MODEL), BF16),
             pl.BlockSpec((None, FF_SH, D_MODEL), lambda m, n, k: (m, 0, 0)),
             (N_SHARD, 1, nw_t), TN, (FF_SH, D_MODEL), scale=0.5)
    hspec = pl.BlockSpec((tkw, D_MODEL), lambda j, n, k: (k, 0))
    gsh = jax.ShapeDtypeStruct((N_SHARD, D_MODEL, FF_SH), BF16)
    gspec = pl.BlockSpec((None, D_MODEL, FF_SH), lambda j, n, k: (j, 0, 0))
    gg = _mm(tag + "_dwg", [(h, hspec, dg, actw(lambda j, n, k: (j, k, 0)))], gsh, gspec,
             (N_SHARD, 1, nw_t), TN, (D_MODEL, FF_SH))
    gu = _mm(tag + "_dwu", [(h, hspec, du, actw(lambda j, n, k: (j, k, 0)))], gsh, gspec,
             (N_SHARD, 1, nw_t), TN, (D_MODEL, FF_SH))
    dg = emit(gg, gu, gd, dg)
    act = lambda j: pl.BlockSpec((None, ROW_T, FF_SH), lambda i, n, k: (j, i, 0))
    wsp = lambda j: pl.BlockSpec((None, D_MODEL, FF_SH), lambda i, n, k: (j, 0, 0))
    dh = _mm(tag + "_dh",
             [(dd, act(j), w, wsp(j)) for j in range(N_SHARD) for dd, w in ((dg, wg), (du, wu))],
             jax.ShapeDtypeStruct((T, D_MODEL), F32), pl.BlockSpec((ROW_T, D_MODEL), lambda i, n, k: (i, 0)),
             (nt, 1, 1), NT, (ROW_T, D_MODEL))
    return _rms_bwd(tag + "_rmsb", dh, x, nw, dxo)


def _seq_rows(ref, start, size, S):
    lo, hi = max(start, 0), min(start + size, S)
    parts = [ref[pl.ds(lo, hi - lo), :]]
    if lo > start:
        parts.insert(0, jnp.zeros((lo - start, ref.shape[1]), F32))
    if start + size > hi:
        parts.append(jnp.zeros((start + size - hi, ref.shape[1]), F32))
    return parts[0] if len(parts) == 1 else jnp.concatenate(parts, axis=0)


XBC_CB = COL_XBC // CONV_CT


def _conv_fwd(name, proj, w, b, B):
    T = proj.shape[0]
    S = T // B
    C = CONV_DIM

    def body(x_ref, w_ref, b_ref, o_ref):
        wv = w_ref[...]
        for c in range(S // CONV_R):
            r0 = c * CONV_R
            ch = _seq_rows(x_ref, r0 - PAD_R, CONV_R + PAD_R, S)
            pre = ch[PAD_R:] * wv[3:4] + b_ref[...]
            for s in range(1, CONV_K):
                pre = pre + pltpu.roll(ch, s, axis=0)[PAD_R:] * wv[3 - s:4 - s]
            o_ref[pl.ds(r0, CONV_R), :] = pre * _sigmoid(pre)

    return pl.pallas_call(
        body, out_shape=jax.ShapeDtypeStruct((T, C), F32), grid=(B, C // CONV_CT),
        in_specs=[pl.BlockSpec((S, CONV_CT), lambda bi, ci: (bi, XBC_CB + ci)),
                  pl.BlockSpec((CONV_K, CONV_CT), lambda bi, ci: (0, ci)),
                  pl.BlockSpec((1, CONV_CT), lambda bi, ci: (0, ci))],
        out_specs=pl.BlockSpec((S, CONV_CT), lambda bi, ci: (bi, ci)), name=name,
        compiler_params=_cp("parallel", "parallel"))(proj, w, b)


def _conv_bwd(name, proj, dxs, dB, dC, w, b, dproj, B):
    T = proj.shape[0]
    S = T // B
    C = CONV_DIM
    RW = CONV_R + PAD_R
    nx, nb = dxs.shape[1] // CONV_CT, dB.shape[1] // CONV_CT

    def body(x_ref, dx_in, db_in, dc_in, w_ref, b_ref, buf_ref, dx_ref, dw_ref, db_ref):
        @pl.when(pl.program_id(1) == 0)
        def _():
            dw_ref[...] = jnp.zeros_like(dw_ref)
            db_ref[...] = jnp.zeros_like(db_ref)

        ci = pl.program_id(0)
        wv = w_ref[...]
        dw = [jnp.zeros((1, CONV_CT), F32) for _ in range(CONV_K)]
        db = jnp.zeros((1, CONV_CT), F32)
        for c in range(S // CONV_R):
            r0 = c * CONV_R
            ch = _seq_rows(x_ref, r0 - PAD_R, RW + PAD_R, S)
            xs = [ch[PAD_R:]] + [pltpu.roll(ch, s, axis=0)[PAD_R:] for s in range(1, CONV_K)]
            pre = b_ref[...] + xs[0] * wv[3:4]
            for s in range(1, CONV_K):
                pre = pre + xs[s] * wv[3 - s:4 - s]
            sg = _sigmoid(pre)
            dout = jnp.where(ci < nx, _seq_rows(dx_in, r0, RW, S),
                             jnp.where(ci < nx + nb, _seq_rows(db_in, r0, RW, S), _seq_rows(dc_in, r0, RW, S)))
            dpre = dout * (sg * (1.0 + pre * (1.0 - sg)))
            dx = dpre[:CONV_R] * wv[3:4]
            for s in range(1, CONV_K):
                dx = dx + pltpu.roll(dpre, RW - s, axis=0)[:CONV_R] * wv[3 - s:4 - s]
            dx_ref[pl.ds(r0, CONV_R), :] = dx.astype(BF16)
            dcur = dpre[:CONV_R]
            db = db + jnp.sum(dcur, axis=0, keepdims=True)
            for s in range(CONV_K):
                dw[3 - s] = dw[3 - s] + jnp.sum(dcur * xs[s][:CONV_R], axis=0, keepdims=True)
        db_ref[...] += db
        for k in range(CONV_K):
            dw_ref[k:k + 1, :] += dw[k]

    seq = lambda f: pl.BlockSpec((S, CONV_CT), f)
    return pl.pallas_call(
        body,
        out_shape=(jax.ShapeDtypeStruct(dproj.shape, dproj.dtype), jax.ShapeDtypeStruct((CONV_K, C), F32),
                   jax.ShapeDtypeStruct((1, C), F32)),
        grid=(C // CONV_CT, B),
        in_specs=[seq(lambda ci, bi: (bi, XBC_CB + ci)),
                  seq(lambda ci, bi: (bi, jnp.minimum(ci, nx - 1))),
                  seq(lambda ci, bi: (bi, jnp.clip(ci - nx, 0, nb - 1))),
                  seq(lambda ci, bi: (bi, jnp.clip(ci - nx - nb, 0, nb - 1))),
                  pl.BlockSpec((CONV_K, CONV_CT), lambda ci, bi: (0, ci)),
                  pl.BlockSpec((1, CONV_CT), lambda ci, bi: (0, ci)), ANY],
        out_specs=(seq(lambda ci, bi: (bi, XBC_CB + ci)),
                   pl.BlockSpec((CONV_K, CONV_CT), lambda ci, bi: (0, ci)),
                   pl.BlockSpec((1, CONV_CT), lambda ci, bi: (0, ci))),
        input_output_aliases={6: 0},
        name=name, compiler_params=_cp("parallel", "arbitrary"))(proj, dxs, dB, dC, w, b, dproj)


def _tri_sum(tri, x, dims, tri_first):
    hi = x.astype(BF16)
    r1 = x - hi.astype(F32)
    mid = r1.astype(BF16)
    lo = (r1 - mid.astype(F32)).astype(BF16)
    out = None
    for part in (hi, mid, lo):
        d = _dot(tri, part, dims) if tri_first else _dot(part, tri, dims)
        out = d if out is None else out + d
    return out


def _total(x):
    return jnp.sum(jnp.sum(x, axis=0, keepdims=True), axis=-1, keepdims=True)


def _ssd_common(dtc_ref, dtr_ref, pcol_ref, prow_ref, b_ref, c_ref):
    L = SSD_L
    bias_c, alog_c = pcol_ref[0:1, :], pcol_ref[1:2, :]
    a_c = -jnp.exp(alog_c)
    dt_c = _softplus(dtc_ref[...] + bias_c)
    row = lax.broadcasted_iota(jnp.int32, (L, L), 0)
    col = lax.broadcasted_iota(jnp.int32, (L, L), 1)
    causal = row >= col
    tri = causal.astype(BF16)
    cum_c = _tri_sum(tri, dt_c * a_c, NN, True)
    a_r = -jnp.exp(prow_ref[:, 1:2])
    dt_r = _softplus(dtr_ref[...] + prow_ref[:, 0:1])
    cum_r = _tri_sum(tri, dt_r * a_r, NT, False)
    bb = b_ref[...].astype(BF16)
    cb = c_ref[...].astype(BF16)
    G = _dot(cb, bb, NT)
    return a_c, dt_c, causal, tri, cum_c, cum_r, bb, cb, G


def _ssd_fwd(name, xc, proj, dtc, dtr, pcol, prow, nw, B):
    T = xc.shape[0]
    S = T // B
    nb = S // SSD_L
    L = SSD_L

    def body(xs_ref, b_ref, c_ref, z_ref, dtc_ref, dtr_ref, pcol_ref, prow_ref, nw_ref, y_ref, yn_ref, hs_ref, H, yo_s):
        @pl.when(pl.program_id(2) == 0)
        def _():
            H[...] = jnp.zeros_like(H)

        a_c, dt_c, causal, tri, cum_c, cum_r, bb, cb, G = _ssd_common(dtc_ref, dtr_ref, pcol_ref, prow_ref, b_ref, c_ref)
        dsk = pcol_ref[2:3, :]
        clast = cum_c[L - 1:L, :]
        bf = b_ref[...]
        for h in range(4):
            hs_ref[h] = H[h]
            yo_s[h] = _dot(cb, H[h].astype(BF16), NN)
        for h in range(4):
            sl = slice(HEAD_DIM * h, HEAD_DIM * (h + 1))
            cc = cum_c[:, h:h + 1]
            lm = jnp.exp(jnp.where(causal, cc - cum_r[h:h + 1, :], NEG))
            M = (G * lm).astype(BF16)
            xh = xs_ref[:, sl]
            Xb = (xh * dt_c[:, h:h + 1]).astype(BF16)
            Hh = H[h]
            y = _dot(M, Xb, NN) + jnp.exp(cc) * yo_s[h]
            y_ref[:, sl] = y + dsk[:, h:h + 1] * xh
            cl = clast[:, h:h + 1]
            Bw = (bf * jnp.exp(cl - cc)).astype(BF16)
            H[h] = jnp.exp(cl) * Hh + _dot(Bw, Xb, TN)
        zv = z_ref[...]
        y2 = y_ref[...] * (zv * _sigmoid(zv))
        r = lax.rsqrt(jnp.mean(y2 * y2, axis=-1, keepdims=True) + EPS)
        yn_ref[...] = (y2 * r * nw_ref[...]).astype(BF16)

    rowi = lambda b, g, i: b * nb + i
    grp = pl.BlockSpec((L, GROUP_W), lambda b, g, i: (rowi(b, g, i), g))
    return pl.pallas_call(
        body,
        out_shape=(jax.ShapeDtypeStruct((T, 1024), F32), jax.ShapeDtypeStruct((T, 1024), BF16),
                   jax.ShapeDtypeStruct((B, SSD_GROUPS, nb, 4, SSD_STATE, HEAD_DIM), F32)),
        grid=(B, SSD_GROUPS, nb),
        in_specs=[grp,
                  pl.BlockSpec((L, SSD_STATE), lambda b, g, i: (rowi(b, g, i), 8 + g)),
                  pl.BlockSpec((L, SSD_STATE), lambda b, g, i: (rowi(b, g, i), 12 + g)),
                  grp,
                  pl.BlockSpec((None, L, 4), lambda b, g, i: (g, rowi(b, g, i), 0)),
                  pl.BlockSpec((None, 4, L), lambda b, g, i: (g, 0, rowi(b, g, i))),
                  pl.BlockSpec((None, 3, 4), lambda b, g, i: (g, 0, 0)),
                  pl.BlockSpec((None, 4, 3), lambda b, g, i: (g, 0, 0)),
                  pl.BlockSpec((1, GROUP_W), lambda b, g, i: (0, g))],
        out_specs=(grp, grp,
                   pl.BlockSpec((None, None, None, 4, SSD_STATE, HEAD_DIM), lambda b, g, i: (b, g, i, 0, 0, 0))),
        scratch_shapes=[pltpu.VMEM((4, SSD_STATE, HEAD_DIM), F32), pltpu.VMEM((4, L, HEAD_DIM), F32)], name=name,
        compiler_params=_cp("parallel", "parallel", "arbitrary"))(xc, xc, xc, proj, dtc, dtr, pcol, prow, nw)


def _ssd_bwd(name, dyn, Y, xc, proj, dtc, dtr, pcol, prow, nw, hs, dproj, B):
    T = xc.shape[0]
    S = T // B
    nb = S // SSD_L
    L = SSD_L

    def body(dyn_ref, y_ref, xs_ref, b_ref, c_ref, z_ref, dtc_ref, dtr_ref, pcol_ref, prow_ref, nw_ref, hs_ref, buf_ref,
             dxs_ref, db_ref, dc_ref, dz_ref, ddt_ref, dpar_ref, dnw_ref, dH, dm_s, dxo_s, ea_s, ex_s):
        @pl.when(pl.program_id(2) == 0)
        def _():
            dH[...] = jnp.zeros_like(dH)
            dpar_ref[...] = jnp.zeros_like(dpar_ref)
            dnw_ref[...] = jnp.zeros_like(dnw_ref)

        a_c, dt_c, causal, tri, cum_c, cum_r, bb, cb, G = _ssd_common(dtc_ref, dtr_ref, pcol_ref, prow_ref, b_ref, c_ref)
        dsk = pcol_ref[2:3, :]
        clast = cum_c[L - 1:L, :]
        bf = b_ref[...]
        cf = c_ref[...]
        Yv = y_ref[...]
        zv = z_ref[...]
        sz = _sigmoid(zv)
        silu = zv * sz
        y2 = Yv * silu
        r = lax.rsqrt(jnp.mean(y2 * y2, axis=-1, keepdims=True) + EPS)
        yhat = y2 * r
        dyv = dyn_ref[...]
        dnw_ref[...] += jnp.sum(dyv * yhat, axis=0, keepdims=True)
        dyhat = dyv * nw_ref[...]
        dy2 = r * (dyhat - yhat * jnp.mean(dyhat * yhat, axis=-1, keepdims=True))
        dY = dy2 * silu
        dz_ref[...] = (dy2 * Yv * (sz * (1.0 + zv * (1.0 - sz)))).astype(BF16)

        lane4 = lax.broadcasted_iota(jnp.int32, (1, 4), 1)
        dG = jnp.zeros((L, L), F32)
        dBs = jnp.zeros((L, SSD_STATE), F32)
        dCs = jnp.zeros((L, SSD_STATE), F32)
        ddsk = jnp.zeros((1, 4), F32)
        dcl = jnp.zeros((1, 4), F32)
        for h in range(4):
            sl = slice(HEAD_DIM * h, HEAD_DIM * (h + 1))
            xb = (xs_ref[:, sl] * dt_c[:, h:h + 1]).astype(BF16)
            dm_s[h] = _dot(dY[:, sl].astype(BF16), xb, NT)
            dxo_s[h] = _dot(bb, dH[h].astype(BF16), NN)
        for h in range(4):
            sl = slice(HEAD_DIM * h, HEAD_DIM * (h + 1))
            onehot = (lane4 == h).astype(F32)
            cc = cum_c[:, h:h + 1]
            cl = clast[:, h:h + 1]
            lm = jnp.exp(jnp.where(causal, cc - cum_r[h:h + 1, :], NEG))
            M = (G * lm).astype(BF16)
            xh = xs_ref[:, sl]
            dth = dt_c[:, h:h + 1]
            X = xh * dth
            Xb = X.astype(BF16)
            dYh = dY[:, sl]
            dYb = dYh.astype(BF16)
            Hb = hs_ref[h].astype(BF16)
            dHh = dH[h]
            dHb = dHh.astype(BF16)
            alpha = jnp.exp(cc)
            beta = jnp.exp(cl - cc)
            dXoff = beta * dxo_s[h]
            dX = _dot(M, dYb, TN) + dXoff
            dG = dG + dm_s[h] * lm
            dCs = dCs + _dot((alpha * dYh).astype(BF16), Hb, NT)
            dBs = dBs + _dot((beta * X).astype(BF16), dHb, NT)
            ypre = Yv[:, sl] - dsk[:, h:h + 1] * xh
            ea_s[:, sl] = dYb.astype(F32) * ypre - Xb.astype(F32) * dX
            ex_s[:, sl] = dX * xh
            dcl_h = (_total(dHh * (jnp.exp(cl) * hs_ref[h])) + _total(Xb.astype(F32) * dXoff))
            dcl = dcl + dcl_h * onehot
            ddsk = ddsk + _total(dYh * xh) * onehot
            dxs_ref[:, sl] = dsk[:, h:h + 1] * dYh + dX * dth
            dH[h] = jnp.exp(cl) * dHh + _dot((alpha * cf).astype(BF16), dYb, TN)
        dGb = dG.astype(BF16)
        dc_ref[...] = _dot(dGb, bb, NN) + dCs
        db_ref[...] = _dot(dGb, cb, TN) + dBs
        feat = lax.broadcasted_iota(jnp.int32, (GROUP_W, 4), 0)
        head = lax.broadcasted_iota(jnp.int32, (GROUP_W, 4), 1) * HEAD_DIM
        sel = ((feat >= head) & (feat < head + HEAD_DIM)).astype(BF16)
        dA = _tri_sum(sel, ea_s[...], NN, False)
        ddtx = _tri_sum(sel, ex_s[...], NN, False)
        last = lax.broadcasted_iota(jnp.int32, (L, 1), 0) == L - 1
        dA = dA + jnp.where(last, dcl, 0.0)
        dadt = _tri_sum(tri, dA, TN, True)
        ddt = dadt * a_c + ddtx
        d_a = jnp.sum(dadt * dt_c, axis=0, keepdims=True)
        ddraw = ddt * _sigmoid(dtc_ref[...] + pcol_ref[0:1, :])
        ddt_ref[...] = ddraw
        dpar_ref[0:1, :] += jnp.sum(ddraw, axis=0, keepdims=True)
        dpar_ref[1:2, :] += d_a * a_c
        dpar_ref[2:3, :] += ddsk

    rowi = lambda b, g, i: b * nb + (nb - 1 - i)
    grp = pl.BlockSpec((L, GROUP_W), lambda b, g, i: (rowi(b, g, i), g))
    st = pl.BlockSpec((L, SSD_STATE), lambda b, g, i: (rowi(b, g, i), g))
    f = jax.ShapeDtypeStruct
    return pl.pallas_call(
        body,
        out_shape=(f((T, 1024), F32), f((T, 512), F32), f((T, 512), F32), f(dproj.shape, dproj.dtype),
                   f((SSD_GROUPS, T, 4), F32), f((B, SSD_GROUPS, 3, 4), F32), f((B, 1, 1024), F32)),
        grid=(B, SSD_GROUPS, nb),
        in_specs=[grp, grp, grp,
                  pl.BlockSpec((L, SSD_STATE), lambda b, g, i: (rowi(b, g, i), 8 + g)),
                  pl.BlockSpec((L, SSD_STATE), lambda b, g, i: (rowi(b, g, i), 12 + g)),
                  grp,
                  pl.BlockSpec((None, L, 4), lambda b, g, i: (g, rowi(b, g, i), 0)),
                  pl.BlockSpec((None, 4, L), lambda b, g, i: (g, 0, rowi(b, g, i))),
                  pl.BlockSpec((None, 3, 4), lambda b, g, i: (g, 0, 0)),
                  pl.BlockSpec((None, 4, 3), lambda b, g, i: (g, 0, 0)),
                  pl.BlockSpec((1, GROUP_W), lambda b, g, i: (0, g)),
                  pl.BlockSpec((None, None, None, 4, SSD_STATE, HEAD_DIM), lambda b, g, i: (b, g, nb - 1 - i, 0, 0, 0)),
                  ANY],
        out_specs=(grp, st, st, grp,
                   pl.BlockSpec((None, L, 4), lambda b, g, i: (g, rowi(b, g, i), 0)),
                   pl.BlockSpec((None, None, 3, 4), lambda b, g, i: (b, g, 0, 0)),
                   pl.BlockSpec((None, 1, GROUP_W), lambda b, g, i: (b, 0, g))),
        input_output_aliases={12: 3},
        scratch_shapes=[pltpu.VMEM((4, SSD_STATE, HEAD_DIM), F32), pltpu.VMEM((4, L, L), F32),
                        pltpu.VMEM((4, L, HEAD_DIM), F32), pltpu.VMEM((L, GROUP_W), F32),
                        pltpu.VMEM((L, GROUP_W), F32)], name=name,
        compiler_params=_cp("parallel", "parallel", "arbitrary"))(
            dyn, Y, xc, xc, xc, proj, dtc, dtr, pcol, prow, nw, hs, dproj)


def _headnorm_fwd(name, proj, col_block, w):
    T = proj.shape[0]

    def body(x_ref, w_ref, o_ref):
        for h in range(ATT_HEADS):
            sl = slice(HEAD_DIM * h, HEAD_DIM * (h + 1))
            xh = x_ref[:, sl]
            r = lax.rsqrt(jnp.mean(xh * xh, axis=-1, keepdims=True) + EPS)
            o_ref[:, sl] = (xh * r * w_ref[...]).astype(BF16)

    return pl.pallas_call(
        body, out_shape=jax.ShapeDtypeStruct((T, 1024), BF16), grid=(T // ROW_T,),
        in_specs=[pl.BlockSpec((ROW_T, 1024), lambda i: (i, col_block)), pl.BlockSpec((1, HEAD_DIM), lambda i: (0, 0))],
        out_specs=pl.BlockSpec((ROW_T, 1024), lambda i: (i, 0)), name=name, compiler_params=_cp("parallel"))(proj, w)


def _headnorm_bwd(name, dn, proj, col_block, w, dproj):
    T = proj.shape[0]

    def body(dn_ref, x_ref, w_ref, buf_ref, dx_ref, dw_ref):
        @pl.when(pl.program_id(0) == 0)
        def _():
            dw_ref[...] = jnp.zeros_like(dw_ref)

        dw = jnp.zeros((1, HEAD_DIM), F32)
        for h in range(ATT_HEADS):
            sl = slice(HEAD_DIM * h, HEAD_DIM * (h + 1))
            xh = x_ref[:, sl]
            r = lax.rsqrt(jnp.mean(xh * xh, axis=-1, keepdims=True) + EPS)
            xhat = xh * r
            dnh = dn_ref[:, sl]
            dxhat = dnh * w_ref[...]
            dx_ref[:, sl] = (r * (dxhat - xhat * jnp.mean(dxhat * xhat, axis=-1, keepdims=True))).astype(BF16)
            dw = dw + jnp.sum(dnh * xhat, axis=0, keepdims=True)
        dw_ref[...] += dw

    here = pl.BlockSpec((ROW_T, 1024), lambda i: (i, col_block))
    return pl.pallas_call(
        body, out_shape=(jax.ShapeDtypeStruct(dproj.shape, dproj.dtype), jax.ShapeDtypeStruct((1, HEAD_DIM), F32)),
        grid=(T // ROW_T,),
        in_specs=[pl.BlockSpec((ROW_T, 1024), lambda i: (i, 0)), here, pl.BlockSpec((1, HEAD_DIM), lambda i: (0, 0)), ANY],
        out_specs=(here, pl.BlockSpec((1, HEAD_DIM), lambda i: (0, 0))), input_output_aliases={3: 0},
        name=name, compiler_params=_cp("arbitrary"))(dn, proj, w, dproj)


def _att_bias(nq):
    j = np.arange(ATT_B)[:, None]
    i = np.arange(ATT_B)[None, :]
    out = np.empty((nq, ATT_B, ATT_B), np.float32)
    for dblk in range(nq):
        dl = ATT_B * dblk + i - j
        cnt = ((dl >= 0) & (dl <= 128)).astype(np.float32)
        cnt += ((dl >= 0) & (dl % 4 == 0) & (dl <= 512))
        cnt += ((dl >= 0) & (dl % 16 == 0) & (dl <= 2048))
        out[dblk] = np.where(cnt > 0, np.log(np.maximum(cnt, 1.0)), NEG)
    return jnp.asarray(out)


def _row_pair(nq):
    def f(r, c):
        first = c <= r
        return jnp.where(first, r, nq - 1 - r), jnp.where(first, c, c - (r + 1))
    return f


def _col_pair(nq):
    def f(r, c):
        first = c < nq - r
        kj = jnp.where(first, r, nq - 1 - r)
        return jnp.where(first, r + c, nq - 1 - r + (c - (nq - r))), kj
    return f


ATT_SCALE = 1.0 / math.sqrt(HEAD_DIM)
ATT_HS = 4
ATT_W = ATT_HS * HEAD_DIM


def _att_maps(nq, qk):
    return dict(
        q_tok=lambda b, g, r, c: (b * nq + qk(r, c)[0], g),
        k_tok=lambda b, g, r, c: (b * nq + qk(r, c)[1], g),
        v_tok=lambda b, g, r, c: (b * nq + qk(r, c)[1], COL_V // ATT_W + g),
        q_feat=lambda b, g, r, c: (g, b * nq + qk(r, c)[0]),
        k_feat=lambda b, g, r, c: (g, b * nq + qk(r, c)[1]),
        bias=lambda b, g, r, c: (qk(r, c)[0] - qk(r, c)[1], 0, 0),
        lse=lambda b, g, r, c: (g, 0, b * nq + qk(r, c)[0]),
        do_tok=lambda b, g, r, c: (b * nq + qk(r, c)[0], ATT_HS + g))


def _att_fwd(name, kn, qT, vT, bias, B):
    T = kn.shape[0]
    nq = (T // B) // ATT_B
    qk = _row_pair(nq)
    mp = _att_maps(nq, qk)

    def body(k_ref, qT_ref, vT_ref, bias_ref, oT_ref, lse_ref, m_s, l_s, acc_s, s_s):
        qi, kj = qk(pl.program_id(2), pl.program_id(3))

        @pl.when(kj == 0)
        def _():
            m_s[...] = jnp.full_like(m_s, NEG)
            l_s[...] = jnp.zeros_like(l_s)
            acc_s[...] = jnp.zeros_like(acc_s)

        bv = bias_ref[...]
        for h in range(ATT_HS):
            rs = slice(HEAD_DIM * h, HEAD_DIM * (h + 1))
            s_s[h] = _dot(k_ref[:, rs], qT_ref[rs, :], NN)
        for h in range(ATT_HS):
            rs = slice(HEAD_DIM * h, HEAD_DIM * (h + 1))
            s = s_s[h] + bv
            m_prev = m_s[h:h + 1, :]
            m_new = jnp.maximum(m_prev, jnp.max(s, axis=0, keepdims=True))
            alpha = jnp.exp(m_prev - m_new)
            p = jnp.exp(s - m_new)
            l_s[h:h + 1, :] = alpha * l_s[h:h + 1, :] + jnp.sum(p, axis=0, keepdims=True)
            acc_s[rs, :] = alpha * acc_s[rs, :] + _dot(vT_ref[rs, :], p.astype(BF16), NN)
            m_s[h:h + 1, :] = m_new

        @pl.when(kj == qi)
        def _():
            for h in range(ATT_HS):
                rs = slice(HEAD_DIM * h, HEAD_DIM * (h + 1))
                oT_ref[rs, :] = (acc_s[rs, :] / l_s[h:h + 1, :]).astype(BF16)
            lse_ref[...] = m_s[...] + jnp.log(l_s[...])

    tok = (ATT_B, ATT_W)
    feat = (ATT_W, ATT_B)
    return pl.pallas_call(
        body,
        out_shape=(jax.ShapeDtypeStruct((1024, T), BF16), jax.ShapeDtypeStruct((ATT_HEADS // ATT_HS, ATT_HS, T), F32)),
        grid=(B, ATT_HEADS // ATT_HS, nq // 2, nq + 1),
        in_specs=[pl.BlockSpec(tok, mp["k_tok"]), pl.BlockSpec(feat, mp["q_feat"]), pl.BlockSpec(feat, mp["k_feat"]),
                  pl.BlockSpec((None, ATT_B, ATT_B), mp["bias"])],
        out_specs=(pl.BlockSpec(feat, mp["q_feat"]), pl.BlockSpec((None, ATT_HS, ATT_B), mp["lse"])),
        scratch_shapes=[pltpu.VMEM((ATT_HS, ATT_B), F32), pltpu.VMEM((ATT_HS, ATT_B), F32),
                        pltpu.VMEM((ATT_W, ATT_B), F32), pltpu.VMEM((ATT_HS, ATT_B, ATT_B), F32)],
        name=name, compiler_params=_cp("parallel", "parallel", "arbitrary", "arbitrary"))(kn, qT, vT, bias)


def _att_scores(k_ref, qT_ref, v_ref, doT_ref, s_s, dp_s):
    for h in range(ATT_HS):
        rs = slice(HEAD_DIM * h, HEAD_DIM * (h + 1))
        s_s[h] = _dot(k_ref[:, rs], qT_ref[rs, :], NN)
        dp_s[h] = _dot(v_ref[:, rs].astype(BF16), doT_ref[rs, :].astype(BF16), NN)


def _att_p_ds(s_s, dp_s, doT_ref, oT_ref, lse_ref, bv, h):
    rs = slice(HEAD_DIM * h, HEAD_DIM * (h + 1))
    delta = jnp.sum(doT_ref[rs, :] * oT_ref[rs, :].astype(F32), axis=0, keepdims=True)
    p = jnp.exp(s_s[h] + bv - lse_ref[h:h + 1, :])
    return p, p * (dp_s[h] - delta)


def _att_bwd_dq(name, kn, qT, vb, knT, bias, doT, oT, lse, B):
    T = kn.shape[0]
    nq = (T // B) // ATT_B
    qk = _row_pair(nq)
    mp = _att_maps(nq, qk)

    def body(k_ref, qT_ref, v_ref, kT_ref, bias_ref, doT_ref, oT_ref, lse_ref, dqT_ref, acc_s, s_s, dp_s):
        qi, kj = qk(pl.program_id(2), pl.program_id(3))

        @pl.when(kj == 0)
        def _():
            acc_s[...] = jnp.zeros_like(acc_s)

        bv = bias_ref[...]
        _att_scores(k_ref, qT_ref, v_ref, doT_ref, s_s, dp_s)
        for h in range(ATT_HS):
            rs = slice(HEAD_DIM * h, HEAD_DIM * (h + 1))
            p, ds = _att_p_ds(s_s, dp_s, doT_ref, oT_ref, lse_ref, bv, h)
            acc_s[rs, :] += _dot(kT_ref[rs, :], ds.astype(BF16), NN)

        @pl.when(kj == qi)
        def _():
            dqT_ref[...] = acc_s[...] * ATT_SCALE

    tok = (ATT_B, ATT_W)
    feat = (ATT_W, ATT_B)
    return pl.pallas_call(
        body, out_shape=jax.ShapeDtypeStruct((1024, T), F32), grid=(B, ATT_HEADS // ATT_HS, nq // 2, nq + 1),
        in_specs=[pl.BlockSpec(tok, mp["k_tok"]), pl.BlockSpec(feat, mp["q_feat"]), pl.BlockSpec(tok, mp["v_tok"]),
                  pl.BlockSpec(feat, mp["k_feat"]), pl.BlockSpec((None, ATT_B, ATT_B), mp["bias"]),
                  pl.BlockSpec(feat, mp["q_feat"]), pl.BlockSpec(feat, mp["q_feat"]),
                  pl.BlockSpec((None, ATT_HS, ATT_B), mp["lse"])],
        out_specs=pl.BlockSpec(feat, mp["q_feat"]),
        scratch_shapes=[pltpu.VMEM((ATT_W, ATT_B), F32), pltpu.VMEM((ATT_HS, ATT_B, ATT_B), F32),
                        pltpu.VMEM((ATT_HS, ATT_B, ATT_B), F32)],
        name=name, compiler_params=_cp("parallel", "parallel", "arbitrary", "arbitrary"))(
            kn, qT, vb, knT, bias, doT, oT, lse)


def _att_bwd_dkv(name, kn, qT, vb, qn, bias, doT, oT, lse, dyn, dproj, B):
    T = kn.shape[0]
    nq = (T // B) // ATT_B
    qk = _col_pair(nq)
    mp = _att_maps(nq, qk)

    def body(k_ref, qT_ref, v_ref, q_ref, bias_ref, doT_ref, oT_ref, lse_ref, do_ref, buf_ref, dk_ref, dv_ref, dk_s, dv_s,
             s_s, dp_s):
        qi, kj = qk(pl.program_id(2), pl.program_id(3))

        @pl.when(qi == kj)
        def _():
            dk_s[...] = jnp.zeros_like(dk_s)
            dv_s[...] = jnp.zeros_like(dv_s)

        bv = bias_ref[...]
        _att_scores(k_ref, qT_ref, v_ref, doT_ref, s_s, dp_s)
        for h in range(ATT_HS):
            rs = slice(HEAD_DIM * h, HEAD_DIM * (h + 1))
            p, ds = _att_p_ds(s_s, dp_s, doT_ref, oT_ref, lse_ref, bv, h)
            dv_s[h] += _dot(p.astype(BF16), do_ref[:, rs].astype(BF16), NN)
            dk_s[h] += _dot(ds.astype(BF16), q_ref[:, rs], NN)

        @pl.when(qi == nq - 1)
        def _():
            for h in range(ATT_HS):
                rs = slice(HEAD_DIM * h, HEAD_DIM * (h + 1))
                dk_ref[:, rs] = dk_s[h] * ATT_SCALE
                dv_ref[:, rs] = dv_s[h].astype(BF16)

    tok = (ATT_B, ATT_W)
    feat = (ATT_W, ATT_B)
    v_cb = COL_V // ATT_W
    return pl.pallas_call(
        body, out_shape=(jax.ShapeDtypeStruct((T, 1024), F32), jax.ShapeDtypeStruct(dproj.shape, dproj.dtype)),
        grid=(B, ATT_HEADS // ATT_HS, nq // 2, nq + 1),
        in_specs=[pl.BlockSpec(tok, mp["k_tok"]), pl.BlockSpec(feat, mp["q_feat"]), pl.BlockSpec(tok, mp["v_tok"]),
                  pl.BlockSpec(tok, mp["q_tok"]), pl.BlockSpec((None, ATT_B, ATT_B), mp["bias"]),
                  pl.BlockSpec(feat, mp["q_feat"]), pl.BlockSpec(feat, mp["q_feat"]),
                  pl.BlockSpec((None, ATT_HS, ATT_B), mp["lse"]), pl.BlockSpec(tok, mp["do_tok"]), ANY],
        out_specs=(pl.BlockSpec(tok, mp["k_tok"]),
                   pl.BlockSpec(tok, lambda b, g, r, c: (b * nq + qk(r, c)[1], v_cb + g))),
        input_output_aliases={9: 1},
        scratch_shapes=[pltpu.VMEM((ATT_HS, ATT_B, HEAD_DIM), F32), pltpu.VMEM((ATT_HS, ATT_B, HEAD_DIM), F32),
                        pltpu.VMEM((ATT_HS, ATT_B, ATT_B), F32), pltpu.VMEM((ATT_HS, ATT_B, ATT_B), F32)],
        name=name, compiler_params=_cp("parallel", "parallel", "arbitrary", "arbitrary"))(
            kn, qT, vb, qn, bias, doT, oT, lse, dyn, dproj)


def _group_cols(v):
    return v.reshape(SSD_GROUPS, 4)


def _ssd_params(p):
    rows = jnp.stack([_group_cols(p["dt_bias"]), _group_cols(p["a_log"]), _group_cols(p["d_skip"])], axis=1)
    return rows, jnp.swapaxes(rows, 1, 2)


def _dymix(name, dx, wout):
    T = dx.shape[0]

    def body(dx_ref, w_ref, o_ref):
        dxb = dx_ref[...].astype(BF16)
        for n in range(N_SHARD):
            o_ref[:, MIX_SH * n:MIX_SH * (n + 1)] = _dot(dxb, w_ref[n], NT)

    return pl.pallas_call(
        body, out_shape=jax.ShapeDtypeStruct((T, MIX_W), F32), grid=(T // ROW_T,),
        in_specs=[pl.BlockSpec((ROW_T, D_MODEL), lambda i: (i, 0)),
                  pl.BlockSpec((N_SHARD, MIX_SH, D_MODEL), lambda i: (0, 0, 0))],
        out_specs=pl.BlockSpec((ROW_T, MIX_W), lambda i: (i, 0)), name=name, compiler_params=_cp("parallel"))(dx, wout)


def _mixer_fwd(tag, x1, p, weights, bias, B):
    T = x1.shape[0]
    S = T // B
    nt = T // ROW_T
    h2 = _rms_fwd(tag + "_mixrms", x1, p["mix_norm"][None])
    wi = weights("win", h2)
    win, cw = wi["win"], wi["cw"]
    proj = _mm(tag + "_proj",
               [(h2, pl.BlockSpec((ROW_T, D_MODEL), lambda j, i, k: (i, 0)),
                 win, pl.BlockSpec((D_MODEL, PROJ_TN), lambda j, i, k: (0, j)))],
               jax.ShapeDtypeStruct((T, IN_PAD), F32), pl.BlockSpec((ROW_T, PROJ_TN), lambda j, i, k: (i, j)),
               (IN_PAD // PROJ_TN, nt, 1), NN, (ROW_T, PROJ_TN))
    xc = _conv_fwd(tag + "_conv", proj, cw, p["conv_b"][None], B)
    dtraw = proj[:, COL_DT:COL_DT + SSD_HEADS].reshape(T, SSD_GROUPS, 4)
    dtc = jnp.transpose(dtraw, (1, 0, 2))
    dtr = jnp.transpose(dtraw, (1, 2, 0))
    pcol, prow = _ssd_params(p)
    Y, y_ssd, hs = _ssd_fwd(tag + "_ssd", xc, proj, dtc, dtr, pcol, prow, p["ssd_norm"][None], B)
    qn = _headnorm_fwd(tag + "_qn", proj, COL_Q // 1024, p["q_norm"][None])
    kn = _headnorm_fwd(tag + "_kn", proj, COL_K // 1024, p["k_norm"][None])
    qT = (qn * ATT_SCALE).T
    oT, lse = _att_fwd(tag + "_att", kn, qT, proj[:, COL_V:COL_V + 1024].T.astype(BF16), bias, B)
    ymix = jnp.concatenate([y_ssd, oT.T], axis=1)
    rest = weights("rest", ymix)
    x2 = _mm(tag + "_out",
             [(ymix, pl.BlockSpec((ROW_T, MIX_SH), lambda i, n, k, j=j: (i, j)),
               rest["wout"], pl.BlockSpec((None, MIX_SH, D_MODEL), lambda i, n, k, j=j: (j, 0, 0)))
              for j in range(N_SHARD)],
             jax.ShapeDtypeStruct((T, D_MODEL), F32), pl.BlockSpec((ROW_T, D_MODEL), lambda i, n, k: (i, 0)),
             (nt, 1, 1), NN, (ROW_T, D_MODEL),
             res=(x1, pl.BlockSpec((ROW_T, D_MODEL), lambda i, n, k: (i, 0))))
    saved = dict(x1=x1, h2=h2, proj=proj, xc=xc, dtc=dtc, dtr=dtr, Y=Y, hs=hs,
                 qn=qn, kn=kn, qT=qT, oT=oT, lse=lse, ymix=ymix, win=win, cw=cw, wout=rest["wout"])
    return x2, saved


def _mixer_bwd(tag, dx2, sv, p, bias, B):
    T = dx2.shape[0]
    S = T // B
    nt = T // ROW_T
    sg = {}
    dymix = _dymix(tag + "_dymix", dx2, sv["wout"])
    tkw = min(TK_W, T)
    gwout = _mm(tag + "_dwout",
                [(sv["ymix"], pl.BlockSpec((tkw, MIX_SH), lambda m, n, k: (k, m)),
                  dx2, pl.BlockSpec((tkw, D_MODEL), lambda m, n, k: (k, 0)))],
                jax.ShapeDtypeStruct((N_SHARD, MIX_SH, D_MODEL), BF16),
                pl.BlockSpec((None, MIX_SH, D_MODEL), lambda m, n, k: (m, 0, 0)),
                (N_SHARD, 1, T // tkw), TN, (MIX_SH, D_MODEL))
    proj = sv["proj"]
    doT = dymix[:, 1024:].T
    dqn = _att_bwd_dq(tag + "_attdq", sv["kn"], sv["qT"], proj, sv["kn"].T, bias, doT, sv["oT"], sv["lse"], B).T
    dproj = lax.empty((T, IN_PAD), BF16)
    dkn, dproj = _att_bwd_dkv(tag + "_attdkv", sv["kn"], sv["qT"], proj, sv["qn"], bias, doT, sv["oT"], sv["lse"],
                              dymix, dproj, B)
    dproj, sg["q_norm"] = _headnorm_bwd(tag + "_qnb", dqn, proj, COL_Q // 1024, p["q_norm"][None], dproj)
    dproj, sg["k_norm"] = _headnorm_bwd(tag + "_knb", dkn, proj, COL_K // 1024, p["k_norm"][None], dproj)
    pcol, prow = _ssd_params(p)
    dxs, dB, dC, dproj, ddt, dpar, dnw = _ssd_bwd(tag + "_ssdb", dymix, sv["Y"], sv["xc"], proj, sv["dtc"], sv["dtr"],
                                                  pcol, prow, p["ssd_norm"][None], sv["hs"], dproj, B)
    dpar = jnp.sum(dpar, axis=0)
    sg["dt_bias"] = dpar[:, 0, :].reshape(SSD_HEADS)
    sg["a_log"] = dpar[:, 1, :].reshape(SSD_HEADS)
    sg["d_skip"] = dpar[:, 2, :].reshape(SSD_HEADS)
    sg["ssd_norm"] = jnp.sum(dnw, axis=0)
    dproj, sg["conv_w"], sg["conv_b"] = _conv_bwd(tag + "_convb", proj, dxs, dB, dC, sv["cw"], p["conv_b"][None],
                                                  dproj, B)
    ddt16 = jnp.transpose(ddt, (1, 0, 2)).reshape(T, SSD_HEADS)
    dproj = lax.dynamic_update_slice(dproj, jnp.pad(ddt16, ((0, 0), (0, IN_PAD - COL_DT - SSD_HEADS))).astype(BF16),
                                     (0, COL_DT))
    win = sv["win"]
    gwin = _mm(tag + "_dwin",
               [(sv["h2"], pl.BlockSpec((tkw, D_MODEL), lambda n, m, k: (k, 0)),
                 dproj, pl.BlockSpec((tkw, PROJ_TN), lambda n, m, k: (k, n)))],
               jax.ShapeDtypeStruct((D_MODEL, IN_PAD), BF16), pl.BlockSpec((D_MODEL, PROJ_TN), lambda n, m, k: (0, n)),
               (IN_PAD // PROJ_TN, 1, T // tkw), TN, (D_MODEL, PROJ_TN))
    dh2 = _mm(tag + "_dh2",
              [(dproj, pl.BlockSpec((ROW_T, PROJ_TN), lambda i, n, k, j=j: (i, j)),
                win, pl.BlockSpec((D_MODEL, PROJ_TN), lambda i, n, k, j=j: (0, j))) for j in range(IN_PAD // PROJ_TN)],
              jax.ShapeDtypeStruct((T, D_MODEL), F32), pl.BlockSpec((ROW_T, D_MODEL), lambda i, n, k: (i, 0)),
              (nt, 1, 1), NT, (ROW_T, D_MODEL))
    dx1, sg["mix_norm"] = _rms_bwd(tag + "_mixrmsb", dh2, sv["x1"], p["mix_norm"][None], dx2)
    return dx1, sg, gwout, gwin


def _win_pack(w):
    return jnp.concatenate([w[:, :3072], w[:, 3088:], w[:, 3072:3088],
                            jnp.zeros((w.shape[0], IN_PAD - IN_PROJ), w.dtype)], axis=1)


def _win_unpack(g):
    return jnp.concatenate([g[:, :3072], g[:, COL_DT:COL_DT + SSD_HEADS], g[:, 3072:COL_DT]], axis=1)


DT_LO = IN_SH * 2 - COL_Q


def _win_from_shards(sh):
    main = IN_SH - DT_LO
    return jnp.concatenate([sh[0], sh[1][:, :main], sh[2][:, SSD_HEADS - DT_LO:], sh[3], sh[1][:, main:],
                            sh[2][:, :SSD_HEADS - DT_LO], jnp.zeros((sh.shape[1], IN_PAD - IN_PROJ), sh.dtype)], axis=1)


def _win_to_shards(g):
    main = IN_SH - DT_LO
    a, b = IN_SH + main, IN_SH + 2 * main
    return jnp.stack([g[:, :IN_SH],
                      jnp.concatenate([g[:, IN_SH:a], g[:, COL_DT:COL_DT + DT_LO]], axis=1),
                      jnp.concatenate([g[:, COL_DT + DT_LO:COL_DT + SSD_HEADS], g[:, a:b]], axis=1),
                      g[:, b:COL_DT]])


def _local_step(x, target, small, weights, scatter, B):
    T = x.shape[0]
    bias = _att_bias((T // B) // ATT_B)
    saved = []
    h = x
    for l in range(DEPTH):
        tag = "l%d" % l
        p = {k: v[l] for k, v in small.items()}
        w1 = weights(l, "ffn1", h)
        x1, ffn1, d1 = _ffn_fwd(tag + "f1", h, p["ffn1_norm"][None], w1["g1"], w1["u1"],
                                lambda after, l=l: weights(l, "ffn1d", after)["d1"])
        x2, sv = _mixer_fwd(tag, x1, p, functools.partial(weights, l), bias, B)
        w2 = weights(l, "rest", x2)
        h, ffn2, _ = _ffn_fwd(tag + "f2", x2, p["ffn2_norm"][None], w2["g2"], w2["u2"], w2["d2"])
        saved.append((ffn1, sv, ffn2, dict(g1=w1["g1"], u1=w1["u1"], d1=d1), w2))
    d, lsum = _loss_grad("loss", h, target)
    sgrads = [None] * DEPTH
    for l in reversed(range(DEPTH)):
        tag = "l%db" % l
        p = {k: v[l] for k, v in small.items()}
        ffn1, sv, ffn2, w1, w2 = saved[l]
        sg = {}
        d, sg["ffn2_norm"] = _ffn_bwd(tag + "f2", d, ffn2, p["ffn2_norm"][None], w2["g2"], w2["u2"], w2["d2"],
                                      lambda gg, gu, gd, c, l=l: scatter(l, "ffn2", dict(g2=gg, u2=gu, d2=gd), c))
        d, sgm, gwout, gwin = _mixer_bwd(tag, d, sv, p, bias, B)
        sg.update(sgm)
        d = scatter(l, "mixer", dict(wout=gwout, win=gwin), d)
        d, sg["ffn1_norm"] = _ffn_bwd(tag + "f1", d, ffn1, p["ffn1_norm"][None], w1["g1"], w1["u1"], w1["d1"],
                                      lambda gg, gu, gd, c, l=l: scatter(l, "ffn1", dict(g1=gg, u1=gu, d1=gd), c))
        sgrads[l] = sg
    return lsum, d, sgrads


MESH = pl.DeviceIdType.MESH
ANY = pl.BlockSpec(memory_space=pl.ANY)


def _place():
    return lax.axis_index("x"), lax.axis_index("y"), lax.axis_index("c")


def _other_chips(x, y):
    return [(1 - x, y), (x, 1 - y), (1 - x, 1 - y)]


HBM = pl.BlockSpec(memory_space=pltpu.HBM)
SEM = pl.BlockSpec(memory_space=pltpu.SEMAPHORE)
EFFECT = pltpu.SideEffectType.DATAFLOW_SIDE_EFFECTING


def _hbm(a):
    return pltpu.with_memory_space_constraint(a, pltpu.HBM)


def _exchange(gather, layer, src, land, send, recv, n, act):
    x, y, c = _place()
    for k, (px, py) in enumerate(_other_chips(x, y)):
        for a in range(n):
            if gather:
                s_out, d_out, d_in = src[a].at[layer], land[a].at[2 * x + y], land[a].at[2 * px + py]
            else:
                s_out, d_out, d_in = src[a].at[2 * px + py], land[a].at[k], land[a].at[k]
            act(pltpu.make_async_remote_copy(
                src_ref=s_out, dst_ref=d_out if act is _start else d_in, send_sem=send.at[k * n + a],
                recv_sem=recv.at[k * n + a], device_id=(px, py, c), device_id_type=MESH))


def _start(cp):
    cp.start()


def _finish(cp):
    cp.wait_send()
    cp.wait_recv()


def _exchange_start(name, gather, layer, srcs, carry):
    n = len(srcs)
    lands = [lax.empty(((N_SHARD,) + s.shape[1:]) if gather else ((3,) + s.shape[1:]), s.dtype) for s in srcs]

    def body(*refs):
        _exchange(gather, layer, refs[:n], refs[n:2 * n], refs[2 * n + 1], refs[2 * n + 2], n, _start)

    srcs = [_hbm(a) for a in srcs]
    thru = [_hbm(a) for a in lands + [carry]]
    out = pl.pallas_call(
        body, name=name,
        out_shape=(pltpu.SemaphoreType.DMA((3 * n,)), pltpu.SemaphoreType.DMA((3 * n,)),
                   *[pltpu.HBM(a.shape, a.dtype) for a in thru]),
        in_specs=[HBM] * (2 * n + 1), out_specs=(SEM, SEM, *[HBM] * (n + 1)),
        input_output_aliases={n + i: 2 + i for i in range(n + 1)},
        compiler_params=pltpu.CompilerParams(has_side_effects=EFFECT))(*srcs, *thru)
    return dict(gather=gather, layer=layer, send=out[0], recv=out[1], srcs=srcs, lands=list(out[2:2 + n])), out[-1]


def _exchange_wait(name, ex, after):
    n = len(ex["srcs"])

    def body(*refs):
        _exchange(ex["gather"], ex["layer"], refs[:n], refs[n:2 * n], refs[2 * n], refs[2 * n + 1], n, _finish)

    out = pl.pallas_call(
        body, name=name, out_shape=[pltpu.HBM(a.shape, a.dtype) for a in ex["lands"]],
        in_specs=[HBM] * (2 * n) + [SEM, SEM, ANY], out_specs=[HBM] * n,
        input_output_aliases={n + i: i for i in range(n)},
        compiler_params=pltpu.CompilerParams(has_side_effects=EFFECT))(
            *ex["srcs"], *ex["lands"], ex["send"], ex["recv"], after)
    return list(out)


def _swap_sibling(parts):
    n = len(parts)

    def body(*refs):
        src, dst = refs[:n], refs[n:2 * n]
        send, recv = refs[2 * n:]
        x, y, c = _place()
        cps = [pltpu.make_async_remote_copy(src_ref=src[a], dst_ref=dst[a], send_sem=send.at[a], recv_sem=recv.at[a],
                                            device_id=(x, y, 1 - c), device_id_type=MESH) for a in range(n)]
        for cp in cps:
            cp.start()
        for cp in cps:
            cp.wait_recv()
        for cp in cps:
            cp.wait_send()

    return pl.pallas_call(
        body, out_shape=[jax.ShapeDtypeStruct(p.shape, p.dtype) for p in parts],
        in_specs=[ANY] * n, out_specs=[ANY] * n,
        scratch_shapes=[pltpu.SemaphoreType.DMA((n,)), pltpu.SemaphoreType.DMA((n,))],
        name="swap_sibling")(*parts)


def _allreduce_small(name, v):
    R = v.shape[0]

    def body(v_ref, o_ref, buf, send, recv):
        x, y, c = _place()
        me = 4 * x + 2 * y + c
        buf[me] = v_ref[...]
        cps = []
        for k in range(1, 8):
            fx, fy, fc = (k >> 2) & 1, (k >> 1) & 1, k & 1
            px = 1 - x if fx else x
            py = 1 - y if fy else y
            pc = 1 - c if fc else c
            cp = pltpu.make_async_remote_copy(src_ref=v_ref, dst_ref=buf.at[me], send_sem=send.at[k - 1],
                                              recv_sem=recv.at[k - 1], device_id=(px, py, pc), device_id_type=MESH)
            cp.start()
            cps.append((cp, 4 * px + 2 * py + pc))
        for k, (cp, peer) in enumerate(cps):
            pltpu.make_async_remote_copy(src_ref=v_ref, dst_ref=buf.at[peer], send_sem=send.at[k], recv_sem=recv.at[k],
                                         device_id=(x, y, c), device_id_type=MESH).wait_recv()
        for cp, _ in cps:
            cp.wait_send()
        acc = buf[0]
        for d in range(1, 8):
            acc = acc + buf[d]
        o_ref[...] = acc

    return pl.pallas_call(
        body, out_shape=jax.ShapeDtypeStruct((R, 128), F32),
        in_specs=[pl.BlockSpec(memory_space=pltpu.VMEM)], out_specs=pl.BlockSpec(memory_space=pltpu.VMEM),
        scratch_shapes=[pltpu.VMEM((8, R, 128), F32), pltpu.SemaphoreType.DMA((7,)), pltpu.SemaphoreType.DMA((7,))],
        name=name)(v)


def _row_tile(r):
    for t in (256, 128, 64, 32, 16, 8):
        if r % t == 0:
            return t
    raise ValueError(r)


def _sum4(name, me, parts, got):
    _, R, C = parts.shape
    tr = _row_tile(R)

    def body(me_ref, o_ref, g_ref, s_ref):
        s = o_ref[...].astype(F32)
        for k in range(3):
            s = s + g_ref[k].astype(F32)
        s_ref[...] = s.astype(BF16)

    return pl.pallas_call(
        body, out_shape=jax.ShapeDtypeStruct((R, C), BF16),
        grid_spec=pltpu.PrefetchScalarGridSpec(
            num_scalar_prefetch=1, grid=(R // tr,),
            in_specs=[pl.BlockSpec((None, tr, C), lambda i, me_ref: (me_ref[0], i, 0)),
                      pl.BlockSpec((3, tr, C), lambda i, me_ref: (0, i, 0))],
            out_specs=pl.BlockSpec((tr, C), lambda i, me_ref: (i, 0))),
        name=name, compiler_params=_cp("parallel"))(me, parts, got)


def _adamw(name, w, gparts, m, v):
    R, C = w.shape
    tr = _row_tile(R)
    ng = len(gparts)
    c1 = 1.0 - ADAM_B1 ** ADAM_STEP
    c2 = 1.0 - ADAM_B2 ** ADAM_STEP

    def body(*refs):
        w_ref = refs[0]
        g_refs = refs[1:1 + ng]
        m_ref, v_ref, go_ref, d_ref, mo_ref, vo_ref = refs[1 + ng:]
        g = g_refs[0][...]
        for r in g_refs[1:]:
            g = g + r[...]
        mn = ADAM_B1 * m_ref[...] + (1.0 - ADAM_B1) * g
        vn = ADAM_B2 * v_ref[...] + (1.0 - ADAM_B2) * (g * g)
        go_ref[...] = g
        mo_ref[...] = mn
        vo_ref[...] = vn
        d_ref[...] = -ADAM_LR * ((mn / c1) / (jnp.sqrt(vn / c2) + ADAM_EPS) + ADAM_WD * w_ref[...])

    blk = pl.BlockSpec((tr, C), lambda i: (i, 0))
    osh = jax.ShapeDtypeStruct((R, C), F32)
    return pl.pallas_call(
        body, out_shape=(osh, osh, osh, osh), grid=(R // tr,), in_specs=[blk] * (3 + ng), out_specs=(blk,) * 4,
        name=name, compiler_params=_cp("parallel"))(w, *gparts, m, v)


def _adamw_layers(name, w, sums, m, v):
    R2, C = w.shape
    R = R2 // DEPTH
    tr = _row_tile(R)
    nr = R // tr
    c1 = 1.0 - ADAM_B1 ** ADAM_STEP
    c2 = 1.0 - ADAM_B2 ** ADAM_STEP

    def body(w_ref, a0, b0, a1, b1, m_ref, v_ref, go_ref, d_ref, mo_ref, vo_ref):
        f = lambda r: r[...].astype(F32)
        g = jnp.where(pl.program_id(0) == 0, f(a0) + f(b0), f(a1) + f(b1))
        mn = ADAM_B1 * m_ref[...] + (1.0 - ADAM_B1) * g
        vn = ADAM_B2 * v_ref[...] + (1.0 - ADAM_B2) * (g * g)
        go_ref[...] = g
        mo_ref[...] = mn
        vo_ref[...] = vn
        d_ref[...] = -ADAM_LR * ((mn / c1) / (jnp.sqrt(vn / c2) + ADAM_EPS) + ADAM_WD * w_ref[...])

    blk = pl.BlockSpec((tr, C), lambda l, i: (l * nr + i, 0))
    lay0 = pl.BlockSpec((tr, C), lambda l, i: (jnp.where(l == 0, i, nr - 1), 0))
    lay1 = pl.BlockSpec((tr, C), lambda l, i: (jnp.where(l == 1, i, 0), 0))
    osh = jax.ShapeDtypeStruct((R2, C), F32)
    return pl.pallas_call(
        body, out_shape=(osh, osh, osh, osh), grid=(DEPTH, nr),
        in_specs=[blk, lay0, lay0, lay1, lay1, blk, blk], out_specs=(blk,) * 4,
        name=name, compiler_params=_cp("arbitrary", "arbitrary"))(w, *sums[0], *sums[1], m, v)


BIG = [("ffn1_w_gate", "g1"), ("ffn1_w_up", "u1"), ("ffn1_w_down", "d1"), ("w_in", "win"), ("w_out", "wout"),
       ("ffn2_w_gate", "g2"), ("ffn2_w_up", "u2"), ("ffn2_w_down", "d2")]
SMALL = ["ffn1_norm", "mix_norm", "conv_b", "dt_bias", "a_log", "d_skip", "ssd_norm", "q_norm", "k_norm", "ffn2_norm"]
WEIGHTS = ["ffn1_norm", "ffn1_w_gate", "ffn1_w_up", "ffn1_w_down", "mix_norm", "w_in", "conv_w", "conv_b", "dt_bias",
           "a_log", "d_skip", "ssd_norm", "q_norm", "k_norm", "w_out", "ffn2_norm", "ffn2_w_gate", "ffn2_w_up",
           "ffn2_w_down"]
CONV_SH = CONV_DIM // N_SHARD
GATHER_GROUPS = [(0, "ffn1", ["g1", "u1"]), (0, "ffn1d", ["d1"]), (0, "win", ["win", "cw"]),
                 (0, "rest", ["wout", "g2", "u2", "d2"]),
                 (1, "all", ["g1", "u1", "d1", "win", "cw", "wout", "g2", "u2", "d2"])]


def _pad128(v):
    v = v.reshape(-1)
    return jnp.pad(v, (0, (-v.shape[0]) % 128))


def _pack(pieces):
    flat, offs, pos = [], [], 0
    for p in pieces:
        q = _pad128(p.astype(F32))
        offs.append(pos)
        pos += q.shape[0] // 128
        flat.append(q)
    total = -(-pos // 8) * 8
    out = jnp.concatenate(flat + [jnp.zeros(((total - pos) * 128,), F32)]).reshape(total, 128)
    return out, offs


def _unpack(packed, offs, shapes):
    out = []
    for off, shp in zip(offs, shapes):
        n = int(np.prod(shp))
        rows = -(-n // 128)
        out.append(packed[off:off + rows].reshape(-1)[:n].reshape(shp))
    return out


def kernel(x, ffn1_norm, ffn1_w_gate, ffn1_w_up, ffn1_w_down, mix_norm, w_in, conv_w, conv_b, dt_bias, a_log, d_skip, ssd_norm, q_norm, k_norm, w_out, ffn2_norm, ffn2_w_gate, ffn2_w_up, ffn2_w_down, loss_target, m_ffn1_norm, m_ffn1_w_gate, m_ffn1_w_up, m_ffn1_w_down, m_mix_norm, m_w_in, m_conv_w, m_conv_b, m_dt_bias, m_a_log, m_d_skip, m_ssd_norm, m_q_norm, m_k_norm, m_w_out, m_ffn2_norm, m_ffn2_w_gate, m_ffn2_w_up, m_ffn2_w_down, v_ffn1_norm, v_ffn1_w_gate, v_ffn1_w_up, v_ffn1_w_down, v_mix_norm, v_w_in, v_conv_w, v_conv_b, v_dt_bias, v_a_log, v_d_skip, v_ssd_norm, v_q_norm, v_k_norm, v_w_out, v_ffn2_norm, v_ffn2_w_gate, v_ffn2_w_up, v_ffn2_w_down):
    A = dict(locals())
    ix, iy, ic = _place()
    me = 2 * ix + iy
    B, S, _ = x.shape
    T = B * S

    own = {key: A[name].astype(BF16) for name, key in BIG}
    own["cw"] = conv_w
    exs, first_norm = [], ffn1_norm
    for gi, (l, _, keys) in enumerate(GATHER_GROUPS):
        ex, first_norm = _exchange_start("gather_start%d" % gi, True, l, [own[key] for key in keys], first_norm)
        exs.append(ex)
    landed = {}

    def weights(l, group, after):
        gi = [i for i, (gl, gname, _) in enumerate(GATHER_GROUPS) if gl == l and gname in (group, "all")][0]
        if gi not in landed:
            lands = _exchange_wait("gather_wait%d" % gi, exs[gi], after)
            landed[gi] = {}
            for key, land in zip(GATHER_GROUPS[gi][2], lands):
                full = lax.dynamic_update_slice(land, own[key][l][None], (me, 0, 0))
                if key == "win":
                    full = _win_from_shards(full)
                if key == "cw":
                    full = jnp.transpose(full, (1, 0, 2)).reshape(CONV_K, CONV_DIM)
                landed[gi][key] = full
        return landed[gi]

    pending = []

    def scatter(l, group, grads, carry):
        keys = sorted(grads)
        arrs = [grads[key] for key in keys]
        if "win" in grads:
            arrs[keys.index("win")] = _win_to_shards(grads["win"])
        ex, carry = _exchange_start("scatter_start_l%d_%s" % (l, group), False, None, arrs, carry)
        pending.append((l, keys, ex))
        return carry

    small = {name: A[name] for name in SMALL}
    small["ffn1_norm"] = first_norm
    lsum, dx, sgrads = _local_step(x.reshape(T, D_MODEL), loss_target.reshape(T, D_MODEL), small, weights, scatter, B)

    names = SMALL + ["conv_w"]
    shapes = [A[n].shape for n in SMALL] + [(DEPTH, CONV_K, CONV_DIM), ()]
    pieces = [jnp.stack([sgrads[l][n].reshape(shp[1:]) for l in range(DEPTH)]) for n, shp in zip(names, shapes)]
    pieces.append(0.5 / D_MODEL * jnp.sum(lsum))
    packed, offs = _pack(pieces)
    red = _allreduce_small("allreduce_small", packed)
    red = _unpack(red, offs, shapes)
    loss = red[-1]
    sg = dict(zip(names, red[:-1]))

    sums, after = {}, dx
    me1 = jnp.reshape(me, (1,)).astype(jnp.int32)
    for idx, (l, keys, ex) in enumerate(pending):
        lands = _exchange_wait("scatter_wait%d" % idx, ex, after)
        for key, g, got in zip(keys, ex["srcs"], lands):
            sums[key, l] = after = _sum4("sum_%s_l%d" % (key, l), me1, g, got)
    order = [(key, l) for _, key in BIG for l in range(DEPTH)]
    theirs = dict(zip(order, _swap_sibling([sums[k] for k in order])))

    out = {}
    for name, key in BIG:
        shp = A[name].shape
        flat = lambda a: a.reshape(shp[0] * shp[1], shp[2])
        res = _adamw_layers("adamw_" + key, flat(A[name]), [(sums[key, l], theirs[key, l]) for l in range(DEPTH)],
                            flat(A["m_" + name]), flat(A["v_" + name]))
        out[name] = [r.reshape(shp) for r in res]

    wp, offs = _pack([A[n] for n in SMALL])
    gp, _ = _pack([sg[n] for n in SMALL])
    mp, _ = _pack([A["m_" + n] for n in SMALL])
    vp, _ = _pack([A["v_" + n] for n in SMALL])
    res = _adamw("adamw_small", wp, [gp], mp, vp)
    shapes = [A[n].shape for n in SMALL]
    res = [_unpack(r, offs, shapes) for r in res]
    for i, n in enumerate(SMALL):
        out[n] = [res[q][i] for q in range(4)]
    gcw = lax.dynamic_slice_in_dim(sg["conv_w"], me * CONV_SH, CONV_SH, axis=2)
    flat = lambda a: a.reshape(DEPTH * CONV_K, CONV_SH)
    res = _adamw("adamw_conv_w", flat(conv_w), [flat(gcw)], flat(m_conv_w), flat(v_conv_w))
    out["conv_w"] = [r.reshape(conv_w.shape) for r in res]

    outs = [loss, dx.reshape(B, S, D_MODEL)]
    for q in range(4):
        outs += [out[n][q] for n in WEIGHTS]
    return tuple(outs)
```

```python
import functools
import math

import numpy as np
import jax
import jax.numpy as jnp
from jax import lax
from jax.experimental import pallas as pl
from jax.experimental.pallas import tpu as pltpu

F32 = jnp.float32
BF16 = jnp.bfloat16

D_MODEL = 1024
DEPTH = 2
N_SHARD = 4
D_FF = 2816
FF_SH = D_FF // N_SHARD
SSD_HEADS = 16
HEAD_DIM = 64
SSD_GROUPS = 4
GROUP_W = 256
SSD_STATE = 128
CONV_K = 4
CONV_DIM = 2048
ATT_HEADS = 16
MIX_W = 2048
MIX_SH = MIX_W // N_SHARD
IN_PROJ = 6160
IN_SH = IN_PROJ // N_SHARD
IN_PAD = 6272
PROJ_TN = 896
COL_Z, COL_XBC, COL_Q, COL_K, COL_V, COL_DT = 0, 1024, 3072, 4096, 5120, 6144
EPS = 1e-6
NEG = -1e30
SSD_L = 256
ATT_B = 256
ROW_T = 512
HALF_T = ROW_T // 2
TK_W = 2048
CONV_CT = 256
CONV_R = 256
PAD_R = 8

ADAM_LR, ADAM_B1, ADAM_B2, ADAM_EPS, ADAM_WD, ADAM_STEP = 0.001, 0.9, 0.999, 1e-08, 0.01, 10

NN = (((1,), (0,)), ((), ()))
NT = (((1,), (1,)), ((), ()))
TN = (((0,), (0,)), ((), ()))

VMEM_LIMIT = 56 * 1024 * 1024


def _cp(*sem):
    return pltpu.CompilerParams(dimension_semantics=sem, vmem_limit_bytes=VMEM_LIMIT)


def _dot(a, b, dims):
    return lax.dot_general(a, b, dims, preferred_element_type=F32)


def _sigmoid(x):
    return 0.5 * jnp.tanh(0.5 * x) + 0.5


def _softplus(x):
    return jnp.maximum(x, 0.0) + jnp.log(1.0 + jnp.exp(-jnp.abs(x)))


def _mm(name, pairs, out_shape, out_spec, grid, dims, acc_shape, res=None, scale=1.0):
    nk = grid[2]
    npair = len(pairs)

    def body(*refs):
        ab = refs[:2 * npair]
        pos = 2 * npair
        res_ref = None
        if res is not None:
            res_ref = refs[pos]
            pos += 1
        out_ref = refs[pos]
        s = None
        for p in range(npair):
            d = _dot(ab[2 * p][...].astype(BF16), ab[2 * p + 1][...].astype(BF16), dims)
            s = d if s is None else s + d

        def finish(r):
            if scale != 1.0:
                r = r * scale
            if res_ref is not None:
                r = r + res_ref[...]
            out_ref[...] = r.astype(out_ref.dtype)

        if nk == 1:
            finish(s)
            return
        acc = refs[pos + 1]
        k = pl.program_id(2)

        @pl.when(k == 0)
        def _():
            acc[...] = s

        @pl.when(k > 0)
        def _():
            acc[...] += s

        @pl.when(k == nk - 1)
        def _():
            finish(acc[...])

    args, specs = [], []
    for a, a_spec, b, b_spec in pairs:
        args += [a, b]
        specs += [a_spec, b_spec]
    if res is not None:
        args.append(res[0])
        specs.append(res[1])
    return pl.pallas_call(
        body, out_shape=out_shape, grid=grid, in_specs=specs, out_specs=out_spec,
        scratch_shapes=[] if nk == 1 else [pltpu.VMEM(acc_shape, F32)], name=name,
        compiler_params=_cp("parallel", "parallel", "arbitrary"))(*args)


def _rms_fwd(name, x, w):
    T = x.shape[0]

    def body(x_ref, w_ref, o_ref):
        xv = x_ref[...]
        r = lax.rsqrt(jnp.mean(xv * xv, axis=-1, keepdims=True) + EPS)
        o_ref[...] = (xv * r * w_ref[...]).astype(BF16)

    return pl.pallas_call(
        body, out_shape=jax.ShapeDtypeStruct((T, D_MODEL), BF16), grid=(T // ROW_T,),
        in_specs=[pl.BlockSpec((ROW_T, D_MODEL), lambda i: (i, 0)), pl.BlockSpec((1, D_MODEL), lambda i: (0, 0))],
        out_specs=pl.BlockSpec((ROW_T, D_MODEL), lambda i: (i, 0)), name=name, compiler_params=_cp("parallel"))(x, w)


def _rms_bwd(name, dh, x, w, dres):
    T = x.shape[0]

    def body(dh_ref, x_ref, w_ref, dres_ref, dx_ref, dw_ref):
        @pl.when(pl.program_id(0) == 0)
        def _():
            dw_ref[...] = jnp.zeros_like(dw_ref)

        xv = x_ref[...]
        r = lax.rsqrt(jnp.mean(xv * xv, axis=-1, keepdims=True) + EPS)
        xhat = xv * r
        dhv = dh_ref[...]
        dxhat = dhv * w_ref[...]
        m = jnp.mean(dxhat * xhat, axis=-1, keepdims=True)
        dx_ref[...] = dres_ref[...] + r * (dxhat - xhat * m)
        dw_ref[...] += jnp.sum(dhv * xhat, axis=0, keepdims=True)

    row = pl.BlockSpec((ROW_T, D_MODEL), lambda i: (i, 0))
    vec = pl.BlockSpec((1, D_MODEL), lambda i: (0, 0))
    return pl.pallas_call(
        body, out_shape=(jax.ShapeDtypeStruct((T, D_MODEL), F32), jax.ShapeDtypeStruct((1, D_MODEL), F32)),
        grid=(T // ROW_T,), in_specs=[row, row, vec, row], out_specs=(row, vec), name=name,
        compiler_params=_cp("arbitrary"))(dh, x, w, dres)


def _loss_grad(name, y, t):
    T = y.shape[0]

    def body(y_ref, t_ref, dy_ref, l_ref):
        @pl.when(pl.program_id(0) == 0)
        def _():
            l_ref[...] = jnp.zeros_like(l_ref)

        e = y_ref[...] - t_ref[...]
        dy_ref[...] = e * (1.0 / D_MODEL)
        l_ref[...] += jnp.sum(e * e, axis=0, keepdims=True)

    row = pl.BlockSpec((ROW_T, D_MODEL), lambda i: (i, 0))
    vec = pl.BlockSpec((1, D_MODEL), lambda i: (0, 0))
    return pl.pallas_call(
        body, out_shape=(jax.ShapeDtypeStruct((T, D_MODEL), F32), jax.ShapeDtypeStruct((1, D_MODEL), F32)),
        grid=(T // ROW_T,), in_specs=[row, row], out_specs=(row, vec), name=name,
        compiler_params=_cp("arbitrary"))(y, t)


def _ffn_gate_up(name, h, wg, wu):
    T = h.shape[0]

    def body(h_ref, wg_ref, wu_ref, dgf_ref, duf_ref, a_ref):
        for r in range(0, ROW_T, HALF_T):
            rows = slice(r, r + HALF_T)
            hv = h_ref[rows, :]
            g = _dot(hv, wg_ref[...], NN)
            u = _dot(hv, wu_ref[...], NN)
            sg = _sigmoid(g)
            silu = g * sg
            dgf_ref[rows, :] = (u * (sg * (1.0 + g * (1.0 - sg)))).astype(BF16)
            duf_ref[rows, :] = silu.astype(BF16)
            a_ref[rows, :] = (silu * u).astype(BF16)

    wspec = pl.BlockSpec((None, D_MODEL, FF_SH), lambda j, i: (j, 0, 0))
    ospec = pl.BlockSpec((None, ROW_T, FF_SH), lambda j, i: (j, i, 0))
    osh = jax.ShapeDtypeStruct((N_SHARD, T, FF_SH), BF16)
    return pl.pallas_call(
        body, out_shape=(osh, osh, osh), grid=(N_SHARD, T // ROW_T),
        in_specs=[pl.BlockSpec((ROW_T, D_MODEL), lambda j, i: (i, 0)), wspec, wspec],
        out_specs=(ospec, ospec, ospec), name=name, compiler_params=_cp("parallel", "parallel"))(h, wg, wu)


def _ffn_dact(name, dx, wd, g, u):
    T = dx.shape[0]

    def body(dx_ref, wd_ref, g_ref, u_ref, dg_ref, du_ref):
        for r in range(0, ROW_T, HALF_T):
            rows = slice(r, r + HALF_T)
            da = 0.5 * _dot(dx_ref[rows, :].astype(BF16), wd_ref[...], NT)
            dg_ref[rows, :] = (da * g_ref[rows, :].astype(F32)).astype(BF16)
            du_ref[rows, :] = (da * u_ref[rows, :].astype(F32)).astype(BF16)

    aspec = pl.BlockSpec((None, ROW_T, FF_SH), lambda j, i: (j, i, 0))
    osh = jax.ShapeDtypeStruct((N_SHARD, T, FF_SH), BF16)
    return pl.pallas_call(
        body, out_shape=(osh, osh), grid=(N_SHARD, T // ROW_T),
        in_specs=[pl.BlockSpec((ROW_T, D_MODEL), lambda j, i: (i, 0)),
                  pl.BlockSpec((None, FF_SH, D_MODEL), lambda j, i: (j, 0, 0)), aspec, aspec],
        out_specs=(aspec, aspec), name=name, compiler_params=_cp("parallel", "parallel"))(dx, wd, g, u)


def _ffn_fwd(tag, x, nw, wg, wu, wd):
    T = x.shape[0]
    h = _rms_fwd(tag + "_rms", x, nw)
    g, u, a = _ffn_gate_up(tag + "_gu", h, wg, wu)
    if callable(wd):
        wd = wd(a)
    nt = T // ROW_T
    xo = _mm(tag + "_down",
             [(a, pl.BlockSpec((None, ROW_T, FF_SH), lambda i, n, k, j=j: (j, i, 0)),
               wd, pl.BlockSpec((None, FF_SH, D_MODEL), lambda i, n, k, j=j: (j, 0, 0))) for j in range(N_SHARD)],
             jax.ShapeDtypeStruct((T, D_MODEL), F32), pl.BlockSpec((ROW_T, D_MODEL), lambda i, n, k: (i, 0)),
             (nt, 1, 1), NN, (ROW_T, D_MODEL),
             res=(x, pl.BlockSpec((ROW_T, D_MODEL), lambda i, n, k: (i, 0))), scale=0.5)
    return xo, (x, h, g, u, a), wd


def _ffn_bwd(tag, dxo, saved, nw, wg, wu, wd, emit):
    x, h, g, u, a = saved
    T = x.shape[0]
    nt = T // ROW_T
    tkw = min(TK_W, T)
    nw_t = T // tkw
    dg, du = _ffn_dact(tag + "_dact", dxo, wd, g, u)
    actw = lambda f: pl.BlockSpec((None, tkw, FF_SH), f)
    gd = _mm(tag + "_dwd",
             [(a, actw(lambda m, n, k: (m, k, 0)), dxo, pl.BlockSpec((tkw, D_MODEL), lambda m, n, k: (k, 0)))],
             jax.ShapeDtypeStruct((N_SHARD, FF_SH, D_MODEL), BF16),
             pl.BlockSpec((None, FF_SH, D_MODEL), lambda m, n, k: (m, 0, 0)),
             (N_SHARD, 1, nw_t), TN, (FF_SH, D_MODEL), scale=0.5)
    hspec = pl.BlockSpec((tkw, D_MODEL), lambda j, n, k: (k, 0))
    gsh = jax.ShapeDtypeStruct((N_SHARD, D_MODEL, FF_SH), BF16)
    gspec = pl.BlockSpec((None, D_MODEL, FF_SH), lambda j, n, k: (j, 0, 0))
    gg = _mm(tag + "_dwg", [(h, hspec, dg, actw(lambda j, n, k: (j, k, 0)))], gsh, gspec,
             (N_SHARD, 1, nw_t), TN, (D_MODEL, FF_SH))
    gu = _mm(tag + "_dwu", [(h, hspec, du, actw(lambda j, n, k: (j, k, 0)))], gsh, gspec,
             (N_SHARD, 1, nw_t), TN, (D_MODEL, FF_SH))
    dg = emit(gg, gu, gd, dg)
    act = lambda j: pl.BlockSpec((None, ROW_T, FF_SH), lambda i, n, k: (j, i, 0))
    wsp = lambda j: pl.BlockSpec((None, D_MODEL, FF_SH), lambda i, n, k: (j, 0, 0))
    dh = _mm(tag + "_dh",
             [(dd, act(j), w, wsp(j)) for j in range(N_SHARD) for dd, w in ((dg, wg), (du, wu))],
             jax.ShapeDtypeStruct((T, D_MODEL), F32), pl.BlockSpec((ROW_T, D_MODEL), lambda i, n, k: (i, 0)),
             (nt, 1, 1), NT, (ROW_T, D_MODEL))
    return _rms_bwd(tag + "_rmsb", dh, x, nw, dxo)


def _seq_rows(ref, start, size, S):
    lo, hi = max(start, 0), min(start + size, S)
    parts = [ref[pl.ds(lo, hi - lo), :]]
    if lo > start:
        parts.insert(0, jnp.zeros((lo - start, ref.shape[1]), F32))
    if start + size > hi:
        parts.append(jnp.zeros((start + size - hi, ref.shape[1]), F32))
    return parts[0] if len(parts) == 1 else jnp.concatenate(parts, axis=0)


XBC_CB = COL_XBC // CONV_CT


def _conv_fwd(name, proj, w, b, B):
    T = proj.shape[0]
    S = T // B
    C = CONV_DIM

    def body(x_ref, w_ref, b_ref, o_ref):
        wv = w_ref[...]
        for c in range(S // CONV_R):
            r0 = c * CONV_R
            ch = _seq_rows(x_ref, r0 - PAD_R, CONV_R + PAD_R, S)
            pre = ch[PAD_R:] * wv[3:4] + b_ref[...]
            for s in range(1, CONV_K):
                pre = pre + pltpu.roll(ch, s, axis=0)[PAD_R:] * wv[3 - s:4 - s]
            o_ref[pl.ds(r0, CONV_R), :] = pre * _sigmoid(pre)

    return pl.pallas_call(
        body, out_shape=jax.ShapeDtypeStruct((T, C), F32), grid=(B, C // CONV_CT),
        in_specs=[pl.BlockSpec((S, CONV_CT), lambda bi, ci: (bi, XBC_CB + ci)),
                  pl.BlockSpec((CONV_K, CONV_CT), lambda bi, ci: (0, ci)),
                  pl.BlockSpec((1, CONV_CT), lambda bi, ci: (0, ci))],
        out_specs=pl.BlockSpec((S, CONV_CT), lambda bi, ci: (bi, ci)), name=name,
        compiler_params=_cp("parallel", "parallel"))(proj, w, b)


def _conv_bwd(name, proj, dxs, dB, dC, w, b, dproj, B):
    T = proj.shape[0]
    S = T // B
    C = CONV_DIM
    RW = CONV_R + PAD_R
    nx, nb = dxs.shape[1] // CONV_CT, dB.shape[1] // CONV_CT

    def body(x_ref, dx_in, db_in, dc_in, w_ref, b_ref, buf_ref, dx_ref, dw_ref, db_ref):
        @pl.when(pl.program_id(1) == 0)
        def _():
            dw_ref[...] = jnp.zeros_like(dw_ref)
            db_ref[...] = jnp.zeros_like(db_ref)

        ci = pl.program_id(0)
        wv = w_ref[...]
        dw = [jnp.zeros((1, CONV_CT), F32) for _ in range(CONV_K)]
        db = jnp.zeros((1, CONV_CT), F32)
        for c in range(S // CONV_R):
            r0 = c * CONV_R
            ch = _seq_rows(x_ref, r0 - PAD_R, RW + PAD_R, S)
            xs = [ch[PAD_R:]] + [pltpu.roll(ch, s, axis=0)[PAD_R:] for s in range(1, CONV_K)]
            pre = b_ref[...] + xs[0] * wv[3:4]
            for s in range(1, CONV_K):
                pre = pre + xs[s] * wv[3 - s:4 - s]
            sg = _sigmoid(pre)
            dout = jnp.where(ci < nx, _seq_rows(dx_in, r0, RW, S),
                             jnp.where(ci < nx + nb, _seq_rows(db_in, r0, RW, S), _seq_rows(dc_in, r0, RW, S)))
            dpre = dout * (sg * (1.0 + pre * (1.0 - sg)))
            dx = dpre[:CONV_R] * wv[3:4]
            for s in range(1, CONV_K):
                dx = dx + pltpu.roll(dpre, RW - s, axis=0)[:CONV_R] * wv[3 - s:4 - s]
            dx_ref[pl.ds(r0, CONV_R), :] = dx.astype(BF16)
            dcur = dpre[:CONV_R]
            db = db + jnp.sum(dcur, axis=0, keepdims=True)
            for s in range(CONV_K):
                dw[3 - s] = dw[3 - s] + jnp.sum(dcur * xs[s][:CONV_R], axis=0, keepdims=True)
        db_ref[...] += db
        for k in range(CONV_K):
            dw_ref[k:k + 1, :] += dw[k]

    seq = lambda f: pl.BlockSpec((S, CONV_CT), f)
    return pl.pallas_call(
        body,
        out_shape=(jax.ShapeDtypeStruct(dproj.shape, dproj.dtype), jax.ShapeDtypeStruct((CONV_K, C), F32),
                   jax.ShapeDtypeStruct((1, C), F32)),
        grid=(C // CONV_CT, B),
        in_specs=[seq(lambda ci, bi: (bi, XBC_CB + ci)),
                  seq(lambda ci, bi: (bi, jnp.minimum(ci, nx - 1))),
                  seq(lambda ci, bi: (bi, jnp.clip(ci - nx, 0, nb - 1))),
                  seq(lambda ci, bi: (bi, jnp.clip(ci - nx - nb, 0, nb - 1))),
                  pl.BlockSpec((CONV_K, CONV_CT), lambda ci, bi: (0, ci)),
                  pl.BlockSpec((1, CONV_CT), lambda ci, bi: (0, ci)), ANY],
        out_specs=(seq(lambda ci, bi: (bi, XBC_CB + ci)),
                   pl.BlockSpec((CONV_K, CONV_CT), lambda ci, bi: (0, ci)),
                   pl.BlockSpec((1, CONV_CT), lambda ci, bi: (0, ci))),
        input_output_aliases={6: 0},
        name=name, compiler_params=_cp("parallel", "arbitrary"))(proj, dxs, dB, dC, w, b, dproj)


def _tri_sum(tri, x, dims, tri_first, terms=3):
    out, rest = None, x
    for t in range(terms):
        part = rest.astype(BF16)
        if t + 1 < terms:
            rest = rest - part.astype(F32)
        d = _dot(tri, part, dims) if tri_first else _dot(part, tri, dims)
        out = d if out is None else out + d
    return out


def _total(x):
    return jnp.sum(jnp.sum(x, axis=0, keepdims=True), axis=-1, keepdims=True)


def _ssd_common(dtc_ref, dtr_ref, pcol_ref, prow_ref, b_ref, c_ref):
    L = SSD_L
    bias_c, alog_c = pcol_ref[0:1, :], pcol_ref[1:2, :]
    a_c = -jnp.exp(alog_c)
    dt_c = _softplus(dtc_ref[...] + bias_c)
    row = lax.broadcasted_iota(jnp.int32, (L, L), 0)
    col = lax.broadcasted_iota(jnp.int32, (L, L), 1)
    causal = row >= col
    tri = causal.astype(BF16)
    cum_c = _tri_sum(tri, dt_c * a_c, NN, True)
    a_r = -jnp.exp(prow_ref[:, 1:2])
    dt_r = _softplus(dtr_ref[...] + prow_ref[:, 0:1])
    cum_r = _tri_sum(tri, dt_r * a_r, NT, False)
    bb = b_ref[...].astype(BF16)
    cb = c_ref[...].astype(BF16)
    G = _dot(cb, bb, NT)
    return a_c, dt_c, causal, tri, cum_c, cum_r, bb, cb, G


def _ssd_fwd(name, xc, proj, dtc, dtr, pcol, prow, nw, B):
    T = xc.shape[0]
    S = T // B
    nb = S // SSD_L
    L = SSD_L

    def body(xs_ref, b_ref, c_ref, z_ref, dtc_ref, dtr_ref, pcol_ref, prow_ref, nw_ref, y_ref, yn_ref, hs_ref, H, yo_s):
        @pl.when(pl.program_id(2) == 0)
        def _():
            H[...] = jnp.zeros_like(H)

        a_c, dt_c, causal, tri, cum_c, cum_r, bb, cb, G = _ssd_common(dtc_ref, dtr_ref, pcol_ref, prow_ref, b_ref, c_ref)
        dsk = pcol_ref[2:3, :]
        clast = cum_c[L - 1:L, :]
        bf = b_ref[...]
        for h in range(4):
            hs_ref[h] = H[h]
            yo_s[h] = _dot(cb, H[h].astype(BF16), NN)
        for h in range(4):
            sl = slice(HEAD_DIM * h, HEAD_DIM * (h + 1))
            cc = cum_c[:, h:h + 1]
            lm = jnp.exp(jnp.where(causal, cc - cum_r[h:h + 1, :], NEG))
            M = (G * lm).astype(BF16)
            xh = xs_ref[:, sl]
            Xb = (xh * dt_c[:, h:h + 1]).astype(BF16)
            Hh = H[h]
            y = _dot(M, Xb, NN) + jnp.exp(cc) * yo_s[h]
            y_ref[:, sl] = y + dsk[:, h:h + 1] * xh
            cl = clast[:, h:h + 1]
            Bw = (bf * jnp.exp(cl - cc)).astype(BF16)
            H[h] = jnp.exp(cl) * Hh + _dot(Bw, Xb, TN)
        zv = z_ref[...]
        y2 = y_ref[...] * (zv * _sigmoid(zv))
        r = lax.rsqrt(jnp.mean(y2 * y2, axis=-1, keepdims=True) + EPS)
        yn_ref[...] = (y2 * r * nw_ref[...]).astype(BF16)

    rowi = lambda b, g, i: b * nb + i
    grp = pl.BlockSpec((L, GROUP_W), lambda b, g, i: (rowi(b, g, i), g))
    return pl.pallas_call(
        body,
        out_shape=(jax.ShapeDtypeStruct((T, 1024), F32), jax.ShapeDtypeStruct((T, 1024), BF16),
                   jax.ShapeDtypeStruct((B, SSD_GROUPS, nb, 4, SSD_STATE, HEAD_DIM), F32)),
        grid=(B, SSD_GROUPS, nb),
        in_specs=[grp,
                  pl.BlockSpec((L, SSD_STATE), lambda b, g, i: (rowi(b, g, i), 8 + g)),
                  pl.BlockSpec((L, SSD_STATE), lambda b, g, i: (rowi(b, g, i), 12 + g)),
                  grp,
                  pl.BlockSpec((None, L, 4), lambda b, g, i: (g, rowi(b, g, i), 0)),
                  pl.BlockSpec((None, 4, L), lambda b, g, i: (g, 0, rowi(b, g, i))),
                  pl.BlockSpec((None, 3, 4), lambda b, g, i: (g, 0, 0)),
                  pl.BlockSpec((None, 4, 3), lambda b, g, i: (g, 0, 0)),
                  pl.BlockSpec((1, GROUP_W), lambda b, g, i: (0, g))],
        out_specs=(grp, grp,
                   pl.BlockSpec((None, None, None, 4, SSD_STATE, HEAD_DIM), lambda b, g, i: (b, g, i, 0, 0, 0))),
        scratch_shapes=[pltpu.VMEM((4, SSD_STATE, HEAD_DIM), F32), pltpu.VMEM((4, L, HEAD_DIM), F32)], name=name,
        compiler_params=_cp("parallel", "parallel", "arbitrary"))(xc, xc, xc, proj, dtc, dtr, pcol, prow, nw)


def _ssd_bwd(name, dyn, Y, xc, proj, dtc, dtr, pcol, prow, nw, hs, dproj, B):
    T = xc.shape[0]
    S = T // B
    nb = S // SSD_L
    L = SSD_L

    def body(dyn_ref, y_ref, xs_ref, b_ref, c_ref, z_ref, dtc_ref, dtr_ref, pcol_ref, prow_ref, nw_ref, hs_ref, buf_ref,
             dxs_ref, db_ref, dc_ref, dz_ref, ddt_ref, dpar_ref, dnw_ref, dH, dm_s, dxo_s, ea_s, ex_s):
        @pl.when(pl.program_id(2) == 0)
        def _():
            dH[...] = jnp.zeros_like(dH)
            dpar_ref[...] = jnp.zeros_like(dpar_ref)
            dnw_ref[...] = jnp.zeros_like(dnw_ref)

        a_c, dt_c, causal, tri, cum_c, cum_r, bb, cb, G = _ssd_common(dtc_ref, dtr_ref, pcol_ref, prow_ref, b_ref, c_ref)
        dsk = pcol_ref[2:3, :]
        clast = cum_c[L - 1:L, :]
        bf = b_ref[...]
        cf = c_ref[...]
        Yv = y_ref[...]
        zv = z_ref[...]
        sz = _sigmoid(zv)
        silu = zv * sz
        y2 = Yv * silu
        r = lax.rsqrt(jnp.mean(y2 * y2, axis=-1, keepdims=True) + EPS)
        yhat = y2 * r
        dyv = dyn_ref[...]
        dnw_ref[...] += jnp.sum(dyv * yhat, axis=0, keepdims=True)
        dyhat = dyv * nw_ref[...]
        dy2 = r * (dyhat - yhat * jnp.mean(dyhat * yhat, axis=-1, keepdims=True))
        dY = dy2 * silu
        dz_ref[...] = (dy2 * Yv * (sz * (1.0 + zv * (1.0 - sz)))).astype(BF16)

        lane4 = lax.broadcasted_iota(jnp.int32, (1, 4), 1)
        dG = jnp.zeros((L, L), F32)
        dBs = jnp.zeros((L, SSD_STATE), F32)
        dCs = jnp.zeros((L, SSD_STATE), F32)
        ddsk = jnp.zeros((1, 4), F32)
        dcl = jnp.zeros((1, 4), F32)
        for h in range(4):
            sl = slice(HEAD_DIM * h, HEAD_DIM * (h + 1))
            xb = (xs_ref[:, sl] * dt_c[:, h:h + 1]).astype(BF16)
            dm_s[h] = _dot(dY[:, sl].astype(BF16), xb, NT)
            dxo_s[h] = _dot(bb, dH[h].astype(BF16), NN)
        for h in range(4):
            sl = slice(HEAD_DIM * h, HEAD_DIM * (h + 1))
            onehot = (lane4 == h).astype(F32)
            cc = cum_c[:, h:h + 1]
            cl = clast[:, h:h + 1]
            lm = jnp.exp(jnp.where(causal, cc - cum_r[h:h + 1, :], NEG))
            M = (G * lm).astype(BF16)
            xh = xs_ref[:, sl]
            dth = dt_c[:, h:h + 1]
            X = xh * dth
            Xb = X.astype(BF16)
            dYh = dY[:, sl]
            dYb = dYh.astype(BF16)
            Hb = hs_ref[h].astype(BF16)
            dHh = dH[h]
            dHb = dHh.astype(BF16)
            alpha = jnp.exp(cc)
            beta = jnp.exp(cl - cc)
            dXoff = beta * dxo_s[h]
            dX = _dot(M, dYb, TN) + dXoff
            dG = dG + dm_s[h] * lm
            dCs = dCs + _dot((alpha * dYh).astype(BF16), Hb, NT)
            dBs = dBs + _dot((beta * X).astype(BF16), dHb, NT)
            ypre = Yv[:, sl] - dsk[:, h:h + 1] * xh
            ea_s[:, sl] = dYb.astype(F32) * ypre - Xb.astype(F32) * dX
            ex_s[:, sl] = dX * xh
            dcl_h = (_total(dHh * (jnp.exp(cl) * hs_ref[h])) + _total(Xb.astype(F32) * dXoff))
            dcl = dcl + dcl_h * onehot
            ddsk = ddsk + _total(dYh * xh) * onehot
            dxs_ref[:, sl] = dsk[:, h:h + 1] * dYh + dX * dth
            dH[h] = jnp.exp(cl) * dHh + _dot((alpha * cf).astype(BF16), dYb, TN)
        dGb = dG.astype(BF16)
        dc_ref[...] = _dot(dGb, bb, NN) + dCs
        db_ref[...] = _dot(dGb, cb, TN) + dBs
        feat = lax.broadcasted_iota(jnp.int32, (GROUP_W, 4), 0)
        head = lax.broadcasted_iota(jnp.int32, (GROUP_W, 4), 1) * HEAD_DIM
        sel = ((feat >= head) & (feat < head + HEAD_DIM)).astype(BF16)
        dA = _tri_sum(sel, ea_s[...], NN, False)
        ddtx = _tri_sum(sel, ex_s[...], NN, False)
        last = lax.broadcasted_iota(jnp.int32, (L, 1), 0) == L - 1
        dA = dA + jnp.where(last, dcl, 0.0)
        dadt = _tri_sum(tri, dA, TN, True)
        ddt = dadt * a_c + ddtx
        d_a = jnp.sum(dadt * dt_c, axis=0, keepdims=True)
        ddraw = ddt * _sigmoid(dtc_ref[...] + pcol_ref[0:1, :])
        ddt_ref[...] = ddraw
        dpar_ref[0:1, :] += jnp.sum(ddraw, axis=0, keepdims=True)
        dpar_ref[1:2, :] += d_a * a_c
        dpar_ref[2:3, :] += ddsk

    rowi = lambda b, g, i: b * nb + (nb - 1 - i)
    grp = pl.BlockSpec((L, GROUP_W), lambda b, g, i: (rowi(b, g, i), g))
    st = pl.BlockSpec((L, SSD_STATE), lambda b, g, i: (rowi(b, g, i), g))
    f = jax.ShapeDtypeStruct
    return pl.pallas_call(
        body,
        out_shape=(f((T, 1024), F32), f((T, 512), F32), f((T, 512), F32), f(dproj.shape, dproj.dtype),
                   f((SSD_GROUPS, T, 4), F32), f((B, SSD_GROUPS, 3, 4), F32), f((B, 1, 1024), F32)),
        grid=(B, SSD_GROUPS, nb),
        in_specs=[grp, grp, grp,
                  pl.BlockSpec((L, SSD_STATE), lambda b, g, i: (rowi(b, g, i), 8 + g)),
                  pl.BlockSpec((L, SSD_STATE), lambda b, g, i: (rowi(b, g, i), 12 + g)),
                  grp,
                  pl.BlockSpec((None, L, 4), lambda b, g, i: (g, rowi(b, g, i), 0)),
                  pl.BlockSpec((None, 4, L), lambda b, g, i: (g, 0, rowi(b, g, i))),
                  pl.BlockSpec((None, 3, 4), lambda b, g, i: (g, 0, 0)),
                  pl.BlockSpec((None, 4, 3), lambda b, g, i: (g, 0, 0)),
                  pl.BlockSpec((1, GROUP_W), lambda b, g, i: (0, g)),
                  pl.BlockSpec((None, None, None, 4, SSD_STATE, HEAD_DIM), lambda b, g, i: (b, g, nb - 1 - i, 0, 0, 0)),
                  ANY],
        out_specs=(grp, st, st, grp,
                   pl.BlockSpec((None, L, 4), lambda b, g, i: (g, rowi(b, g, i), 0)),
                   pl.BlockSpec((None, None, 3, 4), lambda b, g, i: (b, g, 0, 0)),
                   pl.BlockSpec((None, 1, GROUP_W), lambda b, g, i: (b, 0, g))),
        input_output_aliases={12: 3},
        scratch_shapes=[pltpu.VMEM((4, SSD_STATE, HEAD_DIM), F32), pltpu.VMEM((4, L, L), F32),
                        pltpu.VMEM((4, L, HEAD_DIM), F32), pltpu.VMEM((L, GROUP_W), F32),
                        pltpu.VMEM((L, GROUP_W), F32)], name=name,
        compiler_params=_cp("parallel", "parallel", "arbitrary"))(
            dyn, Y, xc, xc, xc, proj, dtc, dtr, pcol, prow, nw, hs, dproj)


def _head_sel():
    sel = (np.arange(1024)[:, None] // HEAD_DIM == np.arange(ATT_HEADS)[None, :]).astype(np.float32)
    return jnp.asarray(sel, BF16), jnp.asarray(sel.T, BF16)


def _head_rms(xv, sel, selT):
    ms = _tri_sum(sel, xv * xv, NN, False, 1) * (1.0 / HEAD_DIM)
    return _tri_sum(selT, lax.rsqrt(ms + EPS), NN, False, 2)


def _headnorm_fwd(name, proj, col_block, w):
    T = proj.shape[0]
    sel, selT = _head_sel()

    def body(x_ref, w_ref, sel_ref, selT_ref, o_ref):
        xv = x_ref[...]
        o_ref[...] = (xv * _head_rms(xv, sel_ref[...], selT_ref[...]) * w_ref[...]).astype(BF16)

    full = lambda shp: pl.BlockSpec(shp, lambda i: (0, 0))
    return pl.pallas_call(
        body, out_shape=jax.ShapeDtypeStruct((T, 1024), BF16), grid=(T // ROW_T,),
        in_specs=[pl.BlockSpec((ROW_T, 1024), lambda i: (i, col_block)), full((1, 1024)), full((1024, ATT_HEADS)),
                  full((ATT_HEADS, 1024))],
        out_specs=pl.BlockSpec((ROW_T, 1024), lambda i: (i, 0)), name=name, compiler_params=_cp("parallel"))(
            proj, jnp.tile(w, (1, ATT_HEADS)), sel, selT)


def _headnorm_bwd(name, dn, proj, col_block, w, dproj):
    T = proj.shape[0]
    sel, selT = _head_sel()

    def body(dn_ref, x_ref, w_ref, sel_ref, selT_ref, buf_ref, dx_ref, dw_ref):
        @pl.when(pl.program_id(0) == 0)
        def _():
            dw_ref[...] = jnp.zeros_like(dw_ref)

        xv = x_ref[...]
        sl, slT = sel_ref[...], selT_ref[...]
        rb = _head_rms(xv, sl, slT)
        xhat = xv * rb
        dnv = dn_ref[...]
        dxhat = dnv * w_ref[...]
        mean = _tri_sum(slT, _tri_sum(sl, dxhat * xhat, NN, False, 2) * (1.0 / HEAD_DIM), NN, False, 2)
        dx_ref[...] = (rb * (dxhat - xhat * mean)).astype(BF16)
        dw_ref[...] += jnp.sum(dnv * xhat, axis=0, keepdims=True)

    here = pl.BlockSpec((ROW_T, 1024), lambda i: (i, col_block))
    full = lambda shp: pl.BlockSpec(shp, lambda i: (0, 0))
    dx, dw = pl.pallas_call(
        body, out_shape=(jax.ShapeDtypeStruct(dproj.shape, dproj.dtype), jax.ShapeDtypeStruct((1, 1024), F32)),
        grid=(T // ROW_T,),
        in_specs=[pl.BlockSpec((ROW_T, 1024), lambda i: (i, 0)), here, full((1, 1024)), full((1024, ATT_HEADS)),
                  full((ATT_HEADS, 1024)), ANY],
        out_specs=(here, full((1, 1024))), input_output_aliases={5: 0},
        name=name, compiler_params=_cp("arbitrary"))(dn, proj, jnp.tile(w, (1, ATT_HEADS)), sel, selT, dproj)
    return dx, jnp.sum(dw.reshape(ATT_HEADS, HEAD_DIM), axis=0, keepdims=True)


def _att_bias(nq):
    j = np.arange(ATT_B)[:, None]
    i = np.arange(ATT_B)[None, :]
    out = np.empty((nq, ATT_B, ATT_B), np.float32)
    for dblk in range(nq):
        dl = ATT_B * dblk + i - j
        cnt = ((dl >= 0) & (dl <= 128)).astype(np.float32)
        cnt += ((dl >= 0) & (dl % 4 == 0) & (dl <= 512))
        cnt += ((dl >= 0) & (dl % 16 == 0) & (dl <= 2048))
        out[dblk] = np.where(cnt > 0, np.log(np.maximum(cnt, 1.0)), NEG)
    return jnp.asarray(out)


def _row_pair(nq):
    def f(r, c):
        first = c <= r
        return jnp.where(first, r, nq - 1 - r), jnp.where(first, c, c - (r + 1))
    return f


def _col_pair(nq):
    def f(r, c):
        first = c < nq - r
        kj = jnp.where(first, r, nq - 1 - r)
        return jnp.where(first, r + c, nq - 1 - r + (c - (nq - r))), kj
    return f


ATT_SCALE = 1.0 / math.sqrt(HEAD_DIM)
ATT_HS = 4
ATT_W = ATT_HS * HEAD_DIM


def _att_maps(nq, qk):
    return dict(
        q_tok=lambda b, g, r, c: (b * nq + qk(r, c)[0], g),
        k_tok=lambda b, g, r, c: (b * nq + qk(r, c)[1], g),
        v_tok=lambda b, g, r, c: (b * nq + qk(r, c)[1], COL_V // ATT_W + g),
        q_feat=lambda b, g, r, c: (g, b * nq + qk(r, c)[0]),
        k_feat=lambda b, g, r, c: (g, b * nq + qk(r, c)[1]),
        bias=lambda b, g, r, c: (qk(r, c)[0] - qk(r, c)[1], 0, 0),
        lse=lambda b, g, r, c: (g, 0, b * nq + qk(r, c)[0]),
        do_tok=lambda b, g, r, c: (b * nq + qk(r, c)[0], ATT_HS + g))


def _att_fwd(name, kn, qT, vT, bias, B):
    T = kn.shape[0]
    nq = (T // B) // ATT_B
    qk = _row_pair(nq)
    mp = _att_maps(nq, qk)

    def body(k_ref, qT_ref, vT_ref, bias_ref, oT_ref, lse_ref, m_s, l_s, acc_s, s_s):
        qi, kj = qk(pl.program_id(2), pl.program_id(3))

        @pl.when(kj == 0)
        def _():
            m_s[...] = jnp.full_like(m_s, NEG)
            l_s[...] = jnp.zeros_like(l_s)
            acc_s[...] = jnp.zeros_like(acc_s)

        bv = bias_ref[...]
        for h in range(ATT_HS):
            rs = slice(HEAD_DIM * h, HEAD_DIM * (h + 1))
            s_s[h] = _dot(k_ref[:, rs], qT_ref[rs, :], NN)
        for h in range(ATT_HS):
            rs = slice(HEAD_DIM * h, HEAD_DIM * (h + 1))
            s = s_s[h] + bv
            m_prev = m_s[h:h + 1, :]
            m_new = jnp.maximum(m_prev, jnp.max(s, axis=0, keepdims=True))
            alpha = jnp.exp(m_prev - m_new)
            p = jnp.exp(s - m_new)
            l_s[h:h + 1, :] = alpha * l_s[h:h + 1, :] + jnp.sum(p, axis=0, keepdims=True)
            acc_s[rs, :] = alpha * acc_s[rs, :] + _dot(vT_ref[rs, :], p.astype(BF16), NN)
            m_s[h:h + 1, :] = m_new

        @pl.when(kj == qi)
        def _():
            for h in range(ATT_HS):
                rs = slice(HEAD_DIM * h, HEAD_DIM * (h + 1))
                oT_ref[rs, :] = (acc_s[rs, :] / l_s[h:h + 1, :]).astype(BF16)
            lse_ref[...] = m_s[...] + jnp.log(l_s[...])

    tok = (ATT_B, ATT_W)
    feat = (ATT_W, ATT_B)
    return pl.pallas_call(
        body,
        out_shape=(jax.ShapeDtypeStruct((1024, T), BF16), jax.ShapeDtypeStruct((ATT_HEADS // ATT_HS, ATT_HS, T), F32)),
        grid=(B, ATT_HEADS // ATT_HS, nq // 2, nq + 1),
        in_specs=[pl.BlockSpec(tok, mp["k_tok"]), pl.BlockSpec(feat, mp["q_feat"]), pl.BlockSpec(feat, mp["k_feat"]),
                  pl.BlockSpec((None, ATT_B, ATT_B), mp["bias"])],
        out_specs=(pl.BlockSpec(feat, mp["q_feat"]), pl.BlockSpec((None, ATT_HS, ATT_B), mp["lse"])),
        scratch_shapes=[pltpu.VMEM((ATT_HS, ATT_B), F32), pltpu.VMEM((ATT_HS, ATT_B), F32),
                        pltpu.VMEM((ATT_W, ATT_B), F32), pltpu.VMEM((ATT_HS, ATT_B, ATT_B), F32)],
        name=name, compiler_params=_cp("parallel", "parallel", "arbitrary", "arbitrary"))(kn, qT, vT, bias)


def _att_scores(k_ref, qT_ref, v_ref, doT_ref, s_s, dp_s):
    for h in range(ATT_HS):
        rs = slice(HEAD_DIM * h, HEAD_DIM * (h + 1))
        s_s[h] = _dot(k_ref[:, rs], qT_ref[rs, :], NN)
        dp_s[h] = _dot(v_ref[:, rs].astype(BF16), doT_ref[rs, :].astype(BF16), NN)


def _att_p_ds(s_s, dp_s, doT_ref, oT_ref, lse_ref, bv, h):
    rs = slice(HEAD_DIM * h, HEAD_DIM * (h + 1))
    delta = jnp.sum(doT_ref[rs, :] * oT_ref[rs, :].astype(F32), axis=0, keepdims=True)
    p = jnp.exp(s_s[h] + bv - lse_ref[h:h + 1, :])
    return p, p * (dp_s[h] - delta)


def _att_bwd_dq(name, kn, qT, vb, knT, bias, doT, oT, lse, B):
    T = kn.shape[0]
    nq = (T // B) // ATT_B
    qk = _row_pair(nq)
    mp = _att_maps(nq, qk)

    def body(k_ref, qT_ref, v_ref, kT_ref, bias_ref, doT_ref, oT_ref, lse_ref, dqT_ref, acc_s, s_s, dp_s):
        qi, kj = qk(pl.program_id(2), pl.program_id(3))

        @pl.when(kj == 0)
        def _():
            acc_s[...] = jnp.zeros_like(acc_s)

        bv = bias_ref[...]
        _att_scores(k_ref, qT_ref, v_ref, doT_ref, s_s, dp_s)
        for h in range(ATT_HS):
            rs = slice(HEAD_DIM * h, HEAD_DIM * (h + 1))
            p, ds = _att_p_ds(s_s, dp_s, doT_ref, oT_ref, lse_ref, bv, h)
            acc_s[rs, :] += _dot(kT_ref[rs, :], ds.astype(BF16), NN)

        @pl.when(kj == qi)
        def _():
            dqT_ref[...] = acc_s[...] * ATT_SCALE

    tok = (ATT_B, ATT_W)
    feat = (ATT_W, ATT_B)
    return pl.pallas_call(
        body, out_shape=jax.ShapeDtypeStruct((1024, T), F32), grid=(B, ATT_HEADS // ATT_HS, nq // 2, nq + 1),
        in_specs=[pl.BlockSpec(tok, mp["k_tok"]), pl.BlockSpec(feat, mp["q_feat"]), pl.BlockSpec(tok, mp["v_tok"]),
                  pl.BlockSpec(feat, mp["k_feat"]), pl.BlockSpec((None, ATT_B, ATT_B), mp["bias"]),
                  pl.BlockSpec(feat, mp["q_feat"]), pl.BlockSpec(feat, mp["q_feat"]),
                  pl.BlockSpec((None, ATT_HS, ATT_B), mp["lse"])],
        out_specs=pl.BlockSpec(feat, mp["q_feat"]),
        scratch_shapes=[pltpu.VMEM((ATT_W, ATT_B), F32), pltpu.VMEM((ATT_HS, ATT_B, ATT_B), F32),
                        pltpu.VMEM((ATT_HS, ATT_B, ATT_B), F32)],
        name=name, compiler_params=_cp("parallel", "parallel", "arbitrary", "arbitrary"))(
            kn, qT, vb, knT, bias, doT, oT, lse)


def _att_bwd_dkv(name, kn, qT, vb, qn, bias, doT, oT, lse, dyn, dproj, B):
    T = kn.shape[0]
    nq = (T // B) // ATT_B
    qk = _col_pair(nq)
    mp = _att_maps(nq, qk)

    def body(k_ref, qT_ref, v_ref, q_ref, bias_ref, doT_ref, oT_ref, lse_ref, do_ref, buf_ref, dk_ref, dv_ref, dk_s, dv_s,
             s_s, dp_s):
        qi, kj = qk(pl.program_id(2), pl.program_id(3))

        @pl.when(qi == kj)
        def _():
            dk_s[...] = jnp.zeros_like(dk_s)
            dv_s[...] = jnp.zeros_like(dv_s)

        bv = bias_ref[...]
        _att_scores(k_ref, qT_ref, v_ref, doT_ref, s_s, dp_s)
        for h in range(ATT_HS):
            rs = slice(HEAD_DIM * h, HEAD_DIM * (h + 1))
            p, ds = _att_p_ds(s_s, dp_s, doT_ref, oT_ref, lse_ref, bv, h)
            dv_s[h] += _dot(p.astype(BF16), do_ref[:, rs].astype(BF16), NN)
            dk_s[h] += _dot(ds.astype(BF16), q_ref[:, rs], NN)

        @pl.when(qi == nq - 1)
        def _():
            for h in range(ATT_HS):
                rs = slice(HEAD_DIM * h, HEAD_DIM * (h + 1))
                dk_ref[:, rs] = dk_s[h] * ATT_SCALE
                dv_ref[:, rs] = dv_s[h].astype(BF16)

    tok = (ATT_B, ATT_W)
    feat = (ATT_W, ATT_B)
    v_cb = COL_V // ATT_W
    return pl.pallas_call(
        body, out_shape=(jax.ShapeDtypeStruct((T, 1024), F32), jax.ShapeDtypeStruct(dproj.shape, dproj.dtype)),
        grid=(B, ATT_HEADS // ATT_HS, nq // 2, nq + 1),
        in_specs=[pl.BlockSpec(tok, mp["k_tok"]), pl.BlockSpec(feat, mp["q_feat"]), pl.BlockSpec(tok, mp["v_tok"]),
                  pl.BlockSpec(tok, mp["q_tok"]), pl.BlockSpec((None, ATT_B, ATT_B), mp["bias"]),
                  pl.BlockSpec(feat, mp["q_feat"]), pl.BlockSpec(feat, mp["q_feat"]),
                  pl.BlockSpec((None, ATT_HS, ATT_B), mp["lse"]), pl.BlockSpec(tok, mp["do_tok"]), ANY],
        out_specs=(pl.BlockSpec(tok, mp["k_tok"]),
                   pl.BlockSpec(tok, lambda b, g, r, c: (b * nq + qk(r, c)[1], v_cb + g))),
        input_output_aliases={9: 1},
        scratch_shapes=[pltpu.VMEM((ATT_HS, ATT_B, HEAD_DIM), F32), pltpu.VMEM((ATT_HS, ATT_B, HEAD_DIM), F32),
                        pltpu.VMEM((ATT_HS, ATT_B, ATT_B), F32), pltpu.VMEM((ATT_HS, ATT_B, ATT_B), F32)],
        name=name, compiler_params=_cp("parallel", "parallel", "arbitrary", "arbitrary"))(
            kn, qT, vb, qn, bias, doT, oT, lse, dyn, dproj)


def _group_cols(v):
    return v.reshape(SSD_GROUPS, 4)


def _ssd_params(p):
    rows = jnp.stack([_group_cols(p["dt_bias"]), _group_cols(p["a_log"]), _group_cols(p["d_skip"])], axis=1)
    return rows, jnp.swapaxes(rows, 1, 2)


def _dymix(name, dx, wout):
    T = dx.shape[0]

    def body(dx_ref, w_ref, o_ref):
        dxb = dx_ref[...].astype(BF16)
        for n in range(N_SHARD):
            o_ref[:, MIX_SH * n:MIX_SH * (n + 1)] = _dot(dxb, w_ref[n], NT)

    return pl.pallas_call(
        body, out_shape=jax.ShapeDtypeStruct((T, MIX_W), F32), grid=(T // ROW_T,),
        in_specs=[pl.BlockSpec((ROW_T, D_MODEL), lambda i: (i, 0)),
                  pl.BlockSpec((N_SHARD, MIX_SH, D_MODEL), lambda i: (0, 0, 0))],
        out_specs=pl.BlockSpec((ROW_T, MIX_W), lambda i: (i, 0)), name=name, compiler_params=_cp("parallel"))(dx, wout)


def _mixer_fwd(tag, x1, p, weights, bias, B):
    T = x1.shape[0]
    S = T // B
    nt = T // ROW_T
    h2 = _rms_fwd(tag + "_mixrms", x1, p["mix_norm"][None])
    wi = weights("win", h2)
    win, cw = wi["win"], wi["cw"]
    proj = _mm(tag + "_proj",
               [(h2, pl.BlockSpec((ROW_T, D_MODEL), lambda j, i, k: (i, 0)),
                 win, pl.BlockSpec((D_MODEL, PROJ_TN), lambda j, i, k: (0, j)))],
               jax.ShapeDtypeStruct((T, IN_PAD), F32), pl.BlockSpec((ROW_T, PROJ_TN), lambda j, i, k: (i, j)),
               (IN_PAD // PROJ_TN, nt, 1), NN, (ROW_T, PROJ_TN))
    xc = _conv_fwd(tag + "_conv", proj, cw, p["conv_b"][None], B)
    dtraw = proj[:, COL_DT:COL_DT + SSD_HEADS].reshape(T, SSD_GROUPS, 4)
    dtc = jnp.transpose(dtraw, (1, 0, 2))
    dtr = jnp.transpose(dtraw, (1, 2, 0))
    pcol, prow = _ssd_params(p)
    Y, y_ssd, hs = _ssd_fwd(tag + "_ssd", xc, proj, dtc, dtr, pcol, prow, p["ssd_norm"][None], B)
    qn = _headnorm_fwd(tag + "_qn", proj, COL_Q // 1024, p["q_norm"][None])
    kn = _headnorm_fwd(tag + "_kn", proj, COL_K // 1024, p["k_norm"][None])
    qT = (qn * ATT_SCALE).T
    oT, lse = _att_fwd(tag + "_att", kn, qT, proj[:, COL_V:COL_V + 1024].T.astype(BF16), bias, B)
    ymix = jnp.concatenate([y_ssd, oT.T], axis=1)
    rest = weights("rest", ymix)
    x2 = _mm(tag + "_out",
             [(ymix, pl.BlockSpec((ROW_T, MIX_SH), lambda i, n, k, j=j: (i, j)),
               rest["wout"], pl.BlockSpec((None, MIX_SH, D_MODEL), lambda i, n, k, j=j: (j, 0, 0)))
              for j in range(N_SHARD)],
             jax.ShapeDtypeStruct((T, D_MODEL), F32), pl.BlockSpec((ROW_T, D_MODEL), lambda i, n, k: (i, 0)),
             (nt, 1, 1), NN, (ROW_T, D_MODEL),
             res=(x1, pl.BlockSpec((ROW_T, D_MODEL), lambda i, n, k: (i, 0))))
    saved = dict(x1=x1, h2=h2, proj=proj, xc=xc, dtc=dtc, dtr=dtr, Y=Y, hs=hs,
                 qn=qn, kn=kn, qT=qT, oT=oT, lse=lse, ymix=ymix, win=win, cw=cw, wout=rest["wout"])
    return x2, saved


def _mixer_bwd(tag, dx2, sv, p, bias, B):
    T = dx2.shape[0]
    S = T // B
    nt = T // ROW_T
    sg = {}
    dymix = _dymix(tag + "_dymix", dx2, sv["wout"])
    tkw = min(TK_W, T)
    gwout = _mm(tag + "_dwout",
                [(sv["ymix"], pl.BlockSpec((tkw, MIX_SH), lambda m, n, k: (k, m)),
                  dx2, pl.BlockSpec((tkw, D_MODEL), lambda m, n, k: (k, 0)))],
                jax.ShapeDtypeStruct((N_SHARD, MIX_SH, D_MODEL), BF16),
                pl.BlockSpec((None, MIX_SH, D_MODEL), lambda m, n, k: (m, 0, 0)),
                (N_SHARD, 1, T // tkw), TN, (MIX_SH, D_MODEL))
    proj = sv["proj"]
    doT = dymix[:, 1024:].T
    dqn = _att_bwd_dq(tag + "_attdq", sv["kn"], sv["qT"], proj, sv["kn"].T, bias, doT, sv["oT"], sv["lse"], B).T
    dproj = lax.empty((T, IN_PAD), BF16)
    dkn, dproj = _att_bwd_dkv(tag + "_attdkv", sv["kn"], sv["qT"], proj, sv["qn"], bias, doT, sv["oT"], sv["lse"],
                              dymix, dproj, B)
    dproj, sg["q_norm"] = _headnorm_bwd(tag + "_qnb", dqn, proj, COL_Q // 1024, p["q_norm"][None], dproj)
    dproj, sg["k_norm"] = _headnorm_bwd(tag + "_knb", dkn, proj, COL_K // 1024, p["k_norm"][None], dproj)
    pcol, prow = _ssd_params(p)
    dxs, dB, dC, dproj, ddt, dpar, dnw = _ssd_bwd(tag + "_ssdb", dymix, sv["Y"], sv["xc"], proj, sv["dtc"], sv["dtr"],
                                                  pcol, prow, p["ssd_norm"][None], sv["hs"], dproj, B)
    dpar = jnp.sum(dpar, axis=0)
    sg["dt_bias"] = dpar[:, 0, :].reshape(SSD_HEADS)
    sg["a_log"] = dpar[:, 1, :].reshape(SSD_HEADS)
    sg["d_skip"] = dpar[:, 2, :].reshape(SSD_HEADS)
    sg["ssd_norm"] = jnp.sum(dnw, axis=0)
    dproj, sg["conv_w"], sg["conv_b"] = _conv_bwd(tag + "_convb", proj, dxs, dB, dC, sv["cw"], p["conv_b"][None],
                                                  dproj, B)
    ddt16 = jnp.transpose(ddt, (1, 0, 2)).reshape(T, SSD_HEADS)
    dproj = lax.dynamic_update_slice(dproj, jnp.pad(ddt16, ((0, 0), (0, IN_PAD - COL_DT - SSD_HEADS))).astype(BF16),
                                     (0, COL_DT))
    win = sv["win"]
    gwin = _mm(tag + "_dwin",
               [(sv["h2"], pl.BlockSpec((tkw, D_MODEL), lambda n, m, k: (k, 0)),
                 dproj, pl.BlockSpec((tkw, PROJ_TN), lambda n, m, k: (k, n)))],
               jax.ShapeDtypeStruct((D_MODEL, IN_PAD), BF16), pl.BlockSpec((D_MODEL, PROJ_TN), lambda n, m, k: (0, n)),
               (IN_PAD // PROJ_TN, 1, T // tkw), TN, (D_MODEL, PROJ_TN))
    dh2 = _mm(tag + "_dh2",
              [(dproj, pl.BlockSpec((ROW_T, PROJ_TN), lambda i, n, k, j=j: (i, j)),
                win, pl.BlockSpec((D_MODEL, PROJ_TN), lambda i, n, k, j=j: (0, j))) for j in range(IN_PAD // PROJ_TN)],
              jax.ShapeDtypeStruct((T, D_MODEL), F32), pl.BlockSpec((ROW_T, D_MODEL), lambda i, n, k: (i, 0)),
              (nt, 1, 1), NT, (ROW_T, D_MODEL))
    dx1, sg["mix_norm"] = _rms_bwd(tag + "_mixrmsb", dh2, sv["x1"], p["mix_norm"][None], dx2)
    return dx1, sg, gwout, gwin


def _win_pack(w):
    return jnp.concatenate([w[:, :3072], w[:, 3088:], w[:, 3072:3088],
                            jnp.zeros((w.shape[0], IN_PAD - IN_PROJ), w.dtype)], axis=1)


def _win_unpack(g):
    return jnp.concatenate([g[:, :3072], g[:, COL_DT:COL_DT + SSD_HEADS], g[:, 3072:COL_DT]], axis=1)


DT_LO = IN_SH * 2 - COL_Q


def _win_from_shards(sh):
    main = IN_SH - DT_LO
    return jnp.concatenate([sh[0], sh[1][:, :main], sh[2][:, SSD_HEADS - DT_LO:], sh[3], sh[1][:, main:],
                            sh[2][:, :SSD_HEADS - DT_LO], jnp.zeros((sh.shape[1], IN_PAD - IN_PROJ), sh.dtype)], axis=1)


def _win_to_shards(g):
    main = IN_SH - DT_LO
    a, b = IN_SH + main, IN_SH + 2 * main
    return jnp.stack([g[:, :IN_SH],
                      jnp.concatenate([g[:, IN_SH:a], g[:, COL_DT:COL_DT + DT_LO]], axis=1),
                      jnp.concatenate([g[:, COL_DT + DT_LO:COL_DT + SSD_HEADS], g[:, a:b]], axis=1),
                      g[:, b:COL_DT]])


def _local_step(x, target, small, weights, scatter, B):
    T = x.shape[0]
    bias = _att_bias((T // B) // ATT_B)
    saved = []
    h = x
    for l in range(DEPTH):
        tag = "l%d" % l
        p = {k: v[l] for k, v in small.items()}
        w1 = weights(l, "ffn1", h)
        x1, ffn1, d1 = _ffn_fwd(tag + "f1", h, p["ffn1_norm"][None], w1["g1"], w1["u1"],
                                lambda after, l=l: weights(l, "ffn1d", after)["d1"])
        x2, sv = _mixer_fwd(tag, x1, p, functools.partial(weights, l), bias, B)
        w2 = weights(l, "rest", x2)
        h, ffn2, _ = _ffn_fwd(tag + "f2", x2, p["ffn2_norm"][None], w2["g2"], w2["u2"], w2["d2"])
        saved.append((ffn1, sv, ffn2, dict(g1=w1["g1"], u1=w1["u1"], d1=d1), w2))
    d, lsum = _loss_grad("loss", h, target)
    sgrads = [None] * DEPTH
    for l in reversed(range(DEPTH)):
        tag = "l%db" % l
        p = {k: v[l] for k, v in small.items()}
        ffn1, sv, ffn2, w1, w2 = saved[l]
        sg = {}
        d, sg["ffn2_norm"] = _ffn_bwd(tag + "f2", d, ffn2, p["ffn2_norm"][None], w2["g2"], w2["u2"], w2["d2"],
                                      lambda gg, gu, gd, c, l=l: scatter(l, "ffn2", dict(g2=gg, u2=gu, d2=gd), c))
        d, sgm, gwout, gwin = _mixer_bwd(tag, d, sv, p, bias, B)
        sg.update(sgm)
        d = scatter(l, "mixer", dict(wout=gwout, win=gwin), d)
        d, sg["ffn1_norm"] = _ffn_bwd(tag + "f1", d, ffn1, p["ffn1_norm"][None], w1["g1"], w1["u1"], w1["d1"],
                                      lambda gg, gu, gd, c, l=l: scatter(l, "ffn1", dict(g1=gg, u1=gu, d1=gd), c))
        sgrads[l] = sg
    return lsum, d, sgrads


MESH = pl.DeviceIdType.MESH
ANY = pl.BlockSpec(memory_space=pl.ANY)


def _place():
    return lax.axis_index("x"), lax.axis_index("y"), lax.axis_index("c")


def _other_chips(x, y):
    return [(1 - x, y), (x, 1 - y), (1 - x, 1 - y)]


HBM = pl.BlockSpec(memory_space=pltpu.HBM)
SEM = pl.BlockSpec(memory_space=pltpu.SEMAPHORE)
EFFECT = pltpu.SideEffectType.DATAFLOW_SIDE_EFFECTING


def _hbm(a):
    return pltpu.with_memory_space_constraint(a, pltpu.HBM)


def _exchange(gather, layer, src, land, send, recv, n, act):
    x, y, c = _place()
    for k, (px, py) in enumerate(_other_chips(x, y)):
        for a in range(n):
            if gather:
                s_out, d_out, d_in = src[a].at[layer], land[a].at[2 * x + y], land[a].at[2 * px + py]
            else:
                s_out, d_out, d_in = src[a].at[2 * px + py], land[a].at[k], land[a].at[k]
            act(pltpu.make_async_remote_copy(
                src_ref=s_out, dst_ref=d_out if act is _start else d_in, send_sem=send.at[k * n + a],
                recv_sem=recv.at[k * n + a], device_id=(px, py, c), device_id_type=MESH))


def _start(cp):
    cp.start()


def _finish(cp):
    cp.wait_send()
    cp.wait_recv()


def _exchange_start(name, gather, layer, srcs, carry):
    n = len(srcs)
    lands = [lax.empty(((N_SHARD,) + s.shape[1:]) if gather else ((3,) + s.shape[1:]), s.dtype) for s in srcs]

    def body(*refs):
        _exchange(gather, layer, refs[:n], refs[n:2 * n], refs[2 * n + 1], refs[2 * n + 2], n, _start)

    srcs = [_hbm(a) for a in srcs]
    thru = [_hbm(a) for a in lands + [carry]]
    out = pl.pallas_call(
        body, name=name,
        out_shape=(pltpu.SemaphoreType.DMA((3 * n,)), pltpu.SemaphoreType.DMA((3 * n,)),
                   *[pltpu.HBM(a.shape, a.dtype) for a in thru]),
        in_specs=[HBM] * (2 * n + 1), out_specs=(SEM, SEM, *[HBM] * (n + 1)),
        input_output_aliases={n + i: 2 + i for i in range(n + 1)},
        compiler_params=pltpu.CompilerParams(has_side_effects=EFFECT))(*srcs, *thru)
    return dict(gather=gather, layer=layer, send=out[0], recv=out[1], srcs=srcs, lands=list(out[2:2 + n])), out[-1]


def _exchange_wait(name, ex, after):
    n = len(ex["srcs"])

    def body(*refs):
        _exchange(ex["gather"], ex["layer"], refs[:n], refs[n:2 * n], refs[2 * n], refs[2 * n + 1], n, _finish)

    out = pl.pallas_call(
        body, name=name, out_shape=[pltpu.HBM(a.shape, a.dtype) for a in ex["lands"]],
        in_specs=[HBM] * (2 * n) + [SEM, SEM, ANY], out_specs=[HBM] * n,
        input_output_aliases={n + i: i for i in range(n)},
        compiler_params=pltpu.CompilerParams(has_side_effects=EFFECT))(
            *ex["srcs"], *ex["lands"], ex["send"], ex["recv"], after)
    return list(out)


def _swap_sibling(parts):
    n = len(parts)

    def body(*refs):
        src, dst = refs[:n], refs[n:2 * n]
        send, recv = refs[2 * n:]
        x, y, c = _place()
        cps = [pltpu.make_async_remote_copy(src_ref=src[a], dst_ref=dst[a], send_sem=send.at[a], recv_sem=recv.at[a],
                                            device_id=(x, y, 1 - c), device_id_type=MESH) for a in range(n)]
        for cp in cps:
            cp.start()
        for cp in cps:
            cp.wait_recv()
        for cp in cps:
            cp.wait_send()

    return pl.pallas_call(
        body, out_shape=[jax.ShapeDtypeStruct(p.shape, p.dtype) for p in parts],
        in_specs=[ANY] * n, out_specs=[ANY] * n,
        scratch_shapes=[pltpu.SemaphoreType.DMA((n,)), pltpu.SemaphoreType.DMA((n,))],
        name="swap_sibling")(*parts)


def _allreduce_small(name, v):
    R = v.shape[0]

    def body(v_ref, o_ref, buf, send, recv):
        x, y, c = _place()
        me = 4 * x + 2 * y + c
        buf[me] = v_ref[...]
        cps = []
        for k in range(1, 8):
            fx, fy, fc = (k >> 2) & 1, (k >> 1) & 1, k & 1
            px = 1 - x if fx else x
            py = 1 - y if fy else y
            pc = 1 - c if fc else c
            cp = pltpu.make_async_remote_copy(src_ref=v_ref, dst_ref=buf.at[me], send_sem=send.at[k - 1],
                                              recv_sem=recv.at[k - 1], device_id=(px, py, pc), device_id_type=MESH)
            cp.start()
            cps.append((cp, 4 * px + 2 * py + pc))
        for k, (cp, peer) in enumerate(cps):
            pltpu.make_async_remote_copy(src_ref=v_ref, dst_ref=buf.at[peer], send_sem=send.at[k], recv_sem=recv.at[k],
                                         device_id=(x, y, c), device_id_type=MESH).wait_recv()
        for cp, _ in cps:
            cp.wait_send()
        acc = buf[0]
        for d in range(1, 8):
            acc = acc + buf[d]
        o_ref[...] = acc

    return pl.pallas_call(
        body, out_shape=jax.ShapeDtypeStruct((R, 128), F32),
        in_specs=[pl.BlockSpec(memory_space=pltpu.VMEM)], out_specs=pl.BlockSpec(memory_space=pltpu.VMEM),
        scratch_shapes=[pltpu.VMEM((8, R, 128), F32), pltpu.SemaphoreType.DMA((7,)), pltpu.SemaphoreType.DMA((7,))],
        name=name)(v)


def _row_tile(r):
    for t in (256, 128, 64, 32, 16, 8):
        if r % t == 0:
            return t
    raise ValueError(r)


def _sum4(name, me, parts, got):
    _, R, C = parts.shape
    tr = _row_tile(R)

    def body(me_ref, o_ref, g_ref, s_ref):
        s = o_ref[...].astype(F32)
        for k in range(3):
            s = s + g_ref[k].astype(F32)
        s_ref[...] = s.astype(BF16)

    return pl.pallas_call(
        body, out_shape=jax.ShapeDtypeStruct((R, C), BF16),
        grid_spec=pltpu.PrefetchScalarGridSpec(
            num_scalar_prefetch=1, grid=(R // tr,),
            in_specs=[pl.BlockSpec((None, tr, C), lambda i, me_ref: (me_ref[0], i, 0)),
                      pl.BlockSpec((3, tr, C), lambda i, me_ref: (0, i, 0))],
            out_specs=pl.BlockSpec((tr, C), lambda i, me_ref: (i, 0))),
        name=name, compiler_params=_cp("parallel"))(me, parts, got)


def _adamw(name, w, gparts, m, v):
    R, C = w.shape
    tr = _row_tile(R)
    ng = len(gparts)
    c1 = 1.0 - ADAM_B1 ** ADAM_STEP
    c2 = 1.0 - ADAM_B2 ** ADAM_STEP

    def body(*refs):
        w_ref = refs[0]
        g_refs = refs[1:1 + ng]
        m_ref, v_ref, go_ref, d_ref, mo_ref, vo_ref = refs[1 + ng:]
        g = g_refs[0][...]
        for r in g_refs[1:]:
            g = g + r[...]
        mn = ADAM_B1 * m_ref[...] + (1.0 - ADAM_B1) * g
        vn = ADAM_B2 * v_ref[...] + (1.0 - ADAM_B2) * (g * g)
        go_ref[...] = g
        mo_ref[...] = mn
        vo_ref[...] = vn
        d_ref[...] = -ADAM_LR * ((mn / c1) / (jnp.sqrt(vn / c2) + ADAM_EPS) + ADAM_WD * w_ref[...])

    blk = pl.BlockSpec((tr, C), lambda i: (i, 0))
    osh = jax.ShapeDtypeStruct((R, C), F32)
    return pl.pallas_call(
        body, out_shape=(osh, osh, osh, osh), grid=(R // tr,), in_specs=[blk] * (3 + ng), out_specs=(blk,) * 4,
        name=name, compiler_params=_cp("parallel"))(w, *gparts, m, v)


def _adamw_layers(name, w, sums, m, v):
    _, R, C = w.shape
    tr = _row_tile(R)
    nr = R // tr
    c1 = 1.0 - ADAM_B1 ** ADAM_STEP
    c2 = 1.0 - ADAM_B2 ** ADAM_STEP

    def body(w_ref, a0, b0, a1, b1, m_ref, v_ref, go_ref, d_ref, mo_ref, vo_ref):
        f = lambda r: r[...].astype(F32)
        g = jnp.where(pl.program_id(0) == 0, f(a0) + f(b0), f(a1) + f(b1))
        mn = ADAM_B1 * m_ref[...] + (1.0 - ADAM_B1) * g
        vn = ADAM_B2 * v_ref[...] + (1.0 - ADAM_B2) * (g * g)
        go_ref[...] = g
        mo_ref[...] = mn
        vo_ref[...] = vn
        d_ref[...] = -ADAM_LR * ((mn / c1) / (jnp.sqrt(vn / c2) + ADAM_EPS) + ADAM_WD * w_ref[...])

    blk = pl.BlockSpec((None, tr, C), lambda l, i: (l, i, 0))
    lay0 = pl.BlockSpec((tr, C), lambda l, i: (jnp.where(l == 0, i, nr - 1), 0))
    lay1 = pl.BlockSpec((tr, C), lambda l, i: (jnp.where(l == 1, i, 0), 0))
    osh = jax.ShapeDtypeStruct((DEPTH, R, C), F32)
    return pl.pallas_call(
        body, out_shape=(osh, osh, osh, osh), grid=(DEPTH, nr),
        in_specs=[blk, lay0, lay0, lay1, lay1, blk, blk], out_specs=(blk,) * 4,
        name=name, compiler_params=_cp("arbitrary", "arbitrary"))(w, *sums[0], *sums[1], m, v)


BIG = [("ffn1_w_gate", "g1"), ("ffn1_w_up", "u1"), ("ffn1_w_down", "d1"), ("w_in", "win"), ("w_out", "wout"),
       ("ffn2_w_gate", "g2"), ("ffn2_w_up", "u2"), ("ffn2_w_down", "d2")]
SMALL = ["ffn1_norm", "mix_norm", "conv_b", "dt_bias", "a_log", "d_skip", "ssd_norm", "q_norm", "k_norm", "ffn2_norm"]
WEIGHTS = ["ffn1_norm", "ffn1_w_gate", "ffn1_w_up", "ffn1_w_down", "mix_norm", "w_in", "conv_w", "conv_b", "dt_bias",
           "a_log", "d_skip", "ssd_norm", "q_norm", "k_norm", "w_out", "ffn2_norm", "ffn2_w_gate", "ffn2_w_up",
           "ffn2_w_down"]
CONV_SH = CONV_DIM // N_SHARD
GATHER_GROUPS = [(0, "ffn1", ["g1", "u1"]), (0, "ffn1d", ["d1"]), (0, "win", ["win", "cw"]),
                 (0, "rest", ["wout", "g2", "u2", "d2"]),
                 (1, "all", ["g1", "u1", "d1", "win", "cw", "wout", "g2", "u2", "d2"])]


def _pad128(v):
    v = v.reshape(-1)
    return jnp.pad(v, (0, (-v.shape[0]) % 128))


def _pack(pieces):
    flat, offs, pos = [], [], 0
    for p in pieces:
        q = _pad128(p.astype(F32))
        offs.append(pos)
        pos += q.shape[0] // 128
        flat.append(q)
    total = -(-pos // 8) * 8
    out = jnp.concatenate(flat + [jnp.zeros(((total - pos) * 128,), F32)]).reshape(total, 128)
    return out, offs


def _unpack(packed, offs, shapes):
    out = []
    for off, shp in zip(offs, shapes):
        n = int(np.prod(shp))
        rows = -(-n // 128)
        out.append(packed[off:off + rows].reshape(-1)[:n].reshape(shp))
    return out


def kernel(x, ffn1_norm, ffn1_w_gate, ffn1_w_up, ffn1_w_down, mix_norm, w_in, conv_w, conv_b, dt_bias, a_log, d_skip, ssd_norm, q_norm, k_norm, w_out, ffn2_norm, ffn2_w_gate, ffn2_w_up, ffn2_w_down, loss_target, m_ffn1_norm, m_ffn1_w_gate, m_ffn1_w_up, m_ffn1_w_down, m_mix_norm, m_w_in, m_conv_w, m_conv_b, m_dt_bias, m_a_log, m_d_skip, m_ssd_norm, m_q_norm, m_k_norm, m_w_out, m_ffn2_norm, m_ffn2_w_gate, m_ffn2_w_up, m_ffn2_w_down, v_ffn1_norm, v_ffn1_w_gate, v_ffn1_w_up, v_ffn1_w_down, v_mix_norm, v_w_in, v_conv_w, v_conv_b, v_dt_bias, v_a_log, v_d_skip, v_ssd_norm, v_q_norm, v_k_norm, v_w_out, v_ffn2_norm, v_ffn2_w_gate, v_ffn2_w_up, v_ffn2_w_down):
    A = dict(locals())
    ix, iy, ic = _place()
    me = 2 * ix + iy
    B, S, _ = x.shape
    T = B * S

    own = {key: A[name].astype(BF16) for name, key in BIG}
    own["cw"] = conv_w
    exs, first_norm = [], ffn1_norm
    for gi, (l, _, keys) in enumerate(GATHER_GROUPS):
        ex, first_norm = _exchange_start("gather_start%d" % gi, True, l, [own[key] for key in keys], first_norm)
        exs.append(ex)
    landed = {}

    def weights(l, group, after):
        gi = [i for i, (gl, gname, _) in enumerate(GATHER_GROUPS) if gl == l and gname in (group, "all")][0]
        if gi not in landed:
            lands = _exchange_wait("gather_wait%d" % gi, exs[gi], after)
            landed[gi] = {}
            for key, land in zip(GATHER_GROUPS[gi][2], lands):
                full = lax.dynamic_update_slice(land, own[key][l][None], (me, 0, 0))
                if key == "win":
                    full = _win_from_shards(full)
                if key == "cw":
                    full = jnp.transpose(full, (1, 0, 2)).reshape(CONV_K, CONV_DIM)
                landed[gi][key] = full
        return landed[gi]

    pending = []

    def scatter(l, group, grads, carry):
        keys = sorted(grads)
        arrs = [grads[key] for key in keys]
        if "win" in grads:
            arrs[keys.index("win")] = _win_to_shards(grads["win"])
        ex, carry = _exchange_start("scatter_start_l%d_%s" % (l, group), False, None, arrs, carry)
        pending.append((l, keys, ex))
        return carry

    small = {name: A[name] for name in SMALL}
    small["ffn1_norm"] = first_norm
    lsum, dx, sgrads = _local_step(x.reshape(T, D_MODEL), loss_target.reshape(T, D_MODEL), small, weights, scatter, B)

    names = SMALL + ["conv_w"]
    shapes = [A[n].shape for n in SMALL] + [(DEPTH, CONV_K, CONV_DIM), ()]
    pieces = [jnp.stack([sgrads[l][n].reshape(shp[1:]) for l in range(DEPTH)]) for n, shp in zip(names, shapes)]
    pieces.append(0.5 / D_MODEL * jnp.sum(lsum))
    packed, offs = _pack(pieces)
    red = _allreduce_small("allreduce_small", packed)
    red = _unpack(red, offs, shapes)
    loss = red[-1]
    sg = dict(zip(names, red[:-1]))

    sums, after = {}, dx
    me1 = jnp.reshape(me, (1,)).astype(jnp.int32)
    for idx, (l, keys, ex) in enumerate(pending):
        lands = _exchange_wait("scatter_wait%d" % idx, ex, after)
        for key, g, got in zip(keys, ex["srcs"], lands):
            sums[key, l] = after = _sum4("sum_%s_l%d" % (key, l), me1, g, got)
    order = [(key, l) for _, key in BIG for l in range(DEPTH)]
    theirs = dict(zip(order, _swap_sibling([sums[k] for k in order])))

    out = {}
    for name, key in BIG:
        out[name] = _adamw_layers("adamw_" + key, A[name], [(sums[key, l], theirs[key, l]) for l in range(DEPTH)],
                                  A["m_" + name], A["v_" + name])

    wp, offs = _pack([A[n] for n in SMALL])
    gp, _ = _pack([sg[n] for n in SMALL])
    mp, _ = _pack([A["m_" + n] for n in SMALL])
    vp, _ = _pack([A["v_" + n] for n in SMALL])
    res = _adamw("adamw_small", wp, [gp], mp, vp)
    shapes = [A[n].shape for n in SMALL]
    res = [_unpack(r, offs, shapes) for r in res]
    for i, n in enumerate(SMALL):
        out[n] = [res[q][i] for q in range(4)]
    gcw = lax.dynamic_slice_in_dim(sg["conv_w"], me * CONV_SH, CONV_SH, axis=2)
    flat = lambda a: a.reshape(DEPTH * CONV_K, CONV_SH)
    res = _adamw("adamw_conv_w", flat(conv_w), [flat(gcw)], flat(m_conv_w), flat(v_conv_w))
    out["conv_w"] = [r.reshape(conv_w.shape) for r in res]

    outs = [loss, dx.reshape(B, S, D_MODEL)]
    for q in range(4):
        outs += [out[n][q] for n in WEIGHTS]
    return tuple(outs)
```

```python
import functools
import math

import numpy as np
import jax
import jax.numpy as jnp
from jax import lax
from jax.experimental import pallas as pl
from jax.experimental.pallas import tpu as pltpu

F32 = jnp.float32
BF16 = jnp.bfloat16

D_MODEL = 1024
DEPTH = 2
N_SHARD = 4
D_FF = 2816
FF_SH = D_FF // N_SHARD
SSD_HEADS = 16
HEAD_DIM = 64
SSD_GROUPS = 4
GROUP_W = 256
SSD_STATE = 128
CONV_K = 4
CONV_DIM = 2048
ATT_HEADS = 16
MIX_W = 2048
MIX_SH = MIX_W // N_SHARD
IN_PROJ = 6160
IN_SH = IN_PROJ // N_SHARD
IN_PAD = 6272
PROJ_TN = 896
COL_Z, COL_XBC, COL_Q, COL_K, COL_V, COL_DT = 0, 1024, 3072, 4096, 5120, 6144
EPS = 1e-6
NEG = -1e30
SSD_L = 256
ATT_B = 256
ROW_T = 512
HALF_T = ROW_T // 2
TK_W = 2048
CONV_CT = 256
CONV_R = 256
PAD_R = 8

ADAM_LR, ADAM_B1, ADAM_B2, ADAM_EPS, ADAM_WD, ADAM_STEP = 0.001, 0.9, 0.999, 1e-08, 0.01, 10

NN = (((1,), (0,)), ((), ()))
NT = (((1,), (1,)), ((), ()))
TN = (((0,), (0,)), ((), ()))

VMEM_LIMIT = 56 * 1024 * 1024


def _cp(*sem):
    return pltpu.CompilerParams(dimension_semantics=sem, vmem_limit_bytes=VMEM_LIMIT)


def _dot(a, b, dims):
    return lax.dot_general(a, b, dims, preferred_element_type=F32)


def _sigmoid(x):
    return 0.5 * jnp.tanh(0.5 * x) + 0.5


def _softplus(x):
    return jnp.maximum(x, 0.0) + jnp.log(1.0 + jnp.exp(-jnp.abs(x)))


def _mm(name, pairs, out_shape, out_spec, grid, dims, acc_shape, res=None, scale=1.0):
    nk = grid[2]
    npair = len(pairs)

    def body(*refs):
        ab = refs[:2 * npair]
        pos = 2 * npair
        res_ref = None
        if res is not None:
            res_ref = refs[pos]
            pos += 1
        out_ref = refs[pos]
        s = None
        for p in range(npair):
            d = _dot(ab[2 * p][...].astype(BF16), ab[2 * p + 1][...].astype(BF16), dims)
            s = d if s is None else s + d

        def finish(r):
            if scale != 1.0:
                r = r * scale
            if res_ref is not None:
                r = r + res_ref[...]
            out_ref[...] = r.astype(out_ref.dtype)

        if nk == 1:
            finish(s)
            return
        acc = refs[pos + 1]
        k = pl.program_id(2)

        @pl.when(k == 0)
        def _():
            acc[...] = s

        @pl.when(k > 0)
        def _():
            acc[...] += s

        @pl.when(k == nk - 1)
        def _():
            finish(acc[...])

    args, specs = [], []
    for a, a_spec, b, b_spec in pairs:
        args += [a, b]
        specs += [a_spec, b_spec]
    if res is not None:
        args.append(res[0])
        specs.append(res[1])
    return pl.pallas_call(
        body, out_shape=out_shape, grid=grid, in_specs=specs, out_specs=out_spec,
        scratch_shapes=[] if nk == 1 else [pltpu.VMEM(acc_shape, F32)], name=name,
        compiler_params=_cp("parallel", "parallel", "arbitrary"))(*args)


def _rms_fwd(name, x, w):
    T = x.shape[0]

    def body(x_ref, w_ref, o_ref):
        xv = x_ref[...]
        r = lax.rsqrt(jnp.mean(xv * xv, axis=-1, keepdims=True) + EPS)
        o_ref[...] = (xv * r * w_ref[...]).astype(BF16)

    return pl.pallas_call(
        body, out_shape=jax.ShapeDtypeStruct((T, D_MODEL), BF16), grid=(T // ROW_T,),
        in_specs=[pl.BlockSpec((ROW_T, D_MODEL), lambda i: (i, 0)), pl.BlockSpec((1, D_MODEL), lambda i: (0, 0))],
        out_specs=pl.BlockSpec((ROW_T, D_MODEL), lambda i: (i, 0)), name=name, compiler_params=_cp("parallel"))(x, w)


def _rms_bwd(name, dh, x, w, dres):
    T = x.shape[0]

    def body(dh_ref, x_ref, w_ref, dres_ref, dx_ref, dw_ref):
        @pl.when(pl.program_id(0) == 0)
        def _():
            dw_ref[...] = jnp.zeros_like(dw_ref)

        xv = x_ref[...]
        r = lax.rsqrt(jnp.mean(xv * xv, axis=-1, keepdims=True) + EPS)
        xhat = xv * r
        dhv = dh_ref[...]
        dxhat = dhv * w_ref[...]
        m = jnp.mean(dxhat * xhat, axis=-1, keepdims=True)
        dx_ref[...] = dres_ref[...] + r * (dxhat - xhat * m)
        dw_ref[...] += jnp.sum(dhv * xhat, axis=0, keepdims=True)

    row = pl.BlockSpec((ROW_T, D_MODEL), lambda i: (i, 0))
    vec = pl.BlockSpec((1, D_MODEL), lambda i: (0, 0))
    return pl.pallas_call(
        body, out_shape=(jax.ShapeDtypeStruct((T, D_MODEL), F32), jax.ShapeDtypeStruct((1, D_MODEL), F32)),
        grid=(T // ROW_T,), in_specs=[row, row, vec, row], out_specs=(row, vec), name=name,
        compiler_params=_cp("arbitrary"))(dh, x, w, dres)


def _loss_grad(name, y, t):
    T = y.shape[0]

    def body(y_ref, t_ref, dy_ref, l_ref):
        @pl.when(pl.program_id(0) == 0)
        def _():
            l_ref[...] = jnp.zeros_like(l_ref)

        e = y_ref[...] - t_ref[...]
        dy_ref[...] = e * (1.0 / D_MODEL)
        l_ref[...] += jnp.sum(e * e, axis=0, keepdims=True)

    row = pl.BlockSpec((ROW_T, D_MODEL), lambda i: (i, 0))
    vec = pl.BlockSpec((1, D_MODEL), lambda i: (0, 0))
    return pl.pallas_call(
        body, out_shape=(jax.ShapeDtypeStruct((T, D_MODEL), F32), jax.ShapeDtypeStruct((1, D_MODEL), F32)),
        grid=(T // ROW_T,), in_specs=[row, row], out_specs=(row, vec), name=name,
        compiler_params=_cp("arbitrary"))(y, t)


def _ffn_gate_up(name, h, wg, wu):
    T = h.shape[0]

    def body(h_ref, wg_ref, wu_ref, dgf_ref, duf_ref, a_ref):
        for r in range(0, ROW_T, HALF_T):
            rows = slice(r, r + HALF_T)
            hv = h_ref[rows, :]
            g = _dot(hv, wg_ref[...], NN)
            u = _dot(hv, wu_ref[...], NN)
            sg = _sigmoid(g)
            silu = g * sg
            dgf_ref[rows, :] = (u * (sg * (1.0 + g * (1.0 - sg)))).astype(BF16)
            duf_ref[rows, :] = silu.astype(BF16)
            a_ref[rows, :] = (silu * u).astype(BF16)

    wspec = pl.BlockSpec((None, D_MODEL, FF_SH), lambda j, i: (j, 0, 0))
    ospec = pl.BlockSpec((None, ROW_T, FF_SH), lambda j, i: (j, i, 0))
    osh = jax.ShapeDtypeStruct((N_SHARD, T, FF_SH), BF16)
    return pl.pallas_call(
        body, out_shape=(osh, osh, osh), grid=(N_SHARD, T // ROW_T),
        in_specs=[pl.BlockSpec((ROW_T, D_MODEL), lambda j, i: (i, 0)), wspec, wspec],
        out_specs=(ospec, ospec, ospec), name=name, compiler_params=_cp("parallel", "parallel"))(h, wg, wu)


def _ffn_dact(name, dx, wd, g, u):
    T = dx.shape[0]

    def body(dx_ref, wd_ref, g_ref, u_ref, dg_ref, du_ref):
        for r in range(0, ROW_T, HALF_T):
            rows = slice(r, r + HALF_T)
            da = 0.5 * _dot(dx_ref[rows, :].astype(BF16), wd_ref[...], NT)
            dg_ref[rows, :] = (da * g_ref[rows, :].astype(F32)).astype(BF16)
            du_ref[rows, :] = (da * u_ref[rows, :].astype(F32)).astype(BF16)

    aspec = pl.BlockSpec((None, ROW_T, FF_SH), lambda j, i: (j, i, 0))
    osh = jax.ShapeDtypeStruct((N_SHARD, T, FF_SH), BF16)
    return pl.pallas_call(
        body, out_shape=(osh, osh), grid=(N_SHARD, T // ROW_T),
        in_specs=[pl.BlockSpec((ROW_T, D_MODEL), lambda j, i: (i, 0)),
                  pl.BlockSpec((None, FF_SH, D_MODEL), lambda j, i: (j, 0, 0)), aspec, aspec],
        out_specs=(aspec, aspec), name=name, compiler_params=_cp("parallel", "parallel"))(dx, wd, g, u)


def _ffn_fwd(tag, x, nw, wg, wu, wd):
    T = x.shape[0]
    h = _rms_fwd(tag + "_rms", x, nw)
    g, u, a = _ffn_gate_up(tag + "_gu", h, wg, wu)
    if callable(wd):
        wd = wd(a)
    nt = T // ROW_T
    xo = _mm(tag + "_down",
             [(a, pl.BlockSpec((None, ROW_T, FF_SH), lambda i, n, k, j=j: (j, i, 0)),
               wd, pl.BlockSpec((None, FF_SH, D_MODEL), lambda i, n, k, j=j: (j, 0, 0))) for j in range(N_SHARD)],
             jax.ShapeDtypeStruct((T, D_MODEL), F32), pl.BlockSpec((ROW_T, D_MODEL), lambda i, n, k: (i, 0)),
             (nt, 1, 1), NN, (ROW_T, D_MODEL),
             res=(x, pl.BlockSpec((ROW_T, D_MODEL), lambda i, n, k: (i, 0))), scale=0.5)
    return xo, (x, h, g, u, a), wd


def _ffn_bwd(tag, dxo, saved, nw, wg, wu, wd, emit):
    x, h, g, u, a = saved
    T = x.shape[0]
    nt = T // ROW_T
    tkw = min(TK_W, T)
    nw_t = T // tkw
    dg, du = _ffn_dact(tag + "_dact", dxo, wd, g, u)
    actw = lambda f: pl.BlockSpec((None, tkw, FF_SH), f)
    gd = _mm(tag + "_dwd",
             [(a, actw(lambda m, n, k: (m, k, 0)), dxo, pl.BlockSpec((tkw, D_MODEL), lambda m, n, k: (k, 0)))],
             jax.ShapeDtypeStruct((N_SHARD, FF_SH, D_MODEL), BF16),
             pl.BlockSpec((None, FF_SH, D_MODEL), lambda m, n, k: (m, 0, 0)),
             (N_SHARD, 1, nw_t), TN, (FF_SH, D_MODEL), scale=0.5)
    hspec = pl.BlockSpec((tkw, D_MODEL), lambda j, n, k: (k, 0))
    gsh = jax.ShapeDtypeStruct((N_SHARD, D_MODEL, FF_SH), BF16)
    gspec = pl.BlockSpec((None, D_MODEL, FF_SH), lambda j, n, k: (j, 0, 0))
    gg = _mm(tag + "_dwg", [(h, hspec, dg, actw(lambda j, n, k: (j, k, 0)))], gsh, gspec,
             (N_SHARD, 1, nw_t), TN, (D_MODEL, FF_SH))
    gu = _mm(tag + "_dwu", [(h, hspec, du, actw(lambda j, n, k: (j, k, 0)))], gsh, gspec,
             (N_SHARD, 1, nw_t), TN, (D_MODEL, FF_SH))
    dg = emit(gg, gu, gd, dg)
    act = lambda j: pl.BlockSpec((None, ROW_T, FF_SH), lambda i, n, k: (j, i, 0))
    wsp = lambda j: pl.BlockSpec((None, D_MODEL, FF_SH), lambda i, n, k: (j, 0, 0))
    dh = _mm(tag + "_dh",
             [(dd, act(j), w, wsp(j)) for j in range(N_SHARD) for dd, w in ((dg, wg), (du, wu))],
             jax.ShapeDtypeStruct((T, D_MODEL), F32), pl.BlockSpec((ROW_T, D_MODEL), lambda i, n, k: (i, 0)),
             (nt, 1, 1), NT, (ROW_T, D_MODEL))
    return _rms_bwd(tag + "_rmsb", dh, x, nw, dxo)


def _seq_rows(ref, start, size, S):
    lo, hi = max(start, 0), min(start + size, S)
    parts = [ref[pl.ds(lo, hi - lo), :]]
    if lo > start:
        parts.insert(0, jnp.zeros((lo - start, ref.shape[1]), F32))
    if start + size > hi:
        parts.append(jnp.zeros((start + size - hi, ref.shape[1]), F32))
    return parts[0] if len(parts) == 1 else jnp.concatenate(parts, axis=0)


XBC_CB = COL_XBC // CONV_CT


def _conv_fwd(name, proj, w, b, B):
    T = proj.shape[0]
    S = T // B
    C = CONV_DIM

    def body(x_ref, w_ref, b_ref, o_ref):
        wv = w_ref[...]
        for c in range(S // CONV_R):
            r0 = c * CONV_R
            ch = _seq_rows(x_ref, r0 - PAD_R, CONV_R + PAD_R, S)
            pre = ch[PAD_R:] * wv[3:4] + b_ref[...]
            for s in range(1, CONV_K):
                pre = pre + pltpu.roll(ch, s, axis=0)[PAD_R:] * wv[3 - s:4 - s]
            o_ref[pl.ds(r0, CONV_R), :] = pre * _sigmoid(pre)

    return pl.pallas_call(
        body, out_shape=jax.ShapeDtypeStruct((T, C), F32), grid=(B, C // CONV_CT),
        in_specs=[pl.BlockSpec((S, CONV_CT), lambda bi, ci: (bi, XBC_CB + ci)),
                  pl.BlockSpec((CONV_K, CONV_CT), lambda bi, ci: (0, ci)),
                  pl.BlockSpec((1, CONV_CT), lambda bi, ci: (0, ci))],
        out_specs=pl.BlockSpec((S, CONV_CT), lambda bi, ci: (bi, ci)), name=name,
        compiler_params=_cp("parallel", "parallel"))(proj, w, b)


def _conv_bwd(name, proj, dxs, dB, dC, w, b, dproj, B):
    T = proj.shape[0]
    S = T // B
    C = CONV_DIM
    RW = CONV_R + PAD_R
    nx, nb = dxs.shape[1] // CONV_CT, dB.shape[1] // CONV_CT

    def body(x_ref, dx_in, db_in, dc_in, w_ref, b_ref, buf_ref, dx_ref, dw_ref, db_ref):
        @pl.when(pl.program_id(1) == 0)
        def _():
            dw_ref[...] = jnp.zeros_like(dw_ref)
            db_ref[...] = jnp.zeros_like(db_ref)

        ci = pl.program_id(0)
        wv = w_ref[...]
        dw = [jnp.zeros((1, CONV_CT), F32) for _ in range(CONV_K)]
        db = jnp.zeros((1, CONV_CT), F32)
        for c in range(S // CONV_R):
            r0 = c * CONV_R
            ch = _seq_rows(x_ref, r0 - PAD_R, RW + PAD_R, S)
            xs = [ch[PAD_R:]] + [pltpu.roll(ch, s, axis=0)[PAD_R:] for s in range(1, CONV_K)]
            pre = b_ref[...] + xs[0] * wv[3:4]
            for s in range(1, CONV_K):
                pre = pre + xs[s] * wv[3 - s:4 - s]
            sg = _sigmoid(pre)
            dout = jnp.where(ci < nx, _seq_rows(dx_in, r0, RW, S),
                             jnp.where(ci < nx + nb, _seq_rows(db_in, r0, RW, S), _seq_rows(dc_in, r0, RW, S)))
            dpre = dout * (sg * (1.0 + pre * (1.0 - sg)))
            dx = dpre[:CONV_R] * wv[3:4]
            for s in range(1, CONV_K):
                dx = dx + pltpu.roll(dpre, RW - s, axis=0)[:CONV_R] * wv[3 - s:4 - s]
            dx_ref[pl.ds(r0, CONV_R), :] = dx.astype(BF16)
            dcur = dpre[:CONV_R]
            db = db + jnp.sum(dcur, axis=0, keepdims=True)
            for s in range(CONV_K):
                dw[3 - s] = dw[3 - s] + jnp.sum(dcur * xs[s][:CONV_R], axis=0, keepdims=True)
        db_ref[...] += db
        for k in range(CONV_K):
            dw_ref[k:k + 1, :] += dw[k]

    seq = lambda f: pl.BlockSpec((S, CONV_CT), f)
    return pl.pallas_call(
        body,
        out_shape=(jax.ShapeDtypeStruct(dproj.shape, dproj.dtype), jax.ShapeDtypeStruct((CONV_K, C), F32),
                   jax.ShapeDtypeStruct((1, C), F32)),
        grid=(C // CONV_CT, B),
        in_specs=[seq(lambda ci, bi: (bi, XBC_CB + ci)),
                  seq(lambda ci, bi: (bi, jnp.minimum(ci, nx - 1))),
                  seq(lambda ci, bi: (bi, jnp.clip(ci - nx, 0, nb - 1))),
                  seq(lambda ci, bi: (bi, jnp.clip(ci - nx - nb, 0, nb - 1))),
                  pl.BlockSpec((CONV_K, CONV_CT), lambda ci, bi: (0, ci)),
                  pl.BlockSpec((1, CONV_CT), lambda ci, bi: (0, ci)), ANY],
        out_specs=(seq(lambda ci, bi: (bi, XBC_CB + ci)),
                   pl.BlockSpec((CONV_K, CONV_CT), lambda ci, bi: (0, ci)),
                   pl.BlockSpec((1, CONV_CT), lambda ci, bi: (0, ci))),
        input_output_aliases={6: 0},
        name=name, compiler_params=_cp("parallel", "arbitrary"))(proj, dxs, dB, dC, w, b, dproj)


def _tri_sum(tri, x, dims, tri_first, terms=3):
    out, rest = None, x
    for t in range(terms):
        part = rest.astype(BF16)
        if t + 1 < terms:
            rest = rest - part.astype(F32)
        d = _dot(tri, part, dims) if tri_first else _dot(part, tri, dims)
        out = d if out is None else out + d
    return out


def _total(x):
    return jnp.sum(jnp.sum(x, axis=0, keepdims=True), axis=-1, keepdims=True)


def _ssd_common(dtc_ref, dtr_ref, pcol_ref, prow_ref, b_ref, c_ref):
    L = SSD_L
    bias_c, alog_c = pcol_ref[0:1, :], pcol_ref[1:2, :]
    a_c = -jnp.exp(alog_c)
    dt_c = _softplus(dtc_ref[...] + bias_c)
    row = lax.broadcasted_iota(jnp.int32, (L, L), 0)
    col = lax.broadcasted_iota(jnp.int32, (L, L), 1)
    causal = row >= col
    tri = causal.astype(BF16)
    cum_c = _tri_sum(tri, dt_c * a_c, NN, True)
    a_r = -jnp.exp(prow_ref[:, 1:2])
    dt_r = _softplus(dtr_ref[...] + prow_ref[:, 0:1])
    cum_r = _tri_sum(tri, dt_r * a_r, NT, False)
    bb = b_ref[...].astype(BF16)
    cb = c_ref[...].astype(BF16)
    G = _dot(cb, bb, NT)
    return a_c, dt_c, causal, tri, cum_c, cum_r, bb, cb, G


def _ssd_fwd(name, xc, proj, dtc, dtr, pcol, prow, nw, B):
    T = xc.shape[0]
    S = T // B
    nb = S // SSD_L
    L = SSD_L

    def body(xs_ref, b_ref, c_ref, z_ref, dtc_ref, dtr_ref, pcol_ref, prow_ref, nw_ref, y_ref, yn_ref, hs_ref, H, yo_s):
        @pl.when(pl.program_id(2) == 0)
        def _():
            H[...] = jnp.zeros_like(H)

        a_c, dt_c, causal, tri, cum_c, cum_r, bb, cb, G = _ssd_common(dtc_ref, dtr_ref, pcol_ref, prow_ref, b_ref, c_ref)
        dsk = pcol_ref[2:3, :]
        clast = cum_c[L - 1:L, :]
        bf = b_ref[...]
        for h in range(4):
            hs_ref[h] = H[h]
            yo_s[h] = _dot(cb, H[h].astype(BF16), NN)
        for h in range(4):
            sl = slice(HEAD_DIM * h, HEAD_DIM * (h + 1))
            cc = cum_c[:, h:h + 1]
            lm = jnp.exp(jnp.where(causal, cc - cum_r[h:h + 1, :], NEG))
            M = (G * lm).astype(BF16)
            xh = xs_ref[:, sl]
            Xb = (xh * dt_c[:, h:h + 1]).astype(BF16)
            Hh = H[h]
            y = _dot(M, Xb, NN) + jnp.exp(cc) * yo_s[h]
            y_ref[:, sl] = y + dsk[:, h:h + 1] * xh
            cl = clast[:, h:h + 1]
            Bw = (bf * jnp.exp(cl - cc)).astype(BF16)
            H[h] = jnp.exp(cl) * Hh + _dot(Bw, Xb, TN)
        zv = z_ref[...]
        y2 = y_ref[...] * (zv * _sigmoid(zv))
        r = lax.rsqrt(jnp.mean(y2 * y2, axis=-1, keepdims=True) + EPS)
        yn_ref[...] = (y2 * r * nw_ref[...]).astype(BF16)

    rowi = lambda b, g, i: b * nb + i
    grp = pl.BlockSpec((L, GROUP_W), lambda b, g, i: (rowi(b, g, i), g))
    return pl.pallas_call(
        body,
        out_shape=(jax.ShapeDtypeStruct((T, 1024), F32), jax.ShapeDtypeStruct((T, 1024), BF16),
                   jax.ShapeDtypeStruct((B, SSD_GROUPS, nb, 4, SSD_STATE, HEAD_DIM), F32)),
        grid=(B, SSD_GROUPS, nb),
        in_specs=[grp,
                  pl.BlockSpec((L, SSD_STATE), lambda b, g, i: (rowi(b, g, i), 8 + g)),
                  pl.BlockSpec((L, SSD_STATE), lambda b, g, i: (rowi(b, g, i), 12 + g)),
                  grp,
                  pl.BlockSpec((None, L, 4), lambda b, g, i: (g, rowi(b, g, i), 0)),
                  pl.BlockSpec((None, 4, L), lambda b, g, i: (g, 0, rowi(b, g, i))),
                  pl.BlockSpec((None, 3, 4), lambda b, g, i: (g, 0, 0)),
                  pl.BlockSpec((None, 4, 3), lambda b, g, i: (g, 0, 0)),
                  pl.BlockSpec((1, GROUP_W), lambda b, g, i: (0, g))],
        out_specs=(grp, grp,
                   pl.BlockSpec((None, None, None, 4, SSD_STATE, HEAD_DIM), lambda b, g, i: (b, g, i, 0, 0, 0))),
        scratch_shapes=[pltpu.VMEM((4, SSD_STATE, HEAD_DIM), F32), pltpu.VMEM((4, L, HEAD_DIM), F32)], name=name,
        compiler_params=_cp("parallel", "parallel", "arbitrary"))(xc, xc, xc, proj, dtc, dtr, pcol, prow, nw)


def _ssd_bwd(name, dyn, Y, xc, proj, dtc, dtr, pcol, prow, nw, hs, dproj, B):
    T = xc.shape[0]
    S = T // B
    nb = S // SSD_L
    L = SSD_L

    def body(dyn_ref, y_ref, xs_ref, b_ref, c_ref, z_ref, dtc_ref, dtr_ref, pcol_ref, prow_ref, nw_ref, hs_ref, buf_ref,
             dxs_ref, db_ref, dc_ref, dz_ref, ddt_ref, dpar_ref, dnw_ref, dH, dm_s, dxo_s, ea_s, ex_s):
        @pl.when(pl.program_id(2) == 0)
        def _():
            dH[...] = jnp.zeros_like(dH)
            dpar_ref[...] = jnp.zeros_like(dpar_ref)
            dnw_ref[...] = jnp.zeros_like(dnw_ref)

        a_c, dt_c, causal, tri, cum_c, cum_r, bb, cb, G = _ssd_common(dtc_ref, dtr_ref, pcol_ref, prow_ref, b_ref, c_ref)
        dsk = pcol_ref[2:3, :]
        clast = cum_c[L - 1:L, :]
        bf = b_ref[...]
        cf = c_ref[...]
        Yv = y_ref[...]
        zv = z_ref[...]
        sz = _sigmoid(zv)
        silu = zv * sz
        y2 = Yv * silu
        r = lax.rsqrt(jnp.mean(y2 * y2, axis=-1, keepdims=True) + EPS)
        yhat = y2 * r
        dyv = dyn_ref[...]
        dnw_ref[...] += jnp.sum(dyv * yhat, axis=0, keepdims=True)
        dyhat = dyv * nw_ref[...]
        dy2 = r * (dyhat - yhat * jnp.mean(dyhat * yhat, axis=-1, keepdims=True))
        dY = dy2 * silu
        dz_ref[...] = (dy2 * Yv * (sz * (1.0 + zv * (1.0 - sz)))).astype(BF16)

        lane4 = lax.broadcasted_iota(jnp.int32, (1, 4), 1)
        dG = jnp.zeros((L, L), F32)
        dBs = jnp.zeros((L, SSD_STATE), F32)
        dCs = jnp.zeros((L, SSD_STATE), F32)
        ddsk = jnp.zeros((1, 4), F32)
        dcl = jnp.zeros((1, 4), F32)
        for h in range(4):
            sl = slice(HEAD_DIM * h, HEAD_DIM * (h + 1))
            xb = (xs_ref[:, sl] * dt_c[:, h:h + 1]).astype(BF16)
            dm_s[h] = _dot(dY[:, sl].astype(BF16), xb, NT)
            dxo_s[h] = _dot(bb, dH[h].astype(BF16), NN)
        for h in range(4):
            sl = slice(HEAD_DIM * h, HEAD_DIM * (h + 1))
            onehot = (lane4 == h).astype(F32)
            cc = cum_c[:, h:h + 1]
            cl = clast[:, h:h + 1]
            lm = jnp.exp(jnp.where(causal, cc - cum_r[h:h + 1, :], NEG))
            M = (G * lm).astype(BF16)
            xh = xs_ref[:, sl]
            dth = dt_c[:, h:h + 1]
            X = xh * dth
            Xb = X.astype(BF16)
            dYh = dY[:, sl]
            dYb = dYh.astype(BF16)
            Hb = hs_ref[h].astype(BF16)
            dHh = dH[h]
            dHb = dHh.astype(BF16)
            alpha = jnp.exp(cc)
            beta = jnp.exp(cl - cc)
            dXoff = beta * dxo_s[h]
            dX = _dot(M, dYb, TN) + dXoff
            dG = dG + dm_s[h] * lm
            dCs = dCs + _dot((alpha * dYh).astype(BF16), Hb, NT)
            dBs = dBs + _dot((beta * X).astype(BF16), dHb, NT)
            ypre = Yv[:, sl] - dsk[:, h:h + 1] * xh
            ea_s[:, sl] = dYb.astype(F32) * ypre - Xb.astype(F32) * dX
            ex_s[:, sl] = dX * xh
            dcl_h = (_total(dHh * (jnp.exp(cl) * hs_ref[h])) + _total(Xb.astype(F32) * dXoff))
            dcl = dcl + dcl_h * onehot
            ddsk = ddsk + _total(dYh * xh) * onehot
            dxs_ref[:, sl] = dsk[:, h:h + 1] * dYh + dX * dth
            dH[h] = jnp.exp(cl) * dHh + _dot((alpha * cf).astype(BF16), dYb, TN)
        dGb = dG.astype(BF16)
        dc_ref[...] = _dot(dGb, bb, NN) + dCs
        db_ref[...] = _dot(dGb, cb, TN) + dBs
        feat = lax.broadcasted_iota(jnp.int32, (GROUP_W, 4), 0)
        head = lax.broadcasted_iota(jnp.int32, (GROUP_W, 4), 1) * HEAD_DIM
        sel = ((feat >= head) & (feat < head + HEAD_DIM)).astype(BF16)
        dA = _tri_sum(sel, ea_s[...], NN, False)
        ddtx = _tri_sum(sel, ex_s[...], NN, False)
        last = lax.broadcasted_iota(jnp.int32, (L, 1), 0) == L - 1
        dA = dA + jnp.where(last, dcl, 0.0)
        dadt = _tri_sum(tri, dA, TN, True)
        ddt = dadt * a_c + ddtx
        d_a = jnp.sum(dadt * dt_c, axis=0, keepdims=True)
        ddraw = ddt * _sigmoid(dtc_ref[...] + pcol_ref[0:1, :])
        ddt_ref[...] = ddraw
        dpar_ref[0:1, :] += jnp.sum(ddraw, axis=0, keepdims=True)
        dpar_ref[1:2, :] += d_a * a_c
        dpar_ref[2:3, :] += ddsk

    rowi = lambda b, g, i: b * nb + (nb - 1 - i)
    grp = pl.BlockSpec((L, GROUP_W), lambda b, g, i: (rowi(b, g, i), g))
    st = pl.BlockSpec((L, SSD_STATE), lambda b, g, i: (rowi(b, g, i), g))
    f = jax.ShapeDtypeStruct
    return pl.pallas_call(
        body,
        out_shape=(f((T, 1024), F32), f((T, 512), F32), f((T, 512), F32), f(dproj.shape, dproj.dtype),
                   f((SSD_GROUPS, T, 4), F32), f((B, SSD_GROUPS, 3, 4), F32), f((B, 1, 1024), F32)),
        grid=(B, SSD_GROUPS, nb),
        in_specs=[grp, grp, grp,
                  pl.BlockSpec((L, SSD_STATE), lambda b, g, i: (rowi(b, g, i), 8 + g)),
                  pl.BlockSpec((L, SSD_STATE), lambda b, g, i: (rowi(b, g, i), 12 + g)),
                  grp,
                  pl.BlockSpec((None, L, 4), lambda b, g, i: (g, rowi(b, g, i), 0)),
                  pl.BlockSpec((None, 4, L), lambda b, g, i: (g, 0, rowi(b, g, i))),
                  pl.BlockSpec((None, 3, 4), lambda b, g, i: (g, 0, 0)),
                  pl.BlockSpec((None, 4, 3), lambda b, g, i: (g, 0, 0)),
                  pl.BlockSpec((1, GROUP_W), lambda b, g, i: (0, g)),
                  pl.BlockSpec((None, None, None, 4, SSD_STATE, HEAD_DIM), lambda b, g, i: (b, g, nb - 1 - i, 0, 0, 0)),
                  ANY],
        out_specs=(grp, st, st, grp,
                   pl.BlockSpec((None, L, 4), lambda b, g, i: (g, rowi(b, g, i), 0)),
                   pl.BlockSpec((None, None, 3, 4), lambda b, g, i: (b, g, 0, 0)),
                   pl.BlockSpec((None, 1, GROUP_W), lambda b, g, i: (b, 0, g))),
        input_output_aliases={12: 3},
        scratch_shapes=[pltpu.VMEM((4, SSD_STATE, HEAD_DIM), F32), pltpu.VMEM((4, L, L), F32),
                        pltpu.VMEM((4, L, HEAD_DIM), F32), pltpu.VMEM((L, GROUP_W), F32),
                        pltpu.VMEM((L, GROUP_W), F32)], name=name,
        compiler_params=_cp("parallel", "parallel", "arbitrary"))(
            dyn, Y, xc, xc, xc, proj, dtc, dtr, pcol, prow, nw, hs, dproj)


def _head_sel():
    sel = (np.arange(1024)[:, None] // HEAD_DIM == np.arange(ATT_HEADS)[None, :]).astype(np.float32)
    return jnp.asarray(sel, BF16), jnp.asarray(sel.T, BF16)


def _head_rms(xv, sel, selT):
    ms = _tri_sum(sel, xv * xv, NN, False, 1) * (1.0 / HEAD_DIM)
    return _tri_sum(selT, lax.rsqrt(ms + EPS), NN, False, 2)


def _headnorm_fwd(name, proj, col_block, w):
    T = proj.shape[0]
    sel, selT = _head_sel()

    def body(x_ref, w_ref, sel_ref, selT_ref, o_ref):
        xv = x_ref[...]
        o_ref[...] = (xv * _head_rms(xv, sel_ref[...], selT_ref[...]) * w_ref[...]).astype(BF16)

    full = lambda shp: pl.BlockSpec(shp, lambda i: (0, 0))
    return pl.pallas_call(
        body, out_shape=jax.ShapeDtypeStruct((T, 1024), BF16), grid=(T // ROW_T,),
        in_specs=[pl.BlockSpec((ROW_T, 1024), lambda i: (i, col_block)), full((1, 1024)), full((1024, ATT_HEADS)),
                  full((ATT_HEADS, 1024))],
        out_specs=pl.BlockSpec((ROW_T, 1024), lambda i: (i, 0)), name=name, compiler_params=_cp("parallel"))(
            proj, jnp.tile(w, (1, ATT_HEADS)), sel, selT)


def _headnorm_bwd(name, dn, proj, col_block, w, dproj):
    T = proj.shape[0]
    sel, selT = _head_sel()

    def body(dn_ref, x_ref, w_ref, sel_ref, selT_ref, buf_ref, dx_ref, dw_ref):
        @pl.when(pl.program_id(0) == 0)
        def _():
            dw_ref[...] = jnp.zeros_like(dw_ref)

        xv = x_ref[...]
        sl, slT = sel_ref[...], selT_ref[...]
        rb = _head_rms(xv, sl, slT)
        xhat = xv * rb
        dnv = dn_ref[...]
        dxhat = dnv * w_ref[...]
        mean = _tri_sum(slT, _tri_sum(sl, dxhat * xhat, NN, False, 2) * (1.0 / HEAD_DIM), NN, False, 2)
        dx_ref[...] = (rb * (dxhat - xhat * mean)).astype(BF16)
        dw_ref[...] += jnp.sum(dnv * xhat, axis=0, keepdims=True)

    here = pl.BlockSpec((ROW_T, 1024), lambda i: (i, col_block))
    full = lambda shp: pl.BlockSpec(shp, lambda i: (0, 0))
    dx, dw = pl.pallas_call(
        body, out_shape=(jax.ShapeDtypeStruct(dproj.shape, dproj.dtype), jax.ShapeDtypeStruct((1, 1024), F32)),
        grid=(T // ROW_T,),
        in_specs=[pl.BlockSpec((ROW_T, 1024), lambda i: (i, 0)), here, full((1, 1024)), full((1024, ATT_HEADS)),
                  full((ATT_HEADS, 1024)), ANY],
        out_specs=(here, full((1, 1024))), input_output_aliases={5: 0},
        name=name, compiler_params=_cp("arbitrary"))(dn, proj, jnp.tile(w, (1, ATT_HEADS)), sel, selT, dproj)
    return dx, jnp.sum(dw.reshape(ATT_HEADS, HEAD_DIM), axis=0, keepdims=True)


def _att_bias(nq):
    j = np.arange(ATT_B)[:, None]
    i = np.arange(ATT_B)[None, :]
    out = np.empty((nq, ATT_B, ATT_B), np.float32)
    for dblk in range(nq):
        dl = ATT_B * dblk + i - j
        cnt = ((dl >= 0) & (dl <= 128)).astype(np.float32)
        cnt += ((dl >= 0) & (dl % 4 == 0) & (dl <= 512))
        cnt += ((dl >= 0) & (dl % 16 == 0) & (dl <= 2048))
        out[dblk] = np.where(cnt > 0, np.log(np.maximum(cnt, 1.0)), NEG)
    return jnp.asarray(out)


def _row_pair(nq):
    def f(r, c):
        first = c <= r
        return jnp.where(first, r, nq - 1 - r), jnp.where(first, c, c - (r + 1))
    return f


def _col_pair(nq):
    def f(r, c):
        first = c < nq - r
        kj = jnp.where(first, r, nq - 1 - r)
        return jnp.where(first, r + c, nq - 1 - r + (c - (nq - r))), kj
    return f


ATT_SCALE = 1.0 / math.sqrt(HEAD_DIM)
ATT_HS = 4
ATT_W = ATT_HS * HEAD_DIM


def _att_maps(nq, qk):
    return dict(
        q_tok=lambda b, g, r, c: (b * nq + qk(r, c)[0], g),
        k_tok=lambda b, g, r, c: (b * nq + qk(r, c)[1], g),
        v_tok=lambda b, g, r, c: (b * nq + qk(r, c)[1], COL_V // ATT_W + g),
        q_feat=lambda b, g, r, c: (g, b * nq + qk(r, c)[0]),
        k_feat=lambda b, g, r, c: (g, b * nq + qk(r, c)[1]),
        bias=lambda b, g, r, c: (qk(r, c)[0] - qk(r, c)[1], 0, 0),
        lse=lambda b, g, r, c: (g, 0, b * nq + qk(r, c)[0]),
        do_tok=lambda b, g, r, c: (b * nq + qk(r, c)[0], ATT_HS + g))


def _att_fwd(name, kn, qT, vT, bias, B):
    T = kn.shape[0]
    nq = (T // B) // ATT_B
    qk = _row_pair(nq)
    mp = _att_maps(nq, qk)

    def body(k_ref, qT_ref, vT_ref, bias_ref, oT_ref, lse_ref, m_s, l_s, acc_s, s_s):
        qi, kj = qk(pl.program_id(2), pl.program_id(3))

        @pl.when(kj == 0)
        def _():
            m_s[...] = jnp.full_like(m_s, NEG)
            l_s[...] = jnp.zeros_like(l_s)
            acc_s[...] = jnp.zeros_like(acc_s)

        bv = bias_ref[...]
        for h in range(ATT_HS):
            rs = slice(HEAD_DIM * h, HEAD_DIM * (h + 1))
            s_s[h] = _dot(k_ref[:, rs], qT_ref[rs, :], NN)
        for h in range(ATT_HS):
            rs = slice(HEAD_DIM * h, HEAD_DIM * (h + 1))
            s = s_s[h] + bv
            m_prev = m_s[h:h + 1, :]
            m_new = jnp.maximum(m_prev, jnp.max(s, axis=0, keepdims=True))
            alpha = jnp.exp(m_prev - m_new)
            p = jnp.exp(s - m_new)
            l_s[h:h + 1, :] = alpha * l_s[h:h + 1, :] + jnp.sum(p, axis=0, keepdims=True)
            acc_s[rs, :] = alpha * acc_s[rs, :] + _dot(vT_ref[rs, :], p.astype(BF16), NN)
            m_s[h:h + 1, :] = m_new

        @pl.when(kj == qi)
        def _():
            for h in range(ATT_HS):
                rs = slice(HEAD_DIM * h, HEAD_DIM * (h + 1))
                oT_ref[rs, :] = (acc_s[rs, :] / l_s[h:h + 1, :]).astype(BF16)
            lse_ref[...] = m_s[...] + jnp.log(l_s[...])

    tok = (ATT_B, ATT_W)
    feat = (ATT_W, ATT_B)
    return pl.pallas_call(
        body,
        out_shape=(jax.ShapeDtypeStruct((1024, T), BF16), jax.ShapeDtypeStruct((ATT_HEADS // ATT_HS, ATT_HS, T), F32)),
        grid=(B, ATT_HEADS // ATT_HS, nq // 2, nq + 1),
        in_specs=[pl.BlockSpec(tok, mp["k_tok"]), pl.BlockSpec(feat, mp["q_feat"]), pl.BlockSpec(feat, mp["k_feat"]),
                  pl.BlockSpec((None, ATT_B, ATT_B), mp["bias"])],
        out_specs=(pl.BlockSpec(feat, mp["q_feat"]), pl.BlockSpec((None, ATT_HS, ATT_B), mp["lse"])),
        scratch_shapes=[pltpu.VMEM((ATT_HS, ATT_B), F32), pltpu.VMEM((ATT_HS, ATT_B), F32),
                        pltpu.VMEM((ATT_W, ATT_B), F32), pltpu.VMEM((ATT_HS, ATT_B, ATT_B), F32)],
        name=name, compiler_params=_cp("parallel", "parallel", "arbitrary", "arbitrary"))(kn, qT, vT, bias)


def _att_scores(k_ref, qT_ref, v_ref, doT_ref, s_s, dp_s):
    for h in range(ATT_HS):
        rs = slice(HEAD_DIM * h, HEAD_DIM * (h + 1))
        s_s[h] = _dot(k_ref[:, rs], qT_ref[rs, :], NN)
        dp_s[h] = _dot(v_ref[:, rs].astype(BF16), doT_ref[rs, :].astype(BF16), NN)


def _att_p_ds(s_s, dp_s, doT_ref, oT_ref, lse_ref, bv, h):
    rs = slice(HEAD_DIM * h, HEAD_DIM * (h + 1))
    delta = jnp.sum(doT_ref[rs, :] * oT_ref[rs, :].astype(F32), axis=0, keepdims=True)
    p = jnp.exp(s_s[h] + bv - lse_ref[h:h + 1, :])
    return p, p * (dp_s[h] - delta)


def _att_bwd_dq(name, kn, qT, vb, knT, bias, doT, oT, lse, B):
    T = kn.shape[0]
    nq = (T // B) // ATT_B
    qk = _row_pair(nq)
    mp = _att_maps(nq, qk)

    def body(k_ref, qT_ref, v_ref, kT_ref, bias_ref, doT_ref, oT_ref, lse_ref, dqT_ref, acc_s, s_s, dp_s):
        qi, kj = qk(pl.program_id(2), pl.program_id(3))

        @pl.when(kj == 0)
        def _():
            acc_s[...] = jnp.zeros_like(acc_s)

        bv = bias_ref[...]
        _att_scores(k_ref, qT_ref, v_ref, doT_ref, s_s, dp_s)
        for h in range(ATT_HS):
            rs = slice(HEAD_DIM * h, HEAD_DIM * (h + 1))
            p, ds = _att_p_ds(s_s, dp_s, doT_ref, oT_ref, lse_ref, bv, h)
            acc_s[rs, :] += _dot(kT_ref[rs, :], ds.astype(BF16), NN)

        @pl.when(kj == qi)
        def _():
            dqT_ref[...] = acc_s[...] * ATT_SCALE

    tok = (ATT_B, ATT_W)
    feat = (ATT_W, ATT_B)
    return pl.pallas_call(
        body, out_shape=jax.ShapeDtypeStruct((1024, T), F32), grid=(B, ATT_HEADS // ATT_HS, nq // 2, nq + 1),
        in_specs=[pl.BlockSpec(tok, mp["k_tok"]), pl.BlockSpec(feat, mp["q_feat"]), pl.BlockSpec(tok, mp["v_tok"]),
                  pl.BlockSpec(feat, mp["k_feat"]), pl.BlockSpec((None, ATT_B, ATT_B), mp["bias"]),
                  pl.BlockSpec(feat, mp["q_feat"]), pl.BlockSpec(feat, mp["q_feat"]),
                  pl.BlockSpec((None, ATT_HS, ATT_B), mp["lse"])],
        out_specs=pl.BlockSpec(feat, mp["q_feat"]),
        scratch_shapes=[pltpu.VMEM((ATT_W, ATT_B), F32), pltpu.VMEM((ATT_HS, ATT_B, ATT_B), F32),
                        pltpu.VMEM((ATT_HS, ATT_B, ATT_B), F32)],
        name=name, compiler_params=_cp("parallel", "parallel", "arbitrary", "arbitrary"))(
            kn, qT, vb, knT, bias, doT, oT, lse)


def _att_bwd_dkv(name, kn, qT, vb, qn, bias, doT, oT, lse, dyn, dproj, B):
    T = kn.shape[0]
    nq = (T // B) // ATT_B
    qk = _col_pair(nq)
    mp = _att_maps(nq, qk)

    def body(k_ref, qT_ref, v_ref, q_ref, bias_ref, doT_ref, oT_ref, lse_ref, do_ref, buf_ref, dk_ref, dv_ref, dk_s, dv_s,
             s_s, dp_s):
        qi, kj = qk(pl.program_id(2), pl.program_id(3))

        @pl.when(qi == kj)
        def _():
            dk_s[...] = jnp.zeros_like(dk_s)
            dv_s[...] = jnp.zeros_like(dv_s)

        bv = bias_ref[...]
        _att_scores(k_ref, qT_ref, v_ref, doT_ref, s_s, dp_s)
        for h in range(ATT_HS):
            rs = slice(HEAD_DIM * h, HEAD_DIM * (h + 1))
            p, ds = _att_p_ds(s_s, dp_s, doT_ref, oT_ref, lse_ref, bv, h)
            dv_s[h] += _dot(p.astype(BF16), do_ref[:, rs].astype(BF16), NN)
            dk_s[h] += _dot(ds.astype(BF16), q_ref[:, rs], NN)

        @pl.when(qi == nq - 1)
        def _():
            for h in range(ATT_HS):
                rs = slice(HEAD_DIM * h, HEAD_DIM * (h + 1))
                dk_ref[:, rs] = dk_s[h] * ATT_SCALE
                dv_ref[:, rs] = dv_s[h].astype(BF16)

    tok = (ATT_B, ATT_W)
    feat = (ATT_W, ATT_B)
    v_cb = COL_V // ATT_W
    return pl.pallas_call(
        body, out_shape=(jax.ShapeDtypeStruct((T, 1024), F32), jax.ShapeDtypeStruct(dproj.shape, dproj.dtype)),
        grid=(B, ATT_HEADS // ATT_HS, nq // 2, nq + 1),
        in_specs=[pl.BlockSpec(tok, mp["k_tok"]), pl.BlockSpec(feat, mp["q_feat"]), pl.BlockSpec(tok, mp["v_tok"]),
                  pl.BlockSpec(tok, mp["q_tok"]), pl.BlockSpec((None, ATT_B, ATT_B), mp["bias"]),
                  pl.BlockSpec(feat, mp["q_feat"]), pl.BlockSpec(feat, mp["q_feat"]),
                  pl.BlockSpec((None, ATT_HS, ATT_B), mp["lse"]), pl.BlockSpec(tok, mp["do_tok"]), ANY],
        out_specs=(pl.BlockSpec(tok, mp["k_tok"]),
                   pl.BlockSpec(tok, lambda b, g, r, c: (b * nq + qk(r, c)[1], v_cb + g))),
        input_output_aliases={9: 1},
        scratch_shapes=[pltpu.VMEM((ATT_HS, ATT_B, HEAD_DIM), F32), pltpu.VMEM((ATT_HS, ATT_B, HEAD_DIM), F32),
                        pltpu.VMEM((ATT_HS, ATT_B, ATT_B), F32), pltpu.VMEM((ATT_HS, ATT_B, ATT_B), F32)],
        name=name, compiler_params=_cp("parallel", "parallel", "arbitrary", "arbitrary"))(
            kn, qT, vb, qn, bias, doT, oT, lse, dyn, dproj)


def _group_cols(v):
    return v.reshape(SSD_GROUPS, 4)


def _ssd_params(p):
    rows = jnp.stack([_group_cols(p["dt_bias"]), _group_cols(p["a_log"]), _group_cols(p["d_skip"])], axis=1)
    return rows, jnp.swapaxes(rows, 1, 2)


def _dymix(name, dx, wout):
    T = dx.shape[0]

    def body(dx_ref, w_ref, o_ref):
        dxb = dx_ref[...].astype(BF16)
        for n in range(N_SHARD):
            o_ref[:, MIX_SH * n:MIX_SH * (n + 1)] = _dot(dxb, w_ref[n], NT)

    return pl.pallas_call(
        body, out_shape=jax.ShapeDtypeStruct((T, MIX_W), F32), grid=(T // ROW_T,),
        in_specs=[pl.BlockSpec((ROW_T, D_MODEL), lambda i: (i, 0)),
                  pl.BlockSpec((N_SHARD, MIX_SH, D_MODEL), lambda i: (0, 0, 0))],
        out_specs=pl.BlockSpec((ROW_T, MIX_W), lambda i: (i, 0)), name=name, compiler_params=_cp("parallel"))(dx, wout)


def _mixer_fwd(tag, x1, p, weights, bias, B):
    T = x1.shape[0]
    S = T // B
    nt = T // ROW_T
    h2 = _rms_fwd(tag + "_mixrms", x1, p["mix_norm"][None])
    wi = weights("win", h2)
    win, cw = wi["win"], wi["cw"]
    proj = _mm(tag + "_proj",
               [(h2, pl.BlockSpec((ROW_T, D_MODEL), lambda j, i, k: (i, 0)),
                 win, pl.BlockSpec((D_MODEL, PROJ_TN), lambda j, i, k: (0, j)))],
               jax.ShapeDtypeStruct((T, IN_PAD), F32), pl.BlockSpec((ROW_T, PROJ_TN), lambda j, i, k: (i, j)),
               (IN_PAD // PROJ_TN, nt, 1), NN, (ROW_T, PROJ_TN))
    xc = _conv_fwd(tag + "_conv", proj, cw, p["conv_b"][None], B)
    dtraw = proj[:, COL_DT:COL_DT + SSD_HEADS].reshape(T, SSD_GROUPS, 4)
    dtc = jnp.transpose(dtraw, (1, 0, 2))
    dtr = jnp.transpose(dtraw, (1, 2, 0))
    pcol, prow = _ssd_params(p)
    Y, y_ssd, hs = _ssd_fwd(tag + "_ssd", xc, proj, dtc, dtr, pcol, prow, p["ssd_norm"][None], B)
    qn = _headnorm_fwd(tag + "_qn", proj, COL_Q // 1024, p["q_norm"][None])
    kn = _headnorm_fwd(tag + "_kn", proj, COL_K // 1024, p["k_norm"][None])
    qT = (qn * ATT_SCALE).T
    oT, lse = _att_fwd(tag + "_att", kn, qT, proj[:, COL_V:COL_V + 1024].T.astype(BF16), bias, B)
    ymix = jnp.concatenate([y_ssd, oT.T], axis=1)
    rest = weights("rest", ymix)
    x2 = _mm(tag + "_out",
             [(ymix, pl.BlockSpec((ROW_T, MIX_SH), lambda i, n, k, j=j: (i, j)),
               rest["wout"], pl.BlockSpec((None, MIX_SH, D_MODEL), lambda i, n, k, j=j: (j, 0, 0)))
              for j in range(N_SHARD)],
             jax.ShapeDtypeStruct((T, D_MODEL), F32), pl.BlockSpec((ROW_T, D_MODEL), lambda i, n, k: (i, 0)),
             (nt, 1, 1), NN, (ROW_T, D_MODEL),
             res=(x1, pl.BlockSpec((ROW_T, D_MODEL), lambda i, n, k: (i, 0))))
    saved = dict(x1=x1, h2=h2, proj=proj, xc=xc, dtc=dtc, dtr=dtr, Y=Y, hs=hs,
                 qn=qn, kn=kn, qT=qT, oT=oT, lse=lse, ymix=ymix, win=win, cw=cw, wout=rest["wout"])
    return x2, saved


def _mixer_bwd(tag, dx2, sv, p, bias, B):
    T = dx2.shape[0]
    S = T // B
    nt = T // ROW_T
    sg = {}
    dymix = _dymix(tag + "_dymix", dx2, sv["wout"])
    tkw = min(TK_W, T)
    gwout = _mm(tag + "_dwout",
                [(sv["ymix"], pl.BlockSpec((tkw, MIX_SH), lambda m, n, k: (k, m)),
                  dx2, pl.BlockSpec((tkw, D_MODEL), lambda m, n, k: (k, 0)))],
                jax.ShapeDtypeStruct((N_SHARD, MIX_SH, D_MODEL), BF16),
                pl.BlockSpec((None, MIX_SH, D_MODEL), lambda m, n, k: (m, 0, 0)),
                (N_SHARD, 1, T // tkw), TN, (MIX_SH, D_MODEL))
    proj = sv["proj"]
    doT = dymix[:, 1024:].T
    dqn = _att_bwd_dq(tag + "_attdq", sv["kn"], sv["qT"], proj, sv["kn"].T, bias, doT, sv["oT"], sv["lse"], B).T
    dproj = lax.empty((T, IN_PAD), BF16)
    dkn, dproj = _att_bwd_dkv(tag + "_attdkv", sv["kn"], sv["qT"], proj, sv["qn"], bias, doT, sv["oT"], sv["lse"],
                              dymix, dproj, B)
    dproj, sg["q_norm"] = _headnorm_bwd(tag + "_qnb", dqn, proj, COL_Q // 1024, p["q_norm"][None], dproj)
    dproj, sg["k_norm"] = _headnorm_bwd(tag + "_knb", dkn, proj, COL_K // 1024, p["k_norm"][None], dproj)
    pcol, prow = _ssd_params(p)
    dxs, dB, dC, dproj, ddt, dpar, dnw = _ssd_bwd(tag + "_ssdb", dymix, sv["Y"], sv["xc"], proj, sv["dtc"], sv["dtr"],
                                                  pcol, prow, p["ssd_norm"][None], sv["hs"], dproj, B)
    dpar = jnp.sum(dpar, axis=0)
    sg["dt_bias"] = dpar[:, 0, :].reshape(SSD_HEADS)
    sg["a_log"] = dpar[:, 1, :].reshape(SSD_HEADS)
    sg["d_skip"] = dpar[:, 2, :].reshape(SSD_HEADS)
    sg["ssd_norm"] = jnp.sum(dnw, axis=0)
    dproj, sg["conv_w"], sg["conv_b"] = _conv_bwd(tag + "_convb", proj, dxs, dB, dC, sv["cw"], p["conv_b"][None],
                                                  dproj, B)
    ddt16 = jnp.transpose(ddt, (1, 0, 2)).reshape(T, SSD_HEADS)
    dproj = lax.dynamic_update_slice(dproj, jnp.pad(ddt16, ((0, 0), (0, IN_PAD - COL_DT - SSD_HEADS))).astype(BF16),
                                     (0, COL_DT))
    win = sv["win"]
    gwin = _mm(tag + "_dwin",
               [(sv["h2"], pl.BlockSpec((tkw, D_MODEL), lambda n, m, k: (k, 0)),
                 dproj, pl.BlockSpec((tkw, PROJ_TN), lambda n, m, k: (k, n)))],
               jax.ShapeDtypeStruct((D_MODEL, IN_PAD), BF16), pl.BlockSpec((D_MODEL, PROJ_TN), lambda n, m, k: (0, n)),
               (IN_PAD // PROJ_TN, 1, T // tkw), TN, (D_MODEL, PROJ_TN))
    dh2 = _mm(tag + "_dh2",
              [(dproj, pl.BlockSpec((ROW_T, PROJ_TN), lambda i, n, k, j=j: (i, j)),
                win, pl.BlockSpec((D_MODEL, PROJ_TN), lambda i, n, k, j=j: (0, j))) for j in range(IN_PAD // PROJ_TN)],
              jax.ShapeDtypeStruct((T, D_MODEL), F32), pl.BlockSpec((ROW_T, D_MODEL), lambda i, n, k: (i, 0)),
              (nt, 1, 1), NT, (ROW_T, D_MODEL))
    dx1, sg["mix_norm"] = _rms_bwd(tag + "_mixrmsb", dh2, sv["x1"], p["mix_norm"][None], dx2)
    return dx1, sg, gwout, gwin


def _win_pack(w):
    return jnp.concatenate([w[:, :3072], w[:, 3088:], w[:, 3072:3088],
                            jnp.zeros((w.shape[0], IN_PAD - IN_PROJ), w.dtype)], axis=1)


def _win_unpack(g):
    return jnp.concatenate([g[:, :3072], g[:, COL_DT:COL_DT + SSD_HEADS], g[:, 3072:COL_DT]], axis=1)


DT_LO = IN_SH * 2 - COL_Q


def _win_from_shards(sh):
    main = IN_SH - DT_LO
    return jnp.concatenate([sh[0], sh[1][:, :main], sh[2][:, SSD_HEADS - DT_LO:], sh[3], sh[1][:, main:],
                            sh[2][:, :SSD_HEADS - DT_LO], jnp.zeros((sh.shape[1], IN_PAD - IN_PROJ), sh.dtype)], axis=1)


def _win_to_shards(g):
    main = IN_SH - DT_LO
    a, b = IN_SH + main, IN_SH + 2 * main
    return jnp.stack([g[:, :IN_SH],
                      jnp.concatenate([g[:, IN_SH:a], g[:, COL_DT:COL_DT + DT_LO]], axis=1),
                      jnp.concatenate([g[:, COL_DT + DT_LO:COL_DT + SSD_HEADS], g[:, a:b]], axis=1),
                      g[:, b:COL_DT]])


def _local_step(x, target, small, weights, scatter, B):
    T = x.shape[0]
    bias = _att_bias((T // B) // ATT_B)
    saved = []
    h = x
    for l in range(DEPTH):
        tag = "l%d" % l
        p = {k: v[l] for k, v in small.items()}
        w1 = weights(l, "ffn1", h)
        x1, ffn1, d1 = _ffn_fwd(tag + "f1", h, p["ffn1_norm"][None], w1["g1"], w1["u1"],
                                lambda after, l=l: weights(l, "ffn1d", after)["d1"])
        x2, sv = _mixer_fwd(tag, x1, p, functools.partial(weights, l), bias, B)
        w2 = weights(l, "rest", x2)
        h, ffn2, _ = _ffn_fwd(tag + "f2", x2, p["ffn2_norm"][None], w2["g2"], w2["u2"], w2["d2"])
        saved.append((ffn1, sv, ffn2, dict(g1=w1["g1"], u1=w1["u1"], d1=d1), w2))
    d, lsum = _loss_grad("loss", h, target)
    sgrads = [None] * DEPTH
    for l in reversed(range(DEPTH)):
        tag = "l%db" % l
        p = {k: v[l] for k, v in small.items()}
        ffn1, sv, ffn2, w1, w2 = saved[l]
        sg = {}
        d, sg["ffn2_norm"] = _ffn_bwd(tag + "f2", d, ffn2, p["ffn2_norm"][None], w2["g2"], w2["u2"], w2["d2"],
                                      lambda gg, gu, gd, c, l=l: scatter(l, "ffn2", dict(g2=gg, u2=gu, d2=gd), c))
        d, sgm, gwout, gwin = _mixer_bwd(tag, d, sv, p, bias, B)
        sg.update(sgm)
        d = scatter(l, "mixer", dict(wout=gwout, win=gwin), d)
        d, sg["ffn1_norm"] = _ffn_bwd(tag + "f1", d, ffn1, p["ffn1_norm"][None], w1["g1"], w1["u1"], w1["d1"],
                                      lambda gg, gu, gd, c, l=l: scatter(l, "ffn1", dict(g1=gg, u1=gu, d1=gd), c))
        sgrads[l] = sg
    return lsum, d, sgrads


MESH = pl.DeviceIdType.MESH
ANY = pl.BlockSpec(memory_space=pl.ANY)


def _place():
    return lax.axis_index("x"), lax.axis_index("y"), lax.axis_index("c")


def _other_chips(x, y):
    return [(1 - x, y), (x, 1 - y), (1 - x, 1 - y)]


HBM = pl.BlockSpec(memory_space=pltpu.HBM)
SEM = pl.BlockSpec(memory_space=pltpu.SEMAPHORE)
EFFECT = pltpu.SideEffectType.DATAFLOW_SIDE_EFFECTING


def _hbm(a):
    return pltpu.with_memory_space_constraint(a, pltpu.HBM)


def _exchange(gather, layer, src, land, send, recv, n, act):
    x, y, c = _place()
    for k, (px, py) in enumerate(_other_chips(x, y)):
        for a in range(n):
            if gather:
                s_out, d_out, d_in = src[a].at[layer], land[a].at[2 * x + y], land[a].at[2 * px + py]
            else:
                s_out, d_out, d_in = src[a].at[2 * px + py], land[a].at[k], land[a].at[k]
            act(pltpu.make_async_remote_copy(
                src_ref=s_out, dst_ref=d_out if act is _start else d_in, send_sem=send.at[k * n + a],
                recv_sem=recv.at[k * n + a], device_id=(px, py, c), device_id_type=MESH))


def _start(cp):
    cp.start()


def _finish(cp):
    cp.wait_send()
    cp.wait_recv()


def _exchange_start(name, gather, layer, srcs, carry):
    n = len(srcs)
    lands = [lax.empty(((N_SHARD,) + s.shape[1:]) if gather else ((3,) + s.shape[1:]), s.dtype) for s in srcs]

    def body(*refs):
        _exchange(gather, layer, refs[:n], refs[n:2 * n], refs[2 * n + 1], refs[2 * n + 2], n, _start)

    srcs = [_hbm(a) for a in srcs]
    thru = [_hbm(a) for a in lands + [carry]]
    out = pl.pallas_call(
        body, name=name,
        out_shape=(pltpu.SemaphoreType.DMA((3 * n,)), pltpu.SemaphoreType.DMA((3 * n,)),
                   *[pltpu.HBM(a.shape, a.dtype) for a in thru]),
        in_specs=[HBM] * (2 * n + 1), out_specs=(SEM, SEM, *[HBM] * (n + 1)),
        input_output_aliases={n + i: 2 + i for i in range(n + 1)},
        compiler_params=pltpu.CompilerParams(has_side_effects=EFFECT))(*srcs, *thru)
    return dict(gather=gather, layer=layer, send=out[0], recv=out[1], srcs=srcs, lands=list(out[2:2 + n])), out[-1]


def _exchange_wait(name, ex, after):
    n = len(ex["srcs"])

    def body(*refs):
        _exchange(ex["gather"], ex["layer"], refs[:n], refs[n:2 * n], refs[2 * n], refs[2 * n + 1], n, _finish)

    out = pl.pallas_call(
        body, name=name, out_shape=[pltpu.HBM(a.shape, a.dtype) for a in ex["lands"]],
        in_specs=[HBM] * (2 * n) + [SEM, SEM, ANY], out_specs=[HBM] * n,
        input_output_aliases={n + i: i for i in range(n)},
        compiler_params=pltpu.CompilerParams(has_side_effects=EFFECT))(
            *ex["srcs"], *ex["lands"], ex["send"], ex["recv"], after)
    return list(out)


def _swap_sibling(parts):
    n = len(parts)

    def body(*refs):
        src, dst = refs[:n], refs[n:2 * n]
        send, recv = refs[2 * n:]
        x, y, c = _place()
        cps = [pltpu.make_async_remote_copy(src_ref=src[a], dst_ref=dst[a], send_sem=send.at[a], recv_sem=recv.at[a],
                                            device_id=(x, y, 1 - c), device_id_type=MESH) for a in range(n)]
        for cp in cps:
            cp.start()
        for cp in cps:
            cp.wait_recv()
        for cp in cps:
            cp.wait_send()

    return pl.pallas_call(
        body, out_shape=[jax.ShapeDtypeStruct(p.shape, p.dtype) for p in parts],
        in_specs=[ANY] * n, out_specs=[ANY] * n,
        scratch_shapes=[pltpu.SemaphoreType.DMA((n,)), pltpu.SemaphoreType.DMA((n,))],
        name="swap_sibling")(*parts)


def _allreduce_small(name, v, after):
    R = v.shape[0]

    def body(v_ref, after_ref, o_ref, buf, send, recv):
        x, y, c = _place()
        me = 4 * x + 2 * y + c
        buf[me] = v_ref[...]
        cps = []
        for k in range(1, 8):
            fx, fy, fc = (k >> 2) & 1, (k >> 1) & 1, k & 1
            px = 1 - x if fx else x
            py = 1 - y if fy else y
            pc = 1 - c if fc else c
            cp = pltpu.make_async_remote_copy(src_ref=v_ref, dst_ref=buf.at[me], send_sem=send.at[k - 1],
                                              recv_sem=recv.at[k - 1], device_id=(px, py, pc), device_id_type=MESH)
            cp.start()
            cps.append((cp, 4 * px + 2 * py + pc))
        for k, (cp, peer) in enumerate(cps):
            pltpu.make_async_remote_copy(src_ref=v_ref, dst_ref=buf.at[peer], send_sem=send.at[k], recv_sem=recv.at[k],
                                         device_id=(x, y, c), device_id_type=MESH).wait_recv()
        for cp, _ in cps:
            cp.wait_send()
        acc = buf[0]
        for d in range(1, 8):
            acc = acc + buf[d]
        o_ref[...] = acc

    return pl.pallas_call(
        body, out_shape=jax.ShapeDtypeStruct((R, 128), F32),
        in_specs=[pl.BlockSpec(memory_space=pltpu.VMEM), ANY], out_specs=pl.BlockSpec(memory_space=pltpu.VMEM),
        scratch_shapes=[pltpu.VMEM((8, R, 128), F32), pltpu.SemaphoreType.DMA((7,)), pltpu.SemaphoreType.DMA((7,))],
        name=name)(v, after)


def _row_tile(r):
    for t in (256, 128, 64, 32, 16, 8):
        if r % t == 0:
            return t
    raise ValueError(r)


def _sum4(name, me, parts, got):
    _, R, C = parts.shape
    tr = _row_tile(R)

    def body(me_ref, o_ref, g_ref, s_ref):
        s = o_ref[...].astype(F32)
        for k in range(3):
            s = s + g_ref[k].astype(F32)
        s_ref[...] = s.astype(BF16)

    return pl.pallas_call(
        body, out_shape=jax.ShapeDtypeStruct((R, C), BF16),
        grid_spec=pltpu.PrefetchScalarGridSpec(
            num_scalar_prefetch=1, grid=(R // tr,),
            in_specs=[pl.BlockSpec((None, tr, C), lambda i, me_ref: (me_ref[0], i, 0)),
                      pl.BlockSpec((3, tr, C), lambda i, me_ref: (0, i, 0))],
            out_specs=pl.BlockSpec((tr, C), lambda i, me_ref: (i, 0))),
        name=name, compiler_params=_cp("parallel"))(me, parts, got)


def _adamw(name, w, gparts, m, v):
    R, C = w.shape
    tr = _row_tile(R)
    ng = len(gparts)
    c1 = 1.0 - ADAM_B1 ** ADAM_STEP
    c2 = 1.0 - ADAM_B2 ** ADAM_STEP

    def body(*refs):
        w_ref = refs[0]
        g_refs = refs[1:1 + ng]
        m_ref, v_ref, go_ref, d_ref, mo_ref, vo_ref = refs[1 + ng:]
        g = g_refs[0][...]
        for r in g_refs[1:]:
            g = g + r[...]
        mn = ADAM_B1 * m_ref[...] + (1.0 - ADAM_B1) * g
        vn = ADAM_B2 * v_ref[...] + (1.0 - ADAM_B2) * (g * g)
        go_ref[...] = g
        mo_ref[...] = mn
        vo_ref[...] = vn
        d_ref[...] = -ADAM_LR * ((mn / c1) / (jnp.sqrt(vn / c2) + ADAM_EPS) + ADAM_WD * w_ref[...])

    blk = pl.BlockSpec((tr, C), lambda i: (i, 0))
    osh = jax.ShapeDtypeStruct((R, C), F32)
    return pl.pallas_call(
        body, out_shape=(osh, osh, osh, osh), grid=(R // tr,), in_specs=[blk] * (3 + ng), out_specs=(blk,) * 4,
        name=name, compiler_params=_cp("parallel"))(w, *gparts, m, v)


def _adamw_layers(name, w, sums, m, v):
    _, R, C = w.shape
    tr = _row_tile(R)
    nr = R // tr
    c1 = 1.0 - ADAM_B1 ** ADAM_STEP
    c2 = 1.0 - ADAM_B2 ** ADAM_STEP

    def body(w_ref, a0, b0, a1, b1, m_ref, v_ref, go_ref, d_ref, mo_ref, vo_ref):
        f = lambda r: r[...].astype(F32)
        g = jnp.where(pl.program_id(0) == 0, f(a0) + f(b0), f(a1) + f(b1))
        mn = ADAM_B1 * m_ref[...] + (1.0 - ADAM_B1) * g
        vn = ADAM_B2 * v_ref[...] + (1.0 - ADAM_B2) * (g * g)
        go_ref[...] = g
        mo_ref[...] = mn
        vo_ref[...] = vn
        d_ref[...] = -ADAM_LR * ((mn / c1) / (jnp.sqrt(vn / c2) + ADAM_EPS) + ADAM_WD * w_ref[...])

    blk = pl.BlockSpec((None, tr, C), lambda l, i: (l, i, 0))
    lay0 = pl.BlockSpec((tr, C), lambda l, i: (jnp.where(l == 0, i, nr - 1), 0))
    lay1 = pl.BlockSpec((tr, C), lambda l, i: (jnp.where(l == 1, i, 0), 0))
    oblk = pl.BlockSpec((tr, C), lambda l, i: (l * nr + i, 0))
    osh = jax.ShapeDtypeStruct((DEPTH * R, C), F32)
    res = pl.pallas_call(
        body, out_shape=(osh, osh, osh, osh), grid=(DEPTH, nr),
        in_specs=[blk, lay0, lay0, lay1, lay1, blk, blk], out_specs=(oblk,) * 4,
        name=name, compiler_params=_cp("arbitrary", "arbitrary"))(w, *sums[0], *sums[1], m, v)
    return [r.reshape(w.shape) for r in res]


BIG = [("ffn1_w_gate", "g1"), ("ffn1_w_up", "u1"), ("ffn1_w_down", "d1"), ("w_in", "win"), ("w_out", "wout"),
       ("ffn2_w_gate", "g2"), ("ffn2_w_up", "u2"), ("ffn2_w_down", "d2")]
SMALL = ["ffn1_norm", "mix_norm", "conv_b", "dt_bias", "a_log", "d_skip", "ssd_norm", "q_norm", "k_norm", "ffn2_norm"]
WEIGHTS = ["ffn1_norm", "ffn1_w_gate", "ffn1_w_up", "ffn1_w_down", "mix_norm", "w_in", "conv_w", "conv_b", "dt_bias",
           "a_log", "d_skip", "ssd_norm", "q_norm", "k_norm", "w_out", "ffn2_norm", "ffn2_w_gate", "ffn2_w_up",
           "ffn2_w_down"]
CONV_SH = CONV_DIM // N_SHARD
GATHER_GROUPS = [(0, "ffn1", ["g1", "u1"]), (0, "ffn1d", ["d1"]), (0, "win", ["win", "cw"]),
                 (0, "rest", ["wout", "g2", "u2", "d2"]),
                 (1, "all", ["g1", "u1", "d1", "win", "cw", "wout", "g2", "u2", "d2"])]


def _pad128(v):
    v = v.reshape(-1)
    return jnp.pad(v, (0, (-v.shape[0]) % 128))


def _pack(pieces):
    flat, offs, pos = [], [], 0
    for p in pieces:
        q = _pad128(p.astype(F32))
        offs.append(pos)
        pos += q.shape[0] // 128
        flat.append(q)
    total = -(-pos // 8) * 8
    out = jnp.concatenate(flat + [jnp.zeros(((total - pos) * 128,), F32)]).reshape(total, 128)
    return out, offs


def _unpack(packed, offs, shapes):
    out = []
    for off, shp in zip(offs, shapes):
        n = int(np.prod(shp))
        rows = -(-n // 128)
        out.append(packed[off:off + rows].reshape(-1)[:n].reshape(shp))
    return out


def kernel(x, ffn1_norm, ffn1_w_gate, ffn1_w_up, ffn1_w_down, mix_norm, w_in, conv_w, conv_b, dt_bias, a_log, d_skip, ssd_norm, q_norm, k_norm, w_out, ffn2_norm, ffn2_w_gate, ffn2_w_up, ffn2_w_down, loss_target, m_ffn1_norm, m_ffn1_w_gate, m_ffn1_w_up, m_ffn1_w_down, m_mix_norm, m_w_in, m_conv_w, m_conv_b, m_dt_bias, m_a_log, m_d_skip, m_ssd_norm, m_q_norm, m_k_norm, m_w_out, m_ffn2_norm, m_ffn2_w_gate, m_ffn2_w_up, m_ffn2_w_down, v_ffn1_norm, v_ffn1_w_gate, v_ffn1_w_up, v_ffn1_w_down, v_mix_norm, v_w_in, v_conv_w, v_conv_b, v_dt_bias, v_a_log, v_d_skip, v_ssd_norm, v_q_norm, v_k_norm, v_w_out, v_ffn2_norm, v_ffn2_w_gate, v_ffn2_w_up, v_ffn2_w_down):
    A = dict(locals())
    ix, iy, ic = _place()
    me = 2 * ix + iy
    B, S, _ = x.shape
    T = B * S

    own = {key: A[name].astype(BF16) for name, key in BIG}
    own["cw"] = conv_w
    exs, first_norm = [], ffn1_norm
    for gi, (l, _, keys) in enumerate(GATHER_GROUPS):
        ex, first_norm = _exchange_start("gather_start%d" % gi, True, l, [own[key] for key in keys], first_norm)
        exs.append(ex)
    landed = {}

    def weights(l, group, after):
        gi = [i for i, (gl, gname, _) in enumerate(GATHER_GROUPS) if gl == l and gname in (group, "all")][0]
        if gi not in landed:
            lands = _exchange_wait("gather_wait%d" % gi, exs[gi], after)
            landed[gi] = {}
            for key, land in zip(GATHER_GROUPS[gi][2], lands):
                full = lax.dynamic_update_slice(land, own[key][l][None], (me, 0, 0))
                if key == "win":
                    full = _win_from_shards(full)
                if key == "cw":
                    full = jnp.transpose(full, (1, 0, 2)).reshape(CONV_K, CONV_DIM)
                landed[gi][key] = full
        return landed[gi]

    pending = []

    def scatter(l, group, grads, carry):
        keys = sorted(grads)
        arrs = [grads[key] for key in keys]
        if "win" in grads:
            arrs[keys.index("win")] = _win_to_shards(grads["win"])
        ex, carry = _exchange_start("scatter_start_l%d_%s" % (l, group), False, None, arrs, carry)
        pending.append((l, keys, ex))
        return carry

    small = {name: A[name] for name in SMALL}
    small["ffn1_norm"] = first_norm
    lsum, dx, sgrads = _local_step(x.reshape(T, D_MODEL), loss_target.reshape(T, D_MODEL), small, weights, scatter, B)

    names = SMALL + ["conv_w"]
    shapes = [A[n].shape for n in SMALL] + [(DEPTH, CONV_K, CONV_DIM), ()]
    pieces = [jnp.stack([sgrads[l][n].reshape(shp[1:]) for l in range(DEPTH)]) for n, shp in zip(names, shapes)]
    pieces.append(0.5 / D_MODEL * jnp.sum(lsum))
    packed, offs = _pack(pieces)

    sums, after = {}, dx
    me1 = jnp.reshape(me, (1,)).astype(jnp.int32)
    for idx, (l, keys, ex) in enumerate(pending):
        lands = _exchange_wait("scatter_wait%d" % idx, ex, after)
        for key, g, got in zip(keys, ex["srcs"], lands):
            sums[key, l] = after = _sum4("sum_%s_l%d" % (key, l), me1, g, got)

    red = _unpack(_allreduce_small("allreduce_small", packed, after), offs, shapes)
    loss = red[-1]
    sg = dict(zip(names, red[:-1]))
    order = [(key, l) for _, key in BIG for l in range(DEPTH)]
    theirs = dict(zip(order, _swap_sibling([sums[k] for k in order])))

    out = {}
    for name, key in BIG:
        out[name] = _adamw_layers("adamw_" + key, A[name], [(sums[key, l], theirs[key, l]) for l in range(DEPTH)],
                                  A["m_" + name], A["v_" + name])

    wp, offs = _pack([A[n] for n in SMALL])
    gp, _ = _pack([sg[n] for n in SMALL])
    mp, _ = _pack([A["m_" + n] for n in SMALL])
    vp, _ = _pack([A["v_" + n] for n in SMALL])
    res = _adamw("adamw_small", wp, [gp], mp, vp)
    shapes = [A[n].shape for n in SMALL]
    res = [_unpack(r, offs, shapes) for r in res]
    for i, n in enumerate(SMALL):
        out[n] = [res[q][i] for q in range(4)]
    gcw = lax.dynamic_slice_in_dim(sg["conv_w"], me * CONV_SH, CONV_SH, axis=2)
    flat = lambda a: a.reshape(DEPTH * CONV_K, CONV_SH)
    res = _adamw("adamw_conv_w", flat(conv_w), [flat(gcw)], flat(m_conv_w), flat(v_conv_w))
    out["conv_w"] = [r.reshape(conv_w.shape) for r in res]

    outs = [loss, dx.reshape(B, S, D_MODEL)]
    for q in range(4):
        outs += [out[n][q] for n in WEIGHTS]
    return tuple(outs)
```

```python
import functools
import math

import numpy as np
import jax
import jax.numpy as jnp
from jax import lax
from jax.experimental import pallas as pl
from jax.experimental.pallas import tpu as pltpu

F32 = jnp.float32
BF16 = jnp.bfloat16

D_MODEL = 1024
DEPTH = 2
N_SHARD = 4
D_FF = 2816
FF_SH = D_FF // N_SHARD
SSD_HEADS = 16
HEAD_DIM = 64
SSD_GROUPS = 4
GROUP_W = 256
SSD_STATE = 128
CONV_K = 4
CONV_DIM = 2048
ATT_HEADS = 16
MIX_W = 2048
MIX_SH = MIX_W // N_SHARD
IN_PROJ = 6160
IN_SH = IN_PROJ // N_SHARD
IN_PAD = 6272
PROJ_TN = 896
COL_Z, COL_XBC, COL_Q, COL_K, COL_V, COL_DT = 0, 1024, 3072, 4096, 5120, 6144
EPS = 1e-6
NEG = -1e30
SSD_L = 256
ATT_B = 256
ROW_T = 512
HALF_T = ROW_T // 2
TK_W = 2048
CONV_CT = 256
CONV_R = 256
PAD_R = 8

ADAM_LR, ADAM_B1, ADAM_B2, ADAM_EPS, ADAM_WD, ADAM_STEP = 0.001, 0.9, 0.999, 1e-08, 0.01, 10

NN = (((1,), (0,)), ((), ()))
NT = (((1,), (1,)), ((), ()))
TN = (((0,), (0,)), ((), ()))

VMEM_LIMIT = 56 * 1024 * 1024


def _cp(*sem):
    return pltpu.CompilerParams(dimension_semantics=sem, vmem_limit_bytes=VMEM_LIMIT)


def _dot(a, b, dims):
    return lax.dot_general(a, b, dims, preferred_element_type=F32)


def _sigmoid(x):
    return 0.5 * jnp.tanh(0.5 * x) + 0.5


def _softplus(x):
    return jnp.maximum(x, 0.0) + jnp.log(1.0 + jnp.exp(-jnp.abs(x)))


def _mm(name, pairs, out_shape, out_spec, grid, dims, acc_shape, res=None, scale=1.0):
    nk = grid[2]
    npair = len(pairs)

    def body(*refs):
        ab = refs[:2 * npair]
        pos = 2 * npair
        res_ref = None
        if res is not None:
            res_ref = refs[pos]
            pos += 1
        out_ref = refs[pos]
        s = None
        for p in range(npair):
            d = _dot(ab[2 * p][...].astype(BF16), ab[2 * p + 1][...].astype(BF16), dims)
            s = d if s is None else s + d

        def finish(r):
            if scale != 1.0:
                r = r * scale
            if res_ref is not None:
                r = r + res_ref[...]
            out_ref[...] = r.astype(out_ref.dtype)

        if nk == 1:
            finish(s)
            return
        acc = refs[pos + 1]
        k = pl.program_id(2)

        @pl.when(k == 0)
        def _():
            acc[...] = s

        @pl.when(k > 0)
        def _():
            acc[...] += s

        @pl.when(k == nk - 1)
        def _():
            finish(acc[...])

    args, specs = [], []
    for a, a_spec, b, b_spec in pairs:
        args += [a, b]
        specs += [a_spec, b_spec]
    if res is not None:
        args.append(res[0])
        specs.append(res[1])
    return pl.pallas_call(
        body, out_shape=out_shape, grid=grid, in_specs=specs, out_specs=out_spec,
        scratch_shapes=[] if nk == 1 else [pltpu.VMEM(acc_shape, F32)], name=name,
        compiler_params=_cp("parallel", "parallel", "arbitrary"))(*args)


def _rms_fwd(name, x, w):
    T = x.shape[0]

    def body(x_ref, w_ref, o_ref):
        xv = x_ref[...]
        r = lax.rsqrt(jnp.mean(xv * xv, axis=-1, keepdims=True) + EPS)
        o_ref[...] = (xv * r * w_ref[...]).astype(BF16)

    return pl.pallas_call(
        body, out_shape=jax.ShapeDtypeStruct((T, D_MODEL), BF16), grid=(T // ROW_T,),
        in_specs=[pl.BlockSpec((ROW_T, D_MODEL), lambda i: (i, 0)), pl.BlockSpec((1, D_MODEL), lambda i: (0, 0))],
        out_specs=pl.BlockSpec((ROW_T, D_MODEL), lambda i: (i, 0)), name=name, compiler_params=_cp("parallel"))(x, w)


def _rms_bwd(name, dh, x, w, dres):
    T = x.shape[0]

    def body(dh_ref, x_ref, w_ref, dres_ref, dx_ref, dw_ref):
        @pl.when(pl.program_id(0) == 0)
        def _():
            dw_ref[...] = jnp.zeros_like(dw_ref)

        xv = x_ref[...]
        r = lax.rsqrt(jnp.mean(xv * xv, axis=-1, keepdims=True) + EPS)
        xhat = xv * r
        dhv = dh_ref[...]
        dxhat = dhv * w_ref[...]
        m = jnp.mean(dxhat * xhat, axis=-1, keepdims=True)
        dx_ref[...] = dres_ref[...] + r * (dxhat - xhat * m)
        dw_ref[...] += jnp.sum(dhv * xhat, axis=0, keepdims=True)

    row = pl.BlockSpec((ROW_T, D_MODEL), lambda i: (i, 0))
    vec = pl.BlockSpec((1, D_MODEL), lambda i: (0, 0))
    return pl.pallas_call(
        body, out_shape=(jax.ShapeDtypeStruct((T, D_MODEL), F32), jax.ShapeDtypeStruct((1, D_MODEL), F32)),
        grid=(T // ROW_T,), in_specs=[row, row, vec, row], out_specs=(row, vec), name=name,
        compiler_params=_cp("arbitrary"))(dh, x, w, dres)


def _loss_grad(name, y, t):
    T = y.shape[0]

    def body(y_ref, t_ref, dy_ref, l_ref):
        @pl.when(pl.program_id(0) == 0)
        def _():
            l_ref[...] = jnp.zeros_like(l_ref)

        e = y_ref[...] - t_ref[...]
        dy_ref[...] = e * (1.0 / D_MODEL)
        l_ref[...] += jnp.sum(e * e, axis=0, keepdims=True)

    row = pl.BlockSpec((ROW_T, D_MODEL), lambda i: (i, 0))
    vec = pl.BlockSpec((1, D_MODEL), lambda i: (0, 0))
    return pl.pallas_call(
        body, out_shape=(jax.ShapeDtypeStruct((T, D_MODEL), F32), jax.ShapeDtypeStruct((1, D_MODEL), F32)),
        grid=(T // ROW_T,), in_specs=[row, row], out_specs=(row, vec), name=name,
        compiler_params=_cp("arbitrary"))(y, t)


def _ffn_gate_up(name, h, wg, wu):
    T = h.shape[0]

    def body(h_ref, wg_ref, wu_ref, dgf_ref, duf_ref, a_ref):
        for r in range(0, ROW_T, HALF_T):
            rows = slice(r, r + HALF_T)
            hv = h_ref[rows, :]
            g = _dot(hv, wg_ref[...], NT)
            u = _dot(hv, wu_ref[...], NT)
            sg = _sigmoid(g)
            silu = g * sg
            dgf_ref[rows, :] = (u * (sg * (1.0 + g * (1.0 - sg)))).astype(BF16)
            duf_ref[rows, :] = silu.astype(BF16)
            a_ref[rows, :] = (silu * u).astype(BF16)

    wspec = pl.BlockSpec((None, FF_SH, D_MODEL), lambda j, i: (j, 0, 0))
    ospec = pl.BlockSpec((None, ROW_T, FF_SH), lambda j, i: (j, i, 0))
    osh = jax.ShapeDtypeStruct((N_SHARD, T, FF_SH), BF16)
    return pl.pallas_call(
        body, out_shape=(osh, osh, osh), grid=(N_SHARD, T // ROW_T),
        in_specs=[pl.BlockSpec((ROW_T, D_MODEL), lambda j, i: (i, 0)), wspec, wspec],
        out_specs=(ospec, ospec, ospec), name=name, compiler_params=_cp("parallel", "parallel"))(h, wg, wu)


def _ffn_dact(name, dx, wd, g, u):
    T = dx.shape[0]

    def body(dx_ref, wd_ref, g_ref, u_ref, dg_ref, du_ref):
        for r in range(0, ROW_T, HALF_T):
            rows = slice(r, r + HALF_T)
            da = 0.5 * _dot(dx_ref[rows, :].astype(BF16), wd_ref[...], NT)
            dg_ref[rows, :] = (da * g_ref[rows, :].astype(F32)).astype(BF16)
            du_ref[rows, :] = (da * u_ref[rows, :].astype(F32)).astype(BF16)

    aspec = pl.BlockSpec((None, ROW_T, FF_SH), lambda j, i: (j, i, 0))
    osh = jax.ShapeDtypeStruct((N_SHARD, T, FF_SH), BF16)
    return pl.pallas_call(
        body, out_shape=(osh, osh), grid=(N_SHARD, T // ROW_T),
        in_specs=[pl.BlockSpec((ROW_T, D_MODEL), lambda j, i: (i, 0)),
                  pl.BlockSpec((None, FF_SH, D_MODEL), lambda j, i: (j, 0, 0)), aspec, aspec],
        out_specs=(aspec, aspec), name=name, compiler_params=_cp("parallel", "parallel"))(dx, wd, g, u)


def _ffn_fwd(tag, x, nw, wg, wu, wd):
    T = x.shape[0]
    h = _rms_fwd(tag + "_rms", x, nw)
    g, u, a = _ffn_gate_up(tag + "_gu", h, wg, wu)
    if callable(wd):
        wd = wd(a)
    nt = T // ROW_T
    xo = _mm(tag + "_down",
             [(a, pl.BlockSpec((None, ROW_T, FF_SH), lambda i, n, k, j=j: (j, i, 0)),
               wd, pl.BlockSpec((None, FF_SH, D_MODEL), lambda i, n, k, j=j: (j, 0, 0))) for j in range(N_SHARD)],
             jax.ShapeDtypeStruct((T, D_MODEL), F32), pl.BlockSpec((ROW_T, D_MODEL), lambda i, n, k: (i, 0)),
             (nt, 1, 1), NN, (ROW_T, D_MODEL),
             res=(x, pl.BlockSpec((ROW_T, D_MODEL), lambda i, n, k: (i, 0))), scale=0.5)
    return xo, (x, h, g, u, a), wd


def _ffn_bwd(tag, dxo, saved, nw, wg, wu, wd, emit):
    x, h, g, u, a = saved
    T = x.shape[0]
    nt = T // ROW_T
    tkw = min(TK_W, T)
    nw_t = T // tkw
    dg, du = _ffn_dact(tag + "_dact", dxo, wd, g, u)
    actw = lambda f: pl.BlockSpec((None, tkw, FF_SH), f)
    gd = _mm(tag + "_dwd",
             [(a, actw(lambda m, n, k: (m, k, 0)), dxo, pl.BlockSpec((tkw, D_MODEL), lambda m, n, k: (k, 0)))],
             jax.ShapeDtypeStruct((N_SHARD, FF_SH, D_MODEL), BF16),
             pl.BlockSpec((None, FF_SH, D_MODEL), lambda m, n, k: (m, 0, 0)),
             (N_SHARD, 1, nw_t), TN, (FF_SH, D_MODEL), scale=0.5)
    hspec = pl.BlockSpec((tkw, D_MODEL), lambda j, n, k: (k, 0))
    gsh = jax.ShapeDtypeStruct((N_SHARD, FF_SH, D_MODEL), BF16)
    gspec = pl.BlockSpec((None, FF_SH, D_MODEL), lambda j, n, k: (j, 0, 0))
    gg = _mm(tag + "_dwg", [(dg, actw(lambda j, n, k: (j, k, 0)), h, hspec)], gsh, gspec,
             (N_SHARD, 1, nw_t), TN, (FF_SH, D_MODEL))
    gu = _mm(tag + "_dwu", [(du, actw(lambda j, n, k: (j, k, 0)), h, hspec)], gsh, gspec,
             (N_SHARD, 1, nw_t), TN, (FF_SH, D_MODEL))
    dg = emit(gg, gu, gd, dg)
    act = lambda j: pl.BlockSpec((None, ROW_T, FF_SH), lambda i, n, k: (j, i, 0))
    wsp = lambda j: pl.BlockSpec((None, FF_SH, D_MODEL), lambda i, n, k: (j, 0, 0))
    dh = _mm(tag + "_dh",
             [(dd, act(j), w, wsp(j)) for j in range(N_SHARD) for dd, w in ((dg, wg), (du, wu))],
             jax.ShapeDtypeStruct((T, D_MODEL), F32), pl.BlockSpec((ROW_T, D_MODEL), lambda i, n, k: (i, 0)),
             (nt, 1, 1), NN, (ROW_T, D_MODEL))
    return _rms_bwd(tag + "_rmsb", dh, x, nw, dxo)


def _seq_rows(ref, start, size, S):
    lo, hi = max(start, 0), min(start + size, S)
    parts = [ref[pl.ds(lo, hi - lo), :]]
    if lo > start:
        parts.insert(0, jnp.zeros((lo - start, ref.shape[1]), F32))
    if start + size > hi:
        parts.append(jnp.zeros((start + size - hi, ref.shape[1]), F32))
    return parts[0] if len(parts) == 1 else jnp.concatenate(parts, axis=0)


XBC_CB = COL_XBC // CONV_CT


def _conv_fwd(name, proj, w, b, B):
    T = proj.shape[0]
    S = T // B
    C = CONV_DIM

    def body(x_ref, w_ref, b_ref, o_ref):
        wv = w_ref[...]
        for c in range(S // CONV_R):
            r0 = c * CONV_R
            ch = _seq_rows(x_ref, r0 - PAD_R, CONV_R + PAD_R, S)
            pre = ch[PAD_R:] * wv[3:4] + b_ref[...]
            for s in range(1, CONV_K):
                pre = pre + pltpu.roll(ch, s, axis=0)[PAD_R:] * wv[3 - s:4 - s]
            o_ref[pl.ds(r0, CONV_R), :] = pre * _sigmoid(pre)

    return pl.pallas_call(
        body, out_shape=jax.ShapeDtypeStruct((T, C), F32), grid=(B, C // CONV_CT),
        in_specs=[pl.BlockSpec((S, CONV_CT), lambda bi, ci: (bi, XBC_CB + ci)),
                  pl.BlockSpec((CONV_K, CONV_CT), lambda bi, ci: (0, ci)),
                  pl.BlockSpec((1, CONV_CT), lambda bi, ci: (0, ci))],
        out_specs=pl.BlockSpec((S, CONV_CT), lambda bi, ci: (bi, ci)), name=name,
        compiler_params=_cp("parallel", "parallel"))(proj, w, b)


def _conv_bwd(name, proj, dxs, dB, dC, w, b, dproj, B):
    T = proj.shape[0]
    S = T // B
    C = CONV_DIM
    RW = CONV_R + PAD_R
    nx, nb = dxs.shape[1] // CONV_CT, dB.shape[1] // CONV_CT

    def body(x_ref, dx_in, db_in, dc_in, w_ref, b_ref, buf_ref, dx_ref, dw_ref, db_ref):
        @pl.when(pl.program_id(1) == 0)
        def _():
            dw_ref[...] = jnp.zeros_like(dw_ref)
            db_ref[...] = jnp.zeros_like(db_ref)

        ci = pl.program_id(0)
        wv = w_ref[...]
        dw = [jnp.zeros((1, CONV_CT), F32) for _ in range(CONV_K)]
        db = jnp.zeros((1, CONV_CT), F32)
        for c in range(S // CONV_R):
            r0 = c * CONV_R
            ch = _seq_rows(x_ref, r0 - PAD_R, RW + PAD_R, S)
            xs = [ch[PAD_R:]] + [pltpu.roll(ch, s, axis=0)[PAD_R:] for s in range(1, CONV_K)]
            pre = b_ref[...] + xs[0] * wv[3:4]
            for s in range(1, CONV_K):
                pre = pre + xs[s] * wv[3 - s:4 - s]
            sg = _sigmoid(pre)
            dout = jnp.where(ci < nx, _seq_rows(dx_in, r0, RW, S),
                             jnp.where(ci < nx + nb, _seq_rows(db_in, r0, RW, S), _seq_rows(dc_in, r0, RW, S)))
            dpre = dout * (sg * (1.0 + pre * (1.0 - sg)))
            dx = dpre[:CONV_R] * wv[3:4]
            for s in range(1, CONV_K):
                dx = dx + pltpu.roll(dpre, RW - s, axis=0)[:CONV_R] * wv[3 - s:4 - s]
            dx_ref[pl.ds(r0, CONV_R), :] = dx.astype(BF16)
            dcur = dpre[:CONV_R]
            db = db + jnp.sum(dcur, axis=0, keepdims=True)
            for s in range(CONV_K):
                dw[3 - s] = dw[3 - s] + jnp.sum(dcur * xs[s][:CONV_R], axis=0, keepdims=True)
        db_ref[...] += db
        for k in range(CONV_K):
            dw_ref[k:k + 1, :] += dw[k]

    seq = lambda f: pl.BlockSpec((S, CONV_CT), f)
    return pl.pallas_call(
        body,
        out_shape=(jax.ShapeDtypeStruct(dproj.shape, dproj.dtype), jax.ShapeDtypeStruct((CONV_K, C), F32),
                   jax.ShapeDtypeStruct((1, C), F32)),
        grid=(C // CONV_CT, B),
        in_specs=[seq(lambda ci, bi: (bi, XBC_CB + ci)),
                  seq(lambda ci, bi: (bi, jnp.minimum(ci, nx - 1))),
                  seq(lambda ci, bi: (bi, jnp.clip(ci - nx, 0, nb - 1))),
                  seq(lambda ci, bi: (bi, jnp.clip(ci - nx - nb, 0, nb - 1))),
                  pl.BlockSpec((CONV_K, CONV_CT), lambda ci, bi: (0, ci)),
                  pl.BlockSpec((1, CONV_CT), lambda ci, bi: (0, ci)), ANY],
        out_specs=(seq(lambda ci, bi: (bi, XBC_CB + ci)),
                   pl.BlockSpec((CONV_K, CONV_CT), lambda ci, bi: (0, ci)),
                   pl.BlockSpec((1, CONV_CT), lambda ci, bi: (0, ci))),
        input_output_aliases={6: 0},
        name=name, compiler_params=_cp("parallel", "arbitrary"))(proj, dxs, dB, dC, w, b, dproj)


def _tri_sum(tri, x, dims, tri_first, terms=3):
    out, rest = None, x
    for t in range(terms):
        part = rest.astype(BF16)
        if t + 1 < terms:
            rest = rest - part.astype(F32)
        d = _dot(tri, part, dims) if tri_first else _dot(part, tri, dims)
        out = d if out is None else out + d
    return out


def _total(x):
    return jnp.sum(jnp.sum(x, axis=0, keepdims=True), axis=-1, keepdims=True)


def _ssd_common(dtc_ref, dtr_ref, pcol_ref, prow_ref, b_ref, c_ref):
    L = SSD_L
    bias_c, alog_c = pcol_ref[0:1, :], pcol_ref[1:2, :]
    a_c = -jnp.exp(alog_c)
    dt_c = _softplus(dtc_ref[...] + bias_c)
    row = lax.broadcasted_iota(jnp.int32, (L, L), 0)
    col = lax.broadcasted_iota(jnp.int32, (L, L), 1)
    causal = row >= col
    tri = causal.astype(BF16)
    cum_c = _tri_sum(tri, dt_c * a_c, NN, True)
    a_r = -jnp.exp(prow_ref[:, 1:2])
    dt_r = _softplus(dtr_ref[...] + prow_ref[:, 0:1])
    cum_r = _tri_sum(tri, dt_r * a_r, NT, False)
    bb = b_ref[...].astype(BF16)
    cb = c_ref[...].astype(BF16)
    G = _dot(cb, bb, NT)
    return a_c, dt_c, causal, tri, cum_c, cum_r, bb, cb, G


def _ssd_fwd(name, xc, proj, dtc, dtr, pcol, prow, nw, B):
    T = xc.shape[0]
    S = T // B
    nb = S // SSD_L
    L = SSD_L

    def body(xs_ref, b_ref, c_ref, z_ref, dtc_ref, dtr_ref, pcol_ref, prow_ref, nw_ref, y_ref, yn_ref, hs_ref, H, yo_s):
        @pl.when(pl.program_id(2) == 0)
        def _():
            H[...] = jnp.zeros_like(H)

        a_c, dt_c, causal, tri, cum_c, cum_r, bb, cb, G = _ssd_common(dtc_ref, dtr_ref, pcol_ref, prow_ref, b_ref, c_ref)
        dsk = pcol_ref[2:3, :]
        clast = cum_c[L - 1:L, :]
        bf = b_ref[...]
        for h in range(4):
            hs_ref[h] = H[h]
            yo_s[h] = _dot(cb, H[h].astype(BF16), NN)
        for h in range(4):
            sl = slice(HEAD_DIM * h, HEAD_DIM * (h + 1))
            cc = cum_c[:, h:h + 1]
            lm = jnp.exp(jnp.where(causal, cc - cum_r[h:h + 1, :], NEG))
            M = (G * lm).astype(BF16)
            xh = xs_ref[:, sl]
            Xb = (xh * dt_c[:, h:h + 1]).astype(BF16)
            Hh = H[h]
            y = _dot(M, Xb, NN) + jnp.exp(cc) * yo_s[h]
            y_ref[:, sl] = y + dsk[:, h:h + 1] * xh
            cl = clast[:, h:h + 1]
            Bw = (bf * jnp.exp(cl - cc)).astype(BF16)
            H[h] = jnp.exp(cl) * Hh + _dot(Bw, Xb, TN)
        zv = z_ref[...]
        y2 = y_ref[...] * (zv * _sigmoid(zv))
        r = lax.rsqrt(jnp.mean(y2 * y2, axis=-1, keepdims=True) + EPS)
        yn_ref[...] = (y2 * r * nw_ref[...]).astype(BF16)

    rowi = lambda b, g, i: b * nb + i
    grp = pl.BlockSpec((L, GROUP_W), lambda b, g, i: (rowi(b, g, i), g))
    return pl.pallas_call(
        body,
        out_shape=(jax.ShapeDtypeStruct((T, 1024), F32), jax.ShapeDtypeStruct((T, 1024), BF16),
                   jax.ShapeDtypeStruct((B, SSD_GROUPS, nb, 4, SSD_STATE, HEAD_DIM), F32)),
        grid=(B, SSD_GROUPS, nb),
        in_specs=[grp,
                  pl.BlockSpec((L, SSD_STATE), lambda b, g, i: (rowi(b, g, i), 8 + g)),
                  pl.BlockSpec((L, SSD_STATE), lambda b, g, i: (rowi(b, g, i), 12 + g)),
                  grp,
                  pl.BlockSpec((None, L, 4), lambda b, g, i: (g, rowi(b, g, i), 0)),
                  pl.BlockSpec((None, 4, L), lambda b, g, i: (g, 0, rowi(b, g, i))),
                  pl.BlockSpec((None, 3, 4), lambda b, g, i: (g, 0, 0)),
                  pl.BlockSpec((None, 4, 3), lambda b, g, i: (g, 0, 0)),
                  pl.BlockSpec((1, GROUP_W), lambda b, g, i: (0, g))],
        out_specs=(grp, grp,
                   pl.BlockSpec((None, None, None, 4, SSD_STATE, HEAD_DIM), lambda b, g, i: (b, g, i, 0, 0, 0))),
        scratch_shapes=[pltpu.VMEM((4, SSD_STATE, HEAD_DIM), F32), pltpu.VMEM((4, L, HEAD_DIM), F32)], name=name,
        compiler_params=_cp("parallel", "parallel", "arbitrary"))(xc, xc, xc, proj, dtc, dtr, pcol, prow, nw)


def _ssd_bwd(name, dyn, Y, xc, proj, dtc, dtr, pcol, prow, nw, hs, dproj, B):
    T = xc.shape[0]
    S = T // B
    nb = S // SSD_L
    L = SSD_L

    def body(dyn_ref, y_ref, xs_ref, b_ref, c_ref, z_ref, dtc_ref, dtr_ref, pcol_ref, prow_ref, nw_ref, hs_ref, buf_ref,
             dxs_ref, db_ref, dc_ref, dz_ref, ddt_ref, dpar_ref, dnw_ref, dH, dm_s, dxo_s, ea_s, ex_s):
        @pl.when(pl.program_id(2) == 0)
        def _():
            dH[...] = jnp.zeros_like(dH)
            dpar_ref[...] = jnp.zeros_like(dpar_ref)
            dnw_ref[...] = jnp.zeros_like(dnw_ref)

        a_c, dt_c, causal, tri, cum_c, cum_r, bb, cb, G = _ssd_common(dtc_ref, dtr_ref, pcol_ref, prow_ref, b_ref, c_ref)
        dsk = pcol_ref[2:3, :]
        clast = cum_c[L - 1:L, :]
        bf = b_ref[...]
        cf = c_ref[...]
        Yv = y_ref[...]
        zv = z_ref[...]
        sz = _sigmoid(zv)
        silu = zv * sz
        y2 = Yv * silu
        r = lax.rsqrt(jnp.mean(y2 * y2, axis=-1, keepdims=True) + EPS)
        yhat = y2 * r
        dyv = dyn_ref[...]
        dnw_ref[...] += jnp.sum(dyv * yhat, axis=0, keepdims=True)
        dyhat = dyv * nw_ref[...]
        dy2 = r * (dyhat - yhat * jnp.mean(dyhat * yhat, axis=-1, keepdims=True))
        dY = dy2 * silu
        dz_ref[...] = (dy2 * Yv * (sz * (1.0 + zv * (1.0 - sz)))).astype(BF16)

        lane4 = lax.broadcasted_iota(jnp.int32, (1, 4), 1)
        dG = jnp.zeros((L, L), F32)
        dBs = jnp.zeros((L, SSD_STATE), F32)
        dCs = jnp.zeros((L, SSD_STATE), F32)
        ddsk = jnp.zeros((1, 4), F32)
        dcl = jnp.zeros((1, 4), F32)
        for h in range(4):
            sl = slice(HEAD_DIM * h, HEAD_DIM * (h + 1))
            xb = (xs_ref[:, sl] * dt_c[:, h:h + 1]).astype(BF16)
            dm_s[h] = _dot(dY[:, sl].astype(BF16), xb, NT)
            dxo_s[h] = _dot(bb, dH[h].astype(BF16), NN)
        for h in range(4):
            sl = slice(HEAD_DIM * h, HEAD_DIM * (h + 1))
            onehot = (lane4 == h).astype(F32)
            cc = cum_c[:, h:h + 1]
            cl = clast[:, h:h + 1]
            lm = jnp.exp(jnp.where(causal, cc - cum_r[h:h + 1, :], NEG))
            M = (G * lm).astype(BF16)
            xh = xs_ref[:, sl]
            dth = dt_c[:, h:h + 1]
            X = xh * dth
            Xb = X.astype(BF16)
            dYh = dY[:, sl]
            dYb = dYh.astype(BF16)
            Hb = hs_ref[h].astype(BF16)
            dHh = dH[h]
            dHb = dHh.astype(BF16)
            alpha = jnp.exp(cc)
            beta = jnp.exp(cl - cc)
            dXoff = beta * dxo_s[h]
            dX = _dot(M, dYb, TN) + dXoff
            dG = dG + dm_s[h] * lm
            dCs = dCs + _dot((alpha * dYh).astype(BF16), Hb, NT)
            dBs = dBs + _dot((beta * X).astype(BF16), dHb, NT)
            ypre = Yv[:, sl] - dsk[:, h:h + 1] * xh
            ea_s[:, sl] = dYb.astype(F32) * ypre - Xb.astype(F32) * dX
            ex_s[:, sl] = dX * xh
            dcl_h = (_total(dHh * (jnp.exp(cl) * hs_ref[h])) + _total(Xb.astype(F32) * dXoff))
            dcl = dcl + dcl_h * onehot
            ddsk = ddsk + _total(dYh * xh) * onehot
            dxs_ref[:, sl] = dsk[:, h:h + 1] * dYh + dX * dth
            dH[h] = jnp.exp(cl) * dHh + _dot((alpha * cf).astype(BF16), dYb, TN)
        dGb = dG.astype(BF16)
        dc_ref[...] = _dot(dGb, bb, NN) + dCs
        db_ref[...] = _dot(dGb, cb, TN) + dBs
        feat = lax.broadcasted_iota(jnp.int32, (GROUP_W, 4), 0)
        head = lax.broadcasted_iota(jnp.int32, (GROUP_W, 4), 1) * HEAD_DIM
        sel = ((feat >= head) & (feat < head + HEAD_DIM)).astype(BF16)
        dA = _tri_sum(sel, ea_s[...], NN, False)
        ddtx = _tri_sum(sel, ex_s[...], NN, False)
        last = lax.broadcasted_iota(jnp.int32, (L, 1), 0) == L - 1
        dA = dA + jnp.where(last, dcl, 0.0)
        dadt = _tri_sum(tri, dA, TN, True)
        ddt = dadt * a_c + ddtx
        d_a = jnp.sum(dadt * dt_c, axis=0, keepdims=True)
        ddraw = ddt * _sigmoid(dtc_ref[...] + pcol_ref[0:1, :])
        ddt_ref[...] = ddraw
        dpar_ref[0:1, :] += jnp.sum(ddraw, axis=0, keepdims=True)
        dpar_ref[1:2, :] += d_a * a_c
        dpar_ref[2:3, :] += ddsk

    rowi = lambda b, g, i: b * nb + (nb - 1 - i)
    grp = pl.BlockSpec((L, GROUP_W), lambda b, g, i: (rowi(b, g, i), g))
    st = pl.BlockSpec((L, SSD_STATE), lambda b, g, i: (rowi(b, g, i), g))
    f = jax.ShapeDtypeStruct
    return pl.pallas_call(
        body,
        out_shape=(f((T, 1024), F32), f((T, 512), F32), f((T, 512), F32), f(dproj.shape, dproj.dtype),
                   f((SSD_GROUPS, T, 4), F32), f((B, SSD_GROUPS, 3, 4), F32), f((B, 1, 1024), F32)),
        grid=(B, SSD_GROUPS, nb),
        in_specs=[grp, grp, grp,
                  pl.BlockSpec((L, SSD_STATE), lambda b, g, i: (rowi(b, g, i), 8 + g)),
                  pl.BlockSpec((L, SSD_STATE), lambda b, g, i: (rowi(b, g, i), 12 + g)),
                  grp,
                  pl.BlockSpec((None, L, 4), lambda b, g, i: (g, rowi(b, g, i), 0)),
                  pl.BlockSpec((None, 4, L), lambda b, g, i: (g, 0, rowi(b, g, i))),
                  pl.BlockSpec((None, 3, 4), lambda b, g, i: (g, 0, 0)),
                  pl.BlockSpec((None, 4, 3), lambda b, g, i: (g, 0, 0)),
                  pl.BlockSpec((1, GROUP_W), lambda b, g, i: (0, g)),
                  pl.BlockSpec((None, None, None, 4, SSD_STATE, HEAD_DIM), lambda b, g, i: (b, g, nb - 1 - i, 0, 0, 0)),
                  ANY],
        out_specs=(grp, st, st, grp,
                   pl.BlockSpec((None, L, 4), lambda b, g, i: (g, rowi(b, g, i), 0)),
                   pl.BlockSpec((None, None, 3, 4), lambda b, g, i: (b, g, 0, 0)),
                   pl.BlockSpec((None, 1, GROUP_W), lambda b, g, i: (b, 0, g))),
        input_output_aliases={12: 3},
        scratch_shapes=[pltpu.VMEM((4, SSD_STATE, HEAD_DIM), F32), pltpu.VMEM((4, L, L), F32),
                        pltpu.VMEM((4, L, HEAD_DIM), F32), pltpu.VMEM((L, GROUP_W), F32),
                        pltpu.VMEM((L, GROUP_W), F32)], name=name,
        compiler_params=_cp("parallel", "parallel", "arbitrary"))(
            dyn, Y, xc, xc, xc, proj, dtc, dtr, pcol, prow, nw, hs, dproj)


def _head_sel():
    sel = (np.arange(1024)[:, None] // HEAD_DIM == np.arange(ATT_HEADS)[None, :]).astype(np.float32)
    return jnp.asarray(sel, BF16), jnp.asarray(sel.T, BF16)


def _head_rms(xv, sel, selT):
    ms = _tri_sum(sel, xv * xv, NN, False, 1) * (1.0 / HEAD_DIM)
    return _tri_sum(selT, lax.rsqrt(ms + EPS), NN, False, 2)


def _headnorm_fwd(name, proj, col_block, w):
    T = proj.shape[0]
    sel, selT = _head_sel()

    def body(x_ref, w_ref, sel_ref, selT_ref, o_ref):
        xv = x_ref[...]
        o_ref[...] = (xv * _head_rms(xv, sel_ref[...], selT_ref[...]) * w_ref[...]).astype(BF16)

    full = lambda shp: pl.BlockSpec(shp, lambda i: (0, 0))
    return pl.pallas_call(
        body, out_shape=jax.ShapeDtypeStruct((T, 1024), BF16), grid=(T // ROW_T,),
        in_specs=[pl.BlockSpec((ROW_T, 1024), lambda i: (i, col_block)), full((1, 1024)), full((1024, ATT_HEADS)),
                  full((ATT_HEADS, 1024))],
        out_specs=pl.BlockSpec((ROW_T, 1024), lambda i: (i, 0)), name=name, compiler_params=_cp("parallel"))(
            proj, jnp.tile(w, (1, ATT_HEADS)), sel, selT)


def _headnorm_bwd(name, dn, proj, col_block, w, dproj):
    T = proj.shape[0]
    sel, selT = _head_sel()

    def body(dn_ref, x_ref, w_ref, sel_ref, selT_ref, buf_ref, dx_ref, dw_ref):
        @pl.when(pl.program_id(0) == 0)
        def _():
            dw_ref[...] = jnp.zeros_like(dw_ref)

        xv = x_ref[...]
        sl, slT = sel_ref[...], selT_ref[...]
        rb = _head_rms(xv, sl, slT)
        xhat = xv * rb
        dnv = dn_ref[...]
        dxhat = dnv * w_ref[...]
        mean = _tri_sum(slT, _tri_sum(sl, dxhat * xhat, NN, False, 2) * (1.0 / HEAD_DIM), NN, False, 2)
        dx_ref[...] = (rb * (dxhat - xhat * mean)).astype(BF16)
        dw_ref[...] += jnp.sum(dnv * xhat, axis=0, keepdims=True)

    here = pl.BlockSpec((ROW_T, 1024), lambda i: (i, col_block))
    full = lambda shp: pl.BlockSpec(shp, lambda i: (0, 0))
    dx, dw = pl.pallas_call(
        body, out_shape=(jax.ShapeDtypeStruct(dproj.shape, dproj.dtype), jax.ShapeDtypeStruct((1, 1024), F32)),
        grid=(T // ROW_T,),
        in_specs=[pl.BlockSpec((ROW_T, 1024), lambda i: (i, 0)), here, full((1, 1024)), full((1024, ATT_HEADS)),
                  full((ATT_HEADS, 1024)), ANY],
        out_specs=(here, full((1, 1024))), input_output_aliases={5: 0},
        name=name, compiler_params=_cp("arbitrary"))(dn, proj, jnp.tile(w, (1, ATT_HEADS)), sel, selT, dproj)
    return dx, jnp.sum(dw.reshape(ATT_HEADS, HEAD_DIM), axis=0, keepdims=True)


def _att_bias(nq):
    j = np.arange(ATT_B)[:, None]
    i = np.arange(ATT_B)[None, :]
    out = np.empty((nq, ATT_B, ATT_B), np.float32)
    for dblk in range(nq):
        dl = ATT_B * dblk + i - j
        cnt = ((dl >= 0) & (dl <= 128)).astype(np.float32)
        cnt += ((dl >= 0) & (dl % 4 == 0) & (dl <= 512))
        cnt += ((dl >= 0) & (dl % 16 == 0) & (dl <= 2048))
        out[dblk] = np.where(cnt > 0, np.log(np.maximum(cnt, 1.0)), NEG)
    return jnp.asarray(out)


def _row_pair(nq):
    def f(r, c):
        first = c <= r
        return jnp.where(first, r, nq - 1 - r), jnp.where(first, c, c - (r + 1))
    return f


def _col_pair(nq):
    def f(r, c):
        first = c < nq - r
        kj = jnp.where(first, r, nq - 1 - r)
        return jnp.where(first, r + c, nq - 1 - r + (c - (nq - r))), kj
    return f


ATT_SCALE = 1.0 / math.sqrt(HEAD_DIM)
ATT_HS = 4
ATT_W = ATT_HS * HEAD_DIM


def _att_maps(nq, qk):
    return dict(
        q_tok=lambda b, g, r, c: (b * nq + qk(r, c)[0], g),
        k_tok=lambda b, g, r, c: (b * nq + qk(r, c)[1], g),
        v_tok=lambda b, g, r, c: (b * nq + qk(r, c)[1], COL_V // ATT_W + g),
        q_feat=lambda b, g, r, c: (g, b * nq + qk(r, c)[0]),
        k_feat=lambda b, g, r, c: (g, b * nq + qk(r, c)[1]),
        bias=lambda b, g, r, c: (qk(r, c)[0] - qk(r, c)[1], 0, 0),
        lse=lambda b, g, r, c: (g, 0, b * nq + qk(r, c)[0]),
        do_tok=lambda b, g, r, c: (b * nq + qk(r, c)[0], ATT_HS + g))


def _att_fwd(name, kn, qT, vT, bias, B):
    T = kn.shape[0]
    nq = (T // B) // ATT_B
    qk = _row_pair(nq)
    mp = _att_maps(nq, qk)

    def body(k_ref, qT_ref, vT_ref, bias_ref, oT_ref, lse_ref, m_s, l_s, acc_s, s_s):
        qi, kj = qk(pl.program_id(2), pl.program_id(3))

        @pl.when(kj == 0)
        def _():
            m_s[...] = jnp.full_like(m_s, NEG)
            l_s[...] = jnp.zeros_like(l_s)
            acc_s[...] = jnp.zeros_like(acc_s)

        bv = bias_ref[...]
        for h in range(ATT_HS):
            rs = slice(HEAD_DIM * h, HEAD_DIM * (h + 1))
            s_s[h] = _dot(k_ref[:, rs], qT_ref[rs, :], NN)
        for h in range(ATT_HS):
            rs = slice(HEAD_DIM * h, HEAD_DIM * (h + 1))
            s = s_s[h] + bv
            m_prev = m_s[h:h + 1, :]
            m_new = jnp.maximum(m_prev, jnp.max(s, axis=0, keepdims=True))
            alpha = jnp.exp(m_prev - m_new)
            p = jnp.exp(s - m_new)
            l_s[h:h + 1, :] = alpha * l_s[h:h + 1, :] + jnp.sum(p, axis=0, keepdims=True)
            acc_s[rs, :] = alpha * acc_s[rs, :] + _dot(vT_ref[rs, :], p.astype(BF16), NN)
            m_s[h:h + 1, :] = m_new

        @pl.when(kj == qi)
        def _():
            for h in range(ATT_HS):
                rs = slice(HEAD_DIM * h, HEAD_DIM * (h + 1))
                oT_ref[rs, :] = (acc_s[rs, :] / l_s[h:h + 1, :]).astype(BF16)
            lse_ref[...] = m_s[...] + jnp.log(l_s[...])

    tok = (ATT_B, ATT_W)
    feat = (ATT_W, ATT_B)
    return pl.pallas_call(
        body,
        out_shape=(jax.ShapeDtypeStruct((1024, T), BF16), jax.ShapeDtypeStruct((ATT_HEADS // ATT_HS, ATT_HS, T), F32)),
        grid=(B, ATT_HEADS // ATT_HS, nq // 2, nq + 1),
        in_specs=[pl.BlockSpec(tok, mp["k_tok"]), pl.BlockSpec(feat, mp["q_feat"]), pl.BlockSpec(feat, mp["k_feat"]),
                  pl.BlockSpec((None, ATT_B, ATT_B), mp["bias"])],
        out_specs=(pl.BlockSpec(feat, mp["q_feat"]), pl.BlockSpec((None, ATT_HS, ATT_B), mp["lse"])),
        scratch_shapes=[pltpu.VMEM((ATT_HS, ATT_B), F32), pltpu.VMEM((ATT_HS, ATT_B), F32),
                        pltpu.VMEM((ATT_W, ATT_B), F32), pltpu.VMEM((ATT_HS, ATT_B, ATT_B), F32)],
        name=name, compiler_params=_cp("parallel", "parallel", "arbitrary", "arbitrary"))(kn, qT, vT, bias)


def _att_scores(k_ref, qT_ref, v_ref, doT_ref, s_s, dp_s):
    for h in range(ATT_HS):
        rs = slice(HEAD_DIM * h, HEAD_DIM * (h + 1))
        s_s[h] = _dot(k_ref[:, rs], qT_ref[rs, :], NN)
        dp_s[h] = _dot(v_ref[:, rs].astype(BF16), doT_ref[rs, :].astype(BF16), NN)


def _att_p_ds(s_s, dp_s, doT_ref, oT_ref, lse_ref, bv, h):
    rs = slice(HEAD_DIM * h, HEAD_DIM * (h + 1))
    delta = jnp.sum(doT_ref[rs, :] * oT_ref[rs, :].astype(F32), axis=0, keepdims=True)
    p = jnp.exp(s_s[h] + bv - lse_ref[h:h + 1, :])
    return p, p * (dp_s[h] - delta)


def _att_bwd_dq(name, kn, qT, vb, knT, bias, doT, oT, lse, B):
    T = kn.shape[0]
    nq = (T // B) // ATT_B
    qk = _row_pair(nq)
    mp = _att_maps(nq, qk)

    def body(k_ref, qT_ref, v_ref, kT_ref, bias_ref, doT_ref, oT_ref, lse_ref, dqT_ref, acc_s, s_s, dp_s):
        qi, kj = qk(pl.program_id(2), pl.program_id(3))

        @pl.when(kj == 0)
        def _():
            acc_s[...] = jnp.zeros_like(acc_s)

        bv = bias_ref[...]
        _att_scores(k_ref, qT_ref, v_ref, doT_ref, s_s, dp_s)
        for h in range(ATT_HS):
            rs = slice(HEAD_DIM * h, HEAD_DIM * (h + 1))
            p, ds = _att_p_ds(s_s, dp_s, doT_ref, oT_ref, lse_ref, bv, h)
            acc_s[rs, :] += _dot(kT_ref[rs, :], ds.astype(BF16), NN)

        @pl.when(kj == qi)
        def _():
            dqT_ref[...] = acc_s[...] * ATT_SCALE

    tok = (ATT_B, ATT_W)
    feat = (ATT_W, ATT_B)
    return pl.pallas_call(
        body, out_shape=jax.ShapeDtypeStruct((1024, T), F32), grid=(B, ATT_HEADS // ATT_HS, nq // 2, nq + 1),
        in_specs=[pl.BlockSpec(tok, mp["k_tok"]), pl.BlockSpec(feat, mp["q_feat"]), pl.BlockSpec(tok, mp["v_tok"]),
                  pl.BlockSpec(feat, mp["k_feat"]), pl.BlockSpec((None, ATT_B, ATT_B), mp["bias"]),
                  pl.BlockSpec(feat, mp["q_feat"]), pl.BlockSpec(feat, mp["q_feat"]),
                  pl.BlockSpec((None, ATT_HS, ATT_B), mp["lse"])],
        out_specs=pl.BlockSpec(feat, mp["q_feat"]),
        scratch_shapes=[pltpu.VMEM((ATT_W, ATT_B), F32), pltpu.VMEM((ATT_HS, ATT_B, ATT_B), F32),
                        pltpu.VMEM((ATT_HS, ATT_B, ATT_B), F32)],
        name=name, compiler_params=_cp("parallel", "parallel", "arbitrary", "arbitrary"))(
            kn, qT, vb, knT, bias, doT, oT, lse)


def _att_bwd_dkv(name, kn, qT, vb, qn, bias, doT, oT, lse, dyn, dproj, B):
    T = kn.shape[0]
    nq = (T // B) // ATT_B
    qk = _col_pair(nq)
    mp = _att_maps(nq, qk)

    def body(k_ref, qT_ref, v_ref, q_ref, bias_ref, doT_ref, oT_ref, lse_ref, do_ref, buf_ref, dk_ref, dv_ref, dk_s, dv_s,
             s_s, dp_s):
        qi, kj = qk(pl.program_id(2), pl.program_id(3))

        @pl.when(qi == kj)
        def _():
            dk_s[...] = jnp.zeros_like(dk_s)
            dv_s[...] = jnp.zeros_like(dv_s)

        bv = bias_ref[...]
        _att_scores(k_ref, qT_ref, v_ref, doT_ref, s_s, dp_s)
        for h in range(ATT_HS):
            rs = slice(HEAD_DIM * h, HEAD_DIM * (h + 1))
            p, ds = _att_p_ds(s_s, dp_s, doT_ref, oT_ref, lse_ref, bv, h)
            dv_s[h] += _dot(p.astype(BF16), do_ref[:, rs].astype(BF16), NN)
            dk_s[h] += _dot(ds.astype(BF16), q_ref[:, rs], NN)

        @pl.when(qi == nq - 1)
        def _():
            for h in range(ATT_HS):
                rs = slice(HEAD_DIM * h, HEAD_DIM * (h + 1))
                dk_ref[:, rs] = dk_s[h] * ATT_SCALE
                dv_ref[:, rs] = dv_s[h].astype(BF16)

    tok = (ATT_B, ATT_W)
    feat = (ATT_W, ATT_B)
    v_cb = COL_V // ATT_W
    return pl.pallas_call(
        body, out_shape=(jax.ShapeDtypeStruct((T, 1024), F32), jax.ShapeDtypeStruct(dproj.shape, dproj.dtype)),
        grid=(B, ATT_HEADS // ATT_HS, nq // 2, nq + 1),
        in_specs=[pl.BlockSpec(tok, mp["k_tok"]), pl.BlockSpec(feat, mp["q_feat"]), pl.BlockSpec(tok, mp["v_tok"]),
                  pl.BlockSpec(tok, mp["q_tok"]), pl.BlockSpec((None, ATT_B, ATT_B), mp["bias"]),
                  pl.BlockSpec(feat, mp["q_feat"]), pl.BlockSpec(feat, mp["q_feat"]),
                  pl.BlockSpec((None, ATT_HS, ATT_B), mp["lse"]), pl.BlockSpec(tok, mp["do_tok"]), ANY],
        out_specs=(pl.BlockSpec(tok, mp["k_tok"]),
                   pl.BlockSpec(tok, lambda b, g, r, c: (b * nq + qk(r, c)[1], v_cb + g))),
        input_output_aliases={9: 1},
        scratch_shapes=[pltpu.VMEM((ATT_HS, ATT_B, HEAD_DIM), F32), pltpu.VMEM((ATT_HS, ATT_B, HEAD_DIM), F32),
                        pltpu.VMEM((ATT_HS, ATT_B, ATT_B), F32), pltpu.VMEM((ATT_HS, ATT_B, ATT_B), F32)],
        name=name, compiler_params=_cp("parallel", "parallel", "arbitrary", "arbitrary"))(
            kn, qT, vb, qn, bias, doT, oT, lse, dyn, dproj)


def _group_cols(v):
    return v.reshape(SSD_GROUPS, 4)


def _ssd_params(p):
    rows = jnp.stack([_group_cols(p["dt_bias"]), _group_cols(p["a_log"]), _group_cols(p["d_skip"])], axis=1)
    return rows, jnp.swapaxes(rows, 1, 2)


def _dymix(name, dx, wout):
    T = dx.shape[0]

    def body(dx_ref, w_ref, o_ref):
        dxb = dx_ref[...].astype(BF16)
        for n in range(N_SHARD):
            o_ref[:, MIX_SH * n:MIX_SH * (n + 1)] = _dot(dxb, w_ref[n], NT)

    return pl.pallas_call(
        body, out_shape=jax.ShapeDtypeStruct((T, MIX_W), F32), grid=(T // ROW_T,),
        in_specs=[pl.BlockSpec((ROW_T, D_MODEL), lambda i: (i, 0)),
                  pl.BlockSpec((N_SHARD, MIX_SH, D_MODEL), lambda i: (0, 0, 0))],
        out_specs=pl.BlockSpec((ROW_T, MIX_W), lambda i: (i, 0)), name=name, compiler_params=_cp("parallel"))(dx, wout)


def _mixer_fwd(tag, x1, p, weights, bias, B):
    T = x1.shape[0]
    S = T // B
    nt = T // ROW_T
    h2 = _rms_fwd(tag + "_mixrms", x1, p["mix_norm"][None])
    wi = weights("win", h2)
    win, cw = wi["win"], wi["cw"]
    proj = _mm(tag + "_proj",
               [(h2, pl.BlockSpec((ROW_T, D_MODEL), lambda j, i, k: (i, 0)),
                 win, pl.BlockSpec((D_MODEL, PROJ_TN), lambda j, i, k: (0, j)))],
               jax.ShapeDtypeStruct((T, IN_PAD), F32), pl.BlockSpec((ROW_T, PROJ_TN), lambda j, i, k: (i, j)),
               (IN_PAD // PROJ_TN, nt, 1), NN, (ROW_T, PROJ_TN))
    xc = _conv_fwd(tag + "_conv", proj, cw, p["conv_b"][None], B)
    dtraw = proj[:, COL_DT:COL_DT + SSD_HEADS].reshape(T, SSD_GROUPS, 4)
    dtc = jnp.transpose(dtraw, (1, 0, 2))
    dtr = jnp.transpose(dtraw, (1, 2, 0))
    pcol, prow = _ssd_params(p)
    Y, y_ssd, hs = _ssd_fwd(tag + "_ssd", xc, proj, dtc, dtr, pcol, prow, p["ssd_norm"][None], B)
    qn = _headnorm_fwd(tag + "_qn", proj, COL_Q // 1024, p["q_norm"][None])
    kn = _headnorm_fwd(tag + "_kn", proj, COL_K // 1024, p["k_norm"][None])
    qT = (qn * ATT_SCALE).T
    oT, lse = _att_fwd(tag + "_att", kn, qT, proj[:, COL_V:COL_V + 1024].T.astype(BF16), bias, B)
    ymix = jnp.concatenate([y_ssd, oT.T], axis=1)
    rest = weights("rest", ymix)
    x2 = _mm(tag + "_out",
             [(ymix, pl.BlockSpec((ROW_T, MIX_SH), lambda i, n, k, j=j: (i, j)),
               rest["wout"], pl.BlockSpec((None, MIX_SH, D_MODEL), lambda i, n, k, j=j: (j, 0, 0)))
              for j in range(N_SHARD)],
             jax.ShapeDtypeStruct((T, D_MODEL), F32), pl.BlockSpec((ROW_T, D_MODEL), lambda i, n, k: (i, 0)),
             (nt, 1, 1), NN, (ROW_T, D_MODEL),
             res=(x1, pl.BlockSpec((ROW_T, D_MODEL), lambda i, n, k: (i, 0))))
    saved = dict(x1=x1, h2=h2, proj=proj, xc=xc, dtc=dtc, dtr=dtr, Y=Y, hs=hs,
                 qn=qn, kn=kn, qT=qT, oT=oT, lse=lse, ymix=ymix, win=win, cw=cw, wout=rest["wout"])
    return x2, saved


def _mixer_bwd(tag, dx2, sv, p, bias, B):
    T = dx2.shape[0]
    S = T // B
    nt = T // ROW_T
    sg = {}
    dymix = _dymix(tag + "_dymix", dx2, sv["wout"])
    tkw = min(TK_W, T)
    gwout = _mm(tag + "_dwout",
                [(sv["ymix"], pl.BlockSpec((tkw, MIX_SH), lambda m, n, k: (k, m)),
                  dx2, pl.BlockSpec((tkw, D_MODEL), lambda m, n, k: (k, 0)))],
                jax.ShapeDtypeStruct((N_SHARD, MIX_SH, D_MODEL), BF16),
                pl.BlockSpec((None, MIX_SH, D_MODEL), lambda m, n, k: (m, 0, 0)),
                (N_SHARD, 1, T // tkw), TN, (MIX_SH, D_MODEL))
    proj = sv["proj"]
    doT = dymix[:, 1024:].T
    dqn = _att_bwd_dq(tag + "_attdq", sv["kn"], sv["qT"], proj, sv["kn"].T, bias, doT, sv["oT"], sv["lse"], B).T
    dproj = lax.empty((T, IN_PAD), BF16)
    dkn, dproj = _att_bwd_dkv(tag + "_attdkv", sv["kn"], sv["qT"], proj, sv["qn"], bias, doT, sv["oT"], sv["lse"],
                              dymix, dproj, B)
    dproj, sg["q_norm"] = _headnorm_bwd(tag + "_qnb", dqn, proj, COL_Q // 1024, p["q_norm"][None], dproj)
    dproj, sg["k_norm"] = _headnorm_bwd(tag + "_knb", dkn, proj, COL_K // 1024, p["k_norm"][None], dproj)
    pcol, prow = _ssd_params(p)
    dxs, dB, dC, dproj, ddt, dpar, dnw = _ssd_bwd(tag + "_ssdb", dymix, sv["Y"], sv["xc"], proj, sv["dtc"], sv["dtr"],
                                                  pcol, prow, p["ssd_norm"][None], sv["hs"], dproj, B)
    dpar = jnp.sum(dpar, axis=0)
    sg["dt_bias"] = dpar[:, 0, :].reshape(SSD_HEADS)
    sg["a_log"] = dpar[:, 1, :].reshape(SSD_HEADS)
    sg["d_skip"] = dpar[:, 2, :].reshape(SSD_HEADS)
    sg["ssd_norm"] = jnp.sum(dnw, axis=0)
    dproj, sg["conv_w"], sg["conv_b"] = _conv_bwd(tag + "_convb", proj, dxs, dB, dC, sv["cw"], p["conv_b"][None],
                                                  dproj, B)
    ddt16 = jnp.transpose(ddt, (1, 0, 2)).reshape(T, SSD_HEADS)
    dproj = lax.dynamic_update_slice(dproj, jnp.pad(ddt16, ((0, 0), (0, IN_PAD - COL_DT - SSD_HEADS))).astype(BF16),
                                     (0, COL_DT))
    win = sv["win"]
    gwin = _mm(tag + "_dwin",
               [(sv["h2"], pl.BlockSpec((tkw, D_MODEL), lambda n, m, k: (k, 0)),
                 dproj, pl.BlockSpec((tkw, PROJ_TN), lambda n, m, k: (k, n)))],
               jax.ShapeDtypeStruct((D_MODEL, IN_PAD), BF16), pl.BlockSpec((D_MODEL, PROJ_TN), lambda n, m, k: (0, n)),
               (IN_PAD // PROJ_TN, 1, T // tkw), TN, (D_MODEL, PROJ_TN))
    dh2 = _mm(tag + "_dh2",
              [(dproj, pl.BlockSpec((ROW_T, PROJ_TN), lambda i, n, k, j=j: (i, j)),
                win, pl.BlockSpec((D_MODEL, PROJ_TN), lambda i, n, k, j=j: (0, j))) for j in range(IN_PAD // PROJ_TN)],
              jax.ShapeDtypeStruct((T, D_MODEL), F32), pl.BlockSpec((ROW_T, D_MODEL), lambda i, n, k: (i, 0)),
              (nt, 1, 1), NT, (ROW_T, D_MODEL))
    dx1, sg["mix_norm"] = _rms_bwd(tag + "_mixrmsb", dh2, sv["x1"], p["mix_norm"][None], dx2)
    return dx1, sg, gwout, gwin


def _win_pack(w):
    return jnp.concatenate([w[:, :3072], w[:, 3088:], w[:, 3072:3088],
                            jnp.zeros((w.shape[0], IN_PAD - IN_PROJ), w.dtype)], axis=1)


def _win_unpack(g):
    return jnp.concatenate([g[:, :3072], g[:, COL_DT:COL_DT + SSD_HEADS], g[:, 3072:COL_DT]], axis=1)


DT_LO = IN_SH * 2 - COL_Q


def _win_from_shards(sh):
    main = IN_SH - DT_LO
    return jnp.concatenate([sh[0], sh[1][:, :main], sh[2][:, SSD_HEADS - DT_LO:], sh[3], sh[1][:, main:],
                            sh[2][:, :SSD_HEADS - DT_LO], jnp.zeros((sh.shape[1], IN_PAD - IN_PROJ), sh.dtype)], axis=1)


def _win_to_shards(g):
    main = IN_SH - DT_LO
    a, b = IN_SH + main, IN_SH + 2 * main
    return jnp.stack([g[:, :IN_SH],
                      jnp.concatenate([g[:, IN_SH:a], g[:, COL_DT:COL_DT + DT_LO]], axis=1),
                      jnp.concatenate([g[:, COL_DT + DT_LO:COL_DT + SSD_HEADS], g[:, a:b]], axis=1),
                      g[:, b:COL_DT]])


def _local_step(x, target, small, weights, scatter, B):
    T = x.shape[0]
    bias = _att_bias((T // B) // ATT_B)
    saved = []
    h = x
    for l in range(DEPTH):
        tag = "l%d" % l
        p = {k: v[l] for k, v in small.items()}
        w1 = weights(l, "ffn1", h)
        x1, ffn1, d1 = _ffn_fwd(tag + "f1", h, p["ffn1_norm"][None], w1["g1"], w1["u1"],
                                lambda after, l=l: weights(l, "ffn1d", after)["d1"])
        x2, sv = _mixer_fwd(tag, x1, p, functools.partial(weights, l), bias, B)
        w2 = weights(l, "rest", x2)
        h, ffn2, _ = _ffn_fwd(tag + "f2", x2, p["ffn2_norm"][None], w2["g2"], w2["u2"], w2["d2"])
        saved.append((ffn1, sv, ffn2, dict(g1=w1["g1"], u1=w1["u1"], d1=d1), w2))
    d, lsum = _loss_grad("loss", h, target)
    sgrads = [None] * DEPTH
    for l in reversed(range(DEPTH)):
        tag = "l%db" % l
        p = {k: v[l] for k, v in small.items()}
        ffn1, sv, ffn2, w1, w2 = saved[l]
        sg = {}
        d, sg["ffn2_norm"] = _ffn_bwd(tag + "f2", d, ffn2, p["ffn2_norm"][None], w2["g2"], w2["u2"], w2["d2"],
                                      lambda gg, gu, gd, c, l=l: scatter(l, "ffn2", dict(g2=gg, u2=gu, d2=gd), c))
        d, sgm, gwout, gwin = _mixer_bwd(tag, d, sv, p, bias, B)
        sg.update(sgm)
        d = scatter(l, "mixer", dict(wout=gwout, win=gwin), d)
        d, sg["ffn1_norm"] = _ffn_bwd(tag + "f1", d, ffn1, p["ffn1_norm"][None], w1["g1"], w1["u1"], w1["d1"],
                                      lambda gg, gu, gd, c, l=l: scatter(l, "ffn1", dict(g1=gg, u1=gu, d1=gd), c))
        sgrads[l] = sg
    return lsum, d, sgrads


MESH = pl.DeviceIdType.MESH
ANY = pl.BlockSpec(memory_space=pl.ANY)


def _place():
    return lax.axis_index("x"), lax.axis_index("y"), lax.axis_index("c")


def _other_chips(x, y):
    return [(1 - x, y), (x, 1 - y), (1 - x, 1 - y)]


HBM = pl.BlockSpec(memory_space=pltpu.HBM)
SEM = pl.BlockSpec(memory_space=pltpu.SEMAPHORE)
EFFECT = pltpu.SideEffectType.DATAFLOW_SIDE_EFFECTING


def _hbm(a):
    return pltpu.with_memory_space_constraint(a, pltpu.HBM)


def _exchange(gather, layer, src, land, send, recv, n, act):
    x, y, c = _place()
    for k, (px, py) in enumerate(_other_chips(x, y)):
        for a in range(n):
            if gather:
                s_out, d_out, d_in = src[a].at[layer], land[a].at[2 * x + y], land[a].at[2 * px + py]
            else:
                s_out, d_out, d_in = src[a].at[2 * px + py], land[a].at[k], land[a].at[k]
            act(pltpu.make_async_remote_copy(
                src_ref=s_out, dst_ref=d_out if act is _start else d_in, send_sem=send.at[k * n + a],
                recv_sem=recv.at[k * n + a], device_id=(px, py, c), device_id_type=MESH))


def _start(cp):
    cp.start()


def _finish(cp):
    cp.wait_send()
    cp.wait_recv()


def _exchange_start(name, gather, layer, srcs, carry):
    n = len(srcs)
    lands = [lax.empty(((N_SHARD,) + s.shape[1:]) if gather else ((3,) + s.shape[1:]), s.dtype) for s in srcs]

    def body(*refs):
        _exchange(gather, layer, refs[:n], refs[n:2 * n], refs[2 * n + 1], refs[2 * n + 2], n, _start)

    srcs = [_hbm(a) for a in srcs]
    thru = [_hbm(a) for a in lands + [carry]]
    out = pl.pallas_call(
        body, name=name,
        out_shape=(pltpu.SemaphoreType.DMA((3 * n,)), pltpu.SemaphoreType.DMA((3 * n,)),
                   *[pltpu.HBM(a.shape, a.dtype) for a in thru]),
        in_specs=[HBM] * (2 * n + 1), out_specs=(SEM, SEM, *[HBM] * (n + 1)),
        input_output_aliases={n + i: 2 + i for i in range(n + 1)},
        compiler_params=pltpu.CompilerParams(has_side_effects=EFFECT))(*srcs, *thru)
    return dict(gather=gather, layer=layer, send=out[0], recv=out[1], srcs=srcs, lands=list(out[2:2 + n])), out[-1]


def _exchange_wait(name, ex, after):
    n = len(ex["srcs"])

    def body(*refs):
        _exchange(ex["gather"], ex["layer"], refs[:n], refs[n:2 * n], refs[2 * n], refs[2 * n + 1], n, _finish)

    out = pl.pallas_call(
        body, name=name, out_shape=[pltpu.HBM(a.shape, a.dtype) for a in ex["lands"]],
        in_specs=[HBM] * (2 * n) + [SEM, SEM, ANY], out_specs=[HBM] * n,
        input_output_aliases={n + i: i for i in range(n)},
        compiler_params=pltpu.CompilerParams(has_side_effects=EFFECT))(
            *ex["srcs"], *ex["lands"], ex["send"], ex["recv"], after)
    return list(out)


def _swap_sibling(parts):
    n = len(parts)

    def body(*refs):
        src, dst = refs[:n], refs[n:2 * n]
        send, recv = refs[2 * n:]
        x, y, c = _place()
        cps = [pltpu.make_async_remote_copy(src_ref=src[a], dst_ref=dst[a], send_sem=send.at[a], recv_sem=recv.at[a],
                                            device_id=(x, y, 1 - c), device_id_type=MESH) for a in range(n)]
        for cp in cps:
            cp.start()
        for cp in cps:
            cp.wait_recv()
        for cp in cps:
            cp.wait_send()

    return pl.pallas_call(
        body, out_shape=[jax.ShapeDtypeStruct(p.shape, p.dtype) for p in parts],
        in_specs=[ANY] * n, out_specs=[ANY] * n,
        scratch_shapes=[pltpu.SemaphoreType.DMA((n,)), pltpu.SemaphoreType.DMA((n,))],
        name="swap_sibling")(*parts)


def _allreduce_small(name, v, after):
    R = v.shape[0]

    def body(v_ref, after_ref, o_ref, buf, send, recv):
        x, y, c = _place()
        me = 4 * x + 2 * y + c
        buf[me] = v_ref[...]
        cps = []
        for k in range(1, 8):
            fx, fy, fc = (k >> 2) & 1, (k >> 1) & 1, k & 1
            px = 1 - x if fx else x
            py = 1 - y if fy else y
            pc = 1 - c if fc else c
            cp = pltpu.make_async_remote_copy(src_ref=v_ref, dst_ref=buf.at[me], send_sem=send.at[k - 1],
                                              recv_sem=recv.at[k - 1], device_id=(px, py, pc), device_id_type=MESH)
            cp.start()
            cps.append((cp, 4 * px + 2 * py + pc))
        for k, (cp, peer) in enumerate(cps):
            pltpu.make_async_remote_copy(src_ref=v_ref, dst_ref=buf.at[peer], send_sem=send.at[k], recv_sem=recv.at[k],
                                         device_id=(x, y, c), device_id_type=MESH).wait_recv()
        for cp, _ in cps:
            cp.wait_send()
        acc = buf[0]
        for d in range(1, 8):
            acc = acc + buf[d]
        o_ref[...] = acc

    return pl.pallas_call(
        body, out_shape=jax.ShapeDtypeStruct((R, 128), F32),
        in_specs=[pl.BlockSpec(memory_space=pltpu.VMEM), ANY], out_specs=pl.BlockSpec(memory_space=pltpu.VMEM),
        scratch_shapes=[pltpu.VMEM((8, R, 128), F32), pltpu.SemaphoreType.DMA((7,)), pltpu.SemaphoreType.DMA((7,))],
        name=name)(v, after)


def _row_tile(r):
    for t in (256, 128, 64, 32, 16, 8):
        if r % t == 0:
            return t
    raise ValueError(r)


def _sum4(name, me, parts, got):
    _, R, C = parts.shape
    tr = _row_tile(R)

    def body(me_ref, o_ref, g_ref, s_ref):
        s = o_ref[...].astype(F32)
        for k in range(3):
            s = s + g_ref[k].astype(F32)
        s_ref[...] = s.astype(BF16)

    return pl.pallas_call(
        body, out_shape=jax.ShapeDtypeStruct((R, C), BF16),
        grid_spec=pltpu.PrefetchScalarGridSpec(
            num_scalar_prefetch=1, grid=(R // tr,),
            in_specs=[pl.BlockSpec((None, tr, C), lambda i, me_ref: (me_ref[0], i, 0)),
                      pl.BlockSpec((3, tr, C), lambda i, me_ref: (0, i, 0))],
            out_specs=pl.BlockSpec((tr, C), lambda i, me_ref: (i, 0))),
        name=name, compiler_params=_cp("parallel"))(me, parts, got)


def _adamw(name, w, gparts, m, v):
    R, C = w.shape
    tr = _row_tile(R)
    ng = len(gparts)
    c1 = 1.0 - ADAM_B1 ** ADAM_STEP
    c2 = 1.0 - ADAM_B2 ** ADAM_STEP

    def body(*refs):
        w_ref = refs[0]
        g_refs = refs[1:1 + ng]
        m_ref, v_ref, go_ref, d_ref, mo_ref, vo_ref = refs[1 + ng:]
        g = g_refs[0][...]
        for r in g_refs[1:]:
            g = g + r[...]
        mn = ADAM_B1 * m_ref[...] + (1.0 - ADAM_B1) * g
        vn = ADAM_B2 * v_ref[...] + (1.0 - ADAM_B2) * (g * g)
        go_ref[...] = g
        mo_ref[...] = mn
        vo_ref[...] = vn
        d_ref[...] = -ADAM_LR * ((mn / c1) / (jnp.sqrt(vn / c2) + ADAM_EPS) + ADAM_WD * w_ref[...])

    blk = pl.BlockSpec((tr, C), lambda i: (i, 0))
    osh = jax.ShapeDtypeStruct((R, C), F32)
    return pl.pallas_call(
        body, out_shape=(osh, osh, osh, osh), grid=(R // tr,), in_specs=[blk] * (3 + ng), out_specs=(blk,) * 4,
        name=name, compiler_params=_cp("parallel"))(w, *gparts, m, v)


def _adamw_layers(name, w, sums, m, v):
    _, R, C = w.shape
    tr = _row_tile(R)
    nr = R // tr
    c1 = 1.0 - ADAM_B1 ** ADAM_STEP
    c2 = 1.0 - ADAM_B2 ** ADAM_STEP

    def body(w_ref, a0, b0, a1, b1, m_ref, v_ref, go_ref, d_ref, mo_ref, vo_ref):
        f = lambda r: r[...].astype(F32)
        g = jnp.where(pl.program_id(0) == 0, f(a0) + f(b0), f(a1) + f(b1))
        mn = ADAM_B1 * m_ref[...] + (1.0 - ADAM_B1) * g
        vn = ADAM_B2 * v_ref[...] + (1.0 - ADAM_B2) * (g * g)
        go_ref[...] = g
        mo_ref[...] = mn
        vo_ref[...] = vn
        d_ref[...] = -ADAM_LR * ((mn / c1) / (jnp.sqrt(vn / c2) + ADAM_EPS) + ADAM_WD * w_ref[...])

    blk = pl.BlockSpec((None, tr, C), lambda l, i: (l, i, 0))
    lay0 = pl.BlockSpec((tr, C), lambda l, i: (jnp.where(l == 0, i, nr - 1), 0))
    lay1 = pl.BlockSpec((tr, C), lambda l, i: (jnp.where(l == 1, i, 0), 0))
    oblk = pl.BlockSpec((tr, C), lambda l, i: (l * nr + i, 0))
    osh = jax.ShapeDtypeStruct((DEPTH * R, C), F32)
    res = pl.pallas_call(
        body, out_shape=(osh, osh, osh, osh), grid=(DEPTH, nr),
        in_specs=[blk, lay0, lay0, lay1, lay1, blk, blk], out_specs=(oblk,) * 4,
        name=name, compiler_params=_cp("arbitrary", "arbitrary"))(w, *sums[0], *sums[1], m, v)
    return [r.reshape(w.shape) for r in res]


BIG = [("ffn1_w_gate", "g1"), ("ffn1_w_up", "u1"), ("ffn1_w_down", "d1"), ("w_in", "win"), ("w_out", "wout"),
       ("ffn2_w_gate", "g2"), ("ffn2_w_up", "u2"), ("ffn2_w_down", "d2")]
SMALL = ["ffn1_norm", "mix_norm", "conv_b", "dt_bias", "a_log", "d_skip", "ssd_norm", "q_norm", "k_norm", "ffn2_norm"]
WEIGHTS = ["ffn1_norm", "ffn1_w_gate", "ffn1_w_up", "ffn1_w_down", "mix_norm", "w_in", "conv_w", "conv_b", "dt_bias",
           "a_log", "d_skip", "ssd_norm", "q_norm", "k_norm", "w_out", "ffn2_norm", "ffn2_w_gate", "ffn2_w_up",
           "ffn2_w_down"]
CONV_SH = CONV_DIM // N_SHARD
TRANSPOSED = ("g1", "u1", "g2", "u2")
GATHER_GROUPS = [(0, "ffn1", ["g1", "u1"]), (0, "ffn1d", ["d1"]), (0, "win", ["win", "cw"]),
                 (0, "rest", ["wout", "g2", "u2", "d2"]),
                 (1, "all", ["g1", "u1", "d1", "win", "cw", "wout", "g2", "u2", "d2"])]


def _pad128(v):
    v = v.reshape(-1)
    return jnp.pad(v, (0, (-v.shape[0]) % 128))


def _pack(pieces):
    flat, offs, pos = [], [], 0
    for p in pieces:
        q = _pad128(p.astype(F32))
        offs.append(pos)
        pos += q.shape[0] // 128
        flat.append(q)
    total = -(-pos // 8) * 8
    out = jnp.concatenate(flat + [jnp.zeros(((total - pos) * 128,), F32)]).reshape(total, 128)
    return out, offs


def _unpack(packed, offs, shapes):
    out = []
    for off, shp in zip(offs, shapes):
        n = int(np.prod(shp))
        rows = -(-n // 128)
        out.append(packed[off:off + rows].reshape(-1)[:n].reshape(shp))
    return out


def kernel(x, ffn1_norm, ffn1_w_gate, ffn1_w_up, ffn1_w_down, mix_norm, w_in, conv_w, conv_b, dt_bias, a_log, d_skip, ssd_norm, q_norm, k_norm, w_out, ffn2_norm, ffn2_w_gate, ffn2_w_up, ffn2_w_down, loss_target, m_ffn1_norm, m_ffn1_w_gate, m_ffn1_w_up, m_ffn1_w_down, m_mix_norm, m_w_in, m_conv_w, m_conv_b, m_dt_bias, m_a_log, m_d_skip, m_ssd_norm, m_q_norm, m_k_norm, m_w_out, m_ffn2_norm, m_ffn2_w_gate, m_ffn2_w_up, m_ffn2_w_down, v_ffn1_norm, v_ffn1_w_gate, v_ffn1_w_up, v_ffn1_w_down, v_mix_norm, v_w_in, v_conv_w, v_conv_b, v_dt_bias, v_a_log, v_d_skip, v_ssd_norm, v_q_norm, v_k_norm, v_w_out, v_ffn2_norm, v_ffn2_w_gate, v_ffn2_w_up, v_ffn2_w_down):
    A = dict(locals())
    ix, iy, ic = _place()
    me = 2 * ix + iy
    B, S, _ = x.shape
    T = B * S

    view = lambda a, key: jnp.swapaxes(a, 1, 2) if key in TRANSPOSED else a
    own = {key: view(A[name], key).astype(BF16) for name, key in BIG}
    own["cw"] = conv_w
    exs, first_norm = [], ffn1_norm
    for gi, (l, _, keys) in enumerate(GATHER_GROUPS):
        ex, first_norm = _exchange_start("gather_start%d" % gi, True, l, [own[key] for key in keys], first_norm)
        exs.append(ex)
    landed = {}

    def weights(l, group, after):
        gi = [i for i, (gl, gname, _) in enumerate(GATHER_GROUPS) if gl == l and gname in (group, "all")][0]
        if gi not in landed:
            lands = _exchange_wait("gather_wait%d" % gi, exs[gi], after)
            landed[gi] = {}
            for key, land in zip(GATHER_GROUPS[gi][2], lands):
                full = lax.dynamic_update_slice(land, own[key][l][None], (me, 0, 0))
                if key == "win":
                    full = _win_from_shards(full)
                if key == "cw":
                    full = jnp.transpose(full, (1, 0, 2)).reshape(CONV_K, CONV_DIM)
                landed[gi][key] = full
        return landed[gi]

    pending = []

    def scatter(l, group, grads, carry):
        keys = sorted(grads)
        arrs = [grads[key] for key in keys]
        if "win" in grads:
            arrs[keys.index("win")] = _win_to_shards(grads["win"])
        ex, carry = _exchange_start("scatter_start_l%d_%s" % (l, group), False, None, arrs, carry)
        pending.append((l, keys, ex))
        return carry

    small = {name: A[name] for name in SMALL}
    small["ffn1_norm"] = first_norm
    lsum, dx, sgrads = _local_step(x.reshape(T, D_MODEL), loss_target.reshape(T, D_MODEL), small, weights, scatter, B)

    names = SMALL + ["conv_w"]
    shapes = [A[n].shape for n in SMALL] + [(DEPTH, CONV_K, CONV_DIM), ()]
    pieces = [jnp.stack([sgrads[l][n].reshape(shp[1:]) for l in range(DEPTH)]) for n, shp in zip(names, shapes)]
    pieces.append(0.5 / D_MODEL * jnp.sum(lsum))
    packed, offs = _pack(pieces)

    sums, after = {}, dx
    me1 = jnp.reshape(me, (1,)).astype(jnp.int32)
    for idx, (l, keys, ex) in enumerate(pending):
        lands = _exchange_wait("scatter_wait%d" % idx, ex, after)
        for key, g, got in zip(keys, ex["srcs"], lands):
            sums[key, l] = after = _sum4("sum_%s_l%d" % (key, l), me1, g, got)

    red = _unpack(_allreduce_small("allreduce_small", packed, after), offs, shapes)
    loss = red[-1]
    sg = dict(zip(names, red[:-1]))
    order = [(key, l) for _, key in BIG for l in range(DEPTH)]
    theirs = dict(zip(order, _swap_sibling([sums[k] for k in order])))

    out = {}
    for name, key in BIG:
        res = _adamw_layers("adamw_" + key, view(A[name], key), [(sums[key, l], theirs[key, l]) for l in range(DEPTH)],
                            view(A["m_" + name], key), view(A["v_" + name], key))
        out[name] = [view(r, key) for r in res]

    wp, offs = _pack([A[n] for n in SMALL])
    gp, _ = _pack([sg[n] for n in SMALL])
    mp, _ = _pack([A["m_" + n] for n in SMALL])
    vp, _ = _pack([A["v_" + n] for n in SMALL])
    res = _adamw("adamw_small", wp, [gp], mp, vp)
    shapes = [A[n].shape for n in SMALL]
    res = [_unpack(r, offs, shapes) for r in res]
    for i, n in enumerate(SMALL):
        out[n] = [res[q][i] for q in range(4)]
    gcw = lax.dynamic_slice_in_dim(sg["conv_w"], me * CONV_SH, CONV_SH, axis=2)
    flat = lambda a: a.reshape(DEPTH * CONV_K, CONV_SH)
    res = _adamw("adamw_conv_w", flat(conv_w), [flat(gcw)], flat(m_conv_w), flat(v_conv_w))
    out["conv_w"] = [r.reshape(conv_w.shape) for r in res]

    outs = [loss, dx.reshape(B, S, D_MODEL)]
    for q in range(4):
        outs += [out[n][q] for n in WEIGHTS]
    return tuple(outs)
```

```python
import functools
import math

import numpy as np
import jax
import jax.numpy as jnp
from jax import lax
from jax.experimental import pallas as pl
from jax.experimental.pallas import tpu as pltpu

F32 = jnp.float32
BF16 = jnp.bfloat16

D_MODEL = 1024
DEPTH = 2
N_SHARD = 4
D_FF = 2816
FF_SH = D_FF // N_SHARD
SSD_HEADS = 16
HEAD_DIM = 64
SSD_GROUPS = 4
GROUP_W = 256
SSD_STATE = 128
CONV_K = 4
CONV_DIM = 2048
ATT_HEADS = 16
MIX_W = 2048
MIX_SH = MIX_W // N_SHARD
IN_PROJ = 6160
IN_SH = IN_PROJ // N_SHARD
IN_PAD = 6272
PROJ_TN = 896
COL_Z, COL_XBC, COL_Q, COL_K, COL_V, COL_DT = 0, 1024, 3072, 4096, 5120, 6144
EPS = 1e-6
NEG = -1e30
SSD_L = 256
ATT_B = 256
ROW_T = 512
HALF_T = ROW_T // 2
TK_W = 2048
CONV_CT = 256
CONV_R = 256
PAD_R = 8

ADAM_LR, ADAM_B1, ADAM_B2, ADAM_EPS, ADAM_WD, ADAM_STEP = 0.001, 0.9, 0.999, 1e-08, 0.01, 10

NN = (((1,), (0,)), ((), ()))
NT = (((1,), (1,)), ((), ()))
TN = (((0,), (0,)), ((), ()))

VMEM_LIMIT = 56 * 1024 * 1024


def _cp(*sem):
    return pltpu.CompilerParams(dimension_semantics=sem, vmem_limit_bytes=VMEM_LIMIT)


def _dot(a, b, dims):
    return lax.dot_general(a, b, dims, preferred_element_type=F32)


def _sigmoid(x):
    return 0.5 * jnp.tanh(0.5 * x) + 0.5


def _softplus(x):
    return jnp.maximum(x, 0.0) + jnp.log(1.0 + jnp.exp(-jnp.abs(x)))


def _mm(name, pairs, out_shape, out_spec, grid, dims, acc_shape, res=None, scale=1.0, post=None, post_in=()):
    nk = grid[2]
    npair = len(pairs)
    npost = len(post_in)

    def body(*refs):
        ab = refs[:2 * npair]
        pos = 2 * npair
        res_ref = None
        if res is not None:
            res_ref = refs[pos]
            pos += 1
        pin = refs[pos:pos + npost]
        pos += npost
        out_ref = refs[pos]
        pos += 1
        if post is not None:
            out2_ref = refs[pos]
            pos += 1
        s = None
        for p in range(npair):
            d = _dot(ab[2 * p][...].astype(BF16), ab[2 * p + 1][...].astype(BF16), dims)
            s = d if s is None else s + d

        def finish(r):
            if scale != 1.0:
                r = r * scale
            if res_ref is not None:
                r = r + res_ref[...]
            if post == "rmsb":
                @pl.when(pl.program_id(0) == 0)
                def _():
                    out2_ref[...] = jnp.zeros_like(out2_ref)

                xv = pin[0][...]
                rr = lax.rsqrt(jnp.mean(xv * xv, axis=-1, keepdims=True) + EPS)
                xhat = xv * rr
                dxhat = r * pin[1][...]
                out_ref[...] = pin[2][...] + rr * (dxhat - xhat * jnp.mean(dxhat * xhat, axis=-1, keepdims=True))
                out2_ref[...] += jnp.sum(r * xhat, axis=0, keepdims=True)
                return
            out_ref[...] = r.astype(out_ref.dtype)
            if post == "norm":
                rr = lax.rsqrt(jnp.mean(r * r, axis=-1, keepdims=True) + EPS)
                out2_ref[...] = (r * rr * pin[0][...]).astype(BF16)

        if nk == 1:
            finish(s)
            return
        acc = refs[pos]
        k = pl.program_id(2)

        @pl.when(k == 0)
        def _():
            acc[...] = s

        @pl.when(k > 0)
        def _():
            acc[...] += s

        @pl.when(k == nk - 1)
        def _():
            finish(acc[...])

    args, specs = [], []
    for a, a_spec, b, b_spec in pairs:
        args += [a, b]
        specs += [a_spec, b_spec]
    for arr, spec in ([res] if res is not None else []) + list(post_in):
        args.append(arr)
        specs.append(spec)
    sems = ("arbitrary",) * 3 if post == "rmsb" else ("parallel", "parallel", "arbitrary")
    return pl.pallas_call(
        body, out_shape=out_shape, grid=grid, in_specs=specs, out_specs=out_spec,
        scratch_shapes=[] if nk == 1 else [pltpu.VMEM(acc_shape, F32)], name=name,
        compiler_params=_cp(*sems))(*args)


def _rms_fwd(name, x, w):
    T = x.shape[0]

    def body(x_ref, w_ref, o_ref):
        xv = x_ref[...]
        r = lax.rsqrt(jnp.mean(xv * xv, axis=-1, keepdims=True) + EPS)
        o_ref[...] = (xv * r * w_ref[...]).astype(BF16)

    return pl.pallas_call(
        body, out_shape=jax.ShapeDtypeStruct((T, D_MODEL), BF16), grid=(T // ROW_T,),
        in_specs=[pl.BlockSpec((ROW_T, D_MODEL), lambda i: (i, 0)), pl.BlockSpec((1, D_MODEL), lambda i: (0, 0))],
        out_specs=pl.BlockSpec((ROW_T, D_MODEL), lambda i: (i, 0)), name=name, compiler_params=_cp("parallel"))(x, w)


def _loss_grad(name, y, t):
    T = y.shape[0]

    def body(y_ref, t_ref, dy_ref, l_ref):
        @pl.when(pl.program_id(0) == 0)
        def _():
            l_ref[...] = jnp.zeros_like(l_ref)

        e = y_ref[...] - t_ref[...]
        dy_ref[...] = e * (1.0 / D_MODEL)
        l_ref[...] += jnp.sum(e * e, axis=0, keepdims=True)

    row = pl.BlockSpec((ROW_T, D_MODEL), lambda i: (i, 0))
    vec = pl.BlockSpec((1, D_MODEL), lambda i: (0, 0))
    return pl.pallas_call(
        body, out_shape=(jax.ShapeDtypeStruct((T, D_MODEL), F32), jax.ShapeDtypeStruct((1, D_MODEL), F32)),
        grid=(T // ROW_T,), in_specs=[row, row], out_specs=(row, vec), name=name,
        compiler_params=_cp("arbitrary"))(y, t)


def _ffn_gate_up(name, h, wg, wu):
    T = h.shape[0]

    def body(h_ref, wg_ref, wu_ref, dgf_ref, duf_ref, a_ref):
        for r in range(0, ROW_T, HALF_T):
            rows = slice(r, r + HALF_T)
            hv = h_ref[rows, :]
            g = _dot(hv, wg_ref[...], NT)
            u = _dot(hv, wu_ref[...], NT)
            sg = _sigmoid(g)
            silu = g * sg
            dgf_ref[rows, :] = (u * (sg * (1.0 + g * (1.0 - sg)))).astype(BF16)
            duf_ref[rows, :] = silu.astype(BF16)
            a_ref[rows, :] = (silu * u).astype(BF16)

    wspec = pl.BlockSpec((None, FF_SH, D_MODEL), lambda j, i: (j, 0, 0))
    ospec = pl.BlockSpec((None, ROW_T, FF_SH), lambda j, i: (j, i, 0))
    osh = jax.ShapeDtypeStruct((N_SHARD, T, FF_SH), BF16)
    return pl.pallas_call(
        body, out_shape=(osh, osh, osh), grid=(N_SHARD, T // ROW_T),
        in_specs=[pl.BlockSpec((ROW_T, D_MODEL), lambda j, i: (i, 0)), wspec, wspec],
        out_specs=(ospec, ospec, ospec), name=name, compiler_params=_cp("parallel", "parallel"))(h, wg, wu)


def _ffn_dact(name, dx, wd, g, u):
    T = dx.shape[0]

    def body(dx_ref, wd_ref, g_ref, u_ref, dg_ref, du_ref):
        for r in range(0, ROW_T, HALF_T):
            rows = slice(r, r + HALF_T)
            da = 0.5 * _dot(dx_ref[rows, :].astype(BF16), wd_ref[...], NT)
            dg_ref[rows, :] = (da * g_ref[rows, :].astype(F32)).astype(BF16)
            du_ref[rows, :] = (da * u_ref[rows, :].astype(F32)).astype(BF16)

    aspec = pl.BlockSpec((None, ROW_T, FF_SH), lambda j, i: (j, i, 0))
    osh = jax.ShapeDtypeStruct((N_SHARD, T, FF_SH), BF16)
    return pl.pallas_call(
        body, out_shape=(osh, osh), grid=(N_SHARD, T // ROW_T),
        in_specs=[pl.BlockSpec((ROW_T, D_MODEL), lambda j, i: (i, 0)),
                  pl.BlockSpec((None, FF_SH, D_MODEL), lambda j, i: (j, 0, 0)), aspec, aspec],
        out_specs=(aspec, aspec), name=name, compiler_params=_cp("parallel", "parallel"))(dx, wd, g, u)


def _row3():
    return pl.BlockSpec((ROW_T, D_MODEL), lambda i, n, k: (i, 0))


def _vec3():
    return pl.BlockSpec((1, D_MODEL), lambda i, n, k: (0, 0))


def _with_norm(T, next_nw):
    if next_nw is None:
        return dict(out_shape=jax.ShapeDtypeStruct((T, D_MODEL), F32), out_spec=_row3())
    return dict(out_shape=(jax.ShapeDtypeStruct((T, D_MODEL), F32), jax.ShapeDtypeStruct((T, D_MODEL), BF16)),
                out_spec=(_row3(), _row3()), post="norm", post_in=[(next_nw, _vec3())])


def _ffn_fwd(tag, x, h, wg, wu, wd, next_nw):
    T = x.shape[0]
    g, u, a = _ffn_gate_up(tag + "_gu", h, wg, wu)
    if callable(wd):
        wd = wd(a)
    nt = T // ROW_T
    o = _with_norm(T, next_nw)
    xo = _mm(tag + "_down",
             [(a, pl.BlockSpec((None, ROW_T, FF_SH), lambda i, n, k, j=j: (j, i, 0)),
               wd, pl.BlockSpec((None, FF_SH, D_MODEL), lambda i, n, k, j=j: (j, 0, 0))) for j in range(N_SHARD)],
             o.pop("out_shape"), o.pop("out_spec"), (nt, 1, 1), NN, (ROW_T, D_MODEL),
             res=(x, _row3()), scale=0.5, **o)
    return xo, (x, h, g, u, a), wd


def _ffn_bwd(tag, dxo, saved, nw, wg, wu, wd, emit):
    x, h, g, u, a = saved
    T = x.shape[0]
    nt = T // ROW_T
    tkw = min(TK_W, T)
    nw_t = T // tkw
    dg, du = _ffn_dact(tag + "_dact", dxo, wd, g, u)
    actw = lambda f: pl.BlockSpec((None, tkw, FF_SH), f)
    gd = _mm(tag + "_dwd",
             [(a, actw(lambda m, n, k: (m, k, 0)), dxo, pl.BlockSpec((tkw, D_MODEL), lambda m, n, k: (k, 0)))],
             jax.ShapeDtypeStruct((N_SHARD, FF_SH, D_MODEL), BF16),
             pl.BlockSpec((None, FF_SH, D_MODEL), lambda m, n, k: (m, 0, 0)),
             (N_SHARD, 1, nw_t), TN, (FF_SH, D_MODEL), scale=0.5)
    hspec = pl.BlockSpec((tkw, D_MODEL), lambda j, n, k: (k, 0))
    gsh = jax.ShapeDtypeStruct((N_SHARD, FF_SH, D_MODEL), BF16)
    gspec = pl.BlockSpec((None, FF_SH, D_MODEL), lambda j, n, k: (j, 0, 0))
    gg = _mm(tag + "_dwg", [(dg, actw(lambda j, n, k: (j, k, 0)), h, hspec)], gsh, gspec,
             (N_SHARD, 1, nw_t), TN, (FF_SH, D_MODEL))
    gu = _mm(tag + "_dwu", [(du, actw(lambda j, n, k: (j, k, 0)), h, hspec)], gsh, gspec,
             (N_SHARD, 1, nw_t), TN, (FF_SH, D_MODEL))
    dg = emit(gg, gu, gd, dg)
    act = lambda j: pl.BlockSpec((None, ROW_T, FF_SH), lambda i, n, k: (j, i, 0))
    wsp = lambda j: pl.BlockSpec((None, FF_SH, D_MODEL), lambda i, n, k: (j, 0, 0))
    return _mm(tag + "_dh",
               [(dd, act(j), w, wsp(j)) for j in range(N_SHARD) for dd, w in ((dg, wg), (du, wu))],
               (jax.ShapeDtypeStruct((T, D_MODEL), F32), jax.ShapeDtypeStruct((1, D_MODEL), F32)), (_row3(), _vec3()),
               (nt, 1, 1), NN, (ROW_T, D_MODEL), post="rmsb", post_in=[(x, _row3()), (nw, _vec3()), (dxo, _row3())])


def _seq_rows(ref, start, size, S):
    lo, hi = max(start, 0), min(start + size, S)
    parts = [ref[pl.ds(lo, hi - lo), :]]
    if lo > start:
        parts.insert(0, jnp.zeros((lo - start, ref.shape[1]), F32))
    if start + size > hi:
        parts.append(jnp.zeros((start + size - hi, ref.shape[1]), F32))
    return parts[0] if len(parts) == 1 else jnp.concatenate(parts, axis=0)


XBC_CB = COL_XBC // CONV_CT


def _conv_fwd(name, proj, w, b, B):
    T = proj.shape[0]
    S = T // B
    C = CONV_DIM

    def body(x_ref, w_ref, b_ref, o_ref):
        wv = w_ref[...]
        for c in range(S // CONV_R):
            r0 = c * CONV_R
            ch = _seq_rows(x_ref, r0 - PAD_R, CONV_R + PAD_R, S)
            pre = ch[PAD_R:] * wv[3:4] + b_ref[...]
            for s in range(1, CONV_K):
                pre = pre + pltpu.roll(ch, s, axis=0)[PAD_R:] * wv[3 - s:4 - s]
            o_ref[pl.ds(r0, CONV_R), :] = pre * _sigmoid(pre)

    return pl.pallas_call(
        body, out_shape=jax.ShapeDtypeStruct((T, C), F32), grid=(B, C // CONV_CT),
        in_specs=[pl.BlockSpec((S, CONV_CT), lambda bi, ci: (bi, XBC_CB + ci)),
                  pl.BlockSpec((CONV_K, CONV_CT), lambda bi, ci: (0, ci)),
                  pl.BlockSpec((1, CONV_CT), lambda bi, ci: (0, ci))],
        out_specs=pl.BlockSpec((S, CONV_CT), lambda bi, ci: (bi, ci)), name=name,
        compiler_params=_cp("parallel", "parallel"))(proj, w, b)


def _conv_bwd(name, proj, dxs, dB, dC, w, b, dproj, B):
    T = proj.shape[0]
    S = T // B
    C = CONV_DIM
    RW = CONV_R + PAD_R
    nx, nb = dxs.shape[1] // CONV_CT, dB.shape[1] // CONV_CT

    def body(x_ref, dx_in, db_in, dc_in, w_ref, b_ref, buf_ref, dx_ref, dw_ref, db_ref):
        @pl.when(pl.program_id(1) == 0)
        def _():
            dw_ref[...] = jnp.zeros_like(dw_ref)
            db_ref[...] = jnp.zeros_like(db_ref)

        ci = pl.program_id(0)
        wv = w_ref[...]
        dw = [jnp.zeros((1, CONV_CT), F32) for _ in range(CONV_K)]
        db = jnp.zeros((1, CONV_CT), F32)
        for c in range(S // CONV_R):
            r0 = c * CONV_R
            ch = _seq_rows(x_ref, r0 - PAD_R, RW + PAD_R, S)
            xs = [ch[PAD_R:]] + [pltpu.roll(ch, s, axis=0)[PAD_R:] for s in range(1, CONV_K)]
            pre = b_ref[...] + xs[0] * wv[3:4]
            for s in range(1, CONV_K):
                pre = pre + xs[s] * wv[3 - s:4 - s]
            sg = _sigmoid(pre)
            dout = jnp.where(ci < nx, _seq_rows(dx_in, r0, RW, S),
                             jnp.where(ci < nx + nb, _seq_rows(db_in, r0, RW, S), _seq_rows(dc_in, r0, RW, S)))
            dpre = dout * (sg * (1.0 + pre * (1.0 - sg)))
            dx = dpre[:CONV_R] * wv[3:4]
            for s in range(1, CONV_K):
                dx = dx + pltpu.roll(dpre, RW - s, axis=0)[:CONV_R] * wv[3 - s:4 - s]
            dx_ref[pl.ds(r0, CONV_R), :] = dx.astype(BF16)
            dcur = dpre[:CONV_R]
            db = db + jnp.sum(dcur, axis=0, keepdims=True)
            for s in range(CONV_K):
                dw[3 - s] = dw[3 - s] + jnp.sum(dcur * xs[s][:CONV_R], axis=0, keepdims=True)
        db_ref[...] += db
        for k in range(CONV_K):
            dw_ref[k:k + 1, :] += dw[k]

    seq = lambda f: pl.BlockSpec((S, CONV_CT), f)
    return pl.pallas_call(
        body,
        out_shape=(jax.ShapeDtypeStruct(dproj.shape, dproj.dtype), jax.ShapeDtypeStruct((CONV_K, C), F32),
                   jax.ShapeDtypeStruct((1, C), F32)),
        grid=(C // CONV_CT, B),
        in_specs=[seq(lambda ci, bi: (bi, XBC_CB + ci)),
                  seq(lambda ci, bi: (bi, jnp.minimum(ci, nx - 1))),
                  seq(lambda ci, bi: (bi, jnp.clip(ci - nx, 0, nb - 1))),
                  seq(lambda ci, bi: (bi, jnp.clip(ci - nx - nb, 0, nb - 1))),
                  pl.BlockSpec((CONV_K, CONV_CT), lambda ci, bi: (0, ci)),
                  pl.BlockSpec((1, CONV_CT), lambda ci, bi: (0, ci)), ANY],
        out_specs=(seq(lambda ci, bi: (bi, XBC_CB + ci)),
                   pl.BlockSpec((CONV_K, CONV_CT), lambda ci, bi: (0, ci)),
                   pl.BlockSpec((1, CONV_CT), lambda ci, bi: (0, ci))),
        input_output_aliases={6: 0},
        name=name, compiler_params=_cp("parallel", "arbitrary"))(proj, dxs, dB, dC, w, b, dproj)


def _tri_sum(tri, x, dims, tri_first, terms=3):
    out, rest = None, x
    for t in range(terms):
        part = rest.astype(BF16)
        if t + 1 < terms:
            rest = rest - part.astype(F32)
        d = _dot(tri, part, dims) if tri_first else _dot(part, tri, dims)
        out = d if out is None else out + d
    return out


def _total(x):
    return jnp.sum(jnp.sum(x, axis=0, keepdims=True), axis=-1, keepdims=True)


def _ssd_common(dtc_ref, dtr_ref, pcol_ref, prow_ref, b_ref, c_ref):
    L = SSD_L
    bias_c, alog_c = pcol_ref[0:1, :], pcol_ref[1:2, :]
    a_c = -jnp.exp(alog_c)
    dt_c = _softplus(dtc_ref[...] + bias_c)
    row = lax.broadcasted_iota(jnp.int32, (L, L), 0)
    col = lax.broadcasted_iota(jnp.int32, (L, L), 1)
    causal = row >= col
    tri = causal.astype(BF16)
    cum_c = _tri_sum(tri, dt_c * a_c, NN, True)
    a_r = -jnp.exp(prow_ref[:, 1:2])
    dt_r = _softplus(dtr_ref[...] + prow_ref[:, 0:1])
    cum_r = _tri_sum(tri, dt_r * a_r, NT, False)
    bb = b_ref[...].astype(BF16)
    cb = c_ref[...].astype(BF16)
    G = _dot(cb, bb, NT)
    return a_c, dt_c, causal, tri, cum_c, cum_r, bb, cb, G


def _ssd_fwd(name, xc, proj, dtc, dtr, pcol, prow, nw, B):
    T = xc.shape[0]
    S = T // B
    nb = S // SSD_L
    L = SSD_L

    def body(xs_ref, b_ref, c_ref, z_ref, dtc_ref, dtr_ref, pcol_ref, prow_ref, nw_ref, y_ref, yn_ref, hs_ref, H, yo_s):
        @pl.when(pl.program_id(2) == 0)
        def _():
            H[...] = jnp.zeros_like(H)

        a_c, dt_c, causal, tri, cum_c, cum_r, bb, cb, G = _ssd_common(dtc_ref, dtr_ref, pcol_ref, prow_ref, b_ref, c_ref)
        dsk = pcol_ref[2:3, :]
        clast = cum_c[L - 1:L, :]
        bf = b_ref[...]
        for h in range(4):
            hs_ref[h] = H[h]
            yo_s[h] = _dot(cb, H[h].astype(BF16), NN)
        for h in range(4):
            sl = slice(HEAD_DIM * h, HEAD_DIM * (h + 1))
            cc = cum_c[:, h:h + 1]
            lm = jnp.exp(jnp.where(causal, cc - cum_r[h:h + 1, :], NEG))
            M = (G * lm).astype(BF16)
            xh = xs_ref[:, sl]
            Xb = (xh * dt_c[:, h:h + 1]).astype(BF16)
            Hh = H[h]
            y = _dot(M, Xb, NN) + jnp.exp(cc) * yo_s[h]
            y_ref[:, sl] = y + dsk[:, h:h + 1] * xh
            cl = clast[:, h:h + 1]
            Bw = (bf * jnp.exp(cl - cc)).astype(BF16)
            H[h] = jnp.exp(cl) * Hh + _dot(Bw, Xb, TN)
        zv = z_ref[...]
        y2 = y_ref[...] * (zv * _sigmoid(zv))
        r = lax.rsqrt(jnp.mean(y2 * y2, axis=-1, keepdims=True) + EPS)
        yn_ref[...] = (y2 * r * nw_ref[...]).astype(BF16)

    rowi = lambda b, g, i: b * nb + i
    grp = pl.BlockSpec((L, GROUP_W), lambda b, g, i: (rowi(b, g, i), g))
    return pl.pallas_call(
        body,
        out_shape=(jax.ShapeDtypeStruct((T, 1024), F32), jax.ShapeDtypeStruct((T, 1024), BF16),
                   jax.ShapeDtypeStruct((B, SSD_GROUPS, nb, 4, SSD_STATE, HEAD_DIM), F32)),
        grid=(B, SSD_GROUPS, nb),
        in_specs=[grp,
                  pl.BlockSpec((L, SSD_STATE), lambda b, g, i: (rowi(b, g, i), 8 + g)),
                  pl.BlockSpec((L, SSD_STATE), lambda b, g, i: (rowi(b, g, i), 12 + g)),
                  grp,
                  pl.BlockSpec((None, L, 4), lambda b, g, i: (g, rowi(b, g, i), 0)),
                  pl.BlockSpec((None, 4, L), lambda b, g, i: (g, 0, rowi(b, g, i))),
                  pl.BlockSpec((None, 3, 4), lambda b, g, i: (g, 0, 0)),
                  pl.BlockSpec((None, 4, 3), lambda b, g, i: (g, 0, 0)),
                  pl.BlockSpec((1, GROUP_W), lambda b, g, i: (0, g))],
        out_specs=(grp, grp,
                   pl.BlockSpec((None, None, None, 4, SSD_STATE, HEAD_DIM), lambda b, g, i: (b, g, i, 0, 0, 0))),
        scratch_shapes=[pltpu.VMEM((4, SSD_STATE, HEAD_DIM), F32), pltpu.VMEM((4, L, HEAD_DIM), F32)], name=name,
        compiler_params=_cp("parallel", "parallel", "arbitrary"))(xc, xc, xc, proj, dtc, dtr, pcol, prow, nw)


def _ssd_bwd(name, dyn, Y, xc, proj, dtc, dtr, pcol, prow, nw, hs, dproj, B):
    T = xc.shape[0]
    S = T // B
    nb = S // SSD_L
    L = SSD_L

    def body(dyn_ref, y_ref, xs_ref, b_ref, c_ref, z_ref, dtc_ref, dtr_ref, pcol_ref, prow_ref, nw_ref, hs_ref, buf_ref,
             dxs_ref, db_ref, dc_ref, dz_ref, ddt_ref, dpar_ref, dnw_ref, dH, dm_s, dxo_s, ea_s, ex_s):
        @pl.when(pl.program_id(2) == 0)
        def _():
            dH[...] = jnp.zeros_like(dH)
            dpar_ref[...] = jnp.zeros_like(dpar_ref)
            dnw_ref[...] = jnp.zeros_like(dnw_ref)

        a_c, dt_c, causal, tri, cum_c, cum_r, bb, cb, G = _ssd_common(dtc_ref, dtr_ref, pcol_ref, prow_ref, b_ref, c_ref)
        dsk = pcol_ref[2:3, :]
        clast = cum_c[L - 1:L, :]
        bf = b_ref[...]
        cf = c_ref[...]
        Yv = y_ref[...]
        zv = z_ref[...]
        sz = _sigmoid(zv)
        silu = zv * sz
        y2 = Yv * silu
        r = lax.rsqrt(jnp.mean(y2 * y2, axis=-1, keepdims=True) + EPS)
        yhat = y2 * r
        dyv = dyn_ref[...]
        dnw_ref[...] += jnp.sum(dyv * yhat, axis=0, keepdims=True)
        dyhat = dyv * nw_ref[...]
        dy2 = r * (dyhat - yhat * jnp.mean(dyhat * yhat, axis=-1, keepdims=True))
        dY = dy2 * silu
        dz_ref[...] = (dy2 * Yv * (sz * (1.0 + zv * (1.0 - sz)))).astype(BF16)

        lane4 = lax.broadcasted_iota(jnp.int32, (1, 4), 1)
        dG = jnp.zeros((L, L), F32)
        dBs = jnp.zeros((L, SSD_STATE), F32)
        dCs = jnp.zeros((L, SSD_STATE), F32)
        ddsk = jnp.zeros((1, 4), F32)
        dcl = jnp.zeros((1, 4), F32)
        for h in range(4):
            sl = slice(HEAD_DIM * h, HEAD_DIM * (h + 1))
            xb = (xs_ref[:, sl] * dt_c[:, h:h + 1]).astype(BF16)
            dm_s[h] = _dot(dY[:, sl].astype(BF16), xb, NT)
            dxo_s[h] = _dot(bb, dH[h].astype(BF16), NN)
        for h in range(4):
            sl = slice(HEAD_DIM * h, HEAD_DIM * (h + 1))
            onehot = (lane4 == h).astype(F32)
            cc = cum_c[:, h:h + 1]
            cl = clast[:, h:h + 1]
            lm = jnp.exp(jnp.where(causal, cc - cum_r[h:h + 1, :], NEG))
            M = (G * lm).astype(BF16)
            xh = xs_ref[:, sl]
            dth = dt_c[:, h:h + 1]
            X = xh * dth
            Xb = X.astype(BF16)
            dYh = dY[:, sl]
            dYb = dYh.astype(BF16)
            Hb = hs_ref[h].astype(BF16)
            dHh = dH[h]
            dHb = dHh.astype(BF16)
            alpha = jnp.exp(cc)
            beta = jnp.exp(cl - cc)
            dXoff = beta * dxo_s[h]
            dX = _dot(M, dYb, TN) + dXoff
            dG = dG + dm_s[h] * lm
            dCs = dCs + _dot((alpha * dYh).astype(BF16), Hb, NT)
            dBs = dBs + _dot((beta * X).astype(BF16), dHb, NT)
            ypre = Yv[:, sl] - dsk[:, h:h + 1] * xh
            ea_s[:, sl] = dYb.astype(F32) * ypre - Xb.astype(F32) * dX
            ex_s[:, sl] = dX * xh
            dcl_h = (_total(dHh * (jnp.exp(cl) * hs_ref[h])) + _total(Xb.astype(F32) * dXoff))
            dcl = dcl + dcl_h * onehot
            ddsk = ddsk + _total(dYh * xh) * onehot
            dxs_ref[:, sl] = dsk[:, h:h + 1] * dYh + dX * dth
            dH[h] = jnp.exp(cl) * dHh + _dot((alpha * cf).astype(BF16), dYb, TN)
        dGb = dG.astype(BF16)
        dc_ref[...] = _dot(dGb, bb, NN) + dCs
        db_ref[...] = _dot(dGb, cb, TN) + dBs
        feat = lax.broadcasted_iota(jnp.int32, (GROUP_W, 4), 0)
        head = lax.broadcasted_iota(jnp.int32, (GROUP_W, 4), 1) * HEAD_DIM
        sel = ((feat >= head) & (feat < head + HEAD_DIM)).astype(BF16)
        dA = _tri_sum(sel, ea_s[...], NN, False)
        ddtx = _tri_sum(sel, ex_s[...], NN, False)
        last = lax.broadcasted_iota(jnp.int32, (L, 1), 0) == L - 1
        dA = dA + jnp.where(last, dcl, 0.0)
        dadt = _tri_sum(tri, dA, TN, True)
        ddt = dadt * a_c + ddtx
        d_a = jnp.sum(dadt * dt_c, axis=0, keepdims=True)
        ddraw = ddt * _sigmoid(dtc_ref[...] + pcol_ref[0:1, :])
        ddt_ref[...] = ddraw
        dpar_ref[0:1, :] += jnp.sum(ddraw, axis=0, keepdims=True)
        dpar_ref[1:2, :] += d_a * a_c
        dpar_ref[2:3, :] += ddsk

    rowi = lambda b, g, i: b * nb + (nb - 1 - i)
    grp = pl.BlockSpec((L, GROUP_W), lambda b, g, i: (rowi(b, g, i), g))
    st = pl.BlockSpec((L, SSD_STATE), lambda b, g, i: (rowi(b, g, i), g))
    f = jax.ShapeDtypeStruct
    return pl.pallas_call(
        body,
        out_shape=(f((T, 1024), F32), f((T, 512), F32), f((T, 512), F32), f(dproj.shape, dproj.dtype),
                   f((SSD_GROUPS, T, 4), F32), f((B, SSD_GROUPS, 3, 4), F32), f((B, 1, 1024), F32)),
        grid=(B, SSD_GROUPS, nb),
        in_specs=[grp, grp, grp,
                  pl.BlockSpec((L, SSD_STATE), lambda b, g, i: (rowi(b, g, i), 8 + g)),
                  pl.BlockSpec((L, SSD_STATE), lambda b, g, i: (rowi(b, g, i), 12 + g)),
                  grp,
                  pl.BlockSpec((None, L, 4), lambda b, g, i: (g, rowi(b, g, i), 0)),
                  pl.BlockSpec((None, 4, L), lambda b, g, i: (g, 0, rowi(b, g, i))),
                  pl.BlockSpec((None, 3, 4), lambda b, g, i: (g, 0, 0)),
                  pl.BlockSpec((None, 4, 3), lambda b, g, i: (g, 0, 0)),
                  pl.BlockSpec((1, GROUP_W), lambda b, g, i: (0, g)),
                  pl.BlockSpec((None, None, None, 4, SSD_STATE, HEAD_DIM), lambda b, g, i: (b, g, nb - 1 - i, 0, 0, 0)),
                  ANY],
        out_specs=(grp, st, st, grp,
                   pl.BlockSpec((None, L, 4), lambda b, g, i: (g, rowi(b, g, i), 0)),
                   pl.BlockSpec((None, None, 3, 4), lambda b, g, i: (b, g, 0, 0)),
                   pl.BlockSpec((None, 1, GROUP_W), lambda b, g, i: (b, 0, g))),
        input_output_aliases={12: 3},
        scratch_shapes=[pltpu.VMEM((4, SSD_STATE, HEAD_DIM), F32), pltpu.VMEM((4, L, L), F32),
                        pltpu.VMEM((4, L, HEAD_DIM), F32), pltpu.VMEM((L, GROUP_W), F32),
                        pltpu.VMEM((L, GROUP_W), F32)], name=name,
        compiler_params=_cp("parallel", "parallel", "arbitrary"))(
            dyn, Y, xc, xc, xc, proj, dtc, dtr, pcol, prow, nw, hs, dproj)


def _head_sel():
    sel = (np.arange(1024)[:, None] // HEAD_DIM == np.arange(ATT_HEADS)[None, :]).astype(np.float32)
    return jnp.asarray(sel, BF16), jnp.asarray(sel.T, BF16)


def _head_rms(xv, sel, selT):
    ms = _tri_sum(sel, xv * xv, NN, False, 1) * (1.0 / HEAD_DIM)
    return _tri_sum(selT, lax.rsqrt(ms + EPS), NN, False, 2)


def _headnorm_fwd(name, proj, col_block, w):
    T = proj.shape[0]
    sel, selT = _head_sel()

    def body(x_ref, w_ref, sel_ref, selT_ref, o_ref):
        xv = x_ref[...]
        o_ref[...] = (xv * _head_rms(xv, sel_ref[...], selT_ref[...]) * w_ref[...]).astype(BF16)

    full = lambda shp: pl.BlockSpec(shp, lambda i: (0, 0))
    return pl.pallas_call(
        body, out_shape=jax.ShapeDtypeStruct((T, 1024), BF16), grid=(T // ROW_T,),
        in_specs=[pl.BlockSpec((ROW_T, 1024), lambda i: (i, col_block)), full((1, 1024)), full((1024, ATT_HEADS)),
                  full((ATT_HEADS, 1024))],
        out_specs=pl.BlockSpec((ROW_T, 1024), lambda i: (i, 0)), name=name, compiler_params=_cp("parallel"))(
            proj, jnp.tile(w, (1, ATT_HEADS)), sel, selT)


def _headnorm_bwd(name, dn, proj, col_block, w, dproj):
    T = proj.shape[0]
    sel, selT = _head_sel()

    def body(dn_ref, x_ref, w_ref, sel_ref, selT_ref, buf_ref, dx_ref, dw_ref):
        @pl.when(pl.program_id(0) == 0)
        def _():
            dw_ref[...] = jnp.zeros_like(dw_ref)

        xv = x_ref[...]
        sl, slT = sel_ref[...], selT_ref[...]
        rb = _head_rms(xv, sl, slT)
        xhat = xv * rb
        dnv = dn_ref[...]
        dxhat = dnv * w_ref[...]
        mean = _tri_sum(slT, _tri_sum(sl, dxhat * xhat, NN, False, 2) * (1.0 / HEAD_DIM), NN, False, 2)
        dx_ref[...] = (rb * (dxhat - xhat * mean)).astype(BF16)
        dw_ref[...] += jnp.sum(dnv * xhat, axis=0, keepdims=True)

    here = pl.BlockSpec((ROW_T, 1024), lambda i: (i, col_block))
    full = lambda shp: pl.BlockSpec(shp, lambda i: (0, 0))
    dx, dw = pl.pallas_call(
        body, out_shape=(jax.ShapeDtypeStruct(dproj.shape, dproj.dtype), jax.ShapeDtypeStruct((1, 1024), F32)),
        grid=(T // ROW_T,),
        in_specs=[pl.BlockSpec((ROW_T, 1024), lambda i: (i, 0)), here, full((1, 1024)), full((1024, ATT_HEADS)),
                  full((ATT_HEADS, 1024)), ANY],
        out_specs=(here, full((1, 1024))), input_output_aliases={5: 0},
        name=name, compiler_params=_cp("arbitrary"))(dn, proj, jnp.tile(w, (1, ATT_HEADS)), sel, selT, dproj)
    return dx, jnp.sum(dw.reshape(ATT_HEADS, HEAD_DIM), axis=0, keepdims=True)


def _att_bias(nq):
    j = np.arange(ATT_B)[:, None]
    i = np.arange(ATT_B)[None, :]
    out = np.empty((nq, ATT_B, ATT_B), np.float32)
    for dblk in range(nq):
        dl = ATT_B * dblk + i - j
        cnt = ((dl >= 0) & (dl <= 128)).astype(np.float32)
        cnt += ((dl >= 0) & (dl % 4 == 0) & (dl <= 512))
        cnt += ((dl >= 0) & (dl % 16 == 0) & (dl <= 2048))
        out[dblk] = np.where(cnt > 0, np.log(np.maximum(cnt, 1.0)), NEG)
    return jnp.asarray(out)


def _row_pair(nq):
    def f(r, c):
        first = c <= r
        return jnp.where(first, r, nq - 1 - r), jnp.where(first, c, c - (r + 1))
    return f


def _col_pair(nq):
    def f(r, c):
        first = c < nq - r
        kj = jnp.where(first, r, nq - 1 - r)
        return jnp.where(first, r + c, nq - 1 - r + (c - (nq - r))), kj
    return f


ATT_SCALE = 1.0 / math.sqrt(HEAD_DIM)
ATT_HS = 4
ATT_W = ATT_HS * HEAD_DIM


def _att_maps(nq, qk):
    return dict(
        q_tok=lambda b, g, r, c: (b * nq + qk(r, c)[0], g),
        k_tok=lambda b, g, r, c: (b * nq + qk(r, c)[1], g),
        v_tok=lambda b, g, r, c: (b * nq + qk(r, c)[1], COL_V // ATT_W + g),
        q_feat=lambda b, g, r, c: (g, b * nq + qk(r, c)[0]),
        k_feat=lambda b, g, r, c: (g, b * nq + qk(r, c)[1]),
        bias=lambda b, g, r, c: (qk(r, c)[0] - qk(r, c)[1], 0, 0),
        lse=lambda b, g, r, c: (g, 0, b * nq + qk(r, c)[0]),
        do_tok=lambda b, g, r, c: (b * nq + qk(r, c)[0], ATT_HS + g))


def _att_fwd(name, kn, qT, vT, bias, B):
    T = kn.shape[0]
    nq = (T // B) // ATT_B
    qk = _row_pair(nq)
    mp = _att_maps(nq, qk)

    def body(k_ref, qT_ref, vT_ref, bias_ref, oT_ref, lse_ref, m_s, l_s, acc_s, s_s):
        qi, kj = qk(pl.program_id(2), pl.program_id(3))

        @pl.when(kj == 0)
        def _():
            m_s[...] = jnp.full_like(m_s, NEG)
            l_s[...] = jnp.zeros_like(l_s)
            acc_s[...] = jnp.zeros_like(acc_s)

        bv = bias_ref[...]
        for h in range(ATT_HS):
            rs = slice(HEAD_DIM * h, HEAD_DIM * (h + 1))
            s_s[h] = _dot(k_ref[:, rs], qT_ref[rs, :], NN)
        for h in range(ATT_HS):
            rs = slice(HEAD_DIM * h, HEAD_DIM * (h + 1))
            s = s_s[h] + bv
            m_prev = m_s[h:h + 1, :]
            m_new = jnp.maximum(m_prev, jnp.max(s, axis=0, keepdims=True))
            alpha = jnp.exp(m_prev - m_new)
            p = jnp.exp(s - m_new)
            l_s[h:h + 1, :] = alpha * l_s[h:h + 1, :] + jnp.sum(p, axis=0, keepdims=True)
            acc_s[rs, :] = alpha * acc_s[rs, :] + _dot(vT_ref[rs, :], p.astype(BF16), NN)
            m_s[h:h + 1, :] = m_new

        @pl.when(kj == qi)
        def _():
            for h in range(ATT_HS):
                rs = slice(HEAD_DIM * h, HEAD_DIM * (h + 1))
                oT_ref[rs, :] = (acc_s[rs, :] / l_s[h:h + 1, :]).astype(BF16)
            lse_ref[...] = m_s[...] + jnp.log(l_s[...])

    tok = (ATT_B, ATT_W)
    feat = (ATT_W, ATT_B)
    return pl.pallas_call(
        body,
        out_shape=(jax.ShapeDtypeStruct((1024, T), BF16), jax.ShapeDtypeStruct((ATT_HEADS // ATT_HS, ATT_HS, T), F32)),
        grid=(B, ATT_HEADS // ATT_HS, nq // 2, nq + 1),
        in_specs=[pl.BlockSpec(tok, mp["k_tok"]), pl.BlockSpec(feat, mp["q_feat"]), pl.BlockSpec(feat, mp["k_feat"]),
                  pl.BlockSpec((None, ATT_B, ATT_B), mp["bias"])],
        out_specs=(pl.BlockSpec(feat, mp["q_feat"]), pl.BlockSpec((None, ATT_HS, ATT_B), mp["lse"])),
        scratch_shapes=[pltpu.VMEM((ATT_HS, ATT_B), F32), pltpu.VMEM((ATT_HS, ATT_B), F32),
                        pltpu.VMEM((ATT_W, ATT_B), F32), pltpu.VMEM((ATT_HS, ATT_B, ATT_B), F32)],
        name=name, compiler_params=_cp("parallel", "parallel", "arbitrary", "arbitrary"))(kn, qT, vT, bias)


def _att_scores(k_ref, qT_ref, v_ref, doT_ref, s_s, dp_s):
    for h in range(ATT_HS):
        rs = slice(HEAD_DIM * h, HEAD_DIM * (h + 1))
        s_s[h] = _dot(k_ref[:, rs], qT_ref[rs, :], NN)
        dp_s[h] = _dot(v_ref[:, rs].astype(BF16), doT_ref[rs, :].astype(BF16), NN)


def _att_p_ds(s_s, dp_s, doT_ref, oT_ref, lse_ref, bv, h):
    rs = slice(HEAD_DIM * h, HEAD_DIM * (h + 1))
    delta = jnp.sum(doT_ref[rs, :] * oT_ref[rs, :].astype(F32), axis=0, keepdims=True)
    p = jnp.exp(s_s[h] + bv - lse_ref[h:h + 1, :])
    return p, p * (dp_s[h] - delta)


def _att_bwd_dq(name, kn, qT, vb, knT, bias, doT, oT, lse, B):
    T = kn.shape[0]
    nq = (T // B) // ATT_B
    qk = _row_pair(nq)
    mp = _att_maps(nq, qk)

    def body(k_ref, qT_ref, v_ref, kT_ref, bias_ref, doT_ref, oT_ref, lse_ref, dqT_ref, acc_s, s_s, dp_s):
        qi, kj = qk(pl.program_id(2), pl.program_id(3))

        @pl.when(kj == 0)
        def _():
            acc_s[...] = jnp.zeros_like(acc_s)

        bv = bias_ref[...]
        _att_scores(k_ref, qT_ref, v_ref, doT_ref, s_s, dp_s)
        for h in range(ATT_HS):
            rs = slice(HEAD_DIM * h, HEAD_DIM * (h + 1))
            p, ds = _att_p_ds(s_s, dp_s, doT_ref, oT_ref, lse_ref, bv, h)
            acc_s[rs, :] += _dot(kT_ref[rs, :], ds.astype(BF16), NN)

        @pl.when(kj == qi)
        def _():
            dqT_ref[...] = acc_s[...] * ATT_SCALE

    tok = (ATT_B, ATT_W)
    feat = (ATT_W, ATT_B)
    return pl.pallas_call(
        body, out_shape=jax.ShapeDtypeStruct((1024, T), F32), grid=(B, ATT_HEADS // ATT_HS, nq // 2, nq + 1),
        in_specs=[pl.BlockSpec(tok, mp["k_tok"]), pl.BlockSpec(feat, mp["q_feat"]), pl.BlockSpec(tok, mp["v_tok"]),
                  pl.BlockSpec(feat, mp["k_feat"]), pl.BlockSpec((None, ATT_B, ATT_B), mp["bias"]),
                  pl.BlockSpec(feat, mp["q_feat"]), pl.BlockSpec(feat, mp["q_feat"]),
                  pl.BlockSpec((None, ATT_HS, ATT_B), mp["lse"])],
        out_specs=pl.BlockSpec(feat, mp["q_feat"]),
        scratch_shapes=[pltpu.VMEM((ATT_W, ATT_B), F32), pltpu.VMEM((ATT_HS, ATT_B, ATT_B), F32),
                        pltpu.VMEM((ATT_HS, ATT_B, ATT_B), F32)],
        name=name, compiler_params=_cp("parallel", "parallel", "arbitrary", "arbitrary"))(
            kn, qT, vb, knT, bias, doT, oT, lse)


def _att_bwd_dkv(name, kn, qT, vb, qn, bias, doT, oT, lse, dyn, dproj, B):
    T = kn.shape[0]
    nq = (T // B) // ATT_B
    qk = _col_pair(nq)
    mp = _att_maps(nq, qk)

    def body(k_ref, qT_ref, v_ref, q_ref, bias_ref, doT_ref, oT_ref, lse_ref, do_ref, buf_ref, dk_ref, dv_ref, dk_s, dv_s,
             s_s, dp_s):
        qi, kj = qk(pl.program_id(2), pl.program_id(3))

        @pl.when(qi == kj)
        def _():
            dk_s[...] = jnp.zeros_like(dk_s)
            dv_s[...] = jnp.zeros_like(dv_s)

        bv = bias_ref[...]
        _att_scores(k_ref, qT_ref, v_ref, doT_ref, s_s, dp_s)
        for h in range(ATT_HS):
            rs = slice(HEAD_DIM * h, HEAD_DIM * (h + 1))
            p, ds = _att_p_ds(s_s, dp_s, doT_ref, oT_ref, lse_ref, bv, h)
            dv_s[h] += _dot(p.astype(BF16), do_ref[:, rs].astype(BF16), NN)
            dk_s[h] += _dot(ds.astype(BF16), q_ref[:, rs], NN)

        @pl.when(qi == nq - 1)
        def _():
            for h in range(ATT_HS):
                rs = slice(HEAD_DIM * h, HEAD_DIM * (h + 1))
                dk_ref[:, rs] = dk_s[h] * ATT_SCALE
                dv_ref[:, rs] = dv_s[h].astype(BF16)

    tok = (ATT_B, ATT_W)
    feat = (ATT_W, ATT_B)
    v_cb = COL_V // ATT_W
    return pl.pallas_call(
        body, out_shape=(jax.ShapeDtypeStruct((T, 1024), F32), jax.ShapeDtypeStruct(dproj.shape, dproj.dtype)),
        grid=(B, ATT_HEADS // ATT_HS, nq // 2, nq + 1),
        in_specs=[pl.BlockSpec(tok, mp["k_tok"]), pl.BlockSpec(feat, mp["q_feat"]), pl.BlockSpec(tok, mp["v_tok"]),
                  pl.BlockSpec(tok, mp["q_tok"]), pl.BlockSpec((None, ATT_B, ATT_B), mp["bias"]),
                  pl.BlockSpec(feat, mp["q_feat"]), pl.BlockSpec(feat, mp["q_feat"]),
                  pl.BlockSpec((None, ATT_HS, ATT_B), mp["lse"]), pl.BlockSpec(tok, mp["do_tok"]), ANY],
        out_specs=(pl.BlockSpec(tok, mp["k_tok"]),
                   pl.BlockSpec(tok, lambda b, g, r, c: (b * nq + qk(r, c)[1], v_cb + g))),
        input_output_aliases={9: 1},
        scratch_shapes=[pltpu.VMEM((ATT_HS, ATT_B, HEAD_DIM), F32), pltpu.VMEM((ATT_HS, ATT_B, HEAD_DIM), F32),
                        pltpu.VMEM((ATT_HS, ATT_B, ATT_B), F32), pltpu.VMEM((ATT_HS, ATT_B, ATT_B), F32)],
        name=name, compiler_params=_cp("parallel", "parallel", "arbitrary", "arbitrary"))(
            kn, qT, vb, qn, bias, doT, oT, lse, dyn, dproj)


def _group_cols(v):
    return v.reshape(SSD_GROUPS, 4)


def _ssd_params(p):
    rows = jnp.stack([_group_cols(p["dt_bias"]), _group_cols(p["a_log"]), _group_cols(p["d_skip"])], axis=1)
    return rows, jnp.swapaxes(rows, 1, 2)


def _dymix(name, dx, wout):
    T = dx.shape[0]

    def body(dx_ref, w_ref, o_ref):
        dxb = dx_ref[...].astype(BF16)
        for n in range(N_SHARD):
            o_ref[:, MIX_SH * n:MIX_SH * (n + 1)] = _dot(dxb, w_ref[n], NT)

    return pl.pallas_call(
        body, out_shape=jax.ShapeDtypeStruct((T, MIX_W), F32), grid=(T // ROW_T,),
        in_specs=[pl.BlockSpec((ROW_T, D_MODEL), lambda i: (i, 0)),
                  pl.BlockSpec((N_SHARD, MIX_SH, D_MODEL), lambda i: (0, 0, 0))],
        out_specs=pl.BlockSpec((ROW_T, MIX_W), lambda i: (i, 0)), name=name, compiler_params=_cp("parallel"))(dx, wout)


def _mixer_fwd(tag, x1, h2, p, weights, bias, B):
    T = x1.shape[0]
    S = T // B
    nt = T // ROW_T
    wi = weights("win", h2)
    win, cw = wi["win"], wi["cw"]
    proj = _mm(tag + "_proj",
               [(h2, pl.BlockSpec((ROW_T, D_MODEL), lambda j, i, k: (i, 0)),
                 win, pl.BlockSpec((D_MODEL, PROJ_TN), lambda j, i, k: (0, j)))],
               jax.ShapeDtypeStruct((T, IN_PAD), F32), pl.BlockSpec((ROW_T, PROJ_TN), lambda j, i, k: (i, j)),
               (IN_PAD // PROJ_TN, nt, 1), NN, (ROW_T, PROJ_TN))
    xc = _conv_fwd(tag + "_conv", proj, cw, p["conv_b"][None], B)
    dtraw = proj[:, COL_DT:COL_DT + SSD_HEADS].reshape(T, SSD_GROUPS, 4)
    dtc = jnp.transpose(dtraw, (1, 0, 2))
    dtr = jnp.transpose(dtraw, (1, 2, 0))
    pcol, prow = _ssd_params(p)
    Y, y_ssd, hs = _ssd_fwd(tag + "_ssd", xc, proj, dtc, dtr, pcol, prow, p["ssd_norm"][None], B)
    qn = _headnorm_fwd(tag + "_qn", proj, COL_Q // 1024, p["q_norm"][None])
    kn = _headnorm_fwd(tag + "_kn", proj, COL_K // 1024, p["k_norm"][None])
    qT = (qn * ATT_SCALE).T
    oT, lse = _att_fwd(tag + "_att", kn, qT, proj[:, COL_V:COL_V + 1024].T.astype(BF16), bias, B)
    ymix = jnp.concatenate([y_ssd, oT.T], axis=1)
    rest = weights("rest", ymix)
    o = _with_norm(T, p["ffn2_norm"][None])
    x2, h3 = _mm(tag + "_out",
                 [(ymix, pl.BlockSpec((ROW_T, MIX_SH), lambda i, n, k, j=j: (i, j)),
                   rest["wout"], pl.BlockSpec((None, MIX_SH, D_MODEL), lambda i, n, k, j=j: (j, 0, 0)))
                  for j in range(N_SHARD)],
                 o.pop("out_shape"), o.pop("out_spec"), (nt, 1, 1), NN, (ROW_T, D_MODEL), res=(x1, _row3()), **o)
    saved = dict(x1=x1, h2=h2, proj=proj, xc=xc, dtc=dtc, dtr=dtr, Y=Y, hs=hs,
                 qn=qn, kn=kn, qT=qT, oT=oT, lse=lse, ymix=ymix, win=win, cw=cw, wout=rest["wout"])
    return x2, h3, saved


def _mixer_bwd(tag, dx2, sv, p, bias, B):
    T = dx2.shape[0]
    S = T // B
    nt = T // ROW_T
    sg = {}
    dymix = _dymix(tag + "_dymix", dx2, sv["wout"])
    tkw = min(TK_W, T)
    gwout = _mm(tag + "_dwout",
                [(sv["ymix"], pl.BlockSpec((tkw, MIX_SH), lambda m, n, k: (k, m)),
                  dx2, pl.BlockSpec((tkw, D_MODEL), lambda m, n, k: (k, 0)))],
                jax.ShapeDtypeStruct((N_SHARD, MIX_SH, D_MODEL), BF16),
                pl.BlockSpec((None, MIX_SH, D_MODEL), lambda m, n, k: (m, 0, 0)),
                (N_SHARD, 1, T // tkw), TN, (MIX_SH, D_MODEL))
    proj = sv["proj"]
    doT = dymix[:, 1024:].T
    dqn = _att_bwd_dq(tag + "_attdq", sv["kn"], sv["qT"], proj, sv["kn"].T, bias, doT, sv["oT"], sv["lse"], B).T
    dproj = lax.empty((T, IN_PAD), BF16)
    dkn, dproj = _att_bwd_dkv(tag + "_attdkv", sv["kn"], sv["qT"], proj, sv["qn"], bias, doT, sv["oT"], sv["lse"],
                              dymix, dproj, B)
    dproj, sg["q_norm"] = _headnorm_bwd(tag + "_qnb", dqn, proj, COL_Q // 1024, p["q_norm"][None], dproj)
    dproj, sg["k_norm"] = _headnorm_bwd(tag + "_knb", dkn, proj, COL_K // 1024, p["k_norm"][None], dproj)
    pcol, prow = _ssd_params(p)
    dxs, dB, dC, dproj, ddt, dpar, dnw = _ssd_bwd(tag + "_ssdb", dymix, sv["Y"], sv["xc"], proj, sv["dtc"], sv["dtr"],
                                                  pcol, prow, p["ssd_norm"][None], sv["hs"], dproj, B)
    dpar = jnp.sum(dpar, axis=0)
    sg["dt_bias"] = dpar[:, 0, :].reshape(SSD_HEADS)
    sg["a_log"] = dpar[:, 1, :].reshape(SSD_HEADS)
    sg["d_skip"] = dpar[:, 2, :].reshape(SSD_HEADS)
    sg["ssd_norm"] = jnp.sum(dnw, axis=0)
    dproj, sg["conv_w"], sg["conv_b"] = _conv_bwd(tag + "_convb", proj, dxs, dB, dC, sv["cw"], p["conv_b"][None],
                                                  dproj, B)
    ddt16 = jnp.transpose(ddt, (1, 0, 2)).reshape(T, SSD_HEADS)
    dproj = lax.dynamic_update_slice(dproj, jnp.pad(ddt16, ((0, 0), (0, IN_PAD - COL_DT - SSD_HEADS))).astype(BF16),
                                     (0, COL_DT))
    win = sv["win"]
    gwin = _mm(tag + "_dwin",
               [(sv["h2"], pl.BlockSpec((tkw, D_MODEL), lambda n, m, k: (k, 0)),
                 dproj, pl.BlockSpec((tkw, PROJ_TN), lambda n, m, k: (k, n)))],
               jax.ShapeDtypeStruct((D_MODEL, IN_PAD), BF16), pl.BlockSpec((D_MODEL, PROJ_TN), lambda n, m, k: (0, n)),
               (IN_PAD // PROJ_TN, 1, T // tkw), TN, (D_MODEL, PROJ_TN))
    dx1, sg["mix_norm"] = _mm(
        tag + "_dh2",
        [(dproj, pl.BlockSpec((ROW_T, PROJ_TN), lambda i, n, k, j=j: (i, j)),
          win, pl.BlockSpec((D_MODEL, PROJ_TN), lambda i, n, k, j=j: (0, j))) for j in range(IN_PAD // PROJ_TN)],
        (jax.ShapeDtypeStruct((T, D_MODEL), F32), jax.ShapeDtypeStruct((1, D_MODEL), F32)), (_row3(), _vec3()),
        (nt, 1, 1), NT, (ROW_T, D_MODEL), post="rmsb",
        post_in=[(sv["x1"], _row3()), (p["mix_norm"][None], _vec3()), (dx2, _row3())])
    return dx1, sg, gwout, gwin


def _win_pack(w):
    return jnp.concatenate([w[:, :3072], w[:, 3088:], w[:, 3072:3088],
                            jnp.zeros((w.shape[0], IN_PAD - IN_PROJ), w.dtype)], axis=1)


def _win_unpack(g):
    return jnp.concatenate([g[:, :3072], g[:, COL_DT:COL_DT + SSD_HEADS], g[:, 3072:COL_DT]], axis=1)


DT_LO = IN_SH * 2 - COL_Q


def _win_from_shards(sh):
    main = IN_SH - DT_LO
    return jnp.concatenate([sh[0], sh[1][:, :main], sh[2][:, SSD_HEADS - DT_LO:], sh[3], sh[1][:, main:],
                            sh[2][:, :SSD_HEADS - DT_LO], jnp.zeros((sh.shape[1], IN_PAD - IN_PROJ), sh.dtype)], axis=1)


def _win_to_shards(g):
    main = IN_SH - DT_LO
    a, b = IN_SH + main, IN_SH + 2 * main
    return jnp.stack([g[:, :IN_SH],
                      jnp.concatenate([g[:, IN_SH:a], g[:, COL_DT:COL_DT + DT_LO]], axis=1),
                      jnp.concatenate([g[:, COL_DT + DT_LO:COL_DT + SSD_HEADS], g[:, a:b]], axis=1),
                      g[:, b:COL_DT]])


def _local_step(x, target, small, weights, scatter, B):
    T = x.shape[0]
    bias = _att_bias((T // B) // ATT_B)
    saved = []
    xl = x
    hl = _rms_fwd("l0f1_rms", x, small["ffn1_norm"][0][None])
    for l in range(DEPTH):
        tag = "l%d" % l
        p = {k: v[l] for k, v in small.items()}
        w1 = weights(l, "ffn1", hl)
        (x1, h2), ffn1, d1 = _ffn_fwd(tag + "f1", xl, hl, w1["g1"], w1["u1"],
                                      lambda after, l=l: weights(l, "ffn1d", after)["d1"], p["mix_norm"][None])
        x2, h3, sv = _mixer_fwd(tag, x1, h2, p, functools.partial(weights, l), bias, B)
        w2 = weights(l, "rest", x2)
        nxt = small["ffn1_norm"][l + 1][None] if l + 1 < DEPTH else None
        xo, ffn2, _ = _ffn_fwd(tag + "f2", x2, h3, w2["g2"], w2["u2"], w2["d2"], nxt)
        xl, hl = xo if nxt is not None else (xo, None)
        saved.append((ffn1, sv, ffn2, dict(g1=w1["g1"], u1=w1["u1"], d1=d1), w2))
    d, lsum = _loss_grad("loss", xl, target)
    sgrads = [None] * DEPTH
    for l in reversed(range(DEPTH)):
        tag = "l%db" % l
        p = {k: v[l] for k, v in small.items()}
        ffn1, sv, ffn2, w1, w2 = saved[l]
        sg = {}
        d, sg["ffn2_norm"] = _ffn_bwd(tag + "f2", d, ffn2, p["ffn2_norm"][None], w2["g2"], w2["u2"], w2["d2"],
                                      lambda gg, gu, gd, c, l=l: scatter(l, "ffn2", dict(g2=gg, u2=gu, d2=gd), c))
        d, sgm, gwout, gwin = _mixer_bwd(tag, d, sv, p, bias, B)
        sg.update(sgm)
        d = scatter(l, "mixer", dict(wout=gwout, win=gwin), d)
        d, sg["ffn1_norm"] = _ffn_bwd(tag + "f1", d, ffn1, p["ffn1_norm"][None], w1["g1"], w1["u1"], w1["d1"],
                                      lambda gg, gu, gd, c, l=l: scatter(l, "ffn1", dict(g1=gg, u1=gu, d1=gd), c))
        sgrads[l] = sg
    return lsum, d, sgrads


MESH = pl.DeviceIdType.MESH
ANY = pl.BlockSpec(memory_space=pl.ANY)


def _place():
    return lax.axis_index("x"), lax.axis_index("y"), lax.axis_index("c")


def _other_chips(x, y):
    return [(1 - x, y), (x, 1 - y), (1 - x, 1 - y)]


HBM = pl.BlockSpec(memory_space=pltpu.HBM)
SEM = pl.BlockSpec(memory_space=pltpu.SEMAPHORE)
EFFECT = pltpu.SideEffectType.DATAFLOW_SIDE_EFFECTING


def _hbm(a):
    return pltpu.with_memory_space_constraint(a, pltpu.HBM)


def _exchange(gather, layer, src, land, send, recv, n, act):
    x, y, c = _place()
    for k, (px, py) in enumerate(_other_chips(x, y)):
        for a in range(n):
            if gather:
                s_out, d_out, d_in = src[a].at[layer], land[a].at[2 * x + y], land[a].at[2 * px + py]
            else:
                s_out, d_out, d_in = src[a].at[2 * px + py], land[a].at[k], land[a].at[k]
            act(pltpu.make_async_remote_copy(
                src_ref=s_out, dst_ref=d_out if act is _start else d_in, send_sem=send.at[k * n + a],
                recv_sem=recv.at[k * n + a], device_id=(px, py, c), device_id_type=MESH))


def _start(cp):
    cp.start()


def _finish(cp):
    cp.wait_send()
    cp.wait_recv()


def _exchange_start(name, gather, layer, srcs, carry):
    n = len(srcs)
    lands = [lax.empty(((N_SHARD,) + s.shape[1:]) if gather else ((3,) + s.shape[1:]), s.dtype) for s in srcs]

    def body(*refs):
        _exchange(gather, layer, refs[:n], refs[n:2 * n], refs[2 * n + 1], refs[2 * n + 2], n, _start)

    srcs = [_hbm(a) for a in srcs]
    thru = [_hbm(a) for a in lands + [carry]]
    out = pl.pallas_call(
        body, name=name,
        out_shape=(pltpu.SemaphoreType.DMA((3 * n,)), pltpu.SemaphoreType.DMA((3 * n,)),
                   *[pltpu.HBM(a.shape, a.dtype) for a in thru]),
        in_specs=[HBM] * (2 * n + 1), out_specs=(SEM, SEM, *[HBM] * (n + 1)),
        input_output_aliases={n + i: 2 + i for i in range(n + 1)},
        compiler_params=pltpu.CompilerParams(has_side_effects=EFFECT))(*srcs, *thru)
    return dict(gather=gather, layer=layer, send=out[0], recv=out[1], srcs=srcs, lands=list(out[2:2 + n])), out[-1]


def _exchange_wait(name, ex, after):
    n = len(ex["srcs"])

    def body(*refs):
        _exchange(ex["gather"], ex["layer"], refs[:n], refs[n:2 * n], refs[2 * n], refs[2 * n + 1], n, _finish)

    out = pl.pallas_call(
        body, name=name, out_shape=[pltpu.HBM(a.shape, a.dtype) for a in ex["lands"]],
        in_specs=[HBM] * (2 * n) + [SEM, SEM, ANY], out_specs=[HBM] * n,
        input_output_aliases={n + i: i for i in range(n)},
        compiler_params=pltpu.CompilerParams(has_side_effects=EFFECT))(
            *ex["srcs"], *ex["lands"], ex["send"], ex["recv"], after)
    return list(out)


def _swap_sibling(parts):
    n = len(parts)

    def body(*refs):
        src, dst = refs[:n], refs[n:2 * n]
        send, recv = refs[2 * n:]
        x, y, c = _place()
        cps = [pltpu.make_async_remote_copy(src_ref=src[a], dst_ref=dst[a], send_sem=send.at[a], recv_sem=recv.at[a],
                                            device_id=(x, y, 1 - c), device_id_type=MESH) for a in range(n)]
        for cp in cps:
            cp.start()
        for cp in cps:
            cp.wait_recv()
        for cp in cps:
            cp.wait_send()

    return pl.pallas_call(
        body, out_shape=[jax.ShapeDtypeStruct(p.shape, p.dtype) for p in parts],
        in_specs=[ANY] * n, out_specs=[ANY] * n,
        scratch_shapes=[pltpu.SemaphoreType.DMA((n,)), pltpu.SemaphoreType.DMA((n,))],
        name="swap_sibling")(*parts)


def _allreduce_small(name, v, after):
    R = v.shape[0]

    def body(v_ref, after_ref, o_ref, buf, send, recv):
        x, y, c = _place()
        me = 4 * x + 2 * y + c
        buf[me] = v_ref[...]
        cps = []
        for k in range(1, 8):
            fx, fy, fc = (k >> 2) & 1, (k >> 1) & 1, k & 1
            px = 1 - x if fx else x
            py = 1 - y if fy else y
            pc = 1 - c if fc else c
            cp = pltpu.make_async_remote_copy(src_ref=v_ref, dst_ref=buf.at[me], send_sem=send.at[k - 1],
                                              recv_sem=recv.at[k - 1], device_id=(px, py, pc), device_id_type=MESH)
            cp.start()
            cps.append((cp, 4 * px + 2 * py + pc))
        for k, (cp, peer) in enumerate(cps):
            pltpu.make_async_remote_copy(src_ref=v_ref, dst_ref=buf.at[peer], send_sem=send.at[k], recv_sem=recv.at[k],
                                         device_id=(x, y, c), device_id_type=MESH).wait_recv()
        for cp, _ in cps:
            cp.wait_send()
        acc = buf[0]
        for d in range(1, 8):
            acc = acc + buf[d]
        o_ref[...] = acc

    return pl.pallas_call(
        body, out_shape=jax.ShapeDtypeStruct((R, 128), F32),
        in_specs=[pl.BlockSpec(memory_space=pltpu.VMEM), ANY], out_specs=pl.BlockSpec(memory_space=pltpu.VMEM),
        scratch_shapes=[pltpu.VMEM((8, R, 128), F32), pltpu.SemaphoreType.DMA((7,)), pltpu.SemaphoreType.DMA((7,))],
        name=name)(v, after)


def _row_tile(r):
    for t in (256, 128, 64, 32, 16, 8):
        if r % t == 0:
            return t
    raise ValueError(r)


def _sum4(name, me, parts, got):
    _, R, C = parts.shape
    tr = _row_tile(R)

    def body(me_ref, o_ref, g_ref, s_ref):
        s = o_ref[...].astype(F32)
        for k in range(3):
            s = s + g_ref[k].astype(F32)
        s_ref[...] = s.astype(BF16)

    return pl.pallas_call(
        body, out_shape=jax.ShapeDtypeStruct((R, C), BF16),
        grid_spec=pltpu.PrefetchScalarGridSpec(
            num_scalar_prefetch=1, grid=(R // tr,),
            in_specs=[pl.BlockSpec((None, tr, C), lambda i, me_ref: (me_ref[0], i, 0)),
                      pl.BlockSpec((3, tr, C), lambda i, me_ref: (0, i, 0))],
            out_specs=pl.BlockSpec((tr, C), lambda i, me_ref: (i, 0))),
        name=name, compiler_params=_cp("parallel"))(me, parts, got)


def _adamw(name, w, gparts, m, v):
    R, C = w.shape
    tr = _row_tile(R)
    ng = len(gparts)
    c1 = 1.0 - ADAM_B1 ** ADAM_STEP
    c2 = 1.0 - ADAM_B2 ** ADAM_STEP

    def body(*refs):
        w_ref = refs[0]
        g_refs = refs[1:1 + ng]
        m_ref, v_ref, go_ref, d_ref, mo_ref, vo_ref = refs[1 + ng:]
        g = g_refs[0][...]
        for r in g_refs[1:]:
            g = g + r[...]
        mn = ADAM_B1 * m_ref[...] + (1.0 - ADAM_B1) * g
        vn = ADAM_B2 * v_ref[...] + (1.0 - ADAM_B2) * (g * g)
        go_ref[...] = g
        mo_ref[...] = mn
        vo_ref[...] = vn
        d_ref[...] = -ADAM_LR * ((mn / c1) / (jnp.sqrt(vn / c2) + ADAM_EPS) + ADAM_WD * w_ref[...])

    blk = pl.BlockSpec((tr, C), lambda i: (i, 0))
    osh = jax.ShapeDtypeStruct((R, C), F32)
    return pl.pallas_call(
        body, out_shape=(osh, osh, osh, osh), grid=(R // tr,), in_specs=[blk] * (3 + ng), out_specs=(blk,) * 4,
        name=name, compiler_params=_cp("parallel"))(w, *gparts, m, v)


def _adamw_layers(name, w, sums, m, v):
    _, R, C = w.shape
    tr = _row_tile(R)
    nr = R // tr
    c1 = 1.0 - ADAM_B1 ** ADAM_STEP
    c2 = 1.0 - ADAM_B2 ** ADAM_STEP

    def body(w_ref, a0, b0, a1, b1, m_ref, v_ref, go_ref, d_ref, mo_ref, vo_ref):
        f = lambda r: r[...].astype(F32)
        g = jnp.where(pl.program_id(0) == 0, f(a0) + f(b0), f(a1) + f(b1))
        mn = ADAM_B1 * m_ref[...] + (1.0 - ADAM_B1) * g
        vn = ADAM_B2 * v_ref[...] + (1.0 - ADAM_B2) * (g * g)
        go_ref[...] = g
        mo_ref[...] = mn
        vo_ref[...] = vn
        d_ref[...] = -ADAM_LR * ((mn / c1) / (jnp.sqrt(vn / c2) + ADAM_EPS) + ADAM_WD * w_ref[...])

    blk = pl.BlockSpec((None, tr, C), lambda l, i: (l, i, 0))
    lay0 = pl.BlockSpec((tr, C), lambda l, i: (jnp.where(l == 0, i, nr - 1), 0))
    lay1 = pl.BlockSpec((tr, C), lambda l, i: (jnp.where(l == 1, i, 0), 0))
    oblk = pl.BlockSpec((tr, C), lambda l, i: (l * nr + i, 0))
    osh = jax.ShapeDtypeStruct((DEPTH * R, C), F32)
    res = pl.pallas_call(
        body, out_shape=(osh, osh, osh, osh), grid=(DEPTH, nr),
        in_specs=[blk, lay0, lay0, lay1, lay1, blk, blk], out_specs=(oblk,) * 4,
        name=name, compiler_params=_cp("arbitrary", "arbitrary"))(w, *sums[0], *sums[1], m, v)
    return [r.reshape(w.shape) for r in res]


BIG = [("ffn1_w_gate", "g1"), ("ffn1_w_up", "u1"), ("ffn1_w_down", "d1"), ("w_in", "win"), ("w_out", "wout"),
       ("ffn2_w_gate", "g2"), ("ffn2_w_up", "u2"), ("ffn2_w_down", "d2")]
SMALL = ["ffn1_norm", "mix_norm", "conv_b", "dt_bias", "a_log", "d_skip", "ssd_norm", "q_norm", "k_norm", "ffn2_norm"]
WEIGHTS = ["ffn1_norm", "ffn1_w_gate", "ffn1_w_up", "ffn1_w_down", "mix_norm", "w_in", "conv_w", "conv_b", "dt_bias",
           "a_log", "d_skip", "ssd_norm", "q_norm", "k_norm", "w_out", "ffn2_norm", "ffn2_w_gate", "ffn2_w_up",
           "ffn2_w_down"]
CONV_SH = CONV_DIM // N_SHARD
TRANSPOSED = ("g1", "u1", "g2", "u2")
GATHER_GROUPS = [(0, "ffn1", ["g1", "u1"]), (0, "ffn1d", ["d1"]), (0, "win", ["win", "cw"]),
                 (0, "rest", ["wout", "g2", "u2", "d2"]),
                 (1, "all", ["g1", "u1", "d1", "win", "cw", "wout", "g2", "u2", "d2"])]


def _pad128(v):
    v = v.reshape(-1)
    return jnp.pad(v, (0, (-v.shape[0]) % 128))


def _pack(pieces):
    flat, offs, pos = [], [], 0
    for p in pieces:
        q = _pad128(p.astype(F32))
        offs.append(pos)
        pos += q.shape[0] // 128
        flat.append(q)
    total = -(-pos // 8) * 8
    out = jnp.concatenate(flat + [jnp.zeros(((total - pos) * 128,), F32)]).reshape(total, 128)
    return out, offs


def _unpack(packed, offs, shapes):
    out = []
    for off, shp in zip(offs, shapes):
        n = int(np.prod(shp))
        rows = -(-n // 128)
        out.append(packed[off:off + rows].reshape(-1)[:n].reshape(shp))
    return out


def kernel(x, ffn1_norm, ffn1_w_gate, ffn1_w_up, ffn1_w_down, mix_norm, w_in, conv_w, conv_b, dt_bias, a_log, d_skip, ssd_norm, q_norm, k_norm, w_out, ffn2_norm, ffn2_w_gate, ffn2_w_up, ffn2_w_down, loss_target, m_ffn1_norm, m_ffn1_w_gate, m_ffn1_w_up, m_ffn1_w_down, m_mix_norm, m_w_in, m_conv_w, m_conv_b, m_dt_bias, m_a_log, m_d_skip, m_ssd_norm, m_q_norm, m_k_norm, m_w_out, m_ffn2_norm, m_ffn2_w_gate, m_ffn2_w_up, m_ffn2_w_down, v_ffn1_norm, v_ffn1_w_gate, v_ffn1_w_up, v_ffn1_w_down, v_mix_norm, v_w_in, v_conv_w, v_conv_b, v_dt_bias, v_a_log, v_d_skip, v_ssd_norm, v_q_norm, v_k_norm, v_w_out, v_ffn2_norm, v_ffn2_w_gate, v_ffn2_w_up, v_ffn2_w_down):
    A = dict(locals())
    ix, iy, ic = _place()
    me = 2 * ix + iy
    B, S, _ = x.shape
    T = B * S

    view = lambda a, key: jnp.swapaxes(a, 1, 2) if key in TRANSPOSED else a
    own = {key: view(A[name], key).astype(BF16) for name, key in BIG}
    own["cw"] = conv_w
    exs, first_norm = [], ffn1_norm
    for gi, (l, _, keys) in enumerate(GATHER_GROUPS):
        ex, first_norm = _exchange_start("gather_start%d" % gi, True, l, [own[key] for key in keys], first_norm)
        exs.append(ex)
    landed = {}

    def weights(l, group, after):
        gi = [i for i, (gl, gname, _) in enumerate(GATHER_GROUPS) if gl == l and gname in (group, "all")][0]
        if gi not in landed:
            lands = _exchange_wait("gather_wait%d" % gi, exs[gi], after)
            landed[gi] = {}
            for key, land in zip(GATHER_GROUPS[gi][2], lands):
                full = lax.dynamic_update_slice(land, own[key][l][None], (me, 0, 0))
                if key == "win":
                    full = _win_from_shards(full)
                if key == "cw":
                    full = jnp.transpose(full, (1, 0, 2)).reshape(CONV_K, CONV_DIM)
                landed[gi][key] = full
        return landed[gi]

    pending = []

    def scatter(l, group, grads, carry):
        keys = sorted(grads)
        arrs = [grads[key] for key in keys]
        if "win" in grads:
            arrs[keys.index("win")] = _win_to_shards(grads["win"])
        ex, carry = _exchange_start("scatter_start_l%d_%s" % (l, group), False, None, arrs, carry)
        pending.append((l, keys, ex))
        return carry

    small = {name: A[name] for name in SMALL}
    small["ffn1_norm"] = first_norm
    lsum, dx, sgrads = _local_step(x.reshape(T, D_MODEL), loss_target.reshape(T, D_MODEL), small, weights, scatter, B)

    names = SMALL + ["conv_w"]
    shapes = [A[n].shape for n in SMALL] + [(DEPTH, CONV_K, CONV_DIM), ()]
    pieces = [jnp.stack([sgrads[l][n].reshape(shp[1:]) for l in range(DEPTH)]) for n, shp in zip(names, shapes)]
    pieces.append(0.5 / D_MODEL * jnp.sum(lsum))
    packed, offs = _pack(pieces)

    sums, after = {}, dx
    me1 = jnp.reshape(me, (1,)).astype(jnp.int32)
    for idx, (l, keys, ex) in enumerate(pending):
        lands = _exchange_wait("scatter_wait%d" % idx, ex, after)
        for key, g, got in zip(keys, ex["srcs"], lands):
            sums[key, l] = after = _sum4("sum_%s_l%d" % (key, l), me1, g, got)

    red = _unpack(_allreduce_small("allreduce_small", packed, after), offs, shapes)
    loss = red[-1]
    sg = dict(zip(names, red[:-1]))
    order = [(key, l) for _, key in BIG for l in range(DEPTH)]
    theirs = dict(zip(order, _swap_sibling([sums[k] for k in order])))

    out = {}
    for name, key in BIG:
        res = _adamw_layers("adamw_" + key, view(A[name], key), [(sums[key, l], theirs[key, l]) for l in range(DEPTH)],
                            view(A["m_" + name], key), view(A["v_" + name], key))
        out[name] = [view(r, key) for r in res]

    wp, offs = _pack([A[n] for n in SMALL])
    gp, _ = _pack([sg[n] for n in SMALL])
    mp, _ = _pack([A["m_" + n] for n in SMALL])
    vp, _ = _pack([A["v_" + n] for n in SMALL])
    res = _adamw("adamw_small", wp, [gp], mp, vp)
    shapes = [A[n].shape for n in SMALL]
    res = [_unpack(r, offs, shapes) for r in res]
    for i, n in enumerate(SMALL):
        out[n] = [res[q][i] for q in range(4)]
    gcw = lax.dynamic_slice_in_dim(sg["conv_w"], me * CONV_SH, CONV_SH, axis=2)
    flat = lambda a: a.reshape(DEPTH * CONV_K, CONV_SH)
    res = _adamw("adamw_conv_w", flat(conv_w), [flat(gcw)], flat(m_conv_w), flat(v_conv_w))
    out["conv_w"] = [r.reshape(conv_w.shape) for r in res]

    outs = [loss, dx.reshape(B, S, D_MODEL)]
    for q in range(4):
        outs += [out[n][q] for n in WEIGHTS]
    return tuple(outs)
```

```python
import functools
import math

import numpy as np
import jax
import jax.numpy as jnp
from jax import lax
from jax.experimental import pallas as pl
from jax.experimental.pallas import tpu as pltpu

F32 = jnp.float32
BF16 = jnp.bfloat16

D_MODEL = 1024
DEPTH = 2
N_SHARD = 4
D_FF = 2816
FF_SH = D_FF // N_SHARD
SSD_HEADS = 16
HEAD_DIM = 64
SSD_GROUPS = 4
GROUP_W = 256
SSD_STATE = 128
CONV_K = 4
CONV_DIM = 2048
ATT_HEADS = 16
MIX_W = 2048
MIX_SH = MIX_W // N_SHARD
IN_PROJ = 6160
IN_SH = IN_PROJ // N_SHARD
IN_PAD = 6272
PROJ_TN = 896
COL_Z, COL_XBC, COL_Q, COL_K, COL_V, COL_DT = 0, 1024, 3072, 4096, 5120, 6144
EPS = 1e-6
NEG = -1e30
SSD_L = 256
ATT_B = 256
ROW_T = 512
HALF_T = ROW_T // 2
TK_W = 2048
CONV_CT = 256
CONV_R = 256
PAD_R = 8

ADAM_LR, ADAM_B1, ADAM_B2, ADAM_EPS, ADAM_WD, ADAM_STEP = 0.001, 0.9, 0.999, 1e-08, 0.01, 10

NN = (((1,), (0,)), ((), ()))
NT = (((1,), (1,)), ((), ()))
TN = (((0,), (0,)), ((), ()))

VMEM_LIMIT = 56 * 1024 * 1024


def _cp(*sem):
    return pltpu.CompilerParams(dimension_semantics=sem, vmem_limit_bytes=VMEM_LIMIT)


def _dot(a, b, dims):
    return lax.dot_general(a, b, dims, preferred_element_type=F32)


def _sigmoid(x):
    return 0.5 * jnp.tanh(0.5 * x) + 0.5


def _softplus(x):
    return jnp.maximum(x, 0.0) + jnp.log(1.0 + jnp.exp(-jnp.abs(x)))


def _mm(name, pairs, out_shape, out_spec, grid, dims, acc_shape, res=None, scale=1.0, post=None, post_in=()):
    nk = grid[2]
    npair = len(pairs)
    npost = len(post_in)

    def body(*refs):
        ab = refs[:2 * npair]
        pos = 2 * npair
        res_ref = None
        if res is not None:
            res_ref = refs[pos]
            pos += 1
        pin = refs[pos:pos + npost]
        pos += npost
        out_ref = refs[pos]
        pos += 1
        if post is not None:
            out2_ref = refs[pos]
            pos += 1
        s = None
        for p in range(npair):
            d = _dot(ab[2 * p][...].astype(BF16), ab[2 * p + 1][...].astype(BF16), dims)
            s = d if s is None else s + d

        def finish(r):
            if scale != 1.0:
                r = r * scale
            if res_ref is not None:
                r = r + res_ref[...]
            if post == "rmsb":
                @pl.when(pl.program_id(0) == 0)
                def _():
                    out2_ref[...] = jnp.zeros_like(out2_ref)

                xv = pin[0][...]
                rr = lax.rsqrt(jnp.mean(xv * xv, axis=-1, keepdims=True) + EPS)
                xhat = xv * rr
                dxhat = r * pin[1][...]
                out_ref[...] = pin[2][...] + rr * (dxhat - xhat * jnp.mean(dxhat * xhat, axis=-1, keepdims=True))
                out2_ref[...] += jnp.sum(r * xhat, axis=0, keepdims=True)
                return
            out_ref[...] = r.astype(out_ref.dtype)
            if post == "norm":
                rr = lax.rsqrt(jnp.mean(r * r, axis=-1, keepdims=True) + EPS)
                out2_ref[...] = (r * rr * pin[0][...]).astype(BF16)

        if nk == 1:
            finish(s)
            return
        acc = refs[pos]
        k = pl.program_id(2)

        @pl.when(k == 0)
        def _():
            acc[...] = s

        @pl.when(k > 0)
        def _():
            acc[...] += s

        @pl.when(k == nk - 1)
        def _():
            finish(acc[...])

    args, specs = [], []
    for a, a_spec, b, b_spec in pairs:
        args += [a, b]
        specs += [a_spec, b_spec]
    for arr, spec in ([res] if res is not None else []) + list(post_in):
        args.append(arr)
        specs.append(spec)
    sems = ("arbitrary",) * 3 if post == "rmsb" else ("parallel", "parallel", "arbitrary")
    return pl.pallas_call(
        body, out_shape=out_shape, grid=grid, in_specs=specs, out_specs=out_spec,
        scratch_shapes=[] if nk == 1 else [pltpu.VMEM(acc_shape, F32)], name=name,
        compiler_params=_cp(*sems))(*args)


def _rms_fwd(name, x, w):
    T = x.shape[0]

    def body(x_ref, w_ref, o_ref):
        xv = x_ref[...]
        r = lax.rsqrt(jnp.mean(xv * xv, axis=-1, keepdims=True) + EPS)
        o_ref[...] = (xv * r * w_ref[...]).astype(BF16)

    return pl.pallas_call(
        body, out_shape=jax.ShapeDtypeStruct((T, D_MODEL), BF16), grid=(T // ROW_T,),
        in_specs=[pl.BlockSpec((ROW_T, D_MODEL), lambda i: (i, 0)), pl.BlockSpec((1, D_MODEL), lambda i: (0, 0))],
        out_specs=pl.BlockSpec((ROW_T, D_MODEL), lambda i: (i, 0)), name=name, compiler_params=_cp("parallel"))(x, w)


def _loss_grad(name, y, t):
    T = y.shape[0]

    def body(y_ref, t_ref, dy_ref, l_ref):
        @pl.when(pl.program_id(0) == 0)
        def _():
            l_ref[...] = jnp.zeros_like(l_ref)

        e = y_ref[...] - t_ref[...]
        dy_ref[...] = e * (1.0 / D_MODEL)
        l_ref[...] += jnp.sum(e * e, axis=0, keepdims=True)

    row = pl.BlockSpec((ROW_T, D_MODEL), lambda i: (i, 0))
    vec = pl.BlockSpec((1, D_MODEL), lambda i: (0, 0))
    return pl.pallas_call(
        body, out_shape=(jax.ShapeDtypeStruct((T, D_MODEL), F32), jax.ShapeDtypeStruct((1, D_MODEL), F32)),
        grid=(T // ROW_T,), in_specs=[row, row], out_specs=(row, vec), name=name,
        compiler_params=_cp("arbitrary"))(y, t)


def _ffn_gate_up(name, h, wg, wu):
    T = h.shape[0]

    def body(h_ref, wg_ref, wu_ref, dgf_ref, duf_ref, a_ref):
        for r in range(0, ROW_T, HALF_T):
            rows = slice(r, r + HALF_T)
            hv = h_ref[rows, :]
            g = _dot(hv, wg_ref[...], NT)
            u = _dot(hv, wu_ref[...], NT)
            sg = _sigmoid(g)
            silu = g * sg
            dgf_ref[rows, :] = (u * (sg * (1.0 + g * (1.0 - sg)))).astype(BF16)
            duf_ref[rows, :] = silu.astype(BF16)
            a_ref[rows, :] = (silu * u).astype(BF16)

    wspec = pl.BlockSpec((None, FF_SH, D_MODEL), lambda j, i: (j, 0, 0))
    ospec = pl.BlockSpec((None, ROW_T, FF_SH), lambda j, i: (j, i, 0))
    osh = jax.ShapeDtypeStruct((N_SHARD, T, FF_SH), BF16)
    return pl.pallas_call(
        body, out_shape=(osh, osh, osh), grid=(N_SHARD, T // ROW_T),
        in_specs=[pl.BlockSpec((ROW_T, D_MODEL), lambda j, i: (i, 0)), wspec, wspec],
        out_specs=(ospec, ospec, ospec), name=name, compiler_params=_cp("parallel", "parallel"))(h, wg, wu)


def _ffn_dact(name, dx, wd, g, u):
    T = dx.shape[0]

    def body(dx_ref, wd_ref, g_ref, u_ref, dg_ref, du_ref):
        for r in range(0, ROW_T, HALF_T):
            rows = slice(r, r + HALF_T)
            da = 0.5 * _dot(dx_ref[rows, :].astype(BF16), wd_ref[...], NT)
            dg_ref[rows, :] = (da * g_ref[rows, :].astype(F32)).astype(BF16)
            du_ref[rows, :] = (da * u_ref[rows, :].astype(F32)).astype(BF16)

    aspec = pl.BlockSpec((None, ROW_T, FF_SH), lambda j, i: (j, i, 0))
    osh = jax.ShapeDtypeStruct((N_SHARD, T, FF_SH), BF16)
    return pl.pallas_call(
        body, out_shape=(osh, osh), grid=(N_SHARD, T // ROW_T),
        in_specs=[pl.BlockSpec((ROW_T, D_MODEL), lambda j, i: (i, 0)),
                  pl.BlockSpec((None, FF_SH, D_MODEL), lambda j, i: (j, 0, 0)), aspec, aspec],
        out_specs=(aspec, aspec), name=name, compiler_params=_cp("parallel", "parallel"))(dx, wd, g, u)


def _row3():
    return pl.BlockSpec((ROW_T, D_MODEL), lambda i, n, k: (i, 0))


def _vec3():
    return pl.BlockSpec((1, D_MODEL), lambda i, n, k: (0, 0))


def _with_norm(T, next_nw):
    if next_nw is None:
        return dict(out_shape=jax.ShapeDtypeStruct((T, D_MODEL), F32), out_spec=_row3())
    return dict(out_shape=(jax.ShapeDtypeStruct((T, D_MODEL), F32), jax.ShapeDtypeStruct((T, D_MODEL), BF16)),
                out_spec=(_row3(), _row3()), post="norm", post_in=[(next_nw, _vec3())])


def _ffn_fwd(tag, x, h, wg, wu, wd, next_nw):
    T = x.shape[0]
    g, u, a = _ffn_gate_up(tag + "_gu", h, wg, wu)
    if callable(wd):
        wd = wd(a)
    nt = T // ROW_T
    o = _with_norm(T, next_nw)
    xo = _mm(tag + "_down",
             [(a, pl.BlockSpec((None, ROW_T, FF_SH), lambda i, n, k, j=j: (j, i, 0)),
               wd, pl.BlockSpec((None, FF_SH, D_MODEL), lambda i, n, k, j=j: (j, 0, 0))) for j in range(N_SHARD)],
             o.pop("out_shape"), o.pop("out_spec"), (nt, 1, 1), NN, (ROW_T, D_MODEL),
             res=(x, _row3()), scale=0.5, **o)
    return xo, (x, h, g, u, a), wd


def _ffn_bwd(tag, dxo, saved, nw, wg, wu, wd, emit):
    x, h, g, u, a = saved
    T = x.shape[0]
    nt = T // ROW_T
    tkw = min(TK_W, T)
    nw_t = T // tkw
    dg, du = _ffn_dact(tag + "_dact", dxo, wd, g, u)
    actw = lambda f: pl.BlockSpec((None, tkw, FF_SH), f)
    gd = _mm(tag + "_dwd",
             [(a, actw(lambda m, n, k: (m, k, 0)), dxo, pl.BlockSpec((tkw, D_MODEL), lambda m, n, k: (k, 0)))],
             jax.ShapeDtypeStruct((N_SHARD, FF_SH, D_MODEL), BF16),
             pl.BlockSpec((None, FF_SH, D_MODEL), lambda m, n, k: (m, 0, 0)),
             (N_SHARD, 1, nw_t), TN, (FF_SH, D_MODEL), scale=0.5)
    hspec = pl.BlockSpec((tkw, D_MODEL), lambda j, n, k: (k, 0))
    gsh = jax.ShapeDtypeStruct((N_SHARD, FF_SH, D_MODEL), BF16)
    gspec = pl.BlockSpec((None, FF_SH, D_MODEL), lambda j, n, k: (j, 0, 0))
    gg = _mm(tag + "_dwg", [(dg, actw(lambda j, n, k: (j, k, 0)), h, hspec)], gsh, gspec,
             (N_SHARD, 1, nw_t), TN, (FF_SH, D_MODEL))
    gu = _mm(tag + "_dwu", [(du, actw(lambda j, n, k: (j, k, 0)), h, hspec)], gsh, gspec,
             (N_SHARD, 1, nw_t), TN, (FF_SH, D_MODEL))
    dg = emit(gg, gu, gd, dg)
    act = lambda j: pl.BlockSpec((None, ROW_T, FF_SH), lambda i, n, k: (j, i, 0))
    wsp = lambda j: pl.BlockSpec((None, FF_SH, D_MODEL), lambda i, n, k: (j, 0, 0))
    return _mm(tag + "_dh",
               [(dd, act(j), w, wsp(j)) for j in range(N_SHARD) for dd, w in ((dg, wg), (du, wu))],
               (jax.ShapeDtypeStruct((T, D_MODEL), F32), jax.ShapeDtypeStruct((1, D_MODEL), F32)), (_row3(), _vec3()),
               (nt, 1, 1), NN, (ROW_T, D_MODEL), post="rmsb", post_in=[(x, _row3()), (nw, _vec3()), (dxo, _row3())])


def _seq_rows(ref, start, size, S):
    lo, hi = max(start, 0), min(start + size, S)
    parts = [ref[pl.ds(lo, hi - lo), :]]
    if lo > start:
        parts.insert(0, jnp.zeros((lo - start, ref.shape[1]), F32))
    if start + size > hi:
        parts.append(jnp.zeros((start + size - hi, ref.shape[1]), F32))
    return parts[0] if len(parts) == 1 else jnp.concatenate(parts, axis=0)


XBC_CB = COL_XBC // CONV_CT


def _conv_fwd(name, proj, w, b, B):
    T = proj.shape[0]
    S = T // B
    C = CONV_DIM

    def body(x_ref, w_ref, b_ref, o_ref):
        wv = w_ref[...]
        for c in range(S // CONV_R):
            r0 = c * CONV_R
            ch = _seq_rows(x_ref, r0 - PAD_R, CONV_R + PAD_R, S)
            pre = ch[PAD_R:] * wv[3:4] + b_ref[...]
            for s in range(1, CONV_K):
                pre = pre + pltpu.roll(ch, s, axis=0)[PAD_R:] * wv[3 - s:4 - s]
            o_ref[pl.ds(r0, CONV_R), :] = pre * _sigmoid(pre)

    return pl.pallas_call(
        body, out_shape=jax.ShapeDtypeStruct((T, C), F32), grid=(B, C // CONV_CT),
        in_specs=[pl.BlockSpec((S, CONV_CT), lambda bi, ci: (bi, XBC_CB + ci)),
                  pl.BlockSpec((CONV_K, CONV_CT), lambda bi, ci: (0, ci)),
                  pl.BlockSpec((1, CONV_CT), lambda bi, ci: (0, ci))],
        out_specs=pl.BlockSpec((S, CONV_CT), lambda bi, ci: (bi, ci)), name=name,
        compiler_params=_cp("parallel", "parallel"))(proj, w, b)


def _conv_bwd(name, proj, dxs, dB, dC, w, b, dproj, B):
    T = proj.shape[0]
    S = T // B
    C = CONV_DIM
    RW = CONV_R + PAD_R
    nx, nb = dxs.shape[1] // CONV_CT, dB.shape[1] // CONV_CT

    def body(x_ref, dx_in, db_in, dc_in, w_ref, b_ref, buf_ref, dx_ref, dw_ref, db_ref):
        @pl.when(pl.program_id(1) == 0)
        def _():
            dw_ref[...] = jnp.zeros_like(dw_ref)
            db_ref[...] = jnp.zeros_like(db_ref)

        ci = pl.program_id(0)
        wv = w_ref[...]
        dw = [jnp.zeros((1, CONV_CT), F32) for _ in range(CONV_K)]
        db = jnp.zeros((1, CONV_CT), F32)
        for c in range(S // CONV_R):
            r0 = c * CONV_R
            ch = _seq_rows(x_ref, r0 - PAD_R, RW + PAD_R, S)
            xs = [ch[PAD_R:]] + [pltpu.roll(ch, s, axis=0)[PAD_R:] for s in range(1, CONV_K)]
            pre = b_ref[...] + xs[0] * wv[3:4]
            for s in range(1, CONV_K):
                pre = pre + xs[s] * wv[3 - s:4 - s]
            sg = _sigmoid(pre)
            dout = jnp.where(ci < nx, _seq_rows(dx_in, r0, RW, S),
                             jnp.where(ci < nx + nb, _seq_rows(db_in, r0, RW, S), _seq_rows(dc_in, r0, RW, S)))
            dpre = dout * (sg * (1.0 + pre * (1.0 - sg)))
            dx = dpre[:CONV_R] * wv[3:4]
            for s in range(1, CONV_K):
                dx = dx + pltpu.roll(dpre, RW - s, axis=0)[:CONV_R] * wv[3 - s:4 - s]
            dx_ref[pl.ds(r0, CONV_R), :] = dx.astype(BF16)
            dcur = dpre[:CONV_R]
            db = db + jnp.sum(dcur, axis=0, keepdims=True)
            for s in range(CONV_K):
                dw[3 - s] = dw[3 - s] + jnp.sum(dcur * xs[s][:CONV_R], axis=0, keepdims=True)
        db_ref[...] += db
        for k in range(CONV_K):
            dw_ref[k:k + 1, :] += dw[k]

    seq = lambda f: pl.BlockSpec((S, CONV_CT), f)
    return pl.pallas_call(
        body,
        out_shape=(jax.ShapeDtypeStruct(dproj.shape, dproj.dtype), jax.ShapeDtypeStruct((CONV_K, C), F32),
                   jax.ShapeDtypeStruct((1, C), F32)),
        grid=(C // CONV_CT, B),
        in_specs=[seq(lambda ci, bi: (bi, XBC_CB + ci)),
                  seq(lambda ci, bi: (bi, jnp.minimum(ci, nx - 1))),
                  seq(lambda ci, bi: (bi, jnp.clip(ci - nx, 0, nb - 1))),
                  seq(lambda ci, bi: (bi, jnp.clip(ci - nx - nb, 0, nb - 1))),
                  pl.BlockSpec((CONV_K, CONV_CT), lambda ci, bi: (0, ci)),
                  pl.BlockSpec((1, CONV_CT), lambda ci, bi: (0, ci)), ANY],
        out_specs=(seq(lambda ci, bi: (bi, XBC_CB + ci)),
                   pl.BlockSpec((CONV_K, CONV_CT), lambda ci, bi: (0, ci)),
                   pl.BlockSpec((1, CONV_CT), lambda ci, bi: (0, ci))),
        input_output_aliases={6: 0},
        name=name, compiler_params=_cp("parallel", "arbitrary"))(proj, dxs, dB, dC, w, b, dproj)


def _tri_sum(tri, x, dims, tri_first, terms=3):
    out, rest = None, x
    for t in range(terms):
        part = rest.astype(BF16)
        if t + 1 < terms:
            rest = rest - part.astype(F32)
        d = _dot(tri, part, dims) if tri_first else _dot(part, tri, dims)
        out = d if out is None else out + d
    return out


def _total(x):
    return jnp.sum(jnp.sum(x, axis=0, keepdims=True), axis=-1, keepdims=True)


def _ssd_common(dtc_ref, dtr_ref, pcol_ref, prow_ref, b_ref, c_ref):
    L = SSD_L
    bias_c, alog_c = pcol_ref[0:1, :], pcol_ref[1:2, :]
    a_c = -jnp.exp(alog_c)
    dt_c = _softplus(dtc_ref[...] + bias_c)
    row = lax.broadcasted_iota(jnp.int32, (L, L), 0)
    col = lax.broadcasted_iota(jnp.int32, (L, L), 1)
    causal = row >= col
    tri = causal.astype(BF16)
    cum_c = _tri_sum(tri, dt_c * a_c, NN, True)
    a_r = -jnp.exp(prow_ref[:, 1:2])
    dt_r = _softplus(dtr_ref[...] + prow_ref[:, 0:1])
    cum_r = _tri_sum(tri, dt_r * a_r, NT, False)
    bb = b_ref[...].astype(BF16)
    cb = c_ref[...].astype(BF16)
    G = _dot(cb, bb, NT)
    return a_c, dt_c, causal, tri, cum_c, cum_r, bb, cb, G


def _ssd_fwd(name, xc, proj, dtc, dtr, pcol, prow, nw, B):
    T = xc.shape[0]
    S = T // B
    nb = S // SSD_L
    L = SSD_L

    def body(xs_ref, b_ref, c_ref, z_ref, dtc_ref, dtr_ref, pcol_ref, prow_ref, nw_ref, y_ref, yn_ref, hs_ref, H, yo_s):
        @pl.when(pl.program_id(2) == 0)
        def _():
            H[...] = jnp.zeros_like(H)

        a_c, dt_c, causal, tri, cum_c, cum_r, bb, cb, G = _ssd_common(dtc_ref, dtr_ref, pcol_ref, prow_ref, b_ref, c_ref)
        dsk = pcol_ref[2:3, :]
        clast = cum_c[L - 1:L, :]
        bf = b_ref[...]
        for h in range(4):
            hs_ref[h] = H[h]
            yo_s[h] = _dot(cb, H[h].astype(BF16), NN)
        for h in range(4):
            sl = slice(HEAD_DIM * h, HEAD_DIM * (h + 1))
            cc = cum_c[:, h:h + 1]
            lm = jnp.exp(jnp.where(causal, cc - cum_r[h:h + 1, :], NEG))
            M = (G * lm).astype(BF16)
            xh = xs_ref[:, sl]
            Xb = (xh * dt_c[:, h:h + 1]).astype(BF16)
            Hh = H[h]
            y = _dot(M, Xb, NN) + jnp.exp(cc) * yo_s[h]
            y_ref[:, sl] = y + dsk[:, h:h + 1] * xh
            cl = clast[:, h:h + 1]
            Bw = (bf * jnp.exp(cl - cc)).astype(BF16)
            H[h] = jnp.exp(cl) * Hh + _dot(Bw, Xb, TN)
        zv = z_ref[...]
        y2 = y_ref[...] * (zv * _sigmoid(zv))
        r = lax.rsqrt(jnp.mean(y2 * y2, axis=-1, keepdims=True) + EPS)
        yn_ref[...] = (y2 * r * nw_ref[...]).astype(BF16)

    rowi = lambda b, g, i: b * nb + i
    grp = pl.BlockSpec((L, GROUP_W), lambda b, g, i: (rowi(b, g, i), g))
    return pl.pallas_call(
        body,
        out_shape=(jax.ShapeDtypeStruct((T, 1024), F32), jax.ShapeDtypeStruct((T, 1024), BF16),
                   jax.ShapeDtypeStruct((B, SSD_GROUPS, nb, 4, SSD_STATE, HEAD_DIM), F32)),
        grid=(B, SSD_GROUPS, nb),
        in_specs=[grp,
                  pl.BlockSpec((L, SSD_STATE), lambda b, g, i: (rowi(b, g, i), 8 + g)),
                  pl.BlockSpec((L, SSD_STATE), lambda b, g, i: (rowi(b, g, i), 12 + g)),
                  grp,
                  pl.BlockSpec((None, L, 4), lambda b, g, i: (g, rowi(b, g, i), 0)),
                  pl.BlockSpec((None, 4, L), lambda b, g, i: (g, 0, rowi(b, g, i))),
                  pl.BlockSpec((None, 3, 4), lambda b, g, i: (g, 0, 0)),
                  pl.BlockSpec((None, 4, 3), lambda b, g, i: (g, 0, 0)),
                  pl.BlockSpec((1, GROUP_W), lambda b, g, i: (0, g))],
        out_specs=(grp, grp,
                   pl.BlockSpec((None, None, None, 4, SSD_STATE, HEAD_DIM), lambda b, g, i: (b, g, i, 0, 0, 0))),
        scratch_shapes=[pltpu.VMEM((4, SSD_STATE, HEAD_DIM), F32), pltpu.VMEM((4, L, HEAD_DIM), F32)], name=name,
        compiler_params=_cp("parallel", "parallel", "arbitrary"))(xc, xc, xc, proj, dtc, dtr, pcol, prow, nw)


def _ssd_bwd(name, dyn, Y, xc, proj, dtc, dtr, pcol, prow, nw, hs, dproj, B):
    T = xc.shape[0]
    S = T // B
    nb = S // SSD_L
    L = SSD_L

    def body(dyn_ref, y_ref, xs_ref, b_ref, c_ref, z_ref, dtc_ref, dtr_ref, pcol_ref, prow_ref, nw_ref, hs_ref, buf_ref,
             dxs_ref, db_ref, dc_ref, dz_ref, ddt_ref, dpar_ref, dnw_ref, dH, dm_s, dxo_s, ea_s, ex_s):
        @pl.when(pl.program_id(2) == 0)
        def _():
            dH[...] = jnp.zeros_like(dH)
            dpar_ref[...] = jnp.zeros_like(dpar_ref)
            dnw_ref[...] = jnp.zeros_like(dnw_ref)

        a_c, dt_c, causal, tri, cum_c, cum_r, bb, cb, G = _ssd_common(dtc_ref, dtr_ref, pcol_ref, prow_ref, b_ref, c_ref)
        dsk = pcol_ref[2:3, :]
        clast = cum_c[L - 1:L, :]
        bf = b_ref[...]
        cf = c_ref[...]
        Yv = y_ref[...]
        zv = z_ref[...]
        sz = _sigmoid(zv)
        silu = zv * sz
        y2 = Yv * silu
        r = lax.rsqrt(jnp.mean(y2 * y2, axis=-1, keepdims=True) + EPS)
        yhat = y2 * r
        dyv = dyn_ref[...]
        dnw_ref[...] += jnp.sum(dyv * yhat, axis=0, keepdims=True)
        dyhat = dyv * nw_ref[...]
        dy2 = r * (dyhat - yhat * jnp.mean(dyhat * yhat, axis=-1, keepdims=True))
        dY = dy2 * silu
        dz_ref[...] = (dy2 * Yv * (sz * (1.0 + zv * (1.0 - sz)))).astype(BF16)

        lane4 = lax.broadcasted_iota(jnp.int32, (1, 4), 1)
        dG = jnp.zeros((L, L), F32)
        dBs = jnp.zeros((L, SSD_STATE), F32)
        dCs = jnp.zeros((L, SSD_STATE), F32)
        ddsk = jnp.zeros((1, 4), F32)
        dcl = jnp.zeros((1, 4), F32)
        for h in range(4):
            sl = slice(HEAD_DIM * h, HEAD_DIM * (h + 1))
            xb = (xs_ref[:, sl] * dt_c[:, h:h + 1]).astype(BF16)
            dm_s[h] = _dot(dY[:, sl].astype(BF16), xb, NT)
            dxo_s[h] = _dot(bb, dH[h].astype(BF16), NN)
        for h in range(4):
            sl = slice(HEAD_DIM * h, HEAD_DIM * (h + 1))
            onehot = (lane4 == h).astype(F32)
            cc = cum_c[:, h:h + 1]
            cl = clast[:, h:h + 1]
            lm = jnp.exp(jnp.where(causal, cc - cum_r[h:h + 1, :], NEG))
            M = (G * lm).astype(BF16)
            xh = xs_ref[:, sl]
            dth = dt_c[:, h:h + 1]
            X = xh * dth
            Xb = X.astype(BF16)
            dYh = dY[:, sl]
            dYb = dYh.astype(BF16)
            Hb = hs_ref[h].astype(BF16)
            dHh = dH[h]
            dHb = dHh.astype(BF16)
            alpha = jnp.exp(cc)
            beta = jnp.exp(cl - cc)
            dXoff = beta * dxo_s[h]
            dX = _dot(M, dYb, TN) + dXoff
            dG = dG + dm_s[h] * lm
            dCs = dCs + _dot((alpha * dYh).astype(BF16), Hb, NT)
            dBs = dBs + _dot((beta * X).astype(BF16), dHb, NT)
            ypre = Yv[:, sl] - dsk[:, h:h + 1] * xh
            ea_s[:, sl] = dYb.astype(F32) * ypre - Xb.astype(F32) * dX
            ex_s[:, sl] = dX * xh
            dcl_h = (_total(dHh * (jnp.exp(cl) * hs_ref[h])) + _total(Xb.astype(F32) * dXoff))
            dcl = dcl + dcl_h * onehot
            ddsk = ddsk + _total(dYh * xh) * onehot
            dxs_ref[:, sl] = dsk[:, h:h + 1] * dYh + dX * dth
            dH[h] = jnp.exp(cl) * dHh + _dot((alpha * cf).astype(BF16), dYb, TN)
        dGb = dG.astype(BF16)
        dc_ref[...] = _dot(dGb, bb, NN) + dCs
        db_ref[...] = _dot(dGb, cb, TN) + dBs
        feat = lax.broadcasted_iota(jnp.int32, (GROUP_W, 4), 0)
        head = lax.broadcasted_iota(jnp.int32, (GROUP_W, 4), 1) * HEAD_DIM
        sel = ((feat >= head) & (feat < head + HEAD_DIM)).astype(BF16)
        dA = _tri_sum(sel, ea_s[...], NN, False)
        ddtx = _tri_sum(sel, ex_s[...], NN, False)
        last = lax.broadcasted_iota(jnp.int32, (L, 1), 0) == L - 1
        dA = dA + jnp.where(last, dcl, 0.0)
        dadt = _tri_sum(tri, dA, TN, True)
        ddt = dadt * a_c + ddtx
        d_a = jnp.sum(dadt * dt_c, axis=0, keepdims=True)
        ddraw = ddt * _sigmoid(dtc_ref[...] + pcol_ref[0:1, :])
        ddt_ref[...] = ddraw
        dpar_ref[0:1, :] += jnp.sum(ddraw, axis=0, keepdims=True)
        dpar_ref[1:2, :] += d_a * a_c
        dpar_ref[2:3, :] += ddsk

    rowi = lambda b, g, i: b * nb + (nb - 1 - i)
    grp = pl.BlockSpec((L, GROUP_W), lambda b, g, i: (rowi(b, g, i), g))
    st = pl.BlockSpec((L, SSD_STATE), lambda b, g, i: (rowi(b, g, i), g))
    f = jax.ShapeDtypeStruct
    return pl.pallas_call(
        body,
        out_shape=(f((T, 1024), F32), f((T, 512), F32), f((T, 512), F32), f(dproj.shape, dproj.dtype),
                   f((SSD_GROUPS, T, 4), F32), f((B, SSD_GROUPS, 3, 4), F32), f((B, 1, 1024), F32)),
        grid=(B, SSD_GROUPS, nb),
        in_specs=[grp, grp, grp,
                  pl.BlockSpec((L, SSD_STATE), lambda b, g, i: (rowi(b, g, i), 8 + g)),
                  pl.BlockSpec((L, SSD_STATE), lambda b, g, i: (rowi(b, g, i), 12 + g)),
                  grp,
                  pl.BlockSpec((None, L, 4), lambda b, g, i: (g, rowi(b, g, i), 0)),
                  pl.BlockSpec((None, 4, L), lambda b, g, i: (g, 0, rowi(b, g, i))),
                  pl.BlockSpec((None, 3, 4), lambda b, g, i: (g, 0, 0)),
                  pl.BlockSpec((None, 4, 3), lambda b, g, i: (g, 0, 0)),
                  pl.BlockSpec((1, GROUP_W), lambda b, g, i: (0, g)),
                  pl.BlockSpec((None, None, None, 4, SSD_STATE, HEAD_DIM), lambda b, g, i: (b, g, nb - 1 - i, 0, 0, 0)),
                  ANY],
        out_specs=(grp, st, st, grp,
                   pl.BlockSpec((None, L, 4), lambda b, g, i: (g, rowi(b, g, i), 0)),
                   pl.BlockSpec((None, None, 3, 4), lambda b, g, i: (b, g, 0, 0)),
                   pl.BlockSpec((None, 1, GROUP_W), lambda b, g, i: (b, 0, g))),
        input_output_aliases={12: 3},
        scratch_shapes=[pltpu.VMEM((4, SSD_STATE, HEAD_DIM), F32), pltpu.VMEM((4, L, L), F32),
                        pltpu.VMEM((4, L, HEAD_DIM), F32), pltpu.VMEM((L, GROUP_W), F32),
                        pltpu.VMEM((L, GROUP_W), F32)], name=name,
        compiler_params=_cp("parallel", "parallel", "arbitrary"))(
            dyn, Y, xc, xc, xc, proj, dtc, dtr, pcol, prow, nw, hs, dproj)


def _head_sel():
    sel = (np.arange(1024)[:, None] // HEAD_DIM == np.arange(ATT_HEADS)[None, :]).astype(np.float32)
    return jnp.asarray(sel, BF16), jnp.asarray(sel.T, BF16)


def _head_rms(xv, sel, selT):
    ms = _tri_sum(sel, xv * xv, NN, False, 1) * (1.0 / HEAD_DIM)
    return _tri_sum(selT, lax.rsqrt(ms + EPS), NN, False, 2)


def _headnorm_fwd(name, proj, col_block, w):
    T = proj.shape[0]
    sel, selT = _head_sel()

    def body(x_ref, w_ref, sel_ref, selT_ref, o_ref):
        xv = x_ref[...]
        o_ref[...] = (xv * _head_rms(xv, sel_ref[...], selT_ref[...]) * w_ref[...]).astype(BF16)

    full = lambda shp: pl.BlockSpec(shp, lambda i: (0, 0))
    return pl.pallas_call(
        body, out_shape=jax.ShapeDtypeStruct((T, 1024), BF16), grid=(T // ROW_T,),
        in_specs=[pl.BlockSpec((ROW_T, 1024), lambda i: (i, col_block)), full((1, 1024)), full((1024, ATT_HEADS)),
                  full((ATT_HEADS, 1024))],
        out_specs=pl.BlockSpec((ROW_T, 1024), lambda i: (i, 0)), name=name, compiler_params=_cp("parallel"))(
            proj, jnp.tile(w, (1, ATT_HEADS)), sel, selT)


def _headnorm_bwd(name, dn, proj, col_block, w, dproj):
    T = proj.shape[0]
    sel, selT = _head_sel()

    def body(dn_ref, x_ref, w_ref, sel_ref, selT_ref, buf_ref, dx_ref, dw_ref):
        @pl.when(pl.program_id(0) == 0)
        def _():
            dw_ref[...] = jnp.zeros_like(dw_ref)

        xv = x_ref[...]
        sl, slT = sel_ref[...], selT_ref[...]
        rb = _head_rms(xv, sl, slT)
        xhat = xv * rb
        dnv = dn_ref[...]
        dxhat = dnv * w_ref[...]
        mean = _tri_sum(slT, _tri_sum(sl, dxhat * xhat, NN, False, 2) * (1.0 / HEAD_DIM), NN, False, 2)
        dx_ref[...] = (rb * (dxhat - xhat * mean)).astype(BF16)
        dw_ref[...] += jnp.sum(dnv * xhat, axis=0, keepdims=True)

    here = pl.BlockSpec((ROW_T, 1024), lambda i: (i, col_block))
    full = lambda shp: pl.BlockSpec(shp, lambda i: (0, 0))
    dx, dw = pl.pallas_call(
        body, out_shape=(jax.ShapeDtypeStruct(dproj.shape, dproj.dtype), jax.ShapeDtypeStruct((1, 1024), F32)),
        grid=(T // ROW_T,),
        in_specs=[pl.BlockSpec((ROW_T, 1024), lambda i: (i, 0)), here, full((1, 1024)), full((1024, ATT_HEADS)),
                  full((ATT_HEADS, 1024)), ANY],
        out_specs=(here, full((1, 1024))), input_output_aliases={5: 0},
        name=name, compiler_params=_cp("arbitrary"))(dn, proj, jnp.tile(w, (1, ATT_HEADS)), sel, selT, dproj)
    return dx, jnp.sum(dw.reshape(ATT_HEADS, HEAD_DIM), axis=0, keepdims=True)


def _att_bias(nq):
    j = np.arange(ATT_B)[:, None]
    i = np.arange(ATT_B)[None, :]
    out = np.empty((nq, ATT_B, ATT_B), np.float32)
    for dblk in range(nq):
        dl = ATT_B * dblk + i - j
        cnt = ((dl >= 0) & (dl <= 128)).astype(np.float32)
        cnt += ((dl >= 0) & (dl % 4 == 0) & (dl <= 512))
        cnt += ((dl >= 0) & (dl % 16 == 0) & (dl <= 2048))
        out[dblk] = np.where(cnt > 0, np.log(np.maximum(cnt, 1.0)), NEG)
    return jnp.asarray(out)


def _row_pair(nq):
    def f(r, c):
        first = c <= r
        return jnp.where(first, r, nq - 1 - r), jnp.where(first, c, c - (r + 1))
    return f


def _col_pair(nq):
    def f(r, c):
        first = c < nq - r
        kj = jnp.where(first, r, nq - 1 - r)
        return jnp.where(first, r + c, nq - 1 - r + (c - (nq - r))), kj
    return f


ATT_SCALE = 1.0 / math.sqrt(HEAD_DIM)
ATT_HS = 4
ATT_W = ATT_HS * HEAD_DIM


def _att_maps(nq, qk):
    return dict(
        q_tok=lambda b, g, r, c: (b * nq + qk(r, c)[0], g),
        k_tok=lambda b, g, r, c: (b * nq + qk(r, c)[1], g),
        v_tok=lambda b, g, r, c: (b * nq + qk(r, c)[1], COL_V // ATT_W + g),
        q_feat=lambda b, g, r, c: (g, b * nq + qk(r, c)[0]),
        k_feat=lambda b, g, r, c: (g, b * nq + qk(r, c)[1]),
        bias=lambda b, g, r, c: (qk(r, c)[0] - qk(r, c)[1], 0, 0),
        lse=lambda b, g, r, c: (g, 0, b * nq + qk(r, c)[0]),
        do_tok=lambda b, g, r, c: (b * nq + qk(r, c)[0], ATT_HS + g))


def _att_fwd(name, kn, qT, vT, bias, B):
    T = kn.shape[0]
    nq = (T // B) // ATT_B
    qk = _row_pair(nq)
    mp = _att_maps(nq, qk)

    def body(k_ref, qT_ref, vT_ref, bias_ref, oT_ref, lse_ref, m_s, l_s, acc_s, s_s):
        qi, kj = qk(pl.program_id(2), pl.program_id(3))

        @pl.when(kj == 0)
        def _():
            m_s[...] = jnp.full_like(m_s, NEG)
            l_s[...] = jnp.zeros_like(l_s)
            acc_s[...] = jnp.zeros_like(acc_s)

        bv = bias_ref[...]
        for h in range(ATT_HS):
            rs = slice(HEAD_DIM * h, HEAD_DIM * (h + 1))
            s_s[h] = _dot(k_ref[:, rs], qT_ref[rs, :], NN)
        for h in range(ATT_HS):
            rs = slice(HEAD_DIM * h, HEAD_DIM * (h + 1))
            s = s_s[h] + bv
            m_prev = m_s[h:h + 1, :]
            m_new = jnp.maximum(m_prev, jnp.max(s, axis=0, keepdims=True))
            alpha = jnp.exp(m_prev - m_new)
            p = jnp.exp(s - m_new)
            l_s[h:h + 1, :] = alpha * l_s[h:h + 1, :] + jnp.sum(p, axis=0, keepdims=True)
            acc_s[rs, :] = alpha * acc_s[rs, :] + _dot(vT_ref[rs, :], p.astype(BF16), NN)
            m_s[h:h + 1, :] = m_new

        @pl.when(kj == qi)
        def _():
            for h in range(ATT_HS):
                rs = slice(HEAD_DIM * h, HEAD_DIM * (h + 1))
                oT_ref[rs, :] = (acc_s[rs, :] / l_s[h:h + 1, :]).astype(BF16)
            lse_ref[...] = m_s[...] + jnp.log(l_s[...])

    tok = (ATT_B, ATT_W)
    feat = (ATT_W, ATT_B)
    return pl.pallas_call(
        body,
        out_shape=(jax.ShapeDtypeStruct((1024, T), BF16), jax.ShapeDtypeStruct((ATT_HEADS // ATT_HS, ATT_HS, T), F32)),
        grid=(B, ATT_HEADS // ATT_HS, nq // 2, nq + 1),
        in_specs=[pl.BlockSpec(tok, mp["k_tok"]), pl.BlockSpec(feat, mp["q_feat"]), pl.BlockSpec(feat, mp["k_feat"]),
                  pl.BlockSpec((None, ATT_B, ATT_B), mp["bias"])],
        out_specs=(pl.BlockSpec(feat, mp["q_feat"]), pl.BlockSpec((None, ATT_HS, ATT_B), mp["lse"])),
        scratch_shapes=[pltpu.VMEM((ATT_HS, ATT_B), F32), pltpu.VMEM((ATT_HS, ATT_B), F32),
                        pltpu.VMEM((ATT_W, ATT_B), F32), pltpu.VMEM((ATT_HS, ATT_B, ATT_B), F32)],
        name=name, compiler_params=_cp("parallel", "parallel", "arbitrary", "arbitrary"))(kn, qT, vT, bias)


def _att_scores(k_ref, qT_ref, v_ref, doT_ref, s_s, dp_s):
    for h in range(ATT_HS):
        rs = slice(HEAD_DIM * h, HEAD_DIM * (h + 1))
        s_s[h] = _dot(k_ref[:, rs], qT_ref[rs, :], NN)
        dp_s[h] = _dot(v_ref[:, rs].astype(BF16), doT_ref[rs, :].astype(BF16), NN)


def _att_p_ds(s_s, dp_s, doT_ref, oT_ref, lse_ref, bv, h):
    rs = slice(HEAD_DIM * h, HEAD_DIM * (h + 1))
    delta = jnp.sum(doT_ref[rs, :] * oT_ref[rs, :].astype(F32), axis=0, keepdims=True)
    p = jnp.exp(s_s[h] + bv - lse_ref[h:h + 1, :])
    return p, p * (dp_s[h] - delta)


def _att_bwd(name, kn, qT, proj, qn, knT, bias, doT, oT, lse, dyn, dproj, B):
    T = kn.shape[0]
    S = T // B
    nq = S // ATT_B
    qk = _col_pair(nq)
    mp = _att_maps(nq, qk)

    def body(k_ref, qT_ref, v_ref, q_ref, kT_ref, bias_ref, doT_ref, oT_ref, lse_ref, do_ref, buf_ref,
             dqT_ref, dk_ref, dv_ref, dk_s, dv_s, dq_s, s_s, dp_s):
        r, c = pl.program_id(2), pl.program_id(3)
        qi, kj = qk(r, c)

        @pl.when((r == 0) & (c == 0))
        def _():
            dq_s[...] = jnp.zeros_like(dq_s)

        @pl.when(qi == kj)
        def _():
            dk_s[...] = jnp.zeros_like(dk_s)
            dv_s[...] = jnp.zeros_like(dv_s)

        bv = bias_ref[...]
        _att_scores(k_ref, qT_ref, v_ref, doT_ref, s_s, dp_s)
        dq_blk = dq_s.at[qi]
        for h in range(ATT_HS):
            rs = slice(HEAD_DIM * h, HEAD_DIM * (h + 1))
            p, ds = _att_p_ds(s_s, dp_s, doT_ref, oT_ref, lse_ref, bv, h)
            dsb = ds.astype(BF16)
            dv_s[h] += _dot(p.astype(BF16), do_ref[:, rs].astype(BF16), NN)
            dk_s[h] += _dot(dsb, q_ref[:, rs], NN)
            dq_blk[rs, :] += _dot(kT_ref[rs, :], dsb, NN)

        @pl.when(qi == nq - 1)
        def _():
            for h in range(ATT_HS):
                rs = slice(HEAD_DIM * h, HEAD_DIM * (h + 1))
                dk_ref[:, rs] = dk_s[h] * ATT_SCALE
                dv_ref[:, rs] = dv_s[h].astype(BF16)

        @pl.when((r == nq // 2 - 1) & (c == nq))
        def _():
            for q in range(nq):
                dqT_ref[:, ATT_B * q:ATT_B * (q + 1)] = dq_s[q] * ATT_SCALE

    tok = (ATT_B, ATT_W)
    feat = (ATT_W, ATT_B)
    v_cb = COL_V // ATT_W
    return pl.pallas_call(
        body,
        out_shape=(jax.ShapeDtypeStruct((1024, T), F32), jax.ShapeDtypeStruct((T, 1024), F32),
                   jax.ShapeDtypeStruct(dproj.shape, dproj.dtype)),
        grid=(B, ATT_HEADS // ATT_HS, nq // 2, nq + 1),
        in_specs=[pl.BlockSpec(tok, mp["k_tok"]), pl.BlockSpec(feat, mp["q_feat"]), pl.BlockSpec(tok, mp["v_tok"]),
                  pl.BlockSpec(tok, mp["q_tok"]), pl.BlockSpec(feat, mp["k_feat"]),
                  pl.BlockSpec((None, ATT_B, ATT_B), mp["bias"]),
                  pl.BlockSpec(feat, mp["q_feat"]), pl.BlockSpec(feat, mp["q_feat"]),
                  pl.BlockSpec((None, ATT_HS, ATT_B), mp["lse"]), pl.BlockSpec(tok, mp["do_tok"]), ANY],
        out_specs=(pl.BlockSpec((ATT_W, S), lambda b, g, r, c: (g, b)),
                   pl.BlockSpec(tok, mp["k_tok"]),
                   pl.BlockSpec(tok, lambda b, g, r, c: (b * nq + qk(r, c)[1], v_cb + g))),
        input_output_aliases={10: 2},
        scratch_shapes=[pltpu.VMEM((ATT_HS, ATT_B, HEAD_DIM), F32), pltpu.VMEM((ATT_HS, ATT_B, HEAD_DIM), F32),
                        pltpu.VMEM((nq, ATT_W, ATT_B), F32),
                        pltpu.VMEM((ATT_HS, ATT_B, ATT_B), F32), pltpu.VMEM((ATT_HS, ATT_B, ATT_B), F32)],
        name=name, compiler_params=_cp("parallel", "parallel", "arbitrary", "arbitrary"))(
            kn, qT, proj, qn, knT, bias, doT, oT, lse, dyn, dproj)


def _group_cols(v):
    return v.reshape(SSD_GROUPS, 4)


def _ssd_params(p):
    rows = jnp.stack([_group_cols(p["dt_bias"]), _group_cols(p["a_log"]), _group_cols(p["d_skip"])], axis=1)
    return rows, jnp.swapaxes(rows, 1, 2)


def _dymix(name, dx, wout):
    T = dx.shape[0]

    def body(dx_ref, w_ref, o_ref):
        dxb = dx_ref[...].astype(BF16)
        for n in range(N_SHARD):
            o_ref[:, MIX_SH * n:MIX_SH * (n + 1)] = _dot(dxb, w_ref[n], NT)

    return pl.pallas_call(
        body, out_shape=jax.ShapeDtypeStruct((T, MIX_W), F32), grid=(T // ROW_T,),
        in_specs=[pl.BlockSpec((ROW_T, D_MODEL), lambda i: (i, 0)),
                  pl.BlockSpec((N_SHARD, MIX_SH, D_MODEL), lambda i: (0, 0, 0))],
        out_specs=pl.BlockSpec((ROW_T, MIX_W), lambda i: (i, 0)), name=name, compiler_params=_cp("parallel"))(dx, wout)


def _mixer_fwd(tag, x1, h2, p, weights, bias, B):
    T = x1.shape[0]
    S = T // B
    nt = T // ROW_T
    wi = weights("win", h2)
    win, cw = wi["win"], wi["cw"]
    proj = _mm(tag + "_proj",
               [(h2, pl.BlockSpec((ROW_T, D_MODEL), lambda j, i, k: (i, 0)),
                 win, pl.BlockSpec((D_MODEL, PROJ_TN), lambda j, i, k: (0, j)))],
               jax.ShapeDtypeStruct((T, IN_PAD), F32), pl.BlockSpec((ROW_T, PROJ_TN), lambda j, i, k: (i, j)),
               (IN_PAD // PROJ_TN, nt, 1), NN, (ROW_T, PROJ_TN))
    xc = _conv_fwd(tag + "_conv", proj, cw, p["conv_b"][None], B)
    dtraw = proj[:, COL_DT:COL_DT + SSD_HEADS].reshape(T, SSD_GROUPS, 4)
    dtc = jnp.transpose(dtraw, (1, 0, 2))
    dtr = jnp.transpose(dtraw, (1, 2, 0))
    pcol, prow = _ssd_params(p)
    Y, y_ssd, hs = _ssd_fwd(tag + "_ssd", xc, proj, dtc, dtr, pcol, prow, p["ssd_norm"][None], B)
    qn = _headnorm_fwd(tag + "_qn", proj, COL_Q // 1024, p["q_norm"][None])
    kn = _headnorm_fwd(tag + "_kn", proj, COL_K // 1024, p["k_norm"][None])
    qT = (qn * ATT_SCALE).T
    oT, lse = _att_fwd(tag + "_att", kn, qT, proj[:, COL_V:COL_V + 1024].T.astype(BF16), bias, B)
    ymix = jnp.concatenate([y_ssd, oT.T], axis=1)
    rest = weights("rest", ymix)
    o = _with_norm(T, p["ffn2_norm"][None])
    x2, h3 = _mm(tag + "_out",
                 [(ymix, pl.BlockSpec((ROW_T, MIX_SH), lambda i, n, k, j=j: (i, j)),
                   rest["wout"], pl.BlockSpec((None, MIX_SH, D_MODEL), lambda i, n, k, j=j: (j, 0, 0)))
                  for j in range(N_SHARD)],
                 o.pop("out_shape"), o.pop("out_spec"), (nt, 1, 1), NN, (ROW_T, D_MODEL), res=(x1, _row3()), **o)
    saved = dict(x1=x1, h2=h2, proj=proj, xc=xc, dtc=dtc, dtr=dtr, Y=Y, hs=hs,
                 qn=qn, kn=kn, qT=qT, oT=oT, lse=lse, ymix=ymix, win=win, cw=cw, wout=rest["wout"])
    return x2, h3, saved


def _mixer_bwd(tag, dx2, sv, p, bias, B):
    T = dx2.shape[0]
    S = T // B
    nt = T // ROW_T
    sg = {}
    dymix = _dymix(tag + "_dymix", dx2, sv["wout"])
    tkw = min(TK_W, T)
    gwout = _mm(tag + "_dwout",
                [(sv["ymix"], pl.BlockSpec((tkw, MIX_SH), lambda m, n, k: (k, m)),
                  dx2, pl.BlockSpec((tkw, D_MODEL), lambda m, n, k: (k, 0)))],
                jax.ShapeDtypeStruct((N_SHARD, MIX_SH, D_MODEL), BF16),
                pl.BlockSpec((None, MIX_SH, D_MODEL), lambda m, n, k: (m, 0, 0)),
                (N_SHARD, 1, T // tkw), TN, (MIX_SH, D_MODEL))
    proj = sv["proj"]
    doT = dymix[:, 1024:].T
    dproj = lax.empty((T, IN_PAD), BF16)
    dqT, dkn, dproj = _att_bwd(tag + "_attb", sv["kn"], sv["qT"], proj, sv["qn"], sv["kn"].T, bias, doT, sv["oT"],
                               sv["lse"], dymix, dproj, B)
    dproj, sg["q_norm"] = _headnorm_bwd(tag + "_qnb", dqT.T, proj, COL_Q // 1024, p["q_norm"][None], dproj)
    dproj, sg["k_norm"] = _headnorm_bwd(tag + "_knb", dkn, proj, COL_K // 1024, p["k_norm"][None], dproj)
    pcol, prow = _ssd_params(p)
    dxs, dB, dC, dproj, ddt, dpar, dnw = _ssd_bwd(tag + "_ssdb", dymix, sv["Y"], sv["xc"], proj, sv["dtc"], sv["dtr"],
                                                  pcol, prow, p["ssd_norm"][None], sv["hs"], dproj, B)
    dpar = jnp.sum(dpar, axis=0)
    sg["dt_bias"] = dpar[:, 0, :].reshape(SSD_HEADS)
    sg["a_log"] = dpar[:, 1, :].reshape(SSD_HEADS)
    sg["d_skip"] = dpar[:, 2, :].reshape(SSD_HEADS)
    sg["ssd_norm"] = jnp.sum(dnw, axis=0)
    dproj, sg["conv_w"], sg["conv_b"] = _conv_bwd(tag + "_convb", proj, dxs, dB, dC, sv["cw"], p["conv_b"][None],
                                                  dproj, B)
    ddt16 = jnp.transpose(ddt, (1, 0, 2)).reshape(T, SSD_HEADS)
    dproj = lax.dynamic_update_slice(dproj, jnp.pad(ddt16, ((0, 0), (0, IN_PAD - COL_DT - SSD_HEADS))).astype(BF16),
                                     (0, COL_DT))
    win = sv["win"]
    gwin = _mm(tag + "_dwin",
               [(sv["h2"], pl.BlockSpec((tkw, D_MODEL), lambda n, m, k: (k, 0)),
                 dproj, pl.BlockSpec((tkw, PROJ_TN), lambda n, m, k: (k, n)))],
               jax.ShapeDtypeStruct((D_MODEL, IN_PAD), BF16), pl.BlockSpec((D_MODEL, PROJ_TN), lambda n, m, k: (0, n)),
               (IN_PAD // PROJ_TN, 1, T // tkw), TN, (D_MODEL, PROJ_TN))
    dx1, sg["mix_norm"] = _mm(
        tag + "_dh2",
        [(dproj, pl.BlockSpec((ROW_T, PROJ_TN), lambda i, n, k, j=j: (i, j)),
          win, pl.BlockSpec((D_MODEL, PROJ_TN), lambda i, n, k, j=j: (0, j))) for j in range(IN_PAD // PROJ_TN)],
        (jax.ShapeDtypeStruct((T, D_MODEL), F32), jax.ShapeDtypeStruct((1, D_MODEL), F32)), (_row3(), _vec3()),
        (nt, 1, 1), NT, (ROW_T, D_MODEL), post="rmsb",
        post_in=[(sv["x1"], _row3()), (p["mix_norm"][None], _vec3()), (dx2, _row3())])
    return dx1, sg, gwout, gwin


def _win_pack(w):
    return jnp.concatenate([w[:, :3072], w[:, 3088:], w[:, 3072:3088],
                            jnp.zeros((w.shape[0], IN_PAD - IN_PROJ), w.dtype)], axis=1)


def _win_unpack(g):
    return jnp.concatenate([g[:, :3072], g[:, COL_DT:COL_DT + SSD_HEADS], g[:, 3072:COL_DT]], axis=1)


DT_LO = IN_SH * 2 - COL_Q


def _win_from_shards(sh):
    main = IN_SH - DT_LO
    return jnp.concatenate([sh[0], sh[1][:, :main], sh[2][:, SSD_HEADS - DT_LO:], sh[3], sh[1][:, main:],
                            sh[2][:, :SSD_HEADS - DT_LO], jnp.zeros((sh.shape[1], IN_PAD - IN_PROJ), sh.dtype)], axis=1)


def _win_to_shards(g):
    main = IN_SH - DT_LO
    a, b = IN_SH + main, IN_SH + 2 * main
    return jnp.stack([g[:, :IN_SH],
                      jnp.concatenate([g[:, IN_SH:a], g[:, COL_DT:COL_DT + DT_LO]], axis=1),
                      jnp.concatenate([g[:, COL_DT + DT_LO:COL_DT + SSD_HEADS], g[:, a:b]], axis=1),
                      g[:, b:COL_DT]])


def _local_step(x, target, small, weights, scatter, B):
    T = x.shape[0]
    bias = _att_bias((T // B) // ATT_B)
    saved = []
    xl = x
    hl = _rms_fwd("l0f1_rms", x, small["ffn1_norm"][0][None])
    for l in range(DEPTH):
        tag = "l%d" % l
        p = {k: v[l] for k, v in small.items()}
        w1 = weights(l, "ffn1", hl)
        (x1, h2), ffn1, d1 = _ffn_fwd(tag + "f1", xl, hl, w1["g1"], w1["u1"],
                                      lambda after, l=l: weights(l, "ffn1d", after)["d1"], p["mix_norm"][None])
        x2, h3, sv = _mixer_fwd(tag, x1, h2, p, functools.partial(weights, l), bias, B)
        w2 = weights(l, "rest", x2)
        nxt = small["ffn1_norm"][l + 1][None] if l + 1 < DEPTH else None
        xo, ffn2, _ = _ffn_fwd(tag + "f2", x2, h3, w2["g2"], w2["u2"], w2["d2"], nxt)
        xl, hl = xo if nxt is not None else (xo, None)
        saved.append((ffn1, sv, ffn2, dict(g1=w1["g1"], u1=w1["u1"], d1=d1), w2))
    d, lsum = _loss_grad("loss", xl, target)
    sgrads = [None] * DEPTH
    for l in reversed(range(DEPTH)):
        tag = "l%db" % l
        p = {k: v[l] for k, v in small.items()}
        ffn1, sv, ffn2, w1, w2 = saved[l]
        sg = {}
        d, sg["ffn2_norm"] = _ffn_bwd(tag + "f2", d, ffn2, p["ffn2_norm"][None], w2["g2"], w2["u2"], w2["d2"],
                                      lambda gg, gu, gd, c, l=l: scatter(l, "ffn2", dict(g2=gg, u2=gu, d2=gd), c))
        d, sgm, gwout, gwin = _mixer_bwd(tag, d, sv, p, bias, B)
        sg.update(sgm)
        d = scatter(l, "mixer", dict(wout=gwout, win=gwin), d)
        d, sg["ffn1_norm"] = _ffn_bwd(tag + "f1", d, ffn1, p["ffn1_norm"][None], w1["g1"], w1["u1"], w1["d1"],
                                      lambda gg, gu, gd, c, l=l: scatter(l, "ffn1", dict(g1=gg, u1=gu, d1=gd), c))
        sgrads[l] = sg
    return lsum, d, sgrads


MESH = pl.DeviceIdType.MESH
ANY = pl.BlockSpec(memory_space=pl.ANY)


def _place():
    return lax.axis_index("x"), lax.axis_index("y"), lax.axis_index("c")


def _other_chips(x, y):
    return [(1 - x, y), (x, 1 - y), (1 - x, 1 - y)]


HBM = pl.BlockSpec(memory_space=pltpu.HBM)
SEM = pl.BlockSpec(memory_space=pltpu.SEMAPHORE)
EFFECT = pltpu.SideEffectType.DATAFLOW_SIDE_EFFECTING


def _hbm(a):
    return pltpu.with_memory_space_constraint(a, pltpu.HBM)


def _exchange(gather, layer, src, land, send, recv, n, act):
    x, y, c = _place()
    for k, (px, py) in enumerate(_other_chips(x, y)):
        for a in range(n):
            if gather:
                s_out, d_out, d_in = src[a].at[layer], land[a].at[2 * x + y], land[a].at[2 * px + py]
            else:
                s_out, d_out, d_in = src[a].at[2 * px + py], land[a].at[k], land[a].at[k]
            act(pltpu.make_async_remote_copy(
                src_ref=s_out, dst_ref=d_out if act is _start else d_in, send_sem=send.at[k * n + a],
                recv_sem=recv.at[k * n + a], device_id=(px, py, c), device_id_type=MESH))


def _start(cp):
    cp.start()


def _finish(cp):
    cp.wait_send()
    cp.wait_recv()


def _exchange_start(name, gather, layer, srcs, carry):
    n = len(srcs)
    lands = [lax.empty(((N_SHARD,) + s.shape[1:]) if gather else ((3,) + s.shape[1:]), s.dtype) for s in srcs]

    def body(*refs):
        _exchange(gather, layer, refs[:n], refs[n:2 * n], refs[2 * n + 1], refs[2 * n + 2], n, _start)

    srcs = [_hbm(a) for a in srcs]
    thru = [_hbm(a) for a in lands + [carry]]
    out = pl.pallas_call(
        body, name=name,
        out_shape=(pltpu.SemaphoreType.DMA((3 * n,)), pltpu.SemaphoreType.DMA((3 * n,)),
                   *[pltpu.HBM(a.shape, a.dtype) for a in thru]),
        in_specs=[HBM] * (2 * n + 1), out_specs=(SEM, SEM, *[HBM] * (n + 1)),
        input_output_aliases={n + i: 2 + i for i in range(n + 1)},
        compiler_params=pltpu.CompilerParams(has_side_effects=EFFECT))(*srcs, *thru)
    return dict(gather=gather, layer=layer, send=out[0], recv=out[1], srcs=srcs, lands=list(out[2:2 + n])), out[-1]


def _exchange_wait(name, ex, after):
    n = len(ex["srcs"])

    def body(*refs):
        _exchange(ex["gather"], ex["layer"], refs[:n], refs[n:2 * n], refs[2 * n], refs[2 * n + 1], n, _finish)

    out = pl.pallas_call(
        body, name=name, out_shape=[pltpu.HBM(a.shape, a.dtype) for a in ex["lands"]],
        in_specs=[HBM] * (2 * n) + [SEM, SEM, ANY], out_specs=[HBM] * n,
        input_output_aliases={n + i: i for i in range(n)},
        compiler_params=pltpu.CompilerParams(has_side_effects=EFFECT))(
            *ex["srcs"], *ex["lands"], ex["send"], ex["recv"], after)
    return list(out)


def _swap_sibling(parts):
    n = len(parts)

    def body(*refs):
        src, dst = refs[:n], refs[n:2 * n]
        send, recv = refs[2 * n:]
        x, y, c = _place()
        cps = [pltpu.make_async_remote_copy(src_ref=src[a], dst_ref=dst[a], send_sem=send.at[a], recv_sem=recv.at[a],
                                            device_id=(x, y, 1 - c), device_id_type=MESH) for a in range(n)]
        for cp in cps:
            cp.start()
        for cp in cps:
            cp.wait_recv()
        for cp in cps:
            cp.wait_send()

    return pl.pallas_call(
        body, out_shape=[jax.ShapeDtypeStruct(p.shape, p.dtype) for p in parts],
        in_specs=[ANY] * n, out_specs=[ANY] * n,
        scratch_shapes=[pltpu.SemaphoreType.DMA((n,)), pltpu.SemaphoreType.DMA((n,))],
        name="swap_sibling")(*parts)


def _allreduce_small(name, v, after):
    R = v.shape[0]

    def body(v_ref, after_ref, o_ref, buf, send, recv):
        x, y, c = _place()
        me = 4 * x + 2 * y + c
        buf[me] = v_ref[...]
        cps = []
        for k in range(1, 8):
            fx, fy, fc = (k >> 2) & 1, (k >> 1) & 1, k & 1
            px = 1 - x if fx else x
            py = 1 - y if fy else y
            pc = 1 - c if fc else c
            cp = pltpu.make_async_remote_copy(src_ref=v_ref, dst_ref=buf.at[me], send_sem=send.at[k - 1],
                                              recv_sem=recv.at[k - 1], device_id=(px, py, pc), device_id_type=MESH)
            cp.start()
            cps.append((cp, 4 * px + 2 * py + pc))
        for k, (cp, peer) in enumerate(cps):
            pltpu.make_async_remote_copy(src_ref=v_ref, dst_ref=buf.at[peer], send_sem=send.at[k], recv_sem=recv.at[k],
                                         device_id=(x, y, c), device_id_type=MESH).wait_recv()
        for cp, _ in cps:
            cp.wait_send()
        acc = buf[0]
        for d in range(1, 8):
            acc = acc + buf[d]
        o_ref[...] = acc

    return pl.pallas_call(
        body, out_shape=jax.ShapeDtypeStruct((R, 128), F32),
        in_specs=[pl.BlockSpec(memory_space=pltpu.VMEM), ANY], out_specs=pl.BlockSpec(memory_space=pltpu.VMEM),
        scratch_shapes=[pltpu.VMEM((8, R, 128), F32), pltpu.SemaphoreType.DMA((7,)), pltpu.SemaphoreType.DMA((7,))],
        name=name)(v, after)


def _row_tile(r):
    for t in (256, 128, 64, 32, 16, 8):
        if r % t == 0:
            return t
    raise ValueError(r)


def _sum4(name, me, parts, got):
    _, R, C = parts.shape
    tr = _row_tile(R)

    def body(me_ref, o_ref, g_ref, s_ref):
        s = o_ref[...].astype(F32)
        for k in range(3):
            s = s + g_ref[k].astype(F32)
        s_ref[...] = s.astype(BF16)

    return pl.pallas_call(
        body, out_shape=jax.ShapeDtypeStruct((R, C), BF16),
        grid_spec=pltpu.PrefetchScalarGridSpec(
            num_scalar_prefetch=1, grid=(R // tr,),
            in_specs=[pl.BlockSpec((None, tr, C), lambda i, me_ref: (me_ref[0], i, 0)),
                      pl.BlockSpec((3, tr, C), lambda i, me_ref: (0, i, 0))],
            out_specs=pl.BlockSpec((tr, C), lambda i, me_ref: (i, 0))),
        name=name, compiler_params=_cp("parallel"))(me, parts, got)


def _adamw(name, w, gparts, m, v):
    R, C = w.shape
    tr = _row_tile(R)
    ng = len(gparts)
    c1 = 1.0 - ADAM_B1 ** ADAM_STEP
    c2 = 1.0 - ADAM_B2 ** ADAM_STEP

    def body(*refs):
        w_ref = refs[0]
        g_refs = refs[1:1 + ng]
        m_ref, v_ref, go_ref, d_ref, mo_ref, vo_ref = refs[1 + ng:]
        g = g_refs[0][...]
        for r in g_refs[1:]:
            g = g + r[...]
        mn = ADAM_B1 * m_ref[...] + (1.0 - ADAM_B1) * g
        vn = ADAM_B2 * v_ref[...] + (1.0 - ADAM_B2) * (g * g)
        go_ref[...] = g
        mo_ref[...] = mn
        vo_ref[...] = vn
        d_ref[...] = -ADAM_LR * ((mn / c1) / (jnp.sqrt(vn / c2) + ADAM_EPS) + ADAM_WD * w_ref[...])

    blk = pl.BlockSpec((tr, C), lambda i: (i, 0))
    osh = jax.ShapeDtypeStruct((R, C), F32)
    return pl.pallas_call(
        body, out_shape=(osh, osh, osh, osh), grid=(R // tr,), in_specs=[blk] * (3 + ng), out_specs=(blk,) * 4,
        name=name, compiler_params=_cp("parallel"))(w, *gparts, m, v)


def _adamw_layers(name, w, sums, m, v):
    _, R, C = w.shape
    tr = _row_tile(R)
    nr = R // tr
    c1 = 1.0 - ADAM_B1 ** ADAM_STEP
    c2 = 1.0 - ADAM_B2 ** ADAM_STEP

    def body(w_ref, a0, b0, a1, b1, m_ref, v_ref, go_ref, d_ref, mo_ref, vo_ref):
        f = lambda r: r[...].astype(F32)
        g = jnp.where(pl.program_id(0) == 0, f(a0) + f(b0), f(a1) + f(b1))
        mn = ADAM_B1 * m_ref[...] + (1.0 - ADAM_B1) * g
        vn = ADAM_B2 * v_ref[...] + (1.0 - ADAM_B2) * (g * g)
        go_ref[...] = g
        mo_ref[...] = mn
        vo_ref[...] = vn
        d_ref[...] = -ADAM_LR * ((mn / c1) / (jnp.sqrt(vn / c2) + ADAM_EPS) + ADAM_WD * w_ref[...])

    blk = pl.BlockSpec((None, tr, C), lambda l, i: (l, i, 0))
    lay0 = pl.BlockSpec((tr, C), lambda l, i: (jnp.where(l == 0, i, nr - 1), 0))
    lay1 = pl.BlockSpec((tr, C), lambda l, i: (jnp.where(l == 1, i, 0), 0))
    oblk = pl.BlockSpec((tr, C), lambda l, i: (l * nr + i, 0))
    osh = jax.ShapeDtypeStruct((DEPTH * R, C), F32)
    res = pl.pallas_call(
        body, out_shape=(osh, osh, osh, osh), grid=(DEPTH, nr),
        in_specs=[blk, lay0, lay0, lay1, lay1, blk, blk], out_specs=(oblk,) * 4,
        name=name, compiler_params=_cp("arbitrary", "arbitrary"))(w, *sums[0], *sums[1], m, v)
    return [r.reshape(w.shape) for r in res]


BIG = [("ffn1_w_gate", "g1"), ("ffn1_w_up", "u1"), ("ffn1_w_down", "d1"), ("w_in", "win"), ("w_out", "wout"),
       ("ffn2_w_gate", "g2"), ("ffn2_w_up", "u2"), ("ffn2_w_down", "d2")]
SMALL = ["ffn1_norm", "mix_norm", "conv_b", "dt_bias", "a_log", "d_skip", "ssd_norm", "q_norm", "k_norm", "ffn2_norm"]
WEIGHTS = ["ffn1_norm", "ffn1_w_gate", "ffn1_w_up", "ffn1_w_down", "mix_norm", "w_in", "conv_w", "conv_b", "dt_bias",
           "a_log", "d_skip", "ssd_norm", "q_norm", "k_norm", "w_out", "ffn2_norm", "ffn2_w_gate", "ffn2_w_up",
           "ffn2_w_down"]
CONV_SH = CONV_DIM // N_SHARD
TRANSPOSED = ("g1", "u1", "g2", "u2")
GATHER_GROUPS = [(0, "ffn1", ["g1", "u1"]), (0, "ffn1d", ["d1"]), (0, "win", ["win", "cw"]),
                 (0, "rest", ["wout", "g2", "u2", "d2"]),
                 (1, "all", ["g1", "u1", "d1", "win", "cw", "wout", "g2", "u2", "d2"])]


def _pad128(v):
    v = v.reshape(-1)
    return jnp.pad(v, (0, (-v.shape[0]) % 128))


def _pack(pieces):
    flat, offs, pos = [], [], 0
    for p in pieces:
        q = _pad128(p.astype(F32))
        offs.append(pos)
        pos += q.shape[0] // 128
        flat.append(q)
    total = -(-pos // 8) * 8
    out = jnp.concatenate(flat + [jnp.zeros(((total - pos) * 128,), F32)]).reshape(total, 128)
    return out, offs


def _unpack(packed, offs, shapes):
    out = []
    for off, shp in zip(offs, shapes):
        n = int(np.prod(shp))
        rows = -(-n // 128)
        out.append(packed[off:off + rows].reshape(-1)[:n].reshape(shp))
    return out


def kernel(x, ffn1_norm, ffn1_w_gate, ffn1_w_up, ffn1_w_down, mix_norm, w_in, conv_w, conv_b, dt_bias, a_log, d_skip, ssd_norm, q_norm, k_norm, w_out, ffn2_norm, ffn2_w_gate, ffn2_w_up, ffn2_w_down, loss_target, m_ffn1_norm, m_ffn1_w_gate, m_ffn1_w_up, m_ffn1_w_down, m_mix_norm, m_w_in, m_conv_w, m_conv_b, m_dt_bias, m_a_log, m_d_skip, m_ssd_norm, m_q_norm, m_k_norm, m_w_out, m_ffn2_norm, m_ffn2_w_gate, m_ffn2_w_up, m_ffn2_w_down, v_ffn1_norm, v_ffn1_w_gate, v_ffn1_w_up, v_ffn1_w_down, v_mix_norm, v_w_in, v_conv_w, v_conv_b, v_dt_bias, v_a_log, v_d_skip, v_ssd_norm, v_q_norm, v_k_norm, v_w_out, v_ffn2_norm, v_ffn2_w_gate, v_ffn2_w_up, v_ffn2_w_down):
    A = dict(locals())
    ix, iy, ic = _place()
    me = 2 * ix + iy
    B, S, _ = x.shape
    T = B * S

    view = lambda a, key: jnp.swapaxes(a, 1, 2) if key in TRANSPOSED else a
    own = {key: view(A[name], key).astype(BF16) for name, key in BIG}
    own["cw"] = conv_w
    exs, first_norm = [], ffn1_norm
    for gi, (l, _, keys) in enumerate(GATHER_GROUPS):
        ex, first_norm = _exchange_start("gather_start%d" % gi, True, l, [own[key] for key in keys], first_norm)
        exs.append(ex)
    landed = {}

    def weights(l, group, after):
        gi = [i for i, (gl, gname, _) in enumerate(GATHER_GROUPS) if gl == l and gname in (group, "all")][0]
        if gi not in landed:
            lands = _exchange_wait("gather_wait%d" % gi, exs[gi], after)
            landed[gi] = {}
            for key, land in zip(GATHER_GROUPS[gi][2], lands):
                full = lax.dynamic_update_slice(land, own[key][l][None], (me, 0, 0))
                if key == "win":
                    full = _win_from_shards(full)
                if key == "cw":
                    full = jnp.transpose(full, (1, 0, 2)).reshape(CONV_K, CONV_DIM)
                landed[gi][key] = full
        return landed[gi]

    pending = []

    def scatter(l, group, grads, carry):
        keys = sorted(grads)
        arrs = [grads[key] for key in keys]
        if "win" in grads:
            arrs[keys.index("win")] = _win_to_shards(grads["win"])
        ex, carry = _exchange_start("scatter_start_l%d_%s" % (l, group), False, None, arrs, carry)
        pending.append((l, keys, ex))
        return carry

    small = {name: A[name] for name in SMALL}
    small["ffn1_norm"] = first_norm
    lsum, dx, sgrads = _local_step(x.reshape(T, D_MODEL), loss_target.reshape(T, D_MODEL), small, weights, scatter, B)

    names = SMALL + ["conv_w"]
    shapes = [A[n].shape for n in SMALL] + [(DEPTH, CONV_K, CONV_DIM), ()]
    pieces = [jnp.stack([sgrads[l][n].reshape(shp[1:]) for l in range(DEPTH)]) for n, shp in zip(names, shapes)]
    pieces.append(0.5 / D_MODEL * jnp.sum(lsum))
    packed, offs = _pack(pieces)

    sums, after = {}, dx
    me1 = jnp.reshape(me, (1,)).astype(jnp.int32)
    for idx, (l, keys, ex) in enumerate(pending):
        lands = _exchange_wait("scatter_wait%d" % idx, ex, after)
        for key, g, got in zip(keys, ex["srcs"], lands):
            sums[key, l] = after = _sum4("sum_%s_l%d" % (key, l), me1, g, got)

    red = _unpack(_allreduce_small("allreduce_small", packed, after), offs, shapes)
    loss = red[-1]
    sg = dict(zip(names, red[:-1]))
    order = [(key, l) for _, key in BIG for l in range(DEPTH)]
    theirs = dict(zip(order, _swap_sibling([sums[k] for k in order])))

    out = {}
    for name, key in BIG:
        res = _adamw_layers("adamw_" + key, view(A[name], key), [(sums[key, l], theirs[key, l]) for l in range(DEPTH)],
                            view(A["m_" + name], key), view(A["v_" + name], key))
        out[name] = [view(r, key) for r in res]

    wp, offs = _pack([A[n] for n in SMALL])
    gp, _ = _pack([sg[n] for n in SMALL])
    mp, _ = _pack([A["m_" + n] for n in SMALL])
    vp, _ = _pack([A["v_" + n] for n in SMALL])
    res = _adamw("adamw_small", wp, [gp], mp, vp)
    shapes = [A[n].shape for n in SMALL]
    res = [_unpack(r, offs, shapes) for r in res]
    for i, n in enumerate(SMALL):
        out[n] = [res[q][i] for q in range(4)]
    gcw = lax.dynamic_slice_in_dim(sg["conv_w"], me * CONV_SH, CONV_SH, axis=2)
    flat = lambda a: a.reshape(DEPTH * CONV_K, CONV_SH)
    res = _adamw("adamw_conv_w", flat(conv_w), [flat(gcw)], flat(m_conv_w), flat(v_conv_w))
    out["conv_w"] = [r.reshape(conv_w.shape) for r in res]

    outs = [loss, dx.reshape(B, S, D_MODEL)]
    for q in range(4):
        outs += [out[n][q] for n in WEIGHTS]
    return tuple(outs)
```

```python
import functools
import math

import numpy as np
import jax
import jax.numpy as jnp
from jax import lax
from jax.experimental import pallas as pl
from jax.experimental.pallas import tpu as pltpu

F32 = jnp.float32
BF16 = jnp.bfloat16

D_MODEL = 1024
DEPTH = 2
N_SHARD = 4
D_FF = 2816
FF_SH = D_FF // N_SHARD
SSD_HEADS = 16
HEAD_DIM = 64
SSD_GROUPS = 4
GROUP_W = 256
SSD_STATE = 128
CONV_K = 4
CONV_DIM = 2048
ATT_HEADS = 16
MIX_W = 2048
MIX_SH = MIX_W // N_SHARD
IN_PROJ = 6160
IN_SH = IN_PROJ // N_SHARD
IN_PAD = 6272
PROJ_TN = 896
COL_Z, COL_XBC, COL_Q, COL_K, COL_V, COL_DT = 0, 1024, 3072, 4096, 5120, 6144
EPS = 1e-6
NEG = -1e30
SSD_L = 256
ATT_B = 256
ROW_T = 512
HALF_T = ROW_T // 2
TK_W = 2048
CONV_CT = 256
CONV_R = 256
PAD_R = 8

ADAM_LR, ADAM_B1, ADAM_B2, ADAM_EPS, ADAM_WD, ADAM_STEP = 0.001, 0.9, 0.999, 1e-08, 0.01, 10

NN = (((1,), (0,)), ((), ()))
NT = (((1,), (1,)), ((), ()))
TN = (((0,), (0,)), ((), ()))

VMEM_LIMIT = 56 * 1024 * 1024


def _cp(*sem):
    return pltpu.CompilerParams(dimension_semantics=sem, vmem_limit_bytes=VMEM_LIMIT)


def _dot(a, b, dims):
    return lax.dot_general(a, b, dims, preferred_element_type=F32)


def _sigmoid(x):
    return 0.5 * jnp.tanh(0.5 * x) + 0.5


def _softplus(x):
    return jnp.maximum(x, 0.0) + jnp.log(1.0 + jnp.exp(-jnp.abs(x)))


def _mm(name, pairs, out_shape, out_spec, grid, dims, acc_shape, res=None, scale=1.0, post=None, post_in=()):
    nk = grid[2]
    npair = len(pairs)
    npost = len(post_in)

    def body(*refs):
        ab = refs[:2 * npair]
        pos = 2 * npair
        res_ref = None
        if res is not None:
            res_ref = refs[pos]
            pos += 1
        pin = refs[pos:pos + npost]
        pos += npost
        out_ref = refs[pos]
        pos += 1
        if post is not None:
            out2_ref = refs[pos]
            pos += 1
        s = None
        for p in range(npair):
            d = _dot(ab[2 * p][...].astype(BF16), ab[2 * p + 1][...].astype(BF16), dims)
            s = d if s is None else s + d

        def finish(r):
            if scale != 1.0:
                r = r * scale
            if res_ref is not None:
                r = r + res_ref[...]
            if post == "rmsb":
                @pl.when(pl.program_id(0) == 0)
                def _():
                    out2_ref[...] = jnp.zeros_like(out2_ref)

                xv = pin[0][...]
                rr = lax.rsqrt(jnp.mean(xv * xv, axis=-1, keepdims=True) + EPS)
                xhat = xv * rr
                dxhat = r * pin[1][...]
                out_ref[...] = pin[2][...] + rr * (dxhat - xhat * jnp.mean(dxhat * xhat, axis=-1, keepdims=True))
                out2_ref[...] += jnp.sum(r * xhat, axis=0, keepdims=True)
                return
            out_ref[...] = r.astype(out_ref.dtype)
            if post == "norm":
                rr = lax.rsqrt(jnp.mean(r * r, axis=-1, keepdims=True) + EPS)
                out2_ref[...] = (r * rr * pin[0][...]).astype(BF16)

        if nk == 1:
            finish(s)
            return
        acc = refs[pos]
        k = pl.program_id(2)

        @pl.when(k == 0)
        def _():
            acc[...] = s

        @pl.when(k > 0)
        def _():
            acc[...] += s

        @pl.when(k == nk - 1)
        def _():
            finish(acc[...])

    args, specs = [], []
    for a, a_spec, b, b_spec in pairs:
        args += [a, b]
        specs += [a_spec, b_spec]
    for arr, spec in ([res] if res is not None else []) + list(post_in):
        args.append(arr)
        specs.append(spec)
    sems = ("arbitrary",) * 3 if post == "rmsb" else ("parallel", "parallel", "arbitrary")
    return pl.pallas_call(
        body, out_shape=out_shape, grid=grid, in_specs=specs, out_specs=out_spec,
        scratch_shapes=[] if nk == 1 else [pltpu.VMEM(acc_shape, F32)], name=name,
        compiler_params=_cp(*sems))(*args)


def _rms_fwd(name, x, w):
    T = x.shape[0]

    def body(x_ref, w_ref, o_ref):
        xv = x_ref[...]
        r = lax.rsqrt(jnp.mean(xv * xv, axis=-1, keepdims=True) + EPS)
        o_ref[...] = (xv * r * w_ref[...]).astype(BF16)

    return pl.pallas_call(
        body, out_shape=jax.ShapeDtypeStruct((T, D_MODEL), BF16), grid=(T // ROW_T,),
        in_specs=[pl.BlockSpec((ROW_T, D_MODEL), lambda i: (i, 0)), pl.BlockSpec((1, D_MODEL), lambda i: (0, 0))],
        out_specs=pl.BlockSpec((ROW_T, D_MODEL), lambda i: (i, 0)), name=name, compiler_params=_cp("parallel"))(x, w)


def _loss_grad(name, y, t):
    T = y.shape[0]

    def body(y_ref, t_ref, dy_ref, l_ref):
        @pl.when(pl.program_id(0) == 0)
        def _():
            l_ref[...] = jnp.zeros_like(l_ref)

        e = y_ref[...] - t_ref[...]
        dy_ref[...] = e * (1.0 / D_MODEL)
        l_ref[...] += jnp.sum(e * e, axis=0, keepdims=True)

    row = pl.BlockSpec((ROW_T, D_MODEL), lambda i: (i, 0))
    vec = pl.BlockSpec((1, D_MODEL), lambda i: (0, 0))
    return pl.pallas_call(
        body, out_shape=(jax.ShapeDtypeStruct((T, D_MODEL), F32), jax.ShapeDtypeStruct((1, D_MODEL), F32)),
        grid=(T // ROW_T,), in_specs=[row, row], out_specs=(row, vec), name=name,
        compiler_params=_cp("arbitrary"))(y, t)


def _ffn_gate_up(name, h, wg, wu):
    T = h.shape[0]

    def body(h_ref, wg_ref, wu_ref, dgf_ref, duf_ref, a_ref):
        for r in range(0, ROW_T, HALF_T):
            rows = slice(r, r + HALF_T)
            hv = h_ref[rows, :]
            g = _dot(hv, wg_ref[...], NT)
            u = _dot(hv, wu_ref[...], NT)
            sg = _sigmoid(g)
            silu = g * sg
            dgf_ref[rows, :] = (u * (sg * (1.0 + g * (1.0 - sg)))).astype(BF16)
            duf_ref[rows, :] = silu.astype(BF16)
            a_ref[rows, :] = (silu * u).astype(BF16)

    wspec = pl.BlockSpec((None, FF_SH, D_MODEL), lambda j, i: (j, 0, 0))
    ospec = pl.BlockSpec((None, ROW_T, FF_SH), lambda j, i: (j, i, 0))
    osh = jax.ShapeDtypeStruct((N_SHARD, T, FF_SH), BF16)
    return pl.pallas_call(
        body, out_shape=(osh, osh, osh), grid=(N_SHARD, T // ROW_T),
        in_specs=[pl.BlockSpec((ROW_T, D_MODEL), lambda j, i: (i, 0)), wspec, wspec],
        out_specs=(ospec, ospec, ospec), name=name, compiler_params=_cp("parallel", "parallel"))(h, wg, wu)


def _ffn_dact(name, dx, wd, g, u):
    T = dx.shape[0]

    def body(dx_ref, wd_ref, g_ref, u_ref, dg_ref, du_ref):
        for r in range(0, ROW_T, HALF_T):
            rows = slice(r, r + HALF_T)
            da = 0.5 * _dot(dx_ref[rows, :].astype(BF16), wd_ref[...], NT)
            dg_ref[rows, :] = (da * g_ref[rows, :].astype(F32)).astype(BF16)
            du_ref[rows, :] = (da * u_ref[rows, :].astype(F32)).astype(BF16)

    aspec = pl.BlockSpec((None, ROW_T, FF_SH), lambda j, i: (j, i, 0))
    osh = jax.ShapeDtypeStruct((N_SHARD, T, FF_SH), BF16)
    return pl.pallas_call(
        body, out_shape=(osh, osh), grid=(N_SHARD, T // ROW_T),
        in_specs=[pl.BlockSpec((ROW_T, D_MODEL), lambda j, i: (i, 0)),
                  pl.BlockSpec((None, FF_SH, D_MODEL), lambda j, i: (j, 0, 0)), aspec, aspec],
        out_specs=(aspec, aspec), name=name, compiler_params=_cp("parallel", "parallel"))(dx, wd, g, u)


def _row3():
    return pl.BlockSpec((ROW_T, D_MODEL), lambda i, n, k: (i, 0))


def _vec3():
    return pl.BlockSpec((1, D_MODEL), lambda i, n, k: (0, 0))


def _with_norm(T, next_nw):
    if next_nw is None:
        return dict(out_shape=jax.ShapeDtypeStruct((T, D_MODEL), F32), out_spec=_row3())
    return dict(out_shape=(jax.ShapeDtypeStruct((T, D_MODEL), F32), jax.ShapeDtypeStruct((T, D_MODEL), BF16)),
                out_spec=(_row3(), _row3()), post="norm", post_in=[(next_nw, _vec3())])


def _ffn_fwd(tag, x, h, wg, wu, wd, next_nw):
    T = x.shape[0]
    g, u, a = _ffn_gate_up(tag + "_gu", h, wg, wu)
    if callable(wd):
        wd = wd(a)
    nt = T // ROW_T
    o = _with_norm(T, next_nw)
    xo = _mm(tag + "_down",
             [(a, pl.BlockSpec((None, ROW_T, FF_SH), lambda i, n, k, j=j: (j, i, 0)),
               wd, pl.BlockSpec((None, FF_SH, D_MODEL), lambda i, n, k, j=j: (j, 0, 0))) for j in range(N_SHARD)],
             o.pop("out_shape"), o.pop("out_spec"), (nt, 1, 1), NN, (ROW_T, D_MODEL),
             res=(x, _row3()), scale=0.5, **o)
    return xo, (x, h, g, u, a), wd


def _ffn_bwd(tag, dxo, saved, nw, wg, wu, wd, emit):
    x, h, g, u, a = saved
    T = x.shape[0]
    nt = T // ROW_T
    tkw = min(TK_W, T)
    nw_t = T // tkw
    dg, du = _ffn_dact(tag + "_dact", dxo, wd, g, u)
    actw = lambda f: pl.BlockSpec((None, tkw, FF_SH), f)
    gd = _mm(tag + "_dwd",
             [(a, actw(lambda m, n, k: (m, k, 0)), dxo, pl.BlockSpec((tkw, D_MODEL), lambda m, n, k: (k, 0)))],
             jax.ShapeDtypeStruct((N_SHARD, FF_SH, D_MODEL), BF16),
             pl.BlockSpec((None, FF_SH, D_MODEL), lambda m, n, k: (m, 0, 0)),
             (N_SHARD, 1, nw_t), TN, (FF_SH, D_MODEL), scale=0.5)
    hspec = pl.BlockSpec((tkw, D_MODEL), lambda j, n, k: (k, 0))
    gsh = jax.ShapeDtypeStruct((N_SHARD, FF_SH, D_MODEL), BF16)
    gspec = pl.BlockSpec((None, FF_SH, D_MODEL), lambda j, n, k: (j, 0, 0))
    gg = _mm(tag + "_dwg", [(dg, actw(lambda j, n, k: (j, k, 0)), h, hspec)], gsh, gspec,
             (N_SHARD, 1, nw_t), TN, (FF_SH, D_MODEL))
    gu = _mm(tag + "_dwu", [(du, actw(lambda j, n, k: (j, k, 0)), h, hspec)], gsh, gspec,
             (N_SHARD, 1, nw_t), TN, (FF_SH, D_MODEL))
    dg = emit(gg, gu, gd, dg)
    act = lambda j: pl.BlockSpec((None, ROW_T, FF_SH), lambda i, n, k: (j, i, 0))
    wsp = lambda j: pl.BlockSpec((None, FF_SH, D_MODEL), lambda i, n, k: (j, 0, 0))
    return _mm(tag + "_dh",
               [(dd, act(j), w, wsp(j)) for j in range(N_SHARD) for dd, w in ((dg, wg), (du, wu))],
               (jax.ShapeDtypeStruct((T, D_MODEL), F32), jax.ShapeDtypeStruct((1, D_MODEL), F32)), (_row3(), _vec3()),
               (nt, 1, 1), NN, (ROW_T, D_MODEL), post="rmsb", post_in=[(x, _row3()), (nw, _vec3()), (dxo, _row3())])


def _seq_rows(ref, start, size, S):
    lo, hi = max(start, 0), min(start + size, S)
    parts = [ref[pl.ds(lo, hi - lo), :]]
    if lo > start:
        parts.insert(0, jnp.zeros((lo - start, ref.shape[1]), F32))
    if start + size > hi:
        parts.append(jnp.zeros((start + size - hi, ref.shape[1]), F32))
    return parts[0] if len(parts) == 1 else jnp.concatenate(parts, axis=0)


XBC_CB = COL_XBC // CONV_CT


def _conv_fwd(name, proj, w, b, B):
    T = proj.shape[0]
    S = T // B
    C = CONV_DIM

    def body(x_ref, w_ref, b_ref, o_ref):
        wv = w_ref[...]
        for c in range(S // CONV_R):
            r0 = c * CONV_R
            ch = _seq_rows(x_ref, r0 - PAD_R, CONV_R + PAD_R, S)
            pre = ch[PAD_R:] * wv[3:4] + b_ref[...]
            for s in range(1, CONV_K):
                pre = pre + pltpu.roll(ch, s, axis=0)[PAD_R:] * wv[3 - s:4 - s]
            o_ref[pl.ds(r0, CONV_R), :] = pre * _sigmoid(pre)

    return pl.pallas_call(
        body, out_shape=jax.ShapeDtypeStruct((T, C), F32), grid=(B, C // CONV_CT),
        in_specs=[pl.BlockSpec((S, CONV_CT), lambda bi, ci: (bi, XBC_CB + ci)),
                  pl.BlockSpec((CONV_K, CONV_CT), lambda bi, ci: (0, ci)),
                  pl.BlockSpec((1, CONV_CT), lambda bi, ci: (0, ci))],
        out_specs=pl.BlockSpec((S, CONV_CT), lambda bi, ci: (bi, ci)), name=name,
        compiler_params=_cp("parallel", "parallel"))(proj, w, b)


def _conv_bwd(name, proj, dxs, dB, dC, w, b, dproj, B):
    T = proj.shape[0]
    S = T // B
    C = CONV_DIM
    RW = CONV_R + PAD_R
    nx, nb = dxs.shape[1] // CONV_CT, dB.shape[1] // CONV_CT

    def body(x_ref, dx_in, db_in, dc_in, w_ref, b_ref, buf_ref, dx_ref, dw_ref, db_ref):
        @pl.when(pl.program_id(1) == 0)
        def _():
            dw_ref[...] = jnp.zeros_like(dw_ref)
            db_ref[...] = jnp.zeros_like(db_ref)

        ci = pl.program_id(0)
        wv = w_ref[...]
        dw = [jnp.zeros((1, CONV_CT), F32) for _ in range(CONV_K)]
        db = jnp.zeros((1, CONV_CT), F32)
        for c in range(S // CONV_R):
            r0 = c * CONV_R
            ch = _seq_rows(x_ref, r0 - PAD_R, RW + PAD_R, S)
            xs = [ch[PAD_R:]] + [pltpu.roll(ch, s, axis=0)[PAD_R:] for s in range(1, CONV_K)]
            pre = b_ref[...] + xs[0] * wv[3:4]
            for s in range(1, CONV_K):
                pre = pre + xs[s] * wv[3 - s:4 - s]
            sg = _sigmoid(pre)
            dout = jnp.where(ci < nx, _seq_rows(dx_in, r0, RW, S),
                             jnp.where(ci < nx + nb, _seq_rows(db_in, r0, RW, S), _seq_rows(dc_in, r0, RW, S)))
            dpre = dout * (sg * (1.0 + pre * (1.0 - sg)))
            dx = dpre[:CONV_R] * wv[3:4]
            for s in range(1, CONV_K):
                dx = dx + pltpu.roll(dpre, RW - s, axis=0)[:CONV_R] * wv[3 - s:4 - s]
            dx_ref[pl.ds(r0, CONV_R), :] = dx.astype(BF16)
            dcur = dpre[:CONV_R]
            db = db + jnp.sum(dcur, axis=0, keepdims=True)
            for s in range(CONV_K):
                dw[3 - s] = dw[3 - s] + jnp.sum(dcur * xs[s][:CONV_R], axis=0, keepdims=True)
        db_ref[...] += db
        for k in range(CONV_K):
            dw_ref[k:k + 1, :] += dw[k]

    seq = lambda f: pl.BlockSpec((S, CONV_CT), f)
    return pl.pallas_call(
        body,
        out_shape=(jax.ShapeDtypeStruct(dproj.shape, dproj.dtype), jax.ShapeDtypeStruct((CONV_K, C), F32),
                   jax.ShapeDtypeStruct((1, C), F32)),
        grid=(C // CONV_CT, B),
        in_specs=[seq(lambda ci, bi: (bi, XBC_CB + ci)),
                  seq(lambda ci, bi: (bi, jnp.minimum(ci, nx - 1))),
                  seq(lambda ci, bi: (bi, jnp.clip(ci - nx, 0, nb - 1))),
                  seq(lambda ci, bi: (bi, jnp.clip(ci - nx - nb, 0, nb - 1))),
                  pl.BlockSpec((CONV_K, CONV_CT), lambda ci, bi: (0, ci)),
                  pl.BlockSpec((1, CONV_CT), lambda ci, bi: (0, ci)), ANY],
        out_specs=(seq(lambda ci, bi: (bi, XBC_CB + ci)),
                   pl.BlockSpec((CONV_K, CONV_CT), lambda ci, bi: (0, ci)),
                   pl.BlockSpec((1, CONV_CT), lambda ci, bi: (0, ci))),
        input_output_aliases={6: 0},
        name=name, compiler_params=_cp("parallel", "arbitrary"))(proj, dxs, dB, dC, w, b, dproj)


def _tri_sum(tri, x, dims, tri_first, terms=3):
    out, rest = None, x
    for t in range(terms):
        part = rest.astype(BF16)
        if t + 1 < terms:
            rest = rest - part.astype(F32)
        d = _dot(tri, part, dims) if tri_first else _dot(part, tri, dims)
        out = d if out is None else out + d
    return out


def _total(x):
    return jnp.sum(jnp.sum(x, axis=0, keepdims=True), axis=-1, keepdims=True)


def _ssd_common(dtc_ref, dtr_ref, pcol_ref, prow_ref, b_ref, c_ref):
    L = SSD_L
    bias_c, alog_c = pcol_ref[0:1, :], pcol_ref[1:2, :]
    a_c = -jnp.exp(alog_c)
    dt_c = _softplus(dtc_ref[...] + bias_c)
    row = lax.broadcasted_iota(jnp.int32, (L, L), 0)
    col = lax.broadcasted_iota(jnp.int32, (L, L), 1)
    causal = row >= col
    tri = causal.astype(BF16)
    cum_c = _tri_sum(tri, dt_c * a_c, NN, True)
    a_r = -jnp.exp(prow_ref[:, 1:2])
    dt_r = _softplus(dtr_ref[...] + prow_ref[:, 0:1])
    cum_r = _tri_sum(tri, dt_r * a_r, NT, False)
    bb = b_ref[...].astype(BF16)
    cb = c_ref[...].astype(BF16)
    G = _dot(cb, bb, NT)
    return a_c, dt_c, causal, tri, cum_c, cum_r, bb, cb, G


def _ssd_fwd(name, xc, proj, dtc, dtr, pcol, prow, nw, B):
    T = xc.shape[0]
    S = T // B
    nb = S // SSD_L
    L = SSD_L

    def body(xs_ref, b_ref, c_ref, z_ref, dtc_ref, dtr_ref, pcol_ref, prow_ref, nw_ref, y_ref, yn_ref, hs_ref, H, yo_s):
        @pl.when(pl.program_id(2) == 0)
        def _():
            H[...] = jnp.zeros_like(H)

        a_c, dt_c, causal, tri, cum_c, cum_r, bb, cb, G = _ssd_common(dtc_ref, dtr_ref, pcol_ref, prow_ref, b_ref, c_ref)
        dsk = pcol_ref[2:3, :]
        clast = cum_c[L - 1:L, :]
        bf = b_ref[...]
        for h in range(4):
            hs_ref[h] = H[h]
            yo_s[h] = _dot(cb, H[h].astype(BF16), NN)
        for h in range(4):
            sl = slice(HEAD_DIM * h, HEAD_DIM * (h + 1))
            cc = cum_c[:, h:h + 1]
            lm = jnp.exp(jnp.where(causal, cc - cum_r[h:h + 1, :], NEG))
            M = (G * lm).astype(BF16)
            xh = xs_ref[:, sl]
            Xb = (xh * dt_c[:, h:h + 1]).astype(BF16)
            Hh = H[h]
            y = _dot(M, Xb, NN) + jnp.exp(cc) * yo_s[h]
            y_ref[:, sl] = y + dsk[:, h:h + 1] * xh
            cl = clast[:, h:h + 1]
            Bw = (bf * jnp.exp(cl - cc)).astype(BF16)
            H[h] = jnp.exp(cl) * Hh + _dot(Bw, Xb, TN)
        zv = z_ref[...]
        y2 = y_ref[...] * (zv * _sigmoid(zv))
        r = lax.rsqrt(jnp.mean(y2 * y2, axis=-1, keepdims=True) + EPS)
        yn_ref[...] = (y2 * r * nw_ref[...]).astype(BF16)

    rowi = lambda b, g, i: b * nb + i
    grp = pl.BlockSpec((L, GROUP_W), lambda b, g, i: (rowi(b, g, i), g))
    return pl.pallas_call(
        body,
        out_shape=(jax.ShapeDtypeStruct((T, 1024), F32), jax.ShapeDtypeStruct((T, 1024), BF16),
                   jax.ShapeDtypeStruct((B, SSD_GROUPS, nb, 4, SSD_STATE, HEAD_DIM), F32)),
        grid=(B, SSD_GROUPS, nb),
        in_specs=[grp,
                  pl.BlockSpec((L, SSD_STATE), lambda b, g, i: (rowi(b, g, i), 8 + g)),
                  pl.BlockSpec((L, SSD_STATE), lambda b, g, i: (rowi(b, g, i), 12 + g)),
                  grp,
                  pl.BlockSpec((None, L, 4), lambda b, g, i: (g, rowi(b, g, i), 0)),
                  pl.BlockSpec((None, 4, L), lambda b, g, i: (g, 0, rowi(b, g, i))),
                  pl.BlockSpec((None, 3, 4), lambda b, g, i: (g, 0, 0)),
                  pl.BlockSpec((None, 4, 3), lambda b, g, i: (g, 0, 0)),
                  pl.BlockSpec((1, GROUP_W), lambda b, g, i: (0, g))],
        out_specs=(grp, grp,
                   pl.BlockSpec((None, None, None, 4, SSD_STATE, HEAD_DIM), lambda b, g, i: (b, g, i, 0, 0, 0))),
        scratch_shapes=[pltpu.VMEM((4, SSD_STATE, HEAD_DIM), F32), pltpu.VMEM((4, L, HEAD_DIM), F32)], name=name,
        compiler_params=_cp("parallel", "parallel", "arbitrary"))(xc, xc, xc, proj, dtc, dtr, pcol, prow, nw)


def _ssd_bwd(name, dyn, Y, xc, proj, dtc, dtr, pcol, prow, nw, hs, dproj, B):
    T = xc.shape[0]
    S = T // B
    nb = S // SSD_L
    L = SSD_L

    def body(dyn_ref, y_ref, xs_ref, b_ref, c_ref, z_ref, dtc_ref, dtr_ref, pcol_ref, prow_ref, nw_ref, hs_ref, buf_ref,
             dxs_ref, db_ref, dc_ref, dz_ref, ddt_ref, dpar_ref, dnw_ref, dH, dm_s, dxo_s, ea_s, ex_s):
        @pl.when(pl.program_id(2) == 0)
        def _():
            dH[...] = jnp.zeros_like(dH)
            dpar_ref[...] = jnp.zeros_like(dpar_ref)
            dnw_ref[...] = jnp.zeros_like(dnw_ref)

        a_c, dt_c, causal, tri, cum_c, cum_r, bb, cb, G = _ssd_common(dtc_ref, dtr_ref, pcol_ref, prow_ref, b_ref, c_ref)
        dsk = pcol_ref[2:3, :]
        clast = cum_c[L - 1:L, :]
        bf = b_ref[...]
        cf = c_ref[...]
        Yv = y_ref[...]
        zv = z_ref[...]
        sz = _sigmoid(zv)
        silu = zv * sz
        y2 = Yv * silu
        r = lax.rsqrt(jnp.mean(y2 * y2, axis=-1, keepdims=True) + EPS)
        yhat = y2 * r
        dyv = dyn_ref[...]
        dnw_ref[...] += jnp.sum(dyv * yhat, axis=0, keepdims=True)
        dyhat = dyv * nw_ref[...]
        dy2 = r * (dyhat - yhat * jnp.mean(dyhat * yhat, axis=-1, keepdims=True))
        dY = dy2 * silu
        dz_ref[...] = (dy2 * Yv * (sz * (1.0 + zv * (1.0 - sz)))).astype(BF16)

        lane4 = lax.broadcasted_iota(jnp.int32, (1, 4), 1)
        dG = jnp.zeros((L, L), F32)
        dBs = jnp.zeros((L, SSD_STATE), F32)
        dCs = jnp.zeros((L, SSD_STATE), F32)
        ddsk = jnp.zeros((1, 4), F32)
        dcl = jnp.zeros((1, 4), F32)
        for h in range(4):
            sl = slice(HEAD_DIM * h, HEAD_DIM * (h + 1))
            xb = (xs_ref[:, sl] * dt_c[:, h:h + 1]).astype(BF16)
            dm_s[h] = _dot(dY[:, sl].astype(BF16), xb, NT)
            dxo_s[h] = _dot(bb, dH[h].astype(BF16), NN)
        for h in range(4):
            sl = slice(HEAD_DIM * h, HEAD_DIM * (h + 1))
            onehot = (lane4 == h).astype(F32)
            cc = cum_c[:, h:h + 1]
            cl = clast[:, h:h + 1]
            lm = jnp.exp(jnp.where(causal, cc - cum_r[h:h + 1, :], NEG))
            M = (G * lm).astype(BF16)
            xh = xs_ref[:, sl]
            dth = dt_c[:, h:h + 1]
            X = xh * dth
            Xb = X.astype(BF16)
            dYh = dY[:, sl]
            dYb = dYh.astype(BF16)
            Hb = hs_ref[h].astype(BF16)
            dHh = dH[h]
            dHb = dHh.astype(BF16)
            alpha = jnp.exp(cc)
            beta = jnp.exp(cl - cc)
            dXoff = beta * dxo_s[h]
            dX = _dot(M, dYb, TN) + dXoff
            dG = dG + dm_s[h] * lm
            dCs = dCs + _dot((alpha * dYh).astype(BF16), Hb, NT)
            dBs = dBs + _dot((beta * X).astype(BF16), dHb, NT)
            ypre = Yv[:, sl] - dsk[:, h:h + 1] * xh
            ea_s[:, sl] = dYb.astype(F32) * ypre - Xb.astype(F32) * dX
            ex_s[:, sl] = dX * xh
            dcl_h = (_total(dHh * (jnp.exp(cl) * hs_ref[h])) + _total(Xb.astype(F32) * dXoff))
            dcl = dcl + dcl_h * onehot
            ddsk = ddsk + _total(dYh * xh) * onehot
            dxs_ref[:, sl] = dsk[:, h:h + 1] * dYh + dX * dth
            dH[h] = jnp.exp(cl) * dHh + _dot((alpha * cf).astype(BF16), dYb, TN)
        dGb = dG.astype(BF16)
        dc_ref[...] = _dot(dGb, bb, NN) + dCs
        db_ref[...] = _dot(dGb, cb, TN) + dBs
        feat = lax.broadcasted_iota(jnp.int32, (GROUP_W, 4), 0)
        head = lax.broadcasted_iota(jnp.int32, (GROUP_W, 4), 1) * HEAD_DIM
        sel = ((feat >= head) & (feat < head + HEAD_DIM)).astype(BF16)
        dA = _tri_sum(sel, ea_s[...], NN, False)
        ddtx = _tri_sum(sel, ex_s[...], NN, False)
        last = lax.broadcasted_iota(jnp.int32, (L, 1), 0) == L - 1
        dA = dA + jnp.where(last, dcl, 0.0)
        dadt = _tri_sum(tri, dA, TN, True)
        ddt = dadt * a_c + ddtx
        d_a = jnp.sum(dadt * dt_c, axis=0, keepdims=True)
        ddraw = ddt * _sigmoid(dtc_ref[...] + pcol_ref[0:1, :])
        ddt_ref[...] = ddraw
        dpar_ref[0:1, :] += jnp.sum(ddraw, axis=0, keepdims=True)
        dpar_ref[1:2, :] += d_a * a_c
        dpar_ref[2:3, :] += ddsk

    rowi = lambda b, g, i: b * nb + (nb - 1 - i)
    grp = pl.BlockSpec((L, GROUP_W), lambda b, g, i: (rowi(b, g, i), g))
    st = pl.BlockSpec((L, SSD_STATE), lambda b, g, i: (rowi(b, g, i), g))
    f = jax.ShapeDtypeStruct
    return pl.pallas_call(
        body,
        out_shape=(f((T, 1024), F32), f((T, 512), F32), f((T, 512), F32), f(dproj.shape, dproj.dtype),
                   f((SSD_GROUPS, T, 4), F32), f((B, SSD_GROUPS, 3, 4), F32), f((B, 1, 1024), F32)),
        grid=(B, SSD_GROUPS, nb),
        in_specs=[grp, grp, grp,
                  pl.BlockSpec((L, SSD_STATE), lambda b, g, i: (rowi(b, g, i), 8 + g)),
                  pl.BlockSpec((L, SSD_STATE), lambda b, g, i: (rowi(b, g, i), 12 + g)),
                  grp,
                  pl.BlockSpec((None, L, 4), lambda b, g, i: (g, rowi(b, g, i), 0)),
                  pl.BlockSpec((None, 4, L), lambda b, g, i: (g, 0, rowi(b, g, i))),
                  pl.BlockSpec((None, 3, 4), lambda b, g, i: (g, 0, 0)),
                  pl.BlockSpec((None, 4, 3), lambda b, g, i: (g, 0, 0)),
                  pl.BlockSpec((1, GROUP_W), lambda b, g, i: (0, g)),
                  pl.BlockSpec((None, None, None, 4, SSD_STATE, HEAD_DIM), lambda b, g, i: (b, g, nb - 1 - i, 0, 0, 0)),
                  ANY],
        out_specs=(grp, st, st, grp,
                   pl.BlockSpec((None, L, 4), lambda b, g, i: (g, rowi(b, g, i), 0)),
                   pl.BlockSpec((None, None, 3, 4), lambda b, g, i: (b, g, 0, 0)),
                   pl.BlockSpec((None, 1, GROUP_W), lambda b, g, i: (b, 0, g))),
        input_output_aliases={12: 3},
        scratch_shapes=[pltpu.VMEM((4, SSD_STATE, HEAD_DIM), F32), pltpu.VMEM((4, L, L), F32),
                        pltpu.VMEM((4, L, HEAD_DIM), F32), pltpu.VMEM((L, GROUP_W), F32),
                        pltpu.VMEM((L, GROUP_W), F32)], name=name,
        compiler_params=_cp("parallel", "parallel", "arbitrary"))(
            dyn, Y, xc, xc, xc, proj, dtc, dtr, pcol, prow, nw, hs, dproj)


def _head_sel():
    sel = (np.arange(1024)[:, None] // HEAD_DIM == np.arange(ATT_HEADS)[None, :]).astype(np.float32)
    return jnp.asarray(sel, BF16), jnp.asarray(sel.T, BF16)


def _head_rms(xv, sel, selT):
    ms = _tri_sum(sel, xv * xv, NN, False, 1) * (1.0 / HEAD_DIM)
    return _tri_sum(selT, lax.rsqrt(ms + EPS), NN, False, 2)


def _headnorm_fwd(name, proj, col_block, w):
    T = proj.shape[0]
    sel, selT = _head_sel()

    def body(x_ref, w_ref, sel_ref, selT_ref, o_ref):
        xv = x_ref[...]
        o_ref[...] = (xv * _head_rms(xv, sel_ref[...], selT_ref[...]) * w_ref[...]).astype(BF16)

    full = lambda shp: pl.BlockSpec(shp, lambda i: (0, 0))
    return pl.pallas_call(
        body, out_shape=jax.ShapeDtypeStruct((T, 1024), BF16), grid=(T // ROW_T,),
        in_specs=[pl.BlockSpec((ROW_T, 1024), lambda i: (i, col_block)), full((1, 1024)), full((1024, ATT_HEADS)),
                  full((ATT_HEADS, 1024))],
        out_specs=pl.BlockSpec((ROW_T, 1024), lambda i: (i, 0)), name=name, compiler_params=_cp("parallel"))(
            proj, jnp.tile(w, (1, ATT_HEADS)), sel, selT)


def _headnorm_bwd(name, dn, proj, col_block, w, dproj):
    T = proj.shape[0]
    sel, selT = _head_sel()

    def body(dn_ref, x_ref, w_ref, sel_ref, selT_ref, buf_ref, dx_ref, dw_ref):
        @pl.when(pl.program_id(0) == 0)
        def _():
            dw_ref[...] = jnp.zeros_like(dw_ref)

        xv = x_ref[...]
        sl, slT = sel_ref[...], selT_ref[...]
        rb = _head_rms(xv, sl, slT)
        xhat = xv * rb
        dnv = dn_ref[...]
        dxhat = dnv * w_ref[...]
        mean = _tri_sum(slT, _tri_sum(sl, dxhat * xhat, NN, False, 2) * (1.0 / HEAD_DIM), NN, False, 2)
        dx_ref[...] = (rb * (dxhat - xhat * mean)).astype(BF16)
        dw_ref[...] += jnp.sum(dnv * xhat, axis=0, keepdims=True)

    here = pl.BlockSpec((ROW_T, 1024), lambda i: (i, col_block))
    full = lambda shp: pl.BlockSpec(shp, lambda i: (0, 0))
    dx, dw = pl.pallas_call(
        body, out_shape=(jax.ShapeDtypeStruct(dproj.shape, dproj.dtype), jax.ShapeDtypeStruct((1, 1024), F32)),
        grid=(T // ROW_T,),
        in_specs=[pl.BlockSpec((ROW_T, 1024), lambda i: (i, 0)), here, full((1, 1024)), full((1024, ATT_HEADS)),
                  full((ATT_HEADS, 1024)), ANY],
        out_specs=(here, full((1, 1024))), input_output_aliases={5: 0},
        name=name, compiler_params=_cp("arbitrary"))(dn, proj, jnp.tile(w, (1, ATT_HEADS)), sel, selT, dproj)
    return dx, jnp.sum(dw.reshape(ATT_HEADS, HEAD_DIM), axis=0, keepdims=True)


def _att_bias(nq):
    j = np.arange(ATT_B)[:, None]
    i = np.arange(ATT_B)[None, :]
    out = np.empty((nq, ATT_B, ATT_B), np.float32)
    for dblk in range(nq):
        dl = ATT_B * dblk + i - j
        cnt = ((dl >= 0) & (dl <= 128)).astype(np.float32)
        cnt += ((dl >= 0) & (dl % 4 == 0) & (dl <= 512))
        cnt += ((dl >= 0) & (dl % 16 == 0) & (dl <= 2048))
        out[dblk] = np.where(cnt > 0, np.log(np.maximum(cnt, 1.0)), NEG)
    return jnp.asarray(out)


def _row_pair(nq):
    def f(r, c):
        first = c <= r
        return jnp.where(first, r, nq - 1 - r), jnp.where(first, c, c - (r + 1))
    return f


def _col_pair(nq):
    def f(r, c):
        first = c < nq - r
        kj = jnp.where(first, r, nq - 1 - r)
        return jnp.where(first, r + c, nq - 1 - r + (c - (nq - r))), kj
    return f


ATT_SCALE = 1.0 / math.sqrt(HEAD_DIM)
ATT_HS = 4
ATT_W = ATT_HS * HEAD_DIM


def _att_maps(nq, qk):
    return dict(
        q_tok=lambda b, g, r, c: (b * nq + qk(r, c)[0], g),
        k_tok=lambda b, g, r, c: (b * nq + qk(r, c)[1], g),
        v_tok=lambda b, g, r, c: (b * nq + qk(r, c)[1], COL_V // ATT_W + g),
        q_feat=lambda b, g, r, c: (g, b * nq + qk(r, c)[0]),
        k_feat=lambda b, g, r, c: (g, b * nq + qk(r, c)[1]),
        bias=lambda b, g, r, c: (qk(r, c)[0] - qk(r, c)[1], 0, 0),
        lse=lambda b, g, r, c: (g, 0, b * nq + qk(r, c)[0]),
        do_tok=lambda b, g, r, c: (b * nq + qk(r, c)[0], ATT_HS + g))


def _att_fwd(name, kn, qT, vT, bias, B):
    T = kn.shape[0]
    nq = (T // B) // ATT_B
    qk = _row_pair(nq)
    mp = _att_maps(nq, qk)

    def body(k_ref, qT_ref, vT_ref, bias_ref, oT_ref, lse_ref, m_s, l_s, acc_s, s_s):
        qi, kj = qk(pl.program_id(2), pl.program_id(3))

        @pl.when(kj == 0)
        def _():
            m_s[...] = jnp.full_like(m_s, NEG)
            l_s[...] = jnp.zeros_like(l_s)
            acc_s[...] = jnp.zeros_like(acc_s)

        bv = bias_ref[...]
        for h in range(ATT_HS):
            rs = slice(HEAD_DIM * h, HEAD_DIM * (h + 1))
            s_s[h] = _dot(k_ref[:, rs], qT_ref[rs, :], NN)
        for h in range(ATT_HS):
            rs = slice(HEAD_DIM * h, HEAD_DIM * (h + 1))
            s = s_s[h] + bv
            m_prev = m_s[h:h + 1, :]
            m_new = jnp.maximum(m_prev, jnp.max(s, axis=0, keepdims=True))
            alpha = jnp.exp(m_prev - m_new)
            p = jnp.exp(s - m_new)
            l_s[h:h + 1, :] = alpha * l_s[h:h + 1, :] + jnp.sum(p, axis=0, keepdims=True)
            acc_s[rs, :] = alpha * acc_s[rs, :] + _dot(vT_ref[rs, :], p.astype(BF16), NN)
            m_s[h:h + 1, :] = m_new

        @pl.when(kj == qi)
        def _():
            for h in range(ATT_HS):
                rs = slice(HEAD_DIM * h, HEAD_DIM * (h + 1))
                oT_ref[rs, :] = (acc_s[rs, :] / l_s[h:h + 1, :]).astype(BF16)
            lse_ref[...] = m_s[...] + jnp.log(l_s[...])

    tok = (ATT_B, ATT_W)
    feat = (ATT_W, ATT_B)
    return pl.pallas_call(
        body,
        out_shape=(jax.ShapeDtypeStruct((1024, T), BF16), jax.ShapeDtypeStruct((ATT_HEADS // ATT_HS, ATT_HS, T), F32)),
        grid=(B, ATT_HEADS // ATT_HS, nq // 2, nq + 1),
        in_specs=[pl.BlockSpec(tok, mp["k_tok"]), pl.BlockSpec(feat, mp["q_feat"]), pl.BlockSpec(feat, mp["k_feat"]),
                  pl.BlockSpec((None, ATT_B, ATT_B), mp["bias"])],
        out_specs=(pl.BlockSpec(feat, mp["q_feat"]), pl.BlockSpec((None, ATT_HS, ATT_B), mp["lse"])),
        scratch_shapes=[pltpu.VMEM((ATT_HS, ATT_B), F32), pltpu.VMEM((ATT_HS, ATT_B), F32),
                        pltpu.VMEM((ATT_W, ATT_B), F32), pltpu.VMEM((ATT_HS, ATT_B, ATT_B), F32)],
        name=name, compiler_params=_cp("parallel", "parallel", "arbitrary", "arbitrary"))(kn, qT, vT, bias)


def _att_scores(k_ref, qT_ref, v_ref, doT_ref, s_s, dp_s):
    for h in range(ATT_HS):
        rs = slice(HEAD_DIM * h, HEAD_DIM * (h + 1))
        s_s[h] = _dot(k_ref[:, rs], qT_ref[rs, :], NN)
        dp_s[h] = _dot(v_ref[:, rs].astype(BF16), doT_ref[rs, :].astype(BF16), NN)


def _att_p_ds(s_s, dp_s, doT_ref, oT_ref, lse_ref, bv, h):
    rs = slice(HEAD_DIM * h, HEAD_DIM * (h + 1))
    delta = jnp.sum(doT_ref[rs, :] * oT_ref[rs, :].astype(F32), axis=0, keepdims=True)
    p = jnp.exp(s_s[h] + bv - lse_ref[h:h + 1, :])
    return p, p * (dp_s[h] - delta)


def _att_bwd(name, kn, qT, proj, qn, knT, bias, doT, oT, lse, dyn, dproj, B):
    T = kn.shape[0]
    S = T // B
    nq = S // ATT_B
    qk = _col_pair(nq)
    mp = _att_maps(nq, qk)

    def body(k_ref, qT_ref, v_ref, q_ref, kT_ref, bias_ref, doT_ref, oT_ref, lse_ref, do_ref, buf_ref,
             dqT_ref, dk_ref, dv_ref, dk_s, dv_s, dq_s, s_s, dp_s):
        r, c = pl.program_id(2), pl.program_id(3)
        qi, kj = qk(r, c)

        @pl.when((r == 0) & (c == 0))
        def _():
            dq_s[...] = jnp.zeros_like(dq_s)

        @pl.when(qi == kj)
        def _():
            dk_s[...] = jnp.zeros_like(dk_s)
            dv_s[...] = jnp.zeros_like(dv_s)

        bv = bias_ref[...]
        _att_scores(k_ref, qT_ref, v_ref, doT_ref, s_s, dp_s)
        dq_blk = dq_s.at[qi]
        for h in range(ATT_HS):
            rs = slice(HEAD_DIM * h, HEAD_DIM * (h + 1))
            p, ds = _att_p_ds(s_s, dp_s, doT_ref, oT_ref, lse_ref, bv, h)
            dsb = ds.astype(BF16)
            dv_s[h] += _dot(p.astype(BF16), do_ref[:, rs].astype(BF16), NN)
            dk_s[h] += _dot(dsb, q_ref[:, rs], NN)
            dq_blk[rs, :] += _dot(kT_ref[rs, :], dsb, NN)

        @pl.when(qi == nq - 1)
        def _():
            for h in range(ATT_HS):
                rs = slice(HEAD_DIM * h, HEAD_DIM * (h + 1))
                dk_ref[:, rs] = dk_s[h] * ATT_SCALE
                dv_ref[:, rs] = dv_s[h].astype(BF16)

        @pl.when((r == nq // 2 - 1) & (c == nq))
        def _():
            for q in range(nq):
                dqT_ref[:, ATT_B * q:ATT_B * (q + 1)] = dq_s[q] * ATT_SCALE

    tok = (ATT_B, ATT_W)
    feat = (ATT_W, ATT_B)
    v_cb = COL_V // ATT_W
    return pl.pallas_call(
        body,
        out_shape=(jax.ShapeDtypeStruct((1024, T), F32), jax.ShapeDtypeStruct((T, 1024), F32),
                   jax.ShapeDtypeStruct(dproj.shape, dproj.dtype)),
        grid=(B, ATT_HEADS // ATT_HS, nq // 2, nq + 1),
        in_specs=[pl.BlockSpec(tok, mp["k_tok"]), pl.BlockSpec(feat, mp["q_feat"]), pl.BlockSpec(tok, mp["v_tok"]),
                  pl.BlockSpec(tok, mp["q_tok"]), pl.BlockSpec(feat, mp["k_feat"]),
                  pl.BlockSpec((None, ATT_B, ATT_B), mp["bias"]),
                  pl.BlockSpec(feat, mp["q_feat"]), pl.BlockSpec(feat, mp["q_feat"]),
                  pl.BlockSpec((None, ATT_HS, ATT_B), mp["lse"]), pl.BlockSpec(tok, mp["do_tok"]), ANY],
        out_specs=(pl.BlockSpec((ATT_W, S), lambda b, g, r, c: (g, b)),
                   pl.BlockSpec(tok, mp["k_tok"]),
                   pl.BlockSpec(tok, lambda b, g, r, c: (b * nq + qk(r, c)[1], v_cb + g))),
        input_output_aliases={10: 2},
        scratch_shapes=[pltpu.VMEM((ATT_HS, ATT_B, HEAD_DIM), F32), pltpu.VMEM((ATT_HS, ATT_B, HEAD_DIM), F32),
                        pltpu.VMEM((nq, ATT_W, ATT_B), F32),
                        pltpu.VMEM((ATT_HS, ATT_B, ATT_B), F32), pltpu.VMEM((ATT_HS, ATT_B, ATT_B), F32)],
        name=name, compiler_params=_cp("parallel", "parallel", "arbitrary", "arbitrary"))(
            kn, qT, proj, qn, knT, bias, doT, oT, lse, dyn, dproj)


def _group_cols(v):
    return v.reshape(SSD_GROUPS, 4)


def _ssd_params(p):
    rows = jnp.stack([_group_cols(p["dt_bias"]), _group_cols(p["a_log"]), _group_cols(p["d_skip"])], axis=1)
    return rows, jnp.swapaxes(rows, 1, 2)


def _dymix(name, dx, wout):
    T = dx.shape[0]

    def body(dx_ref, w_ref, o_ref):
        dxb = dx_ref[...].astype(BF16)
        for n in range(N_SHARD):
            o_ref[:, MIX_SH * n:MIX_SH * (n + 1)] = _dot(dxb, w_ref[n], NT)

    return pl.pallas_call(
        body, out_shape=jax.ShapeDtypeStruct((T, MIX_W), F32), grid=(T // ROW_T,),
        in_specs=[pl.BlockSpec((ROW_T, D_MODEL), lambda i: (i, 0)),
                  pl.BlockSpec((N_SHARD, MIX_SH, D_MODEL), lambda i: (0, 0, 0))],
        out_specs=pl.BlockSpec((ROW_T, MIX_W), lambda i: (i, 0)), name=name, compiler_params=_cp("parallel"))(dx, wout)


def _mixer_fwd(tag, x1, h2, p, weights, bias, B):
    T = x1.shape[0]
    S = T // B
    nt = T // ROW_T
    wi = weights("win", h2)
    win, cw = wi["win"], wi["cw"]
    proj = _mm(tag + "_proj",
               [(h2, pl.BlockSpec((ROW_T, D_MODEL), lambda j, i, k: (i, 0)),
                 win, pl.BlockSpec((D_MODEL, PROJ_TN), lambda j, i, k: (0, j)))],
               jax.ShapeDtypeStruct((T, IN_PAD), F32), pl.BlockSpec((ROW_T, PROJ_TN), lambda j, i, k: (i, j)),
               (IN_PAD // PROJ_TN, nt, 1), NN, (ROW_T, PROJ_TN))
    xc = _conv_fwd(tag + "_conv", proj, cw, p["conv_b"][None], B)
    dtraw = proj[:, COL_DT:COL_DT + SSD_HEADS].reshape(T, SSD_GROUPS, 4)
    dtc = jnp.transpose(dtraw, (1, 0, 2))
    dtr = jnp.transpose(dtraw, (1, 2, 0))
    pcol, prow = _ssd_params(p)
    Y, y_ssd, hs = _ssd_fwd(tag + "_ssd", xc, proj, dtc, dtr, pcol, prow, p["ssd_norm"][None], B)
    qn = _headnorm_fwd(tag + "_qn", proj, COL_Q // 1024, p["q_norm"][None])
    kn = _headnorm_fwd(tag + "_kn", proj, COL_K // 1024, p["k_norm"][None])
    qT = (qn * ATT_SCALE).T
    oT, lse = _att_fwd(tag + "_att", kn, qT, proj[:, COL_V:COL_V + 1024].T.astype(BF16), bias, B)
    ymix = jnp.concatenate([y_ssd, oT.T], axis=1)
    rest = weights("rest", ymix)
    o = _with_norm(T, p["ffn2_norm"][None])
    x2, h3 = _mm(tag + "_out",
                 [(ymix, pl.BlockSpec((ROW_T, MIX_SH), lambda i, n, k, j=j: (i, j)),
                   rest["wout"], pl.BlockSpec((None, MIX_SH, D_MODEL), lambda i, n, k, j=j: (j, 0, 0)))
                  for j in range(N_SHARD)],
                 o.pop("out_shape"), o.pop("out_spec"), (nt, 1, 1), NN, (ROW_T, D_MODEL), res=(x1, _row3()), **o)
    saved = dict(x1=x1, h2=h2, proj=proj, xc=xc, dtc=dtc, dtr=dtr, Y=Y, hs=hs,
                 qn=qn, kn=kn, qT=qT, oT=oT, lse=lse, ymix=ymix, win=win, cw=cw, wout=rest["wout"])
    return x2, h3, saved


def _mixer_bwd(tag, dx2, sv, p, bias, B):
    T = dx2.shape[0]
    S = T // B
    nt = T // ROW_T
    sg = {}
    dymix = _dymix(tag + "_dymix", dx2, sv["wout"])
    tkw = min(TK_W, T)
    gwout = _mm(tag + "_dwout",
                [(sv["ymix"], pl.BlockSpec((tkw, MIX_SH), lambda m, n, k: (k, m)),
                  dx2, pl.BlockSpec((tkw, D_MODEL), lambda m, n, k: (k, 0)))],
                jax.ShapeDtypeStruct((N_SHARD, MIX_SH, D_MODEL), BF16),
                pl.BlockSpec((None, MIX_SH, D_MODEL), lambda m, n, k: (m, 0, 0)),
                (N_SHARD, 1, T // tkw), TN, (MIX_SH, D_MODEL))
    proj = sv["proj"]
    doT = dymix[:, 1024:].T
    dproj = lax.empty((T, IN_PAD), BF16)
    dqT, dkn, dproj = _att_bwd(tag + "_attb", sv["kn"], sv["qT"], proj, sv["qn"], sv["kn"].T, bias, doT, sv["oT"],
                               sv["lse"], dymix, dproj, B)
    dproj, sg["q_norm"] = _headnorm_bwd(tag + "_qnb", dqT.T, proj, COL_Q // 1024, p["q_norm"][None], dproj)
    dproj, sg["k_norm"] = _headnorm_bwd(tag + "_knb", dkn, proj, COL_K // 1024, p["k_norm"][None], dproj)
    pcol, prow = _ssd_params(p)
    dxs, dB, dC, dproj, ddt, dpar, dnw = _ssd_bwd(tag + "_ssdb", dymix, sv["Y"], sv["xc"], proj, sv["dtc"], sv["dtr"],
                                                  pcol, prow, p["ssd_norm"][None], sv["hs"], dproj, B)
    dpar = jnp.sum(dpar, axis=0)
    sg["dt_bias"] = dpar[:, 0, :].reshape(SSD_HEADS)
    sg["a_log"] = dpar[:, 1, :].reshape(SSD_HEADS)
    sg["d_skip"] = dpar[:, 2, :].reshape(SSD_HEADS)
    sg["ssd_norm"] = jnp.sum(dnw, axis=0)
    dproj, sg["conv_w"], sg["conv_b"] = _conv_bwd(tag + "_convb", proj, dxs, dB, dC, sv["cw"], p["conv_b"][None],
                                                  dproj, B)
    ddt16 = jnp.transpose(ddt, (1, 0, 2)).reshape(T, SSD_HEADS)
    dproj = lax.dynamic_update_slice(dproj, jnp.pad(ddt16, ((0, 0), (0, IN_PAD - COL_DT - SSD_HEADS))).astype(BF16),
                                     (0, COL_DT))
    win = sv["win"]
    gwin = _mm(tag + "_dwin",
               [(sv["h2"], pl.BlockSpec((tkw, D_MODEL), lambda n, m, k: (k, 0)),
                 dproj, pl.BlockSpec((tkw, PROJ_TN), lambda n, m, k: (k, n)))],
               jax.ShapeDtypeStruct((D_MODEL, IN_PAD), BF16), pl.BlockSpec((D_MODEL, PROJ_TN), lambda n, m, k: (0, n)),
               (IN_PAD // PROJ_TN, 1, T // tkw), TN, (D_MODEL, PROJ_TN))
    dx1, sg["mix_norm"] = _mm(
        tag + "_dh2",
        [(dproj, pl.BlockSpec((ROW_T, PROJ_TN), lambda i, n, k, j=j: (i, j)),
          win, pl.BlockSpec((D_MODEL, PROJ_TN), lambda i, n, k, j=j: (0, j))) for j in range(IN_PAD // PROJ_TN)],
        (jax.ShapeDtypeStruct((T, D_MODEL), F32), jax.ShapeDtypeStruct((1, D_MODEL), F32)), (_row3(), _vec3()),
        (nt, 1, 1), NT, (ROW_T, D_MODEL), post="rmsb",
        post_in=[(sv["x1"], _row3()), (p["mix_norm"][None], _vec3()), (dx2, _row3())])
    return dx1, sg, gwout, gwin


def _win_pack(w):
    return jnp.concatenate([w[:, :3072], w[:, 3088:], w[:, 3072:3088],
                            jnp.zeros((w.shape[0], IN_PAD - IN_PROJ), w.dtype)], axis=1)


def _win_unpack(g):
    return jnp.concatenate([g[:, :3072], g[:, COL_DT:COL_DT + SSD_HEADS], g[:, 3072:COL_DT]], axis=1)


DT_LO = IN_SH * 2 - COL_Q


def _win_from_shards(sh):
    main = IN_SH - DT_LO
    return jnp.concatenate([sh[0], sh[1][:, :main], sh[2][:, SSD_HEADS - DT_LO:], sh[3], sh[1][:, main:],
                            sh[2][:, :SSD_HEADS - DT_LO], jnp.zeros((sh.shape[1], IN_PAD - IN_PROJ), sh.dtype)], axis=1)


def _win_to_shards(g):
    main = IN_SH - DT_LO
    a, b = IN_SH + main, IN_SH + 2 * main
    return jnp.stack([g[:, :IN_SH],
                      jnp.concatenate([g[:, IN_SH:a], g[:, COL_DT:COL_DT + DT_LO]], axis=1),
                      jnp.concatenate([g[:, COL_DT + DT_LO:COL_DT + SSD_HEADS], g[:, a:b]], axis=1),
                      g[:, b:COL_DT]])


def _local_step(x, target, small, weights, scatter, B):
    T = x.shape[0]
    bias = _att_bias((T // B) // ATT_B)
    saved = []
    xl = x
    hl = _rms_fwd("l0f1_rms", x, small["ffn1_norm"][0][None])
    for l in range(DEPTH):
        tag = "l%d" % l
        p = {k: v[l] for k, v in small.items()}
        w1 = weights(l, "ffn1", hl)
        (x1, h2), ffn1, d1 = _ffn_fwd(tag + "f1", xl, hl, w1["g1"], w1["u1"],
                                      lambda after, l=l: weights(l, "ffn1d", after)["d1"], p["mix_norm"][None])
        x2, h3, sv = _mixer_fwd(tag, x1, h2, p, functools.partial(weights, l), bias, B)
        w2 = weights(l, "rest", x2)
        nxt = small["ffn1_norm"][l + 1][None] if l + 1 < DEPTH else None
        xo, ffn2, _ = _ffn_fwd(tag + "f2", x2, h3, w2["g2"], w2["u2"], w2["d2"], nxt)
        xl, hl = xo if nxt is not None else (xo, None)
        saved.append((ffn1, sv, ffn2, dict(g1=w1["g1"], u1=w1["u1"], d1=d1), w2))
    d, lsum = _loss_grad("loss", xl, target)
    sgrads = [None] * DEPTH
    for l in reversed(range(DEPTH)):
        tag = "l%db" % l
        p = {k: v[l] for k, v in small.items()}
        ffn1, sv, ffn2, w1, w2 = saved[l]
        sg = {}
        d, sg["ffn2_norm"] = _ffn_bwd(tag + "f2", d, ffn2, p["ffn2_norm"][None], w2["g2"], w2["u2"], w2["d2"],
                                      lambda gg, gu, gd, c, l=l: scatter(l, "ffn2", dict(g2=gg, u2=gu, d2=gd), c))
        d, sgm, gwout, gwin = _mixer_bwd(tag, d, sv, p, bias, B)
        sg.update(sgm)
        d = scatter(l, "mixer", dict(wout=gwout, win=gwin), d)
        d, sg["ffn1_norm"] = _ffn_bwd(tag + "f1", d, ffn1, p["ffn1_norm"][None], w1["g1"], w1["u1"], w1["d1"],
                                      lambda gg, gu, gd, c, l=l: scatter(l, "ffn1", dict(g1=gg, u1=gu, d1=gd), c))
        sgrads[l] = sg
    return lsum, d, sgrads


MESH = pl.DeviceIdType.MESH
ANY = pl.BlockSpec(memory_space=pl.ANY)


def _place():
    return lax.axis_index("x"), lax.axis_index("y"), lax.axis_index("c")


def _other_chips(x, y):
    return [(1 - x, y), (x, 1 - y), (1 - x, 1 - y)]


HBM = pl.BlockSpec(memory_space=pltpu.HBM)
SEM = pl.BlockSpec(memory_space=pltpu.SEMAPHORE)
EFFECT = pltpu.SideEffectType.DATAFLOW_SIDE_EFFECTING


def _hbm(a):
    return pltpu.with_memory_space_constraint(a, pltpu.HBM)


def _exchange(gather, layer, src, land, send, recv, n, act):
    x, y, c = _place()
    for k, (px, py) in enumerate(_other_chips(x, y)):
        for a in range(n):
            if gather:
                s_out, d_out, d_in = src[a].at[layer], land[a].at[2 * x + y], land[a].at[2 * px + py]
            else:
                s_out, d_out, d_in = src[a].at[2 * px + py], land[a].at[k], land[a].at[k]
            act(pltpu.make_async_remote_copy(
                src_ref=s_out, dst_ref=d_out if act is _start else d_in, send_sem=send.at[k * n + a],
                recv_sem=recv.at[k * n + a], device_id=(px, py, c), device_id_type=MESH))


def _start(cp):
    cp.start()


def _finish(cp):
    cp.wait_send()
    cp.wait_recv()


def _exchange_start(name, gather, layer, srcs, carry):
    n = len(srcs)
    lands = [lax.empty(((N_SHARD,) + s.shape[1:]) if gather else ((3,) + s.shape[1:]), s.dtype) for s in srcs]

    def body(*refs):
        _exchange(gather, layer, refs[:n], refs[n:2 * n], refs[2 * n + 1], refs[2 * n + 2], n, _start)

    srcs = [_hbm(a) for a in srcs]
    thru = [_hbm(a) for a in lands + [carry]]
    out = pl.pallas_call(
        body, name=name,
        out_shape=(pltpu.SemaphoreType.DMA((3 * n,)), pltpu.SemaphoreType.DMA((3 * n,)),
                   *[pltpu.HBM(a.shape, a.dtype) for a in thru]),
        in_specs=[HBM] * (2 * n + 1), out_specs=(SEM, SEM, *[HBM] * (n + 1)),
        input_output_aliases={n + i: 2 + i for i in range(n + 1)},
        compiler_params=pltpu.CompilerParams(has_side_effects=EFFECT))(*srcs, *thru)
    return dict(gather=gather, layer=layer, send=out[0], recv=out[1], srcs=srcs, lands=list(out[2:2 + n])), out[-1]


def _exchange_wait(name, ex, after):
    n = len(ex["srcs"])

    def body(*refs):
        _exchange(ex["gather"], ex["layer"], refs[:n], refs[n:2 * n], refs[2 * n], refs[2 * n + 1], n, _finish)

    out = pl.pallas_call(
        body, name=name, out_shape=[pltpu.HBM(a.shape, a.dtype) for a in ex["lands"]],
        in_specs=[HBM] * (2 * n) + [SEM, SEM, ANY], out_specs=[HBM] * n,
        input_output_aliases={n + i: i for i in range(n)},
        compiler_params=pltpu.CompilerParams(has_side_effects=EFFECT))(
            *ex["srcs"], *ex["lands"], ex["send"], ex["recv"], after)
    return list(out)


def _swap_sibling(name, parts):
    n = len(parts)

    def body(*refs):
        src, dst = refs[:n], refs[n:2 * n]
        send, recv = refs[2 * n:]
        x, y, c = _place()
        cps = [pltpu.make_async_remote_copy(src_ref=src[a], dst_ref=dst[a], send_sem=send.at[a], recv_sem=recv.at[a],
                                            device_id=(x, y, 1 - c), device_id_type=MESH) for a in range(n)]
        for cp in cps:
            cp.start()
        for cp in cps:
            cp.wait_recv()
        for cp in cps:
            cp.wait_send()

    return pl.pallas_call(
        body, out_shape=[jax.ShapeDtypeStruct(p.shape, p.dtype) for p in parts],
        in_specs=[ANY] * n, out_specs=[ANY] * n,
        scratch_shapes=[pltpu.SemaphoreType.DMA((n,)), pltpu.SemaphoreType.DMA((n,))],
        name=name)(*parts)


def _allreduce_small(name, v, after):
    R = v.shape[0]

    def body(v_ref, after_ref, o_ref, buf, send, recv):
        x, y, c = _place()
        me = 4 * x + 2 * y + c
        buf[me] = v_ref[...]
        cps = []
        for k in range(1, 8):
            fx, fy, fc = (k >> 2) & 1, (k >> 1) & 1, k & 1
            px = 1 - x if fx else x
            py = 1 - y if fy else y
            pc = 1 - c if fc else c
            cp = pltpu.make_async_remote_copy(src_ref=v_ref, dst_ref=buf.at[me], send_sem=send.at[k - 1],
                                              recv_sem=recv.at[k - 1], device_id=(px, py, pc), device_id_type=MESH)
            cp.start()
            cps.append((cp, 4 * px + 2 * py + pc))
        for k, (cp, peer) in enumerate(cps):
            pltpu.make_async_remote_copy(src_ref=v_ref, dst_ref=buf.at[peer], send_sem=send.at[k], recv_sem=recv.at[k],
                                         device_id=(x, y, c), device_id_type=MESH).wait_recv()
        for cp, _ in cps:
            cp.wait_send()
        acc = buf[0]
        for d in range(1, 8):
            acc = acc + buf[d]
        o_ref[...] = acc

    return pl.pallas_call(
        body, out_shape=jax.ShapeDtypeStruct((R, 128), F32),
        in_specs=[pl.BlockSpec(memory_space=pltpu.VMEM), ANY], out_specs=pl.BlockSpec(memory_space=pltpu.VMEM),
        scratch_shapes=[pltpu.VMEM((8, R, 128), F32), pltpu.SemaphoreType.DMA((7,)), pltpu.SemaphoreType.DMA((7,))],
        name=name)(v, after)


TILE_BYTES = 1600 * 1024


def _row_tile(r, c=1024):
    for t in (512, 352, 256, 128, 64, 32, 16, 8):
        if r % t == 0 and (t * c * 4 <= TILE_BYTES or t == 8):
            return t
    raise ValueError(r)


def _sum4(name, me, parts, got):
    _, R, C = parts.shape
    tr = _row_tile(R, C)

    def body(me_ref, o_ref, g_ref, s_ref):
        s = o_ref[...].astype(F32)
        for k in range(3):
            s = s + g_ref[k].astype(F32)
        s_ref[...] = s.astype(BF16)

    return pl.pallas_call(
        body, out_shape=jax.ShapeDtypeStruct((R, C), BF16),
        grid_spec=pltpu.PrefetchScalarGridSpec(
            num_scalar_prefetch=1, grid=(R // tr,),
            in_specs=[pl.BlockSpec((None, tr, C), lambda i, me_ref: (me_ref[0], i, 0)),
                      pl.BlockSpec((3, tr, C), lambda i, me_ref: (0, i, 0))],
            out_specs=pl.BlockSpec((tr, C), lambda i, me_ref: (i, 0))),
        name=name, compiler_params=_cp("parallel"))(me, parts, got)


def _adamw(name, w, gparts, m, v):
    R, C = w.shape
    tr = _row_tile(R, C)
    ng = len(gparts)
    c1 = 1.0 - ADAM_B1 ** ADAM_STEP
    c2 = 1.0 - ADAM_B2 ** ADAM_STEP

    def body(*refs):
        w_ref = refs[0]
        g_refs = refs[1:1 + ng]
        m_ref, v_ref, go_ref, d_ref, mo_ref, vo_ref = refs[1 + ng:]
        g = g_refs[0][...]
        for r in g_refs[1:]:
            g = g + r[...]
        mn = ADAM_B1 * m_ref[...] + (1.0 - ADAM_B1) * g
        vn = ADAM_B2 * v_ref[...] + (1.0 - ADAM_B2) * (g * g)
        go_ref[...] = g
        mo_ref[...] = mn
        vo_ref[...] = vn
        d_ref[...] = -ADAM_LR * ((mn / c1) / (jnp.sqrt(vn / c2) + ADAM_EPS) + ADAM_WD * w_ref[...])

    blk = pl.BlockSpec((tr, C), lambda i: (i, 0))
    osh = jax.ShapeDtypeStruct((R, C), F32)
    return pl.pallas_call(
        body, out_shape=(osh, osh, osh, osh), grid=(R // tr,), in_specs=[blk] * (3 + ng), out_specs=(blk,) * 4,
        name=name, compiler_params=_cp("parallel"))(w, *gparts, m, v)


def _adamw_layers(name, w, sums, m, v):
    _, R, C = w.shape
    tr = _row_tile(R, C)
    nr = R // tr
    c1 = 1.0 - ADAM_B1 ** ADAM_STEP
    c2 = 1.0 - ADAM_B2 ** ADAM_STEP

    def body(w_ref, a0, b0, a1, b1, m_ref, v_ref, go_ref, d_ref, mo_ref, vo_ref):
        f = lambda r: r[...].astype(F32)
        g = jnp.where(pl.program_id(0) == 0, f(a0) + f(b0), f(a1) + f(b1))
        mn = ADAM_B1 * m_ref[...] + (1.0 - ADAM_B1) * g
        vn = ADAM_B2 * v_ref[...] + (1.0 - ADAM_B2) * (g * g)
        go_ref[...] = g
        mo_ref[...] = mn
        vo_ref[...] = vn
        d_ref[...] = -ADAM_LR * ((mn / c1) / (jnp.sqrt(vn / c2) + ADAM_EPS) + ADAM_WD * w_ref[...])

    blk = pl.BlockSpec((None, tr, C), lambda l, i: (l, i, 0))
    lay0 = pl.BlockSpec((tr, C), lambda l, i: (jnp.where(l == 0, i, nr - 1), 0))
    lay1 = pl.BlockSpec((tr, C), lambda l, i: (jnp.where(l == 1, i, 0), 0))
    oblk = pl.BlockSpec((tr, C), lambda l, i: (l * nr + i, 0))
    osh = jax.ShapeDtypeStruct((DEPTH * R, C), F32)
    res = pl.pallas_call(
        body, out_shape=(osh, osh, osh, osh), grid=(DEPTH, nr),
        in_specs=[blk, lay0, lay0, lay1, lay1, blk, blk], out_specs=(oblk,) * 4,
        name=name, compiler_params=_cp("arbitrary", "arbitrary"))(w, *sums[0], *sums[1], m, v)
    return [r.reshape(w.shape) for r in res]


BIG = [("ffn1_w_gate", "g1"), ("ffn1_w_up", "u1"), ("ffn1_w_down", "d1"), ("w_in", "win"), ("w_out", "wout"),
       ("ffn2_w_gate", "g2"), ("ffn2_w_up", "u2"), ("ffn2_w_down", "d2")]
SMALL = ["ffn1_norm", "mix_norm", "conv_b", "dt_bias", "a_log", "d_skip", "ssd_norm", "q_norm", "k_norm", "ffn2_norm"]
WEIGHTS = ["ffn1_norm", "ffn1_w_gate", "ffn1_w_up", "ffn1_w_down", "mix_norm", "w_in", "conv_w", "conv_b", "dt_bias",
           "a_log", "d_skip", "ssd_norm", "q_norm", "k_norm", "w_out", "ffn2_norm", "ffn2_w_gate", "ffn2_w_up",
           "ffn2_w_down"]
CONV_SH = CONV_DIM // N_SHARD
TRANSPOSED = ("g1", "u1", "g2", "u2")
GATHER_GROUPS = [(0, "ffn1", ["g1", "u1"]), (0, "ffn1d", ["d1"]), (0, "win", ["win", "cw"]),
                 (0, "rest", ["wout", "g2", "u2", "d2"]),
                 (1, "all", ["g1", "u1", "d1", "win", "cw", "wout", "g2", "u2", "d2"])]


def _pad128(v):
    v = v.reshape(-1)
    return jnp.pad(v, (0, (-v.shape[0]) % 128))


def _pack(pieces):
    flat, offs, pos = [], [], 0
    for p in pieces:
        q = _pad128(p.astype(F32))
        offs.append(pos)
        pos += q.shape[0] // 128
        flat.append(q)
    total = -(-pos // 8) * 8
    out = jnp.concatenate(flat + [jnp.zeros(((total - pos) * 128,), F32)]).reshape(total, 128)
    return out, offs


def _unpack(packed, offs, shapes):
    out = []
    for off, shp in zip(offs, shapes):
        n = int(np.prod(shp))
        rows = -(-n // 128)
        out.append(packed[off:off + rows].reshape(-1)[:n].reshape(shp))
    return out


def kernel(x, ffn1_norm, ffn1_w_gate, ffn1_w_up, ffn1_w_down, mix_norm, w_in, conv_w, conv_b, dt_bias, a_log, d_skip, ssd_norm, q_norm, k_norm, w_out, ffn2_norm, ffn2_w_gate, ffn2_w_up, ffn2_w_down, loss_target, m_ffn1_norm, m_ffn1_w_gate, m_ffn1_w_up, m_ffn1_w_down, m_mix_norm, m_w_in, m_conv_w, m_conv_b, m_dt_bias, m_a_log, m_d_skip, m_ssd_norm, m_q_norm, m_k_norm, m_w_out, m_ffn2_norm, m_ffn2_w_gate, m_ffn2_w_up, m_ffn2_w_down, v_ffn1_norm, v_ffn1_w_gate, v_ffn1_w_up, v_ffn1_w_down, v_mix_norm, v_w_in, v_conv_w, v_conv_b, v_dt_bias, v_a_log, v_d_skip, v_ssd_norm, v_q_norm, v_k_norm, v_w_out, v_ffn2_norm, v_ffn2_w_gate, v_ffn2_w_up, v_ffn2_w_down):
    A = dict(locals())
    ix, iy, ic = _place()
    me = 2 * ix + iy
    B, S, _ = x.shape
    T = B * S

    view = lambda a, key: jnp.swapaxes(a, 1, 2) if key in TRANSPOSED else a
    own = {key: view(A[name], key).astype(BF16) for name, key in BIG}
    own["cw"] = conv_w
    exs, first_norm = [], ffn1_norm
    for gi, (l, _, keys) in enumerate(GATHER_GROUPS):
        ex, first_norm = _exchange_start("gather_start%d" % gi, True, l, [own[key] for key in keys], first_norm)
        exs.append(ex)
    landed = {}

    def weights(l, group, after):
        gi = [i for i, (gl, gname, _) in enumerate(GATHER_GROUPS) if gl == l and gname in (group, "all")][0]
        if gi not in landed:
            lands = _exchange_wait("gather_wait%d" % gi, exs[gi], after)
            landed[gi] = {}
            for key, land in zip(GATHER_GROUPS[gi][2], lands):
                full = lax.dynamic_update_slice(land, own[key][l][None], (me, 0, 0))
                if key == "win":
                    full = _win_from_shards(full)
                if key == "cw":
                    full = jnp.transpose(full, (1, 0, 2)).reshape(CONV_K, CONV_DIM)
                landed[gi][key] = full
        return landed[gi]

    pending = []

    def scatter(l, group, grads, carry):
        keys = sorted(grads)
        arrs = [grads[key] for key in keys]
        if "win" in grads:
            arrs[keys.index("win")] = _win_to_shards(grads["win"])
        ex, carry = _exchange_start("scatter_start_l%d_%s" % (l, group), False, None, arrs, carry)
        pending.append((l, keys, ex))
        return carry

    small = {name: A[name] for name in SMALL}
    small["ffn1_norm"] = first_norm
    lsum, dx, sgrads = _local_step(x.reshape(T, D_MODEL), loss_target.reshape(T, D_MODEL), small, weights, scatter, B)

    names = SMALL + ["conv_w"]
    shapes = [A[n].shape for n in SMALL] + [(DEPTH, CONV_K, CONV_DIM), ()]
    pieces = [jnp.stack([sgrads[l][n].reshape(shp[1:]) for l in range(DEPTH)]) for n, shp in zip(names, shapes)]
    pieces.append(0.5 / D_MODEL * jnp.sum(lsum))
    packed, offs = _pack(pieces)

    sums, theirs, out = {}, {}, {}
    me1 = jnp.reshape(me, (1,)).astype(jnp.int32)

    def update(tag, after):
        todo = [k for k in sums if k not in theirs]
        theirs.update(zip(todo, _swap_sibling("swap_sibling_" + tag, [sums[k] for k in todo])))
        for name, key in BIG:
            if name not in out and all((key, l) in theirs for l in range(DEPTH)):
                res = _adamw_layers("adamw_" + key, view(A[name], key),
                                    [(sums[key, l], theirs[key, l]) for l in range(DEPTH)],
                                    view(A["m_" + name], key), view(A["v_" + name], key))
                out[name] = [view(r, key) for r in res]
                after = res[0]
        return after

    after = dx
    for idx, (l, keys, ex) in enumerate(pending):
        if idx == len(pending) - 1:
            after = update("a", after)
        lands = _exchange_wait("scatter_wait%d" % idx, ex, after)
        for key, g, got in zip(keys, ex["srcs"], lands):
            sums[key, l] = after = _sum4("sum_%s_l%d" % (key, l), me1, g, got)
    after = update("b", after)

    red = _unpack(_allreduce_small("allreduce_small", packed, after), offs, shapes)
    loss = red[-1]
    sg = dict(zip(names, red[:-1]))

    wp, offs = _pack([A[n] for n in SMALL])
    gp, _ = _pack([sg[n] for n in SMALL])
    mp, _ = _pack([A["m_" + n] for n in SMALL])
    vp, _ = _pack([A["v_" + n] for n in SMALL])
    res = _adamw("adamw_small", wp, [gp], mp, vp)
    shapes = [A[n].shape for n in SMALL]
    res = [_unpack(r, offs, shapes) for r in res]
    for i, n in enumerate(SMALL):
        out[n] = [res[q][i] for q in range(4)]
    gcw = lax.dynamic_slice_in_dim(sg["conv_w"], me * CONV_SH, CONV_SH, axis=2)
    flat = lambda a: a.reshape(DEPTH * CONV_K, CONV_SH)
    res = _adamw("adamw_conv_w", flat(conv_w), [flat(gcw)], flat(m_conv_w), flat(v_conv_w))
    out["conv_w"] = [r.reshape(conv_w.shape) for r in res]

    outs = [loss, dx.reshape(B, S, D_MODEL)]
    for q in range(4):
        outs += [out[n][q] for n in WEIGHTS]
    return tuple(outs)
```

```python
import functools
import math

import numpy as np
import jax
import jax.numpy as jnp
from jax import lax
from jax.experimental import pallas as pl
from jax.experimental.pallas import tpu as pltpu

F32 = jnp.float32
BF16 = jnp.bfloat16

D_MODEL = 1024
DEPTH = 2
N_SHARD = 4
D_FF = 2816
FF_SH = D_FF // N_SHARD
SSD_HEADS = 16
HEAD_DIM = 64
SSD_GROUPS = 4
GROUP_W = 256
SSD_STATE = 128
CONV_K = 4
CONV_DIM = 2048
ATT_HEADS = 16
MIX_W = 2048
MIX_SH = MIX_W // N_SHARD
IN_PROJ = 6160
IN_SH = IN_PROJ // N_SHARD
IN_PAD = 6272
PROJ_TN = 896
COL_Z, COL_XBC, COL_Q, COL_K, COL_V, COL_DT = 0, 1024, 3072, 4096, 5120, 6144
EPS = 1e-6
NEG = -1e30
SSD_L = 256
ATT_B = 512
ROW_T = 512
HALF_T = ROW_T // 2
TK_W = 2048
CONV_CT = 256
CONV_R = 256
PAD_R = 8

ADAM_LR, ADAM_B1, ADAM_B2, ADAM_EPS, ADAM_WD, ADAM_STEP = 0.001, 0.9, 0.999, 1e-08, 0.01, 10

NN = (((1,), (0,)), ((), ()))
NT = (((1,), (1,)), ((), ()))
TN = (((0,), (0,)), ((), ()))

VMEM_LIMIT = 56 * 1024 * 1024


def _cp(*sem):
    return pltpu.CompilerParams(dimension_semantics=sem, vmem_limit_bytes=VMEM_LIMIT)


def _dot(a, b, dims):
    return lax.dot_general(a, b, dims, preferred_element_type=F32)


def _sigmoid(x):
    return 0.5 * jnp.tanh(0.5 * x) + 0.5


def _softplus(x):
    return jnp.maximum(x, 0.0) + jnp.log(1.0 + jnp.exp(-jnp.abs(x)))


def _mm(name, pairs, out_shape, out_spec, grid, dims, acc_shape, res=None, scale=1.0, post=None, post_in=()):
    nk = grid[2]
    npair = len(pairs)
    npost = len(post_in)

    def body(*refs):
        ab = refs[:2 * npair]
        pos = 2 * npair
        res_ref = None
        if res is not None:
            res_ref = refs[pos]
            pos += 1
        pin = refs[pos:pos + npost]
        pos += npost
        out_ref = refs[pos]
        pos += 1
        if post is not None:
            out2_ref = refs[pos]
            pos += 1
        s = None
        for p in range(npair):
            d = _dot(ab[2 * p][...].astype(BF16), ab[2 * p + 1][...].astype(BF16), dims)
            s = d if s is None else s + d

        def finish(r):
            if scale != 1.0:
                r = r * scale
            if res_ref is not None:
                r = r + res_ref[...]
            if post == "rmsb":
                @pl.when(pl.program_id(0) == 0)
                def _():
                    out2_ref[...] = jnp.zeros_like(out2_ref)

                xv = pin[0][...]
                rr = lax.rsqrt(jnp.mean(xv * xv, axis=-1, keepdims=True) + EPS)
                xhat = xv * rr
                dxhat = r * pin[1][...]
                out_ref[...] = pin[2][...] + rr * (dxhat - xhat * jnp.mean(dxhat * xhat, axis=-1, keepdims=True))
                out2_ref[...] += jnp.sum(r * xhat, axis=0, keepdims=True)
                return
            out_ref[...] = r.astype(out_ref.dtype)
            if post == "norm":
                rr = lax.rsqrt(jnp.mean(r * r, axis=-1, keepdims=True) + EPS)
                out2_ref[...] = (r * rr * pin[0][...]).astype(BF16)

        if nk == 1:
            finish(s)
            return
        acc = refs[pos]
        k = pl.program_id(2)

        @pl.when(k == 0)
        def _():
            acc[...] = s

        @pl.when(k > 0)
        def _():
            acc[...] += s

        @pl.when(k == nk - 1)
        def _():
            finish(acc[...])

    args, specs = [], []
    for a, a_spec, b, b_spec in pairs:
        args += [a, b]
        specs += [a_spec, b_spec]
    for arr, spec in ([res] if res is not None else []) + list(post_in):
        args.append(arr)
        specs.append(spec)
    sems = ("arbitrary",) * 3 if post == "rmsb" else ("parallel", "parallel", "arbitrary")
    return pl.pallas_call(
        body, out_shape=out_shape, grid=grid, in_specs=specs, out_specs=out_spec,
        scratch_shapes=[] if nk == 1 else [pltpu.VMEM(acc_shape, F32)], name=name,
        compiler_params=_cp(*sems))(*args)


def _rms_fwd(name, x, w):
    T = x.shape[0]

    def body(x_ref, w_ref, o_ref):
        xv = x_ref[...]
        r = lax.rsqrt(jnp.mean(xv * xv, axis=-1, keepdims=True) + EPS)
        o_ref[...] = (xv * r * w_ref[...]).astype(BF16)

    return pl.pallas_call(
        body, out_shape=jax.ShapeDtypeStruct((T, D_MODEL), BF16), grid=(T // ROW_T,),
        in_specs=[pl.BlockSpec((ROW_T, D_MODEL), lambda i: (i, 0)), pl.BlockSpec((1, D_MODEL), lambda i: (0, 0))],
        out_specs=pl.BlockSpec((ROW_T, D_MODEL), lambda i: (i, 0)), name=name, compiler_params=_cp("parallel"))(x, w)


def _loss_grad(name, y, t):
    T = y.shape[0]

    def body(y_ref, t_ref, dy_ref, l_ref):
        @pl.when(pl.program_id(0) == 0)
        def _():
            l_ref[...] = jnp.zeros_like(l_ref)

        e = y_ref[...] - t_ref[...]
        dy_ref[...] = e * (1.0 / D_MODEL)
        l_ref[...] += jnp.sum(e * e, axis=0, keepdims=True)

    row = pl.BlockSpec((ROW_T, D_MODEL), lambda i: (i, 0))
    vec = pl.BlockSpec((1, D_MODEL), lambda i: (0, 0))
    return pl.pallas_call(
        body, out_shape=(jax.ShapeDtypeStruct((T, D_MODEL), F32), jax.ShapeDtypeStruct((1, D_MODEL), F32)),
        grid=(T // ROW_T,), in_specs=[row, row], out_specs=(row, vec), name=name,
        compiler_params=_cp("arbitrary"))(y, t)


def _ffn_gate_up(name, h, wg, wu):
    T = h.shape[0]

    def body(h_ref, wg_ref, wu_ref, dgf_ref, duf_ref, a_ref):
        for r in range(0, ROW_T, HALF_T):
            rows = slice(r, r + HALF_T)
            hv = h_ref[rows, :]
            g = _dot(hv, wg_ref[...], NT)
            u = _dot(hv, wu_ref[...], NT)
            sg = _sigmoid(g)
            silu = g * sg
            dgf_ref[rows, :] = (u * (sg * (1.0 + g * (1.0 - sg)))).astype(BF16)
            duf_ref[rows, :] = silu.astype(BF16)
            a_ref[rows, :] = (silu * u).astype(BF16)

    wspec = pl.BlockSpec((None, FF_SH, D_MODEL), lambda j, i: (j, 0, 0))
    ospec = pl.BlockSpec((None, ROW_T, FF_SH), lambda j, i: (j, i, 0))
    osh = jax.ShapeDtypeStruct((N_SHARD, T, FF_SH), BF16)
    return pl.pallas_call(
        body, out_shape=(osh, osh, osh), grid=(N_SHARD, T // ROW_T),
        in_specs=[pl.BlockSpec((ROW_T, D_MODEL), lambda j, i: (i, 0)), wspec, wspec],
        out_specs=(ospec, ospec, ospec), name=name, compiler_params=_cp("parallel", "parallel"))(h, wg, wu)


def _ffn_dact(name, dx, wd, g, u):
    T = dx.shape[0]

    def body(dx_ref, wd_ref, g_ref, u_ref, dg_ref, du_ref):
        for r in range(0, ROW_T, HALF_T):
            rows = slice(r, r + HALF_T)
            da = 0.5 * _dot(dx_ref[rows, :].astype(BF16), wd_ref[...], NT)
            dg_ref[rows, :] = (da * g_ref[rows, :].astype(F32)).astype(BF16)
            du_ref[rows, :] = (da * u_ref[rows, :].astype(F32)).astype(BF16)

    aspec = pl.BlockSpec((None, ROW_T, FF_SH), lambda j, i: (j, i, 0))
    osh = jax.ShapeDtypeStruct((N_SHARD, T, FF_SH), BF16)
    return pl.pallas_call(
        body, out_shape=(osh, osh), grid=(N_SHARD, T // ROW_T),
        in_specs=[pl.BlockSpec((ROW_T, D_MODEL), lambda j, i: (i, 0)),
                  pl.BlockSpec((None, FF_SH, D_MODEL), lambda j, i: (j, 0, 0)), aspec, aspec],
        out_specs=(aspec, aspec), name=name, compiler_params=_cp("parallel", "parallel"))(dx, wd, g, u)


def _row3():
    return pl.BlockSpec((ROW_T, D_MODEL), lambda i, n, k: (i, 0))


def _vec3():
    return pl.BlockSpec((1, D_MODEL), lambda i, n, k: (0, 0))


def _with_norm(T, next_nw):
    if next_nw is None:
        return dict(out_shape=jax.ShapeDtypeStruct((T, D_MODEL), F32), out_spec=_row3())
    return dict(out_shape=(jax.ShapeDtypeStruct((T, D_MODEL), F32), jax.ShapeDtypeStruct((T, D_MODEL), BF16)),
                out_spec=(_row3(), _row3()), post="norm", post_in=[(next_nw, _vec3())])


def _ffn_fwd(tag, x, h, wg, wu, wd, next_nw):
    T = x.shape[0]
    g, u, a = _ffn_gate_up(tag + "_gu", h, wg, wu)
    if callable(wd):
        wd = wd(a)
    nt = T // ROW_T
    o = _with_norm(T, next_nw)
    xo = _mm(tag + "_down",
             [(a, pl.BlockSpec((None, ROW_T, FF_SH), lambda i, n, k, j=j: (j, i, 0)),
               wd, pl.BlockSpec((None, FF_SH, D_MODEL), lambda i, n, k, j=j: (j, 0, 0))) for j in range(N_SHARD)],
             o.pop("out_shape"), o.pop("out_spec"), (nt, 1, 1), NN, (ROW_T, D_MODEL),
             res=(x, _row3()), scale=0.5, **o)
    return xo, (x, h, g, u, a), wd


def _ffn_bwd(tag, dxo, saved, nw, wg, wu, wd, emit):
    x, h, g, u, a = saved
    T = x.shape[0]
    nt = T // ROW_T
    tkw = min(TK_W, T)
    nw_t = T // tkw
    dg, du = _ffn_dact(tag + "_dact", dxo, wd, g, u)
    actw = lambda f: pl.BlockSpec((None, tkw, FF_SH), f)
    gd = _mm(tag + "_dwd",
             [(a, actw(lambda m, n, k: (m, k, 0)), dxo, pl.BlockSpec((tkw, D_MODEL), lambda m, n, k: (k, 0)))],
             jax.ShapeDtypeStruct((N_SHARD, FF_SH, D_MODEL), BF16),
             pl.BlockSpec((None, FF_SH, D_MODEL), lambda m, n, k: (m, 0, 0)),
             (N_SHARD, 1, nw_t), TN, (FF_SH, D_MODEL), scale=0.5)
    hspec = pl.BlockSpec((tkw, D_MODEL), lambda j, n, k: (k, 0))
    gsh = jax.ShapeDtypeStruct((N_SHARD, FF_SH, D_MODEL), BF16)
    gspec = pl.BlockSpec((None, FF_SH, D_MODEL), lambda j, n, k: (j, 0, 0))
    gg = _mm(tag + "_dwg", [(dg, actw(lambda j, n, k: (j, k, 0)), h, hspec)], gsh, gspec,
             (N_SHARD, 1, nw_t), TN, (FF_SH, D_MODEL))
    gu = _mm(tag + "_dwu", [(du, actw(lambda j, n, k: (j, k, 0)), h, hspec)], gsh, gspec,
             (N_SHARD, 1, nw_t), TN, (FF_SH, D_MODEL))
    dg = emit(gg, gu, gd, dg)
    act = lambda j: pl.BlockSpec((None, ROW_T, FF_SH), lambda i, n, k: (j, i, 0))
    wsp = lambda j: pl.BlockSpec((None, FF_SH, D_MODEL), lambda i, n, k: (j, 0, 0))
    return _mm(tag + "_dh",
               [(dd, act(j), w, wsp(j)) for j in range(N_SHARD) for dd, w in ((dg, wg), (du, wu))],
               (jax.ShapeDtypeStruct((T, D_MODEL), F32), jax.ShapeDtypeStruct((1, D_MODEL), F32)), (_row3(), _vec3()),
               (nt, 1, 1), NN, (ROW_T, D_MODEL), post="rmsb", post_in=[(x, _row3()), (nw, _vec3()), (dxo, _row3())])


def _seq_rows(ref, start, size, S):
    lo, hi = max(start, 0), min(start + size, S)
    parts = [ref[pl.ds(lo, hi - lo), :]]
    if lo > start:
        parts.insert(0, jnp.zeros((lo - start, ref.shape[1]), F32))
    if start + size > hi:
        parts.append(jnp.zeros((start + size - hi, ref.shape[1]), F32))
    return parts[0] if len(parts) == 1 else jnp.concatenate(parts, axis=0)


XBC_CB = COL_XBC // CONV_CT


def _conv_fwd(name, proj, w, b, B):
    T = proj.shape[0]
    S = T // B
    C = CONV_DIM

    def body(x_ref, w_ref, b_ref, o_ref):
        wv = w_ref[...]
        for c in range(S // CONV_R):
            r0 = c * CONV_R
            ch = _seq_rows(x_ref, r0 - PAD_R, CONV_R + PAD_R, S)
            pre = ch[PAD_R:] * wv[3:4] + b_ref[...]
            for s in range(1, CONV_K):
                pre = pre + pltpu.roll(ch, s, axis=0)[PAD_R:] * wv[3 - s:4 - s]
            o_ref[pl.ds(r0, CONV_R), :] = pre * _sigmoid(pre)

    return pl.pallas_call(
        body, out_shape=jax.ShapeDtypeStruct((T, C), F32), grid=(B, C // CONV_CT),
        in_specs=[pl.BlockSpec((S, CONV_CT), lambda bi, ci: (bi, XBC_CB + ci)),
                  pl.BlockSpec((CONV_K, CONV_CT), lambda bi, ci: (0, ci)),
                  pl.BlockSpec((1, CONV_CT), lambda bi, ci: (0, ci))],
        out_specs=pl.BlockSpec((S, CONV_CT), lambda bi, ci: (bi, ci)), name=name,
        compiler_params=_cp("parallel", "parallel"))(proj, w, b)


def _conv_bwd(name, proj, dxs, dB, dC, w, b, dproj, B):
    T = proj.shape[0]
    S = T // B
    C = CONV_DIM
    RW = CONV_R + PAD_R
    nx, nb = dxs.shape[1] // CONV_CT, dB.shape[1] // CONV_CT

    def body(x_ref, dx_in, db_in, dc_in, w_ref, b_ref, buf_ref, dx_ref, dw_ref, db_ref):
        @pl.when(pl.program_id(1) == 0)
        def _():
            dw_ref[...] = jnp.zeros_like(dw_ref)
            db_ref[...] = jnp.zeros_like(db_ref)

        ci = pl.program_id(0)
        wv = w_ref[...]
        dw = [jnp.zeros((1, CONV_CT), F32) for _ in range(CONV_K)]
        db = jnp.zeros((1, CONV_CT), F32)
        for c in range(S // CONV_R):
            r0 = c * CONV_R
            ch = _seq_rows(x_ref, r0 - PAD_R, RW + PAD_R, S)
            xs = [ch[PAD_R:]] + [pltpu.roll(ch, s, axis=0)[PAD_R:] for s in range(1, CONV_K)]
            pre = b_ref[...] + xs[0] * wv[3:4]
            for s in range(1, CONV_K):
                pre = pre + xs[s] * wv[3 - s:4 - s]
            sg = _sigmoid(pre)
            dout = jnp.where(ci < nx, _seq_rows(dx_in, r0, RW, S),
                             jnp.where(ci < nx + nb, _seq_rows(db_in, r0, RW, S), _seq_rows(dc_in, r0, RW, S)))
            dpre = dout * (sg * (1.0 + pre * (1.0 - sg)))
            dx = dpre[:CONV_R] * wv[3:4]
            for s in range(1, CONV_K):
                dx = dx + pltpu.roll(dpre, RW - s, axis=0)[:CONV_R] * wv[3 - s:4 - s]
            dx_ref[pl.ds(r0, CONV_R), :] = dx.astype(BF16)
            dcur = dpre[:CONV_R]
            db = db + jnp.sum(dcur, axis=0, keepdims=True)
            for s in range(CONV_K):
                dw[3 - s] = dw[3 - s] + jnp.sum(dcur * xs[s][:CONV_R], axis=0, keepdims=True)
        db_ref[...] += db
        for k in range(CONV_K):
            dw_ref[k:k + 1, :] += dw[k]

    seq = lambda f: pl.BlockSpec((S, CONV_CT), f)
    return pl.pallas_call(
        body,
        out_shape=(jax.ShapeDtypeStruct(dproj.shape, dproj.dtype), jax.ShapeDtypeStruct((CONV_K, C), F32),
                   jax.ShapeDtypeStruct((1, C), F32)),
        grid=(C // CONV_CT, B),
        in_specs=[seq(lambda ci, bi: (bi, XBC_CB + ci)),
                  seq(lambda ci, bi: (bi, jnp.minimum(ci, nx - 1))),
                  seq(lambda ci, bi: (bi, jnp.clip(ci - nx, 0, nb - 1))),
                  seq(lambda ci, bi: (bi, jnp.clip(ci - nx - nb, 0, nb - 1))),
                  pl.BlockSpec((CONV_K, CONV_CT), lambda ci, bi: (0, ci)),
                  pl.BlockSpec((1, CONV_CT), lambda ci, bi: (0, ci)), ANY],
        out_specs=(seq(lambda ci, bi: (bi, XBC_CB + ci)),
                   pl.BlockSpec((CONV_K, CONV_CT), lambda ci, bi: (0, ci)),
                   pl.BlockSpec((1, CONV_CT), lambda ci, bi: (0, ci))),
        input_output_aliases={6: 0},
        name=name, compiler_params=_cp("parallel", "arbitrary"))(proj, dxs, dB, dC, w, b, dproj)


def _tri_sum(tri, x, dims, tri_first, terms=3):
    out, rest = None, x
    for t in range(terms):
        part = rest.astype(BF16)
        if t + 1 < terms:
            rest = rest - part.astype(F32)
        d = _dot(tri, part, dims) if tri_first else _dot(part, tri, dims)
        out = d if out is None else out + d
    return out


def _total(x):
    return jnp.sum(jnp.sum(x, axis=0, keepdims=True), axis=-1, keepdims=True)


def _ssd_common(dtc_ref, dtr_ref, pcol_ref, prow_ref, b_ref, c_ref):
    L = SSD_L
    bias_c, alog_c = pcol_ref[0:1, :], pcol_ref[1:2, :]
    a_c = -jnp.exp(alog_c)
    dt_c = _softplus(dtc_ref[...] + bias_c)
    row = lax.broadcasted_iota(jnp.int32, (L, L), 0)
    col = lax.broadcasted_iota(jnp.int32, (L, L), 1)
    causal = row >= col
    tri = causal.astype(BF16)
    cum_c = _tri_sum(tri, dt_c * a_c, NN, True)
    a_r = -jnp.exp(prow_ref[:, 1:2])
    dt_r = _softplus(dtr_ref[...] + prow_ref[:, 0:1])
    cum_r = _tri_sum(tri, dt_r * a_r, NT, False)
    bb = b_ref[...].astype(BF16)
    cb = c_ref[...].astype(BF16)
    G = _dot(cb, bb, NT)
    return a_c, dt_c, causal, tri, cum_c, cum_r, bb, cb, G


def _ssd_fwd(name, xc, proj, dtc, dtr, pcol, prow, nw, B):
    T = xc.shape[0]
    S = T // B
    nb = S // SSD_L
    L = SSD_L

    def body(xs_ref, b_ref, c_ref, z_ref, dtc_ref, dtr_ref, pcol_ref, prow_ref, nw_ref, y_ref, yn_ref, hs_ref, H, yo_s):
        @pl.when(pl.program_id(2) == 0)
        def _():
            H[...] = jnp.zeros_like(H)

        a_c, dt_c, causal, tri, cum_c, cum_r, bb, cb, G = _ssd_common(dtc_ref, dtr_ref, pcol_ref, prow_ref, b_ref, c_ref)
        dsk = pcol_ref[2:3, :]
        clast = cum_c[L - 1:L, :]
        bf = b_ref[...]
        for h in range(4):
            hs_ref[h] = H[h]
            yo_s[h] = _dot(cb, H[h].astype(BF16), NN)
        for h in range(4):
            sl = slice(HEAD_DIM * h, HEAD_DIM * (h + 1))
            cc = cum_c[:, h:h + 1]
            lm = jnp.exp(jnp.where(causal, cc - cum_r[h:h + 1, :], NEG))
            M = (G * lm).astype(BF16)
            xh = xs_ref[:, sl]
            Xb = (xh * dt_c[:, h:h + 1]).astype(BF16)
            Hh = H[h]
            y = _dot(M, Xb, NN) + jnp.exp(cc) * yo_s[h]
            y_ref[:, sl] = y + dsk[:, h:h + 1] * xh
            cl = clast[:, h:h + 1]
            Bw = (bf * jnp.exp(cl - cc)).astype(BF16)
            H[h] = jnp.exp(cl) * Hh + _dot(Bw, Xb, TN)
        zv = z_ref[...]
        y2 = y_ref[...] * (zv * _sigmoid(zv))
        r = lax.rsqrt(jnp.mean(y2 * y2, axis=-1, keepdims=True) + EPS)
        yn_ref[...] = (y2 * r * nw_ref[...]).astype(BF16)

    rowi = lambda b, g, i: b * nb + i
    grp = pl.BlockSpec((L, GROUP_W), lambda b, g, i: (rowi(b, g, i), g))
    return pl.pallas_call(
        body,
        out_shape=(jax.ShapeDtypeStruct((T, 1024), F32), jax.ShapeDtypeStruct((T, 1024), BF16),
                   jax.ShapeDtypeStruct((B, SSD_GROUPS, nb, 4, SSD_STATE, HEAD_DIM), F32)),
        grid=(B, SSD_GROUPS, nb),
        in_specs=[grp,
                  pl.BlockSpec((L, SSD_STATE), lambda b, g, i: (rowi(b, g, i), 8 + g)),
                  pl.BlockSpec((L, SSD_STATE), lambda b, g, i: (rowi(b, g, i), 12 + g)),
                  grp,
                  pl.BlockSpec((None, L, 4), lambda b, g, i: (g, rowi(b, g, i), 0)),
                  pl.BlockSpec((None, 4, L), lambda b, g, i: (g, 0, rowi(b, g, i))),
                  pl.BlockSpec((None, 3, 4), lambda b, g, i: (g, 0, 0)),
                  pl.BlockSpec((None, 4, 3), lambda b, g, i: (g, 0, 0)),
                  pl.BlockSpec((1, GROUP_W), lambda b, g, i: (0, g))],
        out_specs=(grp, grp,
                   pl.BlockSpec((None, None, None, 4, SSD_STATE, HEAD_DIM), lambda b, g, i: (b, g, i, 0, 0, 0))),
        scratch_shapes=[pltpu.VMEM((4, SSD_STATE, HEAD_DIM), F32), pltpu.VMEM((4, L, HEAD_DIM), F32)], name=name,
        compiler_params=_cp("parallel", "parallel", "arbitrary"))(xc, xc, xc, proj, dtc, dtr, pcol, prow, nw)


def _ssd_bwd(name, dyn, Y, xc, proj, dtc, dtr, pcol, prow, nw, hs, dproj, B):
    T = xc.shape[0]
    S = T // B
    nb = S // SSD_L
    L = SSD_L

    def body(dyn_ref, y_ref, xs_ref, b_ref, c_ref, z_ref, dtc_ref, dtr_ref, pcol_ref, prow_ref, nw_ref, hs_ref, buf_ref,
             dxs_ref, db_ref, dc_ref, dz_ref, ddt_ref, dpar_ref, dnw_ref, dH, dm_s, dxo_s, ea_s, ex_s):
        @pl.when(pl.program_id(2) == 0)
        def _():
            dH[...] = jnp.zeros_like(dH)
            dpar_ref[...] = jnp.zeros_like(dpar_ref)
            dnw_ref[...] = jnp.zeros_like(dnw_ref)

        a_c, dt_c, causal, tri, cum_c, cum_r, bb, cb, G = _ssd_common(dtc_ref, dtr_ref, pcol_ref, prow_ref, b_ref, c_ref)
        dsk = pcol_ref[2:3, :]
        clast = cum_c[L - 1:L, :]
        bf = b_ref[...]
        cf = c_ref[...]
        Yv = y_ref[...]
        zv = z_ref[...]
        sz = _sigmoid(zv)
        silu = zv * sz
        y2 = Yv * silu
        r = lax.rsqrt(jnp.mean(y2 * y2, axis=-1, keepdims=True) + EPS)
        yhat = y2 * r
        dyv = dyn_ref[...]
        dnw_ref[...] += jnp.sum(dyv * yhat, axis=0, keepdims=True)
        dyhat = dyv * nw_ref[...]
        dy2 = r * (dyhat - yhat * jnp.mean(dyhat * yhat, axis=-1, keepdims=True))
        dY = dy2 * silu
        dz_ref[...] = (dy2 * Yv * (sz * (1.0 + zv * (1.0 - sz)))).astype(BF16)

        lane4 = lax.broadcasted_iota(jnp.int32, (1, 4), 1)
        dG = jnp.zeros((L, L), F32)
        dBs = jnp.zeros((L, SSD_STATE), F32)
        dCs = jnp.zeros((L, SSD_STATE), F32)
        ddsk = jnp.zeros((1, 4), F32)
        dcl = jnp.zeros((1, 4), F32)
        for h in range(4):
            sl = slice(HEAD_DIM * h, HEAD_DIM * (h + 1))
            xb = (xs_ref[:, sl] * dt_c[:, h:h + 1]).astype(BF16)
            dm_s[h] = _dot(dY[:, sl].astype(BF16), xb, NT)
            dxo_s[h] = _dot(bb, dH[h].astype(BF16), NN)
        for h in range(4):
            sl = slice(HEAD_DIM * h, HEAD_DIM * (h + 1))
            onehot = (lane4 == h).astype(F32)
            cc = cum_c[:, h:h + 1]
            cl = clast[:, h:h + 1]
            lm = jnp.exp(jnp.where(causal, cc - cum_r[h:h + 1, :], NEG))
            M = (G * lm).astype(BF16)
            xh = xs_ref[:, sl]
            dth = dt_c[:, h:h + 1]
            X = xh * dth
            Xb = X.astype(BF16)
            dYh = dY[:, sl]
            dYb = dYh.astype(BF16)
            Hb = hs_ref[h].astype(BF16)
            dHh = dH[h]
            dHb = dHh.astype(BF16)
            alpha = jnp.exp(cc)
            beta = jnp.exp(cl - cc)
            dXoff = beta * dxo_s[h]
            dX = _dot(M, dYb, TN) + dXoff
            dG = dG + dm_s[h] * lm
            dCs = dCs + _dot((alpha * dYh).astype(BF16), Hb, NT)
            dBs = dBs + _dot((beta * X).astype(BF16), dHb, NT)
            ypre = Yv[:, sl] - dsk[:, h:h + 1] * xh
            ea_s[:, sl] = dYb.astype(F32) * ypre - Xb.astype(F32) * dX
            ex_s[:, sl] = dX * xh
            dcl_h = (_total(dHh * (jnp.exp(cl) * hs_ref[h])) + _total(Xb.astype(F32) * dXoff))
            dcl = dcl + dcl_h * onehot
            ddsk = ddsk + _total(dYh * xh) * onehot
            dxs_ref[:, sl] = dsk[:, h:h + 1] * dYh + dX * dth
            dH[h] = jnp.exp(cl) * dHh + _dot((alpha * cf).astype(BF16), dYb, TN)
        dGb = dG.astype(BF16)
        dc_ref[...] = _dot(dGb, bb, NN) + dCs
        db_ref[...] = _dot(dGb, cb, TN) + dBs
        feat = lax.broadcasted_iota(jnp.int32, (GROUP_W, 4), 0)
        head = lax.broadcasted_iota(jnp.int32, (GROUP_W, 4), 1) * HEAD_DIM
        sel = ((feat >= head) & (feat < head + HEAD_DIM)).astype(BF16)
        dA = _tri_sum(sel, ea_s[...], NN, False)
        ddtx = _tri_sum(sel, ex_s[...], NN, False)
        last = lax.broadcasted_iota(jnp.int32, (L, 1), 0) == L - 1
        dA = dA + jnp.where(last, dcl, 0.0)
        dadt = _tri_sum(tri, dA, TN, True)
        ddt = dadt * a_c + ddtx
        d_a = jnp.sum(dadt * dt_c, axis=0, keepdims=True)
        ddraw = ddt * _sigmoid(dtc_ref[...] + pcol_ref[0:1, :])
        ddt_ref[...] = ddraw
        dpar_ref[0:1, :] += jnp.sum(ddraw, axis=0, keepdims=True)
        dpar_ref[1:2, :] += d_a * a_c
        dpar_ref[2:3, :] += ddsk

    rowi = lambda b, g, i: b * nb + (nb - 1 - i)
    grp = pl.BlockSpec((L, GROUP_W), lambda b, g, i: (rowi(b, g, i), g))
    st = pl.BlockSpec((L, SSD_STATE), lambda b, g, i: (rowi(b, g, i), g))
    f = jax.ShapeDtypeStruct
    return pl.pallas_call(
        body,
        out_shape=(f((T, 1024), F32), f((T, 512), F32), f((T, 512), F32), f(dproj.shape, dproj.dtype),
                   f((SSD_GROUPS, T, 4), F32), f((B, SSD_GROUPS, 3, 4), F32), f((B, 1, 1024), F32)),
        grid=(B, SSD_GROUPS, nb),
        in_specs=[grp, grp, grp,
                  pl.BlockSpec((L, SSD_STATE), lambda b, g, i: (rowi(b, g, i), 8 + g)),
                  pl.BlockSpec((L, SSD_STATE), lambda b, g, i: (rowi(b, g, i), 12 + g)),
                  grp,
                  pl.BlockSpec((None, L, 4), lambda b, g, i: (g, rowi(b, g, i), 0)),
                  pl.BlockSpec((None, 4, L), lambda b, g, i: (g, 0, rowi(b, g, i))),
                  pl.BlockSpec((None, 3, 4), lambda b, g, i: (g, 0, 0)),
                  pl.BlockSpec((None, 4, 3), lambda b, g, i: (g, 0, 0)),
                  pl.BlockSpec((1, GROUP_W), lambda b, g, i: (0, g)),
                  pl.BlockSpec((None, None, None, 4, SSD_STATE, HEAD_DIM), lambda b, g, i: (b, g, nb - 1 - i, 0, 0, 0)),
                  ANY],
        out_specs=(grp, st, st, grp,
                   pl.BlockSpec((None, L, 4), lambda b, g, i: (g, rowi(b, g, i), 0)),
                   pl.BlockSpec((None, None, 3, 4), lambda b, g, i: (b, g, 0, 0)),
                   pl.BlockSpec((None, 1, GROUP_W), lambda b, g, i: (b, 0, g))),
        input_output_aliases={12: 3},
        scratch_shapes=[pltpu.VMEM((4, SSD_STATE, HEAD_DIM), F32), pltpu.VMEM((4, L, L), F32),
                        pltpu.VMEM((4, L, HEAD_DIM), F32), pltpu.VMEM((L, GROUP_W), F32),
                        pltpu.VMEM((L, GROUP_W), F32)], name=name,
        compiler_params=_cp("parallel", "parallel", "arbitrary"))(
            dyn, Y, xc, xc, xc, proj, dtc, dtr, pcol, prow, nw, hs, dproj)


def _head_sel():
    sel = (np.arange(1024)[:, None] // HEAD_DIM == np.arange(ATT_HEADS)[None, :]).astype(np.float32)
    return jnp.asarray(sel, BF16), jnp.asarray(sel.T, BF16)


def _head_rms(xv, sel, selT):
    ms = _tri_sum(sel, xv * xv, NN, False, 1) * (1.0 / HEAD_DIM)
    return _tri_sum(selT, lax.rsqrt(ms + EPS), NN, False, 2)


def _headnorm_fwd(name, proj, col_block, w):
    T = proj.shape[0]
    sel, selT = _head_sel()

    def body(x_ref, w_ref, sel_ref, selT_ref, o_ref):
        xv = x_ref[...]
        o_ref[...] = (xv * _head_rms(xv, sel_ref[...], selT_ref[...]) * w_ref[...]).astype(BF16)

    full = lambda shp: pl.BlockSpec(shp, lambda i: (0, 0))
    return pl.pallas_call(
        body, out_shape=jax.ShapeDtypeStruct((T, 1024), BF16), grid=(T // ROW_T,),
        in_specs=[pl.BlockSpec((ROW_T, 1024), lambda i: (i, col_block)), full((1, 1024)), full((1024, ATT_HEADS)),
                  full((ATT_HEADS, 1024))],
        out_specs=pl.BlockSpec((ROW_T, 1024), lambda i: (i, 0)), name=name, compiler_params=_cp("parallel"))(
            proj, jnp.tile(w, (1, ATT_HEADS)), sel, selT)


def _headnorm_bwd(name, dn, proj, col_block, w, dproj):
    T = proj.shape[0]
    sel, selT = _head_sel()

    def body(dn_ref, x_ref, w_ref, sel_ref, selT_ref, buf_ref, dx_ref, dw_ref):
        @pl.when(pl.program_id(0) == 0)
        def _():
            dw_ref[...] = jnp.zeros_like(dw_ref)

        xv = x_ref[...]
        sl, slT = sel_ref[...], selT_ref[...]
        rb = _head_rms(xv, sl, slT)
        xhat = xv * rb
        dnv = dn_ref[...]
        dxhat = dnv * w_ref[...]
        mean = _tri_sum(slT, _tri_sum(sl, dxhat * xhat, NN, False, 2) * (1.0 / HEAD_DIM), NN, False, 2)
        dx_ref[...] = (rb * (dxhat - xhat * mean)).astype(BF16)
        dw_ref[...] += jnp.sum(dnv * xhat, axis=0, keepdims=True)

    here = pl.BlockSpec((ROW_T, 1024), lambda i: (i, col_block))
    full = lambda shp: pl.BlockSpec(shp, lambda i: (0, 0))
    dx, dw = pl.pallas_call(
        body, out_shape=(jax.ShapeDtypeStruct(dproj.shape, dproj.dtype), jax.ShapeDtypeStruct((1, 1024), F32)),
        grid=(T // ROW_T,),
        in_specs=[pl.BlockSpec((ROW_T, 1024), lambda i: (i, 0)), here, full((1, 1024)), full((1024, ATT_HEADS)),
                  full((ATT_HEADS, 1024)), ANY],
        out_specs=(here, full((1, 1024))), input_output_aliases={5: 0},
        name=name, compiler_params=_cp("arbitrary"))(dn, proj, jnp.tile(w, (1, ATT_HEADS)), sel, selT, dproj)
    return dx, jnp.sum(dw.reshape(ATT_HEADS, HEAD_DIM), axis=0, keepdims=True)


def _att_bias(nq):
    j = np.arange(ATT_B)[:, None]
    i = np.arange(ATT_B)[None, :]
    out = np.empty((nq, ATT_B, ATT_B), np.float32)
    for dblk in range(nq):
        dl = ATT_B * dblk + i - j
        cnt = ((dl >= 0) & (dl <= 128)).astype(np.float32)
        cnt += ((dl >= 0) & (dl % 4 == 0) & (dl <= 512))
        cnt += ((dl >= 0) & (dl % 16 == 0) & (dl <= 2048))
        out[dblk] = np.where(cnt > 0, np.log(np.maximum(cnt, 1.0)), NEG)
    return jnp.asarray(out)


def _row_pair(nq):
    def f(r, c):
        first = c <= r
        return jnp.where(first, r, nq - 1 - r), jnp.where(first, c, c - (r + 1))
    return f


def _col_pair(nq):
    def f(r, c):
        first = c < nq - r
        kj = jnp.where(first, r, nq - 1 - r)
        return jnp.where(first, r + c, nq - 1 - r + (c - (nq - r))), kj
    return f


ATT_SCALE = 1.0 / math.sqrt(HEAD_DIM)
ATT_HS = 4
ATT_W = ATT_HS * HEAD_DIM


def _att_maps(nq, qk):
    return dict(
        q_tok=lambda b, g, r, c: (b * nq + qk(r, c)[0], g),
        k_tok=lambda b, g, r, c: (b * nq + qk(r, c)[1], g),
        v_tok=lambda b, g, r, c: (b * nq + qk(r, c)[1], COL_V // ATT_W + g),
        q_feat=lambda b, g, r, c: (g, b * nq + qk(r, c)[0]),
        k_feat=lambda b, g, r, c: (g, b * nq + qk(r, c)[1]),
        bias=lambda b, g, r, c: (qk(r, c)[0] - qk(r, c)[1], 0, 0),
        lse=lambda b, g, r, c: (g, 0, b * nq + qk(r, c)[0]),
        do_tok=lambda b, g, r, c: (b * nq + qk(r, c)[0], ATT_HS + g))


def _att_fwd(name, kn, qT, vT, bias, B):
    T = kn.shape[0]
    nq = (T // B) // ATT_B
    qk = _row_pair(nq)
    mp = _att_maps(nq, qk)

    def body(k_ref, qT_ref, vT_ref, bias_ref, oT_ref, lse_ref, m_s, l_s, acc_s, s_s):
        qi, kj = qk(pl.program_id(2), pl.program_id(3))

        @pl.when(kj == 0)
        def _():
            m_s[...] = jnp.full_like(m_s, NEG)
            l_s[...] = jnp.zeros_like(l_s)
            acc_s[...] = jnp.zeros_like(acc_s)

        bv = bias_ref[...]
        for h in range(ATT_HS):
            rs = slice(HEAD_DIM * h, HEAD_DIM * (h + 1))
            s_s[h] = _dot(k_ref[:, rs], qT_ref[rs, :], NN)
        for h in range(ATT_HS):
            rs = slice(HEAD_DIM * h, HEAD_DIM * (h + 1))
            s = s_s[h] + bv
            m_prev = m_s[h:h + 1, :]
            m_new = jnp.maximum(m_prev, jnp.max(s, axis=0, keepdims=True))
            alpha = jnp.exp(m_prev - m_new)
            p = jnp.exp(s - m_new)
            l_s[h:h + 1, :] = alpha * l_s[h:h + 1, :] + jnp.sum(p, axis=0, keepdims=True)
            acc_s[rs, :] = alpha * acc_s[rs, :] + _dot(vT_ref[rs, :], p.astype(BF16), NN)
            m_s[h:h + 1, :] = m_new

        @pl.when(kj == qi)
        def _():
            for h in range(ATT_HS):
                rs = slice(HEAD_DIM * h, HEAD_DIM * (h + 1))
                oT_ref[rs, :] = (acc_s[rs, :] / l_s[h:h + 1, :]).astype(BF16)
            lse_ref[...] = m_s[...] + jnp.log(l_s[...])

    tok = (ATT_B, ATT_W)
    feat = (ATT_W, ATT_B)
    return pl.pallas_call(
        body,
        out_shape=(jax.ShapeDtypeStruct((1024, T), BF16), jax.ShapeDtypeStruct((ATT_HEADS // ATT_HS, ATT_HS, T), F32)),
        grid=(B, ATT_HEADS // ATT_HS, nq // 2, nq + 1),
        in_specs=[pl.BlockSpec(tok, mp["k_tok"]), pl.BlockSpec(feat, mp["q_feat"]), pl.BlockSpec(feat, mp["k_feat"]),
                  pl.BlockSpec((None, ATT_B, ATT_B), mp["bias"])],
        out_specs=(pl.BlockSpec(feat, mp["q_feat"]), pl.BlockSpec((None, ATT_HS, ATT_B), mp["lse"])),
        scratch_shapes=[pltpu.VMEM((ATT_HS, ATT_B), F32), pltpu.VMEM((ATT_HS, ATT_B), F32),
                        pltpu.VMEM((ATT_W, ATT_B), F32), pltpu.VMEM((ATT_HS, ATT_B, ATT_B), F32)],
        name=name, compiler_params=_cp("parallel", "parallel", "arbitrary", "arbitrary"))(kn, qT, vT, bias)


def _att_scores(k_ref, qT_ref, v_ref, doT_ref, s_s, dp_s):
    for h in range(ATT_HS):
        rs = slice(HEAD_DIM * h, HEAD_DIM * (h + 1))
        s_s[h] = _dot(k_ref[:, rs], qT_ref[rs, :], NN)
        dp_s[h] = _dot(v_ref[:, rs].astype(BF16), doT_ref[rs, :].astype(BF16), NN)


def _att_p_ds(s_s, dp_s, doT_ref, oT_ref, lse_ref, bv, h):
    rs = slice(HEAD_DIM * h, HEAD_DIM * (h + 1))
    delta = jnp.sum(doT_ref[rs, :] * oT_ref[rs, :].astype(F32), axis=0, keepdims=True)
    p = jnp.exp(s_s[h] + bv - lse_ref[h:h + 1, :])
    return p, p * (dp_s[h] - delta)


def _att_bwd(name, kn, qT, proj, qn, knT, bias, doT, oT, lse, dyn, dproj, B):
    T = kn.shape[0]
    S = T // B
    nq = S // ATT_B
    qk = _col_pair(nq)
    mp = _att_maps(nq, qk)

    def body(k_ref, qT_ref, v_ref, q_ref, kT_ref, bias_ref, doT_ref, oT_ref, lse_ref, do_ref, buf_ref,
             dqT_ref, dk_ref, dv_ref, dk_s, dv_s, dq_s, s_s, dp_s):
        r, c = pl.program_id(2), pl.program_id(3)
        qi, kj = qk(r, c)

        @pl.when((r == 0) & (c == 0))
        def _():
            dq_s[...] = jnp.zeros_like(dq_s)

        @pl.when(qi == kj)
        def _():
            dk_s[...] = jnp.zeros_like(dk_s)
            dv_s[...] = jnp.zeros_like(dv_s)

        bv = bias_ref[...]
        _att_scores(k_ref, qT_ref, v_ref, doT_ref, s_s, dp_s)
        dq_blk = dq_s.at[qi]
        for h in range(ATT_HS):
            rs = slice(HEAD_DIM * h, HEAD_DIM * (h + 1))
            p, ds = _att_p_ds(s_s, dp_s, doT_ref, oT_ref, lse_ref, bv, h)
            dsb = ds.astype(BF16)
            dv_s[h] += _dot(p.astype(BF16), do_ref[:, rs].astype(BF16), NN)
            dk_s[h] += _dot(dsb, q_ref[:, rs], NN)
            dq_blk[rs, :] += _dot(kT_ref[rs, :], dsb, NN)

        @pl.when(qi == nq - 1)
        def _():
            for h in range(ATT_HS):
                rs = slice(HEAD_DIM * h, HEAD_DIM * (h + 1))
                dk_ref[:, rs] = dk_s[h] * ATT_SCALE
                dv_ref[:, rs] = dv_s[h].astype(BF16)

        @pl.when((r == nq // 2 - 1) & (c == nq))
        def _():
            for q in range(nq):
                dqT_ref[:, ATT_B * q:ATT_B * (q + 1)] = dq_s[q] * ATT_SCALE

    tok = (ATT_B, ATT_W)
    feat = (ATT_W, ATT_B)
    v_cb = COL_V // ATT_W
    return pl.pallas_call(
        body,
        out_shape=(jax.ShapeDtypeStruct((1024, T), F32), jax.ShapeDtypeStruct((T, 1024), F32),
                   jax.ShapeDtypeStruct(dproj.shape, dproj.dtype)),
        grid=(B, ATT_HEADS // ATT_HS, nq // 2, nq + 1),
        in_specs=[pl.BlockSpec(tok, mp["k_tok"]), pl.BlockSpec(feat, mp["q_feat"]), pl.BlockSpec(tok, mp["v_tok"]),
                  pl.BlockSpec(tok, mp["q_tok"]), pl.BlockSpec(feat, mp["k_feat"]),
                  pl.BlockSpec((None, ATT_B, ATT_B), mp["bias"]),
                  pl.BlockSpec(feat, mp["q_feat"]), pl.BlockSpec(feat, mp["q_feat"]),
                  pl.BlockSpec((None, ATT_HS, ATT_B), mp["lse"]), pl.BlockSpec(tok, mp["do_tok"]), ANY],
        out_specs=(pl.BlockSpec((ATT_W, S), lambda b, g, r, c: (g, b)),
                   pl.BlockSpec(tok, mp["k_tok"]),
                   pl.BlockSpec(tok, lambda b, g, r, c: (b * nq + qk(r, c)[1], v_cb + g))),
        input_output_aliases={10: 2},
        scratch_shapes=[pltpu.VMEM((ATT_HS, ATT_B, HEAD_DIM), F32), pltpu.VMEM((ATT_HS, ATT_B, HEAD_DIM), F32),
                        pltpu.VMEM((nq, ATT_W, ATT_B), F32),
                        pltpu.VMEM((ATT_HS, ATT_B, ATT_B), F32), pltpu.VMEM((ATT_HS, ATT_B, ATT_B), F32)],
        name=name, compiler_params=_cp("parallel", "parallel", "arbitrary", "arbitrary"))(
            kn, qT, proj, qn, knT, bias, doT, oT, lse, dyn, dproj)


def _group_cols(v):
    return v.reshape(SSD_GROUPS, 4)


def _ssd_params(p):
    rows = jnp.stack([_group_cols(p["dt_bias"]), _group_cols(p["a_log"]), _group_cols(p["d_skip"])], axis=1)
    return rows, jnp.swapaxes(rows, 1, 2)


def _dymix(name, dx, wout):
    T = dx.shape[0]

    def body(dx_ref, w_ref, o_ref):
        dxb = dx_ref[...].astype(BF16)
        for n in range(N_SHARD):
            o_ref[:, MIX_SH * n:MIX_SH * (n + 1)] = _dot(dxb, w_ref[n], NT)

    return pl.pallas_call(
        body, out_shape=jax.ShapeDtypeStruct((T, MIX_W), F32), grid=(T // ROW_T,),
        in_specs=[pl.BlockSpec((ROW_T, D_MODEL), lambda i: (i, 0)),
                  pl.BlockSpec((N_SHARD, MIX_SH, D_MODEL), lambda i: (0, 0, 0))],
        out_specs=pl.BlockSpec((ROW_T, MIX_W), lambda i: (i, 0)), name=name, compiler_params=_cp("parallel"))(dx, wout)


def _mixer_fwd(tag, x1, h2, p, weights, bias, B):
    T = x1.shape[0]
    S = T // B
    nt = T // ROW_T
    wi = weights("win", h2)
    win, cw = wi["win"], wi["cw"]
    proj = _mm(tag + "_proj",
               [(h2, pl.BlockSpec((ROW_T, D_MODEL), lambda j, i, k: (i, 0)),
                 win, pl.BlockSpec((D_MODEL, PROJ_TN), lambda j, i, k: (0, j)))],
               jax.ShapeDtypeStruct((T, IN_PAD), F32), pl.BlockSpec((ROW_T, PROJ_TN), lambda j, i, k: (i, j)),
               (IN_PAD // PROJ_TN, nt, 1), NN, (ROW_T, PROJ_TN))
    xc = _conv_fwd(tag + "_conv", proj, cw, p["conv_b"][None], B)
    dtraw = proj[:, COL_DT:COL_DT + SSD_HEADS].reshape(T, SSD_GROUPS, 4)
    dtc = jnp.transpose(dtraw, (1, 0, 2))
    dtr = jnp.transpose(dtraw, (1, 2, 0))
    pcol, prow = _ssd_params(p)
    Y, y_ssd, hs = _ssd_fwd(tag + "_ssd", xc, proj, dtc, dtr, pcol, prow, p["ssd_norm"][None], B)
    qn = _headnorm_fwd(tag + "_qn", proj, COL_Q // 1024, p["q_norm"][None])
    kn = _headnorm_fwd(tag + "_kn", proj, COL_K // 1024, p["k_norm"][None])
    qT = (qn * ATT_SCALE).T
    oT, lse = _att_fwd(tag + "_att", kn, qT, proj[:, COL_V:COL_V + 1024].T.astype(BF16), bias, B)
    ymix = jnp.concatenate([y_ssd, oT.T], axis=1)
    rest = weights("rest", ymix)
    o = _with_norm(T, p["ffn2_norm"][None])
    x2, h3 = _mm(tag + "_out",
                 [(ymix, pl.BlockSpec((ROW_T, MIX_SH), lambda i, n, k, j=j: (i, j)),
                   rest["wout"], pl.BlockSpec((None, MIX_SH, D_MODEL), lambda i, n, k, j=j: (j, 0, 0)))
                  for j in range(N_SHARD)],
                 o.pop("out_shape"), o.pop("out_spec"), (nt, 1, 1), NN, (ROW_T, D_MODEL), res=(x1, _row3()), **o)
    saved = dict(x1=x1, h2=h2, proj=proj, xc=xc, dtc=dtc, dtr=dtr, Y=Y, hs=hs,
                 qn=qn, kn=kn, qT=qT, oT=oT, lse=lse, ymix=ymix, win=win, cw=cw, wout=rest["wout"])
    return x2, h3, saved


def _mixer_bwd(tag, dx2, sv, p, bias, B):
    T = dx2.shape[0]
    S = T // B
    nt = T // ROW_T
    sg = {}
    dymix = _dymix(tag + "_dymix", dx2, sv["wout"])
    tkw = min(TK_W, T)
    gwout = _mm(tag + "_dwout",
                [(sv["ymix"], pl.BlockSpec((tkw, MIX_SH), lambda m, n, k: (k, m)),
                  dx2, pl.BlockSpec((tkw, D_MODEL), lambda m, n, k: (k, 0)))],
                jax.ShapeDtypeStruct((N_SHARD, MIX_SH, D_MODEL), BF16),
                pl.BlockSpec((None, MIX_SH, D_MODEL), lambda m, n, k: (m, 0, 0)),
                (N_SHARD, 1, T // tkw), TN, (MIX_SH, D_MODEL))
    proj = sv["proj"]
    doT = dymix[:, 1024:].T
    dproj = lax.empty((T, IN_PAD), BF16)
    dqT, dkn, dproj = _att_bwd(tag + "_attb", sv["kn"], sv["qT"], proj, sv["qn"], sv["kn"].T, bias, doT, sv["oT"],
                               sv["lse"], dymix, dproj, B)
    dproj, sg["q_norm"] = _headnorm_bwd(tag + "_qnb", dqT.T, proj, COL_Q // 1024, p["q_norm"][None], dproj)
    dproj, sg["k_norm"] = _headnorm_bwd(tag + "_knb", dkn, proj, COL_K // 1024, p["k_norm"][None], dproj)
    pcol, prow = _ssd_params(p)
    dxs, dB, dC, dproj, ddt, dpar, dnw = _ssd_bwd(tag + "_ssdb", dymix, sv["Y"], sv["xc"], proj, sv["dtc"], sv["dtr"],
                                                  pcol, prow, p["ssd_norm"][None], sv["hs"], dproj, B)
    dpar = jnp.sum(dpar, axis=0)
    sg["dt_bias"] = dpar[:, 0, :].reshape(SSD_HEADS)
    sg["a_log"] = dpar[:, 1, :].reshape(SSD_HEADS)
    sg["d_skip"] = dpar[:, 2, :].reshape(SSD_HEADS)
    sg["ssd_norm"] = jnp.sum(dnw, axis=0)
    dproj, sg["conv_w"], sg["conv_b"] = _conv_bwd(tag + "_convb", proj, dxs, dB, dC, sv["cw"], p["conv_b"][None],
                                                  dproj, B)
    ddt16 = jnp.transpose(ddt, (1, 0, 2)).reshape(T, SSD_HEADS)
    dproj = lax.dynamic_update_slice(dproj, jnp.pad(ddt16, ((0, 0), (0, IN_PAD - COL_DT - SSD_HEADS))).astype(BF16),
                                     (0, COL_DT))
    win = sv["win"]
    gwin = _mm(tag + "_dwin",
               [(sv["h2"], pl.BlockSpec((tkw, D_MODEL), lambda n, m, k: (k, 0)),
                 dproj, pl.BlockSpec((tkw, PROJ_TN), lambda n, m, k: (k, n)))],
               jax.ShapeDtypeStruct((D_MODEL, IN_PAD), BF16), pl.BlockSpec((D_MODEL, PROJ_TN), lambda n, m, k: (0, n)),
               (IN_PAD // PROJ_TN, 1, T // tkw), TN, (D_MODEL, PROJ_TN))
    dx1, sg["mix_norm"] = _mm(
        tag + "_dh2",
        [(dproj, pl.BlockSpec((ROW_T, PROJ_TN), lambda i, n, k, j=j: (i, j)),
          win, pl.BlockSpec((D_MODEL, PROJ_TN), lambda i, n, k, j=j: (0, j))) for j in range(IN_PAD // PROJ_TN)],
        (jax.ShapeDtypeStruct((T, D_MODEL), F32), jax.ShapeDtypeStruct((1, D_MODEL), F32)), (_row3(), _vec3()),
        (nt, 1, 1), NT, (ROW_T, D_MODEL), post="rmsb",
        post_in=[(sv["x1"], _row3()), (p["mix_norm"][None], _vec3()), (dx2, _row3())])
    return dx1, sg, gwout, gwin


def _win_pack(w):
    return jnp.concatenate([w[:, :3072], w[:, 3088:], w[:, 3072:3088],
                            jnp.zeros((w.shape[0], IN_PAD - IN_PROJ), w.dtype)], axis=1)


def _win_unpack(g):
    return jnp.concatenate([g[:, :3072], g[:, COL_DT:COL_DT + SSD_HEADS], g[:, 3072:COL_DT]], axis=1)


DT_LO = IN_SH * 2 - COL_Q


def _win_from_shards(sh):
    main = IN_SH - DT_LO
    return jnp.concatenate([sh[0], sh[1][:, :main], sh[2][:, SSD_HEADS - DT_LO:], sh[3], sh[1][:, main:],
                            sh[2][:, :SSD_HEADS - DT_LO], jnp.zeros((sh.shape[1], IN_PAD - IN_PROJ), sh.dtype)], axis=1)


def _win_to_shards(g):
    main = IN_SH - DT_LO
    a, b = IN_SH + main, IN_SH + 2 * main
    return jnp.stack([g[:, :IN_SH],
                      jnp.concatenate([g[:, IN_SH:a], g[:, COL_DT:COL_DT + DT_LO]], axis=1),
                      jnp.concatenate([g[:, COL_DT + DT_LO:COL_DT + SSD_HEADS], g[:, a:b]], axis=1),
                      g[:, b:COL_DT]])


def _local_step(x, target, small, weights, scatter, B):
    T = x.shape[0]
    bias = _att_bias((T // B) // ATT_B)
    saved = []
    xl = x
    hl = _rms_fwd("l0f1_rms", x, small["ffn1_norm"][0][None])
    for l in range(DEPTH):
        tag = "l%d" % l
        p = {k: v[l] for k, v in small.items()}
        w1 = weights(l, "ffn1", hl)
        (x1, h2), ffn1, d1 = _ffn_fwd(tag + "f1", xl, hl, w1["g1"], w1["u1"],
                                      lambda after, l=l: weights(l, "ffn1d", after)["d1"], p["mix_norm"][None])
        x2, h3, sv = _mixer_fwd(tag, x1, h2, p, functools.partial(weights, l), bias, B)
        w2 = weights(l, "rest", x2)
        nxt = small["ffn1_norm"][l + 1][None] if l + 1 < DEPTH else None
        xo, ffn2, _ = _ffn_fwd(tag + "f2", x2, h3, w2["g2"], w2["u2"], w2["d2"], nxt)
        xl, hl = xo if nxt is not None else (xo, None)
        saved.append((ffn1, sv, ffn2, dict(g1=w1["g1"], u1=w1["u1"], d1=d1), w2))
    d, lsum = _loss_grad("loss", xl, target)
    sgrads = [None] * DEPTH
    for l in reversed(range(DEPTH)):
        tag = "l%db" % l
        p = {k: v[l] for k, v in small.items()}
        ffn1, sv, ffn2, w1, w2 = saved[l]
        sg = {}
        d, sg["ffn2_norm"] = _ffn_bwd(tag + "f2", d, ffn2, p["ffn2_norm"][None], w2["g2"], w2["u2"], w2["d2"],
                                      lambda gg, gu, gd, c, l=l: scatter(l, "ffn2", dict(g2=gg, u2=gu, d2=gd), c))
        d, sgm, gwout, gwin = _mixer_bwd(tag, d, sv, p, bias, B)
        sg.update(sgm)
        d = scatter(l, "mixer", dict(wout=gwout, win=gwin), d)
        d, sg["ffn1_norm"] = _ffn_bwd(tag + "f1", d, ffn1, p["ffn1_norm"][None], w1["g1"], w1["u1"], w1["d1"],
                                      lambda gg, gu, gd, c, l=l: scatter(l, "ffn1", dict(g1=gg, u1=gu, d1=gd), c))
        sgrads[l] = sg
    return lsum, d, sgrads


MESH = pl.DeviceIdType.MESH
ANY = pl.BlockSpec(memory_space=pl.ANY)


def _place():
    return lax.axis_index("x"), lax.axis_index("y"), lax.axis_index("c")


def _other_chips(x, y):
    return [(1 - x, y), (x, 1 - y), (1 - x, 1 - y)]


HBM = pl.BlockSpec(memory_space=pltpu.HBM)
SEM = pl.BlockSpec(memory_space=pltpu.SEMAPHORE)
EFFECT = pltpu.SideEffectType.DATAFLOW_SIDE_EFFECTING


def _hbm(a):
    return pltpu.with_memory_space_constraint(a, pltpu.HBM)


def _exchange(gather, layer, src, land, send, recv, n, act):
    x, y, c = _place()
    for k, (px, py) in enumerate(_other_chips(x, y)):
        for a in range(n):
            if gather:
                s_out, d_out, d_in = src[a].at[layer], land[a].at[2 * x + y], land[a].at[2 * px + py]
            else:
                s_out, d_out, d_in = src[a].at[2 * px + py], land[a].at[k], land[a].at[k]
            act(pltpu.make_async_remote_copy(
                src_ref=s_out, dst_ref=d_out if act is _start else d_in, send_sem=send.at[k * n + a],
                recv_sem=recv.at[k * n + a], device_id=(px, py, c), device_id_type=MESH))


def _start(cp):
    cp.start()


def _finish(cp):
    cp.wait_send()
    cp.wait_recv()


def _exchange_start(name, gather, layer, srcs, carry):
    n = len(srcs)
    lands = [lax.empty(((N_SHARD,) + s.shape[1:]) if gather else ((3,) + s.shape[1:]), s.dtype) for s in srcs]

    def body(*refs):
        _exchange(gather, layer, refs[:n], refs[n:2 * n], refs[2 * n + 1], refs[2 * n + 2], n, _start)

    srcs = [_hbm(a) for a in srcs]
    thru = [_hbm(a) for a in lands + [carry]]
    out = pl.pallas_call(
        body, name=name,
        out_shape=(pltpu.SemaphoreType.DMA((3 * n,)), pltpu.SemaphoreType.DMA((3 * n,)),
                   *[pltpu.HBM(a.shape, a.dtype) for a in thru]),
        in_specs=[HBM] * (2 * n + 1), out_specs=(SEM, SEM, *[HBM] * (n + 1)),
        input_output_aliases={n + i: 2 + i for i in range(n + 1)},
        compiler_params=pltpu.CompilerParams(has_side_effects=EFFECT))(*srcs, *thru)
    return dict(gather=gather, layer=layer, send=out[0], recv=out[1], srcs=srcs, lands=list(out[2:2 + n])), out[-1]


def _exchange_wait(name, ex, after):
    n = len(ex["srcs"])

    def body(*refs):
        _exchange(ex["gather"], ex["layer"], refs[:n], refs[n:2 * n], refs[2 * n], refs[2 * n + 1], n, _finish)

    out = pl.pallas_call(
        body, name=name, out_shape=[pltpu.HBM(a.shape, a.dtype) for a in ex["lands"]],
        in_specs=[HBM] * (2 * n) + [SEM, SEM, ANY], out_specs=[HBM] * n,
        input_output_aliases={n + i: i for i in range(n)},
        compiler_params=pltpu.CompilerParams(has_side_effects=EFFECT))(
            *ex["srcs"], *ex["lands"], ex["send"], ex["recv"], after)
    return list(out)


def _swap_sibling(name, parts):
    n = len(parts)

    def body(*refs):
        src, dst = refs[:n], refs[n:2 * n]
        send, recv = refs[2 * n:]
        x, y, c = _place()
        cps = [pltpu.make_async_remote_copy(src_ref=src[a], dst_ref=dst[a], send_sem=send.at[a], recv_sem=recv.at[a],
                                            device_id=(x, y, 1 - c), device_id_type=MESH) for a in range(n)]
        for cp in cps:
            cp.start()
        for cp in cps:
            cp.wait_recv()
        for cp in cps:
            cp.wait_send()

    return pl.pallas_call(
        body, out_shape=[jax.ShapeDtypeStruct(p.shape, p.dtype) for p in parts],
        in_specs=[ANY] * n, out_specs=[ANY] * n,
        scratch_shapes=[pltpu.SemaphoreType.DMA((n,)), pltpu.SemaphoreType.DMA((n,))],
        name=name)(*parts)


def _allreduce_small(name, v, after):
    R = v.shape[0]

    def body(v_ref, after_ref, o_ref, buf, send, recv):
        x, y, c = _place()
        me = 4 * x + 2 * y + c
        buf[me] = v_ref[...]
        cps = []
        for k in range(1, 8):
            fx, fy, fc = (k >> 2) & 1, (k >> 1) & 1, k & 1
            px = 1 - x if fx else x
            py = 1 - y if fy else y
            pc = 1 - c if fc else c
            cp = pltpu.make_async_remote_copy(src_ref=v_ref, dst_ref=buf.at[me], send_sem=send.at[k - 1],
                                              recv_sem=recv.at[k - 1], device_id=(px, py, pc), device_id_type=MESH)
            cp.start()
            cps.append((cp, 4 * px + 2 * py + pc))
        for k, (cp, peer) in enumerate(cps):
            pltpu.make_async_remote_copy(src_ref=v_ref, dst_ref=buf.at[peer], send_sem=send.at[k], recv_sem=recv.at[k],
                                         device_id=(x, y, c), device_id_type=MESH).wait_recv()
        for cp, _ in cps:
            cp.wait_send()
        acc = buf[0]
        for d in range(1, 8):
            acc = acc + buf[d]
        o_ref[...] = acc

    return pl.pallas_call(
        body, out_shape=jax.ShapeDtypeStruct((R, 128), F32),
        in_specs=[pl.BlockSpec(memory_space=pltpu.VMEM), ANY], out_specs=pl.BlockSpec(memory_space=pltpu.VMEM),
        scratch_shapes=[pltpu.VMEM((8, R, 128), F32), pltpu.SemaphoreType.DMA((7,)), pltpu.SemaphoreType.DMA((7,))],
        name=name)(v, after)


TILE_BYTES = 1600 * 1024


def _row_tile(r, c=1024):
    for t in (512, 352, 256, 128, 64, 32, 16, 8):
        if r % t == 0 and (t * c * 4 <= TILE_BYTES or t == 8):
            return t
    raise ValueError(r)


def _sum4(name, me, parts, got):
    _, R, C = parts.shape
    tr = _row_tile(R, C)

    def body(me_ref, o_ref, g_ref, s_ref):
        s = o_ref[...].astype(F32)
        for k in range(3):
            s = s + g_ref[k].astype(F32)
        s_ref[...] = s.astype(BF16)

    return pl.pallas_call(
        body, out_shape=jax.ShapeDtypeStruct((R, C), BF16),
        grid_spec=pltpu.PrefetchScalarGridSpec(
            num_scalar_prefetch=1, grid=(R // tr,),
            in_specs=[pl.BlockSpec((None, tr, C), lambda i, me_ref: (me_ref[0], i, 0)),
                      pl.BlockSpec((3, tr, C), lambda i, me_ref: (0, i, 0))],
            out_specs=pl.BlockSpec((tr, C), lambda i, me_ref: (i, 0))),
        name=name, compiler_params=_cp("parallel"))(me, parts, got)


def _adamw(name, w, gparts, m, v):
    R, C = w.shape
    tr = _row_tile(R, C)
    ng = len(gparts)
    c1 = 1.0 - ADAM_B1 ** ADAM_STEP
    c2 = 1.0 - ADAM_B2 ** ADAM_STEP

    def body(*refs):
        w_ref = refs[0]
        g_refs = refs[1:1 + ng]
        m_ref, v_ref, go_ref, d_ref, mo_ref, vo_ref = refs[1 + ng:]
        g = g_refs[0][...]
        for r in g_refs[1:]:
            g = g + r[...]
        mn = ADAM_B1 * m_ref[...] + (1.0 - ADAM_B1) * g
        vn = ADAM_B2 * v_ref[...] + (1.0 - ADAM_B2) * (g * g)
        go_ref[...] = g
        mo_ref[...] = mn
        vo_ref[...] = vn
        d_ref[...] = -ADAM_LR * ((mn / c1) / (jnp.sqrt(vn / c2) + ADAM_EPS) + ADAM_WD * w_ref[...])

    blk = pl.BlockSpec((tr, C), lambda i: (i, 0))
    osh = jax.ShapeDtypeStruct((R, C), F32)
    return pl.pallas_call(
        body, out_shape=(osh, osh, osh, osh), grid=(R // tr,), in_specs=[blk] * (3 + ng), out_specs=(blk,) * 4,
        name=name, compiler_params=_cp("parallel"))(w, *gparts, m, v)


def _adamw_layers(name, w, sums, m, v):
    _, R, C = w.shape
    tr = _row_tile(R, C)
    nr = R // tr
    c1 = 1.0 - ADAM_B1 ** ADAM_STEP
    c2 = 1.0 - ADAM_B2 ** ADAM_STEP

    def body(w_ref, a0, b0, a1, b1, m_ref, v_ref, go_ref, d_ref, mo_ref, vo_ref):
        f = lambda r: r[...].astype(F32)
        g = jnp.where(pl.program_id(0) == 0, f(a0) + f(b0), f(a1) + f(b1))
        mn = ADAM_B1 * m_ref[...] + (1.0 - ADAM_B1) * g
        vn = ADAM_B2 * v_ref[...] + (1.0 - ADAM_B2) * (g * g)
        go_ref[...] = g
        mo_ref[...] = mn
        vo_ref[...] = vn
        d_ref[...] = -ADAM_LR * ((mn / c1) / (jnp.sqrt(vn / c2) + ADAM_EPS) + ADAM_WD * w_ref[...])

    blk = pl.BlockSpec((None, tr, C), lambda l, i: (l, i, 0))
    lay0 = pl.BlockSpec((tr, C), lambda l, i: (jnp.where(l == 0, i, nr - 1), 0))
    lay1 = pl.BlockSpec((tr, C), lambda l, i: (jnp.where(l == 1, i, 0), 0))
    oblk = pl.BlockSpec((tr, C), lambda l, i: (l * nr + i, 0))
    osh = jax.ShapeDtypeStruct((DEPTH * R, C), F32)
    res = pl.pallas_call(
        body, out_shape=(osh, osh, osh, osh), grid=(DEPTH, nr),
        in_specs=[blk, lay0, lay0, lay1, lay1, blk, blk], out_specs=(oblk,) * 4,
        name=name, compiler_params=_cp("arbitrary", "arbitrary"))(w, *sums[0], *sums[1], m, v)
    return [r.reshape(w.shape) for r in res]


BIG = [("ffn1_w_gate", "g1"), ("ffn1_w_up", "u1"), ("ffn1_w_down", "d1"), ("w_in", "win"), ("w_out", "wout"),
       ("ffn2_w_gate", "g2"), ("ffn2_w_up", "u2"), ("ffn2_w_down", "d2")]
SMALL = ["ffn1_norm", "mix_norm", "conv_b", "dt_bias", "a_log", "d_skip", "ssd_norm", "q_norm", "k_norm", "ffn2_norm"]
WEIGHTS = ["ffn1_norm", "ffn1_w_gate", "ffn1_w_up", "ffn1_w_down", "mix_norm", "w_in", "conv_w", "conv_b", "dt_bias",
           "a_log", "d_skip", "ssd_norm", "q_norm", "k_norm", "w_out", "ffn2_norm", "ffn2_w_gate", "ffn2_w_up",
           "ffn2_w_down"]
CONV_SH = CONV_DIM // N_SHARD
TRANSPOSED = ("g1", "u1", "g2", "u2")
GATHER_GROUPS = [(0, "ffn1", ["g1", "u1"]), (0, "ffn1d", ["d1"]), (0, "win", ["win", "cw"]),
                 (0, "rest", ["wout", "g2", "u2", "d2"]),
                 (1, "all", ["g1", "u1", "d1", "win", "cw", "wout", "g2", "u2", "d2"])]


def _pad128(v):
    v = v.reshape(-1)
    return jnp.pad(v, (0, (-v.shape[0]) % 128))


def _pack(pieces):
    flat, offs, pos = [], [], 0
    for p in pieces:
        q = _pad128(p.astype(F32))
        offs.append(pos)
        pos += q.shape[0] // 128
        flat.append(q)
    total = -(-pos // 8) * 8
    out = jnp.concatenate(flat + [jnp.zeros(((total - pos) * 128,), F32)]).reshape(total, 128)
    return out, offs


def _unpack(packed, offs, shapes):
    out = []
    for off, shp in zip(offs, shapes):
        n = int(np.prod(shp))
        rows = -(-n // 128)
        out.append(packed[off:off + rows].reshape(-1)[:n].reshape(shp))
    return out


def kernel(x, ffn1_norm, ffn1_w_gate, ffn1_w_up, ffn1_w_down, mix_norm, w_in, conv_w, conv_b, dt_bias, a_log, d_skip, ssd_norm, q_norm, k_norm, w_out, ffn2_norm, ffn2_w_gate, ffn2_w_up, ffn2_w_down, loss_target, m_ffn1_norm, m_ffn1_w_gate, m_ffn1_w_up, m_ffn1_w_down, m_mix_norm, m_w_in, m_conv_w, m_conv_b, m_dt_bias, m_a_log, m_d_skip, m_ssd_norm, m_q_norm, m_k_norm, m_w_out, m_ffn2_norm, m_ffn2_w_gate, m_ffn2_w_up, m_ffn2_w_down, v_ffn1_norm, v_ffn1_w_gate, v_ffn1_w_up, v_ffn1_w_down, v_mix_norm, v_w_in, v_conv_w, v_conv_b, v_dt_bias, v_a_log, v_d_skip, v_ssd_norm, v_q_norm, v_k_norm, v_w_out, v_ffn2_norm, v_ffn2_w_gate, v_ffn2_w_up, v_ffn2_w_down):
    A = dict(locals())
    ix, iy, ic = _place()
    me = 2 * ix + iy
    B, S, _ = x.shape
    T = B * S

    view = lambda a, key: jnp.swapaxes(a, 1, 2) if key in TRANSPOSED else a
    own = {key: view(A[name], key).astype(BF16) for name, key in BIG}
    own["cw"] = conv_w
    exs, first_norm = [], ffn1_norm
    for gi, (l, _, keys) in enumerate(GATHER_GROUPS):
        ex, first_norm = _exchange_start("gather_start%d" % gi, True, l, [own[key] for key in keys], first_norm)
        exs.append(ex)
    landed = {}

    def weights(l, group, after):
        gi = [i for i, (gl, gname, _) in enumerate(GATHER_GROUPS) if gl == l and gname in (group, "all")][0]
        if gi not in landed:
            lands = _exchange_wait("gather_wait%d" % gi, exs[gi], after)
            landed[gi] = {}
            for key, land in zip(GATHER_GROUPS[gi][2], lands):
                full = lax.dynamic_update_slice(land, own[key][l][None], (me, 0, 0))
                if key == "win":
                    full = _win_from_shards(full)
                if key == "cw":
                    full = jnp.transpose(full, (1, 0, 2)).reshape(CONV_K, CONV_DIM)
                landed[gi][key] = full
        return landed[gi]

    pending = []

    def scatter(l, group, grads, carry):
        keys = sorted(grads)
        arrs = [grads[key] for key in keys]
        if "win" in grads:
            arrs[keys.index("win")] = _win_to_shards(grads["win"])
        ex, carry = _exchange_start("scatter_start_l%d_%s" % (l, group), False, None, arrs, carry)
        pending.append((l, keys, ex))
        return carry

    small = {name: A[name] for name in SMALL}
    small["ffn1_norm"] = first_norm
    lsum, dx, sgrads = _local_step(x.reshape(T, D_MODEL), loss_target.reshape(T, D_MODEL), small, weights, scatter, B)

    names = SMALL + ["conv_w"]
    shapes = [A[n].shape for n in SMALL] + [(DEPTH, CONV_K, CONV_DIM), ()]
    pieces = [jnp.stack([sgrads[l][n].reshape(shp[1:]) for l in range(DEPTH)]) for n, shp in zip(names, shapes)]
    pieces.append(0.5 / D_MODEL * jnp.sum(lsum))
    packed, offs = _pack(pieces)

    sums, theirs, out = {}, {}, {}
    me1 = jnp.reshape(me, (1,)).astype(jnp.int32)

    def update(tag, after):
        todo = [k for k in sums if k not in theirs]
        theirs.update(zip(todo, _swap_sibling("swap_sibling_" + tag, [sums[k] for k in todo])))
        for name, key in BIG:
            if name not in out and all((key, l) in theirs for l in range(DEPTH)):
                res = _adamw_layers("adamw_" + key, view(A[name], key),
                                    [(sums[key, l], theirs[key, l]) for l in range(DEPTH)],
                                    view(A["m_" + name], key), view(A["v_" + name], key))
                out[name] = [view(r, key) for r in res]
                after = res[0]
        return after

    after = dx
    for idx, (l, keys, ex) in enumerate(pending):
        if idx == len(pending) - 1:
            after = update("a", after)
        lands = _exchange_wait("scatter_wait%d" % idx, ex, after)
        for key, g, got in zip(keys, ex["srcs"], lands):
            sums[key, l] = after = _sum4("sum_%s_l%d" % (key, l), me1, g, got)
    after = update("b", after)

    red = _unpack(_allreduce_small("allreduce_small", packed, after), offs, shapes)
    loss = red[-1]
    sg = dict(zip(names, red[:-1]))

    wp, offs = _pack([A[n] for n in SMALL])
    gp, _ = _pack([sg[n] for n in SMALL])
    mp, _ = _pack([A["m_" + n] for n in SMALL])
    vp, _ = _pack([A["v_" + n] for n in SMALL])
    res = _adamw("adamw_small", wp, [gp], mp, vp)
    shapes = [A[n].shape for n in SMALL]
    res = [_unpack(r, offs, shapes) for r in res]
    for i, n in enumerate(SMALL):
        out[n] = [res[q][i] for q in range(4)]
    gcw = lax.dynamic_slice_in_dim(sg["conv_w"], me * CONV_SH, CONV_SH, axis=2)
    flat = lambda a: a.reshape(DEPTH * CONV_K, CONV_SH)
    res = _adamw("adamw_conv_w", flat(conv_w), [flat(gcw)], flat(m_conv_w), flat(v_conv_w))
    out["conv_w"] = [r.reshape(conv_w.shape) for r in res]

    outs = [loss, dx.reshape(B, S, D_MODEL)]
    for q in range(4):
        outs += [out[n][q] for n in WEIGHTS]
    return tuple(outs)
```

```python
import functools
import math

import numpy as np
import jax
import jax.numpy as jnp
from jax import lax
from jax.experimental import pallas as pl
from jax.experimental.pallas import tpu as pltpu

F32 = jnp.float32
BF16 = jnp.bfloat16

D_MODEL = 1024
DEPTH = 2
N_SHARD = 4
D_FF = 2816
FF_SH = D_FF // N_SHARD
SSD_HEADS = 16
HEAD_DIM = 64
SSD_GROUPS = 4
GROUP_W = 256
SSD_STATE = 128
CONV_K = 4
CONV_DIM = 2048
ATT_HEADS = 16
MIX_W = 2048
MIX_SH = MIX_W // N_SHARD
IN_PROJ = 6160
IN_SH = IN_PROJ // N_SHARD
IN_PAD = 6272
PROJ_TN = 896
COL_Z, COL_XBC, COL_Q, COL_K, COL_V, COL_DT = 0, 1024, 3072, 4096, 5120, 6144
EPS = 1e-6
NEG = -1e30
SSD_L = 256
ATT_B = 512
ROW_T = 512
HALF_T = ROW_T // 2
GU_T = 1024
TK_W = 2048
CONV_CT = 256
CONV_R = 256
PAD_R = 8

ADAM_LR, ADAM_B1, ADAM_B2, ADAM_EPS, ADAM_WD, ADAM_STEP = 0.001, 0.9, 0.999, 1e-08, 0.01, 10

NN = (((1,), (0,)), ((), ()))
NT = (((1,), (1,)), ((), ()))
TN = (((0,), (0,)), ((), ()))

VMEM_LIMIT = 56 * 1024 * 1024


def _cp(*sem):
    return pltpu.CompilerParams(dimension_semantics=sem, vmem_limit_bytes=VMEM_LIMIT)


def _dot(a, b, dims):
    return lax.dot_general(a, b, dims, preferred_element_type=F32)


def _sigmoid(x):
    return 0.5 * jnp.tanh(0.5 * x) + 0.5


def _softplus(x):
    return jnp.maximum(x, 0.0) + jnp.log(1.0 + jnp.exp(-jnp.abs(x)))


def _mm(name, pairs, out_shape, out_spec, grid, dims, acc_shape, res=None, scale=1.0, post=None, post_in=()):
    nk = grid[2]
    npair = len(pairs)
    npost = len(post_in)

    def body(*refs):
        ab = refs[:2 * npair]
        pos = 2 * npair
        res_ref = None
        if res is not None:
            res_ref = refs[pos]
            pos += 1
        pin = refs[pos:pos + npost]
        pos += npost
        out_ref = refs[pos]
        pos += 1
        if post is not None:
            out2_ref = refs[pos]
            pos += 1
        s = None
        for p in range(npair):
            d = _dot(ab[2 * p][...].astype(BF16), ab[2 * p + 1][...].astype(BF16), dims)
            s = d if s is None else s + d

        def finish(r):
            if scale != 1.0:
                r = r * scale
            if res_ref is not None:
                r = r + res_ref[...]
            if post == "rmsb":
                @pl.when(pl.program_id(0) == 0)
                def _():
                    out2_ref[...] = jnp.zeros_like(out2_ref)

                xv = pin[0][...]
                rr = lax.rsqrt(jnp.mean(xv * xv, axis=-1, keepdims=True) + EPS)
                xhat = xv * rr
                dxhat = r * pin[1][...]
                out_ref[...] = pin[2][...] + rr * (dxhat - xhat * jnp.mean(dxhat * xhat, axis=-1, keepdims=True))
                out2_ref[...] += jnp.sum(r * xhat, axis=0, keepdims=True)
                return
            out_ref[...] = r.astype(out_ref.dtype)
            if post == "norm":
                rr = lax.rsqrt(jnp.mean(r * r, axis=-1, keepdims=True) + EPS)
                out2_ref[...] = (r * rr * pin[0][...]).astype(BF16)

        if nk == 1:
            finish(s)
            return
        acc = refs[pos]
        k = pl.program_id(2)

        @pl.when(k == 0)
        def _():
            acc[...] = s

        @pl.when(k > 0)
        def _():
            acc[...] += s

        @pl.when(k == nk - 1)
        def _():
            finish(acc[...])

    args, specs = [], []
    for a, a_spec, b, b_spec in pairs:
        args += [a, b]
        specs += [a_spec, b_spec]
    for arr, spec in ([res] if res is not None else []) + list(post_in):
        args.append(arr)
        specs.append(spec)
    sems = ("arbitrary",) * 3 if post == "rmsb" else ("parallel", "parallel", "arbitrary")
    return pl.pallas_call(
        body, out_shape=out_shape, grid=grid, in_specs=specs, out_specs=out_spec,
        scratch_shapes=[] if nk == 1 else [pltpu.VMEM(acc_shape, F32)], name=name,
        compiler_params=_cp(*sems))(*args)


def _rms_fwd(name, x, w):
    T = x.shape[0]

    def body(x_ref, w_ref, o_ref):
        xv = x_ref[...]
        r = lax.rsqrt(jnp.mean(xv * xv, axis=-1, keepdims=True) + EPS)
        o_ref[...] = (xv * r * w_ref[...]).astype(BF16)

    return pl.pallas_call(
        body, out_shape=jax.ShapeDtypeStruct((T, D_MODEL), BF16), grid=(T // ROW_T,),
        in_specs=[pl.BlockSpec((ROW_T, D_MODEL), lambda i: (i, 0)), pl.BlockSpec((1, D_MODEL), lambda i: (0, 0))],
        out_specs=pl.BlockSpec((ROW_T, D_MODEL), lambda i: (i, 0)), name=name, compiler_params=_cp("parallel"))(x, w)


def _loss_grad(name, y, t):
    T = y.shape[0]

    def body(y_ref, t_ref, dy_ref, l_ref):
        @pl.when(pl.program_id(0) == 0)
        def _():
            l_ref[...] = jnp.zeros_like(l_ref)

        e = y_ref[...] - t_ref[...]
        dy_ref[...] = e * (1.0 / D_MODEL)
        l_ref[...] += jnp.sum(e * e, axis=0, keepdims=True)

    row = pl.BlockSpec((ROW_T, D_MODEL), lambda i: (i, 0))
    vec = pl.BlockSpec((1, D_MODEL), lambda i: (0, 0))
    return pl.pallas_call(
        body, out_shape=(jax.ShapeDtypeStruct((T, D_MODEL), F32), jax.ShapeDtypeStruct((1, D_MODEL), F32)),
        grid=(T // ROW_T,), in_specs=[row, row], out_specs=(row, vec), name=name,
        compiler_params=_cp("arbitrary"))(y, t)


def _ffn_gate_up(name, h, wg, wu):
    T = h.shape[0]
    tm = min(GU_T, T)

    def body(h_ref, wg_ref, wu_ref, dgf_ref, duf_ref, a_ref):
        for r in range(0, tm, HALF_T):
            rows = slice(r, r + HALF_T)
            hv = h_ref[rows, :]
            g = _dot(hv, wg_ref[...], NT)
            u = _dot(hv, wu_ref[...], NT)
            sg = _sigmoid(g)
            silu = g * sg
            dgf_ref[rows, :] = (u * (sg * (1.0 + g * (1.0 - sg)))).astype(BF16)
            duf_ref[rows, :] = silu.astype(BF16)
            a_ref[rows, :] = (silu * u).astype(BF16)

    wspec = pl.BlockSpec((None, FF_SH, D_MODEL), lambda j, i: (j, 0, 0))
    ospec = pl.BlockSpec((None, tm, FF_SH), lambda j, i: (j, i, 0))
    osh = jax.ShapeDtypeStruct((N_SHARD, T, FF_SH), BF16)
    return pl.pallas_call(
        body, out_shape=(osh, osh, osh), grid=(N_SHARD, T // tm),
        in_specs=[pl.BlockSpec((tm, D_MODEL), lambda j, i: (i, 0)), wspec, wspec],
        out_specs=(ospec, ospec, ospec), name=name, compiler_params=_cp("parallel", "parallel"))(h, wg, wu)


def _ffn_dact(name, dx, wd, g, u):
    T = dx.shape[0]
    tm = min(GU_T, T)

    def body(dx_ref, wd_ref, g_ref, u_ref, dg_ref, du_ref):
        for r in range(0, tm, HALF_T):
            rows = slice(r, r + HALF_T)
            da = 0.5 * _dot(dx_ref[rows, :].astype(BF16), wd_ref[...], NT)
            dg_ref[rows, :] = (da * g_ref[rows, :].astype(F32)).astype(BF16)
            du_ref[rows, :] = (da * u_ref[rows, :].astype(F32)).astype(BF16)

    aspec = pl.BlockSpec((None, tm, FF_SH), lambda j, i: (j, i, 0))
    osh = jax.ShapeDtypeStruct((N_SHARD, T, FF_SH), BF16)
    return pl.pallas_call(
        body, out_shape=(osh, osh), grid=(N_SHARD, T // tm),
        in_specs=[pl.BlockSpec((tm, D_MODEL), lambda j, i: (i, 0)),
                  pl.BlockSpec((None, FF_SH, D_MODEL), lambda j, i: (j, 0, 0)), aspec, aspec],
        out_specs=(aspec, aspec), name=name, compiler_params=_cp("parallel", "parallel"))(dx, wd, g, u)


def _row3():
    return pl.BlockSpec((ROW_T, D_MODEL), lambda i, n, k: (i, 0))


def _vec3():
    return pl.BlockSpec((1, D_MODEL), lambda i, n, k: (0, 0))


def _with_norm(T, next_nw):
    if next_nw is None:
        return dict(out_shape=jax.ShapeDtypeStruct((T, D_MODEL), F32), out_spec=_row3())
    return dict(out_shape=(jax.ShapeDtypeStruct((T, D_MODEL), F32), jax.ShapeDtypeStruct((T, D_MODEL), BF16)),
                out_spec=(_row3(), _row3()), post="norm", post_in=[(next_nw, _vec3())])


def _ffn_fwd(tag, x, h, wg, wu, wd, next_nw):
    T = x.shape[0]
    g, u, a = _ffn_gate_up(tag + "_gu", h, wg, wu)
    if callable(wd):
        wd = wd(a)
    nt = T // ROW_T
    o = _with_norm(T, next_nw)
    xo = _mm(tag + "_down",
             [(a, pl.BlockSpec((None, ROW_T, FF_SH), lambda i, n, k, j=j: (j, i, 0)),
               wd, pl.BlockSpec((None, FF_SH, D_MODEL), lambda i, n, k, j=j: (j, 0, 0))) for j in range(N_SHARD)],
             o.pop("out_shape"), o.pop("out_spec"), (nt, 1, 1), NN, (ROW_T, D_MODEL),
             res=(x, _row3()), scale=0.5, **o)
    return xo, (x, h, g, u, a), wd


def _ffn_bwd(tag, dxo, saved, nw, wg, wu, wd, emit):
    x, h, g, u, a = saved
    T = x.shape[0]
    nt = T // ROW_T
    tkw = min(TK_W, T)
    nw_t = T // tkw
    dg, du = _ffn_dact(tag + "_dact", dxo, wd, g, u)
    actw = lambda f: pl.BlockSpec((None, tkw, FF_SH), f)
    gd = _mm(tag + "_dwd",
             [(a, actw(lambda m, n, k: (m, k, 0)), dxo, pl.BlockSpec((tkw, D_MODEL), lambda m, n, k: (k, 0)))],
             jax.ShapeDtypeStruct((N_SHARD, FF_SH, D_MODEL), BF16),
             pl.BlockSpec((None, FF_SH, D_MODEL), lambda m, n, k: (m, 0, 0)),
             (N_SHARD, 1, nw_t), TN, (FF_SH, D_MODEL), scale=0.5)
    hspec = pl.BlockSpec((tkw, D_MODEL), lambda j, n, k: (k, 0))
    gsh = jax.ShapeDtypeStruct((N_SHARD, FF_SH, D_MODEL), BF16)
    gspec = pl.BlockSpec((None, FF_SH, D_MODEL), lambda j, n, k: (j, 0, 0))
    gg = _mm(tag + "_dwg", [(dg, actw(lambda j, n, k: (j, k, 0)), h, hspec)], gsh, gspec,
             (N_SHARD, 1, nw_t), TN, (FF_SH, D_MODEL))
    gu = _mm(tag + "_dwu", [(du, actw(lambda j, n, k: (j, k, 0)), h, hspec)], gsh, gspec,
             (N_SHARD, 1, nw_t), TN, (FF_SH, D_MODEL))
    dg = emit(gg, gu, gd, dg)
    act = lambda j: pl.BlockSpec((None, ROW_T, FF_SH), lambda i, n, k: (j, i, 0))
    wsp = lambda j: pl.BlockSpec((None, FF_SH, D_MODEL), lambda i, n, k: (j, 0, 0))
    return _mm(tag + "_dh",
               [(dd, act(j), w, wsp(j)) for j in range(N_SHARD) for dd, w in ((dg, wg), (du, wu))],
               (jax.ShapeDtypeStruct((T, D_MODEL), F32), jax.ShapeDtypeStruct((1, D_MODEL), F32)), (_row3(), _vec3()),
               (nt, 1, 1), NN, (ROW_T, D_MODEL), post="rmsb", post_in=[(x, _row3()), (nw, _vec3()), (dxo, _row3())])


def _seq_rows(ref, start, size, S):
    lo, hi = max(start, 0), min(start + size, S)
    parts = [ref[pl.ds(lo, hi - lo), :]]
    if lo > start:
        parts.insert(0, jnp.zeros((lo - start, ref.shape[1]), F32))
    if start + size > hi:
        parts.append(jnp.zeros((start + size - hi, ref.shape[1]), F32))
    return parts[0] if len(parts) == 1 else jnp.concatenate(parts, axis=0)


XBC_CB = COL_XBC // CONV_CT


def _conv_fwd(name, proj, w, b, B):
    T = proj.shape[0]
    S = T // B
    C = CONV_DIM

    def body(x_ref, w_ref, b_ref, o_ref):
        wv = w_ref[...]
        for c in range(S // CONV_R):
            r0 = c * CONV_R
            ch = _seq_rows(x_ref, r0 - PAD_R, CONV_R + PAD_R, S)
            pre = ch[PAD_R:] * wv[3:4] + b_ref[...]
            for s in range(1, CONV_K):
                pre = pre + pltpu.roll(ch, s, axis=0)[PAD_R:] * wv[3 - s:4 - s]
            o_ref[pl.ds(r0, CONV_R), :] = pre * _sigmoid(pre)

    return pl.pallas_call(
        body, out_shape=jax.ShapeDtypeStruct((T, C), F32), grid=(B, C // CONV_CT),
        in_specs=[pl.BlockSpec((S, CONV_CT), lambda bi, ci: (bi, XBC_CB + ci)),
                  pl.BlockSpec((CONV_K, CONV_CT), lambda bi, ci: (0, ci)),
                  pl.BlockSpec((1, CONV_CT), lambda bi, ci: (0, ci))],
        out_specs=pl.BlockSpec((S, CONV_CT), lambda bi, ci: (bi, ci)), name=name,
        compiler_params=_cp("parallel", "parallel"))(proj, w, b)


def _conv_bwd(name, proj, dxs, dB, dC, w, b, dproj, B):
    T = proj.shape[0]
    S = T // B
    C = CONV_DIM
    RW = CONV_R + PAD_R
    nx, nb = dxs.shape[1] // CONV_CT, dB.shape[1] // CONV_CT

    def body(x_ref, dx_in, db_in, dc_in, w_ref, b_ref, buf_ref, dx_ref, dw_ref, db_ref):
        @pl.when(pl.program_id(1) == 0)
        def _():
            dw_ref[...] = jnp.zeros_like(dw_ref)
            db_ref[...] = jnp.zeros_like(db_ref)

        ci = pl.program_id(0)
        wv = w_ref[...]
        dw = [jnp.zeros((1, CONV_CT), F32) for _ in range(CONV_K)]
        db = jnp.zeros((1, CONV_CT), F32)
        for c in range(S // CONV_R):
            r0 = c * CONV_R
            ch = _seq_rows(x_ref, r0 - PAD_R, RW + PAD_R, S)
            xs = [ch[PAD_R:]] + [pltpu.roll(ch, s, axis=0)[PAD_R:] for s in range(1, CONV_K)]
            pre = b_ref[...] + xs[0] * wv[3:4]
            for s in range(1, CONV_K):
                pre = pre + xs[s] * wv[3 - s:4 - s]
            sg = _sigmoid(pre)
            dout = jnp.where(ci < nx, _seq_rows(dx_in, r0, RW, S),
                             jnp.where(ci < nx + nb, _seq_rows(db_in, r0, RW, S), _seq_rows(dc_in, r0, RW, S)))
            dpre = dout * (sg * (1.0 + pre * (1.0 - sg)))
            dx = dpre[:CONV_R] * wv[3:4]
            for s in range(1, CONV_K):
                dx = dx + pltpu.roll(dpre, RW - s, axis=0)[:CONV_R] * wv[3 - s:4 - s]
            dx_ref[pl.ds(r0, CONV_R), :] = dx.astype(BF16)
            dcur = dpre[:CONV_R]
            db = db + jnp.sum(dcur, axis=0, keepdims=True)
            for s in range(CONV_K):
                dw[3 - s] = dw[3 - s] + jnp.sum(dcur * xs[s][:CONV_R], axis=0, keepdims=True)
        db_ref[...] += db
        for k in range(CONV_K):
            dw_ref[k:k + 1, :] += dw[k]

    seq = lambda f: pl.BlockSpec((S, CONV_CT), f)
    return pl.pallas_call(
        body,
        out_shape=(jax.ShapeDtypeStruct(dproj.shape, dproj.dtype), jax.ShapeDtypeStruct((CONV_K, C), F32),
                   jax.ShapeDtypeStruct((1, C), F32)),
        grid=(C // CONV_CT, B),
        in_specs=[seq(lambda ci, bi: (bi, XBC_CB + ci)),
                  seq(lambda ci, bi: (bi, jnp.minimum(ci, nx - 1))),
                  seq(lambda ci, bi: (bi, jnp.clip(ci - nx, 0, nb - 1))),
                  seq(lambda ci, bi: (bi, jnp.clip(ci - nx - nb, 0, nb - 1))),
                  pl.BlockSpec((CONV_K, CONV_CT), lambda ci, bi: (0, ci)),
                  pl.BlockSpec((1, CONV_CT), lambda ci, bi: (0, ci)), ANY],
        out_specs=(seq(lambda ci, bi: (bi, XBC_CB + ci)),
                   pl.BlockSpec((CONV_K, CONV_CT), lambda ci, bi: (0, ci)),
                   pl.BlockSpec((1, CONV_CT), lambda ci, bi: (0, ci))),
        input_output_aliases={6: 0},
        name=name, compiler_params=_cp("parallel", "arbitrary"))(proj, dxs, dB, dC, w, b, dproj)


def _tri_sum(tri, x, dims, tri_first, terms=3):
    out, rest = None, x
    for t in range(terms):
        part = rest.astype(BF16)
        if t + 1 < terms:
            rest = rest - part.astype(F32)
        d = _dot(tri, part, dims) if tri_first else _dot(part, tri, dims)
        out = d if out is None else out + d
    return out


def _total(x):
    return jnp.sum(jnp.sum(x, axis=0, keepdims=True), axis=-1, keepdims=True)


def _ssd_common(dtc_ref, dtr_ref, pcol_ref, prow_ref, b_ref, c_ref):
    L = SSD_L
    bias_c, alog_c = pcol_ref[0:1, :], pcol_ref[1:2, :]
    a_c = -jnp.exp(alog_c)
    dt_c = _softplus(dtc_ref[...] + bias_c)
    row = lax.broadcasted_iota(jnp.int32, (L, L), 0)
    col = lax.broadcasted_iota(jnp.int32, (L, L), 1)
    causal = row >= col
    tri = causal.astype(BF16)
    cum_c = _tri_sum(tri, dt_c * a_c, NN, True)
    a_r = -jnp.exp(prow_ref[:, 1:2])
    dt_r = _softplus(dtr_ref[...] + prow_ref[:, 0:1])
    cum_r = _tri_sum(tri, dt_r * a_r, NT, False)
    bb = b_ref[...].astype(BF16)
    cb = c_ref[...].astype(BF16)
    G = _dot(cb, bb, NT)
    return a_c, dt_c, causal, tri, cum_c, cum_r, bb, cb, G


def _ssd_fwd(name, xc, proj, dtc, dtr, pcol, prow, nw, B):
    T = xc.shape[0]
    S = T // B
    nb = S // SSD_L
    L = SSD_L

    def body(xs_ref, b_ref, c_ref, z_ref, dtc_ref, dtr_ref, pcol_ref, prow_ref, nw_ref, y_ref, yn_ref, hs_ref, H, yo_s):
        @pl.when(pl.program_id(2) == 0)
        def _():
            H[...] = jnp.zeros_like(H)

        a_c, dt_c, causal, tri, cum_c, cum_r, bb, cb, G = _ssd_common(dtc_ref, dtr_ref, pcol_ref, prow_ref, b_ref, c_ref)
        dsk = pcol_ref[2:3, :]
        clast = cum_c[L - 1:L, :]
        bf = b_ref[...]
        for h in range(4):
            hs_ref[h] = H[h]
            yo_s[h] = _dot(cb, H[h].astype(BF16), NN)
        for h in range(4):
            sl = slice(HEAD_DIM * h, HEAD_DIM * (h + 1))
            cc = cum_c[:, h:h + 1]
            lm = jnp.exp(jnp.where(causal, cc - cum_r[h:h + 1, :], NEG))
            M = (G * lm).astype(BF16)
            xh = xs_ref[:, sl]
            Xb = (xh * dt_c[:, h:h + 1]).astype(BF16)
            Hh = H[h]
            y = _dot(M, Xb, NN) + jnp.exp(cc) * yo_s[h]
            y_ref[:, sl] = y + dsk[:, h:h + 1] * xh
            cl = clast[:, h:h + 1]
            Bw = (bf * jnp.exp(cl - cc)).astype(BF16)
            H[h] = jnp.exp(cl) * Hh + _dot(Bw, Xb, TN)
        zv = z_ref[...]
        y2 = y_ref[...] * (zv * _sigmoid(zv))
        r = lax.rsqrt(jnp.mean(y2 * y2, axis=-1, keepdims=True) + EPS)
        yn_ref[...] = (y2 * r * nw_ref[...]).astype(BF16)

    rowi = lambda b, g, i: b * nb + i
    grp = pl.BlockSpec((L, GROUP_W), lambda b, g, i: (rowi(b, g, i), g))
    return pl.pallas_call(
        body,
        out_shape=(jax.ShapeDtypeStruct((T, 1024), F32), jax.ShapeDtypeStruct((T, 1024), BF16),
                   jax.ShapeDtypeStruct((B, SSD_GROUPS, nb, 4, SSD_STATE, HEAD_DIM), F32)),
        grid=(B, SSD_GROUPS, nb),
        in_specs=[grp,
                  pl.BlockSpec((L, SSD_STATE), lambda b, g, i: (rowi(b, g, i), 8 + g)),
                  pl.BlockSpec((L, SSD_STATE), lambda b, g, i: (rowi(b, g, i), 12 + g)),
                  grp,
                  pl.BlockSpec((None, L, 4), lambda b, g, i: (g, rowi(b, g, i), 0)),
                  pl.BlockSpec((None, 4, L), lambda b, g, i: (g, 0, rowi(b, g, i))),
                  pl.BlockSpec((None, 3, 4), lambda b, g, i: (g, 0, 0)),
                  pl.BlockSpec((None, 4, 3), lambda b, g, i: (g, 0, 0)),
                  pl.BlockSpec((1, GROUP_W), lambda b, g, i: (0, g))],
        out_specs=(grp, grp,
                   pl.BlockSpec((None, None, None, 4, SSD_STATE, HEAD_DIM), lambda b, g, i: (b, g, i, 0, 0, 0))),
        scratch_shapes=[pltpu.VMEM((4, SSD_STATE, HEAD_DIM), F32), pltpu.VMEM((4, L, HEAD_DIM), F32)], name=name,
        compiler_params=_cp("parallel", "parallel", "arbitrary"))(xc, xc, xc, proj, dtc, dtr, pcol, prow, nw)


def _ssd_bwd(name, dyn, Y, xc, proj, dtc, dtr, pcol, prow, nw, hs, dproj, B):
    T = xc.shape[0]
    S = T // B
    nb = S // SSD_L
    L = SSD_L

    def body(dyn_ref, y_ref, xs_ref, b_ref, c_ref, z_ref, dtc_ref, dtr_ref, pcol_ref, prow_ref, nw_ref, hs_ref, buf_ref,
             dxs_ref, db_ref, dc_ref, dz_ref, ddt_ref, dpar_ref, dnw_ref, dH, dm_s, dxo_s, ea_s, ex_s):
        @pl.when(pl.program_id(2) == 0)
        def _():
            dH[...] = jnp.zeros_like(dH)
            dpar_ref[...] = jnp.zeros_like(dpar_ref)
            dnw_ref[...] = jnp.zeros_like(dnw_ref)

        a_c, dt_c, causal, tri, cum_c, cum_r, bb, cb, G = _ssd_common(dtc_ref, dtr_ref, pcol_ref, prow_ref, b_ref, c_ref)
        dsk = pcol_ref[2:3, :]
        clast = cum_c[L - 1:L, :]
        bf = b_ref[...]
        cf = c_ref[...]
        Yv = y_ref[...]
        zv = z_ref[...]
        sz = _sigmoid(zv)
        silu = zv * sz
        y2 = Yv * silu
        r = lax.rsqrt(jnp.mean(y2 * y2, axis=-1, keepdims=True) + EPS)
        yhat = y2 * r
        dyv = dyn_ref[...]
        dnw_ref[...] += jnp.sum(dyv * yhat, axis=0, keepdims=True)
        dyhat = dyv * nw_ref[...]
        dy2 = r * (dyhat - yhat * jnp.mean(dyhat * yhat, axis=-1, keepdims=True))
        dY = dy2 * silu
        dz_ref[...] = (dy2 * Yv * (sz * (1.0 + zv * (1.0 - sz)))).astype(BF16)

        lane4 = lax.broadcasted_iota(jnp.int32, (1, 4), 1)
        dG = jnp.zeros((L, L), F32)
        dBs = jnp.zeros((L, SSD_STATE), F32)
        dCs = jnp.zeros((L, SSD_STATE), F32)
        ddsk = jnp.zeros((1, 4), F32)
        dcl = jnp.zeros((1, 4), F32)
        for h in range(4):
            sl = slice(HEAD_DIM * h, HEAD_DIM * (h + 1))
            xb = (xs_ref[:, sl] * dt_c[:, h:h + 1]).astype(BF16)
            dm_s[h] = _dot(dY[:, sl].astype(BF16), xb, NT)
            dxo_s[h] = _dot(bb, dH[h].astype(BF16), NN)
        for h in range(4):
            sl = slice(HEAD_DIM * h, HEAD_DIM * (h + 1))
            onehot = (lane4 == h).astype(F32)
            cc = cum_c[:, h:h + 1]
            cl = clast[:, h:h + 1]
            lm = jnp.exp(jnp.where(causal, cc - cum_r[h:h + 1, :], NEG))
            M = (G * lm).astype(BF16)
            xh = xs_ref[:, sl]
            dth = dt_c[:, h:h + 1]
            X = xh * dth
            Xb = X.astype(BF16)
            dYh = dY[:, sl]
            dYb = dYh.astype(BF16)
            Hb = hs_ref[h].astype(BF16)
            dHh = dH[h]
            dHb = dHh.astype(BF16)
            alpha = jnp.exp(cc)
            beta = jnp.exp(cl - cc)
            dXoff = beta * dxo_s[h]
            dX = _dot(M, dYb, TN) + dXoff
            dG = dG + dm_s[h] * lm
            dCs = dCs + _dot((alpha * dYh).astype(BF16), Hb, NT)
            dBs = dBs + _dot((beta * X).astype(BF16), dHb, NT)
            ypre = Yv[:, sl] - dsk[:, h:h + 1] * xh
            ea_s[:, sl] = dYb.astype(F32) * ypre - Xb.astype(F32) * dX
            ex_s[:, sl] = dX * xh
            dcl_h = (_total(dHh * (jnp.exp(cl) * hs_ref[h])) + _total(Xb.astype(F32) * dXoff))
            dcl = dcl + dcl_h * onehot
            ddsk = ddsk + _total(dYh * xh) * onehot
            dxs_ref[:, sl] = dsk[:, h:h + 1] * dYh + dX * dth
            dH[h] = jnp.exp(cl) * dHh + _dot((alpha * cf).astype(BF16), dYb, TN)
        dGb = dG.astype(BF16)
        dc_ref[...] = _dot(dGb, bb, NN) + dCs
        db_ref[...] = _dot(dGb, cb, TN) + dBs
        feat = lax.broadcasted_iota(jnp.int32, (GROUP_W, 4), 0)
        head = lax.broadcasted_iota(jnp.int32, (GROUP_W, 4), 1) * HEAD_DIM
        sel = ((feat >= head) & (feat < head + HEAD_DIM)).astype(BF16)
        dA = _tri_sum(sel, ea_s[...], NN, False)
        ddtx = _tri_sum(sel, ex_s[...], NN, False)
        last = lax.broadcasted_iota(jnp.int32, (L, 1), 0) == L - 1
        dA = dA + jnp.where(last, dcl, 0.0)
        dadt = _tri_sum(tri, dA, TN, True)
        ddt = dadt * a_c + ddtx
        d_a = jnp.sum(dadt * dt_c, axis=0, keepdims=True)
        ddraw = ddt * _sigmoid(dtc_ref[...] + pcol_ref[0:1, :])
        ddt_ref[...] = ddraw
        dpar_ref[0:1, :] += jnp.sum(ddraw, axis=0, keepdims=True)
        dpar_ref[1:2, :] += d_a * a_c
        dpar_ref[2:3, :] += ddsk

    rowi = lambda b, g, i: b * nb + (nb - 1 - i)
    grp = pl.BlockSpec((L, GROUP_W), lambda b, g, i: (rowi(b, g, i), g))
    st = pl.BlockSpec((L, SSD_STATE), lambda b, g, i: (rowi(b, g, i), g))
    f = jax.ShapeDtypeStruct
    return pl.pallas_call(
        body,
        out_shape=(f((T, 1024), F32), f((T, 512), F32), f((T, 512), F32), f(dproj.shape, dproj.dtype),
                   f((SSD_GROUPS, T, 4), F32), f((B, SSD_GROUPS, 3, 4), F32), f((B, 1, 1024), F32)),
        grid=(B, SSD_GROUPS, nb),
        in_specs=[grp, grp, grp,
                  pl.BlockSpec((L, SSD_STATE), lambda b, g, i: (rowi(b, g, i), 8 + g)),
                  pl.BlockSpec((L, SSD_STATE), lambda b, g, i: (rowi(b, g, i), 12 + g)),
                  grp,
                  pl.BlockSpec((None, L, 4), lambda b, g, i: (g, rowi(b, g, i), 0)),
                  pl.BlockSpec((None, 4, L), lambda b, g, i: (g, 0, rowi(b, g, i))),
                  pl.BlockSpec((None, 3, 4), lambda b, g, i: (g, 0, 0)),
                  pl.BlockSpec((None, 4, 3), lambda b, g, i: (g, 0, 0)),
                  pl.BlockSpec((1, GROUP_W), lambda b, g, i: (0, g)),
                  pl.BlockSpec((None, None, None, 4, SSD_STATE, HEAD_DIM), lambda b, g, i: (b, g, nb - 1 - i, 0, 0, 0)),
                  ANY],
        out_specs=(grp, st, st, grp,
                   pl.BlockSpec((None, L, 4), lambda b, g, i: (g, rowi(b, g, i), 0)),
                   pl.BlockSpec((None, None, 3, 4), lambda b, g, i: (b, g, 0, 0)),
                   pl.BlockSpec((None, 1, GROUP_W), lambda b, g, i: (b, 0, g))),
        input_output_aliases={12: 3},
        scratch_shapes=[pltpu.VMEM((4, SSD_STATE, HEAD_DIM), F32), pltpu.VMEM((4, L, L), F32),
                        pltpu.VMEM((4, L, HEAD_DIM), F32), pltpu.VMEM((L, GROUP_W), F32),
                        pltpu.VMEM((L, GROUP_W), F32)], name=name,
        compiler_params=_cp("parallel", "parallel", "arbitrary"))(
            dyn, Y, xc, xc, xc, proj, dtc, dtr, pcol, prow, nw, hs, dproj)


def _head_sel():
    sel = (np.arange(1024)[:, None] // HEAD_DIM == np.arange(ATT_HEADS)[None, :]).astype(np.float32)
    return jnp.asarray(sel, BF16), jnp.asarray(sel.T, BF16)


def _head_rms(xv, sel, selT):
    ms = _tri_sum(sel, xv * xv, NN, False, 1) * (1.0 / HEAD_DIM)
    return _tri_sum(selT, lax.rsqrt(ms + EPS), NN, False, 2)


def _headnorm_fwd(name, proj, col_block, w):
    T = proj.shape[0]
    sel, selT = _head_sel()

    def body(x_ref, w_ref, sel_ref, selT_ref, o_ref):
        xv = x_ref[...]
        o_ref[...] = (xv * _head_rms(xv, sel_ref[...], selT_ref[...]) * w_ref[...]).astype(BF16)

    full = lambda shp: pl.BlockSpec(shp, lambda i: (0, 0))
    return pl.pallas_call(
        body, out_shape=jax.ShapeDtypeStruct((T, 1024), BF16), grid=(T // ROW_T,),
        in_specs=[pl.BlockSpec((ROW_T, 1024), lambda i: (i, col_block)), full((1, 1024)), full((1024, ATT_HEADS)),
                  full((ATT_HEADS, 1024))],
        out_specs=pl.BlockSpec((ROW_T, 1024), lambda i: (i, 0)), name=name, compiler_params=_cp("parallel"))(
            proj, jnp.tile(w, (1, ATT_HEADS)), sel, selT)


def _headnorm_bwd(name, dn, proj, col_block, w, dproj):
    T = proj.shape[0]
    sel, selT = _head_sel()

    def body(dn_ref, x_ref, w_ref, sel_ref, selT_ref, buf_ref, dx_ref, dw_ref):
        @pl.when(pl.program_id(0) == 0)
        def _():
            dw_ref[...] = jnp.zeros_like(dw_ref)

        xv = x_ref[...]
        sl, slT = sel_ref[...], selT_ref[...]
        rb = _head_rms(xv, sl, slT)
        xhat = xv * rb
        dnv = dn_ref[...]
        dxhat = dnv * w_ref[...]
        mean = _tri_sum(slT, _tri_sum(sl, dxhat * xhat, NN, False, 2) * (1.0 / HEAD_DIM), NN, False, 2)
        dx_ref[...] = (rb * (dxhat - xhat * mean)).astype(BF16)
        dw_ref[...] += jnp.sum(dnv * xhat, axis=0, keepdims=True)

    here = pl.BlockSpec((ROW_T, 1024), lambda i: (i, col_block))
    full = lambda shp: pl.BlockSpec(shp, lambda i: (0, 0))
    dx, dw = pl.pallas_call(
        body, out_shape=(jax.ShapeDtypeStruct(dproj.shape, dproj.dtype), jax.ShapeDtypeStruct((1, 1024), F32)),
        grid=(T // ROW_T,),
        in_specs=[pl.BlockSpec((ROW_T, 1024), lambda i: (i, 0)), here, full((1, 1024)), full((1024, ATT_HEADS)),
                  full((ATT_HEADS, 1024)), ANY],
        out_specs=(here, full((1, 1024))), input_output_aliases={5: 0},
        name=name, compiler_params=_cp("arbitrary"))(dn, proj, jnp.tile(w, (1, ATT_HEADS)), sel, selT, dproj)
    return dx, jnp.sum(dw.reshape(ATT_HEADS, HEAD_DIM), axis=0, keepdims=True)


def _att_bias(nq):
    j = np.arange(ATT_B)[:, None]
    i = np.arange(ATT_B)[None, :]
    out = np.empty((nq, ATT_B, ATT_B), np.float32)
    for dblk in range(nq):
        dl = ATT_B * dblk + i - j
        cnt = ((dl >= 0) & (dl <= 128)).astype(np.float32)
        cnt += ((dl >= 0) & (dl % 4 == 0) & (dl <= 512))
        cnt += ((dl >= 0) & (dl % 16 == 0) & (dl <= 2048))
        out[dblk] = np.where(cnt > 0, np.log(np.maximum(cnt, 1.0)), NEG)
    return jnp.asarray(out)


def _row_pair(nq):
    def f(r, c):
        first = c <= r
        return jnp.where(first, r, nq - 1 - r), jnp.where(first, c, c - (r + 1))
    return f


def _col_pair(nq):
    def f(r, c):
        first = c < nq - r
        kj = jnp.where(first, r, nq - 1 - r)
        return jnp.where(first, r + c, nq - 1 - r + (c - (nq - r))), kj
    return f


ATT_SCALE = 1.0 / math.sqrt(HEAD_DIM)
ATT_HS = 8
ATT_W = ATT_HS * HEAD_DIM


def _att_maps(nq, qk):
    return dict(
        q_tok=lambda b, g, r, c: (b * nq + qk(r, c)[0], g),
        k_tok=lambda b, g, r, c: (b * nq + qk(r, c)[1], g),
        v_tok=lambda b, g, r, c: (b * nq + qk(r, c)[1], COL_V // ATT_W + g),
        q_feat=lambda b, g, r, c: (g, b * nq + qk(r, c)[0]),
        k_feat=lambda b, g, r, c: (g, b * nq + qk(r, c)[1]),
        bias=lambda b, g, r, c: (qk(r, c)[0] - qk(r, c)[1], 0, 0),
        lse=lambda b, g, r, c: (g, 0, b * nq + qk(r, c)[0]),
        do_tok=lambda b, g, r, c: (b * nq + qk(r, c)[0], 1024 // ATT_W + g))


def _att_fwd(name, kn, qT, vT, bias, B):
    T = kn.shape[0]
    nq = (T // B) // ATT_B
    qk = _row_pair(nq)
    mp = _att_maps(nq, qk)

    def body(k_ref, qT_ref, vT_ref, bias_ref, oT_ref, lse_ref, m_s, l_s, acc_s, s_s):
        qi, kj = qk(pl.program_id(2), pl.program_id(3))

        @pl.when(kj == 0)
        def _():
            m_s[...] = jnp.full_like(m_s, NEG)
            l_s[...] = jnp.zeros_like(l_s)
            acc_s[...] = jnp.zeros_like(acc_s)

        bv = bias_ref[...]
        for h in range(ATT_HS):
            rs = slice(HEAD_DIM * h, HEAD_DIM * (h + 1))
            s_s[h] = _dot(k_ref[:, rs], qT_ref[rs, :], NN)
        for h in range(ATT_HS):
            rs = slice(HEAD_DIM * h, HEAD_DIM * (h + 1))
            s = s_s[h] + bv
            m_prev = m_s[h:h + 1, :]
            m_new = jnp.maximum(m_prev, jnp.max(s, axis=0, keepdims=True))
            alpha = jnp.exp(m_prev - m_new)
            p = jnp.exp(s - m_new)
            l_s[h:h + 1, :] = alpha * l_s[h:h + 1, :] + jnp.sum(p, axis=0, keepdims=True)
            acc_s[rs, :] = alpha * acc_s[rs, :] + _dot(vT_ref[rs, :], p.astype(BF16), NN)
            m_s[h:h + 1, :] = m_new

        @pl.when(kj == qi)
        def _():
            for h in range(ATT_HS):
                rs = slice(HEAD_DIM * h, HEAD_DIM * (h + 1))
                oT_ref[rs, :] = (acc_s[rs, :] / l_s[h:h + 1, :]).astype(BF16)
            lse_ref[...] = m_s[...] + jnp.log(l_s[...])

    tok = (ATT_B, ATT_W)
    feat = (ATT_W, ATT_B)
    return pl.pallas_call(
        body,
        out_shape=(jax.ShapeDtypeStruct((1024, T), BF16), jax.ShapeDtypeStruct((ATT_HEADS // ATT_HS, ATT_HS, T), F32)),
        grid=(B, ATT_HEADS // ATT_HS, nq // 2, nq + 1),
        in_specs=[pl.BlockSpec(tok, mp["k_tok"]), pl.BlockSpec(feat, mp["q_feat"]), pl.BlockSpec(feat, mp["k_feat"]),
                  pl.BlockSpec((None, ATT_B, ATT_B), mp["bias"])],
        out_specs=(pl.BlockSpec(feat, mp["q_feat"]), pl.BlockSpec((None, ATT_HS, ATT_B), mp["lse"])),
        scratch_shapes=[pltpu.VMEM((ATT_HS, ATT_B), F32), pltpu.VMEM((ATT_HS, ATT_B), F32),
                        pltpu.VMEM((ATT_W, ATT_B), F32), pltpu.VMEM((ATT_HS, ATT_B, ATT_B), F32)],
        name=name, compiler_params=_cp("parallel", "parallel", "arbitrary", "arbitrary"))(kn, qT, vT, bias)


def _att_scores(k_ref, qT_ref, v_ref, doT_ref, s_s, dp_s):
    for h in range(ATT_HS):
        rs = slice(HEAD_DIM * h, HEAD_DIM * (h + 1))
        s_s[h] = _dot(k_ref[:, rs], qT_ref[rs, :], NN)
        dp_s[h] = _dot(v_ref[:, rs].astype(BF16), doT_ref[rs, :].astype(BF16), NN)


def _att_p_ds(s_s, dp_s, doT_ref, oT_ref, lse_ref, bv, h):
    rs = slice(HEAD_DIM * h, HEAD_DIM * (h + 1))
    delta = jnp.sum(doT_ref[rs, :] * oT_ref[rs, :].astype(F32), axis=0, keepdims=True)
    p = jnp.exp(s_s[h] + bv - lse_ref[h:h + 1, :])
    return p, p * (dp_s[h] - delta)


def _att_bwd(name, kn, qT, proj, qn, knT, bias, doT, oT, lse, dyn, dproj, B):
    T = kn.shape[0]
    S = T // B
    nq = S // ATT_B
    qk = _col_pair(nq)
    mp = _att_maps(nq, qk)

    def body(k_ref, qT_ref, v_ref, q_ref, kT_ref, bias_ref, doT_ref, oT_ref, lse_ref, do_ref, buf_ref,
             dqT_ref, dk_ref, dv_ref, dk_s, dv_s, dq_s, s_s, dp_s):
        r, c = pl.program_id(2), pl.program_id(3)
        qi, kj = qk(r, c)

        @pl.when((r == 0) & (c == 0))
        def _():
            dq_s[...] = jnp.zeros_like(dq_s)

        @pl.when(qi == kj)
        def _():
            dk_s[...] = jnp.zeros_like(dk_s)
            dv_s[...] = jnp.zeros_like(dv_s)

        bv = bias_ref[...]
        _att_scores(k_ref, qT_ref, v_ref, doT_ref, s_s, dp_s)
        dq_blk = dq_s.at[qi]
        for h in range(ATT_HS):
            rs = slice(HEAD_DIM * h, HEAD_DIM * (h + 1))
            p, ds = _att_p_ds(s_s, dp_s, doT_ref, oT_ref, lse_ref, bv, h)
            dsb = ds.astype(BF16)
            dv_s[h] += _dot(p.astype(BF16), do_ref[:, rs].astype(BF16), NN)
            dk_s[h] += _dot(dsb, q_ref[:, rs], NN)
            dq_blk[rs, :] += _dot(kT_ref[rs, :], dsb, NN)

        @pl.when(qi == nq - 1)
        def _():
            for h in range(ATT_HS):
                rs = slice(HEAD_DIM * h, HEAD_DIM * (h + 1))
                dk_ref[:, rs] = dk_s[h] * ATT_SCALE
                dv_ref[:, rs] = dv_s[h].astype(BF16)

        @pl.when((r == nq // 2 - 1) & (c == nq))
        def _():
            for q in range(nq):
                dqT_ref[:, ATT_B * q:ATT_B * (q + 1)] = dq_s[q] * ATT_SCALE

    tok = (ATT_B, ATT_W)
    feat = (ATT_W, ATT_B)
    v_cb = COL_V // ATT_W
    return pl.pallas_call(
        body,
        out_shape=(jax.ShapeDtypeStruct((1024, T), F32), jax.ShapeDtypeStruct((T, 1024), F32),
                   jax.ShapeDtypeStruct(dproj.shape, dproj.dtype)),
        grid=(B, ATT_HEADS // ATT_HS, nq // 2, nq + 1),
        in_specs=[pl.BlockSpec(tok, mp["k_tok"]), pl.BlockSpec(feat, mp["q_feat"]), pl.BlockSpec(tok, mp["v_tok"]),
                  pl.BlockSpec(tok, mp["q_tok"]), pl.BlockSpec(feat, mp["k_feat"]),
                  pl.BlockSpec((None, ATT_B, ATT_B), mp["bias"]),
                  pl.BlockSpec(feat, mp["q_feat"]), pl.BlockSpec(feat, mp["q_feat"]),
                  pl.BlockSpec((None, ATT_HS, ATT_B), mp["lse"]), pl.BlockSpec(tok, mp["do_tok"]), ANY],
        out_specs=(pl.BlockSpec((ATT_W, S), lambda b, g, r, c: (g, b)),
                   pl.BlockSpec(tok, mp["k_tok"]),
                   pl.BlockSpec(tok, lambda b, g, r, c: (b * nq + qk(r, c)[1], v_cb + g))),
        input_output_aliases={10: 2},
        scratch_shapes=[pltpu.VMEM((ATT_HS, ATT_B, HEAD_DIM), F32), pltpu.VMEM((ATT_HS, ATT_B, HEAD_DIM), F32),
                        pltpu.VMEM((nq, ATT_W, ATT_B), F32),
                        pltpu.VMEM((ATT_HS, ATT_B, ATT_B), F32), pltpu.VMEM((ATT_HS, ATT_B, ATT_B), F32)],
        name=name, compiler_params=_cp("parallel", "parallel", "arbitrary", "arbitrary"))(
            kn, qT, proj, qn, knT, bias, doT, oT, lse, dyn, dproj)


def _group_cols(v):
    return v.reshape(SSD_GROUPS, 4)


def _ssd_params(p):
    rows = jnp.stack([_group_cols(p["dt_bias"]), _group_cols(p["a_log"]), _group_cols(p["d_skip"])], axis=1)
    return rows, jnp.swapaxes(rows, 1, 2)


def _dymix(name, dx, wout):
    T = dx.shape[0]

    def body(dx_ref, w_ref, o_ref):
        dxb = dx_ref[...].astype(BF16)
        for n in range(N_SHARD):
            o_ref[:, MIX_SH * n:MIX_SH * (n + 1)] = _dot(dxb, w_ref[n], NT)

    return pl.pallas_call(
        body, out_shape=jax.ShapeDtypeStruct((T, MIX_W), F32), grid=(T // ROW_T,),
        in_specs=[pl.BlockSpec((ROW_T, D_MODEL), lambda i: (i, 0)),
                  pl.BlockSpec((N_SHARD, MIX_SH, D_MODEL), lambda i: (0, 0, 0))],
        out_specs=pl.BlockSpec((ROW_T, MIX_W), lambda i: (i, 0)), name=name, compiler_params=_cp("parallel"))(dx, wout)


def _mixer_fwd(tag, x1, h2, p, weights, bias, B):
    T = x1.shape[0]
    S = T // B
    nt = T // ROW_T
    wi = weights("win", h2)
    win, cw = wi["win"], wi["cw"]
    tm = min(GU_T, T)
    proj = _mm(tag + "_proj",
               [(h2, pl.BlockSpec((tm, D_MODEL), lambda j, i, k: (i, 0)),
                 win, pl.BlockSpec((D_MODEL, PROJ_TN), lambda j, i, k: (0, j)))],
               jax.ShapeDtypeStruct((T, IN_PAD), F32), pl.BlockSpec((tm, PROJ_TN), lambda j, i, k: (i, j)),
               (IN_PAD // PROJ_TN, T // tm, 1), NN, (tm, PROJ_TN))
    xc = _conv_fwd(tag + "_conv", proj, cw, p["conv_b"][None], B)
    dtraw = proj[:, COL_DT:COL_DT + SSD_HEADS].reshape(T, SSD_GROUPS, 4)
    dtc = jnp.transpose(dtraw, (1, 0, 2))
    dtr = jnp.transpose(dtraw, (1, 2, 0))
    pcol, prow = _ssd_params(p)
    Y, y_ssd, hs = _ssd_fwd(tag + "_ssd", xc, proj, dtc, dtr, pcol, prow, p["ssd_norm"][None], B)
    qn = _headnorm_fwd(tag + "_qn", proj, COL_Q // 1024, p["q_norm"][None])
    kn = _headnorm_fwd(tag + "_kn", proj, COL_K // 1024, p["k_norm"][None])
    qT = (qn * ATT_SCALE).T
    oT, lse = _att_fwd(tag + "_att", kn, qT, proj[:, COL_V:COL_V + 1024].T.astype(BF16), bias, B)
    ymix = jnp.concatenate([y_ssd, oT.T], axis=1)
    rest = weights("rest", ymix)
    o = _with_norm(T, p["ffn2_norm"][None])
    x2, h3 = _mm(tag + "_out",
                 [(ymix, pl.BlockSpec((ROW_T, MIX_SH), lambda i, n, k, j=j: (i, j)),
                   rest["wout"], pl.BlockSpec((None, MIX_SH, D_MODEL), lambda i, n, k, j=j: (j, 0, 0)))
                  for j in range(N_SHARD)],
                 o.pop("out_shape"), o.pop("out_spec"), (nt, 1, 1), NN, (ROW_T, D_MODEL), res=(x1, _row3()), **o)
    saved = dict(x1=x1, h2=h2, proj=proj, xc=xc, dtc=dtc, dtr=dtr, Y=Y, hs=hs,
                 qn=qn, kn=kn, qT=qT, oT=oT, lse=lse, ymix=ymix, win=win, cw=cw, wout=rest["wout"])
    return x2, h3, saved


def _mixer_bwd(tag, dx2, sv, p, bias, B):
    T = dx2.shape[0]
    S = T // B
    nt = T // ROW_T
    sg = {}
    dymix = _dymix(tag + "_dymix", dx2, sv["wout"])
    tkw = min(TK_W, T)
    gwout = _mm(tag + "_dwout",
                [(sv["ymix"], pl.BlockSpec((tkw, MIX_SH), lambda m, n, k: (k, m)),
                  dx2, pl.BlockSpec((tkw, D_MODEL), lambda m, n, k: (k, 0)))],
                jax.ShapeDtypeStruct((N_SHARD, MIX_SH, D_MODEL), BF16),
                pl.BlockSpec((None, MIX_SH, D_MODEL), lambda m, n, k: (m, 0, 0)),
                (N_SHARD, 1, T // tkw), TN, (MIX_SH, D_MODEL))
    proj = sv["proj"]
    doT = dymix[:, 1024:].T
    dproj = lax.empty((T, IN_PAD), BF16)
    dqT, dkn, dproj = _att_bwd(tag + "_attb", sv["kn"], sv["qT"], proj, sv["qn"], sv["kn"].T, bias, doT, sv["oT"],
                               sv["lse"], dymix, dproj, B)
    dproj, sg["q_norm"] = _headnorm_bwd(tag + "_qnb", dqT.T, proj, COL_Q // 1024, p["q_norm"][None], dproj)
    dproj, sg["k_norm"] = _headnorm_bwd(tag + "_knb", dkn, proj, COL_K // 1024, p["k_norm"][None], dproj)
    pcol, prow = _ssd_params(p)
    dxs, dB, dC, dproj, ddt, dpar, dnw = _ssd_bwd(tag + "_ssdb", dymix, sv["Y"], sv["xc"], proj, sv["dtc"], sv["dtr"],
                                                  pcol, prow, p["ssd_norm"][None], sv["hs"], dproj, B)
    dpar = jnp.sum(dpar, axis=0)
    sg["dt_bias"] = dpar[:, 0, :].reshape(SSD_HEADS)
    sg["a_log"] = dpar[:, 1, :].reshape(SSD_HEADS)
    sg["d_skip"] = dpar[:, 2, :].reshape(SSD_HEADS)
    sg["ssd_norm"] = jnp.sum(dnw, axis=0)
    dproj, sg["conv_w"], sg["conv_b"] = _conv_bwd(tag + "_convb", proj, dxs, dB, dC, sv["cw"], p["conv_b"][None],
                                                  dproj, B)
    ddt16 = jnp.transpose(ddt, (1, 0, 2)).reshape(T, SSD_HEADS)
    dproj = lax.dynamic_update_slice(dproj, jnp.pad(ddt16, ((0, 0), (0, IN_PAD - COL_DT - SSD_HEADS))).astype(BF16),
                                     (0, COL_DT))
    win = sv["win"]
    gwin = _mm(tag + "_dwin",
               [(sv["h2"], pl.BlockSpec((tkw, D_MODEL), lambda n, m, k: (k, 0)),
                 dproj, pl.BlockSpec((tkw, PROJ_TN), lambda n, m, k: (k, n)))],
               jax.ShapeDtypeStruct((D_MODEL, IN_PAD), BF16), pl.BlockSpec((D_MODEL, PROJ_TN), lambda n, m, k: (0, n)),
               (IN_PAD // PROJ_TN, 1, T // tkw), TN, (D_MODEL, PROJ_TN))
    dx1, sg["mix_norm"] = _mm(
        tag + "_dh2",
        [(dproj, pl.BlockSpec((ROW_T, PROJ_TN), lambda i, n, k, j=j: (i, j)),
          win, pl.BlockSpec((D_MODEL, PROJ_TN), lambda i, n, k, j=j: (0, j))) for j in range(IN_PAD // PROJ_TN)],
        (jax.ShapeDtypeStruct((T, D_MODEL), F32), jax.ShapeDtypeStruct((1, D_MODEL), F32)), (_row3(), _vec3()),
        (nt, 1, 1), NT, (ROW_T, D_MODEL), post="rmsb",
        post_in=[(sv["x1"], _row3()), (p["mix_norm"][None], _vec3()), (dx2, _row3())])
    return dx1, sg, gwout, gwin


def _win_pack(w):
    return jnp.concatenate([w[:, :3072], w[:, 3088:], w[:, 3072:3088],
                            jnp.zeros((w.shape[0], IN_PAD - IN_PROJ), w.dtype)], axis=1)


def _win_unpack(g):
    return jnp.concatenate([g[:, :3072], g[:, COL_DT:COL_DT + SSD_HEADS], g[:, 3072:COL_DT]], axis=1)


DT_LO = IN_SH * 2 - COL_Q


def _win_from_shards(sh):
    main = IN_SH - DT_LO
    return jnp.concatenate([sh[0], sh[1][:, :main], sh[2][:, SSD_HEADS - DT_LO:], sh[3], sh[1][:, main:],
                            sh[2][:, :SSD_HEADS - DT_LO], jnp.zeros((sh.shape[1], IN_PAD - IN_PROJ), sh.dtype)], axis=1)


def _win_to_shards(g):
    main = IN_SH - DT_LO
    a, b = IN_SH + main, IN_SH + 2 * main
    return jnp.stack([g[:, :IN_SH],
                      jnp.concatenate([g[:, IN_SH:a], g[:, COL_DT:COL_DT + DT_LO]], axis=1),
                      jnp.concatenate([g[:, COL_DT + DT_LO:COL_DT + SSD_HEADS], g[:, a:b]], axis=1),
                      g[:, b:COL_DT]])


def _local_step(x, target, small, weights, scatter, B):
    T = x.shape[0]
    bias = _att_bias((T // B) // ATT_B)
    saved = []
    xl = x
    hl = _rms_fwd("l0f1_rms", x, small["ffn1_norm"][0][None])
    for l in range(DEPTH):
        tag = "l%d" % l
        p = {k: v[l] for k, v in small.items()}
        w1 = weights(l, "ffn1", hl)
        (x1, h2), ffn1, d1 = _ffn_fwd(tag + "f1", xl, hl, w1["g1"], w1["u1"],
                                      lambda after, l=l: weights(l, "ffn1d", after)["d1"], p["mix_norm"][None])
        x2, h3, sv = _mixer_fwd(tag, x1, h2, p, functools.partial(weights, l), bias, B)
        w2 = weights(l, "rest", x2)
        nxt = small["ffn1_norm"][l + 1][None] if l + 1 < DEPTH else None
        xo, ffn2, _ = _ffn_fwd(tag + "f2", x2, h3, w2["g2"], w2["u2"], w2["d2"], nxt)
        xl, hl = xo if nxt is not None else (xo, None)
        saved.append((ffn1, sv, ffn2, dict(g1=w1["g1"], u1=w1["u1"], d1=d1), w2))
    d, lsum = _loss_grad("loss", xl, target)
    sgrads = [None] * DEPTH
    for l in reversed(range(DEPTH)):
        tag = "l%db" % l
        p = {k: v[l] for k, v in small.items()}
        ffn1, sv, ffn2, w1, w2 = saved[l]
        sg = {}
        d, sg["ffn2_norm"] = _ffn_bwd(tag + "f2", d, ffn2, p["ffn2_norm"][None], w2["g2"], w2["u2"], w2["d2"],
                                      lambda gg, gu, gd, c, l=l: scatter(l, "ffn2", dict(g2=gg, u2=gu, d2=gd), c))
        d, sgm, gwout, gwin = _mixer_bwd(tag, d, sv, p, bias, B)
        sg.update(sgm)
        d = scatter(l, "mixer", dict(wout=gwout, win=gwin), d)
        d, sg["ffn1_norm"] = _ffn_bwd(tag + "f1", d, ffn1, p["ffn1_norm"][None], w1["g1"], w1["u1"], w1["d1"],
                                      lambda gg, gu, gd, c, l=l: scatter(l, "ffn1", dict(g1=gg, u1=gu, d1=gd), c))
        sgrads[l] = sg
    return lsum, d, sgrads


MESH = pl.DeviceIdType.MESH
ANY = pl.BlockSpec(memory_space=pl.ANY)


def _place():
    return lax.axis_index("x"), lax.axis_index("y"), lax.axis_index("c")


def _other_chips(x, y):
    return [(1 - x, y), (x, 1 - y), (1 - x, 1 - y)]


HBM = pl.BlockSpec(memory_space=pltpu.HBM)
SEM = pl.BlockSpec(memory_space=pltpu.SEMAPHORE)
EFFECT = pltpu.SideEffectType.DATAFLOW_SIDE_EFFECTING


def _hbm(a):
    return pltpu.with_memory_space_constraint(a, pltpu.HBM)


def _exchange(gather, layer, src, land, send, recv, n, act):
    x, y, c = _place()
    for k, (px, py) in enumerate(_other_chips(x, y)):
        for a in range(n):
            if gather:
                s_out, d_out, d_in = src[a].at[layer], land[a].at[2 * x + y], land[a].at[2 * px + py]
            else:
                s_out, d_out, d_in = src[a].at[2 * px + py], land[a].at[k], land[a].at[k]
            act(pltpu.make_async_remote_copy(
                src_ref=s_out, dst_ref=d_out if act is _start else d_in, send_sem=send.at[k * n + a],
                recv_sem=recv.at[k * n + a], device_id=(px, py, c), device_id_type=MESH))


def _start(cp):
    cp.start()


def _finish(cp):
    cp.wait_send()
    cp.wait_recv()


def _exchange_start(name, gather, layer, srcs, carry):
    n = len(srcs)
    lands = [lax.empty(((N_SHARD,) + s.shape[1:]) if gather else ((3,) + s.shape[1:]), s.dtype) for s in srcs]

    def body(*refs):
        _exchange(gather, layer, refs[:n], refs[n:2 * n], refs[2 * n + 1], refs[2 * n + 2], n, _start)

    srcs = [_hbm(a) for a in srcs]
    thru = [_hbm(a) for a in lands + [carry]]
    out = pl.pallas_call(
        body, name=name,
        out_shape=(pltpu.SemaphoreType.DMA((3 * n,)), pltpu.SemaphoreType.DMA((3 * n,)),
                   *[pltpu.HBM(a.shape, a.dtype) for a in thru]),
        in_specs=[HBM] * (2 * n + 1), out_specs=(SEM, SEM, *[HBM] * (n + 1)),
        input_output_aliases={n + i: 2 + i for i in range(n + 1)},
        compiler_params=pltpu.CompilerParams(has_side_effects=EFFECT))(*srcs, *thru)
    return dict(gather=gather, layer=layer, send=out[0], recv=out[1], srcs=srcs, lands=list(out[2:2 + n])), out[-1]


def _exchange_wait(name, ex, after):
    n = len(ex["srcs"])

    def body(*refs):
        _exchange(ex["gather"], ex["layer"], refs[:n], refs[n:2 * n], refs[2 * n], refs[2 * n + 1], n, _finish)

    out = pl.pallas_call(
        body, name=name, out_shape=[pltpu.HBM(a.shape, a.dtype) for a in ex["lands"]],
        in_specs=[HBM] * (2 * n) + [SEM, SEM, ANY], out_specs=[HBM] * n,
        input_output_aliases={n + i: i for i in range(n)},
        compiler_params=pltpu.CompilerParams(has_side_effects=EFFECT))(
            *ex["srcs"], *ex["lands"], ex["send"], ex["recv"], after)
    return list(out)


def _swap_sibling(name, parts):
    n = len(parts)

    def body(*refs):
        src, dst = refs[:n], refs[n:2 * n]
        send, recv = refs[2 * n:]
        x, y, c = _place()
        cps = [pltpu.make_async_remote_copy(src_ref=src[a], dst_ref=dst[a], send_sem=send.at[a], recv_sem=recv.at[a],
                                            device_id=(x, y, 1 - c), device_id_type=MESH) for a in range(n)]
        for cp in cps:
            cp.start()
        for cp in cps:
            cp.wait_recv()
        for cp in cps:
            cp.wait_send()

    return pl.pallas_call(
        body, out_shape=[jax.ShapeDtypeStruct(p.shape, p.dtype) for p in parts],
        in_specs=[ANY] * n, out_specs=[ANY] * n,
        scratch_shapes=[pltpu.SemaphoreType.DMA((n,)), pltpu.SemaphoreType.DMA((n,))],
        name=name)(*parts)


def _allreduce_small(name, v, after):
    R = v.shape[0]

    def body(v_ref, after_ref, o_ref, buf, send, recv):
        x, y, c = _place()
        me = 4 * x + 2 * y + c
        buf[me] = v_ref[...]
        cps = []
        for k in range(1, 8):
            fx, fy, fc = (k >> 2) & 1, (k >> 1) & 1, k & 1
            px = 1 - x if fx else x
            py = 1 - y if fy else y
            pc = 1 - c if fc else c
            cp = pltpu.make_async_remote_copy(src_ref=v_ref, dst_ref=buf.at[me], send_sem=send.at[k - 1],
                                              recv_sem=recv.at[k - 1], device_id=(px, py, pc), device_id_type=MESH)
            cp.start()
            cps.append((cp, 4 * px + 2 * py + pc))
        for k, (cp, peer) in enumerate(cps):
            pltpu.make_async_remote_copy(src_ref=v_ref, dst_ref=buf.at[peer], send_sem=send.at[k], recv_sem=recv.at[k],
                                         device_id=(x, y, c), device_id_type=MESH).wait_recv()
        for cp, _ in cps:
            cp.wait_send()
        acc = buf[0]
        for d in range(1, 8):
            acc = acc + buf[d]
        o_ref[...] = acc

    return pl.pallas_call(
        body, out_shape=jax.ShapeDtypeStruct((R, 128), F32),
        in_specs=[pl.BlockSpec(memory_space=pltpu.VMEM), ANY], out_specs=pl.BlockSpec(memory_space=pltpu.VMEM),
        scratch_shapes=[pltpu.VMEM((8, R, 128), F32), pltpu.SemaphoreType.DMA((7,)), pltpu.SemaphoreType.DMA((7,))],
        name=name)(v, after)


TILE_BYTES = 1600 * 1024


def _row_tile(r, c=1024):
    for t in (512, 352, 256, 128, 64, 32, 16, 8):
        if r % t == 0 and (t * c * 4 <= TILE_BYTES or t == 8):
            return t
    raise ValueError(r)


def _sum4(name, me, parts, got):
    _, R, C = parts.shape
    tr = _row_tile(R, C)

    def body(me_ref, o_ref, g_ref, s_ref):
        s = o_ref[...].astype(F32)
        for k in range(3):
            s = s + g_ref[k].astype(F32)
        s_ref[...] = s.astype(BF16)

    return pl.pallas_call(
        body, out_shape=jax.ShapeDtypeStruct((R, C), BF16),
        grid_spec=pltpu.PrefetchScalarGridSpec(
            num_scalar_prefetch=1, grid=(R // tr,),
            in_specs=[pl.BlockSpec((None, tr, C), lambda i, me_ref: (me_ref[0], i, 0)),
                      pl.BlockSpec((3, tr, C), lambda i, me_ref: (0, i, 0))],
            out_specs=pl.BlockSpec((tr, C), lambda i, me_ref: (i, 0))),
        name=name, compiler_params=_cp("parallel"))(me, parts, got)


def _adamw(name, w, gparts, m, v):
    R, C = w.shape
    tr = _row_tile(R, C)
    ng = len(gparts)
    c1 = 1.0 - ADAM_B1 ** ADAM_STEP
    c2 = 1.0 - ADAM_B2 ** ADAM_STEP

    def body(*refs):
        w_ref = refs[0]
        g_refs = refs[1:1 + ng]
        m_ref, v_ref, go_ref, d_ref, mo_ref, vo_ref = refs[1 + ng:]
        g = g_refs[0][...]
        for r in g_refs[1:]:
            g = g + r[...]
        mn = ADAM_B1 * m_ref[...] + (1.0 - ADAM_B1) * g
        vn = ADAM_B2 * v_ref[...] + (1.0 - ADAM_B2) * (g * g)
        go_ref[...] = g
        mo_ref[...] = mn
        vo_ref[...] = vn
        d_ref[...] = -ADAM_LR * ((mn / c1) / (jnp.sqrt(vn / c2) + ADAM_EPS) + ADAM_WD * w_ref[...])

    blk = pl.BlockSpec((tr, C), lambda i: (i, 0))
    osh = jax.ShapeDtypeStruct((R, C), F32)
    return pl.pallas_call(
        body, out_shape=(osh, osh, osh, osh), grid=(R // tr,), in_specs=[blk] * (3 + ng), out_specs=(blk,) * 4,
        name=name, compiler_params=_cp("parallel"))(w, *gparts, m, v)


def _adamw_layers(name, w, sums, m, v):
    _, R, C = w.shape
    tr = _row_tile(R, C)
    nr = R // tr
    c1 = 1.0 - ADAM_B1 ** ADAM_STEP
    c2 = 1.0 - ADAM_B2 ** ADAM_STEP

    def body(w_ref, a0, b0, a1, b1, m_ref, v_ref, go_ref, d_ref, mo_ref, vo_ref):
        f = lambda r: r[...].astype(F32)
        g = jnp.where(pl.program_id(0) == 0, f(a0) + f(b0), f(a1) + f(b1))
        mn = ADAM_B1 * m_ref[...] + (1.0 - ADAM_B1) * g
        vn = ADAM_B2 * v_ref[...] + (1.0 - ADAM_B2) * (g * g)
        go_ref[...] = g
        mo_ref[...] = mn
        vo_ref[...] = vn
        d_ref[...] = -ADAM_LR * ((mn / c1) / (jnp.sqrt(vn / c2) + ADAM_EPS) + ADAM_WD * w_ref[...])

    blk = pl.BlockSpec((None, tr, C), lambda l, i: (l, i, 0))
    lay0 = pl.BlockSpec((tr, C), lambda l, i: (jnp.where(l == 0, i, nr - 1), 0))
    lay1 = pl.BlockSpec((tr, C), lambda l, i: (jnp.where(l == 1, i, 0), 0))
    oblk = pl.BlockSpec((tr, C), lambda l, i: (l * nr + i, 0))
    osh = jax.ShapeDtypeStruct((DEPTH * R, C), F32)
    res = pl.pallas_call(
        body, out_shape=(osh, osh, osh, osh), grid=(DEPTH, nr),
        in_specs=[blk, lay0, lay0, lay1, lay1, blk, blk], out_specs=(oblk,) * 4,
        name=name, compiler_params=_cp("arbitrary", "arbitrary"))(w, *sums[0], *sums[1], m, v)
    return [r.reshape(w.shape) for r in res]


BIG = [("ffn1_w_gate", "g1"), ("ffn1_w_up", "u1"), ("ffn1_w_down", "d1"), ("w_in", "win"), ("w_out", "wout"),
       ("ffn2_w_gate", "g2"), ("ffn2_w_up", "u2"), ("ffn2_w_down", "d2")]
SMALL = ["ffn1_norm", "mix_norm", "conv_b", "dt_bias", "a_log", "d_skip", "ssd_norm", "q_norm", "k_norm", "ffn2_norm"]
WEIGHTS = ["ffn1_norm", "ffn1_w_gate", "ffn1_w_up", "ffn1_w_down", "mix_norm", "w_in", "conv_w", "conv_b", "dt_bias",
           "a_log", "d_skip", "ssd_norm", "q_norm", "k_norm", "w_out", "ffn2_norm", "ffn2_w_gate", "ffn2_w_up",
           "ffn2_w_down"]
CONV_SH = CONV_DIM // N_SHARD
TRANSPOSED = ("g1", "u1", "g2", "u2")
GATHER_GROUPS = [(0, "ffn1", ["g1", "u1"]), (0, "ffn1d", ["d1"]), (0, "win", ["win", "cw"]),
                 (0, "rest", ["wout", "g2", "u2", "d2"]),
                 (1, "all", ["g1", "u1", "d1", "win", "cw", "wout", "g2", "u2", "d2"])]


def _pad128(v):
    v = v.reshape(-1)
    return jnp.pad(v, (0, (-v.shape[0]) % 128))


def _pack(pieces):
    flat, offs, pos = [], [], 0
    for p in pieces:
        q = _pad128(p.astype(F32))
        offs.append(pos)
        pos += q.shape[0] // 128
        flat.append(q)
    total = -(-pos // 8) * 8
    out = jnp.concatenate(flat + [jnp.zeros(((total - pos) * 128,), F32)]).reshape(total, 128)
    return out, offs


def _unpack(packed, offs, shapes):
    out = []
    for off, shp in zip(offs, shapes):
        n = int(np.prod(shp))
        rows = -(-n // 128)
        out.append(packed[off:off + rows].reshape(-1)[:n].reshape(shp))
    return out


def kernel(x, ffn1_norm, ffn1_w_gate, ffn1_w_up, ffn1_w_down, mix_norm, w_in, conv_w, conv_b, dt_bias, a_log, d_skip, ssd_norm, q_norm, k_norm, w_out, ffn2_norm, ffn2_w_gate, ffn2_w_up, ffn2_w_down, loss_target, m_ffn1_norm, m_ffn1_w_gate, m_ffn1_w_up, m_ffn1_w_down, m_mix_norm, m_w_in, m_conv_w, m_conv_b, m_dt_bias, m_a_log, m_d_skip, m_ssd_norm, m_q_norm, m_k_norm, m_w_out, m_ffn2_norm, m_ffn2_w_gate, m_ffn2_w_up, m_ffn2_w_down, v_ffn1_norm, v_ffn1_w_gate, v_ffn1_w_up, v_ffn1_w_down, v_mix_norm, v_w_in, v_conv_w, v_conv_b, v_dt_bias, v_a_log, v_d_skip, v_ssd_norm, v_q_norm, v_k_norm, v_w_out, v_ffn2_norm, v_ffn2_w_gate, v_ffn2_w_up, v_ffn2_w_down):
    A = dict(locals())
    ix, iy, ic = _place()
    me = 2 * ix + iy
    B, S, _ = x.shape
    T = B * S

    view = lambda a, key: jnp.swapaxes(a, 1, 2) if key in TRANSPOSED else a
    own = {key: view(A[name], key).astype(BF16) for name, key in BIG}
    own["cw"] = conv_w
    exs, first_norm = [], ffn1_norm
    for gi, (l, _, keys) in enumerate(GATHER_GROUPS):
        ex, first_norm = _exchange_start("gather_start%d" % gi, True, l, [own[key] for key in keys], first_norm)
        exs.append(ex)
    landed = {}

    def weights(l, group, after):
        gi = [i for i, (gl, gname, _) in enumerate(GATHER_GROUPS) if gl == l and gname in (group, "all")][0]
        if gi not in landed:
            lands = _exchange_wait("gather_wait%d" % gi, exs[gi], after)
            landed[gi] = {}
            for key, land in zip(GATHER_GROUPS[gi][2], lands):
                full = lax.dynamic_update_slice(land, own[key][l][None], (me, 0, 0))
                if key == "win":
                    full = _win_from_shards(full)
                if key == "cw":
                    full = jnp.transpose(full, (1, 0, 2)).reshape(CONV_K, CONV_DIM)
                landed[gi][key] = full
        return landed[gi]

    pending = []

    def scatter(l, group, grads, carry):
        keys = sorted(grads)
        arrs = [grads[key] for key in keys]
        if "win" in grads:
            arrs[keys.index("win")] = _win_to_shards(grads["win"])
        ex, carry = _exchange_start("scatter_start_l%d_%s" % (l, group), False, None, arrs, carry)
        pending.append((l, keys, ex))
        return carry

    small = {name: A[name] for name in SMALL}
    small["ffn1_norm"] = first_norm
    lsum, dx, sgrads = _local_step(x.reshape(T, D_MODEL), loss_target.reshape(T, D_MODEL), small, weights, scatter, B)

    names = SMALL + ["conv_w"]
    shapes = [A[n].shape for n in SMALL] + [(DEPTH, CONV_K, CONV_DIM), ()]
    pieces = [jnp.stack([sgrads[l][n].reshape(shp[1:]) for l in range(DEPTH)]) for n, shp in zip(names, shapes)]
    pieces.append(0.5 / D_MODEL * jnp.sum(lsum))
    packed, offs = _pack(pieces)

    sums, theirs, out = {}, {}, {}
    me1 = jnp.reshape(me, (1,)).astype(jnp.int32)

    def update(tag, after):
        todo = [k for k in sums if k not in theirs]
        theirs.update(zip(todo, _swap_sibling("swap_sibling_" + tag, [sums[k] for k in todo])))
        for name, key in BIG:
            if name not in out and all((key, l) in theirs for l in range(DEPTH)):
                res = _adamw_layers("adamw_" + key, view(A[name], key),
                                    [(sums[key, l], theirs[key, l]) for l in range(DEPTH)],
                                    view(A["m_" + name], key), view(A["v_" + name], key))
                out[name] = [view(r, key) for r in res]
                after = res[0]
        return after

    after = dx
    for idx, (l, keys, ex) in enumerate(pending):
        if idx == len(pending) - 1:
            after = update("a", after)
        lands = _exchange_wait("scatter_wait%d" % idx, ex, after)
        for key, g, got in zip(keys, ex["srcs"], lands):
            sums[key, l] = after = _sum4("sum_%s_l%d" % (key, l), me1, g, got)
    after = update("b", after)

    red = _unpack(_allreduce_small("allreduce_small", packed, after), offs, shapes)
    loss = red[-1]
    sg = dict(zip(names, red[:-1]))

    wp, offs = _pack([A[n] for n in SMALL])
    gp, _ = _pack([sg[n] for n in SMALL])
    mp, _ = _pack([A["m_" + n] for n in SMALL])
    vp, _ = _pack([A["v_" + n] for n in SMALL])
    res = _adamw("adamw_small", wp, [gp], mp, vp)
    shapes = [A[n].shape for n in SMALL]
    res = [_unpack(r, offs, shapes) for r in res]
    for i, n in enumerate(SMALL):
        out[n] = [res[q][i] for q in range(4)]
    gcw = lax.dynamic_slice_in_dim(sg["conv_w"], me * CONV_SH, CONV_SH, axis=2)
    flat = lambda a: a.reshape(DEPTH * CONV_K, CONV_SH)
    res = _adamw("adamw_conv_w", flat(conv_w), [flat(gcw)], flat(m_conv_w), flat(v_conv_w))
    out["conv_w"] = [r.reshape(conv_w.shape) for r in res]

    outs = [loss, dx.reshape(B, S, D_MODEL)]
    for q in range(4):
        outs += [out[n][q] for n in WEIGHTS]
    return tuple(outs)
```

```python
import functools
import math

import numpy as np
import jax
import jax.numpy as jnp
from jax import lax
from jax.experimental import pallas as pl
from jax.experimental.pallas import tpu as pltpu

F32 = jnp.float32
BF16 = jnp.bfloat16

D_MODEL = 1024
DEPTH = 2
N_SHARD = 4
D_FF = 2816
FF_SH = D_FF // N_SHARD
SSD_HEADS = 16
HEAD_DIM = 64
SSD_GROUPS = 4
GROUP_W = 256
SSD_STATE = 128
CONV_K = 4
CONV_DIM = 2048
ATT_HEADS = 16
MIX_W = 2048
MIX_SH = MIX_W // N_SHARD
IN_PROJ = 6160
IN_SH = IN_PROJ // N_SHARD
IN_PAD = 6272
PROJ_TN = 896
COL_Z, COL_XBC, COL_Q, COL_K, COL_V, COL_DT = 0, 1024, 3072, 4096, 5120, 6144
EPS = 1e-6
NEG = -1e30
SSD_L = 256
ATT_B = 512
ROW_T = 512
HALF_T = ROW_T // 2
GU_T = 1024
TK_W = 2048
CONV_CT = 256
CONV_R = 256
PAD_R = 8

ADAM_LR, ADAM_B1, ADAM_B2, ADAM_EPS, ADAM_WD, ADAM_STEP = 0.001, 0.9, 0.999, 1e-08, 0.01, 10

NN = (((1,), (0,)), ((), ()))
NT = (((1,), (1,)), ((), ()))
TN = (((0,), (0,)), ((), ()))

VMEM_LIMIT = 56 * 1024 * 1024


def _cp(*sem):
    return pltpu.CompilerParams(dimension_semantics=sem, vmem_limit_bytes=VMEM_LIMIT)


def _dot(a, b, dims):
    return lax.dot_general(a, b, dims, preferred_element_type=F32)


def _sigmoid(x):
    return 0.5 * jnp.tanh(0.5 * x) + 0.5


def _softplus(x):
    return jnp.maximum(x, 0.0) + jnp.log(1.0 + jnp.exp(-jnp.abs(x)))


def _mm(name, pairs, out_shape, out_spec, grid, dims, acc_shape, res=None, scale=1.0, post=None, post_in=()):
    nk = grid[2]
    npair = len(pairs)
    npost = len(post_in)

    def body(*refs):
        ab = refs[:2 * npair]
        pos = 2 * npair
        res_ref = None
        if res is not None:
            res_ref = refs[pos]
            pos += 1
        pin = refs[pos:pos + npost]
        pos += npost
        out_ref = refs[pos]
        pos += 1
        if post is not None:
            out2_ref = refs[pos]
            pos += 1
        s = None
        for p in range(npair):
            d = _dot(ab[2 * p][...].astype(BF16), ab[2 * p + 1][...].astype(BF16), dims)
            s = d if s is None else s + d

        def finish(r):
            if scale != 1.0:
                r = r * scale
            if res_ref is not None:
                r = r + res_ref[...]
            if post == "rmsb":
                @pl.when(pl.program_id(0) == 0)
                def _():
                    out2_ref[...] = jnp.zeros_like(out2_ref)

                xv = pin[0][...]
                rr = lax.rsqrt(jnp.mean(xv * xv, axis=-1, keepdims=True) + EPS)
                xhat = xv * rr
                dxhat = r * pin[1][...]
                out_ref[...] = pin[2][...] + rr * (dxhat - xhat * jnp.mean(dxhat * xhat, axis=-1, keepdims=True))
                out2_ref[...] += jnp.sum(r * xhat, axis=0, keepdims=True)
                return
            out_ref[...] = r.astype(out_ref.dtype)
            if post == "norm":
                rr = lax.rsqrt(jnp.mean(r * r, axis=-1, keepdims=True) + EPS)
                out2_ref[...] = (r * rr * pin[0][...]).astype(BF16)

        if nk == 1:
            finish(s)
            return
        acc = refs[pos]
        k = pl.program_id(2)

        @pl.when(k == 0)
        def _():
            acc[...] = s

        @pl.when(k > 0)
        def _():
            acc[...] += s

        @pl.when(k == nk - 1)
        def _():
            finish(acc[...])

    args, specs = [], []
    for a, a_spec, b, b_spec in pairs:
        args += [a, b]
        specs += [a_spec, b_spec]
    for arr, spec in ([res] if res is not None else []) + list(post_in):
        args.append(arr)
        specs.append(spec)
    sems = ("arbitrary",) * 3 if post == "rmsb" else ("parallel", "parallel", "arbitrary")
    return pl.pallas_call(
        body, out_shape=out_shape, grid=grid, in_specs=specs, out_specs=out_spec,
        scratch_shapes=[] if nk == 1 else [pltpu.VMEM(acc_shape, F32)], name=name,
        compiler_params=_cp(*sems))(*args)


def _rms_fwd(name, x, w):
    T = x.shape[0]

    def body(x_ref, w_ref, o_ref):
        xv = x_ref[...]
        r = lax.rsqrt(jnp.mean(xv * xv, axis=-1, keepdims=True) + EPS)
        o_ref[...] = (xv * r * w_ref[...]).astype(BF16)

    return pl.pallas_call(
        body, out_shape=jax.ShapeDtypeStruct((T, D_MODEL), BF16), grid=(T // ROW_T,),
        in_specs=[pl.BlockSpec((ROW_T, D_MODEL), lambda i: (i, 0)), pl.BlockSpec((1, D_MODEL), lambda i: (0, 0))],
        out_specs=pl.BlockSpec((ROW_T, D_MODEL), lambda i: (i, 0)), name=name, compiler_params=_cp("parallel"))(x, w)


def _loss_grad(name, y, t):
    T = y.shape[0]

    def body(y_ref, t_ref, dy_ref, l_ref):
        @pl.when(pl.program_id(0) == 0)
        def _():
            l_ref[...] = jnp.zeros_like(l_ref)

        e = y_ref[...] - t_ref[...]
        dy_ref[...] = e * (1.0 / D_MODEL)
        l_ref[...] += jnp.sum(e * e, axis=0, keepdims=True)

    row = pl.BlockSpec((ROW_T, D_MODEL), lambda i: (i, 0))
    vec = pl.BlockSpec((1, D_MODEL), lambda i: (0, 0))
    return pl.pallas_call(
        body, out_shape=(jax.ShapeDtypeStruct((T, D_MODEL), F32), jax.ShapeDtypeStruct((1, D_MODEL), F32)),
        grid=(T // ROW_T,), in_specs=[row, row], out_specs=(row, vec), name=name,
        compiler_params=_cp("arbitrary"))(y, t)


def _ffn_gate_up(name, h, wg, wu):
    T = h.shape[0]
    tm = min(GU_T, T)

    def body(h_ref, wg_ref, wu_ref, dgf_ref, duf_ref, a_ref):
        for r in range(0, tm, HALF_T):
            rows = slice(r, r + HALF_T)
            hv = h_ref[rows, :]
            g = _dot(hv, wg_ref[...], NT)
            u = _dot(hv, wu_ref[...], NT)
            sg = _sigmoid(g)
            silu = g * sg
            dgf_ref[rows, :] = (u * (sg * (1.0 + g * (1.0 - sg)))).astype(BF16)
            duf_ref[rows, :] = silu.astype(BF16)
            a_ref[rows, :] = (silu * u).astype(BF16)

    wspec = pl.BlockSpec((None, FF_SH, D_MODEL), lambda j, i: (j, 0, 0))
    ospec = pl.BlockSpec((None, tm, FF_SH), lambda j, i: (j, i, 0))
    osh = jax.ShapeDtypeStruct((N_SHARD, T, FF_SH), BF16)
    return pl.pallas_call(
        body, out_shape=(osh, osh, osh), grid=(N_SHARD, T // tm),
        in_specs=[pl.BlockSpec((tm, D_MODEL), lambda j, i: (i, 0)), wspec, wspec],
        out_specs=(ospec, ospec, ospec), name=name, compiler_params=_cp("parallel", "parallel"))(h, wg, wu)


def _ffn_dact(name, dx, wd, g, u):
    T = dx.shape[0]
    tm = min(GU_T, T)

    def body(dx_ref, wd_ref, g_ref, u_ref, dg_ref, du_ref):
        for r in range(0, tm, HALF_T):
            rows = slice(r, r + HALF_T)
            da = 0.5 * _dot(dx_ref[rows, :].astype(BF16), wd_ref[...], NT)
            dg_ref[rows, :] = (da * g_ref[rows, :].astype(F32)).astype(BF16)
            du_ref[rows, :] = (da * u_ref[rows, :].astype(F32)).astype(BF16)

    aspec = pl.BlockSpec((None, tm, FF_SH), lambda j, i: (j, i, 0))
    osh = jax.ShapeDtypeStruct((N_SHARD, T, FF_SH), BF16)
    return pl.pallas_call(
        body, out_shape=(osh, osh), grid=(N_SHARD, T // tm),
        in_specs=[pl.BlockSpec((tm, D_MODEL), lambda j, i: (i, 0)),
                  pl.BlockSpec((None, FF_SH, D_MODEL), lambda j, i: (j, 0, 0)), aspec, aspec],
        out_specs=(aspec, aspec), name=name, compiler_params=_cp("parallel", "parallel"))(dx, wd, g, u)


def _row3():
    return pl.BlockSpec((ROW_T, D_MODEL), lambda i, n, k: (i, 0))


def _vec3():
    return pl.BlockSpec((1, D_MODEL), lambda i, n, k: (0, 0))


def _with_norm(T, next_nw):
    if next_nw is None:
        return dict(out_shape=jax.ShapeDtypeStruct((T, D_MODEL), F32), out_spec=_row3())
    return dict(out_shape=(jax.ShapeDtypeStruct((T, D_MODEL), F32), jax.ShapeDtypeStruct((T, D_MODEL), BF16)),
                out_spec=(_row3(), _row3()), post="norm", post_in=[(next_nw, _vec3())])


def _ffn_fwd(tag, x, h, wg, wu, wd, next_nw):
    T = x.shape[0]
    g, u, a = _ffn_gate_up(tag + "_gu", h, wg, wu)
    if callable(wd):
        wd = wd(a)
    nt = T // ROW_T
    o = _with_norm(T, next_nw)
    xo = _mm(tag + "_down",
             [(a, pl.BlockSpec((None, ROW_T, FF_SH), lambda i, n, k, j=j: (j, i, 0)),
               wd, pl.BlockSpec((None, FF_SH, D_MODEL), lambda i, n, k, j=j: (j, 0, 0))) for j in range(N_SHARD)],
             o.pop("out_shape"), o.pop("out_spec"), (nt, 1, 1), NN, (ROW_T, D_MODEL),
             res=(x, _row3()), scale=0.5, **o)
    return xo, (x, h, g, u, a), wd


def _ffn_bwd(tag, dxo, saved, nw, wg, wu, wd, emit):
    x, h, g, u, a = saved
    T = x.shape[0]
    nt = T // ROW_T
    tkw = min(TK_W, T)
    nw_t = T // tkw
    dg, du = _ffn_dact(tag + "_dact", dxo, wd, g, u)
    actw = lambda f: pl.BlockSpec((None, tkw, FF_SH), f)
    gd = _mm(tag + "_dwd",
             [(a, actw(lambda m, n, k: (m, k, 0)), dxo, pl.BlockSpec((tkw, D_MODEL), lambda m, n, k: (k, 0)))],
             jax.ShapeDtypeStruct((N_SHARD, FF_SH, D_MODEL), BF16),
             pl.BlockSpec((None, FF_SH, D_MODEL), lambda m, n, k: (m, 0, 0)),
             (N_SHARD, 1, nw_t), TN, (FF_SH, D_MODEL), scale=0.5)
    hspec = pl.BlockSpec((tkw, D_MODEL), lambda j, n, k: (k, 0))
    gsh = jax.ShapeDtypeStruct((N_SHARD, FF_SH, D_MODEL), BF16)
    gspec = pl.BlockSpec((None, FF_SH, D_MODEL), lambda j, n, k: (j, 0, 0))
    gg = _mm(tag + "_dwg", [(dg, actw(lambda j, n, k: (j, k, 0)), h, hspec)], gsh, gspec,
             (N_SHARD, 1, nw_t), TN, (FF_SH, D_MODEL))
    gu = _mm(tag + "_dwu", [(du, actw(lambda j, n, k: (j, k, 0)), h, hspec)], gsh, gspec,
             (N_SHARD, 1, nw_t), TN, (FF_SH, D_MODEL))
    dg = emit(gg, gu, gd, dg)
    act = lambda j: pl.BlockSpec((None, ROW_T, FF_SH), lambda i, n, k: (j, i, 0))
    wsp = lambda j: pl.BlockSpec((None, FF_SH, D_MODEL), lambda i, n, k: (j, 0, 0))
    return _mm(tag + "_dh",
               [(dd, act(j), w, wsp(j)) for j in range(N_SHARD) for dd, w in ((dg, wg), (du, wu))],
               (jax.ShapeDtypeStruct((T, D_MODEL), F32), jax.ShapeDtypeStruct((1, D_MODEL), F32)), (_row3(), _vec3()),
               (nt, 1, 1), NN, (ROW_T, D_MODEL), post="rmsb", post_in=[(x, _row3()), (nw, _vec3()), (dxo, _row3())])


def _seq_rows(ref, start, size, S):
    lo, hi = max(start, 0), min(start + size, S)
    parts = [ref[pl.ds(lo, hi - lo), :]]
    if lo > start:
        parts.insert(0, jnp.zeros((lo - start, ref.shape[1]), F32))
    if start + size > hi:
        parts.append(jnp.zeros((start + size - hi, ref.shape[1]), F32))
    return parts[0] if len(parts) == 1 else jnp.concatenate(parts, axis=0)


XBC_CB = COL_XBC // CONV_CT


def _conv_fwd(name, proj, w, b, B):
    T = proj.shape[0]
    S = T // B
    C = CONV_DIM

    def body(x_ref, w_ref, b_ref, o_ref):
        wv = w_ref[...]
        for c in range(S // CONV_R):
            r0 = c * CONV_R
            ch = _seq_rows(x_ref, r0 - PAD_R, CONV_R + PAD_R, S)
            pre = ch[PAD_R:] * wv[3:4] + b_ref[...]
            for s in range(1, CONV_K):
                pre = pre + pltpu.roll(ch, s, axis=0)[PAD_R:] * wv[3 - s:4 - s]
            o_ref[pl.ds(r0, CONV_R), :] = pre * _sigmoid(pre)

    return pl.pallas_call(
        body, out_shape=jax.ShapeDtypeStruct((T, C), F32), grid=(B, C // CONV_CT),
        in_specs=[pl.BlockSpec((S, CONV_CT), lambda bi, ci: (bi, XBC_CB + ci)),
                  pl.BlockSpec((CONV_K, CONV_CT), lambda bi, ci: (0, ci)),
                  pl.BlockSpec((1, CONV_CT), lambda bi, ci: (0, ci))],
        out_specs=pl.BlockSpec((S, CONV_CT), lambda bi, ci: (bi, ci)), name=name,
        compiler_params=_cp("parallel", "parallel"))(proj, w, b)


def _conv_bwd(name, proj, dxs, dB, dC, w, b, dproj, B):
    T = proj.shape[0]
    S = T // B
    C = CONV_DIM
    RW = CONV_R + PAD_R
    nx, nb = dxs.shape[1] // CONV_CT, dB.shape[1] // CONV_CT

    def body(x_ref, dx_in, db_in, dc_in, w_ref, b_ref, buf_ref, dx_ref, dw_ref, db_ref):
        @pl.when(pl.program_id(1) == 0)
        def _():
            dw_ref[...] = jnp.zeros_like(dw_ref)
            db_ref[...] = jnp.zeros_like(db_ref)

        ci = pl.program_id(0)
        wv = w_ref[...]
        dw = [jnp.zeros((1, CONV_CT), F32) for _ in range(CONV_K)]
        db = jnp.zeros((1, CONV_CT), F32)
        for c in range(S // CONV_R):
            r0 = c * CONV_R
            ch = _seq_rows(x_ref, r0 - PAD_R, RW + PAD_R, S)
            xs = [ch[PAD_R:]] + [pltpu.roll(ch, s, axis=0)[PAD_R:] for s in range(1, CONV_K)]
            pre = b_ref[...] + xs[0] * wv[3:4]
            for s in range(1, CONV_K):
                pre = pre + xs[s] * wv[3 - s:4 - s]
            sg = _sigmoid(pre)
            dout = jnp.where(ci < nx, _seq_rows(dx_in, r0, RW, S),
                             jnp.where(ci < nx + nb, _seq_rows(db_in, r0, RW, S), _seq_rows(dc_in, r0, RW, S)))
            dpre = dout * (sg * (1.0 + pre * (1.0 - sg)))
            dx = dpre[:CONV_R] * wv[3:4]
            for s in range(1, CONV_K):
                dx = dx + pltpu.roll(dpre, RW - s, axis=0)[:CONV_R] * wv[3 - s:4 - s]
            dx_ref[pl.ds(r0, CONV_R), :] = dx.astype(BF16)
            dcur = dpre[:CONV_R]
            db = db + jnp.sum(dcur, axis=0, keepdims=True)
            for s in range(CONV_K):
                dw[3 - s] = dw[3 - s] + jnp.sum(dcur * xs[s][:CONV_R], axis=0, keepdims=True)
        db_ref[...] += db
        for k in range(CONV_K):
            dw_ref[k:k + 1, :] += dw[k]

    seq = lambda f: pl.BlockSpec((S, CONV_CT), f)
    return pl.pallas_call(
        body,
        out_shape=(jax.ShapeDtypeStruct(dproj.shape, dproj.dtype), jax.ShapeDtypeStruct((CONV_K, C), F32),
                   jax.ShapeDtypeStruct((1, C), F32)),
        grid=(C // CONV_CT, B),
        in_specs=[seq(lambda ci, bi: (bi, XBC_CB + ci)),
                  seq(lambda ci, bi: (bi, jnp.minimum(ci, nx - 1))),
                  seq(lambda ci, bi: (bi, jnp.clip(ci - nx, 0, nb - 1))),
                  seq(lambda ci, bi: (bi, jnp.clip(ci - nx - nb, 0, nb - 1))),
                  pl.BlockSpec((CONV_K, CONV_CT), lambda ci, bi: (0, ci)),
                  pl.BlockSpec((1, CONV_CT), lambda ci, bi: (0, ci)), ANY],
        out_specs=(seq(lambda ci, bi: (bi, XBC_CB + ci)),
                   pl.BlockSpec((CONV_K, CONV_CT), lambda ci, bi: (0, ci)),
                   pl.BlockSpec((1, CONV_CT), lambda ci, bi: (0, ci))),
        input_output_aliases={6: 0},
        name=name, compiler_params=_cp("parallel", "arbitrary"))(proj, dxs, dB, dC, w, b, dproj)


def _tri_sum(tri, x, dims, tri_first, terms=3):
    out, rest = None, x
    for t in range(terms):
        part = rest.astype(BF16)
        if t + 1 < terms:
            rest = rest - part.astype(F32)
        d = _dot(tri, part, dims) if tri_first else _dot(part, tri, dims)
        out = d if out is None else out + d
    return out


def _total(x):
    return jnp.sum(jnp.sum(x, axis=0, keepdims=True), axis=-1, keepdims=True)


def _ssd_common(dtc_ref, dtr_ref, pcol_ref, prow_ref, b_ref, c_ref):
    L = SSD_L
    bias_c, alog_c = pcol_ref[0:1, :], pcol_ref[1:2, :]
    a_c = -jnp.exp(alog_c)
    dt_c = _softplus(dtc_ref[...] + bias_c)
    row = lax.broadcasted_iota(jnp.int32, (L, L), 0)
    col = lax.broadcasted_iota(jnp.int32, (L, L), 1)
    causal = row >= col
    tri = causal.astype(BF16)
    cum_c = _tri_sum(tri, dt_c * a_c, NN, True)
    a_r = -jnp.exp(prow_ref[:, 1:2])
    dt_r = _softplus(dtr_ref[...] + prow_ref[:, 0:1])
    cum_r = _tri_sum(tri, dt_r * a_r, NT, False)
    bb = b_ref[...].astype(BF16)
    cb = c_ref[...].astype(BF16)
    G = _dot(cb, bb, NT)
    return a_c, dt_c, causal, tri, cum_c, cum_r, bb, cb, G


def _ssd_fwd(name, xc, proj, dtc, dtr, pcol, prow, nw, B):
    T = xc.shape[0]
    S = T // B
    nb = S // SSD_L
    L = SSD_L

    def body(xs_ref, b_ref, c_ref, z_ref, dtc_ref, dtr_ref, pcol_ref, prow_ref, nw_ref, y_ref, yn_ref, hs_ref, H, yo_s):
        @pl.when(pl.program_id(2) == 0)
        def _():
            H[...] = jnp.zeros_like(H)

        a_c, dt_c, causal, tri, cum_c, cum_r, bb, cb, G = _ssd_common(dtc_ref, dtr_ref, pcol_ref, prow_ref, b_ref, c_ref)
        dsk = pcol_ref[2:3, :]
        clast = cum_c[L - 1:L, :]
        bf = b_ref[...]
        for h in range(4):
            hs_ref[h] = H[h]
            yo_s[h] = _dot(cb, H[h].astype(BF16), NN)
        for h in range(4):
            sl = slice(HEAD_DIM * h, HEAD_DIM * (h + 1))
            cc = cum_c[:, h:h + 1]
            lm = jnp.exp(jnp.where(causal, cc - cum_r[h:h + 1, :], NEG))
            M = (G * lm).astype(BF16)
            xh = xs_ref[:, sl]
            Xb = (xh * dt_c[:, h:h + 1]).astype(BF16)
            Hh = H[h]
            y = _dot(M, Xb, NN) + jnp.exp(cc) * yo_s[h]
            y_ref[:, sl] = y + dsk[:, h:h + 1] * xh
            cl = clast[:, h:h + 1]
            Bw = (bf * jnp.exp(cl - cc)).astype(BF16)
            H[h] = jnp.exp(cl) * Hh + _dot(Bw, Xb, TN)
        zv = z_ref[...]
        y2 = y_ref[...] * (zv * _sigmoid(zv))
        r = lax.rsqrt(jnp.mean(y2 * y2, axis=-1, keepdims=True) + EPS)
        yn_ref[...] = (y2 * r * nw_ref[...]).astype(BF16)

    rowi = lambda b, g, i: b * nb + i
    grp = pl.BlockSpec((L, GROUP_W), lambda b, g, i: (rowi(b, g, i), g))
    return pl.pallas_call(
        body,
        out_shape=(jax.ShapeDtypeStruct((T, 1024), F32), jax.ShapeDtypeStruct((T, 1024), BF16),
                   jax.ShapeDtypeStruct((B, SSD_GROUPS, nb, 4, SSD_STATE, HEAD_DIM), F32)),
        grid=(B, SSD_GROUPS, nb),
        in_specs=[grp,
                  pl.BlockSpec((L, SSD_STATE), lambda b, g, i: (rowi(b, g, i), 8 + g)),
                  pl.BlockSpec((L, SSD_STATE), lambda b, g, i: (rowi(b, g, i), 12 + g)),
                  grp,
                  pl.BlockSpec((None, L, 4), lambda b, g, i: (g, rowi(b, g, i), 0)),
                  pl.BlockSpec((None, 4, L), lambda b, g, i: (g, 0, rowi(b, g, i))),
                  pl.BlockSpec((None, 3, 4), lambda b, g, i: (g, 0, 0)),
                  pl.BlockSpec((None, 4, 3), lambda b, g, i: (g, 0, 0)),
                  pl.BlockSpec((1, GROUP_W), lambda b, g, i: (0, g))],
        out_specs=(grp, grp,
                   pl.BlockSpec((None, None, None, 4, SSD_STATE, HEAD_DIM), lambda b, g, i: (b, g, i, 0, 0, 0))),
        scratch_shapes=[pltpu.VMEM((4, SSD_STATE, HEAD_DIM), F32), pltpu.VMEM((4, L, HEAD_DIM), F32)], name=name,
        compiler_params=_cp("parallel", "parallel", "arbitrary"))(xc, xc, xc, proj, dtc, dtr, pcol, prow, nw)


def _ssd_bwd(name, dyn, Y, xc, proj, dtc, dtr, pcol, prow, nw, hs, dproj, B):
    T = xc.shape[0]
    S = T // B
    nb = S // SSD_L
    L = SSD_L

    def body(dyn_ref, y_ref, xs_ref, b_ref, c_ref, z_ref, dtc_ref, dtr_ref, pcol_ref, prow_ref, nw_ref, hs_ref, buf_ref,
             dxs_ref, db_ref, dc_ref, dz_ref, ddt_ref, dpar_ref, dnw_ref, dH, dm_s, dxo_s, ea_s, ex_s):
        @pl.when(pl.program_id(2) == 0)
        def _():
            dH[...] = jnp.zeros_like(dH)
            dpar_ref[...] = jnp.zeros_like(dpar_ref)
            dnw_ref[...] = jnp.zeros_like(dnw_ref)

        a_c, dt_c, causal, tri, cum_c, cum_r, bb, cb, G = _ssd_common(dtc_ref, dtr_ref, pcol_ref, prow_ref, b_ref, c_ref)
        dsk = pcol_ref[2:3, :]
        clast = cum_c[L - 1:L, :]
        bf = b_ref[...]
        cf = c_ref[...]
        Yv = y_ref[...]
        zv = z_ref[...]
        sz = _sigmoid(zv)
        silu = zv * sz
        y2 = Yv * silu
        r = lax.rsqrt(jnp.mean(y2 * y2, axis=-1, keepdims=True) + EPS)
        yhat = y2 * r
        dyv = dyn_ref[...]
        dnw_ref[...] += jnp.sum(dyv * yhat, axis=0, keepdims=True)
        dyhat = dyv * nw_ref[...]
        dy2 = r * (dyhat - yhat * jnp.mean(dyhat * yhat, axis=-1, keepdims=True))
        dY = dy2 * silu
        dz_ref[...] = (dy2 * Yv * (sz * (1.0 + zv * (1.0 - sz)))).astype(BF16)

        lane4 = lax.broadcasted_iota(jnp.int32, (1, 4), 1)
        dG = jnp.zeros((L, L), F32)
        dBs = jnp.zeros((L, SSD_STATE), F32)
        dCs = jnp.zeros((L, SSD_STATE), F32)
        ddsk = jnp.zeros((1, 4), F32)
        dcl = jnp.zeros((1, 4), F32)
        for h in range(4):
            sl = slice(HEAD_DIM * h, HEAD_DIM * (h + 1))
            xb = (xs_ref[:, sl] * dt_c[:, h:h + 1]).astype(BF16)
            dm_s[h] = _dot(dY[:, sl].astype(BF16), xb, NT)
            dxo_s[h] = _dot(bb, dH[h].astype(BF16), NN)
        for h in range(4):
            sl = slice(HEAD_DIM * h, HEAD_DIM * (h + 1))
            onehot = (lane4 == h).astype(F32)
            cc = cum_c[:, h:h + 1]
            cl = clast[:, h:h + 1]
            lm = jnp.exp(jnp.where(causal, cc - cum_r[h:h + 1, :], NEG))
            M = (G * lm).astype(BF16)
            xh = xs_ref[:, sl]
            dth = dt_c[:, h:h + 1]
            X = xh * dth
            Xb = X.astype(BF16)
            dYh = dY[:, sl]
            dYb = dYh.astype(BF16)
            Hb = hs_ref[h].astype(BF16)
            dHh = dH[h]
            dHb = dHh.astype(BF16)
            alpha = jnp.exp(cc)
            beta = jnp.exp(cl - cc)
            dXoff = beta * dxo_s[h]
            dX = _dot(M, dYb, TN) + dXoff
            dG = dG + dm_s[h] * lm
            dCs = dCs + _dot((alpha * dYh).astype(BF16), Hb, NT)
            dBs = dBs + _dot((beta * X).astype(BF16), dHb, NT)
            ypre = Yv[:, sl] - dsk[:, h:h + 1] * xh
            ea_s[:, sl] = dYb.astype(F32) * ypre - Xb.astype(F32) * dX
            ex_s[:, sl] = dX * xh
            dcl_h = (_total(dHh * (jnp.exp(cl) * hs_ref[h])) + _total(Xb.astype(F32) * dXoff))
            dcl = dcl + dcl_h * onehot
            ddsk = ddsk + _total(dYh * xh) * onehot
            dxs_ref[:, sl] = dsk[:, h:h + 1] * dYh + dX * dth
            dH[h] = jnp.exp(cl) * dHh + _dot((alpha * cf).astype(BF16), dYb, TN)
        dGb = dG.astype(BF16)
        dc_ref[...] = _dot(dGb, bb, NN) + dCs
        db_ref[...] = _dot(dGb, cb, TN) + dBs
        feat = lax.broadcasted_iota(jnp.int32, (GROUP_W, 4), 0)
        head = lax.broadcasted_iota(jnp.int32, (GROUP_W, 4), 1) * HEAD_DIM
        sel = ((feat >= head) & (feat < head + HEAD_DIM)).astype(BF16)
        dA = _tri_sum(sel, ea_s[...], NN, False)
        ddtx = _tri_sum(sel, ex_s[...], NN, False)
        last = lax.broadcasted_iota(jnp.int32, (L, 1), 0) == L - 1
        dA = dA + jnp.where(last, dcl, 0.0)
        dadt = _tri_sum(tri, dA, TN, True)
        ddt = dadt * a_c + ddtx
        d_a = jnp.sum(dadt * dt_c, axis=0, keepdims=True)
        ddraw = ddt * _sigmoid(dtc_ref[...] + pcol_ref[0:1, :])
        ddt_ref[...] = ddraw
        dpar_ref[0:1, :] += jnp.sum(ddraw, axis=0, keepdims=True)
        dpar_ref[1:2, :] += d_a * a_c
        dpar_ref[2:3, :] += ddsk

    rowi = lambda b, g, i: b * nb + (nb - 1 - i)
    grp = pl.BlockSpec((L, GROUP_W), lambda b, g, i: (rowi(b, g, i), g))
    st = pl.BlockSpec((L, SSD_STATE), lambda b, g, i: (rowi(b, g, i), g))
    f = jax.ShapeDtypeStruct
    return pl.pallas_call(
        body,
        out_shape=(f((T, 1024), F32), f((T, 512), F32), f((T, 512), F32), f(dproj.shape, dproj.dtype),
                   f((SSD_GROUPS, T, 4), F32), f((B, SSD_GROUPS, 3, 4), F32), f((B, 1, 1024), F32)),
        grid=(B, SSD_GROUPS, nb),
        in_specs=[grp, grp, grp,
                  pl.BlockSpec((L, SSD_STATE), lambda b, g, i: (rowi(b, g, i), 8 + g)),
                  pl.BlockSpec((L, SSD_STATE), lambda b, g, i: (rowi(b, g, i), 12 + g)),
                  grp,
                  pl.BlockSpec((None, L, 4), lambda b, g, i: (g, rowi(b, g, i), 0)),
                  pl.BlockSpec((None, 4, L), lambda b, g, i: (g, 0, rowi(b, g, i))),
                  pl.BlockSpec((None, 3, 4), lambda b, g, i: (g, 0, 0)),
                  pl.BlockSpec((None, 4, 3), lambda b, g, i: (g, 0, 0)),
                  pl.BlockSpec((1, GROUP_W), lambda b, g, i: (0, g)),
                  pl.BlockSpec((None, None, None, 4, SSD_STATE, HEAD_DIM), lambda b, g, i: (b, g, nb - 1 - i, 0, 0, 0)),
                  ANY],
        out_specs=(grp, st, st, grp,
                   pl.BlockSpec((None, L, 4), lambda b, g, i: (g, rowi(b, g, i), 0)),
                   pl.BlockSpec((None, None, 3, 4), lambda b, g, i: (b, g, 0, 0)),
                   pl.BlockSpec((None, 1, GROUP_W), lambda b, g, i: (b, 0, g))),
        input_output_aliases={12: 3},
        scratch_shapes=[pltpu.VMEM((4, SSD_STATE, HEAD_DIM), F32), pltpu.VMEM((4, L, L), F32),
                        pltpu.VMEM((4, L, HEAD_DIM), F32), pltpu.VMEM((L, GROUP_W), F32),
                        pltpu.VMEM((L, GROUP_W), F32)], name=name,
        compiler_params=_cp("parallel", "parallel", "arbitrary"))(
            dyn, Y, xc, xc, xc, proj, dtc, dtr, pcol, prow, nw, hs, dproj)


def _head_sel():
    sel = (np.arange(1024)[:, None] // HEAD_DIM == np.arange(ATT_HEADS)[None, :]).astype(np.float32)
    return jnp.asarray(sel, BF16), jnp.asarray(sel.T, BF16)


def _head_rms(xv, sel, selT):
    ms = _tri_sum(sel, xv * xv, NN, False, 1) * (1.0 / HEAD_DIM)
    return _tri_sum(selT, lax.rsqrt(ms + EPS), NN, False, 2)


def _headnorm_fwd(name, proj, col_block, w):
    T = proj.shape[0]
    sel, selT = _head_sel()

    def body(x_ref, w_ref, sel_ref, selT_ref, o_ref):
        xv = x_ref[...]
        o_ref[...] = (xv * _head_rms(xv, sel_ref[...], selT_ref[...]) * w_ref[...]).astype(BF16)

    full = lambda shp: pl.BlockSpec(shp, lambda i: (0, 0))
    return pl.pallas_call(
        body, out_shape=jax.ShapeDtypeStruct((T, 1024), BF16), grid=(T // ROW_T,),
        in_specs=[pl.BlockSpec((ROW_T, 1024), lambda i: (i, col_block)), full((1, 1024)), full((1024, ATT_HEADS)),
                  full((ATT_HEADS, 1024))],
        out_specs=pl.BlockSpec((ROW_T, 1024), lambda i: (i, 0)), name=name, compiler_params=_cp("parallel"))(
            proj, jnp.tile(w, (1, ATT_HEADS)), sel, selT)


def _headnorm_bwd(name, dn, proj, col_block, w, dproj):
    T = proj.shape[0]
    sel, selT = _head_sel()

    def body(dn_ref, x_ref, w_ref, sel_ref, selT_ref, buf_ref, dx_ref, dw_ref):
        @pl.when(pl.program_id(0) == 0)
        def _():
            dw_ref[...] = jnp.zeros_like(dw_ref)

        xv = x_ref[...]
        sl, slT = sel_ref[...], selT_ref[...]
        rb = _head_rms(xv, sl, slT)
        xhat = xv * rb
        dnv = dn_ref[...]
        dxhat = dnv * w_ref[...]
        mean = _tri_sum(slT, _tri_sum(sl, dxhat * xhat, NN, False, 2) * (1.0 / HEAD_DIM), NN, False, 2)
        dx_ref[...] = (rb * (dxhat - xhat * mean)).astype(BF16)
        dw_ref[...] += jnp.sum(dnv * xhat, axis=0, keepdims=True)

    here = pl.BlockSpec((ROW_T, 1024), lambda i: (i, col_block))
    full = lambda shp: pl.BlockSpec(shp, lambda i: (0, 0))
    dx, dw = pl.pallas_call(
        body, out_shape=(jax.ShapeDtypeStruct(dproj.shape, dproj.dtype), jax.ShapeDtypeStruct((1, 1024), F32)),
        grid=(T // ROW_T,),
        in_specs=[pl.BlockSpec((ROW_T, 1024), lambda i: (i, 0)), here, full((1, 1024)), full((1024, ATT_HEADS)),
                  full((ATT_HEADS, 1024)), ANY],
        out_specs=(here, full((1, 1024))), input_output_aliases={5: 0},
        name=name, compiler_params=_cp("arbitrary"))(dn, proj, jnp.tile(w, (1, ATT_HEADS)), sel, selT, dproj)
    return dx, jnp.sum(dw.reshape(ATT_HEADS, HEAD_DIM), axis=0, keepdims=True)


def _att_bias(nq):
    j = np.arange(ATT_B)[:, None]
    i = np.arange(ATT_B)[None, :]
    out = np.empty((nq, ATT_B, ATT_B), np.float32)
    for dblk in range(nq):
        dl = ATT_B * dblk + i - j
        cnt = ((dl >= 0) & (dl <= 128)).astype(np.float32)
        cnt += ((dl >= 0) & (dl % 4 == 0) & (dl <= 512))
        cnt += ((dl >= 0) & (dl % 16 == 0) & (dl <= 2048))
        out[dblk] = np.where(cnt > 0, np.log(np.maximum(cnt, 1.0)), NEG)
    return jnp.asarray(out)


def _row_pair(nq):
    def f(r, c):
        first = c <= r
        return jnp.where(first, r, nq - 1 - r), jnp.where(first, c, c - (r + 1))
    return f


def _col_pair(nq):
    def f(r, c):
        first = c < nq - r
        kj = jnp.where(first, r, nq - 1 - r)
        return jnp.where(first, r + c, nq - 1 - r + (c - (nq - r))), kj
    return f


ATT_SCALE = 1.0 / math.sqrt(HEAD_DIM)
ATT_HS = 8
ATT_W = ATT_HS * HEAD_DIM


def _att_maps(nq, qk):
    return dict(
        q_tok=lambda b, g, r, c: (b * nq + qk(r, c)[0], g),
        k_tok=lambda b, g, r, c: (b * nq + qk(r, c)[1], g),
        v_tok=lambda b, g, r, c: (b * nq + qk(r, c)[1], COL_V // ATT_W + g),
        q_feat=lambda b, g, r, c: (g, b * nq + qk(r, c)[0]),
        k_feat=lambda b, g, r, c: (g, b * nq + qk(r, c)[1]),
        bias=lambda b, g, r, c: (qk(r, c)[0] - qk(r, c)[1], 0, 0),
        lse=lambda b, g, r, c: (g, 0, b * nq + qk(r, c)[0]),
        do_tok=lambda b, g, r, c: (b * nq + qk(r, c)[0], 1024 // ATT_W + g))


def _att_fwd(name, kn, qT, vT, bias, B):
    T = kn.shape[0]
    nq = (T // B) // ATT_B
    qk = _row_pair(nq)
    mp = _att_maps(nq, qk)

    def body(k_ref, qT_ref, vT_ref, bias_ref, oT_ref, lse_ref, m_s, l_s, acc_s, s_s):
        qi, kj = qk(pl.program_id(2), pl.program_id(3))

        @pl.when(kj == 0)
        def _():
            m_s[...] = jnp.full_like(m_s, NEG)
            l_s[...] = jnp.zeros_like(l_s)
            acc_s[...] = jnp.zeros_like(acc_s)

        bv = bias_ref[...]
        for h in range(ATT_HS):
            rs = slice(HEAD_DIM * h, HEAD_DIM * (h + 1))
            s_s[h] = _dot(k_ref[:, rs], qT_ref[rs, :], NN)
        for h in range(ATT_HS):
            rs = slice(HEAD_DIM * h, HEAD_DIM * (h + 1))
            s = s_s[h] + bv
            m_prev = m_s[h:h + 1, :]
            m_new = jnp.maximum(m_prev, jnp.max(s, axis=0, keepdims=True))
            alpha = jnp.exp(m_prev - m_new)
            p = jnp.exp(s - m_new)
            l_s[h:h + 1, :] = alpha * l_s[h:h + 1, :] + jnp.sum(p, axis=0, keepdims=True)
            acc_s[rs, :] = alpha * acc_s[rs, :] + _dot(vT_ref[rs, :], p.astype(BF16), NN)
            m_s[h:h + 1, :] = m_new

        @pl.when(kj == qi)
        def _():
            for h in range(ATT_HS):
                rs = slice(HEAD_DIM * h, HEAD_DIM * (h + 1))
                oT_ref[rs, :] = (acc_s[rs, :] / l_s[h:h + 1, :]).astype(BF16)
            lse_ref[...] = m_s[...] + jnp.log(l_s[...])

    tok = (ATT_B, ATT_W)
    feat = (ATT_W, ATT_B)
    return pl.pallas_call(
        body,
        out_shape=(jax.ShapeDtypeStruct((1024, T), BF16), jax.ShapeDtypeStruct((ATT_HEADS // ATT_HS, ATT_HS, T), F32)),
        grid=(B, ATT_HEADS // ATT_HS, nq // 2, nq + 1),
        in_specs=[pl.BlockSpec(tok, mp["k_tok"]), pl.BlockSpec(feat, mp["q_feat"]), pl.BlockSpec(feat, mp["k_feat"]),
                  pl.BlockSpec((None, ATT_B, ATT_B), mp["bias"])],
        out_specs=(pl.BlockSpec(feat, mp["q_feat"]), pl.BlockSpec((None, ATT_HS, ATT_B), mp["lse"])),
        scratch_shapes=[pltpu.VMEM((ATT_HS, ATT_B), F32), pltpu.VMEM((ATT_HS, ATT_B), F32),
                        pltpu.VMEM((ATT_W, ATT_B), F32), pltpu.VMEM((ATT_HS, ATT_B, ATT_B), F32)],
        name=name, compiler_params=_cp("parallel", "parallel", "arbitrary", "arbitrary"))(kn, qT, vT, bias)


def _att_scores(k_ref, qT_ref, v_ref, doT_ref, s_s, dp_s):
    for h in range(ATT_HS):
        rs = slice(HEAD_DIM * h, HEAD_DIM * (h + 1))
        s_s[h] = _dot(k_ref[:, rs], qT_ref[rs, :], NN)
        dp_s[h] = _dot(v_ref[:, rs].astype(BF16), doT_ref[rs, :].astype(BF16), NN)


def _att_p_ds(s_s, dp_s, doT_ref, oT_ref, lse_ref, bv, h):
    rs = slice(HEAD_DIM * h, HEAD_DIM * (h + 1))
    delta = jnp.sum(doT_ref[rs, :] * oT_ref[rs, :].astype(F32), axis=0, keepdims=True)
    p = jnp.exp(s_s[h] + bv - lse_ref[h:h + 1, :])
    return p, p * (dp_s[h] - delta)


def _att_bwd(name, kn, qT, proj, qn, knT, bias, doT, oT, lse, dyn, dproj, B):
    T = kn.shape[0]
    S = T // B
    nq = S // ATT_B
    qk = _col_pair(nq)
    mp = _att_maps(nq, qk)

    def body(k_ref, qT_ref, v_ref, q_ref, kT_ref, bias_ref, doT_ref, oT_ref, lse_ref, do_ref, buf_ref,
             dqT_ref, dk_ref, dv_ref, dk_s, dv_s, dq_s, s_s, dp_s):
        r, c = pl.program_id(2), pl.program_id(3)
        qi, kj = qk(r, c)

        @pl.when((r == 0) & (c == 0))
        def _():
            dq_s[...] = jnp.zeros_like(dq_s)

        @pl.when(qi == kj)
        def _():
            dk_s[...] = jnp.zeros_like(dk_s)
            dv_s[...] = jnp.zeros_like(dv_s)

        bv = bias_ref[...]
        _att_scores(k_ref, qT_ref, v_ref, doT_ref, s_s, dp_s)
        dq_blk = dq_s.at[qi]
        for h in range(ATT_HS):
            rs = slice(HEAD_DIM * h, HEAD_DIM * (h + 1))
            p, ds = _att_p_ds(s_s, dp_s, doT_ref, oT_ref, lse_ref, bv, h)
            dsb = ds.astype(BF16)
            dv_s[h] += _dot(p.astype(BF16), do_ref[:, rs].astype(BF16), NN)
            dk_s[h] += _dot(dsb, q_ref[:, rs], NN)
            dq_blk[rs, :] += _dot(kT_ref[rs, :], dsb, NN)

        @pl.when(qi == nq - 1)
        def _():
            for h in range(ATT_HS):
                rs = slice(HEAD_DIM * h, HEAD_DIM * (h + 1))
                dk_ref[:, rs] = dk_s[h] * ATT_SCALE
                dv_ref[:, rs] = dv_s[h].astype(BF16)

        @pl.when((r == nq // 2 - 1) & (c == nq))
        def _():
            for q in range(nq):
                dqT_ref[:, ATT_B * q:ATT_B * (q + 1)] = dq_s[q] * ATT_SCALE

    tok = (ATT_B, ATT_W)
    feat = (ATT_W, ATT_B)
    v_cb = COL_V // ATT_W
    return pl.pallas_call(
        body,
        out_shape=(jax.ShapeDtypeStruct((1024, T), F32), jax.ShapeDtypeStruct((T, 1024), F32),
                   jax.ShapeDtypeStruct(dproj.shape, dproj.dtype)),
        grid=(B, ATT_HEADS // ATT_HS, nq // 2, nq + 1),
        in_specs=[pl.BlockSpec(tok, mp["k_tok"]), pl.BlockSpec(feat, mp["q_feat"]), pl.BlockSpec(tok, mp["v_tok"]),
                  pl.BlockSpec(tok, mp["q_tok"]), pl.BlockSpec(feat, mp["k_feat"]),
                  pl.BlockSpec((None, ATT_B, ATT_B), mp["bias"]),
                  pl.BlockSpec(feat, mp["q_feat"]), pl.BlockSpec(feat, mp["q_feat"]),
                  pl.BlockSpec((None, ATT_HS, ATT_B), mp["lse"]), pl.BlockSpec(tok, mp["do_tok"]), ANY],
        out_specs=(pl.BlockSpec((ATT_W, S), lambda b, g, r, c: (g, b)),
                   pl.BlockSpec(tok, mp["k_tok"]),
                   pl.BlockSpec(tok, lambda b, g, r, c: (b * nq + qk(r, c)[1], v_cb + g))),
        input_output_aliases={10: 2},
        scratch_shapes=[pltpu.VMEM((ATT_HS, ATT_B, HEAD_DIM), F32), pltpu.VMEM((ATT_HS, ATT_B, HEAD_DIM), F32),
                        pltpu.VMEM((nq, ATT_W, ATT_B), F32),
                        pltpu.VMEM((ATT_HS, ATT_B, ATT_B), F32), pltpu.VMEM((ATT_HS, ATT_B, ATT_B), F32)],
        name=name, compiler_params=_cp("parallel", "parallel", "arbitrary", "arbitrary"))(
            kn, qT, proj, qn, knT, bias, doT, oT, lse, dyn, dproj)


def _group_cols(v):
    return v.reshape(SSD_GROUPS, 4)


def _ssd_params(p):
    rows = jnp.stack([_group_cols(p["dt_bias"]), _group_cols(p["a_log"]), _group_cols(p["d_skip"])], axis=1)
    return rows, jnp.swapaxes(rows, 1, 2)


def _dymix(name, dx, wout):
    T = dx.shape[0]

    def body(dx_ref, w_ref, o_ref):
        dxb = dx_ref[...].astype(BF16)
        for n in range(N_SHARD):
            o_ref[:, MIX_SH * n:MIX_SH * (n + 1)] = _dot(dxb, w_ref[n], NT)

    return pl.pallas_call(
        body, out_shape=jax.ShapeDtypeStruct((T, MIX_W), F32), grid=(T // ROW_T,),
        in_specs=[pl.BlockSpec((ROW_T, D_MODEL), lambda i: (i, 0)),
                  pl.BlockSpec((N_SHARD, MIX_SH, D_MODEL), lambda i: (0, 0, 0))],
        out_specs=pl.BlockSpec((ROW_T, MIX_W), lambda i: (i, 0)), name=name, compiler_params=_cp("parallel"))(dx, wout)


def _mixer_fwd(tag, x1, h2, p, weights, bias, B):
    T = x1.shape[0]
    S = T // B
    nt = T // ROW_T
    wi = weights("win", h2)
    win, cw = wi["win"], wi["cw"]
    tm = min(GU_T, T)
    proj = _mm(tag + "_proj",
               [(h2, pl.BlockSpec((tm, D_MODEL), lambda j, i, k: (i, 0)),
                 win, pl.BlockSpec((D_MODEL, PROJ_TN), lambda j, i, k: (0, j)))],
               jax.ShapeDtypeStruct((T, IN_PAD), F32), pl.BlockSpec((tm, PROJ_TN), lambda j, i, k: (i, j)),
               (IN_PAD // PROJ_TN, T // tm, 1), NN, (tm, PROJ_TN))
    xc = _conv_fwd(tag + "_conv", proj, cw, p["conv_b"][None], B)
    dtraw = proj[:, COL_DT:COL_DT + SSD_HEADS].reshape(T, SSD_GROUPS, 4)
    dtc = jnp.transpose(dtraw, (1, 0, 2))
    dtr = jnp.transpose(dtraw, (1, 2, 0))
    pcol, prow = _ssd_params(p)
    Y, y_ssd, hs = _ssd_fwd(tag + "_ssd", xc, proj, dtc, dtr, pcol, prow, p["ssd_norm"][None], B)
    qn = _headnorm_fwd(tag + "_qn", proj, COL_Q // 1024, p["q_norm"][None])
    kn = _headnorm_fwd(tag + "_kn", proj, COL_K // 1024, p["k_norm"][None])
    qT = (qn * ATT_SCALE).T
    oT, lse = _att_fwd(tag + "_att", kn, qT, proj[:, COL_V:COL_V + 1024].T.astype(BF16), bias, B)
    ymix = jnp.concatenate([y_ssd, oT.T], axis=1)
    rest = weights("rest", ymix)
    o = _with_norm(T, p["ffn2_norm"][None])
    x2, h3 = _mm(tag + "_out",
                 [(ymix, pl.BlockSpec((ROW_T, MIX_SH), lambda i, n, k, j=j: (i, j)),
                   rest["wout"], pl.BlockSpec((None, MIX_SH, D_MODEL), lambda i, n, k, j=j: (j, 0, 0)))
                  for j in range(N_SHARD)],
                 o.pop("out_shape"), o.pop("out_spec"), (nt, 1, 1), NN, (ROW_T, D_MODEL), res=(x1, _row3()), **o)
    saved = dict(x1=x1, h2=h2, proj=proj, xc=xc, dtc=dtc, dtr=dtr, Y=Y, hs=hs,
                 qn=qn, kn=kn, qT=qT, oT=oT, lse=lse, ymix=ymix, win=win, cw=cw, wout=rest["wout"])
    return x2, h3, saved


def _mixer_bwd(tag, dx2, sv, p, bias, B):
    T = dx2.shape[0]
    S = T // B
    nt = T // ROW_T
    sg = {}
    dymix = _dymix(tag + "_dymix", dx2, sv["wout"])
    tkw = min(TK_W, T)
    gwout = _mm(tag + "_dwout",
                [(sv["ymix"], pl.BlockSpec((tkw, MIX_SH), lambda m, n, k: (k, m)),
                  dx2, pl.BlockSpec((tkw, D_MODEL), lambda m, n, k: (k, 0)))],
                jax.ShapeDtypeStruct((N_SHARD, MIX_SH, D_MODEL), BF16),
                pl.BlockSpec((None, MIX_SH, D_MODEL), lambda m, n, k: (m, 0, 0)),
                (N_SHARD, 1, T // tkw), TN, (MIX_SH, D_MODEL))
    proj = sv["proj"]
    doT = dymix[:, 1024:].T
    dproj = lax.empty((T, IN_PAD), BF16)
    dqT, dkn, dproj = _att_bwd(tag + "_attb", sv["kn"], sv["qT"], proj, sv["qn"], sv["kn"].T, bias, doT, sv["oT"],
                               sv["lse"], dymix, dproj, B)
    dproj, sg["q_norm"] = _headnorm_bwd(tag + "_qnb", dqT.T, proj, COL_Q // 1024, p["q_norm"][None], dproj)
    dproj, sg["k_norm"] = _headnorm_bwd(tag + "_knb", dkn, proj, COL_K // 1024, p["k_norm"][None], dproj)
    pcol, prow = _ssd_params(p)
    dxs, dB, dC, dproj, ddt, dpar, dnw = _ssd_bwd(tag + "_ssdb", dymix, sv["Y"], sv["xc"], proj, sv["dtc"], sv["dtr"],
                                                  pcol, prow, p["ssd_norm"][None], sv["hs"], dproj, B)
    dpar = jnp.sum(dpar, axis=0)
    sg["dt_bias"] = dpar[:, 0, :].reshape(SSD_HEADS)
    sg["a_log"] = dpar[:, 1, :].reshape(SSD_HEADS)
    sg["d_skip"] = dpar[:, 2, :].reshape(SSD_HEADS)
    sg["ssd_norm"] = jnp.sum(dnw, axis=0)
    dproj, sg["conv_w"], sg["conv_b"] = _conv_bwd(tag + "_convb", proj, dxs, dB, dC, sv["cw"], p["conv_b"][None],
                                                  dproj, B)
    ddt16 = jnp.transpose(ddt, (1, 0, 2)).reshape(T, SSD_HEADS)
    dproj = lax.dynamic_update_slice(dproj, jnp.pad(ddt16, ((0, 0), (0, IN_PAD - COL_DT - SSD_HEADS))).astype(BF16),
                                     (0, COL_DT))
    win = sv["win"]
    gwin = _mm(tag + "_dwin",
               [(sv["h2"], pl.BlockSpec((tkw, D_MODEL), lambda n, m, k: (k, 0)),
                 dproj, pl.BlockSpec((tkw, PROJ_TN), lambda n, m, k: (k, n)))],
               jax.ShapeDtypeStruct((D_MODEL, IN_PAD), BF16), pl.BlockSpec((D_MODEL, PROJ_TN), lambda n, m, k: (0, n)),
               (IN_PAD // PROJ_TN, 1, T // tkw), TN, (D_MODEL, PROJ_TN))
    dx1, sg["mix_norm"] = _mm(
        tag + "_dh2",
        [(dproj, pl.BlockSpec((ROW_T, PROJ_TN), lambda i, n, k, j=j: (i, j)),
          win, pl.BlockSpec((D_MODEL, PROJ_TN), lambda i, n, k, j=j: (0, j))) for j in range(IN_PAD // PROJ_TN)],
        (jax.ShapeDtypeStruct((T, D_MODEL), F32), jax.ShapeDtypeStruct((1, D_MODEL), F32)), (_row3(), _vec3()),
        (nt, 1, 1), NT, (ROW_T, D_MODEL), post="rmsb",
        post_in=[(sv["x1"], _row3()), (p["mix_norm"][None], _vec3()), (dx2, _row3())])
    return dx1, sg, gwout, gwin


def _win_pack(w):
    return jnp.concatenate([w[:, :3072], w[:, 3088:], w[:, 3072:3088],
                            jnp.zeros((w.shape[0], IN_PAD - IN_PROJ), w.dtype)], axis=1)


def _win_unpack(g):
    return jnp.concatenate([g[:, :3072], g[:, COL_DT:COL_DT + SSD_HEADS], g[:, 3072:COL_DT]], axis=1)


DT_LO = IN_SH * 2 - COL_Q


def _win_from_shards(sh):
    main = IN_SH - DT_LO
    return jnp.concatenate([sh[0], sh[1][:, :main], sh[2][:, SSD_HEADS - DT_LO:], sh[3], sh[1][:, main:],
                            sh[2][:, :SSD_HEADS - DT_LO], jnp.zeros((sh.shape[1], IN_PAD - IN_PROJ), sh.dtype)], axis=1)


def _win_to_shards(g):
    main = IN_SH - DT_LO
    a, b = IN_SH + main, IN_SH + 2 * main
    return jnp.stack([g[:, :IN_SH],
                      jnp.concatenate([g[:, IN_SH:a], g[:, COL_DT:COL_DT + DT_LO]], axis=1),
                      jnp.concatenate([g[:, COL_DT + DT_LO:COL_DT + SSD_HEADS], g[:, a:b]], axis=1),
                      g[:, b:COL_DT]])


def _local_step(x, target, small, weights, scatter, B):
    T = x.shape[0]
    bias = _att_bias((T // B) // ATT_B)
    saved = []
    xl = x
    hl = _rms_fwd("l0f1_rms", x, small["ffn1_norm"][0][None])
    for l in range(DEPTH):
        tag = "l%d" % l
        p = {k: v[l] for k, v in small.items()}
        w1 = weights(l, "ffn1", hl)
        (x1, h2), ffn1, d1 = _ffn_fwd(tag + "f1", xl, hl, w1["g1"], w1["u1"],
                                      lambda after, l=l: weights(l, "ffn1d", after)["d1"], p["mix_norm"][None])
        x2, h3, sv = _mixer_fwd(tag, x1, h2, p, functools.partial(weights, l), bias, B)
        w2 = weights(l, "rest", x2)
        nxt = small["ffn1_norm"][l + 1][None] if l + 1 < DEPTH else None
        xo, ffn2, _ = _ffn_fwd(tag + "f2", x2, h3, w2["g2"], w2["u2"], w2["d2"], nxt)
        xl, hl = xo if nxt is not None else (xo, None)
        saved.append((ffn1, sv, ffn2, dict(g1=w1["g1"], u1=w1["u1"], d1=d1), w2))
    d, lsum = _loss_grad("loss", xl, target)
    sgrads = [None] * DEPTH
    for l in reversed(range(DEPTH)):
        tag = "l%db" % l
        p = {k: v[l] for k, v in small.items()}
        ffn1, sv, ffn2, w1, w2 = saved[l]
        sg = {}
        d, sg["ffn2_norm"] = _ffn_bwd(tag + "f2", d, ffn2, p["ffn2_norm"][None], w2["g2"], w2["u2"], w2["d2"],
                                      lambda gg, gu, gd, c, l=l: scatter(l, "ffn2", dict(g2=gg, u2=gu, d2=gd), c))
        d, sgm, gwout, gwin = _mixer_bwd(tag, d, sv, p, bias, B)
        sg.update(sgm)
        d = scatter(l, "mixer", dict(wout=gwout, win=gwin), d)
        d, sg["ffn1_norm"] = _ffn_bwd(tag + "f1", d, ffn1, p["ffn1_norm"][None], w1["g1"], w1["u1"], w1["d1"],
                                      lambda gg, gu, gd, c, l=l: scatter(l, "ffn1", dict(g1=gg, u1=gu, d1=gd), c))
        sgrads[l] = sg
    return lsum, d, sgrads


MESH = pl.DeviceIdType.MESH
ANY = pl.BlockSpec(memory_space=pl.ANY)


def _place():
    return lax.axis_index("x"), lax.axis_index("y"), lax.axis_index("c")


def _other_chips(x, y):
    return [(1 - x, y), (x, 1 - y), (1 - x, 1 - y)]


HBM = pl.BlockSpec(memory_space=pltpu.HBM)
SEM = pl.BlockSpec(memory_space=pltpu.SEMAPHORE)
EFFECT = pltpu.SideEffectType.DATAFLOW_SIDE_EFFECTING


def _hbm(a):
    return pltpu.with_memory_space_constraint(a, pltpu.HBM)


def _my_half(ref, c):
    hr = ref.shape[0] // 2
    return ref.at[pl.ds(pl.multiple_of(c * hr, 16), hr)]


def _exchange(gather, layer, halves, src, land, send, recv, n, act):
    x, y, c = _place()
    for k, (px, py) in enumerate(_other_chips(x, y)):
        for a in range(n):
            if gather:
                s_out, d_out, d_in = src[a].at[layer], land[a].at[2 * x + y], land[a].at[2 * px + py]
                if halves is not None and halves[a]:
                    s_out, d_out, d_in = _my_half(s_out, c), _my_half(d_out, c), _my_half(d_in, c)
            else:
                s_out, d_out, d_in = src[a].at[2 * px + py], land[a].at[k], land[a].at[k]
            act(pltpu.make_async_remote_copy(
                src_ref=s_out, dst_ref=d_out if act is _start else d_in, send_sem=send.at[k * n + a],
                recv_sem=recv.at[k * n + a], device_id=(px, py, c), device_id_type=MESH))


def _start(cp):
    cp.start()


def _finish(cp):
    cp.wait_send()
    cp.wait_recv()


def _exchange_start(name, gather, layer, srcs, carry, halves=None):
    n = len(srcs)
    lands = [lax.empty(((N_SHARD,) + s.shape[1:]) if gather else ((3,) + s.shape[1:]), s.dtype) for s in srcs]

    def body(*refs):
        _exchange(gather, layer, halves, refs[:n], refs[n:2 * n], refs[2 * n + 1], refs[2 * n + 2], n, _start)

    srcs = [_hbm(a) for a in srcs]
    thru = [_hbm(a) for a in lands + [carry]]
    out = pl.pallas_call(
        body, name=name,
        out_shape=(pltpu.SemaphoreType.DMA((3 * n,)), pltpu.SemaphoreType.DMA((3 * n,)),
                   *[pltpu.HBM(a.shape, a.dtype) for a in thru]),
        in_specs=[HBM] * (2 * n + 1), out_specs=(SEM, SEM, *[HBM] * (n + 1)),
        input_output_aliases={n + i: 2 + i for i in range(n + 1)},
        compiler_params=pltpu.CompilerParams(has_side_effects=EFFECT))(*srcs, *thru)
    return dict(gather=gather, layer=layer, halves=halves, send=out[0], recv=out[1], srcs=srcs,
                lands=list(out[2:2 + n])), out[-1]


def _exchange_wait(name, ex, after):
    n = len(ex["srcs"])

    def body(*refs):
        _exchange(ex["gather"], ex["layer"], ex["halves"], refs[:n], refs[n:2 * n], refs[2 * n], refs[2 * n + 1], n,
                  _finish)

    out = pl.pallas_call(
        body, name=name, out_shape=[pltpu.HBM(a.shape, a.dtype) for a in ex["lands"]],
        in_specs=[HBM] * (2 * n) + [SEM, SEM, ANY], out_specs=[HBM] * n,
        input_output_aliases={n + i: i for i in range(n)},
        compiler_params=pltpu.CompilerParams(has_side_effects=EFFECT))(
            *ex["srcs"], *ex["lands"], ex["send"], ex["recv"], after)
    return list(out)


def _sibling_fill(name, lands):
    n = len(lands)

    def body(*refs):
        land = refs[:n]
        send, recv = refs[2 * n], refs[2 * n + 1]
        x, y, c = _place()
        for k, (px, py) in enumerate(_other_chips(x, y)):
            for a in range(n):
                slot = land[a].at[2 * px + py]
                pltpu.make_async_remote_copy(src_ref=_my_half(slot, c), dst_ref=_my_half(slot, c),
                                             send_sem=send.at[k * n + a], recv_sem=recv.at[k * n + a],
                                             device_id=(x, y, 1 - c), device_id_type=MESH).start()
        for k, (px, py) in enumerate(_other_chips(x, y)):
            for a in range(n):
                slot = land[a].at[2 * px + py]
                cp = pltpu.make_async_remote_copy(src_ref=_my_half(slot, c), dst_ref=_my_half(slot, 1 - c),
                                                  send_sem=send.at[k * n + a], recv_sem=recv.at[k * n + a],
                                                  device_id=(x, y, 1 - c), device_id_type=MESH)
                cp.wait_recv()
                cp.wait_send()

    return pl.pallas_call(
        body, out_shape=[jax.ShapeDtypeStruct(a.shape, a.dtype) for a in lands],
        in_specs=[ANY] * n, out_specs=[ANY] * n, input_output_aliases={i: i for i in range(n)},
        scratch_shapes=[pltpu.SemaphoreType.DMA((3 * n,)), pltpu.SemaphoreType.DMA((3 * n,))],
        name=name)(*lands)


def _swap_sibling(name, parts):
    n = len(parts)

    def body(*refs):
        src, dst = refs[:n], refs[n:2 * n]
        send, recv = refs[2 * n:]
        x, y, c = _place()
        cps = [pltpu.make_async_remote_copy(src_ref=src[a], dst_ref=dst[a], send_sem=send.at[a], recv_sem=recv.at[a],
                                            device_id=(x, y, 1 - c), device_id_type=MESH) for a in range(n)]
        for cp in cps:
            cp.start()
        for cp in cps:
            cp.wait_recv()
        for cp in cps:
            cp.wait_send()

    return pl.pallas_call(
        body, out_shape=[jax.ShapeDtypeStruct(p.shape, p.dtype) for p in parts],
        in_specs=[ANY] * n, out_specs=[ANY] * n,
        scratch_shapes=[pltpu.SemaphoreType.DMA((n,)), pltpu.SemaphoreType.DMA((n,))],
        name=name)(*parts)


def _allreduce_small(name, v, after):
    R = v.shape[0]

    def body(v_ref, after_ref, o_ref, buf, send, recv):
        x, y, c = _place()
        me = 4 * x + 2 * y + c
        buf[me] = v_ref[...]
        cps = []
        for k in range(1, 8):
            fx, fy, fc = (k >> 2) & 1, (k >> 1) & 1, k & 1
            px = 1 - x if fx else x
            py = 1 - y if fy else y
            pc = 1 - c if fc else c
            cp = pltpu.make_async_remote_copy(src_ref=v_ref, dst_ref=buf.at[me], send_sem=send.at[k - 1],
                                              recv_sem=recv.at[k - 1], device_id=(px, py, pc), device_id_type=MESH)
            cp.start()
            cps.append((cp, 4 * px + 2 * py + pc))
        for k, (cp, peer) in enumerate(cps):
            pltpu.make_async_remote_copy(src_ref=v_ref, dst_ref=buf.at[peer], send_sem=send.at[k], recv_sem=recv.at[k],
                                         device_id=(x, y, c), device_id_type=MESH).wait_recv()
        for cp, _ in cps:
            cp.wait_send()
        acc = buf[0]
        for d in range(1, 8):
            acc = acc + buf[d]
        o_ref[...] = acc

    return pl.pallas_call(
        body, out_shape=jax.ShapeDtypeStruct((R, 128), F32),
        in_specs=[pl.BlockSpec(memory_space=pltpu.VMEM), ANY], out_specs=pl.BlockSpec(memory_space=pltpu.VMEM),
        scratch_shapes=[pltpu.VMEM((8, R, 128), F32), pltpu.SemaphoreType.DMA((7,)), pltpu.SemaphoreType.DMA((7,))],
        name=name)(v, after)


TILE_BYTES = 1600 * 1024


def _row_tile(r, c=1024):
    for t in (512, 352, 256, 128, 64, 32, 16, 8):
        if r % t == 0 and (t * c * 4 <= TILE_BYTES or t == 8):
            return t
    raise ValueError(r)


def _sum4(name, me, parts, got):
    _, R, C = parts.shape
    tr = _row_tile(R, C)

    def body(me_ref, o_ref, g_ref, s_ref):
        s = o_ref[...].astype(F32)
        for k in range(3):
            s = s + g_ref[k].astype(F32)
        s_ref[...] = s.astype(BF16)

    return pl.pallas_call(
        body, out_shape=jax.ShapeDtypeStruct((R, C), BF16),
        grid_spec=pltpu.PrefetchScalarGridSpec(
            num_scalar_prefetch=1, grid=(R // tr,),
            in_specs=[pl.BlockSpec((None, tr, C), lambda i, me_ref: (me_ref[0], i, 0)),
                      pl.BlockSpec((3, tr, C), lambda i, me_ref: (0, i, 0))],
            out_specs=pl.BlockSpec((tr, C), lambda i, me_ref: (i, 0))),
        name=name, compiler_params=_cp("parallel"))(me, parts, got)


def _adamw(name, w, gparts, m, v):
    R, C = w.shape
    tr = _row_tile(R, C)
    ng = len(gparts)
    c1 = 1.0 - ADAM_B1 ** ADAM_STEP
    c2 = 1.0 - ADAM_B2 ** ADAM_STEP

    def body(*refs):
        w_ref = refs[0]
        g_refs = refs[1:1 + ng]
        m_ref, v_ref, go_ref, d_ref, mo_ref, vo_ref = refs[1 + ng:]
        g = g_refs[0][...]
        for r in g_refs[1:]:
            g = g + r[...]
        mn = ADAM_B1 * m_ref[...] + (1.0 - ADAM_B1) * g
        vn = ADAM_B2 * v_ref[...] + (1.0 - ADAM_B2) * (g * g)
        go_ref[...] = g
        mo_ref[...] = mn
        vo_ref[...] = vn
        d_ref[...] = -ADAM_LR * ((mn / c1) / (jnp.sqrt(vn / c2) + ADAM_EPS) + ADAM_WD * w_ref[...])

    blk = pl.BlockSpec((tr, C), lambda i: (i, 0))
    osh = jax.ShapeDtypeStruct((R, C), F32)
    return pl.pallas_call(
        body, out_shape=(osh, osh, osh, osh), grid=(R // tr,), in_specs=[blk] * (3 + ng), out_specs=(blk,) * 4,
        name=name, compiler_params=_cp("parallel"))(w, *gparts, m, v)


def _adamw_layers(name, w, sums, m, v):
    _, R, C = w.shape
    tr = _row_tile(R, C)
    nr = R // tr
    c1 = 1.0 - ADAM_B1 ** ADAM_STEP
    c2 = 1.0 - ADAM_B2 ** ADAM_STEP

    def body(w_ref, a0, b0, a1, b1, m_ref, v_ref, go_ref, d_ref, mo_ref, vo_ref):
        f = lambda r: r[...].astype(F32)
        g = jnp.where(pl.program_id(0) == 0, f(a0) + f(b0), f(a1) + f(b1))
        mn = ADAM_B1 * m_ref[...] + (1.0 - ADAM_B1) * g
        vn = ADAM_B2 * v_ref[...] + (1.0 - ADAM_B2) * (g * g)
        go_ref[...] = g
        mo_ref[...] = mn
        vo_ref[...] = vn
        d_ref[...] = -ADAM_LR * ((mn / c1) / (jnp.sqrt(vn / c2) + ADAM_EPS) + ADAM_WD * w_ref[...])

    blk = pl.BlockSpec((None, tr, C), lambda l, i: (l, i, 0))
    lay0 = pl.BlockSpec((tr, C), lambda l, i: (jnp.where(l == 0, i, nr - 1), 0))
    lay1 = pl.BlockSpec((tr, C), lambda l, i: (jnp.where(l == 1, i, 0), 0))
    oblk = pl.BlockSpec((tr, C), lambda l, i: (l * nr + i, 0))
    osh = jax.ShapeDtypeStruct((DEPTH * R, C), F32)
    res = pl.pallas_call(
        body, out_shape=(osh, osh, osh, osh), grid=(DEPTH, nr),
        in_specs=[blk, lay0, lay0, lay1, lay1, blk, blk], out_specs=(oblk,) * 4,
        name=name, compiler_params=_cp("arbitrary", "arbitrary"))(w, *sums[0], *sums[1], m, v)
    return [r.reshape(w.shape) for r in res]


BIG = [("ffn1_w_gate", "g1"), ("ffn1_w_up", "u1"), ("ffn1_w_down", "d1"), ("w_in", "win"), ("w_out", "wout"),
       ("ffn2_w_gate", "g2"), ("ffn2_w_up", "u2"), ("ffn2_w_down", "d2")]
SMALL = ["ffn1_norm", "mix_norm", "conv_b", "dt_bias", "a_log", "d_skip", "ssd_norm", "q_norm", "k_norm", "ffn2_norm"]
WEIGHTS = ["ffn1_norm", "ffn1_w_gate", "ffn1_w_up", "ffn1_w_down", "mix_norm", "w_in", "conv_w", "conv_b", "dt_bias",
           "a_log", "d_skip", "ssd_norm", "q_norm", "k_norm", "w_out", "ffn2_norm", "ffn2_w_gate", "ffn2_w_up",
           "ffn2_w_down"]
CONV_SH = CONV_DIM // N_SHARD
TRANSPOSED = ("g1", "u1", "g2", "u2")
GATHER_GROUPS = [(0, "ffn1", ["g1", "u1"]), (0, "ffn1d", ["d1"]), (0, "win", ["win", "cw"]),
                 (0, "rest", ["wout", "g2", "u2", "d2"]),
                 (1, "all", ["g1", "u1", "d1", "win", "cw", "wout", "g2", "u2", "d2"])]


def _pad128(v):
    v = v.reshape(-1)
    return jnp.pad(v, (0, (-v.shape[0]) % 128))


def _pack(pieces):
    flat, offs, pos = [], [], 0
    for p in pieces:
        q = _pad128(p.astype(F32))
        offs.append(pos)
        pos += q.shape[0] // 128
        flat.append(q)
    total = -(-pos // 8) * 8
    out = jnp.concatenate(flat + [jnp.zeros(((total - pos) * 128,), F32)]).reshape(total, 128)
    return out, offs


def _unpack(packed, offs, shapes):
    out = []
    for off, shp in zip(offs, shapes):
        n = int(np.prod(shp))
        rows = -(-n // 128)
        out.append(packed[off:off + rows].reshape(-1)[:n].reshape(shp))
    return out


def kernel(x, ffn1_norm, ffn1_w_gate, ffn1_w_up, ffn1_w_down, mix_norm, w_in, conv_w, conv_b, dt_bias, a_log, d_skip, ssd_norm, q_norm, k_norm, w_out, ffn2_norm, ffn2_w_gate, ffn2_w_up, ffn2_w_down, loss_target, m_ffn1_norm, m_ffn1_w_gate, m_ffn1_w_up, m_ffn1_w_down, m_mix_norm, m_w_in, m_conv_w, m_conv_b, m_dt_bias, m_a_log, m_d_skip, m_ssd_norm, m_q_norm, m_k_norm, m_w_out, m_ffn2_norm, m_ffn2_w_gate, m_ffn2_w_up, m_ffn2_w_down, v_ffn1_norm, v_ffn1_w_gate, v_ffn1_w_up, v_ffn1_w_down, v_mix_norm, v_w_in, v_conv_w, v_conv_b, v_dt_bias, v_a_log, v_d_skip, v_ssd_norm, v_q_norm, v_k_norm, v_w_out, v_ffn2_norm, v_ffn2_w_gate, v_ffn2_w_up, v_ffn2_w_down):
    A = dict(locals())
    ix, iy, ic = _place()
    me = 2 * ix + iy
    B, S, _ = x.shape
    T = B * S

    view = lambda a, key: jnp.swapaxes(a, 1, 2) if key in TRANSPOSED else a
    own = {key: view(A[name], key).astype(BF16) for name, key in BIG}
    own["cw"] = conv_w
    exs, first_norm = [], ffn1_norm
    split = lambda l, key: l == 0 and key != "cw"
    for gi, (l, _, keys) in enumerate(GATHER_GROUPS):
        ex, first_norm = _exchange_start("gather_start%d" % gi, True, l, [own[key] for key in keys], first_norm,
                                         [split(l, key) for key in keys])
        exs.append(ex)
    landed = {}

    def weights(l, group, after):
        gi = [i for i, (gl, gname, _) in enumerate(GATHER_GROUPS) if gl == l and gname in (group, "all")][0]
        if gi not in landed:
            lands = _exchange_wait("gather_wait%d" % gi, exs[gi], after)
            keys = GATHER_GROUPS[gi][2]
            halved = [i for i, key in enumerate(keys) if split(l, key)]
            if halved:
                for i, whole in zip(halved, _sibling_fill("gather_fill%d" % gi, [lands[i] for i in halved])):
                    lands[i] = whole
            landed[gi] = {}
            for key, land in zip(GATHER_GROUPS[gi][2], lands):
                full = lax.dynamic_update_slice(land, own[key][l][None], (me, 0, 0))
                if key == "win":
                    full = _win_from_shards(full)
                if key == "cw":
                    full = jnp.transpose(full, (1, 0, 2)).reshape(CONV_K, CONV_DIM)
                landed[gi][key] = full
        return landed[gi]

    pending = []

    def scatter(l, group, grads, carry):
        keys = sorted(grads)
        arrs = [grads[key] for key in keys]
        if "win" in grads:
            arrs[keys.index("win")] = _win_to_shards(grads["win"])
        ex, carry = _exchange_start("scatter_start_l%d_%s" % (l, group), False, None, arrs, carry)
        pending.append((l, keys, ex))
        return carry

    small = {name: A[name] for name in SMALL}
    small["ffn1_norm"] = first_norm
    lsum, dx, sgrads = _local_step(x.reshape(T, D_MODEL), loss_target.reshape(T, D_MODEL), small, weights, scatter, B)

    names = SMALL + ["conv_w"]
    shapes = [A[n].shape for n in SMALL] + [(DEPTH, CONV_K, CONV_DIM), ()]
    pieces = [jnp.stack([sgrads[l][n].reshape(shp[1:]) for l in range(DEPTH)]) for n, shp in zip(names, shapes)]
    pieces.append(0.5 / D_MODEL * jnp.sum(lsum))
    packed, offs = _pack(pieces)

    sums, theirs, out = {}, {}, {}
    me1 = jnp.reshape(me, (1,)).astype(jnp.int32)

    def update(tag, after):
        todo = [k for k in sums if k not in theirs]
        theirs.update(zip(todo, _swap_sibling("swap_sibling_" + tag, [sums[k] for k in todo])))
        for name, key in BIG:
            if name not in out and all((key, l) in theirs for l in range(DEPTH)):
                res = _adamw_layers("adamw_" + key, view(A[name], key),
                                    [(sums[key, l], theirs[key, l]) for l in range(DEPTH)],
                                    view(A["m_" + name], key), view(A["v_" + name], key))
                out[name] = [view(r, key) for r in res]
                after = res[0]
        return after

    after = dx
    for idx, (l, keys, ex) in enumerate(pending):
        if idx == len(pending) - 1:
            after = update("a", after)
        lands = _exchange_wait("scatter_wait%d" % idx, ex, after)
        for key, g, got in zip(keys, ex["srcs"], lands):
            sums[key, l] = after = _sum4("sum_%s_l%d" % (key, l), me1, g, got)
    after = update("b", after)

    red = _unpack(_allreduce_small("allreduce_small", packed, after), offs, shapes)
    loss = red[-1]
    sg = dict(zip(names, red[:-1]))

    wp, offs = _pack([A[n] for n in SMALL])
    gp, _ = _pack([sg[n] for n in SMALL])
    mp, _ = _pack([A["m_" + n] for n in SMALL])
    vp, _ = _pack([A["v_" + n] for n in SMALL])
    res = _adamw("adamw_small", wp, [gp], mp, vp)
    shapes = [A[n].shape for n in SMALL]
    res = [_unpack(r, offs, shapes) for r in res]
    for i, n in enumerate(SMALL):
        out[n] = [res[q][i] for q in range(4)]
    gcw = lax.dynamic_slice_in_dim(sg["conv_w"], me * CONV_SH, CONV_SH, axis=2)
    flat = lambda a: a.reshape(DEPTH * CONV_K, CONV_SH)
    res = _adamw("adamw_conv_w", flat(conv_w), [flat(gcw)], flat(m_conv_w), flat(v_conv_w))
    out["conv_w"] = [r.reshape(conv_w.shape) for r in res]

    outs = [loss, dx.reshape(B, S, D_MODEL)]
    for q in range(4):
        outs += [out[n][q] for n in WEIGHTS]
    return tuple(outs)
```

```python
import functools
import math

import numpy as np
import jax
import jax.numpy as jnp
from jax import lax
from jax.experimental import pallas as pl
from jax.experimental.pallas import tpu as pltpu

F32 = jnp.float32
BF16 = jnp.bfloat16

D_MODEL = 1024
DEPTH = 2
N_SHARD = 4
D_FF = 2816
FF_SH = D_FF // N_SHARD
SSD_HEADS = 16
HEAD_DIM = 64
SSD_GROUPS = 4
GROUP_W = 256
SSD_STATE = 128
CONV_K = 4
CONV_DIM = 2048
ATT_HEADS = 16
MIX_W = 2048
MIX_SH = MIX_W // N_SHARD
IN_PROJ = 6160
IN_SH = IN_PROJ // N_SHARD
IN_PAD = 6272
PROJ_TN = 896
COL_Z, COL_XBC, COL_Q, COL_K, COL_V, COL_DT = 0, 1024, 3072, 4096, 5120, 6144
EPS = 1e-6
NEG = -1e30
SSD_L = 512
ATT_B = 512
ROW_T = 512
HALF_T = ROW_T // 2
GU_T = 1024
TK_W = 2048
CONV_CT = 256
CONV_R = 256
PAD_R = 8

ADAM_LR, ADAM_B1, ADAM_B2, ADAM_EPS, ADAM_WD, ADAM_STEP = 0.001, 0.9, 0.999, 1e-08, 0.01, 10

NN = (((1,), (0,)), ((), ()))
NT = (((1,), (1,)), ((), ()))
TN = (((0,), (0,)), ((), ()))

VMEM_LIMIT = 56 * 1024 * 1024


def _cp(*sem):
    return pltpu.CompilerParams(dimension_semantics=sem, vmem_limit_bytes=VMEM_LIMIT)


def _dot(a, b, dims):
    return lax.dot_general(a, b, dims, preferred_element_type=F32)


def _sigmoid(x):
    return 0.5 * jnp.tanh(0.5 * x) + 0.5


def _softplus(x):
    return jnp.maximum(x, 0.0) + jnp.log(1.0 + jnp.exp(-jnp.abs(x)))


def _mm(name, pairs, out_shape, out_spec, grid, dims, acc_shape, res=None, scale=1.0, post=None, post_in=()):
    nk = grid[2]
    npair = len(pairs)
    npost = len(post_in)

    def body(*refs):
        ab = refs[:2 * npair]
        pos = 2 * npair
        res_ref = None
        if res is not None:
            res_ref = refs[pos]
            pos += 1
        pin = refs[pos:pos + npost]
        pos += npost
        out_ref = refs[pos]
        pos += 1
        if post is not None:
            out2_ref = refs[pos]
            pos += 1
        s = None
        for p in range(npair):
            d = _dot(ab[2 * p][...].astype(BF16), ab[2 * p + 1][...].astype(BF16), dims)
            s = d if s is None else s + d

        def finish(r):
            if scale != 1.0:
                r = r * scale
            if res_ref is not None:
                r = r + res_ref[...]
            if post == "rmsb":
                @pl.when(pl.program_id(0) == 0)
                def _():
                    out2_ref[...] = jnp.zeros_like(out2_ref)

                xv = pin[0][...]
                rr = lax.rsqrt(jnp.mean(xv * xv, axis=-1, keepdims=True) + EPS)
                xhat = xv * rr
                dxhat = r * pin[1][...]
                out_ref[...] = pin[2][...] + rr * (dxhat - xhat * jnp.mean(dxhat * xhat, axis=-1, keepdims=True))
                out2_ref[...] += jnp.sum(r * xhat, axis=0, keepdims=True)
                return
            out_ref[...] = r.astype(out_ref.dtype)
            if post == "norm":
                rr = lax.rsqrt(jnp.mean(r * r, axis=-1, keepdims=True) + EPS)
                out2_ref[...] = (r * rr * pin[0][...]).astype(BF16)

        if nk == 1:
            finish(s)
            return
        acc = refs[pos]
        k = pl.program_id(2)

        @pl.when(k == 0)
        def _():
            acc[...] = s

        @pl.when(k > 0)
        def _():
            acc[...] += s

        @pl.when(k == nk - 1)
        def _():
            finish(acc[...])

    args, specs = [], []
    for a, a_spec, b, b_spec in pairs:
        args += [a, b]
        specs += [a_spec, b_spec]
    for arr, spec in ([res] if res is not None else []) + list(post_in):
        args.append(arr)
        specs.append(spec)
    sems = ("arbitrary",) * 3 if post == "rmsb" else ("parallel", "parallel", "arbitrary")
    return pl.pallas_call(
        body, out_shape=out_shape, grid=grid, in_specs=specs, out_specs=out_spec,
        scratch_shapes=[] if nk == 1 else [pltpu.VMEM(acc_shape, F32)], name=name,
        compiler_params=_cp(*sems))(*args)


def _rms_fwd(name, x, w):
    T = x.shape[0]

    def body(x_ref, w_ref, o_ref):
        xv = x_ref[...]
        r = lax.rsqrt(jnp.mean(xv * xv, axis=-1, keepdims=True) + EPS)
        o_ref[...] = (xv * r * w_ref[...]).astype(BF16)

    return pl.pallas_call(
        body, out_shape=jax.ShapeDtypeStruct((T, D_MODEL), BF16), grid=(T // ROW_T,),
        in_specs=[pl.BlockSpec((ROW_T, D_MODEL), lambda i: (i, 0)), pl.BlockSpec((1, D_MODEL), lambda i: (0, 0))],
        out_specs=pl.BlockSpec((ROW_T, D_MODEL), lambda i: (i, 0)), name=name, compiler_params=_cp("parallel"))(x, w)


def _loss_grad(name, y, t):
    T = y.shape[0]

    def body(y_ref, t_ref, dy_ref, l_ref):
        @pl.when(pl.program_id(0) == 0)
        def _():
            l_ref[...] = jnp.zeros_like(l_ref)

        e = y_ref[...] - t_ref[...]
        dy_ref[...] = e * (1.0 / D_MODEL)
        l_ref[...] += jnp.sum(e * e, axis=0, keepdims=True)

    row = pl.BlockSpec((ROW_T, D_MODEL), lambda i: (i, 0))
    vec = pl.BlockSpec((1, D_MODEL), lambda i: (0, 0))
    return pl.pallas_call(
        body, out_shape=(jax.ShapeDtypeStruct((T, D_MODEL), F32), jax.ShapeDtypeStruct((1, D_MODEL), F32)),
        grid=(T // ROW_T,), in_specs=[row, row], out_specs=(row, vec), name=name,
        compiler_params=_cp("arbitrary"))(y, t)


def _ffn_gate_up(name, h, wg, wu):
    T = h.shape[0]
    tm = min(GU_T, T)

    def body(h_ref, wg_ref, wu_ref, dgf_ref, duf_ref, a_ref):
        for r in range(0, tm, HALF_T):
            rows = slice(r, r + HALF_T)
            hv = h_ref[rows, :]
            g = _dot(hv, wg_ref[...], NT)
            u = _dot(hv, wu_ref[...], NT)
            sg = _sigmoid(g)
            silu = g * sg
            dgf_ref[rows, :] = (u * (sg * (1.0 + g * (1.0 - sg)))).astype(BF16)
            duf_ref[rows, :] = silu.astype(BF16)
            a_ref[rows, :] = (silu * u).astype(BF16)

    wspec = pl.BlockSpec((None, FF_SH, D_MODEL), lambda j, i: (j, 0, 0))
    ospec = pl.BlockSpec((None, tm, FF_SH), lambda j, i: (j, i, 0))
    osh = jax.ShapeDtypeStruct((N_SHARD, T, FF_SH), BF16)
    return pl.pallas_call(
        body, out_shape=(osh, osh, osh), grid=(N_SHARD, T // tm),
        in_specs=[pl.BlockSpec((tm, D_MODEL), lambda j, i: (i, 0)), wspec, wspec],
        out_specs=(ospec, ospec, ospec), name=name, compiler_params=_cp("parallel", "parallel"))(h, wg, wu)


def _ffn_dact(name, dx, wd, g, u):
    T = dx.shape[0]
    tm = min(GU_T, T)

    def body(dx_ref, wd_ref, g_ref, u_ref, dg_ref, du_ref):
        for r in range(0, tm, HALF_T):
            rows = slice(r, r + HALF_T)
            da = 0.5 * _dot(dx_ref[rows, :].astype(BF16), wd_ref[...], NT)
            dg_ref[rows, :] = (da * g_ref[rows, :].astype(F32)).astype(BF16)
            du_ref[rows, :] = (da * u_ref[rows, :].astype(F32)).astype(BF16)

    aspec = pl.BlockSpec((None, tm, FF_SH), lambda j, i: (j, i, 0))
    osh = jax.ShapeDtypeStruct((N_SHARD, T, FF_SH), BF16)
    return pl.pallas_call(
        body, out_shape=(osh, osh), grid=(N_SHARD, T // tm),
        in_specs=[pl.BlockSpec((tm, D_MODEL), lambda j, i: (i, 0)),
                  pl.BlockSpec((None, FF_SH, D_MODEL), lambda j, i: (j, 0, 0)), aspec, aspec],
        out_specs=(aspec, aspec), name=name, compiler_params=_cp("parallel", "parallel"))(dx, wd, g, u)


def _row3():
    return pl.BlockSpec((ROW_T, D_MODEL), lambda i, n, k: (i, 0))


def _vec3():
    return pl.BlockSpec((1, D_MODEL), lambda i, n, k: (0, 0))


def _with_norm(T, next_nw):
    if next_nw is None:
        return dict(out_shape=jax.ShapeDtypeStruct((T, D_MODEL), F32), out_spec=_row3())
    return dict(out_shape=(jax.ShapeDtypeStruct((T, D_MODEL), F32), jax.ShapeDtypeStruct((T, D_MODEL), BF16)),
                out_spec=(_row3(), _row3()), post="norm", post_in=[(next_nw, _vec3())])


def _ffn_fwd(tag, x, h, wg, wu, wd, next_nw):
    T = x.shape[0]
    g, u, a = _ffn_gate_up(tag + "_gu", h, wg, wu)
    if callable(wd):
        wd = wd(a)
    nt = T // ROW_T
    o = _with_norm(T, next_nw)
    xo = _mm(tag + "_down",
             [(a, pl.BlockSpec((None, ROW_T, FF_SH), lambda i, n, k, j=j: (j, i, 0)),
               wd, pl.BlockSpec((None, FF_SH, D_MODEL), lambda i, n, k, j=j: (j, 0, 0))) for j in range(N_SHARD)],
             o.pop("out_shape"), o.pop("out_spec"), (nt, 1, 1), NN, (ROW_T, D_MODEL),
             res=(x, _row3()), scale=0.5, **o)
    return xo, (x, h, g, u, a), wd


def _ffn_bwd(tag, dxo, saved, nw, wg, wu, wd, emit):
    x, h, g, u, a = saved
    T = x.shape[0]
    nt = T // ROW_T
    tkw = min(TK_W, T)
    nw_t = T // tkw
    dg, du = _ffn_dact(tag + "_dact", dxo, wd, g, u)
    actw = lambda f: pl.BlockSpec((None, tkw, FF_SH), f)
    gd = _mm(tag + "_dwd",
             [(a, actw(lambda m, n, k: (m, k, 0)), dxo, pl.BlockSpec((tkw, D_MODEL), lambda m, n, k: (k, 0)))],
             jax.ShapeDtypeStruct((N_SHARD, FF_SH, D_MODEL), BF16),
             pl.BlockSpec((None, FF_SH, D_MODEL), lambda m, n, k: (m, 0, 0)),
             (N_SHARD, 1, nw_t), TN, (FF_SH, D_MODEL), scale=0.5)
    hspec = pl.BlockSpec((tkw, D_MODEL), lambda j, n, k: (k, 0))
    gsh = jax.ShapeDtypeStruct((N_SHARD, FF_SH, D_MODEL), BF16)
    gspec = pl.BlockSpec((None, FF_SH, D_MODEL), lambda j, n, k: (j, 0, 0))
    gg = _mm(tag + "_dwg", [(dg, actw(lambda j, n, k: (j, k, 0)), h, hspec)], gsh, gspec,
             (N_SHARD, 1, nw_t), TN, (FF_SH, D_MODEL))
    gu = _mm(tag + "_dwu", [(du, actw(lambda j, n, k: (j, k, 0)), h, hspec)], gsh, gspec,
             (N_SHARD, 1, nw_t), TN, (FF_SH, D_MODEL))
    dg = emit(gg, gu, gd, dg)
    act = lambda j: pl.BlockSpec((None, ROW_T, FF_SH), lambda i, n, k: (j, i, 0))
    wsp = lambda j: pl.BlockSpec((None, FF_SH, D_MODEL), lambda i, n, k: (j, 0, 0))
    return _mm(tag + "_dh",
               [(dd, act(j), w, wsp(j)) for j in range(N_SHARD) for dd, w in ((dg, wg), (du, wu))],
               (jax.ShapeDtypeStruct((T, D_MODEL), F32), jax.ShapeDtypeStruct((1, D_MODEL), F32)), (_row3(), _vec3()),
               (nt, 1, 1), NN, (ROW_T, D_MODEL), post="rmsb", post_in=[(x, _row3()), (nw, _vec3()), (dxo, _row3())])


def _seq_rows(ref, start, size, S):
    lo, hi = max(start, 0), min(start + size, S)
    parts = [ref[pl.ds(lo, hi - lo), :]]
    if lo > start:
        parts.insert(0, jnp.zeros((lo - start, ref.shape[1]), F32))
    if start + size > hi:
        parts.append(jnp.zeros((start + size - hi, ref.shape[1]), F32))
    return parts[0] if len(parts) == 1 else jnp.concatenate(parts, axis=0)


XBC_CB = COL_XBC // CONV_CT


def _conv_fwd(name, proj, w, b, B):
    T = proj.shape[0]
    S = T // B
    C = CONV_DIM

    def body(x_ref, w_ref, b_ref, o_ref):
        wv = w_ref[...]
        for c in range(S // CONV_R):
            r0 = c * CONV_R
            ch = _seq_rows(x_ref, r0 - PAD_R, CONV_R + PAD_R, S)
            pre = ch[PAD_R:] * wv[3:4] + b_ref[...]
            for s in range(1, CONV_K):
                pre = pre + pltpu.roll(ch, s, axis=0)[PAD_R:] * wv[3 - s:4 - s]
            o_ref[pl.ds(r0, CONV_R), :] = pre * _sigmoid(pre)

    return pl.pallas_call(
        body, out_shape=jax.ShapeDtypeStruct((T, C), F32), grid=(B, C // CONV_CT),
        in_specs=[pl.BlockSpec((S, CONV_CT), lambda bi, ci: (bi, XBC_CB + ci)),
                  pl.BlockSpec((CONV_K, CONV_CT), lambda bi, ci: (0, ci)),
                  pl.BlockSpec((1, CONV_CT), lambda bi, ci: (0, ci))],
        out_specs=pl.BlockSpec((S, CONV_CT), lambda bi, ci: (bi, ci)), name=name,
        compiler_params=_cp("parallel", "parallel"))(proj, w, b)


def _conv_bwd(name, proj, dxs, dB, dC, w, b, dproj, B):
    T = proj.shape[0]
    S = T // B
    C = CONV_DIM
    RW = CONV_R + PAD_R
    nx, nb = dxs.shape[1] // CONV_CT, dB.shape[1] // CONV_CT

    def body(x_ref, dx_in, db_in, dc_in, w_ref, b_ref, buf_ref, dx_ref, dw_ref, db_ref):
        @pl.when(pl.program_id(1) == 0)
        def _():
            dw_ref[...] = jnp.zeros_like(dw_ref)
            db_ref[...] = jnp.zeros_like(db_ref)

        ci = pl.program_id(0)
        wv = w_ref[...]
        dw = [jnp.zeros((1, CONV_CT), F32) for _ in range(CONV_K)]
        db = jnp.zeros((1, CONV_CT), F32)
        for c in range(S // CONV_R):
            r0 = c * CONV_R
            ch = _seq_rows(x_ref, r0 - PAD_R, RW + PAD_R, S)
            xs = [ch[PAD_R:]] + [pltpu.roll(ch, s, axis=0)[PAD_R:] for s in range(1, CONV_K)]
            pre = b_ref[...] + xs[0] * wv[3:4]
            for s in range(1, CONV_K):
                pre = pre + xs[s] * wv[3 - s:4 - s]
            sg = _sigmoid(pre)
            dout = jnp.where(ci < nx, _seq_rows(dx_in, r0, RW, S),
                             jnp.where(ci < nx + nb, _seq_rows(db_in, r0, RW, S), _seq_rows(dc_in, r0, RW, S)))
            dpre = dout * (sg * (1.0 + pre * (1.0 - sg)))
            dx = dpre[:CONV_R] * wv[3:4]
            for s in range(1, CONV_K):
                dx = dx + pltpu.roll(dpre, RW - s, axis=0)[:CONV_R] * wv[3 - s:4 - s]
            dx_ref[pl.ds(r0, CONV_R), :] = dx.astype(BF16)
            dcur = dpre[:CONV_R]
            db = db + jnp.sum(dcur, axis=0, keepdims=True)
            for s in range(CONV_K):
                dw[3 - s] = dw[3 - s] + jnp.sum(dcur * xs[s][:CONV_R], axis=0, keepdims=True)
        db_ref[...] += db
        for k in range(CONV_K):
            dw_ref[k:k + 1, :] += dw[k]

    seq = lambda f: pl.BlockSpec((S, CONV_CT), f)
    return pl.pallas_call(
        body,
        out_shape=(jax.ShapeDtypeStruct(dproj.shape, dproj.dtype), jax.ShapeDtypeStruct((CONV_K, C), F32),
                   jax.ShapeDtypeStruct((1, C), F32)),
        grid=(C // CONV_CT, B),
        in_specs=[seq(lambda ci, bi: (bi, XBC_CB + ci)),
                  seq(lambda ci, bi: (bi, jnp.minimum(ci, nx - 1))),
                  seq(lambda ci, bi: (bi, jnp.clip(ci - nx, 0, nb - 1))),
                  seq(lambda ci, bi: (bi, jnp.clip(ci - nx - nb, 0, nb - 1))),
                  pl.BlockSpec((CONV_K, CONV_CT), lambda ci, bi: (0, ci)),
                  pl.BlockSpec((1, CONV_CT), lambda ci, bi: (0, ci)), ANY],
        out_specs=(seq(lambda ci, bi: (bi, XBC_CB + ci)),
                   pl.BlockSpec((CONV_K, CONV_CT), lambda ci, bi: (0, ci)),
                   pl.BlockSpec((1, CONV_CT), lambda ci, bi: (0, ci))),
        input_output_aliases={6: 0},
        name=name, compiler_params=_cp("parallel", "arbitrary"))(proj, dxs, dB, dC, w, b, dproj)


def _tri_sum(tri, x, dims, tri_first, terms=3):
    out, rest = None, x
    for t in range(terms):
        part = rest.astype(BF16)
        if t + 1 < terms:
            rest = rest - part.astype(F32)
        d = _dot(tri, part, dims) if tri_first else _dot(part, tri, dims)
        out = d if out is None else out + d
    return out


def _total(x):
    return jnp.sum(jnp.sum(x, axis=0, keepdims=True), axis=-1, keepdims=True)


def _ssd_common(dtc_ref, dtr_ref, pcol_ref, prow_ref, b_ref, c_ref):
    L = SSD_L
    bias_c, alog_c = pcol_ref[0:1, :], pcol_ref[1:2, :]
    a_c = -jnp.exp(alog_c)
    dt_c = _softplus(dtc_ref[...] + bias_c)
    row = lax.broadcasted_iota(jnp.int32, (L, L), 0)
    col = lax.broadcasted_iota(jnp.int32, (L, L), 1)
    causal = row >= col
    tri = causal.astype(BF16)
    cum_c = _tri_sum(tri, dt_c * a_c, NN, True)
    a_r = -jnp.exp(prow_ref[:, 1:2])
    dt_r = _softplus(dtr_ref[...] + prow_ref[:, 0:1])
    cum_r = _tri_sum(tri, dt_r * a_r, NT, False)
    bb = b_ref[...].astype(BF16)
    cb = c_ref[...].astype(BF16)
    G = _dot(cb, bb, NT)
    return a_c, dt_c, causal, tri, cum_c, cum_r, bb, cb, G


def _ssd_fwd(name, xc, proj, dtc, dtr, pcol, prow, nw, B):
    T = xc.shape[0]
    S = T // B
    nb = S // SSD_L
    L = SSD_L

    def body(xs_ref, b_ref, c_ref, z_ref, dtc_ref, dtr_ref, pcol_ref, prow_ref, nw_ref, y_ref, yn_ref, hs_ref, H, yo_s):
        @pl.when(pl.program_id(2) == 0)
        def _():
            H[...] = jnp.zeros_like(H)

        a_c, dt_c, causal, tri, cum_c, cum_r, bb, cb, G = _ssd_common(dtc_ref, dtr_ref, pcol_ref, prow_ref, b_ref, c_ref)
        dsk = pcol_ref[2:3, :]
        clast = cum_c[L - 1:L, :]
        bf = b_ref[...]
        for h in range(4):
            hs_ref[h] = H[h]
            yo_s[h] = _dot(cb, H[h].astype(BF16), NN)
        for h in range(4):
            sl = slice(HEAD_DIM * h, HEAD_DIM * (h + 1))
            cc = cum_c[:, h:h + 1]
            lm = jnp.exp(jnp.where(causal, cc - cum_r[h:h + 1, :], NEG))
            M = (G * lm).astype(BF16)
            xh = xs_ref[:, sl]
            Xb = (xh * dt_c[:, h:h + 1]).astype(BF16)
            Hh = H[h]
            y = _dot(M, Xb, NN) + jnp.exp(cc) * yo_s[h]
            y_ref[:, sl] = y + dsk[:, h:h + 1] * xh
            cl = clast[:, h:h + 1]
            Bw = (bf * jnp.exp(cl - cc)).astype(BF16)
            H[h] = jnp.exp(cl) * Hh + _dot(Bw, Xb, TN)
        zv = z_ref[...]
        y2 = y_ref[...] * (zv * _sigmoid(zv))
        r = lax.rsqrt(jnp.mean(y2 * y2, axis=-1, keepdims=True) + EPS)
        yn_ref[...] = (y2 * r * nw_ref[...]).astype(BF16)

    rowi = lambda b, g, i: b * nb + i
    grp = pl.BlockSpec((L, GROUP_W), lambda b, g, i: (rowi(b, g, i), g))
    return pl.pallas_call(
        body,
        out_shape=(jax.ShapeDtypeStruct((T, 1024), F32), jax.ShapeDtypeStruct((T, 1024), BF16),
                   jax.ShapeDtypeStruct((B, SSD_GROUPS, nb, 4, SSD_STATE, HEAD_DIM), F32)),
        grid=(B, SSD_GROUPS, nb),
        in_specs=[grp,
                  pl.BlockSpec((L, SSD_STATE), lambda b, g, i: (rowi(b, g, i), 8 + g)),
                  pl.BlockSpec((L, SSD_STATE), lambda b, g, i: (rowi(b, g, i), 12 + g)),
                  grp,
                  pl.BlockSpec((None, L, 4), lambda b, g, i: (g, rowi(b, g, i), 0)),
                  pl.BlockSpec((None, 4, L), lambda b, g, i: (g, 0, rowi(b, g, i))),
                  pl.BlockSpec((None, 3, 4), lambda b, g, i: (g, 0, 0)),
                  pl.BlockSpec((None, 4, 3), lambda b, g, i: (g, 0, 0)),
                  pl.BlockSpec((1, GROUP_W), lambda b, g, i: (0, g))],
        out_specs=(grp, grp,
                   pl.BlockSpec((None, None, None, 4, SSD_STATE, HEAD_DIM), lambda b, g, i: (b, g, i, 0, 0, 0))),
        scratch_shapes=[pltpu.VMEM((4, SSD_STATE, HEAD_DIM), F32), pltpu.VMEM((4, L, HEAD_DIM), F32)], name=name,
        compiler_params=_cp("parallel", "parallel", "arbitrary"))(xc, xc, xc, proj, dtc, dtr, pcol, prow, nw)


def _ssd_bwd(name, dyn, Y, xc, proj, dtc, dtr, pcol, prow, nw, hs, dproj, B):
    T = xc.shape[0]
    S = T // B
    nb = S // SSD_L
    L = SSD_L

    def body(dyn_ref, y_ref, xs_ref, b_ref, c_ref, z_ref, dtc_ref, dtr_ref, pcol_ref, prow_ref, nw_ref, hs_ref, buf_ref,
             dxs_ref, db_ref, dc_ref, dz_ref, ddt_ref, dpar_ref, dnw_ref, dH, dm_s, dxo_s, ea_s, ex_s):
        @pl.when(pl.program_id(2) == 0)
        def _():
            dH[...] = jnp.zeros_like(dH)
            dpar_ref[...] = jnp.zeros_like(dpar_ref)
            dnw_ref[...] = jnp.zeros_like(dnw_ref)

        a_c, dt_c, causal, tri, cum_c, cum_r, bb, cb, G = _ssd_common(dtc_ref, dtr_ref, pcol_ref, prow_ref, b_ref, c_ref)
        dsk = pcol_ref[2:3, :]
        clast = cum_c[L - 1:L, :]
        bf = b_ref[...]
        cf = c_ref[...]
        Yv = y_ref[...]
        zv = z_ref[...]
        sz = _sigmoid(zv)
        silu = zv * sz
        y2 = Yv * silu
        r = lax.rsqrt(jnp.mean(y2 * y2, axis=-1, keepdims=True) + EPS)
        yhat = y2 * r
        dyv = dyn_ref[...]
        dnw_ref[...] += jnp.sum(dyv * yhat, axis=0, keepdims=True)
        dyhat = dyv * nw_ref[...]
        dy2 = r * (dyhat - yhat * jnp.mean(dyhat * yhat, axis=-1, keepdims=True))
        dY = dy2 * silu
        dz_ref[...] = (dy2 * Yv * (sz * (1.0 + zv * (1.0 - sz)))).astype(BF16)

        lane4 = lax.broadcasted_iota(jnp.int32, (1, 4), 1)
        dG = jnp.zeros((L, L), F32)
        dBs = jnp.zeros((L, SSD_STATE), F32)
        dCs = jnp.zeros((L, SSD_STATE), F32)
        ddsk = jnp.zeros((1, 4), F32)
        dcl = jnp.zeros((1, 4), F32)
        for h in range(4):
            sl = slice(HEAD_DIM * h, HEAD_DIM * (h + 1))
            xb = (xs_ref[:, sl] * dt_c[:, h:h + 1]).astype(BF16)
            dm_s[h] = _dot(dY[:, sl].astype(BF16), xb, NT)
            dxo_s[h] = _dot(bb, dH[h].astype(BF16), NN)
        for h in range(4):
            sl = slice(HEAD_DIM * h, HEAD_DIM * (h + 1))
            onehot = (lane4 == h).astype(F32)
            cc = cum_c[:, h:h + 1]
            cl = clast[:, h:h + 1]
            lm = jnp.exp(jnp.where(causal, cc - cum_r[h:h + 1, :], NEG))
            M = (G * lm).astype(BF16)
            xh = xs_ref[:, sl]
            dth = dt_c[:, h:h + 1]
            X = xh * dth
            Xb = X.astype(BF16)
            dYh = dY[:, sl]
            dYb = dYh.astype(BF16)
            Hb = hs_ref[h].astype(BF16)
            dHh = dH[h]
            dHb = dHh.astype(BF16)
            alpha = jnp.exp(cc)
            beta = jnp.exp(cl - cc)
            dXoff = beta * dxo_s[h]
            dX = _dot(M, dYb, TN) + dXoff
            dG = dG + dm_s[h] * lm
            dCs = dCs + _dot((alpha * dYh).astype(BF16), Hb, NT)
            dBs = dBs + _dot((beta * X).astype(BF16), dHb, NT)
            ypre = Yv[:, sl] - dsk[:, h:h + 1] * xh
            ea_s[:, sl] = dYb.astype(F32) * ypre - Xb.astype(F32) * dX
            ex_s[:, sl] = dX * xh
            dcl_h = (_total(dHh * (jnp.exp(cl) * hs_ref[h])) + _total(Xb.astype(F32) * dXoff))
            dcl = dcl + dcl_h * onehot
            ddsk = ddsk + _total(dYh * xh) * onehot
            dxs_ref[:, sl] = dsk[:, h:h + 1] * dYh + dX * dth
            dH[h] = jnp.exp(cl) * dHh + _dot((alpha * cf).astype(BF16), dYb, TN)
        dGb = dG.astype(BF16)
        dc_ref[...] = _dot(dGb, bb, NN) + dCs
        db_ref[...] = _dot(dGb, cb, TN) + dBs
        feat = lax.broadcasted_iota(jnp.int32, (GROUP_W, 4), 0)
        head = lax.broadcasted_iota(jnp.int32, (GROUP_W, 4), 1) * HEAD_DIM
        sel = ((feat >= head) & (feat < head + HEAD_DIM)).astype(BF16)
        dA = _tri_sum(sel, ea_s[...], NN, False)
        ddtx = _tri_sum(sel, ex_s[...], NN, False)
        last = lax.broadcasted_iota(jnp.int32, (L, 1), 0) == L - 1
        dA = dA + jnp.where(last, dcl, 0.0)
        dadt = _tri_sum(tri, dA, TN, True)
        ddt = dadt * a_c + ddtx
        d_a = jnp.sum(dadt * dt_c, axis=0, keepdims=True)
        ddraw = ddt * _sigmoid(dtc_ref[...] + pcol_ref[0:1, :])
        ddt_ref[...] = ddraw
        dpar_ref[0:1, :] += jnp.sum(ddraw, axis=0, keepdims=True)
        dpar_ref[1:2, :] += d_a * a_c
        dpar_ref[2:3, :] += ddsk

    rowi = lambda b, g, i: b * nb + (nb - 1 - i)
    grp = pl.BlockSpec((L, GROUP_W), lambda b, g, i: (rowi(b, g, i), g))
    st = pl.BlockSpec((L, SSD_STATE), lambda b, g, i: (rowi(b, g, i), g))
    f = jax.ShapeDtypeStruct
    return pl.pallas_call(
        body,
        out_shape=(f((T, 1024), F32), f((T, 512), F32), f((T, 512), F32), f(dproj.shape, dproj.dtype),
                   f((SSD_GROUPS, T, 4), F32), f((B, SSD_GROUPS, 3, 4), F32), f((B, 1, 1024), F32)),
        grid=(B, SSD_GROUPS, nb),
        in_specs=[grp, grp, grp,
                  pl.BlockSpec((L, SSD_STATE), lambda b, g, i: (rowi(b, g, i), 8 + g)),
                  pl.BlockSpec((L, SSD_STATE), lambda b, g, i: (rowi(b, g, i), 12 + g)),
                  grp,
                  pl.BlockSpec((None, L, 4), lambda b, g, i: (g, rowi(b, g, i), 0)),
                  pl.BlockSpec((None, 4, L), lambda b, g, i: (g, 0, rowi(b, g, i))),
                  pl.BlockSpec((None, 3, 4), lambda b, g, i: (g, 0, 0)),
                  pl.BlockSpec((None, 4, 3), lambda b, g, i: (g, 0, 0)),
                  pl.BlockSpec((1, GROUP_W), lambda b, g, i: (0, g)),
                  pl.BlockSpec((None, None, None, 4, SSD_STATE, HEAD_DIM), lambda b, g, i: (b, g, nb - 1 - i, 0, 0, 0)),
                  ANY],
        out_specs=(grp, st, st, grp,
                   pl.BlockSpec((None, L, 4), lambda b, g, i: (g, rowi(b, g, i), 0)),
                   pl.BlockSpec((None, None, 3, 4), lambda b, g, i: (b, g, 0, 0)),
                   pl.BlockSpec((None, 1, GROUP_W), lambda b, g, i: (b, 0, g))),
        input_output_aliases={12: 3},
        scratch_shapes=[pltpu.VMEM((4, SSD_STATE, HEAD_DIM), F32), pltpu.VMEM((4, L, L), F32),
                        pltpu.VMEM((4, L, HEAD_DIM), F32), pltpu.VMEM((L, GROUP_W), F32),
                        pltpu.VMEM((L, GROUP_W), F32)], name=name,
        compiler_params=_cp("parallel", "parallel", "arbitrary"))(
            dyn, Y, xc, xc, xc, proj, dtc, dtr, pcol, prow, nw, hs, dproj)


def _head_sel():
    sel = (np.arange(1024)[:, None] // HEAD_DIM == np.arange(ATT_HEADS)[None, :]).astype(np.float32)
    return jnp.asarray(sel, BF16), jnp.asarray(sel.T, BF16)


def _head_rms(xv, sel, selT):
    ms = _tri_sum(sel, xv * xv, NN, False, 1) * (1.0 / HEAD_DIM)
    return _tri_sum(selT, lax.rsqrt(ms + EPS), NN, False, 2)


def _headnorm_fwd(name, proj, col_block, w):
    T = proj.shape[0]
    sel, selT = _head_sel()

    def body(x_ref, w_ref, sel_ref, selT_ref, o_ref):
        xv = x_ref[...]
        o_ref[...] = (xv * _head_rms(xv, sel_ref[...], selT_ref[...]) * w_ref[...]).astype(BF16)

    full = lambda shp: pl.BlockSpec(shp, lambda i: (0, 0))
    return pl.pallas_call(
        body, out_shape=jax.ShapeDtypeStruct((T, 1024), BF16), grid=(T // ROW_T,),
        in_specs=[pl.BlockSpec((ROW_T, 1024), lambda i: (i, col_block)), full((1, 1024)), full((1024, ATT_HEADS)),
                  full((ATT_HEADS, 1024))],
        out_specs=pl.BlockSpec((ROW_T, 1024), lambda i: (i, 0)), name=name, compiler_params=_cp("parallel"))(
            proj, jnp.tile(w, (1, ATT_HEADS)), sel, selT)


def _headnorm_bwd(name, dn, proj, col_block, w, dproj):
    T = proj.shape[0]
    sel, selT = _head_sel()

    def body(dn_ref, x_ref, w_ref, sel_ref, selT_ref, buf_ref, dx_ref, dw_ref):
        @pl.when(pl.program_id(0) == 0)
        def _():
            dw_ref[...] = jnp.zeros_like(dw_ref)

        xv = x_ref[...]
        sl, slT = sel_ref[...], selT_ref[...]
        rb = _head_rms(xv, sl, slT)
        xhat = xv * rb
        dnv = dn_ref[...]
        dxhat = dnv * w_ref[...]
        mean = _tri_sum(slT, _tri_sum(sl, dxhat * xhat, NN, False, 2) * (1.0 / HEAD_DIM), NN, False, 2)
        dx_ref[...] = (rb * (dxhat - xhat * mean)).astype(BF16)
        dw_ref[...] += jnp.sum(dnv * xhat, axis=0, keepdims=True)

    here = pl.BlockSpec((ROW_T, 1024), lambda i: (i, col_block))
    full = lambda shp: pl.BlockSpec(shp, lambda i: (0, 0))
    dx, dw = pl.pallas_call(
        body, out_shape=(jax.ShapeDtypeStruct(dproj.shape, dproj.dtype), jax.ShapeDtypeStruct((1, 1024), F32)),
        grid=(T // ROW_T,),
        in_specs=[pl.BlockSpec((ROW_T, 1024), lambda i: (i, 0)), here, full((1, 1024)), full((1024, ATT_HEADS)),
                  full((ATT_HEADS, 1024)), ANY],
        out_specs=(here, full((1, 1024))), input_output_aliases={5: 0},
        name=name, compiler_params=_cp("arbitrary"))(dn, proj, jnp.tile(w, (1, ATT_HEADS)), sel, selT, dproj)
    return dx, jnp.sum(dw.reshape(ATT_HEADS, HEAD_DIM), axis=0, keepdims=True)


def _att_bias(nq):
    j = np.arange(ATT_B)[:, None]
    i = np.arange(ATT_B)[None, :]
    out = np.empty((nq, ATT_B, ATT_B), np.float32)
    for dblk in range(nq):
        dl = ATT_B * dblk + i - j
        cnt = ((dl >= 0) & (dl <= 128)).astype(np.float32)
        cnt += ((dl >= 0) & (dl % 4 == 0) & (dl <= 512))
        cnt += ((dl >= 0) & (dl % 16 == 0) & (dl <= 2048))
        out[dblk] = np.where(cnt > 0, np.log(np.maximum(cnt, 1.0)), NEG)
    return jnp.asarray(out)


def _row_pair(nq):
    def f(r, c):
        first = c <= r
        return jnp.where(first, r, nq - 1 - r), jnp.where(first, c, c - (r + 1))
    return f


def _col_pair(nq):
    def f(r, c):
        first = c < nq - r
        kj = jnp.where(first, r, nq - 1 - r)
        return jnp.where(first, r + c, nq - 1 - r + (c - (nq - r))), kj
    return f


ATT_SCALE = 1.0 / math.sqrt(HEAD_DIM)
ATT_HS = 8
ATT_W = ATT_HS * HEAD_DIM


def _att_maps(nq, qk):
    return dict(
        q_tok=lambda b, g, r, c: (b * nq + qk(r, c)[0], g),
        k_tok=lambda b, g, r, c: (b * nq + qk(r, c)[1], g),
        v_tok=lambda b, g, r, c: (b * nq + qk(r, c)[1], COL_V // ATT_W + g),
        q_feat=lambda b, g, r, c: (g, b * nq + qk(r, c)[0]),
        k_feat=lambda b, g, r, c: (g, b * nq + qk(r, c)[1]),
        bias=lambda b, g, r, c: (qk(r, c)[0] - qk(r, c)[1], 0, 0),
        lse=lambda b, g, r, c: (g, 0, b * nq + qk(r, c)[0]),
        do_tok=lambda b, g, r, c: (b * nq + qk(r, c)[0], 1024 // ATT_W + g))


def _att_fwd(name, kn, qT, vT, bias, B):
    T = kn.shape[0]
    nq = (T // B) // ATT_B
    qk = _row_pair(nq)
    mp = _att_maps(nq, qk)

    def body(k_ref, qT_ref, vT_ref, bias_ref, oT_ref, lse_ref, m_s, l_s, acc_s, s_s):
        qi, kj = qk(pl.program_id(2), pl.program_id(3))

        @pl.when(kj == 0)
        def _():
            m_s[...] = jnp.full_like(m_s, NEG)
            l_s[...] = jnp.zeros_like(l_s)
            acc_s[...] = jnp.zeros_like(acc_s)

        bv = bias_ref[...]
        for h in range(ATT_HS):
            rs = slice(HEAD_DIM * h, HEAD_DIM * (h + 1))
            s_s[h] = _dot(k_ref[:, rs], qT_ref[rs, :], NN)
        for h in range(ATT_HS):
            rs = slice(HEAD_DIM * h, HEAD_DIM * (h + 1))
            s = s_s[h] + bv
            m_prev = m_s[h:h + 1, :]
            m_new = jnp.maximum(m_prev, jnp.max(s, axis=0, keepdims=True))
            alpha = jnp.exp(m_prev - m_new)
            p = jnp.exp(s - m_new)
            l_s[h:h + 1, :] = alpha * l_s[h:h + 1, :] + jnp.sum(p, axis=0, keepdims=True)
            acc_s[rs, :] = alpha * acc_s[rs, :] + _dot(vT_ref[rs, :], p.astype(BF16), NN)
            m_s[h:h + 1, :] = m_new

        @pl.when(kj == qi)
        def _():
            for h in range(ATT_HS):
                rs = slice(HEAD_DIM * h, HEAD_DIM * (h + 1))
                oT_ref[rs, :] = (acc_s[rs, :] / l_s[h:h + 1, :]).astype(BF16)
            lse_ref[...] = m_s[...] + jnp.log(l_s[...])

    tok = (ATT_B, ATT_W)
    feat = (ATT_W, ATT_B)
    return pl.pallas_call(
        body,
        out_shape=(jax.ShapeDtypeStruct((1024, T), BF16), jax.ShapeDtypeStruct((ATT_HEADS // ATT_HS, ATT_HS, T), F32)),
        grid=(B, ATT_HEADS // ATT_HS, nq // 2, nq + 1),
        in_specs=[pl.BlockSpec(tok, mp["k_tok"]), pl.BlockSpec(feat, mp["q_feat"]), pl.BlockSpec(feat, mp["k_feat"]),
                  pl.BlockSpec((None, ATT_B, ATT_B), mp["bias"])],
        out_specs=(pl.BlockSpec(feat, mp["q_feat"]), pl.BlockSpec((None, ATT_HS, ATT_B), mp["lse"])),
        scratch_shapes=[pltpu.VMEM((ATT_HS, ATT_B), F32), pltpu.VMEM((ATT_HS, ATT_B), F32),
                        pltpu.VMEM((ATT_W, ATT_B), F32), pltpu.VMEM((ATT_HS, ATT_B, ATT_B), F32)],
        name=name, compiler_params=_cp("parallel", "parallel", "arbitrary", "arbitrary"))(kn, qT, vT, bias)


def _att_scores(k_ref, qT_ref, v_ref, doT_ref, s_s, dp_s):
    for h in range(ATT_HS):
        rs = slice(HEAD_DIM * h, HEAD_DIM * (h + 1))
        s_s[h] = _dot(k_ref[:, rs], qT_ref[rs, :], NN)
        dp_s[h] = _dot(v_ref[:, rs].astype(BF16), doT_ref[rs, :].astype(BF16), NN)


def _att_p_ds(s_s, dp_s, doT_ref, oT_ref, lse_ref, bv, h):
    rs = slice(HEAD_DIM * h, HEAD_DIM * (h + 1))
    delta = jnp.sum(doT_ref[rs, :] * oT_ref[rs, :].astype(F32), axis=0, keepdims=True)
    p = jnp.exp(s_s[h] + bv - lse_ref[h:h + 1, :])
    return p, p * (dp_s[h] - delta)


def _att_bwd(name, kn, qT, proj, qn, knT, bias, doT, oT, lse, dyn, dproj, B):
    T = kn.shape[0]
    S = T // B
    nq = S // ATT_B
    qk = _col_pair(nq)
    mp = _att_maps(nq, qk)

    def body(k_ref, qT_ref, v_ref, q_ref, kT_ref, bias_ref, doT_ref, oT_ref, lse_ref, do_ref, buf_ref,
             dqT_ref, dk_ref, dv_ref, dk_s, dv_s, dq_s, s_s, dp_s):
        r, c = pl.program_id(2), pl.program_id(3)
        qi, kj = qk(r, c)

        @pl.when((r == 0) & (c == 0))
        def _():
            dq_s[...] = jnp.zeros_like(dq_s)

        @pl.when(qi == kj)
        def _():
            dk_s[...] = jnp.zeros_like(dk_s)
            dv_s[...] = jnp.zeros_like(dv_s)

        bv = bias_ref[...]
        _att_scores(k_ref, qT_ref, v_ref, doT_ref, s_s, dp_s)
        dq_blk = dq_s.at[qi]
        for h in range(ATT_HS):
            rs = slice(HEAD_DIM * h, HEAD_DIM * (h + 1))
            p, ds = _att_p_ds(s_s, dp_s, doT_ref, oT_ref, lse_ref, bv, h)
            dsb = ds.astype(BF16)
            dv_s[h] += _dot(p.astype(BF16), do_ref[:, rs].astype(BF16), NN)
            dk_s[h] += _dot(dsb, q_ref[:, rs], NN)
            dq_blk[rs, :] += _dot(kT_ref[rs, :], dsb, NN)

        @pl.when(qi == nq - 1)
        def _():
            for h in range(ATT_HS):
                rs = slice(HEAD_DIM * h, HEAD_DIM * (h + 1))
                dk_ref[:, rs] = dk_s[h] * ATT_SCALE
                dv_ref[:, rs] = dv_s[h].astype(BF16)

        @pl.when((r == nq // 2 - 1) & (c == nq))
        def _():
            for q in range(nq):
                dqT_ref[:, ATT_B * q:ATT_B * (q + 1)] = dq_s[q] * ATT_SCALE

    tok = (ATT_B, ATT_W)
    feat = (ATT_W, ATT_B)
    v_cb = COL_V // ATT_W
    return pl.pallas_call(
        body,
        out_shape=(jax.ShapeDtypeStruct((1024, T), F32), jax.ShapeDtypeStruct((T, 1024), F32),
                   jax.ShapeDtypeStruct(dproj.shape, dproj.dtype)),
        grid=(B, ATT_HEADS // ATT_HS, nq // 2, nq + 1),
        in_specs=[pl.BlockSpec(tok, mp["k_tok"]), pl.BlockSpec(feat, mp["q_feat"]), pl.BlockSpec(tok, mp["v_tok"]),
                  pl.BlockSpec(tok, mp["q_tok"]), pl.BlockSpec(feat, mp["k_feat"]),
                  pl.BlockSpec((None, ATT_B, ATT_B), mp["bias"]),
                  pl.BlockSpec(feat, mp["q_feat"]), pl.BlockSpec(feat, mp["q_feat"]),
                  pl.BlockSpec((None, ATT_HS, ATT_B), mp["lse"]), pl.BlockSpec(tok, mp["do_tok"]), ANY],
        out_specs=(pl.BlockSpec((ATT_W, S), lambda b, g, r, c: (g, b)),
                   pl.BlockSpec(tok, mp["k_tok"]),
                   pl.BlockSpec(tok, lambda b, g, r, c: (b * nq + qk(r, c)[1], v_cb + g))),
        input_output_aliases={10: 2},
        scratch_shapes=[pltpu.VMEM((ATT_HS, ATT_B, HEAD_DIM), F32), pltpu.VMEM((ATT_HS, ATT_B, HEAD_DIM), F32),
                        pltpu.VMEM((nq, ATT_W, ATT_B), F32),
                        pltpu.VMEM((ATT_HS, ATT_B, ATT_B), F32), pltpu.VMEM((ATT_HS, ATT_B, ATT_B), F32)],
        name=name, compiler_params=_cp("parallel", "parallel", "arbitrary", "arbitrary"))(
            kn, qT, proj, qn, knT, bias, doT, oT, lse, dyn, dproj)


def _group_cols(v):
    return v.reshape(SSD_GROUPS, 4)


def _ssd_params(p):
    rows = jnp.stack([_group_cols(p["dt_bias"]), _group_cols(p["a_log"]), _group_cols(p["d_skip"])], axis=1)
    return rows, jnp.swapaxes(rows, 1, 2)


def _dymix(name, dx, wout):
    T = dx.shape[0]

    def body(dx_ref, w_ref, o_ref):
        dxb = dx_ref[...].astype(BF16)
        for n in range(N_SHARD):
            o_ref[:, MIX_SH * n:MIX_SH * (n + 1)] = _dot(dxb, w_ref[n], NT)

    return pl.pallas_call(
        body, out_shape=jax.ShapeDtypeStruct((T, MIX_W), F32), grid=(T // ROW_T,),
        in_specs=[pl.BlockSpec((ROW_T, D_MODEL), lambda i: (i, 0)),
                  pl.BlockSpec((N_SHARD, MIX_SH, D_MODEL), lambda i: (0, 0, 0))],
        out_specs=pl.BlockSpec((ROW_T, MIX_W), lambda i: (i, 0)), name=name, compiler_params=_cp("parallel"))(dx, wout)


def _mixer_fwd(tag, x1, h2, p, weights, bias, B):
    T = x1.shape[0]
    S = T // B
    nt = T // ROW_T
    wi = weights("win", h2)
    win, cw = wi["win"], wi["cw"]
    tm = min(GU_T, T)
    proj = _mm(tag + "_proj",
               [(h2, pl.BlockSpec((tm, D_MODEL), lambda j, i, k: (i, 0)),
                 win, pl.BlockSpec((D_MODEL, PROJ_TN), lambda j, i, k: (0, j)))],
               jax.ShapeDtypeStruct((T, IN_PAD), F32), pl.BlockSpec((tm, PROJ_TN), lambda j, i, k: (i, j)),
               (IN_PAD // PROJ_TN, T // tm, 1), NN, (tm, PROJ_TN))
    xc = _conv_fwd(tag + "_conv", proj, cw, p["conv_b"][None], B)
    dtraw = proj[:, COL_DT:COL_DT + SSD_HEADS].reshape(T, SSD_GROUPS, 4)
    dtc = jnp.transpose(dtraw, (1, 0, 2))
    dtr = jnp.transpose(dtraw, (1, 2, 0))
    pcol, prow = _ssd_params(p)
    Y, y_ssd, hs = _ssd_fwd(tag + "_ssd", xc, proj, dtc, dtr, pcol, prow, p["ssd_norm"][None], B)
    qn = _headnorm_fwd(tag + "_qn", proj, COL_Q // 1024, p["q_norm"][None])
    kn = _headnorm_fwd(tag + "_kn", proj, COL_K // 1024, p["k_norm"][None])
    qT = (qn * ATT_SCALE).T
    oT, lse = _att_fwd(tag + "_att", kn, qT, proj[:, COL_V:COL_V + 1024].T.astype(BF16), bias, B)
    ymix = jnp.concatenate([y_ssd, oT.T], axis=1)
    rest = weights("rest", ymix)
    o = _with_norm(T, p["ffn2_norm"][None])
    x2, h3 = _mm(tag + "_out",
                 [(ymix, pl.BlockSpec((ROW_T, MIX_SH), lambda i, n, k, j=j: (i, j)),
                   rest["wout"], pl.BlockSpec((None, MIX_SH, D_MODEL), lambda i, n, k, j=j: (j, 0, 0)))
                  for j in range(N_SHARD)],
                 o.pop("out_shape"), o.pop("out_spec"), (nt, 1, 1), NN, (ROW_T, D_MODEL), res=(x1, _row3()), **o)
    saved = dict(x1=x1, h2=h2, proj=proj, xc=xc, dtc=dtc, dtr=dtr, Y=Y, hs=hs,
                 qn=qn, kn=kn, qT=qT, oT=oT, lse=lse, ymix=ymix, win=win, cw=cw, wout=rest["wout"])
    return x2, h3, saved


def _mixer_bwd(tag, dx2, sv, p, bias, B):
    T = dx2.shape[0]
    S = T // B
    nt = T // ROW_T
    sg = {}
    dymix = _dymix(tag + "_dymix", dx2, sv["wout"])
    tkw = min(TK_W, T)
    gwout = _mm(tag + "_dwout",
                [(sv["ymix"], pl.BlockSpec((tkw, MIX_SH), lambda m, n, k: (k, m)),
                  dx2, pl.BlockSpec((tkw, D_MODEL), lambda m, n, k: (k, 0)))],
                jax.ShapeDtypeStruct((N_SHARD, MIX_SH, D_MODEL), BF16),
                pl.BlockSpec((None, MIX_SH, D_MODEL), lambda m, n, k: (m, 0, 0)),
                (N_SHARD, 1, T // tkw), TN, (MIX_SH, D_MODEL))
    proj = sv["proj"]
    doT = dymix[:, 1024:].T
    dproj = lax.empty((T, IN_PAD), BF16)
    dqT, dkn, dproj = _att_bwd(tag + "_attb", sv["kn"], sv["qT"], proj, sv["qn"], sv["kn"].T, bias, doT, sv["oT"],
                               sv["lse"], dymix, dproj, B)
    dproj, sg["q_norm"] = _headnorm_bwd(tag + "_qnb", dqT.T, proj, COL_Q // 1024, p["q_norm"][None], dproj)
    dproj, sg["k_norm"] = _headnorm_bwd(tag + "_knb", dkn, proj, COL_K // 1024, p["k_norm"][None], dproj)
    pcol, prow = _ssd_params(p)
    dxs, dB, dC, dproj, ddt, dpar, dnw = _ssd_bwd(tag + "_ssdb", dymix, sv["Y"], sv["xc"], proj, sv["dtc"], sv["dtr"],
                                                  pcol, prow, p["ssd_norm"][None], sv["hs"], dproj, B)
    dpar = jnp.sum(dpar, axis=0)
    sg["dt_bias"] = dpar[:, 0, :].reshape(SSD_HEADS)
    sg["a_log"] = dpar[:, 1, :].reshape(SSD_HEADS)
    sg["d_skip"] = dpar[:, 2, :].reshape(SSD_HEADS)
    sg["ssd_norm"] = jnp.sum(dnw, axis=0)
    dproj, sg["conv_w"], sg["conv_b"] = _conv_bwd(tag + "_convb", proj, dxs, dB, dC, sv["cw"], p["conv_b"][None],
                                                  dproj, B)
    ddt16 = jnp.transpose(ddt, (1, 0, 2)).reshape(T, SSD_HEADS)
    dproj = lax.dynamic_update_slice(dproj, jnp.pad(ddt16, ((0, 0), (0, IN_PAD - COL_DT - SSD_HEADS))).astype(BF16),
                                     (0, COL_DT))
    win = sv["win"]
    gwin = _mm(tag + "_dwin",
               [(sv["h2"], pl.BlockSpec((tkw, D_MODEL), lambda n, m, k: (k, 0)),
                 dproj, pl.BlockSpec((tkw, PROJ_TN), lambda n, m, k: (k, n)))],
               jax.ShapeDtypeStruct((D_MODEL, IN_PAD), BF16), pl.BlockSpec((D_MODEL, PROJ_TN), lambda n, m, k: (0, n)),
               (IN_PAD // PROJ_TN, 1, T // tkw), TN, (D_MODEL, PROJ_TN))
    dx1, sg["mix_norm"] = _mm(
        tag + "_dh2",
        [(dproj, pl.BlockSpec((ROW_T, PROJ_TN), lambda i, n, k, j=j: (i, j)),
          win, pl.BlockSpec((D_MODEL, PROJ_TN), lambda i, n, k, j=j: (0, j))) for j in range(IN_PAD // PROJ_TN)],
        (jax.ShapeDtypeStruct((T, D_MODEL), F32), jax.ShapeDtypeStruct((1, D_MODEL), F32)), (_row3(), _vec3()),
        (nt, 1, 1), NT, (ROW_T, D_MODEL), post="rmsb",
        post_in=[(sv["x1"], _row3()), (p["mix_norm"][None], _vec3()), (dx2, _row3())])
    return dx1, sg, gwout, gwin


def _win_pack(w):
    return jnp.concatenate([w[:, :3072], w[:, 3088:], w[:, 3072:3088],
                            jnp.zeros((w.shape[0], IN_PAD - IN_PROJ), w.dtype)], axis=1)


def _win_unpack(g):
    return jnp.concatenate([g[:, :3072], g[:, COL_DT:COL_DT + SSD_HEADS], g[:, 3072:COL_DT]], axis=1)


DT_LO = IN_SH * 2 - COL_Q


def _win_from_shards(sh):
    main = IN_SH - DT_LO
    return jnp.concatenate([sh[0], sh[1][:, :main], sh[2][:, SSD_HEADS - DT_LO:], sh[3], sh[1][:, main:],
                            sh[2][:, :SSD_HEADS - DT_LO], jnp.zeros((sh.shape[1], IN_PAD - IN_PROJ), sh.dtype)], axis=1)


def _win_to_shards(g):
    main = IN_SH - DT_LO
    a, b = IN_SH + main, IN_SH + 2 * main
    return jnp.stack([g[:, :IN_SH],
                      jnp.concatenate([g[:, IN_SH:a], g[:, COL_DT:COL_DT + DT_LO]], axis=1),
                      jnp.concatenate([g[:, COL_DT + DT_LO:COL_DT + SSD_HEADS], g[:, a:b]], axis=1),
                      g[:, b:COL_DT]])


def _local_step(x, target, small, weights, scatter, B):
    T = x.shape[0]
    bias = _att_bias((T // B) // ATT_B)
    saved = []
    xl = x
    hl = _rms_fwd("l0f1_rms", x, small["ffn1_norm"][0][None])
    for l in range(DEPTH):
        tag = "l%d" % l
        p = {k: v[l] for k, v in small.items()}
        w1 = weights(l, "ffn1", hl)
        (x1, h2), ffn1, d1 = _ffn_fwd(tag + "f1", xl, hl, w1["g1"], w1["u1"],
                                      lambda after, l=l: weights(l, "ffn1d", after)["d1"], p["mix_norm"][None])
        x2, h3, sv = _mixer_fwd(tag, x1, h2, p, functools.partial(weights, l), bias, B)
        w2 = weights(l, "rest", x2)
        nxt = small["ffn1_norm"][l + 1][None] if l + 1 < DEPTH else None
        xo, ffn2, _ = _ffn_fwd(tag + "f2", x2, h3, w2["g2"], w2["u2"], w2["d2"], nxt)
        xl, hl = xo if nxt is not None else (xo, None)
        saved.append((ffn1, sv, ffn2, dict(g1=w1["g1"], u1=w1["u1"], d1=d1), w2))
    d, lsum = _loss_grad("loss", xl, target)
    sgrads = [None] * DEPTH
    for l in reversed(range(DEPTH)):
        tag = "l%db" % l
        p = {k: v[l] for k, v in small.items()}
        ffn1, sv, ffn2, w1, w2 = saved[l]
        sg = {}
        d, sg["ffn2_norm"] = _ffn_bwd(tag + "f2", d, ffn2, p["ffn2_norm"][None], w2["g2"], w2["u2"], w2["d2"],
                                      lambda gg, gu, gd, c, l=l: scatter(l, "ffn2", dict(g2=gg, u2=gu, d2=gd), c))
        d, sgm, gwout, gwin = _mixer_bwd(tag, d, sv, p, bias, B)
        sg.update(sgm)
        d = scatter(l, "mixer", dict(wout=gwout, win=gwin), d)
        d, sg["ffn1_norm"] = _ffn_bwd(tag + "f1", d, ffn1, p["ffn1_norm"][None], w1["g1"], w1["u1"], w1["d1"],
                                      lambda gg, gu, gd, c, l=l: scatter(l, "ffn1", dict(g1=gg, u1=gu, d1=gd), c))
        sgrads[l] = sg
    return lsum, d, sgrads


MESH = pl.DeviceIdType.MESH
ANY = pl.BlockSpec(memory_space=pl.ANY)


def _place():
    return lax.axis_index("x"), lax.axis_index("y"), lax.axis_index("c")


def _other_chips(x, y):
    return [(1 - x, y), (x, 1 - y), (1 - x, 1 - y)]


HBM = pl.BlockSpec(memory_space=pltpu.HBM)
SEM = pl.BlockSpec(memory_space=pltpu.SEMAPHORE)
EFFECT = pltpu.SideEffectType.DATAFLOW_SIDE_EFFECTING


def _hbm(a):
    return pltpu.with_memory_space_constraint(a, pltpu.HBM)


def _my_half(ref, c):
    hr = ref.shape[0] // 2
    return ref.at[pl.ds(pl.multiple_of(c * hr, 16), hr)]


def _exchange(gather, layer, halves, src, land, send, recv, n, act):
    x, y, c = _place()
    for k, (px, py) in enumerate(_other_chips(x, y)):
        for a in range(n):
            if gather:
                s_out, d_out, d_in = src[a].at[layer], land[a].at[2 * x + y], land[a].at[2 * px + py]
                if halves is not None and halves[a]:
                    s_out, d_out, d_in = _my_half(s_out, c), _my_half(d_out, c), _my_half(d_in, c)
            else:
                s_out, d_out, d_in = src[a].at[2 * px + py], land[a].at[k], land[a].at[k]
            act(pltpu.make_async_remote_copy(
                src_ref=s_out, dst_ref=d_out if act is _start else d_in, send_sem=send.at[k * n + a],
                recv_sem=recv.at[k * n + a], device_id=(px, py, c), device_id_type=MESH))


def _start(cp):
    cp.start()


def _finish(cp):
    cp.wait_send()
    cp.wait_recv()


def _exchange_start(name, gather, layer, srcs, carry, halves=None):
    n = len(srcs)
    lands = [lax.empty(((N_SHARD,) + s.shape[1:]) if gather else ((3,) + s.shape[1:]), s.dtype) for s in srcs]

    def body(*refs):
        _exchange(gather, layer, halves, refs[:n], refs[n:2 * n], refs[2 * n + 1], refs[2 * n + 2], n, _start)

    srcs = [_hbm(a) for a in srcs]
    thru = [_hbm(a) for a in lands + [carry]]
    out = pl.pallas_call(
        body, name=name,
        out_shape=(pltpu.SemaphoreType.DMA((3 * n,)), pltpu.SemaphoreType.DMA((3 * n,)),
                   *[pltpu.HBM(a.shape, a.dtype) for a in thru]),
        in_specs=[HBM] * (2 * n + 1), out_specs=(SEM, SEM, *[HBM] * (n + 1)),
        input_output_aliases={n + i: 2 + i for i in range(n + 1)},
        compiler_params=pltpu.CompilerParams(has_side_effects=EFFECT))(*srcs, *thru)
    return dict(gather=gather, layer=layer, halves=halves, send=out[0], recv=out[1], srcs=srcs,
                lands=list(out[2:2 + n])), out[-1]


def _exchange_wait(name, ex, after):
    n = len(ex["srcs"])

    def body(*refs):
        _exchange(ex["gather"], ex["layer"], ex["halves"], refs[:n], refs[n:2 * n], refs[2 * n], refs[2 * n + 1], n,
                  _finish)

    out = pl.pallas_call(
        body, name=name, out_shape=[pltpu.HBM(a.shape, a.dtype) for a in ex["lands"]],
        in_specs=[HBM] * (2 * n) + [SEM, SEM, ANY], out_specs=[HBM] * n,
        input_output_aliases={n + i: i for i in range(n)},
        compiler_params=pltpu.CompilerParams(has_side_effects=EFFECT))(
            *ex["srcs"], *ex["lands"], ex["send"], ex["recv"], after)
    return list(out)


def _sibling_fill(name, lands):
    n = len(lands)

    def body(*refs):
        land = refs[:n]
        send, recv = refs[2 * n], refs[2 * n + 1]
        x, y, c = _place()
        for k, (px, py) in enumerate(_other_chips(x, y)):
            for a in range(n):
                slot = land[a].at[2 * px + py]
                pltpu.make_async_remote_copy(src_ref=_my_half(slot, c), dst_ref=_my_half(slot, c),
                                             send_sem=send.at[k * n + a], recv_sem=recv.at[k * n + a],
                                             device_id=(x, y, 1 - c), device_id_type=MESH).start()
        for k, (px, py) in enumerate(_other_chips(x, y)):
            for a in range(n):
                slot = land[a].at[2 * px + py]
                cp = pltpu.make_async_remote_copy(src_ref=_my_half(slot, c), dst_ref=_my_half(slot, 1 - c),
                                                  send_sem=send.at[k * n + a], recv_sem=recv.at[k * n + a],
                                                  device_id=(x, y, 1 - c), device_id_type=MESH)
                cp.wait_recv()
                cp.wait_send()

    return pl.pallas_call(
        body, out_shape=[jax.ShapeDtypeStruct(a.shape, a.dtype) for a in lands],
        in_specs=[ANY] * n, out_specs=[ANY] * n, input_output_aliases={i: i for i in range(n)},
        scratch_shapes=[pltpu.SemaphoreType.DMA((3 * n,)), pltpu.SemaphoreType.DMA((3 * n,))],
        name=name)(*lands)


def _swap_sibling(name, parts):
    n = len(parts)

    def body(*refs):
        src, dst = refs[:n], refs[n:2 * n]
        send, recv = refs[2 * n:]
        x, y, c = _place()
        cps = [pltpu.make_async_remote_copy(src_ref=src[a], dst_ref=dst[a], send_sem=send.at[a], recv_sem=recv.at[a],
                                            device_id=(x, y, 1 - c), device_id_type=MESH) for a in range(n)]
        for cp in cps:
            cp.start()
        for cp in cps:
            cp.wait_recv()
        for cp in cps:
            cp.wait_send()

    return pl.pallas_call(
        body, out_shape=[jax.ShapeDtypeStruct(p.shape, p.dtype) for p in parts],
        in_specs=[ANY] * n, out_specs=[ANY] * n,
        scratch_shapes=[pltpu.SemaphoreType.DMA((n,)), pltpu.SemaphoreType.DMA((n,))],
        name=name)(*parts)


def _allreduce_small(name, v, after):
    R = v.shape[0]

    def body(v_ref, after_ref, o_ref, buf, send, recv):
        x, y, c = _place()
        me = 4 * x + 2 * y + c
        buf[me] = v_ref[...]
        cps = []
        for k in range(1, 8):
            fx, fy, fc = (k >> 2) & 1, (k >> 1) & 1, k & 1
            px = 1 - x if fx else x
            py = 1 - y if fy else y
            pc = 1 - c if fc else c
            cp = pltpu.make_async_remote_copy(src_ref=v_ref, dst_ref=buf.at[me], send_sem=send.at[k - 1],
                                              recv_sem=recv.at[k - 1], device_id=(px, py, pc), device_id_type=MESH)
            cp.start()
            cps.append((cp, 4 * px + 2 * py + pc))
        for k, (cp, peer) in enumerate(cps):
            pltpu.make_async_remote_copy(src_ref=v_ref, dst_ref=buf.at[peer], send_sem=send.at[k], recv_sem=recv.at[k],
                                         device_id=(x, y, c), device_id_type=MESH).wait_recv()
        for cp, _ in cps:
            cp.wait_send()
        acc = buf[0]
        for d in range(1, 8):
            acc = acc + buf[d]
        o_ref[...] = acc

    return pl.pallas_call(
        body, out_shape=jax.ShapeDtypeStruct((R, 128), F32),
        in_specs=[pl.BlockSpec(memory_space=pltpu.VMEM), ANY], out_specs=pl.BlockSpec(memory_space=pltpu.VMEM),
        scratch_shapes=[pltpu.VMEM((8, R, 128), F32), pltpu.SemaphoreType.DMA((7,)), pltpu.SemaphoreType.DMA((7,))],
        name=name)(v, after)


TILE_BYTES = 1600 * 1024


def _row_tile(r, c=1024):
    for t in (512, 352, 256, 128, 64, 32, 16, 8):
        if r % t == 0 and (t * c * 4 <= TILE_BYTES or t == 8):
            return t
    raise ValueError(r)


def _sum4(name, me, parts, got):
    _, R, C = parts.shape
    tr = _row_tile(R, C)

    def body(me_ref, o_ref, g_ref, s_ref):
        s = o_ref[...].astype(F32)
        for k in range(3):
            s = s + g_ref[k].astype(F32)
        s_ref[...] = s.astype(BF16)

    return pl.pallas_call(
        body, out_shape=jax.ShapeDtypeStruct((R, C), BF16),
        grid_spec=pltpu.PrefetchScalarGridSpec(
            num_scalar_prefetch=1, grid=(R // tr,),
            in_specs=[pl.BlockSpec((None, tr, C), lambda i, me_ref: (me_ref[0], i, 0)),
                      pl.BlockSpec((3, tr, C), lambda i, me_ref: (0, i, 0))],
            out_specs=pl.BlockSpec((tr, C), lambda i, me_ref: (i, 0))),
        name=name, compiler_params=_cp("parallel"))(me, parts, got)


def _adamw(name, w, gparts, m, v):
    R, C = w.shape
    tr = _row_tile(R, C)
    ng = len(gparts)
    c1 = 1.0 - ADAM_B1 ** ADAM_STEP
    c2 = 1.0 - ADAM_B2 ** ADAM_STEP

    def body(*refs):
        w_ref = refs[0]
        g_refs = refs[1:1 + ng]
        m_ref, v_ref, go_ref, d_ref, mo_ref, vo_ref = refs[1 + ng:]
        g = g_refs[0][...]
        for r in g_refs[1:]:
            g = g + r[...]
        mn = ADAM_B1 * m_ref[...] + (1.0 - ADAM_B1) * g
        vn = ADAM_B2 * v_ref[...] + (1.0 - ADAM_B2) * (g * g)
        go_ref[...] = g
        mo_ref[...] = mn
        vo_ref[...] = vn
        d_ref[...] = -ADAM_LR * ((mn / c1) / (jnp.sqrt(vn / c2) + ADAM_EPS) + ADAM_WD * w_ref[...])

    blk = pl.BlockSpec((tr, C), lambda i: (i, 0))
    osh = jax.ShapeDtypeStruct((R, C), F32)
    return pl.pallas_call(
        body, out_shape=(osh, osh, osh, osh), grid=(R // tr,), in_specs=[blk] * (3 + ng), out_specs=(blk,) * 4,
        name=name, compiler_params=_cp("parallel"))(w, *gparts, m, v)


def _adamw_layers(name, w, sums, m, v):
    _, R, C = w.shape
    tr = _row_tile(R, C)
    nr = R // tr
    c1 = 1.0 - ADAM_B1 ** ADAM_STEP
    c2 = 1.0 - ADAM_B2 ** ADAM_STEP

    def body(w_ref, a0, b0, a1, b1, m_ref, v_ref, go_ref, d_ref, mo_ref, vo_ref):
        f = lambda r: r[...].astype(F32)
        g = jnp.where(pl.program_id(0) == 0, f(a0) + f(b0), f(a1) + f(b1))
        mn = ADAM_B1 * m_ref[...] + (1.0 - ADAM_B1) * g
        vn = ADAM_B2 * v_ref[...] + (1.0 - ADAM_B2) * (g * g)
        go_ref[...] = g
        mo_ref[...] = mn
        vo_ref[...] = vn
        d_ref[...] = -ADAM_LR * ((mn / c1) / (jnp.sqrt(vn / c2) + ADAM_EPS) + ADAM_WD * w_ref[...])

    blk = pl.BlockSpec((None, tr, C), lambda l, i: (l, i, 0))
    lay0 = pl.BlockSpec((tr, C), lambda l, i: (jnp.where(l == 0, i, nr - 1), 0))
    lay1 = pl.BlockSpec((tr, C), lambda l, i: (jnp.where(l == 1, i, 0), 0))
    oblk = pl.BlockSpec((tr, C), lambda l, i: (l * nr + i, 0))
    osh = jax.ShapeDtypeStruct((DEPTH * R, C), F32)
    res = pl.pallas_call(
        body, out_shape=(osh, osh, osh, osh), grid=(DEPTH, nr),
        in_specs=[blk, lay0, lay0, lay1, lay1, blk, blk], out_specs=(oblk,) * 4,
        name=name, compiler_params=_cp("arbitrary", "arbitrary"))(w, *sums[0], *sums[1], m, v)
    return [r.reshape(w.shape) for r in res]


BIG = [("ffn1_w_gate", "g1"), ("ffn1_w_up", "u1"), ("ffn1_w_down", "d1"), ("w_in", "win"), ("w_out", "wout"),
       ("ffn2_w_gate", "g2"), ("ffn2_w_up", "u2"), ("ffn2_w_down", "d2")]
SMALL = ["ffn1_norm", "mix_norm", "conv_b", "dt_bias", "a_log", "d_skip", "ssd_norm", "q_norm", "k_norm", "ffn2_norm"]
WEIGHTS = ["ffn1_norm", "ffn1_w_gate", "ffn1_w_up", "ffn1_w_down", "mix_norm", "w_in", "conv_w", "conv_b", "dt_bias",
           "a_log", "d_skip", "ssd_norm", "q_norm", "k_norm", "w_out", "ffn2_norm", "ffn2_w_gate", "ffn2_w_up",
           "ffn2_w_down"]
CONV_SH = CONV_DIM // N_SHARD
TRANSPOSED = ("g1", "u1", "g2", "u2")
GATHER_GROUPS = [(0, "ffn1", ["g1", "u1"]), (0, "ffn1d", ["d1"]), (0, "win", ["win", "cw"]),
                 (0, "rest", ["wout", "g2", "u2", "d2"]),
                 (1, "all", ["g1", "u1", "d1", "win", "cw", "wout", "g2", "u2", "d2"])]


def _pad128(v):
    v = v.reshape(-1)
    return jnp.pad(v, (0, (-v.shape[0]) % 128))


def _pack(pieces):
    flat, offs, pos = [], [], 0
    for p in pieces:
        q = _pad128(p.astype(F32))
        offs.append(pos)
        pos += q.shape[0] // 128
        flat.append(q)
    total = -(-pos // 8) * 8
    out = jnp.concatenate(flat + [jnp.zeros(((total - pos) * 128,), F32)]).reshape(total, 128)
    return out, offs


def _unpack(packed, offs, shapes):
    out = []
    for off, shp in zip(offs, shapes):
        n = int(np.prod(shp))
        rows = -(-n // 128)
        out.append(packed[off:off + rows].reshape(-1)[:n].reshape(shp))
    return out


def kernel(x, ffn1_norm, ffn1_w_gate, ffn1_w_up, ffn1_w_down, mix_norm, w_in, conv_w, conv_b, dt_bias, a_log, d_skip, ssd_norm, q_norm, k_norm, w_out, ffn2_norm, ffn2_w_gate, ffn2_w_up, ffn2_w_down, loss_target, m_ffn1_norm, m_ffn1_w_gate, m_ffn1_w_up, m_ffn1_w_down, m_mix_norm, m_w_in, m_conv_w, m_conv_b, m_dt_bias, m_a_log, m_d_skip, m_ssd_norm, m_q_norm, m_k_norm, m_w_out, m_ffn2_norm, m_ffn2_w_gate, m_ffn2_w_up, m_ffn2_w_down, v_ffn1_norm, v_ffn1_w_gate, v_ffn1_w_up, v_ffn1_w_down, v_mix_norm, v_w_in, v_conv_w, v_conv_b, v_dt_bias, v_a_log, v_d_skip, v_ssd_norm, v_q_norm, v_k_norm, v_w_out, v_ffn2_norm, v_ffn2_w_gate, v_ffn2_w_up, v_ffn2_w_down):
    A = dict(locals())
    ix, iy, ic = _place()
    me = 2 * ix + iy
    B, S, _ = x.shape
    T = B * S

    view = lambda a, key: jnp.swapaxes(a, 1, 2) if key in TRANSPOSED else a
    own = {key: view(A[name], key).astype(BF16) for name, key in BIG}
    own["cw"] = conv_w
    exs, first_norm = [], ffn1_norm
    split = lambda l, key: l == 0 and key != "cw"
    for gi, (l, _, keys) in enumerate(GATHER_GROUPS):
        ex, first_norm = _exchange_start("gather_start%d" % gi, True, l, [own[key] for key in keys], first_norm,
                                         [split(l, key) for key in keys])
        exs.append(ex)
    landed = {}

    def weights(l, group, after):
        gi = [i for i, (gl, gname, _) in enumerate(GATHER_GROUPS) if gl == l and gname in (group, "all")][0]
        if gi not in landed:
            lands = _exchange_wait("gather_wait%d" % gi, exs[gi], after)
            keys = GATHER_GROUPS[gi][2]
            halved = [i for i, key in enumerate(keys) if split(l, key)]
            if halved:
                for i, whole in zip(halved, _sibling_fill("gather_fill%d" % gi, [lands[i] for i in halved])):
                    lands[i] = whole
            landed[gi] = {}
            for key, land in zip(GATHER_GROUPS[gi][2], lands):
                full = lax.dynamic_update_slice(land, own[key][l][None], (me, 0, 0))
                if key == "win":
                    full = _win_from_shards(full)
                if key == "cw":
                    full = jnp.transpose(full, (1, 0, 2)).reshape(CONV_K, CONV_DIM)
                landed[gi][key] = full
        return landed[gi]

    pending = []

    def scatter(l, group, grads, carry):
        keys = sorted(grads)
        arrs = [grads[key] for key in keys]
        if "win" in grads:
            arrs[keys.index("win")] = _win_to_shards(grads["win"])
        ex, carry = _exchange_start("scatter_start_l%d_%s" % (l, group), False, None, arrs, carry)
        pending.append((l, keys, ex))
        return carry

    small = {name: A[name] for name in SMALL}
    small["ffn1_norm"] = first_norm
    lsum, dx, sgrads = _local_step(x.reshape(T, D_MODEL), loss_target.reshape(T, D_MODEL), small, weights, scatter, B)

    names = SMALL + ["conv_w"]
    shapes = [A[n].shape for n in SMALL] + [(DEPTH, CONV_K, CONV_DIM), ()]
    pieces = [jnp.stack([sgrads[l][n].reshape(shp[1:]) for l in range(DEPTH)]) for n, shp in zip(names, shapes)]
    pieces.append(0.5 / D_MODEL * jnp.sum(lsum))
    packed, offs = _pack(pieces)

    sums, theirs, out = {}, {}, {}
    me1 = jnp.reshape(me, (1,)).astype(jnp.int32)

    def update(tag, after):
        todo = [k for k in sums if k not in theirs]
        theirs.update(zip(todo, _swap_sibling("swap_sibling_" + tag, [sums[k] for k in todo])))
        for name, key in BIG:
            if name not in out and all((key, l) in theirs for l in range(DEPTH)):
                res = _adamw_layers("adamw_" + key, view(A[name], key),
                                    [(sums[key, l], theirs[key, l]) for l in range(DEPTH)],
                                    view(A["m_" + name], key), view(A["v_" + name], key))
                out[name] = [view(r, key) for r in res]
                after = res[0]
        return after

    after = dx
    for idx, (l, keys, ex) in enumerate(pending):
        if idx == len(pending) - 1:
            after = update("a", after)
        lands = _exchange_wait("scatter_wait%d" % idx, ex, after)
        for key, g, got in zip(keys, ex["srcs"], lands):
            sums[key, l] = after = _sum4("sum_%s_l%d" % (key, l), me1, g, got)
    after = update("b", after)

    red = _unpack(_allreduce_small("allreduce_small", packed, after), offs, shapes)
    loss = red[-1]
    sg = dict(zip(names, red[:-1]))

    wp, offs = _pack([A[n] for n in SMALL])
    gp, _ = _pack([sg[n] for n in SMALL])
    mp, _ = _pack([A["m_" + n] for n in SMALL])
    vp, _ = _pack([A["v_" + n] for n in SMALL])
    res = _adamw("adamw_small", wp, [gp], mp, vp)
    shapes = [A[n].shape for n in SMALL]
    res = [_unpack(r, offs, shapes) for r in res]
    for i, n in enumerate(SMALL):
        out[n] = [res[q][i] for q in range(4)]
    gcw = lax.dynamic_slice_in_dim(sg["conv_w"], me * CONV_SH, CONV_SH, axis=2)
    flat = lambda a: a.reshape(DEPTH * CONV_K, CONV_SH)
    res = _adamw("adamw_conv_w", flat(conv_w), [flat(gcw)], flat(m_conv_w), flat(v_conv_w))
    out["conv_w"] = [r.reshape(conv_w.shape) for r in res]

    outs = [loss, dx.reshape(B, S, D_MODEL)]
    for q in range(4):
        outs += [out[n][q] for n in WEIGHTS]
    return tuple(outs)
```

```python
import functools
import math

import numpy as np
import jax
import jax.numpy as jnp
from jax import lax
from jax.experimental import pallas as pl
from jax.experimental.pallas import tpu as pltpu

F32 = jnp.float32
BF16 = jnp.bfloat16

D_MODEL = 1024
DEPTH = 2
N_SHARD = 4
D_FF = 2816
FF_SH = D_FF // N_SHARD
SSD_HEADS = 16
HEAD_DIM = 64
SSD_GROUPS = 4
GROUP_W = 256
SSD_STATE = 128
CONV_K = 4
CONV_DIM = 2048
ATT_HEADS = 16
MIX_W = 2048
MIX_SH = MIX_W // N_SHARD
IN_PROJ = 6160
IN_SH = IN_PROJ // N_SHARD
IN_PAD = 6272
PROJ_TN = 896
COL_Z, COL_XBC, COL_Q, COL_K, COL_V, COL_DT = 0, 1024, 3072, 4096, 5120, 6144
EPS = 1e-6
NEG = -1e30
SSD_L = 512
ATT_B = 512
ROW_T = 512
HALF_T = ROW_T // 2
GU_T = 1024
TK_W = 2048
CONV_CT = 256
CONV_R = 256
PAD_R = 8

ADAM_LR, ADAM_B1, ADAM_B2, ADAM_EPS, ADAM_WD, ADAM_STEP = 0.001, 0.9, 0.999, 1e-08, 0.01, 10

NN = (((1,), (0,)), ((), ()))
NT = (((1,), (1,)), ((), ()))
TN = (((0,), (0,)), ((), ()))

VMEM_LIMIT = 56 * 1024 * 1024


def _cp(*sem):
    return pltpu.CompilerParams(dimension_semantics=sem, vmem_limit_bytes=VMEM_LIMIT)


def _dot(a, b, dims):
    return lax.dot_general(a, b, dims, preferred_element_type=F32)


def _sigmoid(x):
    return 0.5 * jnp.tanh(0.5 * x) + 0.5


def _softplus(x):
    return jnp.maximum(x, 0.0) + jnp.log(1.0 + jnp.exp(-jnp.abs(x)))


def _mm(name, pairs, out_shape, out_spec, grid, dims, acc_shape, res=None, scale=1.0, post=None, post_in=()):
    nk = grid[2]
    npair = len(pairs)
    npost = len(post_in)

    def body(*refs):
        ab = refs[:2 * npair]
        pos = 2 * npair
        res_ref = None
        if res is not None:
            res_ref = refs[pos]
            pos += 1
        pin = refs[pos:pos + npost]
        pos += npost
        out_ref = refs[pos]
        pos += 1
        if post is not None:
            out2_ref = refs[pos]
            pos += 1
        if post == "rmsb":
            out3_ref = refs[pos]
            pos += 1
        s = None
        for p in range(npair):
            d = _dot(ab[2 * p][...].astype(BF16), ab[2 * p + 1][...].astype(BF16), dims)
            s = d if s is None else s + d

        def finish(r):
            if scale != 1.0:
                r = r * scale
            if res_ref is not None:
                r = r + res_ref[...]
            if post == "rmsb":
                @pl.when(pl.program_id(0) == 0)
                def _():
                    out2_ref[...] = jnp.zeros_like(out2_ref)

                xv = pin[0][...]
                rr = lax.rsqrt(jnp.mean(xv * xv, axis=-1, keepdims=True) + EPS)
                xhat = xv * rr
                dxhat = r * pin[1][...]
                dx = pin[2][...] + rr * (dxhat - xhat * jnp.mean(dxhat * xhat, axis=-1, keepdims=True))
                out_ref[...] = dx
                out2_ref[...] += jnp.sum(r * xhat, axis=0, keepdims=True)
                out3_ref[...] = dx.astype(BF16)
                return
            out_ref[...] = r.astype(out_ref.dtype)
            if post == "norm":
                rr = lax.rsqrt(jnp.mean(r * r, axis=-1, keepdims=True) + EPS)
                out2_ref[...] = (r * rr * pin[0][...]).astype(BF16)

        if nk == 1:
            finish(s)
            return
        acc = refs[pos]
        k = pl.program_id(2)

        @pl.when(k == 0)
        def _():
            acc[...] = s

        @pl.when(k > 0)
        def _():
            acc[...] += s

        @pl.when(k == nk - 1)
        def _():
            finish(acc[...])

    args, specs = [], []
    for a, a_spec, b, b_spec in pairs:
        args += [a, b]
        specs += [a_spec, b_spec]
    for arr, spec in ([res] if res is not None else []) + list(post_in):
        args.append(arr)
        specs.append(spec)
    sems = ("arbitrary",) * 3 if post == "rmsb" else ("parallel", "parallel", "arbitrary")
    return pl.pallas_call(
        body, out_shape=out_shape, grid=grid, in_specs=specs, out_specs=out_spec,
        scratch_shapes=[] if nk == 1 else [pltpu.VMEM(acc_shape, F32)], name=name,
        compiler_params=_cp(*sems))(*args)


def _rms_fwd(name, x, w):
    T = x.shape[0]

    def body(x_ref, w_ref, o_ref):
        xv = x_ref[...]
        r = lax.rsqrt(jnp.mean(xv * xv, axis=-1, keepdims=True) + EPS)
        o_ref[...] = (xv * r * w_ref[...]).astype(BF16)

    return pl.pallas_call(
        body, out_shape=jax.ShapeDtypeStruct((T, D_MODEL), BF16), grid=(T // ROW_T,),
        in_specs=[pl.BlockSpec((ROW_T, D_MODEL), lambda i: (i, 0)), pl.BlockSpec((1, D_MODEL), lambda i: (0, 0))],
        out_specs=pl.BlockSpec((ROW_T, D_MODEL), lambda i: (i, 0)), name=name, compiler_params=_cp("parallel"))(x, w)


def _loss_grad(name, y, t):
    T = y.shape[0]

    def body(y_ref, t_ref, dy_ref, dyb_ref, l_ref):
        @pl.when(pl.program_id(0) == 0)
        def _():
            l_ref[...] = jnp.zeros_like(l_ref)

        e = y_ref[...] - t_ref[...]
        dy = e * (1.0 / D_MODEL)
        dy_ref[...] = dy
        dyb_ref[...] = dy.astype(BF16)
        l_ref[...] += jnp.sum(e * e, axis=0, keepdims=True)

    row = pl.BlockSpec((ROW_T, D_MODEL), lambda i: (i, 0))
    vec = pl.BlockSpec((1, D_MODEL), lambda i: (0, 0))
    return pl.pallas_call(
        body, out_shape=(jax.ShapeDtypeStruct((T, D_MODEL), F32), jax.ShapeDtypeStruct((T, D_MODEL), BF16),
                         jax.ShapeDtypeStruct((1, D_MODEL), F32)),
        grid=(T // ROW_T,), in_specs=[row, row], out_specs=(row, row, vec), name=name,
        compiler_params=_cp("arbitrary"))(y, t)


def _ffn_gate_up(name, h, wg, wu):
    T = h.shape[0]
    tm = min(GU_T, T)

    def body(h_ref, wg_ref, wu_ref, dgf_ref, duf_ref, a_ref):
        for r in range(0, tm, HALF_T):
            rows = slice(r, r + HALF_T)
            hv = h_ref[rows, :]
            g = _dot(hv, wg_ref[...], NT)
            u = _dot(hv, wu_ref[...], NT)
            sg = _sigmoid(g)
            silu = g * sg
            dgf_ref[rows, :] = (u * (sg * (1.0 + g * (1.0 - sg)))).astype(BF16)
            duf_ref[rows, :] = silu.astype(BF16)
            a_ref[rows, :] = (silu * u).astype(BF16)

    wspec = pl.BlockSpec((None, FF_SH, D_MODEL), lambda j, i: (j, 0, 0))
    ospec = pl.BlockSpec((None, tm, FF_SH), lambda j, i: (j, i, 0))
    osh = jax.ShapeDtypeStruct((N_SHARD, T, FF_SH), BF16)
    return pl.pallas_call(
        body, out_shape=(osh, osh, osh), grid=(N_SHARD, T // tm),
        in_specs=[pl.BlockSpec((tm, D_MODEL), lambda j, i: (i, 0)), wspec, wspec],
        out_specs=(ospec, ospec, ospec), name=name, compiler_params=_cp("parallel", "parallel"))(h, wg, wu)


def _ffn_dact(name, dx, wd, g, u):
    T = dx.shape[0]
    tm = min(GU_T, T)

    def body(dx_ref, wd_ref, g_ref, u_ref, dg_ref, du_ref):
        for r in range(0, tm, HALF_T):
            rows = slice(r, r + HALF_T)
            da = 0.5 * _dot(dx_ref[rows, :].astype(BF16), wd_ref[...], NT)
            dg_ref[rows, :] = (da * g_ref[rows, :].astype(F32)).astype(BF16)
            du_ref[rows, :] = (da * u_ref[rows, :].astype(F32)).astype(BF16)

    aspec = pl.BlockSpec((None, tm, FF_SH), lambda j, i: (j, i, 0))
    osh = jax.ShapeDtypeStruct((N_SHARD, T, FF_SH), BF16)
    return pl.pallas_call(
        body, out_shape=(osh, osh), grid=(N_SHARD, T // tm),
        in_specs=[pl.BlockSpec((tm, D_MODEL), lambda j, i: (i, 0)),
                  pl.BlockSpec((None, FF_SH, D_MODEL), lambda j, i: (j, 0, 0)), aspec, aspec],
        out_specs=(aspec, aspec), name=name, compiler_params=_cp("parallel", "parallel"))(dx, wd, g, u)


def _row3():
    return pl.BlockSpec((ROW_T, D_MODEL), lambda i, n, k: (i, 0))


def _vec3():
    return pl.BlockSpec((1, D_MODEL), lambda i, n, k: (0, 0))


def _with_norm(T, next_nw):
    if next_nw is None:
        return dict(out_shape=jax.ShapeDtypeStruct((T, D_MODEL), F32), out_spec=_row3())
    return dict(out_shape=(jax.ShapeDtypeStruct((T, D_MODEL), F32), jax.ShapeDtypeStruct((T, D_MODEL), BF16)),
                out_spec=(_row3(), _row3()), post="norm", post_in=[(next_nw, _vec3())])


def _ffn_fwd(tag, x, h, wg, wu, wd, next_nw):
    T = x.shape[0]
    g, u, a = _ffn_gate_up(tag + "_gu", h, wg, wu)
    if callable(wd):
        wd = wd(a)
    nt = T // ROW_T
    o = _with_norm(T, next_nw)
    xo = _mm(tag + "_down",
             [(a, pl.BlockSpec((None, ROW_T, FF_SH), lambda i, n, k, j=j: (j, i, 0)),
               wd, pl.BlockSpec((None, FF_SH, D_MODEL), lambda i, n, k, j=j: (j, 0, 0))) for j in range(N_SHARD)],
             o.pop("out_shape"), o.pop("out_spec"), (nt, 1, 1), NN, (ROW_T, D_MODEL),
             res=(x, _row3()), scale=0.5, **o)
    return xo, (x, h, g, u, a), wd


def _rmsb_out(T):
    f = jax.ShapeDtypeStruct
    return (f((T, D_MODEL), F32), f((1, D_MODEL), F32), f((T, D_MODEL), BF16)), (_row3(), _vec3(), _row3())


def _ffn_bwd(tag, dxo, dxo_b, saved, nw, wg, wu, wd, emit):
    x, h, g, u, a = saved
    T = x.shape[0]
    nt = T // ROW_T
    tkw = min(TK_W, T)
    nw_t = T // tkw
    dg, du = _ffn_dact(tag + "_dact", dxo_b, wd, g, u)
    actw = lambda f: pl.BlockSpec((None, tkw, FF_SH), f)
    gd = _mm(tag + "_dwd",
             [(a, actw(lambda m, n, k: (m, k, 0)), dxo_b, pl.BlockSpec((tkw, D_MODEL), lambda m, n, k: (k, 0)))],
             jax.ShapeDtypeStruct((N_SHARD, FF_SH, D_MODEL), BF16),
             pl.BlockSpec((None, FF_SH, D_MODEL), lambda m, n, k: (m, 0, 0)),
             (N_SHARD, 1, nw_t), TN, (FF_SH, D_MODEL), scale=0.5)
    hspec = pl.BlockSpec((tkw, D_MODEL), lambda j, n, k: (k, 0))
    gsh = jax.ShapeDtypeStruct((N_SHARD, FF_SH, D_MODEL), BF16)
    gspec = pl.BlockSpec((None, FF_SH, D_MODEL), lambda j, n, k: (j, 0, 0))
    gg = _mm(tag + "_dwg", [(dg, actw(lambda j, n, k: (j, k, 0)), h, hspec)], gsh, gspec,
             (N_SHARD, 1, nw_t), TN, (FF_SH, D_MODEL))
    gu = _mm(tag + "_dwu", [(du, actw(lambda j, n, k: (j, k, 0)), h, hspec)], gsh, gspec,
             (N_SHARD, 1, nw_t), TN, (FF_SH, D_MODEL))
    dg = emit(gg, gu, gd, dg)
    act = lambda j: pl.BlockSpec((None, ROW_T, FF_SH), lambda i, n, k: (j, i, 0))
    wsp = lambda j: pl.BlockSpec((None, FF_SH, D_MODEL), lambda i, n, k: (j, 0, 0))
    return _mm(tag + "_dh",
               [(dd, act(j), w, wsp(j)) for j in range(N_SHARD) for dd, w in ((dg, wg), (du, wu))],
               *_rmsb_out(T), (nt, 1, 1), NN, (ROW_T, D_MODEL), post="rmsb",
               post_in=[(x, _row3()), (nw, _vec3()), (dxo, _row3())])


def _seq_rows(ref, start, size, S):
    lo, hi = max(start, 0), min(start + size, S)
    parts = [ref[pl.ds(lo, hi - lo), :]]
    if lo > start:
        parts.insert(0, jnp.zeros((lo - start, ref.shape[1]), F32))
    if start + size > hi:
        parts.append(jnp.zeros((start + size - hi, ref.shape[1]), F32))
    return parts[0] if len(parts) == 1 else jnp.concatenate(parts, axis=0)


XBC_CB = COL_XBC // CONV_CT


def _conv_fwd(name, proj, w, b, B):
    T = proj.shape[0]
    S = T // B
    C = CONV_DIM

    def body(x_ref, w_ref, b_ref, o_ref):
        wv = w_ref[...]
        for c in range(S // CONV_R):
            r0 = c * CONV_R
            ch = _seq_rows(x_ref, r0 - PAD_R, CONV_R + PAD_R, S)
            pre = ch[PAD_R:] * wv[3:4] + b_ref[...]
            for s in range(1, CONV_K):
                pre = pre + pltpu.roll(ch, s, axis=0)[PAD_R:] * wv[3 - s:4 - s]
            o_ref[pl.ds(r0, CONV_R), :] = pre * _sigmoid(pre)

    return pl.pallas_call(
        body, out_shape=jax.ShapeDtypeStruct((T, C), F32), grid=(B, C // CONV_CT),
        in_specs=[pl.BlockSpec((S, CONV_CT), lambda bi, ci: (bi, XBC_CB + ci)),
                  pl.BlockSpec((CONV_K, CONV_CT), lambda bi, ci: (0, ci)),
                  pl.BlockSpec((1, CONV_CT), lambda bi, ci: (0, ci))],
        out_specs=pl.BlockSpec((S, CONV_CT), lambda bi, ci: (bi, ci)), name=name,
        compiler_params=_cp("parallel", "parallel"))(proj, w, b)


def _conv_bwd(name, proj, dxs, dB, dC, w, b, dproj, B):
    T = proj.shape[0]
    S = T // B
    C = CONV_DIM
    RW = CONV_R + PAD_R
    nx, nb = dxs.shape[1] // CONV_CT, dB.shape[1] // CONV_CT

    def body(x_ref, dx_in, db_in, dc_in, w_ref, b_ref, buf_ref, dx_ref, dw_ref, db_ref):
        @pl.when(pl.program_id(1) == 0)
        def _():
            dw_ref[...] = jnp.zeros_like(dw_ref)
            db_ref[...] = jnp.zeros_like(db_ref)

        ci = pl.program_id(0)
        wv = w_ref[...]
        dw = [jnp.zeros((1, CONV_CT), F32) for _ in range(CONV_K)]
        db = jnp.zeros((1, CONV_CT), F32)
        for c in range(S // CONV_R):
            r0 = c * CONV_R
            ch = _seq_rows(x_ref, r0 - PAD_R, RW + PAD_R, S)
            xs = [ch[PAD_R:]] + [pltpu.roll(ch, s, axis=0)[PAD_R:] for s in range(1, CONV_K)]
            pre = b_ref[...] + xs[0] * wv[3:4]
            for s in range(1, CONV_K):
                pre = pre + xs[s] * wv[3 - s:4 - s]
            sg = _sigmoid(pre)
            dout = jnp.where(ci < nx, _seq_rows(dx_in, r0, RW, S),
                             jnp.where(ci < nx + nb, _seq_rows(db_in, r0, RW, S), _seq_rows(dc_in, r0, RW, S)))
            dpre = dout * (sg * (1.0 + pre * (1.0 - sg)))
            dx = dpre[:CONV_R] * wv[3:4]
            for s in range(1, CONV_K):
                dx = dx + pltpu.roll(dpre, RW - s, axis=0)[:CONV_R] * wv[3 - s:4 - s]
            dx_ref[pl.ds(r0, CONV_R), :] = dx.astype(BF16)
            dcur = dpre[:CONV_R]
            db = db + jnp.sum(dcur, axis=0, keepdims=True)
            for s in range(CONV_K):
                dw[3 - s] = dw[3 - s] + jnp.sum(dcur * xs[s][:CONV_R], axis=0, keepdims=True)
        db_ref[...] += db
        for k in range(CONV_K):
            dw_ref[k:k + 1, :] += dw[k]

    seq = lambda f: pl.BlockSpec((S, CONV_CT), f)
    return pl.pallas_call(
        body,
        out_shape=(jax.ShapeDtypeStruct(dproj.shape, dproj.dtype), jax.ShapeDtypeStruct((CONV_K, C), F32),
                   jax.ShapeDtypeStruct((1, C), F32)),
        grid=(C // CONV_CT, B),
        in_specs=[seq(lambda ci, bi: (bi, XBC_CB + ci)),
                  seq(lambda ci, bi: (bi, jnp.minimum(ci, nx - 1))),
                  seq(lambda ci, bi: (bi, jnp.clip(ci - nx, 0, nb - 1))),
                  seq(lambda ci, bi: (bi, jnp.clip(ci - nx - nb, 0, nb - 1))),
                  pl.BlockSpec((CONV_K, CONV_CT), lambda ci, bi: (0, ci)),
                  pl.BlockSpec((1, CONV_CT), lambda ci, bi: (0, ci)), ANY],
        out_specs=(seq(lambda ci, bi: (bi, XBC_CB + ci)),
                   pl.BlockSpec((CONV_K, CONV_CT), lambda ci, bi: (0, ci)),
                   pl.BlockSpec((1, CONV_CT), lambda ci, bi: (0, ci))),
        input_output_aliases={6: 0},
        name=name, compiler_params=_cp("parallel", "arbitrary"))(proj, dxs, dB, dC, w, b, dproj)


def _tri_sum(tri, x, dims, tri_first, terms=3):
    out, rest = None, x
    for t in range(terms):
        part = rest.astype(BF16)
        if t + 1 < terms:
            rest = rest - part.astype(F32)
        d = _dot(tri, part, dims) if tri_first else _dot(part, tri, dims)
        out = d if out is None else out + d
    return out


def _total(x):
    return jnp.sum(jnp.sum(x, axis=0, keepdims=True), axis=-1, keepdims=True)


def _ssd_common(dtc_ref, dtr_ref, pcol_ref, prow_ref, b_ref, c_ref):
    L = SSD_L
    bias_c, alog_c = pcol_ref[0:1, :], pcol_ref[1:2, :]
    a_c = -jnp.exp(alog_c)
    dt_c = _softplus(dtc_ref[...] + bias_c)
    row = lax.broadcasted_iota(jnp.int32, (L, L), 0)
    col = lax.broadcasted_iota(jnp.int32, (L, L), 1)
    causal = row >= col
    tri = causal.astype(BF16)
    cum_c = _tri_sum(tri, dt_c * a_c, NN, True)
    a_r = -jnp.exp(prow_ref[:, 1:2])
    dt_r = _softplus(dtr_ref[...] + prow_ref[:, 0:1])
    cum_r = _tri_sum(tri, dt_r * a_r, NT, False)
    bb = b_ref[...].astype(BF16)
    cb = c_ref[...].astype(BF16)
    G = _dot(cb, bb, NT)
    return a_c, dt_c, causal, tri, cum_c, cum_r, bb, cb, G


def _ssd_fwd(name, xc, proj, dtc, dtr, pcol, prow, nw, B):
    T = xc.shape[0]
    S = T // B
    nb = S // SSD_L
    L = SSD_L

    def body(xs_ref, b_ref, c_ref, z_ref, dtc_ref, dtr_ref, pcol_ref, prow_ref, nw_ref, y_ref, yn_ref, hs_ref, H, yo_s):
        @pl.when(pl.program_id(2) == 0)
        def _():
            H[...] = jnp.zeros_like(H)

        a_c, dt_c, causal, tri, cum_c, cum_r, bb, cb, G = _ssd_common(dtc_ref, dtr_ref, pcol_ref, prow_ref, b_ref, c_ref)
        dsk = pcol_ref[2:3, :]
        clast = cum_c[L - 1:L, :]
        bf = b_ref[...]
        for h in range(4):
            hs_ref[h] = H[h]
            yo_s[h] = _dot(cb, H[h].astype(BF16), NN)
        for h in range(4):
            sl = slice(HEAD_DIM * h, HEAD_DIM * (h + 1))
            cc = cum_c[:, h:h + 1]
            lm = jnp.exp(jnp.where(causal, cc - cum_r[h:h + 1, :], NEG))
            M = (G * lm).astype(BF16)
            xh = xs_ref[:, sl]
            Xb = (xh * dt_c[:, h:h + 1]).astype(BF16)
            Hh = H[h]
            y = _dot(M, Xb, NN) + jnp.exp(cc) * yo_s[h]
            y_ref[:, sl] = y + dsk[:, h:h + 1] * xh
            cl = clast[:, h:h + 1]
            Bw = (bf * jnp.exp(cl - cc)).astype(BF16)
            H[h] = jnp.exp(cl) * Hh + _dot(Bw, Xb, TN)
        zv = z_ref[...]
        y2 = y_ref[...] * (zv * _sigmoid(zv))
        r = lax.rsqrt(jnp.mean(y2 * y2, axis=-1, keepdims=True) + EPS)
        yn_ref[...] = (y2 * r * nw_ref[...]).astype(BF16)

    rowi = lambda b, g, i: b * nb + i
    grp = pl.BlockSpec((L, GROUP_W), lambda b, g, i: (rowi(b, g, i), g))
    return pl.pallas_call(
        body,
        out_shape=(jax.ShapeDtypeStruct((T, 1024), F32), jax.ShapeDtypeStruct((T, 1024), BF16),
                   jax.ShapeDtypeStruct((B, SSD_GROUPS, nb, 4, SSD_STATE, HEAD_DIM), F32)),
        grid=(B, SSD_GROUPS, nb),
        in_specs=[grp,
                  pl.BlockSpec((L, SSD_STATE), lambda b, g, i: (rowi(b, g, i), 8 + g)),
                  pl.BlockSpec((L, SSD_STATE), lambda b, g, i: (rowi(b, g, i), 12 + g)),
                  grp,
                  pl.BlockSpec((None, L, 4), lambda b, g, i: (g, rowi(b, g, i), 0)),
                  pl.BlockSpec((None, 4, L), lambda b, g, i: (g, 0, rowi(b, g, i))),
                  pl.BlockSpec((None, 3, 4), lambda b, g, i: (g, 0, 0)),
                  pl.BlockSpec((None, 4, 3), lambda b, g, i: (g, 0, 0)),
                  pl.BlockSpec((1, GROUP_W), lambda b, g, i: (0, g))],
        out_specs=(grp, grp,
                   pl.BlockSpec((None, None, None, 4, SSD_STATE, HEAD_DIM), lambda b, g, i: (b, g, i, 0, 0, 0))),
        scratch_shapes=[pltpu.VMEM((4, SSD_STATE, HEAD_DIM), F32), pltpu.VMEM((4, L, HEAD_DIM), F32)], name=name,
        compiler_params=_cp("parallel", "parallel", "arbitrary"))(xc, xc, xc, proj, dtc, dtr, pcol, prow, nw)


def _ssd_bwd(name, dyn, Y, xc, proj, dtc, dtr, pcol, prow, nw, hs, dproj, B):
    T = xc.shape[0]
    S = T // B
    nb = S // SSD_L
    L = SSD_L

    def body(dyn_ref, y_ref, xs_ref, b_ref, c_ref, z_ref, dtc_ref, dtr_ref, pcol_ref, prow_ref, nw_ref, hs_ref, buf_ref,
             dxs_ref, db_ref, dc_ref, dz_ref, ddt_ref, dpar_ref, dnw_ref, dH, dm_s, dxo_s, ea_s, ex_s):
        @pl.when(pl.program_id(2) == 0)
        def _():
            dH[...] = jnp.zeros_like(dH)
            dpar_ref[...] = jnp.zeros_like(dpar_ref)
            dnw_ref[...] = jnp.zeros_like(dnw_ref)

        a_c, dt_c, causal, tri, cum_c, cum_r, bb, cb, G = _ssd_common(dtc_ref, dtr_ref, pcol_ref, prow_ref, b_ref, c_ref)
        dsk = pcol_ref[2:3, :]
        clast = cum_c[L - 1:L, :]
        bf = b_ref[...]
        cf = c_ref[...]
        Yv = y_ref[...]
        zv = z_ref[...]
        sz = _sigmoid(zv)
        silu = zv * sz
        y2 = Yv * silu
        r = lax.rsqrt(jnp.mean(y2 * y2, axis=-1, keepdims=True) + EPS)
        yhat = y2 * r
        dyv = dyn_ref[...]
        dnw_ref[...] += jnp.sum(dyv * yhat, axis=0, keepdims=True)
        dyhat = dyv * nw_ref[...]
        dy2 = r * (dyhat - yhat * jnp.mean(dyhat * yhat, axis=-1, keepdims=True))
        dY = dy2 * silu
        dz_ref[...] = (dy2 * Yv * (sz * (1.0 + zv * (1.0 - sz)))).astype(BF16)

        lane4 = lax.broadcasted_iota(jnp.int32, (1, 4), 1)
        dG = jnp.zeros((L, L), F32)
        dBs = jnp.zeros((L, SSD_STATE), F32)
        dCs = jnp.zeros((L, SSD_STATE), F32)
        ddsk = jnp.zeros((1, 4), F32)
        dcl = jnp.zeros((1, 4), F32)
        for h in range(4):
            sl = slice(HEAD_DIM * h, HEAD_DIM * (h + 1))
            xb = (xs_ref[:, sl] * dt_c[:, h:h + 1]).astype(BF16)
            dm_s[h] = _dot(dY[:, sl].astype(BF16), xb, NT)
            dxo_s[h] = _dot(bb, dH[h].astype(BF16), NN)
        for h in range(4):
            sl = slice(HEAD_DIM * h, HEAD_DIM * (h + 1))
            onehot = (lane4 == h).astype(F32)
            cc = cum_c[:, h:h + 1]
            cl = clast[:, h:h + 1]
            lm = jnp.exp(jnp.where(causal, cc - cum_r[h:h + 1, :], NEG))
            M = (G * lm).astype(BF16)
            xh = xs_ref[:, sl]
            dth = dt_c[:, h:h + 1]
            X = xh * dth
            Xb = X.astype(BF16)
            dYh = dY[:, sl]
            dYb = dYh.astype(BF16)
            Hb = hs_ref[h].astype(BF16)
            dHh = dH[h]
            dHb = dHh.astype(BF16)
            alpha = jnp.exp(cc)
            beta = jnp.exp(cl - cc)
            dXoff = beta * dxo_s[h]
            dX = _dot(M, dYb, TN) + dXoff
            dG = dG + dm_s[h] * lm
            dCs = dCs + _dot((alpha * dYh).astype(BF16), Hb, NT)
            dBs = dBs + _dot((beta * X).astype(BF16), dHb, NT)
            ypre = Yv[:, sl] - dsk[:, h:h + 1] * xh
            ea_s[:, sl] = dYb.astype(F32) * ypre - Xb.astype(F32) * dX
            ex_s[:, sl] = dX * xh
            dcl_h = (_total(dHh * (jnp.exp(cl) * hs_ref[h])) + _total(Xb.astype(F32) * dXoff))
            dcl = dcl + dcl_h * onehot
            ddsk = ddsk + _total(dYh * xh) * onehot
            dxs_ref[:, sl] = dsk[:, h:h + 1] * dYh + dX * dth
            dH[h] = jnp.exp(cl) * dHh + _dot((alpha * cf).astype(BF16), dYb, TN)
        dGb = dG.astype(BF16)
        dc_ref[...] = _dot(dGb, bb, NN) + dCs
        db_ref[...] = _dot(dGb, cb, TN) + dBs
        feat = lax.broadcasted_iota(jnp.int32, (GROUP_W, 4), 0)
        head = lax.broadcasted_iota(jnp.int32, (GROUP_W, 4), 1) * HEAD_DIM
        sel = ((feat >= head) & (feat < head + HEAD_DIM)).astype(BF16)
        dA = _tri_sum(sel, ea_s[...], NN, False)
        ddtx = _tri_sum(sel, ex_s[...], NN, False)
        last = lax.broadcasted_iota(jnp.int32, (L, 1), 0) == L - 1
        dA = dA + jnp.where(last, dcl, 0.0)
        dadt = _tri_sum(tri, dA, TN, True)
        ddt = dadt * a_c + ddtx
        d_a = jnp.sum(dadt * dt_c, axis=0, keepdims=True)
        ddraw = ddt * _sigmoid(dtc_ref[...] + pcol_ref[0:1, :])
        ddt_ref[...] = ddraw
        dpar_ref[0:1, :] += jnp.sum(ddraw, axis=0, keepdims=True)
        dpar_ref[1:2, :] += d_a * a_c
        dpar_ref[2:3, :] += ddsk

    rowi = lambda b, g, i: b * nb + (nb - 1 - i)
    grp = pl.BlockSpec((L, GROUP_W), lambda b, g, i: (rowi(b, g, i), g))
    st = pl.BlockSpec((L, SSD_STATE), lambda b, g, i: (rowi(b, g, i), g))
    f = jax.ShapeDtypeStruct
    return pl.pallas_call(
        body,
        out_shape=(f((T, 1024), F32), f((T, 512), F32), f((T, 512), F32), f(dproj.shape, dproj.dtype),
                   f((SSD_GROUPS, T, 4), F32), f((B, SSD_GROUPS, 3, 4), F32), f((B, 1, 1024), F32)),
        grid=(B, SSD_GROUPS, nb),
        in_specs=[grp, grp, grp,
                  pl.BlockSpec((L, SSD_STATE), lambda b, g, i: (rowi(b, g, i), 8 + g)),
                  pl.BlockSpec((L, SSD_STATE), lambda b, g, i: (rowi(b, g, i), 12 + g)),
                  grp,
                  pl.BlockSpec((None, L, 4), lambda b, g, i: (g, rowi(b, g, i), 0)),
                  pl.BlockSpec((None, 4, L), lambda b, g, i: (g, 0, rowi(b, g, i))),
                  pl.BlockSpec((None, 3, 4), lambda b, g, i: (g, 0, 0)),
                  pl.BlockSpec((None, 4, 3), lambda b, g, i: (g, 0, 0)),
                  pl.BlockSpec((1, GROUP_W), lambda b, g, i: (0, g)),
                  pl.BlockSpec((None, None, None, 4, SSD_STATE, HEAD_DIM), lambda b, g, i: (b, g, nb - 1 - i, 0, 0, 0)),
                  ANY],
        out_specs=(grp, st, st, grp,
                   pl.BlockSpec((None, L, 4), lambda b, g, i: (g, rowi(b, g, i), 0)),
                   pl.BlockSpec((None, None, 3, 4), lambda b, g, i: (b, g, 0, 0)),
                   pl.BlockSpec((None, 1, GROUP_W), lambda b, g, i: (b, 0, g))),
        input_output_aliases={12: 3},
        scratch_shapes=[pltpu.VMEM((4, SSD_STATE, HEAD_DIM), F32), pltpu.VMEM((4, L, L), F32),
                        pltpu.VMEM((4, L, HEAD_DIM), F32), pltpu.VMEM((L, GROUP_W), F32),
                        pltpu.VMEM((L, GROUP_W), F32)], name=name,
        compiler_params=_cp("parallel", "parallel", "arbitrary"))(
            dyn, Y, xc, xc, xc, proj, dtc, dtr, pcol, prow, nw, hs, dproj)


def _head_sel():
    sel = (np.arange(1024)[:, None] // HEAD_DIM == np.arange(ATT_HEADS)[None, :]).astype(np.float32)
    return jnp.asarray(sel, BF16), jnp.asarray(sel.T, BF16)


def _head_rms(xv, sel, selT):
    ms = _tri_sum(sel, xv * xv, NN, False, 1) * (1.0 / HEAD_DIM)
    return _tri_sum(selT, lax.rsqrt(ms + EPS), NN, False, 2)


def _headnorm_fwd(name, proj, col_block, w):
    T = proj.shape[0]
    sel, selT = _head_sel()

    def body(x_ref, w_ref, sel_ref, selT_ref, o_ref):
        xv = x_ref[...]
        o_ref[...] = (xv * _head_rms(xv, sel_ref[...], selT_ref[...]) * w_ref[...]).astype(BF16)

    full = lambda shp: pl.BlockSpec(shp, lambda i: (0, 0))
    return pl.pallas_call(
        body, out_shape=jax.ShapeDtypeStruct((T, 1024), BF16), grid=(T // ROW_T,),
        in_specs=[pl.BlockSpec((ROW_T, 1024), lambda i: (i, col_block)), full((1, 1024)), full((1024, ATT_HEADS)),
                  full((ATT_HEADS, 1024))],
        out_specs=pl.BlockSpec((ROW_T, 1024), lambda i: (i, 0)), name=name, compiler_params=_cp("parallel"))(
            proj, jnp.tile(w, (1, ATT_HEADS)), sel, selT)


def _headnorm_bwd(name, dn, proj, col_block, w, dproj):
    T = proj.shape[0]
    sel, selT = _head_sel()

    def body(dn_ref, x_ref, w_ref, sel_ref, selT_ref, buf_ref, dx_ref, dw_ref):
        @pl.when(pl.program_id(0) == 0)
        def _():
            dw_ref[...] = jnp.zeros_like(dw_ref)

        xv = x_ref[...]
        sl, slT = sel_ref[...], selT_ref[...]
        rb = _head_rms(xv, sl, slT)
        xhat = xv * rb
        dnv = dn_ref[...]
        dxhat = dnv * w_ref[...]
        mean = _tri_sum(slT, _tri_sum(sl, dxhat * xhat, NN, False, 2) * (1.0 / HEAD_DIM), NN, False, 2)
        dx_ref[...] = (rb * (dxhat - xhat * mean)).astype(BF16)
        dw_ref[...] += jnp.sum(dnv * xhat, axis=0, keepdims=True)

    here = pl.BlockSpec((ROW_T, 1024), lambda i: (i, col_block))
    full = lambda shp: pl.BlockSpec(shp, lambda i: (0, 0))
    dx, dw = pl.pallas_call(
        body, out_shape=(jax.ShapeDtypeStruct(dproj.shape, dproj.dtype), jax.ShapeDtypeStruct((1, 1024), F32)),
        grid=(T // ROW_T,),
        in_specs=[pl.BlockSpec((ROW_T, 1024), lambda i: (i, 0)), here, full((1, 1024)), full((1024, ATT_HEADS)),
                  full((ATT_HEADS, 1024)), ANY],
        out_specs=(here, full((1, 1024))), input_output_aliases={5: 0},
        name=name, compiler_params=_cp("arbitrary"))(dn, proj, jnp.tile(w, (1, ATT_HEADS)), sel, selT, dproj)
    return dx, jnp.sum(dw.reshape(ATT_HEADS, HEAD_DIM), axis=0, keepdims=True)


def _att_bias(nq):
    j = np.arange(ATT_B)[:, None]
    i = np.arange(ATT_B)[None, :]
    out = np.empty((nq, ATT_B, ATT_B), np.float32)
    for dblk in range(nq):
        dl = ATT_B * dblk + i - j
        cnt = ((dl >= 0) & (dl <= 128)).astype(np.float32)
        cnt += ((dl >= 0) & (dl % 4 == 0) & (dl <= 512))
        cnt += ((dl >= 0) & (dl % 16 == 0) & (dl <= 2048))
        out[dblk] = np.where(cnt > 0, np.log(np.maximum(cnt, 1.0)), NEG)
    return jnp.asarray(out)


def _row_pair(nq):
    def f(r, c):
        first = c <= r
        return jnp.where(first, r, nq - 1 - r), jnp.where(first, c, c - (r + 1))
    return f


def _col_pair(nq):
    def f(r, c):
        first = c < nq - r
        kj = jnp.where(first, r, nq - 1 - r)
        return jnp.where(first, r + c, nq - 1 - r + (c - (nq - r))), kj
    return f


ATT_SCALE = 1.0 / math.sqrt(HEAD_DIM)
ATT_HS = 8
ATT_W = ATT_HS * HEAD_DIM


def _att_maps(nq, qk):
    return dict(
        q_tok=lambda b, g, r, c: (b * nq + qk(r, c)[0], g),
        k_tok=lambda b, g, r, c: (b * nq + qk(r, c)[1], g),
        v_tok=lambda b, g, r, c: (b * nq + qk(r, c)[1], COL_V // ATT_W + g),
        q_feat=lambda b, g, r, c: (g, b * nq + qk(r, c)[0]),
        k_feat=lambda b, g, r, c: (g, b * nq + qk(r, c)[1]),
        bias=lambda b, g, r, c: (qk(r, c)[0] - qk(r, c)[1], 0, 0),
        lse=lambda b, g, r, c: (g, 0, b * nq + qk(r, c)[0]),
        do_tok=lambda b, g, r, c: (b * nq + qk(r, c)[0], 1024 // ATT_W + g))


def _att_fwd(name, kn, qT, vT, bias, B):
    T = kn.shape[0]
    nq = (T // B) // ATT_B
    qk = _row_pair(nq)
    mp = _att_maps(nq, qk)

    def body(k_ref, qT_ref, vT_ref, bias_ref, oT_ref, lse_ref, m_s, l_s, acc_s, s_s):
        qi, kj = qk(pl.program_id(2), pl.program_id(3))

        @pl.when(kj == 0)
        def _():
            m_s[...] = jnp.full_like(m_s, NEG)
            l_s[...] = jnp.zeros_like(l_s)
            acc_s[...] = jnp.zeros_like(acc_s)

        bv = bias_ref[...]
        for h in range(ATT_HS):
            rs = slice(HEAD_DIM * h, HEAD_DIM * (h + 1))
            s_s[h] = _dot(k_ref[:, rs], qT_ref[rs, :], NN)
        for h in range(ATT_HS):
            rs = slice(HEAD_DIM * h, HEAD_DIM * (h + 1))
            s = s_s[h] + bv
            m_prev = m_s[h:h + 1, :]
            m_new = jnp.maximum(m_prev, jnp.max(s, axis=0, keepdims=True))
            alpha = jnp.exp(m_prev - m_new)
            p = jnp.exp(s - m_new)
            l_s[h:h + 1, :] = alpha * l_s[h:h + 1, :] + jnp.sum(p, axis=0, keepdims=True)
            acc_s[rs, :] = alpha * acc_s[rs, :] + _dot(vT_ref[rs, :], p.astype(BF16), NN)
            m_s[h:h + 1, :] = m_new

        @pl.when(kj == qi)
        def _():
            for h in range(ATT_HS):
                rs = slice(HEAD_DIM * h, HEAD_DIM * (h + 1))
                oT_ref[rs, :] = (acc_s[rs, :] / l_s[h:h + 1, :]).astype(BF16)
            lse_ref[...] = m_s[...] + jnp.log(l_s[...])

    tok = (ATT_B, ATT_W)
    feat = (ATT_W, ATT_B)
    return pl.pallas_call(
        body,
        out_shape=(jax.ShapeDtypeStruct((1024, T), BF16), jax.ShapeDtypeStruct((ATT_HEADS // ATT_HS, ATT_HS, T), F32)),
        grid=(B, ATT_HEADS // ATT_HS, nq // 2, nq + 1),
        in_specs=[pl.BlockSpec(tok, mp["k_tok"]), pl.BlockSpec(feat, mp["q_feat"]), pl.BlockSpec(feat, mp["k_feat"]),
                  pl.BlockSpec((None, ATT_B, ATT_B), mp["bias"])],
        out_specs=(pl.BlockSpec(feat, mp["q_feat"]), pl.BlockSpec((None, ATT_HS, ATT_B), mp["lse"])),
        scratch_shapes=[pltpu.VMEM((ATT_HS, ATT_B), F32), pltpu.VMEM((ATT_HS, ATT_B), F32),
                        pltpu.VMEM((ATT_W, ATT_B), F32), pltpu.VMEM((ATT_HS, ATT_B, ATT_B), F32)],
        name=name, compiler_params=_cp("parallel", "parallel", "arbitrary", "arbitrary"))(kn, qT, vT, bias)


def _att_scores(k_ref, qT_ref, v_ref, doT_ref, s_s, dp_s):
    for h in range(ATT_HS):
        rs = slice(HEAD_DIM * h, HEAD_DIM * (h + 1))
        s_s[h] = _dot(k_ref[:, rs], qT_ref[rs, :], NN)
        dp_s[h] = _dot(v_ref[:, rs].astype(BF16), doT_ref[rs, :].astype(BF16), NN)


def _att_p_ds(s_s, dp_s, doT_ref, oT_ref, lse_ref, bv, h):
    rs = slice(HEAD_DIM * h, HEAD_DIM * (h + 1))
    delta = jnp.sum(doT_ref[rs, :] * oT_ref[rs, :].astype(F32), axis=0, keepdims=True)
    p = jnp.exp(s_s[h] + bv - lse_ref[h:h + 1, :])
    return p, p * (dp_s[h] - delta)


def _att_bwd(name, kn, qT, proj, qn, knT, bias, doT, oT, lse, dyn, dproj, B):
    T = kn.shape[0]
    S = T // B
    nq = S // ATT_B
    qk = _col_pair(nq)
    mp = _att_maps(nq, qk)

    def body(k_ref, qT_ref, v_ref, q_ref, kT_ref, bias_ref, doT_ref, oT_ref, lse_ref, do_ref, buf_ref,
             dqT_ref, dk_ref, dv_ref, dk_s, dv_s, dq_s, s_s, dp_s):
        r, c = pl.program_id(2), pl.program_id(3)
        qi, kj = qk(r, c)

        @pl.when((r == 0) & (c == 0))
        def _():
            dq_s[...] = jnp.zeros_like(dq_s)

        @pl.when(qi == kj)
        def _():
            dk_s[...] = jnp.zeros_like(dk_s)
            dv_s[...] = jnp.zeros_like(dv_s)

        bv = bias_ref[...]
        _att_scores(k_ref, qT_ref, v_ref, doT_ref, s_s, dp_s)
        dq_blk = dq_s.at[qi]
        for h in range(ATT_HS):
            rs = slice(HEAD_DIM * h, HEAD_DIM * (h + 1))
            p, ds = _att_p_ds(s_s, dp_s, doT_ref, oT_ref, lse_ref, bv, h)
            dsb = ds.astype(BF16)
            dv_s[h] += _dot(p.astype(BF16), do_ref[:, rs].astype(BF16), NN)
            dk_s[h] += _dot(dsb, q_ref[:, rs], NN)
            dq_blk[rs, :] += _dot(kT_ref[rs, :], dsb, NN)

        @pl.when(qi == nq - 1)
        def _():
            for h in range(ATT_HS):
                rs = slice(HEAD_DIM * h, HEAD_DIM * (h + 1))
                dk_ref[:, rs] = dk_s[h] * ATT_SCALE
                dv_ref[:, rs] = dv_s[h].astype(BF16)

        @pl.when((r == nq // 2 - 1) & (c == nq))
        def _():
            for q in range(nq):
                dqT_ref[:, ATT_B * q:ATT_B * (q + 1)] = dq_s[q] * ATT_SCALE

    tok = (ATT_B, ATT_W)
    feat = (ATT_W, ATT_B)
    v_cb = COL_V // ATT_W
    return pl.pallas_call(
        body,
        out_shape=(jax.ShapeDtypeStruct((1024, T), F32), jax.ShapeDtypeStruct((T, 1024), F32),
                   jax.ShapeDtypeStruct(dproj.shape, dproj.dtype)),
        grid=(B, ATT_HEADS // ATT_HS, nq // 2, nq + 1),
        in_specs=[pl.BlockSpec(tok, mp["k_tok"]), pl.BlockSpec(feat, mp["q_feat"]), pl.BlockSpec(tok, mp["v_tok"]),
                  pl.BlockSpec(tok, mp["q_tok"]), pl.BlockSpec(feat, mp["k_feat"]),
                  pl.BlockSpec((None, ATT_B, ATT_B), mp["bias"]),
                  pl.BlockSpec(feat, mp["q_feat"]), pl.BlockSpec(feat, mp["q_feat"]),
                  pl.BlockSpec((None, ATT_HS, ATT_B), mp["lse"]), pl.BlockSpec(tok, mp["do_tok"]), ANY],
        out_specs=(pl.BlockSpec((ATT_W, S), lambda b, g, r, c: (g, b)),
                   pl.BlockSpec(tok, mp["k_tok"]),
                   pl.BlockSpec(tok, lambda b, g, r, c: (b * nq + qk(r, c)[1], v_cb + g))),
        input_output_aliases={10: 2},
        scratch_shapes=[pltpu.VMEM((ATT_HS, ATT_B, HEAD_DIM), F32), pltpu.VMEM((ATT_HS, ATT_B, HEAD_DIM), F32),
                        pltpu.VMEM((nq, ATT_W, ATT_B), F32),
                        pltpu.VMEM((ATT_HS, ATT_B, ATT_B), F32), pltpu.VMEM((ATT_HS, ATT_B, ATT_B), F32)],
        name=name, compiler_params=_cp("parallel", "parallel", "arbitrary", "arbitrary"))(
            kn, qT, proj, qn, knT, bias, doT, oT, lse, dyn, dproj)


def _group_cols(v):
    return v.reshape(SSD_GROUPS, 4)


def _ssd_params(p):
    rows = jnp.stack([_group_cols(p["dt_bias"]), _group_cols(p["a_log"]), _group_cols(p["d_skip"])], axis=1)
    return rows, jnp.swapaxes(rows, 1, 2)


def _dymix(name, dx, wout):
    T = dx.shape[0]

    def body(dx_ref, w_ref, o_ref):
        dxb = dx_ref[...].astype(BF16)
        for n in range(N_SHARD):
            o_ref[:, MIX_SH * n:MIX_SH * (n + 1)] = _dot(dxb, w_ref[n], NT)

    return pl.pallas_call(
        body, out_shape=jax.ShapeDtypeStruct((T, MIX_W), F32), grid=(T // ROW_T,),
        in_specs=[pl.BlockSpec((ROW_T, D_MODEL), lambda i: (i, 0)),
                  pl.BlockSpec((N_SHARD, MIX_SH, D_MODEL), lambda i: (0, 0, 0))],
        out_specs=pl.BlockSpec((ROW_T, MIX_W), lambda i: (i, 0)), name=name, compiler_params=_cp("parallel"))(dx, wout)


def _mixer_fwd(tag, x1, h2, p, weights, bias, B):
    T = x1.shape[0]
    S = T // B
    nt = T // ROW_T
    wi = weights("win", h2)
    win, cw = wi["win"], wi["cw"]
    tm = min(GU_T, T)
    proj = _mm(tag + "_proj",
               [(h2, pl.BlockSpec((tm, D_MODEL), lambda j, i, k: (i, 0)),
                 win, pl.BlockSpec((D_MODEL, PROJ_TN), lambda j, i, k: (0, j)))],
               jax.ShapeDtypeStruct((T, IN_PAD), F32), pl.BlockSpec((tm, PROJ_TN), lambda j, i, k: (i, j)),
               (IN_PAD // PROJ_TN, T // tm, 1), NN, (tm, PROJ_TN))
    xc = _conv_fwd(tag + "_conv", proj, cw, p["conv_b"][None], B)
    dtraw = proj[:, COL_DT:COL_DT + SSD_HEADS].reshape(T, SSD_GROUPS, 4)
    dtc = jnp.transpose(dtraw, (1, 0, 2))
    dtr = jnp.transpose(dtraw, (1, 2, 0))
    pcol, prow = _ssd_params(p)
    Y, y_ssd, hs = _ssd_fwd(tag + "_ssd", xc, proj, dtc, dtr, pcol, prow, p["ssd_norm"][None], B)
    qn = _headnorm_fwd(tag + "_qn", proj, COL_Q // 1024, p["q_norm"][None])
    kn = _headnorm_fwd(tag + "_kn", proj, COL_K // 1024, p["k_norm"][None])
    qT = (qn * ATT_SCALE).T
    oT, lse = _att_fwd(tag + "_att", kn, qT, proj[:, COL_V:COL_V + 1024].T.astype(BF16), bias, B)
    ymix = jnp.concatenate([y_ssd, oT.T], axis=1)
    rest = weights("rest", ymix)
    o = _with_norm(T, p["ffn2_norm"][None])
    x2, h3 = _mm(tag + "_out",
                 [(ymix, pl.BlockSpec((ROW_T, MIX_SH), lambda i, n, k, j=j: (i, j)),
                   rest["wout"], pl.BlockSpec((None, MIX_SH, D_MODEL), lambda i, n, k, j=j: (j, 0, 0)))
                  for j in range(N_SHARD)],
                 o.pop("out_shape"), o.pop("out_spec"), (nt, 1, 1), NN, (ROW_T, D_MODEL), res=(x1, _row3()), **o)
    saved = dict(x1=x1, h2=h2, proj=proj, xc=xc, dtc=dtc, dtr=dtr, Y=Y, hs=hs,
                 qn=qn, kn=kn, qT=qT, oT=oT, lse=lse, ymix=ymix, win=win, cw=cw, wout=rest["wout"])
    return x2, h3, saved


def _mixer_bwd(tag, dx2, dx2_b, sv, p, bias, B):
    T = dx2.shape[0]
    S = T // B
    nt = T // ROW_T
    sg = {}
    dymix = _dymix(tag + "_dymix", dx2_b, sv["wout"])
    tkw = min(TK_W, T)
    gwout = _mm(tag + "_dwout",
                [(sv["ymix"], pl.BlockSpec((tkw, MIX_SH), lambda m, n, k: (k, m)),
                  dx2_b, pl.BlockSpec((tkw, D_MODEL), lambda m, n, k: (k, 0)))],
                jax.ShapeDtypeStruct((N_SHARD, MIX_SH, D_MODEL), BF16),
                pl.BlockSpec((None, MIX_SH, D_MODEL), lambda m, n, k: (m, 0, 0)),
                (N_SHARD, 1, T // tkw), TN, (MIX_SH, D_MODEL))
    proj = sv["proj"]
    doT = dymix[:, 1024:].T
    dproj = lax.empty((T, IN_PAD), BF16)
    dqT, dkn, dproj = _att_bwd(tag + "_attb", sv["kn"], sv["qT"], proj, sv["qn"], sv["kn"].T, bias, doT, sv["oT"],
                               sv["lse"], dymix, dproj, B)
    dproj, sg["q_norm"] = _headnorm_bwd(tag + "_qnb", dqT.T, proj, COL_Q // 1024, p["q_norm"][None], dproj)
    dproj, sg["k_norm"] = _headnorm_bwd(tag + "_knb", dkn, proj, COL_K // 1024, p["k_norm"][None], dproj)
    pcol, prow = _ssd_params(p)
    dxs, dB, dC, dproj, ddt, dpar, dnw = _ssd_bwd(tag + "_ssdb", dymix, sv["Y"], sv["xc"], proj, sv["dtc"], sv["dtr"],
                                                  pcol, prow, p["ssd_norm"][None], sv["hs"], dproj, B)
    dpar = jnp.sum(dpar, axis=0)
    sg["dt_bias"] = dpar[:, 0, :].reshape(SSD_HEADS)
    sg["a_log"] = dpar[:, 1, :].reshape(SSD_HEADS)
    sg["d_skip"] = dpar[:, 2, :].reshape(SSD_HEADS)
    sg["ssd_norm"] = jnp.sum(dnw, axis=0)
    dproj, sg["conv_w"], sg["conv_b"] = _conv_bwd(tag + "_convb", proj, dxs, dB, dC, sv["cw"], p["conv_b"][None],
                                                  dproj, B)
    ddt16 = jnp.transpose(ddt, (1, 0, 2)).reshape(T, SSD_HEADS)
    dproj = lax.dynamic_update_slice(dproj, jnp.pad(ddt16, ((0, 0), (0, IN_PAD - COL_DT - SSD_HEADS))).astype(BF16),
                                     (0, COL_DT))
    win = sv["win"]
    gwin = _mm(tag + "_dwin",
               [(sv["h2"], pl.BlockSpec((tkw, D_MODEL), lambda n, m, k: (k, 0)),
                 dproj, pl.BlockSpec((tkw, PROJ_TN), lambda n, m, k: (k, n)))],
               jax.ShapeDtypeStruct((D_MODEL, IN_PAD), BF16), pl.BlockSpec((D_MODEL, PROJ_TN), lambda n, m, k: (0, n)),
               (IN_PAD // PROJ_TN, 1, T // tkw), TN, (D_MODEL, PROJ_TN))
    dx1, sg["mix_norm"], dx1_b = _mm(
        tag + "_dh2",
        [(dproj, pl.BlockSpec((ROW_T, PROJ_TN), lambda i, n, k, j=j: (i, j)),
          win, pl.BlockSpec((D_MODEL, PROJ_TN), lambda i, n, k, j=j: (0, j))) for j in range(IN_PAD // PROJ_TN)],
        *_rmsb_out(T), (nt, 1, 1), NT, (ROW_T, D_MODEL), post="rmsb",
        post_in=[(sv["x1"], _row3()), (p["mix_norm"][None], _vec3()), (dx2, _row3())])
    return dx1, dx1_b, sg, gwout, gwin


def _win_pack(w):
    return jnp.concatenate([w[:, :3072], w[:, 3088:], w[:, 3072:3088],
                            jnp.zeros((w.shape[0], IN_PAD - IN_PROJ), w.dtype)], axis=1)


def _win_unpack(g):
    return jnp.concatenate([g[:, :3072], g[:, COL_DT:COL_DT + SSD_HEADS], g[:, 3072:COL_DT]], axis=1)


DT_LO = IN_SH * 2 - COL_Q


def _win_from_shards(sh):
    main = IN_SH - DT_LO
    return jnp.concatenate([sh[0], sh[1][:, :main], sh[2][:, SSD_HEADS - DT_LO:], sh[3], sh[1][:, main:],
                            sh[2][:, :SSD_HEADS - DT_LO], jnp.zeros((sh.shape[1], IN_PAD - IN_PROJ), sh.dtype)], axis=1)


def _win_to_shards(g):
    main = IN_SH - DT_LO
    a, b = IN_SH + main, IN_SH + 2 * main
    return jnp.stack([g[:, :IN_SH],
                      jnp.concatenate([g[:, IN_SH:a], g[:, COL_DT:COL_DT + DT_LO]], axis=1),
                      jnp.concatenate([g[:, COL_DT + DT_LO:COL_DT + SSD_HEADS], g[:, a:b]], axis=1),
                      g[:, b:COL_DT]])


def _local_step(x, target, small, weights, scatter, B):
    T = x.shape[0]
    bias = _att_bias((T // B) // ATT_B)
    saved = []
    xl = x
    hl = _rms_fwd("l0f1_rms", x, small["ffn1_norm"][0][None])
    for l in range(DEPTH):
        tag = "l%d" % l
        p = {k: v[l] for k, v in small.items()}
        w1 = weights(l, "ffn1", hl)
        (x1, h2), ffn1, d1 = _ffn_fwd(tag + "f1", xl, hl, w1["g1"], w1["u1"],
                                      lambda after, l=l: weights(l, "ffn1d", after)["d1"], p["mix_norm"][None])
        x2, h3, sv = _mixer_fwd(tag, x1, h2, p, functools.partial(weights, l), bias, B)
        w2 = weights(l, "rest", x2)
        nxt = small["ffn1_norm"][l + 1][None] if l + 1 < DEPTH else None
        xo, ffn2, _ = _ffn_fwd(tag + "f2", x2, h3, w2["g2"], w2["u2"], w2["d2"], nxt)
        xl, hl = xo if nxt is not None else (xo, None)
        saved.append((ffn1, sv, ffn2, dict(g1=w1["g1"], u1=w1["u1"], d1=d1), w2))
    d, db, lsum = _loss_grad("loss", xl, target)
    sgrads = [None] * DEPTH
    for l in reversed(range(DEPTH)):
        tag = "l%db" % l
        p = {k: v[l] for k, v in small.items()}
        ffn1, sv, ffn2, w1, w2 = saved[l]
        sg = {}
        d, sg["ffn2_norm"], db = _ffn_bwd(tag + "f2", d, db, ffn2, p["ffn2_norm"][None], w2["g2"], w2["u2"], w2["d2"],
                                          lambda gg, gu, gd, c, l=l: scatter(l, "ffn2", dict(g2=gg, u2=gu, d2=gd), c))
        d, db, sgm, gwout, gwin = _mixer_bwd(tag, d, db, sv, p, bias, B)
        sg.update(sgm)
        db = scatter(l, "mixer", dict(wout=gwout, win=gwin), db)
        d, sg["ffn1_norm"], db = _ffn_bwd(tag + "f1", d, db, ffn1, p["ffn1_norm"][None], w1["g1"], w1["u1"], w1["d1"],
                                          lambda gg, gu, gd, c, l=l: scatter(l, "ffn1", dict(g1=gg, u1=gu, d1=gd), c))
        sgrads[l] = sg
    return lsum, d, sgrads


MESH = pl.DeviceIdType.MESH
ANY = pl.BlockSpec(memory_space=pl.ANY)


def _place():
    return lax.axis_index("x"), lax.axis_index("y"), lax.axis_index("c")


def _other_chips(x, y):
    return [(1 - x, y), (x, 1 - y), (1 - x, 1 - y)]


HBM = pl.BlockSpec(memory_space=pltpu.HBM)
SEM = pl.BlockSpec(memory_space=pltpu.SEMAPHORE)
EFFECT = pltpu.SideEffectType.DATAFLOW_SIDE_EFFECTING


def _hbm(a):
    return pltpu.with_memory_space_constraint(a, pltpu.HBM)


def _my_half(ref, c):
    hr = ref.shape[0] // 2
    return ref.at[pl.ds(pl.multiple_of(c * hr, 16), hr)]


def _exchange(gather, layer, halves, src, land, send, recv, n, act):
    x, y, c = _place()
    for k, (px, py) in enumerate(_other_chips(x, y)):
        for a in range(n):
            if gather:
                s_out, d_out, d_in = src[a].at[layer], land[a].at[2 * x + y], land[a].at[2 * px + py]
                if halves is not None and halves[a]:
                    s_out, d_out, d_in = _my_half(s_out, c), _my_half(d_out, c), _my_half(d_in, c)
            else:
                s_out, d_out, d_in = src[a].at[2 * px + py], land[a].at[k], land[a].at[k]
            act(pltpu.make_async_remote_copy(
                src_ref=s_out, dst_ref=d_out if act is _start else d_in, send_sem=send.at[k * n + a],
                recv_sem=recv.at[k * n + a], device_id=(px, py, c), device_id_type=MESH))


def _start(cp):
    cp.start()


def _finish(cp):
    cp.wait_send()
    cp.wait_recv()


def _exchange_start(name, gather, layer, srcs, carry, halves=None):
    n = len(srcs)
    lands = [lax.empty(((N_SHARD,) + s.shape[1:]) if gather else ((3,) + s.shape[1:]), s.dtype) for s in srcs]

    def body(*refs):
        _exchange(gather, layer, halves, refs[:n], refs[n:2 * n], refs[2 * n + 1], refs[2 * n + 2], n, _start)

    srcs = [_hbm(a) for a in srcs]
    thru = [_hbm(a) for a in lands + [carry]]
    out = pl.pallas_call(
        body, name=name,
        out_shape=(pltpu.SemaphoreType.DMA((3 * n,)), pltpu.SemaphoreType.DMA((3 * n,)),
                   *[pltpu.HBM(a.shape, a.dtype) for a in thru]),
        in_specs=[HBM] * (2 * n + 1), out_specs=(SEM, SEM, *[HBM] * (n + 1)),
        input_output_aliases={n + i: 2 + i for i in range(n + 1)},
        compiler_params=pltpu.CompilerParams(has_side_effects=EFFECT))(*srcs, *thru)
    return dict(gather=gather, layer=layer, halves=halves, send=out[0], recv=out[1], srcs=srcs,
                lands=list(out[2:2 + n])), out[-1]


def _exchange_wait(name, ex, after):
    n = len(ex["srcs"])

    def body(*refs):
        _exchange(ex["gather"], ex["layer"], ex["halves"], refs[:n], refs[n:2 * n], refs[2 * n], refs[2 * n + 1], n,
                  _finish)

    out = pl.pallas_call(
        body, name=name, out_shape=[pltpu.HBM(a.shape, a.dtype) for a in ex["lands"]],
        in_specs=[HBM] * (2 * n) + [SEM, SEM, ANY], out_specs=[HBM] * n,
        input_output_aliases={n + i: i for i in range(n)},
        compiler_params=pltpu.CompilerParams(has_side_effects=EFFECT))(
            *ex["srcs"], *ex["lands"], ex["send"], ex["recv"], after)
    return list(out)


def _sibling_fill(name, lands):
    n = len(lands)

    def body(*refs):
        land = refs[:n]
        send, recv = refs[2 * n], refs[2 * n + 1]
        x, y, c = _place()
        for k, (px, py) in enumerate(_other_chips(x, y)):
            for a in range(n):
                slot = land[a].at[2 * px + py]
                pltpu.make_async_remote_copy(src_ref=_my_half(slot, c), dst_ref=_my_half(slot, c),
                                             send_sem=send.at[k * n + a], recv_sem=recv.at[k * n + a],
                                             device_id=(x, y, 1 - c), device_id_type=MESH).start()
        for k, (px, py) in enumerate(_other_chips(x, y)):
            for a in range(n):
                slot = land[a].at[2 * px + py]
                cp = pltpu.make_async_remote_copy(src_ref=_my_half(slot, c), dst_ref=_my_half(slot, 1 - c),
                                                  send_sem=send.at[k * n + a], recv_sem=recv.at[k * n + a],
                                                  device_id=(x, y, 1 - c), device_id_type=MESH)
                cp.wait_recv()
                cp.wait_send()

    return pl.pallas_call(
        body, out_shape=[jax.ShapeDtypeStruct(a.shape, a.dtype) for a in lands],
        in_specs=[ANY] * n, out_specs=[ANY] * n, input_output_aliases={i: i for i in range(n)},
        scratch_shapes=[pltpu.SemaphoreType.DMA((3 * n,)), pltpu.SemaphoreType.DMA((3 * n,))],
        name=name)(*lands)


def _swap_sibling(name, parts):
    n = len(parts)

    def body(*refs):
        src, dst = refs[:n], refs[n:2 * n]
        send, recv = refs[2 * n:]
        x, y, c = _place()
        cps = [pltpu.make_async_remote_copy(src_ref=src[a], dst_ref=dst[a], send_sem=send.at[a], recv_sem=recv.at[a],
                                            device_id=(x, y, 1 - c), device_id_type=MESH) for a in range(n)]
        for cp in cps:
            cp.start()
        for cp in cps:
            cp.wait_recv()
        for cp in cps:
            cp.wait_send()

    return pl.pallas_call(
        body, out_shape=[jax.ShapeDtypeStruct(p.shape, p.dtype) for p in parts],
        in_specs=[ANY] * n, out_specs=[ANY] * n,
        scratch_shapes=[pltpu.SemaphoreType.DMA((n,)), pltpu.SemaphoreType.DMA((n,))],
        name=name)(*parts)


def _allreduce_small(name, v, after):
    R = v.shape[0]

    def body(v_ref, after_ref, o_ref, buf, send, recv):
        x, y, c = _place()
        me = 4 * x + 2 * y + c
        buf[me] = v_ref[...]
        cps = []
        for k in range(1, 8):
            fx, fy, fc = (k >> 2) & 1, (k >> 1) & 1, k & 1
            px = 1 - x if fx else x
            py = 1 - y if fy else y
            pc = 1 - c if fc else c
            cp = pltpu.make_async_remote_copy(src_ref=v_ref, dst_ref=buf.at[me], send_sem=send.at[k - 1],
                                              recv_sem=recv.at[k - 1], device_id=(px, py, pc), device_id_type=MESH)
            cp.start()
            cps.append((cp, 4 * px + 2 * py + pc))
        for k, (cp, peer) in enumerate(cps):
            pltpu.make_async_remote_copy(src_ref=v_ref, dst_ref=buf.at[peer], send_sem=send.at[k], recv_sem=recv.at[k],
                                         device_id=(x, y, c), device_id_type=MESH).wait_recv()
        for cp, _ in cps:
            cp.wait_send()
        acc = buf[0]
        for d in range(1, 8):
            acc = acc + buf[d]
        o_ref[...] = acc

    return pl.pallas_call(
        body, out_shape=jax.ShapeDtypeStruct((R, 128), F32),
        in_specs=[pl.BlockSpec(memory_space=pltpu.VMEM), ANY], out_specs=pl.BlockSpec(memory_space=pltpu.VMEM),
        scratch_shapes=[pltpu.VMEM((8, R, 128), F32), pltpu.SemaphoreType.DMA((7,)), pltpu.SemaphoreType.DMA((7,))],
        name=name)(v, after)


TILE_BYTES = 1600 * 1024


def _row_tile(r, c=1024):
    for t in (512, 352, 256, 128, 64, 32, 16, 8):
        if r % t == 0 and (t * c * 4 <= TILE_BYTES or t == 8):
            return t
    raise ValueError(r)


def _sum4(name, me, parts, got):
    _, R, C = parts.shape
    tr = _row_tile(R, C)

    def body(me_ref, o_ref, g_ref, s_ref):
        s = o_ref[...].astype(F32)
        for k in range(3):
            s = s + g_ref[k].astype(F32)
        s_ref[...] = s.astype(BF16)

    return pl.pallas_call(
        body, out_shape=jax.ShapeDtypeStruct((R, C), BF16),
        grid_spec=pltpu.PrefetchScalarGridSpec(
            num_scalar_prefetch=1, grid=(R // tr,),
            in_specs=[pl.BlockSpec((None, tr, C), lambda i, me_ref: (me_ref[0], i, 0)),
                      pl.BlockSpec((3, tr, C), lambda i, me_ref: (0, i, 0))],
            out_specs=pl.BlockSpec((tr, C), lambda i, me_ref: (i, 0))),
        name=name, compiler_params=_cp("parallel"))(me, parts, got)


def _adamw(name, w, gparts, m, v):
    R, C = w.shape
    tr = _row_tile(R, C)
    ng = len(gparts)
    c1 = 1.0 - ADAM_B1 ** ADAM_STEP
    c2 = 1.0 - ADAM_B2 ** ADAM_STEP

    def body(*refs):
        w_ref = refs[0]
        g_refs = refs[1:1 + ng]
        m_ref, v_ref, go_ref, d_ref, mo_ref, vo_ref = refs[1 + ng:]
        g = g_refs[0][...]
        for r in g_refs[1:]:
            g = g + r[...]
        mn = ADAM_B1 * m_ref[...] + (1.0 - ADAM_B1) * g
        vn = ADAM_B2 * v_ref[...] + (1.0 - ADAM_B2) * (g * g)
        go_ref[...] = g
        mo_ref[...] = mn
        vo_ref[...] = vn
        d_ref[...] = -ADAM_LR * ((mn / c1) / (jnp.sqrt(vn / c2) + ADAM_EPS) + ADAM_WD * w_ref[...])

    blk = pl.BlockSpec((tr, C), lambda i: (i, 0))
    osh = jax.ShapeDtypeStruct((R, C), F32)
    return pl.pallas_call(
        body, out_shape=(osh, osh, osh, osh), grid=(R // tr,), in_specs=[blk] * (3 + ng), out_specs=(blk,) * 4,
        name=name, compiler_params=_cp("parallel"))(w, *gparts, m, v)


def _adamw_layers(name, w, sums, m, v):
    _, R, C = w.shape
    tr = _row_tile(R, C)
    nr = R // tr
    c1 = 1.0 - ADAM_B1 ** ADAM_STEP
    c2 = 1.0 - ADAM_B2 ** ADAM_STEP

    def body(w_ref, a0, b0, a1, b1, m_ref, v_ref, go_ref, d_ref, mo_ref, vo_ref):
        f = lambda r: r[...].astype(F32)
        g = jnp.where(pl.program_id(0) == 0, f(a0) + f(b0), f(a1) + f(b1))
        mn = ADAM_B1 * m_ref[...] + (1.0 - ADAM_B1) * g
        vn = ADAM_B2 * v_ref[...] + (1.0 - ADAM_B2) * (g * g)
        go_ref[...] = g
        mo_ref[...] = mn
        vo_ref[...] = vn
        d_ref[...] = -ADAM_LR * ((mn / c1) / (jnp.sqrt(vn / c2) + ADAM_EPS) + ADAM_WD * w_ref[...])

    blk = pl.BlockSpec((None, tr, C), lambda l, i: (l, i, 0))
    lay0 = pl.BlockSpec((tr, C), lambda l, i: (jnp.where(l == 0, i, nr - 1), 0))
    lay1 = pl.BlockSpec((tr, C), lambda l, i: (jnp.where(l == 1, i, 0), 0))
    oblk = pl.BlockSpec((tr, C), lambda l, i: (l * nr + i, 0))
    osh = jax.ShapeDtypeStruct((DEPTH * R, C), F32)
    res = pl.pallas_call(
        body, out_shape=(osh, osh, osh, osh), grid=(DEPTH, nr),
        in_specs=[blk, lay0, lay0, lay1, lay1, blk, blk], out_specs=(oblk,) * 4,
        name=name, compiler_params=_cp("arbitrary", "arbitrary"))(w, *sums[0], *sums[1], m, v)
    return [r.reshape(w.shape) for r in res]


BIG = [("ffn1_w_gate", "g1"), ("ffn1_w_up", "u1"), ("ffn1_w_down", "d1"), ("w_in", "win"), ("w_out", "wout"),
       ("ffn2_w_gate", "g2"), ("ffn2_w_up", "u2"), ("ffn2_w_down", "d2")]
SMALL = ["ffn1_norm", "mix_norm", "conv_b", "dt_bias", "a_log", "d_skip", "ssd_norm", "q_norm", "k_norm", "ffn2_norm"]
WEIGHTS = ["ffn1_norm", "ffn1_w_gate", "ffn1_w_up", "ffn1_w_down", "mix_norm", "w_in", "conv_w", "conv_b", "dt_bias",
           "a_log", "d_skip", "ssd_norm", "q_norm", "k_norm", "w_out", "ffn2_norm", "ffn2_w_gate", "ffn2_w_up",
           "ffn2_w_down"]
CONV_SH = CONV_DIM // N_SHARD
TRANSPOSED = ("g1", "u1", "g2", "u2")
GATHER_GROUPS = [(0, "ffn1", ["g1", "u1"]), (0, "ffn1d", ["d1"]), (0, "win", ["win", "cw"]),
                 (0, "rest", ["wout", "g2", "u2", "d2"]),
                 (1, "all", ["g1", "u1", "d1", "win", "cw", "wout", "g2", "u2", "d2"])]


def _pad128(v):
    v = v.reshape(-1)
    return jnp.pad(v, (0, (-v.shape[0]) % 128))


def _pack(pieces):
    flat, offs, pos = [], [], 0
    for p in pieces:
        q = _pad128(p.astype(F32))
        offs.append(pos)
        pos += q.shape[0] // 128
        flat.append(q)
    total = -(-pos // 8) * 8
    out = jnp.concatenate(flat + [jnp.zeros(((total - pos) * 128,), F32)]).reshape(total, 128)
    return out, offs


def _unpack(packed, offs, shapes):
    out = []
    for off, shp in zip(offs, shapes):
        n = int(np.prod(shp))
        rows = -(-n // 128)
        out.append(packed[off:off + rows].reshape(-1)[:n].reshape(shp))
    return out


def kernel(x, ffn1_norm, ffn1_w_gate, ffn1_w_up, ffn1_w_down, mix_norm, w_in, conv_w, conv_b, dt_bias, a_log, d_skip, ssd_norm, q_norm, k_norm, w_out, ffn2_norm, ffn2_w_gate, ffn2_w_up, ffn2_w_down, loss_target, m_ffn1_norm, m_ffn1_w_gate, m_ffn1_w_up, m_ffn1_w_down, m_mix_norm, m_w_in, m_conv_w, m_conv_b, m_dt_bias, m_a_log, m_d_skip, m_ssd_norm, m_q_norm, m_k_norm, m_w_out, m_ffn2_norm, m_ffn2_w_gate, m_ffn2_w_up, m_ffn2_w_down, v_ffn1_norm, v_ffn1_w_gate, v_ffn1_w_up, v_ffn1_w_down, v_mix_norm, v_w_in, v_conv_w, v_conv_b, v_dt_bias, v_a_log, v_d_skip, v_ssd_norm, v_q_norm, v_k_norm, v_w_out, v_ffn2_norm, v_ffn2_w_gate, v_ffn2_w_up, v_ffn2_w_down):
    A = dict(locals())
    ix, iy, ic = _place()
    me = 2 * ix + iy
    B, S, _ = x.shape
    T = B * S

    view = lambda a, key: jnp.swapaxes(a, 1, 2) if key in TRANSPOSED else a
    own = {key: view(A[name], key).astype(BF16) for name, key in BIG}
    own["cw"] = conv_w
    exs, first_norm = [], ffn1_norm
    split = lambda l, key: l == 0 and key != "cw"
    for gi, (l, _, keys) in enumerate(GATHER_GROUPS):
        ex, first_norm = _exchange_start("gather_start%d" % gi, True, l, [own[key] for key in keys], first_norm,
                                         [split(l, key) for key in keys])
        exs.append(ex)
    landed = {}

    def weights(l, group, after):
        gi = [i for i, (gl, gname, _) in enumerate(GATHER_GROUPS) if gl == l and gname in (group, "all")][0]
        if gi not in landed:
            lands = _exchange_wait("gather_wait%d" % gi, exs[gi], after)
            keys = GATHER_GROUPS[gi][2]
            halved = [i for i, key in enumerate(keys) if split(l, key)]
            if halved:
                for i, whole in zip(halved, _sibling_fill("gather_fill%d" % gi, [lands[i] for i in halved])):
                    lands[i] = whole
            landed[gi] = {}
            for key, land in zip(GATHER_GROUPS[gi][2], lands):
                full = lax.dynamic_update_slice(land, own[key][l][None], (me, 0, 0))
                if key == "win":
                    full = _win_from_shards(full)
                if key == "cw":
                    full = jnp.transpose(full, (1, 0, 2)).reshape(CONV_K, CONV_DIM)
                landed[gi][key] = full
        return landed[gi]

    pending = []

    def scatter(l, group, grads, carry):
        keys = sorted(grads)
        arrs = [grads[key] for key in keys]
        if "win" in grads:
            arrs[keys.index("win")] = _win_to_shards(grads["win"])
        ex, carry = _exchange_start("scatter_start_l%d_%s" % (l, group), False, None, arrs, carry)
        pending.append((l, keys, ex))
        return carry

    small = {name: A[name] for name in SMALL}
    small["ffn1_norm"] = first_norm
    lsum, dx, sgrads = _local_step(x.reshape(T, D_MODEL), loss_target.reshape(T, D_MODEL), small, weights, scatter, B)

    names = SMALL + ["conv_w"]
    shapes = [A[n].shape for n in SMALL] + [(DEPTH, CONV_K, CONV_DIM), ()]
    pieces = [jnp.stack([sgrads[l][n].reshape(shp[1:]) for l in range(DEPTH)]) for n, shp in zip(names, shapes)]
    pieces.append(0.5 / D_MODEL * jnp.sum(lsum))
    packed, offs = _pack(pieces)

    sums, theirs, out = {}, {}, {}
    me1 = jnp.reshape(me, (1,)).astype(jnp.int32)

    def update(tag, after):
        todo = [k for k in sums if k not in theirs]
        theirs.update(zip(todo, _swap_sibling("swap_sibling_" + tag, [sums[k] for k in todo])))
        for name, key in BIG:
            if name not in out and all((key, l) in theirs for l in range(DEPTH)):
                res = _adamw_layers("adamw_" + key, view(A[name], key),
                                    [(sums[key, l], theirs[key, l]) for l in range(DEPTH)],
                                    view(A["m_" + name], key), view(A["v_" + name], key))
                out[name] = [view(r, key) for r in res]
                after = res[0]
        return after

    after = dx
    for idx, (l, keys, ex) in enumerate(pending):
        if idx == len(pending) - 1:
            after = update("a", after)
        lands = _exchange_wait("scatter_wait%d" % idx, ex, after)
        for key, g, got in zip(keys, ex["srcs"], lands):
            sums[key, l] = after = _sum4("sum_%s_l%d" % (key, l), me1, g, got)
    after = update("b", after)

    red = _unpack(_allreduce_small("allreduce_small", packed, after), offs, shapes)
    loss = red[-1]
    sg = dict(zip(names, red[:-1]))

    wp, offs = _pack([A[n] for n in SMALL])
    gp, _ = _pack([sg[n] for n in SMALL])
    mp, _ = _pack([A["m_" + n] for n in SMALL])
    vp, _ = _pack([A["v_" + n] for n in SMALL])
    res = _adamw("adamw_small", wp, [gp], mp, vp)
    shapes = [A[n].shape for n in SMALL]
    res = [_unpack(r, offs, shapes) for r in res]
    for i, n in enumerate(SMALL):
        out[n] = [res[q][i] for q in range(4)]
    gcw = lax.dynamic_slice_in_dim(sg["conv_w"], me * CONV_SH, CONV_SH, axis=2)
    flat = lambda a: a.reshape(DEPTH * CONV_K, CONV_SH)
    res = _adamw("adamw_conv_w", flat(conv_w), [flat(gcw)], flat(m_conv_w), flat(v_conv_w))
    out["conv_w"] = [r.reshape(conv_w.shape) for r in res]

    outs = [loss, dx.reshape(B, S, D_MODEL)]
    for q in range(4):
        outs += [out[n][q] for n in WEIGHTS]
    return tuple(outs)
```

```python
import functools
import math

import numpy as np
import jax
import jax.numpy as jnp
from jax import lax
from jax.experimental import pallas as pl
from jax.experimental.pallas import tpu as pltpu

F32 = jnp.float32
BF16 = jnp.bfloat16

D_MODEL = 1024
DEPTH = 2
N_SHARD = 4
D_FF = 2816
FF_SH = D_FF // N_SHARD
SSD_HEADS = 16
HEAD_DIM = 64
SSD_GROUPS = 4
GROUP_W = 256
SSD_STATE = 128
CONV_K = 4
CONV_DIM = 2048
ATT_HEADS = 16
MIX_W = 2048
MIX_SH = MIX_W // N_SHARD
IN_PROJ = 6160
IN_SH = IN_PROJ // N_SHARD
IN_PAD = 6272
PROJ_TN = 896
COL_Z, COL_XBC, COL_Q, COL_K, COL_V, COL_DT = 0, 1024, 3072, 4096, 5120, 6144
EPS = 1e-6
NEG = -1e30
SSD_L = 512
ATT_B = 512
ROW_T = 512
HALF_T = ROW_T // 2
GU_T = 1024
TK_W = 4096
CONV_CT = 256
CONV_R = 256
PAD_R = 8

ADAM_LR, ADAM_B1, ADAM_B2, ADAM_EPS, ADAM_WD, ADAM_STEP = 0.001, 0.9, 0.999, 1e-08, 0.01, 10

NN = (((1,), (0,)), ((), ()))
NT = (((1,), (1,)), ((), ()))
TN = (((0,), (0,)), ((), ()))

VMEM_LIMIT = 56 * 1024 * 1024


def _cp(*sem):
    return pltpu.CompilerParams(dimension_semantics=sem, vmem_limit_bytes=VMEM_LIMIT)


def _dot(a, b, dims):
    return lax.dot_general(a, b, dims, preferred_element_type=F32)


def _sigmoid(x):
    return 0.5 * jnp.tanh(0.5 * x) + 0.5


def _softplus(x):
    return jnp.maximum(x, 0.0) + jnp.log(1.0 + jnp.exp(-jnp.abs(x)))


def _mm(name, pairs, out_shape, out_spec, grid, dims, acc_shape, res=None, scale=1.0, post=None, post_in=()):
    nk = grid[2]
    npair = len(pairs)
    npost = len(post_in)

    def body(*refs):
        ab = refs[:2 * npair]
        pos = 2 * npair
        res_ref = None
        if res is not None:
            res_ref = refs[pos]
            pos += 1
        pin = refs[pos:pos + npost]
        pos += npost
        out_ref = refs[pos]
        pos += 1
        if post is not None:
            out2_ref = refs[pos]
            pos += 1
        if post == "rmsb":
            out3_ref = refs[pos]
            pos += 1
        s = None
        for p in range(npair):
            d = _dot(ab[2 * p][...].astype(BF16), ab[2 * p + 1][...].astype(BF16), dims)
            s = d if s is None else s + d

        def finish(r):
            if scale != 1.0:
                r = r * scale
            if res_ref is not None:
                r = r + res_ref[...]
            if post == "rmsb":
                @pl.when(pl.program_id(0) == 0)
                def _():
                    out2_ref[...] = jnp.zeros_like(out2_ref)

                xv = pin[0][...]
                rr = lax.rsqrt(jnp.mean(xv * xv, axis=-1, keepdims=True) + EPS)
                xhat = xv * rr
                dxhat = r * pin[1][...]
                dx = pin[2][...] + rr * (dxhat - xhat * jnp.mean(dxhat * xhat, axis=-1, keepdims=True))
                out_ref[...] = dx
                out2_ref[...] += jnp.sum(r * xhat, axis=0, keepdims=True)
                out3_ref[...] = dx.astype(BF16)
                return
            out_ref[...] = r.astype(out_ref.dtype)
            if post == "norm":
                rr = lax.rsqrt(jnp.mean(r * r, axis=-1, keepdims=True) + EPS)
                out2_ref[...] = (r * rr * pin[0][...]).astype(BF16)

        if nk == 1:
            finish(s)
            return
        acc = refs[pos]
        k = pl.program_id(2)

        @pl.when(k == 0)
        def _():
            acc[...] = s

        @pl.when(k > 0)
        def _():
            acc[...] += s

        @pl.when(k == nk - 1)
        def _():
            finish(acc[...])

    args, specs = [], []
    for a, a_spec, b, b_spec in pairs:
        args += [a, b]
        specs += [a_spec, b_spec]
    for arr, spec in ([res] if res is not None else []) + list(post_in):
        args.append(arr)
        specs.append(spec)
    sems = ("arbitrary",) * 3 if post == "rmsb" else ("parallel", "parallel", "arbitrary")
    return pl.pallas_call(
        body, out_shape=out_shape, grid=grid, in_specs=specs, out_specs=out_spec,
        scratch_shapes=[] if nk == 1 else [pltpu.VMEM(acc_shape, F32)], name=name,
        compiler_params=_cp(*sems))(*args)


def _rms_fwd(name, x, w):
    T = x.shape[0]

    def body(x_ref, w_ref, o_ref):
        xv = x_ref[...]
        r = lax.rsqrt(jnp.mean(xv * xv, axis=-1, keepdims=True) + EPS)
        o_ref[...] = (xv * r * w_ref[...]).astype(BF16)

    return pl.pallas_call(
        body, out_shape=jax.ShapeDtypeStruct((T, D_MODEL), BF16), grid=(T // ROW_T,),
        in_specs=[pl.BlockSpec((ROW_T, D_MODEL), lambda i: (i, 0)), pl.BlockSpec((1, D_MODEL), lambda i: (0, 0))],
        out_specs=pl.BlockSpec((ROW_T, D_MODEL), lambda i: (i, 0)), name=name, compiler_params=_cp("parallel"))(x, w)


def _loss_grad(name, y, t):
    T = y.shape[0]

    def body(y_ref, t_ref, dy_ref, dyb_ref, l_ref):
        @pl.when(pl.program_id(0) == 0)
        def _():
            l_ref[...] = jnp.zeros_like(l_ref)

        e = y_ref[...] - t_ref[...]
        dy = e * (1.0 / D_MODEL)
        dy_ref[...] = dy
        dyb_ref[...] = dy.astype(BF16)
        l_ref[...] += jnp.sum(e * e, axis=0, keepdims=True)

    row = pl.BlockSpec((ROW_T, D_MODEL), lambda i: (i, 0))
    vec = pl.BlockSpec((1, D_MODEL), lambda i: (0, 0))
    return pl.pallas_call(
        body, out_shape=(jax.ShapeDtypeStruct((T, D_MODEL), F32), jax.ShapeDtypeStruct((T, D_MODEL), BF16),
                         jax.ShapeDtypeStruct((1, D_MODEL), F32)),
        grid=(T // ROW_T,), in_specs=[row, row], out_specs=(row, row, vec), name=name,
        compiler_params=_cp("arbitrary"))(y, t)


def _ffn_gate_up(name, h, wg, wu):
    T = h.shape[0]
    tm = min(GU_T, T)

    def body(h_ref, wg_ref, wu_ref, dgf_ref, duf_ref, a_ref):
        for r in range(0, tm, HALF_T):
            rows = slice(r, r + HALF_T)
            hv = h_ref[rows, :]
            g = _dot(hv, wg_ref[...], NT)
            u = _dot(hv, wu_ref[...], NT)
            sg = _sigmoid(g)
            silu = g * sg
            dgf_ref[rows, :] = (u * (sg * (1.0 + g * (1.0 - sg)))).astype(BF16)
            duf_ref[rows, :] = silu.astype(BF16)
            a_ref[rows, :] = (silu * u).astype(BF16)

    wspec = pl.BlockSpec((None, FF_SH, D_MODEL), lambda j, i: (j, 0, 0))
    ospec = pl.BlockSpec((None, tm, FF_SH), lambda j, i: (j, i, 0))
    osh = jax.ShapeDtypeStruct((N_SHARD, T, FF_SH), BF16)
    return pl.pallas_call(
        body, out_shape=(osh, osh, osh), grid=(N_SHARD, T // tm),
        in_specs=[pl.BlockSpec((tm, D_MODEL), lambda j, i: (i, 0)), wspec, wspec],
        out_specs=(ospec, ospec, ospec), name=name, compiler_params=_cp("parallel", "parallel"))(h, wg, wu)


def _ffn_dact(name, dx, wd, g, u):
    T = dx.shape[0]
    tm = min(GU_T, T)

    def body(dx_ref, wd_ref, g_ref, u_ref, dg_ref, du_ref):
        for r in range(0, tm, HALF_T):
            rows = slice(r, r + HALF_T)
            da = 0.5 * _dot(dx_ref[rows, :].astype(BF16), wd_ref[...], NT)
            dg_ref[rows, :] = (da * g_ref[rows, :].astype(F32)).astype(BF16)
            du_ref[rows, :] = (da * u_ref[rows, :].astype(F32)).astype(BF16)

    aspec = pl.BlockSpec((None, tm, FF_SH), lambda j, i: (j, i, 0))
    osh = jax.ShapeDtypeStruct((N_SHARD, T, FF_SH), BF16)
    return pl.pallas_call(
        body, out_shape=(osh, osh), grid=(N_SHARD, T // tm),
        in_specs=[pl.BlockSpec((tm, D_MODEL), lambda j, i: (i, 0)),
                  pl.BlockSpec((None, FF_SH, D_MODEL), lambda j, i: (j, 0, 0)), aspec, aspec],
        out_specs=(aspec, aspec), name=name, compiler_params=_cp("parallel", "parallel"))(dx, wd, g, u)


def _row3():
    return pl.BlockSpec((ROW_T, D_MODEL), lambda i, n, k: (i, 0))


def _vec3():
    return pl.BlockSpec((1, D_MODEL), lambda i, n, k: (0, 0))


def _with_norm(T, next_nw):
    if next_nw is None:
        return dict(out_shape=jax.ShapeDtypeStruct((T, D_MODEL), F32), out_spec=_row3())
    return dict(out_shape=(jax.ShapeDtypeStruct((T, D_MODEL), F32), jax.ShapeDtypeStruct((T, D_MODEL), BF16)),
                out_spec=(_row3(), _row3()), post="norm", post_in=[(next_nw, _vec3())])


def _ffn_fwd(tag, x, h, wg, wu, wd, next_nw):
    T = x.shape[0]
    g, u, a = _ffn_gate_up(tag + "_gu", h, wg, wu)
    if callable(wd):
        wd = wd(a)
    nt = T // ROW_T
    o = _with_norm(T, next_nw)
    xo = _mm(tag + "_down",
             [(a, pl.BlockSpec((None, ROW_T, FF_SH), lambda i, n, k, j=j: (j, i, 0)),
               wd, pl.BlockSpec((None, FF_SH, D_MODEL), lambda i, n, k, j=j: (j, 0, 0))) for j in range(N_SHARD)],
             o.pop("out_shape"), o.pop("out_spec"), (nt, 1, 1), NN, (ROW_T, D_MODEL),
             res=(x, _row3()), scale=0.5, **o)
    return xo, (x, h, g, u, a), wd


def _rmsb_out(T):
    f = jax.ShapeDtypeStruct
    return (f((T, D_MODEL), F32), f((1, D_MODEL), F32), f((T, D_MODEL), BF16)), (_row3(), _vec3(), _row3())


def _ffn_bwd(tag, dxo, dxo_b, saved, nw, wg, wu, wd, emit):
    x, h, g, u, a = saved
    T = x.shape[0]
    nt = T // ROW_T
    tkw = min(TK_W, T)
    nw_t = T // tkw
    dg, du = _ffn_dact(tag + "_dact", dxo_b, wd, g, u)
    actw = lambda f: pl.BlockSpec((None, tkw, FF_SH), f)
    gd = _mm(tag + "_dwd",
             [(a, actw(lambda m, n, k: (m, k, 0)), dxo_b, pl.BlockSpec((tkw, D_MODEL), lambda m, n, k: (k, 0)))],
             jax.ShapeDtypeStruct((N_SHARD, FF_SH, D_MODEL), BF16),
             pl.BlockSpec((None, FF_SH, D_MODEL), lambda m, n, k: (m, 0, 0)),
             (N_SHARD, 1, nw_t), TN, (FF_SH, D_MODEL), scale=0.5)
    hspec = pl.BlockSpec((tkw, D_MODEL), lambda j, n, k: (k, 0))
    gsh = jax.ShapeDtypeStruct((N_SHARD, FF_SH, D_MODEL), BF16)
    gspec = pl.BlockSpec((None, FF_SH, D_MODEL), lambda j, n, k: (j, 0, 0))
    gg = _mm(tag + "_dwg", [(dg, actw(lambda j, n, k: (j, k, 0)), h, hspec)], gsh, gspec,
             (N_SHARD, 1, nw_t), TN, (FF_SH, D_MODEL))
    gu = _mm(tag + "_dwu", [(du, actw(lambda j, n, k: (j, k, 0)), h, hspec)], gsh, gspec,
             (N_SHARD, 1, nw_t), TN, (FF_SH, D_MODEL))
    dg = emit(gg, gu, gd, dg)
    act = lambda j: pl.BlockSpec((None, ROW_T, FF_SH), lambda i, n, k: (j, i, 0))
    wsp = lambda j: pl.BlockSpec((None, FF_SH, D_MODEL), lambda i, n, k: (j, 0, 0))
    return _mm(tag + "_dh",
               [(dd, act(j), w, wsp(j)) for j in range(N_SHARD) for dd, w in ((dg, wg), (du, wu))],
               *_rmsb_out(T), (nt, 1, 1), NN, (ROW_T, D_MODEL), post="rmsb",
               post_in=[(x, _row3()), (nw, _vec3()), (dxo, _row3())])


def _seq_rows(ref, start, size, S):
    lo, hi = max(start, 0), min(start + size, S)
    parts = [ref[pl.ds(lo, hi - lo), :]]
    if lo > start:
        parts.insert(0, jnp.zeros((lo - start, ref.shape[1]), F32))
    if start + size > hi:
        parts.append(jnp.zeros((start + size - hi, ref.shape[1]), F32))
    return parts[0] if len(parts) == 1 else jnp.concatenate(parts, axis=0)


XBC_CB = COL_XBC // CONV_CT


def _conv_fwd(name, proj, w, b, B):
    T = proj.shape[0]
    S = T // B
    C = CONV_DIM

    def body(x_ref, w_ref, b_ref, o_ref):
        wv = w_ref[...]
        for c in range(S // CONV_R):
            r0 = c * CONV_R
            ch = _seq_rows(x_ref, r0 - PAD_R, CONV_R + PAD_R, S)
            pre = ch[PAD_R:] * wv[3:4] + b_ref[...]
            for s in range(1, CONV_K):
                pre = pre + pltpu.roll(ch, s, axis=0)[PAD_R:] * wv[3 - s:4 - s]
            o_ref[pl.ds(r0, CONV_R), :] = pre * _sigmoid(pre)

    return pl.pallas_call(
        body, out_shape=jax.ShapeDtypeStruct((T, C), F32), grid=(B, C // CONV_CT),
        in_specs=[pl.BlockSpec((S, CONV_CT), lambda bi, ci: (bi, XBC_CB + ci)),
                  pl.BlockSpec((CONV_K, CONV_CT), lambda bi, ci: (0, ci)),
                  pl.BlockSpec((1, CONV_CT), lambda bi, ci: (0, ci))],
        out_specs=pl.BlockSpec((S, CONV_CT), lambda bi, ci: (bi, ci)), name=name,
        compiler_params=_cp("parallel", "parallel"))(proj, w, b)


def _conv_bwd(name, proj, dxs, dB, dC, w, b, dproj, B):
    T = proj.shape[0]
    S = T // B
    C = CONV_DIM
    RW = CONV_R + PAD_R
    nx, nb = dxs.shape[1] // CONV_CT, dB.shape[1] // CONV_CT

    def body(x_ref, dx_in, db_in, dc_in, w_ref, b_ref, buf_ref, dx_ref, dw_ref, db_ref):
        @pl.when(pl.program_id(1) == 0)
        def _():
            dw_ref[...] = jnp.zeros_like(dw_ref)
            db_ref[...] = jnp.zeros_like(db_ref)

        ci = pl.program_id(0)
        wv = w_ref[...]
        dw = [jnp.zeros((1, CONV_CT), F32) for _ in range(CONV_K)]
        db = jnp.zeros((1, CONV_CT), F32)
        for c in range(S // CONV_R):
            r0 = c * CONV_R
            ch = _seq_rows(x_ref, r0 - PAD_R, RW + PAD_R, S)
            xs = [ch[PAD_R:]] + [pltpu.roll(ch, s, axis=0)[PAD_R:] for s in range(1, CONV_K)]
            pre = b_ref[...] + xs[0] * wv[3:4]
            for s in range(1, CONV_K):
                pre = pre + xs[s] * wv[3 - s:4 - s]
            sg = _sigmoid(pre)
            dout = jnp.where(ci < nx, _seq_rows(dx_in, r0, RW, S),
                             jnp.where(ci < nx + nb, _seq_rows(db_in, r0, RW, S), _seq_rows(dc_in, r0, RW, S)))
            dpre = dout * (sg * (1.0 + pre * (1.0 - sg)))
            dx = dpre[:CONV_R] * wv[3:4]
            for s in range(1, CONV_K):
                dx = dx + pltpu.roll(dpre, RW - s, axis=0)[:CONV_R] * wv[3 - s:4 - s]
            dx_ref[pl.ds(r0, CONV_R), :] = dx.astype(BF16)
            dcur = dpre[:CONV_R]
            db = db + jnp.sum(dcur, axis=0, keepdims=True)
            for s in range(CONV_K):
                dw[3 - s] = dw[3 - s] + jnp.sum(dcur * xs[s][:CONV_R], axis=0, keepdims=True)
        db_ref[...] += db
        for k in range(CONV_K):
            dw_ref[k:k + 1, :] += dw[k]

    seq = lambda f: pl.BlockSpec((S, CONV_CT), f)
    return pl.pallas_call(
        body,
        out_shape=(jax.ShapeDtypeStruct(dproj.shape, dproj.dtype), jax.ShapeDtypeStruct((CONV_K, C), F32),
                   jax.ShapeDtypeStruct((1, C), F32)),
        grid=(C // CONV_CT, B),
        in_specs=[seq(lambda ci, bi: (bi, XBC_CB + ci)),
                  seq(lambda ci, bi: (bi, jnp.minimum(ci, nx - 1))),
                  seq(lambda ci, bi: (bi, jnp.clip(ci - nx, 0, nb - 1))),
                  seq(lambda ci, bi: (bi, jnp.clip(ci - nx - nb, 0, nb - 1))),
                  pl.BlockSpec((CONV_K, CONV_CT), lambda ci, bi: (0, ci)),
                  pl.BlockSpec((1, CONV_CT), lambda ci, bi: (0, ci)), ANY],
        out_specs=(seq(lambda ci, bi: (bi, XBC_CB + ci)),
                   pl.BlockSpec((CONV_K, CONV_CT), lambda ci, bi: (0, ci)),
                   pl.BlockSpec((1, CONV_CT), lambda ci, bi: (0, ci))),
        input_output_aliases={6: 0},
        name=name, compiler_params=_cp("parallel", "arbitrary"))(proj, dxs, dB, dC, w, b, dproj)


def _tri_sum(tri, x, dims, tri_first, terms=3):
    out, rest = None, x
    for t in range(terms):
        part = rest.astype(BF16)
        if t + 1 < terms:
            rest = rest - part.astype(F32)
        d = _dot(tri, part, dims) if tri_first else _dot(part, tri, dims)
        out = d if out is None else out + d
    return out


def _total(x):
    return jnp.sum(jnp.sum(x, axis=0, keepdims=True), axis=-1, keepdims=True)


def _ssd_common(dtc_ref, dtr_ref, pcol_ref, prow_ref, b_ref, c_ref):
    L = SSD_L
    bias_c, alog_c = pcol_ref[0:1, :], pcol_ref[1:2, :]
    a_c = -jnp.exp(alog_c)
    dt_c = _softplus(dtc_ref[...] + bias_c)
    row = lax.broadcasted_iota(jnp.int32, (L, L), 0)
    col = lax.broadcasted_iota(jnp.int32, (L, L), 1)
    causal = row >= col
    tri = causal.astype(BF16)
    cum_c = _tri_sum(tri, dt_c * a_c, NN, True)
    a_r = -jnp.exp(prow_ref[:, 1:2])
    dt_r = _softplus(dtr_ref[...] + prow_ref[:, 0:1])
    cum_r = _tri_sum(tri, dt_r * a_r, NT, False)
    bb = b_ref[...].astype(BF16)
    cb = c_ref[...].astype(BF16)
    G = _dot(cb, bb, NT)
    return a_c, dt_c, causal, tri, cum_c, cum_r, bb, cb, G


def _ssd_fwd(name, xc, proj, dtc, dtr, pcol, prow, nw, B):
    T = xc.shape[0]
    S = T // B
    nb = S // SSD_L
    L = SSD_L

    def body(xs_ref, b_ref, c_ref, z_ref, dtc_ref, dtr_ref, pcol_ref, prow_ref, nw_ref, y_ref, yn_ref, hs_ref, H, yo_s):
        @pl.when(pl.program_id(2) == 0)
        def _():
            H[...] = jnp.zeros_like(H)

        a_c, dt_c, causal, tri, cum_c, cum_r, bb, cb, G = _ssd_common(dtc_ref, dtr_ref, pcol_ref, prow_ref, b_ref, c_ref)
        dsk = pcol_ref[2:3, :]
        clast = cum_c[L - 1:L, :]
        bf = b_ref[...]
        for h in range(4):
            hs_ref[h] = H[h]
            yo_s[h] = _dot(cb, H[h].astype(BF16), NN)
        for h in range(4):
            sl = slice(HEAD_DIM * h, HEAD_DIM * (h + 1))
            cc = cum_c[:, h:h + 1]
            lm = jnp.exp(jnp.where(causal, cc - cum_r[h:h + 1, :], NEG))
            M = (G * lm).astype(BF16)
            xh = xs_ref[:, sl]
            Xb = (xh * dt_c[:, h:h + 1]).astype(BF16)
            Hh = H[h]
            y = _dot(M, Xb, NN) + jnp.exp(cc) * yo_s[h]
            y_ref[:, sl] = y + dsk[:, h:h + 1] * xh
            cl = clast[:, h:h + 1]
            Bw = (bf * jnp.exp(cl - cc)).astype(BF16)
            H[h] = jnp.exp(cl) * Hh + _dot(Bw, Xb, TN)
        zv = z_ref[...]
        y2 = y_ref[...] * (zv * _sigmoid(zv))
        r = lax.rsqrt(jnp.mean(y2 * y2, axis=-1, keepdims=True) + EPS)
        yn_ref[...] = (y2 * r * nw_ref[...]).astype(BF16)

    rowi = lambda b, g, i: b * nb + i
    grp = pl.BlockSpec((L, GROUP_W), lambda b, g, i: (rowi(b, g, i), g))
    return pl.pallas_call(
        body,
        out_shape=(jax.ShapeDtypeStruct((T, 1024), F32), jax.ShapeDtypeStruct((T, 1024), BF16),
                   jax.ShapeDtypeStruct((B, SSD_GROUPS, nb, 4, SSD_STATE, HEAD_DIM), F32)),
        grid=(B, SSD_GROUPS, nb),
        in_specs=[grp,
                  pl.BlockSpec((L, SSD_STATE), lambda b, g, i: (rowi(b, g, i), 8 + g)),
                  pl.BlockSpec((L, SSD_STATE), lambda b, g, i: (rowi(b, g, i), 12 + g)),
                  grp,
                  pl.BlockSpec((None, L, 4), lambda b, g, i: (g, rowi(b, g, i), 0)),
                  pl.BlockSpec((None, 4, L), lambda b, g, i: (g, 0, rowi(b, g, i))),
                  pl.BlockSpec((None, 3, 4), lambda b, g, i: (g, 0, 0)),
                  pl.BlockSpec((None, 4, 3), lambda b, g, i: (g, 0, 0)),
                  pl.BlockSpec((1, GROUP_W), lambda b, g, i: (0, g))],
        out_specs=(grp, grp,
                   pl.BlockSpec((None, None, None, 4, SSD_STATE, HEAD_DIM), lambda b, g, i: (b, g, i, 0, 0, 0))),
        scratch_shapes=[pltpu.VMEM((4, SSD_STATE, HEAD_DIM), F32), pltpu.VMEM((4, L, HEAD_DIM), F32)], name=name,
        compiler_params=_cp("parallel", "parallel", "arbitrary"))(xc, xc, xc, proj, dtc, dtr, pcol, prow, nw)


def _ssd_bwd(name, dyn, Y, xc, proj, dtc, dtr, pcol, prow, nw, hs, dproj, B):
    T = xc.shape[0]
    S = T // B
    nb = S // SSD_L
    L = SSD_L

    def body(dyn_ref, y_ref, xs_ref, b_ref, c_ref, z_ref, dtc_ref, dtr_ref, pcol_ref, prow_ref, nw_ref, hs_ref, buf_ref,
             dxs_ref, db_ref, dc_ref, dz_ref, ddt_ref, dpar_ref, dnw_ref, dH, dm_s, dxo_s, ea_s, ex_s):
        @pl.when(pl.program_id(2) == 0)
        def _():
            dH[...] = jnp.zeros_like(dH)
            dpar_ref[...] = jnp.zeros_like(dpar_ref)
            dnw_ref[...] = jnp.zeros_like(dnw_ref)

        a_c, dt_c, causal, tri, cum_c, cum_r, bb, cb, G = _ssd_common(dtc_ref, dtr_ref, pcol_ref, prow_ref, b_ref, c_ref)
        dsk = pcol_ref[2:3, :]
        clast = cum_c[L - 1:L, :]
        bf = b_ref[...]
        cf = c_ref[...]
        Yv = y_ref[...]
        zv = z_ref[...]
        sz = _sigmoid(zv)
        silu = zv * sz
        y2 = Yv * silu
        r = lax.rsqrt(jnp.mean(y2 * y2, axis=-1, keepdims=True) + EPS)
        yhat = y2 * r
        dyv = dyn_ref[...]
        dnw_ref[...] += jnp.sum(dyv * yhat, axis=0, keepdims=True)
        dyhat = dyv * nw_ref[...]
        dy2 = r * (dyhat - yhat * jnp.mean(dyhat * yhat, axis=-1, keepdims=True))
        dY = dy2 * silu
        dz_ref[...] = (dy2 * Yv * (sz * (1.0 + zv * (1.0 - sz)))).astype(BF16)

        lane4 = lax.broadcasted_iota(jnp.int32, (1, 4), 1)
        dG = jnp.zeros((L, L), F32)
        dBs = jnp.zeros((L, SSD_STATE), F32)
        dCs = jnp.zeros((L, SSD_STATE), F32)
        ddsk = jnp.zeros((1, 4), F32)
        dcl = jnp.zeros((1, 4), F32)
        for h in range(4):
            sl = slice(HEAD_DIM * h, HEAD_DIM * (h + 1))
            xb = (xs_ref[:, sl] * dt_c[:, h:h + 1]).astype(BF16)
            dm_s[h] = _dot(dY[:, sl].astype(BF16), xb, NT)
            dxo_s[h] = _dot(bb, dH[h].astype(BF16), NN)
        for h in range(4):
            sl = slice(HEAD_DIM * h, HEAD_DIM * (h + 1))
            onehot = (lane4 == h).astype(F32)
            cc = cum_c[:, h:h + 1]
            cl = clast[:, h:h + 1]
            lm = jnp.exp(jnp.where(causal, cc - cum_r[h:h + 1, :], NEG))
            M = (G * lm).astype(BF16)
            xh = xs_ref[:, sl]
            dth = dt_c[:, h:h + 1]
            X = xh * dth
            Xb = X.astype(BF16)
            dYh = dY[:, sl]
            dYb = dYh.astype(BF16)
            Hb = hs_ref[h].astype(BF16)
            dHh = dH[h]
            dHb = dHh.astype(BF16)
            alpha = jnp.exp(cc)
            beta = jnp.exp(cl - cc)
            dXoff = beta * dxo_s[h]
            dX = _dot(M, dYb, TN) + dXoff
            dG = dG + dm_s[h] * lm
            dCs = dCs + _dot((alpha * dYh).astype(BF16), Hb, NT)
            dBs = dBs + _dot((beta * X).astype(BF16), dHb, NT)
            ypre = Yv[:, sl] - dsk[:, h:h + 1] * xh
            ea_s[:, sl] = dYb.astype(F32) * ypre - Xb.astype(F32) * dX
            ex_s[:, sl] = dX * xh
            dcl_h = (_total(dHh * (jnp.exp(cl) * hs_ref[h])) + _total(Xb.astype(F32) * dXoff))
            dcl = dcl + dcl_h * onehot
            ddsk = ddsk + _total(dYh * xh) * onehot
            dxs_ref[:, sl] = dsk[:, h:h + 1] * dYh + dX * dth
            dH[h] = jnp.exp(cl) * dHh + _dot((alpha * cf).astype(BF16), dYb, TN)
        dGb = dG.astype(BF16)
        dc_ref[...] = _dot(dGb, bb, NN) + dCs
        db_ref[...] = _dot(dGb, cb, TN) + dBs
        feat = lax.broadcasted_iota(jnp.int32, (GROUP_W, 4), 0)
        head = lax.broadcasted_iota(jnp.int32, (GROUP_W, 4), 1) * HEAD_DIM
        sel = ((feat >= head) & (feat < head + HEAD_DIM)).astype(BF16)
        dA = _tri_sum(sel, ea_s[...], NN, False)
        ddtx = _tri_sum(sel, ex_s[...], NN, False)
        last = lax.broadcasted_iota(jnp.int32, (L, 1), 0) == L - 1
        dA = dA + jnp.where(last, dcl, 0.0)
        dadt = _tri_sum(tri, dA, TN, True)
        ddt = dadt * a_c + ddtx
        d_a = jnp.sum(dadt * dt_c, axis=0, keepdims=True)
        ddraw = ddt * _sigmoid(dtc_ref[...] + pcol_ref[0:1, :])
        ddt_ref[...] = ddraw
        dpar_ref[0:1, :] += jnp.sum(ddraw, axis=0, keepdims=True)
        dpar_ref[1:2, :] += d_a * a_c
        dpar_ref[2:3, :] += ddsk

    rowi = lambda b, g, i: b * nb + (nb - 1 - i)
    grp = pl.BlockSpec((L, GROUP_W), lambda b, g, i: (rowi(b, g, i), g))
    st = pl.BlockSpec((L, SSD_STATE), lambda b, g, i: (rowi(b, g, i), g))
    f = jax.ShapeDtypeStruct
    return pl.pallas_call(
        body,
        out_shape=(f((T, 1024), F32), f((T, 512), F32), f((T, 512), F32), f(dproj.shape, dproj.dtype),
                   f((SSD_GROUPS, T, 4), F32), f((B, SSD_GROUPS, 3, 4), F32), f((B, 1, 1024), F32)),
        grid=(B, SSD_GROUPS, nb),
        in_specs=[grp, grp, grp,
                  pl.BlockSpec((L, SSD_STATE), lambda b, g, i: (rowi(b, g, i), 8 + g)),
                  pl.BlockSpec((L, SSD_STATE), lambda b, g, i: (rowi(b, g, i), 12 + g)),
                  grp,
                  pl.BlockSpec((None, L, 4), lambda b, g, i: (g, rowi(b, g, i), 0)),
                  pl.BlockSpec((None, 4, L), lambda b, g, i: (g, 0, rowi(b, g, i))),
                  pl.BlockSpec((None, 3, 4), lambda b, g, i: (g, 0, 0)),
                  pl.BlockSpec((None, 4, 3), lambda b, g, i: (g, 0, 0)),
                  pl.BlockSpec((1, GROUP_W), lambda b, g, i: (0, g)),
                  pl.BlockSpec((None, None, None, 4, SSD_STATE, HEAD_DIM), lambda b, g, i: (b, g, nb - 1 - i, 0, 0, 0)),
                  ANY],
        out_specs=(grp, st, st, grp,
                   pl.BlockSpec((None, L, 4), lambda b, g, i: (g, rowi(b, g, i), 0)),
                   pl.BlockSpec((None, None, 3, 4), lambda b, g, i: (b, g, 0, 0)),
                   pl.BlockSpec((None, 1, GROUP_W), lambda b, g, i: (b, 0, g))),
        input_output_aliases={12: 3},
        scratch_shapes=[pltpu.VMEM((4, SSD_STATE, HEAD_DIM), F32), pltpu.VMEM((4, L, L), F32),
                        pltpu.VMEM((4, L, HEAD_DIM), F32), pltpu.VMEM((L, GROUP_W), F32),
                        pltpu.VMEM((L, GROUP_W), F32)], name=name,
        compiler_params=_cp("parallel", "parallel", "arbitrary"))(
            dyn, Y, xc, xc, xc, proj, dtc, dtr, pcol, prow, nw, hs, dproj)


def _head_sel():
    sel = (np.arange(1024)[:, None] // HEAD_DIM == np.arange(ATT_HEADS)[None, :]).astype(np.float32)
    return jnp.asarray(sel, BF16), jnp.asarray(sel.T, BF16)


def _head_rms(xv, sel, selT):
    ms = _tri_sum(sel, xv * xv, NN, False, 1) * (1.0 / HEAD_DIM)
    return _tri_sum(selT, lax.rsqrt(ms + EPS), NN, False, 2)


def _headnorm_fwd(name, proj, col_block, w):
    T = proj.shape[0]
    sel, selT = _head_sel()

    def body(x_ref, w_ref, sel_ref, selT_ref, o_ref):
        xv = x_ref[...]
        o_ref[...] = (xv * _head_rms(xv, sel_ref[...], selT_ref[...]) * w_ref[...]).astype(BF16)

    full = lambda shp: pl.BlockSpec(shp, lambda i: (0, 0))
    return pl.pallas_call(
        body, out_shape=jax.ShapeDtypeStruct((T, 1024), BF16), grid=(T // ROW_T,),
        in_specs=[pl.BlockSpec((ROW_T, 1024), lambda i: (i, col_block)), full((1, 1024)), full((1024, ATT_HEADS)),
                  full((ATT_HEADS, 1024))],
        out_specs=pl.BlockSpec((ROW_T, 1024), lambda i: (i, 0)), name=name, compiler_params=_cp("parallel"))(
            proj, jnp.tile(w, (1, ATT_HEADS)), sel, selT)


def _headnorm_bwd(name, dn, proj, col_block, w, dproj):
    T = proj.shape[0]
    sel, selT = _head_sel()

    def body(dn_ref, x_ref, w_ref, sel_ref, selT_ref, buf_ref, dx_ref, dw_ref):
        @pl.when(pl.program_id(0) == 0)
        def _():
            dw_ref[...] = jnp.zeros_like(dw_ref)

        xv = x_ref[...]
        sl, slT = sel_ref[...], selT_ref[...]
        rb = _head_rms(xv, sl, slT)
        xhat = xv * rb
        dnv = dn_ref[...]
        dxhat = dnv * w_ref[...]
        mean = _tri_sum(slT, _tri_sum(sl, dxhat * xhat, NN, False, 2) * (1.0 / HEAD_DIM), NN, False, 2)
        dx_ref[...] = (rb * (dxhat - xhat * mean)).astype(BF16)
        dw_ref[...] += jnp.sum(dnv * xhat, axis=0, keepdims=True)

    here = pl.BlockSpec((ROW_T, 1024), lambda i: (i, col_block))
    full = lambda shp: pl.BlockSpec(shp, lambda i: (0, 0))
    dx, dw = pl.pallas_call(
        body, out_shape=(jax.ShapeDtypeStruct(dproj.shape, dproj.dtype), jax.ShapeDtypeStruct((1, 1024), F32)),
        grid=(T // ROW_T,),
        in_specs=[pl.BlockSpec((ROW_T, 1024), lambda i: (i, 0)), here, full((1, 1024)), full((1024, ATT_HEADS)),
                  full((ATT_HEADS, 1024)), ANY],
        out_specs=(here, full((1, 1024))), input_output_aliases={5: 0},
        name=name, compiler_params=_cp("arbitrary"))(dn, proj, jnp.tile(w, (1, ATT_HEADS)), sel, selT, dproj)
    return dx, jnp.sum(dw.reshape(ATT_HEADS, HEAD_DIM), axis=0, keepdims=True)


def _att_bias(nq):
    j = np.arange(ATT_B)[:, None]
    i = np.arange(ATT_B)[None, :]
    out = np.empty((nq, ATT_B, ATT_B), np.float32)
    for dblk in range(nq):
        dl = ATT_B * dblk + i - j
        cnt = ((dl >= 0) & (dl <= 128)).astype(np.float32)
        cnt += ((dl >= 0) & (dl % 4 == 0) & (dl <= 512))
        cnt += ((dl >= 0) & (dl % 16 == 0) & (dl <= 2048))
        out[dblk] = np.where(cnt > 0, np.log(np.maximum(cnt, 1.0)), NEG)
    return jnp.asarray(out)


def _row_pair(nq):
    def f(r, c):
        first = c <= r
        return jnp.where(first, r, nq - 1 - r), jnp.where(first, c, c - (r + 1))
    return f


def _col_pair(nq):
    def f(r, c):
        first = c < nq - r
        kj = jnp.where(first, r, nq - 1 - r)
        return jnp.where(first, r + c, nq - 1 - r + (c - (nq - r))), kj
    return f


ATT_SCALE = 1.0 / math.sqrt(HEAD_DIM)
ATT_HS = 8
ATT_W = ATT_HS * HEAD_DIM


def _att_maps(nq, qk):
    return dict(
        q_tok=lambda b, g, r, c: (b * nq + qk(r, c)[0], g),
        k_tok=lambda b, g, r, c: (b * nq + qk(r, c)[1], g),
        v_tok=lambda b, g, r, c: (b * nq + qk(r, c)[1], COL_V // ATT_W + g),
        q_feat=lambda b, g, r, c: (g, b * nq + qk(r, c)[0]),
        k_feat=lambda b, g, r, c: (g, b * nq + qk(r, c)[1]),
        bias=lambda b, g, r, c: (qk(r, c)[0] - qk(r, c)[1], 0, 0),
        lse=lambda b, g, r, c: (g, 0, b * nq + qk(r, c)[0]),
        do_tok=lambda b, g, r, c: (b * nq + qk(r, c)[0], 1024 // ATT_W + g))


def _att_fwd(name, kn, qT, vT, bias, B):
    T = kn.shape[0]
    nq = (T // B) // ATT_B
    qk = _row_pair(nq)
    mp = _att_maps(nq, qk)

    def body(k_ref, qT_ref, vT_ref, bias_ref, oT_ref, lse_ref, m_s, l_s, acc_s, s_s):
        qi, kj = qk(pl.program_id(2), pl.program_id(3))

        @pl.when(kj == 0)
        def _():
            m_s[...] = jnp.full_like(m_s, NEG)
            l_s[...] = jnp.zeros_like(l_s)
            acc_s[...] = jnp.zeros_like(acc_s)

        bv = bias_ref[...]
        for h in range(ATT_HS):
            rs = slice(HEAD_DIM * h, HEAD_DIM * (h + 1))
            s_s[h] = _dot(k_ref[:, rs], qT_ref[rs, :], NN)
        for h in range(ATT_HS):
            rs = slice(HEAD_DIM * h, HEAD_DIM * (h + 1))
            s = s_s[h] + bv
            m_prev = m_s[h:h + 1, :]
            m_new = jnp.maximum(m_prev, jnp.max(s, axis=0, keepdims=True))
            alpha = jnp.exp(m_prev - m_new)
            p = jnp.exp(s - m_new)
            l_s[h:h + 1, :] = alpha * l_s[h:h + 1, :] + jnp.sum(p, axis=0, keepdims=True)
            acc_s[rs, :] = alpha * acc_s[rs, :] + _dot(vT_ref[rs, :], p.astype(BF16), NN)
            m_s[h:h + 1, :] = m_new

        @pl.when(kj == qi)
        def _():
            for h in range(ATT_HS):
                rs = slice(HEAD_DIM * h, HEAD_DIM * (h + 1))
                oT_ref[rs, :] = (acc_s[rs, :] / l_s[h:h + 1, :]).astype(BF16)
            lse_ref[...] = m_s[...] + jnp.log(l_s[...])

    tok = (ATT_B, ATT_W)
    feat = (ATT_W, ATT_B)
    return pl.pallas_call(
        body,
        out_shape=(jax.ShapeDtypeStruct((1024, T), BF16), jax.ShapeDtypeStruct((ATT_HEADS // ATT_HS, ATT_HS, T), F32)),
        grid=(B, ATT_HEADS // ATT_HS, nq // 2, nq + 1),
        in_specs=[pl.BlockSpec(tok, mp["k_tok"]), pl.BlockSpec(feat, mp["q_feat"]), pl.BlockSpec(feat, mp["k_feat"]),
                  pl.BlockSpec((None, ATT_B, ATT_B), mp["bias"])],
        out_specs=(pl.BlockSpec(feat, mp["q_feat"]), pl.BlockSpec((None, ATT_HS, ATT_B), mp["lse"])),
        scratch_shapes=[pltpu.VMEM((ATT_HS, ATT_B), F32), pltpu.VMEM((ATT_HS, ATT_B), F32),
                        pltpu.VMEM((ATT_W, ATT_B), F32), pltpu.VMEM((ATT_HS, ATT_B, ATT_B), F32)],
        name=name, compiler_params=_cp("parallel", "parallel", "arbitrary", "arbitrary"))(kn, qT, vT, bias)


def _att_scores(k_ref, qT_ref, v_ref, doT_ref, s_s, dp_s):
    for h in range(ATT_HS):
        rs = slice(HEAD_DIM * h, HEAD_DIM * (h + 1))
        s_s[h] = _dot(k_ref[:, rs], qT_ref[rs, :], NN)
        dp_s[h] = _dot(v_ref[:, rs].astype(BF16), doT_ref[rs, :].astype(BF16), NN)


def _att_p_ds(s_s, dp_s, doT_ref, oT_ref, lse_ref, bv, h):
    rs = slice(HEAD_DIM * h, HEAD_DIM * (h + 1))
    delta = jnp.sum(doT_ref[rs, :] * oT_ref[rs, :].astype(F32), axis=0, keepdims=True)
    p = jnp.exp(s_s[h] + bv - lse_ref[h:h + 1, :])
    return p, p * (dp_s[h] - delta)


def _att_bwd(name, kn, qT, proj, qn, knT, bias, doT, oT, lse, dyn, dproj, B):
    T = kn.shape[0]
    S = T // B
    nq = S // ATT_B
    qk = _col_pair(nq)
    mp = _att_maps(nq, qk)

    def body(k_ref, qT_ref, v_ref, q_ref, kT_ref, bias_ref, doT_ref, oT_ref, lse_ref, do_ref, buf_ref,
             dqT_ref, dk_ref, dv_ref, dk_s, dv_s, dq_s, s_s, dp_s):
        r, c = pl.program_id(2), pl.program_id(3)
        qi, kj = qk(r, c)

        @pl.when((r == 0) & (c == 0))
        def _():
            dq_s[...] = jnp.zeros_like(dq_s)

        @pl.when(qi == kj)
        def _():
            dk_s[...] = jnp.zeros_like(dk_s)
            dv_s[...] = jnp.zeros_like(dv_s)

        bv = bias_ref[...]
        _att_scores(k_ref, qT_ref, v_ref, doT_ref, s_s, dp_s)
        dq_blk = dq_s.at[qi]
        for h in range(ATT_HS):
            rs = slice(HEAD_DIM * h, HEAD_DIM * (h + 1))
            p, ds = _att_p_ds(s_s, dp_s, doT_ref, oT_ref, lse_ref, bv, h)
            dsb = ds.astype(BF16)
            dv_s[h] += _dot(p.astype(BF16), do_ref[:, rs].astype(BF16), NN)
            dk_s[h] += _dot(dsb, q_ref[:, rs], NN)
            dq_blk[rs, :] += _dot(kT_ref[rs, :], dsb, NN)

        @pl.when(qi == nq - 1)
        def _():
            for h in range(ATT_HS):
                rs = slice(HEAD_DIM * h, HEAD_DIM * (h + 1))
                dk_ref[:, rs] = dk_s[h] * ATT_SCALE
                dv_ref[:, rs] = dv_s[h].astype(BF16)

        @pl.when((r == nq // 2 - 1) & (c == nq))
        def _():
            for q in range(nq):
                dqT_ref[:, ATT_B * q:ATT_B * (q + 1)] = dq_s[q] * ATT_SCALE

    tok = (ATT_B, ATT_W)
    feat = (ATT_W, ATT_B)
    v_cb = COL_V // ATT_W
    return pl.pallas_call(
        body,
        out_shape=(jax.ShapeDtypeStruct((1024, T), F32), jax.ShapeDtypeStruct((T, 1024), F32),
                   jax.ShapeDtypeStruct(dproj.shape, dproj.dtype)),
        grid=(B, ATT_HEADS // ATT_HS, nq // 2, nq + 1),
        in_specs=[pl.BlockSpec(tok, mp["k_tok"]), pl.BlockSpec(feat, mp["q_feat"]), pl.BlockSpec(tok, mp["v_tok"]),
                  pl.BlockSpec(tok, mp["q_tok"]), pl.BlockSpec(feat, mp["k_feat"]),
                  pl.BlockSpec((None, ATT_B, ATT_B), mp["bias"]),
                  pl.BlockSpec(feat, mp["q_feat"]), pl.BlockSpec(feat, mp["q_feat"]),
                  pl.BlockSpec((None, ATT_HS, ATT_B), mp["lse"]), pl.BlockSpec(tok, mp["do_tok"]), ANY],
        out_specs=(pl.BlockSpec((ATT_W, S), lambda b, g, r, c: (g, b)),
                   pl.BlockSpec(tok, mp["k_tok"]),
                   pl.BlockSpec(tok, lambda b, g, r, c: (b * nq + qk(r, c)[1], v_cb + g))),
        input_output_aliases={10: 2},
        scratch_shapes=[pltpu.VMEM((ATT_HS, ATT_B, HEAD_DIM), F32), pltpu.VMEM((ATT_HS, ATT_B, HEAD_DIM), F32),
                        pltpu.VMEM((nq, ATT_W, ATT_B), F32),
                        pltpu.VMEM((ATT_HS, ATT_B, ATT_B), F32), pltpu.VMEM((ATT_HS, ATT_B, ATT_B), F32)],
        name=name, compiler_params=_cp("parallel", "parallel", "arbitrary", "arbitrary"))(
            kn, qT, proj, qn, knT, bias, doT, oT, lse, dyn, dproj)


def _group_cols(v):
    return v.reshape(SSD_GROUPS, 4)


def _ssd_params(p):
    rows = jnp.stack([_group_cols(p["dt_bias"]), _group_cols(p["a_log"]), _group_cols(p["d_skip"])], axis=1)
    return rows, jnp.swapaxes(rows, 1, 2)


def _dymix(name, dx, wout):
    T = dx.shape[0]

    def body(dx_ref, w_ref, o_ref):
        dxb = dx_ref[...].astype(BF16)
        for n in range(N_SHARD):
            o_ref[:, MIX_SH * n:MIX_SH * (n + 1)] = _dot(dxb, w_ref[n], NT)

    return pl.pallas_call(
        body, out_shape=jax.ShapeDtypeStruct((T, MIX_W), F32), grid=(T // ROW_T,),
        in_specs=[pl.BlockSpec((ROW_T, D_MODEL), lambda i: (i, 0)),
                  pl.BlockSpec((N_SHARD, MIX_SH, D_MODEL), lambda i: (0, 0, 0))],
        out_specs=pl.BlockSpec((ROW_T, MIX_W), lambda i: (i, 0)), name=name, compiler_params=_cp("parallel"))(dx, wout)


def _mixer_fwd(tag, x1, h2, p, weights, bias, B):
    T = x1.shape[0]
    S = T // B
    nt = T // ROW_T
    wi = weights("win", h2)
    win, cw = wi["win"], wi["cw"]
    tm = min(GU_T, T)
    proj = _mm(tag + "_proj",
               [(h2, pl.BlockSpec((tm, D_MODEL), lambda j, i, k: (i, 0)),
                 win, pl.BlockSpec((D_MODEL, PROJ_TN), lambda j, i, k: (0, j)))],
               jax.ShapeDtypeStruct((T, IN_PAD), F32), pl.BlockSpec((tm, PROJ_TN), lambda j, i, k: (i, j)),
               (IN_PAD // PROJ_TN, T // tm, 1), NN, (tm, PROJ_TN))
    xc = _conv_fwd(tag + "_conv", proj, cw, p["conv_b"][None], B)
    dtraw = proj[:, COL_DT:COL_DT + SSD_HEADS].reshape(T, SSD_GROUPS, 4)
    dtc = jnp.transpose(dtraw, (1, 0, 2))
    dtr = jnp.transpose(dtraw, (1, 2, 0))
    pcol, prow = _ssd_params(p)
    Y, y_ssd, hs = _ssd_fwd(tag + "_ssd", xc, proj, dtc, dtr, pcol, prow, p["ssd_norm"][None], B)
    qn = _headnorm_fwd(tag + "_qn", proj, COL_Q // 1024, p["q_norm"][None])
    kn = _headnorm_fwd(tag + "_kn", proj, COL_K // 1024, p["k_norm"][None])
    qT = (qn * ATT_SCALE).T
    oT, lse = _att_fwd(tag + "_att", kn, qT, proj[:, COL_V:COL_V + 1024].T.astype(BF16), bias, B)
    ymix = jnp.concatenate([y_ssd, oT.T], axis=1)
    rest = weights("rest", ymix)
    o = _with_norm(T, p["ffn2_norm"][None])
    x2, h3 = _mm(tag + "_out",
                 [(ymix, pl.BlockSpec((ROW_T, MIX_SH), lambda i, n, k, j=j: (i, j)),
                   rest["wout"], pl.BlockSpec((None, MIX_SH, D_MODEL), lambda i, n, k, j=j: (j, 0, 0)))
                  for j in range(N_SHARD)],
                 o.pop("out_shape"), o.pop("out_spec"), (nt, 1, 1), NN, (ROW_T, D_MODEL), res=(x1, _row3()), **o)
    saved = dict(x1=x1, h2=h2, proj=proj, xc=xc, dtc=dtc, dtr=dtr, Y=Y, hs=hs,
                 qn=qn, kn=kn, qT=qT, oT=oT, lse=lse, ymix=ymix, win=win, cw=cw, wout=rest["wout"])
    return x2, h3, saved


def _mixer_bwd(tag, dx2, dx2_b, sv, p, bias, B):
    T = dx2.shape[0]
    S = T // B
    nt = T // ROW_T
    sg = {}
    dymix = _dymix(tag + "_dymix", dx2_b, sv["wout"])
    tkw = min(TK_W, T)
    gwout = _mm(tag + "_dwout",
                [(sv["ymix"], pl.BlockSpec((tkw, MIX_SH), lambda m, n, k: (k, m)),
                  dx2_b, pl.BlockSpec((tkw, D_MODEL), lambda m, n, k: (k, 0)))],
                jax.ShapeDtypeStruct((N_SHARD, MIX_SH, D_MODEL), BF16),
                pl.BlockSpec((None, MIX_SH, D_MODEL), lambda m, n, k: (m, 0, 0)),
                (N_SHARD, 1, T // tkw), TN, (MIX_SH, D_MODEL))
    proj = sv["proj"]
    doT = dymix[:, 1024:].T
    dproj = lax.empty((T, IN_PAD), BF16)
    dqT, dkn, dproj = _att_bwd(tag + "_attb", sv["kn"], sv["qT"], proj, sv["qn"], sv["kn"].T, bias, doT, sv["oT"],
                               sv["lse"], dymix, dproj, B)
    dproj, sg["q_norm"] = _headnorm_bwd(tag + "_qnb", dqT.T, proj, COL_Q // 1024, p["q_norm"][None], dproj)
    dproj, sg["k_norm"] = _headnorm_bwd(tag + "_knb", dkn, proj, COL_K // 1024, p["k_norm"][None], dproj)
    pcol, prow = _ssd_params(p)
    dxs, dB, dC, dproj, ddt, dpar, dnw = _ssd_bwd(tag + "_ssdb", dymix, sv["Y"], sv["xc"], proj, sv["dtc"], sv["dtr"],
                                                  pcol, prow, p["ssd_norm"][None], sv["hs"], dproj, B)
    dpar = jnp.sum(dpar, axis=0)
    sg["dt_bias"] = dpar[:, 0, :].reshape(SSD_HEADS)
    sg["a_log"] = dpar[:, 1, :].reshape(SSD_HEADS)
    sg["d_skip"] = dpar[:, 2, :].reshape(SSD_HEADS)
    sg["ssd_norm"] = jnp.sum(dnw, axis=0)
    dproj, sg["conv_w"], sg["conv_b"] = _conv_bwd(tag + "_convb", proj, dxs, dB, dC, sv["cw"], p["conv_b"][None],
                                                  dproj, B)
    ddt16 = jnp.transpose(ddt, (1, 0, 2)).reshape(T, SSD_HEADS)
    dproj = lax.dynamic_update_slice(dproj, jnp.pad(ddt16, ((0, 0), (0, IN_PAD - COL_DT - SSD_HEADS))).astype(BF16),
                                     (0, COL_DT))
    win = sv["win"]
    gwin = _mm(tag + "_dwin",
               [(sv["h2"], pl.BlockSpec((tkw, D_MODEL), lambda n, m, k: (k, 0)),
                 dproj, pl.BlockSpec((tkw, PROJ_TN), lambda n, m, k: (k, n)))],
               jax.ShapeDtypeStruct((D_MODEL, IN_PAD), BF16), pl.BlockSpec((D_MODEL, PROJ_TN), lambda n, m, k: (0, n)),
               (IN_PAD // PROJ_TN, 1, T // tkw), TN, (D_MODEL, PROJ_TN))
    dx1, sg["mix_norm"], dx1_b = _mm(
        tag + "_dh2",
        [(dproj, pl.BlockSpec((ROW_T, PROJ_TN), lambda i, n, k, j=j: (i, j)),
          win, pl.BlockSpec((D_MODEL, PROJ_TN), lambda i, n, k, j=j: (0, j))) for j in range(IN_PAD // PROJ_TN)],
        *_rmsb_out(T), (nt, 1, 1), NT, (ROW_T, D_MODEL), post="rmsb",
        post_in=[(sv["x1"], _row3()), (p["mix_norm"][None], _vec3()), (dx2, _row3())])
    return dx1, dx1_b, sg, gwout, gwin


def _win_pack(w):
    return jnp.concatenate([w[:, :3072], w[:, 3088:], w[:, 3072:3088],
                            jnp.zeros((w.shape[0], IN_PAD - IN_PROJ), w.dtype)], axis=1)


def _win_unpack(g):
    return jnp.concatenate([g[:, :3072], g[:, COL_DT:COL_DT + SSD_HEADS], g[:, 3072:COL_DT]], axis=1)


DT_LO = IN_SH * 2 - COL_Q


def _win_from_shards(sh):
    main = IN_SH - DT_LO
    return jnp.concatenate([sh[0], sh[1][:, :main], sh[2][:, SSD_HEADS - DT_LO:], sh[3], sh[1][:, main:],
                            sh[2][:, :SSD_HEADS - DT_LO], jnp.zeros((sh.shape[1], IN_PAD - IN_PROJ), sh.dtype)], axis=1)


def _win_to_shards(g):
    main = IN_SH - DT_LO
    a, b = IN_SH + main, IN_SH + 2 * main
    return jnp.stack([g[:, :IN_SH],
                      jnp.concatenate([g[:, IN_SH:a], g[:, COL_DT:COL_DT + DT_LO]], axis=1),
                      jnp.concatenate([g[:, COL_DT + DT_LO:COL_DT + SSD_HEADS], g[:, a:b]], axis=1),
                      g[:, b:COL_DT]])


def _local_step(x, target, small, weights, scatter, B):
    T = x.shape[0]
    bias = _att_bias((T // B) // ATT_B)
    saved = []
    xl = x
    hl = _rms_fwd("l0f1_rms", x, small["ffn1_norm"][0][None])
    for l in range(DEPTH):
        tag = "l%d" % l
        p = {k: v[l] for k, v in small.items()}
        w1 = weights(l, "ffn1", hl)
        (x1, h2), ffn1, d1 = _ffn_fwd(tag + "f1", xl, hl, w1["g1"], w1["u1"],
                                      lambda after, l=l: weights(l, "ffn1d", after)["d1"], p["mix_norm"][None])
        x2, h3, sv = _mixer_fwd(tag, x1, h2, p, functools.partial(weights, l), bias, B)
        w2 = weights(l, "rest", x2)
        nxt = small["ffn1_norm"][l + 1][None] if l + 1 < DEPTH else None
        xo, ffn2, _ = _ffn_fwd(tag + "f2", x2, h3, w2["g2"], w2["u2"], w2["d2"], nxt)
        xl, hl = xo if nxt is not None else (xo, None)
        saved.append((ffn1, sv, ffn2, dict(g1=w1["g1"], u1=w1["u1"], d1=d1), w2))
    d, db, lsum = _loss_grad("loss", xl, target)
    sgrads = [None] * DEPTH
    for l in reversed(range(DEPTH)):
        tag = "l%db" % l
        p = {k: v[l] for k, v in small.items()}
        ffn1, sv, ffn2, w1, w2 = saved[l]
        sg = {}
        d, sg["ffn2_norm"], db = _ffn_bwd(tag + "f2", d, db, ffn2, p["ffn2_norm"][None], w2["g2"], w2["u2"], w2["d2"],
                                          lambda gg, gu, gd, c, l=l: scatter(l, "ffn2", dict(g2=gg, u2=gu, d2=gd), c))
        d, db, sgm, gwout, gwin = _mixer_bwd(tag, d, db, sv, p, bias, B)
        sg.update(sgm)
        db = scatter(l, "mixer", dict(wout=gwout, win=gwin), db)
        d, sg["ffn1_norm"], db = _ffn_bwd(tag + "f1", d, db, ffn1, p["ffn1_norm"][None], w1["g1"], w1["u1"], w1["d1"],
                                          lambda gg, gu, gd, c, l=l: scatter(l, "ffn1", dict(g1=gg, u1=gu, d1=gd), c))
        sgrads[l] = sg
    return lsum, d, sgrads


MESH = pl.DeviceIdType.MESH
ANY = pl.BlockSpec(memory_space=pl.ANY)


def _place():
    return lax.axis_index("x"), lax.axis_index("y"), lax.axis_index("c")


def _other_chips(x, y):
    return [(1 - x, y), (x, 1 - y), (1 - x, 1 - y)]


HBM = pl.BlockSpec(memory_space=pltpu.HBM)
SEM = pl.BlockSpec(memory_space=pltpu.SEMAPHORE)
EFFECT = pltpu.SideEffectType.DATAFLOW_SIDE_EFFECTING


def _hbm(a):
    return pltpu.with_memory_space_constraint(a, pltpu.HBM)


def _my_half(ref, c):
    hr = ref.shape[0] // 2
    return ref.at[pl.ds(pl.multiple_of(c * hr, 16), hr)]


def _exchange(gather, layer, halves, src, land, send, recv, n, act):
    x, y, c = _place()
    for k, (px, py) in enumerate(_other_chips(x, y)):
        for a in range(n):
            if gather:
                s_out, d_out, d_in = src[a].at[layer], land[a].at[2 * x + y], land[a].at[2 * px + py]
                if halves is not None and halves[a]:
                    s_out, d_out, d_in = _my_half(s_out, c), _my_half(d_out, c), _my_half(d_in, c)
            else:
                s_out, d_out, d_in = src[a].at[2 * px + py], land[a].at[k], land[a].at[k]
            act(pltpu.make_async_remote_copy(
                src_ref=s_out, dst_ref=d_out if act is _start else d_in, send_sem=send.at[k * n + a],
                recv_sem=recv.at[k * n + a], device_id=(px, py, c), device_id_type=MESH))


def _start(cp):
    cp.start()


def _finish(cp):
    cp.wait_send()
    cp.wait_recv()


def _exchange_start(name, gather, layer, srcs, carry, halves=None):
    n = len(srcs)
    lands = [lax.empty(((N_SHARD,) + s.shape[1:]) if gather else ((3,) + s.shape[1:]), s.dtype) for s in srcs]

    def body(*refs):
        _exchange(gather, layer, halves, refs[:n], refs[n:2 * n], refs[2 * n + 1], refs[2 * n + 2], n, _start)

    srcs = [_hbm(a) for a in srcs]
    thru = [_hbm(a) for a in lands + [carry]]
    out = pl.pallas_call(
        body, name=name,
        out_shape=(pltpu.SemaphoreType.DMA((3 * n,)), pltpu.SemaphoreType.DMA((3 * n,)),
                   *[pltpu.HBM(a.shape, a.dtype) for a in thru]),
        in_specs=[HBM] * (2 * n + 1), out_specs=(SEM, SEM, *[HBM] * (n + 1)),
        input_output_aliases={n + i: 2 + i for i in range(n + 1)},
        compiler_params=pltpu.CompilerParams(has_side_effects=EFFECT))(*srcs, *thru)
    return dict(gather=gather, layer=layer, halves=halves, send=out[0], recv=out[1], srcs=srcs,
                lands=list(out[2:2 + n])), out[-1]


def _exchange_wait(name, ex, after):
    n = len(ex["srcs"])

    def body(*refs):
        _exchange(ex["gather"], ex["layer"], ex["halves"], refs[:n], refs[n:2 * n], refs[2 * n], refs[2 * n + 1], n,
                  _finish)

    out = pl.pallas_call(
        body, name=name, out_shape=[pltpu.HBM(a.shape, a.dtype) for a in ex["lands"]],
        in_specs=[HBM] * (2 * n) + [SEM, SEM, ANY], out_specs=[HBM] * n,
        input_output_aliases={n + i: i for i in range(n)},
        compiler_params=pltpu.CompilerParams(has_side_effects=EFFECT))(
            *ex["srcs"], *ex["lands"], ex["send"], ex["recv"], after)
    return list(out)


def _sibling_fill(name, lands):
    n = len(lands)

    def body(*refs):
        land = refs[:n]
        send, recv = refs[2 * n], refs[2 * n + 1]
        x, y, c = _place()
        for k, (px, py) in enumerate(_other_chips(x, y)):
            for a in range(n):
                slot = land[a].at[2 * px + py]
                pltpu.make_async_remote_copy(src_ref=_my_half(slot, c), dst_ref=_my_half(slot, c),
                                             send_sem=send.at[k * n + a], recv_sem=recv.at[k * n + a],
                                             device_id=(x, y, 1 - c), device_id_type=MESH).start()
        for k, (px, py) in enumerate(_other_chips(x, y)):
            for a in range(n):
                slot = land[a].at[2 * px + py]
                cp = pltpu.make_async_remote_copy(src_ref=_my_half(slot, c), dst_ref=_my_half(slot, 1 - c),
                                                  send_sem=send.at[k * n + a], recv_sem=recv.at[k * n + a],
                                                  device_id=(x, y, 1 - c), device_id_type=MESH)
                cp.wait_recv()
                cp.wait_send()

    return pl.pallas_call(
        body, out_shape=[jax.ShapeDtypeStruct(a.shape, a.dtype) for a in lands],
        in_specs=[ANY] * n, out_specs=[ANY] * n, input_output_aliases={i: i for i in range(n)},
        scratch_shapes=[pltpu.SemaphoreType.DMA((3 * n,)), pltpu.SemaphoreType.DMA((3 * n,))],
        name=name)(*lands)


def _swap_sibling(name, parts):
    n = len(parts)

    def body(*refs):
        src, dst = refs[:n], refs[n:2 * n]
        send, recv = refs[2 * n:]
        x, y, c = _place()
        cps = [pltpu.make_async_remote_copy(src_ref=src[a], dst_ref=dst[a], send_sem=send.at[a], recv_sem=recv.at[a],
                                            device_id=(x, y, 1 - c), device_id_type=MESH) for a in range(n)]
        for cp in cps:
            cp.start()
        for cp in cps:
            cp.wait_recv()
        for cp in cps:
            cp.wait_send()

    return pl.pallas_call(
        body, out_shape=[jax.ShapeDtypeStruct(p.shape, p.dtype) for p in parts],
        in_specs=[ANY] * n, out_specs=[ANY] * n,
        scratch_shapes=[pltpu.SemaphoreType.DMA((n,)), pltpu.SemaphoreType.DMA((n,))],
        name=name)(*parts)


def _allreduce_small(name, v, after):
    R = v.shape[0]

    def body(v_ref, after_ref, o_ref, buf, send, recv):
        x, y, c = _place()
        me = 4 * x + 2 * y + c
        buf[me] = v_ref[...]
        cps = []
        for k in range(1, 8):
            fx, fy, fc = (k >> 2) & 1, (k >> 1) & 1, k & 1
            px = 1 - x if fx else x
            py = 1 - y if fy else y
            pc = 1 - c if fc else c
            cp = pltpu.make_async_remote_copy(src_ref=v_ref, dst_ref=buf.at[me], send_sem=send.at[k - 1],
                                              recv_sem=recv.at[k - 1], device_id=(px, py, pc), device_id_type=MESH)
            cp.start()
            cps.append((cp, 4 * px + 2 * py + pc))
        for k, (cp, peer) in enumerate(cps):
            pltpu.make_async_remote_copy(src_ref=v_ref, dst_ref=buf.at[peer], send_sem=send.at[k], recv_sem=recv.at[k],
                                         device_id=(x, y, c), device_id_type=MESH).wait_recv()
        for cp, _ in cps:
            cp.wait_send()
        acc = buf[0]
        for d in range(1, 8):
            acc = acc + buf[d]
        o_ref[...] = acc

    return pl.pallas_call(
        body, out_shape=jax.ShapeDtypeStruct((R, 128), F32),
        in_specs=[pl.BlockSpec(memory_space=pltpu.VMEM), ANY], out_specs=pl.BlockSpec(memory_space=pltpu.VMEM),
        scratch_shapes=[pltpu.VMEM((8, R, 128), F32), pltpu.SemaphoreType.DMA((7,)), pltpu.SemaphoreType.DMA((7,))],
        name=name)(v, after)


TILE_BYTES = 1600 * 1024


def _row_tile(r, c=1024):
    for t in (512, 352, 256, 128, 64, 32, 16, 8):
        if r % t == 0 and (t * c * 4 <= TILE_BYTES or t == 8):
            return t
    raise ValueError(r)


def _sum4(name, me, parts, got):
    _, R, C = parts.shape
    tr = _row_tile(R, C)

    def body(me_ref, o_ref, g_ref, s_ref):
        s = o_ref[...].astype(F32)
        for k in range(3):
            s = s + g_ref[k].astype(F32)
        s_ref[...] = s.astype(BF16)

    return pl.pallas_call(
        body, out_shape=jax.ShapeDtypeStruct((R, C), BF16),
        grid_spec=pltpu.PrefetchScalarGridSpec(
            num_scalar_prefetch=1, grid=(R // tr,),
            in_specs=[pl.BlockSpec((None, tr, C), lambda i, me_ref: (me_ref[0], i, 0)),
                      pl.BlockSpec((3, tr, C), lambda i, me_ref: (0, i, 0))],
            out_specs=pl.BlockSpec((tr, C), lambda i, me_ref: (i, 0))),
        name=name, compiler_params=_cp("parallel"))(me, parts, got)


def _adamw(name, w, gparts, m, v):
    R, C = w.shape
    tr = _row_tile(R, C)
    ng = len(gparts)
    c1 = 1.0 - ADAM_B1 ** ADAM_STEP
    c2 = 1.0 - ADAM_B2 ** ADAM_STEP

    def body(*refs):
        w_ref = refs[0]
        g_refs = refs[1:1 + ng]
        m_ref, v_ref, go_ref, d_ref, mo_ref, vo_ref = refs[1 + ng:]
        g = g_refs[0][...]
        for r in g_refs[1:]:
            g = g + r[...]
        mn = ADAM_B1 * m_ref[...] + (1.0 - ADAM_B1) * g
        vn = ADAM_B2 * v_ref[...] + (1.0 - ADAM_B2) * (g * g)
        go_ref[...] = g
        mo_ref[...] = mn
        vo_ref[...] = vn
        d_ref[...] = -ADAM_LR * ((mn / c1) / (jnp.sqrt(vn / c2) + ADAM_EPS) + ADAM_WD * w_ref[...])

    blk = pl.BlockSpec((tr, C), lambda i: (i, 0))
    osh = jax.ShapeDtypeStruct((R, C), F32)
    return pl.pallas_call(
        body, out_shape=(osh, osh, osh, osh), grid=(R // tr,), in_specs=[blk] * (3 + ng), out_specs=(blk,) * 4,
        name=name, compiler_params=_cp("parallel"))(w, *gparts, m, v)


def _adamw_layers(name, w, sums, m, v):
    _, R, C = w.shape
    tr = _row_tile(R, C)
    nr = R // tr
    c1 = 1.0 - ADAM_B1 ** ADAM_STEP
    c2 = 1.0 - ADAM_B2 ** ADAM_STEP

    def body(w_ref, a0, b0, a1, b1, m_ref, v_ref, go_ref, d_ref, mo_ref, vo_ref):
        f = lambda r: r[...].astype(F32)
        g = jnp.where(pl.program_id(0) == 0, f(a0) + f(b0), f(a1) + f(b1))
        mn = ADAM_B1 * m_ref[...] + (1.0 - ADAM_B1) * g
        vn = ADAM_B2 * v_ref[...] + (1.0 - ADAM_B2) * (g * g)
        go_ref[...] = g
        mo_ref[...] = mn
        vo_ref[...] = vn
        d_ref[...] = -ADAM_LR * ((mn / c1) / (jnp.sqrt(vn / c2) + ADAM_EPS) + ADAM_WD * w_ref[...])

    blk = pl.BlockSpec((None, tr, C), lambda l, i: (l, i, 0))
    lay0 = pl.BlockSpec((tr, C), lambda l, i: (jnp.where(l == 0, i, nr - 1), 0))
    lay1 = pl.BlockSpec((tr, C), lambda l, i: (jnp.where(l == 1, i, 0), 0))
    oblk = pl.BlockSpec((tr, C), lambda l, i: (l * nr + i, 0))
    osh = jax.ShapeDtypeStruct((DEPTH * R, C), F32)
    res = pl.pallas_call(
        body, out_shape=(osh, osh, osh, osh), grid=(DEPTH, nr),
        in_specs=[blk, lay0, lay0, lay1, lay1, blk, blk], out_specs=(oblk,) * 4,
        name=name, compiler_params=_cp("arbitrary", "arbitrary"))(w, *sums[0], *sums[1], m, v)
    return [r.reshape(w.shape) for r in res]


BIG = [("ffn1_w_gate", "g1"), ("ffn1_w_up", "u1"), ("ffn1_w_down", "d1"), ("w_in", "win"), ("w_out", "wout"),
       ("ffn2_w_gate", "g2"), ("ffn2_w_up", "u2"), ("ffn2_w_down", "d2")]
SMALL = ["ffn1_norm", "mix_norm", "conv_b", "dt_bias", "a_log", "d_skip", "ssd_norm", "q_norm", "k_norm", "ffn2_norm"]
WEIGHTS = ["ffn1_norm", "ffn1_w_gate", "ffn1_w_up", "ffn1_w_down", "mix_norm", "w_in", "conv_w", "conv_b", "dt_bias",
           "a_log", "d_skip", "ssd_norm", "q_norm", "k_norm", "w_out", "ffn2_norm", "ffn2_w_gate", "ffn2_w_up",
           "ffn2_w_down"]
CONV_SH = CONV_DIM // N_SHARD
TRANSPOSED = ("g1", "u1", "g2", "u2")
GATHER_GROUPS = [(0, "ffn1", ["g1", "u1"]), (0, "ffn1d", ["d1"]), (0, "win", ["win", "cw"]),
                 (0, "rest", ["wout", "g2", "u2", "d2"]),
                 (1, "all", ["g1", "u1", "d1", "win", "cw", "wout", "g2", "u2", "d2"])]


def _pad128(v):
    v = v.reshape(-1)
    return jnp.pad(v, (0, (-v.shape[0]) % 128))


def _pack(pieces):
    flat, offs, pos = [], [], 0
    for p in pieces:
        q = _pad128(p.astype(F32))
        offs.append(pos)
        pos += q.shape[0] // 128
        flat.append(q)
    total = -(-pos // 8) * 8
    out = jnp.concatenate(flat + [jnp.zeros(((total - pos) * 128,), F32)]).reshape(total, 128)
    return out, offs


def _unpack(packed, offs, shapes):
    out = []
    for off, shp in zip(offs, shapes):
        n = int(np.prod(shp))
        rows = -(-n // 128)
        out.append(packed[off:off + rows].reshape(-1)[:n].reshape(shp))
    return out


def kernel(x, ffn1_norm, ffn1_w_gate, ffn1_w_up, ffn1_w_down, mix_norm, w_in, conv_w, conv_b, dt_bias, a_log, d_skip, ssd_norm, q_norm, k_norm, w_out, ffn2_norm, ffn2_w_gate, ffn2_w_up, ffn2_w_down, loss_target, m_ffn1_norm, m_ffn1_w_gate, m_ffn1_w_up, m_ffn1_w_down, m_mix_norm, m_w_in, m_conv_w, m_conv_b, m_dt_bias, m_a_log, m_d_skip, m_ssd_norm, m_q_norm, m_k_norm, m_w_out, m_ffn2_norm, m_ffn2_w_gate, m_ffn2_w_up, m_ffn2_w_down, v_ffn1_norm, v_ffn1_w_gate, v_ffn1_w_up, v_ffn1_w_down, v_mix_norm, v_w_in, v_conv_w, v_conv_b, v_dt_bias, v_a_log, v_d_skip, v_ssd_norm, v_q_norm, v_k_norm, v_w_out, v_ffn2_norm, v_ffn2_w_gate, v_ffn2_w_up, v_ffn2_w_down):
    A = dict(locals())
    ix, iy, ic = _place()
    me = 2 * ix + iy
    B, S, _ = x.shape
    T = B * S

    view = lambda a, key: jnp.swapaxes(a, 1, 2) if key in TRANSPOSED else a
    own = {key: view(A[name], key).astype(BF16) for name, key in BIG}
    own["cw"] = conv_w
    exs, first_norm = [], ffn1_norm
    split = lambda l, key: l == 0 and key != "cw"
    for gi, (l, _, keys) in enumerate(GATHER_GROUPS):
        ex, first_norm = _exchange_start("gather_start%d" % gi, True, l, [own[key] for key in keys], first_norm,
                                         [split(l, key) for key in keys])
        exs.append(ex)
    landed = {}

    def weights(l, group, after):
        gi = [i for i, (gl, gname, _) in enumerate(GATHER_GROUPS) if gl == l and gname in (group, "all")][0]
        if gi not in landed:
            lands = _exchange_wait("gather_wait%d" % gi, exs[gi], after)
            keys = GATHER_GROUPS[gi][2]
            halved = [i for i, key in enumerate(keys) if split(l, key)]
            if halved:
                for i, whole in zip(halved, _sibling_fill("gather_fill%d" % gi, [lands[i] for i in halved])):
                    lands[i] = whole
            landed[gi] = {}
            for key, land in zip(GATHER_GROUPS[gi][2], lands):
                full = lax.dynamic_update_slice(land, own[key][l][None], (me, 0, 0))
                if key == "win":
                    full = _win_from_shards(full)
                if key == "cw":
                    full = jnp.transpose(full, (1, 0, 2)).reshape(CONV_K, CONV_DIM)
                landed[gi][key] = full
        return landed[gi]

    pending = []

    def scatter(l, group, grads, carry):
        keys = sorted(grads)
        arrs = [grads[key] for key in keys]
        if "win" in grads:
            arrs[keys.index("win")] = _win_to_shards(grads["win"])
        ex, carry = _exchange_start("scatter_start_l%d_%s" % (l, group), False, None, arrs, carry)
        pending.append((l, keys, ex))
        return carry

    small = {name: A[name] for name in SMALL}
    small["ffn1_norm"] = first_norm
    lsum, dx, sgrads = _local_step(x.reshape(T, D_MODEL), loss_target.reshape(T, D_MODEL), small, weights, scatter, B)

    names = SMALL + ["conv_w"]
    shapes = [A[n].shape for n in SMALL] + [(DEPTH, CONV_K, CONV_DIM), ()]
    pieces = [jnp.stack([sgrads[l][n].reshape(shp[1:]) for l in range(DEPTH)]) for n, shp in zip(names, shapes)]
    pieces.append(0.5 / D_MODEL * jnp.sum(lsum))
    packed, offs = _pack(pieces)

    sums, theirs, out = {}, {}, {}
    me1 = jnp.reshape(me, (1,)).astype(jnp.int32)

    def update(tag, after):
        todo = [k for k in sums if k not in theirs]
        theirs.update(zip(todo, _swap_sibling("swap_sibling_" + tag, [sums[k] for k in todo])))
        for name, key in BIG:
            if name not in out and all((key, l) in theirs for l in range(DEPTH)):
                res = _adamw_layers("adamw_" + key, view(A[name], key),
                                    [(sums[key, l], theirs[key, l]) for l in range(DEPTH)],
                                    view(A["m_" + name], key), view(A["v_" + name], key))
                out[name] = [view(r, key) for r in res]
                after = res[0]
        return after

    after = dx
    for idx, (l, keys, ex) in enumerate(pending):
        if idx == len(pending) - 1:
            after = update("a", after)
        lands = _exchange_wait("scatter_wait%d" % idx, ex, after)
        for key, g, got in zip(keys, ex["srcs"], lands):
            sums[key, l] = after = _sum4("sum_%s_l%d" % (key, l), me1, g, got)
    after = update("b", after)

    red = _unpack(_allreduce_small("allreduce_small", packed, after), offs, shapes)
    loss = red[-1]
    sg = dict(zip(names, red[:-1]))

    wp, offs = _pack([A[n] for n in SMALL])
    gp, _ = _pack([sg[n] for n in SMALL])
    mp, _ = _pack([A["m_" + n] for n in SMALL])
    vp, _ = _pack([A["v_" + n] for n in SMALL])
    res = _adamw("adamw_small", wp, [gp], mp, vp)
    shapes = [A[n].shape for n in SMALL]
    res = [_unpack(r, offs, shapes) for r in res]
    for i, n in enumerate(SMALL):
        out[n] = [res[q][i] for q in range(4)]
    gcw = lax.dynamic_slice_in_dim(sg["conv_w"], me * CONV_SH, CONV_SH, axis=2)
    flat = lambda a: a.reshape(DEPTH * CONV_K, CONV_SH)
    res = _adamw("adamw_conv_w", flat(conv_w), [flat(gcw)], flat(m_conv_w), flat(v_conv_w))
    out["conv_w"] = [r.reshape(conv_w.shape) for r in res]

    outs = [loss, dx.reshape(B, S, D_MODEL)]
    for q in range(4):
        outs += [out[n][q] for n in WEIGHTS]
    return tuple(outs)
```

```python
import functools
import math

import numpy as np
import jax
import jax.numpy as jnp
from jax import lax
from jax.experimental import pallas as pl
from jax.experimental.pallas import tpu as pltpu

F32 = jnp.float32
BF16 = jnp.bfloat16

D_MODEL = 1024
DEPTH = 2
N_SHARD = 4
D_FF = 2816
FF_SH = D_FF // N_SHARD
SSD_HEADS = 16
HEAD_DIM = 64
SSD_GROUPS = 4
GROUP_W = 256
SSD_STATE = 128
CONV_K = 4
CONV_DIM = 2048
ATT_HEADS = 16
MIX_W = 2048
MIX_SH = MIX_W // N_SHARD
IN_PROJ = 6160
IN_SH = IN_PROJ // N_SHARD
IN_PAD = 6272
PROJ_TN = 896
COL_Z, COL_XBC, COL_Q, COL_K, COL_V, COL_DT = 0, 1024, 3072, 4096, 5120, 6144
EPS = 1e-6
NEG = -1e30
SSD_L = 512
ATT_B = 512
ROW_T = 512
HALF_T = ROW_T // 2
GU_T = 2048
TK_W = 4096
CONV_CT = 256
CONV_R = 256
PAD_R = 8

ADAM_LR, ADAM_B1, ADAM_B2, ADAM_EPS, ADAM_WD, ADAM_STEP = 0.001, 0.9, 0.999, 1e-08, 0.01, 10

NN = (((1,), (0,)), ((), ()))
NT = (((1,), (1,)), ((), ()))
TN = (((0,), (0,)), ((), ()))

VMEM_LIMIT = 56 * 1024 * 1024


def _cp(*sem):
    return pltpu.CompilerParams(dimension_semantics=sem, vmem_limit_bytes=VMEM_LIMIT)


def _dot(a, b, dims):
    return lax.dot_general(a, b, dims, preferred_element_type=F32)


def _sigmoid(x):
    return 0.5 * jnp.tanh(0.5 * x) + 0.5


def _softplus(x):
    return jnp.maximum(x, 0.0) + jnp.log(1.0 + jnp.exp(-jnp.abs(x)))


def _mm(name, pairs, out_shape, out_spec, grid, dims, acc_shape, res=None, scale=1.0, post=None, post_in=()):
    nk = grid[2]
    npair = len(pairs)
    npost = len(post_in)

    def body(*refs):
        ab = refs[:2 * npair]
        pos = 2 * npair
        res_ref = None
        if res is not None:
            res_ref = refs[pos]
            pos += 1
        pin = refs[pos:pos + npost]
        pos += npost
        out_ref = refs[pos]
        pos += 1
        if post is not None:
            out2_ref = refs[pos]
            pos += 1
        if post == "rmsb":
            out3_ref = refs[pos]
            pos += 1
        s = None
        for p in range(npair):
            d = _dot(ab[2 * p][...].astype(BF16), ab[2 * p + 1][...].astype(BF16), dims)
            s = d if s is None else s + d

        def finish(r):
            if scale != 1.0:
                r = r * scale
            if res_ref is not None:
                r = r + res_ref[...]
            if post == "rmsb":
                @pl.when(pl.program_id(0) == 0)
                def _():
                    out2_ref[...] = jnp.zeros_like(out2_ref)

                xv = pin[0][...]
                rr = lax.rsqrt(jnp.mean(xv * xv, axis=-1, keepdims=True) + EPS)
                xhat = xv * rr
                dxhat = r * pin[1][...]
                dx = pin[2][...] + rr * (dxhat - xhat * jnp.mean(dxhat * xhat, axis=-1, keepdims=True))
                out_ref[...] = dx
                out2_ref[...] += jnp.sum(r * xhat, axis=0, keepdims=True)
                out3_ref[...] = dx.astype(BF16)
                return
            out_ref[...] = r.astype(out_ref.dtype)
            if post == "norm":
                rr = lax.rsqrt(jnp.mean(r * r, axis=-1, keepdims=True) + EPS)
                out2_ref[...] = (r * rr * pin[0][...]).astype(BF16)

        if nk == 1:
            finish(s)
            return
        acc = refs[pos]
        k = pl.program_id(2)

        @pl.when(k == 0)
        def _():
            acc[...] = s

        @pl.when(k > 0)
        def _():
            acc[...] += s

        @pl.when(k == nk - 1)
        def _():
            finish(acc[...])

    args, specs = [], []
    for a, a_spec, b, b_spec in pairs:
        args += [a, b]
        specs += [a_spec, b_spec]
    for arr, spec in ([res] if res is not None else []) + list(post_in):
        args.append(arr)
        specs.append(spec)
    sems = ("arbitrary",) * 3 if post == "rmsb" else ("parallel", "parallel", "arbitrary")
    return pl.pallas_call(
        body, out_shape=out_shape, grid=grid, in_specs=specs, out_specs=out_spec,
        scratch_shapes=[] if nk == 1 else [pltpu.VMEM(acc_shape, F32)], name=name,
        compiler_params=_cp(*sems))(*args)


def _rms_fwd(name, x, w):
    T = x.shape[0]

    def body(x_ref, w_ref, o_ref):
        xv = x_ref[...]
        r = lax.rsqrt(jnp.mean(xv * xv, axis=-1, keepdims=True) + EPS)
        o_ref[...] = (xv * r * w_ref[...]).astype(BF16)

    return pl.pallas_call(
        body, out_shape=jax.ShapeDtypeStruct((T, D_MODEL), BF16), grid=(T // ROW_T,),
        in_specs=[pl.BlockSpec((ROW_T, D_MODEL), lambda i: (i, 0)), pl.BlockSpec((1, D_MODEL), lambda i: (0, 0))],
        out_specs=pl.BlockSpec((ROW_T, D_MODEL), lambda i: (i, 0)), name=name, compiler_params=_cp("parallel"))(x, w)


def _loss_grad(name, y, t):
    T = y.shape[0]

    def body(y_ref, t_ref, dy_ref, dyb_ref, l_ref):
        @pl.when(pl.program_id(0) == 0)
        def _():
            l_ref[...] = jnp.zeros_like(l_ref)

        e = y_ref[...] - t_ref[...]
        dy = e * (1.0 / D_MODEL)
        dy_ref[...] = dy
        dyb_ref[...] = dy.astype(BF16)
        l_ref[...] += jnp.sum(e * e, axis=0, keepdims=True)

    row = pl.BlockSpec((ROW_T, D_MODEL), lambda i: (i, 0))
    vec = pl.BlockSpec((1, D_MODEL), lambda i: (0, 0))
    return pl.pallas_call(
        body, out_shape=(jax.ShapeDtypeStruct((T, D_MODEL), F32), jax.ShapeDtypeStruct((T, D_MODEL), BF16),
                         jax.ShapeDtypeStruct((1, D_MODEL), F32)),
        grid=(T // ROW_T,), in_specs=[row, row], out_specs=(row, row, vec), name=name,
        compiler_params=_cp("arbitrary"))(y, t)


def _ffn_gate_up(name, h, wg, wu):
    T = h.shape[0]
    tm = min(GU_T, T)

    def body(h_ref, wg_ref, wu_ref, dgf_ref, duf_ref, a_ref):
        for r in range(0, tm, HALF_T):
            rows = slice(r, r + HALF_T)
            hv = h_ref[rows, :]
            g = _dot(hv, wg_ref[...], NT)
            u = _dot(hv, wu_ref[...], NT)
            sg = _sigmoid(g)
            silu = g * sg
            dgf_ref[rows, :] = (u * (sg * (1.0 + g * (1.0 - sg)))).astype(BF16)
            duf_ref[rows, :] = silu.astype(BF16)
            a_ref[rows, :] = (silu * u).astype(BF16)

    wspec = pl.BlockSpec((None, FF_SH, D_MODEL), lambda j, i: (j, 0, 0))
    ospec = pl.BlockSpec((None, tm, FF_SH), lambda j, i: (j, i, 0))
    osh = jax.ShapeDtypeStruct((N_SHARD, T, FF_SH), BF16)
    return pl.pallas_call(
        body, out_shape=(osh, osh, osh), grid=(N_SHARD, T // tm),
        in_specs=[pl.BlockSpec((tm, D_MODEL), lambda j, i: (i, 0)), wspec, wspec],
        out_specs=(ospec, ospec, ospec), name=name, compiler_params=_cp("parallel", "parallel"))(h, wg, wu)


def _ffn_dact(name, dx, wd, g, u):
    T = dx.shape[0]
    tm = min(GU_T, T)

    def body(dx_ref, wd_ref, g_ref, u_ref, dg_ref, du_ref):
        for r in range(0, tm, HALF_T):
            rows = slice(r, r + HALF_T)
            da = 0.5 * _dot(dx_ref[rows, :].astype(BF16), wd_ref[...], NT)
            dg_ref[rows, :] = (da * g_ref[rows, :].astype(F32)).astype(BF16)
            du_ref[rows, :] = (da * u_ref[rows, :].astype(F32)).astype(BF16)

    aspec = pl.BlockSpec((None, tm, FF_SH), lambda j, i: (j, i, 0))
    osh = jax.ShapeDtypeStruct((N_SHARD, T, FF_SH), BF16)
    return pl.pallas_call(
        body, out_shape=(osh, osh), grid=(N_SHARD, T // tm),
        in_specs=[pl.BlockSpec((tm, D_MODEL), lambda j, i: (i, 0)),
                  pl.BlockSpec((None, FF_SH, D_MODEL), lambda j, i: (j, 0, 0)), aspec, aspec],
        out_specs=(aspec, aspec), name=name, compiler_params=_cp("parallel", "parallel"))(dx, wd, g, u)


def _row3():
    return pl.BlockSpec((ROW_T, D_MODEL), lambda i, n, k: (i, 0))


def _vec3():
    return pl.BlockSpec((1, D_MODEL), lambda i, n, k: (0, 0))


def _with_norm(T, next_nw):
    if next_nw is None:
        return dict(out_shape=jax.ShapeDtypeStruct((T, D_MODEL), F32), out_spec=_row3())
    return dict(out_shape=(jax.ShapeDtypeStruct((T, D_MODEL), F32), jax.ShapeDtypeStruct((T, D_MODEL), BF16)),
                out_spec=(_row3(), _row3()), post="norm", post_in=[(next_nw, _vec3())])


def _ffn_fwd(tag, x, h, wg, wu, wd, next_nw):
    T = x.shape[0]
    g, u, a = _ffn_gate_up(tag + "_gu", h, wg, wu)
    if callable(wd):
        wd = wd(a)
    nt = T // ROW_T
    o = _with_norm(T, next_nw)
    xo = _mm(tag + "_down",
             [(a, pl.BlockSpec((None, ROW_T, FF_SH), lambda i, n, k, j=j: (j, i, 0)),
               wd, pl.BlockSpec((None, FF_SH, D_MODEL), lambda i, n, k, j=j: (j, 0, 0))) for j in range(N_SHARD)],
             o.pop("out_shape"), o.pop("out_spec"), (nt, 1, 1), NN, (ROW_T, D_MODEL),
             res=(x, _row3()), scale=0.5, **o)
    return xo, (x, h, g, u, a), wd


def _rmsb_out(T):
    f = jax.ShapeDtypeStruct
    return (f((T, D_MODEL), F32), f((1, D_MODEL), F32), f((T, D_MODEL), BF16)), (_row3(), _vec3(), _row3())


def _ffn_bwd(tag, dxo, dxo_b, saved, nw, wg, wu, wd, emit):
    x, h, g, u, a = saved
    T = x.shape[0]
    nt = T // ROW_T
    tkw = min(TK_W, T)
    nw_t = T // tkw
    dg, du = _ffn_dact(tag + "_dact", dxo_b, wd, g, u)
    actw = lambda f: pl.BlockSpec((None, tkw, FF_SH), f)
    gd = _mm(tag + "_dwd",
             [(a, actw(lambda m, n, k: (m, k, 0)), dxo_b, pl.BlockSpec((tkw, D_MODEL), lambda m, n, k: (k, 0)))],
             jax.ShapeDtypeStruct((N_SHARD, FF_SH, D_MODEL), BF16),
             pl.BlockSpec((None, FF_SH, D_MODEL), lambda m, n, k: (m, 0, 0)),
             (N_SHARD, 1, nw_t), TN, (FF_SH, D_MODEL), scale=0.5)
    hspec = pl.BlockSpec((tkw, D_MODEL), lambda j, n, k: (k, 0))
    gsh = jax.ShapeDtypeStruct((N_SHARD, FF_SH, D_MODEL), BF16)
    gspec = pl.BlockSpec((None, FF_SH, D_MODEL), lambda j, n, k: (j, 0, 0))
    gg = _mm(tag + "_dwg", [(dg, actw(lambda j, n, k: (j, k, 0)), h, hspec)], gsh, gspec,
             (N_SHARD, 1, nw_t), TN, (FF_SH, D_MODEL))
    gu = _mm(tag + "_dwu", [(du, actw(lambda j, n, k: (j, k, 0)), h, hspec)], gsh, gspec,
             (N_SHARD, 1, nw_t), TN, (FF_SH, D_MODEL))
    dg = emit(gg, gu, gd, dg)
    act = lambda j: pl.BlockSpec((None, ROW_T, FF_SH), lambda i, n, k: (j, i, 0))
    wsp = lambda j: pl.BlockSpec((None, FF_SH, D_MODEL), lambda i, n, k: (j, 0, 0))
    return _mm(tag + "_dh",
               [(dd, act(j), w, wsp(j)) for j in range(N_SHARD) for dd, w in ((dg, wg), (du, wu))],
               *_rmsb_out(T), (nt, 1, 1), NN, (ROW_T, D_MODEL), post="rmsb",
               post_in=[(x, _row3()), (nw, _vec3()), (dxo, _row3())])


def _seq_rows(ref, start, size, S):
    lo, hi = max(start, 0), min(start + size, S)
    parts = [ref[pl.ds(lo, hi - lo), :]]
    if lo > start:
        parts.insert(0, jnp.zeros((lo - start, ref.shape[1]), F32))
    if start + size > hi:
        parts.append(jnp.zeros((start + size - hi, ref.shape[1]), F32))
    return parts[0] if len(parts) == 1 else jnp.concatenate(parts, axis=0)


XBC_CB = COL_XBC // CONV_CT


def _conv_fwd(name, proj, w, b, B):
    T = proj.shape[0]
    S = T // B
    C = CONV_DIM

    def body(x_ref, w_ref, b_ref, o_ref):
        wv = w_ref[...]
        for c in range(S // CONV_R):
            r0 = c * CONV_R
            ch = _seq_rows(x_ref, r0 - PAD_R, CONV_R + PAD_R, S)
            pre = ch[PAD_R:] * wv[3:4] + b_ref[...]
            for s in range(1, CONV_K):
                pre = pre + pltpu.roll(ch, s, axis=0)[PAD_R:] * wv[3 - s:4 - s]
            o_ref[pl.ds(r0, CONV_R), :] = pre * _sigmoid(pre)

    return pl.pallas_call(
        body, out_shape=jax.ShapeDtypeStruct((T, C), F32), grid=(B, C // CONV_CT),
        in_specs=[pl.BlockSpec((S, CONV_CT), lambda bi, ci: (bi, XBC_CB + ci)),
                  pl.BlockSpec((CONV_K, CONV_CT), lambda bi, ci: (0, ci)),
                  pl.BlockSpec((1, CONV_CT), lambda bi, ci: (0, ci))],
        out_specs=pl.BlockSpec((S, CONV_CT), lambda bi, ci: (bi, ci)), name=name,
        compiler_params=_cp("parallel", "parallel"))(proj, w, b)


def _conv_bwd(name, proj, dxs, dB, dC, w, b, dproj, B):
    T = proj.shape[0]
    S = T // B
    C = CONV_DIM
    RW = CONV_R + PAD_R
    nx, nb = dxs.shape[1] // CONV_CT, dB.shape[1] // CONV_CT

    def body(x_ref, dx_in, db_in, dc_in, w_ref, b_ref, buf_ref, dx_ref, dw_ref, db_ref):
        @pl.when(pl.program_id(1) == 0)
        def _():
            dw_ref[...] = jnp.zeros_like(dw_ref)
            db_ref[...] = jnp.zeros_like(db_ref)

        ci = pl.program_id(0)
        wv = w_ref[...]
        dw = [jnp.zeros((1, CONV_CT), F32) for _ in range(CONV_K)]
        db = jnp.zeros((1, CONV_CT), F32)
        for c in range(S // CONV_R):
            r0 = c * CONV_R
            ch = _seq_rows(x_ref, r0 - PAD_R, RW + PAD_R, S)
            xs = [ch[PAD_R:]] + [pltpu.roll(ch, s, axis=0)[PAD_R:] for s in range(1, CONV_K)]
            pre = b_ref[...] + xs[0] * wv[3:4]
            for s in range(1, CONV_K):
                pre = pre + xs[s] * wv[3 - s:4 - s]
            sg = _sigmoid(pre)
            dout = jnp.where(ci < nx, _seq_rows(dx_in, r0, RW, S),
                             jnp.where(ci < nx + nb, _seq_rows(db_in, r0, RW, S), _seq_rows(dc_in, r0, RW, S)))
            dpre = dout * (sg * (1.0 + pre * (1.0 - sg)))
            dx = dpre[:CONV_R] * wv[3:4]
            for s in range(1, CONV_K):
                dx = dx + pltpu.roll(dpre, RW - s, axis=0)[:CONV_R] * wv[3 - s:4 - s]
            dx_ref[pl.ds(r0, CONV_R), :] = dx.astype(BF16)
            dcur = dpre[:CONV_R]
            db = db + jnp.sum(dcur, axis=0, keepdims=True)
            for s in range(CONV_K):
                dw[3 - s] = dw[3 - s] + jnp.sum(dcur * xs[s][:CONV_R], axis=0, keepdims=True)
        db_ref[...] += db
        for k in range(CONV_K):
            dw_ref[k:k + 1, :] += dw[k]

    seq = lambda f: pl.BlockSpec((S, CONV_CT), f)
    return pl.pallas_call(
        body,
        out_shape=(jax.ShapeDtypeStruct(dproj.shape, dproj.dtype), jax.ShapeDtypeStruct((CONV_K, C), F32),
                   jax.ShapeDtypeStruct((1, C), F32)),
        grid=(C // CONV_CT, B),
        in_specs=[seq(lambda ci, bi: (bi, XBC_CB + ci)),
                  seq(lambda ci, bi: (bi, jnp.minimum(ci, nx - 1))),
                  seq(lambda ci, bi: (bi, jnp.clip(ci - nx, 0, nb - 1))),
                  seq(lambda ci, bi: (bi, jnp.clip(ci - nx - nb, 0, nb - 1))),
                  pl.BlockSpec((CONV_K, CONV_CT), lambda ci, bi: (0, ci)),
                  pl.BlockSpec((1, CONV_CT), lambda ci, bi: (0, ci)), ANY],
        out_specs=(seq(lambda ci, bi: (bi, XBC_CB + ci)),
                   pl.BlockSpec((CONV_K, CONV_CT), lambda ci, bi: (0, ci)),
                   pl.BlockSpec((1, CONV_CT), lambda ci, bi: (0, ci))),
        input_output_aliases={6: 0},
        name=name, compiler_params=_cp("parallel", "arbitrary"))(proj, dxs, dB, dC, w, b, dproj)


def _tri_sum(tri, x, dims, tri_first, terms=3):
    out, rest = None, x
    for t in range(terms):
        part = rest.astype(BF16)
        if t + 1 < terms:
            rest = rest - part.astype(F32)
        d = _dot(tri, part, dims) if tri_first else _dot(part, tri, dims)
        out = d if out is None else out + d
    return out


def _total(x):
    return jnp.sum(jnp.sum(x, axis=0, keepdims=True), axis=-1, keepdims=True)


def _ssd_common(dtc_ref, dtr_ref, pcol_ref, prow_ref, b_ref, c_ref):
    L = SSD_L
    bias_c, alog_c = pcol_ref[0:1, :], pcol_ref[1:2, :]
    a_c = -jnp.exp(alog_c)
    dt_c = _softplus(dtc_ref[...] + bias_c)
    row = lax.broadcasted_iota(jnp.int32, (L, L), 0)
    col = lax.broadcasted_iota(jnp.int32, (L, L), 1)
    causal = row >= col
    tri = causal.astype(BF16)
    cum_c = _tri_sum(tri, dt_c * a_c, NN, True)
    a_r = -jnp.exp(prow_ref[:, 1:2])
    dt_r = _softplus(dtr_ref[...] + prow_ref[:, 0:1])
    cum_r = _tri_sum(tri, dt_r * a_r, NT, False)
    bb = b_ref[...].astype(BF16)
    cb = c_ref[...].astype(BF16)
    G = _dot(cb, bb, NT)
    return a_c, dt_c, causal, tri, cum_c, cum_r, bb, cb, G


def _ssd_fwd(name, xc, proj, dtc, dtr, pcol, prow, nw, B):
    T = xc.shape[0]
    S = T // B
    nb = S // SSD_L
    L = SSD_L

    def body(xs_ref, b_ref, c_ref, z_ref, dtc_ref, dtr_ref, pcol_ref, prow_ref, nw_ref, y_ref, yn_ref, hs_ref, H, yo_s):
        @pl.when(pl.program_id(2) == 0)
        def _():
            H[...] = jnp.zeros_like(H)

        a_c, dt_c, causal, tri, cum_c, cum_r, bb, cb, G = _ssd_common(dtc_ref, dtr_ref, pcol_ref, prow_ref, b_ref, c_ref)
        dsk = pcol_ref[2:3, :]
        clast = cum_c[L - 1:L, :]
        bf = b_ref[...]
        for h in range(4):
            hs_ref[h] = H[h]
            yo_s[h] = _dot(cb, H[h].astype(BF16), NN)
        for h in range(4):
            sl = slice(HEAD_DIM * h, HEAD_DIM * (h + 1))
            cc = cum_c[:, h:h + 1]
            lm = jnp.exp(jnp.where(causal, cc - cum_r[h:h + 1, :], NEG))
            M = (G * lm).astype(BF16)
            xh = xs_ref[:, sl]
            Xb = (xh * dt_c[:, h:h + 1]).astype(BF16)
            Hh = H[h]
            y = _dot(M, Xb, NN) + jnp.exp(cc) * yo_s[h]
            y_ref[:, sl] = y + dsk[:, h:h + 1] * xh
            cl = clast[:, h:h + 1]
            Bw = (bf * jnp.exp(cl - cc)).astype(BF16)
            H[h] = jnp.exp(cl) * Hh + _dot(Bw, Xb, TN)
        zv = z_ref[...]
        y2 = y_ref[...] * (zv * _sigmoid(zv))
        r = lax.rsqrt(jnp.mean(y2 * y2, axis=-1, keepdims=True) + EPS)
        yn_ref[...] = (y2 * r * nw_ref[...]).astype(BF16)

    rowi = lambda b, g, i: b * nb + i
    grp = pl.BlockSpec((L, GROUP_W), lambda b, g, i: (rowi(b, g, i), g))
    return pl.pallas_call(
        body,
        out_shape=(jax.ShapeDtypeStruct((T, 1024), F32), jax.ShapeDtypeStruct((T, 1024), BF16),
                   jax.ShapeDtypeStruct((B, SSD_GROUPS, nb, 4, SSD_STATE, HEAD_DIM), F32)),
        grid=(B, SSD_GROUPS, nb),
        in_specs=[grp,
                  pl.BlockSpec((L, SSD_STATE), lambda b, g, i: (rowi(b, g, i), 8 + g)),
                  pl.BlockSpec((L, SSD_STATE), lambda b, g, i: (rowi(b, g, i), 12 + g)),
                  grp,
                  pl.BlockSpec((None, L, 4), lambda b, g, i: (g, rowi(b, g, i), 0)),
                  pl.BlockSpec((None, 4, L), lambda b, g, i: (g, 0, rowi(b, g, i))),
                  pl.BlockSpec((None, 3, 4), lambda b, g, i: (g, 0, 0)),
                  pl.BlockSpec((None, 4, 3), lambda b, g, i: (g, 0, 0)),
                  pl.BlockSpec((1, GROUP_W), lambda b, g, i: (0, g))],
        out_specs=(grp, grp,
                   pl.BlockSpec((None, None, None, 4, SSD_STATE, HEAD_DIM), lambda b, g, i: (b, g, i, 0, 0, 0))),
        scratch_shapes=[pltpu.VMEM((4, SSD_STATE, HEAD_DIM), F32), pltpu.VMEM((4, L, HEAD_DIM), F32)], name=name,
        compiler_params=_cp("parallel", "parallel", "arbitrary"))(xc, xc, xc, proj, dtc, dtr, pcol, prow, nw)


def _ssd_bwd(name, dyn, Y, xc, proj, dtc, dtr, pcol, prow, nw, hs, dproj, B):
    T = xc.shape[0]
    S = T // B
    nb = S // SSD_L
    L = SSD_L

    def body(dyn_ref, y_ref, xs_ref, b_ref, c_ref, z_ref, dtc_ref, dtr_ref, pcol_ref, prow_ref, nw_ref, hs_ref, buf_ref,
             dxs_ref, db_ref, dc_ref, dz_ref, ddt_ref, dpar_ref, dnw_ref, dH, dm_s, dxo_s, ea_s, ex_s):
        @pl.when(pl.program_id(2) == 0)
        def _():
            dH[...] = jnp.zeros_like(dH)
            dpar_ref[...] = jnp.zeros_like(dpar_ref)
            dnw_ref[...] = jnp.zeros_like(dnw_ref)

        a_c, dt_c, causal, tri, cum_c, cum_r, bb, cb, G = _ssd_common(dtc_ref, dtr_ref, pcol_ref, prow_ref, b_ref, c_ref)
        dsk = pcol_ref[2:3, :]
        clast = cum_c[L - 1:L, :]
        bf = b_ref[...]
        cf = c_ref[...]
        Yv = y_ref[...]
        zv = z_ref[...]
        sz = _sigmoid(zv)
        silu = zv * sz
        y2 = Yv * silu
        r = lax.rsqrt(jnp.mean(y2 * y2, axis=-1, keepdims=True) + EPS)
        yhat = y2 * r
        dyv = dyn_ref[...]
        dnw_ref[...] += jnp.sum(dyv * yhat, axis=0, keepdims=True)
        dyhat = dyv * nw_ref[...]
        dy2 = r * (dyhat - yhat * jnp.mean(dyhat * yhat, axis=-1, keepdims=True))
        dY = dy2 * silu
        dz_ref[...] = (dy2 * Yv * (sz * (1.0 + zv * (1.0 - sz)))).astype(BF16)

        lane4 = lax.broadcasted_iota(jnp.int32, (1, 4), 1)
        dG = jnp.zeros((L, L), F32)
        dBs = jnp.zeros((L, SSD_STATE), F32)
        dCs = jnp.zeros((L, SSD_STATE), F32)
        ddsk = jnp.zeros((1, 4), F32)
        dcl = jnp.zeros((1, 4), F32)
        for h in range(4):
            sl = slice(HEAD_DIM * h, HEAD_DIM * (h + 1))
            xb = (xs_ref[:, sl] * dt_c[:, h:h + 1]).astype(BF16)
            dm_s[h] = _dot(dY[:, sl].astype(BF16), xb, NT)
            dxo_s[h] = _dot(bb, dH[h].astype(BF16), NN)
        for h in range(4):
            sl = slice(HEAD_DIM * h, HEAD_DIM * (h + 1))
            onehot = (lane4 == h).astype(F32)
            cc = cum_c[:, h:h + 1]
            cl = clast[:, h:h + 1]
            lm = jnp.exp(jnp.where(causal, cc - cum_r[h:h + 1, :], NEG))
            M = (G * lm).astype(BF16)
            xh = xs_ref[:, sl]
            dth = dt_c[:, h:h + 1]
            X = xh * dth
            Xb = X.astype(BF16)
            dYh = dY[:, sl]
            dYb = dYh.astype(BF16)
            Hb = hs_ref[h].astype(BF16)
            dHh = dH[h]
            dHb = dHh.astype(BF16)
            alpha = jnp.exp(cc)
            beta = jnp.exp(cl - cc)
            dXoff = beta * dxo_s[h]
            dX = _dot(M, dYb, TN) + dXoff
            dG = dG + dm_s[h] * lm
            dCs = dCs + _dot((alpha * dYh).astype(BF16), Hb, NT)
            dBs = dBs + _dot((beta * X).astype(BF16), dHb, NT)
            ypre = Yv[:, sl] - dsk[:, h:h + 1] * xh
            ea_s[:, sl] = dYb.astype(F32) * ypre - Xb.astype(F32) * dX
            ex_s[:, sl] = dX * xh
            dcl_h = (_total(dHh * (jnp.exp(cl) * hs_ref[h])) + _total(Xb.astype(F32) * dXoff))
            dcl = dcl + dcl_h * onehot
            ddsk = ddsk + _total(dYh * xh) * onehot
            dxs_ref[:, sl] = dsk[:, h:h + 1] * dYh + dX * dth
            dH[h] = jnp.exp(cl) * dHh + _dot((alpha * cf).astype(BF16), dYb, TN)
        dGb = dG.astype(BF16)
        dc_ref[...] = _dot(dGb, bb, NN) + dCs
        db_ref[...] = _dot(dGb, cb, TN) + dBs
        feat = lax.broadcasted_iota(jnp.int32, (GROUP_W, 4), 0)
        head = lax.broadcasted_iota(jnp.int32, (GROUP_W, 4), 1) * HEAD_DIM
        sel = ((feat >= head) & (feat < head + HEAD_DIM)).astype(BF16)
        dA = _tri_sum(sel, ea_s[...], NN, False)
        ddtx = _tri_sum(sel, ex_s[...], NN, False)
        last = lax.broadcasted_iota(jnp.int32, (L, 1), 0) == L - 1
        dA = dA + jnp.where(last, dcl, 0.0)
        dadt = _tri_sum(tri, dA, TN, True)
        ddt = dadt * a_c + ddtx
        d_a = jnp.sum(dadt * dt_c, axis=0, keepdims=True)
        ddraw = ddt * _sigmoid(dtc_ref[...] + pcol_ref[0:1, :])
        ddt_ref[...] = ddraw
        dpar_ref[0:1, :] += jnp.sum(ddraw, axis=0, keepdims=True)
        dpar_ref[1:2, :] += d_a * a_c
        dpar_ref[2:3, :] += ddsk

    rowi = lambda b, g, i: b * nb + (nb - 1 - i)
    grp = pl.BlockSpec((L, GROUP_W), lambda b, g, i: (rowi(b, g, i), g))
    st = pl.BlockSpec((L, SSD_STATE), lambda b, g, i: (rowi(b, g, i), g))
    f = jax.ShapeDtypeStruct
    return pl.pallas_call(
        body,
        out_shape=(f((T, 1024), F32), f((T, 512), F32), f((T, 512), F32), f(dproj.shape, dproj.dtype),
                   f((SSD_GROUPS, T, 4), F32), f((B, SSD_GROUPS, 3, 4), F32), f((B, 1, 1024), F32)),
        grid=(B, SSD_GROUPS, nb),
        in_specs=[grp, grp, grp,
                  pl.BlockSpec((L, SSD_STATE), lambda b, g, i: (rowi(b, g, i), 8 + g)),
                  pl.BlockSpec((L, SSD_STATE), lambda b, g, i: (rowi(b, g, i), 12 + g)),
                  grp,
                  pl.BlockSpec((None, L, 4), lambda b, g, i: (g, rowi(b, g, i), 0)),
                  pl.BlockSpec((None, 4, L), lambda b, g, i: (g, 0, rowi(b, g, i))),
                  pl.BlockSpec((None, 3, 4), lambda b, g, i: (g, 0, 0)),
                  pl.BlockSpec((None, 4, 3), lambda b, g, i: (g, 0, 0)),
                  pl.BlockSpec((1, GROUP_W), lambda b, g, i: (0, g)),
                  pl.BlockSpec((None, None, None, 4, SSD_STATE, HEAD_DIM), lambda b, g, i: (b, g, nb - 1 - i, 0, 0, 0)),
                  ANY],
        out_specs=(grp, st, st, grp,
                   pl.BlockSpec((None, L, 4), lambda b, g, i: (g, rowi(b, g, i), 0)),
                   pl.BlockSpec((None, None, 3, 4), lambda b, g, i: (b, g, 0, 0)),
                   pl.BlockSpec((None, 1, GROUP_W), lambda b, g, i: (b, 0, g))),
        input_output_aliases={12: 3},
        scratch_shapes=[pltpu.VMEM((4, SSD_STATE, HEAD_DIM), F32), pltpu.VMEM((4, L, L), F32),
                        pltpu.VMEM((4, L, HEAD_DIM), F32), pltpu.VMEM((L, GROUP_W), F32),
                        pltpu.VMEM((L, GROUP_W), F32)], name=name,
        compiler_params=_cp("parallel", "parallel", "arbitrary"))(
            dyn, Y, xc, xc, xc, proj, dtc, dtr, pcol, prow, nw, hs, dproj)


def _head_sel():
    sel = (np.arange(1024)[:, None] // HEAD_DIM == np.arange(ATT_HEADS)[None, :]).astype(np.float32)
    return jnp.asarray(sel, BF16), jnp.asarray(sel.T, BF16)


def _head_rms(xv, sel, selT):
    ms = _tri_sum(sel, xv * xv, NN, False, 1) * (1.0 / HEAD_DIM)
    return _tri_sum(selT, lax.rsqrt(ms + EPS), NN, False, 2)


def _headnorm_fwd(name, proj, col_block, w):
    T = proj.shape[0]
    sel, selT = _head_sel()

    def body(x_ref, w_ref, sel_ref, selT_ref, o_ref):
        xv = x_ref[...]
        o_ref[...] = (xv * _head_rms(xv, sel_ref[...], selT_ref[...]) * w_ref[...]).astype(BF16)

    full = lambda shp: pl.BlockSpec(shp, lambda i: (0, 0))
    return pl.pallas_call(
        body, out_shape=jax.ShapeDtypeStruct((T, 1024), BF16), grid=(T // ROW_T,),
        in_specs=[pl.BlockSpec((ROW_T, 1024), lambda i: (i, col_block)), full((1, 1024)), full((1024, ATT_HEADS)),
                  full((ATT_HEADS, 1024))],
        out_specs=pl.BlockSpec((ROW_T, 1024), lambda i: (i, 0)), name=name, compiler_params=_cp("parallel"))(
            proj, jnp.tile(w, (1, ATT_HEADS)), sel, selT)


def _headnorm_bwd(name, dn, proj, col_block, w, dproj):
    T = proj.shape[0]
    sel, selT = _head_sel()

    def body(dn_ref, x_ref, w_ref, sel_ref, selT_ref, buf_ref, dx_ref, dw_ref):
        @pl.when(pl.program_id(0) == 0)
        def _():
            dw_ref[...] = jnp.zeros_like(dw_ref)

        xv = x_ref[...]
        sl, slT = sel_ref[...], selT_ref[...]
        rb = _head_rms(xv, sl, slT)
        xhat = xv * rb
        dnv = dn_ref[...]
        dxhat = dnv * w_ref[...]
        mean = _tri_sum(slT, _tri_sum(sl, dxhat * xhat, NN, False, 2) * (1.0 / HEAD_DIM), NN, False, 2)
        dx_ref[...] = (rb * (dxhat - xhat * mean)).astype(BF16)
        dw_ref[...] += jnp.sum(dnv * xhat, axis=0, keepdims=True)

    here = pl.BlockSpec((ROW_T, 1024), lambda i: (i, col_block))
    full = lambda shp: pl.BlockSpec(shp, lambda i: (0, 0))
    dx, dw = pl.pallas_call(
        body, out_shape=(jax.ShapeDtypeStruct(dproj.shape, dproj.dtype), jax.ShapeDtypeStruct((1, 1024), F32)),
        grid=(T // ROW_T,),
        in_specs=[pl.BlockSpec((ROW_T, 1024), lambda i: (i, 0)), here, full((1, 1024)), full((1024, ATT_HEADS)),
                  full((ATT_HEADS, 1024)), ANY],
        out_specs=(here, full((1, 1024))), input_output_aliases={5: 0},
        name=name, compiler_params=_cp("arbitrary"))(dn, proj, jnp.tile(w, (1, ATT_HEADS)), sel, selT, dproj)
    return dx, jnp.sum(dw.reshape(ATT_HEADS, HEAD_DIM), axis=0, keepdims=True)


def _att_bias(nq):
    j = np.arange(ATT_B)[:, None]
    i = np.arange(ATT_B)[None, :]
    out = np.empty((nq, ATT_B, ATT_B), np.float32)
    for dblk in range(nq):
        dl = ATT_B * dblk + i - j
        cnt = ((dl >= 0) & (dl <= 128)).astype(np.float32)
        cnt += ((dl >= 0) & (dl % 4 == 0) & (dl <= 512))
        cnt += ((dl >= 0) & (dl % 16 == 0) & (dl <= 2048))
        out[dblk] = np.where(cnt > 0, np.log(np.maximum(cnt, 1.0)), NEG)
    return jnp.asarray(out)


def _row_pair(nq):
    def f(r, c):
        first = c <= r
        return jnp.where(first, r, nq - 1 - r), jnp.where(first, c, c - (r + 1))
    return f


def _col_pair(nq):
    def f(r, c):
        first = c < nq - r
        kj = jnp.where(first, r, nq - 1 - r)
        return jnp.where(first, r + c, nq - 1 - r + (c - (nq - r))), kj
    return f


ATT_SCALE = 1.0 / math.sqrt(HEAD_DIM)
ATT_HS = 8
ATT_W = ATT_HS * HEAD_DIM


def _att_maps(nq, qk):
    return dict(
        q_tok=lambda b, g, r, c: (b * nq + qk(r, c)[0], g),
        k_tok=lambda b, g, r, c: (b * nq + qk(r, c)[1], g),
        v_tok=lambda b, g, r, c: (b * nq + qk(r, c)[1], COL_V // ATT_W + g),
        q_feat=lambda b, g, r, c: (g, b * nq + qk(r, c)[0]),
        k_feat=lambda b, g, r, c: (g, b * nq + qk(r, c)[1]),
        bias=lambda b, g, r, c: (qk(r, c)[0] - qk(r, c)[1], 0, 0),
        lse=lambda b, g, r, c: (g, 0, b * nq + qk(r, c)[0]),
        do_tok=lambda b, g, r, c: (b * nq + qk(r, c)[0], 1024 // ATT_W + g))


def _att_fwd(name, kn, qT, vT, bias, B):
    T = kn.shape[0]
    nq = (T // B) // ATT_B
    qk = _row_pair(nq)
    mp = _att_maps(nq, qk)

    def body(k_ref, qT_ref, vT_ref, bias_ref, oT_ref, lse_ref, m_s, l_s, acc_s, s_s):
        qi, kj = qk(pl.program_id(2), pl.program_id(3))

        @pl.when(kj == 0)
        def _():
            m_s[...] = jnp.full_like(m_s, NEG)
            l_s[...] = jnp.zeros_like(l_s)
            acc_s[...] = jnp.zeros_like(acc_s)

        bv = bias_ref[...]
        for h in range(ATT_HS):
            rs = slice(HEAD_DIM * h, HEAD_DIM * (h + 1))
            s_s[h] = _dot(k_ref[:, rs], qT_ref[rs, :], NN)
        for h in range(ATT_HS):
            rs = slice(HEAD_DIM * h, HEAD_DIM * (h + 1))
            s = s_s[h] + bv
            m_prev = m_s[h:h + 1, :]
            m_new = jnp.maximum(m_prev, jnp.max(s, axis=0, keepdims=True))
            alpha = jnp.exp(m_prev - m_new)
            p = jnp.exp(s - m_new)
            l_s[h:h + 1, :] = alpha * l_s[h:h + 1, :] + jnp.sum(p, axis=0, keepdims=True)
            acc_s[rs, :] = alpha * acc_s[rs, :] + _dot(vT_ref[rs, :], p.astype(BF16), NN)
            m_s[h:h + 1, :] = m_new

        @pl.when(kj == qi)
        def _():
            for h in range(ATT_HS):
                rs = slice(HEAD_DIM * h, HEAD_DIM * (h + 1))
                oT_ref[rs, :] = (acc_s[rs, :] / l_s[h:h + 1, :]).astype(BF16)
            lse_ref[...] = m_s[...] + jnp.log(l_s[...])

    tok = (ATT_B, ATT_W)
    feat = (ATT_W, ATT_B)
    return pl.pallas_call(
        body,
        out_shape=(jax.ShapeDtypeStruct((1024, T), BF16), jax.ShapeDtypeStruct((ATT_HEADS // ATT_HS, ATT_HS, T), F32)),
        grid=(B, ATT_HEADS // ATT_HS, nq // 2, nq + 1),
        in_specs=[pl.BlockSpec(tok, mp["k_tok"]), pl.BlockSpec(feat, mp["q_feat"]), pl.BlockSpec(feat, mp["k_feat"]),
                  pl.BlockSpec((None, ATT_B, ATT_B), mp["bias"])],
        out_specs=(pl.BlockSpec(feat, mp["q_feat"]), pl.BlockSpec((None, ATT_HS, ATT_B), mp["lse"])),
        scratch_shapes=[pltpu.VMEM((ATT_HS, ATT_B), F32), pltpu.VMEM((ATT_HS, ATT_B), F32),
                        pltpu.VMEM((ATT_W, ATT_B), F32), pltpu.VMEM((ATT_HS, ATT_B, ATT_B), F32)],
        name=name, compiler_params=_cp("parallel", "parallel", "arbitrary", "arbitrary"))(kn, qT, vT, bias)


def _att_scores(k_ref, qT_ref, v_ref, doT_ref, s_s, dp_s):
    for h in range(ATT_HS):
        rs = slice(HEAD_DIM * h, HEAD_DIM * (h + 1))
        s_s[h] = _dot(k_ref[:, rs], qT_ref[rs, :], NN)
        dp_s[h] = _dot(v_ref[:, rs].astype(BF16), doT_ref[rs, :].astype(BF16), NN)


def _att_p_ds(s_s, dp_s, doT_ref, oT_ref, lse_ref, bv, h):
    rs = slice(HEAD_DIM * h, HEAD_DIM * (h + 1))
    delta = jnp.sum(doT_ref[rs, :] * oT_ref[rs, :].astype(F32), axis=0, keepdims=True)
    p = jnp.exp(s_s[h] + bv - lse_ref[h:h + 1, :])
    return p, p * (dp_s[h] - delta)


def _att_bwd(name, kn, qT, proj, qn, knT, bias, doT, oT, lse, dyn, dproj, B):
    T = kn.shape[0]
    S = T // B
    nq = S // ATT_B
    qk = _col_pair(nq)
    mp = _att_maps(nq, qk)

    def body(k_ref, qT_ref, v_ref, q_ref, kT_ref, bias_ref, doT_ref, oT_ref, lse_ref, do_ref, buf_ref,
             dqT_ref, dk_ref, dv_ref, dk_s, dv_s, dq_s, s_s, dp_s):
        r, c = pl.program_id(2), pl.program_id(3)
        qi, kj = qk(r, c)

        @pl.when((r == 0) & (c == 0))
        def _():
            dq_s[...] = jnp.zeros_like(dq_s)

        @pl.when(qi == kj)
        def _():
            dk_s[...] = jnp.zeros_like(dk_s)
            dv_s[...] = jnp.zeros_like(dv_s)

        bv = bias_ref[...]
        _att_scores(k_ref, qT_ref, v_ref, doT_ref, s_s, dp_s)
        dq_blk = dq_s.at[qi]
        for h in range(ATT_HS):
            rs = slice(HEAD_DIM * h, HEAD_DIM * (h + 1))
            p, ds = _att_p_ds(s_s, dp_s, doT_ref, oT_ref, lse_ref, bv, h)
            dsb = ds.astype(BF16)
            dv_s[h] += _dot(p.astype(BF16), do_ref[:, rs].astype(BF16), NN)
            dk_s[h] += _dot(dsb, q_ref[:, rs], NN)
            dq_blk[rs, :] += _dot(kT_ref[rs, :], dsb, NN)

        @pl.when(qi == nq - 1)
        def _():
            for h in range(ATT_HS):
                rs = slice(HEAD_DIM * h, HEAD_DIM * (h + 1))
                dk_ref[:, rs] = dk_s[h] * ATT_SCALE
                dv_ref[:, rs] = dv_s[h].astype(BF16)

        @pl.when((r == nq // 2 - 1) & (c == nq))
        def _():
            for q in range(nq):
                dqT_ref[:, ATT_B * q:ATT_B * (q + 1)] = dq_s[q] * ATT_SCALE

    tok = (ATT_B, ATT_W)
    feat = (ATT_W, ATT_B)
    v_cb = COL_V // ATT_W
    return pl.pallas_call(
        body,
        out_shape=(jax.ShapeDtypeStruct((1024, T), F32), jax.ShapeDtypeStruct((T, 1024), F32),
                   jax.ShapeDtypeStruct(dproj.shape, dproj.dtype)),
        grid=(B, ATT_HEADS // ATT_HS, nq // 2, nq + 1),
        in_specs=[pl.BlockSpec(tok, mp["k_tok"]), pl.BlockSpec(feat, mp["q_feat"]), pl.BlockSpec(tok, mp["v_tok"]),
                  pl.BlockSpec(tok, mp["q_tok"]), pl.BlockSpec(feat, mp["k_feat"]),
                  pl.BlockSpec((None, ATT_B, ATT_B), mp["bias"]),
                  pl.BlockSpec(feat, mp["q_feat"]), pl.BlockSpec(feat, mp["q_feat"]),
                  pl.BlockSpec((None, ATT_HS, ATT_B), mp["lse"]), pl.BlockSpec(tok, mp["do_tok"]), ANY],
        out_specs=(pl.BlockSpec((ATT_W, S), lambda b, g, r, c: (g, b)),
                   pl.BlockSpec(tok, mp["k_tok"]),
                   pl.BlockSpec(tok, lambda b, g, r, c: (b * nq + qk(r, c)[1], v_cb + g))),
        input_output_aliases={10: 2},
        scratch_shapes=[pltpu.VMEM((ATT_HS, ATT_B, HEAD_DIM), F32), pltpu.VMEM((ATT_HS, ATT_B, HEAD_DIM), F32),
                        pltpu.VMEM((nq, ATT_W, ATT_B), F32),
                        pltpu.VMEM((ATT_HS, ATT_B, ATT_B), F32), pltpu.VMEM((ATT_HS, ATT_B, ATT_B), F32)],
        name=name, compiler_params=_cp("parallel", "parallel", "arbitrary", "arbitrary"))(
            kn, qT, proj, qn, knT, bias, doT, oT, lse, dyn, dproj)


def _group_cols(v):
    return v.reshape(SSD_GROUPS, 4)


def _ssd_params(p):
    rows = jnp.stack([_group_cols(p["dt_bias"]), _group_cols(p["a_log"]), _group_cols(p["d_skip"])], axis=1)
    return rows, jnp.swapaxes(rows, 1, 2)


def _dymix(name, dx, wout):
    T = dx.shape[0]

    def body(dx_ref, w_ref, o_ref):
        dxb = dx_ref[...].astype(BF16)
        for n in range(N_SHARD):
            o_ref[:, MIX_SH * n:MIX_SH * (n + 1)] = _dot(dxb, w_ref[n], NT)

    return pl.pallas_call(
        body, out_shape=jax.ShapeDtypeStruct((T, MIX_W), F32), grid=(T // ROW_T,),
        in_specs=[pl.BlockSpec((ROW_T, D_MODEL), lambda i: (i, 0)),
                  pl.BlockSpec((N_SHARD, MIX_SH, D_MODEL), lambda i: (0, 0, 0))],
        out_specs=pl.BlockSpec((ROW_T, MIX_W), lambda i: (i, 0)), name=name, compiler_params=_cp("parallel"))(dx, wout)


def _mixer_fwd(tag, x1, h2, p, weights, bias, B):
    T = x1.shape[0]
    nt = T // ROW_T
    wi = weights("win", h2)
    win, cw = wi["win"], wi["cw"]
    tm = min(GU_T, T)
    proj = _mm(tag + "_proj",
               [(h2, pl.BlockSpec((tm, D_MODEL), lambda j, i, k: (i, 0)),
                 win, pl.BlockSpec((D_MODEL, PROJ_TN), lambda j, i, k: (0, j)))],
               jax.ShapeDtypeStruct((T, IN_PAD), F32), pl.BlockSpec((tm, PROJ_TN), lambda j, i, k: (i, j)),
               (IN_PAD // PROJ_TN, T // tm, 1), NN, (tm, PROJ_TN))
    xc = _conv_fwd(tag + "_conv", proj, cw, p["conv_b"][None], B)
    dtraw = proj[:, COL_DT:COL_DT + SSD_HEADS].reshape(T, SSD_GROUPS, 4)
    dtc = jnp.transpose(dtraw, (1, 0, 2))
    dtr = jnp.transpose(dtraw, (1, 2, 0))
    pcol, prow = _ssd_params(p)
    Y, y_ssd, hs = _ssd_fwd(tag + "_ssd", xc, proj, dtc, dtr, pcol, prow, p["ssd_norm"][None], B)
    qn = _headnorm_fwd(tag + "_qn", proj, COL_Q // 1024, p["q_norm"][None])
    kn = _headnorm_fwd(tag + "_kn", proj, COL_K // 1024, p["k_norm"][None])
    qT = (qn * ATT_SCALE).T
    oT, lse = _att_fwd(tag + "_att", kn, qT, proj[:, COL_V:COL_V + 1024].T.astype(BF16), bias, B)
    ymix = jnp.concatenate([y_ssd, oT.T], axis=1)
    rest = weights("rest", ymix)
    o = _with_norm(T, p["ffn2_norm"][None])
    x2, h3 = _mm(tag + "_out",
                 [(ymix, pl.BlockSpec((ROW_T, MIX_SH), lambda i, n, k, j=j: (i, j)),
                   rest["wout"], pl.BlockSpec((None, MIX_SH, D_MODEL), lambda i, n, k, j=j: (j, 0, 0)))
                  for j in range(N_SHARD)],
                 o.pop("out_shape"), o.pop("out_spec"), (nt, 1, 1), NN, (ROW_T, D_MODEL), res=(x1, _row3()), **o)
    saved = dict(x1=x1, h2=h2, proj=proj, xc=xc, dtc=dtc, dtr=dtr, Y=Y, hs=hs,
                 qn=qn, kn=kn, qT=qT, oT=oT, lse=lse, ymix=ymix, win=win, cw=cw, wout=rest["wout"])
    return x2, h3, saved


def _mixer_bwd(tag, dx2, dx2_b, sv, p, bias, B):
    T = dx2.shape[0]
    nt = T // ROW_T
    sg = {}
    dymix = _dymix(tag + "_dymix", dx2_b, sv["wout"])
    tkw = min(TK_W, T)
    gwout = _mm(tag + "_dwout",
                [(sv["ymix"], pl.BlockSpec((tkw, MIX_SH), lambda m, n, k: (k, m)),
                  dx2_b, pl.BlockSpec((tkw, D_MODEL), lambda m, n, k: (k, 0)))],
                jax.ShapeDtypeStruct((N_SHARD, MIX_SH, D_MODEL), BF16),
                pl.BlockSpec((None, MIX_SH, D_MODEL), lambda m, n, k: (m, 0, 0)),
                (N_SHARD, 1, T // tkw), TN, (MIX_SH, D_MODEL))
    proj = sv["proj"]
    doT = dymix[:, 1024:].T
    dproj = lax.empty((T, IN_PAD), BF16)
    dqT, dkn, dproj = _att_bwd(tag + "_attb", sv["kn"], sv["qT"], proj, sv["qn"], sv["kn"].T, bias, doT, sv["oT"],
                               sv["lse"], dymix, dproj, B)
    dproj, sg["q_norm"] = _headnorm_bwd(tag + "_qnb", dqT.T, proj, COL_Q // 1024, p["q_norm"][None], dproj)
    dproj, sg["k_norm"] = _headnorm_bwd(tag + "_knb", dkn, proj, COL_K // 1024, p["k_norm"][None], dproj)
    pcol, prow = _ssd_params(p)
    dxs, dB, dC, dproj, ddt, dpar, dnw = _ssd_bwd(tag + "_ssdb", dymix, sv["Y"], sv["xc"], proj, sv["dtc"], sv["dtr"],
                                                  pcol, prow, p["ssd_norm"][None], sv["hs"], dproj, B)
    dpar = jnp.sum(dpar, axis=0)
    sg["dt_bias"] = dpar[:, 0, :].reshape(SSD_HEADS)
    sg["a_log"] = dpar[:, 1, :].reshape(SSD_HEADS)
    sg["d_skip"] = dpar[:, 2, :].reshape(SSD_HEADS)
    sg["ssd_norm"] = jnp.sum(dnw, axis=0)
    dproj, sg["conv_w"], sg["conv_b"] = _conv_bwd(tag + "_convb", proj, dxs, dB, dC, sv["cw"], p["conv_b"][None],
                                                  dproj, B)
    ddt16 = jnp.transpose(ddt, (1, 0, 2)).reshape(T, SSD_HEADS)
    dproj = lax.dynamic_update_slice(dproj, jnp.pad(ddt16, ((0, 0), (0, IN_PAD - COL_DT - SSD_HEADS))).astype(BF16),
                                     (0, COL_DT))
    win = sv["win"]
    gwin = _mm(tag + "_dwin",
               [(sv["h2"], pl.BlockSpec((tkw, D_MODEL), lambda n, m, k: (k, 0)),
                 dproj, pl.BlockSpec((tkw, PROJ_TN), lambda n, m, k: (k, n)))],
               jax.ShapeDtypeStruct((D_MODEL, IN_PAD), BF16), pl.BlockSpec((D_MODEL, PROJ_TN), lambda n, m, k: (0, n)),
               (IN_PAD // PROJ_TN, 1, T // tkw), TN, (D_MODEL, PROJ_TN))
    dx1, sg["mix_norm"], dx1_b = _mm(
        tag + "_dh2",
        [(dproj, pl.BlockSpec((ROW_T, PROJ_TN), lambda i, n, k, j=j: (i, j)),
          win, pl.BlockSpec((D_MODEL, PROJ_TN), lambda i, n, k, j=j: (0, j))) for j in range(IN_PAD // PROJ_TN)],
        *_rmsb_out(T), (nt, 1, 1), NT, (ROW_T, D_MODEL), post="rmsb",
        post_in=[(sv["x1"], _row3()), (p["mix_norm"][None], _vec3()), (dx2, _row3())])
    return dx1, dx1_b, sg, gwout, gwin


DT_LO =IN_SH * 2 - COL_Q


def _win_from_shards(sh):
    main = IN_SH - DT_LO
    return jnp.concatenate([sh[0], sh[1][:, :main], sh[2][:, SSD_HEADS - DT_LO:], sh[3], sh[1][:, main:],
                            sh[2][:, :SSD_HEADS - DT_LO], jnp.zeros((sh.shape[1], IN_PAD - IN_PROJ), sh.dtype)], axis=1)


def _win_to_shards(g):
    main = IN_SH - DT_LO
    a, b = IN_SH + main, IN_SH + 2 * main
    return jnp.stack([g[:, :IN_SH],
                      jnp.concatenate([g[:, IN_SH:a], g[:, COL_DT:COL_DT + DT_LO]], axis=1),
                      jnp.concatenate([g[:, COL_DT + DT_LO:COL_DT + SSD_HEADS], g[:, a:b]], axis=1),
                      g[:, b:COL_DT]])


def _local_step(x, target, small, weights, scatter, B):
    T = x.shape[0]
    bias = _att_bias((T // B) // ATT_B)
    saved = []
    xl = x
    hl = _rms_fwd("l0f1_rms", x, small["ffn1_norm"][0][None])
    for l in range(DEPTH):
        tag = "l%d" % l
        p = {k: v[l] for k, v in small.items()}
        w1 = weights(l, "ffn1", hl)
        (x1, h2), ffn1, d1 = _ffn_fwd(tag + "f1", xl, hl, w1["g1"], w1["u1"],
                                      lambda after, l=l: weights(l, "ffn1d", after)["d1"], p["mix_norm"][None])
        x2, h3, sv = _mixer_fwd(tag, x1, h2, p, functools.partial(weights, l), bias, B)
        w2 = weights(l, "rest", x2)
        nxt = small["ffn1_norm"][l + 1][None] if l + 1 < DEPTH else None
        xo, ffn2, _ = _ffn_fwd(tag + "f2", x2, h3, w2["g2"], w2["u2"], w2["d2"], nxt)
        xl, hl = xo if nxt is not None else (xo, None)
        saved.append((ffn1, sv, ffn2, dict(g1=w1["g1"], u1=w1["u1"], d1=d1), w2))
    d, db, lsum = _loss_grad("loss", xl, target)
    sgrads = [None] * DEPTH
    for l in reversed(range(DEPTH)):
        tag = "l%db" % l
        p = {k: v[l] for k, v in small.items()}
        ffn1, sv, ffn2, w1, w2 = saved[l]
        sg = {}
        d, sg["ffn2_norm"], db = _ffn_bwd(tag + "f2", d, db, ffn2, p["ffn2_norm"][None], w2["g2"], w2["u2"], w2["d2"],
                                          lambda gg, gu, gd, c, l=l: scatter(l, "ffn2", dict(g2=gg, u2=gu, d2=gd), c))
        d, db, sgm, gwout, gwin = _mixer_bwd(tag, d, db, sv, p, bias, B)
        sg.update(sgm)
        db = scatter(l, "mixer", dict(wout=gwout, win=gwin), db)
        d, sg["ffn1_norm"], db = _ffn_bwd(tag + "f1", d, db, ffn1, p["ffn1_norm"][None], w1["g1"], w1["u1"], w1["d1"],
                                          lambda gg, gu, gd, c, l=l: scatter(l, "ffn1", dict(g1=gg, u1=gu, d1=gd), c))
        sgrads[l] = sg
    return lsum, d, sgrads


MESH = pl.DeviceIdType.MESH
ANY = pl.BlockSpec(memory_space=pl.ANY)


def _place():
    return lax.axis_index("x"), lax.axis_index("y"), lax.axis_index("c")


def _other_chips(x, y):
    return [(1 - x, y), (x, 1 - y), (1 - x, 1 - y)]


HBM = pl.BlockSpec(memory_space=pltpu.HBM)
SEM = pl.BlockSpec(memory_space=pltpu.SEMAPHORE)
EFFECT = pltpu.SideEffectType.DATAFLOW_SIDE_EFFECTING


def _hbm(a):
    return pltpu.with_memory_space_constraint(a, pltpu.HBM)


def _my_half(ref, c):
    hr = ref.shape[0] // 2
    return ref.at[pl.ds(pl.multiple_of(c * hr, 16), hr)]


def _exchange(gather, layer, halves, src, land, send, recv, n, act):
    x, y, c = _place()
    for k, (px, py) in enumerate(_other_chips(x, y)):
        for a in range(n):
            if gather:
                s_out, d_out, d_in = src[a].at[layer], land[a].at[2 * x + y], land[a].at[2 * px + py]
                if halves is not None and halves[a]:
                    s_out, d_out, d_in = _my_half(s_out, c), _my_half(d_out, c), _my_half(d_in, c)
            else:
                s_out, d_out, d_in = src[a].at[2 * px + py], land[a].at[k], land[a].at[k]
            act(pltpu.make_async_remote_copy(
                src_ref=s_out, dst_ref=d_out if act is _start else d_in, send_sem=send.at[k * n + a],
                recv_sem=recv.at[k * n + a], device_id=(px, py, c), device_id_type=MESH))


def _start(cp):
    cp.start()


def _finish(cp):
    cp.wait_send()
    cp.wait_recv()


def _exchange_start(name, gather, layer, srcs, carry, halves=None):
    n = len(srcs)
    lands = [lax.empty(((N_SHARD,) + s.shape[1:]) if gather else ((3,) + s.shape[1:]), s.dtype) for s in srcs]

    def body(*refs):
        _exchange(gather, layer, halves, refs[:n], refs[n:2 * n], refs[2 * n + 1], refs[2 * n + 2], n, _start)

    srcs = [_hbm(a) for a in srcs]
    thru = [_hbm(a) for a in lands + [carry]]
    out = pl.pallas_call(
        body, name=name,
        out_shape=(pltpu.SemaphoreType.DMA((3 * n,)), pltpu.SemaphoreType.DMA((3 * n,)),
                   *[pltpu.HBM(a.shape, a.dtype) for a in thru]),
        in_specs=[HBM] * (2 * n + 1), out_specs=(SEM, SEM, *[HBM] * (n + 1)),
        input_output_aliases={n + i: 2 + i for i in range(n + 1)},
        compiler_params=pltpu.CompilerParams(has_side_effects=EFFECT))(*srcs, *thru)
    return dict(gather=gather, layer=layer, halves=halves, send=out[0], recv=out[1], srcs=srcs,
                lands=list(out[2:2 + n])), out[-1]


def _exchange_wait(name, ex, after):
    n = len(ex["srcs"])

    def body(*refs):
        _exchange(ex["gather"], ex["layer"], ex["halves"], refs[:n], refs[n:2 * n], refs[2 * n], refs[2 * n + 1], n,
                  _finish)

    out = pl.pallas_call(
        body, name=name, out_shape=[pltpu.HBM(a.shape, a.dtype) for a in ex["lands"]],
        in_specs=[HBM] * (2 * n) + [SEM, SEM, ANY], out_specs=[HBM] * n,
        input_output_aliases={n + i: i for i in range(n)},
        compiler_params=pltpu.CompilerParams(has_side_effects=EFFECT))(
            *ex["srcs"], *ex["lands"], ex["send"], ex["recv"], after)
    return list(out)


def _sibling_fill(name, lands):
    n = len(lands)

    def body(*refs):
        land = refs[:n]
        send, recv = refs[2 * n], refs[2 * n + 1]
        x, y, c = _place()
        for k, (px, py) in enumerate(_other_chips(x, y)):
            for a in range(n):
                slot = land[a].at[2 * px + py]
                pltpu.make_async_remote_copy(src_ref=_my_half(slot, c), dst_ref=_my_half(slot, c),
                                             send_sem=send.at[k * n + a], recv_sem=recv.at[k * n + a],
                                             device_id=(x, y, 1 - c), device_id_type=MESH).start()
        for k, (px, py) in enumerate(_other_chips(x, y)):
            for a in range(n):
                slot = land[a].at[2 * px + py]
                cp = pltpu.make_async_remote_copy(src_ref=_my_half(slot, c), dst_ref=_my_half(slot, 1 - c),
                                                  send_sem=send.at[k * n + a], recv_sem=recv.at[k * n + a],
                                                  device_id=(x, y, 1 - c), device_id_type=MESH)
                cp.wait_recv()
                cp.wait_send()

    return pl.pallas_call(
        body, out_shape=[jax.ShapeDtypeStruct(a.shape, a.dtype) for a in lands],
        in_specs=[ANY] * n, out_specs=[ANY] * n, input_output_aliases={i: i for i in range(n)},
        scratch_shapes=[pltpu.SemaphoreType.DMA((3 * n,)), pltpu.SemaphoreType.DMA((3 * n,))],
        name=name)(*lands)


def _swap_sibling(name, parts):
    n = len(parts)

    def body(*refs):
        src, dst = refs[:n], refs[n:2 * n]
        send, recv = refs[2 * n:]
        x, y, c = _place()
        cps = [pltpu.make_async_remote_copy(src_ref=src[a], dst_ref=dst[a], send_sem=send.at[a], recv_sem=recv.at[a],
                                            device_id=(x, y, 1 - c), device_id_type=MESH) for a in range(n)]
        for cp in cps:
            cp.start()
        for cp in cps:
            cp.wait_recv()
        for cp in cps:
            cp.wait_send()

    return pl.pallas_call(
        body, out_shape=[jax.ShapeDtypeStruct(p.shape, p.dtype) for p in parts],
        in_specs=[ANY] * n, out_specs=[ANY] * n,
        scratch_shapes=[pltpu.SemaphoreType.DMA((n,)), pltpu.SemaphoreType.DMA((n,))],
        name=name)(*parts)


def _allreduce_small(name, v, after):
    R = v.shape[0]

    def body(v_ref, after_ref, o_ref, buf, send, recv):
        x, y, c = _place()
        me = 4 * x + 2 * y + c
        buf[me] = v_ref[...]
        cps = []
        for k in range(1, 8):
            fx, fy, fc = (k >> 2) & 1, (k >> 1) & 1, k & 1
            px = 1 - x if fx else x
            py = 1 - y if fy else y
            pc = 1 - c if fc else c
            cp = pltpu.make_async_remote_copy(src_ref=v_ref, dst_ref=buf.at[me], send_sem=send.at[k - 1],
                                              recv_sem=recv.at[k - 1], device_id=(px, py, pc), device_id_type=MESH)
            cp.start()
            cps.append((cp, 4 * px + 2 * py + pc))
        for k, (cp, peer) in enumerate(cps):
            pltpu.make_async_remote_copy(src_ref=v_ref, dst_ref=buf.at[peer], send_sem=send.at[k], recv_sem=recv.at[k],
                                         device_id=(x, y, c), device_id_type=MESH).wait_recv()
        for cp, _ in cps:
            cp.wait_send()
        acc = buf[0]
        for d in range(1, 8):
            acc = acc + buf[d]
        o_ref[...] = acc

    return pl.pallas_call(
        body, out_shape=jax.ShapeDtypeStruct((R, 128), F32),
        in_specs=[pl.BlockSpec(memory_space=pltpu.VMEM), ANY], out_specs=pl.BlockSpec(memory_space=pltpu.VMEM),
        scratch_shapes=[pltpu.VMEM((8, R, 128), F32), pltpu.SemaphoreType.DMA((7,)), pltpu.SemaphoreType.DMA((7,))],
        name=name)(v, after)


TILE_BYTES = 1600 * 1024


def _row_tile(r, c=1024):
    for t in (512, 352, 256, 128, 64, 32, 16, 8):
        if r % t == 0 and (t * c * 4 <= TILE_BYTES or t == 8):
            return t
    raise ValueError(r)


def _sum4(name, me, parts, got):
    _, R, C = parts.shape
    tr = _row_tile(R, C)

    def body(me_ref, o_ref, g_ref, s_ref):
        s = o_ref[...].astype(F32)
        for k in range(3):
            s = s + g_ref[k].astype(F32)
        s_ref[...] = s.astype(BF16)

    return pl.pallas_call(
        body, out_shape=jax.ShapeDtypeStruct((R, C), BF16),
        grid_spec=pltpu.PrefetchScalarGridSpec(
            num_scalar_prefetch=1, grid=(R // tr,),
            in_specs=[pl.BlockSpec((None, tr, C), lambda i, me_ref: (me_ref[0], i, 0)),
                      pl.BlockSpec((3, tr, C), lambda i, me_ref: (0, i, 0))],
            out_specs=pl.BlockSpec((tr, C), lambda i, me_ref: (i, 0))),
        name=name, compiler_params=_cp("parallel"))(me, parts, got)


def _adamw(name, w, gparts, m, v):
    R, C = w.shape
    tr = _row_tile(R, C)
    ng = len(gparts)
    c1 = 1.0 - ADAM_B1 ** ADAM_STEP
    c2 = 1.0 - ADAM_B2 ** ADAM_STEP

    def body(*refs):
        w_ref = refs[0]
        g_refs = refs[1:1 + ng]
        m_ref, v_ref, go_ref, d_ref, mo_ref, vo_ref = refs[1 + ng:]
        g = g_refs[0][...]
        for r in g_refs[1:]:
            g = g + r[...]
        mn = ADAM_B1 * m_ref[...] + (1.0 - ADAM_B1) * g
        vn = ADAM_B2 * v_ref[...] + (1.0 - ADAM_B2) * (g * g)
        go_ref[...] = g
        mo_ref[...] = mn
        vo_ref[...] = vn
        d_ref[...] = -ADAM_LR * ((mn / c1) / (jnp.sqrt(vn / c2) + ADAM_EPS) + ADAM_WD * w_ref[...])

    blk = pl.BlockSpec((tr, C), lambda i: (i, 0))
    osh = jax.ShapeDtypeStruct((R, C), F32)
    return pl.pallas_call(
        body, out_shape=(osh, osh, osh, osh), grid=(R // tr,), in_specs=[blk] * (3 + ng), out_specs=(blk,) * 4,
        name=name, compiler_params=_cp("parallel"))(w, *gparts, m, v)


def _adamw_layers(name, w, sums, m, v):
    _, R, C = w.shape
    tr = _row_tile(R, C)
    nr = R // tr
    c1 = 1.0 - ADAM_B1 ** ADAM_STEP
    c2 = 1.0 - ADAM_B2 ** ADAM_STEP

    def body(w_ref, a0, b0, a1, b1, m_ref, v_ref, go_ref, d_ref, mo_ref, vo_ref):
        f = lambda r: r[...].astype(F32)
        g = jnp.where(pl.program_id(0) == 0, f(a0) + f(b0), f(a1) + f(b1))
        mn = ADAM_B1 * m_ref[...] + (1.0 - ADAM_B1) * g
        vn = ADAM_B2 * v_ref[...] + (1.0 - ADAM_B2) * (g * g)
        go_ref[...] = g
        mo_ref[...] = mn
        vo_ref[...] = vn
        d_ref[...] = -ADAM_LR * ((mn / c1) / (jnp.sqrt(vn / c2) + ADAM_EPS) + ADAM_WD * w_ref[...])

    blk = pl.BlockSpec((None, tr, C), lambda l, i: (l, i, 0))
    lay0 = pl.BlockSpec((tr, C), lambda l, i: (jnp.where(l == 0, i, nr - 1), 0))
    lay1 = pl.BlockSpec((tr, C), lambda l, i: (jnp.where(l == 1, i, 0), 0))
    oblk = pl.BlockSpec((tr, C), lambda l, i: (l * nr + i, 0))
    osh = jax.ShapeDtypeStruct((DEPTH * R, C), F32)
    res = pl.pallas_call(
        body, out_shape=(osh, osh, osh, osh), grid=(DEPTH, nr),
        in_specs=[blk, lay0, lay0, lay1, lay1, blk, blk], out_specs=(oblk,) * 4,
        name=name, compiler_params=_cp("arbitrary", "arbitrary"))(w, *sums[0], *sums[1], m, v)
    return [r.reshape(w.shape) for r in res]


BIG = [("ffn1_w_gate", "g1"), ("ffn1_w_up", "u1"), ("ffn1_w_down", "d1"), ("w_in", "win"), ("w_out", "wout"),
       ("ffn2_w_gate", "g2"), ("ffn2_w_up", "u2"), ("ffn2_w_down", "d2")]
SMALL = ["ffn1_norm", "mix_norm", "conv_b", "dt_bias", "a_log", "d_skip", "ssd_norm", "q_norm", "k_norm", "ffn2_norm"]
WEIGHTS = ["ffn1_norm", "ffn1_w_gate", "ffn1_w_up", "ffn1_w_down", "mix_norm", "w_in", "conv_w", "conv_b", "dt_bias",
           "a_log", "d_skip", "ssd_norm", "q_norm", "k_norm", "w_out", "ffn2_norm", "ffn2_w_gate", "ffn2_w_up",
           "ffn2_w_down"]
CONV_SH = CONV_DIM // N_SHARD
TRANSPOSED = ("g1", "u1", "g2", "u2")
GATHER_GROUPS = [(0, "ffn1", ["g1", "u1"]), (0, "ffn1d", ["d1"]), (0, "win", ["win", "cw"]),
                 (0, "rest", ["wout", "g2", "u2", "d2"]),
                 (1, "all", ["g1", "u1", "d1", "win", "cw", "wout", "g2", "u2", "d2"])]


def _pad128(v):
    v = v.reshape(-1)
    return jnp.pad(v, (0, (-v.shape[0]) % 128))


def _pack(pieces):
    flat, offs, pos = [], [], 0
    for p in pieces:
        q = _pad128(p.astype(F32))
        offs.append(pos)
        pos += q.shape[0] // 128
        flat.append(q)
    total = -(-pos // 8) * 8
    out = jnp.concatenate(flat + [jnp.zeros(((total - pos) * 128,), F32)]).reshape(total, 128)
    return out, offs


def _unpack(packed, offs, shapes):
    out = []
    for off, shp in zip(offs, shapes):
        n = int(np.prod(shp))
        rows = -(-n // 128)
        out.append(packed[off:off + rows].reshape(-1)[:n].reshape(shp))
    return out


def kernel(x, ffn1_norm, ffn1_w_gate, ffn1_w_up, ffn1_w_down, mix_norm, w_in, conv_w, conv_b, dt_bias, a_log, d_skip, ssd_norm, q_norm, k_norm, w_out, ffn2_norm, ffn2_w_gate, ffn2_w_up, ffn2_w_down, loss_target, m_ffn1_norm, m_ffn1_w_gate, m_ffn1_w_up, m_ffn1_w_down, m_mix_norm, m_w_in, m_conv_w, m_conv_b, m_dt_bias, m_a_log, m_d_skip, m_ssd_norm, m_q_norm, m_k_norm, m_w_out, m_ffn2_norm, m_ffn2_w_gate, m_ffn2_w_up, m_ffn2_w_down, v_ffn1_norm, v_ffn1_w_gate, v_ffn1_w_up, v_ffn1_w_down, v_mix_norm, v_w_in, v_conv_w, v_conv_b, v_dt_bias, v_a_log, v_d_skip, v_ssd_norm, v_q_norm, v_k_norm, v_w_out, v_ffn2_norm, v_ffn2_w_gate, v_ffn2_w_up, v_ffn2_w_down):
    A = dict(locals())
    ix, iy, ic = _place()
    me = 2 * ix + iy
    B, S, _ = x.shape
    T = B * S

    view = lambda a, key: jnp.swapaxes(a, 1, 2) if key in TRANSPOSED else a
    own = {key: view(A[name], key).astype(BF16) for name, key in BIG}
    own["cw"] = conv_w
    exs, first_norm = [], ffn1_norm
    split = lambda l, key: l == 0 and key != "cw"
    for gi, (l, _, keys) in enumerate(GATHER_GROUPS):
        ex, first_norm = _exchange_start("gather_start%d" % gi, True, l, [own[key] for key in keys], first_norm,
                                         [split(l, key) for key in keys])
        exs.append(ex)
    landed = {}

    def weights(l, group, after):
        gi = [i for i, (gl, gname, _) in enumerate(GATHER_GROUPS) if gl == l and gname in (group, "all")][0]
        if gi not in landed:
            lands = _exchange_wait("gather_wait%d" % gi, exs[gi], after)
            keys = GATHER_GROUPS[gi][2]
            halved = [i for i, key in enumerate(keys) if split(l, key)]
            if halved:
                for i, whole in zip(halved, _sibling_fill("gather_fill%d" % gi, [lands[i] for i in halved])):
                    lands[i] = whole
            landed[gi] = {}
            for key, land in zip(GATHER_GROUPS[gi][2], lands):
                full = lax.dynamic_update_slice(land, own[key][l][None], (me, 0, 0))
                if key == "win":
                    full = _win_from_shards(full)
                if key == "cw":
                    full = jnp.transpose(full, (1, 0, 2)).reshape(CONV_K, CONV_DIM)
                landed[gi][key] = full
        return landed[gi]

    pending = []

    def scatter(l, group, grads, carry):
        keys = sorted(grads)
        arrs = [grads[key] for key in keys]
        if "win" in grads:
            arrs[keys.index("win")] = _win_to_shards(grads["win"])
        ex, carry = _exchange_start("scatter_start_l%d_%s" % (l, group), False, None, arrs, carry)
        pending.append((l, keys, ex))
        return carry

    small = {name: A[name] for name in SMALL}
    small["ffn1_norm"] = first_norm
    lsum, dx, sgrads = _local_step(x.reshape(T, D_MODEL), loss_target.reshape(T, D_MODEL), small, weights, scatter, B)

    names = SMALL + ["conv_w"]
    shapes = [A[n].shape for n in SMALL] + [(DEPTH, CONV_K, CONV_DIM), ()]
    pieces = [jnp.stack([sgrads[l][n].reshape(shp[1:]) for l in range(DEPTH)]) for n, shp in zip(names, shapes)]
    pieces.append(0.5 / D_MODEL * jnp.sum(lsum))
    packed, offs = _pack(pieces)

    sums, theirs, out = {}, {}, {}
    me1 = jnp.reshape(me, (1,)).astype(jnp.int32)

    def update(tag, after):
        todo = [k for k in sums if k not in theirs]
        theirs.update(zip(todo, _swap_sibling("swap_sibling_" + tag, [sums[k] for k in todo])))
        for name, key in BIG:
            if name not in out and all((key, l) in theirs for l in range(DEPTH)):
                res = _adamw_layers("adamw_" + key, view(A[name], key),
                                    [(sums[key, l], theirs[key, l]) for l in range(DEPTH)],
                                    view(A["m_" + name], key), view(A["v_" + name], key))
                out[name] = [view(r, key) for r in res]
                after = res[0]
        return after

    after = dx
    for idx, (l, keys, ex) in enumerate(pending):
        if idx == len(pending) - 1:
            after = update("a", after)
        lands = _exchange_wait("scatter_wait%d" % idx, ex, after)
        for key, g, got in zip(keys, ex["srcs"], lands):
            sums[key, l] = after = _sum4("sum_%s_l%d" % (key, l), me1, g, got)
    after = update("b", after)

    red = _unpack(_allreduce_small("allreduce_small", packed, after), offs, shapes)
    loss = red[-1]
    sg = dict(zip(names, red[:-1]))

    wp, offs = _pack([A[n] for n in SMALL])
    gp, _ = _pack([sg[n] for n in SMALL])
    mp, _ = _pack([A["m_" + n] for n in SMALL])
    vp, _ = _pack([A["v_" + n] for n in SMALL])
    res = _adamw("adamw_small", wp, [gp], mp, vp)
    shapes = [A[n].shape for n in SMALL]
    res = [_unpack(r, offs, shapes) for r in res]
    for i, n in enumerate(SMALL):
        out[n] = [res[q][i] for q in range(4)]
    gcw = lax.dynamic_slice_in_dim(sg["conv_w"], me * CONV_SH, CONV_SH, axis=2)
    flat = lambda a: a.reshape(DEPTH * CONV_K, CONV_SH)
    res = _adamw("adamw_conv_w", flat(conv_w), [flat(gcw)], flat(m_conv_w), flat(v_conv_w))
    out["conv_w"] = [r.reshape(conv_w.shape) for r in res]

    outs = [loss, dx.reshape(B, S, D_MODEL)]
    for q in range(4):
        outs += [out[n][q] for n in WEIGHTS]
    return tuple(outs)
```

```python
import functools
import math

import numpy as np
import jax
import jax.numpy as jnp
from jax import lax
from jax.experimental import pallas as pl
from jax.experimental.pallas import tpu as pltpu

F32 = jnp.float32
BF16 = jnp.bfloat16

D_MODEL = 1024
DEPTH = 2
N_SHARD = 4
D_FF = 2816
FF_SH = D_FF // N_SHARD
SSD_HEADS = 16
HEAD_DIM = 64
SSD_GROUPS = 4
GROUP_W = 256
SSD_STATE = 128
CONV_K = 4
CONV_DIM = 2048
ATT_HEADS = 16
MIX_W = 2048
MIX_SH = MIX_W // N_SHARD
IN_PROJ = 6160
IN_SH = IN_PROJ // N_SHARD
IN_PAD = 6272
PROJ_TN = 896
COL_Z, COL_XBC, COL_Q, COL_K, COL_V, COL_DT = 0, 1024, 3072, 4096, 5120, 6144
EPS = 1e-6
NEG = -1e30
SSD_L = 512
ATT_B = 512
ROW_T = 512
HALF_T = ROW_T
GU_T = 2048
TK_W = 4096
CONV_CT = 256
CONV_R = 256
PAD_R = 8

ADAM_LR, ADAM_B1, ADAM_B2, ADAM_EPS, ADAM_WD, ADAM_STEP = 0.001, 0.9, 0.999, 1e-08, 0.01, 10

NN = (((1,), (0,)), ((), ()))
NT = (((1,), (1,)), ((), ()))
TN = (((0,), (0,)), ((), ()))

VMEM_LIMIT = 56 * 1024 * 1024


def _cp(*sem):
    return pltpu.CompilerParams(dimension_semantics=sem, vmem_limit_bytes=VMEM_LIMIT)


def _dot(a, b, dims):
    return lax.dot_general(a, b, dims, preferred_element_type=F32)


def _sigmoid(x):
    return 0.5 * jnp.tanh(0.5 * x) + 0.5


def _softplus(x):
    return jnp.maximum(x, 0.0) + jnp.log(1.0 + jnp.exp(-jnp.abs(x)))


def _mm(name, pairs, out_shape, out_spec, grid, dims, acc_shape, res=None, scale=1.0, post=None, post_in=()):
    nk = grid[2]
    npair = len(pairs)
    npost = len(post_in)

    def body(*refs):
        ab = refs[:2 * npair]
        pos = 2 * npair
        res_ref = None
        if res is not None:
            res_ref = refs[pos]
            pos += 1
        pin = refs[pos:pos + npost]
        pos += npost
        out_ref = refs[pos]
        pos += 1
        if post is not None:
            out2_ref = refs[pos]
            pos += 1
        if post == "rmsb":
            out3_ref = refs[pos]
            pos += 1
        s = None
        for p in range(npair):
            d = _dot(ab[2 * p][...].astype(BF16), ab[2 * p + 1][...].astype(BF16), dims)
            s = d if s is None else s + d

        def finish(r):
            if scale != 1.0:
                r = r * scale
            if res_ref is not None:
                r = r + res_ref[...]
            if post == "rmsb":
                @pl.when(pl.program_id(0) == 0)
                def _():
                    out2_ref[...] = jnp.zeros_like(out2_ref)

                xv = pin[0][...]
                rr = lax.rsqrt(jnp.mean(xv * xv, axis=-1, keepdims=True) + EPS)
                xhat = xv * rr
                dxhat = r * pin[1][...]
                dx = pin[2][...] + rr * (dxhat - xhat * jnp.mean(dxhat * xhat, axis=-1, keepdims=True))
                out_ref[...] = dx
                out2_ref[...] += jnp.sum(r * xhat, axis=0, keepdims=True)
                out3_ref[...] = dx.astype(BF16)
                return
            out_ref[...] = r.astype(out_ref.dtype)
            if post == "norm":
                rr = lax.rsqrt(jnp.mean(r * r, axis=-1, keepdims=True) + EPS)
                out2_ref[...] = (r * rr * pin[0][...]).astype(BF16)

        if nk == 1:
            finish(s)
            return
        acc = refs[pos]
        k = pl.program_id(2)

        @pl.when(k == 0)
        def _():
            acc[...] = s

        @pl.when(k > 0)
        def _():
            acc[...] += s

        @pl.when(k == nk - 1)
        def _():
            finish(acc[...])

    args, specs = [], []
    for a, a_spec, b, b_spec in pairs:
        args += [a, b]
        specs += [a_spec, b_spec]
    for arr, spec in ([res] if res is not None else []) + list(post_in):
        args.append(arr)
        specs.append(spec)
    sems = ("arbitrary",) * 3 if post == "rmsb" else ("parallel", "parallel", "arbitrary")
    return pl.pallas_call(
        body, out_shape=out_shape, grid=grid, in_specs=specs, out_specs=out_spec,
        scratch_shapes=[] if nk == 1 else [pltpu.VMEM(acc_shape, F32)], name=name,
        compiler_params=_cp(*sems))(*args)


def _rms_fwd(name, x, w):
    T = x.shape[0]

    def body(x_ref, w_ref, o_ref):
        xv = x_ref[...]
        r = lax.rsqrt(jnp.mean(xv * xv, axis=-1, keepdims=True) + EPS)
        o_ref[...] = (xv * r * w_ref[...]).astype(BF16)

    return pl.pallas_call(
        body, out_shape=jax.ShapeDtypeStruct((T, D_MODEL), BF16), grid=(T // ROW_T,),
        in_specs=[pl.BlockSpec((ROW_T, D_MODEL), lambda i: (i, 0)), pl.BlockSpec((1, D_MODEL), lambda i: (0, 0))],
        out_specs=pl.BlockSpec((ROW_T, D_MODEL), lambda i: (i, 0)), name=name, compiler_params=_cp("parallel"))(x, w)


def _loss_grad(name, y, t):
    T = y.shape[0]

    def body(y_ref, t_ref, dy_ref, dyb_ref, l_ref):
        @pl.when(pl.program_id(0) == 0)
        def _():
            l_ref[...] = jnp.zeros_like(l_ref)

        e = y_ref[...] - t_ref[...]
        dy = e * (1.0 / D_MODEL)
        dy_ref[...] = dy
        dyb_ref[...] = dy.astype(BF16)
        l_ref[...] += jnp.sum(e * e, axis=0, keepdims=True)

    row = pl.BlockSpec((ROW_T, D_MODEL), lambda i: (i, 0))
    vec = pl.BlockSpec((1, D_MODEL), lambda i: (0, 0))
    return pl.pallas_call(
        body, out_shape=(jax.ShapeDtypeStruct((T, D_MODEL), F32), jax.ShapeDtypeStruct((T, D_MODEL), BF16),
                         jax.ShapeDtypeStruct((1, D_MODEL), F32)),
        grid=(T // ROW_T,), in_specs=[row, row], out_specs=(row, row, vec), name=name,
        compiler_params=_cp("arbitrary"))(y, t)


def _ffn_gate_up(name, h, wg, wu):
    T = h.shape[0]
    tm = min(GU_T, T)

    def body(h_ref, wg_ref, wu_ref, dgf_ref, duf_ref, a_ref):
        for r in range(0, tm, HALF_T):
            rows = slice(r, r + HALF_T)
            hv = h_ref[rows, :]
            g = _dot(hv, wg_ref[...], NT)
            u = _dot(hv, wu_ref[...], NT)
            sg = _sigmoid(g)
            silu = g * sg
            dgf_ref[rows, :] = (u * (sg * (1.0 + g * (1.0 - sg)))).astype(BF16)
            duf_ref[rows, :] = silu.astype(BF16)
            a_ref[rows, :] = (silu * u).astype(BF16)

    wspec = pl.BlockSpec((None, FF_SH, D_MODEL), lambda j, i: (j, 0, 0))
    ospec = pl.BlockSpec((None, tm, FF_SH), lambda j, i: (j, i, 0))
    osh = jax.ShapeDtypeStruct((N_SHARD, T, FF_SH), BF16)
    return pl.pallas_call(
        body, out_shape=(osh, osh, osh), grid=(N_SHARD, T // tm),
        in_specs=[pl.BlockSpec((tm, D_MODEL), lambda j, i: (i, 0)), wspec, wspec],
        out_specs=(ospec, ospec, ospec), name=name, compiler_params=_cp("parallel", "parallel"))(h, wg, wu)


def _ffn_dact(name, dx, wd, g, u):
    T = dx.shape[0]
    tm = min(GU_T, T)

    def body(dx_ref, wd_ref, g_ref, u_ref, dg_ref, du_ref):
        for r in range(0, tm, HALF_T):
            rows = slice(r, r + HALF_T)
            da = 0.5 * _dot(dx_ref[rows, :].astype(BF16), wd_ref[...], NT)
            dg_ref[rows, :] = (da * g_ref[rows, :].astype(F32)).astype(BF16)
            du_ref[rows, :] = (da * u_ref[rows, :].astype(F32)).astype(BF16)

    aspec = pl.BlockSpec((None, tm, FF_SH), lambda j, i: (j, i, 0))
    osh = jax.ShapeDtypeStruct((N_SHARD, T, FF_SH), BF16)
    return pl.pallas_call(
        body, out_shape=(osh, osh), grid=(N_SHARD, T // tm),
        in_specs=[pl.BlockSpec((tm, D_MODEL), lambda j, i: (i, 0)),
                  pl.BlockSpec((None, FF_SH, D_MODEL), lambda j, i: (j, 0, 0)), aspec, aspec],
        out_specs=(aspec, aspec), name=name, compiler_params=_cp("parallel", "parallel"))(dx, wd, g, u)


def _row3():
    return pl.BlockSpec((ROW_T, D_MODEL), lambda i, n, k: (i, 0))


def _vec3():
    return pl.BlockSpec((1, D_MODEL), lambda i, n, k: (0, 0))


def _with_norm(T, next_nw):
    if next_nw is None:
        return dict(out_shape=jax.ShapeDtypeStruct((T, D_MODEL), F32), out_spec=_row3())
    return dict(out_shape=(jax.ShapeDtypeStruct((T, D_MODEL), F32), jax.ShapeDtypeStruct((T, D_MODEL), BF16)),
                out_spec=(_row3(), _row3()), post="norm", post_in=[(next_nw, _vec3())])


def _ffn_fwd(tag, x, h, wg, wu, wd, next_nw):
    T = x.shape[0]
    g, u, a = _ffn_gate_up(tag + "_gu", h, wg, wu)
    if callable(wd):
        wd = wd(a)
    nt = T // ROW_T
    o = _with_norm(T, next_nw)
    xo = _mm(tag + "_down",
             [(a, pl.BlockSpec((None, ROW_T, FF_SH), lambda i, n, k, j=j: (j, i, 0)),
               wd, pl.BlockSpec((None, FF_SH, D_MODEL), lambda i, n, k, j=j: (j, 0, 0))) for j in range(N_SHARD)],
             o.pop("out_shape"), o.pop("out_spec"), (nt, 1, 1), NN, (ROW_T, D_MODEL),
             res=(x, _row3()), scale=0.5, **o)
    return xo, (x, h, g, u, a), wd


def _rmsb_out(T):
    f = jax.ShapeDtypeStruct
    return (f((T, D_MODEL), F32), f((1, D_MODEL), F32), f((T, D_MODEL), BF16)), (_row3(), _vec3(), _row3())


def _ffn_bwd(tag, dxo, dxo_b, saved, nw, wg, wu, wd, emit):
    x, h, g, u, a = saved
    T = x.shape[0]
    nt = T // ROW_T
    tkw = min(TK_W, T)
    nw_t = T // tkw
    dg, du = _ffn_dact(tag + "_dact", dxo_b, wd, g, u)
    actw = lambda f: pl.BlockSpec((None, tkw, FF_SH), f)
    gd = _mm(tag + "_dwd",
             [(a, actw(lambda m, n, k: (m, k, 0)), dxo_b, pl.BlockSpec((tkw, D_MODEL), lambda m, n, k: (k, 0)))],
             jax.ShapeDtypeStruct((N_SHARD, FF_SH, D_MODEL), BF16),
             pl.BlockSpec((None, FF_SH, D_MODEL), lambda m, n, k: (m, 0, 0)),
             (N_SHARD, 1, nw_t), TN, (FF_SH, D_MODEL), scale=0.5)
    hspec = pl.BlockSpec((tkw, D_MODEL), lambda j, n, k: (k, 0))
    gsh = jax.ShapeDtypeStruct((N_SHARD, FF_SH, D_MODEL), BF16)
    gspec = pl.BlockSpec((None, FF_SH, D_MODEL), lambda j, n, k: (j, 0, 0))
    gg = _mm(tag + "_dwg", [(dg, actw(lambda j, n, k: (j, k, 0)), h, hspec)], gsh, gspec,
             (N_SHARD, 1, nw_t), TN, (FF_SH, D_MODEL))
    gu = _mm(tag + "_dwu", [(du, actw(lambda j, n, k: (j, k, 0)), h, hspec)], gsh, gspec,
             (N_SHARD, 1, nw_t), TN, (FF_SH, D_MODEL))
    dg = emit(gg, gu, gd, dg)
    act = lambda j: pl.BlockSpec((None, ROW_T, FF_SH), lambda i, n, k: (j, i, 0))
    wsp = lambda j: pl.BlockSpec((None, FF_SH, D_MODEL), lambda i, n, k: (j, 0, 0))
    return _mm(tag + "_dh",
               [(dd, act(j), w, wsp(j)) for j in range(N_SHARD) for dd, w in ((dg, wg), (du, wu))],
               *_rmsb_out(T), (nt, 1, 1), NN, (ROW_T, D_MODEL), post="rmsb",
               post_in=[(x, _row3()), (nw, _vec3()), (dxo, _row3())])


def _seq_rows(ref, start, size, S):
    lo, hi = max(start, 0), min(start + size, S)
    parts = [ref[pl.ds(lo, hi - lo), :]]
    if lo > start:
        parts.insert(0, jnp.zeros((lo - start, ref.shape[1]), F32))
    if start + size > hi:
        parts.append(jnp.zeros((start + size - hi, ref.shape[1]), F32))
    return parts[0] if len(parts) == 1 else jnp.concatenate(parts, axis=0)


XBC_CB = COL_XBC // CONV_CT


def _conv_fwd(name, proj, w, b, B):
    T = proj.shape[0]
    S = T // B
    C = CONV_DIM

    def body(x_ref, w_ref, b_ref, o_ref):
        wv = w_ref[...]
        for c in range(S // CONV_R):
            r0 = c * CONV_R
            ch = _seq_rows(x_ref, r0 - PAD_R, CONV_R + PAD_R, S)
            pre = ch[PAD_R:] * wv[3:4] + b_ref[...]
            for s in range(1, CONV_K):
                pre = pre + pltpu.roll(ch, s, axis=0)[PAD_R:] * wv[3 - s:4 - s]
            o_ref[pl.ds(r0, CONV_R), :] = pre * _sigmoid(pre)

    return pl.pallas_call(
        body, out_shape=jax.ShapeDtypeStruct((T, C), F32), grid=(B, C // CONV_CT),
        in_specs=[pl.BlockSpec((S, CONV_CT), lambda bi, ci: (bi, XBC_CB + ci)),
                  pl.BlockSpec((CONV_K, CONV_CT), lambda bi, ci: (0, ci)),
                  pl.BlockSpec((1, CONV_CT), lambda bi, ci: (0, ci))],
        out_specs=pl.BlockSpec((S, CONV_CT), lambda bi, ci: (bi, ci)), name=name,
        compiler_params=_cp("parallel", "parallel"))(proj, w, b)


def _conv_bwd(name, proj, dxs, dB, dC, w, b, dproj, B):
    T = proj.shape[0]
    S = T // B
    C = CONV_DIM
    RW = CONV_R + PAD_R
    nx, nb = dxs.shape[1] // CONV_CT, dB.shape[1] // CONV_CT

    def body(x_ref, dx_in, db_in, dc_in, w_ref, b_ref, buf_ref, dx_ref, dw_ref, db_ref):
        @pl.when(pl.program_id(1) == 0)
        def _():
            dw_ref[...] = jnp.zeros_like(dw_ref)
            db_ref[...] = jnp.zeros_like(db_ref)

        ci = pl.program_id(0)
        wv = w_ref[...]
        dw = [jnp.zeros((1, CONV_CT), F32) for _ in range(CONV_K)]
        db = jnp.zeros((1, CONV_CT), F32)
        for c in range(S // CONV_R):
            r0 = c * CONV_R
            ch = _seq_rows(x_ref, r0 - PAD_R, RW + PAD_R, S)
            xs = [ch[PAD_R:]] + [pltpu.roll(ch, s, axis=0)[PAD_R:] for s in range(1, CONV_K)]
            pre = b_ref[...] + xs[0] * wv[3:4]
            for s in range(1, CONV_K):
                pre = pre + xs[s] * wv[3 - s:4 - s]
            sg = _sigmoid(pre)
            dout = jnp.where(ci < nx, _seq_rows(dx_in, r0, RW, S),
                             jnp.where(ci < nx + nb, _seq_rows(db_in, r0, RW, S), _seq_rows(dc_in, r0, RW, S)))
            dpre = dout * (sg * (1.0 + pre * (1.0 - sg)))
            dx = dpre[:CONV_R] * wv[3:4]
            for s in range(1, CONV_K):
                dx = dx + pltpu.roll(dpre, RW - s, axis=0)[:CONV_R] * wv[3 - s:4 - s]
            dx_ref[pl.ds(r0, CONV_R), :] = dx.astype(BF16)
            dcur = dpre[:CONV_R]
            db = db + jnp.sum(dcur, axis=0, keepdims=True)
            for s in range(CONV_K):
                dw[3 - s] = dw[3 - s] + jnp.sum(dcur * xs[s][:CONV_R], axis=0, keepdims=True)
        db_ref[...] += db
        for k in range(CONV_K):
            dw_ref[k:k + 1, :] += dw[k]

    seq = lambda f: pl.BlockSpec((S, CONV_CT), f)
    return pl.pallas_call(
        body,
        out_shape=(jax.ShapeDtypeStruct(dproj.shape, dproj.dtype), jax.ShapeDtypeStruct((CONV_K, C), F32),
                   jax.ShapeDtypeStruct((1, C), F32)),
        grid=(C // CONV_CT, B),
        in_specs=[seq(lambda ci, bi: (bi, XBC_CB + ci)),
                  seq(lambda ci, bi: (bi, jnp.minimum(ci, nx - 1))),
                  seq(lambda ci, bi: (bi, jnp.clip(ci - nx, 0, nb - 1))),
                  seq(lambda ci, bi: (bi, jnp.clip(ci - nx - nb, 0, nb - 1))),
                  pl.BlockSpec((CONV_K, CONV_CT), lambda ci, bi: (0, ci)),
                  pl.BlockSpec((1, CONV_CT), lambda ci, bi: (0, ci)), ANY],
        out_specs=(seq(lambda ci, bi: (bi, XBC_CB + ci)),
                   pl.BlockSpec((CONV_K, CONV_CT), lambda ci, bi: (0, ci)),
                   pl.BlockSpec((1, CONV_CT), lambda ci, bi: (0, ci))),
        input_output_aliases={6: 0},
        name=name, compiler_params=_cp("parallel", "arbitrary"))(proj, dxs, dB, dC, w, b, dproj)


def _tri_sum(tri, x, dims, tri_first, terms=3):
    out, rest = None, x
    for t in range(terms):
        part = rest.astype(BF16)
        if t + 1 < terms:
            rest = rest - part.astype(F32)
        d = _dot(tri, part, dims) if tri_first else _dot(part, tri, dims)
        out = d if out is None else out + d
    return out


def _total(x):
    return jnp.sum(jnp.sum(x, axis=0, keepdims=True), axis=-1, keepdims=True)


def _ssd_common(dtc_ref, dtr_ref, pcol_ref, prow_ref, b_ref, c_ref):
    L = SSD_L
    bias_c, alog_c = pcol_ref[0:1, :], pcol_ref[1:2, :]
    a_c = -jnp.exp(alog_c)
    dt_c = _softplus(dtc_ref[...] + bias_c)
    row = lax.broadcasted_iota(jnp.int32, (L, L), 0)
    col = lax.broadcasted_iota(jnp.int32, (L, L), 1)
    causal = row >= col
    tri = causal.astype(BF16)
    cum_c = _tri_sum(tri, dt_c * a_c, NN, True)
    a_r = -jnp.exp(prow_ref[:, 1:2])
    dt_r = _softplus(dtr_ref[...] + prow_ref[:, 0:1])
    cum_r = _tri_sum(tri, dt_r * a_r, NT, False)
    bb = b_ref[...].astype(BF16)
    cb = c_ref[...].astype(BF16)
    G = _dot(cb, bb, NT)
    return a_c, dt_c, causal, tri, cum_c, cum_r, bb, cb, G


def _ssd_fwd(name, xc, proj, dtc, dtr, pcol, prow, nw, B):
    T = xc.shape[0]
    S = T // B
    nb = S // SSD_L
    L = SSD_L

    def body(xs_ref, b_ref, c_ref, z_ref, dtc_ref, dtr_ref, pcol_ref, prow_ref, nw_ref, y_ref, yn_ref, hs_ref, H, yo_s):
        @pl.when(pl.program_id(2) == 0)
        def _():
            H[...] = jnp.zeros_like(H)

        a_c, dt_c, causal, tri, cum_c, cum_r, bb, cb, G = _ssd_common(dtc_ref, dtr_ref, pcol_ref, prow_ref, b_ref, c_ref)
        dsk = pcol_ref[2:3, :]
        clast = cum_c[L - 1:L, :]
        bf = b_ref[...]
        for h in range(4):
            hs_ref[h] = H[h]
            yo_s[h] = _dot(cb, H[h].astype(BF16), NN)
        for h in range(4):
            sl = slice(HEAD_DIM * h, HEAD_DIM * (h + 1))
            cc = cum_c[:, h:h + 1]
            lm = jnp.exp(jnp.where(causal, cc - cum_r[h:h + 1, :], NEG))
            M = (G * lm).astype(BF16)
            xh = xs_ref[:, sl]
            Xb = (xh * dt_c[:, h:h + 1]).astype(BF16)
            Hh = H[h]
            y = _dot(M, Xb, NN) + jnp.exp(cc) * yo_s[h]
            y_ref[:, sl] = y + dsk[:, h:h + 1] * xh
            cl = clast[:, h:h + 1]
            Bw = (bf * jnp.exp(cl - cc)).astype(BF16)
            H[h] = jnp.exp(cl) * Hh + _dot(Bw, Xb, TN)
        zv = z_ref[...]
        y2 = y_ref[...] * (zv * _sigmoid(zv))
        r = lax.rsqrt(jnp.mean(y2 * y2, axis=-1, keepdims=True) + EPS)
        yn_ref[...] = (y2 * r * nw_ref[...]).astype(BF16)

    rowi = lambda b, g, i: b * nb + i
    grp = pl.BlockSpec((L, GROUP_W), lambda b, g, i: (rowi(b, g, i), g))
    return pl.pallas_call(
        body,
        out_shape=(jax.ShapeDtypeStruct((T, 1024), F32), jax.ShapeDtypeStruct((T, 1024), BF16),
                   jax.ShapeDtypeStruct((B, SSD_GROUPS, nb, 4, SSD_STATE, HEAD_DIM), F32)),
        grid=(B, SSD_GROUPS, nb),
        in_specs=[grp,
                  pl.BlockSpec((L, SSD_STATE), lambda b, g, i: (rowi(b, g, i), 8 + g)),
                  pl.BlockSpec((L, SSD_STATE), lambda b, g, i: (rowi(b, g, i), 12 + g)),
                  grp,
                  pl.BlockSpec((None, L, 4), lambda b, g, i: (g, rowi(b, g, i), 0)),
                  pl.BlockSpec((None, 4, L), lambda b, g, i: (g, 0, rowi(b, g, i))),
                  pl.BlockSpec((None, 3, 4), lambda b, g, i: (g, 0, 0)),
                  pl.BlockSpec((None, 4, 3), lambda b, g, i: (g, 0, 0)),
                  pl.BlockSpec((1, GROUP_W), lambda b, g, i: (0, g))],
        out_specs=(grp, grp,
                   pl.BlockSpec((None, None, None, 4, SSD_STATE, HEAD_DIM), lambda b, g, i: (b, g, i, 0, 0, 0))),
        scratch_shapes=[pltpu.VMEM((4, SSD_STATE, HEAD_DIM), F32), pltpu.VMEM((4, L, HEAD_DIM), F32)], name=name,
        compiler_params=_cp("parallel", "parallel", "arbitrary"))(xc, xc, xc, proj, dtc, dtr, pcol, prow, nw)


def _ssd_bwd(name, dyn, Y, xc, proj, dtc, dtr, pcol, prow, nw, hs, dproj, B):
    T = xc.shape[0]
    S = T // B
    nb = S // SSD_L
    L = SSD_L

    def body(dyn_ref, y_ref, xs_ref, b_ref, c_ref, z_ref, dtc_ref, dtr_ref, pcol_ref, prow_ref, nw_ref, hs_ref, buf_ref,
             dxs_ref, db_ref, dc_ref, dz_ref, ddt_ref, dpar_ref, dnw_ref, dH, dm_s, dxo_s, ea_s, ex_s):
        @pl.when(pl.program_id(2) == 0)
        def _():
            dH[...] = jnp.zeros_like(dH)
            dpar_ref[...] = jnp.zeros_like(dpar_ref)
            dnw_ref[...] = jnp.zeros_like(dnw_ref)

        a_c, dt_c, causal, tri, cum_c, cum_r, bb, cb, G = _ssd_common(dtc_ref, dtr_ref, pcol_ref, prow_ref, b_ref, c_ref)
        dsk = pcol_ref[2:3, :]
        clast = cum_c[L - 1:L, :]
        bf = b_ref[...]
        cf = c_ref[...]
        Yv = y_ref[...]
        zv = z_ref[...]
        sz = _sigmoid(zv)
        silu = zv * sz
        y2 = Yv * silu
        r = lax.rsqrt(jnp.mean(y2 * y2, axis=-1, keepdims=True) + EPS)
        yhat = y2 * r
        dyv = dyn_ref[...]
        dnw_ref[...] += jnp.sum(dyv * yhat, axis=0, keepdims=True)
        dyhat = dyv * nw_ref[...]
        dy2 = r * (dyhat - yhat * jnp.mean(dyhat * yhat, axis=-1, keepdims=True))
        dY = dy2 * silu
        dz_ref[...] = (dy2 * Yv * (sz * (1.0 + zv * (1.0 - sz)))).astype(BF16)

        lane4 = lax.broadcasted_iota(jnp.int32, (1, 4), 1)
        dG = jnp.zeros((L, L), F32)
        dBs = jnp.zeros((L, SSD_STATE), F32)
        dCs = jnp.zeros((L, SSD_STATE), F32)
        ddsk = jnp.zeros((1, 4), F32)
        dcl = jnp.zeros((1, 4), F32)
        for h in range(4):
            sl = slice(HEAD_DIM * h, HEAD_DIM * (h + 1))
            xb = (xs_ref[:, sl] * dt_c[:, h:h + 1]).astype(BF16)
            dm_s[h] = _dot(dY[:, sl].astype(BF16), xb, NT)
            dxo_s[h] = _dot(bb, dH[h].astype(BF16), NN)
        for h in range(4):
            sl = slice(HEAD_DIM * h, HEAD_DIM * (h + 1))
            onehot = (lane4 == h).astype(F32)
            cc = cum_c[:, h:h + 1]
            cl = clast[:, h:h + 1]
            lm = jnp.exp(jnp.where(causal, cc - cum_r[h:h + 1, :], NEG))
            M = (G * lm).astype(BF16)
            xh = xs_ref[:, sl]
            dth = dt_c[:, h:h + 1]
            X = xh * dth
            Xb = X.astype(BF16)
            dYh = dY[:, sl]
            dYb = dYh.astype(BF16)
            Hb = hs_ref[h].astype(BF16)
            dHh = dH[h]
            dHb = dHh.astype(BF16)
            alpha = jnp.exp(cc)
            beta = jnp.exp(cl - cc)
            dXoff = beta * dxo_s[h]
            dX = _dot(M, dYb, TN) + dXoff
            dG = dG + dm_s[h] * lm
            dCs = dCs + _dot((alpha * dYh).astype(BF16), Hb, NT)
            dBs = dBs + _dot((beta * X).astype(BF16), dHb, NT)
            ypre = Yv[:, sl] - dsk[:, h:h + 1] * xh
            ea_s[:, sl] = dYb.astype(F32) * ypre - Xb.astype(F32) * dX
            ex_s[:, sl] = dX * xh
            dcl_h = (_total(dHh * (jnp.exp(cl) * hs_ref[h])) + _total(Xb.astype(F32) * dXoff))
            dcl = dcl + dcl_h * onehot
            ddsk = ddsk + _total(dYh * xh) * onehot
            dxs_ref[:, sl] = dsk[:, h:h + 1] * dYh + dX * dth
            dH[h] = jnp.exp(cl) * dHh + _dot((alpha * cf).astype(BF16), dYb, TN)
        dGb = dG.astype(BF16)
        dc_ref[...] = _dot(dGb, bb, NN) + dCs
        db_ref[...] = _dot(dGb, cb, TN) + dBs
        feat = lax.broadcasted_iota(jnp.int32, (GROUP_W, 4), 0)
        head = lax.broadcasted_iota(jnp.int32, (GROUP_W, 4), 1) * HEAD_DIM
        sel = ((feat >= head) & (feat < head + HEAD_DIM)).astype(BF16)
        dA = _tri_sum(sel, ea_s[...], NN, False)
        ddtx = _tri_sum(sel, ex_s[...], NN, False)
        last = lax.broadcasted_iota(jnp.int32, (L, 1), 0) == L - 1
        dA = dA + jnp.where(last, dcl, 0.0)
        dadt = _tri_sum(tri, dA, TN, True)
        ddt = dadt * a_c + ddtx
        d_a = jnp.sum(dadt * dt_c, axis=0, keepdims=True)
        ddraw = ddt * _sigmoid(dtc_ref[...] + pcol_ref[0:1, :])
        ddt_ref[...] = ddraw
        dpar_ref[0:1, :] += jnp.sum(ddraw, axis=0, keepdims=True)
        dpar_ref[1:2, :] += d_a * a_c
        dpar_ref[2:3, :] += ddsk

    rowi = lambda b, g, i: b * nb + (nb - 1 - i)
    grp = pl.BlockSpec((L, GROUP_W), lambda b, g, i: (rowi(b, g, i), g))
    st = pl.BlockSpec((L, SSD_STATE), lambda b, g, i: (rowi(b, g, i), g))
    f = jax.ShapeDtypeStruct
    return pl.pallas_call(
        body,
        out_shape=(f((T, 1024), F32), f((T, 512), F32), f((T, 512), F32), f(dproj.shape, dproj.dtype),
                   f((SSD_GROUPS, T, 4), F32), f((B, SSD_GROUPS, 3, 4), F32), f((B, 1, 1024), F32)),
        grid=(B, SSD_GROUPS, nb),
        in_specs=[grp, grp, grp,
                  pl.BlockSpec((L, SSD_STATE), lambda b, g, i: (rowi(b, g, i), 8 + g)),
                  pl.BlockSpec((L, SSD_STATE), lambda b, g, i: (rowi(b, g, i), 12 + g)),
                  grp,
                  pl.BlockSpec((None, L, 4), lambda b, g, i: (g, rowi(b, g, i), 0)),
                  pl.BlockSpec((None, 4, L), lambda b, g, i: (g, 0, rowi(b, g, i))),
                  pl.BlockSpec((None, 3, 4), lambda b, g, i: (g, 0, 0)),
                  pl.BlockSpec((None, 4, 3), lambda b, g, i: (g, 0, 0)),
                  pl.BlockSpec((1, GROUP_W), lambda b, g, i: (0, g)),
                  pl.BlockSpec((None, None, None, 4, SSD_STATE, HEAD_DIM), lambda b, g, i: (b, g, nb - 1 - i, 0, 0, 0)),
                  ANY],
        out_specs=(grp, st, st, grp,
                   pl.BlockSpec((None, L, 4), lambda b, g, i: (g, rowi(b, g, i), 0)),
                   pl.BlockSpec((None, None, 3, 4), lambda b, g, i: (b, g, 0, 0)),
                   pl.BlockSpec((None, 1, GROUP_W), lambda b, g, i: (b, 0, g))),
        input_output_aliases={12: 3},
        scratch_shapes=[pltpu.VMEM((4, SSD_STATE, HEAD_DIM), F32), pltpu.VMEM((4, L, L), F32),
                        pltpu.VMEM((4, L, HEAD_DIM), F32), pltpu.VMEM((L, GROUP_W), F32),
                        pltpu.VMEM((L, GROUP_W), F32)], name=name,
        compiler_params=_cp("parallel", "parallel", "arbitrary"))(
            dyn, Y, xc, xc, xc, proj, dtc, dtr, pcol, prow, nw, hs, dproj)


def _head_sel():
    sel = (np.arange(1024)[:, None] // HEAD_DIM == np.arange(ATT_HEADS)[None, :]).astype(np.float32)
    return jnp.asarray(sel, BF16), jnp.asarray(sel.T, BF16)


def _head_rms(xv, sel, selT):
    ms = _tri_sum(sel, xv * xv, NN, False, 1) * (1.0 / HEAD_DIM)
    return _tri_sum(selT, lax.rsqrt(ms + EPS), NN, False, 2)


def _headnorm_fwd(name, proj, col_block, w):
    T = proj.shape[0]
    sel, selT = _head_sel()

    def body(x_ref, w_ref, sel_ref, selT_ref, o_ref):
        xv = x_ref[...]
        o_ref[...] = (xv * _head_rms(xv, sel_ref[...], selT_ref[...]) * w_ref[...]).astype(BF16)

    full = lambda shp: pl.BlockSpec(shp, lambda i: (0, 0))
    return pl.pallas_call(
        body, out_shape=jax.ShapeDtypeStruct((T, 1024), BF16), grid=(T // ROW_T,),
        in_specs=[pl.BlockSpec((ROW_T, 1024), lambda i: (i, col_block)), full((1, 1024)), full((1024, ATT_HEADS)),
                  full((ATT_HEADS, 1024))],
        out_specs=pl.BlockSpec((ROW_T, 1024), lambda i: (i, 0)), name=name, compiler_params=_cp("parallel"))(
            proj, jnp.tile(w, (1, ATT_HEADS)), sel, selT)


def _headnorm_bwd(name, dn, proj, col_block, w, dproj):
    T = proj.shape[0]
    sel, selT = _head_sel()

    def body(dn_ref, x_ref, w_ref, sel_ref, selT_ref, buf_ref, dx_ref, dw_ref):
        @pl.when(pl.program_id(0) == 0)
        def _():
            dw_ref[...] = jnp.zeros_like(dw_ref)

        xv = x_ref[...]
        sl, slT = sel_ref[...], selT_ref[...]
        rb = _head_rms(xv, sl, slT)
        xhat = xv * rb
        dnv = dn_ref[...]
        dxhat = dnv * w_ref[...]
        mean = _tri_sum(slT, _tri_sum(sl, dxhat * xhat, NN, False, 2) * (1.0 / HEAD_DIM), NN, False, 2)
        dx_ref[...] = (rb * (dxhat - xhat * mean)).astype(BF16)
        dw_ref[...] += jnp.sum(dnv * xhat, axis=0, keepdims=True)

    here = pl.BlockSpec((ROW_T, 1024), lambda i: (i, col_block))
    full = lambda shp: pl.BlockSpec(shp, lambda i: (0, 0))
    dx, dw = pl.pallas_call(
        body, out_shape=(jax.ShapeDtypeStruct(dproj.shape, dproj.dtype), jax.ShapeDtypeStruct((1, 1024), F32)),
        grid=(T // ROW_T,),
        in_specs=[pl.BlockSpec((ROW_T, 1024), lambda i: (i, 0)), here, full((1, 1024)), full((1024, ATT_HEADS)),
                  full((ATT_HEADS, 1024)), ANY],
        out_specs=(here, full((1, 1024))), input_output_aliases={5: 0},
        name=name, compiler_params=_cp("arbitrary"))(dn, proj, jnp.tile(w, (1, ATT_HEADS)), sel, selT, dproj)
    return dx, jnp.sum(dw.reshape(ATT_HEADS, HEAD_DIM), axis=0, keepdims=True)


def _att_bias(nq):
    j = np.arange(ATT_B)[:, None]
    i = np.arange(ATT_B)[None, :]
    out = np.empty((nq, ATT_B, ATT_B), np.float32)
    for dblk in range(nq):
        dl = ATT_B * dblk + i - j
        cnt = ((dl >= 0) & (dl <= 128)).astype(np.float32)
        cnt += ((dl >= 0) & (dl % 4 == 0) & (dl <= 512))
        cnt += ((dl >= 0) & (dl % 16 == 0) & (dl <= 2048))
        out[dblk] = np.where(cnt > 0, np.log(np.maximum(cnt, 1.0)), NEG)
    return jnp.asarray(out)


def _row_pair(nq):
    def f(r, c):
        first = c <= r
        return jnp.where(first, r, nq - 1 - r), jnp.where(first, c, c - (r + 1))
    return f


def _col_pair(nq):
    def f(r, c):
        first = c < nq - r
        kj = jnp.where(first, r, nq - 1 - r)
        return jnp.where(first, r + c, nq - 1 - r + (c - (nq - r))), kj
    return f


ATT_SCALE = 1.0 / math.sqrt(HEAD_DIM)
ATT_HS = 8
ATT_W = ATT_HS * HEAD_DIM


def _att_maps(nq, qk):
    return dict(
        q_tok=lambda b, g, r, c: (b * nq + qk(r, c)[0], g),
        k_tok=lambda b, g, r, c: (b * nq + qk(r, c)[1], g),
        v_tok=lambda b, g, r, c: (b * nq + qk(r, c)[1], COL_V // ATT_W + g),
        q_feat=lambda b, g, r, c: (g, b * nq + qk(r, c)[0]),
        k_feat=lambda b, g, r, c: (g, b * nq + qk(r, c)[1]),
        bias=lambda b, g, r, c: (qk(r, c)[0] - qk(r, c)[1], 0, 0),
        lse=lambda b, g, r, c: (g, 0, b * nq + qk(r, c)[0]),
        do_tok=lambda b, g, r, c: (b * nq + qk(r, c)[0], 1024 // ATT_W + g))


def _att_fwd(name, kn, qT, vT, bias, B):
    T = kn.shape[0]
    nq = (T // B) // ATT_B
    qk = _row_pair(nq)
    mp = _att_maps(nq, qk)

    def body(k_ref, qT_ref, vT_ref, bias_ref, oT_ref, lse_ref, m_s, l_s, acc_s, s_s):
        qi, kj = qk(pl.program_id(2), pl.program_id(3))

        @pl.when(kj == 0)
        def _():
            m_s[...] = jnp.full_like(m_s, NEG)
            l_s[...] = jnp.zeros_like(l_s)
            acc_s[...] = jnp.zeros_like(acc_s)

        bv = bias_ref[...]
        for h in range(ATT_HS):
            rs = slice(HEAD_DIM * h, HEAD_DIM * (h + 1))
            s_s[h] = _dot(k_ref[:, rs], qT_ref[rs, :], NN)
        for h in range(ATT_HS):
            rs = slice(HEAD_DIM * h, HEAD_DIM * (h + 1))
            s = s_s[h] + bv
            m_prev = m_s[h:h + 1, :]
            m_new = jnp.maximum(m_prev, jnp.max(s, axis=0, keepdims=True))
            alpha = jnp.exp(m_prev - m_new)
            p = jnp.exp(s - m_new)
            l_s[h:h + 1, :] = alpha * l_s[h:h + 1, :] + jnp.sum(p, axis=0, keepdims=True)
            acc_s[rs, :] = alpha * acc_s[rs, :] + _dot(vT_ref[rs, :], p.astype(BF16), NN)
            m_s[h:h + 1, :] = m_new

        @pl.when(kj == qi)
        def _():
            for h in range(ATT_HS):
                rs = slice(HEAD_DIM * h, HEAD_DIM * (h + 1))
                oT_ref[rs, :] = (acc_s[rs, :] / l_s[h:h + 1, :]).astype(BF16)
            lse_ref[...] = m_s[...] + jnp.log(l_s[...])

    tok = (ATT_B, ATT_W)
    feat = (ATT_W, ATT_B)
    return pl.pallas_call(
        body,
        out_shape=(jax.ShapeDtypeStruct((1024, T), BF16), jax.ShapeDtypeStruct((ATT_HEADS // ATT_HS, ATT_HS, T), F32)),
        grid=(B, ATT_HEADS // ATT_HS, nq // 2, nq + 1),
        in_specs=[pl.BlockSpec(tok, mp["k_tok"]), pl.BlockSpec(feat, mp["q_feat"]), pl.BlockSpec(feat, mp["k_feat"]),
                  pl.BlockSpec((None, ATT_B, ATT_B), mp["bias"])],
        out_specs=(pl.BlockSpec(feat, mp["q_feat"]), pl.BlockSpec((None, ATT_HS, ATT_B), mp["lse"])),
        scratch_shapes=[pltpu.VMEM((ATT_HS, ATT_B), F32), pltpu.VMEM((ATT_HS, ATT_B), F32),
                        pltpu.VMEM((ATT_W, ATT_B), F32), pltpu.VMEM((ATT_HS, ATT_B, ATT_B), F32)],
        name=name, compiler_params=_cp("parallel", "parallel", "arbitrary", "arbitrary"))(kn, qT, vT, bias)


def _att_scores(k_ref, qT_ref, v_ref, doT_ref, s_s, dp_s):
    for h in range(ATT_HS):
        rs = slice(HEAD_DIM * h, HEAD_DIM * (h + 1))
        s_s[h] = _dot(k_ref[:, rs], qT_ref[rs, :], NN)
        dp_s[h] = _dot(v_ref[:, rs].astype(BF16), doT_ref[rs, :].astype(BF16), NN)


def _att_p_ds(s_s, dp_s, doT_ref, oT_ref, lse_ref, bv, h):
    rs = slice(HEAD_DIM * h, HEAD_DIM * (h + 1))
    delta = jnp.sum(doT_ref[rs, :] * oT_ref[rs, :].astype(F32), axis=0, keepdims=True)
    p = jnp.exp(s_s[h] + bv - lse_ref[h:h + 1, :])
    return p, p * (dp_s[h] - delta)


def _att_bwd(name, kn, qT, proj, qn, knT, bias, doT, oT, lse, dyn, dproj, B):
    T = kn.shape[0]
    S = T // B
    nq = S // ATT_B
    qk = _col_pair(nq)
    mp = _att_maps(nq, qk)

    def body(k_ref, qT_ref, v_ref, q_ref, kT_ref, bias_ref, doT_ref, oT_ref, lse_ref, do_ref, buf_ref,
             dqT_ref, dk_ref, dv_ref, dk_s, dv_s, dq_s, s_s, dp_s):
        r, c = pl.program_id(2), pl.program_id(3)
        qi, kj = qk(r, c)

        @pl.when((r == 0) & (c == 0))
        def _():
            dq_s[...] = jnp.zeros_like(dq_s)

        @pl.when(qi == kj)
        def _():
            dk_s[...] = jnp.zeros_like(dk_s)
            dv_s[...] = jnp.zeros_like(dv_s)

        bv = bias_ref[...]
        _att_scores(k_ref, qT_ref, v_ref, doT_ref, s_s, dp_s)
        dq_blk = dq_s.at[qi]
        for h in range(ATT_HS):
            rs = slice(HEAD_DIM * h, HEAD_DIM * (h + 1))
            p, ds = _att_p_ds(s_s, dp_s, doT_ref, oT_ref, lse_ref, bv, h)
            dsb = ds.astype(BF16)
            dv_s[h] += _dot(p.astype(BF16), do_ref[:, rs].astype(BF16), NN)
            dk_s[h] += _dot(dsb, q_ref[:, rs], NN)
            dq_blk[rs, :] += _dot(kT_ref[rs, :], dsb, NN)

        @pl.when(qi == nq - 1)
        def _():
            for h in range(ATT_HS):
                rs = slice(HEAD_DIM * h, HEAD_DIM * (h + 1))
                dk_ref[:, rs] = dk_s[h] * ATT_SCALE
                dv_ref[:, rs] = dv_s[h].astype(BF16)

        @pl.when((r == nq // 2 - 1) & (c == nq))
        def _():
            for q in range(nq):
                dqT_ref[:, ATT_B * q:ATT_B * (q + 1)] = dq_s[q] * ATT_SCALE

    tok = (ATT_B, ATT_W)
    feat = (ATT_W, ATT_B)
    v_cb = COL_V // ATT_W
    return pl.pallas_call(
        body,
        out_shape=(jax.ShapeDtypeStruct((1024, T), F32), jax.ShapeDtypeStruct((T, 1024), F32),
                   jax.ShapeDtypeStruct(dproj.shape, dproj.dtype)),
        grid=(B, ATT_HEADS // ATT_HS, nq // 2, nq + 1),
        in_specs=[pl.BlockSpec(tok, mp["k_tok"]), pl.BlockSpec(feat, mp["q_feat"]), pl.BlockSpec(tok, mp["v_tok"]),
                  pl.BlockSpec(tok, mp["q_tok"]), pl.BlockSpec(feat, mp["k_feat"]),
                  pl.BlockSpec((None, ATT_B, ATT_B), mp["bias"]),
                  pl.BlockSpec(feat, mp["q_feat"]), pl.BlockSpec(feat, mp["q_feat"]),
                  pl.BlockSpec((None, ATT_HS, ATT_B), mp["lse"]), pl.BlockSpec(tok, mp["do_tok"]), ANY],
        out_specs=(pl.BlockSpec((ATT_W, S), lambda b, g, r, c: (g, b)),
                   pl.BlockSpec(tok, mp["k_tok"]),
                   pl.BlockSpec(tok, lambda b, g, r, c: (b * nq + qk(r, c)[1], v_cb + g))),
        input_output_aliases={10: 2},
        scratch_shapes=[pltpu.VMEM((ATT_HS, ATT_B, HEAD_DIM), F32), pltpu.VMEM((ATT_HS, ATT_B, HEAD_DIM), F32),
                        pltpu.VMEM((nq, ATT_W, ATT_B), F32),
                        pltpu.VMEM((ATT_HS, ATT_B, ATT_B), F32), pltpu.VMEM((ATT_HS, ATT_B, ATT_B), F32)],
        name=name, compiler_params=_cp("parallel", "parallel", "arbitrary", "arbitrary"))(
            kn, qT, proj, qn, knT, bias, doT, oT, lse, dyn, dproj)


def _group_cols(v):
    return v.reshape(SSD_GROUPS, 4)


def _ssd_params(p):
    rows = jnp.stack([_group_cols(p["dt_bias"]), _group_cols(p["a_log"]), _group_cols(p["d_skip"])], axis=1)
    return rows, jnp.swapaxes(rows, 1, 2)


def _dymix(name, dx, wout):
    T = dx.shape[0]

    def body(dx_ref, w_ref, o_ref):
        dxb = dx_ref[...].astype(BF16)
        for n in range(N_SHARD):
            o_ref[:, MIX_SH * n:MIX_SH * (n + 1)] = _dot(dxb, w_ref[n], NT)

    return pl.pallas_call(
        body, out_shape=jax.ShapeDtypeStruct((T, MIX_W), F32), grid=(T // ROW_T,),
        in_specs=[pl.BlockSpec((ROW_T, D_MODEL), lambda i: (i, 0)),
                  pl.BlockSpec((N_SHARD, MIX_SH, D_MODEL), lambda i: (0, 0, 0))],
        out_specs=pl.BlockSpec((ROW_T, MIX_W), lambda i: (i, 0)), name=name, compiler_params=_cp("parallel"))(dx, wout)


def _mixer_fwd(tag, x1, h2, p, weights, bias, B):
    T = x1.shape[0]
    nt = T // ROW_T
    wi = weights("win", h2)
    win, cw = wi["win"], wi["cw"]
    tm = min(GU_T, T)
    proj = _mm(tag + "_proj",
               [(h2, pl.BlockSpec((tm, D_MODEL), lambda j, i, k: (i, 0)),
                 win, pl.BlockSpec((D_MODEL, PROJ_TN), lambda j, i, k: (0, j)))],
               jax.ShapeDtypeStruct((T, IN_PAD), F32), pl.BlockSpec((tm, PROJ_TN), lambda j, i, k: (i, j)),
               (IN_PAD // PROJ_TN, T // tm, 1), NN, (tm, PROJ_TN))
    xc = _conv_fwd(tag + "_conv", proj, cw, p["conv_b"][None], B)
    dtraw = proj[:, COL_DT:COL_DT + SSD_HEADS].reshape(T, SSD_GROUPS, 4)
    dtc = jnp.transpose(dtraw, (1, 0, 2))
    dtr = jnp.transpose(dtraw, (1, 2, 0))
    pcol, prow = _ssd_params(p)
    Y, y_ssd, hs = _ssd_fwd(tag + "_ssd", xc, proj, dtc, dtr, pcol, prow, p["ssd_norm"][None], B)
    qn = _headnorm_fwd(tag + "_qn", proj, COL_Q // 1024, p["q_norm"][None])
    kn = _headnorm_fwd(tag + "_kn", proj, COL_K // 1024, p["k_norm"][None])
    qT = (qn * ATT_SCALE).T
    oT, lse = _att_fwd(tag + "_att", kn, qT, proj[:, COL_V:COL_V + 1024].T.astype(BF16), bias, B)
    ymix = jnp.concatenate([y_ssd, oT.T], axis=1)
    rest = weights("rest", ymix)
    o = _with_norm(T, p["ffn2_norm"][None])
    x2, h3 = _mm(tag + "_out",
                 [(ymix, pl.BlockSpec((ROW_T, MIX_SH), lambda i, n, k, j=j: (i, j)),
                   rest["wout"], pl.BlockSpec((None, MIX_SH, D_MODEL), lambda i, n, k, j=j: (j, 0, 0)))
                  for j in range(N_SHARD)],
                 o.pop("out_shape"), o.pop("out_spec"), (nt, 1, 1), NN, (ROW_T, D_MODEL), res=(x1, _row3()), **o)
    saved = dict(x1=x1, h2=h2, proj=proj, xc=xc, dtc=dtc, dtr=dtr, Y=Y, hs=hs,
                 qn=qn, kn=kn, qT=qT, oT=oT, lse=lse, ymix=ymix, win=win, cw=cw, wout=rest["wout"])
    return x2, h3, saved


def _mixer_bwd(tag, dx2, dx2_b, sv, p, bias, B):
    T = dx2.shape[0]
    nt = T // ROW_T
    sg = {}
    dymix = _dymix(tag + "_dymix", dx2_b, sv["wout"])
    tkw = min(TK_W, T)
    gwout = _mm(tag + "_dwout",
                [(sv["ymix"], pl.BlockSpec((tkw, MIX_SH), lambda m, n, k: (k, m)),
                  dx2_b, pl.BlockSpec((tkw, D_MODEL), lambda m, n, k: (k, 0)))],
                jax.ShapeDtypeStruct((N_SHARD, MIX_SH, D_MODEL), BF16),
                pl.BlockSpec((None, MIX_SH, D_MODEL), lambda m, n, k: (m, 0, 0)),
                (N_SHARD, 1, T // tkw), TN, (MIX_SH, D_MODEL))
    proj = sv["proj"]
    doT = dymix[:, 1024:].T
    dproj = lax.empty((T, IN_PAD), BF16)
    dqT, dkn, dproj = _att_bwd(tag + "_attb", sv["kn"], sv["qT"], proj, sv["qn"], sv["kn"].T, bias, doT, sv["oT"],
                               sv["lse"], dymix, dproj, B)
    dproj, sg["q_norm"] = _headnorm_bwd(tag + "_qnb", dqT.T, proj, COL_Q // 1024, p["q_norm"][None], dproj)
    dproj, sg["k_norm"] = _headnorm_bwd(tag + "_knb", dkn, proj, COL_K // 1024, p["k_norm"][None], dproj)
    pcol, prow = _ssd_params(p)
    dxs, dB, dC, dproj, ddt, dpar, dnw = _ssd_bwd(tag + "_ssdb", dymix, sv["Y"], sv["xc"], proj, sv["dtc"], sv["dtr"],
                                                  pcol, prow, p["ssd_norm"][None], sv["hs"], dproj, B)
    dpar = jnp.sum(dpar, axis=0)
    sg["dt_bias"] = dpar[:, 0, :].reshape(SSD_HEADS)
    sg["a_log"] = dpar[:, 1, :].reshape(SSD_HEADS)
    sg["d_skip"] = dpar[:, 2, :].reshape(SSD_HEADS)
    sg["ssd_norm"] = jnp.sum(dnw, axis=0)
    dproj, sg["conv_w"], sg["conv_b"] = _conv_bwd(tag + "_convb", proj, dxs, dB, dC, sv["cw"], p["conv_b"][None],
                                                  dproj, B)
    ddt16 = jnp.transpose(ddt, (1, 0, 2)).reshape(T, SSD_HEADS)
    dproj = lax.dynamic_update_slice(dproj, jnp.pad(ddt16, ((0, 0), (0, IN_PAD - COL_DT - SSD_HEADS))).astype(BF16),
                                     (0, COL_DT))
    win = sv["win"]
    gwin = _mm(tag + "_dwin",
               [(sv["h2"], pl.BlockSpec((tkw, D_MODEL), lambda n, m, k: (k, 0)),
                 dproj, pl.BlockSpec((tkw, PROJ_TN), lambda n, m, k: (k, n)))],
               jax.ShapeDtypeStruct((D_MODEL, IN_PAD), BF16), pl.BlockSpec((D_MODEL, PROJ_TN), lambda n, m, k: (0, n)),
               (IN_PAD // PROJ_TN, 1, T // tkw), TN, (D_MODEL, PROJ_TN))
    dx1, sg["mix_norm"], dx1_b = _mm(
        tag + "_dh2",
        [(dproj, pl.BlockSpec((ROW_T, PROJ_TN), lambda i, n, k, j=j: (i, j)),
          win, pl.BlockSpec((D_MODEL, PROJ_TN), lambda i, n, k, j=j: (0, j))) for j in range(IN_PAD // PROJ_TN)],
        *_rmsb_out(T), (nt, 1, 1), NT, (ROW_T, D_MODEL), post="rmsb",
        post_in=[(sv["x1"], _row3()), (p["mix_norm"][None], _vec3()), (dx2, _row3())])
    return dx1, dx1_b, sg, gwout, gwin


DT_LO =IN_SH * 2 - COL_Q


def _win_from_shards(sh):
    main = IN_SH - DT_LO
    return jnp.concatenate([sh[0], sh[1][:, :main], sh[2][:, SSD_HEADS - DT_LO:], sh[3], sh[1][:, main:],
                            sh[2][:, :SSD_HEADS - DT_LO], jnp.zeros((sh.shape[1], IN_PAD - IN_PROJ), sh.dtype)], axis=1)


def _win_to_shards(g):
    main = IN_SH - DT_LO
    a, b = IN_SH + main, IN_SH + 2 * main
    return jnp.stack([g[:, :IN_SH],
                      jnp.concatenate([g[:, IN_SH:a], g[:, COL_DT:COL_DT + DT_LO]], axis=1),
                      jnp.concatenate([g[:, COL_DT + DT_LO:COL_DT + SSD_HEADS], g[:, a:b]], axis=1),
                      g[:, b:COL_DT]])


def _local_step(x, target, small, weights, scatter, B):
    T = x.shape[0]
    bias = _att_bias((T // B) // ATT_B)
    saved = []
    xl = x
    hl = _rms_fwd("l0f1_rms", x, small["ffn1_norm"][0][None])
    for l in range(DEPTH):
        tag = "l%d" % l
        p = {k: v[l] for k, v in small.items()}
        w1 = weights(l, "ffn1", hl)
        (x1, h2), ffn1, d1 = _ffn_fwd(tag + "f1", xl, hl, w1["g1"], w1["u1"],
                                      lambda after, l=l: weights(l, "ffn1d", after)["d1"], p["mix_norm"][None])
        x2, h3, sv = _mixer_fwd(tag, x1, h2, p, functools.partial(weights, l), bias, B)
        w2 = weights(l, "rest", x2)
        nxt = small["ffn1_norm"][l + 1][None] if l + 1 < DEPTH else None
        xo, ffn2, _ = _ffn_fwd(tag + "f2", x2, h3, w2["g2"], w2["u2"], w2["d2"], nxt)
        xl, hl = xo if nxt is not None else (xo, None)
        saved.append((ffn1, sv, ffn2, dict(g1=w1["g1"], u1=w1["u1"], d1=d1), w2))
    d, db, lsum = _loss_grad("loss", xl, target)
    sgrads = [None] * DEPTH
    for l in reversed(range(DEPTH)):
        tag = "l%db" % l
        p = {k: v[l] for k, v in small.items()}
        ffn1, sv, ffn2, w1, w2 = saved[l]
        sg = {}
        d, sg["ffn2_norm"], db = _ffn_bwd(tag + "f2", d, db, ffn2, p["ffn2_norm"][None], w2["g2"], w2["u2"], w2["d2"],
                                          lambda gg, gu, gd, c, l=l: scatter(l, "ffn2", dict(g2=gg, u2=gu, d2=gd), c))
        d, db, sgm, gwout, gwin = _mixer_bwd(tag, d, db, sv, p, bias, B)
        sg.update(sgm)
        db = scatter(l, "mixer", dict(wout=gwout, win=gwin), db)
        d, sg["ffn1_norm"], db = _ffn_bwd(tag + "f1", d, db, ffn1, p["ffn1_norm"][None], w1["g1"], w1["u1"], w1["d1"],
                                          lambda gg, gu, gd, c, l=l: scatter(l, "ffn1", dict(g1=gg, u1=gu, d1=gd), c))
        sgrads[l] = sg
    return lsum, d, sgrads


MESH = pl.DeviceIdType.MESH
ANY = pl.BlockSpec(memory_space=pl.ANY)


def _place():
    return lax.axis_index("x"), lax.axis_index("y"), lax.axis_index("c")


def _other_chips(x, y):
    return [(1 - x, y), (x, 1 - y), (1 - x, 1 - y)]


HBM = pl.BlockSpec(memory_space=pltpu.HBM)
SEM = pl.BlockSpec(memory_space=pltpu.SEMAPHORE)
EFFECT = pltpu.SideEffectType.DATAFLOW_SIDE_EFFECTING


def _hbm(a):
    return pltpu.with_memory_space_constraint(a, pltpu.HBM)


def _my_half(ref, c):
    hr = ref.shape[0] // 2
    return ref.at[pl.ds(pl.multiple_of(c * hr, 16), hr)]


def _exchange(gather, layer, halves, src, land, send, recv, n, act):
    x, y, c = _place()
    for k, (px, py) in enumerate(_other_chips(x, y)):
        for a in range(n):
            if gather:
                s_out, d_out, d_in = src[a].at[layer], land[a].at[2 * x + y], land[a].at[2 * px + py]
                if halves is not None and halves[a]:
                    s_out, d_out, d_in = _my_half(s_out, c), _my_half(d_out, c), _my_half(d_in, c)
            else:
                s_out, d_out, d_in = src[a].at[2 * px + py], land[a].at[k], land[a].at[k]
            act(pltpu.make_async_remote_copy(
                src_ref=s_out, dst_ref=d_out if act is _start else d_in, send_sem=send.at[k * n + a],
                recv_sem=recv.at[k * n + a], device_id=(px, py, c), device_id_type=MESH))


def _start(cp):
    cp.start()


def _finish(cp):
    cp.wait_send()
    cp.wait_recv()


def _exchange_start(name, gather, layer, srcs, carry, halves=None):
    n = len(srcs)
    lands = [lax.empty(((N_SHARD,) + s.shape[1:]) if gather else ((3,) + s.shape[1:]), s.dtype) for s in srcs]

    def body(*refs):
        _exchange(gather, layer, halves, refs[:n], refs[n:2 * n], refs[2 * n + 1], refs[2 * n + 2], n, _start)

    srcs = [_hbm(a) for a in srcs]
    thru = [_hbm(a) for a in lands + [carry]]
    out = pl.pallas_call(
        body, name=name,
        out_shape=(pltpu.SemaphoreType.DMA((3 * n,)), pltpu.SemaphoreType.DMA((3 * n,)),
                   *[pltpu.HBM(a.shape, a.dtype) for a in thru]),
        in_specs=[HBM] * (2 * n + 1), out_specs=(SEM, SEM, *[HBM] * (n + 1)),
        input_output_aliases={n + i: 2 + i for i in range(n + 1)},
        compiler_params=pltpu.CompilerParams(has_side_effects=EFFECT))(*srcs, *thru)
    return dict(gather=gather, layer=layer, halves=halves, send=out[0], recv=out[1], srcs=srcs,
                lands=list(out[2:2 + n])), out[-1]


def _exchange_wait(name, ex, after):
    n = len(ex["srcs"])

    def body(*refs):
        _exchange(ex["gather"], ex["layer"], ex["halves"], refs[:n], refs[n:2 * n], refs[2 * n], refs[2 * n + 1], n,
                  _finish)

    out = pl.pallas_call(
        body, name=name, out_shape=[pltpu.HBM(a.shape, a.dtype) for a in ex["lands"]],
        in_specs=[HBM] * (2 * n) + [SEM, SEM, ANY], out_specs=[HBM] * n,
        input_output_aliases={n + i: i for i in range(n)},
        compiler_params=pltpu.CompilerParams(has_side_effects=EFFECT))(
            *ex["srcs"], *ex["lands"], ex["send"], ex["recv"], after)
    return list(out)


def _sibling_fill(name, lands):
    n = len(lands)

    def body(*refs):
        land = refs[:n]
        send, recv = refs[2 * n], refs[2 * n + 1]
        x, y, c = _place()
        for k, (px, py) in enumerate(_other_chips(x, y)):
            for a in range(n):
                slot = land[a].at[2 * px + py]
                pltpu.make_async_remote_copy(src_ref=_my_half(slot, c), dst_ref=_my_half(slot, c),
                                             send_sem=send.at[k * n + a], recv_sem=recv.at[k * n + a],
                                             device_id=(x, y, 1 - c), device_id_type=MESH).start()
        for k, (px, py) in enumerate(_other_chips(x, y)):
            for a in range(n):
                slot = land[a].at[2 * px + py]
                cp = pltpu.make_async_remote_copy(src_ref=_my_half(slot, c), dst_ref=_my_half(slot, 1 - c),
                                                  send_sem=send.at[k * n + a], recv_sem=recv.at[k * n + a],
                                                  device_id=(x, y, 1 - c), device_id_type=MESH)
                cp.wait_recv()
                cp.wait_send()

    return pl.pallas_call(
        body, out_shape=[jax.ShapeDtypeStruct(a.shape, a.dtype) for a in lands],
        in_specs=[ANY] * n, out_specs=[ANY] * n, input_output_aliases={i: i for i in range(n)},
        scratch_shapes=[pltpu.SemaphoreType.DMA((3 * n,)), pltpu.SemaphoreType.DMA((3 * n,))],
        name=name)(*lands)


def _swap_sibling(name, parts):
    n = len(parts)

    def body(*refs):
        src, dst = refs[:n], refs[n:2 * n]
        send, recv = refs[2 * n:]
        x, y, c = _place()
        cps = [pltpu.make_async_remote_copy(src_ref=src[a], dst_ref=dst[a], send_sem=send.at[a], recv_sem=recv.at[a],
                                            device_id=(x, y, 1 - c), device_id_type=MESH) for a in range(n)]
        for cp in cps:
            cp.start()
        for cp in cps:
            cp.wait_recv()
        for cp in cps:
            cp.wait_send()

    return pl.pallas_call(
        body, out_shape=[jax.ShapeDtypeStruct(p.shape, p.dtype) for p in parts],
        in_specs=[ANY] * n, out_specs=[ANY] * n,
        scratch_shapes=[pltpu.SemaphoreType.DMA((n,)), pltpu.SemaphoreType.DMA((n,))],
        name=name)(*parts)


def _allreduce_small(name, v, after):
    R = v.shape[0]

    def body(v_ref, after_ref, o_ref, buf, send, recv):
        x, y, c = _place()
        me = 4 * x + 2 * y + c
        buf[me] = v_ref[...]
        cps = []
        for k in range(1, 8):
            fx, fy, fc = (k >> 2) & 1, (k >> 1) & 1, k & 1
            px = 1 - x if fx else x
            py = 1 - y if fy else y
            pc = 1 - c if fc else c
            cp = pltpu.make_async_remote_copy(src_ref=v_ref, dst_ref=buf.at[me], send_sem=send.at[k - 1],
                                              recv_sem=recv.at[k - 1], device_id=(px, py, pc), device_id_type=MESH)
            cp.start()
            cps.append((cp, 4 * px + 2 * py + pc))
        for k, (cp, peer) in enumerate(cps):
            pltpu.make_async_remote_copy(src_ref=v_ref, dst_ref=buf.at[peer], send_sem=send.at[k], recv_sem=recv.at[k],
                                         device_id=(x, y, c), device_id_type=MESH).wait_recv()
        for cp, _ in cps:
            cp.wait_send()
        acc = buf[0]
        for d in range(1, 8):
            acc = acc + buf[d]
        o_ref[...] = acc

    return pl.pallas_call(
        body, out_shape=jax.ShapeDtypeStruct((R, 128), F32),
        in_specs=[pl.BlockSpec(memory_space=pltpu.VMEM), ANY], out_specs=pl.BlockSpec(memory_space=pltpu.VMEM),
        scratch_shapes=[pltpu.VMEM((8, R, 128), F32), pltpu.SemaphoreType.DMA((7,)), pltpu.SemaphoreType.DMA((7,))],
        name=name)(v, after)


TILE_BYTES = 1600 * 1024


def _row_tile(r, c=1024):
    for t in (512, 352, 256, 128, 64, 32, 16, 8):
        if r % t == 0 and (t * c * 4 <= TILE_BYTES or t == 8):
            return t
    raise ValueError(r)


def _sum4(name, me, parts, got):
    _, R, C = parts.shape
    tr = _row_tile(R, C)

    def body(me_ref, o_ref, g_ref, s_ref):
        s = o_ref[...].astype(F32)
        for k in range(3):
            s = s + g_ref[k].astype(F32)
        s_ref[...] = s.astype(BF16)

    return pl.pallas_call(
        body, out_shape=jax.ShapeDtypeStruct((R, C), BF16),
        grid_spec=pltpu.PrefetchScalarGridSpec(
            num_scalar_prefetch=1, grid=(R // tr,),
            in_specs=[pl.BlockSpec((None, tr, C), lambda i, me_ref: (me_ref[0], i, 0)),
                      pl.BlockSpec((3, tr, C), lambda i, me_ref: (0, i, 0))],
            out_specs=pl.BlockSpec((tr, C), lambda i, me_ref: (i, 0))),
        name=name, compiler_params=_cp("parallel"))(me, parts, got)


def _adamw(name, w, gparts, m, v):
    R, C = w.shape
    tr = _row_tile(R, C)
    ng = len(gparts)
    c1 = 1.0 - ADAM_B1 ** ADAM_STEP
    c2 = 1.0 - ADAM_B2 ** ADAM_STEP

    def body(*refs):
        w_ref = refs[0]
        g_refs = refs[1:1 + ng]
        m_ref, v_ref, go_ref, d_ref, mo_ref, vo_ref = refs[1 + ng:]
        g = g_refs[0][...]
        for r in g_refs[1:]:
            g = g + r[...]
        mn = ADAM_B1 * m_ref[...] + (1.0 - ADAM_B1) * g
        vn = ADAM_B2 * v_ref[...] + (1.0 - ADAM_B2) * (g * g)
        go_ref[...] = g
        mo_ref[...] = mn
        vo_ref[...] = vn
        d_ref[...] = -ADAM_LR * ((mn / c1) / (jnp.sqrt(vn / c2) + ADAM_EPS) + ADAM_WD * w_ref[...])

    blk = pl.BlockSpec((tr, C), lambda i: (i, 0))
    osh = jax.ShapeDtypeStruct((R, C), F32)
    return pl.pallas_call(
        body, out_shape=(osh, osh, osh, osh), grid=(R // tr,), in_specs=[blk] * (3 + ng), out_specs=(blk,) * 4,
        name=name, compiler_params=_cp("parallel"))(w, *gparts, m, v)


def _adamw_layers(name, w, sums, m, v):
    _, R, C = w.shape
    tr = _row_tile(R, C)
    nr = R // tr
    c1 = 1.0 - ADAM_B1 ** ADAM_STEP
    c2 = 1.0 - ADAM_B2 ** ADAM_STEP

    def body(w_ref, a0, b0, a1, b1, m_ref, v_ref, go_ref, d_ref, mo_ref, vo_ref):
        f = lambda r: r[...].astype(F32)
        g = jnp.where(pl.program_id(0) == 0, f(a0) + f(b0), f(a1) + f(b1))
        mn = ADAM_B1 * m_ref[...] + (1.0 - ADAM_B1) * g
        vn = ADAM_B2 * v_ref[...] + (1.0 - ADAM_B2) * (g * g)
        go_ref[...] = g
        mo_ref[...] = mn
        vo_ref[...] = vn
        d_ref[...] = -ADAM_LR * ((mn / c1) / (jnp.sqrt(vn / c2) + ADAM_EPS) + ADAM_WD * w_ref[...])

    blk = pl.BlockSpec((None, tr, C), lambda l, i: (l, i, 0))
    lay0 = pl.BlockSpec((tr, C), lambda l, i: (jnp.where(l == 0, i, nr - 1), 0))
    lay1 = pl.BlockSpec((tr, C), lambda l, i: (jnp.where(l == 1, i, 0), 0))
    oblk = pl.BlockSpec((tr, C), lambda l, i: (l * nr + i, 0))
    osh = jax.ShapeDtypeStruct((DEPTH * R, C), F32)
    res = pl.pallas_call(
        body, out_shape=(osh, osh, osh, osh), grid=(DEPTH, nr),
        in_specs=[blk, lay0, lay0, lay1, lay1, blk, blk], out_specs=(oblk,) * 4,
        name=name, compiler_params=_cp("arbitrary", "arbitrary"))(w, *sums[0], *sums[1], m, v)
    return [r.reshape(w.shape) for r in res]


BIG = [("ffn1_w_gate", "g1"), ("ffn1_w_up", "u1"), ("ffn1_w_down", "d1"), ("w_in", "win"), ("w_out", "wout"),
       ("ffn2_w_gate", "g2"), ("ffn2_w_up", "u2"), ("ffn2_w_down", "d2")]
SMALL = ["ffn1_norm", "mix_norm", "conv_b", "dt_bias", "a_log", "d_skip", "ssd_norm", "q_norm", "k_norm", "ffn2_norm"]
WEIGHTS = ["ffn1_norm", "ffn1_w_gate", "ffn1_w_up", "ffn1_w_down", "mix_norm", "w_in", "conv_w", "conv_b", "dt_bias",
           "a_log", "d_skip", "ssd_norm", "q_norm", "k_norm", "w_out", "ffn2_norm", "ffn2_w_gate", "ffn2_w_up",
           "ffn2_w_down"]
CONV_SH = CONV_DIM // N_SHARD
TRANSPOSED = ("g1", "u1", "g2", "u2")
GATHER_GROUPS = [(0, "ffn1", ["g1", "u1"]), (0, "ffn1d", ["d1"]), (0, "win", ["win", "cw"]),
                 (0, "rest", ["wout", "g2", "u2", "d2"]),
                 (1, "all", ["g1", "u1", "d1", "win", "cw", "wout", "g2", "u2", "d2"])]


def _pad128(v):
    v = v.reshape(-1)
    return jnp.pad(v, (0, (-v.shape[0]) % 128))


def _pack(pieces):
    flat, offs, pos = [], [], 0
    for p in pieces:
        q = _pad128(p.astype(F32))
        offs.append(pos)
        pos += q.shape[0] // 128
        flat.append(q)
    total = -(-pos // 8) * 8
    out = jnp.concatenate(flat + [jnp.zeros(((total - pos) * 128,), F32)]).reshape(total, 128)
    return out, offs


def _unpack(packed, offs, shapes):
    out = []
    for off, shp in zip(offs, shapes):
        n = int(np.prod(shp))
        rows = -(-n // 128)
        out.append(packed[off:off + rows].reshape(-1)[:n].reshape(shp))
    return out


def kernel(x, ffn1_norm, ffn1_w_gate, ffn1_w_up, ffn1_w_down, mix_norm, w_in, conv_w, conv_b, dt_bias, a_log, d_skip, ssd_norm, q_norm, k_norm, w_out, ffn2_norm, ffn2_w_gate, ffn2_w_up, ffn2_w_down, loss_target, m_ffn1_norm, m_ffn1_w_gate, m_ffn1_w_up, m_ffn1_w_down, m_mix_norm, m_w_in, m_conv_w, m_conv_b, m_dt_bias, m_a_log, m_d_skip, m_ssd_norm, m_q_norm, m_k_norm, m_w_out, m_ffn2_norm, m_ffn2_w_gate, m_ffn2_w_up, m_ffn2_w_down, v_ffn1_norm, v_ffn1_w_gate, v_ffn1_w_up, v_ffn1_w_down, v_mix_norm, v_w_in, v_conv_w, v_conv_b, v_dt_bias, v_a_log, v_d_skip, v_ssd_norm, v_q_norm, v_k_norm, v_w_out, v_ffn2_norm, v_ffn2_w_gate, v_ffn2_w_up, v_ffn2_w_down):
    A = dict(locals())
    ix, iy, ic = _place()
    me = 2 * ix + iy
    B, S, _ = x.shape
    T = B * S

    view = lambda a, key: jnp.swapaxes(a, 1, 2) if key in TRANSPOSED else a
    own = {key: view(A[name], key).astype(BF16) for name, key in BIG}
    own["cw"] = conv_w
    exs, first_norm = [], ffn1_norm
    split = lambda l, key: l == 0 and key != "cw"
    for gi, (l, _, keys) in enumerate(GATHER_GROUPS):
        ex, first_norm = _exchange_start("gather_start%d" % gi, True, l, [own[key] for key in keys], first_norm,
                                         [split(l, key) for key in keys])
        exs.append(ex)
    landed = {}

    def weights(l, group, after):
        gi = [i for i, (gl, gname, _) in enumerate(GATHER_GROUPS) if gl == l and gname in (group, "all")][0]
        if gi not in landed:
            lands = _exchange_wait("gather_wait%d" % gi, exs[gi], after)
            keys = GATHER_GROUPS[gi][2]
            halved = [i for i, key in enumerate(keys) if split(l, key)]
            if halved:
                for i, whole in zip(halved, _sibling_fill("gather_fill%d" % gi, [lands[i] for i in halved])):
                    lands[i] = whole
            landed[gi] = {}
            for key, land in zip(GATHER_GROUPS[gi][2], lands):
                full = lax.dynamic_update_slice(land, own[key][l][None], (me, 0, 0))
                if key == "win":
                    full = _win_from_shards(full)
                if key == "cw":
                    full = jnp.transpose(full, (1, 0, 2)).reshape(CONV_K, CONV_DIM)
                landed[gi][key] = full
        return landed[gi]

    pending = []

    def scatter(l, group, grads, carry):
        keys = sorted(grads)
        arrs = [grads[key] for key in keys]
        if "win" in grads:
            arrs[keys.index("win")] = _win_to_shards(grads["win"])
        ex, carry = _exchange_start("scatter_start_l%d_%s" % (l, group), False, None, arrs, carry)
        pending.append((l, keys, ex))
        return carry

    small = {name: A[name] for name in SMALL}
    small["ffn1_norm"] = first_norm
    lsum, dx, sgrads = _local_step(x.reshape(T, D_MODEL), loss_target.reshape(T, D_MODEL), small, weights, scatter, B)

    names = SMALL + ["conv_w"]
    shapes = [A[n].shape for n in SMALL] + [(DEPTH, CONV_K, CONV_DIM), ()]
    pieces = [jnp.stack([sgrads[l][n].reshape(shp[1:]) for l in range(DEPTH)]) for n, shp in zip(names, shapes)]
    pieces.append(0.5 / D_MODEL * jnp.sum(lsum))
    packed, offs = _pack(pieces)

    sums, theirs, out = {}, {}, {}
    me1 = jnp.reshape(me, (1,)).astype(jnp.int32)

    def update(tag, after):
        todo = [k for k in sums if k not in theirs]
        theirs.update(zip(todo, _swap_sibling("swap_sibling_" + tag, [sums[k] for k in todo])))
        for name, key in BIG:
            if name not in out and all((key, l) in theirs for l in range(DEPTH)):
                res = _adamw_layers("adamw_" + key, view(A[name], key),
                                    [(sums[key, l], theirs[key, l]) for l in range(DEPTH)],
                                    view(A["m_" + name], key), view(A["v_" + name], key))
                out[name] = [view(r, key) for r in res]
                after = res[0]
        return after

    after = dx
    for idx, (l, keys, ex) in enumerate(pending):
        if idx == len(pending) - 1:
            after = update("a", after)
        lands = _exchange_wait("scatter_wait%d" % idx, ex, after)
        for key, g, got in zip(keys, ex["srcs"], lands):
            sums[key, l] = after = _sum4("sum_%s_l%d" % (key, l), me1, g, got)
    after = update("b", after)

    red = _unpack(_allreduce_small("allreduce_small", packed, after), offs, shapes)
    loss = red[-1]
    sg = dict(zip(names, red[:-1]))

    wp, offs = _pack([A[n] for n in SMALL])
    gp, _ = _pack([sg[n] for n in SMALL])
    mp, _ = _pack([A["m_" + n] for n in SMALL])
    vp, _ = _pack([A["v_" + n] for n in SMALL])
    res = _adamw("adamw_small", wp, [gp], mp, vp)
    shapes = [A[n].shape for n in SMALL]
    res = [_unpack(r, offs, shapes) for r in res]
    for i, n in enumerate(SMALL):
        out[n] = [res[q][i] for q in range(4)]
    gcw = lax.dynamic_slice_in_dim(sg["conv_w"], me * CONV_SH, CONV_SH, axis=2)
    flat = lambda a: a.reshape(DEPTH * CONV_K, CONV_SH)
    res = _adamw("adamw_conv_w", flat(conv_w), [flat(gcw)], flat(m_conv_w), flat(v_conv_w))
    out["conv_w"] = [r.reshape(conv_w.shape) for r in res]

    outs = [loss, dx.reshape(B, S, D_MODEL)]
    for q in range(4):
        outs += [out[n][q] for n in WEIGHTS]
    return tuple(outs)
```

```python
import functools
import math

import numpy as np
import jax
import jax.numpy as jnp
from jax import lax
from jax.experimental import pallas as pl
from jax.experimental.pallas import tpu as pltpu

F32 = jnp.float32
BF16 = jnp.bfloat16

D_MODEL = 1024
DEPTH = 2
N_SHARD = 4
D_FF = 2816
FF_SH = D_FF // N_SHARD
SSD_HEADS = 16
HEAD_DIM = 64
SSD_GROUPS = 4
GROUP_W = 256
SSD_STATE = 128
CONV_K = 4
CONV_DIM = 2048
ATT_HEADS = 16
MIX_W = 2048
MIX_SH = MIX_W // N_SHARD
IN_PROJ = 6160
IN_SH = IN_PROJ // N_SHARD
IN_PAD = 6272
PROJ_TN = 896
COL_Z, COL_XBC, COL_Q, COL_K, COL_V, COL_DT = 0, 1024, 3072, 4096, 5120, 6144
EPS = 1e-6
NEG = -1e30
SSD_L = 512
ATT_B = 512
ROW_T = 512
HALF_T = 2 * ROW_T
GU_T = 2048
TK_W = 4096
CONV_CT = 256
CONV_R = 256
PAD_R = 8

ADAM_LR, ADAM_B1, ADAM_B2, ADAM_EPS, ADAM_WD, ADAM_STEP = 0.001, 0.9, 0.999, 1e-08, 0.01, 10

NN = (((1,), (0,)), ((), ()))
NT = (((1,), (1,)), ((), ()))
TN = (((0,), (0,)), ((), ()))

VMEM_LIMIT = 56 * 1024 * 1024


def _cp(*sem):
    return pltpu.CompilerParams(dimension_semantics=sem, vmem_limit_bytes=VMEM_LIMIT)


def _dot(a, b, dims):
    return lax.dot_general(a, b, dims, preferred_element_type=F32)


def _sigmoid(x):
    return 0.5 * jnp.tanh(0.5 * x) + 0.5


def _softplus(x):
    return jnp.maximum(x, 0.0) + jnp.log(1.0 + jnp.exp(-jnp.abs(x)))


def _mm(name, pairs, out_shape, out_spec, grid, dims, acc_shape, res=None, scale=1.0, post=None, post_in=()):
    nk = grid[2]
    npair = len(pairs)
    npost = len(post_in)

    def body(*refs):
        ab = refs[:2 * npair]
        pos = 2 * npair
        res_ref = None
        if res is not None:
            res_ref = refs[pos]
            pos += 1
        pin = refs[pos:pos + npost]
        pos += npost
        out_ref = refs[pos]
        pos += 1
        if post is not None:
            out2_ref = refs[pos]
            pos += 1
        if post == "rmsb":
            out3_ref = refs[pos]
            pos += 1
        s = None
        for p in range(npair):
            d = _dot(ab[2 * p][...].astype(BF16), ab[2 * p + 1][...].astype(BF16), dims)
            s = d if s is None else s + d

        def finish(r):
            if scale != 1.0:
                r = r * scale
            if res_ref is not None:
                r = r + res_ref[...]
            if post == "rmsb":
                @pl.when(pl.program_id(0) == 0)
                def _():
                    out2_ref[...] = jnp.zeros_like(out2_ref)

                xv = pin[0][...]
                rr = lax.rsqrt(jnp.mean(xv * xv, axis=-1, keepdims=True) + EPS)
                xhat = xv * rr
                dxhat = r * pin[1][...]
                dx = pin[2][...] + rr * (dxhat - xhat * jnp.mean(dxhat * xhat, axis=-1, keepdims=True))
                out_ref[...] = dx
                out2_ref[...] += jnp.sum(r * xhat, axis=0, keepdims=True)
                out3_ref[...] = dx.astype(BF16)
                return
            out_ref[...] = r.astype(out_ref.dtype)
            if post == "norm":
                rr = lax.rsqrt(jnp.mean(r * r, axis=-1, keepdims=True) + EPS)
                out2_ref[...] = (r * rr * pin[0][...]).astype(BF16)

        if nk == 1:
            finish(s)
            return
        acc = refs[pos]
        k = pl.program_id(2)

        @pl.when(k == 0)
        def _():
            acc[...] = s

        @pl.when(k > 0)
        def _():
            acc[...] += s

        @pl.when(k == nk - 1)
        def _():
            finish(acc[...])

    args, specs = [], []
    for a, a_spec, b, b_spec in pairs:
        args += [a, b]
        specs += [a_spec, b_spec]
    for arr, spec in ([res] if res is not None else []) + list(post_in):
        args.append(arr)
        specs.append(spec)
    sems = ("arbitrary",) * 3 if post == "rmsb" else ("parallel", "parallel", "arbitrary")
    return pl.pallas_call(
        body, out_shape=out_shape, grid=grid, in_specs=specs, out_specs=out_spec,
        scratch_shapes=[] if nk == 1 else [pltpu.VMEM(acc_shape, F32)], name=name,
        compiler_params=_cp(*sems))(*args)


def _rms_fwd(name, x, w):
    T = x.shape[0]

    def body(x_ref, w_ref, o_ref):
        xv = x_ref[...]
        r = lax.rsqrt(jnp.mean(xv * xv, axis=-1, keepdims=True) + EPS)
        o_ref[...] = (xv * r * w_ref[...]).astype(BF16)

    return pl.pallas_call(
        body, out_shape=jax.ShapeDtypeStruct((T, D_MODEL), BF16), grid=(T // ROW_T,),
        in_specs=[pl.BlockSpec((ROW_T, D_MODEL), lambda i: (i, 0)), pl.BlockSpec((1, D_MODEL), lambda i: (0, 0))],
        out_specs=pl.BlockSpec((ROW_T, D_MODEL), lambda i: (i, 0)), name=name, compiler_params=_cp("parallel"))(x, w)


def _loss_grad(name, y, t):
    T = y.shape[0]

    def body(y_ref, t_ref, dy_ref, dyb_ref, l_ref):
        @pl.when(pl.program_id(0) == 0)
        def _():
            l_ref[...] = jnp.zeros_like(l_ref)

        e = y_ref[...] - t_ref[...]
        dy = e * (1.0 / D_MODEL)
        dy_ref[...] = dy
        dyb_ref[...] = dy.astype(BF16)
        l_ref[...] += jnp.sum(e * e, axis=0, keepdims=True)

    row = pl.BlockSpec((ROW_T, D_MODEL), lambda i: (i, 0))
    vec = pl.BlockSpec((1, D_MODEL), lambda i: (0, 0))
    return pl.pallas_call(
        body, out_shape=(jax.ShapeDtypeStruct((T, D_MODEL), F32), jax.ShapeDtypeStruct((T, D_MODEL), BF16),
                         jax.ShapeDtypeStruct((1, D_MODEL), F32)),
        grid=(T // ROW_T,), in_specs=[row, row], out_specs=(row, row, vec), name=name,
        compiler_params=_cp("arbitrary"))(y, t)


def _ffn_gate_up(name, h, wg, wu):
    T = h.shape[0]
    tm = min(GU_T, T)

    def body(h_ref, wg_ref, wu_ref, dgf_ref, duf_ref, a_ref):
        for r in range(0, tm, HALF_T):
            rows = slice(r, r + HALF_T)
            hv = h_ref[rows, :]
            g = _dot(hv, wg_ref[...], NT)
            u = _dot(hv, wu_ref[...], NT)
            sg = _sigmoid(g)
            silu = g * sg
            dgf_ref[rows, :] = (u * (sg * (1.0 + g * (1.0 - sg)))).astype(BF16)
            duf_ref[rows, :] = silu.astype(BF16)
            a_ref[rows, :] = (silu * u).astype(BF16)

    wspec = pl.BlockSpec((None, FF_SH, D_MODEL), lambda j, i: (j, 0, 0))
    ospec = pl.BlockSpec((None, tm, FF_SH), lambda j, i: (j, i, 0))
    osh = jax.ShapeDtypeStruct((N_SHARD, T, FF_SH), BF16)
    return pl.pallas_call(
        body, out_shape=(osh, osh, osh), grid=(N_SHARD, T // tm),
        in_specs=[pl.BlockSpec((tm, D_MODEL), lambda j, i: (i, 0)), wspec, wspec],
        out_specs=(ospec, ospec, ospec), name=name, compiler_params=_cp("parallel", "parallel"))(h, wg, wu)


def _ffn_dact(name, dx, wd, g, u):
    T = dx.shape[0]
    tm = min(GU_T, T)

    def body(dx_ref, wd_ref, g_ref, u_ref, dg_ref, du_ref):
        for r in range(0, tm, HALF_T):
            rows = slice(r, r + HALF_T)
            da = 0.5 * _dot(dx_ref[rows, :].astype(BF16), wd_ref[...], NT)
            dg_ref[rows, :] = (da * g_ref[rows, :].astype(F32)).astype(BF16)
            du_ref[rows, :] = (da * u_ref[rows, :].astype(F32)).astype(BF16)

    aspec = pl.BlockSpec((None, tm, FF_SH), lambda j, i: (j, i, 0))
    osh = jax.ShapeDtypeStruct((N_SHARD, T, FF_SH), BF16)
    return pl.pallas_call(
        body, out_shape=(osh, osh), grid=(N_SHARD, T // tm),
        in_specs=[pl.BlockSpec((tm, D_MODEL), lambda j, i: (i, 0)),
                  pl.BlockSpec((None, FF_SH, D_MODEL), lambda j, i: (j, 0, 0)), aspec, aspec],
        out_specs=(aspec, aspec), name=name, compiler_params=_cp("parallel", "parallel"))(dx, wd, g, u)


def _row3():
    return pl.BlockSpec((ROW_T, D_MODEL), lambda i, n, k: (i, 0))


def _vec3():
    return pl.BlockSpec((1, D_MODEL), lambda i, n, k: (0, 0))


def _with_norm(T, next_nw):
    if next_nw is None:
        return dict(out_shape=jax.ShapeDtypeStruct((T, D_MODEL), F32), out_spec=_row3())
    return dict(out_shape=(jax.ShapeDtypeStruct((T, D_MODEL), F32), jax.ShapeDtypeStruct((T, D_MODEL), BF16)),
                out_spec=(_row3(), _row3()), post="norm", post_in=[(next_nw, _vec3())])


def _ffn_fwd(tag, x, h, wg, wu, wd, next_nw):
    T = x.shape[0]
    g, u, a = _ffn_gate_up(tag + "_gu", h, wg, wu)
    if callable(wd):
        wd = wd(a)
    nt = T // ROW_T
    o = _with_norm(T, next_nw)
    xo = _mm(tag + "_down",
             [(a, pl.BlockSpec((None, ROW_T, FF_SH), lambda i, n, k, j=j: (j, i, 0)),
               wd, pl.BlockSpec((None, FF_SH, D_MODEL), lambda i, n, k, j=j: (j, 0, 0))) for j in range(N_SHARD)],
             o.pop("out_shape"), o.pop("out_spec"), (nt, 1, 1), NN, (ROW_T, D_MODEL),
             res=(x, _row3()), scale=0.5, **o)
    return xo, (x, h, g, u, a), wd


def _rmsb_out(T):
    f = jax.ShapeDtypeStruct
    return (f((T, D_MODEL), F32), f((1, D_MODEL), F32), f((T, D_MODEL), BF16)), (_row3(), _vec3(), _row3())


def _ffn_bwd(tag, dxo, dxo_b, saved, nw, wg, wu, wd, emit):
    x, h, g, u, a = saved
    T = x.shape[0]
    nt = T // ROW_T
    tkw = min(TK_W, T)
    nw_t = T // tkw
    dg, du = _ffn_dact(tag + "_dact", dxo_b, wd, g, u)
    actw = lambda f: pl.BlockSpec((None, tkw, FF_SH), f)
    gd = _mm(tag + "_dwd",
             [(a, actw(lambda m, n, k: (m, k, 0)), dxo_b, pl.BlockSpec((tkw, D_MODEL), lambda m, n, k: (k, 0)))],
             jax.ShapeDtypeStruct((N_SHARD, FF_SH, D_MODEL), BF16),
             pl.BlockSpec((None, FF_SH, D_MODEL), lambda m, n, k: (m, 0, 0)),
             (N_SHARD, 1, nw_t), TN, (FF_SH, D_MODEL), scale=0.5)
    hspec = pl.BlockSpec((tkw, D_MODEL), lambda j, n, k: (k, 0))
    gsh = jax.ShapeDtypeStruct((N_SHARD, FF_SH, D_MODEL), BF16)
    gspec = pl.BlockSpec((None, FF_SH, D_MODEL), lambda j, n, k: (j, 0, 0))
    gg = _mm(tag + "_dwg", [(dg, actw(lambda j, n, k: (j, k, 0)), h, hspec)], gsh, gspec,
             (N_SHARD, 1, nw_t), TN, (FF_SH, D_MODEL))
    gu = _mm(tag + "_dwu", [(du, actw(lambda j, n, k: (j, k, 0)), h, hspec)], gsh, gspec,
             (N_SHARD, 1, nw_t), TN, (FF_SH, D_MODEL))
    dg = emit(gg, gu, gd, dg)
    act = lambda j: pl.BlockSpec((None, ROW_T, FF_SH), lambda i, n, k: (j, i, 0))
    wsp = lambda j: pl.BlockSpec((None, FF_SH, D_MODEL), lambda i, n, k: (j, 0, 0))
    return _mm(tag + "_dh",
               [(dd, act(j), w, wsp(j)) for j in range(N_SHARD) for dd, w in ((dg, wg), (du, wu))],
               *_rmsb_out(T), (nt, 1, 1), NN, (ROW_T, D_MODEL), post="rmsb",
               post_in=[(x, _row3()), (nw, _vec3()), (dxo, _row3())])


def _seq_rows(ref, start, size, S):
    lo, hi = max(start, 0), min(start + size, S)
    parts = [ref[pl.ds(lo, hi - lo), :]]
    if lo > start:
        parts.insert(0, jnp.zeros((lo - start, ref.shape[1]), F32))
    if start + size > hi:
        parts.append(jnp.zeros((start + size - hi, ref.shape[1]), F32))
    return parts[0] if len(parts) == 1 else jnp.concatenate(parts, axis=0)


XBC_CB = COL_XBC // CONV_CT


def _conv_fwd(name, proj, w, b, B):
    T = proj.shape[0]
    S = T // B
    C = CONV_DIM

    def body(x_ref, w_ref, b_ref, o_ref):
        wv = w_ref[...]
        for c in range(S // CONV_R):
            r0 = c * CONV_R
            ch = _seq_rows(x_ref, r0 - PAD_R, CONV_R + PAD_R, S)
            pre = ch[PAD_R:] * wv[3:4] + b_ref[...]
            for s in range(1, CONV_K):
                pre = pre + pltpu.roll(ch, s, axis=0)[PAD_R:] * wv[3 - s:4 - s]
            o_ref[pl.ds(r0, CONV_R), :] = pre * _sigmoid(pre)

    return pl.pallas_call(
        body, out_shape=jax.ShapeDtypeStruct((T, C), F32), grid=(B, C // CONV_CT),
        in_specs=[pl.BlockSpec((S, CONV_CT), lambda bi, ci: (bi, XBC_CB + ci)),
                  pl.BlockSpec((CONV_K, CONV_CT), lambda bi, ci: (0, ci)),
                  pl.BlockSpec((1, CONV_CT), lambda bi, ci: (0, ci))],
        out_specs=pl.BlockSpec((S, CONV_CT), lambda bi, ci: (bi, ci)), name=name,
        compiler_params=_cp("parallel", "parallel"))(proj, w, b)


def _conv_bwd(name, proj, dxs, dB, dC, w, b, dproj, B):
    T = proj.shape[0]
    S = T // B
    C = CONV_DIM
    RW = CONV_R + PAD_R
    nx, nb = dxs.shape[1] // CONV_CT, dB.shape[1] // CONV_CT

    def body(x_ref, dx_in, db_in, dc_in, w_ref, b_ref, buf_ref, dx_ref, dw_ref, db_ref):
        @pl.when(pl.program_id(1) == 0)
        def _():
            dw_ref[...] = jnp.zeros_like(dw_ref)
            db_ref[...] = jnp.zeros_like(db_ref)

        ci = pl.program_id(0)
        wv = w_ref[...]
        dw = [jnp.zeros((1, CONV_CT), F32) for _ in range(CONV_K)]
        db = jnp.zeros((1, CONV_CT), F32)
        for c in range(S // CONV_R):
            r0 = c * CONV_R
            ch = _seq_rows(x_ref, r0 - PAD_R, RW + PAD_R, S)
            xs = [ch[PAD_R:]] + [pltpu.roll(ch, s, axis=0)[PAD_R:] for s in range(1, CONV_K)]
            pre = b_ref[...] + xs[0] * wv[3:4]
            for s in range(1, CONV_K):
                pre = pre + xs[s] * wv[3 - s:4 - s]
            sg = _sigmoid(pre)
            dout = jnp.where(ci < nx, _seq_rows(dx_in, r0, RW, S),
                             jnp.where(ci < nx + nb, _seq_rows(db_in, r0, RW, S), _seq_rows(dc_in, r0, RW, S)))
            dpre = dout * (sg * (1.0 + pre * (1.0 - sg)))
            dx = dpre[:CONV_R] * wv[3:4]
            for s in range(1, CONV_K):
                dx = dx + pltpu.roll(dpre, RW - s, axis=0)[:CONV_R] * wv[3 - s:4 - s]
            dx_ref[pl.ds(r0, CONV_R), :] = dx.astype(BF16)
            dcur = dpre[:CONV_R]
            db = db + jnp.sum(dcur, axis=0, keepdims=True)
            for s in range(CONV_K):
                dw[3 - s] = dw[3 - s] + jnp.sum(dcur * xs[s][:CONV_R], axis=0, keepdims=True)
        db_ref[...] += db
        for k in range(CONV_K):
            dw_ref[k:k + 1, :] += dw[k]

    seq = lambda f: pl.BlockSpec((S, CONV_CT), f)
    return pl.pallas_call(
        body,
        out_shape=(jax.ShapeDtypeStruct(dproj.shape, dproj.dtype), jax.ShapeDtypeStruct((CONV_K, C), F32),
                   jax.ShapeDtypeStruct((1, C), F32)),
        grid=(C // CONV_CT, B),
        in_specs=[seq(lambda ci, bi: (bi, XBC_CB + ci)),
                  seq(lambda ci, bi: (bi, jnp.minimum(ci, nx - 1))),
                  seq(lambda ci, bi: (bi, jnp.clip(ci - nx, 0, nb - 1))),
                  seq(lambda ci, bi: (bi, jnp.clip(ci - nx - nb, 0, nb - 1))),
                  pl.BlockSpec((CONV_K, CONV_CT), lambda ci, bi: (0, ci)),
                  pl.BlockSpec((1, CONV_CT), lambda ci, bi: (0, ci)), ANY],
        out_specs=(seq(lambda ci, bi: (bi, XBC_CB + ci)),
                   pl.BlockSpec((CONV_K, CONV_CT), lambda ci, bi: (0, ci)),
                   pl.BlockSpec((1, CONV_CT), lambda ci, bi: (0, ci))),
        input_output_aliases={6: 0},
        name=name, compiler_params=_cp("parallel", "arbitrary"))(proj, dxs, dB, dC, w, b, dproj)


def _tri_sum(tri, x, dims, tri_first, terms=3):
    out, rest = None, x
    for t in range(terms):
        part = rest.astype(BF16)
        if t + 1 < terms:
            rest = rest - part.astype(F32)
        d = _dot(tri, part, dims) if tri_first else _dot(part, tri, dims)
        out = d if out is None else out + d
    return out


def _total(x):
    return jnp.sum(jnp.sum(x, axis=0, keepdims=True), axis=-1, keepdims=True)


def _ssd_common(dtc_ref, dtr_ref, pcol_ref, prow_ref, b_ref, c_ref):
    L = SSD_L
    bias_c, alog_c = pcol_ref[0:1, :], pcol_ref[1:2, :]
    a_c = -jnp.exp(alog_c)
    dt_c = _softplus(dtc_ref[...] + bias_c)
    row = lax.broadcasted_iota(jnp.int32, (L, L), 0)
    col = lax.broadcasted_iota(jnp.int32, (L, L), 1)
    causal = row >= col
    tri = causal.astype(BF16)
    cum_c = _tri_sum(tri, dt_c * a_c, NN, True)
    a_r = -jnp.exp(prow_ref[:, 1:2])
    dt_r = _softplus(dtr_ref[...] + prow_ref[:, 0:1])
    cum_r = _tri_sum(tri, dt_r * a_r, NT, False)
    bb = b_ref[...].astype(BF16)
    cb = c_ref[...].astype(BF16)
    G = _dot(cb, bb, NT)
    return a_c, dt_c, causal, tri, cum_c, cum_r, bb, cb, G


def _ssd_fwd(name, xc, proj, dtc, dtr, pcol, prow, nw, B):
    T = xc.shape[0]
    S = T // B
    nb = S // SSD_L
    L = SSD_L

    def body(xs_ref, b_ref, c_ref, z_ref, dtc_ref, dtr_ref, pcol_ref, prow_ref, nw_ref, y_ref, yn_ref, hs_ref, H, yo_s):
        @pl.when(pl.program_id(2) == 0)
        def _():
            H[...] = jnp.zeros_like(H)

        a_c, dt_c, causal, tri, cum_c, cum_r, bb, cb, G = _ssd_common(dtc_ref, dtr_ref, pcol_ref, prow_ref, b_ref, c_ref)
        dsk = pcol_ref[2:3, :]
        clast = cum_c[L - 1:L, :]
        bf = b_ref[...]
        for h in range(4):
            hs_ref[h] = H[h]
            yo_s[h] = _dot(cb, H[h].astype(BF16), NN)
        for h in range(4):
            sl = slice(HEAD_DIM * h, HEAD_DIM * (h + 1))
            cc = cum_c[:, h:h + 1]
            lm = jnp.exp(jnp.where(causal, cc - cum_r[h:h + 1, :], NEG))
            M = (G * lm).astype(BF16)
            xh = xs_ref[:, sl]
            Xb = (xh * dt_c[:, h:h + 1]).astype(BF16)
            Hh = H[h]
            y = _dot(M, Xb, NN) + jnp.exp(cc) * yo_s[h]
            y_ref[:, sl] = y + dsk[:, h:h + 1] * xh
            cl = clast[:, h:h + 1]
            Bw = (bf * jnp.exp(cl - cc)).astype(BF16)
            H[h] = jnp.exp(cl) * Hh + _dot(Bw, Xb, TN)
        zv = z_ref[...]
        y2 = y_ref[...] * (zv * _sigmoid(zv))
        r = lax.rsqrt(jnp.mean(y2 * y2, axis=-1, keepdims=True) + EPS)
        yn_ref[...] = (y2 * r * nw_ref[...]).astype(BF16)

    rowi = lambda b, g, i: b * nb + i
    grp = pl.BlockSpec((L, GROUP_W), lambda b, g, i: (rowi(b, g, i), g))
    return pl.pallas_call(
        body,
        out_shape=(jax.ShapeDtypeStruct((T, 1024), F32), jax.ShapeDtypeStruct((T, 1024), BF16),
                   jax.ShapeDtypeStruct((B, SSD_GROUPS, nb, 4, SSD_STATE, HEAD_DIM), F32)),
        grid=(B, SSD_GROUPS, nb),
        in_specs=[grp,
                  pl.BlockSpec((L, SSD_STATE), lambda b, g, i: (rowi(b, g, i), 8 + g)),
                  pl.BlockSpec((L, SSD_STATE), lambda b, g, i: (rowi(b, g, i), 12 + g)),
                  grp,
                  pl.BlockSpec((None, L, 4), lambda b, g, i: (g, rowi(b, g, i), 0)),
                  pl.BlockSpec((None, 4, L), lambda b, g, i: (g, 0, rowi(b, g, i))),
                  pl.BlockSpec((None, 3, 4), lambda b, g, i: (g, 0, 0)),
                  pl.BlockSpec((None, 4, 3), lambda b, g, i: (g, 0, 0)),
                  pl.BlockSpec((1, GROUP_W), lambda b, g, i: (0, g))],
        out_specs=(grp, grp,
                   pl.BlockSpec((None, None, None, 4, SSD_STATE, HEAD_DIM), lambda b, g, i: (b, g, i, 0, 0, 0))),
        scratch_shapes=[pltpu.VMEM((4, SSD_STATE, HEAD_DIM), F32), pltpu.VMEM((4, L, HEAD_DIM), F32)], name=name,
        compiler_params=_cp("parallel", "parallel", "arbitrary"))(xc, xc, xc, proj, dtc, dtr, pcol, prow, nw)


def _ssd_bwd(name, dyn, Y, xc, proj, dtc, dtr, pcol, prow, nw, hs, dproj, B):
    T = xc.shape[0]
    S = T // B
    nb = S // SSD_L
    L = SSD_L

    def body(dyn_ref, y_ref, xs_ref, b_ref, c_ref, z_ref, dtc_ref, dtr_ref, pcol_ref, prow_ref, nw_ref, hs_ref, buf_ref,
             dxs_ref, db_ref, dc_ref, dz_ref, ddt_ref, dpar_ref, dnw_ref, dH, dm_s, dxo_s, ea_s, ex_s):
        @pl.when(pl.program_id(2) == 0)
        def _():
            dH[...] = jnp.zeros_like(dH)
            dpar_ref[...] = jnp.zeros_like(dpar_ref)
            dnw_ref[...] = jnp.zeros_like(dnw_ref)

        a_c, dt_c, causal, tri, cum_c, cum_r, bb, cb, G = _ssd_common(dtc_ref, dtr_ref, pcol_ref, prow_ref, b_ref, c_ref)
        dsk = pcol_ref[2:3, :]
        clast = cum_c[L - 1:L, :]
        bf = b_ref[...]
        cf = c_ref[...]
        Yv = y_ref[...]
        zv = z_ref[...]
        sz = _sigmoid(zv)
        silu = zv * sz
        y2 = Yv * silu
        r = lax.rsqrt(jnp.mean(y2 * y2, axis=-1, keepdims=True) + EPS)
        yhat = y2 * r
        dyv = dyn_ref[...]
        dnw_ref[...] += jnp.sum(dyv * yhat, axis=0, keepdims=True)
        dyhat = dyv * nw_ref[...]
        dy2 = r * (dyhat - yhat * jnp.mean(dyhat * yhat, axis=-1, keepdims=True))
        dY = dy2 * silu
        dz_ref[...] = (dy2 * Yv * (sz * (1.0 + zv * (1.0 - sz)))).astype(BF16)

        lane4 = lax.broadcasted_iota(jnp.int32, (1, 4), 1)
        dG = jnp.zeros((L, L), F32)
        dBs = jnp.zeros((L, SSD_STATE), F32)
        dCs = jnp.zeros((L, SSD_STATE), F32)
        ddsk = jnp.zeros((1, 4), F32)
        dcl = jnp.zeros((1, 4), F32)
        for h in range(4):
            sl = slice(HEAD_DIM * h, HEAD_DIM * (h + 1))
            xb = (xs_ref[:, sl] * dt_c[:, h:h + 1]).astype(BF16)
            dm_s[h] = _dot(dY[:, sl].astype(BF16), xb, NT)
            dxo_s[h] = _dot(bb, dH[h].astype(BF16), NN)
        for h in range(4):
            sl = slice(HEAD_DIM * h, HEAD_DIM * (h + 1))
            onehot = (lane4 == h).astype(F32)
            cc = cum_c[:, h:h + 1]
            cl = clast[:, h:h + 1]
            lm = jnp.exp(jnp.where(causal, cc - cum_r[h:h + 1, :], NEG))
            M = (G * lm).astype(BF16)
            xh = xs_ref[:, sl]
            dth = dt_c[:, h:h + 1]
            X = xh * dth
            Xb = X.astype(BF16)
            dYh = dY[:, sl]
            dYb = dYh.astype(BF16)
            Hb = hs_ref[h].astype(BF16)
            dHh = dH[h]
            dHb = dHh.astype(BF16)
            alpha = jnp.exp(cc)
            beta = jnp.exp(cl - cc)
            dXoff = beta * dxo_s[h]
            dX = _dot(M, dYb, TN) + dXoff
            dG = dG + dm_s[h] * lm
            dCs = dCs + _dot((alpha * dYh).astype(BF16), Hb, NT)
            dBs = dBs + _dot((beta * X).astype(BF16), dHb, NT)
            ypre = Yv[:, sl] - dsk[:, h:h + 1] * xh
            ea_s[:, sl] = dYb.astype(F32) * ypre - Xb.astype(F32) * dX
            ex_s[:, sl] = dX * xh
            dcl_h = (_total(dHh * (jnp.exp(cl) * hs_ref[h])) + _total(Xb.astype(F32) * dXoff))
            dcl = dcl + dcl_h * onehot
            ddsk = ddsk + _total(dYh * xh) * onehot
            dxs_ref[:, sl] = dsk[:, h:h + 1] * dYh + dX * dth
            dH[h] = jnp.exp(cl) * dHh + _dot((alpha * cf).astype(BF16), dYb, TN)
        dGb = dG.astype(BF16)
        dc_ref[...] = _dot(dGb, bb, NN) + dCs
        db_ref[...] = _dot(dGb, cb, TN) + dBs
        feat = lax.broadcasted_iota(jnp.int32, (GROUP_W, 4), 0)
        head = lax.broadcasted_iota(jnp.int32, (GROUP_W, 4), 1) * HEAD_DIM
        sel = ((feat >= head) & (feat < head + HEAD_DIM)).astype(BF16)
        dA = _tri_sum(sel, ea_s[...], NN, False)
        ddtx = _tri_sum(sel, ex_s[...], NN, False)
        last = lax.broadcasted_iota(jnp.int32, (L, 1), 0) == L - 1
        dA = dA + jnp.where(last, dcl, 0.0)
        dadt = _tri_sum(tri, dA, TN, True)
        ddt = dadt * a_c + ddtx
        d_a = jnp.sum(dadt * dt_c, axis=0, keepdims=True)
        ddraw = ddt * _sigmoid(dtc_ref[...] + pcol_ref[0:1, :])
        ddt_ref[...] = ddraw
        dpar_ref[0:1, :] += jnp.sum(ddraw, axis=0, keepdims=True)
        dpar_ref[1:2, :] += d_a * a_c
        dpar_ref[2:3, :] += ddsk

    rowi = lambda b, g, i: b * nb + (nb - 1 - i)
    grp = pl.BlockSpec((L, GROUP_W), lambda b, g, i: (rowi(b, g, i), g))
    st = pl.BlockSpec((L, SSD_STATE), lambda b, g, i: (rowi(b, g, i), g))
    f = jax.ShapeDtypeStruct
    return pl.pallas_call(
        body,
        out_shape=(f((T, 1024), F32), f((T, 512), F32), f((T, 512), F32), f(dproj.shape, dproj.dtype),
                   f((SSD_GROUPS, T, 4), F32), f((B, SSD_GROUPS, 3, 4), F32), f((B, 1, 1024), F32)),
        grid=(B, SSD_GROUPS, nb),
        in_specs=[grp, grp, grp,
                  pl.BlockSpec((L, SSD_STATE), lambda b, g, i: (rowi(b, g, i), 8 + g)),
                  pl.BlockSpec((L, SSD_STATE), lambda b, g, i: (rowi(b, g, i), 12 + g)),
                  grp,
                  pl.BlockSpec((None, L, 4), lambda b, g, i: (g, rowi(b, g, i), 0)),
                  pl.BlockSpec((None, 4, L), lambda b, g, i: (g, 0, rowi(b, g, i))),
                  pl.BlockSpec((None, 3, 4), lambda b, g, i: (g, 0, 0)),
                  pl.BlockSpec((None, 4, 3), lambda b, g, i: (g, 0, 0)),
                  pl.BlockSpec((1, GROUP_W), lambda b, g, i: (0, g)),
                  pl.BlockSpec((None, None, None, 4, SSD_STATE, HEAD_DIM), lambda b, g, i: (b, g, nb - 1 - i, 0, 0, 0)),
                  ANY],
        out_specs=(grp, st, st, grp,
                   pl.BlockSpec((None, L, 4), lambda b, g, i: (g, rowi(b, g, i), 0)),
                   pl.BlockSpec((None, None, 3, 4), lambda b, g, i: (b, g, 0, 0)),
                   pl.BlockSpec((None, 1, GROUP_W), lambda b, g, i: (b, 0, g))),
        input_output_aliases={12: 3},
        scratch_shapes=[pltpu.VMEM((4, SSD_STATE, HEAD_DIM), F32), pltpu.VMEM((4, L, L), F32),
                        pltpu.VMEM((4, L, HEAD_DIM), F32), pltpu.VMEM((L, GROUP_W), F32),
                        pltpu.VMEM((L, GROUP_W), F32)], name=name,
        compiler_params=_cp("parallel", "parallel", "arbitrary"))(
            dyn, Y, xc, xc, xc, proj, dtc, dtr, pcol, prow, nw, hs, dproj)


def _head_sel():
    sel = (np.arange(1024)[:, None] // HEAD_DIM == np.arange(ATT_HEADS)[None, :]).astype(np.float32)
    return jnp.asarray(sel, BF16), jnp.asarray(sel.T, BF16)


def _head_rms(xv, sel, selT):
    ms = _tri_sum(sel, xv * xv, NN, False, 1) * (1.0 / HEAD_DIM)
    return _tri_sum(selT, lax.rsqrt(ms + EPS), NN, False, 2)


def _headnorm_fwd(name, proj, col_block, w):
    T = proj.shape[0]
    sel, selT = _head_sel()

    def body(x_ref, w_ref, sel_ref, selT_ref, o_ref):
        xv = x_ref[...]
        o_ref[...] = (xv * _head_rms(xv, sel_ref[...], selT_ref[...]) * w_ref[...]).astype(BF16)

    full = lambda shp: pl.BlockSpec(shp, lambda i: (0, 0))
    return pl.pallas_call(
        body, out_shape=jax.ShapeDtypeStruct((T, 1024), BF16), grid=(T // ROW_T,),
        in_specs=[pl.BlockSpec((ROW_T, 1024), lambda i: (i, col_block)), full((1, 1024)), full((1024, ATT_HEADS)),
                  full((ATT_HEADS, 1024))],
        out_specs=pl.BlockSpec((ROW_T, 1024), lambda i: (i, 0)), name=name, compiler_params=_cp("parallel"))(
            proj, jnp.tile(w, (1, ATT_HEADS)), sel, selT)


def _headnorm_bwd(name, dn, proj, col_block, w, dproj):
    T = proj.shape[0]
    sel, selT = _head_sel()

    def body(dn_ref, x_ref, w_ref, sel_ref, selT_ref, buf_ref, dx_ref, dw_ref):
        @pl.when(pl.program_id(0) == 0)
        def _():
            dw_ref[...] = jnp.zeros_like(dw_ref)

        xv = x_ref[...]
        sl, slT = sel_ref[...], selT_ref[...]
        rb = _head_rms(xv, sl, slT)
        xhat = xv * rb
        dnv = dn_ref[...]
        dxhat = dnv * w_ref[...]
        mean = _tri_sum(slT, _tri_sum(sl, dxhat * xhat, NN, False, 2) * (1.0 / HEAD_DIM), NN, False, 2)
        dx_ref[...] = (rb * (dxhat - xhat * mean)).astype(BF16)
        dw_ref[...] += jnp.sum(dnv * xhat, axis=0, keepdims=True)

    here = pl.BlockSpec((ROW_T, 1024), lambda i: (i, col_block))
    full = lambda shp: pl.BlockSpec(shp, lambda i: (0, 0))
    dx, dw = pl.pallas_call(
        body, out_shape=(jax.ShapeDtypeStruct(dproj.shape, dproj.dtype), jax.ShapeDtypeStruct((1, 1024), F32)),
        grid=(T // ROW_T,),
        in_specs=[pl.BlockSpec((ROW_T, 1024), lambda i: (i, 0)), here, full((1, 1024)), full((1024, ATT_HEADS)),
                  full((ATT_HEADS, 1024)), ANY],
        out_specs=(here, full((1, 1024))), input_output_aliases={5: 0},
        name=name, compiler_params=_cp("arbitrary"))(dn, proj, jnp.tile(w, (1, ATT_HEADS)), sel, selT, dproj)
    return dx, jnp.sum(dw.reshape(ATT_HEADS, HEAD_DIM), axis=0, keepdims=True)


def _att_bias(nq):
    j = np.arange(ATT_B)[:, None]
    i = np.arange(ATT_B)[None, :]
    out = np.empty((nq, ATT_B, ATT_B), np.float32)
    for dblk in range(nq):
        dl = ATT_B * dblk + i - j
        cnt = ((dl >= 0) & (dl <= 128)).astype(np.float32)
        cnt += ((dl >= 0) & (dl % 4 == 0) & (dl <= 512))
        cnt += ((dl >= 0) & (dl % 16 == 0) & (dl <= 2048))
        out[dblk] = np.where(cnt > 0, np.log(np.maximum(cnt, 1.0)), NEG)
    return jnp.asarray(out)


def _row_pair(nq):
    def f(r, c):
        first = c <= r
        return jnp.where(first, r, nq - 1 - r), jnp.where(first, c, c - (r + 1))
    return f


def _col_pair(nq):
    def f(r, c):
        first = c < nq - r
        kj = jnp.where(first, r, nq - 1 - r)
        return jnp.where(first, r + c, nq - 1 - r + (c - (nq - r))), kj
    return f


ATT_SCALE = 1.0 / math.sqrt(HEAD_DIM)
ATT_HS = 8
ATT_W = ATT_HS * HEAD_DIM


def _att_maps(nq, qk):
    return dict(
        q_tok=lambda b, g, r, c: (b * nq + qk(r, c)[0], g),
        k_tok=lambda b, g, r, c: (b * nq + qk(r, c)[1], g),
        v_tok=lambda b, g, r, c: (b * nq + qk(r, c)[1], COL_V // ATT_W + g),
        q_feat=lambda b, g, r, c: (g, b * nq + qk(r, c)[0]),
        k_feat=lambda b, g, r, c: (g, b * nq + qk(r, c)[1]),
        bias=lambda b, g, r, c: (qk(r, c)[0] - qk(r, c)[1], 0, 0),
        lse=lambda b, g, r, c: (g, 0, b * nq + qk(r, c)[0]),
        do_tok=lambda b, g, r, c: (b * nq + qk(r, c)[0], 1024 // ATT_W + g))


def _att_fwd(name, kn, qT, vT, bias, B):
    T = kn.shape[0]
    nq = (T // B) // ATT_B
    qk = _row_pair(nq)
    mp = _att_maps(nq, qk)

    def body(k_ref, qT_ref, vT_ref, bias_ref, oT_ref, lse_ref, m_s, l_s, acc_s, s_s):
        qi, kj = qk(pl.program_id(2), pl.program_id(3))

        @pl.when(kj == 0)
        def _():
            m_s[...] = jnp.full_like(m_s, NEG)
            l_s[...] = jnp.zeros_like(l_s)
            acc_s[...] = jnp.zeros_like(acc_s)

        bv = bias_ref[...]
        for h in range(ATT_HS):
            rs = slice(HEAD_DIM * h, HEAD_DIM * (h + 1))
            s_s[h] = _dot(k_ref[:, rs], qT_ref[rs, :], NN)
        for h in range(ATT_HS):
            rs = slice(HEAD_DIM * h, HEAD_DIM * (h + 1))
            s = s_s[h] + bv
            m_prev = m_s[h:h + 1, :]
            m_new = jnp.maximum(m_prev, jnp.max(s, axis=0, keepdims=True))
            alpha = jnp.exp(m_prev - m_new)
            p = jnp.exp(s - m_new)
            l_s[h:h + 1, :] = alpha * l_s[h:h + 1, :] + jnp.sum(p, axis=0, keepdims=True)
            acc_s[rs, :] = alpha * acc_s[rs, :] + _dot(vT_ref[rs, :], p.astype(BF16), NN)
            m_s[h:h + 1, :] = m_new

        @pl.when(kj == qi)
        def _():
            for h in range(ATT_HS):
                rs = slice(HEAD_DIM * h, HEAD_DIM * (h + 1))
                oT_ref[rs, :] = (acc_s[rs, :] / l_s[h:h + 1, :]).astype(BF16)
            lse_ref[...] = m_s[...] + jnp.log(l_s[...])

    tok = (ATT_B, ATT_W)
    feat = (ATT_W, ATT_B)
    return pl.pallas_call(
        body,
        out_shape=(jax.ShapeDtypeStruct((1024, T), BF16), jax.ShapeDtypeStruct((ATT_HEADS // ATT_HS, ATT_HS, T), F32)),
        grid=(B, ATT_HEADS // ATT_HS, nq // 2, nq + 1),
        in_specs=[pl.BlockSpec(tok, mp["k_tok"]), pl.BlockSpec(feat, mp["q_feat"]), pl.BlockSpec(feat, mp["k_feat"]),
                  pl.BlockSpec((None, ATT_B, ATT_B), mp["bias"])],
        out_specs=(pl.BlockSpec(feat, mp["q_feat"]), pl.BlockSpec((None, ATT_HS, ATT_B), mp["lse"])),
        scratch_shapes=[pltpu.VMEM((ATT_HS, ATT_B), F32), pltpu.VMEM((ATT_HS, ATT_B), F32),
                        pltpu.VMEM((ATT_W, ATT_B), F32), pltpu.VMEM((ATT_HS, ATT_B, ATT_B), F32)],
        name=name, compiler_params=_cp("parallel", "parallel", "arbitrary", "arbitrary"))(kn, qT, vT, bias)


def _att_scores(k_ref, qT_ref, v_ref, doT_ref, s_s, dp_s):
    for h in range(ATT_HS):
        rs = slice(HEAD_DIM * h, HEAD_DIM * (h + 1))
        s_s[h] = _dot(k_ref[:, rs], qT_ref[rs, :], NN)
        dp_s[h] = _dot(v_ref[:, rs].astype(BF16), doT_ref[rs, :].astype(BF16), NN)


def _att_p_ds(s_s, dp_s, doT_ref, oT_ref, lse_ref, bv, h):
    rs = slice(HEAD_DIM * h, HEAD_DIM * (h + 1))
    delta = jnp.sum(doT_ref[rs, :] * oT_ref[rs, :].astype(F32), axis=0, keepdims=True)
    p = jnp.exp(s_s[h] + bv - lse_ref[h:h + 1, :])
    return p, p * (dp_s[h] - delta)


def _att_bwd(name, kn, qT, proj, qn, knT, bias, doT, oT, lse, dyn, dproj, B):
    T = kn.shape[0]
    S = T // B
    nq = S // ATT_B
    qk = _col_pair(nq)
    mp = _att_maps(nq, qk)

    def body(k_ref, qT_ref, v_ref, q_ref, kT_ref, bias_ref, doT_ref, oT_ref, lse_ref, do_ref, buf_ref,
             dqT_ref, dk_ref, dv_ref, dk_s, dv_s, dq_s, s_s, dp_s):
        r, c = pl.program_id(2), pl.program_id(3)
        qi, kj = qk(r, c)

        @pl.when((r == 0) & (c == 0))
        def _():
            dq_s[...] = jnp.zeros_like(dq_s)

        @pl.when(qi == kj)
        def _():
            dk_s[...] = jnp.zeros_like(dk_s)
            dv_s[...] = jnp.zeros_like(dv_s)

        bv = bias_ref[...]
        _att_scores(k_ref, qT_ref, v_ref, doT_ref, s_s, dp_s)
        dq_blk = dq_s.at[qi]
        for h in range(ATT_HS):
            rs = slice(HEAD_DIM * h, HEAD_DIM * (h + 1))
            p, ds = _att_p_ds(s_s, dp_s, doT_ref, oT_ref, lse_ref, bv, h)
            dsb = ds.astype(BF16)
            dv_s[h] += _dot(p.astype(BF16), do_ref[:, rs].astype(BF16), NN)
            dk_s[h] += _dot(dsb, q_ref[:, rs], NN)
            dq_blk[rs, :] += _dot(kT_ref[rs, :], dsb, NN)

        @pl.when(qi == nq - 1)
        def _():
            for h in range(ATT_HS):
                rs = slice(HEAD_DIM * h, HEAD_DIM * (h + 1))
                dk_ref[:, rs] = dk_s[h] * ATT_SCALE
                dv_ref[:, rs] = dv_s[h].astype(BF16)

        @pl.when((r == nq // 2 - 1) & (c == nq))
        def _():
            for q in range(nq):
                dqT_ref[:, ATT_B * q:ATT_B * (q + 1)] = dq_s[q] * ATT_SCALE

    tok = (ATT_B, ATT_W)
    feat = (ATT_W, ATT_B)
    v_cb = COL_V // ATT_W
    return pl.pallas_call(
        body,
        out_shape=(jax.ShapeDtypeStruct((1024, T), F32), jax.ShapeDtypeStruct((T, 1024), F32),
                   jax.ShapeDtypeStruct(dproj.shape, dproj.dtype)),
        grid=(B, ATT_HEADS // ATT_HS, nq // 2, nq + 1),
        in_specs=[pl.BlockSpec(tok, mp["k_tok"]), pl.BlockSpec(feat, mp["q_feat"]), pl.BlockSpec(tok, mp["v_tok"]),
                  pl.BlockSpec(tok, mp["q_tok"]), pl.BlockSpec(feat, mp["k_feat"]),
                  pl.BlockSpec((None, ATT_B, ATT_B), mp["bias"]),
                  pl.BlockSpec(feat, mp["q_feat"]), pl.BlockSpec(feat, mp["q_feat"]),
                  pl.BlockSpec((None, ATT_HS, ATT_B), mp["lse"]), pl.BlockSpec(tok, mp["do_tok"]), ANY],
        out_specs=(pl.BlockSpec((ATT_W, S), lambda b, g, r, c: (g, b)),
                   pl.BlockSpec(tok, mp["k_tok"]),
                   pl.BlockSpec(tok, lambda b, g, r, c: (b * nq + qk(r, c)[1], v_cb + g))),
        input_output_aliases={10: 2},
        scratch_shapes=[pltpu.VMEM((ATT_HS, ATT_B, HEAD_DIM), F32), pltpu.VMEM((ATT_HS, ATT_B, HEAD_DIM), F32),
                        pltpu.VMEM((nq, ATT_W, ATT_B), F32),
                        pltpu.VMEM((ATT_HS, ATT_B, ATT_B), F32), pltpu.VMEM((ATT_HS, ATT_B, ATT_B), F32)],
        name=name, compiler_params=_cp("parallel", "parallel", "arbitrary", "arbitrary"))(
            kn, qT, proj, qn, knT, bias, doT, oT, lse, dyn, dproj)


def _group_cols(v):
    return v.reshape(SSD_GROUPS, 4)


def _ssd_params(p):
    rows = jnp.stack([_group_cols(p["dt_bias"]), _group_cols(p["a_log"]), _group_cols(p["d_skip"])], axis=1)
    return rows, jnp.swapaxes(rows, 1, 2)


def _dymix(name, dx, wout):
    T = dx.shape[0]

    def body(dx_ref, w_ref, o_ref):
        dxb = dx_ref[...].astype(BF16)
        for n in range(N_SHARD):
            o_ref[:, MIX_SH * n:MIX_SH * (n + 1)] = _dot(dxb, w_ref[n], NT)

    return pl.pallas_call(
        body, out_shape=jax.ShapeDtypeStruct((T, MIX_W), F32), grid=(T // ROW_T,),
        in_specs=[pl.BlockSpec((ROW_T, D_MODEL), lambda i: (i, 0)),
                  pl.BlockSpec((N_SHARD, MIX_SH, D_MODEL), lambda i: (0, 0, 0))],
        out_specs=pl.BlockSpec((ROW_T, MIX_W), lambda i: (i, 0)), name=name, compiler_params=_cp("parallel"))(dx, wout)


def _mixer_fwd(tag, x1, h2, p, weights, bias, B):
    T = x1.shape[0]
    nt = T // ROW_T
    wi = weights("win", h2)
    win, cw = wi["win"], wi["cw"]
    tm = min(GU_T, T)
    proj = _mm(tag + "_proj",
               [(h2, pl.BlockSpec((tm, D_MODEL), lambda j, i, k: (i, 0)),
                 win, pl.BlockSpec((D_MODEL, PROJ_TN), lambda j, i, k: (0, j)))],
               jax.ShapeDtypeStruct((T, IN_PAD), F32), pl.BlockSpec((tm, PROJ_TN), lambda j, i, k: (i, j)),
               (IN_PAD // PROJ_TN, T // tm, 1), NN, (tm, PROJ_TN))
    xc = _conv_fwd(tag + "_conv", proj, cw, p["conv_b"][None], B)
    dtraw = proj[:, COL_DT:COL_DT + SSD_HEADS].reshape(T, SSD_GROUPS, 4)
    dtc = jnp.transpose(dtraw, (1, 0, 2))
    dtr = jnp.transpose(dtraw, (1, 2, 0))
    pcol, prow = _ssd_params(p)
    Y, y_ssd, hs = _ssd_fwd(tag + "_ssd", xc, proj, dtc, dtr, pcol, prow, p["ssd_norm"][None], B)
    qn = _headnorm_fwd(tag + "_qn", proj, COL_Q // 1024, p["q_norm"][None])
    kn = _headnorm_fwd(tag + "_kn", proj, COL_K // 1024, p["k_norm"][None])
    qT = (qn * ATT_SCALE).T
    oT, lse = _att_fwd(tag + "_att", kn, qT, proj[:, COL_V:COL_V + 1024].T.astype(BF16), bias, B)
    ymix = jnp.concatenate([y_ssd, oT.T], axis=1)
    rest = weights("rest", ymix)
    o = _with_norm(T, p["ffn2_norm"][None])
    x2, h3 = _mm(tag + "_out",
                 [(ymix, pl.BlockSpec((ROW_T, MIX_SH), lambda i, n, k, j=j: (i, j)),
                   rest["wout"], pl.BlockSpec((None, MIX_SH, D_MODEL), lambda i, n, k, j=j: (j, 0, 0)))
                  for j in range(N_SHARD)],
                 o.pop("out_shape"), o.pop("out_spec"), (nt, 1, 1), NN, (ROW_T, D_MODEL), res=(x1, _row3()), **o)
    saved = dict(x1=x1, h2=h2, proj=proj, xc=xc, dtc=dtc, dtr=dtr, Y=Y, hs=hs,
                 qn=qn, kn=kn, qT=qT, oT=oT, lse=lse, ymix=ymix, win=win, cw=cw, wout=rest["wout"])
    return x2, h3, saved


def _mixer_bwd(tag, dx2, dx2_b, sv, p, bias, B):
    T = dx2.shape[0]
    nt = T // ROW_T
    sg = {}
    dymix = _dymix(tag + "_dymix", dx2_b, sv["wout"])
    tkw = min(TK_W, T)
    gwout = _mm(tag + "_dwout",
                [(sv["ymix"], pl.BlockSpec((tkw, MIX_SH), lambda m, n, k: (k, m)),
                  dx2_b, pl.BlockSpec((tkw, D_MODEL), lambda m, n, k: (k, 0)))],
                jax.ShapeDtypeStruct((N_SHARD, MIX_SH, D_MODEL), BF16),
                pl.BlockSpec((None, MIX_SH, D_MODEL), lambda m, n, k: (m, 0, 0)),
                (N_SHARD, 1, T // tkw), TN, (MIX_SH, D_MODEL))
    proj = sv["proj"]
    doT = dymix[:, 1024:].T
    dproj = lax.empty((T, IN_PAD), BF16)
    dqT, dkn, dproj = _att_bwd(tag + "_attb", sv["kn"], sv["qT"], proj, sv["qn"], sv["kn"].T, bias, doT, sv["oT"],
                               sv["lse"], dymix, dproj, B)
    dproj, sg["q_norm"] = _headnorm_bwd(tag + "_qnb", dqT.T, proj, COL_Q // 1024, p["q_norm"][None], dproj)
    dproj, sg["k_norm"] = _headnorm_bwd(tag + "_knb", dkn, proj, COL_K // 1024, p["k_norm"][None], dproj)
    pcol, prow = _ssd_params(p)
    dxs, dB, dC, dproj, ddt, dpar, dnw = _ssd_bwd(tag + "_ssdb", dymix, sv["Y"], sv["xc"], proj, sv["dtc"], sv["dtr"],
                                                  pcol, prow, p["ssd_norm"][None], sv["hs"], dproj, B)
    dpar = jnp.sum(dpar, axis=0)
    sg["dt_bias"] = dpar[:, 0, :].reshape(SSD_HEADS)
    sg["a_log"] = dpar[:, 1, :].reshape(SSD_HEADS)
    sg["d_skip"] = dpar[:, 2, :].reshape(SSD_HEADS)
    sg["ssd_norm"] = jnp.sum(dnw, axis=0)
    dproj, sg["conv_w"], sg["conv_b"] = _conv_bwd(tag + "_convb", proj, dxs, dB, dC, sv["cw"], p["conv_b"][None],
                                                  dproj, B)
    ddt16 = jnp.transpose(ddt, (1, 0, 2)).reshape(T, SSD_HEADS)
    dproj = lax.dynamic_update_slice(dproj, jnp.pad(ddt16, ((0, 0), (0, IN_PAD - COL_DT - SSD_HEADS))).astype(BF16),
                                     (0, COL_DT))
    win = sv["win"]
    gwin = _mm(tag + "_dwin",
               [(sv["h2"], pl.BlockSpec((tkw, D_MODEL), lambda n, m, k: (k, 0)),
                 dproj, pl.BlockSpec((tkw, PROJ_TN), lambda n, m, k: (k, n)))],
               jax.ShapeDtypeStruct((D_MODEL, IN_PAD), BF16), pl.BlockSpec((D_MODEL, PROJ_TN), lambda n, m, k: (0, n)),
               (IN_PAD // PROJ_TN, 1, T // tkw), TN, (D_MODEL, PROJ_TN))
    dx1, sg["mix_norm"], dx1_b = _mm(
        tag + "_dh2",
        [(dproj, pl.BlockSpec((ROW_T, PROJ_TN), lambda i, n, k, j=j: (i, j)),
          win, pl.BlockSpec((D_MODEL, PROJ_TN), lambda i, n, k, j=j: (0, j))) for j in range(IN_PAD // PROJ_TN)],
        *_rmsb_out(T), (nt, 1, 1), NT, (ROW_T, D_MODEL), post="rmsb",
        post_in=[(sv["x1"], _row3()), (p["mix_norm"][None], _vec3()), (dx2, _row3())])
    return dx1, dx1_b, sg, gwout, gwin


DT_LO =IN_SH * 2 - COL_Q


def _win_from_shards(sh):
    main = IN_SH - DT_LO
    return jnp.concatenate([sh[0], sh[1][:, :main], sh[2][:, SSD_HEADS - DT_LO:], sh[3], sh[1][:, main:],
                            sh[2][:, :SSD_HEADS - DT_LO], jnp.zeros((sh.shape[1], IN_PAD - IN_PROJ), sh.dtype)], axis=1)


def _win_to_shards(g):
    main = IN_SH - DT_LO
    a, b = IN_SH + main, IN_SH + 2 * main
    return jnp.stack([g[:, :IN_SH],
                      jnp.concatenate([g[:, IN_SH:a], g[:, COL_DT:COL_DT + DT_LO]], axis=1),
                      jnp.concatenate([g[:, COL_DT + DT_LO:COL_DT + SSD_HEADS], g[:, a:b]], axis=1),
                      g[:, b:COL_DT]])


def _local_step(x, target, small, weights, scatter, B):
    T = x.shape[0]
    bias = _att_bias((T // B) // ATT_B)
    saved = []
    xl = x
    hl = _rms_fwd("l0f1_rms", x, small["ffn1_norm"][0][None])
    for l in range(DEPTH):
        tag = "l%d" % l
        p = {k: v[l] for k, v in small.items()}
        w1 = weights(l, "ffn1", hl)
        (x1, h2), ffn1, d1 = _ffn_fwd(tag + "f1", xl, hl, w1["g1"], w1["u1"],
                                      lambda after, l=l: weights(l, "ffn1d", after)["d1"], p["mix_norm"][None])
        x2, h3, sv = _mixer_fwd(tag, x1, h2, p, functools.partial(weights, l), bias, B)
        w2 = weights(l, "rest", x2)
        nxt = small["ffn1_norm"][l + 1][None] if l + 1 < DEPTH else None
        xo, ffn2, _ = _ffn_fwd(tag + "f2", x2, h3, w2["g2"], w2["u2"], w2["d2"], nxt)
        xl, hl = xo if nxt is not None else (xo, None)
        saved.append((ffn1, sv, ffn2, dict(g1=w1["g1"], u1=w1["u1"], d1=d1), w2))
    d, db, lsum = _loss_grad("loss", xl, target)
    sgrads = [None] * DEPTH
    for l in reversed(range(DEPTH)):
        tag = "l%db" % l
        p = {k: v[l] for k, v in small.items()}
        ffn1, sv, ffn2, w1, w2 = saved[l]
        sg = {}
        d, sg["ffn2_norm"], db = _ffn_bwd(tag + "f2", d, db, ffn2, p["ffn2_norm"][None], w2["g2"], w2["u2"], w2["d2"],
                                          lambda gg, gu, gd, c, l=l: scatter(l, "ffn2", dict(g2=gg, u2=gu, d2=gd), c))
        d, db, sgm, gwout, gwin = _mixer_bwd(tag, d, db, sv, p, bias, B)
        sg.update(sgm)
        db = scatter(l, "mixer", dict(wout=gwout, win=gwin), db)
        d, sg["ffn1_norm"], db = _ffn_bwd(tag + "f1", d, db, ffn1, p["ffn1_norm"][None], w1["g1"], w1["u1"], w1["d1"],
                                          lambda gg, gu, gd, c, l=l: scatter(l, "ffn1", dict(g1=gg, u1=gu, d1=gd), c))
        sgrads[l] = sg
    return lsum, d, sgrads


MESH = pl.DeviceIdType.MESH
ANY = pl.BlockSpec(memory_space=pl.ANY)


def _place():
    return lax.axis_index("x"), lax.axis_index("y"), lax.axis_index("c")


def _other_chips(x, y):
    return [(1 - x, y), (x, 1 - y), (1 - x, 1 - y)]


HBM = pl.BlockSpec(memory_space=pltpu.HBM)
SEM = pl.BlockSpec(memory_space=pltpu.SEMAPHORE)
EFFECT = pltpu.SideEffectType.DATAFLOW_SIDE_EFFECTING


def _hbm(a):
    return pltpu.with_memory_space_constraint(a, pltpu.HBM)


def _my_half(ref, c):
    hr = ref.shape[0] // 2
    return ref.at[pl.ds(pl.multiple_of(c * hr, 16), hr)]


def _exchange(gather, layer, halves, src, land, send, recv, n, act):
    x, y, c = _place()
    for k, (px, py) in enumerate(_other_chips(x, y)):
        for a in range(n):
            if gather:
                s_out, d_out, d_in = src[a].at[layer], land[a].at[2 * x + y], land[a].at[2 * px + py]
                if halves is not None and halves[a]:
                    s_out, d_out, d_in = _my_half(s_out, c), _my_half(d_out, c), _my_half(d_in, c)
            else:
                s_out, d_out, d_in = src[a].at[2 * px + py], land[a].at[k], land[a].at[k]
            act(pltpu.make_async_remote_copy(
                src_ref=s_out, dst_ref=d_out if act is _start else d_in, send_sem=send.at[k * n + a],
                recv_sem=recv.at[k * n + a], device_id=(px, py, c), device_id_type=MESH))


def _start(cp):
    cp.start()


def _finish(cp):
    cp.wait_send()
    cp.wait_recv()


def _exchange_start(name, gather, layer, srcs, carry, halves=None):
    n = len(srcs)
    lands = [lax.empty(((N_SHARD,) + s.shape[1:]) if gather else ((3,) + s.shape[1:]), s.dtype) for s in srcs]

    def body(*refs):
        _exchange(gather, layer, halves, refs[:n], refs[n:2 * n], refs[2 * n + 1], refs[2 * n + 2], n, _start)

    srcs = [_hbm(a) for a in srcs]
    thru = [_hbm(a) for a in lands + [carry]]
    out = pl.pallas_call(
        body, name=name,
        out_shape=(pltpu.SemaphoreType.DMA((3 * n,)), pltpu.SemaphoreType.DMA((3 * n,)),
                   *[pltpu.HBM(a.shape, a.dtype) for a in thru]),
        in_specs=[HBM] * (2 * n + 1), out_specs=(SEM, SEM, *[HBM] * (n + 1)),
        input_output_aliases={n + i: 2 + i for i in range(n + 1)},
        compiler_params=pltpu.CompilerParams(has_side_effects=EFFECT))(*srcs, *thru)
    return dict(gather=gather, layer=layer, halves=halves, send=out[0], recv=out[1], srcs=srcs,
                lands=list(out[2:2 + n])), out[-1]


def _exchange_wait(name, ex, after):
    n = len(ex["srcs"])

    def body(*refs):
        _exchange(ex["gather"], ex["layer"], ex["halves"], refs[:n], refs[n:2 * n], refs[2 * n], refs[2 * n + 1], n,
                  _finish)

    out = pl.pallas_call(
        body, name=name, out_shape=[pltpu.HBM(a.shape, a.dtype) for a in ex["lands"]],
        in_specs=[HBM] * (2 * n) + [SEM, SEM, ANY], out_specs=[HBM] * n,
        input_output_aliases={n + i: i for i in range(n)},
        compiler_params=pltpu.CompilerParams(has_side_effects=EFFECT))(
            *ex["srcs"], *ex["lands"], ex["send"], ex["recv"], after)
    return list(out)


def _sibling_fill(name, lands):
    n = len(lands)

    def body(*refs):
        land = refs[:n]
        send, recv = refs[2 * n], refs[2 * n + 1]
        x, y, c = _place()
        for k, (px, py) in enumerate(_other_chips(x, y)):
            for a in range(n):
                slot = land[a].at[2 * px + py]
                pltpu.make_async_remote_copy(src_ref=_my_half(slot, c), dst_ref=_my_half(slot, c),
                                             send_sem=send.at[k * n + a], recv_sem=recv.at[k * n + a],
                                             device_id=(x, y, 1 - c), device_id_type=MESH).start()
        for k, (px, py) in enumerate(_other_chips(x, y)):
            for a in range(n):
                slot = land[a].at[2 * px + py]
                cp = pltpu.make_async_remote_copy(src_ref=_my_half(slot, c), dst_ref=_my_half(slot, 1 - c),
                                                  send_sem=send.at[k * n + a], recv_sem=recv.at[k * n + a],
                                                  device_id=(x, y, 1 - c), device_id_type=MESH)
                cp.wait_recv()
                cp.wait_send()

    return pl.pallas_call(
        body, out_shape=[jax.ShapeDtypeStruct(a.shape, a.dtype) for a in lands],
        in_specs=[ANY] * n, out_specs=[ANY] * n, input_output_aliases={i: i for i in range(n)},
        scratch_shapes=[pltpu.SemaphoreType.DMA((3 * n,)), pltpu.SemaphoreType.DMA((3 * n,))],
        name=name)(*lands)


def _swap_sibling(name, parts):
    n = len(parts)

    def body(*refs):
        src, dst = refs[:n], refs[n:2 * n]
        send, recv = refs[2 * n:]
        x, y, c = _place()
        cps = [pltpu.make_async_remote_copy(src_ref=src[a], dst_ref=dst[a], send_sem=send.at[a], recv_sem=recv.at[a],
                                            device_id=(x, y, 1 - c), device_id_type=MESH) for a in range(n)]
        for cp in cps:
            cp.start()
        for cp in cps:
            cp.wait_recv()
        for cp in cps:
            cp.wait_send()

    return pl.pallas_call(
        body, out_shape=[jax.ShapeDtypeStruct(p.shape, p.dtype) for p in parts],
        in_specs=[ANY] * n, out_specs=[ANY] * n,
        scratch_shapes=[pltpu.SemaphoreType.DMA((n,)), pltpu.SemaphoreType.DMA((n,))],
        name=name)(*parts)


def _allreduce_small(name, v, after):
    R = v.shape[0]

    def body(v_ref, after_ref, o_ref, buf, send, recv):
        x, y, c = _place()
        me = 4 * x + 2 * y + c
        buf[me] = v_ref[...]
        cps = []
        for k in range(1, 8):
            fx, fy, fc = (k >> 2) & 1, (k >> 1) & 1, k & 1
            px = 1 - x if fx else x
            py = 1 - y if fy else y
            pc = 1 - c if fc else c
            cp = pltpu.make_async_remote_copy(src_ref=v_ref, dst_ref=buf.at[me], send_sem=send.at[k - 1],
                                              recv_sem=recv.at[k - 1], device_id=(px, py, pc), device_id_type=MESH)
            cp.start()
            cps.append((cp, 4 * px + 2 * py + pc))
        for k, (cp, peer) in enumerate(cps):
            pltpu.make_async_remote_copy(src_ref=v_ref, dst_ref=buf.at[peer], send_sem=send.at[k], recv_sem=recv.at[k],
                                         device_id=(x, y, c), device_id_type=MESH).wait_recv()
        for cp, _ in cps:
            cp.wait_send()
        acc = buf[0]
        for d in range(1, 8):
            acc = acc + buf[d]
        o_ref[...] = acc

    return pl.pallas_call(
        body, out_shape=jax.ShapeDtypeStruct((R, 128), F32),
        in_specs=[pl.BlockSpec(memory_space=pltpu.VMEM), ANY], out_specs=pl.BlockSpec(memory_space=pltpu.VMEM),
        scratch_shapes=[pltpu.VMEM((8, R, 128), F32), pltpu.SemaphoreType.DMA((7,)), pltpu.SemaphoreType.DMA((7,))],
        name=name)(v, after)


TILE_BYTES = 1600 * 1024


def _row_tile(r, c=1024):
    for t in (512, 352, 256, 128, 64, 32, 16, 8):
        if r % t == 0 and (t * c * 4 <= TILE_BYTES or t == 8):
            return t
    raise ValueError(r)


def _sum4(name, me, parts, got):
    _, R, C = parts.shape
    tr = _row_tile(R, C)

    def body(me_ref, o_ref, g_ref, s_ref):
        s = o_ref[...].astype(F32)
        for k in range(3):
            s = s + g_ref[k].astype(F32)
        s_ref[...] = s.astype(BF16)

    return pl.pallas_call(
        body, out_shape=jax.ShapeDtypeStruct((R, C), BF16),
        grid_spec=pltpu.PrefetchScalarGridSpec(
            num_scalar_prefetch=1, grid=(R // tr,),
            in_specs=[pl.BlockSpec((None, tr, C), lambda i, me_ref: (me_ref[0], i, 0)),
                      pl.BlockSpec((3, tr, C), lambda i, me_ref: (0, i, 0))],
            out_specs=pl.BlockSpec((tr, C), lambda i, me_ref: (i, 0))),
        name=name, compiler_params=_cp("parallel"))(me, parts, got)


def _adamw(name, w, gparts, m, v):
    R, C = w.shape
    tr = _row_tile(R, C)
    ng = len(gparts)
    c1 = 1.0 - ADAM_B1 ** ADAM_STEP
    c2 = 1.0 - ADAM_B2 ** ADAM_STEP

    def body(*refs):
        w_ref = refs[0]
        g_refs = refs[1:1 + ng]
        m_ref, v_ref, go_ref, d_ref, mo_ref, vo_ref = refs[1 + ng:]
        g = g_refs[0][...]
        for r in g_refs[1:]:
            g = g + r[...]
        mn = ADAM_B1 * m_ref[...] + (1.0 - ADAM_B1) * g
        vn = ADAM_B2 * v_ref[...] + (1.0 - ADAM_B2) * (g * g)
        go_ref[...] = g
        mo_ref[...] = mn
        vo_ref[...] = vn
        d_ref[...] = -ADAM_LR * ((mn / c1) / (jnp.sqrt(vn / c2) + ADAM_EPS) + ADAM_WD * w_ref[...])

    blk = pl.BlockSpec((tr, C), lambda i: (i, 0))
    osh = jax.ShapeDtypeStruct((R, C), F32)
    return pl.pallas_call(
        body, out_shape=(osh, osh, osh, osh), grid=(R // tr,), in_specs=[blk] * (3 + ng), out_specs=(blk,) * 4,
        name=name, compiler_params=_cp("parallel"))(w, *gparts, m, v)


def _adamw_layers(name, w, sums, m, v):
    _, R, C = w.shape
    tr = _row_tile(R, C)
    nr = R // tr
    c1 = 1.0 - ADAM_B1 ** ADAM_STEP
    c2 = 1.0 - ADAM_B2 ** ADAM_STEP

    def body(w_ref, a0, b0, a1, b1, m_ref, v_ref, go_ref, d_ref, mo_ref, vo_ref):
        f = lambda r: r[...].astype(F32)
        g = jnp.where(pl.program_id(0) == 0, f(a0) + f(b0), f(a1) + f(b1))
        mn = ADAM_B1 * m_ref[...] + (1.0 - ADAM_B1) * g
        vn = ADAM_B2 * v_ref[...] + (1.0 - ADAM_B2) * (g * g)
        go_ref[...] = g
        mo_ref[...] = mn
        vo_ref[...] = vn
        d_ref[...] = -ADAM_LR * ((mn / c1) / (jnp.sqrt(vn / c2) + ADAM_EPS) + ADAM_WD * w_ref[...])

    blk = pl.BlockSpec((None, tr, C), lambda l, i: (l, i, 0))
    lay0 = pl.BlockSpec((tr, C), lambda l, i: (jnp.where(l == 0, i, nr - 1), 0))
    lay1 = pl.BlockSpec((tr, C), lambda l, i: (jnp.where(l == 1, i, 0), 0))
    oblk = pl.BlockSpec((tr, C), lambda l, i: (l * nr + i, 0))
    osh = jax.ShapeDtypeStruct((DEPTH * R, C), F32)
    res = pl.pallas_call(
        body, out_shape=(osh, osh, osh, osh), grid=(DEPTH, nr),
        in_specs=[blk, lay0, lay0, lay1, lay1, blk, blk], out_specs=(oblk,) * 4,
        name=name, compiler_params=_cp("arbitrary", "arbitrary"))(w, *sums[0], *sums[1], m, v)
    return [r.reshape(w.shape) for r in res]


BIG = [("ffn1_w_gate", "g1"), ("ffn1_w_up", "u1"), ("ffn1_w_down", "d1"), ("w_in", "win"), ("w_out", "wout"),
       ("ffn2_w_gate", "g2"), ("ffn2_w_up", "u2"), ("ffn2_w_down", "d2")]
SMALL = ["ffn1_norm", "mix_norm", "conv_b", "dt_bias", "a_log", "d_skip", "ssd_norm", "q_norm", "k_norm", "ffn2_norm"]
WEIGHTS = ["ffn1_norm", "ffn1_w_gate", "ffn1_w_up", "ffn1_w_down", "mix_norm", "w_in", "conv_w", "conv_b", "dt_bias",
           "a_log", "d_skip", "ssd_norm", "q_norm", "k_norm", "w_out", "ffn2_norm", "ffn2_w_gate", "ffn2_w_up",
           "ffn2_w_down"]
CONV_SH = CONV_DIM // N_SHARD
TRANSPOSED = ("g1", "u1", "g2", "u2")
GATHER_GROUPS = [(0, "ffn1", ["g1", "u1"]), (0, "ffn1d", ["d1"]), (0, "win", ["win", "cw"]),
                 (0, "rest", ["wout", "g2", "u2", "d2"]),
                 (1, "all", ["g1", "u1", "d1", "win", "cw", "wout", "g2", "u2", "d2"])]


def _pad128(v):
    v = v.reshape(-1)
    return jnp.pad(v, (0, (-v.shape[0]) % 128))


def _pack(pieces):
    flat, offs, pos = [], [], 0
    for p in pieces:
        q = _pad128(p.astype(F32))
        offs.append(pos)
        pos += q.shape[0] // 128
        flat.append(q)
    total = -(-pos // 8) * 8
    out = jnp.concatenate(flat + [jnp.zeros(((total - pos) * 128,), F32)]).reshape(total, 128)
    return out, offs


def _unpack(packed, offs, shapes):
    out = []
    for off, shp in zip(offs, shapes):
        n = int(np.prod(shp))
        rows = -(-n // 128)
        out.append(packed[off:off + rows].reshape(-1)[:n].reshape(shp))
    return out


def kernel(x, ffn1_norm, ffn1_w_gate, ffn1_w_up, ffn1_w_down, mix_norm, w_in, conv_w, conv_b, dt_bias, a_log, d_skip, ssd_norm, q_norm, k_norm, w_out, ffn2_norm, ffn2_w_gate, ffn2_w_up, ffn2_w_down, loss_target, m_ffn1_norm, m_ffn1_w_gate, m_ffn1_w_up, m_ffn1_w_down, m_mix_norm, m_w_in, m_conv_w, m_conv_b, m_dt_bias, m_a_log, m_d_skip, m_ssd_norm, m_q_norm, m_k_norm, m_w_out, m_ffn2_norm, m_ffn2_w_gate, m_ffn2_w_up, m_ffn2_w_down, v_ffn1_norm, v_ffn1_w_gate, v_ffn1_w_up, v_ffn1_w_down, v_mix_norm, v_w_in, v_conv_w, v_conv_b, v_dt_bias, v_a_log, v_d_skip, v_ssd_norm, v_q_norm, v_k_norm, v_w_out, v_ffn2_norm, v_ffn2_w_gate, v_ffn2_w_up, v_ffn2_w_down):
    A = dict(locals())
    ix, iy, ic = _place()
    me = 2 * ix + iy
    B, S, _ = x.shape
    T = B * S

    view = lambda a, key: jnp.swapaxes(a, 1, 2) if key in TRANSPOSED else a
    own = {key: view(A[name], key).astype(BF16) for name, key in BIG}
    own["cw"] = conv_w
    exs, first_norm = [], ffn1_norm
    split = lambda l, key: l == 0 and key != "cw"
    for gi, (l, _, keys) in enumerate(GATHER_GROUPS):
        ex, first_norm = _exchange_start("gather_start%d" % gi, True, l, [own[key] for key in keys], first_norm,
                                         [split(l, key) for key in keys])
        exs.append(ex)
    landed = {}

    def weights(l, group, after):
        gi = [i for i, (gl, gname, _) in enumerate(GATHER_GROUPS) if gl == l and gname in (group, "all")][0]
        if gi not in landed:
            lands = _exchange_wait("gather_wait%d" % gi, exs[gi], after)
            keys = GATHER_GROUPS[gi][2]
            halved = [i for i, key in enumerate(keys) if split(l, key)]
            if halved:
                for i, whole in zip(halved, _sibling_fill("gather_fill%d" % gi, [lands[i] for i in halved])):
                    lands[i] = whole
            landed[gi] = {}
            for key, land in zip(GATHER_GROUPS[gi][2], lands):
                full = lax.dynamic_update_slice(land, own[key][l][None], (me, 0, 0))
                if key == "win":
                    full = _win_from_shards(full)
                if key == "cw":
                    full = jnp.transpose(full, (1, 0, 2)).reshape(CONV_K, CONV_DIM)
                landed[gi][key] = full
        return landed[gi]

    pending = []

    def scatter(l, group, grads, carry):
        keys = sorted(grads)
        arrs = [grads[key] for key in keys]
        if "win" in grads:
            arrs[keys.index("win")] = _win_to_shards(grads["win"])
        ex, carry = _exchange_start("scatter_start_l%d_%s" % (l, group), False, None, arrs, carry)
        pending.append((l, keys, ex))
        return carry

    small = {name: A[name] for name in SMALL}
    small["ffn1_norm"] = first_norm
    lsum, dx, sgrads = _local_step(x.reshape(T, D_MODEL), loss_target.reshape(T, D_MODEL), small, weights, scatter, B)

    names = SMALL + ["conv_w"]
    shapes = [A[n].shape for n in SMALL] + [(DEPTH, CONV_K, CONV_DIM), ()]
    pieces = [jnp.stack([sgrads[l][n].reshape(shp[1:]) for l in range(DEPTH)]) for n, shp in zip(names, shapes)]
    pieces.append(0.5 / D_MODEL * jnp.sum(lsum))
    packed, offs = _pack(pieces)

    sums, theirs, out = {}, {}, {}
    me1 = jnp.reshape(me, (1,)).astype(jnp.int32)

    def update(tag, after):
        todo = [k for k in sums if k not in theirs]
        theirs.update(zip(todo, _swap_sibling("swap_sibling_" + tag, [sums[k] for k in todo])))
        for name, key in BIG:
            if name not in out and all((key, l) in theirs for l in range(DEPTH)):
                res = _adamw_layers("adamw_" + key, view(A[name], key),
                                    [(sums[key, l], theirs[key, l]) for l in range(DEPTH)],
                                    view(A["m_" + name], key), view(A["v_" + name], key))
                out[name] = [view(r, key) for r in res]
                after = res[0]
        return after

    after = dx
    for idx, (l, keys, ex) in enumerate(pending):
        if idx == len(pending) - 1:
            after = update("a", after)
        lands = _exchange_wait("scatter_wait%d" % idx, ex, after)
        for key, g, got in zip(keys, ex["srcs"], lands):
            sums[key, l] = after = _sum4("sum_%s_l%d" % (key, l), me1, g, got)
    after = update("b", after)

    red = _unpack(_allreduce_small("allreduce_small", packed, after), offs, shapes)
    loss = red[-1]
    sg = dict(zip(names, red[:-1]))

    wp, offs = _pack([A[n] for n in SMALL])
    gp, _ = _pack([sg[n] for n in SMALL])
    mp, _ = _pack([A["m_" + n] for n in SMALL])
    vp, _ = _pack([A["v_" + n] for n in SMALL])
    res = _adamw("adamw_small", wp, [gp], mp, vp)
    shapes = [A[n].shape for n in SMALL]
    res = [_unpack(r, offs, shapes) for r in res]
    for i, n in enumerate(SMALL):
        out[n] = [res[q][i] for q in range(4)]
    gcw = lax.dynamic_slice_in_dim(sg["conv_w"], me * CONV_SH, CONV_SH, axis=2)
    flat = lambda a: a.reshape(DEPTH * CONV_K, CONV_SH)
    res = _adamw("adamw_conv_w", flat(conv_w), [flat(gcw)], flat(m_conv_w), flat(v_conv_w))
    out["conv_w"] = [r.reshape(conv_w.shape) for r in res]

    outs = [loss, dx.reshape(B, S, D_MODEL)]
    for q in range(4):
        outs += [out[n][q] for n in WEIGHTS]
    return tuple(outs)
```
